```python
import math
import jax, jax.numpy as jnp
from jax import lax
import numpy as np

D_MODEL = 1024
BATCH = 8
SEQ = 4096
DEPTH = 1

MLA_HEADS = 16
MLA_NOPE = 64
MLA_ROPE = 32
MLA_V = 64
Q_LORA = 256
KV_LORA = 128
SWA_Q_HEADS = 16
SWA_KV_HEADS = 2
SWA_GROUP = SWA_Q_HEADS // SWA_KV_HEADS
SWA_HEAD_DIM = 64
WINDOW = 128
Q_BLOCK = 128
ROPE_THETA = 10000.0
D_FF = 4 * D_MODEL
PLE_DIM = 256
N_BRANCHES = 2
NORM_EPS = 1e-6
NEG = -1e30

IN_SPLITS = (Q_LORA, KV_LORA, MLA_ROPE,
             SWA_Q_HEADS * SWA_HEAD_DIM, SWA_KV_HEADS * SWA_HEAD_DIM, SWA_KV_HEADS * SWA_HEAD_DIM,
             N_BRANCHES * D_MODEL)
IN_COLS = sum(IN_SPLITS)
IN_OFFSETS = tuple(int(v) for v in np.cumsum(IN_SPLITS)[:-1])

kernel_name = "hybrid_mla_swa_gated_sandwich"


def rms_norm(x, g):
    xf = x.astype(jnp.float32)
    y = xf * lax.rsqrt(jnp.mean(xf * xf, axis=-1, keepdims=True) + NORM_EPS)
    return (y * g.astype(jnp.float32)).astype(x.dtype)


def rope(x, pos):
    d = x.shape[-1]
    half = d // 2
    inv = jnp.exp(-math.log(ROPE_THETA) * jnp.arange(half, dtype=jnp.float32) * (2.0 / d))
    ang = pos.astype(jnp.float32)[:, None] * inv[None, :]
    cos = jnp.cos(ang)[None, :, None, :]
    sin = jnp.sin(ang)[None, :, None, :]
    xf = x.astype(jnp.float32)
    x1, x2 = xf[..., :half], xf[..., half:]
    return jnp.concatenate([x1 * cos - x2 * sin, x2 * cos + x1 * sin], axis=-1).astype(x.dtype)


def mla_attention(q, k, v):
    B, S, H, D = q.shape
    nb = S // Q_BLOCK
    scale = D ** -0.5
    qb = q.reshape(B, nb, Q_BLOCK, H, D).transpose(1, 0, 2, 3, 4)
    kpos = jnp.arange(S)

    def one_block(args):
        qblk, i = args
        s = jnp.einsum('bqhd,bkhd->bhqk', qblk, k, preferred_element_type=jnp.float32) * scale
        qpos = i * Q_BLOCK + jnp.arange(Q_BLOCK)
        mask = kpos[None, :] <= qpos[:, None]
        s = jnp.where(mask[None, None], s, NEG)
        pr = jax.nn.softmax(s, axis=-1)
        return jnp.einsum('bhqk,bkhd->bqhd', pr.astype(v.dtype), v)

    out = lax.map(one_block, (qb, jnp.arange(nb)))
    return out.transpose(1, 0, 2, 3, 4).reshape(B, S, H, v.shape[-1])


def swa_attention(q, k, v, sinks):
    B, S, HQ, hd = q.shape
    blk = WINDOW
    nb = S // blk
    scale = hd ** -0.5
    qb = q.reshape(B, nb, blk, SWA_KV_HEADS, SWA_GROUP, hd)

    def band(t):
        tb = t.reshape(B, nb, blk, SWA_KV_HEADS, hd)
        prev = jnp.pad(tb[:, :-1], ((0, 0), (1, 0), (0, 0), (0, 0), (0, 0)))
        return jnp.concatenate([prev, tb], axis=2)

    kb, vb = band(k), band(v)
    s = jnp.einsum('bnqhgd,bnkhd->bnhgqk', qb, kb, preferred_element_type=jnp.float32) * scale
    qi = jnp.arange(blk)[:, None]
    kj = jnp.arange(2 * blk)[None, :] - blk
    rel = qi - kj
    band_mask = (rel >= 0) & (rel < WINDOW)
    valid = (jnp.arange(nb)[:, None, None] * blk + kj[None]) >= 0
    mask = band_mask[None] & valid
    s = jnp.where(mask[None, :, None, None], s, NEG)
    sink = sinks.astype(jnp.float32).reshape(SWA_KV_HEADS, SWA_GROUP)[None, None, :, :, None, None]
    m = jnp.maximum(jnp.max(s, axis=-1, keepdims=True), sink)
    e = jnp.exp(s - m)
    pr = e / (jnp.sum(e, axis=-1, keepdims=True) + jnp.exp(sink - m))
    o = jnp.einsum('bnhgqk,bnkhd->bnqhgd', pr.astype(v.dtype), vb)
    return o.reshape(B, S, HQ, hd)


def _fwd_setup_inputs(seed: int = 0) -> dict:
    key = jax.random.key(seed)
    ks = jax.random.split(key, 24)

    def dense(k, fan_in, fan_out):
        return jax.random.normal(k, (DEPTH, fan_in, fan_out), jnp.float32) * fan_in ** -0.5

    def gain(k, n):
        return 1.0 + 0.05 * jax.random.normal(k, (DEPTH, n), jnp.float32)

    return {
        "x": jax.random.normal(ks[0], (BATCH, SEQ, D_MODEL), jnp.float32),
        "p": jax.random.normal(ks[1], (DEPTH, BATCH, SEQ, PLE_DIM), jnp.float32),
        "g_mix_pre": gain(ks[2], D_MODEL),
        "w_in": dense(ks[3], D_MODEL, IN_COLS),
        "g_q_a": gain(ks[4], Q_LORA),
        "w_q_b": dense(ks[5], Q_LORA, MLA_HEADS * (MLA_NOPE + MLA_ROPE)),
        "g_kv_a": gain(ks[6], KV_LORA),
        "w_kv_b": dense(ks[7], KV_LORA, MLA_HEADS * (MLA_NOPE + MLA_V)),
        "sinks": 0.5 * jax.random.normal(ks[8], (DEPTH, SWA_Q_HEADS), jnp.float32),
        "w_mla_up": dense(ks[9], MLA_HEADS * MLA_V, D_MODEL),
        "w_swa_up": dense(ks[10], SWA_Q_HEADS * SWA_HEAD_DIM, D_MODEL),
        "w_out": dense(ks[11], D_MODEL, D_MODEL),
        "g_mix_post": gain(ks[12], D_MODEL),
        "g_mlp_pre": gain(ks[13], D_MODEL),
        "w_mlp_up": dense(ks[14], D_MODEL, D_FF),
        "w_mlp_down": dense(ks[15], D_FF, D_MODEL),
        "g_mlp_post": gain(ks[16], D_MODEL),
        "w_ple": dense(ks[17], PLE_DIM, D_MODEL),
        "g_ple": gain(ks[18], D_MODEL),
        "w_ple_gate": dense(ks[19], D_MODEL, D_MODEL),
    }


def _fwd_reference(x, p, g_mix_pre, w_in, g_q_a, w_q_b, g_kv_a, w_kv_b, sinks, w_mla_up, w_swa_up,
              w_out, g_mix_post, g_mlp_pre, w_mlp_up, w_mlp_down, g_mlp_post, w_ple, g_ple,
              w_ple_gate):
    B, S, _ = x.shape
    pos = jnp.arange(S)
    for i in range(DEPTH):
        h = rms_norm(x, g_mix_pre[i])
        z = h @ w_in[i]
        q_a, kv_a, k_r, sq, sk, sv, gates = jnp.split(z, IN_OFFSETS, axis=-1)

        qm = (rms_norm(q_a, g_q_a[i]) @ w_q_b[i]).reshape(B, S, MLA_HEADS, MLA_NOPE + MLA_ROPE)
        q_nope, q_rope = qm[..., :MLA_NOPE], rope(qm[..., MLA_NOPE:], pos)
        kvm = (rms_norm(kv_a, g_kv_a[i]) @ w_kv_b[i]).reshape(B, S, MLA_HEADS, MLA_NOPE + MLA_V)
        k_nope, v_m = kvm[..., :MLA_NOPE], kvm[..., MLA_NOPE:]
        k_rope = jnp.broadcast_to(rope(k_r[:, :, None, :], pos), (B, S, MLA_HEADS, MLA_ROPE))
        q_mla = jnp.concatenate([q_nope, q_rope], axis=-1)
        k_mla = jnp.concatenate([k_nope, k_rope], axis=-1)
        o_mla = mla_attention(q_mla, k_mla, v_m).reshape(B, S, MLA_HEADS * MLA_V)

        q_s = rope(sq.reshape(B, S, SWA_Q_HEADS, SWA_HEAD_DIM), pos)
        k_s = rope(sk.reshape(B, S, SWA_KV_HEADS, SWA_HEAD_DIM), pos)
        v_s = sv.reshape(B, S, SWA_KV_HEADS, SWA_HEAD_DIM)
        o_swa = swa_attention(q_s, k_s, v_s, sinks[i]).reshape(B, S, SWA_Q_HEADS * SWA_HEAD_DIM)

        g_a, g_b = gates[..., :D_MODEL], gates[..., D_MODEL:]
        y = jax.nn.sigmoid(g_a) * (o_mla @ w_mla_up[i]) + jax.nn.sigmoid(g_b) * (o_swa @ w_swa_up[i])
        x = x + rms_norm(y @ w_out[i], g_mix_post[i])

        h = rms_norm(x, g_mlp_pre[i])
        u = jnp.square(jax.nn.relu(h @ w_mlp_up[i]))
        x = x + rms_norm(u @ w_mlp_down[i], g_mlp_post[i])

        e = rms_norm(p[i] @ w_ple[i], g_ple[i])
        x = x + jax.nn.sigmoid(x @ w_ple_gate[i]) * e
    return x


import jax as _jax
import jax.numpy as _jnp

TWIN_FORMAT = 'train_step'
FWD_PARAMS = ['x', 'p', 'g_mix_pre', 'w_in', 'g_q_a', 'w_q_b', 'g_kv_a', 'w_kv_b', 'sinks', 'w_mla_up', 'w_swa_up', 'w_out', 'g_mix_post', 'g_mlp_pre', 'w_mlp_up', 'w_mlp_down', 'g_mlp_post', 'w_ple', 'g_ple', 'w_ple_gate']
TWIN_WEIGHTS = ['g_mix_pre', 'w_in', 'g_q_a', 'w_q_b', 'g_kv_a', 'w_kv_b', 'sinks', 'w_mla_up', 'w_swa_up', 'w_out', 'g_mix_post', 'g_mlp_pre', 'w_mlp_up', 'w_mlp_down', 'g_mlp_post', 'w_ple', 'g_ple', 'w_ple_gate']
TWIN_DIFF_INPUT = 'x'
TWIN_INPUTS = ['x', 'p', 'g_mix_pre', 'w_in', 'g_q_a', 'w_q_b', 'g_kv_a', 'w_kv_b', 'sinks', 'w_mla_up', 'w_swa_up', 'w_out', 'g_mix_post', 'g_mlp_pre', 'w_mlp_up', 'w_mlp_down', 'g_mlp_post', 'w_ple', 'g_ple', 'w_ple_gate', 'loss_target', 'm_g_mix_pre', 'm_w_in', 'm_g_q_a', 'm_w_q_b', 'm_g_kv_a', 'm_w_kv_b', 'm_sinks', 'm_w_mla_up', 'm_w_swa_up', 'm_w_out', 'm_g_mix_post', 'm_g_mlp_pre', 'm_w_mlp_up', 'm_w_mlp_down', 'm_g_mlp_post', 'm_w_ple', 'm_g_ple', 'm_w_ple_gate', 'v_g_mix_pre', 'v_w_in', 'v_g_q_a', 'v_w_q_b', 'v_g_kv_a', 'v_w_kv_b', 'v_sinks', 'v_w_mla_up', 'v_w_swa_up', 'v_w_out', 'v_g_mix_post', 'v_g_mlp_pre', 'v_w_mlp_up', 'v_w_mlp_down', 'v_g_mlp_post', 'v_w_ple', 'v_g_ple', 'v_w_ple_gate']
TWIN_OUTPUTS = ['loss', 'grad_x', 'grad_g_mix_pre', 'grad_w_in', 'grad_g_q_a', 'grad_w_q_b', 'grad_g_kv_a', 'grad_w_kv_b', 'grad_sinks', 'grad_w_mla_up', 'grad_w_swa_up', 'grad_w_out', 'grad_g_mix_post', 'grad_g_mlp_pre', 'grad_w_mlp_up', 'grad_w_mlp_down', 'grad_g_mlp_post', 'grad_w_ple', 'grad_g_ple', 'grad_w_ple_gate', 'delta_g_mix_pre', 'delta_w_in', 'delta_g_q_a', 'delta_w_q_b', 'delta_g_kv_a', 'delta_w_kv_b', 'delta_sinks', 'delta_w_mla_up', 'delta_w_swa_up', 'delta_w_out', 'delta_g_mix_post', 'delta_g_mlp_pre', 'delta_w_mlp_up', 'delta_w_mlp_down', 'delta_g_mlp_post', 'delta_w_ple', 'delta_g_ple', 'delta_w_ple_gate', 'new_m_g_mix_pre', 'new_m_w_in', 'new_m_g_q_a', 'new_m_w_q_b', 'new_m_g_kv_a', 'new_m_w_kv_b', 'new_m_sinks', 'new_m_w_mla_up', 'new_m_w_swa_up', 'new_m_w_out', 'new_m_g_mix_post', 'new_m_g_mlp_pre', 'new_m_w_mlp_up', 'new_m_w_mlp_down', 'new_m_g_mlp_post', 'new_m_w_ple', 'new_m_g_ple', 'new_m_w_ple_gate', 'new_v_g_mix_pre', 'new_v_w_in', 'new_v_g_q_a', 'new_v_w_q_b', 'new_v_g_kv_a', 'new_v_w_kv_b', 'new_v_sinks', 'new_v_w_mla_up', 'new_v_w_swa_up', 'new_v_w_out', 'new_v_g_mix_post', 'new_v_g_mlp_pre', 'new_v_w_mlp_up', 'new_v_w_mlp_down', 'new_v_g_mlp_post', 'new_v_w_ple', 'new_v_g_ple', 'new_v_w_ple_gate']
TWIN_LEAF_KINDS = {'loss': 'loss', 'grad_x': 'grad_x', 'grad_g_mix_pre': 'grad_w', 'grad_w_in': 'grad_w', 'grad_g_q_a': 'grad_w', 'grad_w_q_b': 'grad_w', 'grad_g_kv_a': 'grad_w', 'grad_w_kv_b': 'grad_w', 'grad_sinks': 'grad_w', 'grad_w_mla_up': 'grad_w', 'grad_w_swa_up': 'grad_w', 'grad_w_out': 'grad_w', 'grad_g_mix_post': 'grad_w', 'grad_g_mlp_pre': 'grad_w', 'grad_w_mlp_up': 'grad_w', 'grad_w_mlp_down': 'grad_w', 'grad_g_mlp_post': 'grad_w', 'grad_w_ple': 'grad_w', 'grad_g_ple': 'grad_w', 'grad_w_ple_gate': 'grad_w', 'delta_g_mix_pre': 'delta_w', 'delta_w_in': 'delta_w', 'delta_g_q_a': 'delta_w', 'delta_w_q_b': 'delta_w', 'delta_g_kv_a': 'delta_w', 'delta_w_kv_b': 'delta_w', 'delta_sinks': 'delta_w', 'delta_w_mla_up': 'delta_w', 'delta_w_swa_up': 'delta_w', 'delta_w_out': 'delta_w', 'delta_g_mix_post': 'delta_w', 'delta_g_mlp_pre': 'delta_w', 'delta_w_mlp_up': 'delta_w', 'delta_w_mlp_down': 'delta_w', 'delta_g_mlp_post': 'delta_w', 'delta_w_ple': 'delta_w', 'delta_g_ple': 'delta_w', 'delta_w_ple_gate': 'delta_w', 'new_m_g_mix_pre': 'new_m', 'new_m_w_in': 'new_m', 'new_m_g_q_a': 'new_m', 'new_m_w_q_b': 'new_m', 'new_m_g_kv_a': 'new_m', 'new_m_w_kv_b': 'new_m', 'new_m_sinks': 'new_m', 'new_m_w_mla_up': 'new_m', 'new_m_w_swa_up': 'new_m', 'new_m_w_out': 'new_m', 'new_m_g_mix_post': 'new_m', 'new_m_g_mlp_pre': 'new_m', 'new_m_w_mlp_up': 'new_m', 'new_m_w_mlp_down': 'new_m', 'new_m_g_mlp_post': 'new_m', 'new_m_w_ple': 'new_m', 'new_m_g_ple': 'new_m', 'new_m_w_ple_gate': 'new_m', 'new_v_g_mix_pre': 'new_v', 'new_v_w_in': 'new_v', 'new_v_g_q_a': 'new_v', 'new_v_w_q_b': 'new_v', 'new_v_g_kv_a': 'new_v', 'new_v_w_kv_b': 'new_v', 'new_v_sinks': 'new_v', 'new_v_w_mla_up': 'new_v', 'new_v_w_swa_up': 'new_v', 'new_v_w_out': 'new_v', 'new_v_g_mix_post': 'new_v', 'new_v_g_mlp_pre': 'new_v', 'new_v_w_mlp_up': 'new_v', 'new_v_w_mlp_down': 'new_v', 'new_v_g_mlp_post': 'new_v', 'new_v_w_ple': 'new_v', 'new_v_g_ple': 'new_v', 'new_v_w_ple_gate': 'new_v'}


def _forward(args):
    return _fwd_reference(*[args[k] for k in FWD_PARAMS])


def _output_shape():
    out = _jax.eval_shape(lambda: _forward(_fwd_setup_inputs(0)))
    return out.shape, out.dtype

N_MICROBATCH = 1
ADAM_LR = 0.001
ADAM_B1 = 0.9
ADAM_B2 = 0.999
ADAM_EPS = 1e-08
ADAM_WD = 0.01
ADAM_STEP = 10
PER_EXAMPLE_BATCH_AXIS = {'x': 0, 'p': 1, 'loss_target': 0}
SHARED_INPUTS = []
_WEIGHT_DTYPES = {'g_mix_pre': _jnp.float32, 'w_in': _jnp.float32, 'g_q_a': _jnp.float32, 'w_q_b': _jnp.float32, 'g_kv_a': _jnp.float32, 'w_kv_b': _jnp.float32, 'sinks': _jnp.float32, 'w_mla_up': _jnp.float32, 'w_swa_up': _jnp.float32, 'w_out': _jnp.float32, 'g_mix_post': _jnp.float32, 'g_mlp_pre': _jnp.float32, 'w_mlp_up': _jnp.float32, 'w_mlp_down': _jnp.float32, 'g_mlp_post': _jnp.float32, 'w_ple': _jnp.float32, 'g_ple': _jnp.float32, 'w_ple_gate': _jnp.float32}
MOMENT_SCALE = {'g_mix_pre': 1.293849e+00, 'w_in': 5.995149e-01, 'g_q_a': 7.201970e-01, 'w_q_b': 2.892946e-01, 'g_kv_a': 3.334621e+00, 'w_kv_b': 6.573211e-01, 'sinks': 6.234476e-02, 'w_mla_up': 9.050611e-01, 'w_swa_up': 7.913268e-01, 'w_out': 1.272773e+00, 'g_mix_post': 3.226686e+01, 'g_mlp_pre': 9.058580e-01, 'w_mlp_up': 4.331583e-01, 'w_mlp_down': 1.504935e+00, 'g_mlp_post': 3.328223e+01, 'w_ple': 2.289227e-01, 'g_ple': 1.100699e+01, 'w_ple_gate': 9.409685e-01}


def _to_microbatches(a, axis):
    t = _jnp.moveaxis(a, axis, 0)
    t = t.reshape((N_MICROBATCH, t.shape[0] // N_MICROBATCH) + t.shape[1:])
    return _jnp.moveaxis(t, 1, axis + 1)


def setup_inputs(seed: int = 0) -> dict:
    inp = _fwd_setup_inputs(seed)
    key = _jax.random.fold_in(_jax.random.key(seed), 7919)
    shape, _ = _output_shape()
    out = dict(inp)
    out["loss_target"] = _jax.random.normal(_jax.random.fold_in(key, 0), shape, _jnp.float32)
    for i, name in enumerate(TWIN_WEIGHTS):
        w = inp[name].astype(_jnp.float32)
        if MOMENT_SCALE is None:
            s = _jnp.sqrt(_jnp.mean(_jnp.square(w)) + 1e-30)
        else:
            s = MOMENT_SCALE[name]
        km, kv = _jax.random.split(_jax.random.fold_in(key, i + 1))
        out[name] = w
        out["m_" + name] = s * _jax.random.normal(km, w.shape, _jnp.float32)
        out["v_" + name] = (s * s) * _jax.random.uniform(kv, w.shape, _jnp.float32, 0.5, 1.5)
    if N_MICROBATCH > 1:
        for name, axis in PER_EXAMPLE_BATCH_AXIS.items():
            out[name] = _to_microbatches(out[name], axis)
    return {'x': out['x'], 'p': out['p'], 'g_mix_pre': out['g_mix_pre'], 'w_in': out['w_in'], 'g_q_a': out['g_q_a'], 'w_q_b': out['w_q_b'], 'g_kv_a': out['g_kv_a'], 'w_kv_b': out['w_kv_b'], 'sinks': out['sinks'], 'w_mla_up': out['w_mla_up'], 'w_swa_up': out['w_swa_up'], 'w_out': out['w_out'], 'g_mix_post': out['g_mix_post'], 'g_mlp_pre': out['g_mlp_pre'], 'w_mlp_up': out['w_mlp_up'], 'w_mlp_down': out['w_mlp_down'], 'g_mlp_post': out['g_mlp_post'], 'w_ple': out['w_ple'], 'g_ple': out['g_ple'], 'w_ple_gate': out['w_ple_gate'], 'loss_target': out['loss_target'], 'm_g_mix_pre': out['m_g_mix_pre'], 'm_w_in': out['m_w_in'], 'm_g_q_a': out['m_g_q_a'], 'm_w_q_b': out['m_w_q_b'], 'm_g_kv_a': out['m_g_kv_a'], 'm_w_kv_b': out['m_w_kv_b'], 'm_sinks': out['m_sinks'], 'm_w_mla_up': out['m_w_mla_up'], 'm_w_swa_up': out['m_w_swa_up'], 'm_w_out': out['m_w_out'], 'm_g_mix_post': out['m_g_mix_post'], 'm_g_mlp_pre': out['m_g_mlp_pre'], 'm_w_mlp_up': out['m_w_mlp_up'], 'm_w_mlp_down': out['m_w_mlp_down'], 'm_g_mlp_post': out['m_g_mlp_post'], 'm_w_ple': out['m_w_ple'], 'm_g_ple': out['m_g_ple'], 'm_w_ple_gate': out['m_w_ple_gate'], 'v_g_mix_pre': out['v_g_mix_pre'], 'v_w_in': out['v_w_in'], 'v_g_q_a': out['v_g_q_a'], 'v_w_q_b': out['v_w_q_b'], 'v_g_kv_a': out['v_g_kv_a'], 'v_w_kv_b': out['v_w_kv_b'], 'v_sinks': out['v_sinks'], 'v_w_mla_up': out['v_w_mla_up'], 'v_w_swa_up': out['v_w_swa_up'], 'v_w_out': out['v_w_out'], 'v_g_mix_post': out['v_g_mix_post'], 'v_g_mlp_pre': out['v_g_mlp_pre'], 'v_w_mlp_up': out['v_w_mlp_up'], 'v_w_mlp_down': out['v_w_mlp_down'], 'v_g_mlp_post': out['v_g_mlp_post'], 'v_w_ple': out['v_w_ple'], 'v_g_ple': out['v_g_ple'], 'v_w_ple_gate': out['v_w_ple_gate']}


def _loss(weights, diff, rest, loss_target):
    with _jax.named_scope("forward"):
        args = {**rest, TWIN_DIFF_INPUT: diff, **{k: w.astype(_WEIGHT_DTYPES[k]) for k, w in weights.items()}}
        y = _forward(args)
    with _jax.named_scope("loss_head"):
        err = _jnp.square(y.astype(_jnp.float32) - loss_target)
        return 0.5 * _jnp.sum(_jnp.mean(err, axis=-1)) if err.ndim else 0.5 * err


def _adamw(w, g, m, v):
    m = ADAM_B1 * m + (1.0 - ADAM_B1) * g
    v = ADAM_B2 * v + (1.0 - ADAM_B2) * _jnp.square(g)
    m_hat = m / (1.0 - ADAM_B1 ** ADAM_STEP)
    v_hat = v / (1.0 - ADAM_B2 ** ADAM_STEP)
    delta = -ADAM_LR * (m_hat / (_jnp.sqrt(v_hat) + ADAM_EPS) + ADAM_WD * w)
    return delta, m, v


def reference(x, p, g_mix_pre, w_in, g_q_a, w_q_b, g_kv_a, w_kv_b, sinks, w_mla_up, w_swa_up, w_out, g_mix_post, g_mlp_pre, w_mlp_up, w_mlp_down, g_mlp_post, w_ple, g_ple, w_ple_gate, loss_target, m_g_mix_pre, m_w_in, m_g_q_a, m_w_q_b, m_g_kv_a, m_w_kv_b, m_sinks, m_w_mla_up, m_w_swa_up, m_w_out, m_g_mix_post, m_g_mlp_pre, m_w_mlp_up, m_w_mlp_down, m_g_mlp_post, m_w_ple, m_g_ple, m_w_ple_gate, v_g_mix_pre, v_w_in, v_g_q_a, v_w_q_b, v_g_kv_a, v_w_kv_b, v_sinks, v_w_mla_up, v_w_swa_up, v_w_out, v_g_mix_post, v_g_mlp_pre, v_w_mlp_up, v_w_mlp_down, v_g_mlp_post, v_w_ple, v_g_ple, v_w_ple_gate):
    given = dict(x=x, p=p, g_mix_pre=g_mix_pre, w_in=w_in, g_q_a=g_q_a, w_q_b=w_q_b, g_kv_a=g_kv_a, w_kv_b=w_kv_b, sinks=sinks, w_mla_up=w_mla_up, w_swa_up=w_swa_up, w_out=w_out, g_mix_post=g_mix_post, g_mlp_pre=g_mlp_pre, w_mlp_up=w_mlp_up, w_mlp_down=w_mlp_down, g_mlp_post=g_mlp_post, w_ple=w_ple, g_ple=g_ple, w_ple_gate=w_ple_gate, loss_target=loss_target, m_g_mix_pre=m_g_mix_pre, m_w_in=m_w_in, m_g_q_a=m_g_q_a, m_w_q_b=m_w_q_b, m_g_kv_a=m_g_kv_a, m_w_kv_b=m_w_kv_b, m_sinks=m_sinks, m_w_mla_up=m_w_mla_up, m_w_swa_up=m_w_swa_up, m_w_out=m_w_out, m_g_mix_post=m_g_mix_post, m_g_mlp_pre=m_g_mlp_pre, m_w_mlp_up=m_w_mlp_up, m_w_mlp_down=m_w_mlp_down, m_g_mlp_post=m_g_mlp_post, m_w_ple=m_w_ple, m_g_ple=m_g_ple, m_w_ple_gate=m_w_ple_gate, v_g_mix_pre=v_g_mix_pre, v_w_in=v_w_in, v_g_q_a=v_g_q_a, v_w_q_b=v_w_q_b, v_g_kv_a=v_g_kv_a, v_w_kv_b=v_w_kv_b, v_sinks=v_sinks, v_w_mla_up=v_w_mla_up, v_w_swa_up=v_w_swa_up, v_w_out=v_w_out, v_g_mix_post=v_g_mix_post, v_g_mlp_pre=v_g_mlp_pre, v_w_mlp_up=v_w_mlp_up, v_w_mlp_down=v_w_mlp_down, v_g_mlp_post=v_g_mlp_post, v_w_ple=v_w_ple, v_g_ple=v_g_ple, v_w_ple_gate=v_w_ple_gate)
    weights = {n: given[n] for n in TWIN_WEIGHTS}
    shared = {n: given[n] for n in SHARED_INPUTS}
    per_example = {n: given[n] for n in ['x', 'p']}
    grad_fn = _jax.value_and_grad(_loss, argnums=(0, 1))

    def one_microbatch(ex, loss_target):
        ex = dict(ex)
        diff = ex.pop(TWIN_DIFF_INPUT)
        return grad_fn(weights, diff, {**shared, **ex}, loss_target)

    if N_MICROBATCH == 1:
        loss, (grad_w, grad_x) = one_microbatch(per_example, given["loss_target"])
    else:
        def body(carry, xs):
            loss_sum, grad_sum = carry
            l_k, (gw_k, gx_k) = one_microbatch(xs[0], xs[1])
            with _jax.named_scope("update"):
                return (loss_sum + l_k, _jax.tree.map(_jnp.add, grad_sum, gw_k)), gx_k

        init = (_jnp.zeros((), _jnp.float32), _jax.tree.map(_jnp.zeros_like, weights))
        (loss, grad_w), grad_x = _jax.lax.scan(body, init, (per_example, given["loss_target"]))
    with _jax.named_scope("update"):
        delta_w, new_m, new_v = {}, {}, {}
        for n in TWIN_WEIGHTS:
            delta_w[n], new_m[n], new_v[n] = _adamw(weights[n], grad_w[n], given["m_" + n], given["v_" + n])
    return (loss, grad_x, *[grad_w[n] for n in TWIN_WEIGHTS], *[delta_w[n] for n in TWIN_WEIGHTS],
            *[new_m[n] for n in TWIN_WEIGHTS], *[new_v[n] for n in TWIN_WEIGHTS])
```

```python
import math

import jax
import jax.numpy as jnp
from jax import lax
from jax.experimental import pallas as pl
from jax.experimental.pallas import tpu as pltpu

F32 = jnp.float32
BF16 = jnp.bfloat16
SDS = jax.ShapeDtypeStruct

D = 1024
D_FF = 4096
PLE = 256
Q_LORA = 256
KV_LORA = 128
MLA_HEADS = 16
MLA_NOPE = 64
MLA_ROPE = 32
SWA_HEADS = 16
SWA_HD = 64
WINDOW = 128
ROPE_THETA = 10000.0
EPS = 1e-6
NEG = -1e30
NZ = 4096
MLA_SCALE = (MLA_NOPE + MLA_ROPE) ** -0.5
SWA_SCALE = SWA_HD ** -0.5

ADAM_LR = 0.001
ADAM_B1 = 0.9
ADAM_B2 = 0.999
ADAM_EPS = 1e-08
ADAM_WD = 0.01
ADAM_STEP = 10

LANES = 128
N_CHIPS = 4
N_DEV = 8
MESH = pl.DeviceIdType.MESH

NT = (((1,), (1,)), ((), ()))
TN = (((0,), (0,)), ((), ()))

BIG = (("w_in", 1024, 936), ("w_q_b", 256, 384), ("w_kv_b", 128, 512), ("w_mla_up", 256, 1024),
       ("w_swa_up", 256, 1024), ("w_out", 256, 1024), ("w_mlp_up", 1024, 1024), ("w_mlp_down", 1024, 1024),
       ("w_ple", 256, 256), ("w_ple_gate", 256, 1024))
COL_SHARDED = ("w_in", "w_q_b", "w_kv_b", "w_mlp_up", "w_ple")
PACK_ROWS = sum(r * c for _, r, c in BIG) // D
PACK_PAD = 4256
HALF = PACK_PAD // 2
SMALL = (("g_mix_pre", 1024), ("g_q_a", 256), ("g_kv_a", 128), ("sinks", 16), ("g_mix_post", 1024),
         ("g_mlp_pre", 1024), ("g_mlp_post", 1024), ("g_ple", 1024))


def _dot(a, b):
    return jnp.dot(a, b, preferred_element_type=F32)


def _dot_nt(a, b):
    return lax.dot_general(a, b, NT, preferred_element_type=F32)


def _dot_tn(a, b):
    return lax.dot_general(a, b, TN, preferred_element_type=F32)


def _pcall(body, *, name, out_shape, grid=(), in_specs=None, out_specs=None, scratch=(), sem=None, vmem_mb=48):
    params = dict(vmem_limit_bytes=vmem_mb << 20)
    if sem is not None:
        params["dimension_semantics"] = sem
    return pl.pallas_call(body, name=name, grid=grid, in_specs=in_specs, out_specs=out_specs, out_shape=out_shape,
                          scratch_shapes=list(scratch), compiler_params=pltpu.CompilerParams(**params))


def _rows(tm, n, col=0):
    return pl.BlockSpec((tm, n), lambda i: (i, col))


def _full(shape):
    return pl.BlockSpec(shape, lambda i: (0,) * len(shape))


def _rms(x, g):
    r = lax.rsqrt(jnp.mean(x * x, axis=-1, keepdims=True) + EPS)
    return x * r * g


def _rms_bwd(dy, x, g):
    r = lax.rsqrt(jnp.mean(x * x, axis=-1, keepdims=True) + EPS)
    xn = x * r
    dn = dy * g
    dx = r * (dn - xn * jnp.mean(dn * xn, axis=-1, keepdims=True))
    return dx, jnp.sum(dy * xn, axis=0, keepdims=True)


def _sigmoid(x):
    return 1.0 / (1.0 + jnp.exp(-x))


def _rope(x, c, a, b, half):
    return x * c + pltpu.roll(x, LANES - half, 1) * a + pltpu.roll(x, half, 1) * b


def _rope_tables(T, kind):
    lane = jnp.arange(LANES)
    if kind == "mla":
        half = MLA_ROPE // 2
        rel = lane - MLA_NOPE
        on = (rel >= 0) & (rel < MLA_ROPE)
        d = MLA_ROPE
    else:
        half = SWA_HD // 2
        rel = lane % SWA_HD
        on = jnp.ones((LANES,), bool)
        d = SWA_HD
    first = on & (rel < half)
    second = on & (rel >= half)
    f = jnp.where(first, rel, rel - half).astype(F32)
    inv = jnp.exp(-math.log(ROPE_THETA) * f * (2.0 / d))
    ang = jnp.arange(T, dtype=F32)[:, None] * inv[None, :]
    cos, sin = jnp.cos(ang), jnp.sin(ang)
    c = jnp.where(on[None], cos, 1.0)
    a = jnp.where(first[None], -sin, 0.0)
    b = jnp.where(second[None], sin, 0.0)
    return c, a, b


def _fwd_in(x, g1, w_in_p, tm):
    T = x.shape[0]

    def body(x_ref, g_ref, w_ref, z_ref, h_ref):
        h = _rms(x_ref[...], g_ref[...]).astype(BF16)
        h_ref[...] = h
        z_ref[...] = _dot(h, w_ref[...])

    return _pcall(body, name="fwd_in", grid=(T // tm,),
                  in_specs=[_rows(tm, D), _full((1, D)), _full((D, NZ))],
                  out_specs=[_rows(tm, NZ), _rows(tm, D)],
                  out_shape=[SDS((T, NZ), F32), SDS((T, D), BF16)], sem=("parallel",))(x, g1, w_in_p)


def _fwd_qkv(z, gq, gkv, wqb, wkn, wv, tab_m, tab_s, tm):
    T = z.shape[0]

    def body(qa_ref, sq_ref, skd_ref, svd_ref, kva_ref, kr_ref, gq_ref, gkv_ref, wqb_ref, wkn_ref, wv_ref,
             cm_ref, am_ref, bm_ref, cs_ref, as_ref, bs_ref,
             qn_ref, kvn_ref, qm_ref, km_ref, vm_ref, qs_ref, ks_ref, vs_ref):
        qn = _rms(qa_ref[...], gq_ref[...]).astype(BF16)
        qn_ref[...] = qn
        kvn = _rms(kva_ref[...], gkv_ref[...]).astype(BF16)
        kvn_ref[...] = kvn
        cm, am, bm = cm_ref[...], am_ref[...], bm_ref[...]
        cs, as_, bs = cs_ref[...], as_ref[...], bs_ref[...]
        vm_ref[...] = _dot(kvn, wv_ref[...]).astype(BF16)
        k_rope = _rope(kr_ref[...], cm, am, bm, MLA_ROPE // 2)
        for h in range(MLA_HEADS):
            sl = slice(LANES * h, LANES * (h + 1))
            qh = _dot(qn, wqb_ref[:, sl])
            qm_ref[:, sl] = _rope(qh, cm, am, bm, MLA_ROPE // 2).astype(BF16)
            km_ref[:, sl] = (_dot(kvn, wkn_ref[:, sl]) + k_rope).astype(BF16)
        for j in range(D // LANES):
            sl = slice(LANES * j, LANES * (j + 1))
            qs_ref[:, sl] = _rope(sq_ref[:, sl], cs, as_, bs, SWA_HD // 2).astype(BF16)
        for j in range(2):
            sl = slice(LANES * j, LANES * (j + 1))
            ks_ref[:, sl] = _rope(skd_ref[:, sl], cs, as_, bs, SWA_HD // 2).astype(BF16)
        vs_ref[...] = svd_ref[...].astype(BF16)

    tab = [_rows(tm, LANES)] * 6
    return _pcall(body, name="fwd_qkv", grid=(T // tm,),
                  in_specs=[_rows(tm, 256, 12), _rows(tm, 1024, 0), _rows(tm, 256, 13), _rows(tm, 256, 14),
                            _rows(tm, 128, 30), _rows(tm, 128, 31), _full((1, Q_LORA)), _full((1, KV_LORA)),
                            _full((Q_LORA, 2048)), _full((KV_LORA, 2048)), _full((KV_LORA, 1024))] + tab,
                  out_specs=[_rows(tm, Q_LORA), _rows(tm, KV_LORA), _rows(tm, 2048), _rows(tm, 2048), _rows(tm, 1024),
                             _rows(tm, 1024), _rows(tm, 256), _rows(tm, 256)],
                  out_shape=[SDS((T, Q_LORA), BF16), SDS((T, KV_LORA), BF16), SDS((T, 2048), BF16), SDS((T, 2048), BF16),
                             SDS((T, 1024), BF16), SDS((T, 1024), BF16), SDS((T, 256), BF16), SDS((T, 256), BF16)],
                  sem=("parallel",))(z, z, z, z, z, z, gq, gkv, wqb, wkn, wv, *tab_m, *tab_s)


def _mla_fwd(qm, km, vm, tb):
    T = qm.shape[0]

    def body(q_ref, k_ref, v_ref, o_ref, l_ref):
        i = pl.program_id(1)
        lane = lax.broadcasted_iota(jnp.int32, (tb, LANES), 1)
        row = lax.broadcasted_iota(jnp.int32, (tb, tb), 0)
        col = lax.broadcasted_iota(jnp.int32, (tb, tb), 1)
        outs, lses = [], []
        for hh in range(2):
            sl = slice(LANES * hh, LANES * (hh + 1))
            q = q_ref[:, sl]

            def step(j, carry, q=q, sl=sl):
                m, l, acc = carry
                off = pl.multiple_of(j * tb, tb)
                k = k_ref[pl.ds(off, tb), sl]
                v = v_ref[pl.ds(off, tb), :]
                s = _dot_nt(q, k) * MLA_SCALE
                s = jnp.where(col + j * tb <= row + i * tb, s, NEG)
                mn = jnp.maximum(m, jnp.max(s, axis=1, keepdims=True))
                al = jnp.exp(m - mn)
                p = jnp.exp(s - mn)
                l = al * l + jnp.sum(p, axis=1, keepdims=True)
                acc = al * acc + _dot(p.astype(BF16), v)
                return mn, l, acc

            m, l, acc = lax.fori_loop(0, i + 1, step, (jnp.full((tb, 1), NEG, F32), jnp.zeros((tb, 1), F32),
                                                       jnp.zeros((tb, LANES), F32)))
            outs.append(acc / l)
            lses.append(jnp.broadcast_to(m + jnp.log(l), (tb, LANES)))
        o_ref[...] = jnp.where(lane < 64, outs[0], outs[1])
        l_ref[...] = jnp.where(lane < 64, lses[0], lses[1])

    return _pcall(body, name="mla_fwd", grid=(MLA_HEADS // 2, T // tb),
                  in_specs=[pl.BlockSpec((tb, 256), lambda p, i: (i, p)), pl.BlockSpec((T, 256), lambda p, i: (0, p)),
                            pl.BlockSpec((T, LANES), lambda p, i: (0, p))],
                  out_specs=[pl.BlockSpec((tb, LANES), lambda p, i: (i, p))] * 2,
                  out_shape=[SDS((T, D), F32)] * 2, sem=("parallel", "arbitrary"))(qm, km, vm)


def _swa_mask(n):
    row = lax.broadcasted_iota(jnp.int32, (WINDOW, 2 * WINDOW), 0)
    col = lax.broadcasted_iota(jnp.int32, (WINDOW, 2 * WINDOW), 1)
    rel = row - col + WINDOW
    return (rel >= 0) & (rel < WINDOW) & ((col >= WINDOW) | (n > 0))


def _swa_specs(T):
    nb = T // WINDOW
    cur = lambda w: pl.BlockSpec((WINDOW, w), lambda n: (n, 0))
    prev = lambda w: pl.BlockSpec((WINDOW, w), lambda n: (jnp.maximum(n - 1, 0), 0))
    return nb, cur, prev


def _swa_fwd(sinks, qs, ks, vs):
    T = qs.shape[0]
    nb, cur, prev = _swa_specs(T)

    def body(sink_ref, q_ref, kc_ref, kp_ref, vc_ref, vp_ref, o_ref, l_ref):
        n = pl.program_id(0)
        mask = _swa_mask(n)
        lo = lax.broadcasted_iota(jnp.int32, (WINDOW, LANES), 1) < 64
        for g in range(2):
            gs = slice(LANES * g, LANES * (g + 1))
            kb = jnp.concatenate([kp_ref[:, gs], kc_ref[:, gs]], axis=0)
            vb = jnp.concatenate([vp_ref[:, gs], vc_ref[:, gs]], axis=0)
            for jj in range(4):
                j = 4 * g + jj
                sl = slice(LANES * j, LANES * (j + 1))
                qp = q_ref[:, sl]
                outs, lses = [], []
                for hf in range(2):
                    hm = lo if hf == 0 else jnp.logical_not(lo)
                    qh = jnp.where(hm, qp, jnp.zeros_like(qp))
                    s = jnp.where(mask, _dot_nt(qh, kb) * SWA_SCALE, NEG)
                    sk = sink_ref[2 * j + hf]
                    m = jnp.maximum(jnp.max(s, axis=1, keepdims=True), sk)
                    e = jnp.exp(s - m)
                    den = jnp.sum(e, axis=1, keepdims=True) + jnp.exp(sk - m)
                    p = e / den
                    outs.append(_dot(p.astype(BF16), vb))
                    lses.append(jnp.broadcast_to(m + jnp.log(den), (WINDOW, LANES)))
                o_ref[:, sl] = jnp.where(lo, outs[0], outs[1])
                l_ref[:, sl] = jnp.where(lo, lses[0], lses[1])

    return _pcall(body, name="swa_fwd", grid=(nb,),
                  in_specs=[pl.BlockSpec(memory_space=pltpu.SMEM), cur(D), cur(256), prev(256), cur(256), prev(256)],
                  out_specs=[cur(D), cur(D)], out_shape=[SDS((T, D), F32)] * 2,
                  sem=("parallel",))(sinks, qs, ks, ks, vs, vs)


def _fwd_mix(om, os_, z, x, wmu, wsu, wo, g2, tm):
    T = x.shape[0]

    def body(om_ref, os_ref, ga_ref, gb_ref, x_ref, wmu_ref, wsu_ref, wo_ref, g2_ref,
             y_ref, yo_ref, au_ref, bu_ref, x1_ref):
        au = _dot(om_ref[...].astype(BF16), wmu_ref[...])
        bu = _dot(os_ref[...].astype(BF16), wsu_ref[...])
        au_ref[...] = au
        bu_ref[...] = bu
        y = (_sigmoid(ga_ref[...]) * au + _sigmoid(gb_ref[...]) * bu).astype(BF16)
        y_ref[...] = y
        yo = _dot(y, wo_ref[...])
        yo_ref[...] = yo
        x1_ref[...] = x_ref[...] + _rms(yo, g2_ref[...])

    r = _rows(tm, D)
    w = _full((D, D))
    return _pcall(body, name="fwd_mix", grid=(T // tm,),
                  in_specs=[r, r, _rows(tm, D, 1), _rows(tm, D, 2), r, w, w, w, _full((1, D))],
                  out_specs=[r] * 5,
                  out_shape=[SDS((T, D), BF16), SDS((T, D), F32), SDS((T, D), F32), SDS((T, D), F32), SDS((T, D), F32)],
                  sem=("parallel",))(om, os_, z, z, x, wmu, wsu, wo, g2)


def _fwd_mlp_up(x1, g3, w1, tm):
    T = x1.shape[0]

    def body(x_ref, g_ref, w_ref, h_ref, a_ref, u_ref):
        h = _rms(x_ref[...], g_ref[...]).astype(BF16)
        h_ref[...] = h
        a = _dot(h, w_ref[...])
        a_ref[...] = a
        u_ref[...] = jnp.square(jnp.maximum(a, 0.0)).astype(BF16)

    return _pcall(body, name="fwd_mlp_up", grid=(T // tm,),
                  in_specs=[_rows(tm, D), _full((1, D)), _full((D, D_FF))],
                  out_specs=[_rows(tm, D), _rows(tm, D_FF), _rows(tm, D_FF)],
                  out_shape=[SDS((T, D), BF16), SDS((T, D_FF), F32), SDS((T, D_FF), BF16)],
                  sem=("parallel",))(x1, g3, w1)


def _fwd_mlp_down(u, w2, x1, g4, tm):
    T = x1.shape[0]

    def body(u_ref, w_ref, x_ref, g_ref, d_ref, x2_ref):
        d = _dot(u_ref[...], w_ref[...])
        d_ref[...] = d
        x2_ref[...] = x_ref[...] + _rms(d, g_ref[...])

    return _pcall(body, name="fwd_mlp_down", grid=(T // tm,),
                  in_specs=[_rows(tm, D_FF), _full((D_FF, D)), _rows(tm, D), _full((1, D))],
                  out_specs=[_rows(tm, D), _rows(tm, D)], out_shape=[SDS((T, D), F32)] * 2,
                  sem=("parallel",))(u, w2, x1, g4)


def _ple_fwd_bwd(p, x2, tgt, wple, g5, wpg, tm):
    T = x2.shape[0]

    def body(p_ref, x2_ref, t_ref, wple_ref, g5_ref, wpg_ref, loss_ref, dx2_ref, dgt_ref, de0_ref, dg5_ref):
        @pl.when(pl.program_id(0) == 0)
        def _():
            loss_ref[...] = jnp.zeros_like(loss_ref)
            dg5_ref[...] = jnp.zeros_like(dg5_ref)

        e0 = _dot(p_ref[...].astype(BF16), wple_ref[...])
        g5 = g5_ref[...]
        r = lax.rsqrt(jnp.mean(e0 * e0, axis=-1, keepdims=True) + EPS)
        en = e0 * r
        e = en * g5
        x2 = x2_ref[...]
        s = _sigmoid(_dot(x2.astype(BF16), wpg_ref[...]))
        diff = x2 + s * e - t_ref[...]
        sq = jnp.sum(jnp.sum(diff * diff, axis=1, keepdims=True), axis=0, keepdims=True)
        loss_ref[...] += jnp.broadcast_to(sq * (0.5 / D), loss_ref.shape)
        dx3 = diff * (1.0 / D)
        de = dx3 * s
        dgt = (dx3 * e * s * (1.0 - s)).astype(BF16)
        dgt_ref[...] = dgt
        dn = de * g5
        de0_ref[...] = (r * (dn - en * jnp.mean(dn * en, axis=-1, keepdims=True))).astype(BF16)
        dg5_ref[...] += jnp.sum(de * en, axis=0, keepdims=True)
        dx2_ref[...] = dx3 + _dot_nt(dgt, wpg_ref[...])

    r = _rows(tm, D)
    return _pcall(body, name="ple_fwd_bwd", grid=(T // tm,),
                  in_specs=[_rows(tm, PLE), r, r, _full((PLE, D)), _full((1, D)), _full((D, D))],
                  out_specs=[_full((8, LANES)), r, r, r, _full((1, D))],
                  out_shape=[SDS((8, LANES), F32), SDS((T, D), F32), SDS((T, D), BF16), SDS((T, D), BF16), SDS((1, D), F32)],
                  sem=("arbitrary",))(p, x2, tgt, wple, g5, wpg)


def _bwd_mlp_down(dx2, d, g4, w2, a, tm):
    T = dx2.shape[0]

    def body(dx_ref, d_ref, g_ref, w_ref, a_ref, dd_ref, da_ref, dg_ref):
        @pl.when(pl.program_id(0) == 0)
        def _():
            dg_ref[...] = jnp.zeros_like(dg_ref)

        dd, dg = _rms_bwd(dx_ref[...], d_ref[...], g_ref[...])
        dg_ref[...] += dg
        ddb = dd.astype(BF16)
        dd_ref[...] = ddb
        du = _dot_nt(ddb, w_ref[...])
        da_ref[...] = (du * (2.0 * jnp.maximum(a_ref[...], 0.0))).astype(BF16)

    return _pcall(body, name="bwd_mlp_down", grid=(T // tm,),
                  in_specs=[_rows(tm, D), _rows(tm, D), _full((1, D)), _full((D_FF, D)), _rows(tm, D_FF)],
                  out_specs=[_rows(tm, D), _rows(tm, D_FF), _full((1, D))],
                  out_shape=[SDS((T, D), BF16), SDS((T, D_FF), BF16), SDS((1, D), F32)],
                  sem=("arbitrary",))(dx2, d, g4, w2, a)


def _bwd_mlp_up(da, w1, x1, g3, dx2, tm):
    T = dx2.shape[0]

    def body(da_ref, w_ref, x_ref, g_ref, dx2_ref, dx1_ref, dg_ref):
        @pl.when(pl.program_id(0) == 0)
        def _():
            dg_ref[...] = jnp.zeros_like(dg_ref)

        dh = _dot_nt(da_ref[...], w_ref[...])
        dx, dg = _rms_bwd(dh, x_ref[...], g_ref[...])
        dg_ref[...] += dg
        dx1_ref[...] = dx2_ref[...] + dx

    return _pcall(body, name="bwd_mlp_up", grid=(T // tm,),
                  in_specs=[_rows(tm, D_FF), _full((D, D_FF)), _rows(tm, D), _full((1, D)), _rows(tm, D)],
                  out_specs=[_rows(tm, D), _full((1, D))],
                  out_shape=[SDS((T, D), F32), SDS((1, D), F32)], sem=("arbitrary",))(da, w1, x1, g3, dx2)


def _bwd_mix(dx1, yo, g2, wo, z, au, bu, wmu, wsu, tm):
    T = dx1.shape[0]

    def body(dx_ref, yo_ref, g_ref, wo_ref, ga_ref, gb_ref, au_ref, bu_ref, wmu_ref, wsu_ref,
             dyo_ref, dg_ref, dau_ref, dbu_ref, dga_ref, dgb_ref, dom_ref, dos_ref):
        @pl.when(pl.program_id(0) == 0)
        def _():
            dg_ref[...] = jnp.zeros_like(dg_ref)

        dyo, dg = _rms_bwd(dx_ref[...], yo_ref[...], g_ref[...])
        dg_ref[...] += dg
        dyob = dyo.astype(BF16)
        dyo_ref[...] = dyob
        dy = _dot_nt(dyob, wo_ref[...])
        sa = _sigmoid(ga_ref[...])
        sb = _sigmoid(gb_ref[...])
        dau = (dy * sa).astype(BF16)
        dbu = (dy * sb).astype(BF16)
        dau_ref[...] = dau
        dbu_ref[...] = dbu
        dga_ref[...] = (dy * au_ref[...] * sa * (1.0 - sa)).astype(BF16)
        dgb_ref[...] = (dy * bu_ref[...] * sb * (1.0 - sb)).astype(BF16)
        dom_ref[...] = _dot_nt(dau, wmu_ref[...])
        dos_ref[...] = _dot_nt(dbu, wsu_ref[...])

    r = _rows(tm, D)
    w = _full((D, D))
    return _pcall(body, name="bwd_mix", grid=(T // tm,),
                  in_specs=[r, r, _full((1, D)), w, _rows(tm, D, 1), _rows(tm, D, 2), r, r, w, w],
                  out_specs=[r, _full((1, D)), r, r, r, r, r, r],
                  out_shape=[SDS((T, D), BF16), SDS((1, D), F32), SDS((T, D), BF16), SDS((T, D), BF16), SDS((T, D), BF16),
                             SDS((T, D), BF16), SDS((T, D), F32), SDS((T, D), F32)],
                  sem=("arbitrary",))(dx1, yo, g2, wo, z, z, au, bu, wmu, wsu)


def _mla_bwd(qm, km, vm, do, o, lse, tb):
    T = qm.shape[0]
    nb = T // tb

    def body(q_ref, k_ref, v_ref, do_ref, o_ref, l_ref, dq_ref, dk_ref, dv_ref):
        j = pl.program_id(1)

        @pl.when(j == 0)
        def _():
            dq_ref[...] = jnp.zeros_like(dq_ref)

        dk_ref[...] = jnp.zeros_like(dk_ref)
        dv_ref[...] = jnp.zeros_like(dv_ref)
        lo = lax.broadcasted_iota(jnp.int32, (tb, LANES), 1) < 64
        row = lax.broadcasted_iota(jnp.int32, (tb, tb), 0)
        col = lax.broadcasted_iota(jnp.int32, (tb, tb), 1)
        v = v_ref[...]

        def step(i, carry):
            off = pl.multiple_of(i * tb, tb)
            rows = pl.ds(off, tb)
            d_o = do_ref[rows, :]
            prod = d_o * o_ref[rows, :]
            lse_b = l_ref[rows, :]
            msk = col + j * tb <= row + i * tb
            for hh in range(2):
                sl = slice(LANES * hh, LANES * (hh + 1))
                hm = lo if hh == 0 else jnp.logical_not(lo)
                q = q_ref[rows, sl]
                k = k_ref[:, sl]
                s = _dot_nt(q, k) * MLA_SCALE
                lse_h = jnp.max(jnp.where(hm, lse_b, -jnp.inf), axis=1, keepdims=True)
                p = jnp.where(msk, jnp.exp(s - lse_h), 0.0)
                dom = jnp.where(hm, d_o, 0.0).astype(BF16)
                dp = _dot_nt(dom, v)
                delta = jnp.sum(jnp.where(hm, prod, 0.0), axis=1, keepdims=True)
                ds = (p * (dp - delta) * MLA_SCALE).astype(BF16)
                dv_ref[...] += _dot_tn(p.astype(BF16), dom)
                dk_ref[:, sl] += _dot_tn(ds, q)
                dq_ref[rows, sl] += _dot(ds, k)
            return carry

        lax.fori_loop(j, nb, step, 0)

    pair = lambda w: pl.BlockSpec((T, w), lambda p, j: (0, p))
    blk = lambda w: pl.BlockSpec((tb, w), lambda p, j: (j, p))
    return _pcall(body, name="mla_bwd", grid=(MLA_HEADS // 2, nb),
                  in_specs=[pair(256), blk(256), blk(LANES), pair(LANES), pair(LANES), pair(LANES)],
                  out_specs=[pair(256), blk(256), blk(LANES)],
                  out_shape=[SDS((T, 2048), F32), SDS((T, 2048), F32), SDS((T, D), F32)],
                  sem=("parallel", "arbitrary"))(qm, km, vm, do, o, lse)


def _swa_bwd(sinks, qs, ks, vs, do, o, lse):
    T = qs.shape[0]
    nb, cur, prev = _swa_specs(T)

    def body(sink_ref, q_ref, kc_ref, kp_ref, vc_ref, vp_ref, do_ref, o_ref, l_ref,
             dq_ref, dkc_ref, dkp_ref, dvc_ref, dvp_ref, dsink_ref):
        n = pl.program_id(0)

        @pl.when(n == 0)
        def _():
            dsink_ref[...] = jnp.zeros_like(dsink_ref)

        mask = _swa_mask(n)
        lo = lax.broadcasted_iota(jnp.int32, (WINDOW, LANES), 1) < 64
        lane8 = lax.broadcasted_iota(jnp.int32, (8, LANES), 1)
        dsink = jnp.zeros((8, LANES), F32)
        for g in range(2):
            gs = slice(LANES * g, LANES * (g + 1))
            kb = jnp.concatenate([kp_ref[:, gs], kc_ref[:, gs]], axis=0)
            vb = jnp.concatenate([vp_ref[:, gs], vc_ref[:, gs]], axis=0)
            dkb = jnp.zeros((2 * WINDOW, LANES), F32)
            dvb = jnp.zeros((2 * WINDOW, LANES), F32)
            for jj in range(4):
                j = 4 * g + jj
                sl = slice(LANES * j, LANES * (j + 1))
                qp = q_ref[:, sl]
                d_o = do_ref[:, sl]
                prod = d_o * o_ref[:, sl]
                lse_b = l_ref[:, sl]
                dqs = []
                for hf in range(2):
                    hm = lo if hf == 0 else jnp.logical_not(lo)
                    qh = jnp.where(hm, qp, jnp.zeros_like(qp))
                    s = jnp.where(mask, _dot_nt(qh, kb) * SWA_SCALE, NEG)
                    lse_h = jnp.max(jnp.where(hm, lse_b, -jnp.inf), axis=1, keepdims=True)
                    p = jnp.exp(s - lse_h)
                    dom = jnp.where(hm, d_o, 0.0).astype(BF16)
                    dp = _dot_nt(dom, vb)
                    delta = jnp.sum(jnp.where(hm, prod, 0.0), axis=1, keepdims=True)
                    ds = (p * (dp - delta) * SWA_SCALE).astype(BF16)
                    p_sink = jnp.exp(sink_ref[2 * j + hf] - lse_h)
                    d_sink = -jnp.sum(p_sink * delta, axis=0, keepdims=True)
                    dsink = dsink + jnp.where(lane8 == 2 * j + hf, d_sink, 0.0)
                    dvb = dvb + _dot_tn(p.astype(BF16), dom)
                    dkb = dkb + _dot_tn(ds, qh)
                    dqs.append(_dot(ds, kb))
                dq_ref[:, sl] = jnp.where(lo, dqs[0], dqs[1])
            dkp_ref[:, gs] = dkb[:WINDOW]
            dkc_ref[:, gs] = dkb[WINDOW:]
            dvp_ref[:, gs] = dvb[:WINDOW]
            dvc_ref[:, gs] = dvb[WINDOW:]
        dsink_ref[...] += dsink

    return _pcall(body, name="swa_bwd", grid=(nb,),
                  in_specs=[pl.BlockSpec(memory_space=pltpu.SMEM), cur(D), cur(256), prev(256), cur(256), prev(256),
                            cur(D), cur(D), cur(D)],
                  out_specs=[cur(D), cur(256), cur(256), cur(256), cur(256), _full((8, LANES))],
                  out_shape=[SDS((T, D), F32), SDS((T, 256), F32), SDS((T, 256), F32), SDS((T, 256), F32), SDS((T, 256), F32),
                             SDS((8, LANES), F32)],
                  sem=("arbitrary",))(sinks, qs, ks, ks, vs, vs, do, o, lse)


def _bwd_qkv(dqm, dkm, dvm, dqs, dkc, dkp, dvc, dvp, z, gq, gkv, wqb, wkn, wv, tab_m, tab_s):
    T = z.shape[0]
    tm = WINDOW
    nb = T // tm

    def body(dqm_ref, dkm_ref, dvm_ref, dqs_ref, dkc_ref, dkp_ref, dvc_ref, dvp_ref, qa_ref, kva_ref, gq_ref, gkv_ref,
             wqb_ref, wkn_ref, wv_ref, cm_ref, am_ref, bm_ref, cs_ref, as_ref, bs_ref,
             dq_out, dkn_out, dv_out, dsq_ref, drest_ref, dgq_ref, dgkv_ref):
        i = pl.program_id(0)

        @pl.when(i == 0)
        def _():
            dgq_ref[...] = jnp.zeros_like(dgq_ref)
            dgkv_ref[...] = jnp.zeros_like(dgkv_ref)

        cm, am, bm = cm_ref[...], -am_ref[...], -bm_ref[...]
        cs, as_, bs = cs_ref[...], -as_ref[...], -bs_ref[...]
        lane = lax.broadcasted_iota(jnp.int32, (tm, LANES), 1)
        nope = lane < MLA_NOPE
        roped = jnp.logical_and(lane >= MLA_NOPE, lane < MLA_NOPE + MLA_ROPE)
        dkr = jnp.zeros((tm, LANES), F32)
        dqn = jnp.zeros((tm, Q_LORA), F32)
        dkvn = jnp.zeros((tm, KV_LORA), F32)
        for h in range(MLA_HEADS):
            sl = slice(LANES * h, LANES * (h + 1))
            dq_h = _rope(dqm_ref[:, sl], cm, am, bm, MLA_ROPE // 2).astype(BF16)
            dq_out[:, sl] = dq_h
            dqn = dqn + _dot_nt(dq_h, wqb_ref[:, sl])
            dk_h = dkm_ref[:, sl]
            dkn_h = jnp.where(nope, dk_h, 0.0).astype(BF16)
            dkn_out[:, sl] = dkn_h
            dkvn = dkvn + _dot_nt(dkn_h, wkn_ref[:, sl])
            dkr = dkr + jnp.where(roped, dk_h, 0.0)
        dvb = dvm_ref[...].astype(BF16)
        dv_out[...] = dvb
        dkvn = dkvn + _dot_nt(dvb, wv_ref[...])
        dqa, dgq = _rms_bwd(dqn, qa_ref[...], gq_ref[...])
        dkva, dgkv = _rms_bwd(dkvn, kva_ref[...], gkv_ref[...])
        dgq_ref[...] += dgq
        dgkv_ref[...] += dgkv
        for j in range(D // LANES):
            sl = slice(LANES * j, LANES * (j + 1))
            dsq_ref[:, sl] = _rope(dqs_ref[:, sl], cs, as_, bs, SWA_HD // 2).astype(BF16)
        keep = (i < nb - 1).astype(F32)
        drest_ref[:, 0:256] = dqa.astype(BF16)
        for j in range(2):
            sl = slice(LANES * j, LANES * (j + 1))
            dk = dkc_ref[:, sl] + keep * dkp_ref[:, sl]
            drest_ref[:, 256 + LANES * j:256 + LANES * (j + 1)] = _rope(dk, cs, as_, bs, SWA_HD // 2).astype(BF16)
        drest_ref[:, 512:768] = (dvc_ref[...] + keep * dvp_ref[...]).astype(BF16)
        drest_ref[:, 768:896] = dkva.astype(BF16)
        drest_ref[:, 896:1024] = _rope(dkr, cm, am, bm, MLA_ROPE // 2).astype(BF16)

    nxt = pl.BlockSpec((tm, 256), lambda i: (jnp.minimum(i + 1, nb - 1), 0))
    tab = [_rows(tm, LANES)] * 6
    return _pcall(body, name="bwd_qkv", grid=(nb,),
                  in_specs=[_rows(tm, 2048), _rows(tm, 2048), _rows(tm, 1024), _rows(tm, 1024), _rows(tm, 256), nxt,
                            _rows(tm, 256), nxt, _rows(tm, 256, 12), _rows(tm, 128, 30), _full((1, Q_LORA)), _full((1, KV_LORA)),
                            _full((Q_LORA, 2048)), _full((KV_LORA, 2048)), _full((KV_LORA, 1024))] + tab,
                  out_specs=[_rows(tm, 2048), _rows(tm, 2048), _rows(tm, 1024), _rows(tm, 1024), _rows(tm, 1024),
                             _full((1, Q_LORA)), _full((1, KV_LORA))],
                  out_shape=[SDS((T, 2048), BF16), SDS((T, 2048), BF16), SDS((T, 1024), BF16), SDS((T, 1024), BF16),
                             SDS((T, 1024), BF16), SDS((1, Q_LORA), F32), SDS((1, KV_LORA), F32)],
                  sem=("arbitrary",))(dqm, dkm, dvm, dqs, dkc, dkp, dvc, dvp, z, z, gq, gkv, wqb, wkn, wv, *tab_m, *tab_s)


def _bwd_in(dsq, dga, dgb, drest, w_in_p, x, g1, dx1, tm):
    T = x.shape[0]

    def body(a_ref, b_ref, c_ref, d_ref, w_ref, x_ref, g_ref, dx1_ref, dx_ref, dg_ref):
        @pl.when(pl.program_id(0) == 0)
        def _():
            dg_ref[...] = jnp.zeros_like(dg_ref)

        dh = (_dot_nt(a_ref[...], w_ref[:, 0:1024]) + _dot_nt(b_ref[...], w_ref[:, 1024:2048])
              + _dot_nt(c_ref[...], w_ref[:, 2048:3072]) + _dot_nt(d_ref[...], w_ref[:, 3072:4096]))
        dx, dg = _rms_bwd(dh, x_ref[...], g_ref[...])
        dg_ref[...] += dg
        dx_ref[...] = dx1_ref[...] + dx

    r = _rows(tm, D)
    return _pcall(body, name="bwd_in", grid=(T // tm,),
                  in_specs=[r, r, r, r, _full((D, NZ)), r, _full((1, D)), r],
                  out_specs=[r, _full((1, D))], out_shape=[SDS((T, D), F32), SDS((1, D), F32)],
                  sem=("arbitrary",))(dsq, dga, dgb, drest, w_in_p, x, g1, dx1)


def _wgrad(a, g, name):
    T, K = a.shape
    N = g.shape[1]
    tk, tn, tt = min(K, 512), min(N, 1024), min(T, 512)

    def body(a_ref, g_ref, o_ref):
        @pl.when(pl.program_id(2) == 0)
        def _():
            o_ref[...] = jnp.zeros_like(o_ref)

        o_ref[...] += _dot_tn(a_ref[...].astype(BF16), g_ref[...].astype(BF16))

    return _pcall(body, name=name, grid=(K // tk, N // tn, T // tt),
                  in_specs=[pl.BlockSpec((tt, tk), lambda k, n, t: (t, k)), pl.BlockSpec((tt, tn), lambda k, n, t: (t, n))],
                  out_specs=pl.BlockSpec((tk, tn), lambda k, n, t: (k, n)), out_shape=SDS((K, N), F32),
                  sem=("parallel", "parallel", "arbitrary"))(a, g)


def _adamw(w, g, m, v, name):
    R, C = w.shape
    tr = min(R, 256)

    def body(w_ref, g_ref, m_ref, v_ref, d_ref, m2_ref, v2_ref):
        g_ = g_ref[...]
        m2 = ADAM_B1 * m_ref[...] + (1.0 - ADAM_B1) * g_
        v2 = ADAM_B2 * v_ref[...] + (1.0 - ADAM_B2) * jnp.square(g_)
        m_hat = m2 / (1.0 - ADAM_B1 ** ADAM_STEP)
        v_hat = v2 / (1.0 - ADAM_B2 ** ADAM_STEP)
        d_ref[...] = -ADAM_LR * (m_hat / (jnp.sqrt(v_hat) + ADAM_EPS) + ADAM_WD * w_ref[...])
        m2_ref[...] = m2
        v2_ref[...] = v2

    r = _rows(tr, C)
    return _pcall(body, name=name, grid=(R // tr,), in_specs=[r] * 4, out_specs=[r] * 3,
                  out_shape=[SDS((R, C), F32)] * 3, sem=("parallel",))(w, g, m, v)


def _adamw_small(w, parts, m, v):
    def body(w_ref, p_ref, m_ref, v_ref, g_ref, d_ref, m2_ref, v2_ref):
        g_ = p_ref[0]
        for k in range(1, N_DEV):
            g_ = g_ + p_ref[k]
        g_ref[...] = g_
        m2 = ADAM_B1 * m_ref[...] + (1.0 - ADAM_B1) * g_
        v2 = ADAM_B2 * v_ref[...] + (1.0 - ADAM_B2) * jnp.square(g_)
        m_hat = m2 / (1.0 - ADAM_B1 ** ADAM_STEP)
        v_hat = v2 / (1.0 - ADAM_B2 ** ADAM_STEP)
        d_ref[...] = -ADAM_LR * (m_hat / (jnp.sqrt(v_hat) + ADAM_EPS) + ADAM_WD * w_ref[...])
        m2_ref[...] = m2
        v2_ref[...] = v2

    s = _full((8, D))
    return _pcall(body, name="adamw_small", grid=(1,), in_specs=[s, _full((N_DEV, 8, D)), s, s], out_specs=[s] * 4,
                  out_shape=[SDS((8, D), F32)] * 4, sem=("arbitrary",))(w, parts, m, v)


ANY = pl.BlockSpec(memory_space=pl.ANY)


def _place():
    x, y, c = lax.axis_index("x"), lax.axis_index("y"), lax.axis_index("c")
    chips = [(1 - x, y), (x, 1 - y), (1 - x, 1 - y)]
    return x, y, c, chips


def _all_gather(wpk):
    def body(in_ref, out_ref, send_sems, recv_sems, local_sem):
        x, y, c, chips = _place()
        half = pl.ds(pl.multiple_of(c * HALF, 16), HALF)
        other = pl.ds(pl.multiple_of((1 - c) * HALF, 16), HALF)

        def copy(k, src, dst, to):
            return pltpu.make_async_remote_copy(src_ref=src, dst_ref=dst, send_sem=send_sems.at[k], recv_sem=recv_sems.at[k],
                                                device_id=to, device_id_type=MESH)

        mine = pltpu.make_async_copy(in_ref, out_ref.at[2 * x + y], local_sem)
        mine.start()
        first = [copy(k, in_ref.at[half], out_ref.at[2 * x + y, half], (cx, cy, c)) for k, (cx, cy) in enumerate(chips)]
        for cp in first:
            cp.start()
        passed = []
        for k, (cx, cy) in enumerate(chips):
            slot = out_ref.at[2 * cx + cy, half]
            copy(k, slot, slot, (x, y, c)).wait_recv()
            fwd = copy(3 + k, slot, slot, (x, y, 1 - c))
            fwd.start()
            passed.append(fwd)
        for k, (cx, cy) in enumerate(chips):
            slot = out_ref.at[2 * cx + cy, other]
            copy(3 + k, slot, slot, (x, y, c)).wait_recv()
        for cp in first + passed:
            cp.wait_send()
        mine.wait()

    return _pcall(body, name="all_gather_weights", in_specs=[ANY], out_specs=ANY,
                  out_shape=SDS((N_CHIPS, PACK_PAD, D), BF16),
                  scratch=[pltpu.SemaphoreType.DMA((6,)), pltpu.SemaphoreType.DMA((6,)), pltpu.SemaphoreType.DMA])(wpk)


def _rs_sibling(gpk):
    def body(in_ref, out_ref, send_sem, recv_sem):
        x, y, c, _ = _place()
        theirs = pl.ds(pl.multiple_of((1 - c) * HALF, 8), HALF)
        cp = pltpu.make_async_remote_copy(src_ref=in_ref.at[:, theirs], dst_ref=out_ref, send_sem=send_sem, recv_sem=recv_sem,
                                          device_id=(x, y, 1 - c), device_id_type=MESH)
        cp.start()
        cp.wait()

    return _pcall(body, name="rs_sibling", in_specs=[ANY], out_specs=ANY, out_shape=SDS((N_CHIPS, HALF, D), F32),
                  scratch=[pltpu.SemaphoreType.DMA, pltpu.SemaphoreType.DMA])(gpk)


def _rs_add_sibling(cidx, gpk, got):
    th = HALF // 7
    nh = HALF // th

    def body(c_ref, a_ref, b_ref, o_ref):
        o_ref[...] = a_ref[...] + b_ref[...]

    gs = pltpu.PrefetchScalarGridSpec(
        num_scalar_prefetch=1, grid=(N_CHIPS, nh),
        in_specs=[pl.BlockSpec((1, th, D), lambda j, i, c: (j, c[0] * nh + i, 0)), pl.BlockSpec((1, th, D), lambda j, i, c: (j, i, 0))],
        out_specs=pl.BlockSpec((1, th, D), lambda j, i, c: (j, i, 0)))
    return pl.pallas_call(body, name="rs_add_sibling", grid_spec=gs, out_shape=SDS((N_CHIPS, HALF, D), F32),
                          compiler_params=pltpu.CompilerParams(dimension_semantics=("parallel", "parallel"),
                                                               vmem_limit_bytes=48 << 20))(cidx, gpk, got)


def _rs_chips(part, small):
    def body(p_ref, s_ref, o_ref, so_ref, send_sems, recv_sems, ssend_sems, srecv_sems, local_sems):
        x, y, c, chips = _place()
        me = 2 * x + y
        mine = pltpu.make_async_copy(p_ref.at[me], o_ref.at[me], local_sems.at[0])
        mine_s = pltpu.make_async_copy(s_ref, so_ref.at[4 * x + 2 * y + c], local_sems.at[1])
        mine.start()
        mine_s.start()
        sends = []
        for k, (cx, cy) in enumerate(chips):
            sends.append(pltpu.make_async_remote_copy(src_ref=p_ref.at[2 * cx + cy], dst_ref=o_ref.at[me], send_sem=send_sems.at[k],
                                                      recv_sem=recv_sems.at[k], device_id=(cx, cy, c), device_id_type=MESH))
        peers = [(x, y, 1 - c)] + [(cx, cy, c) for cx, cy in chips] + [(cx, cy, 1 - c) for cx, cy in chips]
        for k, to in enumerate(peers):
            sends.append(pltpu.make_async_remote_copy(src_ref=s_ref, dst_ref=so_ref.at[4 * x + 2 * y + c], send_sem=ssend_sems.at[k],
                                                      recv_sem=srecv_sems.at[k], device_id=to, device_id_type=MESH))
        for cp in sends:
            cp.start()
        for k, (cx, cy) in enumerate(chips):
            slot = o_ref.at[2 * cx + cy]
            pltpu.make_async_remote_copy(src_ref=slot, dst_ref=slot, send_sem=send_sems.at[k], recv_sem=recv_sems.at[k],
                                         device_id=(x, y, c), device_id_type=MESH).wait_recv()
        for k, (px, py, pc) in enumerate(peers):
            slot = so_ref.at[4 * px + 2 * py + pc]
            pltpu.make_async_remote_copy(src_ref=slot, dst_ref=slot, send_sem=ssend_sems.at[k], recv_sem=srecv_sems.at[k],
                                         device_id=(x, y, c), device_id_type=MESH).wait_recv()
        for cp in sends:
            cp.wait_send()
        mine.wait()
        mine_s.wait()

    return _pcall(body, name="rs_chips", in_specs=[ANY, ANY], out_specs=[ANY, ANY],
                  out_shape=[SDS((N_CHIPS, HALF, D), part.dtype), SDS((N_DEV, 8, D), F32)],
                  scratch=[pltpu.SemaphoreType.DMA((3,)), pltpu.SemaphoreType.DMA((3,)), pltpu.SemaphoreType.DMA((7,)),
                           pltpu.SemaphoreType.DMA((7,)), pltpu.SemaphoreType.DMA((2,))])(part, small)


def _rs_add_chips(parts):
    th = HALF // 7

    def body(p_ref, o_ref):
        o_ref[...] = ((p_ref[0].astype(F32) + p_ref[1].astype(F32)) + p_ref[2].astype(F32)) + p_ref[3].astype(F32)

    return _pcall(body, name="rs_add_chips", grid=(HALF // th,),
                  in_specs=[pl.BlockSpec((N_CHIPS, th, D), lambda i: (0, i, 0))], out_specs=_rows(th, D),
                  out_shape=SDS((HALF, D), F32), sem=("parallel",))(parts)


def _rs_join(total):
    def body(in_ref, out_ref, send_sem, recv_sem, local_sem):
        x, y, c, _ = _place()
        mine = pltpu.make_async_copy(in_ref, out_ref.at[c], local_sem)
        mine.start()
        cp = pltpu.make_async_remote_copy(src_ref=in_ref, dst_ref=out_ref.at[c], send_sem=send_sem, recv_sem=recv_sem,
                                          device_id=(x, y, 1 - c), device_id_type=MESH)
        cp.start()
        slot = out_ref.at[1 - c]
        pltpu.make_async_remote_copy(src_ref=slot, dst_ref=slot, send_sem=send_sem, recv_sem=recv_sem,
                                     device_id=(x, y, c), device_id_type=MESH).wait_recv()
        cp.wait_send()
        mine.wait()

    return _pcall(body, name="rs_join", in_specs=[ANY], out_specs=ANY, out_shape=SDS((2, HALF, D), F32),
                  scratch=[pltpu.SemaphoreType.DMA, pltpu.SemaphoreType.DMA, pltpu.SemaphoreType.DMA])(total)


def _pack_shards(blocks, dtype):
    rows = [blocks[n].astype(dtype).reshape(-1, D) for n, _, _ in BIG]
    rows.append(jnp.zeros((PACK_PAD - PACK_ROWS, D), dtype))
    return jnp.concatenate(rows, axis=0)


def _unpack_shards(pk):
    out, off = {}, 0
    for n, r, c in BIG:
        k = r * c // D
        out[n] = pk[off:off + k].reshape(r, c)
        off += k
    return out


def _full_weights(gathered):
    per_chip = [_unpack_shards(gathered[j]) for j in range(N_CHIPS)]
    return {n: jnp.concatenate([pc[n] for pc in per_chip], axis=1 if n in COL_SHARDED else 0) for n, _, _ in BIG}


def _split_full_grads(grads, dtype):
    chunks = []
    for j in range(N_CHIPS):
        blocks = {}
        for n, r, c in BIG:
            g = grads[n]
            blocks[n] = g[:, j * c:(j + 1) * c] if n in COL_SHARDED else g[j * r:(j + 1) * r]
        chunks.append(_pack_shards(blocks, dtype))
    return jnp.stack(chunks)


def _w_in_internal(w):
    z = lambda n: jnp.zeros((w.shape[0], n), w.dtype)
    sk0, sk1 = w[:, 1440:1504], w[:, 1504:1568]
    sv0, sv1 = w[:, 1568:1632], w[:, 1632:1696]
    return jnp.concatenate([w[:, 416:1440], w[:, 1696:3744], w[:, 0:256], sk0, sk0, sk1, sk1, sv0, sv0, sv1, sv1,
                            w[:, 256:384], z(64), w[:, 384:416], z(32)], axis=1)


def _w_in_external_grad(g):
    sk = [g[:, 3328 + 128 * j:3392 + 128 * j] + g[:, 3392 + 128 * j:3456 + 128 * j] for j in range(2)]
    sv = [g[:, 3584 + 128 * j:3648 + 128 * j] + g[:, 3648 + 128 * j:3712 + 128 * j] for j in range(2)]
    return jnp.concatenate([g[:, 3072:3328], g[:, 3840:3968], g[:, 4032:4064], g[:, 0:1024], *sk, *sv, g[:, 1024:3072]], axis=1)


def _local_step(x, p, tgt, w, small):
    T = x.shape[0]
    tm = 256
    tb = 256
    w_in_p = _w_in_internal(w["w_in"])
    wqb = jnp.pad(w["w_q_b"].reshape(Q_LORA, MLA_HEADS, 96), ((0, 0), (0, 0), (0, 32))).reshape(Q_LORA, 2048)
    wkv = w["w_kv_b"].reshape(KV_LORA, MLA_HEADS, 128)
    wkn = jnp.pad(wkv[:, :, :64], ((0, 0), (0, 0), (0, 64))).reshape(KV_LORA, 2048)
    wv = wkv[:, :, 64:].reshape(KV_LORA, 1024)
    tab_m = _rope_tables(T, "mla")
    tab_s = _rope_tables(T, "swa")
    g1, gq, gkv, sinks = small["g_mix_pre"], small["g_q_a"], small["g_kv_a"], small["sinks"]
    g2, g3, g4, g5 = small["g_mix_post"], small["g_mlp_pre"], small["g_mlp_post"], small["g_ple"]
    sink_vec = sinks.reshape(SWA_HEADS)

    z, h1 = _fwd_in(x, g1, w_in_p, tm)
    qn, kvn, qm, km, vm, qs, ks, vs = _fwd_qkv(z, gq, gkv, wqb, wkn, wv, tab_m, tab_s, tm)
    om, lse_m = _mla_fwd(qm, km, vm, tb)
    os_, lse_s = _swa_fwd(sink_vec, qs, ks, vs)
    y, yo, au, bu, x1 = _fwd_mix(om, os_, z, x, w["w_mla_up"], w["w_swa_up"], w["w_out"], g2, tm)
    h2, a, u = _fwd_mlp_up(x1, g3, w["w_mlp_up"], tm)
    d, x2 = _fwd_mlp_down(u, w["w_mlp_down"], x1, g4, tm)
    loss, dx2, dgt, de0, dg5 = _ple_fwd_bwd(p, x2, tgt, w["w_ple"], g5, w["w_ple_gate"], tm)

    dd, da, dg4 = _bwd_mlp_down(dx2, d, g4, w["w_mlp_down"], a, tm)
    dx1, dg3 = _bwd_mlp_up(da, w["w_mlp_up"], x1, g3, dx2, tm)
    dyo, dg2, dau, dbu, dga, dgb, dom, dos = _bwd_mix(dx1, yo, g2, w["w_out"], z, au, bu, w["w_mla_up"], w["w_swa_up"], tm)
    dqm, dkm, dvm = _mla_bwd(qm, km, vm, dom, om, lse_m, tb)
    dqs, dkc, dkp, dvc, dvp, dsink = _swa_bwd(sink_vec, qs, ks, vs, dos, os_, lse_s)
    dqb, dknb, dvb, dsq, drest, dgq, dgkv = _bwd_qkv(dqm, dkm, dvm, dqs, dkc, dkp, dvc, dvp, z, gq, gkv, wqb, wkn, wv,
                                                      tab_m, tab_s)
    gx, dg1 = _bwd_in(dsq, dga, dgb, drest, w_in_p, x, g1, dx1, tm)

    g_in_p = jnp.concatenate([_wgrad(h1, dsq, "wgrad_in_sq"), _wgrad(h1, dga, "wgrad_in_ga"), _wgrad(h1, dgb, "wgrad_in_gb"),
                              _wgrad(h1, drest, "wgrad_in_rest")], axis=1)
    g_qb_p = _wgrad(qn, dqb, "wgrad_q_b")
    g_kn_p = _wgrad(kvn, dknb, "wgrad_kv_b_nope")
    g_v_p = _wgrad(kvn, dvb, "wgrad_kv_b_v")
    grads = {
        "w_in": _w_in_external_grad(g_in_p),
        "w_q_b": g_qb_p.reshape(Q_LORA, MLA_HEADS, 128)[:, :, :96].reshape(Q_LORA, 1536),
        "w_kv_b": jnp.concatenate([g_kn_p.reshape(KV_LORA, MLA_HEADS, 128)[:, :, :64], g_v_p.reshape(KV_LORA, MLA_HEADS, 64)],
                                  axis=2).reshape(KV_LORA, 2048),
        "w_mla_up": _wgrad(om, dau, "wgrad_mla_up"),
        "w_swa_up": _wgrad(os_, dbu, "wgrad_swa_up"),
        "w_out": _wgrad(y, dyo, "wgrad_out"),
        "w_mlp_up": _wgrad(h2, da, "wgrad_mlp_up"),
        "w_mlp_down": _wgrad(u, dd, "wgrad_mlp_down"),
        "w_ple": _wgrad(p, de0, "wgrad_ple"),
        "w_ple_gate": _wgrad(x2, dgt, "wgrad_ple_gate"),
    }
    small_grads = {"g_mix_pre": dg1, "g_q_a": dgq, "g_kv_a": dgkv, "sinks": dsink[0:1, 0:SWA_HEADS], "g_mix_post": dg2,
                   "g_mlp_pre": dg3, "g_mlp_post": dg4, "g_ple": dg5}
    return loss, gx, grads, small_grads


def _pack_small(vals, fill):
    wide = [vals[n] for n, k in SMALL if k == D]
    narrow = [vals[n] for n, k in SMALL if k != D]
    used = sum(k for _, k in SMALL if k != D)
    last = jnp.concatenate(narrow + [jnp.full((1, D - used), fill, F32)], axis=1)
    return jnp.concatenate(wide + [last, jnp.full((2, D), fill, F32)], axis=0)


def _unpack_small(pk):
    out, row, off = {}, 0, 0
    for n, k in SMALL:
        if k == D:
            out[n] = pk[row:row + 1]
            row += 1
    for n, k in SMALL:
        if k != D:
            out[n] = pk[5:6, off:off + k]
            off += k
    return out


def kernel(x, p, g_mix_pre, w_in, g_q_a, w_q_b, g_kv_a, w_kv_b, sinks, w_mla_up, w_swa_up, w_out, g_mix_post, g_mlp_pre, w_mlp_up, w_mlp_down, g_mlp_post, w_ple, g_ple, w_ple_gate, loss_target, m_g_mix_pre, m_w_in, m_g_q_a, m_w_q_b, m_g_kv_a, m_w_kv_b, m_sinks, m_w_mla_up, m_w_swa_up, m_w_out, m_g_mix_post, m_g_mlp_pre, m_w_mlp_up, m_w_mlp_down, m_g_mlp_post, m_w_ple, m_g_ple, m_w_ple_gate, v_g_mix_pre, v_w_in, v_g_q_a, v_w_q_b, v_g_kv_a, v_w_kv_b, v_sinks, v_w_mla_up, v_w_swa_up, v_w_out, v_g_mix_post, v_g_mlp_pre, v_w_mlp_up, v_w_mlp_down, v_g_mlp_post, v_w_ple, v_g_ple, v_w_ple_gate):
    given = dict(locals())
    big_w = {n: given[n][0] for n, _, _ in BIG}
    big_m = {n: given["m_" + n][0] for n, _, _ in BIG}
    big_v = {n: given["v_" + n][0] for n, _, _ in BIG}
    small_w = {n: given[n] for n, _ in SMALL}
    small_m = {n: given["m_" + n] for n, _ in SMALL}
    small_v = {n: given["v_" + n] for n, _ in SMALL}

    gathered = _all_gather(_pack_shards(big_w, BF16))
    weights = _full_weights(gathered)
    loss_blk, gx, grads, small_grads = _local_step(x[0], p[0, 0], loss_target[0], weights, small_w)

    cidx = lax.axis_index("c").astype(jnp.int32).reshape(1)
    gpk = _split_full_grads(grads, F32)
    got = _rs_sibling(gpk)
    part = _rs_add_sibling(cidx, gpk, got)
    parts, small_parts = _rs_chips(part, _pack_small(small_grads, 0.0))
    joined = _rs_join(_rs_add_chips(parts))
    big_g = _unpack_shards(joined.reshape(PACK_PAD, D))

    loss = lax.psum(loss_blk[0, 0], ("x", "y", "c"))
    g_small_pk, d_small_pk, m_small_pk, v_small_pk = _adamw_small(
        _pack_small(small_w, 0.0), small_parts, _pack_small(small_m, 0.0), _pack_small(small_v, 1.0))
    g_small, d_small = _unpack_small(g_small_pk), _unpack_small(d_small_pk)
    m_small, v_small = _unpack_small(m_small_pk), _unpack_small(v_small_pk)

    out_g, out_d, out_m, out_v = dict(g_small), dict(d_small), dict(m_small), dict(v_small)
    for n, _, _ in BIG:
        d_, m_, v_ = _adamw(big_w[n], big_g[n], big_m[n], big_v[n], "adamw_" + n)
        out_g[n], out_d[n], out_m[n], out_v[n] = big_g[n][None], d_[None], m_[None], v_[None]
    order = ["g_mix_pre", "w_in", "g_q_a", "w_q_b", "g_kv_a", "w_kv_b", "sinks", "w_mla_up", "w_swa_up", "w_out", "g_mix_post",
             "g_mlp_pre", "w_mlp_up", "w_mlp_down", "g_mlp_post", "w_ple", "g_ple", "w_ple_gate"]
    return (loss, gx[None], *[out_g[n] for n in order], *[out_d[n] for n in order], *[out_m[n] for n in order],
            *[out_v[n] for n in order])
```

```python
import math

import jax
import jax.numpy as jnp
from jax import lax
from jax.experimental import pallas as pl
from jax.experimental.pallas import tpu as pltpu

F32 = jnp.float32
BF16 = jnp.bfloat16
SDS = jax.ShapeDtypeStruct

D = 1024
D_FF = 4096
PLE = 256
Q_LORA = 256
KV_LORA = 128
MLA_HEADS = 16
MLA_NOPE = 64
MLA_ROPE = 32
SWA_HEADS = 16
SWA_HD = 64
WINDOW = 128
ROPE_THETA = 10000.0
EPS = 1e-6
NEG = -1e30
NZ = 4096
MLA_SCALE = (MLA_NOPE + MLA_ROPE) ** -0.5
LOG2_E = math.log2(math.e)
MLA_LOG2_SCALE = MLA_SCALE * LOG2_E
SWA_SCALE = SWA_HD ** -0.5

ADAM_LR = 0.001
ADAM_B1 = 0.9
ADAM_B2 = 0.999
ADAM_EPS = 1e-08
ADAM_WD = 0.01
ADAM_STEP = 10

LANES = 128
ATT_COLS = 128
N_CHIPS = 4
N_DEV = 8
MESH = pl.DeviceIdType.MESH

NT = (((1,), (1,)), ((), ()))
TN = (((0,), (0,)), ((), ()))

BIG = (("w_in", 1024, 936), ("w_q_b", 256, 384), ("w_kv_b", 128, 512), ("w_mla_up", 256, 1024),
       ("w_swa_up", 256, 1024), ("w_out", 256, 1024), ("w_mlp_up", 1024, 1024), ("w_mlp_down", 1024, 1024),
       ("w_ple", 256, 256), ("w_ple_gate", 256, 1024))
COL_SHARDED = ("w_in", "w_q_b", "w_kv_b", "w_mlp_up", "w_ple")
PACK_ROWS = sum(r * c for _, r, c in BIG) // D
PACK_PAD = 4256
HALF = PACK_PAD // 2
SMALL = (("g_mix_pre", 1024), ("g_q_a", 256), ("g_kv_a", 128), ("sinks", 16), ("g_mix_post", 1024),
         ("g_mlp_pre", 1024), ("g_mlp_post", 1024), ("g_ple", 1024))


def _dot(a, b):
    return jnp.dot(a, b, preferred_element_type=F32)


def _dot_nt(a, b):
    return lax.dot_general(a, b, NT, preferred_element_type=F32)


def _dot_tn(a, b):
    return lax.dot_general(a, b, TN, preferred_element_type=F32)


def _pcall(body, *, name, out_shape, grid=(), in_specs=None, out_specs=None, scratch=(), sem=None, vmem_mb=48):
    params = dict(vmem_limit_bytes=vmem_mb << 20)
    if sem is not None:
        params["dimension_semantics"] = sem
    return pl.pallas_call(body, name=name, grid=grid, in_specs=in_specs, out_specs=out_specs, out_shape=out_shape,
                          scratch_shapes=list(scratch), compiler_params=pltpu.CompilerParams(**params))


def _rows(tm, n, col=0):
    return pl.BlockSpec((tm, n), lambda i: (i, col))


def _full(shape):
    return pl.BlockSpec(shape, lambda i: (0,) * len(shape))


def _rms(x, g):
    r = lax.rsqrt(jnp.mean(x * x, axis=-1, keepdims=True) + EPS)
    return x * r * g


def _rms_bwd(dy, x, g):
    r = lax.rsqrt(jnp.mean(x * x, axis=-1, keepdims=True) + EPS)
    xn = x * r
    dn = dy * g
    dx = r * (dn - xn * jnp.mean(dn * xn, axis=-1, keepdims=True))
    return dx, jnp.sum(dy * xn, axis=0, keepdims=True)


def _sigmoid(x):
    return 1.0 / (1.0 + jnp.exp(-x))


def _rope(x, c, a, b, half):
    return x * c + pltpu.roll(x, LANES - half, 1) * a + pltpu.roll(x, half, 1) * b


def _rope_tables(T, kind):
    lane = jnp.arange(LANES)
    if kind == "mla":
        half = MLA_ROPE // 2
        rel = lane - MLA_NOPE
        on = (rel >= 0) & (rel < MLA_ROPE)
        d = MLA_ROPE
    else:
        half = SWA_HD // 2
        rel = lane % SWA_HD
        on = jnp.ones((LANES,), bool)
        d = SWA_HD
    first = on & (rel < half)
    second = on & (rel >= half)
    f = jnp.where(first, rel, rel - half).astype(F32)
    inv = jnp.exp(-math.log(ROPE_THETA) * f * (2.0 / d))
    ang = jnp.arange(T, dtype=F32)[:, None] * inv[None, :]
    cos, sin = jnp.cos(ang), jnp.sin(ang)
    c = jnp.where(on[None], cos, 1.0)
    a = jnp.where(first[None], -sin, 0.0)
    b = jnp.where(second[None], sin, 0.0)
    return c, a, b


def _fwd_in(x, g1, w_in_p, tm):
    T = x.shape[0]

    def body(x_ref, g_ref, w_ref, z_ref, h_ref):
        h = _rms(x_ref[...], g_ref[...]).astype(BF16)
        h_ref[...] = h
        z_ref[...] = _dot(h, w_ref[...])

    return _pcall(body, name="fwd_in", grid=(T // tm,),
                  in_specs=[_rows(tm, D), _full((1, D)), _full((D, NZ))],
                  out_specs=[_rows(tm, NZ), _rows(tm, D)],
                  out_shape=[SDS((T, NZ), F32), SDS((T, D), BF16)], sem=("parallel",))(x, g1, w_in_p)


def _fwd_qkv(z, gq, gkv, wqb, wkn, wv, tab_m, tab_s, tm):
    T = z.shape[0]

    def body(qa_ref, sq_ref, skd_ref, svd_ref, kva_ref, kr_ref, gq_ref, gkv_ref, wqb_ref, wkn_ref, wv_ref,
             cm_ref, am_ref, bm_ref, cs_ref, as_ref, bs_ref,
             qn_ref, kvn_ref, qm_ref, km_ref, vm_ref, kt_ref, vt_ref, qs_ref, ks_ref, vs_ref):
        qn = _rms(qa_ref[...], gq_ref[...]).astype(BF16)
        qn_ref[...] = qn
        kvn = _rms(kva_ref[...], gkv_ref[...]).astype(BF16)
        kvn_ref[...] = kvn
        cm, am, bm = cm_ref[...], am_ref[...], bm_ref[...]
        cs, as_, bs = cs_ref[...], as_ref[...], bs_ref[...]
        k_rope = _rope(kr_ref[...], cm, am, bm, MLA_ROPE // 2)
        for j in range(D // LANES):
            sl = slice(LANES * j, LANES * (j + 1))
            v = _dot(kvn, wv_ref[:, sl])
            vm_ref[:, sl] = v.astype(BF16)
            vt_ref[0, sl, :] = v.T.astype(BF16)
        for h in range(MLA_HEADS):
            sl = slice(LANES * h, LANES * (h + 1))
            qh = _dot(qn, wqb_ref[:, sl])
            qm_ref[:, sl] = _rope(qh, cm, am, bm, MLA_ROPE // 2).astype(BF16)
            k = _dot(kvn, wkn_ref[:, sl]) + k_rope
            km_ref[:, sl] = k.astype(BF16)
            kt_ref[0, sl, :] = k.T.astype(BF16)
        for j in range(D // LANES):
            sl = slice(LANES * j, LANES * (j + 1))
            qs_ref[:, sl] = _rope(sq_ref[:, sl], cs, as_, bs, SWA_HD // 2).astype(BF16)
        for j in range(2):
            sl = slice(LANES * j, LANES * (j + 1))
            ks_ref[:, sl] = _rope(skd_ref[:, sl], cs, as_, bs, SWA_HD // 2).astype(BF16)
        vs_ref[...] = svd_ref[...].astype(BF16)

    tab = [_rows(tm, LANES)] * 6
    return _pcall(body, name="fwd_qkv", grid=(T // tm,),
                  in_specs=[_rows(tm, 256, 12), _rows(tm, 1024, 0), _rows(tm, 256, 13), _rows(tm, 256, 14),
                            _rows(tm, 128, 30), _rows(tm, 128, 31), _full((1, Q_LORA)), _full((1, KV_LORA)),
                            _full((Q_LORA, 2048)), _full((KV_LORA, 2048)), _full((KV_LORA, 1024))] + tab,
                  out_specs=[_rows(tm, Q_LORA), _rows(tm, KV_LORA), _rows(tm, 2048), _rows(tm, 2048), _rows(tm, 1024),
                             pl.BlockSpec((1, 2048, tm), lambda i: (i, 0, 0)), pl.BlockSpec((1, 1024, tm), lambda i: (i, 0, 0)),
                             _rows(tm, 1024), _rows(tm, 256), _rows(tm, 256)],
                  out_shape=[SDS((T, Q_LORA), BF16), SDS((T, KV_LORA), BF16), SDS((T, 2048), BF16), SDS((T, 2048), BF16),
                             SDS((T, 1024), BF16), SDS((T // tm, 2048, tm), BF16), SDS((T // tm, 1024, tm), BF16),
                             SDS((T, 1024), BF16), SDS((T, 256), BF16), SDS((T, 256), BF16)],
                  sem=("parallel",))(z, z, z, z, z, z, gq, gkv, wqb, wkn, wv, *tab_m, *tab_s)


def _mla_fwd(qm, km, vt, tb):
    T = qm.shape[0]
    nb = T // tb
    cc = ATT_COLS

    def body(q_ref, k_ref, vt_ref, o_ref, l_ref, s_ref, p_ref, m_ref, d_ref, acc_ref):
        i = pl.program_id(1)
        m_ref[...] = jnp.full(m_ref.shape, NEG, F32)
        d_ref[...] = jnp.zeros_like(d_ref)
        acc_ref[...] = jnp.zeros_like(acc_ref)
        key = lax.broadcasted_iota(jnp.int32, (tb, cc), 0)
        qry = lax.broadcasted_iota(jnp.int32, (tb, cc), 1)

        def scores(j, slot):
            off = pl.multiple_of(j * tb, tb)
            for hh in range(2):
                sl = slice(LANES * hh, LANES * (hh + 1))
                s_ref[slot, hh] = _dot_nt(k_ref[pl.ds(off, tb), sl], q_ref[:, sl])

        def softmax_pv(j, slot, diagonal):
            for hh in range(2):
                for c in range(tb // cc):
                    cols = slice(cc * c, cc * (c + 1))
                    if diagonal:
                        t = jnp.where(key <= qry + cc * c, s_ref[slot, hh, :, cols] * MLA_LOG2_SCALE, NEG)
                        top = jnp.max(t, axis=0, keepdims=True)
                    else:
                        top = jnp.max(s_ref[slot, hh, :, cols], axis=0, keepdims=True) * MLA_LOG2_SCALE
                    m_old = m_ref[hh, :, cols]
                    mn = jnp.maximum(m_old, top)
                    al = jnp.exp2(m_old - mn)
                    if diagonal:
                        p = jnp.exp2(t - mn)
                    else:
                        p = jnp.exp2(s_ref[slot, hh, :, cols] * MLA_LOG2_SCALE - mn)
                    m_ref[hh, :, cols] = mn
                    d_ref[hh, :, cols] = al * d_ref[hh, :, cols] + jnp.sum(p, axis=0, keepdims=True)
                    acc_ref[hh, :, cols] = al * acc_ref[hh, :, cols]
                    p_ref[hh, :, cols] = p.astype(BF16)
            v_t = vt_ref[j]
            for hh in range(2):
                acc_ref[hh] += _dot(v_t, p_ref[hh])

        def step(t, carry):
            scores(2 * t + 1, 1)
            softmax_pv(2 * t, 0, False)
            scores(2 * t + 2, 0)
            softmax_pv(2 * t + 1, 1, False)
            return carry

        scores(0, 0)
        lax.fori_loop(0, i // 2, step, 0)

        @pl.when(i % 2 == 1)
        def _():
            scores(i, 1)
            softmax_pv(i - 1, 0, False)
            softmax_pv(i, 1, True)

        @pl.when(i % 2 == 0)
        def _():
            softmax_pv(i, 0, True)
        first = lax.broadcasted_iota(jnp.int32, (LANES, tb), 0) < 64
        o_ref[...] = jnp.where(first, acc_ref[0] / d_ref[0], acc_ref[1] / d_ref[1]).T
        sub = lax.broadcasted_iota(jnp.int32, (8, tb), 0)
        lse = [m_ref[hh] + jnp.log(d_ref[hh]) * LOG2_E for hh in range(2)]
        l_ref[0, 0] = jnp.where(sub == 0, lse[0], jnp.where(sub == 1, lse[1], 0.0))

    return _pcall(body, name="mla_fwd", grid=(MLA_HEADS // 2, nb),
                  in_specs=[pl.BlockSpec((tb, 256), lambda p, i: (i, p)), pl.BlockSpec((T, 256), lambda p, i: (0, p)),
                            pl.BlockSpec((nb, LANES, tb), lambda p, i: (0, p, 0))],
                  out_specs=[pl.BlockSpec((tb, LANES), lambda p, i: (i, p)),
                             pl.BlockSpec((1, 1, 8, tb), lambda p, i: (p, i, 0, 0))],
                  out_shape=[SDS((T, D), F32), SDS((MLA_HEADS // 2, nb, 8, tb), F32)],
                  scratch=[pltpu.VMEM((2, 2, tb, tb), F32), pltpu.VMEM((2, tb, tb), BF16), pltpu.VMEM((2, 1, tb), F32),
                           pltpu.VMEM((2, 1, tb), F32), pltpu.VMEM((2, LANES, tb), F32)],
                  sem=("parallel", "arbitrary"))(qm, km, vt)


def _swa_mask(n):
    row = lax.broadcasted_iota(jnp.int32, (WINDOW, 2 * WINDOW), 0)
    col = lax.broadcasted_iota(jnp.int32, (WINDOW, 2 * WINDOW), 1)
    rel = row - col + WINDOW
    return (rel >= 0) & (rel < WINDOW) & ((col >= WINDOW) | (n > 0))


def _swa_specs(T):
    nb = T // WINDOW
    cur = lambda w: pl.BlockSpec((WINDOW, w), lambda n: (n, 0))
    prev = lambda w: pl.BlockSpec((WINDOW, w), lambda n: (jnp.maximum(n - 1, 0), 0))
    return nb, cur, prev


def _swa_fwd(sinks, qs, ks, vs):
    T = qs.shape[0]
    nb, cur, prev = _swa_specs(T)

    def body(sink_ref, q_ref, kc_ref, kp_ref, vc_ref, vp_ref, o_ref, l_ref):
        n = pl.program_id(0)
        mask = _swa_mask(n)
        lo = lax.broadcasted_iota(jnp.int32, (WINDOW, LANES), 1) < 64
        for g in range(2):
            gs = slice(LANES * g, LANES * (g + 1))
            kb = jnp.concatenate([kp_ref[:, gs], kc_ref[:, gs]], axis=0)
            vb = jnp.concatenate([vp_ref[:, gs], vc_ref[:, gs]], axis=0)
            for jj in range(4):
                j = 4 * g + jj
                sl = slice(LANES * j, LANES * (j + 1))
                qp = q_ref[:, sl]
                outs, lses = [], []
                for hf in range(2):
                    hm = lo if hf == 0 else jnp.logical_not(lo)
                    qh = jnp.where(hm, qp, jnp.zeros_like(qp))
                    s = jnp.where(mask, _dot_nt(qh, kb) * SWA_SCALE, NEG)
                    sk = sink_ref[2 * j + hf]
                    m = jnp.maximum(jnp.max(s, axis=1, keepdims=True), sk)
                    e = jnp.exp(s - m)
                    den = jnp.sum(e, axis=1, keepdims=True) + jnp.exp(sk - m)
                    p = e / den
                    outs.append(_dot(p.astype(BF16), vb))
                    lses.append(jnp.broadcast_to(m + jnp.log(den), (WINDOW, LANES)))
                o_ref[:, sl] = jnp.where(lo, outs[0], outs[1])
                l_ref[:, sl] = jnp.where(lo, lses[0], lses[1])

    return _pcall(body, name="swa_fwd", grid=(nb,),
                  in_specs=[pl.BlockSpec(memory_space=pltpu.SMEM), cur(D), cur(256), prev(256), cur(256), prev(256)],
                  out_specs=[cur(D), cur(D)], out_shape=[SDS((T, D), F32)] * 2,
                  sem=("parallel",))(sinks, qs, ks, ks, vs, vs)


def _fwd_mix(om, os_, z, x, wmu, wsu, wo, g2, tm):
    T = x.shape[0]

    def body(om_ref, os_ref, ga_ref, gb_ref, x_ref, wmu_ref, wsu_ref, wo_ref, g2_ref,
             y_ref, yo_ref, au_ref, bu_ref, x1_ref):
        au = _dot(om_ref[...].astype(BF16), wmu_ref[...])
        bu = _dot(os_ref[...].astype(BF16), wsu_ref[...])
        au_ref[...] = au
        bu_ref[...] = bu
        y = (_sigmoid(ga_ref[...]) * au + _sigmoid(gb_ref[...]) * bu).astype(BF16)
        y_ref[...] = y
        yo = _dot(y, wo_ref[...])
        yo_ref[...] = yo
        x1_ref[...] = x_ref[...] + _rms(yo, g2_ref[...])

    r = _rows(tm, D)
    w = _full((D, D))
    return _pcall(body, name="fwd_mix", grid=(T // tm,),
                  in_specs=[r, r, _rows(tm, D, 1), _rows(tm, D, 2), r, w, w, w, _full((1, D))],
                  out_specs=[r] * 5,
                  out_shape=[SDS((T, D), BF16), SDS((T, D), F32), SDS((T, D), F32), SDS((T, D), F32), SDS((T, D), F32)],
                  sem=("parallel",))(om, os_, z, z, x, wmu, wsu, wo, g2)


def _fwd_mlp_up(x1, g3, w1, tm):
    T = x1.shape[0]

    def body(x_ref, g_ref, w_ref, h_ref, a_ref, u_ref):
        h = _rms(x_ref[...], g_ref[...]).astype(BF16)
        h_ref[...] = h
        a = _dot(h, w_ref[...])
        a_ref[...] = a
        u_ref[...] = jnp.square(jnp.maximum(a, 0.0)).astype(BF16)

    return _pcall(body, name="fwd_mlp_up", grid=(T // tm,),
                  in_specs=[_rows(tm, D), _full((1, D)), _full((D, D_FF))],
                  out_specs=[_rows(tm, D), _rows(tm, D_FF), _rows(tm, D_FF)],
                  out_shape=[SDS((T, D), BF16), SDS((T, D_FF), F32), SDS((T, D_FF), BF16)],
                  sem=("parallel",))(x1, g3, w1)


def _fwd_mlp_down(u, w2, x1, g4, tm):
    T = x1.shape[0]

    def body(u_ref, w_ref, x_ref, g_ref, d_ref, x2_ref):
        d = _dot(u_ref[...], w_ref[...])
        d_ref[...] = d
        x2_ref[...] = x_ref[...] + _rms(d, g_ref[...])

    return _pcall(body, name="fwd_mlp_down", grid=(T // tm,),
                  in_specs=[_rows(tm, D_FF), _full((D_FF, D)), _rows(tm, D), _full((1, D))],
                  out_specs=[_rows(tm, D), _rows(tm, D)], out_shape=[SDS((T, D), F32)] * 2,
                  sem=("parallel",))(u, w2, x1, g4)


def _ple_fwd_bwd(p, x2, tgt, wple, g5, wpg, tm):
    T = x2.shape[0]

    def body(p_ref, x2_ref, t_ref, wple_ref, g5_ref, wpg_ref, loss_ref, dx2_ref, dgt_ref, de0_ref, dg5_ref):
        @pl.when(pl.program_id(0) == 0)
        def _():
            loss_ref[...] = jnp.zeros_like(loss_ref)
            dg5_ref[...] = jnp.zeros_like(dg5_ref)

        e0 = _dot(p_ref[...].astype(BF16), wple_ref[...])
        g5 = g5_ref[...]
        r = lax.rsqrt(jnp.mean(e0 * e0, axis=-1, keepdims=True) + EPS)
        en = e0 * r
        e = en * g5
        x2 = x2_ref[...]
        s = _sigmoid(_dot(x2.astype(BF16), wpg_ref[...]))
        diff = x2 + s * e - t_ref[...]
        sq = jnp.sum(jnp.sum(diff * diff, axis=1, keepdims=True), axis=0, keepdims=True)
        loss_ref[...] += jnp.broadcast_to(sq * (0.5 / D), loss_ref.shape)
        dx3 = diff * (1.0 / D)
        de = dx3 * s
        dgt = (dx3 * e * s * (1.0 - s)).astype(BF16)
        dgt_ref[...] = dgt
        dn = de * g5
        de0_ref[...] = (r * (dn - en * jnp.mean(dn * en, axis=-1, keepdims=True))).astype(BF16)
        dg5_ref[...] += jnp.sum(de * en, axis=0, keepdims=True)
        dx2_ref[...] = dx3 + _dot_nt(dgt, wpg_ref[...])

    r = _rows(tm, D)
    return _pcall(body, name="ple_fwd_bwd", grid=(T // tm,),
                  in_specs=[_rows(tm, PLE), r, r, _full((PLE, D)), _full((1, D)), _full((D, D))],
                  out_specs=[_full((8, LANES)), r, r, r, _full((1, D))],
                  out_shape=[SDS((8, LANES), F32), SDS((T, D), F32), SDS((T, D), BF16), SDS((T, D), BF16), SDS((1, D), F32)],
                  sem=("arbitrary",))(p, x2, tgt, wple, g5, wpg)


def _bwd_mlp_down(dx2, d, g4, w2, a, tm):
    T = dx2.shape[0]

    def body(dx_ref, d_ref, g_ref, w_ref, a_ref, dd_ref, da_ref, dg_ref):
        @pl.when(pl.program_id(0) == 0)
        def _():
            dg_ref[...] = jnp.zeros_like(dg_ref)

        dd, dg = _rms_bwd(dx_ref[...], d_ref[...], g_ref[...])
        dg_ref[...] += dg
        ddb = dd.astype(BF16)
        dd_ref[...] = ddb
        du = _dot_nt(ddb, w_ref[...])
        da_ref[...] = (du * (2.0 * jnp.maximum(a_ref[...], 0.0))).astype(BF16)

    return _pcall(body, name="bwd_mlp_down", grid=(T // tm,),
                  in_specs=[_rows(tm, D), _rows(tm, D), _full((1, D)), _full((D_FF, D)), _rows(tm, D_FF)],
                  out_specs=[_rows(tm, D), _rows(tm, D_FF), _full((1, D))],
                  out_shape=[SDS((T, D), BF16), SDS((T, D_FF), BF16), SDS((1, D), F32)],
                  sem=("arbitrary",))(dx2, d, g4, w2, a)


def _bwd_mlp_up(da, w1, x1, g3, dx2, tm):
    T = dx2.shape[0]

    def body(da_ref, w_ref, x_ref, g_ref, dx2_ref, dx1_ref, dg_ref):
        @pl.when(pl.program_id(0) == 0)
        def _():
            dg_ref[...] = jnp.zeros_like(dg_ref)

        dh = _dot_nt(da_ref[...], w_ref[...])
        dx, dg = _rms_bwd(dh, x_ref[...], g_ref[...])
        dg_ref[...] += dg
        dx1_ref[...] = dx2_ref[...] + dx

    return _pcall(body, name="bwd_mlp_up", grid=(T // tm,),
                  in_specs=[_rows(tm, D_FF), _full((D, D_FF)), _rows(tm, D), _full((1, D)), _rows(tm, D)],
                  out_specs=[_rows(tm, D), _full((1, D))],
                  out_shape=[SDS((T, D), F32), SDS((1, D), F32)], sem=("arbitrary",))(da, w1, x1, g3, dx2)


def _bwd_mix(dx1, yo, g2, wo, z, au, bu, wmu, wsu, om, tm):
    T = dx1.shape[0]

    def body(dx_ref, yo_ref, g_ref, wo_ref, ga_ref, gb_ref, au_ref, bu_ref, wmu_ref, wsu_ref, om_ref,
             dyo_ref, dg_ref, dau_ref, dbu_ref, dga_ref, dgb_ref, dom_ref, dos_ref, dl_ref):
        @pl.when(pl.program_id(0) == 0)
        def _():
            dg_ref[...] = jnp.zeros_like(dg_ref)

        dyo, dg = _rms_bwd(dx_ref[...], yo_ref[...], g_ref[...])
        dg_ref[...] += dg
        dyob = dyo.astype(BF16)
        dyo_ref[...] = dyob
        dy = _dot_nt(dyob, wo_ref[...])
        sa = _sigmoid(ga_ref[...])
        sb = _sigmoid(gb_ref[...])
        dau = (dy * sa).astype(BF16)
        dbu = (dy * sb).astype(BF16)
        dau_ref[...] = dau
        dbu_ref[...] = dbu
        dga_ref[...] = (dy * au_ref[...] * sa * (1.0 - sa)).astype(BF16)
        dgb_ref[...] = (dy * bu_ref[...] * sb * (1.0 - sb)).astype(BF16)
        dom = _dot_nt(dau, wmu_ref[...])
        dom_ref[...] = dom
        dos_ref[...] = _dot_nt(dbu, wsu_ref[...])
        prod = dom * om_ref[...]
        sub = lax.broadcasted_iota(jnp.int32, (8, tm), 0)
        for pr in range(MLA_HEADS // 2):
            pt = prod[:, LANES * pr:LANES * (pr + 1)].T
            d0 = jnp.sum(pt[0:64], axis=0, keepdims=True)
            d1 = jnp.sum(pt[64:128], axis=0, keepdims=True)
            dl_ref[pr, 0] = jnp.where(sub == 0, d0, jnp.where(sub == 1, d1, 0.0))

    r = _rows(tm, D)
    w = _full((D, D))
    return _pcall(body, name="bwd_mix", grid=(T // tm,),
                  in_specs=[r, r, _full((1, D)), w, _rows(tm, D, 1), _rows(tm, D, 2), r, r, w, w, r],
                  out_specs=[r, _full((1, D)), r, r, r, r, r, r, pl.BlockSpec((MLA_HEADS // 2, 1, 8, tm), lambda i: (0, i, 0, 0))],
                  out_shape=[SDS((T, D), BF16), SDS((1, D), F32), SDS((T, D), BF16), SDS((T, D), BF16), SDS((T, D), BF16),
                             SDS((T, D), BF16), SDS((T, D), F32), SDS((T, D), F32), SDS((MLA_HEADS // 2, T // tm, 8, tm), F32)],
                  sem=("arbitrary",))(dx1, yo, g2, wo, z, z, au, bu, wmu, wsu, om)


def _mla_bwd(qm, km, kt, vm, do, lse, delta, tb):
    T = qm.shape[0]
    nb = T // tb
    cc = ATT_COLS

    def body(q_ref, k_ref, kt_ref, v_ref, do_ref, l_ref, dl_ref, dqt_ref, dk_ref, dv_ref, s_ref, dp_ref, p_ref, ds_ref, dom_ref):
        j = pl.program_id(1)

        @pl.when(j == 0)
        def _():
            dqt_ref[...] = jnp.zeros_like(dqt_ref)

        dk_ref[...] = jnp.zeros_like(dk_ref)
        dv_ref[...] = jnp.zeros_like(dv_ref)
        lo = lax.broadcasted_iota(jnp.int32, (tb, LANES), 1) < 64
        key = lax.broadcasted_iota(jnp.int32, (tb, cc), 0)
        qry = lax.broadcasted_iota(jnp.int32, (tb, cc), 1)

        def scores(i, slot):
            rows_i = pl.ds(pl.multiple_of(i * tb, tb), tb)
            d_o = do_ref[rows_i, :]
            v = v_ref[...]
            for hh in range(2):
                sl = slice(LANES * hh, LANES * (hh + 1))
                hm = lo if hh == 0 else jnp.logical_not(lo)
                dom_ref[slot, hh] = jnp.where(hm, d_o, 0.0).astype(BF16)
                s_ref[slot, hh] = _dot_nt(k_ref[:, sl], q_ref[rows_i, sl])
                dp_ref[slot, hh] = _dot_nt(v, dom_ref[slot, hh])

        def grads(i, slot, diagonal):
            rows_i = pl.ds(pl.multiple_of(i * tb, tb), tb)
            lse_i = l_ref[0, i]
            delta_i = dl_ref[0, i]
            for hh in range(2):
                for c in range(tb // cc):
                    cols = slice(cc * c, cc * (c + 1))
                    p = jnp.exp2(s_ref[slot, hh, :, cols] * MLA_LOG2_SCALE - lse_i[hh:hh + 1, cols])
                    if diagonal:
                        p = jnp.where(key <= qry + cc * c, p, 0.0)
                    p_ref[hh, :, cols] = p.astype(BF16)
                    ds_ref[hh, :, cols] = (p * (dp_ref[slot, hh, :, cols] - delta_i[hh:hh + 1, cols]) * MLA_SCALE).astype(BF16)
            for hh in range(2):
                sl = slice(LANES * hh, LANES * (hh + 1))
                dv_ref[...] += _dot(p_ref[hh], dom_ref[slot, hh])
                dk_ref[:, sl] += _dot(ds_ref[hh], q_ref[rows_i, sl])
                dqt_ref[i, sl, :] += _dot(kt_ref[0, sl, :], ds_ref[hh])

        n_off = nb - 1 - j

        def step(u, carry):
            i0 = j + 1 + 2 * u
            scores(i0 + 1, 1)
            grads(i0, 0, False)
            scores(jnp.where(i0 + 2 < nb, i0 + 2, j), 0)
            grads(i0 + 1, 1, False)
            return carry

        scores(jnp.where(n_off > 0, j + 1, j), 0)
        lax.fori_loop(0, n_off // 2, step, 0)

        @pl.when(n_off % 2 == 1)
        def _():
            scores(j, 1)
            grads(nb - 1, 0, False)
            grads(j, 1, True)

        @pl.when(n_off % 2 == 0)
        def _():
            grads(j, 0, True)

    pair = lambda w: pl.BlockSpec((T, w), lambda p, j: (0, p))
    blk = lambda w: pl.BlockSpec((tb, w), lambda p, j: (j, p))
    stat = pl.BlockSpec((1, nb, 8, tb), lambda p, j: (p, 0, 0, 0))
    return _pcall(body, name="mla_bwd", grid=(MLA_HEADS // 2, nb),
                  in_specs=[pair(256), blk(256), pl.BlockSpec((1, 256, tb), lambda p, j: (j, p, 0)), blk(LANES), pair(LANES),
                            stat, stat],
                  out_specs=[pl.BlockSpec((nb, 256, tb), lambda p, j: (0, p, 0)), blk(256), blk(LANES)],
                  out_shape=[SDS((nb, 2048, tb), F32), SDS((T, 2048), F32), SDS((T, D), F32)],
                  scratch=[pltpu.VMEM((2, 2, tb, tb), F32), pltpu.VMEM((2, 2, tb, tb), F32), pltpu.VMEM((2, tb, tb), BF16),
                           pltpu.VMEM((2, tb, tb), BF16), pltpu.VMEM((2, 2, tb, LANES), BF16)],
                  sem=("parallel", "arbitrary"))(qm, km, kt, vm, do, lse, delta)


def _swa_bwd(sinks, qs, ks, vs, do, o, lse):
    T = qs.shape[0]
    nb, cur, prev = _swa_specs(T)

    def body(sink_ref, q_ref, kc_ref, kp_ref, vc_ref, vp_ref, do_ref, o_ref, l_ref,
             dq_ref, dkc_ref, dkp_ref, dvc_ref, dvp_ref, dsink_ref):
        n = pl.program_id(0)

        @pl.when(n == 0)
        def _():
            dsink_ref[...] = jnp.zeros_like(dsink_ref)

        mask = _swa_mask(n)
        lo = lax.broadcasted_iota(jnp.int32, (WINDOW, LANES), 1) < 64
        lane8 = lax.broadcasted_iota(jnp.int32, (8, LANES), 1)
        dsink = jnp.zeros((8, LANES), F32)
        for g in range(2):
            gs = slice(LANES * g, LANES * (g + 1))
            kb = jnp.concatenate([kp_ref[:, gs], kc_ref[:, gs]], axis=0)
            vb = jnp.concatenate([vp_ref[:, gs], vc_ref[:, gs]], axis=0)
            dkb = jnp.zeros((2 * WINDOW, LANES), F32)
            dvb = jnp.zeros((2 * WINDOW, LANES), F32)
            for jj in range(4):
                j = 4 * g + jj
                sl = slice(LANES * j, LANES * (j + 1))
                qp = q_ref[:, sl]
                d_o = do_ref[:, sl]
                prod = d_o * o_ref[:, sl]
                lse_b = l_ref[:, sl]
                dqs = []
                for hf in range(2):
                    hm = lo if hf == 0 else jnp.logical_not(lo)
                    qh = jnp.where(hm, qp, jnp.zeros_like(qp))
                    s = jnp.where(mask, _dot_nt(qh, kb) * SWA_SCALE, NEG)
                    lse_h = jnp.max(jnp.where(hm, lse_b, -jnp.inf), axis=1, keepdims=True)
                    p = jnp.exp(s - lse_h)
                    dom = jnp.where(hm, d_o, 0.0).astype(BF16)
                    dp = _dot_nt(dom, vb)
                    delta = jnp.sum(jnp.where(hm, prod, 0.0), axis=1, keepdims=True)
                    ds = (p * (dp - delta) * SWA_SCALE).astype(BF16)
                    p_sink = jnp.exp(sink_ref[2 * j + hf] - lse_h)
                    d_sink = -jnp.sum(p_sink * delta, axis=0, keepdims=True)
                    dsink = dsink + jnp.where(lane8 == 2 * j + hf, d_sink, 0.0)
                    dvb = dvb + _dot_tn(p.astype(BF16), dom)
                    dkb = dkb + _dot_tn(ds, qh)
                    dqs.append(_dot(ds, kb))
                dq_ref[:, sl] = jnp.where(lo, dqs[0], dqs[1])
            dkp_ref[:, gs] = dkb[:WINDOW]
            dkc_ref[:, gs] = dkb[WINDOW:]
            dvp_ref[:, gs] = dvb[:WINDOW]
            dvc_ref[:, gs] = dvb[WINDOW:]
        dsink_ref[...] += dsink

    return _pcall(body, name="swa_bwd", grid=(nb,),
                  in_specs=[pl.BlockSpec(memory_space=pltpu.SMEM), cur(D), cur(256), prev(256), cur(256), prev(256),
                            cur(D), cur(D), cur(D)],
                  out_specs=[cur(D), cur(256), cur(256), cur(256), cur(256), _full((8, LANES))],
                  out_shape=[SDS((T, D), F32), SDS((T, 256), F32), SDS((T, 256), F32), SDS((T, 256), F32), SDS((T, 256), F32),
                             SDS((8, LANES), F32)],
                  sem=("arbitrary",))(sinks, qs, ks, ks, vs, vs, do, o, lse)


def _bwd_qkv(dqm, dkm, dvm, dqs, dkc, dkp, dvc, dvp, z, gq, gkv, wqb, wkn, wv, tab_m, tab_s):
    T = z.shape[0]
    tm = WINDOW
    nb = T // tm
    per = dqm.shape[2] // tm

    def body(dqm_ref, dkm_ref, dvm_ref, dqs_ref, dkc_ref, dkp_ref, dvc_ref, dvp_ref, qa_ref, kva_ref, gq_ref, gkv_ref,
             wqb_ref, wkn_ref, wv_ref, cm_ref, am_ref, bm_ref, cs_ref, as_ref, bs_ref,
             dq_out, dkn_out, dv_out, dsq_ref, drest_ref, dgq_ref, dgkv_ref):
        i = pl.program_id(0)

        @pl.when(i == 0)
        def _():
            dgq_ref[...] = jnp.zeros_like(dgq_ref)
            dgkv_ref[...] = jnp.zeros_like(dgkv_ref)

        cm, am, bm = cm_ref[...], -am_ref[...], -bm_ref[...]
        cs, as_, bs = cs_ref[...], -as_ref[...], -bs_ref[...]
        lane = lax.broadcasted_iota(jnp.int32, (tm, LANES), 1)
        nope = lane < MLA_NOPE
        roped = jnp.logical_and(lane >= MLA_NOPE, lane < MLA_NOPE + MLA_ROPE)
        dkr = jnp.zeros((tm, LANES), F32)
        dqn = jnp.zeros((tm, Q_LORA), F32)
        dkvn = jnp.zeros((tm, KV_LORA), F32)
        for h in range(MLA_HEADS):
            sl = slice(LANES * h, LANES * (h + 1))
            dq_h = _rope(dqm_ref[0, sl, :].T, cm, am, bm, MLA_ROPE // 2).astype(BF16)
            dq_out[:, sl] = dq_h
            dqn = dqn + _dot_nt(dq_h, wqb_ref[:, sl])
            dk_h = dkm_ref[:, sl]
            dkn_h = jnp.where(nope, dk_h, 0.0).astype(BF16)
            dkn_out[:, sl] = dkn_h
            dkvn = dkvn + _dot_nt(dkn_h, wkn_ref[:, sl])
            dkr = dkr + jnp.where(roped, dk_h, 0.0)
        dvb = dvm_ref[...].astype(BF16)
        dv_out[...] = dvb
        dkvn = dkvn + _dot_nt(dvb, wv_ref[...])
        dqa, dgq = _rms_bwd(dqn, qa_ref[...], gq_ref[...])
        dkva, dgkv = _rms_bwd(dkvn, kva_ref[...], gkv_ref[...])
        dgq_ref[...] += dgq
        dgkv_ref[...] += dgkv
        for j in range(D // LANES):
            sl = slice(LANES * j, LANES * (j + 1))
            dsq_ref[:, sl] = _rope(dqs_ref[:, sl], cs, as_, bs, SWA_HD // 2).astype(BF16)
        keep = (i < nb - 1).astype(F32)
        drest_ref[:, 0:256] = dqa.astype(BF16)
        for j in range(2):
            sl = slice(LANES * j, LANES * (j + 1))
            dk = dkc_ref[:, sl] + keep * dkp_ref[:, sl]
            drest_ref[:, 256 + LANES * j:256 + LANES * (j + 1)] = _rope(dk, cs, as_, bs, SWA_HD // 2).astype(BF16)
        drest_ref[:, 512:768] = (dvc_ref[...] + keep * dvp_ref[...]).astype(BF16)
        drest_ref[:, 768:896] = dkva.astype(BF16)
        drest_ref[:, 896:1024] = _rope(dkr, cm, am, bm, MLA_ROPE // 2).astype(BF16)

    nxt = pl.BlockSpec((tm, 256), lambda i: (jnp.minimum(i + 1, nb - 1), 0))
    tab = [_rows(tm, LANES)] * 6
    return _pcall(body, name="bwd_qkv", grid=(nb,),
                  in_specs=[pl.BlockSpec((1, 2048, tm), lambda i: (i // per, 0, i % per)),
                            _rows(tm, 2048), _rows(tm, 1024), _rows(tm, 1024), _rows(tm, 256), nxt,
                            _rows(tm, 256), nxt, _rows(tm, 256, 12), _rows(tm, 128, 30), _full((1, Q_LORA)), _full((1, KV_LORA)),
                            _full((Q_LORA, 2048)), _full((KV_LORA, 2048)), _full((KV_LORA, 1024))] + tab,
                  out_specs=[_rows(tm, 2048), _rows(tm, 2048), _rows(tm, 1024), _rows(tm, 1024), _rows(tm, 1024),
                             _full((1, Q_LORA)), _full((1, KV_LORA))],
                  out_shape=[SDS((T, 2048), BF16), SDS((T, 2048), BF16), SDS((T, 1024), BF16), SDS((T, 1024), BF16),
                             SDS((T, 1024), BF16), SDS((1, Q_LORA), F32), SDS((1, KV_LORA), F32)],
                  sem=("arbitrary",))(dqm, dkm, dvm, dqs, dkc, dkp, dvc, dvp, z, z, gq, gkv, wqb, wkn, wv, *tab_m, *tab_s)


def _bwd_in(dsq, dga, dgb, drest, w_in_p, x, g1, dx1, tm):
    T = x.shape[0]

    def body(a_ref, b_ref, c_ref, d_ref, w_ref, x_ref, g_ref, dx1_ref, dx_ref, dg_ref):
        @pl.when(pl.program_id(0) == 0)
        def _():
            dg_ref[...] = jnp.zeros_like(dg_ref)

        dh = (_dot_nt(a_ref[...], w_ref[:, 0:1024]) + _dot_nt(b_ref[...], w_ref[:, 1024:2048])
              + _dot_nt(c_ref[...], w_ref[:, 2048:3072]) + _dot_nt(d_ref[...], w_ref[:, 3072:4096]))
        dx, dg = _rms_bwd(dh, x_ref[...], g_ref[...])
        dg_ref[...] += dg
        dx_ref[...] = dx1_ref[...] + dx

    r = _rows(tm, D)
    return _pcall(body, name="bwd_in", grid=(T // tm,),
                  in_specs=[r, r, r, r, _full((D, NZ)), r, _full((1, D)), r],
                  out_specs=[r, _full((1, D))], out_shape=[SDS((T, D), F32), SDS((1, D), F32)],
                  sem=("arbitrary",))(dsq, dga, dgb, drest, w_in_p, x, g1, dx1)


def _wgrad(a, g, name):
    T, K = a.shape
    N = g.shape[1]
    tk, tn, tt = min(K, 512), min(N, 1024), min(T, 512)
    assert K % tk == 0 and N % tn == 0 and T % tt == 0, (a.shape, g.shape)

    def body(a_ref, g_ref, o_ref):
        @pl.when(pl.program_id(2) == 0)
        def _():
            o_ref[...] = jnp.zeros_like(o_ref)

        o_ref[...] += _dot_tn(a_ref[...].astype(BF16), g_ref[...].astype(BF16))

    return _pcall(body, name=name, grid=(K // tk, N // tn, T // tt),
                  in_specs=[pl.BlockSpec((tt, tk), lambda k, n, t: (t, k)), pl.BlockSpec((tt, tn), lambda k, n, t: (t, n))],
                  out_specs=pl.BlockSpec((tk, tn), lambda k, n, t: (k, n)), out_shape=SDS((K, N), F32),
                  sem=("parallel", "parallel", "arbitrary"))(a, g)


def _adamw(w, g, m, v, name):
    R, C = w.shape
    tr = min(R, 256)

    def body(w_ref, g_ref, m_ref, v_ref, d_ref, m2_ref, v2_ref):
        g_ = g_ref[...]
        m2 = ADAM_B1 * m_ref[...] + (1.0 - ADAM_B1) * g_
        v2 = ADAM_B2 * v_ref[...] + (1.0 - ADAM_B2) * jnp.square(g_)
        m_hat = m2 / (1.0 - ADAM_B1 ** ADAM_STEP)
        v_hat = v2 / (1.0 - ADAM_B2 ** ADAM_STEP)
        d_ref[...] = -ADAM_LR * (m_hat / (jnp.sqrt(v_hat) + ADAM_EPS) + ADAM_WD * w_ref[...])
        m2_ref[...] = m2
        v2_ref[...] = v2

    r = _rows(tr, C)
    return _pcall(body, name=name, grid=(R // tr,), in_specs=[r] * 4, out_specs=[r] * 3,
                  out_shape=[SDS((R, C), F32)] * 3, sem=("parallel",))(w, g, m, v)


def _adamw_small(w, parts, m, v):
    def body(w_ref, p_ref, m_ref, v_ref, g_ref, d_ref, m2_ref, v2_ref):
        g_ = p_ref[0]
        for k in range(1, N_DEV):
            g_ = g_ + p_ref[k]
        g_ref[...] = g_
        m2 = ADAM_B1 * m_ref[...] + (1.0 - ADAM_B1) * g_
        v2 = ADAM_B2 * v_ref[...] + (1.0 - ADAM_B2) * jnp.square(g_)
        m_hat = m2 / (1.0 - ADAM_B1 ** ADAM_STEP)
        v_hat = v2 / (1.0 - ADAM_B2 ** ADAM_STEP)
        d_ref[...] = -ADAM_LR * (m_hat / (jnp.sqrt(v_hat) + ADAM_EPS) + ADAM_WD * w_ref[...])
        m2_ref[...] = m2
        v2_ref[...] = v2

    s = _full((8, D))
    return _pcall(body, name="adamw_small", grid=(1,), in_specs=[s, _full((N_DEV, 8, D)), s, s], out_specs=[s] * 4,
                  out_shape=[SDS((8, D), F32)] * 4, sem=("arbitrary",))(w, parts, m, v)


ANY = pl.BlockSpec(memory_space=pl.ANY)


def _place():
    x, y, c = lax.axis_index("x"), lax.axis_index("y"), lax.axis_index("c")
    chips = [(1 - x, y), (x, 1 - y), (1 - x, 1 - y)]
    return x, y, c, chips


def _all_gather(wpk):
    def body(in_ref, out_ref, send_sems, recv_sems, local_sem):
        x, y, c, chips = _place()
        half = pl.ds(pl.multiple_of(c * HALF, 16), HALF)
        other = pl.ds(pl.multiple_of((1 - c) * HALF, 16), HALF)

        def copy(k, src, dst, to):
            return pltpu.make_async_remote_copy(src_ref=src, dst_ref=dst, send_sem=send_sems.at[k], recv_sem=recv_sems.at[k],
                                                device_id=to, device_id_type=MESH)

        mine = pltpu.make_async_copy(in_ref, out_ref.at[2 * x + y], local_sem)
        mine.start()
        first = [copy(k, in_ref.at[half], out_ref.at[2 * x + y, half], (cx, cy, c)) for k, (cx, cy) in enumerate(chips)]
        for cp in first:
            cp.start()
        passed = []
        for k, (cx, cy) in enumerate(chips):
            slot = out_ref.at[2 * cx + cy, half]
            copy(k, slot, slot, (x, y, c)).wait_recv()
            fwd = copy(3 + k, slot, slot, (x, y, 1 - c))
            fwd.start()
            passed.append(fwd)
        for k, (cx, cy) in enumerate(chips):
            slot = out_ref.at[2 * cx + cy, other]
            copy(3 + k, slot, slot, (x, y, c)).wait_recv()
        for cp in first + passed:
            cp.wait_send()
        mine.wait()

    return _pcall(body, name="all_gather_weights", in_specs=[ANY], out_specs=ANY,
                  out_shape=SDS((N_CHIPS, PACK_PAD, D), BF16),
                  scratch=[pltpu.SemaphoreType.DMA((6,)), pltpu.SemaphoreType.DMA((6,)), pltpu.SemaphoreType.DMA])(wpk)


def _rs_sibling(gpk):
    def body(in_ref, out_ref, send_sem, recv_sem):
        x, y, c, _ = _place()
        theirs = pl.ds(pl.multiple_of((1 - c) * HALF, 8), HALF)
        cp = pltpu.make_async_remote_copy(src_ref=in_ref.at[:, theirs], dst_ref=out_ref, send_sem=send_sem, recv_sem=recv_sem,
                                          device_id=(x, y, 1 - c), device_id_type=MESH)
        cp.start()
        cp.wait()

    return _pcall(body, name="rs_sibling", in_specs=[ANY], out_specs=ANY, out_shape=SDS((N_CHIPS, HALF, D), F32),
                  scratch=[pltpu.SemaphoreType.DMA, pltpu.SemaphoreType.DMA])(gpk)


def _rs_add_sibling(cidx, gpk, got):
    th = HALF // 7
    nh = HALF // th

    def body(c_ref, a_ref, b_ref, o_ref):
        o_ref[...] = a_ref[...] + b_ref[...]

    gs = pltpu.PrefetchScalarGridSpec(
        num_scalar_prefetch=1, grid=(N_CHIPS, nh),
        in_specs=[pl.BlockSpec((1, th, D), lambda j, i, c: (j, c[0] * nh + i, 0)), pl.BlockSpec((1, th, D), lambda j, i, c: (j, i, 0))],
        out_specs=pl.BlockSpec((1, th, D), lambda j, i, c: (j, i, 0)))
    return pl.pallas_call(body, name="rs_add_sibling", grid_spec=gs, out_shape=SDS((N_CHIPS, HALF, D), F32),
                          compiler_params=pltpu.CompilerParams(dimension_semantics=("parallel", "parallel"),
                                                               vmem_limit_bytes=48 << 20))(cidx, gpk, got)


def _rs_chips(part, small):
    def body(p_ref, s_ref, o_ref, so_ref, send_sems, recv_sems, ssend_sems, srecv_sems, local_sems):
        x, y, c, chips = _place()
        me = 2 * x + y
        mine = pltpu.make_async_copy(p_ref.at[me], o_ref.at[me], local_sems.at[0])
        mine_s = pltpu.make_async_copy(s_ref, so_ref.at[4 * x + 2 * y + c], local_sems.at[1])
        mine.start()
        mine_s.start()
        sends = []
        for k, (cx, cy) in enumerate(chips):
            sends.append(pltpu.make_async_remote_copy(src_ref=p_ref.at[2 * cx + cy], dst_ref=o_ref.at[me], send_sem=send_sems.at[k],
                                                      recv_sem=recv_sems.at[k], device_id=(cx, cy, c), device_id_type=MESH))
        peers = [(x, y, 1 - c)] + [(cx, cy, c) for cx, cy in chips] + [(cx, cy, 1 - c) for cx, cy in chips]
        for k, to in enumerate(peers):
            sends.append(pltpu.make_async_remote_copy(src_ref=s_ref, dst_ref=so_ref.at[4 * x + 2 * y + c], send_sem=ssend_sems.at[k],
                                                      recv_sem=srecv_sems.at[k], device_id=to, device_id_type=MESH))
        for cp in sends:
            cp.start()
        for k, (cx, cy) in enumerate(chips):
            slot = o_ref.at[2 * cx + cy]
            pltpu.make_async_remote_copy(src_ref=slot, dst_ref=slot, send_sem=send_sems.at[k], recv_sem=recv_sems.at[k],
                                         device_id=(x, y, c), device_id_type=MESH).wait_recv()
        for k, (px, py, pc) in enumerate(peers):
            slot = so_ref.at[4 * px + 2 * py + pc]
            pltpu.make_async_remote_copy(src_ref=slot, dst_ref=slot, send_sem=ssend_sems.at[k], recv_sem=srecv_sems.at[k],
                                         device_id=(x, y, c), device_id_type=MESH).wait_recv()
        for cp in sends:
            cp.wait_send()
        mine.wait()
        mine_s.wait()

    return _pcall(body, name="rs_chips", in_specs=[ANY, ANY], out_specs=[ANY, ANY],
                  out_shape=[SDS((N_CHIPS, HALF, D), part.dtype), SDS((N_DEV, 8, D), F32)],
                  scratch=[pltpu.SemaphoreType.DMA((3,)), pltpu.SemaphoreType.DMA((3,)), pltpu.SemaphoreType.DMA((7,)),
                           pltpu.SemaphoreType.DMA((7,)), pltpu.SemaphoreType.DMA((2,))])(part, small)


def _rs_add_chips(parts):
    th = HALF // 7

    def body(p_ref, o_ref):
        o_ref[...] = ((p_ref[0].astype(F32) + p_ref[1].astype(F32)) + p_ref[2].astype(F32)) + p_ref[3].astype(F32)

    return _pcall(body, name="rs_add_chips", grid=(HALF // th,),
                  in_specs=[pl.BlockSpec((N_CHIPS, th, D), lambda i: (0, i, 0))], out_specs=_rows(th, D),
                  out_shape=SDS((HALF, D), F32), sem=("parallel",))(parts)


def _rs_join(total):
    def body(in_ref, out_ref, send_sem, recv_sem, local_sem):
        x, y, c, _ = _place()
        mine = pltpu.make_async_copy(in_ref, out_ref.at[c], local_sem)
        mine.start()
        cp = pltpu.make_async_remote_copy(src_ref=in_ref, dst_ref=out_ref.at[c], send_sem=send_sem, recv_sem=recv_sem,
                                          device_id=(x, y, 1 - c), device_id_type=MESH)
        cp.start()
        slot = out_ref.at[1 - c]
        pltpu.make_async_remote_copy(src_ref=slot, dst_ref=slot, send_sem=send_sem, recv_sem=recv_sem,
                                     device_id=(x, y, c), device_id_type=MESH).wait_recv()
        cp.wait_send()
        mine.wait()

    return _pcall(body, name="rs_join", in_specs=[ANY], out_specs=ANY, out_shape=SDS((2, HALF, D), F32),
                  scratch=[pltpu.SemaphoreType.DMA, pltpu.SemaphoreType.DMA, pltpu.SemaphoreType.DMA])(total)


def _pack_shards(blocks, dtype):
    rows = [blocks[n].astype(dtype).reshape(-1, D) for n, _, _ in BIG]
    rows.append(jnp.zeros((PACK_PAD - PACK_ROWS, D), dtype))
    return jnp.concatenate(rows, axis=0)


def _unpack_shards(pk):
    out, off = {}, 0
    for n, r, c in BIG:
        k = r * c // D
        out[n] = pk[off:off + k].reshape(r, c)
        off += k
    return out


def _full_weights(gathered):
    per_chip = [_unpack_shards(gathered[j]) for j in range(N_CHIPS)]
    return {n: jnp.concatenate([pc[n] for pc in per_chip], axis=1 if n in COL_SHARDED else 0) for n, _, _ in BIG}


def _split_full_grads(grads, dtype):
    chunks = []
    for j in range(N_CHIPS):
        blocks = {}
        for n, r, c in BIG:
            g = grads[n]
            blocks[n] = g[:, j * c:(j + 1) * c] if n in COL_SHARDED else g[j * r:(j + 1) * r]
        chunks.append(_pack_shards(blocks, dtype))
    return jnp.stack(chunks)


def _w_in_internal(w):
    z = lambda n: jnp.zeros((w.shape[0], n), w.dtype)
    sk0, sk1 = w[:, 1440:1504], w[:, 1504:1568]
    sv0, sv1 = w[:, 1568:1632], w[:, 1632:1696]
    return jnp.concatenate([w[:, 416:1440], w[:, 1696:3744], w[:, 0:256], sk0, sk0, sk1, sk1, sv0, sv0, sv1, sv1,
                            w[:, 256:384], z(64), w[:, 384:416], z(32)], axis=1)


def _w_in_external_grad(g):
    sk = [g[:, 3328 + 128 * j:3392 + 128 * j] + g[:, 3392 + 128 * j:3456 + 128 * j] for j in range(2)]
    sv = [g[:, 3584 + 128 * j:3648 + 128 * j] + g[:, 3648 + 128 * j:3712 + 128 * j] for j in range(2)]
    return jnp.concatenate([g[:, 3072:3328], g[:, 3840:3968], g[:, 4032:4064], g[:, 0:1024], *sk, *sv, g[:, 1024:3072]], axis=1)


def _local_step(x, p, tgt, w, small):
    T = x.shape[0]
    tm = 256
    tb = 256
    w_in_p = _w_in_internal(w["w_in"])
    wqb = jnp.pad(w["w_q_b"].reshape(Q_LORA, MLA_HEADS, 96), ((0, 0), (0, 0), (0, 32))).reshape(Q_LORA, 2048)
    wkv = w["w_kv_b"].reshape(KV_LORA, MLA_HEADS, 128)
    wkn = jnp.pad(wkv[:, :, :64], ((0, 0), (0, 0), (0, 64))).reshape(KV_LORA, 2048)
    wv = wkv[:, :, 64:].reshape(KV_LORA, 1024)
    tab_m = _rope_tables(T, "mla")
    tab_s = _rope_tables(T, "swa")
    g1, gq, gkv, sinks = small["g_mix_pre"], small["g_q_a"], small["g_kv_a"], small["sinks"]
    g2, g3, g4, g5 = small["g_mix_post"], small["g_mlp_pre"], small["g_mlp_post"], small["g_ple"]
    sink_vec = sinks.reshape(SWA_HEADS)

    z, h1 = _fwd_in(x, g1, w_in_p, tm)
    qn, kvn, qm, km, vm, kt, vt, qs, ks, vs = _fwd_qkv(z, gq, gkv, wqb, wkn, wv, tab_m, tab_s, tb)
    om, lse_m = _mla_fwd(qm, km, vt, tb)
    os_, lse_s = _swa_fwd(sink_vec, qs, ks, vs)
    y, yo, au, bu, x1 = _fwd_mix(om, os_, z, x, w["w_mla_up"], w["w_swa_up"], w["w_out"], g2, tm)
    h2, a, u = _fwd_mlp_up(x1, g3, w["w_mlp_up"], tm)
    d, x2 = _fwd_mlp_down(u, w["w_mlp_down"], x1, g4, tm)
    loss, dx2, dgt, de0, dg5 = _ple_fwd_bwd(p, x2, tgt, w["w_ple"], g5, w["w_ple_gate"], tm)

    dd, da, dg4 = _bwd_mlp_down(dx2, d, g4, w["w_mlp_down"], a, tm)
    dx1, dg3 = _bwd_mlp_up(da, w["w_mlp_up"], x1, g3, dx2, tm)
    dyo, dg2, dau, dbu, dga, dgb, dom, dos, delta_m = _bwd_mix(dx1, yo, g2, w["w_out"], z, au, bu, w["w_mla_up"],
                                                                w["w_swa_up"], om, tb)
    dqm, dkm, dvm = _mla_bwd(qm, km, kt, vm, dom, lse_m, delta_m, tb)
    dqs, dkc, dkp, dvc, dvp, dsink = _swa_bwd(sink_vec, qs, ks, vs, dos, os_, lse_s)
    dqb, dknb, dvb, dsq, drest, dgq, dgkv = _bwd_qkv(dqm, dkm, dvm, dqs, dkc, dkp, dvc, dvp, z, gq, gkv, wqb, wkn, wv,
                                                      tab_m, tab_s)
    gx, dg1 = _bwd_in(dsq, dga, dgb, drest, w_in_p, x, g1, dx1, tm)

    g_in_p = jnp.concatenate([_wgrad(h1, dsq, "wgrad_in_sq"), _wgrad(h1, dga, "wgrad_in_ga"), _wgrad(h1, dgb, "wgrad_in_gb"),
                              _wgrad(h1, drest, "wgrad_in_rest")], axis=1)
    g_qb_p = _wgrad(qn, dqb, "wgrad_q_b")
    g_kn_p = _wgrad(kvn, dknb, "wgrad_kv_b_nope")
    g_v_p = _wgrad(kvn, dvb, "wgrad_kv_b_v")
    grads = {
        "w_in": _w_in_external_grad(g_in_p),
        "w_q_b": g_qb_p.reshape(Q_LORA, MLA_HEADS, 128)[:, :, :96].reshape(Q_LORA, 1536),
        "w_kv_b": jnp.concatenate([g_kn_p.reshape(KV_LORA, MLA_HEADS, 128)[:, :, :64], g_v_p.reshape(KV_LORA, MLA_HEADS, 64)],
                                  axis=2).reshape(KV_LORA, 2048),
        "w_mla_up": _wgrad(om, dau, "wgrad_mla_up"),
        "w_swa_up": _wgrad(os_, dbu, "wgrad_swa_up"),
        "w_out": _wgrad(y, dyo, "wgrad_out"),
        "w_mlp_up": _wgrad(h2, da, "wgrad_mlp_up"),
        "w_mlp_down": _wgrad(u, dd, "wgrad_mlp_down"),
        "w_ple": _wgrad(p, de0, "wgrad_ple"),
        "w_ple_gate": _wgrad(x2, dgt, "wgrad_ple_gate"),
    }
    small_grads = {"g_mix_pre": dg1, "g_q_a": dgq, "g_kv_a": dgkv, "sinks": dsink[0:1, 0:SWA_HEADS], "g_mix_post": dg2,
                   "g_mlp_pre": dg3, "g_mlp_post": dg4, "g_ple": dg5}
    return loss, gx, grads, small_grads


def _pack_small(vals, fill):
    wide = [vals[n] for n, k in SMALL if k == D]
    narrow = [vals[n] for n, k in SMALL if k != D]
    used = sum(k for _, k in SMALL if k != D)
    last = jnp.concatenate(narrow + [jnp.full((1, D - used), fill, F32)], axis=1)
    return jnp.concatenate(wide + [last, jnp.full((2, D), fill, F32)], axis=0)


def _unpack_small(pk):
    out, row, off = {}, 0, 0
    for n, k in SMALL:
        if k == D:
            out[n] = pk[row:row + 1]
            row += 1
    for n, k in SMALL:
        if k != D:
            out[n] = pk[5:6, off:off + k]
            off += k
    return out


def kernel(x, p, g_mix_pre, w_in, g_q_a, w_q_b, g_kv_a, w_kv_b, sinks, w_mla_up, w_swa_up, w_out, g_mix_post, g_mlp_pre, w_mlp_up, w_mlp_down, g_mlp_post, w_ple, g_ple, w_ple_gate, loss_target, m_g_mix_pre, m_w_in, m_g_q_a, m_w_q_b, m_g_kv_a, m_w_kv_b, m_sinks, m_w_mla_up, m_w_swa_up, m_w_out, m_g_mix_post, m_g_mlp_pre, m_w_mlp_up, m_w_mlp_down, m_g_mlp_post, m_w_ple, m_g_ple, m_w_ple_gate, v_g_mix_pre, v_w_in, v_g_q_a, v_w_q_b, v_g_kv_a, v_w_kv_b, v_sinks, v_w_mla_up, v_w_swa_up, v_w_out, v_g_mix_post, v_g_mlp_pre, v_w_mlp_up, v_w_mlp_down, v_g_mlp_post, v_w_ple, v_g_ple, v_w_ple_gate):
    given = dict(locals())
    big_w = {n: given[n][0] for n, _, _ in BIG}
    big_m = {n: given["m_" + n][0] for n, _, _ in BIG}
    big_v = {n: given["v_" + n][0] for n, _, _ in BIG}
    small_w = {n: given[n] for n, _ in SMALL}
    small_m = {n: given["m_" + n] for n, _ in SMALL}
    small_v = {n: given["v_" + n] for n, _ in SMALL}

    gathered = _all_gather(_pack_shards(big_w, BF16))
    weights = _full_weights(gathered)
    loss_blk, gx, grads, small_grads = _local_step(x[0], p[0, 0], loss_target[0], weights, small_w)

    cidx = lax.axis_index("c").astype(jnp.int32).reshape(1)
    gpk = _split_full_grads(grads, F32)
    got = _rs_sibling(gpk)
    part = _rs_add_sibling(cidx, gpk, got)
    parts, small_parts = _rs_chips(part, _pack_small(small_grads, 0.0))
    joined = _rs_join(_rs_add_chips(parts))
    big_g = _unpack_shards(joined.reshape(PACK_PAD, D))

    loss = lax.psum(loss_blk[0, 0], ("x", "y", "c"))
    g_small_pk, d_small_pk, m_small_pk, v_small_pk = _adamw_small(
        _pack_small(small_w, 0.0), small_parts, _pack_small(small_m, 0.0), _pack_small(small_v, 1.0))
    g_small, d_small = _unpack_small(g_small_pk), _unpack_small(d_small_pk)
    m_small, v_small = _unpack_small(m_small_pk), _unpack_small(v_small_pk)

    out_g, out_d, out_m, out_v = dict(g_small), dict(d_small), dict(m_small), dict(v_small)
    for n, _, _ in BIG:
        d_, m_, v_ = _adamw(big_w[n], big_g[n], big_m[n], big_v[n], "adamw_" + n)
        out_g[n], out_d[n], out_m[n], out_v[n] = big_g[n][None], d_[None], m_[None], v_[None]
    order = ["g_mix_pre", "w_in", "g_q_a", "w_q_b", "g_kv_a", "w_kv_b", "sinks", "w_mla_up", "w_swa_up", "w_out", "g_mix_post",
             "g_mlp_pre", "w_mlp_up", "w_mlp_down", "g_mlp_post", "w_ple", "g_ple", "w_ple_gate"]
    return (loss, gx[None], *[out_g[n] for n in order], *[out_d[n] for n in order], *[out_m[n] for n in order],
            *[out_v[n] for n in order])
```

```python
import math

import jax
import jax.numpy as jnp
from jax import lax
from jax.experimental import pallas as pl
from jax.experimental.pallas import tpu as pltpu

F32 = jnp.float32
BF16 = jnp.bfloat16
SDS = jax.ShapeDtypeStruct

D = 1024
D_FF = 4096
PLE = 256
Q_LORA = 256
KV_LORA = 128
MLA_HEADS = 16
MLA_NOPE = 64
MLA_ROPE = 32
SWA_HEADS = 16
SWA_HD = 64
WINDOW = 128
ROPE_THETA = 10000.0
EPS = 1e-6
NEG = -1e30
NZ = 4096
MLA_SCALE = (MLA_NOPE + MLA_ROPE) ** -0.5
LOG2_E = math.log2(math.e)
MLA_LOG2_SCALE = MLA_SCALE * LOG2_E
SWA_SCALE = SWA_HD ** -0.5

ADAM_LR = 0.001
ADAM_B1 = 0.9
ADAM_B2 = 0.999
ADAM_EPS = 1e-08
ADAM_WD = 0.01
ADAM_STEP = 10

LANES = 128
ATT_COLS = 128
N_CHIPS = 4
N_DEV = 8
MESH = pl.DeviceIdType.MESH

NT = (((1,), (1,)), ((), ()))
TN = (((0,), (0,)), ((), ()))

BIG = (("w_in", 1024, 936), ("w_q_b", 256, 384), ("w_kv_b", 128, 512), ("w_mla_up", 256, 1024),
       ("w_swa_up", 256, 1024), ("w_out", 256, 1024), ("w_mlp_up", 1024, 1024), ("w_mlp_down", 1024, 1024),
       ("w_ple", 256, 256), ("w_ple_gate", 256, 1024))
COL_SHARDED = ("w_in", "w_q_b", "w_kv_b", "w_mlp_up", "w_ple")
PACK_ROWS = sum(r * c for _, r, c in BIG) // D
PACK_PAD = 4256
HALF = PACK_PAD // 2
SMALL = (("g_mix_pre", 1024), ("g_q_a", 256), ("g_kv_a", 128), ("sinks", 16), ("g_mix_post", 1024),
         ("g_mlp_pre", 1024), ("g_mlp_post", 1024), ("g_ple", 1024))


def _dot(a, b):
    return jnp.dot(a, b, preferred_element_type=F32)


def _dot_nt(a, b):
    return lax.dot_general(a, b, NT, preferred_element_type=F32)


def _dot_tn(a, b):
    return lax.dot_general(a, b, TN, preferred_element_type=F32)


def _pcall(body, *, name, out_shape, grid=(), in_specs=None, out_specs=None, scratch=(), sem=None, vmem_mb=48):
    params = dict(vmem_limit_bytes=vmem_mb << 20)
    if sem is not None:
        params["dimension_semantics"] = sem
    return pl.pallas_call(body, name=name, grid=grid, in_specs=in_specs, out_specs=out_specs, out_shape=out_shape,
                          scratch_shapes=list(scratch), compiler_params=pltpu.CompilerParams(**params))


def _rows(tm, n, col=0):
    return pl.BlockSpec((tm, n), lambda i: (i, col))


def _full(shape):
    return pl.BlockSpec(shape, lambda i: (0,) * len(shape))


def _rms(x, g):
    r = lax.rsqrt(jnp.mean(x * x, axis=-1, keepdims=True) + EPS)
    return x * r * g


def _rms_bwd(dy, x, g):
    r = lax.rsqrt(jnp.mean(x * x, axis=-1, keepdims=True) + EPS)
    xn = x * r
    dn = dy * g
    dx = r * (dn - xn * jnp.mean(dn * xn, axis=-1, keepdims=True))
    return dx, jnp.sum(dy * xn, axis=0, keepdims=True)


def _sigmoid(x):
    return 1.0 / (1.0 + jnp.exp(-x))


def _rope(x, c, a, b, half):
    return x * c + pltpu.roll(x, LANES - half, 1) * a + pltpu.roll(x, half, 1) * b


def _rope_tables(T, kind):
    lane = jnp.arange(LANES)
    if kind == "mla":
        half = MLA_ROPE // 2
        rel = lane - MLA_NOPE
        on = (rel >= 0) & (rel < MLA_ROPE)
        d = MLA_ROPE
    else:
        half = SWA_HD // 2
        rel = lane % SWA_HD
        on = jnp.ones((LANES,), bool)
        d = SWA_HD
    first = on & (rel < half)
    second = on & (rel >= half)
    f = jnp.where(first, rel, rel - half).astype(F32)
    inv = jnp.exp(-math.log(ROPE_THETA) * f * (2.0 / d))
    ang = jnp.arange(T, dtype=F32)[:, None] * inv[None, :]
    cos, sin = jnp.cos(ang), jnp.sin(ang)
    c = jnp.where(on[None], cos, 1.0)
    a = jnp.where(first[None], -sin, 0.0)
    b = jnp.where(second[None], sin, 0.0)
    return c, a, b


def _fwd_in(x, g1, w_in_p, tm):
    T = x.shape[0]

    def body(x_ref, g_ref, w_ref, z_ref, h_ref):
        h = _rms(x_ref[...], g_ref[...]).astype(BF16)
        h_ref[...] = h
        z_ref[...] = _dot(h, w_ref[...])

    return _pcall(body, name="fwd_in", grid=(T // tm,),
                  in_specs=[_rows(tm, D), _full((1, D)), _full((D, NZ))],
                  out_specs=[_rows(tm, NZ), _rows(tm, D)],
                  out_shape=[SDS((T, NZ), F32), SDS((T, D), BF16)], sem=("parallel",))(x, g1, w_in_p)


def _fwd_qkv(z, gq, gkv, wqb, wkn, wv, tab_m, tab_s, tm):
    T = z.shape[0]

    def body(qa_ref, sq_ref, skd_ref, svd_ref, kva_ref, kr_ref, gq_ref, gkv_ref, wqb_ref, wkn_ref, wv_ref,
             cm_ref, am_ref, bm_ref, cs_ref, as_ref, bs_ref,
             qn_ref, kvn_ref, qm_ref, km_ref, vm_ref, kt_ref, vt_ref, qs_ref, ks_ref, vs_ref):
        qn = _rms(qa_ref[...], gq_ref[...]).astype(BF16)
        qn_ref[...] = qn
        kvn = _rms(kva_ref[...], gkv_ref[...]).astype(BF16)
        kvn_ref[...] = kvn
        cm, am, bm = cm_ref[...], am_ref[...], bm_ref[...]
        cs, as_, bs = cs_ref[...], as_ref[...], bs_ref[...]
        k_rope = _rope(kr_ref[...], cm, am, bm, MLA_ROPE // 2)
        for j in range(D // LANES):
            sl = slice(LANES * j, LANES * (j + 1))
            v = _dot(kvn, wv_ref[:, sl])
            vm_ref[:, sl] = v.astype(BF16)
            vt_ref[0, sl, :] = v.T.astype(BF16)
        for h in range(MLA_HEADS):
            sl = slice(LANES * h, LANES * (h + 1))
            qh = _dot(qn, wqb_ref[:, sl])
            qm_ref[:, sl] = _rope(qh, cm, am, bm, MLA_ROPE // 2).astype(BF16)
            k = _dot(kvn, wkn_ref[:, sl]) + k_rope
            km_ref[:, sl] = k.astype(BF16)
            kt_ref[0, sl, :] = k.T.astype(BF16)
        for j in range(D // LANES):
            sl = slice(LANES * j, LANES * (j + 1))
            qs_ref[:, sl] = _rope(sq_ref[:, sl], cs, as_, bs, SWA_HD // 2).astype(BF16)
        for j in range(2):
            sl = slice(LANES * j, LANES * (j + 1))
            ks_ref[:, sl] = _rope(skd_ref[:, sl], cs, as_, bs, SWA_HD // 2).astype(BF16)
        vs_ref[...] = svd_ref[...].astype(BF16)

    tab = [_rows(tm, LANES)] * 6
    return _pcall(body, name="fwd_qkv", grid=(T // tm,),
                  in_specs=[_rows(tm, 256, 12), _rows(tm, 1024, 0), _rows(tm, 256, 13), _rows(tm, 256, 14),
                            _rows(tm, 128, 30), _rows(tm, 128, 31), _full((1, Q_LORA)), _full((1, KV_LORA)),
                            _full((Q_LORA, 2048)), _full((KV_LORA, 2048)), _full((KV_LORA, 1024))] + tab,
                  out_specs=[_rows(tm, Q_LORA), _rows(tm, KV_LORA), _rows(tm, 2048), _rows(tm, 2048), _rows(tm, 1024),
                             pl.BlockSpec((1, 2048, tm), lambda i: (i, 0, 0)), pl.BlockSpec((1, 1024, tm), lambda i: (i, 0, 0)),
                             _rows(tm, 1024), _rows(tm, 256), _rows(tm, 256)],
                  out_shape=[SDS((T, Q_LORA), BF16), SDS((T, KV_LORA), BF16), SDS((T, 2048), BF16), SDS((T, 2048), BF16),
                             SDS((T, 1024), BF16), SDS((T // tm, 2048, tm), BF16), SDS((T // tm, 1024, tm), BF16),
                             SDS((T, 1024), BF16), SDS((T, 256), BF16), SDS((T, 256), BF16)],
                  sem=("parallel",))(z, z, z, z, z, z, gq, gkv, wqb, wkn, wv, *tab_m, *tab_s)


def _mla_fwd(qm, km, vt, tb):
    T = qm.shape[0]
    nb = T // tb
    cc = ATT_COLS

    def body(q_ref, k_ref, vt_ref, o_ref, l_ref, s_ref, p_ref, m_ref, d_ref, acc_ref):
        i = pl.program_id(1)
        m_ref[...] = jnp.full(m_ref.shape, NEG, F32)
        d_ref[...] = jnp.zeros_like(d_ref)
        acc_ref[...] = jnp.zeros_like(acc_ref)
        key = lax.broadcasted_iota(jnp.int32, (tb, cc), 0)
        qry = lax.broadcasted_iota(jnp.int32, (tb, cc), 1)

        def scores(j, slot):
            off = pl.multiple_of(j * tb, tb)
            for hh in range(2):
                sl = slice(LANES * hh, LANES * (hh + 1))
                s_ref[slot, hh] = _dot_nt(k_ref[pl.ds(off, tb), sl], q_ref[:, sl])

        def softmax_pv(j, slot, diagonal):
            for hh in range(2):
                for c in range(tb // cc):
                    cols = slice(cc * c, cc * (c + 1))
                    if diagonal:
                        t = jnp.where(key <= qry + cc * c, s_ref[slot, hh, :, cols] * MLA_LOG2_SCALE, NEG)
                        top = jnp.max(t, axis=0, keepdims=True)
                    else:
                        top = jnp.max(s_ref[slot, hh, :, cols], axis=0, keepdims=True) * MLA_LOG2_SCALE
                    m_old = m_ref[hh, :, cols]
                    mn = jnp.maximum(m_old, top)
                    al = jnp.exp2(m_old - mn)
                    if diagonal:
                        p = jnp.exp2(t - mn)
                    else:
                        p = jnp.exp2(s_ref[slot, hh, :, cols] * MLA_LOG2_SCALE - mn)
                    m_ref[hh, :, cols] = mn
                    d_ref[hh, :, cols] = al * d_ref[hh, :, cols] + jnp.sum(p, axis=0, keepdims=True)
                    acc_ref[hh, :, cols] = al * acc_ref[hh, :, cols]
                    p_ref[hh, :, cols] = p.astype(BF16)
            v_t = vt_ref[j]
            for hh in range(2):
                acc_ref[hh] += _dot(v_t, p_ref[hh])

        def step(t, carry):
            scores(2 * t + 1, 1)
            softmax_pv(2 * t, 0, False)
            scores(2 * t + 2, 0)
            softmax_pv(2 * t + 1, 1, False)
            return carry

        scores(0, 0)
        lax.fori_loop(0, i // 2, step, 0)

        @pl.when(i % 2 == 1)
        def _():
            scores(i, 1)
            softmax_pv(i - 1, 0, False)
            softmax_pv(i, 1, True)

        @pl.when(i % 2 == 0)
        def _():
            softmax_pv(i, 0, True)
        first = lax.broadcasted_iota(jnp.int32, (LANES, tb), 0) < 64
        o_ref[...] = jnp.where(first, acc_ref[0] / d_ref[0], acc_ref[1] / d_ref[1]).T
        sub = lax.broadcasted_iota(jnp.int32, (8, tb), 0)
        lse = [m_ref[hh] + jnp.log(d_ref[hh]) * LOG2_E for hh in range(2)]
        l_ref[0, 0] = jnp.where(sub == 0, lse[0], jnp.where(sub == 1, lse[1], 0.0))

    return _pcall(body, name="mla_fwd", grid=(MLA_HEADS // 2, nb),
                  in_specs=[pl.BlockSpec((tb, 256), lambda p, i: (i, p)), pl.BlockSpec((T, 256), lambda p, i: (0, p)),
                            pl.BlockSpec((nb, LANES, tb), lambda p, i: (0, p, 0))],
                  out_specs=[pl.BlockSpec((tb, LANES), lambda p, i: (i, p)),
                             pl.BlockSpec((1, 1, 8, tb), lambda p, i: (p, i, 0, 0))],
                  out_shape=[SDS((T, D), F32), SDS((MLA_HEADS // 2, nb, 8, tb), F32)],
                  scratch=[pltpu.VMEM((2, 2, tb, tb), F32), pltpu.VMEM((2, tb, tb), BF16), pltpu.VMEM((2, 1, tb), F32),
                           pltpu.VMEM((2, 1, tb), F32), pltpu.VMEM((2, LANES, tb), F32)],
                  sem=("parallel", "arbitrary"))(qm, km, vt)


def _swa_mask(n):
    row = lax.broadcasted_iota(jnp.int32, (WINDOW, 2 * WINDOW), 0)
    col = lax.broadcasted_iota(jnp.int32, (WINDOW, 2 * WINDOW), 1)
    rel = row - col + WINDOW
    return (rel >= 0) & (rel < WINDOW) & ((col >= WINDOW) | (n > 0))


def _swa_specs(T):
    nb = T // WINDOW
    cur = lambda w: pl.BlockSpec((WINDOW, w), lambda n: (n, 0))
    prev = lambda w: pl.BlockSpec((WINDOW, w), lambda n: (jnp.maximum(n - 1, 0), 0))
    return nb, cur, prev


def _swa_fwd(sinks, qs, ks, vs):
    T = qs.shape[0]
    nb, cur, prev = _swa_specs(T)

    def body(sink_ref, q_ref, kc_ref, kp_ref, vc_ref, vp_ref, o_ref, l_ref):
        n = pl.program_id(0)
        mask = _swa_mask(n)
        lo = lax.broadcasted_iota(jnp.int32, (WINDOW, LANES), 1) < 64
        for g in range(2):
            gs = slice(LANES * g, LANES * (g + 1))
            kb = jnp.concatenate([kp_ref[:, gs], kc_ref[:, gs]], axis=0)
            vb = jnp.concatenate([vp_ref[:, gs], vc_ref[:, gs]], axis=0)
            for jj in range(4):
                j = 4 * g + jj
                sl = slice(LANES * j, LANES * (j + 1))
                qp = q_ref[:, sl]
                outs, lses = [], []
                for hf in range(2):
                    hm = lo if hf == 0 else jnp.logical_not(lo)
                    qh = jnp.where(hm, qp, jnp.zeros_like(qp))
                    s = jnp.where(mask, _dot_nt(qh, kb) * SWA_SCALE, NEG)
                    sk = sink_ref[2 * j + hf]
                    m = jnp.maximum(jnp.max(s, axis=1, keepdims=True), sk)
                    e = jnp.exp(s - m)
                    den = jnp.sum(e, axis=1, keepdims=True) + jnp.exp(sk - m)
                    p = e / den
                    outs.append(_dot(p.astype(BF16), vb))
                    lses.append(jnp.broadcast_to(m + jnp.log(den), (WINDOW, LANES)))
                o_ref[:, sl] = jnp.where(lo, outs[0], outs[1])
                l_ref[:, sl] = jnp.where(lo, lses[0], lses[1])

    return _pcall(body, name="swa_fwd", grid=(nb,),
                  in_specs=[pl.BlockSpec(memory_space=pltpu.SMEM), cur(D), cur(256), prev(256), cur(256), prev(256)],
                  out_specs=[cur(D), cur(D)], out_shape=[SDS((T, D), F32)] * 2,
                  sem=("parallel",))(sinks, qs, ks, ks, vs, vs)


def _fwd_mix(om, os_, z, x, wmu, wsu, wo, g2, tm):
    T = x.shape[0]

    def body(om_ref, os_ref, ga_ref, gb_ref, x_ref, wmu_ref, wsu_ref, wo_ref, g2_ref,
             y_ref, yo_ref, au_ref, bu_ref, x1_ref):
        au = _dot(om_ref[...].astype(BF16), wmu_ref[...])
        bu = _dot(os_ref[...].astype(BF16), wsu_ref[...])
        au_ref[...] = au
        bu_ref[...] = bu
        y = (_sigmoid(ga_ref[...]) * au + _sigmoid(gb_ref[...]) * bu).astype(BF16)
        y_ref[...] = y
        yo = _dot(y, wo_ref[...])
        yo_ref[...] = yo
        x1_ref[...] = x_ref[...] + _rms(yo, g2_ref[...])

    r = _rows(tm, D)
    w = _full((D, D))
    return _pcall(body, name="fwd_mix", grid=(T // tm,),
                  in_specs=[r, r, _rows(tm, D, 1), _rows(tm, D, 2), r, w, w, w, _full((1, D))],
                  out_specs=[r] * 5,
                  out_shape=[SDS((T, D), BF16), SDS((T, D), F32), SDS((T, D), F32), SDS((T, D), F32), SDS((T, D), F32)],
                  sem=("parallel",))(om, os_, z, z, x, wmu, wsu, wo, g2)


def _fwd_mlp_up(x1, g3, w1, tm):
    T = x1.shape[0]

    def body(x_ref, g_ref, w_ref, h_ref, a_ref, u_ref):
        h = _rms(x_ref[...], g_ref[...]).astype(BF16)
        h_ref[...] = h
        a = _dot(h, w_ref[...])
        a_ref[...] = a
        u_ref[...] = jnp.square(jnp.maximum(a, 0.0)).astype(BF16)

    return _pcall(body, name="fwd_mlp_up", grid=(T // tm,),
                  in_specs=[_rows(tm, D), _full((1, D)), _full((D, D_FF))],
                  out_specs=[_rows(tm, D), _rows(tm, D_FF), _rows(tm, D_FF)],
                  out_shape=[SDS((T, D), BF16), SDS((T, D_FF), F32), SDS((T, D_FF), BF16)],
                  sem=("parallel",))(x1, g3, w1)


def _fwd_mlp_down(u, w2, x1, g4, tm):
    T = x1.shape[0]

    def body(u_ref, w_ref, x_ref, g_ref, d_ref, x2_ref):
        d = _dot(u_ref[...], w_ref[...])
        d_ref[...] = d
        x2_ref[...] = x_ref[...] + _rms(d, g_ref[...])

    return _pcall(body, name="fwd_mlp_down", grid=(T // tm,),
                  in_specs=[_rows(tm, D_FF), _full((D_FF, D)), _rows(tm, D), _full((1, D))],
                  out_specs=[_rows(tm, D), _rows(tm, D)], out_shape=[SDS((T, D), F32)] * 2,
                  sem=("parallel",))(u, w2, x1, g4)


def _ple_fwd_bwd(p, x2, tgt, wple, g5, wpg, tm):
    T = x2.shape[0]

    def body(p_ref, x2_ref, t_ref, wple_ref, g5_ref, wpg_ref, loss_ref, dx2_ref, dgt_ref, de0_ref, dg5_ref):
        @pl.when(pl.program_id(0) == 0)
        def _():
            loss_ref[...] = jnp.zeros_like(loss_ref)
            dg5_ref[...] = jnp.zeros_like(dg5_ref)

        e0 = _dot(p_ref[...].astype(BF16), wple_ref[...])
        g5 = g5_ref[...]
        r = lax.rsqrt(jnp.mean(e0 * e0, axis=-1, keepdims=True) + EPS)
        en = e0 * r
        e = en * g5
        x2 = x2_ref[...]
        s = _sigmoid(_dot(x2.astype(BF16), wpg_ref[...]))
        diff = x2 + s * e - t_ref[...]
        sq = jnp.sum(jnp.sum(diff * diff, axis=1, keepdims=True), axis=0, keepdims=True)
        loss_ref[...] += jnp.broadcast_to(sq * (0.5 / D), loss_ref.shape)
        dx3 = diff * (1.0 / D)
        de = dx3 * s
        dgt = (dx3 * e * s * (1.0 - s)).astype(BF16)
        dgt_ref[...] = dgt
        dn = de * g5
        de0_ref[...] = (r * (dn - en * jnp.mean(dn * en, axis=-1, keepdims=True))).astype(BF16)
        dg5_ref[...] += jnp.sum(de * en, axis=0, keepdims=True)
        dx2_ref[...] = dx3 + _dot_nt(dgt, wpg_ref[...])

    r = _rows(tm, D)
    return _pcall(body, name="ple_fwd_bwd", grid=(T // tm,),
                  in_specs=[_rows(tm, PLE), r, r, _full((PLE, D)), _full((1, D)), _full((D, D))],
                  out_specs=[_full((8, LANES)), r, r, r, _full((1, D))],
                  out_shape=[SDS((8, LANES), F32), SDS((T, D), F32), SDS((T, D), BF16), SDS((T, D), BF16), SDS((1, D), F32)],
                  sem=("arbitrary",))(p, x2, tgt, wple, g5, wpg)


def _bwd_mlp_down(dx2, d, g4, w2, a, tm):
    T = dx2.shape[0]

    def body(dx_ref, d_ref, g_ref, w_ref, a_ref, dd_ref, da_ref, dg_ref):
        @pl.when(pl.program_id(0) == 0)
        def _():
            dg_ref[...] = jnp.zeros_like(dg_ref)

        dd, dg = _rms_bwd(dx_ref[...], d_ref[...], g_ref[...])
        dg_ref[...] += dg
        ddb = dd.astype(BF16)
        dd_ref[...] = ddb
        du = _dot_nt(ddb, w_ref[...])
        da_ref[...] = (du * (2.0 * jnp.maximum(a_ref[...], 0.0))).astype(BF16)

    return _pcall(body, name="bwd_mlp_down", grid=(T // tm,),
                  in_specs=[_rows(tm, D), _rows(tm, D), _full((1, D)), _full((D_FF, D)), _rows(tm, D_FF)],
                  out_specs=[_rows(tm, D), _rows(tm, D_FF), _full((1, D))],
                  out_shape=[SDS((T, D), BF16), SDS((T, D_FF), BF16), SDS((1, D), F32)],
                  sem=("arbitrary",))(dx2, d, g4, w2, a)


def _bwd_mlp_up(da, w1, x1, g3, dx2, tm):
    T = dx2.shape[0]

    def body(da_ref, w_ref, x_ref, g_ref, dx2_ref, dx1_ref, dg_ref):
        @pl.when(pl.program_id(0) == 0)
        def _():
            dg_ref[...] = jnp.zeros_like(dg_ref)

        dh = _dot_nt(da_ref[...], w_ref[...])
        dx, dg = _rms_bwd(dh, x_ref[...], g_ref[...])
        dg_ref[...] += dg
        dx1_ref[...] = dx2_ref[...] + dx

    return _pcall(body, name="bwd_mlp_up", grid=(T // tm,),
                  in_specs=[_rows(tm, D_FF), _full((D, D_FF)), _rows(tm, D), _full((1, D)), _rows(tm, D)],
                  out_specs=[_rows(tm, D), _full((1, D))],
                  out_shape=[SDS((T, D), F32), SDS((1, D), F32)], sem=("arbitrary",))(da, w1, x1, g3, dx2)


def _bwd_mix(dx1, yo, g2, wo, z, au, bu, wmu, wsu, om, tm):
    T = dx1.shape[0]

    def body(dx_ref, yo_ref, g_ref, wo_ref, ga_ref, gb_ref, au_ref, bu_ref, wmu_ref, wsu_ref, om_ref,
             dyo_ref, dg_ref, dau_ref, dbu_ref, dga_ref, dgb_ref, dom_ref, dos_ref, dl_ref):
        @pl.when(pl.program_id(0) == 0)
        def _():
            dg_ref[...] = jnp.zeros_like(dg_ref)

        dyo, dg = _rms_bwd(dx_ref[...], yo_ref[...], g_ref[...])
        dg_ref[...] += dg
        dyob = dyo.astype(BF16)
        dyo_ref[...] = dyob
        dy = _dot_nt(dyob, wo_ref[...])
        sa = _sigmoid(ga_ref[...])
        sb = _sigmoid(gb_ref[...])
        dau = (dy * sa).astype(BF16)
        dbu = (dy * sb).astype(BF16)
        dau_ref[...] = dau
        dbu_ref[...] = dbu
        dga_ref[...] = (dy * au_ref[...] * sa * (1.0 - sa)).astype(BF16)
        dgb_ref[...] = (dy * bu_ref[...] * sb * (1.0 - sb)).astype(BF16)
        dom = _dot_nt(dau, wmu_ref[...])
        dom_ref[...] = dom
        dos_ref[...] = _dot_nt(dbu, wsu_ref[...])
        prod = dom * om_ref[...]
        sub = lax.broadcasted_iota(jnp.int32, (8, tm), 0)
        for pr in range(MLA_HEADS // 2):
            pt = prod[:, LANES * pr:LANES * (pr + 1)].T
            d0 = jnp.sum(pt[0:64], axis=0, keepdims=True)
            d1 = jnp.sum(pt[64:128], axis=0, keepdims=True)
            dl_ref[pr, 0] = jnp.where(sub == 0, d0, jnp.where(sub == 1, d1, 0.0))

    r = _rows(tm, D)
    w = _full((D, D))
    return _pcall(body, name="bwd_mix", grid=(T // tm,),
                  in_specs=[r, r, _full((1, D)), w, _rows(tm, D, 1), _rows(tm, D, 2), r, r, w, w, r],
                  out_specs=[r, _full((1, D)), r, r, r, r, r, r, pl.BlockSpec((MLA_HEADS // 2, 1, 8, tm), lambda i: (0, i, 0, 0))],
                  out_shape=[SDS((T, D), BF16), SDS((1, D), F32), SDS((T, D), BF16), SDS((T, D), BF16), SDS((T, D), BF16),
                             SDS((T, D), BF16), SDS((T, D), F32), SDS((T, D), F32), SDS((MLA_HEADS // 2, T // tm, 8, tm), F32)],
                  sem=("arbitrary",))(dx1, yo, g2, wo, z, z, au, bu, wmu, wsu, om)


def _mla_bwd(qm, km, kt, vm, do, lse, delta, tb):
    T = qm.shape[0]
    nb = T // tb
    cc = ATT_COLS

    def body(q_ref, k_ref, kt_ref, v_ref, do_ref, l_ref, dl_ref, dqt_ref, dk_ref, dv_ref, s_ref, dp_ref, p_ref, ds_ref, dom_ref):
        j = pl.program_id(1)

        @pl.when(j == 0)
        def _():
            dqt_ref[...] = jnp.zeros_like(dqt_ref)

        dk_ref[...] = jnp.zeros_like(dk_ref)
        dv_ref[...] = jnp.zeros_like(dv_ref)
        lo = lax.broadcasted_iota(jnp.int32, (tb, LANES), 1) < 64
        key = lax.broadcasted_iota(jnp.int32, (tb, cc), 0)
        qry = lax.broadcasted_iota(jnp.int32, (tb, cc), 1)

        def scores(i, slot):
            rows_i = pl.ds(pl.multiple_of(i * tb, tb), tb)
            d_o = do_ref[rows_i, :]
            v = v_ref[...]
            for hh in range(2):
                sl = slice(LANES * hh, LANES * (hh + 1))
                hm = lo if hh == 0 else jnp.logical_not(lo)
                dom_ref[slot, hh] = jnp.where(hm, d_o, 0.0).astype(BF16)
                s_ref[slot, hh] = _dot_nt(k_ref[:, sl], q_ref[rows_i, sl])
                dp_ref[slot, hh] = _dot_nt(v, dom_ref[slot, hh])

        def grads(i, slot, diagonal):
            rows_i = pl.ds(pl.multiple_of(i * tb, tb), tb)
            lse_i = l_ref[0, i]
            delta_i = dl_ref[0, i]
            for hh in range(2):
                for c in range(tb // cc):
                    cols = slice(cc * c, cc * (c + 1))
                    p = jnp.exp2(s_ref[slot, hh, :, cols] * MLA_LOG2_SCALE - lse_i[hh:hh + 1, cols])
                    if diagonal:
                        p = jnp.where(key <= qry + cc * c, p, 0.0)
                    p_ref[hh, :, cols] = p.astype(BF16)
                    ds_ref[hh, :, cols] = (p * (dp_ref[slot, hh, :, cols] - delta_i[hh:hh + 1, cols]) * MLA_SCALE).astype(BF16)
            for hh in range(2):
                sl = slice(LANES * hh, LANES * (hh + 1))
                dv_ref[...] += _dot(p_ref[hh], dom_ref[slot, hh])
                dk_ref[:, sl] += _dot(ds_ref[hh], q_ref[rows_i, sl])
                dqt_ref[i, sl, :] += _dot(kt_ref[0, sl, :], ds_ref[hh])

        n_off = nb - 1 - j

        def step(u, carry):
            i0 = j + 1 + 2 * u
            scores(i0 + 1, 1)
            grads(i0, 0, False)
            scores(jnp.where(i0 + 2 < nb, i0 + 2, j), 0)
            grads(i0 + 1, 1, False)
            return carry

        scores(jnp.where(n_off > 0, j + 1, j), 0)
        lax.fori_loop(0, n_off // 2, step, 0)

        @pl.when(n_off % 2 == 1)
        def _():
            scores(j, 1)
            grads(nb - 1, 0, False)
            grads(j, 1, True)

        @pl.when(n_off % 2 == 0)
        def _():
            grads(j, 0, True)

    pair = lambda w: pl.BlockSpec((T, w), lambda p, j: (0, p))
    blk = lambda w: pl.BlockSpec((tb, w), lambda p, j: (j, p))
    stat = pl.BlockSpec((1, nb, 8, tb), lambda p, j: (p, 0, 0, 0))
    return _pcall(body, name="mla_bwd", grid=(MLA_HEADS // 2, nb),
                  in_specs=[pair(256), blk(256), pl.BlockSpec((1, 256, tb), lambda p, j: (j, p, 0)), blk(LANES), pair(LANES),
                            stat, stat],
                  out_specs=[pl.BlockSpec((nb, 256, tb), lambda p, j: (0, p, 0)), blk(256), blk(LANES)],
                  out_shape=[SDS((nb, 2048, tb), F32), SDS((T, 2048), F32), SDS((T, D), F32)],
                  scratch=[pltpu.VMEM((2, 2, tb, tb), F32), pltpu.VMEM((2, 2, tb, tb), F32), pltpu.VMEM((2, tb, tb), BF16),
                           pltpu.VMEM((2, tb, tb), BF16), pltpu.VMEM((2, 2, tb, LANES), BF16)],
                  sem=("parallel", "arbitrary"))(qm, km, kt, vm, do, lse, delta)


def _swa_bwd(sinks, qs, ks, vs, do, o, lse):
    T = qs.shape[0]
    nb, cur, prev = _swa_specs(T)

    def body(sink_ref, q_ref, kc_ref, kp_ref, vc_ref, vp_ref, do_ref, o_ref, l_ref,
             dq_ref, dkc_ref, dkp_ref, dvc_ref, dvp_ref, dsink_ref):
        n = pl.program_id(0)

        @pl.when(n == 0)
        def _():
            dsink_ref[...] = jnp.zeros_like(dsink_ref)

        mask = _swa_mask(n)
        lo = lax.broadcasted_iota(jnp.int32, (WINDOW, LANES), 1) < 64
        lane8 = lax.broadcasted_iota(jnp.int32, (8, LANES), 1)
        dsink = jnp.zeros((8, LANES), F32)
        for g in range(2):
            gs = slice(LANES * g, LANES * (g + 1))
            kb = jnp.concatenate([kp_ref[:, gs], kc_ref[:, gs]], axis=0)
            vb = jnp.concatenate([vp_ref[:, gs], vc_ref[:, gs]], axis=0)
            dkb = jnp.zeros((2 * WINDOW, LANES), F32)
            dvb = jnp.zeros((2 * WINDOW, LANES), F32)
            for jj in range(4):
                j = 4 * g + jj
                sl = slice(LANES * j, LANES * (j + 1))
                qp = q_ref[:, sl]
                d_o = do_ref[:, sl]
                prod = d_o * o_ref[:, sl]
                lse_b = l_ref[:, sl]
                dqs = []
                for hf in range(2):
                    hm = lo if hf == 0 else jnp.logical_not(lo)
                    qh = jnp.where(hm, qp, jnp.zeros_like(qp))
                    s = jnp.where(mask, _dot_nt(qh, kb) * SWA_SCALE, NEG)
                    lse_h = jnp.max(jnp.where(hm, lse_b, -jnp.inf), axis=1, keepdims=True)
                    p = jnp.exp(s - lse_h)
                    dom = jnp.where(hm, d_o, 0.0).astype(BF16)
                    dp = _dot_nt(dom, vb)
                    delta = jnp.sum(jnp.where(hm, prod, 0.0), axis=1, keepdims=True)
                    ds = (p * (dp - delta) * SWA_SCALE).astype(BF16)
                    p_sink = jnp.exp(sink_ref[2 * j + hf] - lse_h)
                    d_sink = -jnp.sum(p_sink * delta, axis=0, keepdims=True)
                    dsink = dsink + jnp.where(lane8 == 2 * j + hf, d_sink, 0.0)
                    dvb = dvb + _dot_tn(p.astype(BF16), dom)
                    dkb = dkb + _dot_tn(ds, qh)
                    dqs.append(_dot(ds, kb))
                dq_ref[:, sl] = jnp.where(lo, dqs[0], dqs[1])
            dkp_ref[:, gs] = dkb[:WINDOW]
            dkc_ref[:, gs] = dkb[WINDOW:]
            dvp_ref[:, gs] = dvb[:WINDOW]
            dvc_ref[:, gs] = dvb[WINDOW:]
        dsink_ref[...] += dsink

    return _pcall(body, name="swa_bwd", grid=(nb,),
                  in_specs=[pl.BlockSpec(memory_space=pltpu.SMEM), cur(D), cur(256), prev(256), cur(256), prev(256),
                            cur(D), cur(D), cur(D)],
                  out_specs=[cur(D), cur(256), cur(256), cur(256), cur(256), _full((8, LANES))],
                  out_shape=[SDS((T, D), F32), SDS((T, 256), F32), SDS((T, 256), F32), SDS((T, 256), F32), SDS((T, 256), F32),
                             SDS((8, LANES), F32)],
                  sem=("arbitrary",))(sinks, qs, ks, ks, vs, vs, do, o, lse)


def _bwd_qkv(dqm, dkm, dvm, dqs, dkc, dkp, dvc, dvp, z, gq, gkv, wqb, wkn, wv, tab_m, tab_s):
    T = z.shape[0]
    tm = WINDOW
    nb = T // tm
    per = dqm.shape[2] // tm

    def body(dqm_ref, dkm_ref, dvm_ref, dqs_ref, dkc_ref, dkp_ref, dvc_ref, dvp_ref, qa_ref, kva_ref, gq_ref, gkv_ref,
             wqb_ref, wkn_ref, wv_ref, cm_ref, am_ref, bm_ref, cs_ref, as_ref, bs_ref,
             dq_out, dkn_out, dv_out, dsq_ref, drest_ref, dgq_ref, dgkv_ref):
        i = pl.program_id(0)

        @pl.when(i == 0)
        def _():
            dgq_ref[...] = jnp.zeros_like(dgq_ref)
            dgkv_ref[...] = jnp.zeros_like(dgkv_ref)

        cm, am, bm = cm_ref[...], -am_ref[...], -bm_ref[...]
        cs, as_, bs = cs_ref[...], -as_ref[...], -bs_ref[...]
        lane = lax.broadcasted_iota(jnp.int32, (tm, LANES), 1)
        nope = lane < MLA_NOPE
        roped = jnp.logical_and(lane >= MLA_NOPE, lane < MLA_NOPE + MLA_ROPE)
        dkr = jnp.zeros((tm, LANES), F32)
        dqn = jnp.zeros((tm, Q_LORA), F32)
        dkvn = jnp.zeros((tm, KV_LORA), F32)
        for h in range(MLA_HEADS):
            sl = slice(LANES * h, LANES * (h + 1))
            dq_h = _rope(dqm_ref[0, sl, :].T, cm, am, bm, MLA_ROPE // 2).astype(BF16)
            dq_out[:, sl] = dq_h
            dqn = dqn + _dot_nt(dq_h, wqb_ref[:, sl])
            dk_h = dkm_ref[:, sl]
            dkn_h = jnp.where(nope, dk_h, 0.0).astype(BF16)
            dkn_out[:, sl] = dkn_h
            dkvn = dkvn + _dot_nt(dkn_h, wkn_ref[:, sl])
            dkr = dkr + jnp.where(roped, dk_h, 0.0)
        dvb = dvm_ref[...].astype(BF16)
        dv_out[...] = dvb
        dkvn = dkvn + _dot_nt(dvb, wv_ref[...])
        dqa, dgq = _rms_bwd(dqn, qa_ref[...], gq_ref[...])
        dkva, dgkv = _rms_bwd(dkvn, kva_ref[...], gkv_ref[...])
        dgq_ref[...] += dgq
        dgkv_ref[...] += dgkv
        for j in range(D // LANES):
            sl = slice(LANES * j, LANES * (j + 1))
            dsq_ref[:, sl] = _rope(dqs_ref[:, sl], cs, as_, bs, SWA_HD // 2).astype(BF16)
        keep = (i < nb - 1).astype(F32)
        drest_ref[:, 0:256] = dqa.astype(BF16)
        for j in range(2):
            sl = slice(LANES * j, LANES * (j + 1))
            dk = dkc_ref[:, sl] + keep * dkp_ref[:, sl]
            drest_ref[:, 256 + LANES * j:256 + LANES * (j + 1)] = _rope(dk, cs, as_, bs, SWA_HD // 2).astype(BF16)
        drest_ref[:, 512:768] = (dvc_ref[...] + keep * dvp_ref[...]).astype(BF16)
        drest_ref[:, 768:896] = dkva.astype(BF16)
        drest_ref[:, 896:1024] = _rope(dkr, cm, am, bm, MLA_ROPE // 2).astype(BF16)

    nxt = pl.BlockSpec((tm, 256), lambda i: (jnp.minimum(i + 1, nb - 1), 0))
    tab = [_rows(tm, LANES)] * 6
    return _pcall(body, name="bwd_qkv", grid=(nb,),
                  in_specs=[pl.BlockSpec((1, 2048, tm), lambda i: (i // per, 0, i % per)),
                            _rows(tm, 2048), _rows(tm, 1024), _rows(tm, 1024), _rows(tm, 256), nxt,
                            _rows(tm, 256), nxt, _rows(tm, 256, 12), _rows(tm, 128, 30), _full((1, Q_LORA)), _full((1, KV_LORA)),
                            _full((Q_LORA, 2048)), _full((KV_LORA, 2048)), _full((KV_LORA, 1024))] + tab,
                  out_specs=[_rows(tm, 2048), _rows(tm, 2048), _rows(tm, 1024), _rows(tm, 1024), _rows(tm, 1024),
                             _full((1, Q_LORA)), _full((1, KV_LORA))],
                  out_shape=[SDS((T, 2048), BF16), SDS((T, 2048), BF16), SDS((T, 1024), BF16), SDS((T, 1024), BF16),
                             SDS((T, 1024), BF16), SDS((1, Q_LORA), F32), SDS((1, KV_LORA), F32)],
                  sem=("arbitrary",))(dqm, dkm, dvm, dqs, dkc, dkp, dvc, dvp, z, z, gq, gkv, wqb, wkn, wv, *tab_m, *tab_s)


def _bwd_in(dsq, dga, dgb, drest, w_in_p, x, g1, dx1, tm):
    T = x.shape[0]

    def body(a_ref, b_ref, c_ref, d_ref, w_ref, x_ref, g_ref, dx1_ref, dx_ref, dg_ref):
        @pl.when(pl.program_id(0) == 0)
        def _():
            dg_ref[...] = jnp.zeros_like(dg_ref)

        dh = (_dot_nt(a_ref[...], w_ref[:, 0:1024]) + _dot_nt(b_ref[...], w_ref[:, 1024:2048])
              + _dot_nt(c_ref[...], w_ref[:, 2048:3072]) + _dot_nt(d_ref[...], w_ref[:, 3072:4096]))
        dx, dg = _rms_bwd(dh, x_ref[...], g_ref[...])
        dg_ref[...] += dg
        dx_ref[...] = dx1_ref[...] + dx

    r = _rows(tm, D)
    return _pcall(body, name="bwd_in", grid=(T // tm,),
                  in_specs=[r, r, r, r, _full((D, NZ)), r, _full((1, D)), r],
                  out_specs=[r, _full((1, D))], out_shape=[SDS((T, D), F32), SDS((1, D), F32)],
                  sem=("arbitrary",))(dsq, dga, dgb, drest, w_in_p, x, g1, dx1)


def _wgrad(a, g, name):
    T, K = a.shape
    N = g.shape[1]
    tk, tn, tt = min(K, 512), min(N, 1024), min(T, 512)
    assert K % tk == 0 and N % tn == 0 and T % tt == 0, (a.shape, g.shape)

    def body(a_ref, g_ref, o_ref):
        @pl.when(pl.program_id(2) == 0)
        def _():
            o_ref[...] = jnp.zeros_like(o_ref)

        o_ref[...] += _dot_tn(a_ref[...].astype(BF16), g_ref[...].astype(BF16))

    return _pcall(body, name=name, grid=(K // tk, N // tn, T // tt),
                  in_specs=[pl.BlockSpec((tt, tk), lambda k, n, t: (t, k)), pl.BlockSpec((tt, tn), lambda k, n, t: (t, n))],
                  out_specs=pl.BlockSpec((tk, tn), lambda k, n, t: (k, n)), out_shape=SDS((K, N), F32),
                  sem=("parallel", "parallel", "arbitrary"))(a, g)


def _adamw(w, g, m, v, name):
    R, C = w.shape
    tr = min(R, 256)

    def body(w_ref, g_ref, m_ref, v_ref, d_ref, m2_ref, v2_ref):
        g_ = g_ref[...]
        m2 = ADAM_B1 * m_ref[...] + (1.0 - ADAM_B1) * g_
        v2 = ADAM_B2 * v_ref[...] + (1.0 - ADAM_B2) * jnp.square(g_)
        m_hat = m2 / (1.0 - ADAM_B1 ** ADAM_STEP)
        v_hat = v2 / (1.0 - ADAM_B2 ** ADAM_STEP)
        d_ref[...] = -ADAM_LR * (m_hat / (jnp.sqrt(v_hat) + ADAM_EPS) + ADAM_WD * w_ref[...])
        m2_ref[...] = m2
        v2_ref[...] = v2

    r = _rows(tr, C)
    return _pcall(body, name=name, grid=(R // tr,), in_specs=[r] * 4, out_specs=[r] * 3,
                  out_shape=[SDS((R, C), F32)] * 3, sem=("parallel",))(w, g, m, v)


def _adamw_small(w, parts, m, v):
    def body(w_ref, p_ref, m_ref, v_ref, g_ref, d_ref, m2_ref, v2_ref):
        g_ = p_ref[0]
        for k in range(1, N_DEV):
            g_ = g_ + p_ref[k]
        g_ref[...] = g_
        m2 = ADAM_B1 * m_ref[...] + (1.0 - ADAM_B1) * g_
        v2 = ADAM_B2 * v_ref[...] + (1.0 - ADAM_B2) * jnp.square(g_)
        m_hat = m2 / (1.0 - ADAM_B1 ** ADAM_STEP)
        v_hat = v2 / (1.0 - ADAM_B2 ** ADAM_STEP)
        d_ref[...] = -ADAM_LR * (m_hat / (jnp.sqrt(v_hat) + ADAM_EPS) + ADAM_WD * w_ref[...])
        m2_ref[...] = m2
        v2_ref[...] = v2

    s = _full((8, D))
    return _pcall(body, name="adamw_small", grid=(1,), in_specs=[s, _full((N_DEV, 8, D)), s, s], out_specs=[s] * 4,
                  out_shape=[SDS((8, D), F32)] * 4, sem=("arbitrary",))(w, parts, m, v)


ANY = pl.BlockSpec(memory_space=pl.ANY)


def _place():
    x, y, c = lax.axis_index("x"), lax.axis_index("y"), lax.axis_index("c")
    chips = [(1 - x, y), (x, 1 - y), (1 - x, 1 - y)]
    return x, y, c, chips


def _all_gather(wpk):
    def body(in_ref, out_ref, send_sems, recv_sems, local_sem):
        x, y, c, chips = _place()
        half = pl.ds(pl.multiple_of(c * HALF, 16), HALF)
        other = pl.ds(pl.multiple_of((1 - c) * HALF, 16), HALF)

        def copy(k, src, dst, to):
            return pltpu.make_async_remote_copy(src_ref=src, dst_ref=dst, send_sem=send_sems.at[k], recv_sem=recv_sems.at[k],
                                                device_id=to, device_id_type=MESH)

        mine = pltpu.make_async_copy(in_ref, out_ref.at[2 * x + y], local_sem)
        mine.start()
        first = [copy(k, in_ref.at[half], out_ref.at[2 * x + y, half], (cx, cy, c)) for k, (cx, cy) in enumerate(chips)]
        for cp in first:
            cp.start()
        passed = []
        for k, (cx, cy) in enumerate(chips):
            slot = out_ref.at[2 * cx + cy, half]
            copy(k, slot, slot, (x, y, c)).wait_recv()
            fwd = copy(3 + k, slot, slot, (x, y, 1 - c))
            fwd.start()
            passed.append(fwd)
        for k, (cx, cy) in enumerate(chips):
            slot = out_ref.at[2 * cx + cy, other]
            copy(3 + k, slot, slot, (x, y, c)).wait_recv()
        for cp in first + passed:
            cp.wait_send()
        mine.wait()

    return _pcall(body, name="all_gather_weights", in_specs=[ANY], out_specs=ANY,
                  out_shape=SDS((N_CHIPS, PACK_PAD, D), BF16),
                  scratch=[pltpu.SemaphoreType.DMA((6,)), pltpu.SemaphoreType.DMA((6,)), pltpu.SemaphoreType.DMA])(wpk)


def _rs_sibling(gpk):
    def body(in_ref, out_ref, send_sem, recv_sem):
        x, y, c, _ = _place()
        theirs = pl.ds(pl.multiple_of((1 - c) * HALF, 8), HALF)
        cp = pltpu.make_async_remote_copy(src_ref=in_ref.at[:, theirs], dst_ref=out_ref, send_sem=send_sem, recv_sem=recv_sem,
                                          device_id=(x, y, 1 - c), device_id_type=MESH)
        cp.start()
        cp.wait()

    return _pcall(body, name="rs_sibling", in_specs=[ANY], out_specs=ANY, out_shape=SDS((N_CHIPS, HALF, D), F32),
                  scratch=[pltpu.SemaphoreType.DMA, pltpu.SemaphoreType.DMA])(gpk)


def _rs_add_sibling(cidx, gpk, got):
    th = HALF // 7
    nh = HALF // th

    def body(c_ref, a_ref, b_ref, o_ref):
        o_ref[...] = (a_ref[...] + b_ref[...]).astype(BF16)

    gs = pltpu.PrefetchScalarGridSpec(
        num_scalar_prefetch=1, grid=(N_CHIPS, nh),
        in_specs=[pl.BlockSpec((1, th, D), lambda j, i, c: (j, c[0] * nh + i, 0)), pl.BlockSpec((1, th, D), lambda j, i, c: (j, i, 0))],
        out_specs=pl.BlockSpec((1, th, D), lambda j, i, c: (j, i, 0)))
    return pl.pallas_call(body, name="rs_add_sibling", grid_spec=gs, out_shape=SDS((N_CHIPS, HALF, D), BF16),
                          compiler_params=pltpu.CompilerParams(dimension_semantics=("parallel", "parallel"),
                                                               vmem_limit_bytes=48 << 20))(cidx, gpk, got)


def _rs_chips(part, small):
    def body(p_ref, s_ref, o_ref, so_ref, send_sems, recv_sems, ssend_sems, srecv_sems, local_sems):
        x, y, c, chips = _place()
        me = 2 * x + y
        mine = pltpu.make_async_copy(p_ref.at[me], o_ref.at[me], local_sems.at[0])
        mine_s = pltpu.make_async_copy(s_ref, so_ref.at[4 * x + 2 * y + c], local_sems.at[1])
        mine.start()
        mine_s.start()
        sends = []
        for k, (cx, cy) in enumerate(chips):
            sends.append(pltpu.make_async_remote_copy(src_ref=p_ref.at[2 * cx + cy], dst_ref=o_ref.at[me], send_sem=send_sems.at[k],
                                                      recv_sem=recv_sems.at[k], device_id=(cx, cy, c), device_id_type=MESH))
        peers = [(x, y, 1 - c)] + [(cx, cy, c) for cx, cy in chips] + [(cx, cy, 1 - c) for cx, cy in chips]
        for k, to in enumerate(peers):
            sends.append(pltpu.make_async_remote_copy(src_ref=s_ref, dst_ref=so_ref.at[4 * x + 2 * y + c], send_sem=ssend_sems.at[k],
                                                      recv_sem=srecv_sems.at[k], device_id=to, device_id_type=MESH))
        for cp in sends:
            cp.start()
        for k, (cx, cy) in enumerate(chips):
            slot = o_ref.at[2 * cx + cy]
            pltpu.make_async_remote_copy(src_ref=slot, dst_ref=slot, send_sem=send_sems.at[k], recv_sem=recv_sems.at[k],
                                         device_id=(x, y, c), device_id_type=MESH).wait_recv()
        for k, (px, py, pc) in enumerate(peers):
            slot = so_ref.at[4 * px + 2 * py + pc]
            pltpu.make_async_remote_copy(src_ref=slot, dst_ref=slot, send_sem=ssend_sems.at[k], recv_sem=srecv_sems.at[k],
                                         device_id=(x, y, c), device_id_type=MESH).wait_recv()
        for cp in sends:
            cp.wait_send()
        mine.wait()
        mine_s.wait()

    return _pcall(body, name="rs_chips", in_specs=[ANY, ANY], out_specs=[ANY, ANY],
                  out_shape=[SDS((N_CHIPS, HALF, D), part.dtype), SDS((N_DEV, 8, D), F32)],
                  scratch=[pltpu.SemaphoreType.DMA((3,)), pltpu.SemaphoreType.DMA((3,)), pltpu.SemaphoreType.DMA((7,)),
                           pltpu.SemaphoreType.DMA((7,)), pltpu.SemaphoreType.DMA((2,))])(part, small)


def _rs_add_chips(parts):
    th = HALF // 7

    def body(p_ref, o_ref):
        o_ref[...] = ((p_ref[0].astype(F32) + p_ref[1].astype(F32)) + p_ref[2].astype(F32)) + p_ref[3].astype(F32)

    return _pcall(body, name="rs_add_chips", grid=(HALF // th,),
                  in_specs=[pl.BlockSpec((N_CHIPS, th, D), lambda i: (0, i, 0))], out_specs=_rows(th, D),
                  out_shape=SDS((HALF, D), F32), sem=("parallel",))(parts)


def _rs_join(total):
    def body(in_ref, out_ref, send_sem, recv_sem, local_sem):
        x, y, c, _ = _place()
        mine = pltpu.make_async_copy(in_ref, out_ref.at[c], local_sem)
        mine.start()
        cp = pltpu.make_async_remote_copy(src_ref=in_ref, dst_ref=out_ref.at[c], send_sem=send_sem, recv_sem=recv_sem,
                                          device_id=(x, y, 1 - c), device_id_type=MESH)
        cp.start()
        slot = out_ref.at[1 - c]
        pltpu.make_async_remote_copy(src_ref=slot, dst_ref=slot, send_sem=send_sem, recv_sem=recv_sem,
                                     device_id=(x, y, c), device_id_type=MESH).wait_recv()
        cp.wait_send()
        mine.wait()

    return _pcall(body, name="rs_join", in_specs=[ANY], out_specs=ANY, out_shape=SDS((2, HALF, D), F32),
                  scratch=[pltpu.SemaphoreType.DMA, pltpu.SemaphoreType.DMA, pltpu.SemaphoreType.DMA])(total)


def _pack_shards(blocks, dtype):
    rows = [blocks[n].astype(dtype).reshape(-1, D) for n, _, _ in BIG]
    rows.append(jnp.zeros((PACK_PAD - PACK_ROWS, D), dtype))
    return jnp.concatenate(rows, axis=0)


def _unpack_shards(pk):
    out, off = {}, 0
    for n, r, c in BIG:
        k = r * c // D
        out[n] = pk[off:off + k].reshape(r, c)
        off += k
    return out


def _full_weights(gathered):
    per_chip = [_unpack_shards(gathered[j]) for j in range(N_CHIPS)]
    return {n: jnp.concatenate([pc[n] for pc in per_chip], axis=1 if n in COL_SHARDED else 0) for n, _, _ in BIG}


def _split_full_grads(grads, dtype):
    chunks = []
    for j in range(N_CHIPS):
        blocks = {}
        for n, r, c in BIG:
            g = grads[n]
            blocks[n] = g[:, j * c:(j + 1) * c] if n in COL_SHARDED else g[j * r:(j + 1) * r]
        chunks.append(_pack_shards(blocks, dtype))
    return jnp.stack(chunks)


def _w_in_internal(w):
    z = lambda n: jnp.zeros((w.shape[0], n), w.dtype)
    sk0, sk1 = w[:, 1440:1504], w[:, 1504:1568]
    sv0, sv1 = w[:, 1568:1632], w[:, 1632:1696]
    return jnp.concatenate([w[:, 416:1440], w[:, 1696:3744], w[:, 0:256], sk0, sk0, sk1, sk1, sv0, sv0, sv1, sv1,
                            w[:, 256:384], z(64), w[:, 384:416], z(32)], axis=1)


def _w_in_external_grad(g):
    sk = [g[:, 3328 + 128 * j:3392 + 128 * j] + g[:, 3392 + 128 * j:3456 + 128 * j] for j in range(2)]
    sv = [g[:, 3584 + 128 * j:3648 + 128 * j] + g[:, 3648 + 128 * j:3712 + 128 * j] for j in range(2)]
    return jnp.concatenate([g[:, 3072:3328], g[:, 3840:3968], g[:, 4032:4064], g[:, 0:1024], *sk, *sv, g[:, 1024:3072]], axis=1)


def _local_step(x, p, tgt, w, small):
    T = x.shape[0]
    tm = 256
    tb = 256
    w_in_p = _w_in_internal(w["w_in"])
    wqb = jnp.pad(w["w_q_b"].reshape(Q_LORA, MLA_HEADS, 96), ((0, 0), (0, 0), (0, 32))).reshape(Q_LORA, 2048)
    wkv = w["w_kv_b"].reshape(KV_LORA, MLA_HEADS, 128)
    wkn = jnp.pad(wkv[:, :, :64], ((0, 0), (0, 0), (0, 64))).reshape(KV_LORA, 2048)
    wv = wkv[:, :, 64:].reshape(KV_LORA, 1024)
    tab_m = _rope_tables(T, "mla")
    tab_s = _rope_tables(T, "swa")
    g1, gq, gkv, sinks = small["g_mix_pre"], small["g_q_a"], small["g_kv_a"], small["sinks"]
    g2, g3, g4, g5 = small["g_mix_post"], small["g_mlp_pre"], small["g_mlp_post"], small["g_ple"]
    sink_vec = sinks.reshape(SWA_HEADS)

    z, h1 = _fwd_in(x, g1, w_in_p, tm)
    qn, kvn, qm, km, vm, kt, vt, qs, ks, vs = _fwd_qkv(z, gq, gkv, wqb, wkn, wv, tab_m, tab_s, tb)
    om, lse_m = _mla_fwd(qm, km, vt, tb)
    os_, lse_s = _swa_fwd(sink_vec, qs, ks, vs)
    y, yo, au, bu, x1 = _fwd_mix(om, os_, z, x, w["w_mla_up"], w["w_swa_up"], w["w_out"], g2, tm)
    h2, a, u = _fwd_mlp_up(x1, g3, w["w_mlp_up"], tm)
    d, x2 = _fwd_mlp_down(u, w["w_mlp_down"], x1, g4, tm)
    loss, dx2, dgt, de0, dg5 = _ple_fwd_bwd(p, x2, tgt, w["w_ple"], g5, w["w_ple_gate"], tm)

    dd, da, dg4 = _bwd_mlp_down(dx2, d, g4, w["w_mlp_down"], a, tm)
    dx1, dg3 = _bwd_mlp_up(da, w["w_mlp_up"], x1, g3, dx2, tm)
    dyo, dg2, dau, dbu, dga, dgb, dom, dos, delta_m = _bwd_mix(dx1, yo, g2, w["w_out"], z, au, bu, w["w_mla_up"],
                                                                w["w_swa_up"], om, tb)
    dqm, dkm, dvm = _mla_bwd(qm, km, kt, vm, dom, lse_m, delta_m, tb)
    dqs, dkc, dkp, dvc, dvp, dsink = _swa_bwd(sink_vec, qs, ks, vs, dos, os_, lse_s)
    dqb, dknb, dvb, dsq, drest, dgq, dgkv = _bwd_qkv(dqm, dkm, dvm, dqs, dkc, dkp, dvc, dvp, z, gq, gkv, wqb, wkn, wv,
                                                      tab_m, tab_s)
    gx, dg1 = _bwd_in(dsq, dga, dgb, drest, w_in_p, x, g1, dx1, tm)

    g_in_p = jnp.concatenate([_wgrad(h1, dsq, "wgrad_in_sq"), _wgrad(h1, dga, "wgrad_in_ga"), _wgrad(h1, dgb, "wgrad_in_gb"),
                              _wgrad(h1, drest, "wgrad_in_rest")], axis=1)
    g_qb_p = _wgrad(qn, dqb, "wgrad_q_b")
    g_kn_p = _wgrad(kvn, dknb, "wgrad_kv_b_nope")
    g_v_p = _wgrad(kvn, dvb, "wgrad_kv_b_v")
    grads = {
        "w_in": _w_in_external_grad(g_in_p),
        "w_q_b": g_qb_p.reshape(Q_LORA, MLA_HEADS, 128)[:, :, :96].reshape(Q_LORA, 1536),
        "w_kv_b": jnp.concatenate([g_kn_p.reshape(KV_LORA, MLA_HEADS, 128)[:, :, :64], g_v_p.reshape(KV_LORA, MLA_HEADS, 64)],
                                  axis=2).reshape(KV_LORA, 2048),
        "w_mla_up": _wgrad(om, dau, "wgrad_mla_up"),
        "w_swa_up": _wgrad(os_, dbu, "wgrad_swa_up"),
        "w_out": _wgrad(y, dyo, "wgrad_out"),
        "w_mlp_up": _wgrad(h2, da, "wgrad_mlp_up"),
        "w_mlp_down": _wgrad(u, dd, "wgrad_mlp_down"),
        "w_ple": _wgrad(p, de0, "wgrad_ple"),
        "w_ple_gate": _wgrad(x2, dgt, "wgrad_ple_gate"),
    }
    small_grads = {"g_mix_pre": dg1, "g_q_a": dgq, "g_kv_a": dgkv, "sinks": dsink[0:1, 0:SWA_HEADS], "g_mix_post": dg2,
                   "g_mlp_pre": dg3, "g_mlp_post": dg4, "g_ple": dg5}
    return loss, gx, grads, small_grads


def _pack_small(vals, fill):
    wide = [vals[n] for n, k in SMALL if k == D]
    narrow = [vals[n] for n, k in SMALL if k != D]
    used = sum(k for _, k in SMALL if k != D)
    last = jnp.concatenate(narrow + [jnp.full((1, D - used), fill, F32)], axis=1)
    return jnp.concatenate(wide + [last, jnp.full((2, D), fill, F32)], axis=0)


def _unpack_small(pk):
    out, row, off = {}, 0, 0
    for n, k in SMALL:
        if k == D:
            out[n] = pk[row:row + 1]
            row += 1
    for n, k in SMALL:
        if k != D:
            out[n] = pk[5:6, off:off + k]
            off += k
    return out


def kernel(x, p, g_mix_pre, w_in, g_q_a, w_q_b, g_kv_a, w_kv_b, sinks, w_mla_up, w_swa_up, w_out, g_mix_post, g_mlp_pre, w_mlp_up, w_mlp_down, g_mlp_post, w_ple, g_ple, w_ple_gate, loss_target, m_g_mix_pre, m_w_in, m_g_q_a, m_w_q_b, m_g_kv_a, m_w_kv_b, m_sinks, m_w_mla_up, m_w_swa_up, m_w_out, m_g_mix_post, m_g_mlp_pre, m_w_mlp_up, m_w_mlp_down, m_g_mlp_post, m_w_ple, m_g_ple, m_w_ple_gate, v_g_mix_pre, v_w_in, v_g_q_a, v_w_q_b, v_g_kv_a, v_w_kv_b, v_sinks, v_w_mla_up, v_w_swa_up, v_w_out, v_g_mix_post, v_g_mlp_pre, v_w_mlp_up, v_w_mlp_down, v_g_mlp_post, v_w_ple, v_g_ple, v_w_ple_gate):
    given = dict(locals())
    big_w = {n: given[n][0] for n, _, _ in BIG}
    big_m = {n: given["m_" + n][0] for n, _, _ in BIG}
    big_v = {n: given["v_" + n][0] for n, _, _ in BIG}
    small_w = {n: given[n] for n, _ in SMALL}
    small_m = {n: given["m_" + n] for n, _ in SMALL}
    small_v = {n: given["v_" + n] for n, _ in SMALL}

    gathered = _all_gather(_pack_shards(big_w, BF16))
    weights = _full_weights(gathered)
    loss_blk, gx, grads, small_grads = _local_step(x[0], p[0, 0], loss_target[0], weights, small_w)

    cidx = lax.axis_index("c").astype(jnp.int32).reshape(1)
    gpk = _split_full_grads(grads, F32)
    got = _rs_sibling(gpk)
    part = _rs_add_sibling(cidx, gpk, got)
    parts, small_parts = _rs_chips(part, _pack_small(small_grads, 0.0))
    joined = _rs_join(_rs_add_chips(parts))
    big_g = _unpack_shards(joined.reshape(PACK_PAD, D))

    loss = lax.psum(loss_blk[0, 0], ("x", "y", "c"))
    g_small_pk, d_small_pk, m_small_pk, v_small_pk = _adamw_small(
        _pack_small(small_w, 0.0), small_parts, _pack_small(small_m, 0.0), _pack_small(small_v, 1.0))
    g_small, d_small = _unpack_small(g_small_pk), _unpack_small(d_small_pk)
    m_small, v_small = _unpack_small(m_small_pk), _unpack_small(v_small_pk)

    out_g, out_d, out_m, out_v = dict(g_small), dict(d_small), dict(m_small), dict(v_small)
    for n, _, _ in BIG:
        d_, m_, v_ = _adamw(big_w[n], big_g[n], big_m[n], big_v[n], "adamw_" + n)
        out_g[n], out_d[n], out_m[n], out_v[n] = big_g[n][None], d_[None], m_[None], v_[None]
    order = ["g_mix_pre", "w_in", "g_q_a", "w_q_b", "g_kv_a", "w_kv_b", "sinks", "w_mla_up", "w_swa_up", "w_out", "g_mix_post",
             "g_mlp_pre", "w_mlp_up", "w_mlp_down", "g_mlp_post", "w_ple", "g_ple", "w_ple_gate"]
    return (loss, gx[None], *[out_g[n] for n in order], *[out_d[n] for n in order], *[out_m[n] for n in order],
            *[out_v[n] for n in order])
```

```python
import math

import jax
import jax.numpy as jnp
from jax import lax
from jax.experimental import pallas as pl
from jax.experimental.pallas import tpu as pltpu

F32 = jnp.float32
BF16 = jnp.bfloat16
SDS = jax.ShapeDtypeStruct

D = 1024
D_FF = 4096
PLE = 256
Q_LORA = 256
KV_LORA = 128
MLA_HEADS = 16
MLA_NOPE = 64
MLA_ROPE = 32
SWA_HEADS = 16
SWA_HD = 64
WINDOW = 128
ROPE_THETA = 10000.0
EPS = 1e-6
NEG = -1e30
NZ = 4096
MLA_SCALE = (MLA_NOPE + MLA_ROPE) ** -0.5
LOG2_E = math.log2(math.e)
MLA_LOG2_SCALE = MLA_SCALE * LOG2_E
SWA_SCALE = SWA_HD ** -0.5

ADAM_LR = 0.001
ADAM_B1 = 0.9
ADAM_B2 = 0.999
ADAM_EPS = 1e-08
ADAM_WD = 0.01
ADAM_STEP = 10

LANES = 128
ATT_COLS = 128
N_CHIPS = 4
N_DEV = 8
MESH = pl.DeviceIdType.MESH

NT = (((1,), (1,)), ((), ()))
TN = (((0,), (0,)), ((), ()))

BIG = (("w_in", 1024, 936), ("w_q_b", 256, 384), ("w_kv_b", 128, 512), ("w_mla_up", 256, 1024),
       ("w_swa_up", 256, 1024), ("w_out", 256, 1024), ("w_mlp_up", 1024, 1024), ("w_mlp_down", 1024, 1024),
       ("w_ple", 256, 256), ("w_ple_gate", 256, 1024))
COL_SHARDED = ("w_in", "w_q_b", "w_kv_b", "w_mlp_up", "w_ple")
PACK_ROWS = sum(r * c for _, r, c in BIG) // D
PACK_PAD = 4256
HALF = PACK_PAD // 2
SMALL = (("g_mix_pre", 1024), ("g_q_a", 256), ("g_kv_a", 128), ("sinks", 16), ("g_mix_post", 1024),
         ("g_mlp_pre", 1024), ("g_mlp_post", 1024), ("g_ple", 1024))


def _dot(a, b):
    return jnp.dot(a, b, preferred_element_type=F32)


def _dot_nt(a, b):
    return lax.dot_general(a, b, NT, preferred_element_type=F32)


def _dot_tn(a, b):
    return lax.dot_general(a, b, TN, preferred_element_type=F32)


def _pcall(body, *, name, out_shape, grid=(), in_specs=None, out_specs=None, scratch=(), sem=None, vmem_mb=48):
    params = dict(vmem_limit_bytes=vmem_mb << 20)
    if sem is not None:
        params["dimension_semantics"] = sem
    return pl.pallas_call(body, name=name, grid=grid, in_specs=in_specs, out_specs=out_specs, out_shape=out_shape,
                          scratch_shapes=list(scratch), compiler_params=pltpu.CompilerParams(**params))


def _rows(tm, n, col=0):
    return pl.BlockSpec((tm, n), lambda i: (i, col))


def _full(shape):
    return pl.BlockSpec(shape, lambda i: (0,) * len(shape))


def _rms(x, g):
    r = lax.rsqrt(jnp.mean(x * x, axis=-1, keepdims=True) + EPS)
    return x * r * g


def _rms_bwd(dy, x, g):
    r = lax.rsqrt(jnp.mean(x * x, axis=-1, keepdims=True) + EPS)
    xn = x * r
    dn = dy * g
    dx = r * (dn - xn * jnp.mean(dn * xn, axis=-1, keepdims=True))
    return dx, jnp.sum(dy * xn, axis=0, keepdims=True)


def _sigmoid(x):
    return 1.0 / (1.0 + jnp.exp(-x))


def _rope(x, c, a, b, half):
    return x * c + pltpu.roll(x, LANES - half, 1) * a + pltpu.roll(x, half, 1) * b


def _rope_tables(T, kind):
    lane = jnp.arange(LANES)
    if kind == "mla":
        half = MLA_ROPE // 2
        rel = lane - MLA_NOPE
        on = (rel >= 0) & (rel < MLA_ROPE)
        d = MLA_ROPE
    else:
        half = SWA_HD // 2
        rel = lane % SWA_HD
        on = jnp.ones((LANES,), bool)
        d = SWA_HD
    first = on & (rel < half)
    second = on & (rel >= half)
    f = jnp.where(first, rel, rel - half).astype(F32)
    inv = jnp.exp(-math.log(ROPE_THETA) * f * (2.0 / d))
    ang = jnp.arange(T, dtype=F32)[:, None] * inv[None, :]
    cos, sin = jnp.cos(ang), jnp.sin(ang)
    c = jnp.where(on[None], cos, 1.0)
    a = jnp.where(first[None], -sin, 0.0)
    b = jnp.where(second[None], sin, 0.0)
    return c, a, b


def _fwd_in(x, g1, w_in_p, tm):
    T = x.shape[0]

    def body(x_ref, g_ref, w_ref, z_ref, h_ref):
        h = _rms(x_ref[...], g_ref[...]).astype(BF16)
        h_ref[...] = h
        z_ref[...] = _dot(h, w_ref[...])

    return _pcall(body, name="fwd_in", grid=(T // tm,),
                  in_specs=[_rows(tm, D), _full((1, D)), _full((D, NZ))],
                  out_specs=[_rows(tm, NZ), _rows(tm, D)],
                  out_shape=[SDS((T, NZ), F32), SDS((T, D), BF16)], sem=("parallel",))(x, g1, w_in_p)


def _fwd_qkv(z, gq, gkv, wqb, wkn, wv, tab_m, tab_s, tm):
    T = z.shape[0]

    def body(qa_ref, sq_ref, skd_ref, svd_ref, kva_ref, kr_ref, gq_ref, gkv_ref, wqb_ref, wkn_ref, wv_ref,
             cm_ref, am_ref, bm_ref, cs_ref, as_ref, bs_ref,
             qn_ref, kvn_ref, qm_ref, km_ref, vm_ref, kt_ref, vt_ref, qs_ref, ks_ref, vs_ref):
        qn = _rms(qa_ref[...], gq_ref[...]).astype(BF16)
        qn_ref[...] = qn
        kvn = _rms(kva_ref[...], gkv_ref[...]).astype(BF16)
        kvn_ref[...] = kvn
        cm, am, bm = cm_ref[...], am_ref[...], bm_ref[...]
        cs, as_, bs = cs_ref[...], as_ref[...], bs_ref[...]
        k_rope = _rope(kr_ref[...], cm, am, bm, MLA_ROPE // 2)
        for j in range(D // LANES):
            sl = slice(LANES * j, LANES * (j + 1))
            v = _dot(kvn, wv_ref[:, sl])
            vm_ref[:, sl] = v.astype(BF16)
            vt_ref[0, sl, :] = v.T.astype(BF16)
        for h in range(MLA_HEADS):
            sl = slice(LANES * h, LANES * (h + 1))
            qh = _dot(qn, wqb_ref[:, sl])
            qm_ref[:, sl] = _rope(qh, cm, am, bm, MLA_ROPE // 2).astype(BF16)
            k = _dot(kvn, wkn_ref[:, sl]) + k_rope
            km_ref[:, sl] = k.astype(BF16)
            kt_ref[0, sl, :] = k.T.astype(BF16)
        for j in range(D // LANES):
            sl = slice(LANES * j, LANES * (j + 1))
            qs_ref[:, sl] = _rope(sq_ref[:, sl], cs, as_, bs, SWA_HD // 2).astype(BF16)
        for j in range(2):
            sl = slice(LANES * j, LANES * (j + 1))
            ks_ref[:, sl] = _rope(skd_ref[:, sl], cs, as_, bs, SWA_HD // 2).astype(BF16)
        vs_ref[...] = svd_ref[...].astype(BF16)

    tab = [_rows(tm, LANES)] * 6
    return _pcall(body, name="fwd_qkv", grid=(T // tm,),
                  in_specs=[_rows(tm, 256, 12), _rows(tm, 1024, 0), _rows(tm, 256, 13), _rows(tm, 256, 14),
                            _rows(tm, 128, 30), _rows(tm, 128, 31), _full((1, Q_LORA)), _full((1, KV_LORA)),
                            _full((Q_LORA, 2048)), _full((KV_LORA, 2048)), _full((KV_LORA, 1024))] + tab,
                  out_specs=[_rows(tm, Q_LORA), _rows(tm, KV_LORA), _rows(tm, 2048), _rows(tm, 2048), _rows(tm, 1024),
                             pl.BlockSpec((1, 2048, tm), lambda i: (i, 0, 0)), pl.BlockSpec((1, 1024, tm), lambda i: (i, 0, 0)),
                             _rows(tm, 1024), _rows(tm, 256), _rows(tm, 256)],
                  out_shape=[SDS((T, Q_LORA), BF16), SDS((T, KV_LORA), BF16), SDS((T, 2048), BF16), SDS((T, 2048), BF16),
                             SDS((T, 1024), BF16), SDS((T // tm, 2048, tm), BF16), SDS((T // tm, 1024, tm), BF16),
                             SDS((T, 1024), BF16), SDS((T, 256), BF16), SDS((T, 256), BF16)],
                  sem=("parallel",))(z, z, z, z, z, z, gq, gkv, wqb, wkn, wv, *tab_m, *tab_s)


def _mla_fwd(qm, km, vt, tb):
    T = qm.shape[0]
    nb = T // tb
    cc = ATT_COLS

    def body(q_ref, k_ref, vt_ref, o_ref, l_ref, s_ref, p_ref, m_ref, d_ref, acc_ref):
        i = pl.program_id(1)
        m_ref[...] = jnp.full(m_ref.shape, NEG, F32)
        d_ref[...] = jnp.zeros_like(d_ref)
        acc_ref[...] = jnp.zeros_like(acc_ref)
        key = lax.broadcasted_iota(jnp.int32, (tb, cc), 0)
        qry = lax.broadcasted_iota(jnp.int32, (tb, cc), 1)

        def scores(j, slot):
            off = pl.multiple_of(j * tb, tb)
            for hh in range(2):
                sl = slice(LANES * hh, LANES * (hh + 1))
                s_ref[slot, hh] = _dot_nt(k_ref[pl.ds(off, tb), sl], q_ref[:, sl])

        def softmax_pv(j, slot, diagonal):
            for hh in range(2):
                for c in range(tb // cc):
                    cols = slice(cc * c, cc * (c + 1))
                    if diagonal:
                        t = jnp.where(key <= qry + cc * c, s_ref[slot, hh, :, cols] * MLA_LOG2_SCALE, NEG)
                        top = jnp.max(t, axis=0, keepdims=True)
                    else:
                        top = jnp.max(s_ref[slot, hh, :, cols], axis=0, keepdims=True) * MLA_LOG2_SCALE
                    m_old = m_ref[hh, :, cols]
                    mn = jnp.maximum(m_old, top)
                    al = jnp.exp2(m_old - mn)
                    if diagonal:
                        p = jnp.exp2(t - mn)
                    else:
                        p = jnp.exp2(s_ref[slot, hh, :, cols] * MLA_LOG2_SCALE - mn)
                    m_ref[hh, :, cols] = mn
                    d_ref[hh, :, cols] = al * d_ref[hh, :, cols] + jnp.sum(p, axis=0, keepdims=True)
                    acc_ref[hh, :, cols] = al * acc_ref[hh, :, cols]
                    p_ref[hh, :, cols] = p.astype(BF16)
            v_t = vt_ref[j]
            for hh in range(2):
                acc_ref[hh] += _dot(v_t, p_ref[hh])

        def step(t, carry):
            scores(2 * t + 1, 1)
            softmax_pv(2 * t, 0, False)
            scores(2 * t + 2, 0)
            softmax_pv(2 * t + 1, 1, False)
            return carry

        scores(0, 0)
        lax.fori_loop(0, i // 2, step, 0)

        @pl.when(i % 2 == 1)
        def _():
            scores(i, 1)
            softmax_pv(i - 1, 0, False)
            softmax_pv(i, 1, True)

        @pl.when(i % 2 == 0)
        def _():
            softmax_pv(i, 0, True)
        first = lax.broadcasted_iota(jnp.int32, (LANES, tb), 0) < 64
        o_ref[...] = jnp.where(first, acc_ref[0] / d_ref[0], acc_ref[1] / d_ref[1]).T
        sub = lax.broadcasted_iota(jnp.int32, (8, tb), 0)
        lse = [m_ref[hh] + jnp.log(d_ref[hh]) * LOG2_E for hh in range(2)]
        l_ref[0, 0] = jnp.where(sub == 0, lse[0], jnp.where(sub == 1, lse[1], 0.0))

    return _pcall(body, name="mla_fwd", grid=(MLA_HEADS // 2, nb),
                  in_specs=[pl.BlockSpec((tb, 256), lambda p, i: (i, p)), pl.BlockSpec((T, 256), lambda p, i: (0, p)),
                            pl.BlockSpec((nb, LANES, tb), lambda p, i: (0, p, 0))],
                  out_specs=[pl.BlockSpec((tb, LANES), lambda p, i: (i, p)),
                             pl.BlockSpec((1, 1, 8, tb), lambda p, i: (p, i, 0, 0))],
                  out_shape=[SDS((T, D), F32), SDS((MLA_HEADS // 2, nb, 8, tb), F32)],
                  scratch=[pltpu.VMEM((2, 2, tb, tb), F32), pltpu.VMEM((2, tb, tb), BF16), pltpu.VMEM((2, 1, tb), F32),
                           pltpu.VMEM((2, 1, tb), F32), pltpu.VMEM((2, LANES, tb), F32)],
                  sem=("parallel", "arbitrary"))(qm, km, vt)


def _swa_mask(n):
    row = lax.broadcasted_iota(jnp.int32, (WINDOW, 2 * WINDOW), 0)
    col = lax.broadcasted_iota(jnp.int32, (WINDOW, 2 * WINDOW), 1)
    rel = row - col + WINDOW
    return (rel >= 0) & (rel < WINDOW) & ((col >= WINDOW) | (n > 0))


def _swa_specs(T):
    nb = T // WINDOW
    cur = lambda w: pl.BlockSpec((WINDOW, w), lambda n: (n, 0))
    prev = lambda w: pl.BlockSpec((WINDOW, w), lambda n: (jnp.maximum(n - 1, 0), 0))
    return nb, cur, prev


def _swa_fwd(sinks, qs, ks, vs):
    T = qs.shape[0]
    nb, cur, prev = _swa_specs(T)

    def body(sink_ref, q_ref, kc_ref, kp_ref, vc_ref, vp_ref, o_ref, l_ref):
        n = pl.program_id(0)
        mask = _swa_mask(n)
        lo = lax.broadcasted_iota(jnp.int32, (WINDOW, LANES), 1) < 64
        for g in range(2):
            gs = slice(LANES * g, LANES * (g + 1))
            kb = jnp.concatenate([kp_ref[:, gs], kc_ref[:, gs]], axis=0)
            vb = jnp.concatenate([vp_ref[:, gs], vc_ref[:, gs]], axis=0)
            for jj in range(4):
                j = 4 * g + jj
                sl = slice(LANES * j, LANES * (j + 1))
                qp = q_ref[:, sl]
                outs, lses = [], []
                for hf in range(2):
                    hm = lo if hf == 0 else jnp.logical_not(lo)
                    qh = jnp.where(hm, qp, jnp.zeros_like(qp))
                    s = jnp.where(mask, _dot_nt(qh, kb) * SWA_SCALE, NEG)
                    sk = sink_ref[2 * j + hf]
                    m = jnp.maximum(jnp.max(s, axis=1, keepdims=True), sk)
                    e = jnp.exp(s - m)
                    den = jnp.sum(e, axis=1, keepdims=True) + jnp.exp(sk - m)
                    p = e / den
                    outs.append(_dot(p.astype(BF16), vb))
                    lses.append(jnp.broadcast_to(m + jnp.log(den), (WINDOW, LANES)))
                o_ref[:, sl] = jnp.where(lo, outs[0], outs[1])
                l_ref[:, sl] = jnp.where(lo, lses[0], lses[1])

    return _pcall(body, name="swa_fwd", grid=(nb,),
                  in_specs=[pl.BlockSpec(memory_space=pltpu.SMEM), cur(D), cur(256), prev(256), cur(256), prev(256)],
                  out_specs=[cur(D), cur(D)], out_shape=[SDS((T, D), F32)] * 2,
                  sem=("parallel",))(sinks, qs, ks, ks, vs, vs)


def _fwd_mix(om, os_, z, x, wmu, wsu, wo, g2, tm):
    T = x.shape[0]

    def body(om_ref, os_ref, ga_ref, gb_ref, x_ref, wmu_ref, wsu_ref, wo_ref, g2_ref,
             y_ref, yo_ref, au_ref, bu_ref, x1_ref):
        au = _dot(om_ref[...].astype(BF16), wmu_ref[...])
        bu = _dot(os_ref[...].astype(BF16), wsu_ref[...])
        au_ref[...] = au
        bu_ref[...] = bu
        y = (_sigmoid(ga_ref[...]) * au + _sigmoid(gb_ref[...]) * bu).astype(BF16)
        y_ref[...] = y
        yo = _dot(y, wo_ref[...])
        yo_ref[...] = yo
        x1_ref[...] = x_ref[...] + _rms(yo, g2_ref[...])

    r = _rows(tm, D)
    w = _full((D, D))
    return _pcall(body, name="fwd_mix", grid=(T // tm,),
                  in_specs=[r, r, _rows(tm, D, 1), _rows(tm, D, 2), r, w, w, w, _full((1, D))],
                  out_specs=[r] * 5,
                  out_shape=[SDS((T, D), BF16), SDS((T, D), F32), SDS((T, D), F32), SDS((T, D), F32), SDS((T, D), F32)],
                  sem=("parallel",))(om, os_, z, z, x, wmu, wsu, wo, g2)


def _fwd_mlp_up(x1, g3, w1, tm):
    T = x1.shape[0]

    def body(x_ref, g_ref, w_ref, h_ref, a_ref, u_ref):
        h = _rms(x_ref[...], g_ref[...]).astype(BF16)
        h_ref[...] = h
        a = _dot(h, w_ref[...])
        a_ref[...] = a
        u_ref[...] = jnp.square(jnp.maximum(a, 0.0)).astype(BF16)

    return _pcall(body, name="fwd_mlp_up", grid=(T // tm,),
                  in_specs=[_rows(tm, D), _full((1, D)), _full((D, D_FF))],
                  out_specs=[_rows(tm, D), _rows(tm, D_FF), _rows(tm, D_FF)],
                  out_shape=[SDS((T, D), BF16), SDS((T, D_FF), F32), SDS((T, D_FF), BF16)],
                  sem=("parallel",))(x1, g3, w1)


def _fwd_mlp_down(u, w2, x1, g4, tm):
    T = x1.shape[0]

    def body(u_ref, w_ref, x_ref, g_ref, d_ref, x2_ref):
        d = _dot(u_ref[...], w_ref[...])
        d_ref[...] = d
        x2_ref[...] = x_ref[...] + _rms(d, g_ref[...])

    return _pcall(body, name="fwd_mlp_down", grid=(T // tm,),
                  in_specs=[_rows(tm, D_FF), _full((D_FF, D)), _rows(tm, D), _full((1, D))],
                  out_specs=[_rows(tm, D), _rows(tm, D)], out_shape=[SDS((T, D), F32)] * 2,
                  sem=("parallel",))(u, w2, x1, g4)


def _ple_fwd_bwd(p, x2, tgt, wple, g5, wpg, tm):
    T = x2.shape[0]

    def body(p_ref, x2_ref, t_ref, wple_ref, g5_ref, wpg_ref, loss_ref, dx2_ref, dgt_ref, de0_ref, dg5_ref):
        @pl.when(pl.program_id(0) == 0)
        def _():
            loss_ref[...] = jnp.zeros_like(loss_ref)
            dg5_ref[...] = jnp.zeros_like(dg5_ref)

        e0 = _dot(p_ref[...].astype(BF16), wple_ref[...])
        g5 = g5_ref[...]
        r = lax.rsqrt(jnp.mean(e0 * e0, axis=-1, keepdims=True) + EPS)
        en = e0 * r
        e = en * g5
        x2 = x2_ref[...]
        s = _sigmoid(_dot(x2.astype(BF16), wpg_ref[...]))
        diff = x2 + s * e - t_ref[...]
        sq = jnp.sum(jnp.sum(diff * diff, axis=1, keepdims=True), axis=0, keepdims=True)
        loss_ref[...] += jnp.broadcast_to(sq * (0.5 / D), loss_ref.shape)
        dx3 = diff * (1.0 / D)
        de = dx3 * s
        dgt = (dx3 * e * s * (1.0 - s)).astype(BF16)
        dgt_ref[...] = dgt
        dn = de * g5
        de0_ref[...] = (r * (dn - en * jnp.mean(dn * en, axis=-1, keepdims=True))).astype(BF16)
        dg5_ref[...] += jnp.sum(de * en, axis=0, keepdims=True)
        dx2_ref[...] = dx3 + _dot_nt(dgt, wpg_ref[...])

    r = _rows(tm, D)
    return _pcall(body, name="ple_fwd_bwd", grid=(T // tm,),
                  in_specs=[_rows(tm, PLE), r, r, _full((PLE, D)), _full((1, D)), _full((D, D))],
                  out_specs=[_full((8, LANES)), r, r, r, _full((1, D))],
                  out_shape=[SDS((8, LANES), F32), SDS((T, D), F32), SDS((T, D), BF16), SDS((T, D), BF16), SDS((1, D), F32)],
                  sem=("arbitrary",))(p, x2, tgt, wple, g5, wpg)


def _bwd_mlp_down(dx2, d, g4, w2, a, tm):
    T = dx2.shape[0]

    def body(dx_ref, d_ref, g_ref, w_ref, a_ref, dd_ref, da_ref, dg_ref):
        @pl.when(pl.program_id(0) == 0)
        def _():
            dg_ref[...] = jnp.zeros_like(dg_ref)

        dd, dg = _rms_bwd(dx_ref[...], d_ref[...], g_ref[...])
        dg_ref[...] += dg
        ddb = dd.astype(BF16)
        dd_ref[...] = ddb
        du = _dot_nt(ddb, w_ref[...])
        da_ref[...] = (du * (2.0 * jnp.maximum(a_ref[...], 0.0))).astype(BF16)

    return _pcall(body, name="bwd_mlp_down", grid=(T // tm,),
                  in_specs=[_rows(tm, D), _rows(tm, D), _full((1, D)), _full((D_FF, D)), _rows(tm, D_FF)],
                  out_specs=[_rows(tm, D), _rows(tm, D_FF), _full((1, D))],
                  out_shape=[SDS((T, D), BF16), SDS((T, D_FF), BF16), SDS((1, D), F32)],
                  sem=("arbitrary",))(dx2, d, g4, w2, a)


def _bwd_mlp_up(da, w1, x1, g3, dx2, tm):
    T = dx2.shape[0]

    def body(da_ref, w_ref, x_ref, g_ref, dx2_ref, dx1_ref, dg_ref):
        @pl.when(pl.program_id(0) == 0)
        def _():
            dg_ref[...] = jnp.zeros_like(dg_ref)

        dh = _dot_nt(da_ref[...], w_ref[...])
        dx, dg = _rms_bwd(dh, x_ref[...], g_ref[...])
        dg_ref[...] += dg
        dx1_ref[...] = dx2_ref[...] + dx

    return _pcall(body, name="bwd_mlp_up", grid=(T // tm,),
                  in_specs=[_rows(tm, D_FF), _full((D, D_FF)), _rows(tm, D), _full((1, D)), _rows(tm, D)],
                  out_specs=[_rows(tm, D), _full((1, D))],
                  out_shape=[SDS((T, D), F32), SDS((1, D), F32)], sem=("arbitrary",))(da, w1, x1, g3, dx2)


def _bwd_mix(dx1, yo, g2, wo, z, au, bu, wmu, wsu, om, tm):
    T = dx1.shape[0]

    def body(dx_ref, yo_ref, g_ref, wo_ref, ga_ref, gb_ref, au_ref, bu_ref, wmu_ref, wsu_ref, om_ref,
             dyo_ref, dg_ref, dau_ref, dbu_ref, dga_ref, dgb_ref, dom_ref, dos_ref, dl_ref):
        @pl.when(pl.program_id(0) == 0)
        def _():
            dg_ref[...] = jnp.zeros_like(dg_ref)

        dyo, dg = _rms_bwd(dx_ref[...], yo_ref[...], g_ref[...])
        dg_ref[...] += dg
        dyob = dyo.astype(BF16)
        dyo_ref[...] = dyob
        dy = _dot_nt(dyob, wo_ref[...])
        sa = _sigmoid(ga_ref[...])
        sb = _sigmoid(gb_ref[...])
        dau = (dy * sa).astype(BF16)
        dbu = (dy * sb).astype(BF16)
        dau_ref[...] = dau
        dbu_ref[...] = dbu
        dga_ref[...] = (dy * au_ref[...] * sa * (1.0 - sa)).astype(BF16)
        dgb_ref[...] = (dy * bu_ref[...] * sb * (1.0 - sb)).astype(BF16)
        dom = _dot_nt(dau, wmu_ref[...])
        dom_ref[...] = dom
        dos_ref[...] = _dot_nt(dbu, wsu_ref[...])
        prod = dom * om_ref[...]
        sub = lax.broadcasted_iota(jnp.int32, (8, tm), 0)
        for pr in range(MLA_HEADS // 2):
            pt = prod[:, LANES * pr:LANES * (pr + 1)].T
            d0 = jnp.sum(pt[0:64], axis=0, keepdims=True)
            d1 = jnp.sum(pt[64:128], axis=0, keepdims=True)
            dl_ref[pr, 0] = jnp.where(sub == 0, d0, jnp.where(sub == 1, d1, 0.0))

    r = _rows(tm, D)
    w = _full((D, D))
    return _pcall(body, name="bwd_mix", grid=(T // tm,),
                  in_specs=[r, r, _full((1, D)), w, _rows(tm, D, 1), _rows(tm, D, 2), r, r, w, w, r],
                  out_specs=[r, _full((1, D)), r, r, r, r, r, r, pl.BlockSpec((MLA_HEADS // 2, 1, 8, tm), lambda i: (0, i, 0, 0))],
                  out_shape=[SDS((T, D), BF16), SDS((1, D), F32), SDS((T, D), BF16), SDS((T, D), BF16), SDS((T, D), BF16),
                             SDS((T, D), BF16), SDS((T, D), F32), SDS((T, D), F32), SDS((MLA_HEADS // 2, T // tm, 8, tm), F32)],
                  sem=("arbitrary",))(dx1, yo, g2, wo, z, z, au, bu, wmu, wsu, om)


def _mla_bwd(qm, km, kt, vm, do, lse, delta, tb):
    T = qm.shape[0]
    nb = T // tb
    cc = ATT_COLS

    def body(q_ref, k_ref, kt_ref, v_ref, do_ref, l_ref, dl_ref, dqt_ref, dk_ref, dv_ref, s_ref, dp_ref, p_ref, ds_ref, dom_ref):
        j = pl.program_id(1)

        @pl.when(j == 0)
        def _():
            dqt_ref[...] = jnp.zeros_like(dqt_ref)

        dk_ref[...] = jnp.zeros_like(dk_ref)
        dv_ref[...] = jnp.zeros_like(dv_ref)
        lo = lax.broadcasted_iota(jnp.int32, (tb, LANES), 1) < 64
        key = lax.broadcasted_iota(jnp.int32, (tb, cc), 0)
        qry = lax.broadcasted_iota(jnp.int32, (tb, cc), 1)

        def scores(i, slot):
            rows_i = pl.ds(pl.multiple_of(i * tb, tb), tb)
            d_o = do_ref[rows_i, :]
            v = v_ref[...]
            for hh in range(2):
                sl = slice(LANES * hh, LANES * (hh + 1))
                hm = lo if hh == 0 else jnp.logical_not(lo)
                dom_ref[slot, hh] = jnp.where(hm, d_o, 0.0).astype(BF16)
                s_ref[slot, hh] = _dot_nt(k_ref[:, sl], q_ref[rows_i, sl])
                dp_ref[slot, hh] = _dot_nt(v, dom_ref[slot, hh])

        def grads(i, slot, diagonal):
            rows_i = pl.ds(pl.multiple_of(i * tb, tb), tb)
            lse_i = l_ref[0, i]
            delta_i = dl_ref[0, i]
            for hh in range(2):
                for c in range(tb // cc):
                    cols = slice(cc * c, cc * (c + 1))
                    p = jnp.exp2(s_ref[slot, hh, :, cols] * MLA_LOG2_SCALE - lse_i[hh:hh + 1, cols])
                    if diagonal:
                        p = jnp.where(key <= qry + cc * c, p, 0.0)
                    p_ref[hh, :, cols] = p.astype(BF16)
                    ds_ref[hh, :, cols] = (p * (dp_ref[slot, hh, :, cols] - delta_i[hh:hh + 1, cols]) * MLA_SCALE).astype(BF16)
            for hh in range(2):
                sl = slice(LANES * hh, LANES * (hh + 1))
                dv_ref[...] += _dot(p_ref[hh], dom_ref[slot, hh])
                dk_ref[:, sl] += _dot(ds_ref[hh], q_ref[rows_i, sl])
                dqt_ref[i, sl, :] += _dot(kt_ref[0, sl, :], ds_ref[hh])

        n_off = nb - 1 - j

        def step(u, carry):
            i0 = j + 1 + 2 * u
            scores(i0 + 1, 1)
            grads(i0, 0, False)
            scores(jnp.where(i0 + 2 < nb, i0 + 2, j), 0)
            grads(i0 + 1, 1, False)
            return carry

        scores(jnp.where(n_off > 0, j + 1, j), 0)
        lax.fori_loop(0, n_off // 2, step, 0)

        @pl.when(n_off % 2 == 1)
        def _():
            scores(j, 1)
            grads(nb - 1, 0, False)
            grads(j, 1, True)

        @pl.when(n_off % 2 == 0)
        def _():
            grads(j, 0, True)

    pair = lambda w: pl.BlockSpec((T, w), lambda p, j: (0, p))
    blk = lambda w: pl.BlockSpec((tb, w), lambda p, j: (j, p))
    stat = pl.BlockSpec((1, nb, 8, tb), lambda p, j: (p, 0, 0, 0))
    return _pcall(body, name="mla_bwd", grid=(MLA_HEADS // 2, nb),
                  in_specs=[pair(256), blk(256), pl.BlockSpec((1, 256, tb), lambda p, j: (j, p, 0)), blk(LANES), pair(LANES),
                            stat, stat],
                  out_specs=[pl.BlockSpec((nb, 256, tb), lambda p, j: (0, p, 0)), blk(256), blk(LANES)],
                  out_shape=[SDS((nb, 2048, tb), F32), SDS((T, 2048), F32), SDS((T, D), F32)],
                  scratch=[pltpu.VMEM((2, 2, tb, tb), F32), pltpu.VMEM((2, 2, tb, tb), F32), pltpu.VMEM((2, tb, tb), BF16),
                           pltpu.VMEM((2, tb, tb), BF16), pltpu.VMEM((2, 2, tb, LANES), BF16)],
                  sem=("parallel", "arbitrary"))(qm, km, kt, vm, do, lse, delta)


def _swa_bwd(sinks, qs, ks, vs, do, o, lse):
    T = qs.shape[0]
    nb, cur, prev = _swa_specs(T)

    def body(sink_ref, q_ref, kc_ref, kp_ref, vc_ref, vp_ref, do_ref, o_ref, l_ref,
             dq_ref, dkc_ref, dkp_ref, dvc_ref, dvp_ref, dsink_ref):
        n = pl.program_id(0)

        @pl.when(n == 0)
        def _():
            dsink_ref[...] = jnp.zeros_like(dsink_ref)

        mask = _swa_mask(n)
        lo = lax.broadcasted_iota(jnp.int32, (WINDOW, LANES), 1) < 64
        lane8 = lax.broadcasted_iota(jnp.int32, (8, LANES), 1)
        dsink = jnp.zeros((8, LANES), F32)
        for g in range(2):
            gs = slice(LANES * g, LANES * (g + 1))
            kb = jnp.concatenate([kp_ref[:, gs], kc_ref[:, gs]], axis=0)
            vb = jnp.concatenate([vp_ref[:, gs], vc_ref[:, gs]], axis=0)
            dkb = jnp.zeros((2 * WINDOW, LANES), F32)
            dvb = jnp.zeros((2 * WINDOW, LANES), F32)
            for jj in range(4):
                j = 4 * g + jj
                sl = slice(LANES * j, LANES * (j + 1))
                qp = q_ref[:, sl]
                d_o = do_ref[:, sl]
                prod = d_o * o_ref[:, sl]
                lse_b = l_ref[:, sl]
                dqs = []
                for hf in range(2):
                    hm = lo if hf == 0 else jnp.logical_not(lo)
                    qh = jnp.where(hm, qp, jnp.zeros_like(qp))
                    s = jnp.where(mask, _dot_nt(qh, kb) * SWA_SCALE, NEG)
                    lse_h = jnp.max(jnp.where(hm, lse_b, -jnp.inf), axis=1, keepdims=True)
                    p = jnp.exp(s - lse_h)
                    dom = jnp.where(hm, d_o, 0.0).astype(BF16)
                    dp = _dot_nt(dom, vb)
                    delta = jnp.sum(jnp.where(hm, prod, 0.0), axis=1, keepdims=True)
                    ds = (p * (dp - delta) * SWA_SCALE).astype(BF16)
                    p_sink = jnp.exp(sink_ref[2 * j + hf] - lse_h)
                    d_sink = -jnp.sum(p_sink * delta, axis=0, keepdims=True)
                    dsink = dsink + jnp.where(lane8 == 2 * j + hf, d_sink, 0.0)
                    dvb = dvb + _dot_tn(p.astype(BF16), dom)
                    dkb = dkb + _dot_tn(ds, qh)
                    dqs.append(_dot(ds, kb))
                dq_ref[:, sl] = jnp.where(lo, dqs[0], dqs[1])
            dkp_ref[:, gs] = dkb[:WINDOW]
            dkc_ref[:, gs] = dkb[WINDOW:]
            dvp_ref[:, gs] = dvb[:WINDOW]
            dvc_ref[:, gs] = dvb[WINDOW:]
        dsink_ref[...] += dsink

    return _pcall(body, name="swa_bwd", grid=(nb,),
                  in_specs=[pl.BlockSpec(memory_space=pltpu.SMEM), cur(D), cur(256), prev(256), cur(256), prev(256),
                            cur(D), cur(D), cur(D)],
                  out_specs=[cur(D), cur(256), cur(256), cur(256), cur(256), _full((8, LANES))],
                  out_shape=[SDS((T, D), F32), SDS((T, 256), F32), SDS((T, 256), F32), SDS((T, 256), F32), SDS((T, 256), F32),
                             SDS((8, LANES), F32)],
                  sem=("arbitrary",))(sinks, qs, ks, ks, vs, vs, do, o, lse)


def _bwd_qkv(dqm, dkm, dvm, dqs, dkc, dkp, dvc, dvp, z, gq, gkv, wqb, wkn, wv, tab_m, tab_s):
    T = z.shape[0]
    tm = WINDOW
    nb = T // tm
    per = dqm.shape[2] // tm

    def body(dqm_ref, dkm_ref, dvm_ref, dqs_ref, dkc_ref, dkp_ref, dvc_ref, dvp_ref, qa_ref, kva_ref, gq_ref, gkv_ref,
             wqb_ref, wkn_ref, wv_ref, cm_ref, am_ref, bm_ref, cs_ref, as_ref, bs_ref,
             dq_out, dkn_out, dv_out, dsq_ref, drest_ref, dgq_ref, dgkv_ref):
        i = pl.program_id(0)

        @pl.when(i == 0)
        def _():
            dgq_ref[...] = jnp.zeros_like(dgq_ref)
            dgkv_ref[...] = jnp.zeros_like(dgkv_ref)

        cm, am, bm = cm_ref[...], -am_ref[...], -bm_ref[...]
        cs, as_, bs = cs_ref[...], -as_ref[...], -bs_ref[...]
        lane = lax.broadcasted_iota(jnp.int32, (tm, LANES), 1)
        nope = lane < MLA_NOPE
        roped = jnp.logical_and(lane >= MLA_NOPE, lane < MLA_NOPE + MLA_ROPE)
        dkr = jnp.zeros((tm, LANES), F32)
        dqn = jnp.zeros((tm, Q_LORA), F32)
        dkvn = jnp.zeros((tm, KV_LORA), F32)
        for h in range(MLA_HEADS):
            sl = slice(LANES * h, LANES * (h + 1))
            dq_h = _rope(dqm_ref[0, sl, :].T, cm, am, bm, MLA_ROPE // 2).astype(BF16)
            dq_out[:, sl] = dq_h
            dqn = dqn + _dot_nt(dq_h, wqb_ref[:, sl])
            dk_h = dkm_ref[:, sl]
            dkn_h = jnp.where(nope, dk_h, 0.0).astype(BF16)
            dkn_out[:, sl] = dkn_h
            dkvn = dkvn + _dot_nt(dkn_h, wkn_ref[:, sl])
            dkr = dkr + jnp.where(roped, dk_h, 0.0)
        dvb = dvm_ref[...].astype(BF16)
        dv_out[...] = dvb
        dkvn = dkvn + _dot_nt(dvb, wv_ref[...])
        dqa, dgq = _rms_bwd(dqn, qa_ref[...], gq_ref[...])
        dkva, dgkv = _rms_bwd(dkvn, kva_ref[...], gkv_ref[...])
        dgq_ref[...] += dgq
        dgkv_ref[...] += dgkv
        for j in range(D // LANES):
            sl = slice(LANES * j, LANES * (j + 1))
            dsq_ref[:, sl] = _rope(dqs_ref[:, sl], cs, as_, bs, SWA_HD // 2).astype(BF16)
        keep = (i < nb - 1).astype(F32)
        drest_ref[:, 0:256] = dqa.astype(BF16)
        for j in range(2):
            sl = slice(LANES * j, LANES * (j + 1))
            dk = dkc_ref[:, sl] + keep * dkp_ref[:, sl]
            drest_ref[:, 256 + LANES * j:256 + LANES * (j + 1)] = _rope(dk, cs, as_, bs, SWA_HD // 2).astype(BF16)
        drest_ref[:, 512:768] = (dvc_ref[...] + keep * dvp_ref[...]).astype(BF16)
        drest_ref[:, 768:896] = dkva.astype(BF16)
        drest_ref[:, 896:1024] = _rope(dkr, cm, am, bm, MLA_ROPE // 2).astype(BF16)

    nxt = pl.BlockSpec((tm, 256), lambda i: (jnp.minimum(i + 1, nb - 1), 0))
    tab = [_rows(tm, LANES)] * 6
    return _pcall(body, name="bwd_qkv", grid=(nb,),
                  in_specs=[pl.BlockSpec((1, 2048, tm), lambda i: (i // per, 0, i % per)),
                            _rows(tm, 2048), _rows(tm, 1024), _rows(tm, 1024), _rows(tm, 256), nxt,
                            _rows(tm, 256), nxt, _rows(tm, 256, 12), _rows(tm, 128, 30), _full((1, Q_LORA)), _full((1, KV_LORA)),
                            _full((Q_LORA, 2048)), _full((KV_LORA, 2048)), _full((KV_LORA, 1024))] + tab,
                  out_specs=[_rows(tm, 2048), _rows(tm, 2048), _rows(tm, 1024), _rows(tm, 1024), _rows(tm, 1024),
                             _full((1, Q_LORA)), _full((1, KV_LORA))],
                  out_shape=[SDS((T, 2048), BF16), SDS((T, 2048), BF16), SDS((T, 1024), BF16), SDS((T, 1024), BF16),
                             SDS((T, 1024), BF16), SDS((1, Q_LORA), F32), SDS((1, KV_LORA), F32)],
                  sem=("arbitrary",))(dqm, dkm, dvm, dqs, dkc, dkp, dvc, dvp, z, z, gq, gkv, wqb, wkn, wv, *tab_m, *tab_s)


def _bwd_in(dsq, dga, dgb, drest, w_in_p, x, g1, dx1, tm):
    T = x.shape[0]

    def body(a_ref, b_ref, c_ref, d_ref, w_ref, x_ref, g_ref, dx1_ref, dx_ref, dg_ref):
        @pl.when(pl.program_id(0) == 0)
        def _():
            dg_ref[...] = jnp.zeros_like(dg_ref)

        dh = (_dot_nt(a_ref[...], w_ref[:, 0:1024]) + _dot_nt(b_ref[...], w_ref[:, 1024:2048])
              + _dot_nt(c_ref[...], w_ref[:, 2048:3072]) + _dot_nt(d_ref[...], w_ref[:, 3072:4096]))
        dx, dg = _rms_bwd(dh, x_ref[...], g_ref[...])
        dg_ref[...] += dg
        dx_ref[...] = dx1_ref[...] + dx

    r = _rows(tm, D)
    return _pcall(body, name="bwd_in", grid=(T // tm,),
                  in_specs=[r, r, r, r, _full((D, NZ)), r, _full((1, D)), r],
                  out_specs=[r, _full((1, D))], out_shape=[SDS((T, D), F32), SDS((1, D), F32)],
                  sem=("arbitrary",))(dsq, dga, dgb, drest, w_in_p, x, g1, dx1)


def _wgrad(a, g, name):
    T, K = a.shape
    N = g.shape[1]
    tk, tn, tt = min(K, 512), min(N, 1024), min(T, 512)
    assert K % tk == 0 and N % tn == 0 and T % tt == 0, (a.shape, g.shape)

    def body(a_ref, g_ref, o_ref):
        @pl.when(pl.program_id(2) == 0)
        def _():
            o_ref[...] = jnp.zeros_like(o_ref)

        o_ref[...] += _dot_tn(a_ref[...].astype(BF16), g_ref[...].astype(BF16))

    return _pcall(body, name=name, grid=(K // tk, N // tn, T // tt),
                  in_specs=[pl.BlockSpec((tt, tk), lambda k, n, t: (t, k)), pl.BlockSpec((tt, tn), lambda k, n, t: (t, n))],
                  out_specs=pl.BlockSpec((tk, tn), lambda k, n, t: (k, n)), out_shape=SDS((K, N), F32),
                  sem=("parallel", "parallel", "arbitrary"))(a, g)


def _adamw(w, g, m, v, name):
    R, C = w.shape
    tr = min(R, 256)

    def body(w_ref, g_ref, m_ref, v_ref, d_ref, m2_ref, v2_ref):
        g_ = g_ref[...]
        m2 = ADAM_B1 * m_ref[...] + (1.0 - ADAM_B1) * g_
        v2 = ADAM_B2 * v_ref[...] + (1.0 - ADAM_B2) * jnp.square(g_)
        m_hat = m2 / (1.0 - ADAM_B1 ** ADAM_STEP)
        v_hat = v2 / (1.0 - ADAM_B2 ** ADAM_STEP)
        d_ref[...] = -ADAM_LR * (m_hat / (jnp.sqrt(v_hat) + ADAM_EPS) + ADAM_WD * w_ref[...])
        m2_ref[...] = m2
        v2_ref[...] = v2

    r = _rows(tr, C)
    return _pcall(body, name=name, grid=(R // tr,), in_specs=[r] * 4, out_specs=[r] * 3,
                  out_shape=[SDS((R, C), F32)] * 3, sem=("parallel",))(w, g, m, v)


def _adamw_small(w, parts, m, v):
    def body(w_ref, p_ref, m_ref, v_ref, g_ref, d_ref, m2_ref, v2_ref):
        g_ = p_ref[0]
        for k in range(1, N_DEV):
            g_ = g_ + p_ref[k]
        g_ref[...] = g_
        m2 = ADAM_B1 * m_ref[...] + (1.0 - ADAM_B1) * g_
        v2 = ADAM_B2 * v_ref[...] + (1.0 - ADAM_B2) * jnp.square(g_)
        m_hat = m2 / (1.0 - ADAM_B1 ** ADAM_STEP)
        v_hat = v2 / (1.0 - ADAM_B2 ** ADAM_STEP)
        d_ref[...] = -ADAM_LR * (m_hat / (jnp.sqrt(v_hat) + ADAM_EPS) + ADAM_WD * w_ref[...])
        m2_ref[...] = m2
        v2_ref[...] = v2

    s = _full((8, D))
    return _pcall(body, name="adamw_small", grid=(1,), in_specs=[s, _full((N_DEV, 8, D)), s, s], out_specs=[s] * 4,
                  out_shape=[SDS((8, D), F32)] * 4, sem=("arbitrary",))(w, parts, m, v)


ANY = pl.BlockSpec(memory_space=pl.ANY)


def _place():
    x, y, c = lax.axis_index("x"), lax.axis_index("y"), lax.axis_index("c")
    chips = [(1 - x, y), (x, 1 - y), (1 - x, 1 - y)]
    return x, y, c, chips


def _all_gather(wpk):
    def body(in_ref, out_ref, send_sems, recv_sems):
        x, y, c, chips = _place()
        half = pl.ds(pl.multiple_of(c * HALF, 16), HALF)
        other = pl.ds(pl.multiple_of((1 - c) * HALF, 16), HALF)

        def copy(k, src, dst, to):
            return pltpu.make_async_remote_copy(src_ref=src, dst_ref=dst, send_sem=send_sems.at[k], recv_sem=recv_sems.at[k],
                                                device_id=to, device_id_type=MESH)

        first = [copy(k, in_ref.at[half], out_ref.at[2 * x + y, half], (cx, cy, c)) for k, (cx, cy) in enumerate(chips)]
        for cp in first:
            cp.start()
        passed = []
        for k, (cx, cy) in enumerate(chips):
            slot = out_ref.at[2 * cx + cy, half]
            copy(k, slot, slot, (x, y, c)).wait_recv()
            fwd = copy(3 + k, slot, slot, (x, y, 1 - c))
            fwd.start()
            passed.append(fwd)
        for k, (cx, cy) in enumerate(chips):
            slot = out_ref.at[2 * cx + cy, other]
            copy(3 + k, slot, slot, (x, y, c)).wait_recv()
        for cp in first + passed:
            cp.wait_send()

    return _pcall(body, name="all_gather_weights", in_specs=[ANY], out_specs=ANY,
                  out_shape=SDS((N_CHIPS, PACK_PAD, D), BF16),
                  scratch=[pltpu.SemaphoreType.DMA((6,)), pltpu.SemaphoreType.DMA((6,))])(wpk)


def _rs_sibling(gpk):
    def body(in_ref, out_ref, send_sem, recv_sem):
        x, y, c, _ = _place()
        theirs = pl.ds(pl.multiple_of((1 - c) * HALF, 8), HALF)
        cp = pltpu.make_async_remote_copy(src_ref=in_ref.at[:, theirs], dst_ref=out_ref, send_sem=send_sem, recv_sem=recv_sem,
                                          device_id=(x, y, 1 - c), device_id_type=MESH)
        cp.start()
        cp.wait()

    return _pcall(body, name="rs_sibling", in_specs=[ANY], out_specs=ANY, out_shape=SDS((N_CHIPS, HALF, D), F32),
                  scratch=[pltpu.SemaphoreType.DMA, pltpu.SemaphoreType.DMA])(gpk)


def _rs_add_sibling(cidx, gpk, got):
    th = HALF // 7
    nh = HALF // th

    def body(c_ref, a_ref, b_ref, o_ref):
        o_ref[...] = (a_ref[...] + b_ref[...]).astype(BF16)

    gs = pltpu.PrefetchScalarGridSpec(
        num_scalar_prefetch=1, grid=(N_CHIPS, nh),
        in_specs=[pl.BlockSpec((1, th, D), lambda j, i, c: (j, c[0] * nh + i, 0)), pl.BlockSpec((1, th, D), lambda j, i, c: (j, i, 0))],
        out_specs=pl.BlockSpec((1, th, D), lambda j, i, c: (j, i, 0)))
    return pl.pallas_call(body, name="rs_add_sibling", grid_spec=gs, out_shape=SDS((N_CHIPS, HALF, D), BF16),
                          compiler_params=pltpu.CompilerParams(dimension_semantics=("parallel", "parallel"),
                                                               vmem_limit_bytes=48 << 20))(cidx, gpk, got)


def _rs_chips(part, small):
    def body(p_ref, s_ref, o_ref, so_ref, send_sems, recv_sems, ssend_sems, srecv_sems, local_sem):
        x, y, c, chips = _place()
        me = 2 * x + y
        mine_s = pltpu.make_async_copy(s_ref, so_ref.at[4 * x + 2 * y + c], local_sem)
        mine_s.start()
        sends = []
        for k, (cx, cy) in enumerate(chips):
            sends.append(pltpu.make_async_remote_copy(src_ref=p_ref.at[2 * cx + cy], dst_ref=o_ref.at[me], send_sem=send_sems.at[k],
                                                      recv_sem=recv_sems.at[k], device_id=(cx, cy, c), device_id_type=MESH))
        peers = [(x, y, 1 - c)] + [(cx, cy, c) for cx, cy in chips] + [(cx, cy, 1 - c) for cx, cy in chips]
        for k, to in enumerate(peers):
            sends.append(pltpu.make_async_remote_copy(src_ref=s_ref, dst_ref=so_ref.at[4 * x + 2 * y + c], send_sem=ssend_sems.at[k],
                                                      recv_sem=srecv_sems.at[k], device_id=to, device_id_type=MESH))
        for cp in sends:
            cp.start()
        for k, (cx, cy) in enumerate(chips):
            slot = o_ref.at[2 * cx + cy]
            pltpu.make_async_remote_copy(src_ref=slot, dst_ref=slot, send_sem=send_sems.at[k], recv_sem=recv_sems.at[k],
                                         device_id=(x, y, c), device_id_type=MESH).wait_recv()
        for k, (px, py, pc) in enumerate(peers):
            slot = so_ref.at[4 * px + 2 * py + pc]
            pltpu.make_async_remote_copy(src_ref=slot, dst_ref=slot, send_sem=ssend_sems.at[k], recv_sem=srecv_sems.at[k],
                                         device_id=(x, y, c), device_id_type=MESH).wait_recv()
        for cp in sends:
            cp.wait_send()
        mine_s.wait()

    return _pcall(body, name="rs_chips", in_specs=[ANY, ANY], out_specs=[ANY, ANY],
                  out_shape=[SDS((N_CHIPS, HALF, D), part.dtype), SDS((N_DEV, 8, D), F32)],
                  scratch=[pltpu.SemaphoreType.DMA((3,)), pltpu.SemaphoreType.DMA((3,)), pltpu.SemaphoreType.DMA((7,)),
                           pltpu.SemaphoreType.DMA((7,)), pltpu.SemaphoreType.DMA])(part, small)


def _rs_add_chips(qidx, part, parts):
    th = HALF // 7

    def body(q_ref, own_ref, p_ref, o_ref):
        for me in range(N_CHIPS):
            @pl.when(q_ref[0] == me)
            def _(me=me):
                t = [(own_ref[0] if j == me else p_ref[j]).astype(F32) for j in range(N_CHIPS)]
                o_ref[...] = ((t[0] + t[1]) + t[2]) + t[3]

    gs = pltpu.PrefetchScalarGridSpec(
        num_scalar_prefetch=1, grid=(HALF // th,),
        in_specs=[pl.BlockSpec((1, th, D), lambda i, q: (q[0], i, 0)), pl.BlockSpec((N_CHIPS, th, D), lambda i, q: (0, i, 0))],
        out_specs=pl.BlockSpec((th, D), lambda i, q: (i, 0)))
    return pl.pallas_call(body, name="rs_add_chips", grid_spec=gs, out_shape=SDS((HALF, D), F32),
                          compiler_params=pltpu.CompilerParams(dimension_semantics=("parallel",),
                                                               vmem_limit_bytes=48 << 20))(qidx, part, parts)


def _rs_join(total):
    def body(in_ref, out_ref, send_sem, recv_sem):
        x, y, c, _ = _place()
        cp = pltpu.make_async_remote_copy(src_ref=in_ref, dst_ref=out_ref, send_sem=send_sem, recv_sem=recv_sem,
                                          device_id=(x, y, 1 - c), device_id_type=MESH)
        cp.start()
        cp.wait()

    return _pcall(body, name="rs_join", in_specs=[ANY], out_specs=ANY, out_shape=SDS((HALF, D), F32),
                  scratch=[pltpu.SemaphoreType.DMA, pltpu.SemaphoreType.DMA])(total)


def _pack_shards(blocks, dtype):
    rows = [blocks[n].astype(dtype).reshape(-1, D) for n, _, _ in BIG]
    rows.append(jnp.zeros((PACK_PAD - PACK_ROWS, D), dtype))
    return jnp.concatenate(rows, axis=0)


def _unpack_shards(pk):
    out, off = {}, 0
    for n, r, c in BIG:
        k = r * c // D
        out[n] = pk[off:off + k].reshape(r, c)
        off += k
    return out


def _full_weights(gathered, own, chip):
    per_chip = [_unpack_shards(jnp.where(chip == j, own, gathered[j])) for j in range(N_CHIPS)]
    return {n: jnp.concatenate([pc[n] for pc in per_chip], axis=1 if n in COL_SHARDED else 0) for n, _, _ in BIG}


def _split_full_grads(grads, dtype):
    chunks = []
    for j in range(N_CHIPS):
        blocks = {}
        for n, r, c in BIG:
            g = grads[n]
            blocks[n] = g[:, j * c:(j + 1) * c] if n in COL_SHARDED else g[j * r:(j + 1) * r]
        chunks.append(_pack_shards(blocks, dtype))
    return jnp.stack(chunks)


def _w_in_internal(w):
    z = lambda n: jnp.zeros((w.shape[0], n), w.dtype)
    sk0, sk1 = w[:, 1440:1504], w[:, 1504:1568]
    sv0, sv1 = w[:, 1568:1632], w[:, 1632:1696]
    return jnp.concatenate([w[:, 416:1440], w[:, 1696:3744], w[:, 0:256], sk0, sk0, sk1, sk1, sv0, sv0, sv1, sv1,
                            w[:, 256:384], z(64), w[:, 384:416], z(32)], axis=1)


def _w_in_external_grad(g):
    sk = [g[:, 3328 + 128 * j:3392 + 128 * j] + g[:, 3392 + 128 * j:3456 + 128 * j] for j in range(2)]
    sv = [g[:, 3584 + 128 * j:3648 + 128 * j] + g[:, 3648 + 128 * j:3712 + 128 * j] for j in range(2)]
    return jnp.concatenate([g[:, 3072:3328], g[:, 3840:3968], g[:, 4032:4064], g[:, 0:1024], *sk, *sv, g[:, 1024:3072]], axis=1)


def _local_step(x, p, tgt, w, small):
    T = x.shape[0]
    tm = 256
    tb = 256
    w_in_p = _w_in_internal(w["w_in"])
    wqb = jnp.pad(w["w_q_b"].reshape(Q_LORA, MLA_HEADS, 96), ((0, 0), (0, 0), (0, 32))).reshape(Q_LORA, 2048)
    wkv = w["w_kv_b"].reshape(KV_LORA, MLA_HEADS, 128)
    wkn = jnp.pad(wkv[:, :, :64], ((0, 0), (0, 0), (0, 64))).reshape(KV_LORA, 2048)
    wv = wkv[:, :, 64:].reshape(KV_LORA, 1024)
    tab_m = _rope_tables(T, "mla")
    tab_s = _rope_tables(T, "swa")
    g1, gq, gkv, sinks = small["g_mix_pre"], small["g_q_a"], small["g_kv_a"], small["sinks"]
    g2, g3, g4, g5 = small["g_mix_post"], small["g_mlp_pre"], small["g_mlp_post"], small["g_ple"]
    sink_vec = sinks.reshape(SWA_HEADS)

    z, h1 = _fwd_in(x, g1, w_in_p, tm)
    qn, kvn, qm, km, vm, kt, vt, qs, ks, vs = _fwd_qkv(z, gq, gkv, wqb, wkn, wv, tab_m, tab_s, tb)
    om, lse_m = _mla_fwd(qm, km, vt, tb)
    os_, lse_s = _swa_fwd(sink_vec, qs, ks, vs)
    y, yo, au, bu, x1 = _fwd_mix(om, os_, z, x, w["w_mla_up"], w["w_swa_up"], w["w_out"], g2, tm)
    h2, a, u = _fwd_mlp_up(x1, g3, w["w_mlp_up"], tm)
    d, x2 = _fwd_mlp_down(u, w["w_mlp_down"], x1, g4, tm)
    loss, dx2, dgt, de0, dg5 = _ple_fwd_bwd(p, x2, tgt, w["w_ple"], g5, w["w_ple_gate"], tm)

    dd, da, dg4 = _bwd_mlp_down(dx2, d, g4, w["w_mlp_down"], a, tm)
    dx1, dg3 = _bwd_mlp_up(da, w["w_mlp_up"], x1, g3, dx2, tm)
    dyo, dg2, dau, dbu, dga, dgb, dom, dos, delta_m = _bwd_mix(dx1, yo, g2, w["w_out"], z, au, bu, w["w_mla_up"],
                                                                w["w_swa_up"], om, tb)
    dqm, dkm, dvm = _mla_bwd(qm, km, kt, vm, dom, lse_m, delta_m, tb)
    dqs, dkc, dkp, dvc, dvp, dsink = _swa_bwd(sink_vec, qs, ks, vs, dos, os_, lse_s)
    dqb, dknb, dvb, dsq, drest, dgq, dgkv = _bwd_qkv(dqm, dkm, dvm, dqs, dkc, dkp, dvc, dvp, z, gq, gkv, wqb, wkn, wv,
                                                      tab_m, tab_s)
    gx, dg1 = _bwd_in(dsq, dga, dgb, drest, w_in_p, x, g1, dx1, tm)

    g_in_p = jnp.concatenate([_wgrad(h1, dsq, "wgrad_in_sq"), _wgrad(h1, dga, "wgrad_in_ga"), _wgrad(h1, dgb, "wgrad_in_gb"),
                              _wgrad(h1, drest, "wgrad_in_rest")], axis=1)
    g_qb_p = _wgrad(qn, dqb, "wgrad_q_b")
    g_kn_p = _wgrad(kvn, dknb, "wgrad_kv_b_nope")
    g_v_p = _wgrad(kvn, dvb, "wgrad_kv_b_v")
    grads = {
        "w_in": _w_in_external_grad(g_in_p),
        "w_q_b": g_qb_p.reshape(Q_LORA, MLA_HEADS, 128)[:, :, :96].reshape(Q_LORA, 1536),
        "w_kv_b": jnp.concatenate([g_kn_p.reshape(KV_LORA, MLA_HEADS, 128)[:, :, :64], g_v_p.reshape(KV_LORA, MLA_HEADS, 64)],
                                  axis=2).reshape(KV_LORA, 2048),
        "w_mla_up": _wgrad(om, dau, "wgrad_mla_up"),
        "w_swa_up": _wgrad(os_, dbu, "wgrad_swa_up"),
        "w_out": _wgrad(y, dyo, "wgrad_out"),
        "w_mlp_up": _wgrad(h2, da, "wgrad_mlp_up"),
        "w_mlp_down": _wgrad(u, dd, "wgrad_mlp_down"),
        "w_ple": _wgrad(p, de0, "wgrad_ple"),
        "w_ple_gate": _wgrad(x2, dgt, "wgrad_ple_gate"),
    }
    small_grads = {"g_mix_pre": dg1, "g_q_a": dgq, "g_kv_a": dgkv, "sinks": dsink[0:1, 0:SWA_HEADS], "g_mix_post": dg2,
                   "g_mlp_pre": dg3, "g_mlp_post": dg4, "g_ple": dg5}
    return loss, gx, grads, small_grads


def _pack_small(vals, fill):
    wide = [vals[n] for n, k in SMALL if k == D]
    narrow = [vals[n] for n, k in SMALL if k != D]
    used = sum(k for _, k in SMALL if k != D)
    last = jnp.concatenate(narrow + [jnp.full((1, D - used), fill, F32)], axis=1)
    return jnp.concatenate(wide + [last, jnp.full((2, D), fill, F32)], axis=0)


def _unpack_small(pk):
    out, row, off = {}, 0, 0
    for n, k in SMALL:
        if k == D:
            out[n] = pk[row:row + 1]
            row += 1
    for n, k in SMALL:
        if k != D:
            out[n] = pk[5:6, off:off + k]
            off += k
    return out


def kernel(x, p, g_mix_pre, w_in, g_q_a, w_q_b, g_kv_a, w_kv_b, sinks, w_mla_up, w_swa_up, w_out, g_mix_post, g_mlp_pre, w_mlp_up, w_mlp_down, g_mlp_post, w_ple, g_ple, w_ple_gate, loss_target, m_g_mix_pre, m_w_in, m_g_q_a, m_w_q_b, m_g_kv_a, m_w_kv_b, m_sinks, m_w_mla_up, m_w_swa_up, m_w_out, m_g_mix_post, m_g_mlp_pre, m_w_mlp_up, m_w_mlp_down, m_g_mlp_post, m_w_ple, m_g_ple, m_w_ple_gate, v_g_mix_pre, v_w_in, v_g_q_a, v_w_q_b, v_g_kv_a, v_w_kv_b, v_sinks, v_w_mla_up, v_w_swa_up, v_w_out, v_g_mix_post, v_g_mlp_pre, v_w_mlp_up, v_w_mlp_down, v_g_mlp_post, v_w_ple, v_g_ple, v_w_ple_gate):
    given = dict(locals())
    big_w = {n: given[n][0] for n, _, _ in BIG}
    big_m = {n: given["m_" + n][0] for n, _, _ in BIG}
    big_v = {n: given["v_" + n][0] for n, _, _ in BIG}
    small_w = {n: given[n] for n, _ in SMALL}
    small_m = {n: given["m_" + n] for n, _ in SMALL}
    small_v = {n: given["v_" + n] for n, _ in SMALL}

    core = lax.axis_index("c")
    chip = 2 * lax.axis_index("x") + lax.axis_index("y")
    own_pk = _pack_shards(big_w, BF16)
    weights = _full_weights(_all_gather(own_pk), own_pk, chip)
    loss_blk, gx, grads, small_grads = _local_step(x[0], p[0, 0], loss_target[0], weights, small_w)

    gpk = _split_full_grads(grads, F32)
    got = _rs_sibling(gpk)
    part = _rs_add_sibling(core.astype(jnp.int32).reshape(1), gpk, got)
    parts, small_parts = _rs_chips(part, _pack_small(small_grads, 0.0))
    mine = _rs_add_chips(chip.astype(jnp.int32).reshape(1), part, parts)
    theirs = _rs_join(mine)
    joined = jnp.where(core == 0, jnp.concatenate([mine, theirs]), jnp.concatenate([theirs, mine]))
    big_g = _unpack_shards(joined)

    loss = lax.psum(loss_blk[0, 0], ("x", "y", "c"))
    g_small_pk, d_small_pk, m_small_pk, v_small_pk = _adamw_small(
        _pack_small(small_w, 0.0), small_parts, _pack_small(small_m, 0.0), _pack_small(small_v, 1.0))
    g_small, d_small = _unpack_small(g_small_pk), _unpack_small(d_small_pk)
    m_small, v_small = _unpack_small(m_small_pk), _unpack_small(v_small_pk)

    out_g, out_d, out_m, out_v = dict(g_small), dict(d_small), dict(m_small), dict(v_small)
    for n, _, _ in BIG:
        d_, m_, v_ = _adamw(big_w[n], big_g[n], big_m[n], big_v[n], "adamw_" + n)
        out_g[n], out_d[n], out_m[n], out_v[n] = big_g[n][None], d_[None], m_[None], v_[None]
    order = ["g_mix_pre", "w_in", "g_q_a", "w_q_b", "g_kv_a", "w_kv_b", "sinks", "w_mla_up", "w_swa_up", "w_out", "g_mix_post",
             "g_mlp_pre", "w_mlp_up", "w_mlp_down", "g_mlp_post", "w_ple", "g_ple", "w_ple_gate"]
    return (loss, gx[None], *[out_g[n] for n in order], *[out_d[n] for n in order], *[out_m[n] for n in order],
            *[out_v[n] for n in order])
```

```python
import math

import jax
import jax.numpy as jnp
from jax import lax
from jax.experimental import pallas as pl
from jax.experimental.pallas import tpu as pltpu

F32 = jnp.float32
BF16 = jnp.bfloat16
SDS = jax.ShapeDtypeStruct

D = 1024
D_FF = 4096
PLE = 256
Q_LORA = 256
KV_LORA = 128
MLA_HEADS = 16
MLA_NOPE = 64
MLA_ROPE = 32
SWA_HEADS = 16
SWA_HD = 64
WINDOW = 128
ROPE_THETA = 10000.0
EPS = 1e-6
NEG = -1e30
NZ = 4096
MLA_SCALE = (MLA_NOPE + MLA_ROPE) ** -0.5
LOG2_E = math.log2(math.e)
MLA_LOG2_SCALE = MLA_SCALE * LOG2_E
SWA_SCALE = SWA_HD ** -0.5

ADAM_LR = 0.001
ADAM_B1 = 0.9
ADAM_B2 = 0.999
ADAM_EPS = 1e-08
ADAM_WD = 0.01
ADAM_STEP = 10

LANES = 128
ATT_COLS = 128
N_CHIPS = 4
N_DEV = 8
MESH = pl.DeviceIdType.MESH

NT = (((1,), (1,)), ((), ()))
TN = (((0,), (0,)), ((), ()))

BIG = (("w_in", 1024, 936), ("w_q_b", 256, 384), ("w_kv_b", 128, 512), ("w_mla_up", 256, 1024),
       ("w_swa_up", 256, 1024), ("w_out", 256, 1024), ("w_mlp_up", 1024, 1024), ("w_mlp_down", 1024, 1024),
       ("w_ple", 256, 256), ("w_ple_gate", 256, 1024))
COL_SHARDED = ("w_in", "w_q_b", "w_kv_b", "w_mlp_up", "w_ple")
PACK_AT = {"w_in": (0, 0), "w_q_b": (1024, 0), "w_ple": (1024, 384), "w_mla_up": (1280, 0), "w_swa_up": (1536, 0),
           "w_out": (1792, 0), "w_ple_gate": (2048, 0), "w_mlp_up": (2304, 0), "w_mlp_down": (3328, 0), "w_kv_b": (4352, 0)}
PACK_PAD = 4480
HALF = PACK_PAD // 2
SMALL = (("g_mix_pre", 1024), ("g_q_a", 256), ("g_kv_a", 128), ("sinks", 16), ("g_mix_post", 1024),
         ("g_mlp_pre", 1024), ("g_mlp_post", 1024), ("g_ple", 1024))


def _dot(a, b):
    return jnp.dot(a, b, preferred_element_type=F32)


def _dot_nt(a, b):
    return lax.dot_general(a, b, NT, preferred_element_type=F32)


def _dot_tn(a, b):
    return lax.dot_general(a, b, TN, preferred_element_type=F32)


def _pcall(body, *, name, out_shape, grid=(), in_specs=None, out_specs=None, scratch=(), sem=None, vmem_mb=48):
    params = dict(vmem_limit_bytes=vmem_mb << 20)
    if sem is not None:
        params["dimension_semantics"] = sem
    return pl.pallas_call(body, name=name, grid=grid, in_specs=in_specs, out_specs=out_specs, out_shape=out_shape,
                          scratch_shapes=list(scratch), compiler_params=pltpu.CompilerParams(**params))


def _rows(tm, n, col=0):
    return pl.BlockSpec((tm, n), lambda i: (i, col))


def _full(shape):
    return pl.BlockSpec(shape, lambda i: (0,) * len(shape))


def _rms(x, g):
    r = lax.rsqrt(jnp.mean(x * x, axis=-1, keepdims=True) + EPS)
    return x * r * g


def _rms_bwd(dy, x, g):
    r = lax.rsqrt(jnp.mean(x * x, axis=-1, keepdims=True) + EPS)
    xn = x * r
    dn = dy * g
    dx = r * (dn - xn * jnp.mean(dn * xn, axis=-1, keepdims=True))
    return dx, jnp.sum(dy * xn, axis=0, keepdims=True)


def _sigmoid(x):
    return 1.0 / (1.0 + jnp.exp(-x))


def _rope(x, c, a, b, half):
    return x * c + pltpu.roll(x, LANES - half, 1) * a + pltpu.roll(x, half, 1) * b


def _rope_tables(T, kind):
    lane = jnp.arange(LANES)
    if kind == "mla":
        half = MLA_ROPE // 2
        rel = lane - MLA_NOPE
        on = (rel >= 0) & (rel < MLA_ROPE)
        d = MLA_ROPE
    else:
        half = SWA_HD // 2
        rel = lane % SWA_HD
        on = jnp.ones((LANES,), bool)
        d = SWA_HD
    first = on & (rel < half)
    second = on & (rel >= half)
    f = jnp.where(first, rel, rel - half).astype(F32)
    inv = jnp.exp(-math.log(ROPE_THETA) * f * (2.0 / d))
    ang = jnp.arange(T, dtype=F32)[:, None] * inv[None, :]
    cos, sin = jnp.cos(ang), jnp.sin(ang)
    c = jnp.where(on[None], cos, 1.0)
    a = jnp.where(first[None], -sin, 0.0)
    b = jnp.where(second[None], sin, 0.0)
    return c, a, b


def _fwd_in(x, g1, w_in_p, tm):
    T = x.shape[0]

    def body(x_ref, g_ref, w_ref, z_ref, h_ref):
        h = _rms(x_ref[...], g_ref[...]).astype(BF16)
        h_ref[...] = h
        z_ref[...] = _dot(h, w_ref[...])

    return _pcall(body, name="fwd_in", grid=(T // tm,),
                  in_specs=[_rows(tm, D), _full((1, D)), _full((D, NZ))],
                  out_specs=[_rows(tm, NZ), _rows(tm, D)],
                  out_shape=[SDS((T, NZ), F32), SDS((T, D), BF16)], sem=("parallel",))(x, g1, w_in_p)


def _fwd_qkv(z, gq, gkv, wqb, wkn, wv, tab_m, tab_s, tm):
    T = z.shape[0]

    def body(qa_ref, sq_ref, skd_ref, svd_ref, kva_ref, kr_ref, gq_ref, gkv_ref, wqb_ref, wkn_ref, wv_ref,
             cm_ref, am_ref, bm_ref, cs_ref, as_ref, bs_ref,
             qn_ref, kvn_ref, qm_ref, km_ref, vm_ref, kt_ref, vt_ref, qs_ref, ks_ref, vs_ref):
        qn = _rms(qa_ref[...], gq_ref[...]).astype(BF16)
        qn_ref[...] = qn
        kvn = _rms(kva_ref[...], gkv_ref[...]).astype(BF16)
        kvn_ref[...] = kvn
        cm, am, bm = cm_ref[...], am_ref[...], bm_ref[...]
        cs, as_, bs = cs_ref[...], as_ref[...], bs_ref[...]
        k_rope = _rope(kr_ref[...], cm, am, bm, MLA_ROPE // 2)
        for j in range(D // LANES):
            sl = slice(LANES * j, LANES * (j + 1))
            v = _dot(kvn, wv_ref[:, sl])
            vm_ref[:, sl] = v.astype(BF16)
            vt_ref[0, sl, :] = v.T.astype(BF16)
        for h in range(MLA_HEADS):
            sl = slice(LANES * h, LANES * (h + 1))
            qh = _dot(qn, wqb_ref[:, sl])
            qm_ref[:, sl] = _rope(qh, cm, am, bm, MLA_ROPE // 2).astype(BF16)
            k = _dot(kvn, wkn_ref[:, sl]) + k_rope
            km_ref[:, sl] = k.astype(BF16)
            kt_ref[0, sl, :] = k.T.astype(BF16)
        for j in range(D // LANES):
            sl = slice(LANES * j, LANES * (j + 1))
            qs_ref[:, sl] = _rope(sq_ref[:, sl], cs, as_, bs, SWA_HD // 2).astype(BF16)
        for j in range(2):
            sl = slice(LANES * j, LANES * (j + 1))
            ks_ref[:, sl] = _rope(skd_ref[:, sl], cs, as_, bs, SWA_HD // 2).astype(BF16)
        vs_ref[...] = svd_ref[...].astype(BF16)

    tab = [_rows(tm, LANES)] * 6
    return _pcall(body, name="fwd_qkv", grid=(T // tm,),
                  in_specs=[_rows(tm, 256, 12), _rows(tm, 1024, 0), _rows(tm, 256, 13), _rows(tm, 256, 14),
                            _rows(tm, 128, 30), _rows(tm, 128, 31), _full((1, Q_LORA)), _full((1, KV_LORA)),
                            _full((Q_LORA, 2048)), _full((KV_LORA, 2048)), _full((KV_LORA, 1024))] + tab,
                  out_specs=[_rows(tm, Q_LORA), _rows(tm, KV_LORA), _rows(tm, 2048), _rows(tm, 2048), _rows(tm, 1024),
                             pl.BlockSpec((1, 2048, tm), lambda i: (i, 0, 0)), pl.BlockSpec((1, 1024, tm), lambda i: (i, 0, 0)),
                             _rows(tm, 1024), _rows(tm, 256), _rows(tm, 256)],
                  out_shape=[SDS((T, Q_LORA), BF16), SDS((T, KV_LORA), BF16), SDS((T, 2048), BF16), SDS((T, 2048), BF16),
                             SDS((T, 1024), BF16), SDS((T // tm, 2048, tm), BF16), SDS((T // tm, 1024, tm), BF16),
                             SDS((T, 1024), BF16), SDS((T, 256), BF16), SDS((T, 256), BF16)],
                  sem=("parallel",))(z, z, z, z, z, z, gq, gkv, wqb, wkn, wv, *tab_m, *tab_s)


def _mla_fwd(qm, km, vt, tb):
    T = qm.shape[0]
    nb = T // tb
    cc = ATT_COLS

    def body(q_ref, k_ref, vt_ref, o_ref, l_ref, s_ref, p_ref, m_ref, d_ref, acc_ref):
        i = pl.program_id(1)
        m_ref[...] = jnp.full(m_ref.shape, NEG, F32)
        d_ref[...] = jnp.zeros_like(d_ref)
        acc_ref[...] = jnp.zeros_like(acc_ref)
        key = lax.broadcasted_iota(jnp.int32, (tb, cc), 0)
        qry = lax.broadcasted_iota(jnp.int32, (tb, cc), 1)

        def scores(j, slot):
            off = pl.multiple_of(j * tb, tb)
            for hh in range(2):
                sl = slice(LANES * hh, LANES * (hh + 1))
                s_ref[slot, hh] = _dot_nt(k_ref[pl.ds(off, tb), sl], q_ref[:, sl])

        def softmax_pv(j, slot, diagonal):
            for hh in range(2):
                for c in range(tb // cc):
                    cols = slice(cc * c, cc * (c + 1))
                    if diagonal:
                        t = jnp.where(key <= qry + cc * c, s_ref[slot, hh, :, cols] * MLA_LOG2_SCALE, NEG)
                        top = jnp.max(t, axis=0, keepdims=True)
                    else:
                        top = jnp.max(s_ref[slot, hh, :, cols], axis=0, keepdims=True) * MLA_LOG2_SCALE
                    m_old = m_ref[hh, :, cols]
                    mn = jnp.maximum(m_old, top)
                    al = jnp.exp2(m_old - mn)
                    if diagonal:
                        p = jnp.exp2(t - mn)
                    else:
                        p = jnp.exp2(s_ref[slot, hh, :, cols] * MLA_LOG2_SCALE - mn)
                    m_ref[hh, :, cols] = mn
                    d_ref[hh, :, cols] = al * d_ref[hh, :, cols] + jnp.sum(p, axis=0, keepdims=True)
                    acc_ref[hh, :, cols] = al * acc_ref[hh, :, cols]
                    p_ref[hh, :, cols] = p.astype(BF16)
            v_t = vt_ref[j]
            for hh in range(2):
                acc_ref[hh] += _dot(v_t, p_ref[hh])

        def step(t, carry):
            scores(2 * t + 1, 1)
            softmax_pv(2 * t, 0, False)
            scores(2 * t + 2, 0)
            softmax_pv(2 * t + 1, 1, False)
            return carry

        scores(0, 0)
        lax.fori_loop(0, i // 2, step, 0)

        @pl.when(i % 2 == 1)
        def _():
            scores(i, 1)
            softmax_pv(i - 1, 0, False)
            softmax_pv(i, 1, True)

        @pl.when(i % 2 == 0)
        def _():
            softmax_pv(i, 0, True)
        first = lax.broadcasted_iota(jnp.int32, (LANES, tb), 0) < 64
        o_ref[...] = jnp.where(first, acc_ref[0] / d_ref[0], acc_ref[1] / d_ref[1]).T
        sub = lax.broadcasted_iota(jnp.int32, (8, tb), 0)
        lse = [m_ref[hh] + jnp.log(d_ref[hh]) * LOG2_E for hh in range(2)]
        l_ref[0, 0] = jnp.where(sub == 0, lse[0], jnp.where(sub == 1, lse[1], 0.0))

    return _pcall(body, name="mla_fwd", grid=(MLA_HEADS // 2, nb),
                  in_specs=[pl.BlockSpec((tb, 256), lambda p, i: (i, p)), pl.BlockSpec((T, 256), lambda p, i: (0, p)),
                            pl.BlockSpec((nb, LANES, tb), lambda p, i: (0, p, 0))],
                  out_specs=[pl.BlockSpec((tb, LANES), lambda p, i: (i, p)),
                             pl.BlockSpec((1, 1, 8, tb), lambda p, i: (p, i, 0, 0))],
                  out_shape=[SDS((T, D), F32), SDS((MLA_HEADS // 2, nb, 8, tb), F32)],
                  scratch=[pltpu.VMEM((2, 2, tb, tb), F32), pltpu.VMEM((2, tb, tb), BF16), pltpu.VMEM((2, 1, tb), F32),
                           pltpu.VMEM((2, 1, tb), F32), pltpu.VMEM((2, LANES, tb), F32)],
                  sem=("parallel", "arbitrary"))(qm, km, vt)


def _swa_mask(n):
    row = lax.broadcasted_iota(jnp.int32, (WINDOW, 2 * WINDOW), 0)
    col = lax.broadcasted_iota(jnp.int32, (WINDOW, 2 * WINDOW), 1)
    rel = row - col + WINDOW
    return (rel >= 0) & (rel < WINDOW) & ((col >= WINDOW) | (n > 0))


def _swa_specs(T):
    nb = T // WINDOW
    cur = lambda w: pl.BlockSpec((WINDOW, w), lambda n: (n, 0))
    prev = lambda w: pl.BlockSpec((WINDOW, w), lambda n: (jnp.maximum(n - 1, 0), 0))
    return nb, cur, prev


def _swa_fwd(sinks, qs, ks, vs):
    T = qs.shape[0]
    nb, cur, prev = _swa_specs(T)

    def body(sink_ref, q_ref, kc_ref, kp_ref, vc_ref, vp_ref, o_ref, l_ref):
        n = pl.program_id(0)
        mask = _swa_mask(n)
        lo = lax.broadcasted_iota(jnp.int32, (WINDOW, LANES), 1) < 64
        for g in range(2):
            gs = slice(LANES * g, LANES * (g + 1))
            kb = jnp.concatenate([kp_ref[:, gs], kc_ref[:, gs]], axis=0)
            vb = jnp.concatenate([vp_ref[:, gs], vc_ref[:, gs]], axis=0)
            for jj in range(4):
                j = 4 * g + jj
                sl = slice(LANES * j, LANES * (j + 1))
                qp = q_ref[:, sl]
                outs, lses = [], []
                for hf in range(2):
                    hm = lo if hf == 0 else jnp.logical_not(lo)
                    qh = jnp.where(hm, qp, jnp.zeros_like(qp))
                    s = jnp.where(mask, _dot_nt(qh, kb) * SWA_SCALE, NEG)
                    sk = sink_ref[2 * j + hf]
                    m = jnp.maximum(jnp.max(s, axis=1, keepdims=True), sk)
                    e = jnp.exp(s - m)
                    den = jnp.sum(e, axis=1, keepdims=True) + jnp.exp(sk - m)
                    p = e / den
                    outs.append(_dot(p.astype(BF16), vb))
                    lses.append(jnp.broadcast_to(m + jnp.log(den), (WINDOW, LANES)))
                o_ref[:, sl] = jnp.where(lo, outs[0], outs[1])
                l_ref[:, sl] = jnp.where(lo, lses[0], lses[1])

    return _pcall(body, name="swa_fwd", grid=(nb,),
                  in_specs=[pl.BlockSpec(memory_space=pltpu.SMEM), cur(D), cur(256), prev(256), cur(256), prev(256)],
                  out_specs=[cur(D), cur(D)], out_shape=[SDS((T, D), F32)] * 2,
                  sem=("parallel",))(sinks, qs, ks, ks, vs, vs)


def _fwd_mix(om, os_, z, x, wmu, wsu, wo, g2, tm):
    T = x.shape[0]

    def body(om_ref, os_ref, ga_ref, gb_ref, x_ref, wmu_ref, wsu_ref, wo_ref, g2_ref,
             y_ref, yo_ref, au_ref, bu_ref, x1_ref):
        au = _dot(om_ref[...].astype(BF16), wmu_ref[...])
        bu = _dot(os_ref[...].astype(BF16), wsu_ref[...])
        au_ref[...] = au
        bu_ref[...] = bu
        y = (_sigmoid(ga_ref[...]) * au + _sigmoid(gb_ref[...]) * bu).astype(BF16)
        y_ref[...] = y
        yo = _dot(y, wo_ref[...])
        yo_ref[...] = yo
        x1_ref[...] = x_ref[...] + _rms(yo, g2_ref[...])

    r = _rows(tm, D)
    w = _full((D, D))
    return _pcall(body, name="fwd_mix", grid=(T // tm,),
                  in_specs=[r, r, _rows(tm, D, 1), _rows(tm, D, 2), r, w, w, w, _full((1, D))],
                  out_specs=[r] * 5,
                  out_shape=[SDS((T, D), BF16), SDS((T, D), F32), SDS((T, D), F32), SDS((T, D), F32), SDS((T, D), F32)],
                  sem=("parallel",))(om, os_, z, z, x, wmu, wsu, wo, g2)


def _fwd_mlp_up(x1, g3, w1, tm):
    T = x1.shape[0]

    def body(x_ref, g_ref, w_ref, h_ref, a_ref, u_ref):
        h = _rms(x_ref[...], g_ref[...]).astype(BF16)
        h_ref[...] = h
        a = _dot(h, w_ref[...])
        a_ref[...] = a
        u_ref[...] = jnp.square(jnp.maximum(a, 0.0)).astype(BF16)

    return _pcall(body, name="fwd_mlp_up", grid=(T // tm,),
                  in_specs=[_rows(tm, D), _full((1, D)), _full((D, D_FF))],
                  out_specs=[_rows(tm, D), _rows(tm, D_FF), _rows(tm, D_FF)],
                  out_shape=[SDS((T, D), BF16), SDS((T, D_FF), F32), SDS((T, D_FF), BF16)],
                  sem=("parallel",))(x1, g3, w1)


def _fwd_mlp_down(u, w2, x1, g4, tm):
    T = x1.shape[0]

    def body(u_ref, w_ref, x_ref, g_ref, d_ref, x2_ref):
        d = _dot(u_ref[...], w_ref[...])
        d_ref[...] = d
        x2_ref[...] = x_ref[...] + _rms(d, g_ref[...])

    return _pcall(body, name="fwd_mlp_down", grid=(T // tm,),
                  in_specs=[_rows(tm, D_FF), _full((D_FF, D)), _rows(tm, D), _full((1, D))],
                  out_specs=[_rows(tm, D), _rows(tm, D)], out_shape=[SDS((T, D), F32)] * 2,
                  sem=("parallel",))(u, w2, x1, g4)


def _ple_fwd_bwd(p, x2, tgt, wple, g5, wpg, tm):
    T = x2.shape[0]

    def body(p_ref, x2_ref, t_ref, wple_ref, g5_ref, wpg_ref, loss_ref, dx2_ref, dgt_ref, de0_ref, dg5_ref):
        @pl.when(pl.program_id(0) == 0)
        def _():
            loss_ref[...] = jnp.zeros_like(loss_ref)
            dg5_ref[...] = jnp.zeros_like(dg5_ref)

        e0 = _dot(p_ref[...].astype(BF16), wple_ref[...])
        g5 = g5_ref[...]
        r = lax.rsqrt(jnp.mean(e0 * e0, axis=-1, keepdims=True) + EPS)
        en = e0 * r
        e = en * g5
        x2 = x2_ref[...]
        s = _sigmoid(_dot(x2.astype(BF16), wpg_ref[...]))
        diff = x2 + s * e - t_ref[...]
        sq = jnp.sum(jnp.sum(diff * diff, axis=1, keepdims=True), axis=0, keepdims=True)
        loss_ref[...] += jnp.broadcast_to(sq * (0.5 / D), loss_ref.shape)
        dx3 = diff * (1.0 / D)
        de = dx3 * s
        dgt = (dx3 * e * s * (1.0 - s)).astype(BF16)
        dgt_ref[...] = dgt
        dn = de * g5
        de0_ref[...] = (r * (dn - en * jnp.mean(dn * en, axis=-1, keepdims=True))).astype(BF16)
        dg5_ref[...] += jnp.sum(de * en, axis=0, keepdims=True)
        dx2_ref[...] = dx3 + _dot_nt(dgt, wpg_ref[...])

    r = _rows(tm, D)
    return _pcall(body, name="ple_fwd_bwd", grid=(T // tm,),
                  in_specs=[_rows(tm, PLE), r, r, _full((PLE, D)), _full((1, D)), _full((D, D))],
                  out_specs=[_full((8, LANES)), r, r, r, _full((1, D))],
                  out_shape=[SDS((8, LANES), F32), SDS((T, D), F32), SDS((T, D), BF16), SDS((T, D), BF16), SDS((1, D), F32)],
                  sem=("arbitrary",))(p, x2, tgt, wple, g5, wpg)


def _bwd_mlp_down(dx2, d, g4, w2, a, tm):
    T = dx2.shape[0]

    def body(dx_ref, d_ref, g_ref, w_ref, a_ref, dd_ref, da_ref, dg_ref):
        @pl.when(pl.program_id(0) == 0)
        def _():
            dg_ref[...] = jnp.zeros_like(dg_ref)

        dd, dg = _rms_bwd(dx_ref[...], d_ref[...], g_ref[...])
        dg_ref[...] += dg
        ddb = dd.astype(BF16)
        dd_ref[...] = ddb
        du = _dot_nt(ddb, w_ref[...])
        da_ref[...] = (du * (2.0 * jnp.maximum(a_ref[...], 0.0))).astype(BF16)

    return _pcall(body, name="bwd_mlp_down", grid=(T // tm,),
                  in_specs=[_rows(tm, D), _rows(tm, D), _full((1, D)), _full((D_FF, D)), _rows(tm, D_FF)],
                  out_specs=[_rows(tm, D), _rows(tm, D_FF), _full((1, D))],
                  out_shape=[SDS((T, D), BF16), SDS((T, D_FF), BF16), SDS((1, D), F32)],
                  sem=("arbitrary",))(dx2, d, g4, w2, a)


def _bwd_mlp_up(da, w1, x1, g3, dx2, tm):
    T = dx2.shape[0]

    def body(da_ref, w_ref, x_ref, g_ref, dx2_ref, dx1_ref, dg_ref):
        @pl.when(pl.program_id(0) == 0)
        def _():
            dg_ref[...] = jnp.zeros_like(dg_ref)

        dh = _dot_nt(da_ref[...], w_ref[...])
        dx, dg = _rms_bwd(dh, x_ref[...], g_ref[...])
        dg_ref[...] += dg
        dx1_ref[...] = dx2_ref[...] + dx

    return _pcall(body, name="bwd_mlp_up", grid=(T // tm,),
                  in_specs=[_rows(tm, D_FF), _full((D, D_FF)), _rows(tm, D), _full((1, D)), _rows(tm, D)],
                  out_specs=[_rows(tm, D), _full((1, D))],
                  out_shape=[SDS((T, D), F32), SDS((1, D), F32)], sem=("arbitrary",))(da, w1, x1, g3, dx2)


def _bwd_mix(dx1, yo, g2, wo, z, au, bu, wmu, wsu, om, tm):
    T = dx1.shape[0]

    def body(dx_ref, yo_ref, g_ref, wo_ref, ga_ref, gb_ref, au_ref, bu_ref, wmu_ref, wsu_ref, om_ref,
             dyo_ref, dg_ref, dau_ref, dbu_ref, dga_ref, dgb_ref, dom_ref, dos_ref, dl_ref):
        @pl.when(pl.program_id(0) == 0)
        def _():
            dg_ref[...] = jnp.zeros_like(dg_ref)

        dyo, dg = _rms_bwd(dx_ref[...], yo_ref[...], g_ref[...])
        dg_ref[...] += dg
        dyob = dyo.astype(BF16)
        dyo_ref[...] = dyob
        dy = _dot_nt(dyob, wo_ref[...])
        sa = _sigmoid(ga_ref[...])
        sb = _sigmoid(gb_ref[...])
        dau = (dy * sa).astype(BF16)
        dbu = (dy * sb).astype(BF16)
        dau_ref[...] = dau
        dbu_ref[...] = dbu
        dga_ref[...] = (dy * au_ref[...] * sa * (1.0 - sa)).astype(BF16)
        dgb_ref[...] = (dy * bu_ref[...] * sb * (1.0 - sb)).astype(BF16)
        dom = _dot_nt(dau, wmu_ref[...])
        dom_ref[...] = dom
        dos_ref[...] = _dot_nt(dbu, wsu_ref[...])
        prod = dom * om_ref[...]
        sub = lax.broadcasted_iota(jnp.int32, (8, tm), 0)
        for pr in range(MLA_HEADS // 2):
            pt = prod[:, LANES * pr:LANES * (pr + 1)].T
            d0 = jnp.sum(pt[0:64], axis=0, keepdims=True)
            d1 = jnp.sum(pt[64:128], axis=0, keepdims=True)
            dl_ref[pr, 0] = jnp.where(sub == 0, d0, jnp.where(sub == 1, d1, 0.0))

    r = _rows(tm, D)
    w = _full((D, D))
    return _pcall(body, name="bwd_mix", grid=(T // tm,),
                  in_specs=[r, r, _full((1, D)), w, _rows(tm, D, 1), _rows(tm, D, 2), r, r, w, w, r],
                  out_specs=[r, _full((1, D)), r, r, r, r, r, r, pl.BlockSpec((MLA_HEADS // 2, 1, 8, tm), lambda i: (0, i, 0, 0))],
                  out_shape=[SDS((T, D), BF16), SDS((1, D), F32), SDS((T, D), BF16), SDS((T, D), BF16), SDS((T, D), BF16),
                             SDS((T, D), BF16), SDS((T, D), F32), SDS((T, D), F32), SDS((MLA_HEADS // 2, T // tm, 8, tm), F32)],
                  sem=("arbitrary",))(dx1, yo, g2, wo, z, z, au, bu, wmu, wsu, om)


def _mla_bwd(qm, km, kt, vm, do, lse, delta, tb):
    T = qm.shape[0]
    nb = T // tb
    cc = ATT_COLS

    def body(q_ref, k_ref, kt_ref, v_ref, do_ref, l_ref, dl_ref, dqt_ref, dk_ref, dv_ref, s_ref, dp_ref, p_ref, ds_ref, dom_ref):
        j = pl.program_id(1)

        @pl.when(j == 0)
        def _():
            dqt_ref[...] = jnp.zeros_like(dqt_ref)

        dk_ref[...] = jnp.zeros_like(dk_ref)
        dv_ref[...] = jnp.zeros_like(dv_ref)
        lo = lax.broadcasted_iota(jnp.int32, (tb, LANES), 1) < 64
        key = lax.broadcasted_iota(jnp.int32, (tb, cc), 0)
        qry = lax.broadcasted_iota(jnp.int32, (tb, cc), 1)

        def scores(i, slot):
            rows_i = pl.ds(pl.multiple_of(i * tb, tb), tb)
            d_o = do_ref[rows_i, :]
            v = v_ref[...]
            for hh in range(2):
                sl = slice(LANES * hh, LANES * (hh + 1))
                hm = lo if hh == 0 else jnp.logical_not(lo)
                dom_ref[slot, hh] = jnp.where(hm, d_o, 0.0).astype(BF16)
                s_ref[slot, hh] = _dot_nt(k_ref[:, sl], q_ref[rows_i, sl])
                dp_ref[slot, hh] = _dot_nt(v, dom_ref[slot, hh])

        def grads(i, slot, diagonal):
            rows_i = pl.ds(pl.multiple_of(i * tb, tb), tb)
            lse_i = l_ref[0, i]
            delta_i = dl_ref[0, i]
            for hh in range(2):
                for c in range(tb // cc):
                    cols = slice(cc * c, cc * (c + 1))
                    p = jnp.exp2(s_ref[slot, hh, :, cols] * MLA_LOG2_SCALE - lse_i[hh:hh + 1, cols])
                    if diagonal:
                        p = jnp.where(key <= qry + cc * c, p, 0.0)
                    p_ref[hh, :, cols] = p.astype(BF16)
                    ds_ref[hh, :, cols] = (p * (dp_ref[slot, hh, :, cols] - delta_i[hh:hh + 1, cols]) * MLA_SCALE).astype(BF16)
            for hh in range(2):
                sl = slice(LANES * hh, LANES * (hh + 1))
                dv_ref[...] += _dot(p_ref[hh], dom_ref[slot, hh])
                dk_ref[:, sl] += _dot(ds_ref[hh], q_ref[rows_i, sl])
                dqt_ref[i, sl, :] += _dot(kt_ref[0, sl, :], ds_ref[hh])

        n_off = nb - 1 - j

        def step(u, carry):
            i0 = j + 1 + 2 * u
            scores(i0 + 1, 1)
            grads(i0, 0, False)
            scores(jnp.where(i0 + 2 < nb, i0 + 2, j), 0)
            grads(i0 + 1, 1, False)
            return carry

        scores(jnp.where(n_off > 0, j + 1, j), 0)
        lax.fori_loop(0, n_off // 2, step, 0)

        @pl.when(n_off % 2 == 1)
        def _():
            scores(j, 1)
            grads(nb - 1, 0, False)
            grads(j, 1, True)

        @pl.when(n_off % 2 == 0)
        def _():
            grads(j, 0, True)

    pair = lambda w: pl.BlockSpec((T, w), lambda p, j: (0, p))
    blk = lambda w: pl.BlockSpec((tb, w), lambda p, j: (j, p))
    stat = pl.BlockSpec((1, nb, 8, tb), lambda p, j: (p, 0, 0, 0))
    return _pcall(body, name="mla_bwd", grid=(MLA_HEADS // 2, nb),
                  in_specs=[pair(256), blk(256), pl.BlockSpec((1, 256, tb), lambda p, j: (j, p, 0)), blk(LANES), pair(LANES),
                            stat, stat],
                  out_specs=[pl.BlockSpec((nb, 256, tb), lambda p, j: (0, p, 0)), blk(256), blk(LANES)],
                  out_shape=[SDS((nb, 2048, tb), F32), SDS((T, 2048), F32), SDS((T, D), F32)],
                  scratch=[pltpu.VMEM((2, 2, tb, tb), F32), pltpu.VMEM((2, 2, tb, tb), F32), pltpu.VMEM((2, tb, tb), BF16),
                           pltpu.VMEM((2, tb, tb), BF16), pltpu.VMEM((2, 2, tb, LANES), BF16)],
                  sem=("parallel", "arbitrary"))(qm, km, kt, vm, do, lse, delta)


def _swa_bwd(sinks, qs, ks, vs, do, o, lse):
    T = qs.shape[0]
    nb, cur, prev = _swa_specs(T)

    def body(sink_ref, q_ref, kc_ref, kp_ref, vc_ref, vp_ref, do_ref, o_ref, l_ref,
             dq_ref, dkc_ref, dkp_ref, dvc_ref, dvp_ref, dsink_ref):
        n = pl.program_id(0)

        @pl.when(n == 0)
        def _():
            dsink_ref[...] = jnp.zeros_like(dsink_ref)

        mask = _swa_mask(n)
        lo = lax.broadcasted_iota(jnp.int32, (WINDOW, LANES), 1) < 64
        lane8 = lax.broadcasted_iota(jnp.int32, (8, LANES), 1)
        dsink = jnp.zeros((8, LANES), F32)
        for g in range(2):
            gs = slice(LANES * g, LANES * (g + 1))
            kb = jnp.concatenate([kp_ref[:, gs], kc_ref[:, gs]], axis=0)
            vb = jnp.concatenate([vp_ref[:, gs], vc_ref[:, gs]], axis=0)
            dkb = jnp.zeros((2 * WINDOW, LANES), F32)
            dvb = jnp.zeros((2 * WINDOW, LANES), F32)
            for jj in range(4):
                j = 4 * g + jj
                sl = slice(LANES * j, LANES * (j + 1))
                qp = q_ref[:, sl]
                d_o = do_ref[:, sl]
                prod = d_o * o_ref[:, sl]
                lse_b = l_ref[:, sl]
                dqs = []
                for hf in range(2):
                    hm = lo if hf == 0 else jnp.logical_not(lo)
                    qh = jnp.where(hm, qp, jnp.zeros_like(qp))
                    s = jnp.where(mask, _dot_nt(qh, kb) * SWA_SCALE, NEG)
                    lse_h = jnp.max(jnp.where(hm, lse_b, -jnp.inf), axis=1, keepdims=True)
                    p = jnp.exp(s - lse_h)
                    dom = jnp.where(hm, d_o, 0.0).astype(BF16)
                    dp = _dot_nt(dom, vb)
                    delta = jnp.sum(jnp.where(hm, prod, 0.0), axis=1, keepdims=True)
                    ds = (p * (dp - delta) * SWA_SCALE).astype(BF16)
                    p_sink = jnp.exp(sink_ref[2 * j + hf] - lse_h)
                    d_sink = -jnp.sum(p_sink * delta, axis=0, keepdims=True)
                    dsink = dsink + jnp.where(lane8 == 2 * j + hf, d_sink, 0.0)
                    dvb = dvb + _dot_tn(p.astype(BF16), dom)
                    dkb = dkb + _dot_tn(ds, qh)
                    dqs.append(_dot(ds, kb))
                dq_ref[:, sl] = jnp.where(lo, dqs[0], dqs[1])
            dkp_ref[:, gs] = dkb[:WINDOW]
            dkc_ref[:, gs] = dkb[WINDOW:]
            dvp_ref[:, gs] = dvb[:WINDOW]
            dvc_ref[:, gs] = dvb[WINDOW:]
        dsink_ref[...] += dsink

    return _pcall(body, name="swa_bwd", grid=(nb,),
                  in_specs=[pl.BlockSpec(memory_space=pltpu.SMEM), cur(D), cur(256), prev(256), cur(256), prev(256),
                            cur(D), cur(D), cur(D)],
                  out_specs=[cur(D), cur(256), cur(256), cur(256), cur(256), _full((8, LANES))],
                  out_shape=[SDS((T, D), F32), SDS((T, 256), F32), SDS((T, 256), F32), SDS((T, 256), F32), SDS((T, 256), F32),
                             SDS((8, LANES), F32)],
                  sem=("arbitrary",))(sinks, qs, ks, ks, vs, vs, do, o, lse)


def _bwd_qkv(dqm, dkm, dvm, dqs, dkc, dkp, dvc, dvp, z, gq, gkv, wqb, wkn, wv, tab_m, tab_s):
    T = z.shape[0]
    tm = WINDOW
    nb = T // tm
    per = dqm.shape[2] // tm

    def body(dqm_ref, dkm_ref, dvm_ref, dqs_ref, dkc_ref, dkp_ref, dvc_ref, dvp_ref, qa_ref, kva_ref, gq_ref, gkv_ref,
             wqb_ref, wkn_ref, wv_ref, cm_ref, am_ref, bm_ref, cs_ref, as_ref, bs_ref,
             dq_out, dkn_out, dv_out, dsq_ref, drest_ref, dgq_ref, dgkv_ref):
        i = pl.program_id(0)

        @pl.when(i == 0)
        def _():
            dgq_ref[...] = jnp.zeros_like(dgq_ref)
            dgkv_ref[...] = jnp.zeros_like(dgkv_ref)

        cm, am, bm = cm_ref[...], -am_ref[...], -bm_ref[...]
        cs, as_, bs = cs_ref[...], -as_ref[...], -bs_ref[...]
        lane = lax.broadcasted_iota(jnp.int32, (tm, LANES), 1)
        nope = lane < MLA_NOPE
        roped = jnp.logical_and(lane >= MLA_NOPE, lane < MLA_NOPE + MLA_ROPE)
        dkr = jnp.zeros((tm, LANES), F32)
        dqn = jnp.zeros((tm, Q_LORA), F32)
        dkvn = jnp.zeros((tm, KV_LORA), F32)
        for h in range(MLA_HEADS):
            sl = slice(LANES * h, LANES * (h + 1))
            dq_h = _rope(dqm_ref[0, sl, :].T, cm, am, bm, MLA_ROPE // 2).astype(BF16)
            dq_out[:, sl] = dq_h
            dqn = dqn + _dot_nt(dq_h, wqb_ref[:, sl])
            dk_h = dkm_ref[:, sl]
            dkn_h = jnp.where(nope, dk_h, 0.0).astype(BF16)
            dkn_out[:, sl] = dkn_h
            dkvn = dkvn + _dot_nt(dkn_h, wkn_ref[:, sl])
            dkr = dkr + jnp.where(roped, dk_h, 0.0)
        dvb = dvm_ref[...].astype(BF16)
        dv_out[...] = dvb
        dkvn = dkvn + _dot_nt(dvb, wv_ref[...])
        dqa, dgq = _rms_bwd(dqn, qa_ref[...], gq_ref[...])
        dkva, dgkv = _rms_bwd(dkvn, kva_ref[...], gkv_ref[...])
        dgq_ref[...] += dgq
        dgkv_ref[...] += dgkv
        for j in range(D // LANES):
            sl = slice(LANES * j, LANES * (j + 1))
            dsq_ref[:, sl] = _rope(dqs_ref[:, sl], cs, as_, bs, SWA_HD // 2).astype(BF16)
        keep = (i < nb - 1).astype(F32)
        drest_ref[:, 0:256] = dqa.astype(BF16)
        for j in range(2):
            sl = slice(LANES * j, LANES * (j + 1))
            dk = dkc_ref[:, sl] + keep * dkp_ref[:, sl]
            drest_ref[:, 256 + LANES * j:256 + LANES * (j + 1)] = _rope(dk, cs, as_, bs, SWA_HD // 2).astype(BF16)
        drest_ref[:, 512:768] = (dvc_ref[...] + keep * dvp_ref[...]).astype(BF16)
        drest_ref[:, 768:896] = dkva.astype(BF16)
        drest_ref[:, 896:1024] = _rope(dkr, cm, am, bm, MLA_ROPE // 2).astype(BF16)

    nxt = pl.BlockSpec((tm, 256), lambda i: (jnp.minimum(i + 1, nb - 1), 0))
    tab = [_rows(tm, LANES)] * 6
    return _pcall(body, name="bwd_qkv", grid=(nb,),
                  in_specs=[pl.BlockSpec((1, 2048, tm), lambda i: (i // per, 0, i % per)),
                            _rows(tm, 2048), _rows(tm, 1024), _rows(tm, 1024), _rows(tm, 256), nxt,
                            _rows(tm, 256), nxt, _rows(tm, 256, 12), _rows(tm, 128, 30), _full((1, Q_LORA)), _full((1, KV_LORA)),
                            _full((Q_LORA, 2048)), _full((KV_LORA, 2048)), _full((KV_LORA, 1024))] + tab,
                  out_specs=[_rows(tm, 2048), _rows(tm, 2048), _rows(tm, 1024), _rows(tm, 1024), _rows(tm, 1024),
                             _full((1, Q_LORA)), _full((1, KV_LORA))],
                  out_shape=[SDS((T, 2048), BF16), SDS((T, 2048), BF16), SDS((T, 1024), BF16), SDS((T, 1024), BF16),
                             SDS((T, 1024), BF16), SDS((1, Q_LORA), F32), SDS((1, KV_LORA), F32)],
                  sem=("arbitrary",))(dqm, dkm, dvm, dqs, dkc, dkp, dvc, dvp, z, z, gq, gkv, wqb, wkn, wv, *tab_m, *tab_s)


def _bwd_in(dsq, dga, dgb, drest, w_in_p, x, g1, dx1, tm):
    T = x.shape[0]

    def body(a_ref, b_ref, c_ref, d_ref, w_ref, x_ref, g_ref, dx1_ref, dx_ref, dg_ref):
        @pl.when(pl.program_id(0) == 0)
        def _():
            dg_ref[...] = jnp.zeros_like(dg_ref)

        dh = (_dot_nt(a_ref[...], w_ref[:, 0:1024]) + _dot_nt(b_ref[...], w_ref[:, 1024:2048])
              + _dot_nt(c_ref[...], w_ref[:, 2048:3072]) + _dot_nt(d_ref[...], w_ref[:, 3072:4096]))
        dx, dg = _rms_bwd(dh, x_ref[...], g_ref[...])
        dg_ref[...] += dg
        dx_ref[...] = dx1_ref[...] + dx

    r = _rows(tm, D)
    return _pcall(body, name="bwd_in", grid=(T // tm,),
                  in_specs=[r, r, r, r, _full((D, NZ)), r, _full((1, D)), r],
                  out_specs=[r, _full((1, D))], out_shape=[SDS((T, D), F32), SDS((1, D), F32)],
                  sem=("arbitrary",))(dsq, dga, dgb, drest, w_in_p, x, g1, dx1)


def _wgrad(a, g, name):
    T, K = a.shape
    N = g.shape[1]
    tk, tn, tt = min(K, 512), min(N, 1024), min(T, 512)
    assert K % tk == 0 and N % tn == 0 and T % tt == 0, (a.shape, g.shape)

    def body(a_ref, g_ref, o_ref):
        @pl.when(pl.program_id(2) == 0)
        def _():
            o_ref[...] = jnp.zeros_like(o_ref)

        o_ref[...] += _dot_tn(a_ref[...].astype(BF16), g_ref[...].astype(BF16))

    return _pcall(body, name=name, grid=(K // tk, N // tn, T // tt),
                  in_specs=[pl.BlockSpec((tt, tk), lambda k, n, t: (t, k)), pl.BlockSpec((tt, tn), lambda k, n, t: (t, n))],
                  out_specs=pl.BlockSpec((tk, tn), lambda k, n, t: (k, n)), out_shape=SDS((K, N), F32),
                  sem=("parallel", "parallel", "arbitrary"))(a, g)


def _adamw(w, packed_g, m, v, name):
    _, R, C = w.shape
    tr = min(R, 256)
    row0, lane0 = PACK_AT[name]
    assert row0 % tr == 0 and R % tr == 0

    def body(w_ref, g_ref, m_ref, v_ref, go_ref, d_ref, m2_ref, v2_ref):
        g_ = g_ref[:, lane0:lane0 + C]
        go_ref[0] = g_
        m2 = ADAM_B1 * m_ref[0] + (1.0 - ADAM_B1) * g_
        v2 = ADAM_B2 * v_ref[0] + (1.0 - ADAM_B2) * jnp.square(g_)
        m_hat = m2 / (1.0 - ADAM_B1 ** ADAM_STEP)
        v_hat = v2 / (1.0 - ADAM_B2 ** ADAM_STEP)
        d_ref[0] = -ADAM_LR * (m_hat / (jnp.sqrt(v_hat) + ADAM_EPS) + ADAM_WD * w_ref[0])
        m2_ref[0] = m2
        v2_ref[0] = v2

    r = pl.BlockSpec((1, tr, C), lambda i: (0, i, 0))
    return _pcall(body, name="adamw_" + name, grid=(R // tr,),
                  in_specs=[r, pl.BlockSpec((tr, D), lambda i: (row0 // tr + i, 0)), r, r], out_specs=[r] * 4,
                  out_shape=[SDS((1, R, C), F32)] * 4, sem=("parallel",))(w, packed_g, m, v)


def _adamw_small(w, parts, m, v):
    def body(w_ref, p_ref, m_ref, v_ref, g_ref, d_ref, m2_ref, v2_ref):
        g_ = p_ref[0]
        for k in range(1, N_DEV):
            g_ = g_ + p_ref[k]
        g_ref[...] = g_
        m2 = ADAM_B1 * m_ref[...] + (1.0 - ADAM_B1) * g_
        v2 = ADAM_B2 * v_ref[...] + (1.0 - ADAM_B2) * jnp.square(g_)
        m_hat = m2 / (1.0 - ADAM_B1 ** ADAM_STEP)
        v_hat = v2 / (1.0 - ADAM_B2 ** ADAM_STEP)
        d_ref[...] = -ADAM_LR * (m_hat / (jnp.sqrt(v_hat) + ADAM_EPS) + ADAM_WD * w_ref[...])
        m2_ref[...] = m2
        v2_ref[...] = v2

    s = _full((8, D))
    return _pcall(body, name="adamw_small", grid=(1,), in_specs=[s, _full((N_DEV, 8, D)), s, s], out_specs=[s] * 4,
                  out_shape=[SDS((8, D), F32)] * 4, sem=("arbitrary",))(w, parts, m, v)


ANY = pl.BlockSpec(memory_space=pl.ANY)


def _place():
    x, y, c = lax.axis_index("x"), lax.axis_index("y"), lax.axis_index("c")
    chips = [(1 - x, y), (x, 1 - y), (1 - x, 1 - y)]
    return x, y, c, chips


def _all_gather(wpk):
    def body(in_ref, out_ref, send_sems, recv_sems):
        x, y, c, chips = _place()
        half = pl.ds(pl.multiple_of(c * HALF, 16), HALF)
        other = pl.ds(pl.multiple_of((1 - c) * HALF, 16), HALF)

        def copy(k, src, dst, to):
            return pltpu.make_async_remote_copy(src_ref=src, dst_ref=dst, send_sem=send_sems.at[k], recv_sem=recv_sems.at[k],
                                                device_id=to, device_id_type=MESH)

        first = [copy(k, in_ref.at[half], out_ref.at[2 * x + y, half], (cx, cy, c)) for k, (cx, cy) in enumerate(chips)]
        for cp in first:
            cp.start()
        passed = []
        for k, (cx, cy) in enumerate(chips):
            slot = out_ref.at[2 * cx + cy, half]
            copy(k, slot, slot, (x, y, c)).wait_recv()
            fwd = copy(3 + k, slot, slot, (x, y, 1 - c))
            fwd.start()
            passed.append(fwd)
        for k, (cx, cy) in enumerate(chips):
            slot = out_ref.at[2 * cx + cy, other]
            copy(3 + k, slot, slot, (x, y, c)).wait_recv()
        for cp in first + passed:
            cp.wait_send()

    return _pcall(body, name="all_gather_weights", in_specs=[ANY], out_specs=ANY,
                  out_shape=SDS((N_CHIPS, PACK_PAD, D), BF16),
                  scratch=[pltpu.SemaphoreType.DMA((6,)), pltpu.SemaphoreType.DMA((6,))])(wpk)


def _rs_sibling(gpk):
    def body(in_ref, out_ref, send_sem, recv_sem):
        x, y, c, _ = _place()
        theirs = pl.ds(pl.multiple_of((1 - c) * HALF, 8), HALF)
        cp = pltpu.make_async_remote_copy(src_ref=in_ref.at[:, theirs], dst_ref=out_ref, send_sem=send_sem, recv_sem=recv_sem,
                                          device_id=(x, y, 1 - c), device_id_type=MESH)
        cp.start()
        cp.wait()

    return _pcall(body, name="rs_sibling", in_specs=[ANY], out_specs=ANY, out_shape=SDS((N_CHIPS, HALF, D), F32),
                  scratch=[pltpu.SemaphoreType.DMA, pltpu.SemaphoreType.DMA])(gpk)


def _rs_add_sibling(cidx, gpk, got):
    th = HALF // 7
    nh = HALF // th

    def body(c_ref, a_ref, b_ref, o_ref):
        o_ref[...] = (a_ref[...] + b_ref[...]).astype(BF16)

    gs = pltpu.PrefetchScalarGridSpec(
        num_scalar_prefetch=1, grid=(N_CHIPS, nh),
        in_specs=[pl.BlockSpec((1, th, D), lambda j, i, c: (j, c[0] * nh + i, 0)), pl.BlockSpec((1, th, D), lambda j, i, c: (j, i, 0))],
        out_specs=pl.BlockSpec((1, th, D), lambda j, i, c: (j, i, 0)))
    return pl.pallas_call(body, name="rs_add_sibling", grid_spec=gs, out_shape=SDS((N_CHIPS, HALF, D), BF16),
                          compiler_params=pltpu.CompilerParams(dimension_semantics=("parallel", "parallel"),
                                                               vmem_limit_bytes=48 << 20))(cidx, gpk, got)


def _rs_chips(part, small):
    def body(p_ref, s_ref, o_ref, so_ref, send_sems, recv_sems, ssend_sems, srecv_sems, local_sem):
        x, y, c, chips = _place()
        me = 2 * x + y
        mine_s = pltpu.make_async_copy(s_ref, so_ref.at[4 * x + 2 * y + c], local_sem)
        mine_s.start()
        sends = []
        for k, (cx, cy) in enumerate(chips):
            sends.append(pltpu.make_async_remote_copy(src_ref=p_ref.at[2 * cx + cy], dst_ref=o_ref.at[me], send_sem=send_sems.at[k],
                                                      recv_sem=recv_sems.at[k], device_id=(cx, cy, c), device_id_type=MESH))
        peers = [(x, y, 1 - c)] + [(cx, cy, c) for cx, cy in chips] + [(cx, cy, 1 - c) for cx, cy in chips]
        for k, to in enumerate(peers):
            sends.append(pltpu.make_async_remote_copy(src_ref=s_ref, dst_ref=so_ref.at[4 * x + 2 * y + c], send_sem=ssend_sems.at[k],
                                                      recv_sem=srecv_sems.at[k], device_id=to, device_id_type=MESH))
        for cp in sends:
            cp.start()
        for k, (cx, cy) in enumerate(chips):
            slot = o_ref.at[2 * cx + cy]
            pltpu.make_async_remote_copy(src_ref=slot, dst_ref=slot, send_sem=send_sems.at[k], recv_sem=recv_sems.at[k],
                                         device_id=(x, y, c), device_id_type=MESH).wait_recv()
        for k, (px, py, pc) in enumerate(peers):
            slot = so_ref.at[4 * px + 2 * py + pc]
            pltpu.make_async_remote_copy(src_ref=slot, dst_ref=slot, send_sem=ssend_sems.at[k], recv_sem=srecv_sems.at[k],
                                         device_id=(x, y, c), device_id_type=MESH).wait_recv()
        for cp in sends:
            cp.wait_send()
        mine_s.wait()

    return _pcall(body, name="rs_chips", in_specs=[ANY, ANY], out_specs=[ANY, ANY],
                  out_shape=[SDS((N_CHIPS, HALF, D), part.dtype), SDS((N_DEV, 8, D), F32)],
                  scratch=[pltpu.SemaphoreType.DMA((3,)), pltpu.SemaphoreType.DMA((3,)), pltpu.SemaphoreType.DMA((7,)),
                           pltpu.SemaphoreType.DMA((7,)), pltpu.SemaphoreType.DMA])(part, small)


def _rs_add_chips(qidx, part, parts):
    th = HALF // 7

    def body(q_ref, own_ref, p_ref, o_ref):
        for me in range(N_CHIPS):
            @pl.when(q_ref[0] == me)
            def _(me=me):
                t = [(own_ref[0] if j == me else p_ref[j]).astype(F32) for j in range(N_CHIPS)]
                o_ref[...] = ((t[0] + t[1]) + t[2]) + t[3]

    gs = pltpu.PrefetchScalarGridSpec(
        num_scalar_prefetch=1, grid=(HALF // th,),
        in_specs=[pl.BlockSpec((1, th, D), lambda i, q: (q[0], i, 0)), pl.BlockSpec((N_CHIPS, th, D), lambda i, q: (0, i, 0))],
        out_specs=pl.BlockSpec((th, D), lambda i, q: (i, 0)))
    return pl.pallas_call(body, name="rs_add_chips", grid_spec=gs, out_shape=SDS((HALF, D), F32),
                          compiler_params=pltpu.CompilerParams(dimension_semantics=("parallel",),
                                                               vmem_limit_bytes=48 << 20))(qidx, part, parts)


def _rs_join(total):
    def body(in_ref, out_ref, send_sem, recv_sem):
        x, y, c, _ = _place()
        cp = pltpu.make_async_remote_copy(src_ref=in_ref, dst_ref=out_ref, send_sem=send_sem, recv_sem=recv_sem,
                                          device_id=(x, y, 1 - c), device_id_type=MESH)
        cp.start()
        cp.wait()

    return _pcall(body, name="rs_join", in_specs=[ANY], out_specs=ANY, out_shape=SDS((HALF, D), F32),
                  scratch=[pltpu.SemaphoreType.DMA, pltpu.SemaphoreType.DMA])(total)


def _pack_shards(b, dtype):
    b = {n: b[n].astype(dtype) for n, _, _ in BIG}
    lanes = lambda a: jnp.pad(a, ((0, 0), (0, D - a.shape[1])))
    pair = jnp.concatenate([b["w_q_b"], b["w_ple"], jnp.zeros((256, D - 640), dtype)], axis=1)
    return jnp.concatenate([lanes(b["w_in"]), pair, b["w_mla_up"], b["w_swa_up"], b["w_out"], b["w_ple_gate"],
                            b["w_mlp_up"], b["w_mlp_down"], lanes(b["w_kv_b"])], axis=0)


def _unpack_shards(pk):
    return {n: pk[PACK_AT[n][0]:PACK_AT[n][0] + r, PACK_AT[n][1]:PACK_AT[n][1] + c] for n, r, c in BIG}


def _full_weights(gathered, own, chip):
    per_chip = [_unpack_shards(jnp.where(chip == j, own, gathered[j])) for j in range(N_CHIPS)]
    return {n: jnp.concatenate([pc[n] for pc in per_chip], axis=1 if n in COL_SHARDED else 0) for n, _, _ in BIG}


def _split_full_grads(grads, dtype):
    chunks = []
    for j in range(N_CHIPS):
        blocks = {}
        for n, r, c in BIG:
            g = grads[n]
            blocks[n] = g[:, j * c:(j + 1) * c] if n in COL_SHARDED else g[j * r:(j + 1) * r]
        chunks.append(_pack_shards(blocks, dtype))
    return jnp.stack(chunks)


def _w_in_internal(w):
    z = lambda n: jnp.zeros((w.shape[0], n), w.dtype)
    sk0, sk1 = w[:, 1440:1504], w[:, 1504:1568]
    sv0, sv1 = w[:, 1568:1632], w[:, 1632:1696]
    return jnp.concatenate([w[:, 416:1440], w[:, 1696:3744], w[:, 0:256], sk0, sk0, sk1, sk1, sv0, sv0, sv1, sv1,
                            w[:, 256:384], z(64), w[:, 384:416], z(32)], axis=1)


def _w_in_external_grad(g):
    sk = [g[:, 3328 + 128 * j:3392 + 128 * j] + g[:, 3392 + 128 * j:3456 + 128 * j] for j in range(2)]
    sv = [g[:, 3584 + 128 * j:3648 + 128 * j] + g[:, 3648 + 128 * j:3712 + 128 * j] for j in range(2)]
    return jnp.concatenate([g[:, 3072:3328], g[:, 3840:3968], g[:, 4032:4064], g[:, 0:1024], *sk, *sv, g[:, 1024:3072]], axis=1)


def _local_step(x, p, tgt, w, small):
    T = x.shape[0]
    tm = 256
    tb = 256
    w_in_p = _w_in_internal(w["w_in"])
    wqb = jnp.pad(w["w_q_b"].reshape(Q_LORA, MLA_HEADS, 96), ((0, 0), (0, 0), (0, 32))).reshape(Q_LORA, 2048)
    wkv = w["w_kv_b"].reshape(KV_LORA, MLA_HEADS, 128)
    wkn = jnp.pad(wkv[:, :, :64], ((0, 0), (0, 0), (0, 64))).reshape(KV_LORA, 2048)
    wv = wkv[:, :, 64:].reshape(KV_LORA, 1024)
    tab_m = _rope_tables(T, "mla")
    tab_s = _rope_tables(T, "swa")
    g1, gq, gkv, sinks = small["g_mix_pre"], small["g_q_a"], small["g_kv_a"], small["sinks"]
    g2, g3, g4, g5 = small["g_mix_post"], small["g_mlp_pre"], small["g_mlp_post"], small["g_ple"]
    sink_vec = sinks.reshape(SWA_HEADS)

    z, h1 = _fwd_in(x, g1, w_in_p, tm)
    qn, kvn, qm, km, vm, kt, vt, qs, ks, vs = _fwd_qkv(z, gq, gkv, wqb, wkn, wv, tab_m, tab_s, tb)
    om, lse_m = _mla_fwd(qm, km, vt, tb)
    os_, lse_s = _swa_fwd(sink_vec, qs, ks, vs)
    y, yo, au, bu, x1 = _fwd_mix(om, os_, z, x, w["w_mla_up"], w["w_swa_up"], w["w_out"], g2, tm)
    h2, a, u = _fwd_mlp_up(x1, g3, w["w_mlp_up"], tm)
    d, x2 = _fwd_mlp_down(u, w["w_mlp_down"], x1, g4, tm)
    loss, dx2, dgt, de0, dg5 = _ple_fwd_bwd(p, x2, tgt, w["w_ple"], g5, w["w_ple_gate"], tm)

    dd, da, dg4 = _bwd_mlp_down(dx2, d, g4, w["w_mlp_down"], a, tm)
    dx1, dg3 = _bwd_mlp_up(da, w["w_mlp_up"], x1, g3, dx2, tm)
    dyo, dg2, dau, dbu, dga, dgb, dom, dos, delta_m = _bwd_mix(dx1, yo, g2, w["w_out"], z, au, bu, w["w_mla_up"],
                                                                w["w_swa_up"], om, tb)
    dqm, dkm, dvm = _mla_bwd(qm, km, kt, vm, dom, lse_m, delta_m, tb)
    dqs, dkc, dkp, dvc, dvp, dsink = _swa_bwd(sink_vec, qs, ks, vs, dos, os_, lse_s)
    dqb, dknb, dvb, dsq, drest, dgq, dgkv = _bwd_qkv(dqm, dkm, dvm, dqs, dkc, dkp, dvc, dvp, z, gq, gkv, wqb, wkn, wv,
                                                      tab_m, tab_s)
    gx, dg1 = _bwd_in(dsq, dga, dgb, drest, w_in_p, x, g1, dx1, tm)

    g_in_p = jnp.concatenate([_wgrad(h1, dsq, "wgrad_in_sq"), _wgrad(h1, dga, "wgrad_in_ga"), _wgrad(h1, dgb, "wgrad_in_gb"),
                              _wgrad(h1, drest, "wgrad_in_rest")], axis=1)
    g_qb_p = _wgrad(qn, dqb, "wgrad_q_b")
    g_kn_p = _wgrad(kvn, dknb, "wgrad_kv_b_nope")
    g_v_p = _wgrad(kvn, dvb, "wgrad_kv_b_v")
    grads = {
        "w_in": _w_in_external_grad(g_in_p),
        "w_q_b": g_qb_p.reshape(Q_LORA, MLA_HEADS, 128)[:, :, :96].reshape(Q_LORA, 1536),
        "w_kv_b": jnp.concatenate([g_kn_p.reshape(KV_LORA, MLA_HEADS, 128)[:, :, :64], g_v_p.reshape(KV_LORA, MLA_HEADS, 64)],
                                  axis=2).reshape(KV_LORA, 2048),
        "w_mla_up": _wgrad(om, dau, "wgrad_mla_up"),
        "w_swa_up": _wgrad(os_, dbu, "wgrad_swa_up"),
        "w_out": _wgrad(y, dyo, "wgrad_out"),
        "w_mlp_up": _wgrad(h2, da, "wgrad_mlp_up"),
        "w_mlp_down": _wgrad(u, dd, "wgrad_mlp_down"),
        "w_ple": _wgrad(p, de0, "wgrad_ple"),
        "w_ple_gate": _wgrad(x2, dgt, "wgrad_ple_gate"),
    }
    small_grads = {"g_mix_pre": dg1, "g_q_a": dgq, "g_kv_a": dgkv, "sinks": dsink[0:1, 0:SWA_HEADS], "g_mix_post": dg2,
                   "g_mlp_pre": dg3, "g_mlp_post": dg4, "g_ple": dg5}
    return loss, gx, grads, small_grads


def _pack_small(vals, fill):
    wide = [vals[n] for n, k in SMALL if k == D]
    narrow = [vals[n] for n, k in SMALL if k != D]
    used = sum(k for _, k in SMALL if k != D)
    last = jnp.concatenate(narrow + [jnp.full((1, D - used), fill, F32)], axis=1)
    return jnp.concatenate(wide + [last, jnp.full((2, D), fill, F32)], axis=0)


def _unpack_small(pk):
    out, row, off = {}, 0, 0
    for n, k in SMALL:
        if k == D:
            out[n] = pk[row:row + 1]
            row += 1
    for n, k in SMALL:
        if k != D:
            out[n] = pk[5:6, off:off + k]
            off += k
    return out


def kernel(x, p, g_mix_pre, w_in, g_q_a, w_q_b, g_kv_a, w_kv_b, sinks, w_mla_up, w_swa_up, w_out, g_mix_post, g_mlp_pre, w_mlp_up, w_mlp_down, g_mlp_post, w_ple, g_ple, w_ple_gate, loss_target, m_g_mix_pre, m_w_in, m_g_q_a, m_w_q_b, m_g_kv_a, m_w_kv_b, m_sinks, m_w_mla_up, m_w_swa_up, m_w_out, m_g_mix_post, m_g_mlp_pre, m_w_mlp_up, m_w_mlp_down, m_g_mlp_post, m_w_ple, m_g_ple, m_w_ple_gate, v_g_mix_pre, v_w_in, v_g_q_a, v_w_q_b, v_g_kv_a, v_w_kv_b, v_sinks, v_w_mla_up, v_w_swa_up, v_w_out, v_g_mix_post, v_g_mlp_pre, v_w_mlp_up, v_w_mlp_down, v_g_mlp_post, v_w_ple, v_g_ple, v_w_ple_gate):
    given = dict(locals())
    big_w = {n: given[n][0] for n, _, _ in BIG}
    small_w = {n: given[n] for n, _ in SMALL}
    small_m = {n: given["m_" + n] for n, _ in SMALL}
    small_v = {n: given["v_" + n] for n, _ in SMALL}

    core = lax.axis_index("c")
    chip = 2 * lax.axis_index("x") + lax.axis_index("y")
    own_pk = _pack_shards(big_w, BF16)
    weights = _full_weights(_all_gather(own_pk), own_pk, chip)
    loss_blk, gx, grads, small_grads = _local_step(x[0], p[0, 0], loss_target[0], weights, small_w)

    gpk = _split_full_grads(grads, F32)
    got = _rs_sibling(gpk)
    part = _rs_add_sibling(core.astype(jnp.int32).reshape(1), gpk, got)
    parts, small_parts = _rs_chips(part, _pack_small(small_grads, 0.0))
    mine = _rs_add_chips(chip.astype(jnp.int32).reshape(1), part, parts)
    theirs = _rs_join(mine)
    joined = jnp.where(core == 0, jnp.concatenate([mine, theirs]), jnp.concatenate([theirs, mine]))

    loss = lax.psum(loss_blk[0, 0], ("x", "y", "c"))
    g_small_pk, d_small_pk, m_small_pk, v_small_pk = _adamw_small(
        _pack_small(small_w, 0.0), small_parts, _pack_small(small_m, 0.0), _pack_small(small_v, 1.0))
    g_small, d_small = _unpack_small(g_small_pk), _unpack_small(d_small_pk)
    m_small, v_small = _unpack_small(m_small_pk), _unpack_small(v_small_pk)

    out_g, out_d, out_m, out_v = dict(g_small), dict(d_small), dict(m_small), dict(v_small)
    for n, _, _ in BIG:
        out_g[n], out_d[n], out_m[n], out_v[n] = _adamw(given[n], joined, given["m_" + n], given["v_" + n], n)
    order = ["g_mix_pre", "w_in", "g_q_a", "w_q_b", "g_kv_a", "w_kv_b", "sinks", "w_mla_up", "w_swa_up", "w_out", "g_mix_post",
             "g_mlp_pre", "w_mlp_up", "w_mlp_down", "g_mlp_post", "w_ple", "g_ple", "w_ple_gate"]
    return (loss, gx[None], *[out_g[n] for n in order], *[out_d[n] for n in order], *[out_m[n] for n in order],
            *[out_v[n] for n in order])
```

```python
import math

import jax
import jax.numpy as jnp
from jax import lax
from jax.experimental import pallas as pl
from jax.experimental.pallas import tpu as pltpu

F32 = jnp.float32
BF16 = jnp.bfloat16
SDS = jax.ShapeDtypeStruct

D = 1024
D_FF = 4096
PLE = 256
Q_LORA = 256
KV_LORA = 128
MLA_HEADS = 16
MLA_NOPE = 64
MLA_ROPE = 32
SWA_HEADS = 16
SWA_HD = 64
WINDOW = 128
ROPE_THETA = 10000.0
EPS = 1e-6
NEG = -1e30
NZ = 4096
MLA_SCALE = (MLA_NOPE + MLA_ROPE) ** -0.5
LOG2_E = math.log2(math.e)
MLA_LOG2_SCALE = MLA_SCALE * LOG2_E
SWA_SCALE = SWA_HD ** -0.5

ADAM_LR = 0.001
ADAM_B1 = 0.9
ADAM_B2 = 0.999
ADAM_EPS = 1e-08
ADAM_WD = 0.01
ADAM_STEP = 10

LANES = 128
ATT_COLS = 128
N_CHIPS = 4
N_DEV = 8
MESH = pl.DeviceIdType.MESH

NT = (((1,), (1,)), ((), ()))
TN = (((0,), (0,)), ((), ()))

BIG = (("w_in", 1024, 936), ("w_q_b", 256, 384), ("w_kv_b", 128, 512), ("w_mla_up", 256, 1024),
       ("w_swa_up", 256, 1024), ("w_out", 256, 1024), ("w_mlp_up", 1024, 1024), ("w_mlp_down", 1024, 1024),
       ("w_ple", 256, 256), ("w_ple_gate", 256, 1024))
COL_SHARDED = ("w_in", "w_q_b", "w_kv_b", "w_mlp_up", "w_ple")
PACK_AT = {"w_in": ("early", 0, 0), "w_q_b": ("early", 1024, 0), "w_ple": ("early", 1024, 384), "w_kv_b": ("early", 1280, 0),
           "w_mla_up": ("late", 0, 0), "w_swa_up": ("late", 256, 0), "w_out": ("late", 512, 0), "w_ple_gate": ("late", 768, 0),
           "w_mlp_up": ("late", 1024, 0), "w_mlp_down": ("late", 2048, 0)}
PACK_ROWS = {"early": 1408, "late": 3072}
SMALL = (("g_mix_pre", 1024), ("g_q_a", 256), ("g_kv_a", 128), ("sinks", 16), ("g_mix_post", 1024),
         ("g_mlp_pre", 1024), ("g_mlp_post", 1024), ("g_ple", 1024))


def _dot(a, b):
    return jnp.dot(a, b, preferred_element_type=F32)


def _dot_nt(a, b):
    return lax.dot_general(a, b, NT, preferred_element_type=F32)


def _dot_tn(a, b):
    return lax.dot_general(a, b, TN, preferred_element_type=F32)


def _pcall(body, *, name, out_shape, grid=(), in_specs=None, out_specs=None, scratch=(), sem=None, vmem_mb=48):
    params = dict(vmem_limit_bytes=vmem_mb << 20)
    if sem is not None:
        params["dimension_semantics"] = sem
    return pl.pallas_call(body, name=name, grid=grid, in_specs=in_specs, out_specs=out_specs, out_shape=out_shape,
                          scratch_shapes=list(scratch), compiler_params=pltpu.CompilerParams(**params))


def _rows(tm, n, col=0):
    return pl.BlockSpec((tm, n), lambda i: (i, col))


def _full(shape):
    return pl.BlockSpec(shape, lambda i: (0,) * len(shape))


def _rms(x, g):
    r = lax.rsqrt(jnp.mean(x * x, axis=-1, keepdims=True) + EPS)
    return x * r * g


def _rms_bwd(dy, x, g):
    r = lax.rsqrt(jnp.mean(x * x, axis=-1, keepdims=True) + EPS)
    xn = x * r
    dn = dy * g
    dx = r * (dn - xn * jnp.mean(dn * xn, axis=-1, keepdims=True))
    return dx, jnp.sum(dy * xn, axis=0, keepdims=True)


def _sigmoid(x):
    return 1.0 / (1.0 + jnp.exp(-x))


def _rope(x, c, a, b, half):
    return x * c + pltpu.roll(x, LANES - half, 1) * a + pltpu.roll(x, half, 1) * b


def _rope_tables(T, kind):
    lane = jnp.arange(LANES)
    if kind == "mla":
        half = MLA_ROPE // 2
        rel = lane - MLA_NOPE
        on = (rel >= 0) & (rel < MLA_ROPE)
        d = MLA_ROPE
    else:
        half = SWA_HD // 2
        rel = lane % SWA_HD
        on = jnp.ones((LANES,), bool)
        d = SWA_HD
    first = on & (rel < half)
    second = on & (rel >= half)
    f = jnp.where(first, rel, rel - half).astype(F32)
    inv = jnp.exp(-math.log(ROPE_THETA) * f * (2.0 / d))
    ang = jnp.arange(T, dtype=F32)[:, None] * inv[None, :]
    cos, sin = jnp.cos(ang), jnp.sin(ang)
    c = jnp.where(on[None], cos, 1.0)
    a = jnp.where(first[None], -sin, 0.0)
    b = jnp.where(second[None], sin, 0.0)
    return c, a, b


def _fwd_in(x, g1, w_in_p, tm):
    T = x.shape[0]

    def body(x_ref, g_ref, w_ref, z_ref, h_ref):
        h = _rms(x_ref[...], g_ref[...]).astype(BF16)
        h_ref[...] = h
        z_ref[...] = _dot(h, w_ref[...])

    return _pcall(body, name="fwd_in", grid=(T // tm,),
                  in_specs=[_rows(tm, D), _full((1, D)), _full((D, NZ))],
                  out_specs=[_rows(tm, NZ), _rows(tm, D)],
                  out_shape=[SDS((T, NZ), F32), SDS((T, D), BF16)], sem=("parallel",))(x, g1, w_in_p)


def _fwd_qkv(z, gq, gkv, wqb, wkn, wv, tab_m, tab_s, tm):
    T = z.shape[0]

    def body(qa_ref, sq_ref, skd_ref, svd_ref, kva_ref, kr_ref, gq_ref, gkv_ref, wqb_ref, wkn_ref, wv_ref,
             cm_ref, am_ref, bm_ref, cs_ref, as_ref, bs_ref,
             qn_ref, kvn_ref, qm_ref, km_ref, vm_ref, kt_ref, vt_ref, qs_ref, ks_ref, vs_ref):
        qn = _rms(qa_ref[...], gq_ref[...]).astype(BF16)
        qn_ref[...] = qn
        kvn = _rms(kva_ref[...], gkv_ref[...]).astype(BF16)
        kvn_ref[...] = kvn
        cm, am, bm = cm_ref[...], am_ref[...], bm_ref[...]
        cs, as_, bs = cs_ref[...], as_ref[...], bs_ref[...]
        k_rope = _rope(kr_ref[...], cm, am, bm, MLA_ROPE // 2)
        for j in range(D // LANES):
            sl = slice(LANES * j, LANES * (j + 1))
            v = _dot(kvn, wv_ref[:, sl])
            vm_ref[:, sl] = v.astype(BF16)
            vt_ref[0, sl, :] = v.T.astype(BF16)
        for h in range(MLA_HEADS):
            sl = slice(LANES * h, LANES * (h + 1))
            qh = _dot(qn, wqb_ref[:, sl])
            qm_ref[:, sl] = _rope(qh, cm, am, bm, MLA_ROPE // 2).astype(BF16)
            k = _dot(kvn, wkn_ref[:, sl]) + k_rope
            km_ref[:, sl] = k.astype(BF16)
            kt_ref[0, sl, :] = k.T.astype(BF16)
        for j in range(D // LANES):
            sl = slice(LANES * j, LANES * (j + 1))
            qs_ref[:, sl] = _rope(sq_ref[:, sl], cs, as_, bs, SWA_HD // 2).astype(BF16)
        for j in range(2):
            sl = slice(LANES * j, LANES * (j + 1))
            ks_ref[:, sl] = _rope(skd_ref[:, sl], cs, as_, bs, SWA_HD // 2).astype(BF16)
        vs_ref[...] = svd_ref[...].astype(BF16)

    tab = [_rows(tm, LANES)] * 6
    return _pcall(body, name="fwd_qkv", grid=(T // tm,),
                  in_specs=[_rows(tm, 256, 12), _rows(tm, 1024, 0), _rows(tm, 256, 13), _rows(tm, 256, 14),
                            _rows(tm, 128, 30), _rows(tm, 128, 31), _full((1, Q_LORA)), _full((1, KV_LORA)),
                            _full((Q_LORA, 2048)), _full((KV_LORA, 2048)), _full((KV_LORA, 1024))] + tab,
                  out_specs=[_rows(tm, Q_LORA), _rows(tm, KV_LORA), _rows(tm, 2048), _rows(tm, 2048), _rows(tm, 1024),
                             pl.BlockSpec((1, 2048, tm), lambda i: (i, 0, 0)), pl.BlockSpec((1, 1024, tm), lambda i: (i, 0, 0)),
                             _rows(tm, 1024), _rows(tm, 256), _rows(tm, 256)],
                  out_shape=[SDS((T, Q_LORA), BF16), SDS((T, KV_LORA), BF16), SDS((T, 2048), BF16), SDS((T, 2048), BF16),
                             SDS((T, 1024), BF16), SDS((T // tm, 2048, tm), BF16), SDS((T // tm, 1024, tm), BF16),
                             SDS((T, 1024), BF16), SDS((T, 256), BF16), SDS((T, 256), BF16)],
                  sem=("parallel",))(z, z, z, z, z, z, gq, gkv, wqb, wkn, wv, *tab_m, *tab_s)


def _mla_fwd(qm, km, vt, tb):
    T = qm.shape[0]
    nb = T // tb
    cc = ATT_COLS

    def body(q_ref, k_ref, vt_ref, o_ref, l_ref, s_ref, p_ref, m_ref, d_ref, acc_ref):
        i = pl.program_id(1)
        m_ref[...] = jnp.full(m_ref.shape, NEG, F32)
        d_ref[...] = jnp.zeros_like(d_ref)
        acc_ref[...] = jnp.zeros_like(acc_ref)
        key = lax.broadcasted_iota(jnp.int32, (tb, cc), 0)
        qry = lax.broadcasted_iota(jnp.int32, (tb, cc), 1)

        def scores(j, slot):
            off = pl.multiple_of(j * tb, tb)
            for hh in range(2):
                sl = slice(LANES * hh, LANES * (hh + 1))
                s_ref[slot, hh] = _dot_nt(k_ref[pl.ds(off, tb), sl], q_ref[:, sl])

        def softmax_pv(j, slot, diagonal):
            for hh in range(2):
                for c in range(tb // cc):
                    cols = slice(cc * c, cc * (c + 1))
                    if diagonal:
                        t = jnp.where(key <= qry + cc * c, s_ref[slot, hh, :, cols] * MLA_LOG2_SCALE, NEG)
                        top = jnp.max(t, axis=0, keepdims=True)
                    else:
                        top = jnp.max(s_ref[slot, hh, :, cols], axis=0, keepdims=True) * MLA_LOG2_SCALE
                    m_old = m_ref[hh, :, cols]
                    mn = jnp.maximum(m_old, top)
                    al = jnp.exp2(m_old - mn)
                    if diagonal:
                        p = jnp.exp2(t - mn)
                    else:
                        p = jnp.exp2(s_ref[slot, hh, :, cols] * MLA_LOG2_SCALE - mn)
                    m_ref[hh, :, cols] = mn
                    d_ref[hh, :, cols] = al * d_ref[hh, :, cols] + jnp.sum(p, axis=0, keepdims=True)
                    acc_ref[hh, :, cols] = al * acc_ref[hh, :, cols]
                    p_ref[hh, :, cols] = p.astype(BF16)
            v_t = vt_ref[j]
            for hh in range(2):
                acc_ref[hh] += _dot(v_t, p_ref[hh])

        def step(t, carry):
            scores(2 * t + 1, 1)
            softmax_pv(2 * t, 0, False)
            scores(2 * t + 2, 0)
            softmax_pv(2 * t + 1, 1, False)
            return carry

        scores(0, 0)
        lax.fori_loop(0, i // 2, step, 0)

        @pl.when(i % 2 == 1)
        def _():
            scores(i, 1)
            softmax_pv(i - 1, 0, False)
            softmax_pv(i, 1, True)

        @pl.when(i % 2 == 0)
        def _():
            softmax_pv(i, 0, True)
        first = lax.broadcasted_iota(jnp.int32, (LANES, tb), 0) < 64
        o_ref[...] = jnp.where(first, acc_ref[0] / d_ref[0], acc_ref[1] / d_ref[1]).T
        sub = lax.broadcasted_iota(jnp.int32, (8, tb), 0)
        lse = [m_ref[hh] + jnp.log(d_ref[hh]) * LOG2_E for hh in range(2)]
        l_ref[0, 0] = jnp.where(sub == 0, lse[0], jnp.where(sub == 1, lse[1], 0.0))

    return _pcall(body, name="mla_fwd", grid=(MLA_HEADS // 2, nb),
                  in_specs=[pl.BlockSpec((tb, 256), lambda p, i: (i, p)), pl.BlockSpec((T, 256), lambda p, i: (0, p)),
                            pl.BlockSpec((nb, LANES, tb), lambda p, i: (0, p, 0))],
                  out_specs=[pl.BlockSpec((tb, LANES), lambda p, i: (i, p)),
                             pl.BlockSpec((1, 1, 8, tb), lambda p, i: (p, i, 0, 0))],
                  out_shape=[SDS((T, D), F32), SDS((MLA_HEADS // 2, nb, 8, tb), F32)],
                  scratch=[pltpu.VMEM((2, 2, tb, tb), F32), pltpu.VMEM((2, tb, tb), BF16), pltpu.VMEM((2, 1, tb), F32),
                           pltpu.VMEM((2, 1, tb), F32), pltpu.VMEM((2, LANES, tb), F32)],
                  sem=("parallel", "arbitrary"))(qm, km, vt)


def _swa_mask(n):
    row = lax.broadcasted_iota(jnp.int32, (WINDOW, 2 * WINDOW), 0)
    col = lax.broadcasted_iota(jnp.int32, (WINDOW, 2 * WINDOW), 1)
    rel = row - col + WINDOW
    return (rel >= 0) & (rel < WINDOW) & ((col >= WINDOW) | (n > 0))


def _swa_specs(T):
    nb = T // WINDOW
    cur = lambda w: pl.BlockSpec((WINDOW, w), lambda n: (n, 0))
    prev = lambda w: pl.BlockSpec((WINDOW, w), lambda n: (jnp.maximum(n - 1, 0), 0))
    return nb, cur, prev


def _swa_fwd(sinks, qs, ks, vs):
    T = qs.shape[0]
    nb, cur, prev = _swa_specs(T)

    def body(sink_ref, q_ref, kc_ref, kp_ref, vc_ref, vp_ref, o_ref, l_ref):
        n = pl.program_id(0)
        mask = _swa_mask(n)
        lo = lax.broadcasted_iota(jnp.int32, (WINDOW, LANES), 1) < 64
        for g in range(2):
            gs = slice(LANES * g, LANES * (g + 1))
            kb = jnp.concatenate([kp_ref[:, gs], kc_ref[:, gs]], axis=0)
            vb = jnp.concatenate([vp_ref[:, gs], vc_ref[:, gs]], axis=0)
            for jj in range(4):
                j = 4 * g + jj
                sl = slice(LANES * j, LANES * (j + 1))
                qp = q_ref[:, sl]
                outs, lses = [], []
                for hf in range(2):
                    hm = lo if hf == 0 else jnp.logical_not(lo)
                    qh = jnp.where(hm, qp, jnp.zeros_like(qp))
                    s = jnp.where(mask, _dot_nt(qh, kb) * SWA_SCALE, NEG)
                    sk = sink_ref[2 * j + hf]
                    m = jnp.maximum(jnp.max(s, axis=1, keepdims=True), sk)
                    e = jnp.exp(s - m)
                    den = jnp.sum(e, axis=1, keepdims=True) + jnp.exp(sk - m)
                    p = e / den
                    outs.append(_dot(p.astype(BF16), vb))
                    lses.append(jnp.broadcast_to(m + jnp.log(den), (WINDOW, LANES)))
                o_ref[:, sl] = jnp.where(lo, outs[0], outs[1])
                l_ref[:, sl] = jnp.where(lo, lses[0], lses[1])

    return _pcall(body, name="swa_fwd", grid=(nb,),
                  in_specs=[pl.BlockSpec(memory_space=pltpu.SMEM), cur(D), cur(256), prev(256), cur(256), prev(256)],
                  out_specs=[cur(D), cur(D)], out_shape=[SDS((T, D), F32)] * 2,
                  sem=("parallel",))(sinks, qs, ks, ks, vs, vs)


def _fwd_mix(om, os_, z, x, wmu, wsu, wo, g2, tm):
    T = x.shape[0]

    def body(om_ref, os_ref, ga_ref, gb_ref, x_ref, wmu_ref, wsu_ref, wo_ref, g2_ref,
             y_ref, yo_ref, au_ref, bu_ref, x1_ref):
        au = _dot(om_ref[...].astype(BF16), wmu_ref[...])
        bu = _dot(os_ref[...].astype(BF16), wsu_ref[...])
        au_ref[...] = au
        bu_ref[...] = bu
        y = (_sigmoid(ga_ref[...]) * au + _sigmoid(gb_ref[...]) * bu).astype(BF16)
        y_ref[...] = y
        yo = _dot(y, wo_ref[...])
        yo_ref[...] = yo
        x1_ref[...] = x_ref[...] + _rms(yo, g2_ref[...])

    r = _rows(tm, D)
    w = _full((D, D))
    return _pcall(body, name="fwd_mix", grid=(T // tm,),
                  in_specs=[r, r, _rows(tm, D, 1), _rows(tm, D, 2), r, w, w, w, _full((1, D))],
                  out_specs=[r] * 5,
                  out_shape=[SDS((T, D), BF16), SDS((T, D), F32), SDS((T, D), F32), SDS((T, D), F32), SDS((T, D), F32)],
                  sem=("parallel",))(om, os_, z, z, x, wmu, wsu, wo, g2)


def _fwd_mlp_up(x1, g3, w1, tm):
    T = x1.shape[0]

    def body(x_ref, g_ref, w_ref, h_ref, a_ref, u_ref):
        h = _rms(x_ref[...], g_ref[...]).astype(BF16)
        h_ref[...] = h
        a = _dot(h, w_ref[...])
        a_ref[...] = a
        u_ref[...] = jnp.square(jnp.maximum(a, 0.0)).astype(BF16)

    return _pcall(body, name="fwd_mlp_up", grid=(T // tm,),
                  in_specs=[_rows(tm, D), _full((1, D)), _full((D, D_FF))],
                  out_specs=[_rows(tm, D), _rows(tm, D_FF), _rows(tm, D_FF)],
                  out_shape=[SDS((T, D), BF16), SDS((T, D_FF), F32), SDS((T, D_FF), BF16)],
                  sem=("parallel",))(x1, g3, w1)


def _fwd_mlp_down(u, w2, x1, g4, tm):
    T = x1.shape[0]

    def body(u_ref, w_ref, x_ref, g_ref, d_ref, x2_ref):
        d = _dot(u_ref[...], w_ref[...])
        d_ref[...] = d
        x2_ref[...] = x_ref[...] + _rms(d, g_ref[...])

    return _pcall(body, name="fwd_mlp_down", grid=(T // tm,),
                  in_specs=[_rows(tm, D_FF), _full((D_FF, D)), _rows(tm, D), _full((1, D))],
                  out_specs=[_rows(tm, D), _rows(tm, D)], out_shape=[SDS((T, D), F32)] * 2,
                  sem=("parallel",))(u, w2, x1, g4)


def _ple_fwd_bwd(p, x2, tgt, wple, g5, wpg, tm):
    T = x2.shape[0]

    def body(p_ref, x2_ref, t_ref, wple_ref, g5_ref, wpg_ref, loss_ref, dx2_ref, dgt_ref, de0_ref, dg5_ref):
        @pl.when(pl.program_id(0) == 0)
        def _():
            loss_ref[...] = jnp.zeros_like(loss_ref)
            dg5_ref[...] = jnp.zeros_like(dg5_ref)

        e0 = _dot(p_ref[...].astype(BF16), wple_ref[...])
        g5 = g5_ref[...]
        r = lax.rsqrt(jnp.mean(e0 * e0, axis=-1, keepdims=True) + EPS)
        en = e0 * r
        e = en * g5
        x2 = x2_ref[...]
        s = _sigmoid(_dot(x2.astype(BF16), wpg_ref[...]))
        diff = x2 + s * e - t_ref[...]
        sq = jnp.sum(jnp.sum(diff * diff, axis=1, keepdims=True), axis=0, keepdims=True)
        loss_ref[...] += jnp.broadcast_to(sq * (0.5 / D), loss_ref.shape)
        dx3 = diff * (1.0 / D)
        de = dx3 * s
        dgt = (dx3 * e * s * (1.0 - s)).astype(BF16)
        dgt_ref[...] = dgt
        dn = de * g5
        de0_ref[...] = (r * (dn - en * jnp.mean(dn * en, axis=-1, keepdims=True))).astype(BF16)
        dg5_ref[...] += jnp.sum(de * en, axis=0, keepdims=True)
        dx2_ref[...] = dx3 + _dot_nt(dgt, wpg_ref[...])

    r = _rows(tm, D)
    return _pcall(body, name="ple_fwd_bwd", grid=(T // tm,),
                  in_specs=[_rows(tm, PLE), r, r, _full((PLE, D)), _full((1, D)), _full((D, D))],
                  out_specs=[_full((8, LANES)), r, r, r, _full((1, D))],
                  out_shape=[SDS((8, LANES), F32), SDS((T, D), F32), SDS((T, D), BF16), SDS((T, D), BF16), SDS((1, D), F32)],
                  sem=("arbitrary",))(p, x2, tgt, wple, g5, wpg)


def _bwd_mlp_down(dx2, d, g4, w2, a, tm):
    T = dx2.shape[0]

    def body(dx_ref, d_ref, g_ref, w_ref, a_ref, dd_ref, da_ref, dg_ref):
        @pl.when(pl.program_id(0) == 0)
        def _():
            dg_ref[...] = jnp.zeros_like(dg_ref)

        dd, dg = _rms_bwd(dx_ref[...], d_ref[...], g_ref[...])
        dg_ref[...] += dg
        ddb = dd.astype(BF16)
        dd_ref[...] = ddb
        du = _dot_nt(ddb, w_ref[...])
        da_ref[...] = (du * (2.0 * jnp.maximum(a_ref[...], 0.0))).astype(BF16)

    return _pcall(body, name="bwd_mlp_down", grid=(T // tm,),
                  in_specs=[_rows(tm, D), _rows(tm, D), _full((1, D)), _full((D_FF, D)), _rows(tm, D_FF)],
                  out_specs=[_rows(tm, D), _rows(tm, D_FF), _full((1, D))],
                  out_shape=[SDS((T, D), BF16), SDS((T, D_FF), BF16), SDS((1, D), F32)],
                  sem=("arbitrary",))(dx2, d, g4, w2, a)


def _bwd_mlp_up(da, w1, x1, g3, dx2, tm):
    T = dx2.shape[0]

    def body(da_ref, w_ref, x_ref, g_ref, dx2_ref, dx1_ref, dg_ref):
        @pl.when(pl.program_id(0) == 0)
        def _():
            dg_ref[...] = jnp.zeros_like(dg_ref)

        dh = _dot_nt(da_ref[...], w_ref[...])
        dx, dg = _rms_bwd(dh, x_ref[...], g_ref[...])
        dg_ref[...] += dg
        dx1_ref[...] = dx2_ref[...] + dx

    return _pcall(body, name="bwd_mlp_up", grid=(T // tm,),
                  in_specs=[_rows(tm, D_FF), _full((D, D_FF)), _rows(tm, D), _full((1, D)), _rows(tm, D)],
                  out_specs=[_rows(tm, D), _full((1, D))],
                  out_shape=[SDS((T, D), F32), SDS((1, D), F32)], sem=("arbitrary",))(da, w1, x1, g3, dx2)


def _bwd_mix(dx1, yo, g2, wo, z, au, bu, wmu, wsu, om, tm):
    T = dx1.shape[0]

    def body(dx_ref, yo_ref, g_ref, wo_ref, ga_ref, gb_ref, au_ref, bu_ref, wmu_ref, wsu_ref, om_ref,
             dyo_ref, dg_ref, dau_ref, dbu_ref, dga_ref, dgb_ref, dom_ref, dos_ref, dl_ref):
        @pl.when(pl.program_id(0) == 0)
        def _():
            dg_ref[...] = jnp.zeros_like(dg_ref)

        dyo, dg = _rms_bwd(dx_ref[...], yo_ref[...], g_ref[...])
        dg_ref[...] += dg
        dyob = dyo.astype(BF16)
        dyo_ref[...] = dyob
        dy = _dot_nt(dyob, wo_ref[...])
        sa = _sigmoid(ga_ref[...])
        sb = _sigmoid(gb_ref[...])
        dau = (dy * sa).astype(BF16)
        dbu = (dy * sb).astype(BF16)
        dau_ref[...] = dau
        dbu_ref[...] = dbu
        dga_ref[...] = (dy * au_ref[...] * sa * (1.0 - sa)).astype(BF16)
        dgb_ref[...] = (dy * bu_ref[...] * sb * (1.0 - sb)).astype(BF16)
        dom = _dot_nt(dau, wmu_ref[...])
        dom_ref[...] = dom
        dos_ref[...] = _dot_nt(dbu, wsu_ref[...])
        prod = dom * om_ref[...]
        sub = lax.broadcasted_iota(jnp.int32, (8, tm), 0)
        for pr in range(MLA_HEADS // 2):
            pt = prod[:, LANES * pr:LANES * (pr + 1)].T
            d0 = jnp.sum(pt[0:64], axis=0, keepdims=True)
            d1 = jnp.sum(pt[64:128], axis=0, keepdims=True)
            dl_ref[pr, 0] = jnp.where(sub == 0, d0, jnp.where(sub == 1, d1, 0.0))

    r = _rows(tm, D)
    w = _full((D, D))
    return _pcall(body, name="bwd_mix", grid=(T // tm,),
                  in_specs=[r, r, _full((1, D)), w, _rows(tm, D, 1), _rows(tm, D, 2), r, r, w, w, r],
                  out_specs=[r, _full((1, D)), r, r, r, r, r, r, pl.BlockSpec((MLA_HEADS // 2, 1, 8, tm), lambda i: (0, i, 0, 0))],
                  out_shape=[SDS((T, D), BF16), SDS((1, D), F32), SDS((T, D), BF16), SDS((T, D), BF16), SDS((T, D), BF16),
                             SDS((T, D), BF16), SDS((T, D), F32), SDS((T, D), F32), SDS((MLA_HEADS // 2, T // tm, 8, tm), F32)],
                  sem=("arbitrary",))(dx1, yo, g2, wo, z, z, au, bu, wmu, wsu, om)


def _mla_bwd(qm, km, kt, vm, do, lse, delta, tb):
    T = qm.shape[0]
    nb = T // tb
    cc = ATT_COLS

    def body(q_ref, k_ref, kt_ref, v_ref, do_ref, l_ref, dl_ref, dqt_ref, dk_ref, dv_ref, s_ref, dp_ref, p_ref, ds_ref, dom_ref):
        j = pl.program_id(1)

        @pl.when(j == 0)
        def _():
            dqt_ref[...] = jnp.zeros_like(dqt_ref)

        dk_ref[...] = jnp.zeros_like(dk_ref)
        dv_ref[...] = jnp.zeros_like(dv_ref)
        lo = lax.broadcasted_iota(jnp.int32, (tb, LANES), 1) < 64
        key = lax.broadcasted_iota(jnp.int32, (tb, cc), 0)
        qry = lax.broadcasted_iota(jnp.int32, (tb, cc), 1)

        def scores(i, slot):
            rows_i = pl.ds(pl.multiple_of(i * tb, tb), tb)
            d_o = do_ref[rows_i, :]
            v = v_ref[...]
            for hh in range(2):
                sl = slice(LANES * hh, LANES * (hh + 1))
                hm = lo if hh == 0 else jnp.logical_not(lo)
                dom_ref[slot, hh] = jnp.where(hm, d_o, 0.0).astype(BF16)
                s_ref[slot, hh] = _dot_nt(k_ref[:, sl], q_ref[rows_i, sl])
                dp_ref[slot, hh] = _dot_nt(v, dom_ref[slot, hh])

        def grads(i, slot, diagonal):
            rows_i = pl.ds(pl.multiple_of(i * tb, tb), tb)
            lse_i = l_ref[0, i]
            delta_i = dl_ref[0, i]
            for hh in range(2):
                for c in range(tb // cc):
                    cols = slice(cc * c, cc * (c + 1))
                    p = jnp.exp2(s_ref[slot, hh, :, cols] * MLA_LOG2_SCALE - lse_i[hh:hh + 1, cols])
                    if diagonal:
                        p = jnp.where(key <= qry + cc * c, p, 0.0)
                    p_ref[hh, :, cols] = p.astype(BF16)
                    ds_ref[hh, :, cols] = (p * (dp_ref[slot, hh, :, cols] - delta_i[hh:hh + 1, cols]) * MLA_SCALE).astype(BF16)
            for hh in range(2):
                sl = slice(LANES * hh, LANES * (hh + 1))
                dv_ref[...] += _dot(p_ref[hh], dom_ref[slot, hh])
                dk_ref[:, sl] += _dot(ds_ref[hh], q_ref[rows_i, sl])
                dqt_ref[i, sl, :] += _dot(kt_ref[0, sl, :], ds_ref[hh])

        n_off = nb - 1 - j

        def step(u, carry):
            i0 = j + 1 + 2 * u
            scores(i0 + 1, 1)
            grads(i0, 0, False)
            scores(jnp.where(i0 + 2 < nb, i0 + 2, j), 0)
            grads(i0 + 1, 1, False)
            return carry

        scores(jnp.where(n_off > 0, j + 1, j), 0)
        lax.fori_loop(0, n_off // 2, step, 0)

        @pl.when(n_off % 2 == 1)
        def _():
            scores(j, 1)
            grads(nb - 1, 0, False)
            grads(j, 1, True)

        @pl.when(n_off % 2 == 0)
        def _():
            grads(j, 0, True)

    pair = lambda w: pl.BlockSpec((T, w), lambda p, j: (0, p))
    blk = lambda w: pl.BlockSpec((tb, w), lambda p, j: (j, p))
    stat = pl.BlockSpec((1, nb, 8, tb), lambda p, j: (p, 0, 0, 0))
    return _pcall(body, name="mla_bwd", grid=(MLA_HEADS // 2, nb),
                  in_specs=[pair(256), blk(256), pl.BlockSpec((1, 256, tb), lambda p, j: (j, p, 0)), blk(LANES), pair(LANES),
                            stat, stat],
                  out_specs=[pl.BlockSpec((nb, 256, tb), lambda p, j: (0, p, 0)), blk(256), blk(LANES)],
                  out_shape=[SDS((nb, 2048, tb), F32), SDS((T, 2048), F32), SDS((T, D), F32)],
                  scratch=[pltpu.VMEM((2, 2, tb, tb), F32), pltpu.VMEM((2, 2, tb, tb), F32), pltpu.VMEM((2, tb, tb), BF16),
                           pltpu.VMEM((2, tb, tb), BF16), pltpu.VMEM((2, 2, tb, LANES), BF16)],
                  sem=("parallel", "arbitrary"))(qm, km, kt, vm, do, lse, delta)


def _swa_bwd(sinks, qs, ks, vs, do, o, lse):
    T = qs.shape[0]
    nb, cur, prev = _swa_specs(T)

    def body(sink_ref, q_ref, kc_ref, kp_ref, vc_ref, vp_ref, do_ref, o_ref, l_ref,
             dq_ref, dkc_ref, dkp_ref, dvc_ref, dvp_ref, dsink_ref):
        n = pl.program_id(0)

        @pl.when(n == 0)
        def _():
            dsink_ref[...] = jnp.zeros_like(dsink_ref)

        mask = _swa_mask(n)
        lo = lax.broadcasted_iota(jnp.int32, (WINDOW, LANES), 1) < 64
        lane8 = lax.broadcasted_iota(jnp.int32, (8, LANES), 1)
        dsink = jnp.zeros((8, LANES), F32)
        for g in range(2):
            gs = slice(LANES * g, LANES * (g + 1))
            kb = jnp.concatenate([kp_ref[:, gs], kc_ref[:, gs]], axis=0)
            vb = jnp.concatenate([vp_ref[:, gs], vc_ref[:, gs]], axis=0)
            dkb = jnp.zeros((2 * WINDOW, LANES), F32)
            dvb = jnp.zeros((2 * WINDOW, LANES), F32)
            for jj in range(4):
                j = 4 * g + jj
                sl = slice(LANES * j, LANES * (j + 1))
                qp = q_ref[:, sl]
                d_o = do_ref[:, sl]
                prod = d_o * o_ref[:, sl]
                lse_b = l_ref[:, sl]
                dqs = []
                for hf in range(2):
                    hm = lo if hf == 0 else jnp.logical_not(lo)
                    qh = jnp.where(hm, qp, jnp.zeros_like(qp))
                    s = jnp.where(mask, _dot_nt(qh, kb) * SWA_SCALE, NEG)
                    lse_h = jnp.max(jnp.where(hm, lse_b, -jnp.inf), axis=1, keepdims=True)
                    p = jnp.exp(s - lse_h)
                    dom = jnp.where(hm, d_o, 0.0).astype(BF16)
                    dp = _dot_nt(dom, vb)
                    delta = jnp.sum(jnp.where(hm, prod, 0.0), axis=1, keepdims=True)
                    ds = (p * (dp - delta) * SWA_SCALE).astype(BF16)
                    p_sink = jnp.exp(sink_ref[2 * j + hf] - lse_h)
                    d_sink = -jnp.sum(p_sink * delta, axis=0, keepdims=True)
                    dsink = dsink + jnp.where(lane8 == 2 * j + hf, d_sink, 0.0)
                    dvb = dvb + _dot_tn(p.astype(BF16), dom)
                    dkb = dkb + _dot_tn(ds, qh)
                    dqs.append(_dot(ds, kb))
                dq_ref[:, sl] = jnp.where(lo, dqs[0], dqs[1])
            dkp_ref[:, gs] = dkb[:WINDOW]
            dkc_ref[:, gs] = dkb[WINDOW:]
            dvp_ref[:, gs] = dvb[:WINDOW]
            dvc_ref[:, gs] = dvb[WINDOW:]
        dsink_ref[...] += dsink

    return _pcall(body, name="swa_bwd", grid=(nb,),
                  in_specs=[pl.BlockSpec(memory_space=pltpu.SMEM), cur(D), cur(256), prev(256), cur(256), prev(256),
                            cur(D), cur(D), cur(D)],
                  out_specs=[cur(D), cur(256), cur(256), cur(256), cur(256), _full((8, LANES))],
                  out_shape=[SDS((T, D), F32), SDS((T, 256), F32), SDS((T, 256), F32), SDS((T, 256), F32), SDS((T, 256), F32),
                             SDS((8, LANES), F32)],
                  sem=("arbitrary",))(sinks, qs, ks, ks, vs, vs, do, o, lse)


def _bwd_qkv(dqm, dkm, dvm, dqs, dkc, dkp, dvc, dvp, z, gq, gkv, wqb, wkn, wv, tab_m, tab_s):
    T = z.shape[0]
    tm = WINDOW
    nb = T // tm
    per = dqm.shape[2] // tm

    def body(dqm_ref, dkm_ref, dvm_ref, dqs_ref, dkc_ref, dkp_ref, dvc_ref, dvp_ref, qa_ref, kva_ref, gq_ref, gkv_ref,
             wqb_ref, wkn_ref, wv_ref, cm_ref, am_ref, bm_ref, cs_ref, as_ref, bs_ref,
             dq_out, dkn_out, dv_out, dsq_ref, drest_ref, dgq_ref, dgkv_ref):
        i = pl.program_id(0)

        @pl.when(i == 0)
        def _():
            dgq_ref[...] = jnp.zeros_like(dgq_ref)
            dgkv_ref[...] = jnp.zeros_like(dgkv_ref)

        cm, am, bm = cm_ref[...], -am_ref[...], -bm_ref[...]
        cs, as_, bs = cs_ref[...], -as_ref[...], -bs_ref[...]
        lane = lax.broadcasted_iota(jnp.int32, (tm, LANES), 1)
        nope = lane < MLA_NOPE
        roped = jnp.logical_and(lane >= MLA_NOPE, lane < MLA_NOPE + MLA_ROPE)
        dkr = jnp.zeros((tm, LANES), F32)
        dqn = jnp.zeros((tm, Q_LORA), F32)
        dkvn = jnp.zeros((tm, KV_LORA), F32)
        for h in range(MLA_HEADS):
            sl = slice(LANES * h, LANES * (h + 1))
            dq_h = _rope(dqm_ref[0, sl, :].T, cm, am, bm, MLA_ROPE // 2).astype(BF16)
            dq_out[:, sl] = dq_h
            dqn = dqn + _dot_nt(dq_h, wqb_ref[:, sl])
            dk_h = dkm_ref[:, sl]
            dkn_h = jnp.where(nope, dk_h, 0.0).astype(BF16)
            dkn_out[:, sl] = dkn_h
            dkvn = dkvn + _dot_nt(dkn_h, wkn_ref[:, sl])
            dkr = dkr + jnp.where(roped, dk_h, 0.0)
        dvb = dvm_ref[...].astype(BF16)
        dv_out[...] = dvb
        dkvn = dkvn + _dot_nt(dvb, wv_ref[...])
        dqa, dgq = _rms_bwd(dqn, qa_ref[...], gq_ref[...])
        dkva, dgkv = _rms_bwd(dkvn, kva_ref[...], gkv_ref[...])
        dgq_ref[...] += dgq
        dgkv_ref[...] += dgkv
        for j in range(D // LANES):
            sl = slice(LANES * j, LANES * (j + 1))
            dsq_ref[:, sl] = _rope(dqs_ref[:, sl], cs, as_, bs, SWA_HD // 2).astype(BF16)
        keep = (i < nb - 1).astype(F32)
        drest_ref[:, 0:256] = dqa.astype(BF16)
        for j in range(2):
            sl = slice(LANES * j, LANES * (j + 1))
            dk = dkc_ref[:, sl] + keep * dkp_ref[:, sl]
            drest_ref[:, 256 + LANES * j:256 + LANES * (j + 1)] = _rope(dk, cs, as_, bs, SWA_HD // 2).astype(BF16)
        drest_ref[:, 512:768] = (dvc_ref[...] + keep * dvp_ref[...]).astype(BF16)
        drest_ref[:, 768:896] = dkva.astype(BF16)
        drest_ref[:, 896:1024] = _rope(dkr, cm, am, bm, MLA_ROPE // 2).astype(BF16)

    nxt = pl.BlockSpec((tm, 256), lambda i: (jnp.minimum(i + 1, nb - 1), 0))
    tab = [_rows(tm, LANES)] * 6
    return _pcall(body, name="bwd_qkv", grid=(nb,),
                  in_specs=[pl.BlockSpec((1, 2048, tm), lambda i: (i // per, 0, i % per)),
                            _rows(tm, 2048), _rows(tm, 1024), _rows(tm, 1024), _rows(tm, 256), nxt,
                            _rows(tm, 256), nxt, _rows(tm, 256, 12), _rows(tm, 128, 30), _full((1, Q_LORA)), _full((1, KV_LORA)),
                            _full((Q_LORA, 2048)), _full((KV_LORA, 2048)), _full((KV_LORA, 1024))] + tab,
                  out_specs=[_rows(tm, 2048), _rows(tm, 2048), _rows(tm, 1024), _rows(tm, 1024), _rows(tm, 1024),
                             _full((1, Q_LORA)), _full((1, KV_LORA))],
                  out_shape=[SDS((T, 2048), BF16), SDS((T, 2048), BF16), SDS((T, 1024), BF16), SDS((T, 1024), BF16),
                             SDS((T, 1024), BF16), SDS((1, Q_LORA), F32), SDS((1, KV_LORA), F32)],
                  sem=("arbitrary",))(dqm, dkm, dvm, dqs, dkc, dkp, dvc, dvp, z, z, gq, gkv, wqb, wkn, wv, *tab_m, *tab_s)


def _bwd_in(dsq, dga, dgb, drest, w_in_p, x, g1, dx1, tm):
    T = x.shape[0]

    def body(a_ref, b_ref, c_ref, d_ref, w_ref, x_ref, g_ref, dx1_ref, dx_ref, dg_ref):
        @pl.when(pl.program_id(0) == 0)
        def _():
            dg_ref[...] = jnp.zeros_like(dg_ref)

        dh = (_dot_nt(a_ref[...], w_ref[:, 0:1024]) + _dot_nt(b_ref[...], w_ref[:, 1024:2048])
              + _dot_nt(c_ref[...], w_ref[:, 2048:3072]) + _dot_nt(d_ref[...], w_ref[:, 3072:4096]))
        dx, dg = _rms_bwd(dh, x_ref[...], g_ref[...])
        dg_ref[...] += dg
        dx_ref[...] = dx1_ref[...] + dx

    r = _rows(tm, D)
    return _pcall(body, name="bwd_in", grid=(T // tm,),
                  in_specs=[r, r, r, r, _full((D, NZ)), r, _full((1, D)), r],
                  out_specs=[r, _full((1, D))], out_shape=[SDS((T, D), F32), SDS((1, D), F32)],
                  sem=("arbitrary",))(dsq, dga, dgb, drest, w_in_p, x, g1, dx1)


def _wgrad(a, g, name):
    T, K = a.shape
    N = g.shape[1]
    tk, tn, tt = min(K, 512), min(N, 1024), min(T, 512)
    assert K % tk == 0 and N % tn == 0 and T % tt == 0, (a.shape, g.shape)

    def body(a_ref, g_ref, o_ref):
        @pl.when(pl.program_id(2) == 0)
        def _():
            o_ref[...] = jnp.zeros_like(o_ref)

        o_ref[...] += _dot_tn(a_ref[...].astype(BF16), g_ref[...].astype(BF16))

    return _pcall(body, name=name, grid=(K // tk, N // tn, T // tt),
                  in_specs=[pl.BlockSpec((tt, tk), lambda k, n, t: (t, k)), pl.BlockSpec((tt, tn), lambda k, n, t: (t, n))],
                  out_specs=pl.BlockSpec((tk, tn), lambda k, n, t: (k, n)), out_shape=SDS((K, N), F32),
                  sem=("parallel", "parallel", "arbitrary"))(a, g)


def _adamw(w, packed_g, m, v, name):
    _, R, C = w.shape
    _, row0, lane0 = PACK_AT[name]
    tr = min(R, 256 if row0 % 256 == 0 else 128)
    assert row0 % tr == 0 and R % tr == 0

    def body(w_ref, g_ref, m_ref, v_ref, go_ref, d_ref, m2_ref, v2_ref):
        g_ = g_ref[:, lane0:lane0 + C]
        go_ref[0] = g_
        m2 = ADAM_B1 * m_ref[0] + (1.0 - ADAM_B1) * g_
        v2 = ADAM_B2 * v_ref[0] + (1.0 - ADAM_B2) * jnp.square(g_)
        m_hat = m2 / (1.0 - ADAM_B1 ** ADAM_STEP)
        v_hat = v2 / (1.0 - ADAM_B2 ** ADAM_STEP)
        d_ref[0] = -ADAM_LR * (m_hat / (jnp.sqrt(v_hat) + ADAM_EPS) + ADAM_WD * w_ref[0])
        m2_ref[0] = m2
        v2_ref[0] = v2

    r = pl.BlockSpec((1, tr, C), lambda i: (0, i, 0))
    return _pcall(body, name="adamw_" + name, grid=(R // tr,),
                  in_specs=[r, pl.BlockSpec((tr, D), lambda i: (row0 // tr + i, 0)), r, r], out_specs=[r] * 4,
                  out_shape=[SDS((1, R, C), F32)] * 4, sem=("parallel",))(w, packed_g, m, v)


def _adamw_small(w, parts, m, v):
    def body(w_ref, p_ref, m_ref, v_ref, g_ref, d_ref, m2_ref, v2_ref):
        g_ = p_ref[0]
        for k in range(1, N_DEV):
            g_ = g_ + p_ref[k]
        g_ref[...] = g_
        m2 = ADAM_B1 * m_ref[...] + (1.0 - ADAM_B1) * g_
        v2 = ADAM_B2 * v_ref[...] + (1.0 - ADAM_B2) * jnp.square(g_)
        m_hat = m2 / (1.0 - ADAM_B1 ** ADAM_STEP)
        v_hat = v2 / (1.0 - ADAM_B2 ** ADAM_STEP)
        d_ref[...] = -ADAM_LR * (m_hat / (jnp.sqrt(v_hat) + ADAM_EPS) + ADAM_WD * w_ref[...])
        m2_ref[...] = m2
        v2_ref[...] = v2

    s = _full((8, D))
    return _pcall(body, name="adamw_small", grid=(1,), in_specs=[s, _full((N_DEV, 8, D)), s, s], out_specs=[s] * 4,
                  out_shape=[SDS((8, D), F32)] * 4, sem=("arbitrary",))(w, parts, m, v)


ANY = pl.BlockSpec(memory_space=pl.ANY)


def _place():
    x, y, c = lax.axis_index("x"), lax.axis_index("y"), lax.axis_index("c")
    chips = [(1 - x, y), (x, 1 - y), (1 - x, 1 - y)]
    return x, y, c, chips


def _all_gather(wpk):
    rows = wpk.shape[0]
    HALF = rows // 2
    assert HALF % 16 == 0

    def body(in_ref, out_ref, send_sems, recv_sems):
        x, y, c, chips = _place()
        half = pl.ds(pl.multiple_of(c * HALF, 16), HALF)
        other = pl.ds(pl.multiple_of((1 - c) * HALF, 16), HALF)

        def copy(k, src, dst, to):
            return pltpu.make_async_remote_copy(src_ref=src, dst_ref=dst, send_sem=send_sems.at[k], recv_sem=recv_sems.at[k],
                                                device_id=to, device_id_type=MESH)

        first = [copy(k, in_ref.at[half], out_ref.at[2 * x + y, half], (cx, cy, c)) for k, (cx, cy) in enumerate(chips)]
        for cp in first:
            cp.start()
        passed = []
        for k, (cx, cy) in enumerate(chips):
            slot = out_ref.at[2 * cx + cy, half]
            copy(k, slot, slot, (x, y, c)).wait_recv()
            fwd = copy(3 + k, slot, slot, (x, y, 1 - c))
            fwd.start()
            passed.append(fwd)
        for k, (cx, cy) in enumerate(chips):
            slot = out_ref.at[2 * cx + cy, other]
            copy(3 + k, slot, slot, (x, y, c)).wait_recv()
        for cp in first + passed:
            cp.wait_send()

    return _pcall(body, name="all_gather_weights", in_specs=[ANY], out_specs=ANY,
                  out_shape=SDS((N_CHIPS, rows, D), BF16),
                  scratch=[pltpu.SemaphoreType.DMA((6,)), pltpu.SemaphoreType.DMA((6,))])(wpk)


HBM = pl.BlockSpec(memory_space=pltpu.HBM)
SEM = pl.BlockSpec(memory_space=pltpu.SEMAPHORE)
DATAFLOW = pltpu.SideEffectType.DATAFLOW_SIDE_EFFECTING


def _in_hbm(a):
    return pltpu.with_memory_space_constraint(a, pltpu.HBM)


def _gather_late_start(wpk, after):
    rows = wpk.shape[0]

    def body(in_ref, land_ref, after_ref, send_sems, recv_sems, in_thru, land_thru, token):
        x, y, c, chips = _place()
        for k, (cx, cy) in enumerate(chips):
            pltpu.make_async_remote_copy(src_ref=in_ref, dst_ref=land_ref.at[2 * x + y], send_sem=send_sems.at[k],
                                         recv_sem=recv_sems.at[k], device_id=(cx, cy, c), device_id_type=MESH).start()
        token[...] = jnp.zeros_like(token)

    return pl.pallas_call(
        body, name="gather_late_start",
        out_shape=(pltpu.SemaphoreType.DMA((3,)), pltpu.SemaphoreType.DMA((3,)), pltpu.HBM(wpk.shape, wpk.dtype),
                   pltpu.HBM((N_CHIPS, rows, D), wpk.dtype), SDS((8, LANES), F32)),
        in_specs=(HBM, HBM, ANY), out_specs=(SEM, SEM, HBM, HBM, pl.BlockSpec(memory_space=pltpu.VMEM)),
        input_output_aliases={0: 2, 1: 3}, compiler_params=pltpu.CompilerParams(has_side_effects=DATAFLOW),
    )(_in_hbm(wpk), _in_hbm(lax.empty((N_CHIPS, rows, D), wpk.dtype)), after)


def _gather_late_wait(send_sems, recv_sems, in_thru, land_thru, after):
    def body(in_ref, land_ref, send_sems, recv_sems, after_ref, in_dead, got_ref):
        x, y, c, chips = _place()
        for k, (cx, cy) in enumerate(chips):
            cp = pltpu.make_async_remote_copy(src_ref=in_ref, dst_ref=land_ref.at[2 * cx + cy], send_sem=send_sems.at[k],
                                              recv_sem=recv_sems.at[k], device_id=(cx, cy, c), device_id_type=MESH)
            cp.wait_send()
            cp.wait_recv()

    return pl.pallas_call(
        body, name="gather_late_wait",
        out_shape=(pltpu.HBM(in_thru.shape, in_thru.dtype), pltpu.HBM(land_thru.shape, land_thru.dtype)),
        in_specs=(HBM, HBM, SEM, SEM, ANY), out_specs=(HBM, HBM), input_output_aliases={0: 0, 1: 1},
        compiler_params=pltpu.CompilerParams(has_side_effects=DATAFLOW),
    )(in_thru, land_thru, send_sems, recv_sems, after)[1]


def _rs_sibling(gpk):
    HALF = gpk.shape[1] // 2

    def body(in_ref, out_ref, send_sem, recv_sem):
        x, y, c, _ = _place()
        theirs = pl.ds(pl.multiple_of((1 - c) * HALF, 8), HALF)
        cp = pltpu.make_async_remote_copy(src_ref=in_ref.at[:, theirs], dst_ref=out_ref, send_sem=send_sem, recv_sem=recv_sem,
                                          device_id=(x, y, 1 - c), device_id_type=MESH)
        cp.start()
        cp.wait()

    return _pcall(body, name="rs_sibling", in_specs=[ANY], out_specs=ANY, out_shape=SDS((N_CHIPS, HALF, D), F32),
                  scratch=[pltpu.SemaphoreType.DMA, pltpu.SemaphoreType.DMA])(gpk)


def _rs_add_sibling(cidx, gpk, got):
    HALF = got.shape[1]
    th = HALF // 4
    nh = HALF // th
    assert th % 16 == 0

    def body(c_ref, a_ref, b_ref, o_ref):
        o_ref[...] = (a_ref[...] + b_ref[...]).astype(BF16)

    gs = pltpu.PrefetchScalarGridSpec(
        num_scalar_prefetch=1, grid=(N_CHIPS, nh),
        in_specs=[pl.BlockSpec((1, th, D), lambda j, i, c: (j, c[0] * nh + i, 0)), pl.BlockSpec((1, th, D), lambda j, i, c: (j, i, 0))],
        out_specs=pl.BlockSpec((1, th, D), lambda j, i, c: (j, i, 0)))
    return pl.pallas_call(body, name="rs_add_sibling", grid_spec=gs, out_shape=SDS((N_CHIPS, HALF, D), BF16),
                          compiler_params=pltpu.CompilerParams(dimension_semantics=("parallel", "parallel"),
                                                               vmem_limit_bytes=48 << 20))(cidx, gpk, got)


def _rs_chips(part, small):
    def body(p_ref, s_ref, o_ref, so_ref, send_sems, recv_sems, ssend_sems, srecv_sems, local_sem):
        x, y, c, chips = _place()
        me = 2 * x + y
        mine_s = pltpu.make_async_copy(s_ref, so_ref.at[4 * x + 2 * y + c], local_sem)
        mine_s.start()
        sends = []
        for k, (cx, cy) in enumerate(chips):
            sends.append(pltpu.make_async_remote_copy(src_ref=p_ref.at[2 * cx + cy], dst_ref=o_ref.at[me], send_sem=send_sems.at[k],
                                                      recv_sem=recv_sems.at[k], device_id=(cx, cy, c), device_id_type=MESH))
        peers = [(x, y, 1 - c)] + [(cx, cy, c) for cx, cy in chips] + [(cx, cy, 1 - c) for cx, cy in chips]
        for k, to in enumerate(peers):
            sends.append(pltpu.make_async_remote_copy(src_ref=s_ref, dst_ref=so_ref.at[4 * x + 2 * y + c], send_sem=ssend_sems.at[k],
                                                      recv_sem=srecv_sems.at[k], device_id=to, device_id_type=MESH))
        for cp in sends:
            cp.start()
        for k, (cx, cy) in enumerate(chips):
            slot = o_ref.at[2 * cx + cy]
            pltpu.make_async_remote_copy(src_ref=slot, dst_ref=slot, send_sem=send_sems.at[k], recv_sem=recv_sems.at[k],
                                         device_id=(x, y, c), device_id_type=MESH).wait_recv()
        for k, (px, py, pc) in enumerate(peers):
            slot = so_ref.at[4 * px + 2 * py + pc]
            pltpu.make_async_remote_copy(src_ref=slot, dst_ref=slot, send_sem=ssend_sems.at[k], recv_sem=srecv_sems.at[k],
                                         device_id=(x, y, c), device_id_type=MESH).wait_recv()
        for cp in sends:
            cp.wait_send()
        mine_s.wait()

    return _pcall(body, name="rs_chips", in_specs=[ANY, ANY], out_specs=[ANY, ANY],
                  out_shape=[SDS(part.shape, part.dtype), SDS((N_DEV, 8, D), F32)],
                  scratch=[pltpu.SemaphoreType.DMA((3,)), pltpu.SemaphoreType.DMA((3,)), pltpu.SemaphoreType.DMA((7,)),
                           pltpu.SemaphoreType.DMA((7,)), pltpu.SemaphoreType.DMA])(part, small)


def _rs_add_chips(qidx, part, parts):
    HALF = part.shape[1]
    th = HALF // 4
    assert th % 16 == 0

    def body(q_ref, own_ref, p_ref, o_ref):
        for me in range(N_CHIPS):
            @pl.when(q_ref[0] == me)
            def _(me=me):
                t = [(own_ref[0] if j == me else p_ref[j]).astype(F32) for j in range(N_CHIPS)]
                o_ref[...] = ((t[0] + t[1]) + t[2]) + t[3]

    gs = pltpu.PrefetchScalarGridSpec(
        num_scalar_prefetch=1, grid=(HALF // th,),
        in_specs=[pl.BlockSpec((1, th, D), lambda i, q: (q[0], i, 0)), pl.BlockSpec((N_CHIPS, th, D), lambda i, q: (0, i, 0))],
        out_specs=pl.BlockSpec((th, D), lambda i, q: (i, 0)))
    return pl.pallas_call(body, name="rs_add_chips", grid_spec=gs, out_shape=SDS((HALF, D), F32),
                          compiler_params=pltpu.CompilerParams(dimension_semantics=("parallel",),
                                                               vmem_limit_bytes=48 << 20))(qidx, part, parts)


def _rs_join(early, late):
    def body(e_ref, l_ref, eo_ref, lo_ref, send_sems, recv_sems):
        x, y, c, _ = _place()
        cps = [pltpu.make_async_remote_copy(src_ref=src, dst_ref=dst, send_sem=send_sems.at[k], recv_sem=recv_sems.at[k],
                                            device_id=(x, y, 1 - c), device_id_type=MESH)
               for k, (src, dst) in enumerate([(e_ref, eo_ref), (l_ref, lo_ref)])]
        for cp in cps:
            cp.start()
        for cp in cps:
            cp.wait()

    return _pcall(body, name="rs_join", in_specs=[ANY, ANY], out_specs=[ANY, ANY],
                  out_shape=[SDS(early.shape, F32), SDS(late.shape, F32)],
                  scratch=[pltpu.SemaphoreType.DMA((2,)), pltpu.SemaphoreType.DMA((2,))])(early, late)


def _reduce_late_start(gpk, after):
    rows = gpk.shape[1]
    HALF = rows // 2
    assert HALF % 16 == 0

    def body(in_ref, land_ref, after_ref, send_sems, recv_sems, in_thru, land_thru, token):
        x, y, c, chips = _place()
        me = 4 * x + 2 * y + c
        peers = [(x, y, 1 - c)] + [(cx, cy, c) for cx, cy in chips] + [(cx, cy, 1 - c) for cx, cy in chips]
        for k, (px, py, pc) in enumerate(peers):
            src = in_ref.at[2 * px + py, pl.ds(pl.multiple_of(pc * HALF, 16), HALF)]
            pltpu.make_async_remote_copy(src_ref=src, dst_ref=land_ref.at[me], send_sem=send_sems.at[k], recv_sem=recv_sems.at[k],
                                         device_id=(px, py, pc), device_id_type=MESH).start()
        token[...] = jnp.zeros_like(token)

    return pl.pallas_call(
        body, name="reduce_late_start",
        out_shape=(pltpu.SemaphoreType.DMA((7,)), pltpu.SemaphoreType.DMA((7,)), pltpu.HBM(gpk.shape, gpk.dtype),
                   pltpu.HBM((N_DEV, HALF, D), gpk.dtype), SDS((8, LANES), F32)),
        in_specs=(HBM, HBM, ANY), out_specs=(SEM, SEM, HBM, HBM, pl.BlockSpec(memory_space=pltpu.VMEM)),
        input_output_aliases={0: 2, 1: 3}, compiler_params=pltpu.CompilerParams(has_side_effects=DATAFLOW),
    )(_in_hbm(gpk), _in_hbm(lax.empty((N_DEV, HALF, D), gpk.dtype)), after)


def _reduce_late_wait(send_sems, recv_sems, in_thru, land_thru, after):
    def body(in_ref, land_ref, send_sems, recv_sems, after_ref, in_out, got_ref):
        x, y, c, chips = _place()
        peers = [(x, y, 1 - c)] + [(cx, cy, c) for cx, cy in chips] + [(cx, cy, 1 - c) for cx, cy in chips]
        for k, (px, py, pc) in enumerate(peers):
            cp = pltpu.make_async_remote_copy(src_ref=land_ref.at[0], dst_ref=land_ref.at[4 * px + 2 * py + pc],
                                              send_sem=send_sems.at[k], recv_sem=recv_sems.at[k],
                                              device_id=(px, py, pc), device_id_type=MESH)
            cp.wait_send()
            cp.wait_recv()

    return pl.pallas_call(
        body, name="reduce_late_wait",
        out_shape=(pltpu.HBM(in_thru.shape, in_thru.dtype), pltpu.HBM(land_thru.shape, land_thru.dtype)),
        in_specs=(HBM, HBM, SEM, SEM, ANY), out_specs=(HBM, HBM), input_output_aliases={0: 0, 1: 1},
        compiler_params=pltpu.CompilerParams(has_side_effects=DATAFLOW),
    )(in_thru, land_thru, send_sems, recv_sems, after)


def _reduce_late_add(didx, gpk, parts):
    HALF = parts.shape[1]
    th = HALF // 4
    nh = HALF // th
    assert th % 16 == 0

    def body(d_ref, own_ref, p_ref, o_ref):
        for me in range(N_DEV):
            @pl.when(d_ref[0] == me)
            def _(me=me):
                t = [(own_ref[0] if j == me else p_ref[j]).astype(F32) for j in range(N_DEV)]
                o_ref[...] = ((((((t[0] + t[1]) + t[2]) + t[3]) + t[4]) + t[5]) + t[6]) + t[7]

    gs = pltpu.PrefetchScalarGridSpec(
        num_scalar_prefetch=1, grid=(nh,),
        in_specs=[pl.BlockSpec((1, th, D), lambda i, d: (d[1], d[2] * nh + i, 0)), pl.BlockSpec((N_DEV, th, D), lambda i, d: (0, i, 0))],
        out_specs=pl.BlockSpec((th, D), lambda i, d: (i, 0)))
    return pl.pallas_call(body, name="reduce_late_add", grid_spec=gs, out_shape=SDS((HALF, D), F32),
                          compiler_params=pltpu.CompilerParams(dimension_semantics=("parallel",),
                                                               vmem_limit_bytes=48 << 20))(didx, gpk, parts)


def _pack_early(b, dtype):
    lanes = lambda a: jnp.pad(a.astype(dtype), ((0, 0), (0, D - a.shape[1])))
    pair = jnp.concatenate([b["w_q_b"].astype(dtype), b["w_ple"].astype(dtype), jnp.zeros((256, D - 640), dtype)], axis=1)
    return jnp.concatenate([lanes(b["w_in"]), pair, lanes(b["w_kv_b"])], axis=0)


def _pack_late(b, dtype):
    return jnp.concatenate([b[n].astype(dtype) for n in ("w_mla_up", "w_swa_up", "w_out", "w_ple_gate", "w_mlp_up", "w_mlp_down")],
                           axis=0)


def _unpack_shards(pk, which):
    return {n: pk[PACK_AT[n][1]:PACK_AT[n][1] + r, PACK_AT[n][2]:PACK_AT[n][2] + c] for n, r, c in BIG if PACK_AT[n][0] == which}


def _full_weights(gathered, own, chip, which):
    per_chip = [_unpack_shards(jnp.where(chip == j, own, gathered[j]), which) for j in range(N_CHIPS)]
    return {n: jnp.concatenate([pc[n] for pc in per_chip], axis=1 if n in COL_SHARDED else 0) for n in per_chip[0]}


def _split_full_grads(grads, pack, dtype):
    shard = {n: (r, c) for n, r, c in BIG}
    chunks = []
    for j in range(N_CHIPS):
        blocks = {}
        for n, g in grads.items():
            r, c = shard[n]
            blocks[n] = g[:, j * c:(j + 1) * c] if n in COL_SHARDED else g[j * r:(j + 1) * r]
        chunks.append(pack(blocks, dtype))
    return jnp.stack(chunks)


def _w_in_internal(w):
    z = lambda n: jnp.zeros((w.shape[0], n), w.dtype)
    sk0, sk1 = w[:, 1440:1504], w[:, 1504:1568]
    sv0, sv1 = w[:, 1568:1632], w[:, 1632:1696]
    return jnp.concatenate([w[:, 416:1440], w[:, 1696:3744], w[:, 0:256], sk0, sk0, sk1, sk1, sv0, sv0, sv1, sv1,
                            w[:, 256:384], z(64), w[:, 384:416], z(32)], axis=1)


def _w_in_external_grad(g):
    sk = [g[:, 3328 + 128 * j:3392 + 128 * j] + g[:, 3392 + 128 * j:3456 + 128 * j] for j in range(2)]
    sv = [g[:, 3584 + 128 * j:3648 + 128 * j] + g[:, 3648 + 128 * j:3712 + 128 * j] for j in range(2)]
    return jnp.concatenate([g[:, 3072:3328], g[:, 3840:3968], g[:, 4032:4064], g[:, 0:1024], *sk, *sv, g[:, 1024:3072]], axis=1)


def _local_step(x, p, tgt, w, small, late_weights, late_grads_out):
    T = x.shape[0]
    tm = 256
    tb = 256
    w_in_p = _w_in_internal(w["w_in"])
    wqb = jnp.pad(w["w_q_b"].reshape(Q_LORA, MLA_HEADS, 96), ((0, 0), (0, 0), (0, 32))).reshape(Q_LORA, 2048)
    wkv = w["w_kv_b"].reshape(KV_LORA, MLA_HEADS, 128)
    wkn = jnp.pad(wkv[:, :, :64], ((0, 0), (0, 0), (0, 64))).reshape(KV_LORA, 2048)
    wv = wkv[:, :, 64:].reshape(KV_LORA, 1024)
    tab_m = _rope_tables(T, "mla")
    tab_s = _rope_tables(T, "swa")
    g1, gq, gkv, sinks = small["g_mix_pre"], small["g_q_a"], small["g_kv_a"], small["sinks"]
    g2, g3, g4, g5 = small["g_mix_post"], small["g_mlp_pre"], small["g_mlp_post"], small["g_ple"]
    sink_vec = sinks.reshape(SWA_HEADS)

    z, h1 = _fwd_in(x, g1, w_in_p, tm)
    qn, kvn, qm, km, vm, kt, vt, qs, ks, vs = _fwd_qkv(z, gq, gkv, wqb, wkn, wv, tab_m, tab_s, tb)
    om, lse_m = _mla_fwd(qm, km, vt, tb)
    os_, lse_s = _swa_fwd(sink_vec, qs, ks, vs)
    w = {**w, **late_weights(os_)}
    y, yo, au, bu, x1 = _fwd_mix(om, os_, z, x, w["w_mla_up"], w["w_swa_up"], w["w_out"], g2, tm)
    h2, a, u = _fwd_mlp_up(x1, g3, w["w_mlp_up"], tm)
    d, x2 = _fwd_mlp_down(u, w["w_mlp_down"], x1, g4, tm)
    loss, dx2, dgt, de0, dg5 = _ple_fwd_bwd(p, x2, tgt, w["w_ple"], g5, w["w_ple_gate"], tm)

    dd, da, dg4 = _bwd_mlp_down(dx2, d, g4, w["w_mlp_down"], a, tm)
    dx1, dg3 = _bwd_mlp_up(da, w["w_mlp_up"], x1, g3, dx2, tm)
    dyo, dg2, dau, dbu, dga, dgb, dom, dos, delta_m = _bwd_mix(dx1, yo, g2, w["w_out"], z, au, bu, w["w_mla_up"],
                                                                w["w_swa_up"], om, tb)
    token = late_grads_out({
        "w_mla_up": _wgrad(om, dau, "wgrad_mla_up"),
        "w_swa_up": _wgrad(os_, dbu, "wgrad_swa_up"),
        "w_out": _wgrad(y, dyo, "wgrad_out"),
        "w_ple_gate": _wgrad(x2, dgt, "wgrad_ple_gate"),
        "w_mlp_up": _wgrad(h2, da, "wgrad_mlp_up"),
        "w_mlp_down": _wgrad(u, dd, "wgrad_mlp_down"),
    })
    delta_m = delta_m + token[0, 0]
    dqm, dkm, dvm = _mla_bwd(qm, km, kt, vm, dom, lse_m, delta_m, tb)
    dqs, dkc, dkp, dvc, dvp, dsink = _swa_bwd(sink_vec, qs, ks, vs, dos, os_, lse_s)
    dqb, dknb, dvb, dsq, drest, dgq, dgkv = _bwd_qkv(dqm, dkm, dvm, dqs, dkc, dkp, dvc, dvp, z, gq, gkv, wqb, wkn, wv,
                                                      tab_m, tab_s)
    gx, dg1 = _bwd_in(dsq, dga, dgb, drest, w_in_p, x, g1, dx1, tm)

    g_in_p = jnp.concatenate([_wgrad(h1, dsq, "wgrad_in_sq"), _wgrad(h1, dga, "wgrad_in_ga"), _wgrad(h1, dgb, "wgrad_in_gb"),
                              _wgrad(h1, drest, "wgrad_in_rest")], axis=1)
    g_qb_p = _wgrad(qn, dqb, "wgrad_q_b")
    g_kn_p = _wgrad(kvn, dknb, "wgrad_kv_b_nope")
    g_v_p = _wgrad(kvn, dvb, "wgrad_kv_b_v")
    grads = {
        "w_in": _w_in_external_grad(g_in_p),
        "w_q_b": g_qb_p.reshape(Q_LORA, MLA_HEADS, 128)[:, :, :96].reshape(Q_LORA, 1536),
        "w_kv_b": jnp.concatenate([g_kn_p.reshape(KV_LORA, MLA_HEADS, 128)[:, :, :64], g_v_p.reshape(KV_LORA, MLA_HEADS, 64)],
                                  axis=2).reshape(KV_LORA, 2048),
        "w_ple": _wgrad(p, de0, "wgrad_ple"),
    }
    small_grads = {"g_mix_pre": dg1, "g_q_a": dgq, "g_kv_a": dgkv, "sinks": dsink[0:1, 0:SWA_HEADS], "g_mix_post": dg2,
                   "g_mlp_pre": dg3, "g_mlp_post": dg4, "g_ple": dg5}
    return loss, gx, grads, small_grads


def _pack_small(vals, fill):
    wide = [vals[n] for n, k in SMALL if k == D]
    narrow = [vals[n] for n, k in SMALL if k != D]
    used = sum(k for _, k in SMALL if k != D)
    last = jnp.concatenate(narrow + [jnp.full((1, D - used), fill, F32)], axis=1)
    return jnp.concatenate(wide + [last, jnp.full((2, D), fill, F32)], axis=0)


def _unpack_small(pk):
    out, row, off = {}, 0, 0
    for n, k in SMALL:
        if k == D:
            out[n] = pk[row:row + 1]
            row += 1
    for n, k in SMALL:
        if k != D:
            out[n] = pk[5:6, off:off + k]
            off += k
    return out


def kernel(x, p, g_mix_pre, w_in, g_q_a, w_q_b, g_kv_a, w_kv_b, sinks, w_mla_up, w_swa_up, w_out, g_mix_post, g_mlp_pre, w_mlp_up, w_mlp_down, g_mlp_post, w_ple, g_ple, w_ple_gate, loss_target, m_g_mix_pre, m_w_in, m_g_q_a, m_w_q_b, m_g_kv_a, m_w_kv_b, m_sinks, m_w_mla_up, m_w_swa_up, m_w_out, m_g_mix_post, m_g_mlp_pre, m_w_mlp_up, m_w_mlp_down, m_g_mlp_post, m_w_ple, m_g_ple, m_w_ple_gate, v_g_mix_pre, v_w_in, v_g_q_a, v_w_q_b, v_g_kv_a, v_w_kv_b, v_sinks, v_w_mla_up, v_w_swa_up, v_w_out, v_g_mix_post, v_g_mlp_pre, v_w_mlp_up, v_w_mlp_down, v_g_mlp_post, v_w_ple, v_g_ple, v_w_ple_gate):
    given = dict(locals())
    big_w = {n: given[n][0] for n, _, _ in BIG}
    small_w = {n: given[n] for n, _ in SMALL}
    small_m = {n: given["m_" + n] for n, _ in SMALL}
    small_v = {n: given["v_" + n] for n, _ in SMALL}

    core = lax.axis_index("c")
    chip = 2 * lax.axis_index("x") + lax.axis_index("y")
    core_i = core.astype(jnp.int32).reshape(1)
    chip_i = chip.astype(jnp.int32).reshape(1)
    dev_i = jnp.stack([2 * chip + core, chip, core]).astype(jnp.int32)

    own_early = _pack_early(big_w, BF16)
    own_late = _pack_late(big_w, BF16)
    got_early = _all_gather(own_early)
    late_flight = _gather_late_start(own_late, got_early)
    weights = _full_weights(got_early, own_early, chip, "early")
    step_small = {**small_w, "g_mix_pre": small_w["g_mix_pre"] + late_flight[4][0, 0]}

    def late_weights(after):
        return _full_weights(_gather_late_wait(*late_flight[:4], after), own_late, chip, "late")

    flight = {}

    def late_grads_out(grads):
        gpk_late = _split_full_grads(grads, _pack_late, BF16)
        flight["late"] = _reduce_late_start(gpk_late, dev_i)
        return flight["late"][4]

    loss_blk, gx, grads, small_grads = _local_step(x[0], p[0, 0], loss_target[0], weights, step_small, late_weights,
                                                   late_grads_out)

    gpk = _split_full_grads(grads, _pack_early, F32)
    got = _rs_sibling(gpk)
    part = _rs_add_sibling(core_i, gpk, got)
    parts, small_parts = _rs_chips(part, _pack_small(small_grads, 0.0))
    mine_early = _rs_add_chips(chip_i, part, parts)
    gpk_late, parts_late = _reduce_late_wait(*flight["late"][:4], mine_early)
    mine_late = _reduce_late_add(dev_i, gpk_late, parts_late)
    theirs_early, theirs_late = _rs_join(mine_early, mine_late)
    joined = {"early": jnp.where(core == 0, jnp.concatenate([mine_early, theirs_early]), jnp.concatenate([theirs_early, mine_early])),
              "late": jnp.where(core == 0, jnp.concatenate([mine_late, theirs_late]), jnp.concatenate([theirs_late, mine_late]))}

    loss = lax.psum(loss_blk[0, 0], ("x", "y", "c"))
    g_small_pk, d_small_pk, m_small_pk, v_small_pk = _adamw_small(
        _pack_small(small_w, 0.0), small_parts, _pack_small(small_m, 0.0), _pack_small(small_v, 1.0))
    g_small, d_small = _unpack_small(g_small_pk), _unpack_small(d_small_pk)
    m_small, v_small = _unpack_small(m_small_pk), _unpack_small(v_small_pk)

    out_g, out_d, out_m, out_v = dict(g_small), dict(d_small), dict(m_small), dict(v_small)
    for n, _, _ in BIG:
        out_g[n], out_d[n], out_m[n], out_v[n] = _adamw(given[n], joined[PACK_AT[n][0]], given["m_" + n], given["v_" + n], n)
    order = ["g_mix_pre", "w_in", "g_q_a", "w_q_b", "g_kv_a", "w_kv_b", "sinks", "w_mla_up", "w_swa_up", "w_out", "g_mix_post",
             "g_mlp_pre", "w_mlp_up", "w_mlp_down", "g_mlp_post", "w_ple", "g_ple", "w_ple_gate"]
    return (loss, gx[None], *[out_g[n] for n in order], *[out_d[n] for n in order], *[out_m[n] for n in order],
            *[out_v[n] for n in order])
```

```python
import math

import jax
import jax.numpy as jnp
from jax import lax
from jax.experimental import pallas as pl
from jax.experimental.pallas import tpu as pltpu

F32 = jnp.float32
BF16 = jnp.bfloat16
SDS = jax.ShapeDtypeStruct

D = 1024
D_FF = 4096
PLE = 256
Q_LORA = 256
KV_LORA = 128
MLA_HEADS = 16
MLA_NOPE = 64
MLA_ROPE = 32
SWA_HEADS = 16
SWA_HD = 64
WINDOW = 128
ROPE_THETA = 10000.0
EPS = 1e-6
NEG = -1e30
NZ = 4096
MLA_SCALE = (MLA_NOPE + MLA_ROPE) ** -0.5
LOG2_E = math.log2(math.e)
MLA_LOG2_SCALE = MLA_SCALE * LOG2_E
SWA_SCALE = SWA_HD ** -0.5

ADAM_LR = 0.001
ADAM_B1 = 0.9
ADAM_B2 = 0.999
ADAM_EPS = 1e-08
ADAM_WD = 0.01
ADAM_STEP = 10

LANES = 128
ATT_COLS = 128
N_CHIPS = 4
N_DEV = 8
MESH = pl.DeviceIdType.MESH

NT = (((1,), (1,)), ((), ()))
TN = (((0,), (0,)), ((), ()))

BIG = (("w_in", 1024, 936), ("w_q_b", 256, 384), ("w_kv_b", 128, 512), ("w_mla_up", 256, 1024),
       ("w_swa_up", 256, 1024), ("w_out", 256, 1024), ("w_mlp_up", 1024, 1024), ("w_mlp_down", 1024, 1024),
       ("w_ple", 256, 256), ("w_ple_gate", 256, 1024))
COL_SHARDED = ("w_in", "w_q_b", "w_kv_b", "w_mlp_up", "w_ple")
PACK_AT = {"w_in": ("early", 0, 0), "w_q_b": ("early", 1024, 0), "w_ple": ("early", 1024, 384), "w_kv_b": ("early", 1280, 0),
           "w_mla_up": ("late", 0, 0), "w_swa_up": ("late", 256, 0), "w_out": ("late", 512, 0), "w_ple_gate": ("late", 768, 0),
           "w_mlp_up": ("late", 1024, 0), "w_mlp_down": ("late", 2048, 0)}
PACK_ROWS = {"early": 1408, "late": 3072}
SMALL = (("g_mix_pre", 1024), ("g_q_a", 256), ("g_kv_a", 128), ("sinks", 16), ("g_mix_post", 1024),
         ("g_mlp_pre", 1024), ("g_mlp_post", 1024), ("g_ple", 1024))


def _dot(a, b):
    return jnp.dot(a, b, preferred_element_type=F32)


def _dot_nt(a, b):
    return lax.dot_general(a, b, NT, preferred_element_type=F32)


def _dot_tn(a, b):
    return lax.dot_general(a, b, TN, preferred_element_type=F32)


def _pcall(body, *, name, out_shape, grid=(), in_specs=None, out_specs=None, scratch=(), sem=None, vmem_mb=48):
    params = dict(vmem_limit_bytes=vmem_mb << 20)
    if sem is not None:
        params["dimension_semantics"] = sem
    return pl.pallas_call(body, name=name, grid=grid, in_specs=in_specs, out_specs=out_specs, out_shape=out_shape,
                          scratch_shapes=list(scratch), compiler_params=pltpu.CompilerParams(**params))


def _rows(tm, n, col=0):
    return pl.BlockSpec((tm, n), lambda i: (i, col))


def _full(shape):
    return pl.BlockSpec(shape, lambda i: (0,) * len(shape))


def _rms(x, g):
    r = lax.rsqrt(jnp.mean(x * x, axis=-1, keepdims=True) + EPS)
    return x * r * g


def _rms_bwd(dy, x, g):
    r = lax.rsqrt(jnp.mean(x * x, axis=-1, keepdims=True) + EPS)
    xn = x * r
    dn = dy * g
    dx = r * (dn - xn * jnp.mean(dn * xn, axis=-1, keepdims=True))
    return dx, jnp.sum(dy * xn, axis=0, keepdims=True)


def _sigmoid(x):
    return 1.0 / (1.0 + jnp.exp(-x))


def _rope(x, c, a, b, half):
    return x * c + pltpu.roll(x, LANES - half, 1) * a + pltpu.roll(x, half, 1) * b


def _rope_tables(T, kind):
    lane = jnp.arange(LANES)
    if kind == "mla":
        half = MLA_ROPE // 2
        rel = lane - MLA_NOPE
        on = (rel >= 0) & (rel < MLA_ROPE)
        d = MLA_ROPE
    else:
        half = SWA_HD // 2
        rel = lane % SWA_HD
        on = jnp.ones((LANES,), bool)
        d = SWA_HD
    first = on & (rel < half)
    second = on & (rel >= half)
    f = jnp.where(first, rel, rel - half).astype(F32)
    inv = jnp.exp(-math.log(ROPE_THETA) * f * (2.0 / d))
    ang = jnp.arange(T, dtype=F32)[:, None] * inv[None, :]
    cos, sin = jnp.cos(ang), jnp.sin(ang)
    c = jnp.where(on[None], cos, 1.0)
    a = jnp.where(first[None], -sin, 0.0)
    b = jnp.where(second[None], sin, 0.0)
    return c, a, b


def _fwd_in(x, g1, w_in_p, tm):
    T = x.shape[0]

    def body(x_ref, g_ref, w_ref, z_ref, h_ref):
        h = _rms(x_ref[...], g_ref[...]).astype(BF16)
        h_ref[...] = h
        z_ref[...] = _dot(h, w_ref[...])

    return _pcall(body, name="fwd_in", grid=(T // tm,),
                  in_specs=[_rows(tm, D), _full((1, D)), _full((D, NZ))],
                  out_specs=[_rows(tm, NZ), _rows(tm, D)],
                  out_shape=[SDS((T, NZ), F32), SDS((T, D), BF16)], sem=("parallel",))(x, g1, w_in_p)


def _fwd_qkv(z, gq, gkv, wqb, wkn, wv, tab_m, tab_s, tm):
    T = z.shape[0]

    def body(qa_ref, sq_ref, skd_ref, svd_ref, kva_ref, kr_ref, gq_ref, gkv_ref, wqb_ref, wkn_ref, wv_ref,
             cm_ref, am_ref, bm_ref, cs_ref, as_ref, bs_ref,
             qn_ref, kvn_ref, qm_ref, km_ref, vm_ref, kt_ref, vt_ref, qs_ref, ks_ref, vs_ref):
        qn = _rms(qa_ref[...], gq_ref[...]).astype(BF16)
        qn_ref[...] = qn
        kvn = _rms(kva_ref[...], gkv_ref[...]).astype(BF16)
        kvn_ref[...] = kvn
        cm, am, bm = cm_ref[...], am_ref[...], bm_ref[...]
        cs, as_, bs = cs_ref[...], as_ref[...], bs_ref[...]
        k_rope = _rope(kr_ref[...], cm, am, bm, MLA_ROPE // 2)
        vt_row = lax.broadcasted_iota(jnp.int32, (LANES, tm), 0)
        for j in range(D // LANES):
            sl = slice(LANES * j, LANES * (j + 1))
            v = _dot(kvn, wv_ref[:, sl])
            vm_ref[:, sl] = v.astype(BF16)
            v_t = v.T
            for hh, rows64 in enumerate((v_t, pltpu.roll(v_t, 64, 0))):
                blk = jnp.where(vt_row < 64, rows64, jnp.where(vt_row == 64, 1.0, 0.0))
                vt_ref[0, LANES * (2 * j + hh):LANES * (2 * j + hh + 1), :] = blk.astype(BF16)
        for h in range(MLA_HEADS):
            sl = slice(LANES * h, LANES * (h + 1))
            qh = _dot(qn, wqb_ref[:, sl])
            qm_ref[:, sl] = _rope(qh, cm, am, bm, MLA_ROPE // 2).astype(BF16)
            k = _dot(kvn, wkn_ref[:, sl]) + k_rope
            km_ref[:, sl] = k.astype(BF16)
            kt_ref[0, sl, :] = k.T.astype(BF16)
        for j in range(D // LANES):
            sl = slice(LANES * j, LANES * (j + 1))
            qs_ref[:, sl] = _rope(sq_ref[:, sl], cs, as_, bs, SWA_HD // 2).astype(BF16)
        for j in range(2):
            sl = slice(LANES * j, LANES * (j + 1))
            ks_ref[:, sl] = _rope(skd_ref[:, sl], cs, as_, bs, SWA_HD // 2).astype(BF16)
        vs_ref[...] = svd_ref[...].astype(BF16)

    tab = [_rows(tm, LANES)] * 6
    return _pcall(body, name="fwd_qkv", grid=(T // tm,),
                  in_specs=[_rows(tm, 256, 12), _rows(tm, 1024, 0), _rows(tm, 256, 13), _rows(tm, 256, 14),
                            _rows(tm, 128, 30), _rows(tm, 128, 31), _full((1, Q_LORA)), _full((1, KV_LORA)),
                            _full((Q_LORA, 2048)), _full((KV_LORA, 2048)), _full((KV_LORA, 1024))] + tab,
                  out_specs=[_rows(tm, Q_LORA), _rows(tm, KV_LORA), _rows(tm, 2048), _rows(tm, 2048), _rows(tm, 1024),
                             pl.BlockSpec((1, 2048, tm), lambda i: (i, 0, 0)), pl.BlockSpec((1, 2048, tm), lambda i: (i, 0, 0)),
                             _rows(tm, 1024), _rows(tm, 256), _rows(tm, 256)],
                  out_shape=[SDS((T, Q_LORA), BF16), SDS((T, KV_LORA), BF16), SDS((T, 2048), BF16), SDS((T, 2048), BF16),
                             SDS((T, 1024), BF16), SDS((T // tm, 2048, tm), BF16), SDS((T // tm, 2048, tm), BF16),
                             SDS((T, 1024), BF16), SDS((T, 256), BF16), SDS((T, 256), BF16)],
                  sem=("parallel",))(z, z, z, z, z, z, gq, gkv, wqb, wkn, wv, *tab_m, *tab_s)


def _mla_fwd(qm, km, vt, tb):
    T = qm.shape[0]
    nb = T // tb
    cc = ATT_COLS

    def body(q_ref, k_ref, vt_ref, o_ref, l_ref, s_ref, p_ref, m_ref, acc_ref):
        i = pl.program_id(1)
        m_ref[...] = jnp.full(m_ref.shape, NEG, F32)
        acc_ref[...] = jnp.zeros_like(acc_ref)
        key = lax.broadcasted_iota(jnp.int32, (tb, cc), 0)
        qry = lax.broadcasted_iota(jnp.int32, (tb, cc), 1)

        def scores(j, slot):
            off = pl.multiple_of(j * tb, tb)
            for hh in range(2):
                sl = slice(LANES * hh, LANES * (hh + 1))
                s_ref[slot, hh] = _dot_nt(k_ref[pl.ds(off, tb), sl], q_ref[:, sl])

        def softmax_pv(j, slot, diagonal):
            chains = [(hh, slice(cc * c, cc * (c + 1)), c) for hh in range(2) for c in range(tb // cc)]

            def scaled(hh, cols, c):
                t = s_ref[slot, hh, :, cols] * MLA_LOG2_SCALE
                return jnp.where(key <= qry + cc * c, t, NEG) if diagonal else t

            tops = []
            for hh, cols, c in chains:
                if diagonal:
                    top = jnp.max(scaled(hh, cols, c), axis=0, keepdims=True)
                else:
                    top = jnp.max(s_ref[slot, hh, :, cols], axis=0, keepdims=True) * MLA_LOG2_SCALE
                m_old = m_ref[hh, :, cols]
                mn = jnp.maximum(m_old, top)
                m_ref[hh, :, cols] = mn
                acc_ref[hh, :, cols] = jnp.exp2(m_old - mn) * acc_ref[hh, :, cols]
                tops.append(mn)
            for (hh, cols, c), mn in zip(chains, tops):
                p_ref[hh, :, cols] = jnp.exp2(scaled(hh, cols, c) - mn).astype(BF16)
            for hh in range(2):
                acc_ref[hh] += _dot(vt_ref[j, LANES * hh:LANES * (hh + 1), :], p_ref[hh])

        def step(t, carry):
            scores(2 * t + 1, 1)
            softmax_pv(2 * t, 0, False)
            scores(2 * t + 2, 0)
            softmax_pv(2 * t + 1, 1, False)
            return carry

        scores(0, 0)
        lax.fori_loop(0, i // 2, step, 0)

        @pl.when(i % 2 == 1)
        def _():
            scores(i, 1)
            softmax_pv(i - 1, 0, False)
            softmax_pv(i, 1, True)

        @pl.when(i % 2 == 0)
        def _():
            softmax_pv(i, 0, True)
        den = [acc_ref[hh, 64:65, :] for hh in range(2)]
        o_ref[...] = jnp.concatenate([acc_ref[hh, 0:64, :] / den[hh] for hh in range(2)], axis=0).T
        sub = lax.broadcasted_iota(jnp.int32, (8, tb), 0)
        lse = [m_ref[hh] + jnp.log(den[hh]) * LOG2_E for hh in range(2)]
        l_ref[0, 0] = jnp.where(sub == 0, lse[0], jnp.where(sub == 1, lse[1], 0.0))

    return _pcall(body, name="mla_fwd", grid=(MLA_HEADS // 2, nb),
                  in_specs=[pl.BlockSpec((tb, 256), lambda p, i: (i, p)), pl.BlockSpec((T, 256), lambda p, i: (0, p)),
                            pl.BlockSpec((nb, 2 * LANES, tb), lambda p, i: (0, p, 0))],
                  out_specs=[pl.BlockSpec((tb, LANES), lambda p, i: (i, p)),
                             pl.BlockSpec((1, 1, 8, tb), lambda p, i: (p, i, 0, 0))],
                  out_shape=[SDS((T, D), F32), SDS((MLA_HEADS // 2, nb, 8, tb), F32)],
                  scratch=[pltpu.VMEM((2, 2, tb, tb), F32), pltpu.VMEM((2, tb, tb), BF16), pltpu.VMEM((2, 1, tb), F32),
                           pltpu.VMEM((2, LANES, tb), F32)],
                  sem=("parallel", "arbitrary"))(qm, km, vt)


def _swa_mask(n):
    row = lax.broadcasted_iota(jnp.int32, (WINDOW, 2 * WINDOW), 0)
    col = lax.broadcasted_iota(jnp.int32, (WINDOW, 2 * WINDOW), 1)
    rel = row - col + WINDOW
    return (rel >= 0) & (rel < WINDOW) & ((col >= WINDOW) | (n > 0))


def _swa_specs(T):
    nb = T // WINDOW
    cur = lambda w: pl.BlockSpec((WINDOW, w), lambda n: (n, 0))
    prev = lambda w: pl.BlockSpec((WINDOW, w), lambda n: (jnp.maximum(n - 1, 0), 0))
    return nb, cur, prev


def _swa_fwd(sinks, qs, ks, vs):
    T = qs.shape[0]
    nb, cur, prev = _swa_specs(T)

    def body(sink_ref, q_ref, kc_ref, kp_ref, vc_ref, vp_ref, o_ref, l_ref):
        n = pl.program_id(0)
        mask = _swa_mask(n)
        lo = lax.broadcasted_iota(jnp.int32, (WINDOW, LANES), 1) < 64
        for g in range(2):
            gs = slice(LANES * g, LANES * (g + 1))
            kb = jnp.concatenate([kp_ref[:, gs], kc_ref[:, gs]], axis=0)
            vb = jnp.concatenate([vp_ref[:, gs], vc_ref[:, gs]], axis=0)
            for jj in range(4):
                j = 4 * g + jj
                sl = slice(LANES * j, LANES * (j + 1))
                qp = q_ref[:, sl]
                outs, lses = [], []
                for hf in range(2):
                    hm = lo if hf == 0 else jnp.logical_not(lo)
                    qh = jnp.where(hm, qp, jnp.zeros_like(qp))
                    s = jnp.where(mask, _dot_nt(qh, kb) * SWA_SCALE, NEG)
                    sk = sink_ref[2 * j + hf]
                    m = jnp.maximum(jnp.max(s, axis=1, keepdims=True), sk)
                    e = jnp.exp(s - m)
                    den = jnp.sum(e, axis=1, keepdims=True) + jnp.exp(sk - m)
                    p = e / den
                    outs.append(_dot(p.astype(BF16), vb))
                    lses.append(jnp.broadcast_to(m + jnp.log(den), (WINDOW, LANES)))
                o_ref[:, sl] = jnp.where(lo, outs[0], outs[1])
                l_ref[:, sl] = jnp.where(lo, lses[0], lses[1])

    return _pcall(body, name="swa_fwd", grid=(nb,),
                  in_specs=[pl.BlockSpec(memory_space=pltpu.SMEM), cur(D), cur(256), prev(256), cur(256), prev(256)],
                  out_specs=[cur(D), cur(D)], out_shape=[SDS((T, D), F32)] * 2,
                  sem=("parallel",))(sinks, qs, ks, ks, vs, vs)


def _fwd_mix(om, os_, z, x, wmu, wsu, wo, g2, tm):
    T = x.shape[0]

    def body(om_ref, os_ref, ga_ref, gb_ref, x_ref, wmu_ref, wsu_ref, wo_ref, g2_ref,
             y_ref, yo_ref, au_ref, bu_ref, x1_ref):
        au = _dot(om_ref[...].astype(BF16), wmu_ref[...])
        bu = _dot(os_ref[...].astype(BF16), wsu_ref[...])
        au_ref[...] = au
        bu_ref[...] = bu
        y = (_sigmoid(ga_ref[...]) * au + _sigmoid(gb_ref[...]) * bu).astype(BF16)
        y_ref[...] = y
        yo = _dot(y, wo_ref[...])
        yo_ref[...] = yo
        x1_ref[...] = x_ref[...] + _rms(yo, g2_ref[...])

    r = _rows(tm, D)
    w = _full((D, D))
    return _pcall(body, name="fwd_mix", grid=(T // tm,),
                  in_specs=[r, r, _rows(tm, D, 1), _rows(tm, D, 2), r, w, w, w, _full((1, D))],
                  out_specs=[r] * 5,
                  out_shape=[SDS((T, D), BF16), SDS((T, D), F32), SDS((T, D), F32), SDS((T, D), F32), SDS((T, D), F32)],
                  sem=("parallel",))(om, os_, z, z, x, wmu, wsu, wo, g2)


def _fwd_mlp_up(x1, g3, w1, tm):
    T = x1.shape[0]

    def body(x_ref, g_ref, w_ref, h_ref, a_ref, u_ref):
        h = _rms(x_ref[...], g_ref[...]).astype(BF16)
        h_ref[...] = h
        a = _dot(h, w_ref[...])
        a_ref[...] = a
        u_ref[...] = jnp.square(jnp.maximum(a, 0.0)).astype(BF16)

    return _pcall(body, name="fwd_mlp_up", grid=(T // tm,),
                  in_specs=[_rows(tm, D), _full((1, D)), _full((D, D_FF))],
                  out_specs=[_rows(tm, D), _rows(tm, D_FF), _rows(tm, D_FF)],
                  out_shape=[SDS((T, D), BF16), SDS((T, D_FF), F32), SDS((T, D_FF), BF16)],
                  sem=("parallel",))(x1, g3, w1)


def _fwd_mlp_down(u, w2, x1, g4, tm):
    T = x1.shape[0]

    def body(u_ref, w_ref, x_ref, g_ref, d_ref, x2_ref):
        d = _dot(u_ref[...], w_ref[...])
        d_ref[...] = d
        x2_ref[...] = x_ref[...] + _rms(d, g_ref[...])

    return _pcall(body, name="fwd_mlp_down", grid=(T // tm,),
                  in_specs=[_rows(tm, D_FF), _full((D_FF, D)), _rows(tm, D), _full((1, D))],
                  out_specs=[_rows(tm, D), _rows(tm, D)], out_shape=[SDS((T, D), F32)] * 2,
                  sem=("parallel",))(u, w2, x1, g4)


def _ple_fwd_bwd(p, x2, tgt, wple, g5, wpg, tm):
    T = x2.shape[0]

    def body(p_ref, x2_ref, t_ref, wple_ref, g5_ref, wpg_ref, loss_ref, dx2_ref, dgt_ref, de0_ref, dg5_ref):
        @pl.when(pl.program_id(0) == 0)
        def _():
            loss_ref[...] = jnp.zeros_like(loss_ref)
            dg5_ref[...] = jnp.zeros_like(dg5_ref)

        e0 = _dot(p_ref[...].astype(BF16), wple_ref[...])
        g5 = g5_ref[...]
        r = lax.rsqrt(jnp.mean(e0 * e0, axis=-1, keepdims=True) + EPS)
        en = e0 * r
        e = en * g5
        x2 = x2_ref[...]
        s = _sigmoid(_dot(x2.astype(BF16), wpg_ref[...]))
        diff = x2 + s * e - t_ref[...]
        sq = jnp.sum(jnp.sum(diff * diff, axis=1, keepdims=True), axis=0, keepdims=True)
        loss_ref[...] += jnp.broadcast_to(sq * (0.5 / D), loss_ref.shape)
        dx3 = diff * (1.0 / D)
        de = dx3 * s
        dgt = (dx3 * e * s * (1.0 - s)).astype(BF16)
        dgt_ref[...] = dgt
        dn = de * g5
        de0_ref[...] = (r * (dn - en * jnp.mean(dn * en, axis=-1, keepdims=True))).astype(BF16)
        dg5_ref[...] += jnp.sum(de * en, axis=0, keepdims=True)
        dx2_ref[...] = dx3 + _dot_nt(dgt, wpg_ref[...])

    r = _rows(tm, D)
    return _pcall(body, name="ple_fwd_bwd", grid=(T // tm,),
                  in_specs=[_rows(tm, PLE), r, r, _full((PLE, D)), _full((1, D)), _full((D, D))],
                  out_specs=[_full((8, LANES)), r, r, r, _full((1, D))],
                  out_shape=[SDS((8, LANES), F32), SDS((T, D), F32), SDS((T, D), BF16), SDS((T, D), BF16), SDS((1, D), F32)],
                  sem=("arbitrary",))(p, x2, tgt, wple, g5, wpg)


def _bwd_mlp_down(dx2, d, g4, w2, a, tm):
    T = dx2.shape[0]

    def body(dx_ref, d_ref, g_ref, w_ref, a_ref, dd_ref, da_ref, dg_ref):
        @pl.when(pl.program_id(0) == 0)
        def _():
            dg_ref[...] = jnp.zeros_like(dg_ref)

        dd, dg = _rms_bwd(dx_ref[...], d_ref[...], g_ref[...])
        dg_ref[...] += dg
        ddb = dd.astype(BF16)
        dd_ref[...] = ddb
        du = _dot_nt(ddb, w_ref[...])
        da_ref[...] = (du * (2.0 * jnp.maximum(a_ref[...], 0.0))).astype(BF16)

    return _pcall(body, name="bwd_mlp_down", grid=(T // tm,),
                  in_specs=[_rows(tm, D), _rows(tm, D), _full((1, D)), _full((D_FF, D)), _rows(tm, D_FF)],
                  out_specs=[_rows(tm, D), _rows(tm, D_FF), _full((1, D))],
                  out_shape=[SDS((T, D), BF16), SDS((T, D_FF), BF16), SDS((1, D), F32)],
                  sem=("arbitrary",))(dx2, d, g4, w2, a)


def _bwd_mlp_up(da, w1, x1, g3, dx2, tm):
    T = dx2.shape[0]

    def body(da_ref, w_ref, x_ref, g_ref, dx2_ref, dx1_ref, dg_ref):
        @pl.when(pl.program_id(0) == 0)
        def _():
            dg_ref[...] = jnp.zeros_like(dg_ref)

        dh = _dot_nt(da_ref[...], w_ref[...])
        dx, dg = _rms_bwd(dh, x_ref[...], g_ref[...])
        dg_ref[...] += dg
        dx1_ref[...] = dx2_ref[...] + dx

    return _pcall(body, name="bwd_mlp_up", grid=(T // tm,),
                  in_specs=[_rows(tm, D_FF), _full((D, D_FF)), _rows(tm, D), _full((1, D)), _rows(tm, D)],
                  out_specs=[_rows(tm, D), _full((1, D))],
                  out_shape=[SDS((T, D), F32), SDS((1, D), F32)], sem=("arbitrary",))(da, w1, x1, g3, dx2)


def _bwd_mix(dx1, yo, g2, wo, z, au, bu, wmu, wsu, om, tm):
    T = dx1.shape[0]

    def body(dx_ref, yo_ref, g_ref, wo_ref, ga_ref, gb_ref, au_ref, bu_ref, wmu_ref, wsu_ref, om_ref,
             dyo_ref, dg_ref, dau_ref, dbu_ref, dga_ref, dgb_ref, dom_ref, dos_ref, dl_ref):
        @pl.when(pl.program_id(0) == 0)
        def _():
            dg_ref[...] = jnp.zeros_like(dg_ref)

        dyo, dg = _rms_bwd(dx_ref[...], yo_ref[...], g_ref[...])
        dg_ref[...] += dg
        dyob = dyo.astype(BF16)
        dyo_ref[...] = dyob
        dy = _dot_nt(dyob, wo_ref[...])
        sa = _sigmoid(ga_ref[...])
        sb = _sigmoid(gb_ref[...])
        dau = (dy * sa).astype(BF16)
        dbu = (dy * sb).astype(BF16)
        dau_ref[...] = dau
        dbu_ref[...] = dbu
        dga_ref[...] = (dy * au_ref[...] * sa * (1.0 - sa)).astype(BF16)
        dgb_ref[...] = (dy * bu_ref[...] * sb * (1.0 - sb)).astype(BF16)
        dom = _dot_nt(dau, wmu_ref[...])
        dom_ref[...] = dom
        dos_ref[...] = _dot_nt(dbu, wsu_ref[...])
        prod = dom * om_ref[...]
        sub = lax.broadcasted_iota(jnp.int32, (8, tm), 0)
        for pr in range(MLA_HEADS // 2):
            pt = prod[:, LANES * pr:LANES * (pr + 1)].T
            d0 = jnp.sum(pt[0:64], axis=0, keepdims=True)
            d1 = jnp.sum(pt[64:128], axis=0, keepdims=True)
            dl_ref[pr, 0] = jnp.where(sub == 0, d0, jnp.where(sub == 1, d1, 0.0))

    r = _rows(tm, D)
    w = _full((D, D))
    return _pcall(body, name="bwd_mix", grid=(T // tm,),
                  in_specs=[r, r, _full((1, D)), w, _rows(tm, D, 1), _rows(tm, D, 2), r, r, w, w, r],
                  out_specs=[r, _full((1, D)), r, r, r, r, r, r, pl.BlockSpec((MLA_HEADS // 2, 1, 8, tm), lambda i: (0, i, 0, 0))],
                  out_shape=[SDS((T, D), BF16), SDS((1, D), F32), SDS((T, D), BF16), SDS((T, D), BF16), SDS((T, D), BF16),
                             SDS((T, D), BF16), SDS((T, D), F32), SDS((T, D), F32), SDS((MLA_HEADS // 2, T // tm, 8, tm), F32)],
                  sem=("arbitrary",))(dx1, yo, g2, wo, z, z, au, bu, wmu, wsu, om)


def _mla_bwd(qm, km, kt, vm, do, lse, delta, tb):
    T = qm.shape[0]
    nb = T // tb
    cc = ATT_COLS

    def body(q_ref, k_ref, kt_ref, v_ref, do_ref, l_ref, dl_ref, dqt_ref, dk_ref, dv_ref, s_ref, dp_ref, p_ref, ds_ref, dom_ref):
        j = pl.program_id(1)

        @pl.when(j == 0)
        def _():
            dqt_ref[...] = jnp.zeros_like(dqt_ref)

        dk_ref[...] = jnp.zeros_like(dk_ref)
        dv_ref[...] = jnp.zeros_like(dv_ref)
        lo = lax.broadcasted_iota(jnp.int32, (tb, LANES), 1) < 64
        key = lax.broadcasted_iota(jnp.int32, (tb, cc), 0)
        qry = lax.broadcasted_iota(jnp.int32, (tb, cc), 1)

        def scores(i, slot):
            rows_i = pl.ds(pl.multiple_of(i * tb, tb), tb)
            d_o = do_ref[rows_i, :]
            v = v_ref[...]
            for hh in range(2):
                sl = slice(LANES * hh, LANES * (hh + 1))
                hm = lo if hh == 0 else jnp.logical_not(lo)
                dom_ref[slot, hh] = jnp.where(hm, d_o, 0.0).astype(BF16)
                s_ref[slot, hh] = _dot_nt(k_ref[:, sl], q_ref[rows_i, sl])
                dp_ref[slot, hh] = _dot_nt(v, dom_ref[slot, hh])

        def grads(i, slot, diagonal):
            rows_i = pl.ds(pl.multiple_of(i * tb, tb), tb)
            lse_i = l_ref[0, i]
            delta_i = dl_ref[0, i]
            for hh in range(2):
                for c in range(tb // cc):
                    cols = slice(cc * c, cc * (c + 1))
                    p = jnp.exp2(s_ref[slot, hh, :, cols] * MLA_LOG2_SCALE - lse_i[hh:hh + 1, cols])
                    if diagonal:
                        p = jnp.where(key <= qry + cc * c, p, 0.0)
                    p_ref[hh, :, cols] = p.astype(BF16)
                    ds_ref[hh, :, cols] = (p * (dp_ref[slot, hh, :, cols] - delta_i[hh:hh + 1, cols]) * MLA_SCALE).astype(BF16)
            for hh in range(2):
                sl = slice(LANES * hh, LANES * (hh + 1))
                dv_ref[...] += _dot(p_ref[hh], dom_ref[slot, hh])
                dk_ref[:, sl] += _dot(ds_ref[hh], q_ref[rows_i, sl])
                dqt_ref[i, sl, :] += _dot(kt_ref[0, sl, :], ds_ref[hh])

        n_off = nb - 1 - j

        def step(u, carry):
            i0 = j + 1 + 2 * u
            scores(i0 + 1, 1)
            grads(i0, 0, False)
            scores(jnp.where(i0 + 2 < nb, i0 + 2, j), 0)
            grads(i0 + 1, 1, False)
            return carry

        scores(jnp.where(n_off > 0, j + 1, j), 0)
        lax.fori_loop(0, n_off // 2, step, 0)

        @pl.when(n_off % 2 == 1)
        def _():
            scores(j, 1)
            grads(nb - 1, 0, False)
            grads(j, 1, True)

        @pl.when(n_off % 2 == 0)
        def _():
            grads(j, 0, True)

    pair = lambda w: pl.BlockSpec((T, w), lambda p, j: (0, p))
    blk = lambda w: pl.BlockSpec((tb, w), lambda p, j: (j, p))
    stat = pl.BlockSpec((1, nb, 8, tb), lambda p, j: (p, 0, 0, 0))
    return _pcall(body, name="mla_bwd", grid=(MLA_HEADS // 2, nb),
                  in_specs=[pair(256), blk(256), pl.BlockSpec((1, 256, tb), lambda p, j: (j, p, 0)), blk(LANES), pair(LANES),
                            stat, stat],
                  out_specs=[pl.BlockSpec((nb, 256, tb), lambda p, j: (0, p, 0)), blk(256), blk(LANES)],
                  out_shape=[SDS((nb, 2048, tb), F32), SDS((T, 2048), F32), SDS((T, D), F32)],
                  scratch=[pltpu.VMEM((2, 2, tb, tb), F32), pltpu.VMEM((2, 2, tb, tb), F32), pltpu.VMEM((2, tb, tb), BF16),
                           pltpu.VMEM((2, tb, tb), BF16), pltpu.VMEM((2, 2, tb, LANES), BF16)],
                  sem=("parallel", "arbitrary"))(qm, km, kt, vm, do, lse, delta)


def _swa_bwd(sinks, qs, ks, vs, do, o, lse):
    T = qs.shape[0]
    nb, cur, prev = _swa_specs(T)

    def body(sink_ref, q_ref, kc_ref, kp_ref, vc_ref, vp_ref, do_ref, o_ref, l_ref,
             dq_ref, dkc_ref, dkp_ref, dvc_ref, dvp_ref, dsink_ref):
        n = pl.program_id(0)

        @pl.when(n == 0)
        def _():
            dsink_ref[...] = jnp.zeros_like(dsink_ref)

        mask = _swa_mask(n)
        lo = lax.broadcasted_iota(jnp.int32, (WINDOW, LANES), 1) < 64
        lane8 = lax.broadcasted_iota(jnp.int32, (8, LANES), 1)
        dsink = jnp.zeros((8, LANES), F32)
        for g in range(2):
            gs = slice(LANES * g, LANES * (g + 1))
            kb = jnp.concatenate([kp_ref[:, gs], kc_ref[:, gs]], axis=0)
            vb = jnp.concatenate([vp_ref[:, gs], vc_ref[:, gs]], axis=0)
            dkb = jnp.zeros((2 * WINDOW, LANES), F32)
            dvb = jnp.zeros((2 * WINDOW, LANES), F32)
            for jj in range(4):
                j = 4 * g + jj
                sl = slice(LANES * j, LANES * (j + 1))
                qp = q_ref[:, sl]
                d_o = do_ref[:, sl]
                prod = d_o * o_ref[:, sl]
                lse_b = l_ref[:, sl]
                dqs = []
                for hf in range(2):
                    hm = lo if hf == 0 else jnp.logical_not(lo)
                    qh = jnp.where(hm, qp, jnp.zeros_like(qp))
                    s = jnp.where(mask, _dot_nt(qh, kb) * SWA_SCALE, NEG)
                    lse_h = jnp.max(jnp.where(hm, lse_b, -jnp.inf), axis=1, keepdims=True)
                    p = jnp.exp(s - lse_h)
                    dom = jnp.where(hm, d_o, 0.0).astype(BF16)
                    dp = _dot_nt(dom, vb)
                    delta = jnp.sum(jnp.where(hm, prod, 0.0), axis=1, keepdims=True)
                    ds = (p * (dp - delta) * SWA_SCALE).astype(BF16)
                    p_sink = jnp.exp(sink_ref[2 * j + hf] - lse_h)
                    d_sink = -jnp.sum(p_sink * delta, axis=0, keepdims=True)
                    dsink = dsink + jnp.where(lane8 == 2 * j + hf, d_sink, 0.0)
                    dvb = dvb + _dot_tn(p.astype(BF16), dom)
                    dkb = dkb + _dot_tn(ds, qh)
                    dqs.append(_dot(ds, kb))
                dq_ref[:, sl] = jnp.where(lo, dqs[0], dqs[1])
            dkp_ref[:, gs] = dkb[:WINDOW]
            dkc_ref[:, gs] = dkb[WINDOW:]
            dvp_ref[:, gs] = dvb[:WINDOW]
            dvc_ref[:, gs] = dvb[WINDOW:]
        dsink_ref[...] += dsink

    return _pcall(body, name="swa_bwd", grid=(nb,),
                  in_specs=[pl.BlockSpec(memory_space=pltpu.SMEM), cur(D), cur(256), prev(256), cur(256), prev(256),
                            cur(D), cur(D), cur(D)],
                  out_specs=[cur(D), cur(256), cur(256), cur(256), cur(256), _full((8, LANES))],
                  out_shape=[SDS((T, D), F32), SDS((T, 256), F32), SDS((T, 256), F32), SDS((T, 256), F32), SDS((T, 256), F32),
                             SDS((8, LANES), F32)],
                  sem=("arbitrary",))(sinks, qs, ks, ks, vs, vs, do, o, lse)


def _bwd_qkv(dqm, dkm, dvm, dqs, dkc, dkp, dvc, dvp, z, gq, gkv, wqb, wkn, wv, tab_m, tab_s):
    T = z.shape[0]
    tm = WINDOW
    nb = T // tm
    per = dqm.shape[2] // tm

    def body(dqm_ref, dkm_ref, dvm_ref, dqs_ref, dkc_ref, dkp_ref, dvc_ref, dvp_ref, qa_ref, kva_ref, gq_ref, gkv_ref,
             wqb_ref, wkn_ref, wv_ref, cm_ref, am_ref, bm_ref, cs_ref, as_ref, bs_ref,
             dq_out, dkn_out, dv_out, dsq_ref, drest_ref, dgq_ref, dgkv_ref):
        i = pl.program_id(0)

        @pl.when(i == 0)
        def _():
            dgq_ref[...] = jnp.zeros_like(dgq_ref)
            dgkv_ref[...] = jnp.zeros_like(dgkv_ref)

        cm, am, bm = cm_ref[...], -am_ref[...], -bm_ref[...]
        cs, as_, bs = cs_ref[...], -as_ref[...], -bs_ref[...]
        lane = lax.broadcasted_iota(jnp.int32, (tm, LANES), 1)
        nope = lane < MLA_NOPE
        roped = jnp.logical_and(lane >= MLA_NOPE, lane < MLA_NOPE + MLA_ROPE)
        dkr = jnp.zeros((tm, LANES), F32)
        dqn = jnp.zeros((tm, Q_LORA), F32)
        dkvn = jnp.zeros((tm, KV_LORA), F32)
        for h in range(MLA_HEADS):
            sl = slice(LANES * h, LANES * (h + 1))
            dq_h = _rope(dqm_ref[0, sl, :].T, cm, am, bm, MLA_ROPE // 2).astype(BF16)
            dq_out[:, sl] = dq_h
            dqn = dqn + _dot_nt(dq_h, wqb_ref[:, sl])
            dk_h = dkm_ref[:, sl]
            dkn_h = jnp.where(nope, dk_h, 0.0).astype(BF16)
            dkn_out[:, sl] = dkn_h
            dkvn = dkvn + _dot_nt(dkn_h, wkn_ref[:, sl])
            dkr = dkr + jnp.where(roped, dk_h, 0.0)
        dvb = dvm_ref[...].astype(BF16)
        dv_out[...] = dvb
        dkvn = dkvn + _dot_nt(dvb, wv_ref[...])
        dqa, dgq = _rms_bwd(dqn, qa_ref[...], gq_ref[...])
        dkva, dgkv = _rms_bwd(dkvn, kva_ref[...], gkv_ref[...])
        dgq_ref[...] += dgq
        dgkv_ref[...] += dgkv
        for j in range(D // LANES):
            sl = slice(LANES * j, LANES * (j + 1))
            dsq_ref[:, sl] = _rope(dqs_ref[:, sl], cs, as_, bs, SWA_HD // 2).astype(BF16)
        keep = (i < nb - 1).astype(F32)
        drest_ref[:, 0:256] = dqa.astype(BF16)
        for j in range(2):
            sl = slice(LANES * j, LANES * (j + 1))
            dk = dkc_ref[:, sl] + keep * dkp_ref[:, sl]
            drest_ref[:, 256 + LANES * j:256 + LANES * (j + 1)] = _rope(dk, cs, as_, bs, SWA_HD // 2).astype(BF16)
        drest_ref[:, 512:768] = (dvc_ref[...] + keep * dvp_ref[...]).astype(BF16)
        drest_ref[:, 768:896] = dkva.astype(BF16)
        drest_ref[:, 896:1024] = _rope(dkr, cm, am, bm, MLA_ROPE // 2).astype(BF16)

    nxt = pl.BlockSpec((tm, 256), lambda i: (jnp.minimum(i + 1, nb - 1), 0))
    tab = [_rows(tm, LANES)] * 6
    return _pcall(body, name="bwd_qkv", grid=(nb,),
                  in_specs=[pl.BlockSpec((1, 2048, tm), lambda i: (i // per, 0, i % per)),
                            _rows(tm, 2048), _rows(tm, 1024), _rows(tm, 1024), _rows(tm, 256), nxt,
                            _rows(tm, 256), nxt, _rows(tm, 256, 12), _rows(tm, 128, 30), _full((1, Q_LORA)), _full((1, KV_LORA)),
                            _full((Q_LORA, 2048)), _full((KV_LORA, 2048)), _full((KV_LORA, 1024))] + tab,
                  out_specs=[_rows(tm, 2048), _rows(tm, 2048), _rows(tm, 1024), _rows(tm, 1024), _rows(tm, 1024),
                             _full((1, Q_LORA)), _full((1, KV_LORA))],
                  out_shape=[SDS((T, 2048), BF16), SDS((T, 2048), BF16), SDS((T, 1024), BF16), SDS((T, 1024), BF16),
                             SDS((T, 1024), BF16), SDS((1, Q_LORA), F32), SDS((1, KV_LORA), F32)],
                  sem=("arbitrary",))(dqm, dkm, dvm, dqs, dkc, dkp, dvc, dvp, z, z, gq, gkv, wqb, wkn, wv, *tab_m, *tab_s)


def _bwd_in(dsq, dga, dgb, drest, w_in_p, x, g1, dx1, tm):
    T = x.shape[0]

    def body(a_ref, b_ref, c_ref, d_ref, w_ref, x_ref, g_ref, dx1_ref, dx_ref, dg_ref):
        @pl.when(pl.program_id(0) == 0)
        def _():
            dg_ref[...] = jnp.zeros_like(dg_ref)

        dh = (_dot_nt(a_ref[...], w_ref[:, 0:1024]) + _dot_nt(b_ref[...], w_ref[:, 1024:2048])
              + _dot_nt(c_ref[...], w_ref[:, 2048:3072]) + _dot_nt(d_ref[...], w_ref[:, 3072:4096]))
        dx, dg = _rms_bwd(dh, x_ref[...], g_ref[...])
        dg_ref[...] += dg
        dx_ref[...] = dx1_ref[...] + dx

    r = _rows(tm, D)
    return _pcall(body, name="bwd_in", grid=(T // tm,),
                  in_specs=[r, r, r, r, _full((D, NZ)), r, _full((1, D)), r],
                  out_specs=[r, _full((1, D))], out_shape=[SDS((T, D), F32), SDS((1, D), F32)],
                  sem=("arbitrary",))(dsq, dga, dgb, drest, w_in_p, x, g1, dx1)


def _wgrad(a, g, name):
    T, K = a.shape
    N = g.shape[1]
    tk, tn, tt = min(K, 1024), min(N, 1024), min(T, 1024)
    assert K % tk == 0 and N % tn == 0 and T % tt == 0, (a.shape, g.shape)

    def body(a_ref, g_ref, o_ref):
        @pl.when(pl.program_id(2) == 0)
        def _():
            o_ref[...] = jnp.zeros_like(o_ref)

        o_ref[...] += _dot_tn(a_ref[...].astype(BF16), g_ref[...].astype(BF16))

    return _pcall(body, name=name, grid=(K // tk, N // tn, T // tt),
                  in_specs=[pl.BlockSpec((tt, tk), lambda k, n, t: (t, k)), pl.BlockSpec((tt, tn), lambda k, n, t: (t, n))],
                  out_specs=pl.BlockSpec((tk, tn), lambda k, n, t: (k, n)), out_shape=SDS((K, N), F32),
                  sem=("parallel", "parallel", "arbitrary"))(a, g)


def _adamw(w, packed_g, m, v, name):
    _, R, C = w.shape
    _, row0, lane0 = PACK_AT[name]
    tr = min(R, 256 if row0 % 256 == 0 else 128)
    assert row0 % tr == 0 and R % tr == 0

    def body(w_ref, g_ref, m_ref, v_ref, go_ref, d_ref, m2_ref, v2_ref):
        g_ = g_ref[:, lane0:lane0 + C]
        go_ref[0] = g_
        m2 = ADAM_B1 * m_ref[0] + (1.0 - ADAM_B1) * g_
        v2 = ADAM_B2 * v_ref[0] + (1.0 - ADAM_B2) * jnp.square(g_)
        m_hat = m2 / (1.0 - ADAM_B1 ** ADAM_STEP)
        v_hat = v2 / (1.0 - ADAM_B2 ** ADAM_STEP)
        d_ref[0] = -ADAM_LR * (m_hat / (jnp.sqrt(v_hat) + ADAM_EPS) + ADAM_WD * w_ref[0])
        m2_ref[0] = m2
        v2_ref[0] = v2

    r = pl.BlockSpec((1, tr, C), lambda i: (0, i, 0))
    return _pcall(body, name="adamw_" + name, grid=(R // tr,),
                  in_specs=[r, pl.BlockSpec((tr, D), lambda i: (row0 // tr + i, 0)), r, r], out_specs=[r] * 4,
                  out_shape=[SDS((1, R, C), F32)] * 4, sem=("parallel",))(w, packed_g, m, v)


def _adamw_small(w, parts, m, v):
    def body(w_ref, p_ref, m_ref, v_ref, g_ref, d_ref, m2_ref, v2_ref):
        g_ = p_ref[0]
        for k in range(1, N_DEV):
            g_ = g_ + p_ref[k]
        g_ref[...] = g_
        m2 = ADAM_B1 * m_ref[...] + (1.0 - ADAM_B1) * g_
        v2 = ADAM_B2 * v_ref[...] + (1.0 - ADAM_B2) * jnp.square(g_)
        m_hat = m2 / (1.0 - ADAM_B1 ** ADAM_STEP)
        v_hat = v2 / (1.0 - ADAM_B2 ** ADAM_STEP)
        d_ref[...] = -ADAM_LR * (m_hat / (jnp.sqrt(v_hat) + ADAM_EPS) + ADAM_WD * w_ref[...])
        m2_ref[...] = m2
        v2_ref[...] = v2

    s = _full((8, D))
    return _pcall(body, name="adamw_small", grid=(1,), in_specs=[s, _full((N_DEV, 8, D)), s, s], out_specs=[s] * 4,
                  out_shape=[SDS((8, D), F32)] * 4, sem=("arbitrary",))(w, parts, m, v)


ANY = pl.BlockSpec(memory_space=pl.ANY)


def _place():
    x, y, c = lax.axis_index("x"), lax.axis_index("y"), lax.axis_index("c")
    chips = [(1 - x, y), (x, 1 - y), (1 - x, 1 - y)]
    return x, y, c, chips


def _all_gather(wpk):
    rows = wpk.shape[0]
    HALF = rows // 2
    assert HALF % 16 == 0

    def body(in_ref, out_ref, send_sems, recv_sems):
        x, y, c, chips = _place()
        half = pl.ds(pl.multiple_of(c * HALF, 16), HALF)
        other = pl.ds(pl.multiple_of((1 - c) * HALF, 16), HALF)

        def copy(k, src, dst, to):
            return pltpu.make_async_remote_copy(src_ref=src, dst_ref=dst, send_sem=send_sems.at[k], recv_sem=recv_sems.at[k],
                                                device_id=to, device_id_type=MESH)

        first = [copy(k, in_ref.at[half], out_ref.at[2 * x + y, half], (cx, cy, c)) for k, (cx, cy) in enumerate(chips)]
        for cp in first:
            cp.start()
        passed = []
        for k, (cx, cy) in enumerate(chips):
            slot = out_ref.at[2 * cx + cy, half]
            copy(k, slot, slot, (x, y, c)).wait_recv()
            fwd = copy(3 + k, slot, slot, (x, y, 1 - c))
            fwd.start()
            passed.append(fwd)
        for k, (cx, cy) in enumerate(chips):
            slot = out_ref.at[2 * cx + cy, other]
            copy(3 + k, slot, slot, (x, y, c)).wait_recv()
        for cp in first + passed:
            cp.wait_send()

    return _pcall(body, name="all_gather_weights", in_specs=[ANY], out_specs=ANY,
                  out_shape=SDS((N_CHIPS, rows, D), BF16),
                  scratch=[pltpu.SemaphoreType.DMA((6,)), pltpu.SemaphoreType.DMA((6,))])(wpk)


HBM = pl.BlockSpec(memory_space=pltpu.HBM)
SEM = pl.BlockSpec(memory_space=pltpu.SEMAPHORE)
DATAFLOW = pltpu.SideEffectType.DATAFLOW_SIDE_EFFECTING


def _in_hbm(a):
    return pltpu.with_memory_space_constraint(a, pltpu.HBM)


def _gather_late_start(wpk, after):
    rows = wpk.shape[0]

    def body(in_ref, land_ref, after_ref, send_sems, recv_sems, in_thru, land_thru, token):
        x, y, c, chips = _place()
        for k, (cx, cy) in enumerate(chips):
            pltpu.make_async_remote_copy(src_ref=in_ref, dst_ref=land_ref.at[2 * x + y], send_sem=send_sems.at[k],
                                         recv_sem=recv_sems.at[k], device_id=(cx, cy, c), device_id_type=MESH).start()
        token[...] = jnp.zeros_like(token)

    return pl.pallas_call(
        body, name="gather_late_start",
        out_shape=(pltpu.SemaphoreType.DMA((3,)), pltpu.SemaphoreType.DMA((3,)), pltpu.HBM(wpk.shape, wpk.dtype),
                   pltpu.HBM((N_CHIPS, rows, D), wpk.dtype), SDS((8, LANES), F32)),
        in_specs=(HBM, HBM, ANY), out_specs=(SEM, SEM, HBM, HBM, pl.BlockSpec(memory_space=pltpu.VMEM)),
        input_output_aliases={0: 2, 1: 3}, compiler_params=pltpu.CompilerParams(has_side_effects=DATAFLOW),
    )(_in_hbm(wpk), _in_hbm(lax.empty((N_CHIPS, rows, D), wpk.dtype)), after)


def _gather_late_wait(send_sems, recv_sems, in_thru, land_thru, after):
    def body(in_ref, land_ref, send_sems, recv_sems, after_ref, in_dead, got_ref):
        x, y, c, chips = _place()
        for k, (cx, cy) in enumerate(chips):
            cp = pltpu.make_async_remote_copy(src_ref=in_ref, dst_ref=land_ref.at[2 * cx + cy], send_sem=send_sems.at[k],
                                              recv_sem=recv_sems.at[k], device_id=(cx, cy, c), device_id_type=MESH)
            cp.wait_send()
            cp.wait_recv()

    return pl.pallas_call(
        body, name="gather_late_wait",
        out_shape=(pltpu.HBM(in_thru.shape, in_thru.dtype), pltpu.HBM(land_thru.shape, land_thru.dtype)),
        in_specs=(HBM, HBM, SEM, SEM, ANY), out_specs=(HBM, HBM), input_output_aliases={0: 0, 1: 1},
        compiler_params=pltpu.CompilerParams(has_side_effects=DATAFLOW),
    )(in_thru, land_thru, send_sems, recv_sems, after)[1]


def _rs_sibling(gpk):
    HALF = gpk.shape[1] // 2

    def body(in_ref, out_ref, send_sem, recv_sem):
        x, y, c, _ = _place()
        theirs = pl.ds(pl.multiple_of((1 - c) * HALF, 8), HALF)
        cp = pltpu.make_async_remote_copy(src_ref=in_ref.at[:, theirs], dst_ref=out_ref, send_sem=send_sem, recv_sem=recv_sem,
                                          device_id=(x, y, 1 - c), device_id_type=MESH)
        cp.start()
        cp.wait()

    return _pcall(body, name="rs_sibling", in_specs=[ANY], out_specs=ANY, out_shape=SDS((N_CHIPS, HALF, D), F32),
                  scratch=[pltpu.SemaphoreType.DMA, pltpu.SemaphoreType.DMA])(gpk)


def _rs_add_sibling(cidx, gpk, got):
    HALF = got.shape[1]
    th = HALF // 4
    nh = HALF // th
    assert th % 16 == 0

    def body(c_ref, a_ref, b_ref, o_ref):
        o_ref[...] = (a_ref[...] + b_ref[...]).astype(BF16)

    gs = pltpu.PrefetchScalarGridSpec(
        num_scalar_prefetch=1, grid=(N_CHIPS, nh),
        in_specs=[pl.BlockSpec((1, th, D), lambda j, i, c: (j, c[0] * nh + i, 0)), pl.BlockSpec((1, th, D), lambda j, i, c: (j, i, 0))],
        out_specs=pl.BlockSpec((1, th, D), lambda j, i, c: (j, i, 0)))
    return pl.pallas_call(body, name="rs_add_sibling", grid_spec=gs, out_shape=SDS((N_CHIPS, HALF, D), BF16),
                          compiler_params=pltpu.CompilerParams(dimension_semantics=("parallel", "parallel"),
                                                               vmem_limit_bytes=48 << 20))(cidx, gpk, got)


def _rs_chips(part, small):
    def body(p_ref, s_ref, o_ref, so_ref, send_sems, recv_sems, ssend_sems, srecv_sems, local_sem):
        x, y, c, chips = _place()
        me = 2 * x + y
        mine_s = pltpu.make_async_copy(s_ref, so_ref.at[4 * x + 2 * y + c], local_sem)
        mine_s.start()
        sends = []
        for k, (cx, cy) in enumerate(chips):
            sends.append(pltpu.make_async_remote_copy(src_ref=p_ref.at[2 * cx + cy], dst_ref=o_ref.at[me], send_sem=send_sems.at[k],
                                                      recv_sem=recv_sems.at[k], device_id=(cx, cy, c), device_id_type=MESH))
        peers = [(x, y, 1 - c)] + [(cx, cy, c) for cx, cy in chips] + [(cx, cy, 1 - c) for cx, cy in chips]
        for k, to in enumerate(peers):
            sends.append(pltpu.make_async_remote_copy(src_ref=s_ref, dst_ref=so_ref.at[4 * x + 2 * y + c], send_sem=ssend_sems.at[k],
                                                      recv_sem=srecv_sems.at[k], device_id=to, device_id_type=MESH))
        for cp in sends:
            cp.start()
        for k, (cx, cy) in enumerate(chips):
            slot = o_ref.at[2 * cx + cy]
            pltpu.make_async_remote_copy(src_ref=slot, dst_ref=slot, send_sem=send_sems.at[k], recv_sem=recv_sems.at[k],
                                         device_id=(x, y, c), device_id_type=MESH).wait_recv()
        for k, (px, py, pc) in enumerate(peers):
            slot = so_ref.at[4 * px + 2 * py + pc]
            pltpu.make_async_remote_copy(src_ref=slot, dst_ref=slot, send_sem=ssend_sems.at[k], recv_sem=srecv_sems.at[k],
                                         device_id=(x, y, c), device_id_type=MESH).wait_recv()
        for cp in sends:
            cp.wait_send()
        mine_s.wait()

    return _pcall(body, name="rs_chips", in_specs=[ANY, ANY], out_specs=[ANY, ANY],
                  out_shape=[SDS(part.shape, part.dtype), SDS((N_DEV, 8, D), F32)],
                  scratch=[pltpu.SemaphoreType.DMA((3,)), pltpu.SemaphoreType.DMA((3,)), pltpu.SemaphoreType.DMA((7,)),
                           pltpu.SemaphoreType.DMA((7,)), pltpu.SemaphoreType.DMA])(part, small)


def _rs_add_chips(qidx, part, parts):
    HALF = part.shape[1]
    th = HALF // 4
    assert th % 16 == 0

    def body(q_ref, own_ref, p_ref, o_ref):
        for me in range(N_CHIPS):
            @pl.when(q_ref[0] == me)
            def _(me=me):
                t = [(own_ref[0] if j == me else p_ref[j]).astype(F32) for j in range(N_CHIPS)]
                o_ref[...] = ((t[0] + t[1]) + t[2]) + t[3]

    gs = pltpu.PrefetchScalarGridSpec(
        num_scalar_prefetch=1, grid=(HALF // th,),
        in_specs=[pl.BlockSpec((1, th, D), lambda i, q: (q[0], i, 0)), pl.BlockSpec((N_CHIPS, th, D), lambda i, q: (0, i, 0))],
        out_specs=pl.BlockSpec((th, D), lambda i, q: (i, 0)))
    return pl.pallas_call(body, name="rs_add_chips", grid_spec=gs, out_shape=SDS((HALF, D), F32),
                          compiler_params=pltpu.CompilerParams(dimension_semantics=("parallel",),
                                                               vmem_limit_bytes=48 << 20))(qidx, part, parts)


def _rs_join(early, late):
    def body(e_ref, l_ref, eo_ref, lo_ref, send_sems, recv_sems):
        x, y, c, _ = _place()
        cps = [pltpu.make_async_remote_copy(src_ref=src, dst_ref=dst, send_sem=send_sems.at[k], recv_sem=recv_sems.at[k],
                                            device_id=(x, y, 1 - c), device_id_type=MESH)
               for k, (src, dst) in enumerate([(e_ref, eo_ref), (l_ref, lo_ref)])]
        for cp in cps:
            cp.start()
        for cp in cps:
            cp.wait()

    return _pcall(body, name="rs_join", in_specs=[ANY, ANY], out_specs=[ANY, ANY],
                  out_shape=[SDS(early.shape, F32), SDS(late.shape, F32)],
                  scratch=[pltpu.SemaphoreType.DMA((2,)), pltpu.SemaphoreType.DMA((2,))])(early, late)


def _reduce_late_start(gpk, after):
    rows = gpk.shape[1]
    HALF = rows // 2
    assert HALF % 16 == 0

    def body(in_ref, land_ref, after_ref, send_sems, recv_sems, in_thru, land_thru, token):
        x, y, c, chips = _place()
        me = 4 * x + 2 * y + c
        peers = [(x, y, 1 - c)] + [(cx, cy, c) for cx, cy in chips] + [(cx, cy, 1 - c) for cx, cy in chips]
        for k, (px, py, pc) in enumerate(peers):
            src = in_ref.at[2 * px + py, pl.ds(pl.multiple_of(pc * HALF, 16), HALF)]
            pltpu.make_async_remote_copy(src_ref=src, dst_ref=land_ref.at[me], send_sem=send_sems.at[k], recv_sem=recv_sems.at[k],
                                         device_id=(px, py, pc), device_id_type=MESH).start()
        token[...] = jnp.zeros_like(token)

    return pl.pallas_call(
        body, name="reduce_late_start",
        out_shape=(pltpu.SemaphoreType.DMA((7,)), pltpu.SemaphoreType.DMA((7,)), pltpu.HBM(gpk.shape, gpk.dtype),
                   pltpu.HBM((N_DEV, HALF, D), gpk.dtype), SDS((8, LANES), F32)),
        in_specs=(HBM, HBM, ANY), out_specs=(SEM, SEM, HBM, HBM, pl.BlockSpec(memory_space=pltpu.VMEM)),
        input_output_aliases={0: 2, 1: 3}, compiler_params=pltpu.CompilerParams(has_side_effects=DATAFLOW),
    )(_in_hbm(gpk), _in_hbm(lax.empty((N_DEV, HALF, D), gpk.dtype)), after)


def _reduce_late_wait(send_sems, recv_sems, in_thru, land_thru, after):
    def body(in_ref, land_ref, send_sems, recv_sems, after_ref, in_out, got_ref):
        x, y, c, chips = _place()
        peers = [(x, y, 1 - c)] + [(cx, cy, c) for cx, cy in chips] + [(cx, cy, 1 - c) for cx, cy in chips]
        for k, (px, py, pc) in enumerate(peers):
            cp = pltpu.make_async_remote_copy(src_ref=land_ref.at[0], dst_ref=land_ref.at[4 * px + 2 * py + pc],
                                              send_sem=send_sems.at[k], recv_sem=recv_sems.at[k],
                                              device_id=(px, py, pc), device_id_type=MESH)
            cp.wait_send()
            cp.wait_recv()

    return pl.pallas_call(
        body, name="reduce_late_wait",
        out_shape=(pltpu.HBM(in_thru.shape, in_thru.dtype), pltpu.HBM(land_thru.shape, land_thru.dtype)),
        in_specs=(HBM, HBM, SEM, SEM, ANY), out_specs=(HBM, HBM), input_output_aliases={0: 0, 1: 1},
        compiler_params=pltpu.CompilerParams(has_side_effects=DATAFLOW),
    )(in_thru, land_thru, send_sems, recv_sems, after)


def _reduce_late_add(didx, gpk, parts):
    HALF = parts.shape[1]
    th = HALF // 4
    nh = HALF // th
    assert th % 16 == 0

    def body(d_ref, own_ref, p_ref, o_ref):
        for me in range(N_DEV):
            @pl.when(d_ref[0] == me)
            def _(me=me):
                t = [(own_ref[0] if j == me else p_ref[j]).astype(F32) for j in range(N_DEV)]
                o_ref[...] = ((((((t[0] + t[1]) + t[2]) + t[3]) + t[4]) + t[5]) + t[6]) + t[7]

    gs = pltpu.PrefetchScalarGridSpec(
        num_scalar_prefetch=1, grid=(nh,),
        in_specs=[pl.BlockSpec((1, th, D), lambda i, d: (d[1], d[2] * nh + i, 0)), pl.BlockSpec((N_DEV, th, D), lambda i, d: (0, i, 0))],
        out_specs=pl.BlockSpec((th, D), lambda i, d: (i, 0)))
    return pl.pallas_call(body, name="reduce_late_add", grid_spec=gs, out_shape=SDS((HALF, D), F32),
                          compiler_params=pltpu.CompilerParams(dimension_semantics=("parallel",),
                                                               vmem_limit_bytes=48 << 20))(didx, gpk, parts)


def _pack_early(b, dtype):
    lanes = lambda a: jnp.pad(a.astype(dtype), ((0, 0), (0, D - a.shape[1])))
    pair = jnp.concatenate([b["w_q_b"].astype(dtype), b["w_ple"].astype(dtype), jnp.zeros((256, D - 640), dtype)], axis=1)
    return jnp.concatenate([lanes(b["w_in"]), pair, lanes(b["w_kv_b"])], axis=0)


def _pack_late(b, dtype):
    return jnp.concatenate([b[n].astype(dtype) for n in ("w_mla_up", "w_swa_up", "w_out", "w_ple_gate", "w_mlp_up", "w_mlp_down")],
                           axis=0)


def _unpack_shards(pk, which):
    return {n: pk[PACK_AT[n][1]:PACK_AT[n][1] + r, PACK_AT[n][2]:PACK_AT[n][2] + c] for n, r, c in BIG if PACK_AT[n][0] == which}


def _full_weights(gathered, own, chip, which):
    own_b = _unpack_shards(own, which)
    per_chip = [{n: jnp.where(chip == j, own_b[n], blk) for n, blk in _unpack_shards(gathered[j], which).items()}
                for j in range(N_CHIPS)]
    out = {}
    for n in own_b:
        shards = [pc[n] for pc in per_chip]
        if n == "w_in":
            out["w_in_p"] = _w_in_internal(shards)
        else:
            out[n] = jnp.concatenate(shards, axis=1 if n in COL_SHARDED else 0)
    return out


def _split_full_grads(grads, pack, dtype):
    shard = {n: (r, c) for n, r, c in BIG}
    chunks = []
    for j in range(N_CHIPS):
        blocks = {}
        for n, g in grads.items():
            if n == "w_in_p":
                blocks["w_in"] = _w_in_grad_shard(g, j)
                continue
            r, c = shard[n]
            blocks[n] = g[:, j * c:(j + 1) * c] if n in COL_SHARDED else g[j * r:(j + 1) * r]
        chunks.append(pack(blocks, dtype))
    return jnp.stack(chunks)


W_IN_SHARD = 936
W_IN_SEGMENTS = ((0, 256, (3072,)), (256, 384, (3840,)), (384, 416, (4032,)), (416, 1440, (0,)), (1440, 1504, (3328, 3392)),
                 (1504, 1568, (3456, 3520)), (1568, 1632, (3584, 3648)), (1632, 1696, (3712, 3776)), (1696, 3744, (1024,)))


def _w_in_internal(shards):
    def cols(a, b):
        out = []
        for j, s in enumerate(shards):
            lo, hi = max(a, W_IN_SHARD * j), min(b, W_IN_SHARD * (j + 1))
            if lo < hi:
                out.append(s[:, lo - W_IN_SHARD * j:hi - W_IN_SHARD * j])
        return out

    pieces = {}
    for a, b, places in W_IN_SEGMENTS:
        for at in places:
            pieces[at] = cols(a, b)
    zeros = lambda n: [jnp.zeros((D, n), shards[0].dtype)]
    pieces[3968] = zeros(64)
    pieces[4064] = zeros(32)
    return jnp.concatenate([piece for at in sorted(pieces) for piece in pieces[at]], axis=1)


def _w_in_grad_shard(g, j):
    out = []
    for a, b, places in W_IN_SEGMENTS:
        lo, hi = max(a, W_IN_SHARD * j), min(b, W_IN_SHARD * (j + 1))
        if lo < hi:
            parts = [g[:, at + lo - a:at + hi - a] for at in places]
            out.append(parts[0] if len(parts) == 1 else parts[0] + parts[1])
    return jnp.concatenate(out, axis=1)


def _local_step(x, p, tgt, w, small, late_weights, late_grads_out):
    T = x.shape[0]
    tm = 256
    tb = 256
    w_in_p = w["w_in_p"]
    wqb = jnp.pad(w["w_q_b"].reshape(Q_LORA, MLA_HEADS, 96), ((0, 0), (0, 0), (0, 32))).reshape(Q_LORA, 2048)
    wkv = w["w_kv_b"].reshape(KV_LORA, MLA_HEADS, 128)
    wkn = jnp.pad(wkv[:, :, :64], ((0, 0), (0, 0), (0, 64))).reshape(KV_LORA, 2048)
    wv = wkv[:, :, 64:].reshape(KV_LORA, 1024)
    tab_m = _rope_tables(T, "mla")
    tab_s = _rope_tables(T, "swa")
    g1, gq, gkv, sinks = small["g_mix_pre"], small["g_q_a"], small["g_kv_a"], small["sinks"]
    g2, g3, g4, g5 = small["g_mix_post"], small["g_mlp_pre"], small["g_mlp_post"], small["g_ple"]
    sink_vec = sinks.reshape(SWA_HEADS)

    z, h1 = _fwd_in(x, g1, w_in_p, tm)
    qn, kvn, qm, km, vm, kt, vt, qs, ks, vs = _fwd_qkv(z, gq, gkv, wqb, wkn, wv, tab_m, tab_s, tb)
    om, lse_m = _mla_fwd(qm, km, vt, tb)
    os_, lse_s = _swa_fwd(sink_vec, qs, ks, vs)
    w = {**w, **late_weights(os_)}
    y, yo, au, bu, x1 = _fwd_mix(om, os_, z, x, w["w_mla_up"], w["w_swa_up"], w["w_out"], g2, tm)
    h2, a, u = _fwd_mlp_up(x1, g3, w["w_mlp_up"], tm)
    d, x2 = _fwd_mlp_down(u, w["w_mlp_down"], x1, g4, tm)
    loss, dx2, dgt, de0, dg5 = _ple_fwd_bwd(p, x2, tgt, w["w_ple"], g5, w["w_ple_gate"], tm)

    dd, da, dg4 = _bwd_mlp_down(dx2, d, g4, w["w_mlp_down"], a, tm)
    dx1, dg3 = _bwd_mlp_up(da, w["w_mlp_up"], x1, g3, dx2, tm)
    dyo, dg2, dau, dbu, dga, dgb, dom, dos, delta_m = _bwd_mix(dx1, yo, g2, w["w_out"], z, au, bu, w["w_mla_up"],
                                                                w["w_swa_up"], om, tb)
    token = late_grads_out({
        "w_mla_up": _wgrad(om, dau, "wgrad_mla_up"),
        "w_swa_up": _wgrad(os_, dbu, "wgrad_swa_up"),
        "w_out": _wgrad(y, dyo, "wgrad_out"),
        "w_ple_gate": _wgrad(x2, dgt, "wgrad_ple_gate"),
        "w_mlp_up": _wgrad(h2, da, "wgrad_mlp_up"),
        "w_mlp_down": _wgrad(u, dd, "wgrad_mlp_down"),
    })
    delta_m = delta_m + token[0, 0]
    dqm, dkm, dvm = _mla_bwd(qm, km, kt, vm, dom, lse_m, delta_m, tb)
    dqs, dkc, dkp, dvc, dvp, dsink = _swa_bwd(sink_vec, qs, ks, vs, dos, os_, lse_s)
    dqb, dknb, dvb, dsq, drest, dgq, dgkv = _bwd_qkv(dqm, dkm, dvm, dqs, dkc, dkp, dvc, dvp, z, gq, gkv, wqb, wkn, wv,
                                                      tab_m, tab_s)
    gx, dg1 = _bwd_in(dsq, dga, dgb, drest, w_in_p, x, g1, dx1, tm)

    g_in_p = jnp.concatenate([_wgrad(h1, dsq, "wgrad_in_sq"), _wgrad(h1, dga, "wgrad_in_ga"), _wgrad(h1, dgb, "wgrad_in_gb"),
                              _wgrad(h1, drest, "wgrad_in_rest")], axis=1)
    g_qb_p = _wgrad(qn, dqb, "wgrad_q_b")
    g_kn_p = _wgrad(kvn, dknb, "wgrad_kv_b_nope")
    g_v_p = _wgrad(kvn, dvb, "wgrad_kv_b_v")
    grads = {
        "w_in_p": g_in_p,
        "w_q_b": g_qb_p.reshape(Q_LORA, MLA_HEADS, 128)[:, :, :96].reshape(Q_LORA, 1536),
        "w_kv_b": jnp.concatenate([g_kn_p.reshape(KV_LORA, MLA_HEADS, 128)[:, :, :64], g_v_p.reshape(KV_LORA, MLA_HEADS, 64)],
                                  axis=2).reshape(KV_LORA, 2048),
        "w_ple": _wgrad(p, de0, "wgrad_ple"),
    }
    small_grads = {"g_mix_pre": dg1, "g_q_a": dgq, "g_kv_a": dgkv, "sinks": dsink[0:1, 0:SWA_HEADS], "g_mix_post": dg2,
                   "g_mlp_pre": dg3, "g_mlp_post": dg4, "g_ple": dg5}
    return loss, gx, grads, small_grads


def _pack_small(vals, fill):
    wide = [vals[n] for n, k in SMALL if k == D]
    narrow = [vals[n] for n, k in SMALL if k != D]
    used = sum(k for _, k in SMALL if k != D)
    last = jnp.concatenate(narrow + [jnp.full((1, D - used), fill, F32)], axis=1)
    return jnp.concatenate(wide + [last, jnp.full((2, D), fill, F32)], axis=0)


def _unpack_small(pk):
    out, row, off = {}, 0, 0
    for n, k in SMALL:
        if k == D:
            out[n] = pk[row:row + 1]
            row += 1
    for n, k in SMALL:
        if k != D:
            out[n] = pk[5:6, off:off + k]
            off += k
    return out


def kernel(x, p, g_mix_pre, w_in, g_q_a, w_q_b, g_kv_a, w_kv_b, sinks, w_mla_up, w_swa_up, w_out, g_mix_post, g_mlp_pre, w_mlp_up, w_mlp_down, g_mlp_post, w_ple, g_ple, w_ple_gate, loss_target, m_g_mix_pre, m_w_in, m_g_q_a, m_w_q_b, m_g_kv_a, m_w_kv_b, m_sinks, m_w_mla_up, m_w_swa_up, m_w_out, m_g_mix_post, m_g_mlp_pre, m_w_mlp_up, m_w_mlp_down, m_g_mlp_post, m_w_ple, m_g_ple, m_w_ple_gate, v_g_mix_pre, v_w_in, v_g_q_a, v_w_q_b, v_g_kv_a, v_w_kv_b, v_sinks, v_w_mla_up, v_w_swa_up, v_w_out, v_g_mix_post, v_g_mlp_pre, v_w_mlp_up, v_w_mlp_down, v_g_mlp_post, v_w_ple, v_g_ple, v_w_ple_gate):
    given = dict(locals())
    big_w = {n: given[n][0] for n, _, _ in BIG}
    small_w = {n: given[n] for n, _ in SMALL}
    small_m = {n: given["m_" + n] for n, _ in SMALL}
    small_v = {n: given["v_" + n] for n, _ in SMALL}

    core = lax.axis_index("c")
    chip = 2 * lax.axis_index("x") + lax.axis_index("y")
    core_i = core.astype(jnp.int32).reshape(1)
    chip_i = chip.astype(jnp.int32).reshape(1)
    dev_i = jnp.stack([2 * chip + core, chip, core]).astype(jnp.int32)

    own_early = _pack_early(big_w, BF16)
    own_late = _pack_late(big_w, BF16)
    got_early = _all_gather(own_early)
    late_flight = _gather_late_start(own_late, got_early)
    weights = _full_weights(got_early, own_early, chip, "early")
    step_small = {**small_w, "g_mix_pre": small_w["g_mix_pre"] + late_flight[4][0, 0]}

    def late_weights(after):
        return _full_weights(_gather_late_wait(*late_flight[:4], after), own_late, chip, "late")

    flight = {}

    def late_grads_out(grads):
        gpk_late = _split_full_grads(grads, _pack_late, BF16)
        flight["late"] = _reduce_late_start(gpk_late, dev_i)
        return flight["late"][4]

    loss_blk, gx, grads, small_grads = _local_step(x[0], p[0, 0], loss_target[0], weights, step_small, late_weights,
                                                   late_grads_out)

    gpk = _split_full_grads(grads, _pack_early, F32)
    got = _rs_sibling(gpk)
    part = _rs_add_sibling(core_i, gpk, got)
    parts, small_parts = _rs_chips(part, _pack_small(small_grads, 0.0))
    mine_early = _rs_add_chips(chip_i, part, parts)
    gpk_late, parts_late = _reduce_late_wait(*flight["late"][:4], mine_early)
    mine_late = _reduce_late_add(dev_i, gpk_late, parts_late)
    theirs_early, theirs_late = _rs_join(mine_early, mine_late)
    joined = {"early": jnp.where(core == 0, jnp.concatenate([mine_early, theirs_early]), jnp.concatenate([theirs_early, mine_early])),
              "late": jnp.where(core == 0, jnp.concatenate([mine_late, theirs_late]), jnp.concatenate([theirs_late, mine_late]))}

    loss = lax.psum(loss_blk[0, 0], ("x", "y", "c"))
    g_small_pk, d_small_pk, m_small_pk, v_small_pk = _adamw_small(
        _pack_small(small_w, 0.0), small_parts, _pack_small(small_m, 0.0), _pack_small(small_v, 1.0))
    g_small, d_small = _unpack_small(g_small_pk), _unpack_small(d_small_pk)
    m_small, v_small = _unpack_small(m_small_pk), _unpack_small(v_small_pk)

    out_g, out_d, out_m, out_v = dict(g_small), dict(d_small), dict(m_small), dict(v_small)
    for n, _, _ in BIG:
        out_g[n], out_d[n], out_m[n], out_v[n] = _adamw(given[n], joined[PACK_AT[n][0]], given["m_" + n], given["v_" + n], n)
    order = ["g_mix_pre", "w_in", "g_q_a", "w_q_b", "g_kv_a", "w_kv_b", "sinks", "w_mla_up", "w_swa_up", "w_out", "g_mix_post",
             "g_mlp_pre", "w_mlp_up", "w_mlp_down", "g_mlp_post", "w_ple", "g_ple", "w_ple_gate"]
    return (loss, gx[None], *[out_g[n] for n in order], *[out_d[n] for n in order], *[out_m[n] for n in order],
            *[out_v[n] for n in order])
```

```python
import math

import jax
import jax.numpy as jnp
from jax import lax
from jax.experimental import pallas as pl
from jax.experimental.pallas import tpu as pltpu

F32 = jnp.float32
BF16 = jnp.bfloat16
SDS = jax.ShapeDtypeStruct

D = 1024
D_FF = 4096
PLE = 256
Q_LORA = 256
KV_LORA = 128
MLA_HEADS = 16
MLA_NOPE = 64
MLA_ROPE = 32
SWA_HEADS = 16
SWA_HD = 64
WINDOW = 128
ROPE_THETA = 10000.0
EPS = 1e-6
NEG = -1e30
NZ = 4096
MLA_SCALE = (MLA_NOPE + MLA_ROPE) ** -0.5
LOG2_E = math.log2(math.e)
MLA_LOG2_SCALE = MLA_SCALE * LOG2_E
SWA_SCALE = SWA_HD ** -0.5

ADAM_LR = 0.001
ADAM_B1 = 0.9
ADAM_B2 = 0.999
ADAM_EPS = 1e-08
ADAM_WD = 0.01
ADAM_STEP = 10

LANES = 128
ATT_COLS = 128
N_CHIPS = 4
N_DEV = 8
MESH = pl.DeviceIdType.MESH

NT = (((1,), (1,)), ((), ()))
TN = (((0,), (0,)), ((), ()))

BIG = (("w_in", 1024, 936), ("w_q_b", 256, 384), ("w_kv_b", 128, 512), ("w_mla_up", 256, 1024),
       ("w_swa_up", 256, 1024), ("w_out", 256, 1024), ("w_mlp_up", 1024, 1024), ("w_mlp_down", 1024, 1024),
       ("w_ple", 256, 256), ("w_ple_gate", 256, 1024))
COL_SHARDED = ("w_in", "w_q_b", "w_kv_b", "w_mlp_up", "w_ple")
PACK_AT = {"w_in": ("early", 0, 0), "w_q_b": ("early", 1024, 0), "w_ple": ("early", 1024, 384), "w_kv_b": ("early", 1280, 0),
           "w_mla_up": ("late", 0, 0), "w_swa_up": ("late", 256, 0), "w_out": ("late", 512, 0), "w_ple_gate": ("late", 768, 0),
           "w_mlp_up": ("late", 1024, 0), "w_mlp_down": ("late", 2048, 0)}
PACK_ROWS = {"early": 1408, "late": 3072}
SMALL = (("g_mix_pre", 1024), ("g_q_a", 256), ("g_kv_a", 128), ("sinks", 16), ("g_mix_post", 1024),
         ("g_mlp_pre", 1024), ("g_mlp_post", 1024), ("g_ple", 1024))


def _dot(a, b):
    return jnp.dot(a, b, preferred_element_type=F32)


def _dot_nt(a, b):
    return lax.dot_general(a, b, NT, preferred_element_type=F32)


def _dot_tn(a, b):
    return lax.dot_general(a, b, TN, preferred_element_type=F32)


def _pcall(body, *, name, out_shape, grid=(), in_specs=None, out_specs=None, scratch=(), sem=None, vmem_mb=48):
    params = dict(vmem_limit_bytes=vmem_mb << 20)
    if sem is not None:
        params["dimension_semantics"] = sem
    return pl.pallas_call(body, name=name, grid=grid, in_specs=in_specs, out_specs=out_specs, out_shape=out_shape,
                          scratch_shapes=list(scratch), compiler_params=pltpu.CompilerParams(**params))


def _rows(tm, n, col=0):
    return pl.BlockSpec((tm, n), lambda i: (i, col))


def _full(shape):
    return pl.BlockSpec(shape, lambda i: (0,) * len(shape))


def _rms(x, g):
    r = lax.rsqrt(jnp.mean(x * x, axis=-1, keepdims=True) + EPS)
    return x * r * g


def _rms_bwd(dy, x, g):
    r = lax.rsqrt(jnp.mean(x * x, axis=-1, keepdims=True) + EPS)
    xn = x * r
    dn = dy * g
    dx = r * (dn - xn * jnp.mean(dn * xn, axis=-1, keepdims=True))
    return dx, jnp.sum(dy * xn, axis=0, keepdims=True)


def _sigmoid(x):
    return 1.0 / (1.0 + jnp.exp(-x))


def _rope(x, c, a, b, half):
    return x * c + pltpu.roll(x, LANES - half, 1) * a + pltpu.roll(x, half, 1) * b


def _rope_tables(T, kind):
    lane = jnp.arange(LANES)
    if kind == "mla":
        half = MLA_ROPE // 2
        rel = lane - MLA_NOPE
        on = (rel >= 0) & (rel < MLA_ROPE)
        d = MLA_ROPE
    else:
        half = SWA_HD // 2
        rel = lane % SWA_HD
        on = jnp.ones((LANES,), bool)
        d = SWA_HD
    first = on & (rel < half)
    second = on & (rel >= half)
    f = jnp.where(first, rel, rel - half).astype(F32)
    inv = jnp.exp(-math.log(ROPE_THETA) * f * (2.0 / d))
    ang = jnp.arange(T, dtype=F32)[:, None] * inv[None, :]
    cos, sin = jnp.cos(ang), jnp.sin(ang)
    c = jnp.where(on[None], cos, 1.0)
    a = jnp.where(first[None], -sin, 0.0)
    b = jnp.where(second[None], sin, 0.0)
    return c, a, b


def _fwd_in(x, g1, w_in_p, tm):
    T = x.shape[0]

    def body(x_ref, g_ref, w_ref, z_ref, h_ref):
        h = _rms(x_ref[...], g_ref[...]).astype(BF16)
        h_ref[...] = h
        z_ref[...] = _dot(h, w_ref[...])

    return _pcall(body, name="fwd_in", grid=(T // tm,),
                  in_specs=[_rows(tm, D), _full((1, D)), _full((D, NZ))],
                  out_specs=[_rows(tm, NZ), _rows(tm, D)],
                  out_shape=[SDS((T, NZ), F32), SDS((T, D), BF16)], sem=("parallel",))(x, g1, w_in_p)


def _fwd_qkv(z, gq, gkv, wqb, wkn, wv, tab_m, tab_s, tm):
    T = z.shape[0]

    def body(qa_ref, sq_ref, skd_ref, svd_ref, kva_ref, kr_ref, gq_ref, gkv_ref, wqb_ref, wkn_ref, wv_ref,
             cm_ref, am_ref, bm_ref, cs_ref, as_ref, bs_ref,
             qn_ref, kvn_ref, qm_ref, km_ref, vm_ref, qt_ref, kt_ref, vt_ref, qs_ref, ks_ref, vs_ref):
        qn = _rms(qa_ref[...], gq_ref[...]).astype(BF16)
        qn_ref[...] = qn
        kvn = _rms(kva_ref[...], gkv_ref[...]).astype(BF16)
        kvn_ref[...] = kvn
        cm, am, bm = cm_ref[...], am_ref[...], bm_ref[...]
        cs, as_, bs = cs_ref[...], as_ref[...], bs_ref[...]
        k_rope = _rope(kr_ref[...], cm, am, bm, MLA_ROPE // 2)
        vt_row = lax.broadcasted_iota(jnp.int32, (LANES, tm), 0)
        for j in range(D // LANES):
            sl = slice(LANES * j, LANES * (j + 1))
            v = _dot(kvn, wv_ref[:, sl])
            vm_ref[:, sl] = v.astype(BF16)
            v_t = v.T
            for hh, rows64 in enumerate((v_t, pltpu.roll(v_t, 64, 0))):
                blk = jnp.where(vt_row < 64, rows64, jnp.where(vt_row == 64, 1.0, 0.0))
                vt_ref[0, LANES * (2 * j + hh):LANES * (2 * j + hh + 1), :] = blk.astype(BF16)
        for h in range(MLA_HEADS):
            sl = slice(LANES * h, LANES * (h + 1))
            qh = _dot(qn, wqb_ref[:, sl])
            qh = _rope(qh, cm, am, bm, MLA_ROPE // 2)
            qm_ref[:, sl] = qh.astype(BF16)
            qt_ref[0, sl, :] = qh.T.astype(BF16)
            k = _dot(kvn, wkn_ref[:, sl]) + k_rope
            km_ref[:, sl] = k.astype(BF16)
            kt_ref[0, sl, :] = k.T.astype(BF16)
        for j in range(D // LANES):
            sl = slice(LANES * j, LANES * (j + 1))
            qs_ref[:, sl] = _rope(sq_ref[:, sl], cs, as_, bs, SWA_HD // 2).astype(BF16)
        for j in range(2):
            sl = slice(LANES * j, LANES * (j + 1))
            ks_ref[:, sl] = _rope(skd_ref[:, sl], cs, as_, bs, SWA_HD // 2).astype(BF16)
        vs_ref[...] = svd_ref[...].astype(BF16)

    tab = [_rows(tm, LANES)] * 6
    return _pcall(body, name="fwd_qkv", grid=(T // tm,),
                  in_specs=[_rows(tm, 256, 12), _rows(tm, 1024, 0), _rows(tm, 256, 13), _rows(tm, 256, 14),
                            _rows(tm, 128, 30), _rows(tm, 128, 31), _full((1, Q_LORA)), _full((1, KV_LORA)),
                            _full((Q_LORA, 2048)), _full((KV_LORA, 2048)), _full((KV_LORA, 1024))] + tab,
                  out_specs=[_rows(tm, Q_LORA), _rows(tm, KV_LORA), _rows(tm, 2048), _rows(tm, 2048), _rows(tm, 1024),
                             pl.BlockSpec((1, 2048, tm), lambda i: (i, 0, 0)), pl.BlockSpec((1, 2048, tm), lambda i: (i, 0, 0)),
                             pl.BlockSpec((1, 2048, tm), lambda i: (i, 0, 0)),
                             _rows(tm, 1024), _rows(tm, 256), _rows(tm, 256)],
                  out_shape=[SDS((T, Q_LORA), BF16), SDS((T, KV_LORA), BF16), SDS((T, 2048), BF16), SDS((T, 2048), BF16),
                             SDS((T, 1024), BF16), SDS((T // tm, 2048, tm), BF16), SDS((T // tm, 2048, tm), BF16),
                             SDS((T // tm, 2048, tm), BF16),
                             SDS((T, 1024), BF16), SDS((T, 256), BF16), SDS((T, 256), BF16)],
                  sem=("parallel",))(z, z, z, z, z, z, gq, gkv, wqb, wkn, wv, *tab_m, *tab_s)


def _mla_fwd(qm, km, vt, tb):
    T = qm.shape[0]
    nb = T // tb
    cc = ATT_COLS

    def body(q_ref, k_ref, vt_ref, o_ref, l_ref, s_ref, p_ref, m_ref, acc_ref):
        i = pl.program_id(1)
        m_ref[...] = jnp.full(m_ref.shape, NEG, F32)
        acc_ref[...] = jnp.zeros_like(acc_ref)
        key = lax.broadcasted_iota(jnp.int32, (tb, cc), 0)
        qry = lax.broadcasted_iota(jnp.int32, (tb, cc), 1)

        def scores(j, slot):
            off = pl.multiple_of(j * tb, tb)
            for hh in range(2):
                sl = slice(LANES * hh, LANES * (hh + 1))
                s_ref[slot, hh] = _dot_nt(k_ref[pl.ds(off, tb), sl], q_ref[:, sl])

        def softmax_pv(j, slot, diagonal):
            chains = [(hh, slice(cc * c, cc * (c + 1)), c) for hh in range(2) for c in range(tb // cc)]

            def scaled(hh, cols, c):
                t = s_ref[slot, hh, :, cols] * MLA_LOG2_SCALE
                return jnp.where(key <= qry + cc * c, t, NEG) if diagonal else t

            tops = []
            for hh, cols, c in chains:
                if diagonal:
                    top = jnp.max(scaled(hh, cols, c), axis=0, keepdims=True)
                else:
                    top = jnp.max(s_ref[slot, hh, :, cols], axis=0, keepdims=True) * MLA_LOG2_SCALE
                m_old = m_ref[hh, :, cols]
                mn = jnp.maximum(m_old, top)
                m_ref[hh, :, cols] = mn
                acc_ref[hh, :, cols] = jnp.exp2(m_old - mn) * acc_ref[hh, :, cols]
                tops.append(mn)
            for (hh, cols, c), mn in zip(chains, tops):
                p_ref[hh, :, cols] = jnp.exp2(scaled(hh, cols, c) - mn).astype(BF16)
            for hh in range(2):
                acc_ref[hh] += _dot(vt_ref[j, LANES * hh:LANES * (hh + 1), :], p_ref[hh])

        def step(t, carry):
            scores(2 * t + 1, 1)
            softmax_pv(2 * t, 0, False)
            scores(2 * t + 2, 0)
            softmax_pv(2 * t + 1, 1, False)
            return carry

        scores(0, 0)
        lax.fori_loop(0, i // 2, step, 0)

        @pl.when(i % 2 == 1)
        def _():
            scores(i, 1)
            softmax_pv(i - 1, 0, False)
            softmax_pv(i, 1, True)

        @pl.when(i % 2 == 0)
        def _():
            softmax_pv(i, 0, True)
        den = [acc_ref[hh, 64:65, :] for hh in range(2)]
        o_ref[...] = jnp.concatenate([acc_ref[hh, 0:64, :] / den[hh] for hh in range(2)], axis=0).T
        sub = lax.broadcasted_iota(jnp.int32, (8, tb), 0)
        lse = [m_ref[hh] + jnp.log(den[hh]) * LOG2_E for hh in range(2)]
        l_ref[0, 0] = jnp.where(sub == 0, lse[0], jnp.where(sub == 1, lse[1], 0.0))

    return _pcall(body, name="mla_fwd", grid=(MLA_HEADS // 2, nb),
                  in_specs=[pl.BlockSpec((tb, 256), lambda p, i: (i, p)), pl.BlockSpec((T, 256), lambda p, i: (0, p)),
                            pl.BlockSpec((nb, 2 * LANES, tb), lambda p, i: (0, p, 0))],
                  out_specs=[pl.BlockSpec((tb, LANES), lambda p, i: (i, p)),
                             pl.BlockSpec((1, 1, 8, tb), lambda p, i: (p, i, 0, 0))],
                  out_shape=[SDS((T, D), F32), SDS((MLA_HEADS // 2, nb, 8, tb), F32)],
                  scratch=[pltpu.VMEM((2, 2, tb, tb), F32), pltpu.VMEM((2, tb, tb), BF16), pltpu.VMEM((2, 1, tb), F32),
                           pltpu.VMEM((2, LANES, tb), F32)],
                  sem=("parallel", "arbitrary"))(qm, km, vt)


def _swa_mask(n):
    row = lax.broadcasted_iota(jnp.int32, (WINDOW, 2 * WINDOW), 0)
    col = lax.broadcasted_iota(jnp.int32, (WINDOW, 2 * WINDOW), 1)
    rel = row - col + WINDOW
    return (rel >= 0) & (rel < WINDOW) & ((col >= WINDOW) | (n > 0))


def _swa_specs(T):
    nb = T // WINDOW
    cur = lambda w: pl.BlockSpec((WINDOW, w), lambda n: (n, 0))
    prev = lambda w: pl.BlockSpec((WINDOW, w), lambda n: (jnp.maximum(n - 1, 0), 0))
    return nb, cur, prev


def _swa_fwd(sinks, qs, ks, vs):
    T = qs.shape[0]
    nb, cur, prev = _swa_specs(T)

    def body(sink_ref, q_ref, kc_ref, kp_ref, vc_ref, vp_ref, o_ref, l_ref):
        n = pl.program_id(0)
        mask = _swa_mask(n)
        lo = lax.broadcasted_iota(jnp.int32, (WINDOW, LANES), 1) < 64
        for g in range(2):
            gs = slice(LANES * g, LANES * (g + 1))
            kb = jnp.concatenate([kp_ref[:, gs], kc_ref[:, gs]], axis=0)
            vb = jnp.concatenate([vp_ref[:, gs], vc_ref[:, gs]], axis=0)
            for jj in range(4):
                j = 4 * g + jj
                sl = slice(LANES * j, LANES * (j + 1))
                qp = q_ref[:, sl]
                outs, lses = [], []
                for hf in range(2):
                    hm = lo if hf == 0 else jnp.logical_not(lo)
                    qh = jnp.where(hm, qp, jnp.zeros_like(qp))
                    s = jnp.where(mask, _dot_nt(qh, kb) * SWA_SCALE, NEG)
                    sk = sink_ref[2 * j + hf]
                    m = jnp.maximum(jnp.max(s, axis=1, keepdims=True), sk)
                    e = jnp.exp(s - m)
                    den = jnp.sum(e, axis=1, keepdims=True) + jnp.exp(sk - m)
                    p = e / den
                    outs.append(_dot(p.astype(BF16), vb))
                    lses.append(jnp.broadcast_to(m + jnp.log(den), (WINDOW, LANES)))
                o_ref[:, sl] = jnp.where(lo, outs[0], outs[1])
                l_ref[:, sl] = jnp.where(lo, lses[0], lses[1])

    return _pcall(body, name="swa_fwd", grid=(nb,),
                  in_specs=[pl.BlockSpec(memory_space=pltpu.SMEM), cur(D), cur(256), prev(256), cur(256), prev(256)],
                  out_specs=[cur(D), cur(D)], out_shape=[SDS((T, D), F32)] * 2,
                  sem=("parallel",))(sinks, qs, ks, ks, vs, vs)


def _fwd_mix(om, os_, z, x, wmu, wsu, wo, g2, tm):
    T = x.shape[0]

    def body(om_ref, os_ref, ga_ref, gb_ref, x_ref, wmu_ref, wsu_ref, wo_ref, g2_ref,
             y_ref, yo_ref, au_ref, bu_ref, x1_ref):
        au = _dot(om_ref[...].astype(BF16), wmu_ref[...])
        bu = _dot(os_ref[...].astype(BF16), wsu_ref[...])
        au_ref[...] = au
        bu_ref[...] = bu
        y = (_sigmoid(ga_ref[...]) * au + _sigmoid(gb_ref[...]) * bu).astype(BF16)
        y_ref[...] = y
        yo = _dot(y, wo_ref[...])
        yo_ref[...] = yo
        x1_ref[...] = x_ref[...] + _rms(yo, g2_ref[...])

    r = _rows(tm, D)
    w = _full((D, D))
    return _pcall(body, name="fwd_mix", grid=(T // tm,),
                  in_specs=[r, r, _rows(tm, D, 1), _rows(tm, D, 2), r, w, w, w, _full((1, D))],
                  out_specs=[r] * 5,
                  out_shape=[SDS((T, D), BF16), SDS((T, D), F32), SDS((T, D), F32), SDS((T, D), F32), SDS((T, D), F32)],
                  sem=("parallel",))(om, os_, z, z, x, wmu, wsu, wo, g2)


def _fwd_mlp_up(x1, g3, w1, tm):
    T = x1.shape[0]

    def body(x_ref, g_ref, w_ref, h_ref, a_ref, u_ref):
        h = _rms(x_ref[...], g_ref[...]).astype(BF16)
        h_ref[...] = h
        a = _dot(h, w_ref[...])
        a_ref[...] = a
        u_ref[...] = jnp.square(jnp.maximum(a, 0.0)).astype(BF16)

    return _pcall(body, name="fwd_mlp_up", grid=(T // tm,),
                  in_specs=[_rows(tm, D), _full((1, D)), _full((D, D_FF))],
                  out_specs=[_rows(tm, D), _rows(tm, D_FF), _rows(tm, D_FF)],
                  out_shape=[SDS((T, D), BF16), SDS((T, D_FF), F32), SDS((T, D_FF), BF16)],
                  sem=("parallel",))(x1, g3, w1)


def _fwd_mlp_down(u, w2, x1, g4, tm):
    T = x1.shape[0]

    def body(u_ref, w_ref, x_ref, g_ref, d_ref, x2_ref):
        d = _dot(u_ref[...], w_ref[...])
        d_ref[...] = d
        x2_ref[...] = x_ref[...] + _rms(d, g_ref[...])

    return _pcall(body, name="fwd_mlp_down", grid=(T // tm,),
                  in_specs=[_rows(tm, D_FF), _full((D_FF, D)), _rows(tm, D), _full((1, D))],
                  out_specs=[_rows(tm, D), _rows(tm, D)], out_shape=[SDS((T, D), F32)] * 2,
                  sem=("parallel",))(u, w2, x1, g4)


def _ple_fwd_bwd(p, x2, tgt, wple, g5, wpg, tm):
    T = x2.shape[0]

    def body(p_ref, x2_ref, t_ref, wple_ref, g5_ref, wpg_ref, loss_ref, dx2_ref, dgt_ref, de0_ref, dg5_ref):
        @pl.when(pl.program_id(0) == 0)
        def _():
            loss_ref[...] = jnp.zeros_like(loss_ref)
            dg5_ref[...] = jnp.zeros_like(dg5_ref)

        e0 = _dot(p_ref[...].astype(BF16), wple_ref[...])
        g5 = g5_ref[...]
        r = lax.rsqrt(jnp.mean(e0 * e0, axis=-1, keepdims=True) + EPS)
        en = e0 * r
        e = en * g5
        x2 = x2_ref[...]
        s = _sigmoid(_dot(x2.astype(BF16), wpg_ref[...]))
        diff = x2 + s * e - t_ref[...]
        sq = jnp.sum(jnp.sum(diff * diff, axis=1, keepdims=True), axis=0, keepdims=True)
        loss_ref[...] += jnp.broadcast_to(sq * (0.5 / D), loss_ref.shape)
        dx3 = diff * (1.0 / D)
        de = dx3 * s
        dgt = (dx3 * e * s * (1.0 - s)).astype(BF16)
        dgt_ref[...] = dgt
        dn = de * g5
        de0_ref[...] = (r * (dn - en * jnp.mean(dn * en, axis=-1, keepdims=True))).astype(BF16)
        dg5_ref[...] += jnp.sum(de * en, axis=0, keepdims=True)
        dx2_ref[...] = dx3 + _dot_nt(dgt, wpg_ref[...])

    r = _rows(tm, D)
    return _pcall(body, name="ple_fwd_bwd", grid=(T // tm,),
                  in_specs=[_rows(tm, PLE), r, r, _full((PLE, D)), _full((1, D)), _full((D, D))],
                  out_specs=[_full((8, LANES)), r, r, r, _full((1, D))],
                  out_shape=[SDS((8, LANES), F32), SDS((T, D), F32), SDS((T, D), BF16), SDS((T, D), BF16), SDS((1, D), F32)],
                  sem=("arbitrary",))(p, x2, tgt, wple, g5, wpg)


def _bwd_mlp_down(dx2, d, g4, w2, a, tm):
    T = dx2.shape[0]

    def body(dx_ref, d_ref, g_ref, w_ref, a_ref, dd_ref, da_ref, dg_ref):
        @pl.when(pl.program_id(0) == 0)
        def _():
            dg_ref[...] = jnp.zeros_like(dg_ref)

        dd, dg = _rms_bwd(dx_ref[...], d_ref[...], g_ref[...])
        dg_ref[...] += dg
        ddb = dd.astype(BF16)
        dd_ref[...] = ddb
        du = _dot_nt(ddb, w_ref[...])
        da_ref[...] = (du * (2.0 * jnp.maximum(a_ref[...], 0.0))).astype(BF16)

    return _pcall(body, name="bwd_mlp_down", grid=(T // tm,),
                  in_specs=[_rows(tm, D), _rows(tm, D), _full((1, D)), _full((D_FF, D)), _rows(tm, D_FF)],
                  out_specs=[_rows(tm, D), _rows(tm, D_FF), _full((1, D))],
                  out_shape=[SDS((T, D), BF16), SDS((T, D_FF), BF16), SDS((1, D), F32)],
                  sem=("arbitrary",))(dx2, d, g4, w2, a)


def _bwd_mlp_up(da, w1, x1, g3, dx2, tm):
    T = dx2.shape[0]

    def body(da_ref, w_ref, x_ref, g_ref, dx2_ref, dx1_ref, dg_ref):
        @pl.when(pl.program_id(0) == 0)
        def _():
            dg_ref[...] = jnp.zeros_like(dg_ref)

        dh = _dot_nt(da_ref[...], w_ref[...])
        dx, dg = _rms_bwd(dh, x_ref[...], g_ref[...])
        dg_ref[...] += dg
        dx1_ref[...] = dx2_ref[...] + dx

    return _pcall(body, name="bwd_mlp_up", grid=(T // tm,),
                  in_specs=[_rows(tm, D_FF), _full((D, D_FF)), _rows(tm, D), _full((1, D)), _rows(tm, D)],
                  out_specs=[_rows(tm, D), _full((1, D))],
                  out_shape=[SDS((T, D), F32), SDS((1, D), F32)], sem=("arbitrary",))(da, w1, x1, g3, dx2)


def _bwd_mix(dx1, yo, g2, wo, z, au, bu, wmu, wsu, om, tm):
    T = dx1.shape[0]

    def body(dx_ref, yo_ref, g_ref, wo_ref, ga_ref, gb_ref, au_ref, bu_ref, wmu_ref, wsu_ref, om_ref,
             dyo_ref, dg_ref, dau_ref, dbu_ref, dga_ref, dgb_ref, dos_ref, dl_ref, dot_ref):
        @pl.when(pl.program_id(0) == 0)
        def _():
            dg_ref[...] = jnp.zeros_like(dg_ref)

        dyo, dg = _rms_bwd(dx_ref[...], yo_ref[...], g_ref[...])
        dg_ref[...] += dg
        dyob = dyo.astype(BF16)
        dyo_ref[...] = dyob
        dy = _dot_nt(dyob, wo_ref[...])
        sa = _sigmoid(ga_ref[...])
        sb = _sigmoid(gb_ref[...])
        dau = (dy * sa).astype(BF16)
        dbu = (dy * sb).astype(BF16)
        dau_ref[...] = dau
        dbu_ref[...] = dbu
        dga_ref[...] = (dy * au_ref[...] * sa * (1.0 - sa)).astype(BF16)
        dgb_ref[...] = (dy * bu_ref[...] * sb * (1.0 - sb)).astype(BF16)
        dom = _dot_nt(dau, wmu_ref[...])
        dos_ref[...] = _dot_nt(dbu, wsu_ref[...])
        prod = dom * om_ref[...]
        sub = lax.broadcasted_iota(jnp.int32, (8, tm), 0)
        for pr in range(MLA_HEADS // 2):
            sl = slice(LANES * pr, LANES * (pr + 1))
            pt = prod[:, sl].T
            d0 = jnp.sum(pt[0:64], axis=0, keepdims=True)
            d1 = jnp.sum(pt[64:128], axis=0, keepdims=True)
            dl_ref[pr, 0] = jnp.where(sub == 0, d0, jnp.where(sub == 1, d1, 0.0))
            dot_ref[0, sl, :] = dom[:, sl].T.astype(BF16)

    r = _rows(tm, D)
    w = _full((D, D))
    return _pcall(body, name="bwd_mix", grid=(T // tm,),
                  in_specs=[r, r, _full((1, D)), w, _rows(tm, D, 1), _rows(tm, D, 2), r, r, w, w, r],
                  out_specs=[r, _full((1, D)), r, r, r, r, r, pl.BlockSpec((MLA_HEADS // 2, 1, 8, tm), lambda i: (0, i, 0, 0)),
                             pl.BlockSpec((1, D, tm), lambda i: (i, 0, 0))],
                  out_shape=[SDS((T, D), BF16), SDS((1, D), F32), SDS((T, D), BF16), SDS((T, D), BF16), SDS((T, D), BF16),
                             SDS((T, D), BF16), SDS((T, D), F32), SDS((MLA_HEADS // 2, T // tm, 8, tm), F32),
                             SDS((T // tm, D, tm), BF16)],
                  sem=("arbitrary",))(dx1, yo, g2, wo, z, z, au, bu, wmu, wsu, om)


def _mla_bwd(qm, qt, km, kt, vm, dot, lse, delta, tb):
    T = qm.shape[0]
    nb = T // tb
    cc = ATT_COLS

    def body(q_ref, qt_ref, k_ref, kt_ref, v_ref, dot_ref, l_ref, dl_ref, dqt_ref, dkt_ref, dvt_ref,
             s_ref, dp_ref, p_ref, ds_ref, vh_ref):
        j = pl.program_id(1)

        @pl.when(j == 0)
        def _():
            dqt_ref[...] = jnp.zeros_like(dqt_ref)

        dkt_ref[...] = jnp.zeros_like(dkt_ref)
        dvt_ref[...] = jnp.zeros_like(dvt_ref)
        lo = lax.broadcasted_iota(jnp.int32, (tb, LANES), 1) < 64
        key = lax.broadcasted_iota(jnp.int32, (tb, cc), 0)
        qry = lax.broadcasted_iota(jnp.int32, (tb, cc), 1)
        v = v_ref[...]
        vh_ref[0] = jnp.where(lo, v, jnp.zeros_like(v))
        vh_ref[1] = jnp.where(lo, jnp.zeros_like(v), v)

        def scores(i, slot):
            rows_i = pl.ds(pl.multiple_of(i * tb, tb), tb)
            for hh in range(2):
                sl = slice(LANES * hh, LANES * (hh + 1))
                s_ref[slot, hh] = _dot_nt(k_ref[:, sl], q_ref[rows_i, sl])
                dp_ref[slot, hh] = _dot(vh_ref[hh], dot_ref[i])

        def grads(i, slot, diagonal):
            lse_i = l_ref[0, i]
            delta_i = dl_ref[0, i]
            for hh in range(2):
                for c in range(tb // cc):
                    cols = slice(cc * c, cc * (c + 1))
                    p = jnp.exp2(s_ref[slot, hh, :, cols] * MLA_LOG2_SCALE - lse_i[hh:hh + 1, cols])
                    if diagonal:
                        p = jnp.where(key <= qry + cc * c, p, 0.0)
                    p_ref[hh, :, cols] = p.astype(BF16)
                    ds_ref[hh, :, cols] = (p * (dp_ref[slot, hh, :, cols] - delta_i[hh:hh + 1, cols]) * MLA_SCALE).astype(BF16)
            for hh in range(2):
                sl = slice(LANES * hh, LANES * (hh + 1))
                half = slice(64 * hh, 64 * (hh + 1))
                dvt_ref[0, half, :] += _dot_nt(dot_ref[i, half, :], p_ref[hh])
                dkt_ref[0, sl, :] += _dot_nt(qt_ref[i, sl, :], ds_ref[hh])
                dqt_ref[i, sl, :] += _dot(kt_ref[0, sl, :], ds_ref[hh])

        n_off = nb - 1 - j

        def step(u, carry):
            i0 = j + 1 + 2 * u
            scores(i0 + 1, 1)
            grads(i0, 0, False)
            scores(jnp.where(i0 + 2 < nb, i0 + 2, j), 0)
            grads(i0 + 1, 1, False)
            return carry

        scores(jnp.where(n_off > 0, j + 1, j), 0)
        lax.fori_loop(0, n_off // 2, step, 0)

        @pl.when(n_off % 2 == 1)
        def _():
            scores(j, 1)
            grads(nb - 1, 0, False)
            grads(j, 1, True)

        @pl.when(n_off % 2 == 0)
        def _():
            grads(j, 0, True)

    blk = lambda w: pl.BlockSpec((tb, w), lambda p, j: (j, p))
    stat = pl.BlockSpec((1, nb, 8, tb), lambda p, j: (p, 0, 0, 0))
    pair_t = lambda w: pl.BlockSpec((nb, w, tb), lambda p, j: (0, p, 0))
    blk_t = lambda w: pl.BlockSpec((1, w, tb), lambda p, j: (j, p, 0))
    return _pcall(body, name="mla_bwd", grid=(MLA_HEADS // 2, nb),
                  in_specs=[pl.BlockSpec((T, 256), lambda p, j: (0, p)), pair_t(256), blk(256), blk_t(256), blk(LANES),
                            pair_t(LANES), stat, stat],
                  out_specs=[pair_t(256), blk_t(256), blk_t(LANES)],
                  out_shape=[SDS((nb, 2048, tb), F32), SDS((nb, 2048, tb), F32), SDS((nb, D, tb), F32)],
                  scratch=[pltpu.VMEM((2, 2, tb, tb), F32), pltpu.VMEM((2, 2, tb, tb), F32), pltpu.VMEM((2, tb, tb), BF16),
                           pltpu.VMEM((2, tb, tb), BF16), pltpu.VMEM((2, tb, LANES), BF16)],
                  sem=("parallel", "arbitrary"))(qm, qt, km, kt, vm, dot, lse, delta)


def _swa_bwd(sinks, qs, ks, vs, do, o, lse):
    T = qs.shape[0]
    nb, cur, prev = _swa_specs(T)

    def body(sink_ref, q_ref, kc_ref, kp_ref, vc_ref, vp_ref, do_ref, o_ref, l_ref,
             dq_ref, dkc_ref, dkp_ref, dvc_ref, dvp_ref, dsink_ref):
        n = pl.program_id(0)

        @pl.when(n == 0)
        def _():
            dsink_ref[...] = jnp.zeros_like(dsink_ref)

        mask = _swa_mask(n)
        lo = lax.broadcasted_iota(jnp.int32, (WINDOW, LANES), 1) < 64
        lane8 = lax.broadcasted_iota(jnp.int32, (8, LANES), 1)
        dsink = jnp.zeros((8, LANES), F32)
        for g in range(2):
            gs = slice(LANES * g, LANES * (g + 1))
            kb = jnp.concatenate([kp_ref[:, gs], kc_ref[:, gs]], axis=0)
            vb = jnp.concatenate([vp_ref[:, gs], vc_ref[:, gs]], axis=0)
            dkb = jnp.zeros((2 * WINDOW, LANES), F32)
            dvb = jnp.zeros((2 * WINDOW, LANES), F32)
            for jj in range(4):
                j = 4 * g + jj
                sl = slice(LANES * j, LANES * (j + 1))
                qp = q_ref[:, sl]
                d_o = do_ref[:, sl]
                prod = d_o * o_ref[:, sl]
                lse_b = l_ref[:, sl]
                dqs = []
                for hf in range(2):
                    hm = lo if hf == 0 else jnp.logical_not(lo)
                    qh = jnp.where(hm, qp, jnp.zeros_like(qp))
                    s = jnp.where(mask, _dot_nt(qh, kb) * SWA_SCALE, NEG)
                    lse_h = jnp.max(jnp.where(hm, lse_b, -jnp.inf), axis=1, keepdims=True)
                    p = jnp.exp(s - lse_h)
                    dom = jnp.where(hm, d_o, 0.0).astype(BF16)
                    dp = _dot_nt(dom, vb)
                    delta = jnp.sum(jnp.where(hm, prod, 0.0), axis=1, keepdims=True)
                    ds = (p * (dp - delta) * SWA_SCALE).astype(BF16)
                    p_sink = jnp.exp(sink_ref[2 * j + hf] - lse_h)
                    d_sink = -jnp.sum(p_sink * delta, axis=0, keepdims=True)
                    dsink = dsink + jnp.where(lane8 == 2 * j + hf, d_sink, 0.0)
                    dvb = dvb + _dot_tn(p.astype(BF16), dom)
                    dkb = dkb + _dot_tn(ds, qh)
                    dqs.append(_dot(ds, kb))
                dq_ref[:, sl] = jnp.where(lo, dqs[0], dqs[1])
            dkp_ref[:, gs] = dkb[:WINDOW]
            dkc_ref[:, gs] = dkb[WINDOW:]
            dvp_ref[:, gs] = dvb[:WINDOW]
            dvc_ref[:, gs] = dvb[WINDOW:]
        dsink_ref[...] += dsink

    return _pcall(body, name="swa_bwd", grid=(nb,),
                  in_specs=[pl.BlockSpec(memory_space=pltpu.SMEM), cur(D), cur(256), prev(256), cur(256), prev(256),
                            cur(D), cur(D), cur(D)],
                  out_specs=[cur(D), cur(256), cur(256), cur(256), cur(256), _full((8, LANES))],
                  out_shape=[SDS((T, D), F32), SDS((T, 256), F32), SDS((T, 256), F32), SDS((T, 256), F32), SDS((T, 256), F32),
                             SDS((8, LANES), F32)],
                  sem=("arbitrary",))(sinks, qs, ks, ks, vs, vs, do, o, lse)


def _bwd_qkv(dqm, dkm, dvm, dqs, dkc, dkp, dvc, dvp, z, gq, gkv, wqb, wkn, wv, tab_m, tab_s):
    T = z.shape[0]
    tm = WINDOW
    nb = T // tm
    per = dqm.shape[2] // tm

    def body(dqm_ref, dkm_ref, dvm_ref, dqs_ref, dkc_ref, dkp_ref, dvc_ref, dvp_ref, qa_ref, kva_ref, gq_ref, gkv_ref,
             wqb_ref, wkn_ref, wv_ref, cm_ref, am_ref, bm_ref, cs_ref, as_ref, bs_ref,
             dq_out, dkn_out, dv_out, dsq_ref, drest_ref, dgq_ref, dgkv_ref):
        i = pl.program_id(0)

        @pl.when(i == 0)
        def _():
            dgq_ref[...] = jnp.zeros_like(dgq_ref)
            dgkv_ref[...] = jnp.zeros_like(dgkv_ref)

        cm, am, bm = cm_ref[...], -am_ref[...], -bm_ref[...]
        cs, as_, bs = cs_ref[...], -as_ref[...], -bs_ref[...]
        lane = lax.broadcasted_iota(jnp.int32, (tm, LANES), 1)
        nope = lane < MLA_NOPE
        roped = jnp.logical_and(lane >= MLA_NOPE, lane < MLA_NOPE + MLA_ROPE)
        dkr = jnp.zeros((tm, LANES), F32)
        dqn = jnp.zeros((tm, Q_LORA), F32)
        dkvn = jnp.zeros((tm, KV_LORA), F32)
        for h in range(MLA_HEADS):
            sl = slice(LANES * h, LANES * (h + 1))
            dq_h = _rope(dqm_ref[0, sl, :].T, cm, am, bm, MLA_ROPE // 2).astype(BF16)
            dq_out[:, sl] = dq_h
            dqn = dqn + _dot_nt(dq_h, wqb_ref[:, sl])
            dk_h = dkm_ref[0, sl, :].T
            dkn_h = jnp.where(nope, dk_h, 0.0).astype(BF16)
            dkn_out[:, sl] = dkn_h
            dkvn = dkvn + _dot_nt(dkn_h, wkn_ref[:, sl])
            dkr = dkr + jnp.where(roped, dk_h, 0.0)
        for j in range(D // LANES):
            sl = slice(LANES * j, LANES * (j + 1))
            dvb = dvm_ref[0, sl, :].T.astype(BF16)
            dv_out[:, sl] = dvb
            dkvn = dkvn + _dot_nt(dvb, wv_ref[:, sl])
        dqa, dgq = _rms_bwd(dqn, qa_ref[...], gq_ref[...])
        dkva, dgkv = _rms_bwd(dkvn, kva_ref[...], gkv_ref[...])
        dgq_ref[...] += dgq
        dgkv_ref[...] += dgkv
        for j in range(D // LANES):
            sl = slice(LANES * j, LANES * (j + 1))
            dsq_ref[:, sl] = _rope(dqs_ref[:, sl], cs, as_, bs, SWA_HD // 2).astype(BF16)
        keep = (i < nb - 1).astype(F32)
        drest_ref[:, 0:256] = dqa.astype(BF16)
        for j in range(2):
            sl = slice(LANES * j, LANES * (j + 1))
            dk = dkc_ref[:, sl] + keep * dkp_ref[:, sl]
            drest_ref[:, 256 + LANES * j:256 + LANES * (j + 1)] = _rope(dk, cs, as_, bs, SWA_HD // 2).astype(BF16)
        drest_ref[:, 512:768] = (dvc_ref[...] + keep * dvp_ref[...]).astype(BF16)
        drest_ref[:, 768:896] = dkva.astype(BF16)
        drest_ref[:, 896:1024] = _rope(dkr, cm, am, bm, MLA_ROPE // 2).astype(BF16)

    nxt = pl.BlockSpec((tm, 256), lambda i: (jnp.minimum(i + 1, nb - 1), 0))
    tab = [_rows(tm, LANES)] * 6
    return _pcall(body, name="bwd_qkv", grid=(nb,),
                  in_specs=[pl.BlockSpec((1, 2048, tm), lambda i: (i // per, 0, i % per)),
                            pl.BlockSpec((1, 2048, tm), lambda i: (i // per, 0, i % per)),
                            pl.BlockSpec((1, 1024, tm), lambda i: (i // per, 0, i % per)), _rows(tm, 1024), _rows(tm, 256), nxt,
                            _rows(tm, 256), nxt, _rows(tm, 256, 12), _rows(tm, 128, 30), _full((1, Q_LORA)), _full((1, KV_LORA)),
                            _full((Q_LORA, 2048)), _full((KV_LORA, 2048)), _full((KV_LORA, 1024))] + tab,
                  out_specs=[_rows(tm, 2048), _rows(tm, 2048), _rows(tm, 1024), _rows(tm, 1024), _rows(tm, 1024),
                             _full((1, Q_LORA)), _full((1, KV_LORA))],
                  out_shape=[SDS((T, 2048), BF16), SDS((T, 2048), BF16), SDS((T, 1024), BF16), SDS((T, 1024), BF16),
                             SDS((T, 1024), BF16), SDS((1, Q_LORA), F32), SDS((1, KV_LORA), F32)],
                  sem=("arbitrary",))(dqm, dkm, dvm, dqs, dkc, dkp, dvc, dvp, z, z, gq, gkv, wqb, wkn, wv, *tab_m, *tab_s)


def _bwd_in(dsq, dga, dgb, drest, w_in_p, x, g1, dx1, tm):
    T = x.shape[0]

    def body(a_ref, b_ref, c_ref, d_ref, w_ref, x_ref, g_ref, dx1_ref, dx_ref, dg_ref):
        @pl.when(pl.program_id(0) == 0)
        def _():
            dg_ref[...] = jnp.zeros_like(dg_ref)

        dh = (_dot_nt(a_ref[...], w_ref[:, 0:1024]) + _dot_nt(b_ref[...], w_ref[:, 1024:2048])
              + _dot_nt(c_ref[...], w_ref[:, 2048:3072]) + _dot_nt(d_ref[...], w_ref[:, 3072:4096]))
        dx, dg = _rms_bwd(dh, x_ref[...], g_ref[...])
        dg_ref[...] += dg
        dx_ref[...] = dx1_ref[...] + dx

    r = _rows(tm, D)
    return _pcall(body, name="bwd_in", grid=(T // tm,),
                  in_specs=[r, r, r, r, _full((D, NZ)), r, _full((1, D)), r],
                  out_specs=[r, _full((1, D))], out_shape=[SDS((T, D), F32), SDS((1, D), F32)],
                  sem=("arbitrary",))(dsq, dga, dgb, drest, w_in_p, x, g1, dx1)


def _wgrad(a, g, name):
    T, K = a.shape
    N = g.shape[1]
    tk, tn, tt = min(K, 1024), min(N, 1024), min(T, 1024)
    assert K % tk == 0 and N % tn == 0 and T % tt == 0, (a.shape, g.shape)

    def body(a_ref, g_ref, o_ref):
        @pl.when(pl.program_id(2) == 0)
        def _():
            o_ref[...] = jnp.zeros_like(o_ref)

        o_ref[...] += _dot_tn(a_ref[...].astype(BF16), g_ref[...].astype(BF16))

    return _pcall(body, name=name, grid=(K // tk, N // tn, T // tt),
                  in_specs=[pl.BlockSpec((tt, tk), lambda k, n, t: (t, k)), pl.BlockSpec((tt, tn), lambda k, n, t: (t, n))],
                  out_specs=pl.BlockSpec((tk, tn), lambda k, n, t: (k, n)), out_shape=SDS((K, N), F32),
                  sem=("parallel", "parallel", "arbitrary"))(a, g)


def _adamw(w, packed_g, m, v, name):
    _, R, C = w.shape
    _, row0, lane0 = PACK_AT[name]
    tr = min(R, 256 if row0 % 256 == 0 else 128)
    assert row0 % tr == 0 and R % tr == 0

    def body(w_ref, g_ref, m_ref, v_ref, go_ref, d_ref, m2_ref, v2_ref):
        g_ = g_ref[:, lane0:lane0 + C]
        go_ref[0] = g_
        m2 = ADAM_B1 * m_ref[0] + (1.0 - ADAM_B1) * g_
        v2 = ADAM_B2 * v_ref[0] + (1.0 - ADAM_B2) * jnp.square(g_)
        m_hat = m2 / (1.0 - ADAM_B1 ** ADAM_STEP)
        v_hat = v2 / (1.0 - ADAM_B2 ** ADAM_STEP)
        d_ref[0] = -ADAM_LR * (m_hat / (jnp.sqrt(v_hat) + ADAM_EPS) + ADAM_WD * w_ref[0])
        m2_ref[0] = m2
        v2_ref[0] = v2

    r = pl.BlockSpec((1, tr, C), lambda i: (0, i, 0))
    return _pcall(body, name="adamw_" + name, grid=(R // tr,),
                  in_specs=[r, pl.BlockSpec((tr, D), lambda i: (row0 // tr + i, 0)), r, r], out_specs=[r] * 4,
                  out_shape=[SDS((1, R, C), F32)] * 4, sem=("parallel",))(w, packed_g, m, v)


def _adamw_small(w, parts, m, v):
    def body(w_ref, p_ref, m_ref, v_ref, g_ref, d_ref, m2_ref, v2_ref):
        g_ = p_ref[0]
        for k in range(1, N_DEV):
            g_ = g_ + p_ref[k]
        g_ref[...] = g_
        m2 = ADAM_B1 * m_ref[...] + (1.0 - ADAM_B1) * g_
        v2 = ADAM_B2 * v_ref[...] + (1.0 - ADAM_B2) * jnp.square(g_)
        m_hat = m2 / (1.0 - ADAM_B1 ** ADAM_STEP)
        v_hat = v2 / (1.0 - ADAM_B2 ** ADAM_STEP)
        d_ref[...] = -ADAM_LR * (m_hat / (jnp.sqrt(v_hat) + ADAM_EPS) + ADAM_WD * w_ref[...])
        m2_ref[...] = m2
        v2_ref[...] = v2

    s = _full((8, D))
    return _pcall(body, name="adamw_small", grid=(1,), in_specs=[s, _full((N_DEV, 8, D)), s, s], out_specs=[s] * 4,
                  out_shape=[SDS((8, D), F32)] * 4, sem=("arbitrary",))(w, parts, m, v)


ANY = pl.BlockSpec(memory_space=pl.ANY)


def _place():
    x, y, c = lax.axis_index("x"), lax.axis_index("y"), lax.axis_index("c")
    chips = [(1 - x, y), (x, 1 - y), (1 - x, 1 - y)]
    return x, y, c, chips


def _all_gather(wpk):
    rows = wpk.shape[0]
    HALF = rows // 2
    assert HALF % 16 == 0

    def body(in_ref, out_ref, send_sems, recv_sems):
        x, y, c, chips = _place()
        half = pl.ds(pl.multiple_of(c * HALF, 16), HALF)
        other = pl.ds(pl.multiple_of((1 - c) * HALF, 16), HALF)

        def copy(k, src, dst, to):
            return pltpu.make_async_remote_copy(src_ref=src, dst_ref=dst, send_sem=send_sems.at[k], recv_sem=recv_sems.at[k],
                                                device_id=to, device_id_type=MESH)

        first = [copy(k, in_ref.at[half], out_ref.at[2 * x + y, half], (cx, cy, c)) for k, (cx, cy) in enumerate(chips)]
        for cp in first:
            cp.start()
        passed = []
        for k, (cx, cy) in enumerate(chips):
            slot = out_ref.at[2 * cx + cy, half]
            copy(k, slot, slot, (x, y, c)).wait_recv()
            fwd = copy(3 + k, slot, slot, (x, y, 1 - c))
            fwd.start()
            passed.append(fwd)
        for k, (cx, cy) in enumerate(chips):
            slot = out_ref.at[2 * cx + cy, other]
            copy(3 + k, slot, slot, (x, y, c)).wait_recv()
        for cp in first + passed:
            cp.wait_send()

    return _pcall(body, name="all_gather_weights", in_specs=[ANY], out_specs=ANY,
                  out_shape=SDS((N_CHIPS, rows, D), BF16),
                  scratch=[pltpu.SemaphoreType.DMA((6,)), pltpu.SemaphoreType.DMA((6,))])(wpk)


HBM = pl.BlockSpec(memory_space=pltpu.HBM)
SEM = pl.BlockSpec(memory_space=pltpu.SEMAPHORE)
DATAFLOW = pltpu.SideEffectType.DATAFLOW_SIDE_EFFECTING


def _in_hbm(a):
    return pltpu.with_memory_space_constraint(a, pltpu.HBM)


def _gather_late_start(wpk, after):
    rows = wpk.shape[0]

    def body(in_ref, land_ref, after_ref, send_sems, recv_sems, in_thru, land_thru, token):
        x, y, c, chips = _place()
        for k, (cx, cy) in enumerate(chips):
            pltpu.make_async_remote_copy(src_ref=in_ref, dst_ref=land_ref.at[2 * x + y], send_sem=send_sems.at[k],
                                         recv_sem=recv_sems.at[k], device_id=(cx, cy, c), device_id_type=MESH).start()
        token[...] = jnp.zeros_like(token)

    return pl.pallas_call(
        body, name="gather_late_start",
        out_shape=(pltpu.SemaphoreType.DMA((3,)), pltpu.SemaphoreType.DMA((3,)), pltpu.HBM(wpk.shape, wpk.dtype),
                   pltpu.HBM((N_CHIPS, rows, D), wpk.dtype), SDS((8, LANES), F32)),
        in_specs=(HBM, HBM, ANY), out_specs=(SEM, SEM, HBM, HBM, pl.BlockSpec(memory_space=pltpu.VMEM)),
        input_output_aliases={0: 2, 1: 3}, compiler_params=pltpu.CompilerParams(has_side_effects=DATAFLOW),
    )(_in_hbm(wpk), _in_hbm(lax.empty((N_CHIPS, rows, D), wpk.dtype)), after)


def _gather_late_wait(send_sems, recv_sems, in_thru, land_thru, after):
    def body(in_ref, land_ref, send_sems, recv_sems, after_ref, in_dead, got_ref):
        x, y, c, chips = _place()
        for k, (cx, cy) in enumerate(chips):
            cp = pltpu.make_async_remote_copy(src_ref=in_ref, dst_ref=land_ref.at[2 * cx + cy], send_sem=send_sems.at[k],
                                              recv_sem=recv_sems.at[k], device_id=(cx, cy, c), device_id_type=MESH)
            cp.wait_send()
            cp.wait_recv()

    return pl.pallas_call(
        body, name="gather_late_wait",
        out_shape=(pltpu.HBM(in_thru.shape, in_thru.dtype), pltpu.HBM(land_thru.shape, land_thru.dtype)),
        in_specs=(HBM, HBM, SEM, SEM, ANY), out_specs=(HBM, HBM), input_output_aliases={0: 0, 1: 1},
        compiler_params=pltpu.CompilerParams(has_side_effects=DATAFLOW),
    )(in_thru, land_thru, send_sems, recv_sems, after)[1]


def _rs_sibling(gpk):
    HALF = gpk.shape[1] // 2

    def body(in_ref, out_ref, send_sem, recv_sem):
        x, y, c, _ = _place()
        theirs = pl.ds(pl.multiple_of((1 - c) * HALF, 8), HALF)
        cp = pltpu.make_async_remote_copy(src_ref=in_ref.at[:, theirs], dst_ref=out_ref, send_sem=send_sem, recv_sem=recv_sem,
                                          device_id=(x, y, 1 - c), device_id_type=MESH)
        cp.start()
        cp.wait()

    return _pcall(body, name="rs_sibling", in_specs=[ANY], out_specs=ANY, out_shape=SDS((N_CHIPS, HALF, D), F32),
                  scratch=[pltpu.SemaphoreType.DMA, pltpu.SemaphoreType.DMA])(gpk)


def _rs_add_sibling(cidx, gpk, got):
    HALF = got.shape[1]
    th = HALF // 4
    nh = HALF // th
    assert th % 16 == 0

    def body(c_ref, a_ref, b_ref, o_ref):
        o_ref[...] = (a_ref[...] + b_ref[...]).astype(BF16)

    gs = pltpu.PrefetchScalarGridSpec(
        num_scalar_prefetch=1, grid=(N_CHIPS, nh),
        in_specs=[pl.BlockSpec((1, th, D), lambda j, i, c: (j, c[0] * nh + i, 0)), pl.BlockSpec((1, th, D), lambda j, i, c: (j, i, 0))],
        out_specs=pl.BlockSpec((1, th, D), lambda j, i, c: (j, i, 0)))
    return pl.pallas_call(body, name="rs_add_sibling", grid_spec=gs, out_shape=SDS((N_CHIPS, HALF, D), BF16),
                          compiler_params=pltpu.CompilerParams(dimension_semantics=("parallel", "parallel"),
                                                               vmem_limit_bytes=48 << 20))(cidx, gpk, got)


def _rs_chips(part, small):
    def body(p_ref, s_ref, o_ref, so_ref, send_sems, recv_sems, ssend_sems, srecv_sems, local_sem):
        x, y, c, chips = _place()
        me = 2 * x + y
        mine_s = pltpu.make_async_copy(s_ref, so_ref.at[4 * x + 2 * y + c], local_sem)
        mine_s.start()
        sends = []
        for k, (cx, cy) in enumerate(chips):
            sends.append(pltpu.make_async_remote_copy(src_ref=p_ref.at[2 * cx + cy], dst_ref=o_ref.at[me], send_sem=send_sems.at[k],
                                                      recv_sem=recv_sems.at[k], device_id=(cx, cy, c), device_id_type=MESH))
        peers = [(x, y, 1 - c)] + [(cx, cy, c) for cx, cy in chips] + [(cx, cy, 1 - c) for cx, cy in chips]
        for k, to in enumerate(peers):
            sends.append(pltpu.make_async_remote_copy(src_ref=s_ref, dst_ref=so_ref.at[4 * x + 2 * y + c], send_sem=ssend_sems.at[k],
                                                      recv_sem=srecv_sems.at[k], device_id=to, device_id_type=MESH))
        for cp in sends:
            cp.start()
        for k, (cx, cy) in enumerate(chips):
            slot = o_ref.at[2 * cx + cy]
            pltpu.make_async_remote_copy(src_ref=slot, dst_ref=slot, send_sem=send_sems.at[k], recv_sem=recv_sems.at[k],
                                         device_id=(x, y, c), device_id_type=MESH).wait_recv()
        for k, (px, py, pc) in enumerate(peers):
            slot = so_ref.at[4 * px + 2 * py + pc]
            pltpu.make_async_remote_copy(src_ref=slot, dst_ref=slot, send_sem=ssend_sems.at[k], recv_sem=srecv_sems.at[k],
                                         device_id=(x, y, c), device_id_type=MESH).wait_recv()
        for cp in sends:
            cp.wait_send()
        mine_s.wait()

    return _pcall(body, name="rs_chips", in_specs=[ANY, ANY], out_specs=[ANY, ANY],
                  out_shape=[SDS(part.shape, part.dtype), SDS((N_DEV, 8, D), F32)],
                  scratch=[pltpu.SemaphoreType.DMA((3,)), pltpu.SemaphoreType.DMA((3,)), pltpu.SemaphoreType.DMA((7,)),
                           pltpu.SemaphoreType.DMA((7,)), pltpu.SemaphoreType.DMA])(part, small)


def _rs_add_chips(qidx, part, parts):
    HALF = part.shape[1]
    th = HALF // 4
    assert th % 16 == 0

    def body(q_ref, own_ref, p_ref, o_ref):
        for me in range(N_CHIPS):
            @pl.when(q_ref[0] == me)
            def _(me=me):
                t = [(own_ref[0] if j == me else p_ref[j]).astype(F32) for j in range(N_CHIPS)]
                o_ref[...] = ((t[0] + t[1]) + t[2]) + t[3]

    gs = pltpu.PrefetchScalarGridSpec(
        num_scalar_prefetch=1, grid=(HALF // th,),
        in_specs=[pl.BlockSpec((1, th, D), lambda i, q: (q[0], i, 0)), pl.BlockSpec((N_CHIPS, th, D), lambda i, q: (0, i, 0))],
        out_specs=pl.BlockSpec((th, D), lambda i, q: (i, 0)))
    return pl.pallas_call(body, name="rs_add_chips", grid_spec=gs, out_shape=SDS((HALF, D), F32),
                          compiler_params=pltpu.CompilerParams(dimension_semantics=("parallel",),
                                                               vmem_limit_bytes=48 << 20))(qidx, part, parts)


def _rs_join(early, late):
    def body(e_ref, l_ref, eo_ref, lo_ref, send_sems, recv_sems):
        x, y, c, _ = _place()
        cps = [pltpu.make_async_remote_copy(src_ref=src, dst_ref=dst, send_sem=send_sems.at[k], recv_sem=recv_sems.at[k],
                                            device_id=(x, y, 1 - c), device_id_type=MESH)
               for k, (src, dst) in enumerate([(e_ref, eo_ref), (l_ref, lo_ref)])]
        for cp in cps:
            cp.start()
        for cp in cps:
            cp.wait()

    return _pcall(body, name="rs_join", in_specs=[ANY, ANY], out_specs=[ANY, ANY],
                  out_shape=[SDS(early.shape, F32), SDS(late.shape, F32)],
                  scratch=[pltpu.SemaphoreType.DMA((2,)), pltpu.SemaphoreType.DMA((2,))])(early, late)


def _reduce_late_start(gpk, after):
    rows = gpk.shape[1]
    HALF = rows // 2
    assert HALF % 16 == 0

    def body(in_ref, land_ref, after_ref, send_sems, recv_sems, in_thru, land_thru, token):
        x, y, c, chips = _place()
        me = 4 * x + 2 * y + c
        peers = [(x, y, 1 - c)] + [(cx, cy, c) for cx, cy in chips] + [(cx, cy, 1 - c) for cx, cy in chips]
        for k, (px, py, pc) in enumerate(peers):
            src = in_ref.at[2 * px + py, pl.ds(pl.multiple_of(pc * HALF, 16), HALF)]
            pltpu.make_async_remote_copy(src_ref=src, dst_ref=land_ref.at[me], send_sem=send_sems.at[k], recv_sem=recv_sems.at[k],
                                         device_id=(px, py, pc), device_id_type=MESH).start()
        token[...] = jnp.zeros_like(token)

    return pl.pallas_call(
        body, name="reduce_late_start",
        out_shape=(pltpu.SemaphoreType.DMA((7,)), pltpu.SemaphoreType.DMA((7,)), pltpu.HBM(gpk.shape, gpk.dtype),
                   pltpu.HBM((N_DEV, HALF, D), gpk.dtype), SDS((8, LANES), F32)),
        in_specs=(HBM, HBM, ANY), out_specs=(SEM, SEM, HBM, HBM, pl.BlockSpec(memory_space=pltpu.VMEM)),
        input_output_aliases={0: 2, 1: 3}, compiler_params=pltpu.CompilerParams(has_side_effects=DATAFLOW),
    )(_in_hbm(gpk), _in_hbm(lax.empty((N_DEV, HALF, D), gpk.dtype)), after)


def _reduce_late_wait(send_sems, recv_sems, in_thru, land_thru, after):
    def body(in_ref, land_ref, send_sems, recv_sems, after_ref, in_out, got_ref):
        x, y, c, chips = _place()
        peers = [(x, y, 1 - c)] + [(cx, cy, c) for cx, cy in chips] + [(cx, cy, 1 - c) for cx, cy in chips]
        for k, (px, py, pc) in enumerate(peers):
            cp = pltpu.make_async_remote_copy(src_ref=land_ref.at[0], dst_ref=land_ref.at[4 * px + 2 * py + pc],
                                              send_sem=send_sems.at[k], recv_sem=recv_sems.at[k],
                                              device_id=(px, py, pc), device_id_type=MESH)
            cp.wait_send()
            cp.wait_recv()

    return pl.pallas_call(
        body, name="reduce_late_wait",
        out_shape=(pltpu.HBM(in_thru.shape, in_thru.dtype), pltpu.HBM(land_thru.shape, land_thru.dtype)),
        in_specs=(HBM, HBM, SEM, SEM, ANY), out_specs=(HBM, HBM), input_output_aliases={0: 0, 1: 1},
        compiler_params=pltpu.CompilerParams(has_side_effects=DATAFLOW),
    )(in_thru, land_thru, send_sems, recv_sems, after)


def _reduce_late_add(didx, gpk, parts):
    HALF = parts.shape[1]
    th = HALF // 4
    nh = HALF // th
    assert th % 16 == 0

    def body(d_ref, own_ref, p_ref, o_ref):
        for me in range(N_DEV):
            @pl.when(d_ref[0] == me)
            def _(me=me):
                t = [(own_ref[0] if j == me else p_ref[j]).astype(F32) for j in range(N_DEV)]
                o_ref[...] = ((((((t[0] + t[1]) + t[2]) + t[3]) + t[4]) + t[5]) + t[6]) + t[7]

    gs = pltpu.PrefetchScalarGridSpec(
        num_scalar_prefetch=1, grid=(nh,),
        in_specs=[pl.BlockSpec((1, th, D), lambda i, d: (d[1], d[2] * nh + i, 0)), pl.BlockSpec((N_DEV, th, D), lambda i, d: (0, i, 0))],
        out_specs=pl.BlockSpec((th, D), lambda i, d: (i, 0)))
    return pl.pallas_call(body, name="reduce_late_add", grid_spec=gs, out_shape=SDS((HALF, D), F32),
                          compiler_params=pltpu.CompilerParams(dimension_semantics=("parallel",),
                                                               vmem_limit_bytes=48 << 20))(didx, gpk, parts)


def _pack_early(b, dtype):
    lanes = lambda a: jnp.pad(a.astype(dtype), ((0, 0), (0, D - a.shape[1])))
    pair = jnp.concatenate([b["w_q_b"].astype(dtype), b["w_ple"].astype(dtype), jnp.zeros((256, D - 640), dtype)], axis=1)
    return jnp.concatenate([lanes(b["w_in"]), pair, lanes(b["w_kv_b"])], axis=0)


def _pack_late(b, dtype):
    return jnp.concatenate([b[n].astype(dtype) for n in ("w_mla_up", "w_swa_up", "w_out", "w_ple_gate", "w_mlp_up", "w_mlp_down")],
                           axis=0)


def _unpack_shards(pk, which):
    return {n: pk[PACK_AT[n][1]:PACK_AT[n][1] + r, PACK_AT[n][2]:PACK_AT[n][2] + c] for n, r, c in BIG if PACK_AT[n][0] == which}


def _full_weights(gathered, own, chip, which):
    own_b = _unpack_shards(own, which)
    per_chip = [{n: jnp.where(chip == j, own_b[n], blk) for n, blk in _unpack_shards(gathered[j], which).items()}
                for j in range(N_CHIPS)]
    out = {}
    for n in own_b:
        shards = [pc[n] for pc in per_chip]
        if n == "w_in":
            out["w_in_p"] = _w_in_internal(shards)
        else:
            out[n] = jnp.concatenate(shards, axis=1 if n in COL_SHARDED else 0)
    return out


def _split_full_grads(grads, pack, dtype):
    shard = {n: (r, c) for n, r, c in BIG}
    chunks = []
    for j in range(N_CHIPS):
        blocks = {}
        for n, g in grads.items():
            if n == "w_in_p":
                blocks["w_in"] = _w_in_grad_shard(g, j)
                continue
            r, c = shard[n]
            blocks[n] = g[:, j * c:(j + 1) * c] if n in COL_SHARDED else g[j * r:(j + 1) * r]
        chunks.append(pack(blocks, dtype))
    return jnp.stack(chunks)


W_IN_SHARD = 936
W_IN_SEGMENTS = ((0, 256, (3072,)), (256, 384, (3840,)), (384, 416, (4032,)), (416, 1440, (0,)), (1440, 1504, (3328, 3392)),
                 (1504, 1568, (3456, 3520)), (1568, 1632, (3584, 3648)), (1632, 1696, (3712, 3776)), (1696, 3744, (1024,)))


def _w_in_internal(shards):
    def cols(a, b):
        out = []
        for j, s in enumerate(shards):
            lo, hi = max(a, W_IN_SHARD * j), min(b, W_IN_SHARD * (j + 1))
            if lo < hi:
                out.append(s[:, lo - W_IN_SHARD * j:hi - W_IN_SHARD * j])
        return out

    pieces = {}
    for a, b, places in W_IN_SEGMENTS:
        for at in places:
            pieces[at] = cols(a, b)
    zeros = lambda n: [jnp.zeros((D, n), shards[0].dtype)]
    pieces[3968] = zeros(64)
    pieces[4064] = zeros(32)
    return jnp.concatenate([piece for at in sorted(pieces) for piece in pieces[at]], axis=1)


def _w_in_grad_shard(g, j):
    out = []
    for a, b, places in W_IN_SEGMENTS:
        lo, hi = max(a, W_IN_SHARD * j), min(b, W_IN_SHARD * (j + 1))
        if lo < hi:
            parts = [g[:, at + lo - a:at + hi - a] for at in places]
            out.append(parts[0] if len(parts) == 1 else parts[0] + parts[1])
    return jnp.concatenate(out, axis=1)


def _local_step(x, p, tgt, w, small, late_weights, late_grads_out):
    T = x.shape[0]
    tm = 256
    tb = 256
    w_in_p = w["w_in_p"]
    wqb = jnp.pad(w["w_q_b"].reshape(Q_LORA, MLA_HEADS, 96), ((0, 0), (0, 0), (0, 32))).reshape(Q_LORA, 2048)
    wkv = w["w_kv_b"].reshape(KV_LORA, MLA_HEADS, 128)
    wkn = jnp.pad(wkv[:, :, :64], ((0, 0), (0, 0), (0, 64))).reshape(KV_LORA, 2048)
    wv = wkv[:, :, 64:].reshape(KV_LORA, 1024)
    tab_m = _rope_tables(T, "mla")
    tab_s = _rope_tables(T, "swa")
    g1, gq, gkv, sinks = small["g_mix_pre"], small["g_q_a"], small["g_kv_a"], small["sinks"]
    g2, g3, g4, g5 = small["g_mix_post"], small["g_mlp_pre"], small["g_mlp_post"], small["g_ple"]
    sink_vec = sinks.reshape(SWA_HEADS)

    z, h1 = _fwd_in(x, g1, w_in_p, tm)
    qn, kvn, qm, km, vm, qt, kt, vt, qs, ks, vs = _fwd_qkv(z, gq, gkv, wqb, wkn, wv, tab_m, tab_s, tb)
    om, lse_m = _mla_fwd(qm, km, vt, tb)
    os_, lse_s = _swa_fwd(sink_vec, qs, ks, vs)
    w = {**w, **late_weights(os_)}
    y, yo, au, bu, x1 = _fwd_mix(om, os_, z, x, w["w_mla_up"], w["w_swa_up"], w["w_out"], g2, tm)
    h2, a, u = _fwd_mlp_up(x1, g3, w["w_mlp_up"], tm)
    d, x2 = _fwd_mlp_down(u, w["w_mlp_down"], x1, g4, tm)
    loss, dx2, dgt, de0, dg5 = _ple_fwd_bwd(p, x2, tgt, w["w_ple"], g5, w["w_ple_gate"], tm)

    dd, da, dg4 = _bwd_mlp_down(dx2, d, g4, w["w_mlp_down"], a, tm)
    dx1, dg3 = _bwd_mlp_up(da, w["w_mlp_up"], x1, g3, dx2, tm)
    dyo, dg2, dau, dbu, dga, dgb, dos, delta_m, dom_t = _bwd_mix(dx1, yo, g2, w["w_out"], z, au, bu, w["w_mla_up"],
                                                                w["w_swa_up"], om, tb)
    token = late_grads_out({
        "w_mla_up": _wgrad(om, dau, "wgrad_mla_up"),
        "w_swa_up": _wgrad(os_, dbu, "wgrad_swa_up"),
        "w_out": _wgrad(y, dyo, "wgrad_out"),
        "w_ple_gate": _wgrad(x2, dgt, "wgrad_ple_gate"),
        "w_mlp_up": _wgrad(h2, da, "wgrad_mlp_up"),
        "w_mlp_down": _wgrad(u, dd, "wgrad_mlp_down"),
    })
    delta_m = delta_m + token[0, 0]
    dqm, dkm, dvm = _mla_bwd(qm, qt, km, kt, vm, dom_t, lse_m, delta_m, tb)
    dqs, dkc, dkp, dvc, dvp, dsink = _swa_bwd(sink_vec, qs, ks, vs, dos, os_, lse_s)
    dqb, dknb, dvb, dsq, drest, dgq, dgkv = _bwd_qkv(dqm, dkm, dvm, dqs, dkc, dkp, dvc, dvp, z, gq, gkv, wqb, wkn, wv,
                                                      tab_m, tab_s)
    gx, dg1 = _bwd_in(dsq, dga, dgb, drest, w_in_p, x, g1, dx1, tm)

    g_in_p = jnp.concatenate([_wgrad(h1, dsq, "wgrad_in_sq"), _wgrad(h1, dga, "wgrad_in_ga"), _wgrad(h1, dgb, "wgrad_in_gb"),
                              _wgrad(h1, drest, "wgrad_in_rest")], axis=1)
    g_qb_p = _wgrad(qn, dqb, "wgrad_q_b")
    g_kn_p = _wgrad(kvn, dknb, "wgrad_kv_b_nope")
    g_v_p = _wgrad(kvn, dvb, "wgrad_kv_b_v")
    grads = {
        "w_in_p": g_in_p,
        "w_q_b": g_qb_p.reshape(Q_LORA, MLA_HEADS, 128)[:, :, :96].reshape(Q_LORA, 1536),
        "w_kv_b": jnp.concatenate([g_kn_p.reshape(KV_LORA, MLA_HEADS, 128)[:, :, :64], g_v_p.reshape(KV_LORA, MLA_HEADS, 64)],
                                  axis=2).reshape(KV_LORA, 2048),
        "w_ple": _wgrad(p, de0, "wgrad_ple"),
    }
    small_grads = {"g_mix_pre": dg1, "g_q_a": dgq, "g_kv_a": dgkv, "sinks": dsink[0:1, 0:SWA_HEADS], "g_mix_post": dg2,
                   "g_mlp_pre": dg3, "g_mlp_post": dg4, "g_ple": dg5}
    return loss, gx, grads, small_grads


def _pack_small(vals, fill):
    wide = [vals[n] for n, k in SMALL if k == D]
    narrow = [vals[n] for n, k in SMALL if k != D]
    used = sum(k for _, k in SMALL if k != D)
    last = jnp.concatenate(narrow + [jnp.full((1, D - used), fill, F32)], axis=1)
    return jnp.concatenate(wide + [last, jnp.full((2, D), fill, F32)], axis=0)


def _unpack_small(pk):
    out, row, off = {}, 0, 0
    for n, k in SMALL:
        if k == D:
            out[n] = pk[row:row + 1]
            row += 1
    for n, k in SMALL:
        if k != D:
            out[n] = pk[5:6, off:off + k]
            off += k
    return out


def kernel(x, p, g_mix_pre, w_in, g_q_a, w_q_b, g_kv_a, w_kv_b, sinks, w_mla_up, w_swa_up, w_out, g_mix_post, g_mlp_pre, w_mlp_up, w_mlp_down, g_mlp_post, w_ple, g_ple, w_ple_gate, loss_target, m_g_mix_pre, m_w_in, m_g_q_a, m_w_q_b, m_g_kv_a, m_w_kv_b, m_sinks, m_w_mla_up, m_w_swa_up, m_w_out, m_g_mix_post, m_g_mlp_pre, m_w_mlp_up, m_w_mlp_down, m_g_mlp_post, m_w_ple, m_g_ple, m_w_ple_gate, v_g_mix_pre, v_w_in, v_g_q_a, v_w_q_b, v_g_kv_a, v_w_kv_b, v_sinks, v_w_mla_up, v_w_swa_up, v_w_out, v_g_mix_post, v_g_mlp_pre, v_w_mlp_up, v_w_mlp_down, v_g_mlp_post, v_w_ple, v_g_ple, v_w_ple_gate):
    given = dict(locals())
    big_w = {n: given[n][0] for n, _, _ in BIG}
    small_w = {n: given[n] for n, _ in SMALL}
    small_m = {n: given["m_" + n] for n, _ in SMALL}
    small_v = {n: given["v_" + n] for n, _ in SMALL}

    core = lax.axis_index("c")
    chip = 2 * lax.axis_index("x") + lax.axis_index("y")
    core_i = core.astype(jnp.int32).reshape(1)
    chip_i = chip.astype(jnp.int32).reshape(1)
    dev_i = jnp.stack([2 * chip + core, chip, core]).astype(jnp.int32)

    own_early = _pack_early(big_w, BF16)
    own_late = _pack_late(big_w, BF16)
    got_early = _all_gather(own_early)
    late_flight = _gather_late_start(own_late, got_early)
    weights = _full_weights(got_early, own_early, chip, "early")
    step_small = {**small_w, "g_mix_pre": small_w["g_mix_pre"] + late_flight[4][0, 0]}

    def late_weights(after):
        return _full_weights(_gather_late_wait(*late_flight[:4], after), own_late, chip, "late")

    flight = {}

    def late_grads_out(grads):
        gpk_late = _split_full_grads(grads, _pack_late, BF16)
        flight["late"] = _reduce_late_start(gpk_late, dev_i)
        return flight["late"][4]

    loss_blk, gx, grads, small_grads = _local_step(x[0], p[0, 0], loss_target[0], weights, step_small, late_weights,
                                                   late_grads_out)

    gpk = _split_full_grads(grads, _pack_early, F32)
    got = _rs_sibling(gpk)
    part = _rs_add_sibling(core_i, gpk, got)
    parts, small_parts = _rs_chips(part, _pack_small(small_grads, 0.0))
    mine_early = _rs_add_chips(chip_i, part, parts)
    gpk_late, parts_late = _reduce_late_wait(*flight["late"][:4], mine_early)
    mine_late = _reduce_late_add(dev_i, gpk_late, parts_late)
    theirs_early, theirs_late = _rs_join(mine_early, mine_late)
    joined = {"early": jnp.where(core == 0, jnp.concatenate([mine_early, theirs_early]), jnp.concatenate([theirs_early, mine_early])),
              "late": jnp.where(core == 0, jnp.concatenate([mine_late, theirs_late]), jnp.concatenate([theirs_late, mine_late]))}

    loss = lax.psum(loss_blk[0, 0], ("x", "y", "c"))
    g_small_pk, d_small_pk, m_small_pk, v_small_pk = _adamw_small(
        _pack_small(small_w, 0.0), small_parts, _pack_small(small_m, 0.0), _pack_small(small_v, 1.0))
    g_small, d_small = _unpack_small(g_small_pk), _unpack_small(d_small_pk)
    m_small, v_small = _unpack_small(m_small_pk), _unpack_small(v_small_pk)

    out_g, out_d, out_m, out_v = dict(g_small), dict(d_small), dict(m_small), dict(v_small)
    for n, _, _ in BIG:
        out_g[n], out_d[n], out_m[n], out_v[n] = _adamw(given[n], joined[PACK_AT[n][0]], given["m_" + n], given["v_" + n], n)
    order = ["g_mix_pre", "w_in", "g_q_a", "w_q_b", "g_kv_a", "w_kv_b", "sinks", "w_mla_up", "w_swa_up", "w_out", "g_mix_post",
             "g_mlp_pre", "w_mlp_up", "w_mlp_down", "g_mlp_post", "w_ple", "g_ple", "w_ple_gate"]
    return (loss, gx[None], *[out_g[n] for n in order], *[out_d[n] for n in order], *[out_m[n] for n in order],
            *[out_v[n] for n in order])
```

```python
import math

import jax
import jax.numpy as jnp
from jax import lax
from jax.experimental import pallas as pl
from jax.experimental.pallas import tpu as pltpu

F32 = jnp.float32
BF16 = jnp.bfloat16
SDS = jax.ShapeDtypeStruct

D = 1024
D_FF = 4096
PLE = 256
Q_LORA = 256
KV_LORA = 128
MLA_HEADS = 16
MLA_NOPE = 64
MLA_ROPE = 32
SWA_HEADS = 16
SWA_HD = 64
WINDOW = 128
ROPE_THETA = 10000.0
EPS = 1e-6
NEG = -1e30
NZ = 4096
MLA_SCALE = (MLA_NOPE + MLA_ROPE) ** -0.5
LOG2_E = math.log2(math.e)
MLA_LOG2_SCALE = MLA_SCALE * LOG2_E
SWA_SCALE = SWA_HD ** -0.5

ADAM_LR = 0.001
ADAM_B1 = 0.9
ADAM_B2 = 0.999
ADAM_EPS = 1e-08
ADAM_WD = 0.01
ADAM_STEP = 10

LANES = 128
ATT_COLS = 128
N_CHIPS = 4
N_DEV = 8
MESH = pl.DeviceIdType.MESH

NT = (((1,), (1,)), ((), ()))
TN = (((0,), (0,)), ((), ()))

BIG = (("w_in", 1024, 936), ("w_q_b", 256, 384), ("w_kv_b", 128, 512), ("w_mla_up", 256, 1024),
       ("w_swa_up", 256, 1024), ("w_out", 256, 1024), ("w_mlp_up", 1024, 1024), ("w_mlp_down", 1024, 1024),
       ("w_ple", 256, 256), ("w_ple_gate", 256, 1024))
COL_SHARDED = ("w_in", "w_q_b", "w_kv_b", "w_mlp_up", "w_ple")
PACK_AT = {"w_in": ("early", 0, 0), "w_q_b": ("early", 1024, 0), "w_ple": ("early", 1024, 384), "w_kv_b": ("early", 1280, 0),
           "w_mla_up": ("late", 0, 0), "w_swa_up": ("late", 256, 0), "w_out": ("late", 512, 0), "w_ple_gate": ("late", 768, 0),
           "w_mlp_up": ("late", 1024, 0), "w_mlp_down": ("late", 2048, 0)}
PACK_ROWS = {"early": 1408, "late": 3072}
SMALL = (("g_mix_pre", 1024), ("g_q_a", 256), ("g_kv_a", 128), ("sinks", 16), ("g_mix_post", 1024),
         ("g_mlp_pre", 1024), ("g_mlp_post", 1024), ("g_ple", 1024))


def _dot(a, b):
    return jnp.dot(a, b, preferred_element_type=F32)


def _dot_nt(a, b):
    return lax.dot_general(a, b, NT, preferred_element_type=F32)


def _dot_tn(a, b):
    return lax.dot_general(a, b, TN, preferred_element_type=F32)


def _pcall(body, *, name, out_shape, grid=(), in_specs=None, out_specs=None, scratch=(), sem=None, vmem_mb=48):
    params = dict(vmem_limit_bytes=vmem_mb << 20)
    if sem is not None:
        params["dimension_semantics"] = sem
    return pl.pallas_call(body, name=name, grid=grid, in_specs=in_specs, out_specs=out_specs, out_shape=out_shape,
                          scratch_shapes=list(scratch), compiler_params=pltpu.CompilerParams(**params))


def _rows(tm, n, col=0):
    return pl.BlockSpec((tm, n), lambda i: (i, col))


def _full(shape):
    return pl.BlockSpec(shape, lambda i: (0,) * len(shape))


def _rms(x, g):
    r = lax.rsqrt(jnp.mean(x * x, axis=-1, keepdims=True) + EPS)
    return x * r * g


def _rms_bwd(dy, x, g):
    r = lax.rsqrt(jnp.mean(x * x, axis=-1, keepdims=True) + EPS)
    xn = x * r
    dn = dy * g
    dx = r * (dn - xn * jnp.mean(dn * xn, axis=-1, keepdims=True))
    return dx, jnp.sum(dy * xn, axis=0, keepdims=True)


def _sigmoid(x):
    return 1.0 / (1.0 + jnp.exp(-x))


def _rope(x, c, a, b, half):
    return x * c + pltpu.roll(x, LANES - half, 1) * a + pltpu.roll(x, half, 1) * b


def _rope_tables(T, kind):
    lane = jnp.arange(LANES)
    if kind == "mla":
        half = MLA_ROPE // 2
        rel = lane - MLA_NOPE
        on = (rel >= 0) & (rel < MLA_ROPE)
        d = MLA_ROPE
    else:
        half = SWA_HD // 2
        rel = lane % SWA_HD
        on = jnp.ones((LANES,), bool)
        d = SWA_HD
    first = on & (rel < half)
    second = on & (rel >= half)
    f = jnp.where(first, rel, rel - half).astype(F32)
    inv = jnp.exp(-math.log(ROPE_THETA) * f * (2.0 / d))
    ang = jnp.arange(T, dtype=F32)[:, None] * inv[None, :]
    cos, sin = jnp.cos(ang), jnp.sin(ang)
    c = jnp.where(on[None], cos, 1.0)
    a = jnp.where(first[None], -sin, 0.0)
    b = jnp.where(second[None], sin, 0.0)
    return c, a, b


def _fwd_in(x, g1, w_in_p, tm):
    T = x.shape[0]

    def body(x_ref, g_ref, w_ref, z_ref, h_ref):
        h = _rms(x_ref[...], g_ref[...]).astype(BF16)
        h_ref[...] = h
        z_ref[...] = _dot(h, w_ref[...])

    return _pcall(body, name="fwd_in", grid=(T // tm,),
                  in_specs=[_rows(tm, D), _full((1, D)), _full((D, NZ))],
                  out_specs=[_rows(tm, NZ), _rows(tm, D)],
                  out_shape=[SDS((T, NZ), F32), SDS((T, D), BF16)], sem=("parallel",))(x, g1, w_in_p)


def _fwd_qkv(z, gq, gkv, wqb, wkn, wv, tab_m, tab_s, tm):
    T = z.shape[0]

    def body(qa_ref, sq_ref, skd_ref, svd_ref, kva_ref, kr_ref, gq_ref, gkv_ref, wqb_ref, wkn_ref, wv_ref,
             cm_ref, am_ref, bm_ref, cs_ref, as_ref, bs_ref,
             qn_ref, kvn_ref, qm_ref, km_ref, vm_ref, qt_ref, kt_ref, vt_ref, qs_ref, ks_ref, vs_ref):
        qn = _rms(qa_ref[...], gq_ref[...]).astype(BF16)
        qn_ref[...] = qn
        kvn = _rms(kva_ref[...], gkv_ref[...]).astype(BF16)
        kvn_ref[...] = kvn
        cm, am, bm = cm_ref[...], am_ref[...], bm_ref[...]
        cs, as_, bs = cs_ref[...], as_ref[...], bs_ref[...]
        k_rope = _rope(kr_ref[...], cm, am, bm, MLA_ROPE // 2)
        vt_row = lax.broadcasted_iota(jnp.int32, (LANES, tm), 0)
        for j in range(D // LANES):
            sl = slice(LANES * j, LANES * (j + 1))
            v = _dot(kvn, wv_ref[:, sl])
            vm_ref[:, sl] = v.astype(BF16)
            v_t = v.T
            for hh, rows64 in enumerate((v_t, pltpu.roll(v_t, 64, 0))):
                blk = jnp.where(vt_row < 64, rows64, jnp.where(vt_row == 64, 1.0, 0.0))
                vt_ref[0, LANES * (2 * j + hh):LANES * (2 * j + hh + 1), :] = blk.astype(BF16)
        for h in range(MLA_HEADS):
            sl = slice(LANES * h, LANES * (h + 1))
            qh = _dot(qn, wqb_ref[:, sl])
            qh = _rope(qh, cm, am, bm, MLA_ROPE // 2)
            qm_ref[:, sl] = qh.astype(BF16)
            qt_ref[0, sl, :] = qh.T.astype(BF16)
            k = _dot(kvn, wkn_ref[:, sl]) + k_rope
            km_ref[:, sl] = k.astype(BF16)
            kt_ref[0, sl, :] = k.T.astype(BF16)
        for j in range(D // LANES):
            sl = slice(LANES * j, LANES * (j + 1))
            qs_ref[:, sl] = _rope(sq_ref[:, sl], cs, as_, bs, SWA_HD // 2).astype(BF16)
        for j in range(2):
            sl = slice(LANES * j, LANES * (j + 1))
            ks_ref[:, sl] = _rope(skd_ref[:, sl], cs, as_, bs, SWA_HD // 2).astype(BF16)
        vs_ref[...] = svd_ref[...].astype(BF16)

    tab = [_rows(tm, LANES)] * 6
    return _pcall(body, name="fwd_qkv", grid=(T // tm,),
                  in_specs=[_rows(tm, 256, 12), _rows(tm, 1024, 0), _rows(tm, 256, 13), _rows(tm, 256, 14),
                            _rows(tm, 128, 30), _rows(tm, 128, 31), _full((1, Q_LORA)), _full((1, KV_LORA)),
                            _full((Q_LORA, 2048)), _full((KV_LORA, 2048)), _full((KV_LORA, 1024))] + tab,
                  out_specs=[_rows(tm, Q_LORA), _rows(tm, KV_LORA), _rows(tm, 2048), _rows(tm, 2048), _rows(tm, 1024),
                             pl.BlockSpec((1, 2048, tm), lambda i: (i, 0, 0)), pl.BlockSpec((1, 2048, tm), lambda i: (i, 0, 0)),
                             pl.BlockSpec((1, 2048, tm), lambda i: (i, 0, 0)),
                             _rows(tm, 1024), _rows(tm, 256), _rows(tm, 256)],
                  out_shape=[SDS((T, Q_LORA), BF16), SDS((T, KV_LORA), BF16), SDS((T, 2048), BF16), SDS((T, 2048), BF16),
                             SDS((T, 1024), BF16), SDS((T // tm, 2048, tm), BF16), SDS((T // tm, 2048, tm), BF16),
                             SDS((T // tm, 2048, tm), BF16),
                             SDS((T, 1024), BF16), SDS((T, 256), BF16), SDS((T, 256), BF16)],
                  sem=("parallel",))(z, z, z, z, z, z, gq, gkv, wqb, wkn, wv, *tab_m, *tab_s)


def _mla_fwd(qm, km, vt, tb):
    T = qm.shape[0]
    nb = T // tb
    cc = ATT_COLS

    def body(q_ref, k_ref, vt_ref, o_ref, l_ref, s_ref, p_ref, al_ref, m_ref, acc_ref):
        i = pl.program_id(1)
        m_ref[...] = jnp.full(m_ref.shape, NEG, F32)
        acc_ref[...] = jnp.zeros_like(acc_ref)
        p_ref[1] = jnp.zeros(p_ref.shape[1:], BF16)
        al_ref[1] = jnp.ones(al_ref.shape[1:], F32)
        key = lax.broadcasted_iota(jnp.int32, (tb, cc), 0)
        qry = lax.broadcasted_iota(jnp.int32, (tb, cc), 1)

        def scores(j, slot):
            off = pl.multiple_of(j * tb, tb)
            for hh in range(2):
                sl = slice(LANES * hh, LANES * (hh + 1))
                s_ref[slot, hh] = _dot_nt(k_ref[pl.ds(off, tb), sl], q_ref[:, sl])

        def softmax(slot, diagonal):
            chains = [(hh, slice(cc * c, cc * (c + 1)), c) for hh in range(2) for c in range(tb // cc)]

            def scaled(hh, cols, c):
                t = s_ref[slot, hh, :, cols] * MLA_LOG2_SCALE
                return jnp.where(key <= qry + cc * c, t, NEG) if diagonal else t

            tops = []
            for hh, cols, c in chains:
                if diagonal:
                    top = jnp.max(scaled(hh, cols, c), axis=0, keepdims=True)
                else:
                    top = jnp.max(s_ref[slot, hh, :, cols], axis=0, keepdims=True) * MLA_LOG2_SCALE
                m_old = m_ref[hh, :, cols]
                mn = jnp.maximum(m_old, top)
                m_ref[hh, :, cols] = mn
                al_ref[slot, hh, :, cols] = jnp.exp2(m_old - mn)
                tops.append(mn)
            for (hh, cols, c), mn in zip(chains, tops):
                p_ref[slot, hh, :, cols] = jnp.exp2(scaled(hh, cols, c) - mn).astype(BF16)

        def accumulate(j, slot):
            for hh in range(2):
                acc_ref[hh] = al_ref[slot, hh] * acc_ref[hh] + _dot(vt_ref[j, LANES * hh:LANES * (hh + 1), :], p_ref[slot, hh])

        def step(t, carry):
            scores(2 * t + 1, 1)
            accumulate(jnp.maximum(2 * t - 1, 0), 1)
            softmax(0, False)
            scores(2 * t + 2, 0)
            accumulate(2 * t, 0)
            softmax(1, False)
            return carry

        scores(0, 0)
        lax.fori_loop(0, i // 2, step, 0)

        @pl.when(i % 2 == 1)
        def _():
            scores(i, 1)
            accumulate(jnp.maximum(i - 2, 0), 1)
            softmax(0, False)
            accumulate(i - 1, 0)
            softmax(1, True)
            accumulate(i, 1)

        @pl.when(i % 2 == 0)
        def _():
            accumulate(jnp.maximum(i - 1, 0), 1)
            softmax(0, True)
            accumulate(i, 0)
        den = [acc_ref[hh, 64:65, :] for hh in range(2)]
        o_ref[...] = jnp.concatenate([acc_ref[hh, 0:64, :] / den[hh] for hh in range(2)], axis=0).T
        sub = lax.broadcasted_iota(jnp.int32, (8, tb), 0)
        lse = [m_ref[hh] + jnp.log(den[hh]) * LOG2_E for hh in range(2)]
        l_ref[0, 0] = jnp.where(sub == 0, lse[0], jnp.where(sub == 1, lse[1], 0.0))

    return _pcall(body, name="mla_fwd", grid=(MLA_HEADS // 2, nb),
                  in_specs=[pl.BlockSpec((tb, 256), lambda p, i: (i, p)), pl.BlockSpec((T, 256), lambda p, i: (0, p)),
                            pl.BlockSpec((nb, 2 * LANES, tb), lambda p, i: (0, p, 0))],
                  out_specs=[pl.BlockSpec((tb, LANES), lambda p, i: (i, p)),
                             pl.BlockSpec((1, 1, 8, tb), lambda p, i: (p, i, 0, 0))],
                  out_shape=[SDS((T, D), F32), SDS((MLA_HEADS // 2, nb, 8, tb), F32)],
                  scratch=[pltpu.VMEM((2, 2, tb, tb), F32), pltpu.VMEM((2, 2, tb, tb), BF16), pltpu.VMEM((2, 2, 1, tb), F32),
                           pltpu.VMEM((2, 1, tb), F32), pltpu.VMEM((2, LANES, tb), F32)],
                  sem=("parallel", "arbitrary"))(qm, km, vt)


def _swa_mask(n):
    row = lax.broadcasted_iota(jnp.int32, (WINDOW, 2 * WINDOW), 0)
    col = lax.broadcasted_iota(jnp.int32, (WINDOW, 2 * WINDOW), 1)
    rel = row - col + WINDOW
    return (rel >= 0) & (rel < WINDOW) & ((col >= WINDOW) | (n > 0))


def _swa_specs(T):
    nb = T // WINDOW
    cur = lambda w: pl.BlockSpec((WINDOW, w), lambda n: (n, 0))
    prev = lambda w: pl.BlockSpec((WINDOW, w), lambda n: (jnp.maximum(n - 1, 0), 0))
    return nb, cur, prev


def _swa_fwd(sinks, qs, ks, vs):
    T = qs.shape[0]
    nb, cur, prev = _swa_specs(T)

    def body(sink_ref, q_ref, kc_ref, kp_ref, vc_ref, vp_ref, o_ref, l_ref):
        n = pl.program_id(0)
        mask = _swa_mask(n)
        lo = lax.broadcasted_iota(jnp.int32, (WINDOW, LANES), 1) < 64
        for g in range(2):
            gs = slice(LANES * g, LANES * (g + 1))
            kb = jnp.concatenate([kp_ref[:, gs], kc_ref[:, gs]], axis=0)
            vb = jnp.concatenate([vp_ref[:, gs], vc_ref[:, gs]], axis=0)
            for jj in range(4):
                j = 4 * g + jj
                sl = slice(LANES * j, LANES * (j + 1))
                qp = q_ref[:, sl]
                outs, lses = [], []
                for hf in range(2):
                    hm = lo if hf == 0 else jnp.logical_not(lo)
                    qh = jnp.where(hm, qp, jnp.zeros_like(qp))
                    s = jnp.where(mask, _dot_nt(qh, kb) * SWA_SCALE, NEG)
                    sk = sink_ref[2 * j + hf]
                    m = jnp.maximum(jnp.max(s, axis=1, keepdims=True), sk)
                    e = jnp.exp(s - m)
                    den = jnp.sum(e, axis=1, keepdims=True) + jnp.exp(sk - m)
                    p = e / den
                    outs.append(_dot(p.astype(BF16), vb))
                    lses.append(jnp.broadcast_to(m + jnp.log(den), (WINDOW, LANES)))
                o_ref[:, sl] = jnp.where(lo, outs[0], outs[1])
                l_ref[:, sl] = jnp.where(lo, lses[0], lses[1])

    return _pcall(body, name="swa_fwd", grid=(nb,),
                  in_specs=[pl.BlockSpec(memory_space=pltpu.SMEM), cur(D), cur(256), prev(256), cur(256), prev(256)],
                  out_specs=[cur(D), cur(D)], out_shape=[SDS((T, D), F32)] * 2,
                  sem=("parallel",))(sinks, qs, ks, ks, vs, vs)


def _fwd_mix(om, os_, z, x, wmu, wsu, wo, g2, tm):
    T = x.shape[0]

    def body(om_ref, os_ref, ga_ref, gb_ref, x_ref, wmu_ref, wsu_ref, wo_ref, g2_ref,
             y_ref, yo_ref, au_ref, bu_ref, x1_ref):
        au = _dot(om_ref[...].astype(BF16), wmu_ref[...])
        bu = _dot(os_ref[...].astype(BF16), wsu_ref[...])
        au_ref[...] = au
        bu_ref[...] = bu
        y = (_sigmoid(ga_ref[...]) * au + _sigmoid(gb_ref[...]) * bu).astype(BF16)
        y_ref[...] = y
        yo = _dot(y, wo_ref[...])
        yo_ref[...] = yo
        x1_ref[...] = x_ref[...] + _rms(yo, g2_ref[...])

    r = _rows(tm, D)
    w = _full((D, D))
    return _pcall(body, name="fwd_mix", grid=(T // tm,),
                  in_specs=[r, r, _rows(tm, D, 1), _rows(tm, D, 2), r, w, w, w, _full((1, D))],
                  out_specs=[r] * 5,
                  out_shape=[SDS((T, D), BF16), SDS((T, D), F32), SDS((T, D), F32), SDS((T, D), F32), SDS((T, D), F32)],
                  sem=("parallel",))(om, os_, z, z, x, wmu, wsu, wo, g2)


def _fwd_mlp_up(x1, g3, w1, tm):
    T = x1.shape[0]

    def body(x_ref, g_ref, w_ref, h_ref, a_ref, u_ref):
        h = _rms(x_ref[...], g_ref[...]).astype(BF16)
        h_ref[...] = h
        a = _dot(h, w_ref[...])
        a_ref[...] = a
        u_ref[...] = jnp.square(jnp.maximum(a, 0.0)).astype(BF16)

    return _pcall(body, name="fwd_mlp_up", grid=(T // tm,),
                  in_specs=[_rows(tm, D), _full((1, D)), _full((D, D_FF))],
                  out_specs=[_rows(tm, D), _rows(tm, D_FF), _rows(tm, D_FF)],
                  out_shape=[SDS((T, D), BF16), SDS((T, D_FF), F32), SDS((T, D_FF), BF16)],
                  sem=("parallel",))(x1, g3, w1)


def _fwd_mlp_down(u, w2, x1, g4, tm):
    T = x1.shape[0]

    def body(u_ref, w_ref, x_ref, g_ref, d_ref, x2_ref):
        d = _dot(u_ref[...], w_ref[...])
        d_ref[...] = d
        x2_ref[...] = x_ref[...] + _rms(d, g_ref[...])

    return _pcall(body, name="fwd_mlp_down", grid=(T // tm,),
                  in_specs=[_rows(tm, D_FF), _full((D_FF, D)), _rows(tm, D), _full((1, D))],
                  out_specs=[_rows(tm, D), _rows(tm, D)], out_shape=[SDS((T, D), F32)] * 2,
                  sem=("parallel",))(u, w2, x1, g4)


def _ple_fwd_bwd(p, x2, tgt, wple, g5, wpg, tm):
    T = x2.shape[0]

    def body(p_ref, x2_ref, t_ref, wple_ref, g5_ref, wpg_ref, loss_ref, dx2_ref, dgt_ref, de0_ref, dg5_ref):
        @pl.when(pl.program_id(0) == 0)
        def _():
            loss_ref[...] = jnp.zeros_like(loss_ref)
            dg5_ref[...] = jnp.zeros_like(dg5_ref)

        e0 = _dot(p_ref[...].astype(BF16), wple_ref[...])
        g5 = g5_ref[...]
        r = lax.rsqrt(jnp.mean(e0 * e0, axis=-1, keepdims=True) + EPS)
        en = e0 * r
        e = en * g5
        x2 = x2_ref[...]
        s = _sigmoid(_dot(x2.astype(BF16), wpg_ref[...]))
        diff = x2 + s * e - t_ref[...]
        sq = jnp.sum(jnp.sum(diff * diff, axis=1, keepdims=True), axis=0, keepdims=True)
        loss_ref[...] += jnp.broadcast_to(sq * (0.5 / D), loss_ref.shape)
        dx3 = diff * (1.0 / D)
        de = dx3 * s
        dgt = (dx3 * e * s * (1.0 - s)).astype(BF16)
        dgt_ref[...] = dgt
        dn = de * g5
        de0_ref[...] = (r * (dn - en * jnp.mean(dn * en, axis=-1, keepdims=True))).astype(BF16)
        dg5_ref[...] += jnp.sum(de * en, axis=0, keepdims=True)
        dx2_ref[...] = dx3 + _dot_nt(dgt, wpg_ref[...])

    r = _rows(tm, D)
    return _pcall(body, name="ple_fwd_bwd", grid=(T // tm,),
                  in_specs=[_rows(tm, PLE), r, r, _full((PLE, D)), _full((1, D)), _full((D, D))],
                  out_specs=[_full((8, LANES)), r, r, r, _full((1, D))],
                  out_shape=[SDS((8, LANES), F32), SDS((T, D), F32), SDS((T, D), BF16), SDS((T, D), BF16), SDS((1, D), F32)],
                  sem=("arbitrary",))(p, x2, tgt, wple, g5, wpg)


def _bwd_mlp_down(dx2, d, g4, w2, a, tm):
    T = dx2.shape[0]

    def body(dx_ref, d_ref, g_ref, w_ref, a_ref, dd_ref, da_ref, dg_ref):
        @pl.when(pl.program_id(0) == 0)
        def _():
            dg_ref[...] = jnp.zeros_like(dg_ref)

        dd, dg = _rms_bwd(dx_ref[...], d_ref[...], g_ref[...])
        dg_ref[...] += dg
        ddb = dd.astype(BF16)
        dd_ref[...] = ddb
        du = _dot_nt(ddb, w_ref[...])
        da_ref[...] = (du * (2.0 * jnp.maximum(a_ref[...], 0.0))).astype(BF16)

    return _pcall(body, name="bwd_mlp_down", grid=(T // tm,),
                  in_specs=[_rows(tm, D), _rows(tm, D), _full((1, D)), _full((D_FF, D)), _rows(tm, D_FF)],
                  out_specs=[_rows(tm, D), _rows(tm, D_FF), _full((1, D))],
                  out_shape=[SDS((T, D), BF16), SDS((T, D_FF), BF16), SDS((1, D), F32)],
                  sem=("arbitrary",))(dx2, d, g4, w2, a)


def _bwd_mlp_up(da, w1, x1, g3, dx2, tm):
    T = dx2.shape[0]

    def body(da_ref, w_ref, x_ref, g_ref, dx2_ref, dx1_ref, dg_ref):
        @pl.when(pl.program_id(0) == 0)
        def _():
            dg_ref[...] = jnp.zeros_like(dg_ref)

        dh = _dot_nt(da_ref[...], w_ref[...])
        dx, dg = _rms_bwd(dh, x_ref[...], g_ref[...])
        dg_ref[...] += dg
        dx1_ref[...] = dx2_ref[...] + dx

    return _pcall(body, name="bwd_mlp_up", grid=(T // tm,),
                  in_specs=[_rows(tm, D_FF), _full((D, D_FF)), _rows(tm, D), _full((1, D)), _rows(tm, D)],
                  out_specs=[_rows(tm, D), _full((1, D))],
                  out_shape=[SDS((T, D), F32), SDS((1, D), F32)], sem=("arbitrary",))(da, w1, x1, g3, dx2)


def _bwd_mix(dx1, yo, g2, wo, z, au, bu, wmu, wsu, om, tm):
    T = dx1.shape[0]

    def body(dx_ref, yo_ref, g_ref, wo_ref, ga_ref, gb_ref, au_ref, bu_ref, wmu_ref, wsu_ref, om_ref,
             dyo_ref, dg_ref, dau_ref, dbu_ref, dga_ref, dgb_ref, dos_ref, dl_ref, dot_ref):
        @pl.when(pl.program_id(0) == 0)
        def _():
            dg_ref[...] = jnp.zeros_like(dg_ref)

        dyo, dg = _rms_bwd(dx_ref[...], yo_ref[...], g_ref[...])
        dg_ref[...] += dg
        dyob = dyo.astype(BF16)
        dyo_ref[...] = dyob
        dy = _dot_nt(dyob, wo_ref[...])
        sa = _sigmoid(ga_ref[...])
        sb = _sigmoid(gb_ref[...])
        dau = (dy * sa).astype(BF16)
        dbu = (dy * sb).astype(BF16)
        dau_ref[...] = dau
        dbu_ref[...] = dbu
        dga_ref[...] = (dy * au_ref[...] * sa * (1.0 - sa)).astype(BF16)
        dgb_ref[...] = (dy * bu_ref[...] * sb * (1.0 - sb)).astype(BF16)
        dom = _dot_nt(dau, wmu_ref[...])
        dos_ref[...] = _dot_nt(dbu, wsu_ref[...])
        prod = dom * om_ref[...]
        sub = lax.broadcasted_iota(jnp.int32, (8, tm), 0)
        for pr in range(MLA_HEADS // 2):
            sl = slice(LANES * pr, LANES * (pr + 1))
            pt = prod[:, sl].T
            d0 = jnp.sum(pt[0:64], axis=0, keepdims=True)
            d1 = jnp.sum(pt[64:128], axis=0, keepdims=True)
            dl_ref[pr, 0] = jnp.where(sub == 0, d0, jnp.where(sub == 1, d1, 0.0))
            dot_ref[0, sl, :] = dom[:, sl].T.astype(BF16)

    r = _rows(tm, D)
    w = _full((D, D))
    return _pcall(body, name="bwd_mix", grid=(T // tm,),
                  in_specs=[r, r, _full((1, D)), w, _rows(tm, D, 1), _rows(tm, D, 2), r, r, w, w, r],
                  out_specs=[r, _full((1, D)), r, r, r, r, r, pl.BlockSpec((MLA_HEADS // 2, 1, 8, tm), lambda i: (0, i, 0, 0)),
                             pl.BlockSpec((1, D, tm), lambda i: (i, 0, 0))],
                  out_shape=[SDS((T, D), BF16), SDS((1, D), F32), SDS((T, D), BF16), SDS((T, D), BF16), SDS((T, D), BF16),
                             SDS((T, D), BF16), SDS((T, D), F32), SDS((MLA_HEADS // 2, T // tm, 8, tm), F32),
                             SDS((T // tm, D, tm), BF16)],
                  sem=("arbitrary",))(dx1, yo, g2, wo, z, z, au, bu, wmu, wsu, om)


def _mla_bwd(qm, qt, km, kt, vm, dot, lse, delta, tb):
    T = qm.shape[0]
    nb = T // tb
    cc = ATT_COLS

    def body(q_ref, qt_ref, k_ref, kt_ref, v_ref, dot_ref, l_ref, dl_ref, dqt_ref, dkt_ref, dvt_ref,
             s_ref, dp_ref, p_ref, ds_ref, vh_ref):
        j = pl.program_id(1)

        @pl.when(j == 0)
        def _():
            dqt_ref[...] = jnp.zeros_like(dqt_ref)

        dkt_ref[...] = jnp.zeros_like(dkt_ref)
        dvt_ref[...] = jnp.zeros_like(dvt_ref)
        lo = lax.broadcasted_iota(jnp.int32, (tb, LANES), 1) < 64
        key = lax.broadcasted_iota(jnp.int32, (tb, cc), 0)
        qry = lax.broadcasted_iota(jnp.int32, (tb, cc), 1)
        v = v_ref[...]
        vh_ref[0] = jnp.where(lo, v, jnp.zeros_like(v))
        vh_ref[1] = jnp.where(lo, jnp.zeros_like(v), v)

        def scores(i, slot):
            rows_i = pl.ds(pl.multiple_of(i * tb, tb), tb)
            for hh in range(2):
                sl = slice(LANES * hh, LANES * (hh + 1))
                s_ref[slot, hh] = _dot_nt(k_ref[:, sl], q_ref[rows_i, sl])
                dp_ref[slot, hh] = _dot(vh_ref[hh], dot_ref[i])

        def softmax_bwd(i, slot, diagonal):
            lse_i = l_ref[0, i]
            delta_i = dl_ref[0, i]
            for hh in range(2):
                for c in range(tb // cc):
                    cols = slice(cc * c, cc * (c + 1))
                    p = jnp.exp2(s_ref[slot, hh, :, cols] * MLA_LOG2_SCALE - lse_i[hh:hh + 1, cols])
                    if diagonal:
                        p = jnp.where(key <= qry + cc * c, p, 0.0)
                    p_ref[slot, hh, :, cols] = p.astype(BF16)
                    ds_ref[slot, hh, :, cols] = (p * (dp_ref[slot, hh, :, cols] - delta_i[hh:hh + 1, cols])
                                                 * MLA_SCALE).astype(BF16)

        def accumulate(i, slot):
            for hh in range(2):
                sl = slice(LANES * hh, LANES * (hh + 1))
                half = slice(64 * hh, 64 * (hh + 1))
                dvt_ref[0, half, :] += _dot_nt(dot_ref[i, half, :], p_ref[slot, hh])
                dkt_ref[0, sl, :] += _dot_nt(qt_ref[i, sl, :], ds_ref[slot, hh])
                dqt_ref[i, sl, :] += _dot(kt_ref[0, sl, :], ds_ref[slot, hh])

        p_ref[1] = jnp.zeros(p_ref.shape[1:], BF16)
        ds_ref[1] = jnp.zeros(ds_ref.shape[1:], BF16)
        n_off = nb - 1 - j

        def step(u, carry):
            i0 = j + 1 + 2 * u
            scores(i0 + 1, 1)
            accumulate(i0 - 1, 1)
            softmax_bwd(i0, 0, False)
            scores(jnp.where(i0 + 2 < nb, i0 + 2, j), 0)
            accumulate(i0, 0)
            softmax_bwd(i0 + 1, 1, False)
            return carry

        scores(jnp.where(n_off > 0, j + 1, j), 0)
        lax.fori_loop(0, n_off // 2, step, 0)
        pending = j + 2 * (n_off // 2)

        @pl.when(n_off % 2 == 1)
        def _():
            scores(j, 1)
            accumulate(pending, 1)
            softmax_bwd(nb - 1, 0, False)
            accumulate(nb - 1, 0)
            softmax_bwd(j, 1, True)
            accumulate(j, 1)

        @pl.when(n_off % 2 == 0)
        def _():
            accumulate(pending, 1)
            softmax_bwd(j, 0, True)
            accumulate(j, 0)

    blk = lambda w: pl.BlockSpec((tb, w), lambda p, j: (j, p))
    stat = pl.BlockSpec((1, nb, 8, tb), lambda p, j: (p, 0, 0, 0))
    pair_t = lambda w: pl.BlockSpec((nb, w, tb), lambda p, j: (0, p, 0))
    blk_t = lambda w: pl.BlockSpec((1, w, tb), lambda p, j: (j, p, 0))
    return _pcall(body, name="mla_bwd", grid=(MLA_HEADS // 2, nb),
                  in_specs=[pl.BlockSpec((T, 256), lambda p, j: (0, p)), pair_t(256), blk(256), blk_t(256), blk(LANES),
                            pair_t(LANES), stat, stat],
                  out_specs=[pair_t(256), blk_t(256), blk_t(LANES)],
                  out_shape=[SDS((nb, 2048, tb), F32), SDS((nb, 2048, tb), F32), SDS((nb, D, tb), F32)],
                  scratch=[pltpu.VMEM((2, 2, tb, tb), F32), pltpu.VMEM((2, 2, tb, tb), F32), pltpu.VMEM((2, 2, tb, tb), BF16),
                           pltpu.VMEM((2, 2, tb, tb), BF16), pltpu.VMEM((2, tb, LANES), BF16)],
                  sem=("parallel", "arbitrary"))(qm, qt, km, kt, vm, dot, lse, delta)


def _swa_bwd(sinks, qs, ks, vs, do, o, lse):
    T = qs.shape[0]
    nb, cur, prev = _swa_specs(T)

    def body(sink_ref, q_ref, kc_ref, kp_ref, vc_ref, vp_ref, do_ref, o_ref, l_ref,
             dq_ref, dkc_ref, dkp_ref, dvc_ref, dvp_ref, dsink_ref):
        n = pl.program_id(0)

        @pl.when(n == 0)
        def _():
            dsink_ref[...] = jnp.zeros_like(dsink_ref)

        mask = _swa_mask(n)
        lo = lax.broadcasted_iota(jnp.int32, (WINDOW, LANES), 1) < 64
        lane8 = lax.broadcasted_iota(jnp.int32, (8, LANES), 1)
        dsink = jnp.zeros((8, LANES), F32)
        for g in range(2):
            gs = slice(LANES * g, LANES * (g + 1))
            kb = jnp.concatenate([kp_ref[:, gs], kc_ref[:, gs]], axis=0)
            vb = jnp.concatenate([vp_ref[:, gs], vc_ref[:, gs]], axis=0)
            dkb = jnp.zeros((2 * WINDOW, LANES), F32)
            dvb = jnp.zeros((2 * WINDOW, LANES), F32)
            for jj in range(4):
                j = 4 * g + jj
                sl = slice(LANES * j, LANES * (j + 1))
                qp = q_ref[:, sl]
                d_o = do_ref[:, sl]
                prod = d_o * o_ref[:, sl]
                lse_b = l_ref[:, sl]
                dqs = []
                for hf in range(2):
                    hm = lo if hf == 0 else jnp.logical_not(lo)
                    qh = jnp.where(hm, qp, jnp.zeros_like(qp))
                    s = jnp.where(mask, _dot_nt(qh, kb) * SWA_SCALE, NEG)
                    lse_h = jnp.max(jnp.where(hm, lse_b, -jnp.inf), axis=1, keepdims=True)
                    p = jnp.exp(s - lse_h)
                    dom = jnp.where(hm, d_o, 0.0).astype(BF16)
                    dp = _dot_nt(dom, vb)
                    delta = jnp.sum(jnp.where(hm, prod, 0.0), axis=1, keepdims=True)
                    ds = (p * (dp - delta) * SWA_SCALE).astype(BF16)
                    p_sink = jnp.exp(sink_ref[2 * j + hf] - lse_h)
                    d_sink = -jnp.sum(p_sink * delta, axis=0, keepdims=True)
                    dsink = dsink + jnp.where(lane8 == 2 * j + hf, d_sink, 0.0)
                    dvb = dvb + _dot_tn(p.astype(BF16), dom)
                    dkb = dkb + _dot_tn(ds, qh)
                    dqs.append(_dot(ds, kb))
                dq_ref[:, sl] = jnp.where(lo, dqs[0], dqs[1])
            dkp_ref[:, gs] = dkb[:WINDOW]
            dkc_ref[:, gs] = dkb[WINDOW:]
            dvp_ref[:, gs] = dvb[:WINDOW]
            dvc_ref[:, gs] = dvb[WINDOW:]
        dsink_ref[...] += dsink

    return _pcall(body, name="swa_bwd", grid=(nb,),
                  in_specs=[pl.BlockSpec(memory_space=pltpu.SMEM), cur(D), cur(256), prev(256), cur(256), prev(256),
                            cur(D), cur(D), cur(D)],
                  out_specs=[cur(D), cur(256), cur(256), cur(256), cur(256), _full((8, LANES))],
                  out_shape=[SDS((T, D), F32), SDS((T, 256), F32), SDS((T, 256), F32), SDS((T, 256), F32), SDS((T, 256), F32),
                             SDS((8, LANES), F32)],
                  sem=("arbitrary",))(sinks, qs, ks, ks, vs, vs, do, o, lse)


def _bwd_qkv(dqm, dkm, dvm, dqs, dkc, dkp, dvc, dvp, z, gq, gkv, wqb, wkn, wv, tab_m, tab_s):
    T = z.shape[0]
    tm = WINDOW
    nb = T // tm
    per = dqm.shape[2] // tm

    def body(dqm_ref, dkm_ref, dvm_ref, dqs_ref, dkc_ref, dkp_ref, dvc_ref, dvp_ref, qa_ref, kva_ref, gq_ref, gkv_ref,
             wqb_ref, wkn_ref, wv_ref, cm_ref, am_ref, bm_ref, cs_ref, as_ref, bs_ref,
             dq_out, dkn_out, dv_out, dsq_ref, drest_ref, dgq_ref, dgkv_ref):
        i = pl.program_id(0)

        @pl.when(i == 0)
        def _():
            dgq_ref[...] = jnp.zeros_like(dgq_ref)
            dgkv_ref[...] = jnp.zeros_like(dgkv_ref)

        cm, am, bm = cm_ref[...], -am_ref[...], -bm_ref[...]
        cs, as_, bs = cs_ref[...], -as_ref[...], -bs_ref[...]
        lane = lax.broadcasted_iota(jnp.int32, (tm, LANES), 1)
        nope = lane < MLA_NOPE
        roped = jnp.logical_and(lane >= MLA_NOPE, lane < MLA_NOPE + MLA_ROPE)
        dkr = jnp.zeros((tm, LANES), F32)
        dqn = jnp.zeros((tm, Q_LORA), F32)
        dkvn = jnp.zeros((tm, KV_LORA), F32)
        for h in range(MLA_HEADS):
            sl = slice(LANES * h, LANES * (h + 1))
            dq_h = _rope(dqm_ref[0, sl, :].T, cm, am, bm, MLA_ROPE // 2).astype(BF16)
            dq_out[:, sl] = dq_h
            dqn = dqn + _dot_nt(dq_h, wqb_ref[:, sl])
            dk_h = dkm_ref[0, sl, :].T
            dkn_h = jnp.where(nope, dk_h, 0.0).astype(BF16)
            dkn_out[:, sl] = dkn_h
            dkvn = dkvn + _dot_nt(dkn_h, wkn_ref[:, sl])
            dkr = dkr + jnp.where(roped, dk_h, 0.0)
        for j in range(D // LANES):
            sl = slice(LANES * j, LANES * (j + 1))
            dvb = dvm_ref[0, sl, :].T.astype(BF16)
            dv_out[:, sl] = dvb
            dkvn = dkvn + _dot_nt(dvb, wv_ref[:, sl])
        dqa, dgq = _rms_bwd(dqn, qa_ref[...], gq_ref[...])
        dkva, dgkv = _rms_bwd(dkvn, kva_ref[...], gkv_ref[...])
        dgq_ref[...] += dgq
        dgkv_ref[...] += dgkv
        for j in range(D // LANES):
            sl = slice(LANES * j, LANES * (j + 1))
            dsq_ref[:, sl] = _rope(dqs_ref[:, sl], cs, as_, bs, SWA_HD // 2).astype(BF16)
        keep = (i < nb - 1).astype(F32)
        drest_ref[:, 0:256] = dqa.astype(BF16)
        for j in range(2):
            sl = slice(LANES * j, LANES * (j + 1))
            dk = dkc_ref[:, sl] + keep * dkp_ref[:, sl]
            drest_ref[:, 256 + LANES * j:256 + LANES * (j + 1)] = _rope(dk, cs, as_, bs, SWA_HD // 2).astype(BF16)
        drest_ref[:, 512:768] = (dvc_ref[...] + keep * dvp_ref[...]).astype(BF16)
        drest_ref[:, 768:896] = dkva.astype(BF16)
        drest_ref[:, 896:1024] = _rope(dkr, cm, am, bm, MLA_ROPE // 2).astype(BF16)

    nxt = pl.BlockSpec((tm, 256), lambda i: (jnp.minimum(i + 1, nb - 1), 0))
    tab = [_rows(tm, LANES)] * 6
    return _pcall(body, name="bwd_qkv", grid=(nb,),
                  in_specs=[pl.BlockSpec((1, 2048, tm), lambda i: (i // per, 0, i % per)),
                            pl.BlockSpec((1, 2048, tm), lambda i: (i // per, 0, i % per)),
                            pl.BlockSpec((1, 1024, tm), lambda i: (i // per, 0, i % per)), _rows(tm, 1024), _rows(tm, 256), nxt,
                            _rows(tm, 256), nxt, _rows(tm, 256, 12), _rows(tm, 128, 30), _full((1, Q_LORA)), _full((1, KV_LORA)),
                            _full((Q_LORA, 2048)), _full((KV_LORA, 2048)), _full((KV_LORA, 1024))] + tab,
                  out_specs=[_rows(tm, 2048), _rows(tm, 2048), _rows(tm, 1024), _rows(tm, 1024), _rows(tm, 1024),
                             _full((1, Q_LORA)), _full((1, KV_LORA))],
                  out_shape=[SDS((T, 2048), BF16), SDS((T, 2048), BF16), SDS((T, 1024), BF16), SDS((T, 1024), BF16),
                             SDS((T, 1024), BF16), SDS((1, Q_LORA), F32), SDS((1, KV_LORA), F32)],
                  sem=("arbitrary",))(dqm, dkm, dvm, dqs, dkc, dkp, dvc, dvp, z, z, gq, gkv, wqb, wkn, wv, *tab_m, *tab_s)


def _bwd_in(dsq, dga, dgb, drest, w_in_p, x, g1, dx1, tm):
    T = x.shape[0]

    def body(a_ref, b_ref, c_ref, d_ref, w_ref, x_ref, g_ref, dx1_ref, dx_ref, dg_ref):
        @pl.when(pl.program_id(0) == 0)
        def _():
            dg_ref[...] = jnp.zeros_like(dg_ref)

        dh = (_dot_nt(a_ref[...], w_ref[:, 0:1024]) + _dot_nt(b_ref[...], w_ref[:, 1024:2048])
              + _dot_nt(c_ref[...], w_ref[:, 2048:3072]) + _dot_nt(d_ref[...], w_ref[:, 3072:4096]))
        dx, dg = _rms_bwd(dh, x_ref[...], g_ref[...])
        dg_ref[...] += dg
        dx_ref[...] = dx1_ref[...] + dx

    r = _rows(tm, D)
    return _pcall(body, name="bwd_in", grid=(T // tm,),
                  in_specs=[r, r, r, r, _full((D, NZ)), r, _full((1, D)), r],
                  out_specs=[r, _full((1, D))], out_shape=[SDS((T, D), F32), SDS((1, D), F32)],
                  sem=("arbitrary",))(dsq, dga, dgb, drest, w_in_p, x, g1, dx1)


def _wgrad(a, g, name):
    T, K = a.shape
    N = g.shape[1]
    tk, tn, tt = min(K, 1024), min(N, 1024), min(T, 1024)
    assert K % tk == 0 and N % tn == 0 and T % tt == 0, (a.shape, g.shape)

    def body(a_ref, g_ref, o_ref):
        @pl.when(pl.program_id(2) == 0)
        def _():
            o_ref[...] = jnp.zeros_like(o_ref)

        o_ref[...] += _dot_tn(a_ref[...].astype(BF16), g_ref[...].astype(BF16))

    return _pcall(body, name=name, grid=(K // tk, N // tn, T // tt),
                  in_specs=[pl.BlockSpec((tt, tk), lambda k, n, t: (t, k)), pl.BlockSpec((tt, tn), lambda k, n, t: (t, n))],
                  out_specs=pl.BlockSpec((tk, tn), lambda k, n, t: (k, n)), out_shape=SDS((K, N), F32),
                  sem=("parallel", "parallel", "arbitrary"))(a, g)


def _adamw(w, packed_g, m, v, name):
    _, R, C = w.shape
    _, row0, lane0 = PACK_AT[name]
    tr = min(R, 256 if row0 % 256 == 0 else 128)
    assert row0 % tr == 0 and R % tr == 0

    def body(w_ref, g_ref, m_ref, v_ref, go_ref, d_ref, m2_ref, v2_ref):
        g_ = g_ref[:, lane0:lane0 + C]
        go_ref[0] = g_
        m2 = ADAM_B1 * m_ref[0] + (1.0 - ADAM_B1) * g_
        v2 = ADAM_B2 * v_ref[0] + (1.0 - ADAM_B2) * jnp.square(g_)
        m_hat = m2 / (1.0 - ADAM_B1 ** ADAM_STEP)
        v_hat = v2 / (1.0 - ADAM_B2 ** ADAM_STEP)
        d_ref[0] = -ADAM_LR * (m_hat / (jnp.sqrt(v_hat) + ADAM_EPS) + ADAM_WD * w_ref[0])
        m2_ref[0] = m2
        v2_ref[0] = v2

    r = pl.BlockSpec((1, tr, C), lambda i: (0, i, 0))
    return _pcall(body, name="adamw_" + name, grid=(R // tr,),
                  in_specs=[r, pl.BlockSpec((tr, D), lambda i: (row0 // tr + i, 0)), r, r], out_specs=[r] * 4,
                  out_shape=[SDS((1, R, C), F32)] * 4, sem=("parallel",))(w, packed_g, m, v)


def _adamw_small(w, parts, m, v):
    def body(w_ref, p_ref, m_ref, v_ref, g_ref, d_ref, m2_ref, v2_ref):
        g_ = p_ref[0]
        for k in range(1, N_DEV):
            g_ = g_ + p_ref[k]
        g_ref[...] = g_
        m2 = ADAM_B1 * m_ref[...] + (1.0 - ADAM_B1) * g_
        v2 = ADAM_B2 * v_ref[...] + (1.0 - ADAM_B2) * jnp.square(g_)
        m_hat = m2 / (1.0 - ADAM_B1 ** ADAM_STEP)
        v_hat = v2 / (1.0 - ADAM_B2 ** ADAM_STEP)
        d_ref[...] = -ADAM_LR * (m_hat / (jnp.sqrt(v_hat) + ADAM_EPS) + ADAM_WD * w_ref[...])
        m2_ref[...] = m2
        v2_ref[...] = v2

    s = _full((8, D))
    return _pcall(body, name="adamw_small", grid=(1,), in_specs=[s, _full((N_DEV, 8, D)), s, s], out_specs=[s] * 4,
                  out_shape=[SDS((8, D), F32)] * 4, sem=("arbitrary",))(w, parts, m, v)


ANY = pl.BlockSpec(memory_space=pl.ANY)


def _place():
    x, y, c = lax.axis_index("x"), lax.axis_index("y"), lax.axis_index("c")
    chips = [(1 - x, y), (x, 1 - y), (1 - x, 1 - y)]
    return x, y, c, chips


def _all_gather(wpk):
    rows = wpk.shape[0]
    HALF = rows // 2
    assert HALF % 16 == 0

    def body(in_ref, out_ref, send_sems, recv_sems):
        x, y, c, chips = _place()
        half = pl.ds(pl.multiple_of(c * HALF, 16), HALF)
        other = pl.ds(pl.multiple_of((1 - c) * HALF, 16), HALF)

        def copy(k, src, dst, to):
            return pltpu.make_async_remote_copy(src_ref=src, dst_ref=dst, send_sem=send_sems.at[k], recv_sem=recv_sems.at[k],
                                                device_id=to, device_id_type=MESH)

        first = [copy(k, in_ref.at[half], out_ref.at[2 * x + y, half], (cx, cy, c)) for k, (cx, cy) in enumerate(chips)]
        for cp in first:
            cp.start()
        passed = []
        for k, (cx, cy) in enumerate(chips):
            slot = out_ref.at[2 * cx + cy, half]
            copy(k, slot, slot, (x, y, c)).wait_recv()
            fwd = copy(3 + k, slot, slot, (x, y, 1 - c))
            fwd.start()
            passed.append(fwd)
        for k, (cx, cy) in enumerate(chips):
            slot = out_ref.at[2 * cx + cy, other]
            copy(3 + k, slot, slot, (x, y, c)).wait_recv()
        for cp in first + passed:
            cp.wait_send()

    return _pcall(body, name="all_gather_weights", in_specs=[ANY], out_specs=ANY,
                  out_shape=SDS((N_CHIPS, rows, D), BF16),
                  scratch=[pltpu.SemaphoreType.DMA((6,)), pltpu.SemaphoreType.DMA((6,))])(wpk)


HBM = pl.BlockSpec(memory_space=pltpu.HBM)
SEM = pl.BlockSpec(memory_space=pltpu.SEMAPHORE)
DATAFLOW = pltpu.SideEffectType.DATAFLOW_SIDE_EFFECTING


def _in_hbm(a):
    return pltpu.with_memory_space_constraint(a, pltpu.HBM)


def _gather_late_start(wpk, after):
    rows = wpk.shape[0]

    def body(in_ref, land_ref, after_ref, send_sems, recv_sems, in_thru, land_thru, token):
        x, y, c, chips = _place()
        for k, (cx, cy) in enumerate(chips):
            pltpu.make_async_remote_copy(src_ref=in_ref, dst_ref=land_ref.at[2 * x + y], send_sem=send_sems.at[k],
                                         recv_sem=recv_sems.at[k], device_id=(cx, cy, c), device_id_type=MESH).start()
        token[...] = jnp.zeros_like(token)

    return pl.pallas_call(
        body, name="gather_late_start",
        out_shape=(pltpu.SemaphoreType.DMA((3,)), pltpu.SemaphoreType.DMA((3,)), pltpu.HBM(wpk.shape, wpk.dtype),
                   pltpu.HBM((N_CHIPS, rows, D), wpk.dtype), SDS((8, LANES), F32)),
        in_specs=(HBM, HBM, ANY), out_specs=(SEM, SEM, HBM, HBM, pl.BlockSpec(memory_space=pltpu.VMEM)),
        input_output_aliases={0: 2, 1: 3}, compiler_params=pltpu.CompilerParams(has_side_effects=DATAFLOW),
    )(_in_hbm(wpk), _in_hbm(lax.empty((N_CHIPS, rows, D), wpk.dtype)), after)


def _gather_late_wait(send_sems, recv_sems, in_thru, land_thru, after):
    def body(in_ref, land_ref, send_sems, recv_sems, after_ref, in_dead, got_ref):
        x, y, c, chips = _place()
        for k, (cx, cy) in enumerate(chips):
            cp = pltpu.make_async_remote_copy(src_ref=in_ref, dst_ref=land_ref.at[2 * cx + cy], send_sem=send_sems.at[k],
                                              recv_sem=recv_sems.at[k], device_id=(cx, cy, c), device_id_type=MESH)
            cp.wait_send()
            cp.wait_recv()

    return pl.pallas_call(
        body, name="gather_late_wait",
        out_shape=(pltpu.HBM(in_thru.shape, in_thru.dtype), pltpu.HBM(land_thru.shape, land_thru.dtype)),
        in_specs=(HBM, HBM, SEM, SEM, ANY), out_specs=(HBM, HBM), input_output_aliases={0: 0, 1: 1},
        compiler_params=pltpu.CompilerParams(has_side_effects=DATAFLOW),
    )(in_thru, land_thru, send_sems, recv_sems, after)[1]


def _rs_sibling(gpk):
    HALF = gpk.shape[1] // 2

    def body(in_ref, out_ref, send_sem, recv_sem):
        x, y, c, _ = _place()
        theirs = pl.ds(pl.multiple_of((1 - c) * HALF, 8), HALF)
        cp = pltpu.make_async_remote_copy(src_ref=in_ref.at[:, theirs], dst_ref=out_ref, send_sem=send_sem, recv_sem=recv_sem,
                                          device_id=(x, y, 1 - c), device_id_type=MESH)
        cp.start()
        cp.wait()

    return _pcall(body, name="rs_sibling", in_specs=[ANY], out_specs=ANY, out_shape=SDS((N_CHIPS, HALF, D), F32),
                  scratch=[pltpu.SemaphoreType.DMA, pltpu.SemaphoreType.DMA])(gpk)


def _rs_add_sibling(cidx, gpk, got):
    HALF = got.shape[1]
    th = HALF // 4
    nh = HALF // th
    assert th % 16 == 0

    def body(c_ref, a_ref, b_ref, o_ref):
        o_ref[...] = (a_ref[...] + b_ref[...]).astype(BF16)

    gs = pltpu.PrefetchScalarGridSpec(
        num_scalar_prefetch=1, grid=(N_CHIPS, nh),
        in_specs=[pl.BlockSpec((1, th, D), lambda j, i, c: (j, c[0] * nh + i, 0)), pl.BlockSpec((1, th, D), lambda j, i, c: (j, i, 0))],
        out_specs=pl.BlockSpec((1, th, D), lambda j, i, c: (j, i, 0)))
    return pl.pallas_call(body, name="rs_add_sibling", grid_spec=gs, out_shape=SDS((N_CHIPS, HALF, D), BF16),
                          compiler_params=pltpu.CompilerParams(dimension_semantics=("parallel", "parallel"),
                                                               vmem_limit_bytes=48 << 20))(cidx, gpk, got)


def _rs_chips(part, small):
    def body(p_ref, s_ref, o_ref, so_ref, send_sems, recv_sems, ssend_sems, srecv_sems, local_sem):
        x, y, c, chips = _place()
        me = 2 * x + y
        mine_s = pltpu.make_async_copy(s_ref, so_ref.at[4 * x + 2 * y + c], local_sem)
        mine_s.start()
        sends = []
        for k, (cx, cy) in enumerate(chips):
            sends.append(pltpu.make_async_remote_copy(src_ref=p_ref.at[2 * cx + cy], dst_ref=o_ref.at[me], send_sem=send_sems.at[k],
                                                      recv_sem=recv_sems.at[k], device_id=(cx, cy, c), device_id_type=MESH))
        peers = [(x, y, 1 - c)] + [(cx, cy, c) for cx, cy in chips] + [(cx, cy, 1 - c) for cx, cy in chips]
        for k, to in enumerate(peers):
            sends.append(pltpu.make_async_remote_copy(src_ref=s_ref, dst_ref=so_ref.at[4 * x + 2 * y + c], send_sem=ssend_sems.at[k],
                                                      recv_sem=srecv_sems.at[k], device_id=to, device_id_type=MESH))
        for cp in sends:
            cp.start()
        for k, (cx, cy) in enumerate(chips):
            slot = o_ref.at[2 * cx + cy]
            pltpu.make_async_remote_copy(src_ref=slot, dst_ref=slot, send_sem=send_sems.at[k], recv_sem=recv_sems.at[k],
                                         device_id=(x, y, c), device_id_type=MESH).wait_recv()
        for k, (px, py, pc) in enumerate(peers):
            slot = so_ref.at[4 * px + 2 * py + pc]
            pltpu.make_async_remote_copy(src_ref=slot, dst_ref=slot, send_sem=ssend_sems.at[k], recv_sem=srecv_sems.at[k],
                                         device_id=(x, y, c), device_id_type=MESH).wait_recv()
        for cp in sends:
            cp.wait_send()
        mine_s.wait()

    return _pcall(body, name="rs_chips", in_specs=[ANY, ANY], out_specs=[ANY, ANY],
                  out_shape=[SDS(part.shape, part.dtype), SDS((N_DEV, 8, D), F32)],
                  scratch=[pltpu.SemaphoreType.DMA((3,)), pltpu.SemaphoreType.DMA((3,)), pltpu.SemaphoreType.DMA((7,)),
                           pltpu.SemaphoreType.DMA((7,)), pltpu.SemaphoreType.DMA])(part, small)


def _rs_add_chips(qidx, part, parts):
    HALF = part.shape[1]
    th = HALF // 4
    assert th % 16 == 0

    def body(q_ref, own_ref, p_ref, o_ref):
        for me in range(N_CHIPS):
            @pl.when(q_ref[0] == me)
            def _(me=me):
                t = [(own_ref[0] if j == me else p_ref[j]).astype(F32) for j in range(N_CHIPS)]
                o_ref[...] = ((t[0] + t[1]) + t[2]) + t[3]

    gs = pltpu.PrefetchScalarGridSpec(
        num_scalar_prefetch=1, grid=(HALF // th,),
        in_specs=[pl.BlockSpec((1, th, D), lambda i, q: (q[0], i, 0)), pl.BlockSpec((N_CHIPS, th, D), lambda i, q: (0, i, 0))],
        out_specs=pl.BlockSpec((th, D), lambda i, q: (i, 0)))
    return pl.pallas_call(body, name="rs_add_chips", grid_spec=gs, out_shape=SDS((HALF, D), F32),
                          compiler_params=pltpu.CompilerParams(dimension_semantics=("parallel",),
                                                               vmem_limit_bytes=48 << 20))(qidx, part, parts)


def _rs_join(early, late):
    def body(e_ref, l_ref, eo_ref, lo_ref, send_sems, recv_sems):
        x, y, c, _ = _place()
        cps = [pltpu.make_async_remote_copy(src_ref=src, dst_ref=dst, send_sem=send_sems.at[k], recv_sem=recv_sems.at[k],
                                            device_id=(x, y, 1 - c), device_id_type=MESH)
               for k, (src, dst) in enumerate([(e_ref, eo_ref), (l_ref, lo_ref)])]
        for cp in cps:
            cp.start()
        for cp in cps:
            cp.wait()

    return _pcall(body, name="rs_join", in_specs=[ANY, ANY], out_specs=[ANY, ANY],
                  out_shape=[SDS(early.shape, F32), SDS(late.shape, F32)],
                  scratch=[pltpu.SemaphoreType.DMA((2,)), pltpu.SemaphoreType.DMA((2,))])(early, late)


def _reduce_late_start(gpk, after):
    rows = gpk.shape[1]
    HALF = rows // 2
    assert HALF % 16 == 0

    def body(in_ref, land_ref, after_ref, send_sems, recv_sems, in_thru, land_thru, token):
        x, y, c, chips = _place()
        me = 4 * x + 2 * y + c
        peers = [(x, y, 1 - c)] + [(cx, cy, c) for cx, cy in chips] + [(cx, cy, 1 - c) for cx, cy in chips]
        for k, (px, py, pc) in enumerate(peers):
            src = in_ref.at[2 * px + py, pl.ds(pl.multiple_of(pc * HALF, 16), HALF)]
            pltpu.make_async_remote_copy(src_ref=src, dst_ref=land_ref.at[me], send_sem=send_sems.at[k], recv_sem=recv_sems.at[k],
                                         device_id=(px, py, pc), device_id_type=MESH).start()
        token[...] = jnp.zeros_like(token)

    return pl.pallas_call(
        body, name="reduce_late_start",
        out_shape=(pltpu.SemaphoreType.DMA((7,)), pltpu.SemaphoreType.DMA((7,)), pltpu.HBM(gpk.shape, gpk.dtype),
                   pltpu.HBM((N_DEV, HALF, D), gpk.dtype), SDS((8, LANES), F32)),
        in_specs=(HBM, HBM, ANY), out_specs=(SEM, SEM, HBM, HBM, pl.BlockSpec(memory_space=pltpu.VMEM)),
        input_output_aliases={0: 2, 1: 3}, compiler_params=pltpu.CompilerParams(has_side_effects=DATAFLOW),
    )(_in_hbm(gpk), _in_hbm(lax.empty((N_DEV, HALF, D), gpk.dtype)), after)


def _reduce_late_wait(send_sems, recv_sems, in_thru, land_thru, after):
    def body(in_ref, land_ref, send_sems, recv_sems, after_ref, in_out, got_ref):
        x, y, c, chips = _place()
        peers = [(x, y, 1 - c)] + [(cx, cy, c) for cx, cy in chips] + [(cx, cy, 1 - c) for cx, cy in chips]
        for k, (px, py, pc) in enumerate(peers):
            cp = pltpu.make_async_remote_copy(src_ref=land_ref.at[0], dst_ref=land_ref.at[4 * px + 2 * py + pc],
                                              send_sem=send_sems.at[k], recv_sem=recv_sems.at[k],
                                              device_id=(px, py, pc), device_id_type=MESH)
            cp.wait_send()
            cp.wait_recv()

    return pl.pallas_call(
        body, name="reduce_late_wait",
        out_shape=(pltpu.HBM(in_thru.shape, in_thru.dtype), pltpu.HBM(land_thru.shape, land_thru.dtype)),
        in_specs=(HBM, HBM, SEM, SEM, ANY), out_specs=(HBM, HBM), input_output_aliases={0: 0, 1: 1},
        compiler_params=pltpu.CompilerParams(has_side_effects=DATAFLOW),
    )(in_thru, land_thru, send_sems, recv_sems, after)


def _reduce_late_add(didx, gpk, parts):
    HALF = parts.shape[1]
    th = HALF // 4
    nh = HALF // th
    assert th % 16 == 0

    def body(d_ref, own_ref, p_ref, o_ref):
        for me in range(N_DEV):
            @pl.when(d_ref[0] == me)
            def _(me=me):
                t = [(own_ref[0] if j == me else p_ref[j]).astype(F32) for j in range(N_DEV)]
                o_ref[...] = ((((((t[0] + t[1]) + t[2]) + t[3]) + t[4]) + t[5]) + t[6]) + t[7]

    gs = pltpu.PrefetchScalarGridSpec(
        num_scalar_prefetch=1, grid=(nh,),
        in_specs=[pl.BlockSpec((1, th, D), lambda i, d: (d[1], d[2] * nh + i, 0)), pl.BlockSpec((N_DEV, th, D), lambda i, d: (0, i, 0))],
        out_specs=pl.BlockSpec((th, D), lambda i, d: (i, 0)))
    return pl.pallas_call(body, name="reduce_late_add", grid_spec=gs, out_shape=SDS((HALF, D), F32),
                          compiler_params=pltpu.CompilerParams(dimension_semantics=("parallel",),
                                                               vmem_limit_bytes=48 << 20))(didx, gpk, parts)


def _pack_early(b, dtype):
    lanes = lambda a: jnp.pad(a.astype(dtype), ((0, 0), (0, D - a.shape[1])))
    pair = jnp.concatenate([b["w_q_b"].astype(dtype), b["w_ple"].astype(dtype), jnp.zeros((256, D - 640), dtype)], axis=1)
    return jnp.concatenate([lanes(b["w_in"]), pair, lanes(b["w_kv_b"])], axis=0)


def _pack_late(b, dtype):
    return jnp.concatenate([b[n].astype(dtype) for n in ("w_mla_up", "w_swa_up", "w_out", "w_ple_gate", "w_mlp_up", "w_mlp_down")],
                           axis=0)


def _unpack_shards(pk, which):
    return {n: pk[PACK_AT[n][1]:PACK_AT[n][1] + r, PACK_AT[n][2]:PACK_AT[n][2] + c] for n, r, c in BIG if PACK_AT[n][0] == which}


def _full_weights(gathered, own, chip, which):
    own_b = _unpack_shards(own, which)
    per_chip = [{n: jnp.where(chip == j, own_b[n], blk) for n, blk in _unpack_shards(gathered[j], which).items()}
                for j in range(N_CHIPS)]
    out = {}
    for n in own_b:
        shards = [pc[n] for pc in per_chip]
        if n == "w_in":
            out["w_in_p"] = _w_in_internal(shards)
        else:
            out[n] = jnp.concatenate(shards, axis=1 if n in COL_SHARDED else 0)
    return out


def _split_full_grads(grads, pack, dtype):
    shard = {n: (r, c) for n, r, c in BIG}
    chunks = []
    for j in range(N_CHIPS):
        blocks = {}
        for n, g in grads.items():
            if n == "w_in_p":
                blocks["w_in"] = _w_in_grad_shard(g, j)
                continue
            r, c = shard[n]
            blocks[n] = g[:, j * c:(j + 1) * c] if n in COL_SHARDED else g[j * r:(j + 1) * r]
        chunks.append(pack(blocks, dtype))
    return jnp.stack(chunks)


W_IN_SHARD = 936
W_IN_SEGMENTS = ((0, 256, (3072,)), (256, 384, (3840,)), (384, 416, (4032,)), (416, 1440, (0,)), (1440, 1504, (3328, 3392)),
                 (1504, 1568, (3456, 3520)), (1568, 1632, (3584, 3648)), (1632, 1696, (3712, 3776)), (1696, 3744, (1024,)))


def _w_in_internal(shards):
    def cols(a, b):
        out = []
        for j, s in enumerate(shards):
            lo, hi = max(a, W_IN_SHARD * j), min(b, W_IN_SHARD * (j + 1))
            if lo < hi:
                out.append(s[:, lo - W_IN_SHARD * j:hi - W_IN_SHARD * j])
        return out

    pieces = {}
    for a, b, places in W_IN_SEGMENTS:
        for at in places:
            pieces[at] = cols(a, b)
    zeros = lambda n: [jnp.zeros((D, n), shards[0].dtype)]
    pieces[3968] = zeros(64)
    pieces[4064] = zeros(32)
    return jnp.concatenate([piece for at in sorted(pieces) for piece in pieces[at]], axis=1)


def _w_in_grad_shard(g, j):
    out = []
    for a, b, places in W_IN_SEGMENTS:
        lo, hi = max(a, W_IN_SHARD * j), min(b, W_IN_SHARD * (j + 1))
        if lo < hi:
            parts = [g[:, at + lo - a:at + hi - a] for at in places]
            out.append(parts[0] if len(parts) == 1 else parts[0] + parts[1])
    return jnp.concatenate(out, axis=1)


def _local_step(x, p, tgt, w, small, late_weights, late_grads_out):
    T = x.shape[0]
    tm = 256
    tb = 256
    w_in_p = w["w_in_p"]
    wqb = jnp.pad(w["w_q_b"].reshape(Q_LORA, MLA_HEADS, 96), ((0, 0), (0, 0), (0, 32))).reshape(Q_LORA, 2048)
    wkv = w["w_kv_b"].reshape(KV_LORA, MLA_HEADS, 128)
    wkn = jnp.pad(wkv[:, :, :64], ((0, 0), (0, 0), (0, 64))).reshape(KV_LORA, 2048)
    wv = wkv[:, :, 64:].reshape(KV_LORA, 1024)
    tab_m = _rope_tables(T, "mla")
    tab_s = _rope_tables(T, "swa")
    g1, gq, gkv, sinks = small["g_mix_pre"], small["g_q_a"], small["g_kv_a"], small["sinks"]
    g2, g3, g4, g5 = small["g_mix_post"], small["g_mlp_pre"], small["g_mlp_post"], small["g_ple"]
    sink_vec = sinks.reshape(SWA_HEADS)

    z, h1 = _fwd_in(x, g1, w_in_p, tm)
    qn, kvn, qm, km, vm, qt, kt, vt, qs, ks, vs = _fwd_qkv(z, gq, gkv, wqb, wkn, wv, tab_m, tab_s, tb)
    om, lse_m = _mla_fwd(qm, km, vt, tb)
    os_, lse_s = _swa_fwd(sink_vec, qs, ks, vs)
    w = {**w, **late_weights(os_)}
    y, yo, au, bu, x1 = _fwd_mix(om, os_, z, x, w["w_mla_up"], w["w_swa_up"], w["w_out"], g2, tm)
    h2, a, u = _fwd_mlp_up(x1, g3, w["w_mlp_up"], tm)
    d, x2 = _fwd_mlp_down(u, w["w_mlp_down"], x1, g4, tm)
    loss, dx2, dgt, de0, dg5 = _ple_fwd_bwd(p, x2, tgt, w["w_ple"], g5, w["w_ple_gate"], tm)

    dd, da, dg4 = _bwd_mlp_down(dx2, d, g4, w["w_mlp_down"], a, tm)
    dx1, dg3 = _bwd_mlp_up(da, w["w_mlp_up"], x1, g3, dx2, tm)
    dyo, dg2, dau, dbu, dga, dgb, dos, delta_m, dom_t = _bwd_mix(dx1, yo, g2, w["w_out"], z, au, bu, w["w_mla_up"],
                                                                w["w_swa_up"], om, tb)
    token = late_grads_out({
        "w_mla_up": _wgrad(om, dau, "wgrad_mla_up"),
        "w_swa_up": _wgrad(os_, dbu, "wgrad_swa_up"),
        "w_out": _wgrad(y, dyo, "wgrad_out"),
        "w_ple_gate": _wgrad(x2, dgt, "wgrad_ple_gate"),
        "w_mlp_up": _wgrad(h2, da, "wgrad_mlp_up"),
        "w_mlp_down": _wgrad(u, dd, "wgrad_mlp_down"),
    })
    delta_m = delta_m + token[0, 0]
    dqm, dkm, dvm = _mla_bwd(qm, qt, km, kt, vm, dom_t, lse_m, delta_m, tb)
    dqs, dkc, dkp, dvc, dvp, dsink = _swa_bwd(sink_vec, qs, ks, vs, dos, os_, lse_s)
    dqb, dknb, dvb, dsq, drest, dgq, dgkv = _bwd_qkv(dqm, dkm, dvm, dqs, dkc, dkp, dvc, dvp, z, gq, gkv, wqb, wkn, wv,
                                                      tab_m, tab_s)
    gx, dg1 = _bwd_in(dsq, dga, dgb, drest, w_in_p, x, g1, dx1, tm)

    g_in_p = jnp.concatenate([_wgrad(h1, dsq, "wgrad_in_sq"), _wgrad(h1, dga, "wgrad_in_ga"), _wgrad(h1, dgb, "wgrad_in_gb"),
                              _wgrad(h1, drest, "wgrad_in_rest")], axis=1)
    g_qb_p = _wgrad(qn, dqb, "wgrad_q_b")
    g_kn_p = _wgrad(kvn, dknb, "wgrad_kv_b_nope")
    g_v_p = _wgrad(kvn, dvb, "wgrad_kv_b_v")
    grads = {
        "w_in_p": g_in_p,
        "w_q_b": g_qb_p.reshape(Q_LORA, MLA_HEADS, 128)[:, :, :96].reshape(Q_LORA, 1536),
        "w_kv_b": jnp.concatenate([g_kn_p.reshape(KV_LORA, MLA_HEADS, 128)[:, :, :64], g_v_p.reshape(KV_LORA, MLA_HEADS, 64)],
                                  axis=2).reshape(KV_LORA, 2048),
        "w_ple": _wgrad(p, de0, "wgrad_ple"),
    }
    small_grads = {"g_mix_pre": dg1, "g_q_a": dgq, "g_kv_a": dgkv, "sinks": dsink[0:1, 0:SWA_HEADS], "g_mix_post": dg2,
                   "g_mlp_pre": dg3, "g_mlp_post": dg4, "g_ple": dg5}
    return loss, gx, grads, small_grads


def _pack_small(vals, fill):
    wide = [vals[n] for n, k in SMALL if k == D]
    narrow = [vals[n] for n, k in SMALL if k != D]
    used = sum(k for _, k in SMALL if k != D)
    last = jnp.concatenate(narrow + [jnp.full((1, D - used), fill, F32)], axis=1)
    return jnp.concatenate(wide + [last, jnp.full((2, D), fill, F32)], axis=0)


def _unpack_small(pk):
    out, row, off = {}, 0, 0
    for n, k in SMALL:
        if k == D:
            out[n] = pk[row:row + 1]
            row += 1
    for n, k in SMALL:
        if k != D:
            out[n] = pk[5:6, off:off + k]
            off += k
    return out


def kernel(x, p, g_mix_pre, w_in, g_q_a, w_q_b, g_kv_a, w_kv_b, sinks, w_mla_up, w_swa_up, w_out, g_mix_post, g_mlp_pre, w_mlp_up, w_mlp_down, g_mlp_post, w_ple, g_ple, w_ple_gate, loss_target, m_g_mix_pre, m_w_in, m_g_q_a, m_w_q_b, m_g_kv_a, m_w_kv_b, m_sinks, m_w_mla_up, m_w_swa_up, m_w_out, m_g_mix_post, m_g_mlp_pre, m_w_mlp_up, m_w_mlp_down, m_g_mlp_post, m_w_ple, m_g_ple, m_w_ple_gate, v_g_mix_pre, v_w_in, v_g_q_a, v_w_q_b, v_g_kv_a, v_w_kv_b, v_sinks, v_w_mla_up, v_w_swa_up, v_w_out, v_g_mix_post, v_g_mlp_pre, v_w_mlp_up, v_w_mlp_down, v_g_mlp_post, v_w_ple, v_g_ple, v_w_ple_gate):
    given = dict(locals())
    big_w = {n: given[n][0] for n, _, _ in BIG}
    small_w = {n: given[n] for n, _ in SMALL}
    small_m = {n: given["m_" + n] for n, _ in SMALL}
    small_v = {n: given["v_" + n] for n, _ in SMALL}

    core = lax.axis_index("c")
    chip = 2 * lax.axis_index("x") + lax.axis_index("y")
    core_i = core.astype(jnp.int32).reshape(1)
    chip_i = chip.astype(jnp.int32).reshape(1)
    dev_i = jnp.stack([2 * chip + core, chip, core]).astype(jnp.int32)

    own_early = _pack_early(big_w, BF16)
    own_late = _pack_late(big_w, BF16)
    got_early = _all_gather(own_early)
    late_flight = _gather_late_start(own_late, got_early)
    weights = _full_weights(got_early, own_early, chip, "early")
    step_small = {**small_w, "g_mix_pre": small_w["g_mix_pre"] + late_flight[4][0, 0]}

    def late_weights(after):
        return _full_weights(_gather_late_wait(*late_flight[:4], after), own_late, chip, "late")

    flight = {}

    def late_grads_out(grads):
        gpk_late = _split_full_grads(grads, _pack_late, BF16)
        flight["late"] = _reduce_late_start(gpk_late, dev_i)
        return flight["late"][4]

    loss_blk, gx, grads, small_grads = _local_step(x[0], p[0, 0], loss_target[0], weights, step_small, late_weights,
                                                   late_grads_out)

    gpk = _split_full_grads(grads, _pack_early, F32)
    got = _rs_sibling(gpk)
    part = _rs_add_sibling(core_i, gpk, got)
    parts, small_parts = _rs_chips(part, _pack_small(small_grads, 0.0))
    mine_early = _rs_add_chips(chip_i, part, parts)
    gpk_late, parts_late = _reduce_late_wait(*flight["late"][:4], mine_early)
    mine_late = _reduce_late_add(dev_i, gpk_late, parts_late)
    theirs_early, theirs_late = _rs_join(mine_early, mine_late)
    joined = {"early": jnp.where(core == 0, jnp.concatenate([mine_early, theirs_early]), jnp.concatenate([theirs_early, mine_early])),
              "late": jnp.where(core == 0, jnp.concatenate([mine_late, theirs_late]), jnp.concatenate([theirs_late, mine_late]))}

    loss = lax.psum(loss_blk[0, 0], ("x", "y", "c"))
    g_small_pk, d_small_pk, m_small_pk, v_small_pk = _adamw_small(
        _pack_small(small_w, 0.0), small_parts, _pack_small(small_m, 0.0), _pack_small(small_v, 1.0))
    g_small, d_small = _unpack_small(g_small_pk), _unpack_small(d_small_pk)
    m_small, v_small = _unpack_small(m_small_pk), _unpack_small(v_small_pk)

    out_g, out_d, out_m, out_v = dict(g_small), dict(d_small), dict(m_small), dict(v_small)
    for n, _, _ in BIG:
        out_g[n], out_d[n], out_m[n], out_v[n] = _adamw(given[n], joined[PACK_AT[n][0]], given["m_" + n], given["v_" + n], n)
    order = ["g_mix_pre", "w_in", "g_q_a", "w_q_b", "g_kv_a", "w_kv_b", "sinks", "w_mla_up", "w_swa_up", "w_out", "g_mix_post",
             "g_mlp_pre", "w_mlp_up", "w_mlp_down", "g_mlp_post", "w_ple", "g_ple", "w_ple_gate"]
    return (loss, gx[None], *[out_g[n] for n in order], *[out_d[n] for n in order], *[out_m[n] for n in order],
            *[out_v[n] for n in order])
```

```python
import math

import jax
import jax.numpy as jnp
from jax import lax
from jax.experimental import pallas as pl
from jax.experimental.pallas import tpu as pltpu

F32 = jnp.float32
BF16 = jnp.bfloat16
SDS = jax.ShapeDtypeStruct

D = 1024
D_FF = 4096
PLE = 256
Q_LORA = 256
KV_LORA = 128
MLA_HEADS = 16
MLA_NOPE = 64
MLA_ROPE = 32
SWA_HEADS = 16
SWA_HD = 64
WINDOW = 128
ROPE_THETA = 10000.0
EPS = 1e-6
NEG = -1e30
NZ = 4096
MLA_SCALE = (MLA_NOPE + MLA_ROPE) ** -0.5
LOG2_E = math.log2(math.e)
MLA_LOG2_SCALE = MLA_SCALE * LOG2_E
SWA_SCALE = SWA_HD ** -0.5

ADAM_LR = 0.001
ADAM_B1 = 0.9
ADAM_B2 = 0.999
ADAM_EPS = 1e-08
ADAM_WD = 0.01
ADAM_STEP = 10

LANES = 128
ATT_COLS = 128
N_CHIPS = 4
N_DEV = 8
MESH = pl.DeviceIdType.MESH

NT = (((1,), (1,)), ((), ()))
TN = (((0,), (0,)), ((), ()))

BIG = (("w_in", 1024, 936), ("w_q_b", 256, 384), ("w_kv_b", 128, 512), ("w_mla_up", 256, 1024),
       ("w_swa_up", 256, 1024), ("w_out", 256, 1024), ("w_mlp_up", 1024, 1024), ("w_mlp_down", 1024, 1024),
       ("w_ple", 256, 256), ("w_ple_gate", 256, 1024))
COL_SHARDED = ("w_in", "w_q_b", "w_kv_b", "w_mlp_up", "w_ple")
PACK_AT = {"w_in": ("early", 0, 0), "w_q_b": ("early", 1024, 0), "w_ple": ("early", 1024, 384), "w_kv_b": ("early", 1280, 0),
           "w_mla_up": ("late", 0, 0), "w_swa_up": ("late", 256, 0), "w_out": ("late", 512, 0), "w_ple_gate": ("late", 768, 0),
           "w_mlp_up": ("late", 1024, 0), "w_mlp_down": ("late", 2048, 0)}
PACK_ROWS = {"early": 1408, "late": 3072}
SMALL = (("g_mix_pre", 1024), ("g_q_a", 256), ("g_kv_a", 128), ("sinks", 16), ("g_mix_post", 1024),
         ("g_mlp_pre", 1024), ("g_mlp_post", 1024), ("g_ple", 1024))


def _dot(a, b):
    return jnp.dot(a, b, preferred_element_type=F32)


def _dot_nt(a, b):
    return lax.dot_general(a, b, NT, preferred_element_type=F32)


def _dot_tn(a, b):
    return lax.dot_general(a, b, TN, preferred_element_type=F32)


def _pcall(body, *, name, out_shape, grid=(), in_specs=None, out_specs=None, scratch=(), sem=None, vmem_mb=48):
    params = dict(vmem_limit_bytes=vmem_mb << 20)
    if sem is not None:
        params["dimension_semantics"] = sem
    return pl.pallas_call(body, name=name, grid=grid, in_specs=in_specs, out_specs=out_specs, out_shape=out_shape,
                          scratch_shapes=list(scratch), compiler_params=pltpu.CompilerParams(**params))


def _rows(tm, n, col=0):
    return pl.BlockSpec((tm, n), lambda i: (i, col))


def _full(shape):
    return pl.BlockSpec(shape, lambda i: (0,) * len(shape))


def _rms(x, g):
    r = lax.rsqrt(jnp.mean(x * x, axis=-1, keepdims=True) + EPS)
    return x * r * g


def _rms_bwd(dy, x, g):
    r = lax.rsqrt(jnp.mean(x * x, axis=-1, keepdims=True) + EPS)
    xn = x * r
    dn = dy * g
    dx = r * (dn - xn * jnp.mean(dn * xn, axis=-1, keepdims=True))
    return dx, jnp.sum(dy * xn, axis=0, keepdims=True)


def _sigmoid(x):
    return 1.0 / (1.0 + jnp.exp(-x))


def _rope(x, c, a, b, half):
    return x * c + pltpu.roll(x, LANES - half, 1) * a + pltpu.roll(x, half, 1) * b


def _rope_tables(T, kind):
    lane = jnp.arange(LANES)
    if kind == "mla":
        half = MLA_ROPE // 2
        rel = lane - MLA_NOPE
        on = (rel >= 0) & (rel < MLA_ROPE)
        d = MLA_ROPE
    else:
        half = SWA_HD // 2
        rel = lane % SWA_HD
        on = jnp.ones((LANES,), bool)
        d = SWA_HD
    first = on & (rel < half)
    second = on & (rel >= half)
    f = jnp.where(first, rel, rel - half).astype(F32)
    inv = jnp.exp(-math.log(ROPE_THETA) * f * (2.0 / d))
    ang = jnp.arange(T, dtype=F32)[:, None] * inv[None, :]
    cos, sin = jnp.cos(ang), jnp.sin(ang)
    c = jnp.where(on[None], cos, 1.0)
    a = jnp.where(first[None], -sin, 0.0)
    b = jnp.where(second[None], sin, 0.0)
    return c, a, b


def _fwd_in(x, g1, w_in_p, tm):
    T = x.shape[0]

    def body(x_ref, g_ref, w_ref, z_ref, h_ref):
        h = _rms(x_ref[...], g_ref[...]).astype(BF16)
        h_ref[...] = h
        z_ref[...] = _dot(h, w_ref[...])

    return _pcall(body, name="fwd_in", grid=(T // tm,),
                  in_specs=[_rows(tm, D), _full((1, D)), _full((D, NZ))],
                  out_specs=[_rows(tm, NZ), _rows(tm, D)],
                  out_shape=[SDS((T, NZ), F32), SDS((T, D), BF16)], sem=("parallel",))(x, g1, w_in_p)


def _fwd_qkv(z, gq, gkv, wqb, wkn, wv, tab_m, tab_s, tm):
    T = z.shape[0]

    def body(qa_ref, sq_ref, skd_ref, svd_ref, kva_ref, kr_ref, gq_ref, gkv_ref, wqb_ref, wkn_ref, wv_ref,
             cm_ref, am_ref, bm_ref, cs_ref, as_ref, bs_ref,
             qn_ref, kvn_ref, qm_ref, km_ref, vm_ref, qt_ref, kt_ref, vt_ref, qs_ref, ks_ref, vs_ref):
        qn = _rms(qa_ref[...], gq_ref[...]).astype(BF16)
        qn_ref[...] = qn
        kvn = _rms(kva_ref[...], gkv_ref[...]).astype(BF16)
        kvn_ref[...] = kvn
        cm, am, bm = cm_ref[...], am_ref[...], bm_ref[...]
        cs, as_, bs = cs_ref[...], as_ref[...], bs_ref[...]
        k_rope = _rope(kr_ref[...], cm, am, bm, MLA_ROPE // 2)
        vt_row = lax.broadcasted_iota(jnp.int32, (LANES, tm), 0)
        v_all = _dot(kvn, wv_ref[...])
        q_all = _dot(qn, wqb_ref[...])
        k_all = _dot(kvn, wkn_ref[...])
        for j in range(D // LANES):
            sl = slice(LANES * j, LANES * (j + 1))
            v = v_all[:, sl]
            vm_ref[:, sl] = v.astype(BF16)
            v_t = v.T
            for hh, rows64 in enumerate((v_t, pltpu.roll(v_t, 64, 0))):
                blk = jnp.where(vt_row < 64, rows64, jnp.where(vt_row == 64, 1.0, 0.0))
                vt_ref[0, LANES * (2 * j + hh):LANES * (2 * j + hh + 1), :] = blk.astype(BF16)
        for h in range(MLA_HEADS):
            sl = slice(LANES * h, LANES * (h + 1))
            qh = _rope(q_all[:, sl], cm, am, bm, MLA_ROPE // 2)
            qm_ref[:, sl] = qh.astype(BF16)
            qt_ref[0, sl, :] = qh.T.astype(BF16)
            k = k_all[:, sl] + k_rope
            km_ref[:, sl] = k.astype(BF16)
            kt_ref[0, sl, :] = k.T.astype(BF16)
        for j in range(D // LANES):
            sl = slice(LANES * j, LANES * (j + 1))
            qs_ref[:, sl] = _rope(sq_ref[:, sl], cs, as_, bs, SWA_HD // 2).astype(BF16)
        for j in range(2):
            sl = slice(LANES * j, LANES * (j + 1))
            ks_ref[:, sl] = _rope(skd_ref[:, sl], cs, as_, bs, SWA_HD // 2).astype(BF16)
        vs_ref[...] = svd_ref[...].astype(BF16)

    tab = [_rows(tm, LANES)] * 6
    return _pcall(body, name="fwd_qkv", grid=(T // tm,),
                  in_specs=[_rows(tm, 256, 12), _rows(tm, 1024, 0), _rows(tm, 256, 13), _rows(tm, 256, 14),
                            _rows(tm, 128, 30), _rows(tm, 128, 31), _full((1, Q_LORA)), _full((1, KV_LORA)),
                            _full((Q_LORA, 2048)), _full((KV_LORA, 2048)), _full((KV_LORA, 1024))] + tab,
                  out_specs=[_rows(tm, Q_LORA), _rows(tm, KV_LORA), _rows(tm, 2048), _rows(tm, 2048), _rows(tm, 1024),
                             pl.BlockSpec((1, 2048, tm), lambda i: (i, 0, 0)), pl.BlockSpec((1, 2048, tm), lambda i: (i, 0, 0)),
                             pl.BlockSpec((1, 2048, tm), lambda i: (i, 0, 0)),
                             _rows(tm, 1024), _rows(tm, 256), _rows(tm, 256)],
                  out_shape=[SDS((T, Q_LORA), BF16), SDS((T, KV_LORA), BF16), SDS((T, 2048), BF16), SDS((T, 2048), BF16),
                             SDS((T, 1024), BF16), SDS((T // tm, 2048, tm), BF16), SDS((T // tm, 2048, tm), BF16),
                             SDS((T // tm, 2048, tm), BF16),
                             SDS((T, 1024), BF16), SDS((T, 256), BF16), SDS((T, 256), BF16)],
                  sem=("parallel",))(z, z, z, z, z, z, gq, gkv, wqb, wkn, wv, *tab_m, *tab_s)


def _mla_fwd(qm, km, vt, tb):
    T = qm.shape[0]
    nb = T // tb
    cc = ATT_COLS

    def body(q_ref, k_ref, vt_ref, o_ref, l_ref, s_ref, p_ref, al_ref, m_ref, acc_ref):
        i = pl.program_id(1)
        m_ref[...] = jnp.full(m_ref.shape, NEG, F32)
        acc_ref[...] = jnp.zeros_like(acc_ref)
        p_ref[1] = jnp.zeros(p_ref.shape[1:], BF16)
        al_ref[1] = jnp.ones(al_ref.shape[1:], F32)
        key = lax.broadcasted_iota(jnp.int32, (tb, cc), 0)
        qry = lax.broadcasted_iota(jnp.int32, (tb, cc), 1)

        def scores(j, slot):
            off = pl.multiple_of(j * tb, tb)
            for hh in range(2):
                sl = slice(LANES * hh, LANES * (hh + 1))
                s_ref[slot, hh] = _dot_nt(k_ref[pl.ds(off, tb), sl], q_ref[:, sl])

        def softmax(slot, diagonal):
            chains = [(hh, slice(cc * c, cc * (c + 1)), c) for hh in range(2) for c in range(tb // cc)]

            def scaled(hh, cols, c):
                t = s_ref[slot, hh, :, cols] * MLA_LOG2_SCALE
                return jnp.where(key <= qry + cc * c, t, NEG) if diagonal else t

            tops = []
            for hh, cols, c in chains:
                if diagonal:
                    top = jnp.max(scaled(hh, cols, c), axis=0, keepdims=True)
                else:
                    top = jnp.max(s_ref[slot, hh, :, cols], axis=0, keepdims=True) * MLA_LOG2_SCALE
                m_old = m_ref[hh, :, cols]
                mn = jnp.maximum(m_old, top)
                m_ref[hh, :, cols] = mn
                al_ref[slot, hh, :, cols] = jnp.exp2(m_old - mn)
                tops.append(mn)
            for (hh, cols, c), mn in zip(chains, tops):
                p_ref[slot, hh, :, cols] = jnp.exp2(scaled(hh, cols, c) - mn).astype(BF16)

        def accumulate(j, slot):
            for hh in range(2):
                acc_ref[hh] = al_ref[slot, hh] * acc_ref[hh] + _dot(vt_ref[j, LANES * hh:LANES * (hh + 1), :], p_ref[slot, hh])

        def step(t, carry):
            scores(2 * t + 1, 1)
            accumulate(jnp.maximum(2 * t - 1, 0), 1)
            softmax(0, False)
            scores(2 * t + 2, 0)
            accumulate(2 * t, 0)
            softmax(1, False)
            return carry

        scores(0, 0)
        lax.fori_loop(0, i // 2, step, 0)

        @pl.when(i % 2 == 1)
        def _():
            scores(i, 1)
            accumulate(jnp.maximum(i - 2, 0), 1)
            softmax(0, False)
            accumulate(i - 1, 0)
            softmax(1, True)
            accumulate(i, 1)

        @pl.when(i % 2 == 0)
        def _():
            accumulate(jnp.maximum(i - 1, 0), 1)
            softmax(0, True)
            accumulate(i, 0)
        den = [acc_ref[hh, 64:65, :] for hh in range(2)]
        o_ref[...] = jnp.concatenate([acc_ref[hh, 0:64, :] / den[hh] for hh in range(2)], axis=0).T
        sub = lax.broadcasted_iota(jnp.int32, (8, tb), 0)
        lse = [m_ref[hh] + jnp.log(den[hh]) * LOG2_E for hh in range(2)]
        l_ref[0, 0] = jnp.where(sub == 0, lse[0], jnp.where(sub == 1, lse[1], 0.0))

    return _pcall(body, name="mla_fwd", grid=(MLA_HEADS // 2, nb),
                  in_specs=[pl.BlockSpec((tb, 256), lambda p, i: (i, p)), pl.BlockSpec((T, 256), lambda p, i: (0, p)),
                            pl.BlockSpec((nb, 2 * LANES, tb), lambda p, i: (0, p, 0))],
                  out_specs=[pl.BlockSpec((tb, LANES), lambda p, i: (i, p)),
                             pl.BlockSpec((1, 1, 8, tb), lambda p, i: (p, i, 0, 0))],
                  out_shape=[SDS((T, D), F32), SDS((MLA_HEADS // 2, nb, 8, tb), F32)],
                  scratch=[pltpu.VMEM((2, 2, tb, tb), F32), pltpu.VMEM((2, 2, tb, tb), BF16), pltpu.VMEM((2, 2, 1, tb), F32),
                           pltpu.VMEM((2, 1, tb), F32), pltpu.VMEM((2, LANES, tb), F32)],
                  sem=("parallel", "arbitrary"))(qm, km, vt)


def _swa_mask(n):
    row = lax.broadcasted_iota(jnp.int32, (WINDOW, 2 * WINDOW), 0)
    col = lax.broadcasted_iota(jnp.int32, (WINDOW, 2 * WINDOW), 1)
    rel = row - col + WINDOW
    return (rel >= 0) & (rel < WINDOW) & ((col >= WINDOW) | (n > 0))


def _swa_specs(T):
    nb = T // WINDOW
    cur = lambda w: pl.BlockSpec((WINDOW, w), lambda n: (n, 0))
    prev = lambda w: pl.BlockSpec((WINDOW, w), lambda n: (jnp.maximum(n - 1, 0), 0))
    return nb, cur, prev


def _swa_fwd(sinks, qs, ks, vs):
    T = qs.shape[0]
    nb, cur, prev = _swa_specs(T)

    def body(sink_ref, q_ref, kc_ref, kp_ref, vc_ref, vp_ref, o_ref, l_ref):
        n = pl.program_id(0)
        mask = _swa_mask(n)
        lo = lax.broadcasted_iota(jnp.int32, (WINDOW, LANES), 1) < 64
        for g in range(2):
            gs = slice(LANES * g, LANES * (g + 1))
            kb = jnp.concatenate([kp_ref[:, gs], kc_ref[:, gs]], axis=0)
            vb = jnp.concatenate([vp_ref[:, gs], vc_ref[:, gs]], axis=0)
            for jj in range(4):
                j = 4 * g + jj
                sl = slice(LANES * j, LANES * (j + 1))
                qp = q_ref[:, sl]
                outs, lses = [], []
                for hf in range(2):
                    hm = lo if hf == 0 else jnp.logical_not(lo)
                    qh = jnp.where(hm, qp, jnp.zeros_like(qp))
                    s = jnp.where(mask, _dot_nt(qh, kb) * SWA_SCALE, NEG)
                    sk = sink_ref[2 * j + hf]
                    m = jnp.maximum(jnp.max(s, axis=1, keepdims=True), sk)
                    e = jnp.exp(s - m)
                    den = jnp.sum(e, axis=1, keepdims=True) + jnp.exp(sk - m)
                    p = e / den
                    outs.append(_dot(p.astype(BF16), vb))
                    lses.append(jnp.broadcast_to(m + jnp.log(den), (WINDOW, LANES)))
                o_ref[:, sl] = jnp.where(lo, outs[0], outs[1])
                l_ref[:, sl] = jnp.where(lo, lses[0], lses[1])

    return _pcall(body, name="swa_fwd", grid=(nb,),
                  in_specs=[pl.BlockSpec(memory_space=pltpu.SMEM), cur(D), cur(256), prev(256), cur(256), prev(256)],
                  out_specs=[cur(D), cur(D)], out_shape=[SDS((T, D), F32)] * 2,
                  sem=("parallel",))(sinks, qs, ks, ks, vs, vs)


def _fwd_mix(om, os_, z, x, wmu, wsu, wo, g2, tm):
    T = x.shape[0]

    def body(om_ref, os_ref, ga_ref, gb_ref, x_ref, wmu_ref, wsu_ref, wo_ref, g2_ref,
             y_ref, yo_ref, au_ref, bu_ref, x1_ref):
        au = _dot(om_ref[...].astype(BF16), wmu_ref[...])
        bu = _dot(os_ref[...].astype(BF16), wsu_ref[...])
        au_ref[...] = au
        bu_ref[...] = bu
        y = (_sigmoid(ga_ref[...]) * au + _sigmoid(gb_ref[...]) * bu).astype(BF16)
        y_ref[...] = y
        yo = _dot(y, wo_ref[...])
        yo_ref[...] = yo
        x1_ref[...] = x_ref[...] + _rms(yo, g2_ref[...])

    r = _rows(tm, D)
    w = _full((D, D))
    return _pcall(body, name="fwd_mix", grid=(T // tm,),
                  in_specs=[r, r, _rows(tm, D, 1), _rows(tm, D, 2), r, w, w, w, _full((1, D))],
                  out_specs=[r] * 5,
                  out_shape=[SDS((T, D), BF16), SDS((T, D), F32), SDS((T, D), F32), SDS((T, D), F32), SDS((T, D), F32)],
                  sem=("parallel",))(om, os_, z, z, x, wmu, wsu, wo, g2)


def _fwd_mlp_up(x1, g3, w1, tm):
    T = x1.shape[0]

    def body(x_ref, g_ref, w_ref, h_ref, a_ref, u_ref):
        h = _rms(x_ref[...], g_ref[...]).astype(BF16)
        h_ref[...] = h
        a = _dot(h, w_ref[...])
        a_ref[...] = a
        u_ref[...] = jnp.square(jnp.maximum(a, 0.0)).astype(BF16)

    return _pcall(body, name="fwd_mlp_up", grid=(T // tm,),
                  in_specs=[_rows(tm, D), _full((1, D)), _full((D, D_FF))],
                  out_specs=[_rows(tm, D), _rows(tm, D_FF), _rows(tm, D_FF)],
                  out_shape=[SDS((T, D), BF16), SDS((T, D_FF), F32), SDS((T, D_FF), BF16)],
                  sem=("parallel",))(x1, g3, w1)


def _fwd_mlp_down(u, w2, x1, g4, tm):
    T = x1.shape[0]

    def body(u_ref, w_ref, x_ref, g_ref, d_ref, x2_ref):
        d = _dot(u_ref[...], w_ref[...])
        d_ref[...] = d
        x2_ref[...] = x_ref[...] + _rms(d, g_ref[...])

    return _pcall(body, name="fwd_mlp_down", grid=(T // tm,),
                  in_specs=[_rows(tm, D_FF), _full((D_FF, D)), _rows(tm, D), _full((1, D))],
                  out_specs=[_rows(tm, D), _rows(tm, D)], out_shape=[SDS((T, D), F32)] * 2,
                  sem=("parallel",))(u, w2, x1, g4)


def _ple_fwd_bwd(p, x2, tgt, wple, g5, wpg, tm):
    T = x2.shape[0]

    def body(p_ref, x2_ref, t_ref, wple_ref, g5_ref, wpg_ref, loss_ref, dx2_ref, dgt_ref, de0_ref, dg5_ref):
        @pl.when(pl.program_id(0) == 0)
        def _():
            loss_ref[...] = jnp.zeros_like(loss_ref)
            dg5_ref[...] = jnp.zeros_like(dg5_ref)

        e0 = _dot(p_ref[...].astype(BF16), wple_ref[...])
        g5 = g5_ref[...]
        r = lax.rsqrt(jnp.mean(e0 * e0, axis=-1, keepdims=True) + EPS)
        en = e0 * r
        e = en * g5
        x2 = x2_ref[...]
        s = _sigmoid(_dot(x2.astype(BF16), wpg_ref[...]))
        diff = x2 + s * e - t_ref[...]
        sq = jnp.sum(jnp.sum(diff * diff, axis=1, keepdims=True), axis=0, keepdims=True)
        loss_ref[...] += jnp.broadcast_to(sq * (0.5 / D), loss_ref.shape)
        dx3 = diff * (1.0 / D)
        de = dx3 * s
        dgt = (dx3 * e * s * (1.0 - s)).astype(BF16)
        dgt_ref[...] = dgt
        dn = de * g5
        de0_ref[...] = (r * (dn - en * jnp.mean(dn * en, axis=-1, keepdims=True))).astype(BF16)
        dg5_ref[...] += jnp.sum(de * en, axis=0, keepdims=True)
        dx2_ref[...] = dx3 + _dot_nt(dgt, wpg_ref[...])

    r = _rows(tm, D)
    return _pcall(body, name="ple_fwd_bwd", grid=(T // tm,),
                  in_specs=[_rows(tm, PLE), r, r, _full((PLE, D)), _full((1, D)), _full((D, D))],
                  out_specs=[_full((8, LANES)), r, r, r, _full((1, D))],
                  out_shape=[SDS((8, LANES), F32), SDS((T, D), F32), SDS((T, D), BF16), SDS((T, D), BF16), SDS((1, D), F32)],
                  sem=("arbitrary",))(p, x2, tgt, wple, g5, wpg)


def _bwd_mlp_down(dx2, d, g4, w2, a, tm):
    T = dx2.shape[0]

    def body(dx_ref, d_ref, g_ref, w_ref, a_ref, dd_ref, da_ref, dg_ref):
        @pl.when(pl.program_id(0) == 0)
        def _():
            dg_ref[...] = jnp.zeros_like(dg_ref)

        dd, dg = _rms_bwd(dx_ref[...], d_ref[...], g_ref[...])
        dg_ref[...] += dg
        ddb = dd.astype(BF16)
        dd_ref[...] = ddb
        du = _dot_nt(ddb, w_ref[...])
        da_ref[...] = (du * (2.0 * jnp.maximum(a_ref[...], 0.0))).astype(BF16)

    return _pcall(body, name="bwd_mlp_down", grid=(T // tm,),
                  in_specs=[_rows(tm, D), _rows(tm, D), _full((1, D)), _full((D_FF, D)), _rows(tm, D_FF)],
                  out_specs=[_rows(tm, D), _rows(tm, D_FF), _full((1, D))],
                  out_shape=[SDS((T, D), BF16), SDS((T, D_FF), BF16), SDS((1, D), F32)],
                  sem=("arbitrary",))(dx2, d, g4, w2, a)


def _bwd_mlp_up(da, w1, x1, g3, dx2, tm):
    T = dx2.shape[0]

    def body(da_ref, w_ref, x_ref, g_ref, dx2_ref, dx1_ref, dg_ref):
        @pl.when(pl.program_id(0) == 0)
        def _():
            dg_ref[...] = jnp.zeros_like(dg_ref)

        dh = _dot_nt(da_ref[...], w_ref[...])
        dx, dg = _rms_bwd(dh, x_ref[...], g_ref[...])
        dg_ref[...] += dg
        dx1_ref[...] = dx2_ref[...] + dx

    return _pcall(body, name="bwd_mlp_up", grid=(T // tm,),
                  in_specs=[_rows(tm, D_FF), _full((D, D_FF)), _rows(tm, D), _full((1, D)), _rows(tm, D)],
                  out_specs=[_rows(tm, D), _full((1, D))],
                  out_shape=[SDS((T, D), F32), SDS((1, D), F32)], sem=("arbitrary",))(da, w1, x1, g3, dx2)


def _bwd_mix(dx1, yo, g2, wo, z, au, bu, wmu, wsu, om, tm):
    T = dx1.shape[0]

    def body(dx_ref, yo_ref, g_ref, wo_ref, ga_ref, gb_ref, au_ref, bu_ref, wmu_ref, wsu_ref, om_ref,
             dyo_ref, dg_ref, dau_ref, dbu_ref, dga_ref, dgb_ref, dos_ref, dl_ref, dot_ref):
        @pl.when(pl.program_id(0) == 0)
        def _():
            dg_ref[...] = jnp.zeros_like(dg_ref)

        dyo, dg = _rms_bwd(dx_ref[...], yo_ref[...], g_ref[...])
        dg_ref[...] += dg
        dyob = dyo.astype(BF16)
        dyo_ref[...] = dyob
        dy = _dot_nt(dyob, wo_ref[...])
        sa = _sigmoid(ga_ref[...])
        sb = _sigmoid(gb_ref[...])
        dau = (dy * sa).astype(BF16)
        dbu = (dy * sb).astype(BF16)
        dau_ref[...] = dau
        dbu_ref[...] = dbu
        dga_ref[...] = (dy * au_ref[...] * sa * (1.0 - sa)).astype(BF16)
        dgb_ref[...] = (dy * bu_ref[...] * sb * (1.0 - sb)).astype(BF16)
        dom = _dot_nt(dau, wmu_ref[...])
        dos_ref[...] = _dot_nt(dbu, wsu_ref[...])
        prod = dom * om_ref[...]
        sub = lax.broadcasted_iota(jnp.int32, (8, tm), 0)
        for pr in range(MLA_HEADS // 2):
            sl = slice(LANES * pr, LANES * (pr + 1))
            pt = prod[:, sl].T
            d0 = jnp.sum(pt[0:64], axis=0, keepdims=True)
            d1 = jnp.sum(pt[64:128], axis=0, keepdims=True)
            dl_ref[pr, 0] = jnp.where(sub == 0, d0, jnp.where(sub == 1, d1, 0.0))
            dot_ref[0, sl, :] = dom[:, sl].T.astype(BF16)

    r = _rows(tm, D)
    w = _full((D, D))
    return _pcall(body, name="bwd_mix", grid=(T // tm,),
                  in_specs=[r, r, _full((1, D)), w, _rows(tm, D, 1), _rows(tm, D, 2), r, r, w, w, r],
                  out_specs=[r, _full((1, D)), r, r, r, r, r, pl.BlockSpec((MLA_HEADS // 2, 1, 8, tm), lambda i: (0, i, 0, 0)),
                             pl.BlockSpec((1, D, tm), lambda i: (i, 0, 0))],
                  out_shape=[SDS((T, D), BF16), SDS((1, D), F32), SDS((T, D), BF16), SDS((T, D), BF16), SDS((T, D), BF16),
                             SDS((T, D), BF16), SDS((T, D), F32), SDS((MLA_HEADS // 2, T // tm, 8, tm), F32),
                             SDS((T // tm, D, tm), BF16)],
                  sem=("arbitrary",))(dx1, yo, g2, wo, z, z, au, bu, wmu, wsu, om)


def _mla_bwd(qm, qt, km, kt, vm, dot, lse, delta, tb):
    T = qm.shape[0]
    nb = T // tb
    cc = ATT_COLS

    def body(q_ref, qt_ref, k_ref, kt_ref, v_ref, dot_ref, l_ref, dl_ref, dqt_ref, dkt_ref, dvt_ref,
             s_ref, dp_ref, p_ref, ds_ref, vh_ref):
        j = pl.program_id(1)

        @pl.when(j == 0)
        def _():
            dqt_ref[...] = jnp.zeros_like(dqt_ref)

        dkt_ref[...] = jnp.zeros_like(dkt_ref)
        dvt_ref[...] = jnp.zeros_like(dvt_ref)
        lo = lax.broadcasted_iota(jnp.int32, (tb, LANES), 1) < 64
        key = lax.broadcasted_iota(jnp.int32, (tb, cc), 0)
        qry = lax.broadcasted_iota(jnp.int32, (tb, cc), 1)
        v = v_ref[...]
        vh_ref[0] = jnp.where(lo, v, jnp.zeros_like(v))
        vh_ref[1] = jnp.where(lo, jnp.zeros_like(v), v)

        def scores(i, slot):
            rows_i = pl.ds(pl.multiple_of(i * tb, tb), tb)
            for hh in range(2):
                sl = slice(LANES * hh, LANES * (hh + 1))
                s_ref[slot, hh] = _dot_nt(k_ref[:, sl], q_ref[rows_i, sl])
                dp_ref[slot, hh] = _dot(vh_ref[hh], dot_ref[i])

        def grads(i, slot, diagonal):
            lse_i = l_ref[0, i]
            delta_i = dl_ref[0, i]
            for hh in range(2):
                for c in range(tb // cc):
                    cols = slice(cc * c, cc * (c + 1))
                    p = jnp.exp2(s_ref[slot, hh, :, cols] * MLA_LOG2_SCALE - lse_i[hh:hh + 1, cols])
                    if diagonal:
                        p = jnp.where(key <= qry + cc * c, p, 0.0)
                    p_ref[hh, :, cols] = p.astype(BF16)
                    ds_ref[hh, :, cols] = (p * (dp_ref[slot, hh, :, cols] - delta_i[hh:hh + 1, cols]) * MLA_SCALE).astype(BF16)
            for hh in range(2):
                sl = slice(LANES * hh, LANES * (hh + 1))
                half = slice(64 * hh, 64 * (hh + 1))
                dvt_ref[0, half, :] += _dot_nt(dot_ref[i, half, :], p_ref[hh])
                dkt_ref[0, sl, :] += _dot_nt(qt_ref[i, sl, :], ds_ref[hh])
                dqt_ref[i, sl, :] += _dot(kt_ref[0, sl, :], ds_ref[hh])

        n_off = nb - 1 - j

        def step(u, carry):
            i0 = j + 1 + 2 * u
            scores(i0 + 1, 1)
            grads(i0, 0, False)
            scores(jnp.where(i0 + 2 < nb, i0 + 2, j), 0)
            grads(i0 + 1, 1, False)
            return carry

        scores(jnp.where(n_off > 0, j + 1, j), 0)
        lax.fori_loop(0, n_off // 2, step, 0)

        @pl.when(n_off % 2 == 1)
        def _():
            scores(j, 1)
            grads(nb - 1, 0, False)
            grads(j, 1, True)

        @pl.when(n_off % 2 == 0)
        def _():
            grads(j, 0, True)

    blk = lambda w: pl.BlockSpec((tb, w), lambda p, j: (j, p))
    stat = pl.BlockSpec((1, nb, 8, tb), lambda p, j: (p, 0, 0, 0))
    pair_t = lambda w: pl.BlockSpec((nb, w, tb), lambda p, j: (0, p, 0))
    blk_t = lambda w: pl.BlockSpec((1, w, tb), lambda p, j: (j, p, 0))
    return _pcall(body, name="mla_bwd", grid=(MLA_HEADS // 2, nb),
                  in_specs=[pl.BlockSpec((T, 256), lambda p, j: (0, p)), pair_t(256), blk(256), blk_t(256), blk(LANES),
                            pair_t(LANES), stat, stat],
                  out_specs=[pair_t(256), blk_t(256), blk_t(LANES)],
                  out_shape=[SDS((nb, 2048, tb), F32), SDS((nb, 2048, tb), F32), SDS((nb, D, tb), F32)],
                  scratch=[pltpu.VMEM((2, 2, tb, tb), F32), pltpu.VMEM((2, 2, tb, tb), F32), pltpu.VMEM((2, tb, tb), BF16),
                           pltpu.VMEM((2, tb, tb), BF16), pltpu.VMEM((2, tb, LANES), BF16)],
                  sem=("parallel", "arbitrary"))(qm, qt, km, kt, vm, dot, lse, delta)


def _swa_bwd(sinks, qs, ks, vs, do, o, lse):
    T = qs.shape[0]
    nb, cur, prev = _swa_specs(T)

    def body(sink_ref, q_ref, kc_ref, kp_ref, vc_ref, vp_ref, do_ref, o_ref, l_ref,
             dq_ref, dkc_ref, dkp_ref, dvc_ref, dvp_ref, dsink_ref):
        n = pl.program_id(0)

        @pl.when(n == 0)
        def _():
            dsink_ref[...] = jnp.zeros_like(dsink_ref)

        mask = _swa_mask(n)
        lo = lax.broadcasted_iota(jnp.int32, (WINDOW, LANES), 1) < 64
        lane8 = lax.broadcasted_iota(jnp.int32, (8, LANES), 1)
        dsink = jnp.zeros((8, LANES), F32)
        for g in range(2):
            gs = slice(LANES * g, LANES * (g + 1))
            kb = jnp.concatenate([kp_ref[:, gs], kc_ref[:, gs]], axis=0)
            vb = jnp.concatenate([vp_ref[:, gs], vc_ref[:, gs]], axis=0)
            dkb = jnp.zeros((2 * WINDOW, LANES), F32)
            dvb = jnp.zeros((2 * WINDOW, LANES), F32)
            for jj in range(4):
                j = 4 * g + jj
                sl = slice(LANES * j, LANES * (j + 1))
                qp = q_ref[:, sl]
                d_o = do_ref[:, sl]
                prod = d_o * o_ref[:, sl]
                lse_b = l_ref[:, sl]
                dqs = []
                for hf in range(2):
                    hm = lo if hf == 0 else jnp.logical_not(lo)
                    qh = jnp.where(hm, qp, jnp.zeros_like(qp))
                    s = jnp.where(mask, _dot_nt(qh, kb) * SWA_SCALE, NEG)
                    lse_h = jnp.max(jnp.where(hm, lse_b, -jnp.inf), axis=1, keepdims=True)
                    p = jnp.exp(s - lse_h)
                    dom = jnp.where(hm, d_o, 0.0).astype(BF16)
                    dp = _dot_nt(dom, vb)
                    delta = jnp.sum(jnp.where(hm, prod, 0.0), axis=1, keepdims=True)
                    ds = (p * (dp - delta) * SWA_SCALE).astype(BF16)
                    p_sink = jnp.exp(sink_ref[2 * j + hf] - lse_h)
                    d_sink = -jnp.sum(p_sink * delta, axis=0, keepdims=True)
                    dsink = dsink + jnp.where(lane8 == 2 * j + hf, d_sink, 0.0)
                    dvb = dvb + _dot_tn(p.astype(BF16), dom)
                    dkb = dkb + _dot_tn(ds, qh)
                    dqs.append(_dot(ds, kb))
                dq_ref[:, sl] = jnp.where(lo, dqs[0], dqs[1])
            dkp_ref[:, gs] = dkb[:WINDOW]
            dkc_ref[:, gs] = dkb[WINDOW:]
            dvp_ref[:, gs] = dvb[:WINDOW]
            dvc_ref[:, gs] = dvb[WINDOW:]
        dsink_ref[...] += dsink

    return _pcall(body, name="swa_bwd", grid=(nb,),
                  in_specs=[pl.BlockSpec(memory_space=pltpu.SMEM), cur(D), cur(256), prev(256), cur(256), prev(256),
                            cur(D), cur(D), cur(D)],
                  out_specs=[cur(D), cur(256), cur(256), cur(256), cur(256), _full((8, LANES))],
                  out_shape=[SDS((T, D), F32), SDS((T, 256), F32), SDS((T, 256), F32), SDS((T, 256), F32), SDS((T, 256), F32),
                             SDS((8, LANES), F32)],
                  sem=("arbitrary",))(sinks, qs, ks, ks, vs, vs, do, o, lse)


def _bwd_qkv(dqm, dkm, dvm, dqs, dkc, dkp, dvc, dvp, z, gq, gkv, wqb, wkn, wv, tab_m, tab_s):
    T = z.shape[0]
    tm = WINDOW
    nb = T // tm
    per = dqm.shape[2] // tm

    def body(dqm_ref, dkm_ref, dvm_ref, dqs_ref, dkc_ref, dkp_ref, dvc_ref, dvp_ref, qa_ref, kva_ref, gq_ref, gkv_ref,
             wqb_ref, wkn_ref, wv_ref, cm_ref, am_ref, bm_ref, cs_ref, as_ref, bs_ref,
             dq_out, dkn_out, dv_out, dsq_ref, drest_ref, dgq_ref, dgkv_ref):
        i = pl.program_id(0)

        @pl.when(i == 0)
        def _():
            dgq_ref[...] = jnp.zeros_like(dgq_ref)
            dgkv_ref[...] = jnp.zeros_like(dgkv_ref)

        cm, am, bm = cm_ref[...], -am_ref[...], -bm_ref[...]
        cs, as_, bs = cs_ref[...], -as_ref[...], -bs_ref[...]
        lane = lax.broadcasted_iota(jnp.int32, (tm, LANES), 1)
        nope = lane < MLA_NOPE
        roped = jnp.logical_and(lane >= MLA_NOPE, lane < MLA_NOPE + MLA_ROPE)
        dkr = jnp.zeros((tm, LANES), F32)
        for h in range(MLA_HEADS):
            sl = slice(LANES * h, LANES * (h + 1))
            dq_out[:, sl] = _rope(dqm_ref[0, sl, :].T, cm, am, bm, MLA_ROPE // 2).astype(BF16)
            dk_h = dkm_ref[0, sl, :].T
            dkn_out[:, sl] = jnp.where(nope, dk_h, 0.0).astype(BF16)
            dkr = dkr + jnp.where(roped, dk_h, 0.0)
        for j in range(D // LANES):
            sl = slice(LANES * j, LANES * (j + 1))
            dv_out[:, sl] = dvm_ref[0, sl, :].T.astype(BF16)
        dqn = _dot_nt(dq_out[...], wqb_ref[...])
        dkvn = _dot_nt(dkn_out[...], wkn_ref[...]) + _dot_nt(dv_out[...], wv_ref[...])
        dqa, dgq = _rms_bwd(dqn, qa_ref[...], gq_ref[...])
        dkva, dgkv = _rms_bwd(dkvn, kva_ref[...], gkv_ref[...])
        dgq_ref[...] += dgq
        dgkv_ref[...] += dgkv
        for j in range(D // LANES):
            sl = slice(LANES * j, LANES * (j + 1))
            dsq_ref[:, sl] = _rope(dqs_ref[:, sl], cs, as_, bs, SWA_HD // 2).astype(BF16)
        keep = (i < nb - 1).astype(F32)
        drest_ref[:, 0:256] = dqa.astype(BF16)
        for j in range(2):
            sl = slice(LANES * j, LANES * (j + 1))
            dk = dkc_ref[:, sl] + keep * dkp_ref[:, sl]
            drest_ref[:, 256 + LANES * j:256 + LANES * (j + 1)] = _rope(dk, cs, as_, bs, SWA_HD // 2).astype(BF16)
        drest_ref[:, 512:768] = (dvc_ref[...] + keep * dvp_ref[...]).astype(BF16)
        drest_ref[:, 768:896] = dkva.astype(BF16)
        drest_ref[:, 896:1024] = _rope(dkr, cm, am, bm, MLA_ROPE // 2).astype(BF16)

    nxt = pl.BlockSpec((tm, 256), lambda i: (jnp.minimum(i + 1, nb - 1), 0))
    tab = [_rows(tm, LANES)] * 6
    return _pcall(body, name="bwd_qkv", grid=(nb,),
                  in_specs=[pl.BlockSpec((1, 2048, tm), lambda i: (i // per, 0, i % per)),
                            pl.BlockSpec((1, 2048, tm), lambda i: (i // per, 0, i % per)),
                            pl.BlockSpec((1, 1024, tm), lambda i: (i // per, 0, i % per)), _rows(tm, 1024), _rows(tm, 256), nxt,
                            _rows(tm, 256), nxt, _rows(tm, 256, 12), _rows(tm, 128, 30), _full((1, Q_LORA)), _full((1, KV_LORA)),
                            _full((Q_LORA, 2048)), _full((KV_LORA, 2048)), _full((KV_LORA, 1024))] + tab,
                  out_specs=[_rows(tm, 2048), _rows(tm, 2048), _rows(tm, 1024), _rows(tm, 1024), _rows(tm, 1024),
                             _full((1, Q_LORA)), _full((1, KV_LORA))],
                  out_shape=[SDS((T, 2048), BF16), SDS((T, 2048), BF16), SDS((T, 1024), BF16), SDS((T, 1024), BF16),
                             SDS((T, 1024), BF16), SDS((1, Q_LORA), F32), SDS((1, KV_LORA), F32)],
                  sem=("arbitrary",))(dqm, dkm, dvm, dqs, dkc, dkp, dvc, dvp, z, z, gq, gkv, wqb, wkn, wv, *tab_m, *tab_s)


def _bwd_in(dsq, dga, dgb, drest, w_in_p, x, g1, dx1, tm):
    T = x.shape[0]

    def body(a_ref, b_ref, c_ref, d_ref, w_ref, x_ref, g_ref, dx1_ref, dx_ref, dg_ref):
        @pl.when(pl.program_id(0) == 0)
        def _():
            dg_ref[...] = jnp.zeros_like(dg_ref)

        dh = (_dot_nt(a_ref[...], w_ref[:, 0:1024]) + _dot_nt(b_ref[...], w_ref[:, 1024:2048])
              + _dot_nt(c_ref[...], w_ref[:, 2048:3072]) + _dot_nt(d_ref[...], w_ref[:, 3072:4096]))
        dx, dg = _rms_bwd(dh, x_ref[...], g_ref[...])
        dg_ref[...] += dg
        dx_ref[...] = dx1_ref[...] + dx

    r = _rows(tm, D)
    return _pcall(body, name="bwd_in", grid=(T // tm,),
                  in_specs=[r, r, r, r, _full((D, NZ)), r, _full((1, D)), r],
                  out_specs=[r, _full((1, D))], out_shape=[SDS((T, D), F32), SDS((1, D), F32)],
                  sem=("arbitrary",))(dsq, dga, dgb, drest, w_in_p, x, g1, dx1)


def _wgrad(a, g, name):
    T, K = a.shape
    N = g.shape[1]
    tk, tn, tt = min(K, 1024), min(N, 1024), min(T, 1024)
    assert K % tk == 0 and N % tn == 0 and T % tt == 0, (a.shape, g.shape)

    def body(a_ref, g_ref, o_ref):
        @pl.when(pl.program_id(2) == 0)
        def _():
            o_ref[...] = jnp.zeros_like(o_ref)

        o_ref[...] += _dot_tn(a_ref[...].astype(BF16), g_ref[...].astype(BF16))

    return _pcall(body, name=name, grid=(K // tk, N // tn, T // tt),
                  in_specs=[pl.BlockSpec((tt, tk), lambda k, n, t: (t, k)), pl.BlockSpec((tt, tn), lambda k, n, t: (t, n))],
                  out_specs=pl.BlockSpec((tk, tn), lambda k, n, t: (k, n)), out_shape=SDS((K, N), F32),
                  sem=("parallel", "parallel", "arbitrary"))(a, g)


def _adamw(w, packed_g, m, v, name):
    _, R, C = w.shape
    _, row0, lane0 = PACK_AT[name]
    tr = min(R, 256 if row0 % 256 == 0 else 128)
    assert row0 % tr == 0 and R % tr == 0

    def body(w_ref, g_ref, m_ref, v_ref, go_ref, d_ref, m2_ref, v2_ref):
        g_ = g_ref[:, lane0:lane0 + C]
        go_ref[0] = g_
        m2 = ADAM_B1 * m_ref[0] + (1.0 - ADAM_B1) * g_
        v2 = ADAM_B2 * v_ref[0] + (1.0 - ADAM_B2) * jnp.square(g_)
        m_hat = m2 / (1.0 - ADAM_B1 ** ADAM_STEP)
        v_hat = v2 / (1.0 - ADAM_B2 ** ADAM_STEP)
        d_ref[0] = -ADAM_LR * (m_hat / (jnp.sqrt(v_hat) + ADAM_EPS) + ADAM_WD * w_ref[0])
        m2_ref[0] = m2
        v2_ref[0] = v2

    r = pl.BlockSpec((1, tr, C), lambda i: (0, i, 0))
    return _pcall(body, name="adamw_" + name, grid=(R // tr,),
                  in_specs=[r, pl.BlockSpec((tr, D), lambda i: (row0 // tr + i, 0)), r, r], out_specs=[r] * 4,
                  out_shape=[SDS((1, R, C), F32)] * 4, sem=("parallel",))(w, packed_g, m, v)


def _adamw_small(w, parts, m, v):
    def body(w_ref, p_ref, m_ref, v_ref, g_ref, d_ref, m2_ref, v2_ref):
        g_ = p_ref[0]
        for k in range(1, N_DEV):
            g_ = g_ + p_ref[k]
        g_ref[...] = g_
        m2 = ADAM_B1 * m_ref[...] + (1.0 - ADAM_B1) * g_
        v2 = ADAM_B2 * v_ref[...] + (1.0 - ADAM_B2) * jnp.square(g_)
        m_hat = m2 / (1.0 - ADAM_B1 ** ADAM_STEP)
        v_hat = v2 / (1.0 - ADAM_B2 ** ADAM_STEP)
        d_ref[...] = -ADAM_LR * (m_hat / (jnp.sqrt(v_hat) + ADAM_EPS) + ADAM_WD * w_ref[...])
        m2_ref[...] = m2
        v2_ref[...] = v2

    s = _full((8, D))
    return _pcall(body, name="adamw_small", grid=(1,), in_specs=[s, _full((N_DEV, 8, D)), s, s], out_specs=[s] * 4,
                  out_shape=[SDS((8, D), F32)] * 4, sem=("arbitrary",))(w, parts, m, v)


ANY = pl.BlockSpec(memory_space=pl.ANY)


def _place():
    x, y, c = lax.axis_index("x"), lax.axis_index("y"), lax.axis_index("c")
    chips = [(1 - x, y), (x, 1 - y), (1 - x, 1 - y)]
    return x, y, c, chips


def _all_gather(wpk):
    rows = wpk.shape[0]
    HALF = rows // 2
    assert HALF % 16 == 0

    def body(in_ref, out_ref, send_sems, recv_sems):
        x, y, c, chips = _place()
        half = pl.ds(pl.multiple_of(c * HALF, 16), HALF)
        other = pl.ds(pl.multiple_of((1 - c) * HALF, 16), HALF)

        def copy(k, src, dst, to):
            return pltpu.make_async_remote_copy(src_ref=src, dst_ref=dst, send_sem=send_sems.at[k], recv_sem=recv_sems.at[k],
                                                device_id=to, device_id_type=MESH)

        first = [copy(k, in_ref.at[half], out_ref.at[2 * x + y, half], (cx, cy, c)) for k, (cx, cy) in enumerate(chips)]
        for cp in first:
            cp.start()
        passed = []
        for k, (cx, cy) in enumerate(chips):
            slot = out_ref.at[2 * cx + cy, half]
            copy(k, slot, slot, (x, y, c)).wait_recv()
            fwd = copy(3 + k, slot, slot, (x, y, 1 - c))
            fwd.start()
            passed.append(fwd)
        for k, (cx, cy) in enumerate(chips):
            slot = out_ref.at[2 * cx + cy, other]
            copy(3 + k, slot, slot, (x, y, c)).wait_recv()
        for cp in first + passed:
            cp.wait_send()

    return _pcall(body, name="all_gather_weights", in_specs=[ANY], out_specs=ANY,
                  out_shape=SDS((N_CHIPS, rows, D), BF16),
                  scratch=[pltpu.SemaphoreType.DMA((6,)), pltpu.SemaphoreType.DMA((6,))])(wpk)


HBM = pl.BlockSpec(memory_space=pltpu.HBM)
SEM = pl.BlockSpec(memory_space=pltpu.SEMAPHORE)
DATAFLOW = pltpu.SideEffectType.DATAFLOW_SIDE_EFFECTING


def _in_hbm(a):
    return pltpu.with_memory_space_constraint(a, pltpu.HBM)


def _gather_late_start(wpk, after):
    rows = wpk.shape[0]

    def body(in_ref, land_ref, after_ref, send_sems, recv_sems, in_thru, land_thru, token):
        x, y, c, chips = _place()
        for k, (cx, cy) in enumerate(chips):
            pltpu.make_async_remote_copy(src_ref=in_ref, dst_ref=land_ref.at[2 * x + y], send_sem=send_sems.at[k],
                                         recv_sem=recv_sems.at[k], device_id=(cx, cy, c), device_id_type=MESH).start()
        token[...] = jnp.zeros_like(token)

    return pl.pallas_call(
        body, name="gather_late_start",
        out_shape=(pltpu.SemaphoreType.DMA((3,)), pltpu.SemaphoreType.DMA((3,)), pltpu.HBM(wpk.shape, wpk.dtype),
                   pltpu.HBM((N_CHIPS, rows, D), wpk.dtype), SDS((8, LANES), F32)),
        in_specs=(HBM, HBM, ANY), out_specs=(SEM, SEM, HBM, HBM, pl.BlockSpec(memory_space=pltpu.VMEM)),
        input_output_aliases={0: 2, 1: 3}, compiler_params=pltpu.CompilerParams(has_side_effects=DATAFLOW),
    )(_in_hbm(wpk), _in_hbm(lax.empty((N_CHIPS, rows, D), wpk.dtype)), after)


def _gather_late_wait(send_sems, recv_sems, in_thru, land_thru, after):
    def body(in_ref, land_ref, send_sems, recv_sems, after_ref, in_dead, got_ref):
        x, y, c, chips = _place()
        for k, (cx, cy) in enumerate(chips):
            cp = pltpu.make_async_remote_copy(src_ref=in_ref, dst_ref=land_ref.at[2 * cx + cy], send_sem=send_sems.at[k],
                                              recv_sem=recv_sems.at[k], device_id=(cx, cy, c), device_id_type=MESH)
            cp.wait_send()
            cp.wait_recv()

    return pl.pallas_call(
        body, name="gather_late_wait",
        out_shape=(pltpu.HBM(in_thru.shape, in_thru.dtype), pltpu.HBM(land_thru.shape, land_thru.dtype)),
        in_specs=(HBM, HBM, SEM, SEM, ANY), out_specs=(HBM, HBM), input_output_aliases={0: 0, 1: 1},
        compiler_params=pltpu.CompilerParams(has_side_effects=DATAFLOW),
    )(in_thru, land_thru, send_sems, recv_sems, after)[1]


def _rs_sibling(gpk):
    HALF = gpk.shape[1] // 2

    def body(in_ref, out_ref, send_sem, recv_sem):
        x, y, c, _ = _place()
        theirs = pl.ds(pl.multiple_of((1 - c) * HALF, 8), HALF)
        cp = pltpu.make_async_remote_copy(src_ref=in_ref.at[:, theirs], dst_ref=out_ref, send_sem=send_sem, recv_sem=recv_sem,
                                          device_id=(x, y, 1 - c), device_id_type=MESH)
        cp.start()
        cp.wait()

    return _pcall(body, name="rs_sibling", in_specs=[ANY], out_specs=ANY, out_shape=SDS((N_CHIPS, HALF, D), F32),
                  scratch=[pltpu.SemaphoreType.DMA, pltpu.SemaphoreType.DMA])(gpk)


def _rs_add_sibling(cidx, gpk, got):
    HALF = got.shape[1]
    th = HALF // 4
    nh = HALF // th
    assert th % 16 == 0

    def body(c_ref, a_ref, b_ref, o_ref):
        o_ref[...] = (a_ref[...] + b_ref[...]).astype(BF16)

    gs = pltpu.PrefetchScalarGridSpec(
        num_scalar_prefetch=1, grid=(N_CHIPS, nh),
        in_specs=[pl.BlockSpec((1, th, D), lambda j, i, c: (j, c[0] * nh + i, 0)), pl.BlockSpec((1, th, D), lambda j, i, c: (j, i, 0))],
        out_specs=pl.BlockSpec((1, th, D), lambda j, i, c: (j, i, 0)))
    return pl.pallas_call(body, name="rs_add_sibling", grid_spec=gs, out_shape=SDS((N_CHIPS, HALF, D), BF16),
                          compiler_params=pltpu.CompilerParams(dimension_semantics=("parallel", "parallel"),
                                                               vmem_limit_bytes=48 << 20))(cidx, gpk, got)


def _rs_chips(part, small):
    def body(p_ref, s_ref, o_ref, so_ref, send_sems, recv_sems, ssend_sems, srecv_sems, local_sem):
        x, y, c, chips = _place()
        me = 2 * x + y
        mine_s = pltpu.make_async_copy(s_ref, so_ref.at[4 * x + 2 * y + c], local_sem)
        mine_s.start()
        sends = []
        for k, (cx, cy) in enumerate(chips):
            sends.append(pltpu.make_async_remote_copy(src_ref=p_ref.at[2 * cx + cy], dst_ref=o_ref.at[me], send_sem=send_sems.at[k],
                                                      recv_sem=recv_sems.at[k], device_id=(cx, cy, c), device_id_type=MESH))
        peers = [(x, y, 1 - c)] + [(cx, cy, c) for cx, cy in chips] + [(cx, cy, 1 - c) for cx, cy in chips]
        for k, to in enumerate(peers):
            sends.append(pltpu.make_async_remote_copy(src_ref=s_ref, dst_ref=so_ref.at[4 * x + 2 * y + c], send_sem=ssend_sems.at[k],
                                                      recv_sem=srecv_sems.at[k], device_id=to, device_id_type=MESH))
        for cp in sends:
            cp.start()
        for k, (cx, cy) in enumerate(chips):
            slot = o_ref.at[2 * cx + cy]
            pltpu.make_async_remote_copy(src_ref=slot, dst_ref=slot, send_sem=send_sems.at[k], recv_sem=recv_sems.at[k],
                                         device_id=(x, y, c), device_id_type=MESH).wait_recv()
        for k, (px, py, pc) in enumerate(peers):
            slot = so_ref.at[4 * px + 2 * py + pc]
            pltpu.make_async_remote_copy(src_ref=slot, dst_ref=slot, send_sem=ssend_sems.at[k], recv_sem=srecv_sems.at[k],
                                         device_id=(x, y, c), device_id_type=MESH).wait_recv()
        for cp in sends:
            cp.wait_send()
        mine_s.wait()

    return _pcall(body, name="rs_chips", in_specs=[ANY, ANY], out_specs=[ANY, ANY],
                  out_shape=[SDS(part.shape, part.dtype), SDS((N_DEV, 8, D), F32)],
                  scratch=[pltpu.SemaphoreType.DMA((3,)), pltpu.SemaphoreType.DMA((3,)), pltpu.SemaphoreType.DMA((7,)),
                           pltpu.SemaphoreType.DMA((7,)), pltpu.SemaphoreType.DMA])(part, small)


def _rs_add_chips(qidx, part, parts):
    HALF = part.shape[1]
    th = HALF // 4
    assert th % 16 == 0

    def body(q_ref, own_ref, p_ref, o_ref):
        for me in range(N_CHIPS):
            @pl.when(q_ref[0] == me)
            def _(me=me):
                t = [(own_ref[0] if j == me else p_ref[j]).astype(F32) for j in range(N_CHIPS)]
                o_ref[...] = ((t[0] + t[1]) + t[2]) + t[3]

    gs = pltpu.PrefetchScalarGridSpec(
        num_scalar_prefetch=1, grid=(HALF // th,),
        in_specs=[pl.BlockSpec((1, th, D), lambda i, q: (q[0], i, 0)), pl.BlockSpec((N_CHIPS, th, D), lambda i, q: (0, i, 0))],
        out_specs=pl.BlockSpec((th, D), lambda i, q: (i, 0)))
    return pl.pallas_call(body, name="rs_add_chips", grid_spec=gs, out_shape=SDS((HALF, D), F32),
                          compiler_params=pltpu.CompilerParams(dimension_semantics=("parallel",),
                                                               vmem_limit_bytes=48 << 20))(qidx, part, parts)


def _rs_join(early, late):
    def body(e_ref, l_ref, eo_ref, lo_ref, send_sems, recv_sems):
        x, y, c, _ = _place()
        cps = [pltpu.make_async_remote_copy(src_ref=src, dst_ref=dst, send_sem=send_sems.at[k], recv_sem=recv_sems.at[k],
                                            device_id=(x, y, 1 - c), device_id_type=MESH)
               for k, (src, dst) in enumerate([(e_ref, eo_ref), (l_ref, lo_ref)])]
        for cp in cps:
            cp.start()
        for cp in cps:
            cp.wait()

    return _pcall(body, name="rs_join", in_specs=[ANY, ANY], out_specs=[ANY, ANY],
                  out_shape=[SDS(early.shape, F32), SDS(late.shape, F32)],
                  scratch=[pltpu.SemaphoreType.DMA((2,)), pltpu.SemaphoreType.DMA((2,))])(early, late)


def _reduce_late_start(gpk, after):
    rows = gpk.shape[1]
    HALF = rows // 2
    assert HALF % 16 == 0

    def body(in_ref, land_ref, after_ref, send_sems, recv_sems, in_thru, land_thru, token):
        x, y, c, chips = _place()
        me = 4 * x + 2 * y + c
        peers = [(x, y, 1 - c)] + [(cx, cy, c) for cx, cy in chips] + [(cx, cy, 1 - c) for cx, cy in chips]
        for k, (px, py, pc) in enumerate(peers):
            src = in_ref.at[2 * px + py, pl.ds(pl.multiple_of(pc * HALF, 16), HALF)]
            pltpu.make_async_remote_copy(src_ref=src, dst_ref=land_ref.at[me], send_sem=send_sems.at[k], recv_sem=recv_sems.at[k],
                                         device_id=(px, py, pc), device_id_type=MESH).start()
        token[...] = jnp.zeros_like(token)

    return pl.pallas_call(
        body, name="reduce_late_start",
        out_shape=(pltpu.SemaphoreType.DMA((7,)), pltpu.SemaphoreType.DMA((7,)), pltpu.HBM(gpk.shape, gpk.dtype),
                   pltpu.HBM((N_DEV, HALF, D), gpk.dtype), SDS((8, LANES), F32)),
        in_specs=(HBM, HBM, ANY), out_specs=(SEM, SEM, HBM, HBM, pl.BlockSpec(memory_space=pltpu.VMEM)),
        input_output_aliases={0: 2, 1: 3}, compiler_params=pltpu.CompilerParams(has_side_effects=DATAFLOW),
    )(_in_hbm(gpk), _in_hbm(lax.empty((N_DEV, HALF, D), gpk.dtype)), after)


def _reduce_late_wait(send_sems, recv_sems, in_thru, land_thru, after):
    def body(in_ref, land_ref, send_sems, recv_sems, after_ref, in_out, got_ref):
        x, y, c, chips = _place()
        peers = [(x, y, 1 - c)] + [(cx, cy, c) for cx, cy in chips] + [(cx, cy, 1 - c) for cx, cy in chips]
        for k, (px, py, pc) in enumerate(peers):
            cp = pltpu.make_async_remote_copy(src_ref=land_ref.at[0], dst_ref=land_ref.at[4 * px + 2 * py + pc],
                                              send_sem=send_sems.at[k], recv_sem=recv_sems.at[k],
                                              device_id=(px, py, pc), device_id_type=MESH)
            cp.wait_send()
            cp.wait_recv()

    return pl.pallas_call(
        body, name="reduce_late_wait",
        out_shape=(pltpu.HBM(in_thru.shape, in_thru.dtype), pltpu.HBM(land_thru.shape, land_thru.dtype)),
        in_specs=(HBM, HBM, SEM, SEM, ANY), out_specs=(HBM, HBM), input_output_aliases={0: 0, 1: 1},
        compiler_params=pltpu.CompilerParams(has_side_effects=DATAFLOW),
    )(in_thru, land_thru, send_sems, recv_sems, after)


def _reduce_late_add(didx, gpk, parts):
    HALF = parts.shape[1]
    th = HALF // 4
    nh = HALF // th
    assert th % 16 == 0

    def body(d_ref, own_ref, p_ref, o_ref):
        for me in range(N_DEV):
            @pl.when(d_ref[0] == me)
            def _(me=me):
                t = [(own_ref[0] if j == me else p_ref[j]).astype(F32) for j in range(N_DEV)]
                o_ref[...] = ((((((t[0] + t[1]) + t[2]) + t[3]) + t[4]) + t[5]) + t[6]) + t[7]

    gs = pltpu.PrefetchScalarGridSpec(
        num_scalar_prefetch=1, grid=(nh,),
        in_specs=[pl.BlockSpec((1, th, D), lambda i, d: (d[1], d[2] * nh + i, 0)), pl.BlockSpec((N_DEV, th, D), lambda i, d: (0, i, 0))],
        out_specs=pl.BlockSpec((th, D), lambda i, d: (i, 0)))
    return pl.pallas_call(body, name="reduce_late_add", grid_spec=gs, out_shape=SDS((HALF, D), F32),
                          compiler_params=pltpu.CompilerParams(dimension_semantics=("parallel",),
                                                               vmem_limit_bytes=48 << 20))(didx, gpk, parts)


def _pack_early(b, dtype):
    lanes = lambda a: jnp.pad(a.astype(dtype), ((0, 0), (0, D - a.shape[1])))
    pair = jnp.concatenate([b["w_q_b"].astype(dtype), b["w_ple"].astype(dtype), jnp.zeros((256, D - 640), dtype)], axis=1)
    return jnp.concatenate([lanes(b["w_in"]), pair, lanes(b["w_kv_b"])], axis=0)


def _pack_late(b, dtype):
    return jnp.concatenate([b[n].astype(dtype) for n in ("w_mla_up", "w_swa_up", "w_out", "w_ple_gate", "w_mlp_up", "w_mlp_down")],
                           axis=0)


def _unpack_shards(pk, which):
    return {n: pk[PACK_AT[n][1]:PACK_AT[n][1] + r, PACK_AT[n][2]:PACK_AT[n][2] + c] for n, r, c in BIG if PACK_AT[n][0] == which}


def _full_weights(gathered, own, chip, which):
    own_b = _unpack_shards(own, which)
    per_chip = [{n: jnp.where(chip == j, own_b[n], blk) for n, blk in _unpack_shards(gathered[j], which).items()}
                for j in range(N_CHIPS)]
    out = {}
    for n in own_b:
        shards = [pc[n] for pc in per_chip]
        if n == "w_in":
            out["w_in_p"] = _w_in_internal(shards)
        else:
            out[n] = jnp.concatenate(shards, axis=1 if n in COL_SHARDED else 0)
    return out


def _split_full_grads(grads, pack, dtype):
    shard = {n: (r, c) for n, r, c in BIG}
    chunks = []
    for j in range(N_CHIPS):
        blocks = {}
        for n, g in grads.items():
            if n == "w_in_p":
                blocks["w_in"] = _w_in_grad_shard(g, j)
                continue
            r, c = shard[n]
            blocks[n] = g[:, j * c:(j + 1) * c] if n in COL_SHARDED else g[j * r:(j + 1) * r]
        chunks.append(pack(blocks, dtype))
    return jnp.stack(chunks)


W_IN_SHARD = 936
W_IN_SEGMENTS = ((0, 256, (3072,)), (256, 384, (3840,)), (384, 416, (4032,)), (416, 1440, (0,)), (1440, 1504, (3328, 3392)),
                 (1504, 1568, (3456, 3520)), (1568, 1632, (3584, 3648)), (1632, 1696, (3712, 3776)), (1696, 3744, (1024,)))


def _w_in_internal(shards):
    def cols(a, b):
        out = []
        for j, s in enumerate(shards):
            lo, hi = max(a, W_IN_SHARD * j), min(b, W_IN_SHARD * (j + 1))
            if lo < hi:
                out.append(s[:, lo - W_IN_SHARD * j:hi - W_IN_SHARD * j])
        return out

    pieces = {}
    for a, b, places in W_IN_SEGMENTS:
        for at in places:
            pieces[at] = cols(a, b)
    zeros = lambda n: [jnp.zeros((D, n), shards[0].dtype)]
    pieces[3968] = zeros(64)
    pieces[4064] = zeros(32)
    return jnp.concatenate([piece for at in sorted(pieces) for piece in pieces[at]], axis=1)


def _w_in_grad_shard(g, j):
    out = []
    for a, b, places in W_IN_SEGMENTS:
        lo, hi = max(a, W_IN_SHARD * j), min(b, W_IN_SHARD * (j + 1))
        if lo < hi:
            parts = [g[:, at + lo - a:at + hi - a] for at in places]
            out.append(parts[0] if len(parts) == 1 else parts[0] + parts[1])
    return jnp.concatenate(out, axis=1)


def _local_step(x, p, tgt, w, small, late_weights, late_grads_out):
    T = x.shape[0]
    tm = 256
    tb = 256
    w_in_p = w["w_in_p"]
    wqb = jnp.pad(w["w_q_b"].reshape(Q_LORA, MLA_HEADS, 96), ((0, 0), (0, 0), (0, 32))).reshape(Q_LORA, 2048)
    wkv = w["w_kv_b"].reshape(KV_LORA, MLA_HEADS, 128)
    wkn = jnp.pad(wkv[:, :, :64], ((0, 0), (0, 0), (0, 64))).reshape(KV_LORA, 2048)
    wv = wkv[:, :, 64:].reshape(KV_LORA, 1024)
    tab_m = _rope_tables(T, "mla")
    tab_s = _rope_tables(T, "swa")
    g1, gq, gkv, sinks = small["g_mix_pre"], small["g_q_a"], small["g_kv_a"], small["sinks"]
    g2, g3, g4, g5 = small["g_mix_post"], small["g_mlp_pre"], small["g_mlp_post"], small["g_ple"]
    sink_vec = sinks.reshape(SWA_HEADS)

    z, h1 = _fwd_in(x, g1, w_in_p, tm)
    qn, kvn, qm, km, vm, qt, kt, vt, qs, ks, vs = _fwd_qkv(z, gq, gkv, wqb, wkn, wv, tab_m, tab_s, tb)
    om, lse_m = _mla_fwd(qm, km, vt, tb)
    os_, lse_s = _swa_fwd(sink_vec, qs, ks, vs)
    w = {**w, **late_weights(os_)}
    y, yo, au, bu, x1 = _fwd_mix(om, os_, z, x, w["w_mla_up"], w["w_swa_up"], w["w_out"], g2, tm)
    h2, a, u = _fwd_mlp_up(x1, g3, w["w_mlp_up"], tm)
    d, x2 = _fwd_mlp_down(u, w["w_mlp_down"], x1, g4, tm)
    loss, dx2, dgt, de0, dg5 = _ple_fwd_bwd(p, x2, tgt, w["w_ple"], g5, w["w_ple_gate"], tm)

    dd, da, dg4 = _bwd_mlp_down(dx2, d, g4, w["w_mlp_down"], a, tm)
    dx1, dg3 = _bwd_mlp_up(da, w["w_mlp_up"], x1, g3, dx2, tm)
    dyo, dg2, dau, dbu, dga, dgb, dos, delta_m, dom_t = _bwd_mix(dx1, yo, g2, w["w_out"], z, au, bu, w["w_mla_up"],
                                                                w["w_swa_up"], om, tb)
    token = late_grads_out({
        "w_mla_up": _wgrad(om, dau, "wgrad_mla_up"),
        "w_swa_up": _wgrad(os_, dbu, "wgrad_swa_up"),
        "w_out": _wgrad(y, dyo, "wgrad_out"),
        "w_ple_gate": _wgrad(x2, dgt, "wgrad_ple_gate"),
        "w_mlp_up": _wgrad(h2, da, "wgrad_mlp_up"),
        "w_mlp_down": _wgrad(u, dd, "wgrad_mlp_down"),
    })
    delta_m = delta_m + token[0, 0]
    dqm, dkm, dvm = _mla_bwd(qm, qt, km, kt, vm, dom_t, lse_m, delta_m, tb)
    dqs, dkc, dkp, dvc, dvp, dsink = _swa_bwd(sink_vec, qs, ks, vs, dos, os_, lse_s)
    dqb, dknb, dvb, dsq, drest, dgq, dgkv = _bwd_qkv(dqm, dkm, dvm, dqs, dkc, dkp, dvc, dvp, z, gq, gkv, wqb, wkn, wv,
                                                      tab_m, tab_s)
    gx, dg1 = _bwd_in(dsq, dga, dgb, drest, w_in_p, x, g1, dx1, tm)

    g_in_p = jnp.concatenate([_wgrad(h1, dsq, "wgrad_in_sq"), _wgrad(h1, dga, "wgrad_in_ga"), _wgrad(h1, dgb, "wgrad_in_gb"),
                              _wgrad(h1, drest, "wgrad_in_rest")], axis=1)
    g_qb_p = _wgrad(qn, dqb, "wgrad_q_b")
    g_kn_p = _wgrad(kvn, dknb, "wgrad_kv_b_nope")
    g_v_p = _wgrad(kvn, dvb, "wgrad_kv_b_v")
    grads = {
        "w_in_p": g_in_p,
        "w_q_b": g_qb_p.reshape(Q_LORA, MLA_HEADS, 128)[:, :, :96].reshape(Q_LORA, 1536),
        "w_kv_b": jnp.concatenate([g_kn_p.reshape(KV_LORA, MLA_HEADS, 128)[:, :, :64], g_v_p.reshape(KV_LORA, MLA_HEADS, 64)],
                                  axis=2).reshape(KV_LORA, 2048),
        "w_ple": _wgrad(p, de0, "wgrad_ple"),
    }
    small_grads = {"g_mix_pre": dg1, "g_q_a": dgq, "g_kv_a": dgkv, "sinks": dsink[0:1, 0:SWA_HEADS], "g_mix_post": dg2,
                   "g_mlp_pre": dg3, "g_mlp_post": dg4, "g_ple": dg5}
    return loss, gx, grads, small_grads


def _pack_small(vals, fill):
    wide = [vals[n] for n, k in SMALL if k == D]
    narrow = [vals[n] for n, k in SMALL if k != D]
    used = sum(k for _, k in SMALL if k != D)
    last = jnp.concatenate(narrow + [jnp.full((1, D - used), fill, F32)], axis=1)
    return jnp.concatenate(wide + [last, jnp.full((2, D), fill, F32)], axis=0)


def _unpack_small(pk):
    out, row, off = {}, 0, 0
    for n, k in SMALL:
        if k == D:
            out[n] = pk[row:row + 1]
            row += 1
    for n, k in SMALL:
        if k != D:
            out[n] = pk[5:6, off:off + k]
            off += k
    return out


def kernel(x, p, g_mix_pre, w_in, g_q_a, w_q_b, g_kv_a, w_kv_b, sinks, w_mla_up, w_swa_up, w_out, g_mix_post, g_mlp_pre, w_mlp_up, w_mlp_down, g_mlp_post, w_ple, g_ple, w_ple_gate, loss_target, m_g_mix_pre, m_w_in, m_g_q_a, m_w_q_b, m_g_kv_a, m_w_kv_b, m_sinks, m_w_mla_up, m_w_swa_up, m_w_out, m_g_mix_post, m_g_mlp_pre, m_w_mlp_up, m_w_mlp_down, m_g_mlp_post, m_w_ple, m_g_ple, m_w_ple_gate, v_g_mix_pre, v_w_in, v_g_q_a, v_w_q_b, v_g_kv_a, v_w_kv_b, v_sinks, v_w_mla_up, v_w_swa_up, v_w_out, v_g_mix_post, v_g_mlp_pre, v_w_mlp_up, v_w_mlp_down, v_g_mlp_post, v_w_ple, v_g_ple, v_w_ple_gate):
    given = dict(locals())
    big_w = {n: given[n][0] for n, _, _ in BIG}
    small_w = {n: given[n] for n, _ in SMALL}
    small_m = {n: given["m_" + n] for n, _ in SMALL}
    small_v = {n: given["v_" + n] for n, _ in SMALL}

    core = lax.axis_index("c")
    chip = 2 * lax.axis_index("x") + lax.axis_index("y")
    core_i = core.astype(jnp.int32).reshape(1)
    chip_i = chip.astype(jnp.int32).reshape(1)
    dev_i = jnp.stack([2 * chip + core, chip, core]).astype(jnp.int32)

    own_early = _pack_early(big_w, BF16)
    own_late = _pack_late(big_w, BF16)
    got_early = _all_gather(own_early)
    late_flight = _gather_late_start(own_late, got_early)
    weights = _full_weights(got_early, own_early, chip, "early")
    step_small = {**small_w, "g_mix_pre": small_w["g_mix_pre"] + late_flight[4][0, 0]}

    def late_weights(after):
        return _full_weights(_gather_late_wait(*late_flight[:4], after), own_late, chip, "late")

    flight = {}

    def late_grads_out(grads):
        gpk_late = _split_full_grads(grads, _pack_late, BF16)
        flight["late"] = _reduce_late_start(gpk_late, dev_i)
        return flight["late"][4]

    loss_blk, gx, grads, small_grads = _local_step(x[0], p[0, 0], loss_target[0], weights, step_small, late_weights,
                                                   late_grads_out)

    gpk = _split_full_grads(grads, _pack_early, F32)
    got = _rs_sibling(gpk)
    part = _rs_add_sibling(core_i, gpk, got)
    parts, small_parts = _rs_chips(part, _pack_small(small_grads, 0.0))
    mine_early = _rs_add_chips(chip_i, part, parts)
    gpk_late, parts_late = _reduce_late_wait(*flight["late"][:4], mine_early)
    mine_late = _reduce_late_add(dev_i, gpk_late, parts_late)
    theirs_early, theirs_late = _rs_join(mine_early, mine_late)
    joined = {"early": jnp.where(core == 0, jnp.concatenate([mine_early, theirs_early]), jnp.concatenate([theirs_early, mine_early])),
              "late": jnp.where(core == 0, jnp.concatenate([mine_late, theirs_late]), jnp.concatenate([theirs_late, mine_late]))}

    loss = lax.psum(loss_blk[0, 0], ("x", "y", "c"))
    g_small_pk, d_small_pk, m_small_pk, v_small_pk = _adamw_small(
        _pack_small(small_w, 0.0), small_parts, _pack_small(small_m, 0.0), _pack_small(small_v, 1.0))
    g_small, d_small = _unpack_small(g_small_pk), _unpack_small(d_small_pk)
    m_small, v_small = _unpack_small(m_small_pk), _unpack_small(v_small_pk)

    out_g, out_d, out_m, out_v = dict(g_small), dict(d_small), dict(m_small), dict(v_small)
    for n, _, _ in BIG:
        out_g[n], out_d[n], out_m[n], out_v[n] = _adamw(given[n], joined[PACK_AT[n][0]], given["m_" + n], given["v_" + n], n)
    order = ["g_mix_pre", "w_in", "g_q_a", "w_q_b", "g_kv_a", "w_kv_b", "sinks", "w_mla_up", "w_swa_up", "w_out", "g_mix_post",
             "g_mlp_pre", "w_mlp_up", "w_mlp_down", "g_mlp_post", "w_ple", "g_ple", "w_ple_gate"]
    return (loss, gx[None], *[out_g[n] for n in order], *[out_d[n] for n in order], *[out_m[n] for n in order],
            *[out_v[n] for n in order])
```

```python
import math

import jax
import jax.numpy as jnp
from jax import lax
from jax.experimental import pallas as pl
from jax.experimental.pallas import tpu as pltpu

F32 = jnp.float32
BF16 = jnp.bfloat16
SDS = jax.ShapeDtypeStruct

D = 1024
D_FF = 4096
PLE = 256
Q_LORA = 256
KV_LORA = 128
MLA_HEADS = 16
MLA_NOPE = 64
MLA_ROPE = 32
SWA_HEADS = 16
SWA_HD = 64
WINDOW = 128
ROPE_THETA = 10000.0
EPS = 1e-6
NEG = -1e30
NZ = 4096
MLA_SCALE = (MLA_NOPE + MLA_ROPE) ** -0.5
LOG2_E = math.log2(math.e)
MLA_LOG2_SCALE = MLA_SCALE * LOG2_E
SWA_SCALE = SWA_HD ** -0.5

ADAM_LR = 0.001
ADAM_B1 = 0.9
ADAM_B2 = 0.999
ADAM_EPS = 1e-08
ADAM_WD = 0.01
ADAM_STEP = 10

LANES = 128
ATT_COLS = 128
N_CHIPS = 4
N_DEV = 8
MESH = pl.DeviceIdType.MESH

NT = (((1,), (1,)), ((), ()))
TN = (((0,), (0,)), ((), ()))

BIG = (("w_in", 1024, 936), ("w_q_b", 256, 384), ("w_kv_b", 128, 512), ("w_mla_up", 256, 1024),
       ("w_swa_up", 256, 1024), ("w_out", 256, 1024), ("w_mlp_up", 1024, 1024), ("w_mlp_down", 1024, 1024),
       ("w_ple", 256, 256), ("w_ple_gate", 256, 1024))
COL_SHARDED = ("w_in", "w_q_b", "w_kv_b", "w_mlp_up", "w_ple")
PACK_AT = {"w_in": ("early", 0, 0), "w_q_b": ("early", 1024, 0), "w_ple": ("early", 1024, 384), "w_kv_b": ("early", 1280, 0),
           "w_mla_up": ("late", 0, 0), "w_swa_up": ("late", 256, 0), "w_out": ("late", 512, 0), "w_ple_gate": ("late", 768, 0),
           "w_mlp_up": ("late", 1024, 0), "w_mlp_down": ("late", 2048, 0)}
PACK_ROWS = {"early": 1408, "late": 3072}
SMALL = (("g_mix_pre", 1024), ("g_q_a", 256), ("g_kv_a", 128), ("sinks", 16), ("g_mix_post", 1024),
         ("g_mlp_pre", 1024), ("g_mlp_post", 1024), ("g_ple", 1024))


def _dot(a, b):
    return jnp.dot(a, b, preferred_element_type=F32)


def _dot_nt(a, b):
    return lax.dot_general(a, b, NT, preferred_element_type=F32)


def _dot_tn(a, b):
    return lax.dot_general(a, b, TN, preferred_element_type=F32)


def _pcall(body, *, name, out_shape, grid=(), in_specs=None, out_specs=None, scratch=(), sem=None, vmem_mb=48):
    params = dict(vmem_limit_bytes=vmem_mb << 20)
    if sem is not None:
        params["dimension_semantics"] = sem
    return pl.pallas_call(body, name=name, grid=grid, in_specs=in_specs, out_specs=out_specs, out_shape=out_shape,
                          scratch_shapes=list(scratch), compiler_params=pltpu.CompilerParams(**params))


def _rows(tm, n, col=0):
    return pl.BlockSpec((tm, n), lambda i: (i, col))


def _full(shape):
    return pl.BlockSpec(shape, lambda i: (0,) * len(shape))


def _rms(x, g):
    r = lax.rsqrt(jnp.mean(x * x, axis=-1, keepdims=True) + EPS)
    return x * r * g


def _rms_bwd(dy, x, g):
    r = lax.rsqrt(jnp.mean(x * x, axis=-1, keepdims=True) + EPS)
    xn = x * r
    dn = dy * g
    dx = r * (dn - xn * jnp.mean(dn * xn, axis=-1, keepdims=True))
    return dx, jnp.sum(dy * xn, axis=0, keepdims=True)


def _sigmoid(x):
    return 1.0 / (1.0 + jnp.exp(-x))


def _rope(x, c, a, b, half):
    return x * c + pltpu.roll(x, LANES - half, 1) * a + pltpu.roll(x, half, 1) * b


def _rope_tables(T, kind):
    lane = jnp.arange(LANES)
    if kind == "mla":
        half = MLA_ROPE // 2
        rel = lane - MLA_NOPE
        on = (rel >= 0) & (rel < MLA_ROPE)
        d = MLA_ROPE
    else:
        half = SWA_HD // 2
        rel = lane % SWA_HD
        on = jnp.ones((LANES,), bool)
        d = SWA_HD
    first = on & (rel < half)
    second = on & (rel >= half)
    f = jnp.where(first, rel, rel - half).astype(F32)
    inv = jnp.exp(-math.log(ROPE_THETA) * f * (2.0 / d))
    ang = jnp.arange(T, dtype=F32)[:, None] * inv[None, :]
    cos, sin = jnp.cos(ang), jnp.sin(ang)
    c = jnp.where(on[None], cos, 1.0)
    a = jnp.where(first[None], -sin, 0.0)
    b = jnp.where(second[None], sin, 0.0)
    return c, a, b


def _fwd_in(x, g1, w_in_p, tm):
    T = x.shape[0]

    def body(x_ref, g_ref, w_ref, z_ref, h_ref):
        h = _rms(x_ref[...], g_ref[...]).astype(BF16)
        h_ref[...] = h
        z_ref[...] = _dot(h, w_ref[...])

    return _pcall(body, name="fwd_in", grid=(T // tm,),
                  in_specs=[_rows(tm, D), _full((1, D)), _full((D, NZ))],
                  out_specs=[_rows(tm, NZ), _rows(tm, D)],
                  out_shape=[SDS((T, NZ), F32), SDS((T, D), BF16)], sem=("parallel",))(x, g1, w_in_p)


def _fwd_qkv(z, gq, gkv, wqb, wkn, wv, tab_m, tab_s, tm):
    T = z.shape[0]

    def body(qa_ref, sq_ref, skd_ref, svd_ref, kva_ref, kr_ref, gq_ref, gkv_ref, wqb_ref, wkn_ref, wv_ref,
             cm_ref, am_ref, bm_ref, cs_ref, as_ref, bs_ref,
             qn_ref, kvn_ref, qm_ref, km_ref, vm_ref, qt_ref, kt_ref, vt_ref, qs_ref, ks_ref, vs_ref):
        qn = _rms(qa_ref[...], gq_ref[...]).astype(BF16)
        qn_ref[...] = qn
        kvn = _rms(kva_ref[...], gkv_ref[...]).astype(BF16)
        kvn_ref[...] = kvn
        cm, am, bm = cm_ref[...], am_ref[...], bm_ref[...]
        cs, as_, bs = cs_ref[...], as_ref[...], bs_ref[...]
        k_rope = _rope(kr_ref[...], cm, am, bm, MLA_ROPE // 2)
        vt_row = lax.broadcasted_iota(jnp.int32, (LANES, tm), 0)
        v_all = _dot(kvn, wv_ref[...])
        q_all = _dot(qn, wqb_ref[...])
        k_all = _dot(kvn, wkn_ref[...])
        for j in range(D // LANES):
            sl = slice(LANES * j, LANES * (j + 1))
            v = v_all[:, sl]
            vm_ref[:, sl] = v.astype(BF16)
            v_t = v.T
            for hh, rows64 in enumerate((v_t, pltpu.roll(v_t, 64, 0))):
                blk = jnp.where(vt_row < 64, rows64, jnp.where(vt_row == 64, 1.0, 0.0))
                vt_ref[0, LANES * (2 * j + hh):LANES * (2 * j + hh + 1), :] = blk.astype(BF16)
        for h in range(MLA_HEADS):
            sl = slice(LANES * h, LANES * (h + 1))
            qh = _rope(q_all[:, sl], cm, am, bm, MLA_ROPE // 2)
            qm_ref[:, sl] = qh.astype(BF16)
            qt_ref[0, sl, :] = qh.T.astype(BF16)
            k = k_all[:, sl] + k_rope
            km_ref[:, sl] = k.astype(BF16)
            kt_ref[0, sl, :] = k.T.astype(BF16)
        for j in range(D // LANES):
            sl = slice(LANES * j, LANES * (j + 1))
            qs_ref[:, sl] = _rope(sq_ref[:, sl], cs, as_, bs, SWA_HD // 2).astype(BF16)
        for j in range(2):
            sl = slice(LANES * j, LANES * (j + 1))
            ks_ref[:, sl] = _rope(skd_ref[:, sl], cs, as_, bs, SWA_HD // 2).astype(BF16)
        vs_ref[...] = svd_ref[...].astype(BF16)

    tab = [_rows(tm, LANES)] * 6
    return _pcall(body, name="fwd_qkv", grid=(T // tm,),
                  in_specs=[_rows(tm, 256, 12), _rows(tm, 1024, 0), _rows(tm, 256, 13), _rows(tm, 256, 14),
                            _rows(tm, 128, 30), _rows(tm, 128, 31), _full((1, Q_LORA)), _full((1, KV_LORA)),
                            _full((Q_LORA, 2048)), _full((KV_LORA, 2048)), _full((KV_LORA, 1024))] + tab,
                  out_specs=[_rows(tm, Q_LORA), _rows(tm, KV_LORA), _rows(tm, 2048), _rows(tm, 2048), _rows(tm, 1024),
                             pl.BlockSpec((1, 2048, tm), lambda i: (i, 0, 0)), pl.BlockSpec((1, 2048, tm), lambda i: (i, 0, 0)),
                             pl.BlockSpec((1, 2048, tm), lambda i: (i, 0, 0)),
                             _rows(tm, 1024), _rows(tm, 256), _rows(tm, 256)],
                  out_shape=[SDS((T, Q_LORA), BF16), SDS((T, KV_LORA), BF16), SDS((T, 2048), BF16), SDS((T, 2048), BF16),
                             SDS((T, 1024), BF16), SDS((T // tm, 2048, tm), BF16), SDS((T // tm, 2048, tm), BF16),
                             SDS((T // tm, 2048, tm), BF16),
                             SDS((T, 1024), BF16), SDS((T, 256), BF16), SDS((T, 256), BF16)],
                  sem=("parallel",))(z, z, z, z, z, z, gq, gkv, wqb, wkn, wv, *tab_m, *tab_s)


def _mla_fwd(qm, km, vt, tb):
    T = qm.shape[0]
    nb = T // tb
    cc = ATT_COLS

    def body(q_ref, k_ref, vt_ref, o_ref, l_ref, s_ref, p_ref, al_ref, m_ref, acc_ref):
        i = pl.program_id(1)
        m_ref[...] = jnp.full(m_ref.shape, NEG, F32)
        acc_ref[...] = jnp.zeros_like(acc_ref)
        p_ref[1] = jnp.zeros(p_ref.shape[1:], BF16)
        al_ref[1] = jnp.ones(al_ref.shape[1:], F32)
        key = lax.broadcasted_iota(jnp.int32, (tb, cc), 0)
        qry = lax.broadcasted_iota(jnp.int32, (tb, cc), 1)

        def scores(j, slot):
            off = pl.multiple_of(j * tb, tb)
            for hh in range(2):
                sl = slice(LANES * hh, LANES * (hh + 1))
                s_ref[slot, hh] = _dot_nt(k_ref[pl.ds(off, tb), sl], q_ref[:, sl])

        def softmax(slot, diagonal):
            chains = [(hh, slice(cc * c, cc * (c + 1)), c) for hh in range(2) for c in range(tb // cc)]

            def scaled(hh, cols, c):
                t = s_ref[slot, hh, :, cols] * MLA_LOG2_SCALE
                return jnp.where(key <= qry + cc * c, t, NEG) if diagonal else t

            tops = []
            for hh, cols, c in chains:
                if diagonal:
                    top = jnp.max(scaled(hh, cols, c), axis=0, keepdims=True)
                else:
                    top = jnp.max(s_ref[slot, hh, :, cols], axis=0, keepdims=True) * MLA_LOG2_SCALE
                m_old = m_ref[hh, :, cols]
                mn = jnp.maximum(m_old, top)
                m_ref[hh, :, cols] = mn
                al_ref[slot, hh, :, cols] = jnp.exp2(m_old - mn)
                tops.append(mn)
            for (hh, cols, c), mn in zip(chains, tops):
                p_ref[slot, hh, :, cols] = jnp.exp2(scaled(hh, cols, c) - mn).astype(BF16)

        def accumulate(j, slot):
            for hh in range(2):
                acc_ref[hh] = al_ref[slot, hh] * acc_ref[hh] + _dot(vt_ref[j, LANES * hh:LANES * (hh + 1), :], p_ref[slot, hh])

        def step(t, carry):
            scores(2 * t + 1, 1)
            accumulate(jnp.maximum(2 * t - 1, 0), 1)
            softmax(0, False)
            scores(2 * t + 2, 0)
            accumulate(2 * t, 0)
            softmax(1, False)
            return carry

        scores(0, 0)
        lax.fori_loop(0, i // 2, step, 0)

        @pl.when(i % 2 == 1)
        def _():
            scores(i, 1)
            accumulate(jnp.maximum(i - 2, 0), 1)
            softmax(0, False)
            accumulate(i - 1, 0)
            softmax(1, True)
            accumulate(i, 1)

        @pl.when(i % 2 == 0)
        def _():
            accumulate(jnp.maximum(i - 1, 0), 1)
            softmax(0, True)
            accumulate(i, 0)
        den = [acc_ref[hh, 64:65, :] for hh in range(2)]
        o_ref[...] = jnp.concatenate([acc_ref[hh, 0:64, :] / den[hh] for hh in range(2)], axis=0).T
        sub = lax.broadcasted_iota(jnp.int32, (8, tb), 0)
        lse = [m_ref[hh] + jnp.log(den[hh]) * LOG2_E for hh in range(2)]
        l_ref[0, 0] = jnp.where(sub == 0, lse[0], jnp.where(sub == 1, lse[1], 0.0))

    return _pcall(body, name="mla_fwd", grid=(MLA_HEADS // 2, nb),
                  in_specs=[pl.BlockSpec((tb, 256), lambda p, i: (i, p)), pl.BlockSpec((T, 256), lambda p, i: (0, p)),
                            pl.BlockSpec((nb, 2 * LANES, tb), lambda p, i: (0, p, 0))],
                  out_specs=[pl.BlockSpec((tb, LANES), lambda p, i: (i, p)),
                             pl.BlockSpec((1, 1, 8, tb), lambda p, i: (p, i, 0, 0))],
                  out_shape=[SDS((T, D), F32), SDS((MLA_HEADS // 2, nb, 8, tb), F32)],
                  scratch=[pltpu.VMEM((2, 2, tb, tb), F32), pltpu.VMEM((2, 2, tb, tb), BF16), pltpu.VMEM((2, 2, 1, tb), F32),
                           pltpu.VMEM((2, 1, tb), F32), pltpu.VMEM((2, LANES, tb), F32)],
                  sem=("parallel", "arbitrary"))(qm, km, vt)


def _swa_mask(n):
    row = lax.broadcasted_iota(jnp.int32, (WINDOW, 2 * WINDOW), 0)
    col = lax.broadcasted_iota(jnp.int32, (WINDOW, 2 * WINDOW), 1)
    rel = row - col + WINDOW
    return (rel >= 0) & (rel < WINDOW) & ((col >= WINDOW) | (n > 0))


def _swa_specs(T):
    nb = T // WINDOW
    cur = lambda w: pl.BlockSpec((WINDOW, w), lambda n: (n, 0))
    prev = lambda w: pl.BlockSpec((WINDOW, w), lambda n: (jnp.maximum(n - 1, 0), 0))
    return nb, cur, prev


def _swa_fwd(sinks, qs, ks, vs):
    T = qs.shape[0]
    nb, cur, prev = _swa_specs(T)

    def body(sink_ref, q_ref, kc_ref, kp_ref, vc_ref, vp_ref, o_ref, l_ref):
        n = pl.program_id(0)
        mask = _swa_mask(n)
        lo = lax.broadcasted_iota(jnp.int32, (WINDOW, LANES), 1) < 64
        for g in range(2):
            gs = slice(LANES * g, LANES * (g + 1))
            kb = jnp.concatenate([kp_ref[:, gs], kc_ref[:, gs]], axis=0)
            vb = jnp.concatenate([vp_ref[:, gs], vc_ref[:, gs]], axis=0)
            for jj in range(4):
                j = 4 * g + jj
                sl = slice(LANES * j, LANES * (j + 1))
                qp = q_ref[:, sl]
                outs, lses = [], []
                for hf in range(2):
                    hm = lo if hf == 0 else jnp.logical_not(lo)
                    qh = jnp.where(hm, qp, jnp.zeros_like(qp))
                    s = jnp.where(mask, _dot_nt(qh, kb) * SWA_SCALE, NEG)
                    sk = sink_ref[2 * j + hf]
                    m = jnp.maximum(jnp.max(s, axis=1, keepdims=True), sk)
                    e = jnp.exp(s - m)
                    den = jnp.sum(e, axis=1, keepdims=True) + jnp.exp(sk - m)
                    p = e / den
                    outs.append(_dot(p.astype(BF16), vb))
                    lses.append(jnp.broadcast_to(m + jnp.log(den), (WINDOW, LANES)))
                o_ref[:, sl] = jnp.where(lo, outs[0], outs[1])
                l_ref[:, sl] = jnp.where(lo, lses[0], lses[1])

    return _pcall(body, name="swa_fwd", grid=(nb,),
                  in_specs=[pl.BlockSpec(memory_space=pltpu.SMEM), cur(D), cur(256), prev(256), cur(256), prev(256)],
                  out_specs=[cur(D), cur(D)], out_shape=[SDS((T, D), F32)] * 2,
                  sem=("parallel",))(sinks, qs, ks, ks, vs, vs)


def _fwd_mix(om, os_, z, x, wmu, wsu, wo, g2, tm):
    T = x.shape[0]

    def body(om_ref, os_ref, ga_ref, gb_ref, x_ref, wmu_ref, wsu_ref, wo_ref, g2_ref,
             y_ref, yo_ref, au_ref, bu_ref, x1_ref):
        au = _dot(om_ref[...].astype(BF16), wmu_ref[...])
        bu = _dot(os_ref[...].astype(BF16), wsu_ref[...])
        au_ref[...] = au
        bu_ref[...] = bu
        y = (_sigmoid(ga_ref[...]) * au + _sigmoid(gb_ref[...]) * bu).astype(BF16)
        y_ref[...] = y
        yo = _dot(y, wo_ref[...])
        yo_ref[...] = yo
        x1_ref[...] = x_ref[...] + _rms(yo, g2_ref[...])

    r = _rows(tm, D)
    w = _full((D, D))
    return _pcall(body, name="fwd_mix", grid=(T // tm,),
                  in_specs=[r, r, _rows(tm, D, 1), _rows(tm, D, 2), r, w, w, w, _full((1, D))],
                  out_specs=[r] * 5,
                  out_shape=[SDS((T, D), BF16), SDS((T, D), F32), SDS((T, D), F32), SDS((T, D), F32), SDS((T, D), F32)],
                  sem=("parallel",))(om, os_, z, z, x, wmu, wsu, wo, g2)


def _fwd_mlp_up(x1, g3, w1, tm):
    T = x1.shape[0]

    def body(x_ref, g_ref, w_ref, h_ref, a_ref, u_ref):
        h = _rms(x_ref[...], g_ref[...]).astype(BF16)
        h_ref[...] = h
        a = _dot(h, w_ref[...])
        a_ref[...] = a
        u_ref[...] = jnp.square(jnp.maximum(a, 0.0)).astype(BF16)

    return _pcall(body, name="fwd_mlp_up", grid=(T // tm,),
                  in_specs=[_rows(tm, D), _full((1, D)), _full((D, D_FF))],
                  out_specs=[_rows(tm, D), _rows(tm, D_FF), _rows(tm, D_FF)],
                  out_shape=[SDS((T, D), BF16), SDS((T, D_FF), F32), SDS((T, D_FF), BF16)],
                  sem=("parallel",))(x1, g3, w1)


def _fwd_mlp_down(u, w2, x1, g4, tm):
    T = x1.shape[0]

    def body(u_ref, w_ref, x_ref, g_ref, d_ref, x2_ref):
        d = _dot(u_ref[...], w_ref[...])
        d_ref[...] = d
        x2_ref[...] = x_ref[...] + _rms(d, g_ref[...])

    return _pcall(body, name="fwd_mlp_down", grid=(T // tm,),
                  in_specs=[_rows(tm, D_FF), _full((D_FF, D)), _rows(tm, D), _full((1, D))],
                  out_specs=[_rows(tm, D), _rows(tm, D)], out_shape=[SDS((T, D), F32)] * 2,
                  sem=("parallel",))(u, w2, x1, g4)


def _ple_fwd_bwd(p, x2, tgt, wple, g5, wpg, tm):
    T = x2.shape[0]

    def body(p_ref, x2_ref, t_ref, wple_ref, g5_ref, wpg_ref, loss_ref, dx2_ref, dgt_ref, de0_ref, dg5_ref):
        @pl.when(pl.program_id(0) == 0)
        def _():
            loss_ref[...] = jnp.zeros_like(loss_ref)
            dg5_ref[...] = jnp.zeros_like(dg5_ref)

        e0 = _dot(p_ref[...].astype(BF16), wple_ref[...])
        g5 = g5_ref[...]
        r = lax.rsqrt(jnp.mean(e0 * e0, axis=-1, keepdims=True) + EPS)
        en = e0 * r
        e = en * g5
        x2 = x2_ref[...]
        s = _sigmoid(_dot(x2.astype(BF16), wpg_ref[...]))
        diff = x2 + s * e - t_ref[...]
        sq = jnp.sum(jnp.sum(diff * diff, axis=1, keepdims=True), axis=0, keepdims=True)
        loss_ref[...] += jnp.broadcast_to(sq * (0.5 / D), loss_ref.shape)
        dx3 = diff * (1.0 / D)
        de = dx3 * s
        dgt = (dx3 * e * s * (1.0 - s)).astype(BF16)
        dgt_ref[...] = dgt
        dn = de * g5
        de0_ref[...] = (r * (dn - en * jnp.mean(dn * en, axis=-1, keepdims=True))).astype(BF16)
        dg5_ref[...] += jnp.sum(de * en, axis=0, keepdims=True)
        dx2_ref[...] = dx3 + _dot_nt(dgt, wpg_ref[...])

    r = _rows(tm, D)
    return _pcall(body, name="ple_fwd_bwd", grid=(T // tm,),
                  in_specs=[_rows(tm, PLE), r, r, _full((PLE, D)), _full((1, D)), _full((D, D))],
                  out_specs=[_full((8, LANES)), r, r, r, _full((1, D))],
                  out_shape=[SDS((8, LANES), F32), SDS((T, D), F32), SDS((T, D), BF16), SDS((T, D), BF16), SDS((1, D), F32)],
                  sem=("arbitrary",))(p, x2, tgt, wple, g5, wpg)


def _bwd_mlp_down(dx2, d, g4, w2, a, tm):
    T = dx2.shape[0]

    def body(dx_ref, d_ref, g_ref, w_ref, a_ref, dd_ref, da_ref, dg_ref):
        @pl.when(pl.program_id(0) == 0)
        def _():
            dg_ref[...] = jnp.zeros_like(dg_ref)

        dd, dg = _rms_bwd(dx_ref[...], d_ref[...], g_ref[...])
        dg_ref[...] += dg
        ddb = dd.astype(BF16)
        dd_ref[...] = ddb
        du = _dot_nt(ddb, w_ref[...])
        da_ref[...] = (du * (2.0 * jnp.maximum(a_ref[...], 0.0))).astype(BF16)

    return _pcall(body, name="bwd_mlp_down", grid=(T // tm,),
                  in_specs=[_rows(tm, D), _rows(tm, D), _full((1, D)), _full((D_FF, D)), _rows(tm, D_FF)],
                  out_specs=[_rows(tm, D), _rows(tm, D_FF), _full((1, D))],
                  out_shape=[SDS((T, D), BF16), SDS((T, D_FF), BF16), SDS((1, D), F32)],
                  sem=("arbitrary",))(dx2, d, g4, w2, a)


def _bwd_mlp_up(da, w1, x1, g3, dx2, tm):
    T = dx2.shape[0]

    def body(da_ref, w_ref, x_ref, g_ref, dx2_ref, dx1_ref, dg_ref):
        @pl.when(pl.program_id(0) == 0)
        def _():
            dg_ref[...] = jnp.zeros_like(dg_ref)

        dh = _dot_nt(da_ref[...], w_ref[...])
        dx, dg = _rms_bwd(dh, x_ref[...], g_ref[...])
        dg_ref[...] += dg
        dx1_ref[...] = dx2_ref[...] + dx

    return _pcall(body, name="bwd_mlp_up", grid=(T // tm,),
                  in_specs=[_rows(tm, D_FF), _full((D, D_FF)), _rows(tm, D), _full((1, D)), _rows(tm, D)],
                  out_specs=[_rows(tm, D), _full((1, D))],
                  out_shape=[SDS((T, D), F32), SDS((1, D), F32)], sem=("arbitrary",))(da, w1, x1, g3, dx2)


def _bwd_mix(dx1, yo, g2, wo, z, au, bu, wmu, wsu, om, tm):
    T = dx1.shape[0]

    def body(dx_ref, yo_ref, g_ref, wo_ref, ga_ref, gb_ref, au_ref, bu_ref, wmu_ref, wsu_ref, om_ref,
             dyo_ref, dg_ref, dau_ref, dbu_ref, dga_ref, dgb_ref, dos_ref, dl_ref, dot_ref):
        @pl.when(pl.program_id(0) == 0)
        def _():
            dg_ref[...] = jnp.zeros_like(dg_ref)

        dyo, dg = _rms_bwd(dx_ref[...], yo_ref[...], g_ref[...])
        dg_ref[...] += dg
        dyob = dyo.astype(BF16)
        dyo_ref[...] = dyob
        dy = _dot_nt(dyob, wo_ref[...])
        sa = _sigmoid(ga_ref[...])
        sb = _sigmoid(gb_ref[...])
        dau = (dy * sa).astype(BF16)
        dbu = (dy * sb).astype(BF16)
        dau_ref[...] = dau
        dbu_ref[...] = dbu
        dga_ref[...] = (dy * au_ref[...] * sa * (1.0 - sa)).astype(BF16)
        dgb_ref[...] = (dy * bu_ref[...] * sb * (1.0 - sb)).astype(BF16)
        dom = _dot_nt(dau, wmu_ref[...])
        dos_ref[...] = _dot_nt(dbu, wsu_ref[...])
        prod = dom * om_ref[...]
        sub = lax.broadcasted_iota(jnp.int32, (8, tm), 0)
        for pr in range(MLA_HEADS // 2):
            sl = slice(LANES * pr, LANES * (pr + 1))
            pt = prod[:, sl].T
            d0 = jnp.sum(pt[0:64], axis=0, keepdims=True)
            d1 = jnp.sum(pt[64:128], axis=0, keepdims=True)
            dl_ref[pr, 0] = jnp.where(sub == 0, d0, jnp.where(sub == 1, d1, 0.0))
            dot_ref[0, sl, :] = dom[:, sl].T.astype(BF16)

    r = _rows(tm, D)
    w = _full((D, D))
    return _pcall(body, name="bwd_mix", grid=(T // tm,),
                  in_specs=[r, r, _full((1, D)), w, _rows(tm, D, 1), _rows(tm, D, 2), r, r, w, w, r],
                  out_specs=[r, _full((1, D)), r, r, r, r, r, pl.BlockSpec((MLA_HEADS // 2, 1, 8, tm), lambda i: (0, i, 0, 0)),
                             pl.BlockSpec((1, D, tm), lambda i: (i, 0, 0))],
                  out_shape=[SDS((T, D), BF16), SDS((1, D), F32), SDS((T, D), BF16), SDS((T, D), BF16), SDS((T, D), BF16),
                             SDS((T, D), BF16), SDS((T, D), F32), SDS((MLA_HEADS // 2, T // tm, 8, tm), F32),
                             SDS((T // tm, D, tm), BF16)],
                  sem=("arbitrary",))(dx1, yo, g2, wo, z, z, au, bu, wmu, wsu, om)


def _mla_bwd(qm, qt, km, kt, vm, dot, lse, delta, tb):
    T = qm.shape[0]
    nb = T // tb
    cc = ATT_COLS

    def body(q_ref, qt_ref, k_ref, kt_ref, v_ref, dot_ref, l_ref, dl_ref, dqt_ref, dkt_ref, dvt_ref,
             s_ref, dp_ref, p_ref, ds_ref, vh_ref):
        j = pl.program_id(1)

        @pl.when(j == 0)
        def _():
            dqt_ref[...] = jnp.zeros_like(dqt_ref)

        dkt_ref[...] = jnp.zeros_like(dkt_ref)
        dvt_ref[...] = jnp.zeros_like(dvt_ref)
        lo = lax.broadcasted_iota(jnp.int32, (tb, LANES), 1) < 64
        key = lax.broadcasted_iota(jnp.int32, (tb, cc), 0)
        qry = lax.broadcasted_iota(jnp.int32, (tb, cc), 1)
        v = v_ref[...]
        vh_ref[0] = jnp.where(lo, v, jnp.zeros_like(v))
        vh_ref[1] = jnp.where(lo, jnp.zeros_like(v), v)

        def scores(i, slot):
            rows_i = pl.ds(pl.multiple_of(i * tb, tb), tb)
            for hh in range(2):
                sl = slice(LANES * hh, LANES * (hh + 1))
                s_ref[slot, hh] = _dot_nt(k_ref[:, sl], q_ref[rows_i, sl])
                dp_ref[slot, hh] = _dot(vh_ref[hh], dot_ref[i])

        def grads(i, slot, diagonal):
            lse_i = l_ref[0, i]
            delta_i = dl_ref[0, i]
            for hh in range(2):
                for c in range(tb // cc):
                    cols = slice(cc * c, cc * (c + 1))
                    p = jnp.exp2(s_ref[slot, hh, :, cols] * MLA_LOG2_SCALE - lse_i[hh:hh + 1, cols])
                    if diagonal:
                        p = jnp.where(key <= qry + cc * c, p, 0.0)
                    p_ref[hh, :, cols] = p.astype(BF16)
                    ds_ref[hh, :, cols] = (p * (dp_ref[slot, hh, :, cols] - delta_i[hh:hh + 1, cols]) * MLA_SCALE).astype(BF16)
            for hh in range(2):
                sl = slice(LANES * hh, LANES * (hh + 1))
                half = slice(64 * hh, 64 * (hh + 1))
                dvt_ref[0, half, :] += _dot_nt(dot_ref[i, half, :], p_ref[hh])
                dkt_ref[0, sl, :] += _dot_nt(qt_ref[i, sl, :], ds_ref[hh])
                dqt_ref[i, sl, :] += _dot(kt_ref[0, sl, :], ds_ref[hh])

        n_off = nb - 1 - j

        def step(u, carry):
            i0 = j + 1 + 2 * u
            scores(i0 + 1, 1)
            grads(i0, 0, False)
            scores(jnp.where(i0 + 2 < nb, i0 + 2, j), 0)
            grads(i0 + 1, 1, False)
            return carry

        scores(jnp.where(n_off > 0, j + 1, j), 0)
        lax.fori_loop(0, n_off // 2, step, 0)

        @pl.when(n_off % 2 == 1)
        def _():
            scores(j, 1)
            grads(nb - 1, 0, False)
            grads(j, 1, True)

        @pl.when(n_off % 2 == 0)
        def _():
            grads(j, 0, True)

    blk = lambda w: pl.BlockSpec((tb, w), lambda p, j: (j, p))
    stat = pl.BlockSpec((1, nb, 8, tb), lambda p, j: (p, 0, 0, 0))
    pair_t = lambda w: pl.BlockSpec((nb, w, tb), lambda p, j: (0, p, 0))
    blk_t = lambda w: pl.BlockSpec((1, w, tb), lambda p, j: (j, p, 0))
    return _pcall(body, name="mla_bwd", grid=(MLA_HEADS // 2, nb),
                  in_specs=[pl.BlockSpec((T, 256), lambda p, j: (0, p)), pair_t(256), blk(256), blk_t(256), blk(LANES),
                            pair_t(LANES), stat, stat],
                  out_specs=[pair_t(256), blk_t(256), blk_t(LANES)],
                  out_shape=[SDS((nb, 2048, tb), F32), SDS((nb, 2048, tb), F32), SDS((nb, D, tb), F32)],
                  scratch=[pltpu.VMEM((2, 2, tb, tb), F32), pltpu.VMEM((2, 2, tb, tb), F32), pltpu.VMEM((2, tb, tb), BF16),
                           pltpu.VMEM((2, tb, tb), BF16), pltpu.VMEM((2, tb, LANES), BF16)],
                  sem=("parallel", "arbitrary"))(qm, qt, km, kt, vm, dot, lse, delta)


def _swa_bwd(sinks, qs, ks, vs, do, o, lse):
    T = qs.shape[0]
    nb, cur, prev = _swa_specs(T)

    def body(sink_ref, q_ref, kc_ref, kp_ref, vc_ref, vp_ref, do_ref, o_ref, l_ref,
             dq_ref, dkc_ref, dkp_ref, dvc_ref, dvp_ref, dsink_ref):
        n = pl.program_id(0)

        @pl.when(n == 0)
        def _():
            dsink_ref[...] = jnp.zeros_like(dsink_ref)

        mask = _swa_mask(n)
        lo = lax.broadcasted_iota(jnp.int32, (WINDOW, LANES), 1) < 64
        lane8 = lax.broadcasted_iota(jnp.int32, (8, LANES), 1)
        dsink = jnp.zeros((8, LANES), F32)
        for g in range(2):
            gs = slice(LANES * g, LANES * (g + 1))
            kb = jnp.concatenate([kp_ref[:, gs], kc_ref[:, gs]], axis=0)
            vb = jnp.concatenate([vp_ref[:, gs], vc_ref[:, gs]], axis=0)
            dkb = jnp.zeros((2 * WINDOW, LANES), F32)
            dvb = jnp.zeros((2 * WINDOW, LANES), F32)
            for jj in range(4):
                j = 4 * g + jj
                sl = slice(LANES * j, LANES * (j + 1))
                qp = q_ref[:, sl]
                d_o = do_ref[:, sl]
                prod = d_o * o_ref[:, sl]
                lse_b = l_ref[:, sl]
                dqs = []
                for hf in range(2):
                    hm = lo if hf == 0 else jnp.logical_not(lo)
                    qh = jnp.where(hm, qp, jnp.zeros_like(qp))
                    s = jnp.where(mask, _dot_nt(qh, kb) * SWA_SCALE, NEG)
                    lse_h = jnp.max(jnp.where(hm, lse_b, -jnp.inf), axis=1, keepdims=True)
                    p = jnp.exp(s - lse_h)
                    dom = jnp.where(hm, d_o, 0.0).astype(BF16)
                    dp = _dot_nt(dom, vb)
                    delta = jnp.sum(jnp.where(hm, prod, 0.0), axis=1, keepdims=True)
                    ds = (p * (dp - delta) * SWA_SCALE).astype(BF16)
                    p_sink = jnp.exp(sink_ref[2 * j + hf] - lse_h)
                    d_sink = -jnp.sum(p_sink * delta, axis=0, keepdims=True)
                    dsink = dsink + jnp.where(lane8 == 2 * j + hf, d_sink, 0.0)
                    dvb = dvb + _dot_tn(p.astype(BF16), dom)
                    dkb = dkb + _dot_tn(ds, qh)
                    dqs.append(_dot(ds, kb))
                dq_ref[:, sl] = jnp.where(lo, dqs[0], dqs[1])
            dkp_ref[:, gs] = dkb[:WINDOW]
            dkc_ref[:, gs] = dkb[WINDOW:]
            dvp_ref[:, gs] = dvb[:WINDOW]
            dvc_ref[:, gs] = dvb[WINDOW:]
        dsink_ref[...] += dsink

    return _pcall(body, name="swa_bwd", grid=(nb,),
                  in_specs=[pl.BlockSpec(memory_space=pltpu.SMEM), cur(D), cur(256), prev(256), cur(256), prev(256),
                            cur(D), cur(D), cur(D)],
                  out_specs=[cur(D), cur(256), cur(256), cur(256), cur(256), _full((8, LANES))],
                  out_shape=[SDS((T, D), F32), SDS((T, 256), F32), SDS((T, 256), F32), SDS((T, 256), F32), SDS((T, 256), F32),
                             SDS((8, LANES), F32)],
                  sem=("arbitrary",))(sinks, qs, ks, ks, vs, vs, do, o, lse)


def _bwd_qkv(dqm, dkm, dvm, dqs, dkc, dkp, dvc, dvp, z, gq, gkv, wqb, wkn, wv, tab_m, tab_s):
    T = z.shape[0]
    tm = WINDOW
    nb = T // tm
    per = dqm.shape[2] // tm

    def body(dqm_ref, dkm_ref, dvm_ref, dqs_ref, dkc_ref, dkp_ref, dvc_ref, dvp_ref, qa_ref, kva_ref, gq_ref, gkv_ref,
             wqb_ref, wkn_ref, wv_ref, cm_ref, am_ref, bm_ref, cs_ref, as_ref, bs_ref,
             dq_out, dkn_out, dv_out, dsq_ref, drest_ref, dgq_ref, dgkv_ref):
        i = pl.program_id(0)

        @pl.when(i == 0)
        def _():
            dgq_ref[...] = jnp.zeros_like(dgq_ref)
            dgkv_ref[...] = jnp.zeros_like(dgkv_ref)

        cm, am, bm = cm_ref[...], -am_ref[...], -bm_ref[...]
        cs, as_, bs = cs_ref[...], -as_ref[...], -bs_ref[...]
        lane = lax.broadcasted_iota(jnp.int32, (tm, LANES), 1)
        nope = lane < MLA_NOPE
        roped = jnp.logical_and(lane >= MLA_NOPE, lane < MLA_NOPE + MLA_ROPE)
        dkr = jnp.zeros((tm, LANES), F32)
        for h in range(MLA_HEADS):
            sl = slice(LANES * h, LANES * (h + 1))
            dq_out[:, sl] = _rope(dqm_ref[0, sl, :].T, cm, am, bm, MLA_ROPE // 2).astype(BF16)
            dk_h = dkm_ref[0, sl, :].T
            dkn_out[:, sl] = jnp.where(nope, dk_h, 0.0).astype(BF16)
            dkr = dkr + jnp.where(roped, dk_h, 0.0)
        for j in range(D // LANES):
            sl = slice(LANES * j, LANES * (j + 1))
            dv_out[:, sl] = dvm_ref[0, sl, :].T.astype(BF16)
        dqn = _dot_nt(dq_out[...], wqb_ref[...])
        dkvn = _dot_nt(dkn_out[...], wkn_ref[...]) + _dot_nt(dv_out[...], wv_ref[...])
        dqa, dgq = _rms_bwd(dqn, qa_ref[...], gq_ref[...])
        dkva, dgkv = _rms_bwd(dkvn, kva_ref[...], gkv_ref[...])
        dgq_ref[...] += dgq
        dgkv_ref[...] += dgkv
        for j in range(D // LANES):
            sl = slice(LANES * j, LANES * (j + 1))
            dsq_ref[:, sl] = _rope(dqs_ref[:, sl], cs, as_, bs, SWA_HD // 2).astype(BF16)
        keep = (i < nb - 1).astype(F32)
        drest_ref[:, 0:256] = dqa.astype(BF16)
        for j in range(2):
            sl = slice(LANES * j, LANES * (j + 1))
            dk = dkc_ref[:, sl] + keep * dkp_ref[:, sl]
            drest_ref[:, 256 + LANES * j:256 + LANES * (j + 1)] = _rope(dk, cs, as_, bs, SWA_HD // 2).astype(BF16)
        drest_ref[:, 512:768] = (dvc_ref[...] + keep * dvp_ref[...]).astype(BF16)
        drest_ref[:, 768:896] = dkva.astype(BF16)
        drest_ref[:, 896:1024] = _rope(dkr, cm, am, bm, MLA_ROPE // 2).astype(BF16)

    nxt = pl.BlockSpec((tm, 256), lambda i: (jnp.minimum(i + 1, nb - 1), 0))
    tab = [_rows(tm, LANES)] * 6
    return _pcall(body, name="bwd_qkv", grid=(nb,),
                  in_specs=[pl.BlockSpec((1, 2048, tm), lambda i: (i // per, 0, i % per)),
                            pl.BlockSpec((1, 2048, tm), lambda i: (i // per, 0, i % per)),
                            pl.BlockSpec((1, 1024, tm), lambda i: (i // per, 0, i % per)), _rows(tm, 1024), _rows(tm, 256), nxt,
                            _rows(tm, 256), nxt, _rows(tm, 256, 12), _rows(tm, 128, 30), _full((1, Q_LORA)), _full((1, KV_LORA)),
                            _full((Q_LORA, 2048)), _full((KV_LORA, 2048)), _full((KV_LORA, 1024))] + tab,
                  out_specs=[_rows(tm, 2048), _rows(tm, 2048), _rows(tm, 1024), _rows(tm, 1024), _rows(tm, 1024),
                             _full((1, Q_LORA)), _full((1, KV_LORA))],
                  out_shape=[SDS((T, 2048), BF16), SDS((T, 2048), BF16), SDS((T, 1024), BF16), SDS((T, 1024), BF16),
                             SDS((T, 1024), BF16), SDS((1, Q_LORA), F32), SDS((1, KV_LORA), F32)],
                  sem=("arbitrary",))(dqm, dkm, dvm, dqs, dkc, dkp, dvc, dvp, z, z, gq, gkv, wqb, wkn, wv, *tab_m, *tab_s)


def _bwd_in(dsq, dga, dgb, drest, w_in_p, x, g1, dx1, tm):
    T = x.shape[0]

    def body(a_ref, b_ref, c_ref, d_ref, w_ref, x_ref, g_ref, dx1_ref, dx_ref, dg_ref):
        @pl.when(pl.program_id(0) == 0)
        def _():
            dg_ref[...] = jnp.zeros_like(dg_ref)

        dh = (_dot_nt(a_ref[...], w_ref[:, 0:1024]) + _dot_nt(b_ref[...], w_ref[:, 1024:2048])
              + _dot_nt(c_ref[...], w_ref[:, 2048:3072]) + _dot_nt(d_ref[...], w_ref[:, 3072:4096]))
        dx, dg = _rms_bwd(dh, x_ref[...], g_ref[...])
        dg_ref[...] += dg
        dx_ref[...] = dx1_ref[...] + dx

    r = _rows(tm, D)
    return _pcall(body, name="bwd_in", grid=(T // tm,),
                  in_specs=[r, r, r, r, _full((D, NZ)), r, _full((1, D)), r],
                  out_specs=[r, _full((1, D))], out_shape=[SDS((T, D), F32), SDS((1, D), F32)],
                  sem=("arbitrary",))(dsq, dga, dgb, drest, w_in_p, x, g1, dx1)


def _wgrad(a, g, name, out_dtype=F32, by_column_block=False):
    T, K = a.shape
    N = g.shape[1]
    tk, tn, tt = min(K, 1024), min(N, 1024), min(T, 1024)
    assert K % tk == 0 and N % tn == 0 and T % tt == 0, (a.shape, g.shape)
    steps = T // tt

    def body(a_ref, g_ref, o_ref, acc_ref):
        t = pl.program_id(2)

        @pl.when(t == 0)
        def _():
            acc_ref[...] = jnp.zeros_like(acc_ref)

        acc_ref[...] += _dot_tn(a_ref[...].astype(BF16), g_ref[...].astype(BF16))

        @pl.when(t == steps - 1)
        def _():
            o_ref[...] = acc_ref[...].astype(out_dtype).reshape(o_ref.shape)

    if by_column_block:
        out_spec, out_shape = pl.BlockSpec((1, tk, tn), lambda k, n, t: (n, k, 0)), SDS((N // tn, K, tn), out_dtype)
    else:
        out_spec, out_shape = pl.BlockSpec((tk, tn), lambda k, n, t: (k, n)), SDS((K, N), out_dtype)
    return _pcall(body, name=name, grid=(K // tk, N // tn, steps),
                  in_specs=[pl.BlockSpec((tt, tk), lambda k, n, t: (t, k)), pl.BlockSpec((tt, tn), lambda k, n, t: (t, n))],
                  out_specs=out_spec, out_shape=out_shape, scratch=[pltpu.VMEM((tk, tn), F32)],
                  sem=("parallel", "parallel", "arbitrary"))(a, g)


def _adamw(w, packed_g, m, v, name):
    _, R, C = w.shape
    _, row0, lane0 = PACK_AT[name]
    tr = min(R, 256 if row0 % 256 == 0 else 128)
    assert row0 % tr == 0 and R % tr == 0

    def body(w_ref, g_ref, m_ref, v_ref, go_ref, d_ref, m2_ref, v2_ref):
        g_ = g_ref[:, lane0:lane0 + C]
        go_ref[0] = g_
        m2 = ADAM_B1 * m_ref[0] + (1.0 - ADAM_B1) * g_
        v2 = ADAM_B2 * v_ref[0] + (1.0 - ADAM_B2) * jnp.square(g_)
        m_hat = m2 / (1.0 - ADAM_B1 ** ADAM_STEP)
        v_hat = v2 / (1.0 - ADAM_B2 ** ADAM_STEP)
        d_ref[0] = -ADAM_LR * (m_hat / (jnp.sqrt(v_hat) + ADAM_EPS) + ADAM_WD * w_ref[0])
        m2_ref[0] = m2
        v2_ref[0] = v2

    r = pl.BlockSpec((1, tr, C), lambda i: (0, i, 0))
    return _pcall(body, name="adamw_" + name, grid=(R // tr,),
                  in_specs=[r, pl.BlockSpec((tr, D), lambda i: (row0 // tr + i, 0)), r, r], out_specs=[r] * 4,
                  out_shape=[SDS((1, R, C), F32)] * 4, sem=("parallel",))(w, packed_g, m, v)


def _adamw_small(w, parts, m, v):
    def body(w_ref, p_ref, m_ref, v_ref, g_ref, d_ref, m2_ref, v2_ref):
        g_ = p_ref[0]
        for k in range(1, N_DEV):
            g_ = g_ + p_ref[k]
        g_ref[...] = g_
        m2 = ADAM_B1 * m_ref[...] + (1.0 - ADAM_B1) * g_
        v2 = ADAM_B2 * v_ref[...] + (1.0 - ADAM_B2) * jnp.square(g_)
        m_hat = m2 / (1.0 - ADAM_B1 ** ADAM_STEP)
        v_hat = v2 / (1.0 - ADAM_B2 ** ADAM_STEP)
        d_ref[...] = -ADAM_LR * (m_hat / (jnp.sqrt(v_hat) + ADAM_EPS) + ADAM_WD * w_ref[...])
        m2_ref[...] = m2
        v2_ref[...] = v2

    s = _full((8, D))
    return _pcall(body, name="adamw_small", grid=(1,), in_specs=[s, _full((N_DEV, 8, D)), s, s], out_specs=[s] * 4,
                  out_shape=[SDS((8, D), F32)] * 4, sem=("arbitrary",))(w, parts, m, v)


ANY = pl.BlockSpec(memory_space=pl.ANY)


def _place():
    x, y, c = lax.axis_index("x"), lax.axis_index("y"), lax.axis_index("c")
    chips = [(1 - x, y), (x, 1 - y), (1 - x, 1 - y)]
    return x, y, c, chips


def _all_gather(wpk):
    rows = wpk.shape[0]
    HALF = rows // 2
    assert HALF % 16 == 0

    def body(in_ref, out_ref, send_sems, recv_sems):
        x, y, c, chips = _place()
        half = pl.ds(pl.multiple_of(c * HALF, 16), HALF)
        other = pl.ds(pl.multiple_of((1 - c) * HALF, 16), HALF)

        def copy(k, src, dst, to):
            return pltpu.make_async_remote_copy(src_ref=src, dst_ref=dst, send_sem=send_sems.at[k], recv_sem=recv_sems.at[k],
                                                device_id=to, device_id_type=MESH)

        first = [copy(k, in_ref.at[half], out_ref.at[2 * x + y, half], (cx, cy, c)) for k, (cx, cy) in enumerate(chips)]
        for cp in first:
            cp.start()
        passed = []
        for k, (cx, cy) in enumerate(chips):
            slot = out_ref.at[2 * cx + cy, half]
            copy(k, slot, slot, (x, y, c)).wait_recv()
            fwd = copy(3 + k, slot, slot, (x, y, 1 - c))
            fwd.start()
            passed.append(fwd)
        for k, (cx, cy) in enumerate(chips):
            slot = out_ref.at[2 * cx + cy, other]
            copy(3 + k, slot, slot, (x, y, c)).wait_recv()
        for cp in first + passed:
            cp.wait_send()

    return _pcall(body, name="all_gather_weights", in_specs=[ANY], out_specs=ANY,
                  out_shape=SDS((N_CHIPS, rows, D), BF16),
                  scratch=[pltpu.SemaphoreType.DMA((6,)), pltpu.SemaphoreType.DMA((6,))])(wpk)


HBM = pl.BlockSpec(memory_space=pltpu.HBM)
SEM = pl.BlockSpec(memory_space=pltpu.SEMAPHORE)
DATAFLOW = pltpu.SideEffectType.DATAFLOW_SIDE_EFFECTING


def _in_hbm(a):
    return pltpu.with_memory_space_constraint(a, pltpu.HBM)


def _gather_late_start(wpk, after):
    rows = wpk.shape[0]

    def body(in_ref, land_ref, after_ref, send_sems, recv_sems, in_thru, land_thru, token):
        x, y, c, chips = _place()
        for k, (cx, cy) in enumerate(chips):
            pltpu.make_async_remote_copy(src_ref=in_ref, dst_ref=land_ref.at[2 * x + y], send_sem=send_sems.at[k],
                                         recv_sem=recv_sems.at[k], device_id=(cx, cy, c), device_id_type=MESH).start()
        token[...] = jnp.zeros_like(token)

    return pl.pallas_call(
        body, name="gather_late_start",
        out_shape=(pltpu.SemaphoreType.DMA((3,)), pltpu.SemaphoreType.DMA((3,)), pltpu.HBM(wpk.shape, wpk.dtype),
                   pltpu.HBM((N_CHIPS, rows, D), wpk.dtype), SDS((8, LANES), F32)),
        in_specs=(HBM, HBM, ANY), out_specs=(SEM, SEM, HBM, HBM, pl.BlockSpec(memory_space=pltpu.VMEM)),
        input_output_aliases={0: 2, 1: 3}, compiler_params=pltpu.CompilerParams(has_side_effects=DATAFLOW),
    )(_in_hbm(wpk), _in_hbm(lax.empty((N_CHIPS, rows, D), wpk.dtype)), after)


def _gather_late_wait(send_sems, recv_sems, in_thru, land_thru, after):
    def body(in_ref, land_ref, send_sems, recv_sems, after_ref, in_dead, got_ref):
        x, y, c, chips = _place()
        for k, (cx, cy) in enumerate(chips):
            cp = pltpu.make_async_remote_copy(src_ref=in_ref, dst_ref=land_ref.at[2 * cx + cy], send_sem=send_sems.at[k],
                                              recv_sem=recv_sems.at[k], device_id=(cx, cy, c), device_id_type=MESH)
            cp.wait_send()
            cp.wait_recv()

    return pl.pallas_call(
        body, name="gather_late_wait",
        out_shape=(pltpu.HBM(in_thru.shape, in_thru.dtype), pltpu.HBM(land_thru.shape, land_thru.dtype)),
        in_specs=(HBM, HBM, SEM, SEM, ANY), out_specs=(HBM, HBM), input_output_aliases={0: 0, 1: 1},
        compiler_params=pltpu.CompilerParams(has_side_effects=DATAFLOW),
    )(in_thru, land_thru, send_sems, recv_sems, after)[1]


def _rs_sibling(gpk):
    HALF = gpk.shape[1] // 2

    def body(in_ref, out_ref, send_sem, recv_sem):
        x, y, c, _ = _place()
        theirs = pl.ds(pl.multiple_of((1 - c) * HALF, 8), HALF)
        cp = pltpu.make_async_remote_copy(src_ref=in_ref.at[:, theirs], dst_ref=out_ref, send_sem=send_sem, recv_sem=recv_sem,
                                          device_id=(x, y, 1 - c), device_id_type=MESH)
        cp.start()
        cp.wait()

    return _pcall(body, name="rs_sibling", in_specs=[ANY], out_specs=ANY, out_shape=SDS((N_CHIPS, HALF, D), F32),
                  scratch=[pltpu.SemaphoreType.DMA, pltpu.SemaphoreType.DMA])(gpk)


def _rs_add_sibling(cidx, gpk, got):
    HALF = got.shape[1]
    th = HALF // 4
    nh = HALF // th
    assert th % 16 == 0

    def body(c_ref, a_ref, b_ref, o_ref):
        o_ref[...] = (a_ref[...] + b_ref[...]).astype(BF16)

    gs = pltpu.PrefetchScalarGridSpec(
        num_scalar_prefetch=1, grid=(N_CHIPS, nh),
        in_specs=[pl.BlockSpec((1, th, D), lambda j, i, c: (j, c[0] * nh + i, 0)), pl.BlockSpec((1, th, D), lambda j, i, c: (j, i, 0))],
        out_specs=pl.BlockSpec((1, th, D), lambda j, i, c: (j, i, 0)))
    return pl.pallas_call(body, name="rs_add_sibling", grid_spec=gs, out_shape=SDS((N_CHIPS, HALF, D), BF16),
                          compiler_params=pltpu.CompilerParams(dimension_semantics=("parallel", "parallel"),
                                                               vmem_limit_bytes=48 << 20))(cidx, gpk, got)


def _rs_chips(part, small):
    def body(p_ref, s_ref, o_ref, so_ref, send_sems, recv_sems, ssend_sems, srecv_sems, local_sem):
        x, y, c, chips = _place()
        me = 2 * x + y
        mine_s = pltpu.make_async_copy(s_ref, so_ref.at[4 * x + 2 * y + c], local_sem)
        mine_s.start()
        sends = []
        for k, (cx, cy) in enumerate(chips):
            sends.append(pltpu.make_async_remote_copy(src_ref=p_ref.at[2 * cx + cy], dst_ref=o_ref.at[me], send_sem=send_sems.at[k],
                                                      recv_sem=recv_sems.at[k], device_id=(cx, cy, c), device_id_type=MESH))
        peers = [(x, y, 1 - c)] + [(cx, cy, c) for cx, cy in chips] + [(cx, cy, 1 - c) for cx, cy in chips]
        for k, to in enumerate(peers):
            sends.append(pltpu.make_async_remote_copy(src_ref=s_ref, dst_ref=so_ref.at[4 * x + 2 * y + c], send_sem=ssend_sems.at[k],
                                                      recv_sem=srecv_sems.at[k], device_id=to, device_id_type=MESH))
        for cp in sends:
            cp.start()
        for k, (cx, cy) in enumerate(chips):
            slot = o_ref.at[2 * cx + cy]
            pltpu.make_async_remote_copy(src_ref=slot, dst_ref=slot, send_sem=send_sems.at[k], recv_sem=recv_sems.at[k],
                                         device_id=(x, y, c), device_id_type=MESH).wait_recv()
        for k, (px, py, pc) in enumerate(peers):
            slot = so_ref.at[4 * px + 2 * py + pc]
            pltpu.make_async_remote_copy(src_ref=slot, dst_ref=slot, send_sem=ssend_sems.at[k], recv_sem=srecv_sems.at[k],
                                         device_id=(x, y, c), device_id_type=MESH).wait_recv()
        for cp in sends:
            cp.wait_send()
        mine_s.wait()

    return _pcall(body, name="rs_chips", in_specs=[ANY, ANY], out_specs=[ANY, ANY],
                  out_shape=[SDS(part.shape, part.dtype), SDS((N_DEV, 8, D), F32)],
                  scratch=[pltpu.SemaphoreType.DMA((3,)), pltpu.SemaphoreType.DMA((3,)), pltpu.SemaphoreType.DMA((7,)),
                           pltpu.SemaphoreType.DMA((7,)), pltpu.SemaphoreType.DMA])(part, small)


def _rs_add_chips(qidx, part, parts):
    HALF = part.shape[1]
    th = HALF // 4
    assert th % 16 == 0

    def body(q_ref, own_ref, p_ref, o_ref):
        for me in range(N_CHIPS):
            @pl.when(q_ref[0] == me)
            def _(me=me):
                t = [(own_ref[0] if j == me else p_ref[j]).astype(F32) for j in range(N_CHIPS)]
                o_ref[...] = ((t[0] + t[1]) + t[2]) + t[3]

    gs = pltpu.PrefetchScalarGridSpec(
        num_scalar_prefetch=1, grid=(HALF // th,),
        in_specs=[pl.BlockSpec((1, th, D), lambda i, q: (q[0], i, 0)), pl.BlockSpec((N_CHIPS, th, D), lambda i, q: (0, i, 0))],
        out_specs=pl.BlockSpec((th, D), lambda i, q: (i, 0)))
    return pl.pallas_call(body, name="rs_add_chips", grid_spec=gs, out_shape=SDS((HALF, D), F32),
                          compiler_params=pltpu.CompilerParams(dimension_semantics=("parallel",),
                                                               vmem_limit_bytes=48 << 20))(qidx, part, parts)


def _rs_join(early, late):
    def body(e_ref, l_ref, eo_ref, lo_ref, send_sems, recv_sems):
        x, y, c, _ = _place()
        cps = [pltpu.make_async_remote_copy(src_ref=src, dst_ref=dst, send_sem=send_sems.at[k], recv_sem=recv_sems.at[k],
                                            device_id=(x, y, 1 - c), device_id_type=MESH)
               for k, (src, dst) in enumerate([(e_ref, eo_ref), (l_ref, lo_ref)])]
        for cp in cps:
            cp.start()
        for cp in cps:
            cp.wait()

    return _pcall(body, name="rs_join", in_specs=[ANY, ANY], out_specs=[ANY, ANY],
                  out_shape=[SDS(early.shape, F32), SDS(late.shape, F32)],
                  scratch=[pltpu.SemaphoreType.DMA((2,)), pltpu.SemaphoreType.DMA((2,))])(early, late)


def _reduce_late_start(gpk, after):
    rows = gpk.shape[1]
    HALF = rows // 2
    assert HALF % 16 == 0

    def body(in_ref, land_ref, after_ref, send_sems, recv_sems, in_thru, land_thru, token):
        x, y, c, chips = _place()
        me = 4 * x + 2 * y + c
        peers = [(x, y, 1 - c)] + [(cx, cy, c) for cx, cy in chips] + [(cx, cy, 1 - c) for cx, cy in chips]
        for k, (px, py, pc) in enumerate(peers):
            src = in_ref.at[2 * px + py, pl.ds(pl.multiple_of(pc * HALF, 16), HALF)]
            pltpu.make_async_remote_copy(src_ref=src, dst_ref=land_ref.at[me], send_sem=send_sems.at[k], recv_sem=recv_sems.at[k],
                                         device_id=(px, py, pc), device_id_type=MESH).start()
        token[...] = jnp.zeros_like(token)

    return pl.pallas_call(
        body, name="reduce_late_start",
        out_shape=(pltpu.SemaphoreType.DMA((7,)), pltpu.SemaphoreType.DMA((7,)), pltpu.HBM(gpk.shape, gpk.dtype),
                   pltpu.HBM((N_DEV, HALF, D), gpk.dtype), SDS((8, LANES), F32)),
        in_specs=(HBM, HBM, ANY), out_specs=(SEM, SEM, HBM, HBM, pl.BlockSpec(memory_space=pltpu.VMEM)),
        input_output_aliases={0: 2, 1: 3}, compiler_params=pltpu.CompilerParams(has_side_effects=DATAFLOW),
    )(_in_hbm(gpk), _in_hbm(lax.empty((N_DEV, HALF, D), gpk.dtype)), after)


def _reduce_late_wait(send_sems, recv_sems, in_thru, land_thru, after):
    def body(in_ref, land_ref, send_sems, recv_sems, after_ref, in_out, got_ref):
        x, y, c, chips = _place()
        peers = [(x, y, 1 - c)] + [(cx, cy, c) for cx, cy in chips] + [(cx, cy, 1 - c) for cx, cy in chips]
        for k, (px, py, pc) in enumerate(peers):
            cp = pltpu.make_async_remote_copy(src_ref=land_ref.at[0], dst_ref=land_ref.at[4 * px + 2 * py + pc],
                                              send_sem=send_sems.at[k], recv_sem=recv_sems.at[k],
                                              device_id=(px, py, pc), device_id_type=MESH)
            cp.wait_send()
            cp.wait_recv()

    return pl.pallas_call(
        body, name="reduce_late_wait",
        out_shape=(pltpu.HBM(in_thru.shape, in_thru.dtype), pltpu.HBM(land_thru.shape, land_thru.dtype)),
        in_specs=(HBM, HBM, SEM, SEM, ANY), out_specs=(HBM, HBM), input_output_aliases={0: 0, 1: 1},
        compiler_params=pltpu.CompilerParams(has_side_effects=DATAFLOW),
    )(in_thru, land_thru, send_sems, recv_sems, after)


def _reduce_late_add(didx, gpk, parts):
    HALF = parts.shape[1]
    th = HALF // 4
    nh = HALF // th
    assert th % 16 == 0

    def body(d_ref, own_ref, p_ref, o_ref):
        for me in range(N_DEV):
            @pl.when(d_ref[0] == me)
            def _(me=me):
                t = [(own_ref[0] if j == me else p_ref[j]).astype(F32) for j in range(N_DEV)]
                o_ref[...] = ((((((t[0] + t[1]) + t[2]) + t[3]) + t[4]) + t[5]) + t[6]) + t[7]

    gs = pltpu.PrefetchScalarGridSpec(
        num_scalar_prefetch=1, grid=(nh,),
        in_specs=[pl.BlockSpec((1, th, D), lambda i, d: (d[1], d[2] * nh + i, 0)), pl.BlockSpec((N_DEV, th, D), lambda i, d: (0, i, 0))],
        out_specs=pl.BlockSpec((th, D), lambda i, d: (i, 0)))
    return pl.pallas_call(body, name="reduce_late_add", grid_spec=gs, out_shape=SDS((HALF, D), F32),
                          compiler_params=pltpu.CompilerParams(dimension_semantics=("parallel",),
                                                               vmem_limit_bytes=48 << 20))(didx, gpk, parts)


def _pack_early(b, dtype):
    lanes = lambda a: jnp.pad(a.astype(dtype), ((0, 0), (0, D - a.shape[1])))
    pair = jnp.concatenate([b["w_q_b"].astype(dtype), b["w_ple"].astype(dtype), jnp.zeros((256, D - 640), dtype)], axis=1)
    return jnp.concatenate([lanes(b["w_in"]), pair, lanes(b["w_kv_b"])], axis=0)


def _pack_late(b, dtype):
    return jnp.concatenate([b[n].astype(dtype) for n in ("w_mla_up", "w_swa_up", "w_out", "w_ple_gate", "w_mlp_up", "w_mlp_down")],
                           axis=0)


def _unpack_shards(pk, which):
    return {n: pk[PACK_AT[n][1]:PACK_AT[n][1] + r, PACK_AT[n][2]:PACK_AT[n][2] + c] for n, r, c in BIG if PACK_AT[n][0] == which}


def _full_weights(gathered, own, chip, which):
    own_b = _unpack_shards(own, which)
    per_chip = [{n: jnp.where(chip == j, own_b[n], blk) for n, blk in _unpack_shards(gathered[j], which).items()}
                for j in range(N_CHIPS)]
    out = {}
    for n in own_b:
        shards = [pc[n] for pc in per_chip]
        if n == "w_in":
            out["w_in_p"] = _w_in_internal(shards)
        else:
            out[n] = jnp.concatenate(shards, axis=1 if n in COL_SHARDED else 0)
    return out


def _split_full_grads(grads, pack, dtype):
    shard = {n: (r, c) for n, r, c in BIG}
    chunks = []
    for j in range(N_CHIPS):
        blocks = {}
        for n, g in grads.items():
            if n == "w_in_p":
                blocks["w_in"] = _w_in_grad_shard(g, j)
                continue
            r, c = shard[n]
            blocks[n] = g[:, j * c:(j + 1) * c] if n in COL_SHARDED else g[j * r:(j + 1) * r]
        chunks.append(pack(blocks, dtype))
    return jnp.stack(chunks)


W_IN_SHARD = 936
W_IN_SEGMENTS = ((0, 256, (3072,)), (256, 384, (3840,)), (384, 416, (4032,)), (416, 1440, (0,)), (1440, 1504, (3328, 3392)),
                 (1504, 1568, (3456, 3520)), (1568, 1632, (3584, 3648)), (1632, 1696, (3712, 3776)), (1696, 3744, (1024,)))


def _w_in_internal(shards):
    def cols(a, b):
        out = []
        for j, s in enumerate(shards):
            lo, hi = max(a, W_IN_SHARD * j), min(b, W_IN_SHARD * (j + 1))
            if lo < hi:
                out.append(s[:, lo - W_IN_SHARD * j:hi - W_IN_SHARD * j])
        return out

    pieces = {}
    for a, b, places in W_IN_SEGMENTS:
        for at in places:
            pieces[at] = cols(a, b)
    zeros = lambda n: [jnp.zeros((D, n), shards[0].dtype)]
    pieces[3968] = zeros(64)
    pieces[4064] = zeros(32)
    return jnp.concatenate([piece for at in sorted(pieces) for piece in pieces[at]], axis=1)


def _w_in_grad_shard(g, j):
    def internal(a, b):
        out = []
        while a < b:
            end = min(b, (a // D + 1) * D)
            out.append(g[a // D][:, a % D:a % D + end - a])
            a = end
        return out

    out = []
    for a, b, places in W_IN_SEGMENTS:
        lo, hi = max(a, W_IN_SHARD * j), min(b, W_IN_SHARD * (j + 1))
        if lo < hi:
            parts = [internal(at + lo - a, at + hi - a) for at in places]
            if len(parts) == 1:
                out += parts[0]
            else:
                assert len(parts[0]) == len(parts[1]) == 1
                out.append(parts[0][0] + parts[1][0])
    return jnp.concatenate(out, axis=1)


def _local_step(x, p, tgt, w, small, late_weights, late_grads_out):
    T = x.shape[0]
    tm = 256
    tb = 256
    w_in_p = w["w_in_p"]
    wqb = jnp.pad(w["w_q_b"].reshape(Q_LORA, MLA_HEADS, 96), ((0, 0), (0, 0), (0, 32))).reshape(Q_LORA, 2048)
    wkv = w["w_kv_b"].reshape(KV_LORA, MLA_HEADS, 128)
    wkn = jnp.pad(wkv[:, :, :64], ((0, 0), (0, 0), (0, 64))).reshape(KV_LORA, 2048)
    wv = wkv[:, :, 64:].reshape(KV_LORA, 1024)
    tab_m = _rope_tables(T, "mla")
    tab_s = _rope_tables(T, "swa")
    g1, gq, gkv, sinks = small["g_mix_pre"], small["g_q_a"], small["g_kv_a"], small["sinks"]
    g2, g3, g4, g5 = small["g_mix_post"], small["g_mlp_pre"], small["g_mlp_post"], small["g_ple"]
    sink_vec = sinks.reshape(SWA_HEADS)

    z, h1 = _fwd_in(x, g1, w_in_p, tm)
    qn, kvn, qm, km, vm, qt, kt, vt, qs, ks, vs = _fwd_qkv(z, gq, gkv, wqb, wkn, wv, tab_m, tab_s, tb)
    om, lse_m = _mla_fwd(qm, km, vt, tb)
    os_, lse_s = _swa_fwd(sink_vec, qs, ks, vs)
    w = {**w, **late_weights(os_)}
    y, yo, au, bu, x1 = _fwd_mix(om, os_, z, x, w["w_mla_up"], w["w_swa_up"], w["w_out"], g2, tm)
    h2, a, u = _fwd_mlp_up(x1, g3, w["w_mlp_up"], tm)
    d, x2 = _fwd_mlp_down(u, w["w_mlp_down"], x1, g4, tm)
    loss, dx2, dgt, de0, dg5 = _ple_fwd_bwd(p, x2, tgt, w["w_ple"], g5, w["w_ple_gate"], tm)

    dd, da, dg4 = _bwd_mlp_down(dx2, d, g4, w["w_mlp_down"], a, tm)
    dx1, dg3 = _bwd_mlp_up(da, w["w_mlp_up"], x1, g3, dx2, tm)
    dyo, dg2, dau, dbu, dga, dgb, dos, delta_m, dom_t = _bwd_mix(dx1, yo, g2, w["w_out"], z, au, bu, w["w_mla_up"],
                                                                w["w_swa_up"], om, tb)
    token = late_grads_out({
        "w_mla_up": _wgrad(om, dau, "wgrad_mla_up", BF16),
        "w_swa_up": _wgrad(os_, dbu, "wgrad_swa_up", BF16),
        "w_out": _wgrad(y, dyo, "wgrad_out", BF16),
        "w_ple_gate": _wgrad(x2, dgt, "wgrad_ple_gate", BF16),
        "w_mlp_up": _wgrad(h2, da, "wgrad_mlp_up", BF16, by_column_block=True),
        "w_mlp_down": _wgrad(u, dd, "wgrad_mlp_down", BF16),
    })
    delta_m = delta_m + token[0, 0]
    dqm, dkm, dvm = _mla_bwd(qm, qt, km, kt, vm, dom_t, lse_m, delta_m, tb)
    dqs, dkc, dkp, dvc, dvp, dsink = _swa_bwd(sink_vec, qs, ks, vs, dos, os_, lse_s)
    dqb, dknb, dvb, dsq, drest, dgq, dgkv = _bwd_qkv(dqm, dkm, dvm, dqs, dkc, dkp, dvc, dvp, z, gq, gkv, wqb, wkn, wv,
                                                      tab_m, tab_s)
    gx, dg1 = _bwd_in(dsq, dga, dgb, drest, w_in_p, x, g1, dx1, tm)

    g_in_p = [_wgrad(h1, dsq, "wgrad_in_sq"), _wgrad(h1, dga, "wgrad_in_ga"), _wgrad(h1, dgb, "wgrad_in_gb"),
              _wgrad(h1, drest, "wgrad_in_rest")]
    g_qb_p = _wgrad(qn, dqb, "wgrad_q_b")
    g_kn_p = _wgrad(kvn, dknb, "wgrad_kv_b_nope")
    g_v_p = _wgrad(kvn, dvb, "wgrad_kv_b_v")
    grads = {
        "w_in_p": g_in_p,
        "w_q_b": g_qb_p.reshape(Q_LORA, MLA_HEADS, 128)[:, :, :96].reshape(Q_LORA, 1536),
        "w_kv_b": jnp.concatenate([g_kn_p.reshape(KV_LORA, MLA_HEADS, 128)[:, :, :64], g_v_p.reshape(KV_LORA, MLA_HEADS, 64)],
                                  axis=2).reshape(KV_LORA, 2048),
        "w_ple": _wgrad(p, de0, "wgrad_ple"),
    }
    small_grads = {"g_mix_pre": dg1, "g_q_a": dgq, "g_kv_a": dgkv, "sinks": dsink[0:1, 0:SWA_HEADS], "g_mix_post": dg2,
                   "g_mlp_pre": dg3, "g_mlp_post": dg4, "g_ple": dg5}
    return loss, gx, grads, small_grads


def _pack_small(vals, fill, scalar=None):
    wide = [vals[n] for n, k in SMALL if k == D]
    narrow = [vals[n] for n, k in SMALL if k != D]
    used = sum(k for _, k in SMALL if k != D)
    last = jnp.concatenate(narrow + [jnp.full((1, D - used), fill, F32)], axis=1)
    rest = jnp.full((2, D), fill, F32)
    if scalar is not None:
        rest = jnp.concatenate([jnp.concatenate([scalar, rest[0:1, 1:]], axis=1), rest[1:2]], axis=0)
    return jnp.concatenate(wide + [last, rest], axis=0)


def _unpack_small(pk):
    out, row, off = {}, 0, 0
    for n, k in SMALL:
        if k == D:
            out[n] = pk[row:row + 1]
            row += 1
    for n, k in SMALL:
        if k != D:
            out[n] = pk[5:6, off:off + k]
            off += k
    return out


def kernel(x, p, g_mix_pre, w_in, g_q_a, w_q_b, g_kv_a, w_kv_b, sinks, w_mla_up, w_swa_up, w_out, g_mix_post, g_mlp_pre, w_mlp_up, w_mlp_down, g_mlp_post, w_ple, g_ple, w_ple_gate, loss_target, m_g_mix_pre, m_w_in, m_g_q_a, m_w_q_b, m_g_kv_a, m_w_kv_b, m_sinks, m_w_mla_up, m_w_swa_up, m_w_out, m_g_mix_post, m_g_mlp_pre, m_w_mlp_up, m_w_mlp_down, m_g_mlp_post, m_w_ple, m_g_ple, m_w_ple_gate, v_g_mix_pre, v_w_in, v_g_q_a, v_w_q_b, v_g_kv_a, v_w_kv_b, v_sinks, v_w_mla_up, v_w_swa_up, v_w_out, v_g_mix_post, v_g_mlp_pre, v_w_mlp_up, v_w_mlp_down, v_g_mlp_post, v_w_ple, v_g_ple, v_w_ple_gate):
    given = dict(locals())
    big_w = {n: given[n][0] for n, _, _ in BIG}
    small_w = {n: given[n] for n, _ in SMALL}
    small_m = {n: given["m_" + n] for n, _ in SMALL}
    small_v = {n: given["v_" + n] for n, _ in SMALL}

    core = lax.axis_index("c")
    chip = 2 * lax.axis_index("x") + lax.axis_index("y")
    core_i = core.astype(jnp.int32).reshape(1)
    chip_i = chip.astype(jnp.int32).reshape(1)
    dev_i = jnp.stack([2 * chip + core, chip, core]).astype(jnp.int32)

    own_early = _pack_early(big_w, BF16)
    own_late = _pack_late(big_w, BF16)
    got_early = _all_gather(own_early)
    late_flight = _gather_late_start(own_late, got_early)
    weights = _full_weights(got_early, own_early, chip, "early")
    step_small = {**small_w, "g_mix_pre": small_w["g_mix_pre"] + late_flight[4][0, 0]}

    def late_weights(after):
        return _full_weights(_gather_late_wait(*late_flight[:4], after), own_late, chip, "late")

    flight = {}

    def late_grads_out(grads):
        order = sorted((n for n in grads), key=lambda n: PACK_AT[n][1])
        gpk_late = jnp.concatenate([grads[n] if grads[n].ndim == 3 else grads[n].reshape(N_CHIPS, -1, D) for n in order], axis=1)
        flight["late"] = _reduce_late_start(gpk_late, dev_i)
        return flight["late"][4]

    loss_blk, gx, grads, small_grads = _local_step(x[0], p[0, 0], loss_target[0], weights, step_small, late_weights,
                                                   late_grads_out)

    gpk = _split_full_grads(grads, _pack_early, F32)
    got = _rs_sibling(gpk)
    part = _rs_add_sibling(core_i, gpk, got)
    parts, small_parts = _rs_chips(part, _pack_small(small_grads, 0.0, loss_blk[0:1, 0:1]))
    mine_early = _rs_add_chips(chip_i, part, parts)
    gpk_late, parts_late = _reduce_late_wait(*flight["late"][:4], mine_early)
    mine_late = _reduce_late_add(dev_i, gpk_late, parts_late)
    theirs_early, theirs_late = _rs_join(mine_early, mine_late)
    joined = {"early": jnp.where(core == 0, jnp.concatenate([mine_early, theirs_early]), jnp.concatenate([theirs_early, mine_early])),
              "late": jnp.where(core == 0, jnp.concatenate([mine_late, theirs_late]), jnp.concatenate([theirs_late, mine_late]))}

    g_small_pk, d_small_pk, m_small_pk, v_small_pk = _adamw_small(
        _pack_small(small_w, 0.0), small_parts, _pack_small(small_m, 0.0), _pack_small(small_v, 1.0))
    loss = g_small_pk[6, 0]
    g_small, d_small = _unpack_small(g_small_pk), _unpack_small(d_small_pk)
    m_small, v_small = _unpack_small(m_small_pk), _unpack_small(v_small_pk)

    out_g, out_d, out_m, out_v = dict(g_small), dict(d_small), dict(m_small), dict(v_small)
    for n, _, _ in BIG:
        out_g[n], out_d[n], out_m[n], out_v[n] = _adamw(given[n], joined[PACK_AT[n][0]], given["m_" + n], given["v_" + n], n)
    order = ["g_mix_pre", "w_in", "g_q_a", "w_q_b", "g_kv_a", "w_kv_b", "sinks", "w_mla_up", "w_swa_up", "w_out", "g_mix_post",
             "g_mlp_pre", "w_mlp_up", "w_mlp_down", "g_mlp_post", "w_ple", "g_ple", "w_ple_gate"]
    return (loss, gx[None], *[out_g[n] for n in order], *[out_d[n] for n in order], *[out_m[n] for n in order],
            *[out_v[n] for n in order])
```

```python
import math

import jax
import jax.numpy as jnp
from jax import lax
from jax.experimental import pallas as pl
from jax.experimental.pallas import tpu as pltpu

F32 = jnp.float32
BF16 = jnp.bfloat16
SDS = jax.ShapeDtypeStruct

D = 1024
D_FF = 4096
PLE = 256
Q_LORA = 256
KV_LORA = 128
MLA_HEADS = 16
MLA_NOPE = 64
MLA_ROPE = 32
SWA_HEADS = 16
SWA_HD = 64
WINDOW = 128
ROPE_THETA = 10000.0
EPS = 1e-6
NEG = -1e30
NZ = 4096
MLA_SCALE = (MLA_NOPE + MLA_ROPE) ** -0.5
LOG2_E = math.log2(math.e)
MLA_LOG2_SCALE = MLA_SCALE * LOG2_E
SWA_SCALE = SWA_HD ** -0.5

ADAM_LR = 0.001
ADAM_B1 = 0.9
ADAM_B2 = 0.999
ADAM_EPS = 1e-08
ADAM_WD = 0.01
ADAM_STEP = 10

LANES = 128
ATT_COLS = 128
N_CHIPS = 4
N_DEV = 8
MESH = pl.DeviceIdType.MESH

NT = (((1,), (1,)), ((), ()))
TN = (((0,), (0,)), ((), ()))

BIG = (("w_in", 1024, 936), ("w_q_b", 256, 384), ("w_kv_b", 128, 512), ("w_mla_up", 256, 1024),
       ("w_swa_up", 256, 1024), ("w_out", 256, 1024), ("w_mlp_up", 1024, 1024), ("w_mlp_down", 1024, 1024),
       ("w_ple", 256, 256), ("w_ple_gate", 256, 1024))
COL_SHARDED = ("w_in", "w_q_b", "w_kv_b", "w_mlp_up", "w_ple")
PACK_AT = {"w_in": ("early", 0, 0), "w_q_b": ("early", 1024, 0), "w_ple": ("early", 1024, 384), "w_kv_b": ("early", 1280, 0),
           "w_mla_up": ("late", 0, 0), "w_swa_up": ("late", 256, 0), "w_out": ("late", 512, 0), "w_ple_gate": ("late", 768, 0),
           "w_mlp_up": ("late", 1024, 0), "w_mlp_down": ("late", 2048, 0)}
PACK_ROWS = {"early": 1408, "late": 3072}
SMALL = (("g_mix_pre", 1024), ("g_q_a", 256), ("g_kv_a", 128), ("sinks", 16), ("g_mix_post", 1024),
         ("g_mlp_pre", 1024), ("g_mlp_post", 1024), ("g_ple", 1024))


def _dot(a, b):
    return jnp.dot(a, b, preferred_element_type=F32)


def _dot_nt(a, b):
    return lax.dot_general(a, b, NT, preferred_element_type=F32)


def _dot_tn(a, b):
    return lax.dot_general(a, b, TN, preferred_element_type=F32)


def _pcall(body, *, name, out_shape, grid=(), in_specs=None, out_specs=None, scratch=(), sem=None, vmem_mb=48):
    params = dict(vmem_limit_bytes=vmem_mb << 20)
    if sem is not None:
        params["dimension_semantics"] = sem
    return pl.pallas_call(body, name=name, grid=grid, in_specs=in_specs, out_specs=out_specs, out_shape=out_shape,
                          scratch_shapes=list(scratch), compiler_params=pltpu.CompilerParams(**params))


def _rows(tm, n, col=0):
    return pl.BlockSpec((tm, n), lambda i: (i, col))


def _full(shape):
    return pl.BlockSpec(shape, lambda i: (0,) * len(shape))


def _rms(x, g):
    r = lax.rsqrt(jnp.mean(x * x, axis=-1, keepdims=True) + EPS)
    return x * r * g


def _rms_bwd(dy, x, g):
    r = lax.rsqrt(jnp.mean(x * x, axis=-1, keepdims=True) + EPS)
    xn = x * r
    dn = dy * g
    dx = r * (dn - xn * jnp.mean(dn * xn, axis=-1, keepdims=True))
    return dx, jnp.sum(dy * xn, axis=0, keepdims=True)


def _sigmoid(x):
    return 1.0 / (1.0 + jnp.exp(-x))


def _rope(x, c, a, b, half):
    return x * c + pltpu.roll(x, LANES - half, 1) * a + pltpu.roll(x, half, 1) * b


def _rope_tables(T, kind):
    lane = jnp.arange(LANES)
    if kind == "mla":
        half = MLA_ROPE // 2
        rel = lane - MLA_NOPE
        on = (rel >= 0) & (rel < MLA_ROPE)
        d = MLA_ROPE
    else:
        half = SWA_HD // 2
        rel = lane % SWA_HD
        on = jnp.ones((LANES,), bool)
        d = SWA_HD
    first = on & (rel < half)
    second = on & (rel >= half)
    f = jnp.where(first, rel, rel - half).astype(F32)
    inv = jnp.exp(-math.log(ROPE_THETA) * f * (2.0 / d))
    ang = jnp.arange(T, dtype=F32)[:, None] * inv[None, :]
    cos, sin = jnp.cos(ang), jnp.sin(ang)
    c = jnp.where(on[None], cos, 1.0)
    a = jnp.where(first[None], -sin, 0.0)
    b = jnp.where(second[None], sin, 0.0)
    return c, a, b


def _fwd_in(x, g1, w_in_p, tm):
    T = x.shape[0]

    def body(x_ref, g_ref, w_ref, z_ref, h_ref):
        h = _rms(x_ref[...], g_ref[...]).astype(BF16)
        h_ref[...] = h
        z_ref[...] = _dot(h, w_ref[...])

    return _pcall(body, name="fwd_in", grid=(T // tm,),
                  in_specs=[_rows(tm, D), _full((1, D)), _full((D, NZ))],
                  out_specs=[_rows(tm, NZ), _rows(tm, D)],
                  out_shape=[SDS((T, NZ), F32), SDS((T, D), BF16)], sem=("parallel",))(x, g1, w_in_p)


def _fwd_qkv(z, gq, gkv, wqb, wkn, wv, tab_m, tab_s, tm):
    T = z.shape[0]

    def body(qa_ref, sq_ref, skd_ref, svd_ref, kva_ref, kr_ref, gq_ref, gkv_ref, wqb_ref, wkn_ref, wv_ref,
             cm_ref, am_ref, bm_ref, cs_ref, as_ref, bs_ref,
             qn_ref, kvn_ref, qm_ref, km_ref, vm_ref, qt_ref, kt_ref, vt_ref, qs_ref, ks_ref, vs_ref):
        qn = _rms(qa_ref[...], gq_ref[...]).astype(BF16)
        qn_ref[...] = qn
        kvn = _rms(kva_ref[...], gkv_ref[...]).astype(BF16)
        kvn_ref[...] = kvn
        cm, am, bm = cm_ref[...], am_ref[...], bm_ref[...]
        cs, as_, bs = cs_ref[...], as_ref[...], bs_ref[...]
        k_rope = _rope(kr_ref[...], cm, am, bm, MLA_ROPE // 2)
        vt_row = lax.broadcasted_iota(jnp.int32, (LANES, tm), 0)
        v_all = _dot(kvn, wv_ref[...])
        q_all = _dot(qn, wqb_ref[...])
        k_all = _dot(kvn, wkn_ref[...])
        for j in range(D // LANES):
            sl = slice(LANES * j, LANES * (j + 1))
            v = v_all[:, sl]
            vm_ref[:, sl] = v.astype(BF16)
            v_t = v.T
            for hh, rows64 in enumerate((v_t, pltpu.roll(v_t, 64, 0))):
                blk = jnp.where(vt_row < 64, rows64, jnp.where(vt_row == 64, 1.0, 0.0))
                vt_ref[0, LANES * (2 * j + hh):LANES * (2 * j + hh + 1), :] = blk.astype(BF16)
        for h in range(MLA_HEADS):
            sl = slice(LANES * h, LANES * (h + 1))
            qh = _rope(q_all[:, sl], cm, am, bm, MLA_ROPE // 2)
            qm_ref[:, sl] = qh.astype(BF16)
            qt_ref[0, sl, :] = qh.T.astype(BF16)
            k = k_all[:, sl] + k_rope
            km_ref[:, sl] = k.astype(BF16)
            kt_ref[0, sl, :] = k.T.astype(BF16)
        for j in range(D // LANES):
            sl = slice(LANES * j, LANES * (j + 1))
            qs_ref[:, sl] = _rope(sq_ref[:, sl], cs, as_, bs, SWA_HD // 2).astype(BF16)
        for j in range(2):
            sl = slice(LANES * j, LANES * (j + 1))
            ks_ref[:, sl] = _rope(skd_ref[:, sl], cs, as_, bs, SWA_HD // 2).astype(BF16)
        vs_ref[...] = svd_ref[...].astype(BF16)

    tab = [_rows(tm, LANES)] * 6
    return _pcall(body, name="fwd_qkv", grid=(T // tm,),
                  in_specs=[_rows(tm, 256, 12), _rows(tm, 1024, 0), _rows(tm, 256, 13), _rows(tm, 256, 14),
                            _rows(tm, 128, 30), _rows(tm, 128, 31), _full((1, Q_LORA)), _full((1, KV_LORA)),
                            _full((Q_LORA, 2048)), _full((KV_LORA, 2048)), _full((KV_LORA, 1024))] + tab,
                  out_specs=[_rows(tm, Q_LORA), _rows(tm, KV_LORA), _rows(tm, 2048), _rows(tm, 2048), _rows(tm, 1024),
                             pl.BlockSpec((1, 2048, tm), lambda i: (i, 0, 0)), pl.BlockSpec((1, 2048, tm), lambda i: (i, 0, 0)),
                             pl.BlockSpec((1, 2048, tm), lambda i: (i, 0, 0)),
                             _rows(tm, 1024), _rows(tm, 256), _rows(tm, 256)],
                  out_shape=[SDS((T, Q_LORA), BF16), SDS((T, KV_LORA), BF16), SDS((T, 2048), BF16), SDS((T, 2048), BF16),
                             SDS((T, 1024), BF16), SDS((T // tm, 2048, tm), BF16), SDS((T // tm, 2048, tm), BF16),
                             SDS((T // tm, 2048, tm), BF16),
                             SDS((T, 1024), BF16), SDS((T, 256), BF16), SDS((T, 256), BF16)],
                  sem=("parallel",))(z, z, z, z, z, z, gq, gkv, wqb, wkn, wv, *tab_m, *tab_s)


def _mla_fwd(qm, km, vt, tb):
    T = qm.shape[0]
    nb = T // tb
    cc = ATT_COLS

    def body(q_ref, k_ref, vt_ref, o_ref, l_ref, s_ref, p_ref, al_ref, m_ref, acc_ref):
        i = pl.program_id(1)
        m_ref[...] = jnp.full(m_ref.shape, NEG, F32)
        acc_ref[...] = jnp.zeros_like(acc_ref)
        p_ref[1] = jnp.zeros(p_ref.shape[1:], BF16)
        al_ref[1] = jnp.ones(al_ref.shape[1:], F32)
        key = lax.broadcasted_iota(jnp.int32, (tb, cc), 0)
        qry = lax.broadcasted_iota(jnp.int32, (tb, cc), 1)

        def scores(j, slot):
            off = pl.multiple_of(j * tb, tb)
            for hh in range(2):
                sl = slice(LANES * hh, LANES * (hh + 1))
                s_ref[slot, hh] = _dot_nt(k_ref[pl.ds(off, tb), sl], q_ref[:, sl])

        def softmax(slot, diagonal):
            chains = [(hh, slice(cc * c, cc * (c + 1)), c) for hh in range(2) for c in range(tb // cc)]

            def scaled(hh, cols, c):
                t = s_ref[slot, hh, :, cols] * MLA_LOG2_SCALE
                return jnp.where(key <= qry + cc * c, t, NEG) if diagonal else t

            tops = []
            for hh, cols, c in chains:
                if diagonal:
                    top = jnp.max(scaled(hh, cols, c), axis=0, keepdims=True)
                else:
                    top = jnp.max(s_ref[slot, hh, :, cols], axis=0, keepdims=True) * MLA_LOG2_SCALE
                m_old = m_ref[hh, :, cols]
                mn = jnp.maximum(m_old, top)
                m_ref[hh, :, cols] = mn
                al_ref[slot, hh, :, cols] = jnp.exp2(m_old - mn)
                tops.append(mn)
            for (hh, cols, c), mn in zip(chains, tops):
                p_ref[slot, hh, :, cols] = jnp.exp2(scaled(hh, cols, c) - mn).astype(BF16)

        def accumulate(j, slot):
            for hh in range(2):
                acc_ref[hh] = al_ref[slot, hh] * acc_ref[hh] + _dot(vt_ref[j, LANES * hh:LANES * (hh + 1), :], p_ref[slot, hh])

        def step(t, carry):
            scores(2 * t + 1, 1)
            accumulate(jnp.maximum(2 * t - 1, 0), 1)
            softmax(0, False)
            scores(2 * t + 2, 0)
            accumulate(2 * t, 0)
            softmax(1, False)
            return carry

        scores(0, 0)
        lax.fori_loop(0, i // 2, step, 0)

        @pl.when(i % 2 == 1)
        def _():
            scores(i, 1)
            accumulate(jnp.maximum(i - 2, 0), 1)
            softmax(0, False)
            accumulate(i - 1, 0)
            softmax(1, True)
            accumulate(i, 1)

        @pl.when(i % 2 == 0)
        def _():
            accumulate(jnp.maximum(i - 1, 0), 1)
            softmax(0, True)
            accumulate(i, 0)
        den = [acc_ref[hh, 64:65, :] for hh in range(2)]
        o_ref[...] = jnp.concatenate([acc_ref[hh, 0:64, :] / den[hh] for hh in range(2)], axis=0).T
        sub = lax.broadcasted_iota(jnp.int32, (8, tb), 0)
        lse = [m_ref[hh] + jnp.log(den[hh]) * LOG2_E for hh in range(2)]
        l_ref[0, 0] = jnp.where(sub == 0, lse[0], jnp.where(sub == 1, lse[1], 0.0))

    return _pcall(body, name="mla_fwd", grid=(MLA_HEADS // 2, nb),
                  in_specs=[pl.BlockSpec((tb, 256), lambda p, i: (i, p)), pl.BlockSpec((T, 256), lambda p, i: (0, p)),
                            pl.BlockSpec((nb, 2 * LANES, tb), lambda p, i: (0, p, 0))],
                  out_specs=[pl.BlockSpec((tb, LANES), lambda p, i: (i, p)),
                             pl.BlockSpec((1, 1, 8, tb), lambda p, i: (p, i, 0, 0))],
                  out_shape=[SDS((T, D), F32), SDS((MLA_HEADS // 2, nb, 8, tb), F32)],
                  scratch=[pltpu.VMEM((2, 2, tb, tb), F32), pltpu.VMEM((2, 2, tb, tb), BF16), pltpu.VMEM((2, 2, 1, tb), F32),
                           pltpu.VMEM((2, 1, tb), F32), pltpu.VMEM((2, LANES, tb), F32)],
                  sem=("parallel", "arbitrary"))(qm, km, vt)


def _swa_mask(n):
    row = lax.broadcasted_iota(jnp.int32, (WINDOW, 2 * WINDOW), 0)
    col = lax.broadcasted_iota(jnp.int32, (WINDOW, 2 * WINDOW), 1)
    rel = row - col + WINDOW
    return (rel >= 0) & (rel < WINDOW) & ((col >= WINDOW) | (n > 0))


def _swa_specs(T):
    nb = T // WINDOW
    cur = lambda w: pl.BlockSpec((WINDOW, w), lambda n: (n, 0))
    prev = lambda w: pl.BlockSpec((WINDOW, w), lambda n: (jnp.maximum(n - 1, 0), 0))
    return nb, cur, prev


def _swa_fwd(sinks, qs, ks, vs):
    T = qs.shape[0]
    nb, cur, prev = _swa_specs(T)

    def body(sink_ref, q_ref, kc_ref, kp_ref, vc_ref, vp_ref, o_ref, l_ref):
        n = pl.program_id(0)
        mask = _swa_mask(n)
        lo = lax.broadcasted_iota(jnp.int32, (WINDOW, LANES), 1) < 64
        for g in range(2):
            gs = slice(LANES * g, LANES * (g + 1))
            kb = jnp.concatenate([kp_ref[:, gs], kc_ref[:, gs]], axis=0)
            vb = jnp.concatenate([vp_ref[:, gs], vc_ref[:, gs]], axis=0)
            for jj in range(4):
                j = 4 * g + jj
                sl = slice(LANES * j, LANES * (j + 1))
                qp = q_ref[:, sl]
                outs, lses = [], []
                for hf in range(2):
                    hm = lo if hf == 0 else jnp.logical_not(lo)
                    qh = jnp.where(hm, qp, jnp.zeros_like(qp))
                    s = jnp.where(mask, _dot_nt(qh, kb) * SWA_SCALE, NEG)
                    sk = sink_ref[2 * j + hf]
                    m = jnp.maximum(jnp.max(s, axis=1, keepdims=True), sk)
                    e = jnp.exp(s - m)
                    den = jnp.sum(e, axis=1, keepdims=True) + jnp.exp(sk - m)
                    p = e / den
                    outs.append(_dot(p.astype(BF16), vb))
                    lses.append(jnp.broadcast_to(m + jnp.log(den), (WINDOW, LANES)))
                o_ref[:, sl] = jnp.where(lo, outs[0], outs[1])
                l_ref[:, sl] = jnp.where(lo, lses[0], lses[1])

    return _pcall(body, name="swa_fwd", grid=(nb,),
                  in_specs=[pl.BlockSpec(memory_space=pltpu.SMEM), cur(D), cur(256), prev(256), cur(256), prev(256)],
                  out_specs=[cur(D), cur(D)], out_shape=[SDS((T, D), F32)] * 2,
                  sem=("parallel",))(sinks, qs, ks, ks, vs, vs)


def _fwd_mix(om, os_, z, x, wmu, wsu, wo, g2, tm):
    T = x.shape[0]

    def body(om_ref, os_ref, ga_ref, gb_ref, x_ref, wmu_ref, wsu_ref, wo_ref, g2_ref,
             y_ref, yo_ref, au_ref, bu_ref, x1_ref):
        au = _dot(om_ref[...].astype(BF16), wmu_ref[...])
        bu = _dot(os_ref[...].astype(BF16), wsu_ref[...])
        au_ref[...] = au
        bu_ref[...] = bu
        y = (_sigmoid(ga_ref[...]) * au + _sigmoid(gb_ref[...]) * bu).astype(BF16)
        y_ref[...] = y
        yo = _dot(y, wo_ref[...])
        yo_ref[...] = yo
        x1_ref[...] = x_ref[...] + _rms(yo, g2_ref[...])

    r = _rows(tm, D)
    w = _full((D, D))
    return _pcall(body, name="fwd_mix", grid=(T // tm,),
                  in_specs=[r, r, _rows(tm, D, 1), _rows(tm, D, 2), r, w, w, w, _full((1, D))],
                  out_specs=[r] * 5,
                  out_shape=[SDS((T, D), BF16), SDS((T, D), F32), SDS((T, D), F32), SDS((T, D), F32), SDS((T, D), F32)],
                  sem=("parallel",))(om, os_, z, z, x, wmu, wsu, wo, g2)


def _fwd_mlp_up(x1, g3, w1, tm):
    T = x1.shape[0]

    def body(x_ref, g_ref, w_ref, h_ref, a_ref, u_ref):
        h = _rms(x_ref[...], g_ref[...]).astype(BF16)
        h_ref[...] = h
        a = _dot(h, w_ref[...])
        a_ref[...] = a
        u_ref[...] = jnp.square(jnp.maximum(a, 0.0)).astype(BF16)

    return _pcall(body, name="fwd_mlp_up", grid=(T // tm,),
                  in_specs=[_rows(tm, D), _full((1, D)), _full((D, D_FF))],
                  out_specs=[_rows(tm, D), _rows(tm, D_FF), _rows(tm, D_FF)],
                  out_shape=[SDS((T, D), BF16), SDS((T, D_FF), F32), SDS((T, D_FF), BF16)],
                  sem=("parallel",))(x1, g3, w1)


def _fwd_mlp_down(u, w2, x1, g4, tm):
    T = x1.shape[0]

    def body(u_ref, w_ref, x_ref, g_ref, d_ref, x2_ref):
        d = _dot(u_ref[...], w_ref[...])
        d_ref[...] = d
        x2_ref[...] = x_ref[...] + _rms(d, g_ref[...])

    return _pcall(body, name="fwd_mlp_down", grid=(T // tm,),
                  in_specs=[_rows(tm, D_FF), _full((D_FF, D)), _rows(tm, D), _full((1, D))],
                  out_specs=[_rows(tm, D), _rows(tm, D)], out_shape=[SDS((T, D), F32)] * 2,
                  sem=("parallel",))(u, w2, x1, g4)


def _ple_fwd_bwd(p, x2, tgt, wple, g5, wpg, tm):
    T = x2.shape[0]

    def body(p_ref, x2_ref, t_ref, wple_ref, g5_ref, wpg_ref, loss_ref, dx2_ref, dgt_ref, de0_ref, dg5_ref):
        @pl.when(pl.program_id(0) == 0)
        def _():
            loss_ref[...] = jnp.zeros_like(loss_ref)
            dg5_ref[...] = jnp.zeros_like(dg5_ref)

        e0 = _dot(p_ref[...].astype(BF16), wple_ref[...])
        g5 = g5_ref[...]
        r = lax.rsqrt(jnp.mean(e0 * e0, axis=-1, keepdims=True) + EPS)
        en = e0 * r
        e = en * g5
        x2 = x2_ref[...]
        s = _sigmoid(_dot(x2.astype(BF16), wpg_ref[...]))
        diff = x2 + s * e - t_ref[...]
        sq = jnp.sum(jnp.sum(diff * diff, axis=1, keepdims=True), axis=0, keepdims=True)
        loss_ref[...] += jnp.broadcast_to(sq * (0.5 / D), loss_ref.shape)
        dx3 = diff * (1.0 / D)
        de = dx3 * s
        dgt = (dx3 * e * s * (1.0 - s)).astype(BF16)
        dgt_ref[...] = dgt
        dn = de * g5
        de0_ref[...] = (r * (dn - en * jnp.mean(dn * en, axis=-1, keepdims=True))).astype(BF16)
        dg5_ref[...] += jnp.sum(de * en, axis=0, keepdims=True)
        dx2_ref[...] = dx3 + _dot_nt(dgt, wpg_ref[...])

    r = _rows(tm, D)
    return _pcall(body, name="ple_fwd_bwd", grid=(T // tm,),
                  in_specs=[_rows(tm, PLE), r, r, _full((PLE, D)), _full((1, D)), _full((D, D))],
                  out_specs=[_full((8, LANES)), r, r, r, _full((1, D))],
                  out_shape=[SDS((8, LANES), F32), SDS((T, D), F32), SDS((T, D), BF16), SDS((T, D), BF16), SDS((1, D), F32)],
                  sem=("arbitrary",))(p, x2, tgt, wple, g5, wpg)


def _bwd_mlp_down(dx2, d, g4, w2, a, tm):
    T = dx2.shape[0]

    def body(dx_ref, d_ref, g_ref, w_ref, a_ref, dd_ref, da_ref, dg_ref):
        @pl.when(pl.program_id(0) == 0)
        def _():
            dg_ref[...] = jnp.zeros_like(dg_ref)

        dd, dg = _rms_bwd(dx_ref[...], d_ref[...], g_ref[...])
        dg_ref[...] += dg
        ddb = dd.astype(BF16)
        dd_ref[...] = ddb
        du = _dot_nt(ddb, w_ref[...])
        da_ref[...] = (du * (2.0 * jnp.maximum(a_ref[...], 0.0))).astype(BF16)

    return _pcall(body, name="bwd_mlp_down", grid=(T // tm,),
                  in_specs=[_rows(tm, D), _rows(tm, D), _full((1, D)), _full((D_FF, D)), _rows(tm, D_FF)],
                  out_specs=[_rows(tm, D), _rows(tm, D_FF), _full((1, D))],
                  out_shape=[SDS((T, D), BF16), SDS((T, D_FF), BF16), SDS((1, D), F32)],
                  sem=("arbitrary",))(dx2, d, g4, w2, a)


def _bwd_mlp_up(da, w1, x1, g3, dx2, tm):
    T = dx2.shape[0]

    def body(da_ref, w_ref, x_ref, g_ref, dx2_ref, dx1_ref, dg_ref):
        @pl.when(pl.program_id(0) == 0)
        def _():
            dg_ref[...] = jnp.zeros_like(dg_ref)

        dh = _dot_nt(da_ref[...], w_ref[...])
        dx, dg = _rms_bwd(dh, x_ref[...], g_ref[...])
        dg_ref[...] += dg
        dx1_ref[...] = dx2_ref[...] + dx

    return _pcall(body, name="bwd_mlp_up", grid=(T // tm,),
                  in_specs=[_rows(tm, D_FF), _full((D, D_FF)), _rows(tm, D), _full((1, D)), _rows(tm, D)],
                  out_specs=[_rows(tm, D), _full((1, D))],
                  out_shape=[SDS((T, D), F32), SDS((1, D), F32)], sem=("arbitrary",))(da, w1, x1, g3, dx2)


def _bwd_mix(dx1, yo, g2, wo, z, au, bu, wmu, wsu, om, tm):
    T = dx1.shape[0]

    def body(dx_ref, yo_ref, g_ref, wo_ref, ga_ref, gb_ref, au_ref, bu_ref, wmu_ref, wsu_ref, om_ref,
             dyo_ref, dg_ref, dau_ref, dbu_ref, dga_ref, dgb_ref, dos_ref, dl_ref, dot_ref):
        @pl.when(pl.program_id(0) == 0)
        def _():
            dg_ref[...] = jnp.zeros_like(dg_ref)

        dyo, dg = _rms_bwd(dx_ref[...], yo_ref[...], g_ref[...])
        dg_ref[...] += dg
        dyob = dyo.astype(BF16)
        dyo_ref[...] = dyob
        dy = _dot_nt(dyob, wo_ref[...])
        sa = _sigmoid(ga_ref[...])
        sb = _sigmoid(gb_ref[...])
        dau = (dy * sa).astype(BF16)
        dbu = (dy * sb).astype(BF16)
        dau_ref[...] = dau
        dbu_ref[...] = dbu
        dga_ref[...] = (dy * au_ref[...] * sa * (1.0 - sa)).astype(BF16)
        dgb_ref[...] = (dy * bu_ref[...] * sb * (1.0 - sb)).astype(BF16)
        dom = _dot_nt(dau, wmu_ref[...])
        dos_ref[...] = _dot_nt(dbu, wsu_ref[...])
        prod = dom * om_ref[...]
        sub = lax.broadcasted_iota(jnp.int32, (8, tm), 0)
        for pr in range(MLA_HEADS // 2):
            sl = slice(LANES * pr, LANES * (pr + 1))
            pt = prod[:, sl].T
            d0 = jnp.sum(pt[0:64], axis=0, keepdims=True)
            d1 = jnp.sum(pt[64:128], axis=0, keepdims=True)
            dl_ref[pr, 0] = jnp.where(sub == 0, d0, jnp.where(sub == 1, d1, 0.0))
            dot_ref[0, sl, :] = dom[:, sl].T.astype(BF16)

    r = _rows(tm, D)
    w = _full((D, D))
    return _pcall(body, name="bwd_mix", grid=(T // tm,),
                  in_specs=[r, r, _full((1, D)), w, _rows(tm, D, 1), _rows(tm, D, 2), r, r, w, w, r],
                  out_specs=[r, _full((1, D)), r, r, r, r, r, pl.BlockSpec((MLA_HEADS // 2, 1, 8, tm), lambda i: (0, i, 0, 0)),
                             pl.BlockSpec((1, D, tm), lambda i: (i, 0, 0))],
                  out_shape=[SDS((T, D), BF16), SDS((1, D), F32), SDS((T, D), BF16), SDS((T, D), BF16), SDS((T, D), BF16),
                             SDS((T, D), BF16), SDS((T, D), F32), SDS((MLA_HEADS // 2, T // tm, 8, tm), F32),
                             SDS((T // tm, D, tm), BF16)],
                  sem=("arbitrary",))(dx1, yo, g2, wo, z, z, au, bu, wmu, wsu, om)


def _mla_bwd(qm, qt, km, kt, vm, dot, lse, delta, tb):
    T = qm.shape[0]
    nb = T // tb
    cc = ATT_COLS

    def body(q_ref, qt_ref, k_ref, kt_ref, v_ref, dot_ref, l_ref, dl_ref, dqt_ref, dkt_ref, dvt_ref,
             s_ref, dp_ref, p_ref, ds_ref, vh_ref):
        j = pl.program_id(1)

        @pl.when(j == 0)
        def _():
            dqt_ref[...] = jnp.zeros_like(dqt_ref)

        dkt_ref[...] = jnp.zeros_like(dkt_ref)
        dvt_ref[...] = jnp.zeros_like(dvt_ref)
        lo = lax.broadcasted_iota(jnp.int32, (tb, LANES), 1) < 64
        key = lax.broadcasted_iota(jnp.int32, (tb, cc), 0)
        qry = lax.broadcasted_iota(jnp.int32, (tb, cc), 1)
        v = v_ref[...]
        vh_ref[0] = jnp.where(lo, v, jnp.zeros_like(v))
        vh_ref[1] = jnp.where(lo, jnp.zeros_like(v), v)

        def scores(i, slot):
            rows_i = pl.ds(pl.multiple_of(i * tb, tb), tb)
            for hh in range(2):
                sl = slice(LANES * hh, LANES * (hh + 1))
                s_ref[slot, hh] = _dot_nt(k_ref[:, sl], q_ref[rows_i, sl])
                dp_ref[slot, hh] = _dot(vh_ref[hh], dot_ref[i])

        def grads(i, slot, diagonal):
            lse_i = l_ref[0, i]
            delta_i = dl_ref[0, i]
            for hh in range(2):
                for c in range(tb // cc):
                    cols = slice(cc * c, cc * (c + 1))
                    p = jnp.exp2(s_ref[slot, hh, :, cols] * MLA_LOG2_SCALE - lse_i[hh:hh + 1, cols])
                    if diagonal:
                        p = jnp.where(key <= qry + cc * c, p, 0.0)
                    p_ref[hh, :, cols] = p.astype(BF16)
                    ds_ref[hh, :, cols] = (p * (dp_ref[slot, hh, :, cols] - delta_i[hh:hh + 1, cols]) * MLA_SCALE).astype(BF16)
            for hh in range(2):
                sl = slice(LANES * hh, LANES * (hh + 1))
                half = slice(64 * hh, 64 * (hh + 1))
                dvt_ref[0, half, :] += _dot_nt(dot_ref[i, half, :], p_ref[hh])
                dkt_ref[0, sl, :] += _dot_nt(qt_ref[i, sl, :], ds_ref[hh])
                dqt_ref[i, sl, :] += _dot(kt_ref[0, sl, :], ds_ref[hh])

        n_off = nb - 1 - j

        def step(u, carry):
            i0 = j + 1 + 2 * u
            scores(i0 + 1, 1)
            grads(i0, 0, False)
            scores(jnp.where(i0 + 2 < nb, i0 + 2, j), 0)
            grads(i0 + 1, 1, False)
            return carry

        scores(jnp.where(n_off > 0, j + 1, j), 0)
        lax.fori_loop(0, n_off // 2, step, 0)

        @pl.when(n_off % 2 == 1)
        def _():
            scores(j, 1)
            grads(nb - 1, 0, False)
            grads(j, 1, True)

        @pl.when(n_off % 2 == 0)
        def _():
            grads(j, 0, True)

    blk = lambda w: pl.BlockSpec((tb, w), lambda p, j: (j, p))
    stat = pl.BlockSpec((1, nb, 8, tb), lambda p, j: (p, 0, 0, 0))
    pair_t = lambda w: pl.BlockSpec((nb, w, tb), lambda p, j: (0, p, 0))
    blk_t = lambda w: pl.BlockSpec((1, w, tb), lambda p, j: (j, p, 0))
    return _pcall(body, name="mla_bwd", grid=(MLA_HEADS // 2, nb),
                  in_specs=[pl.BlockSpec((T, 256), lambda p, j: (0, p)), pair_t(256), blk(256), blk_t(256), blk(LANES),
                            pair_t(LANES), stat, stat],
                  out_specs=[pair_t(256), blk_t(256), blk_t(LANES)],
                  out_shape=[SDS((nb, 2048, tb), F32), SDS((nb, 2048, tb), F32), SDS((nb, D, tb), F32)],
                  scratch=[pltpu.VMEM((2, 2, tb, tb), F32), pltpu.VMEM((2, 2, tb, tb), F32), pltpu.VMEM((2, tb, tb), BF16),
                           pltpu.VMEM((2, tb, tb), BF16), pltpu.VMEM((2, tb, LANES), BF16)],
                  sem=("parallel", "arbitrary"))(qm, qt, km, kt, vm, dot, lse, delta)


def _swa_bwd(sinks, qs, ks, vs, do, o, lse):
    T = qs.shape[0]
    nb, cur, prev = _swa_specs(T)

    def body(sink_ref, q_ref, kc_ref, kp_ref, vc_ref, vp_ref, do_ref, o_ref, l_ref,
             dq_ref, dkc_ref, dkp_ref, dvc_ref, dvp_ref, dsink_ref):
        n = pl.program_id(0)

        @pl.when(n == 0)
        def _():
            dsink_ref[...] = jnp.zeros_like(dsink_ref)

        mask = _swa_mask(n)
        lo = lax.broadcasted_iota(jnp.int32, (WINDOW, LANES), 1) < 64
        lane8 = lax.broadcasted_iota(jnp.int32, (8, LANES), 1)
        dsink = jnp.zeros((8, LANES), F32)
        for g in range(2):
            gs = slice(LANES * g, LANES * (g + 1))
            kb = jnp.concatenate([kp_ref[:, gs], kc_ref[:, gs]], axis=0)
            vb = jnp.concatenate([vp_ref[:, gs], vc_ref[:, gs]], axis=0)
            dkb = jnp.zeros((2 * WINDOW, LANES), F32)
            dvb = jnp.zeros((2 * WINDOW, LANES), F32)
            for jj in range(4):
                j = 4 * g + jj
                sl = slice(LANES * j, LANES * (j + 1))
                qp = q_ref[:, sl]
                d_o = do_ref[:, sl]
                prod = d_o * o_ref[:, sl]
                lse_b = l_ref[:, sl]
                dqs = []
                for hf in range(2):
                    hm = lo if hf == 0 else jnp.logical_not(lo)
                    qh = jnp.where(hm, qp, jnp.zeros_like(qp))
                    s = jnp.where(mask, _dot_nt(qh, kb) * SWA_SCALE, NEG)
                    lse_h = jnp.max(jnp.where(hm, lse_b, -jnp.inf), axis=1, keepdims=True)
                    p = jnp.exp(s - lse_h)
                    dom = jnp.where(hm, d_o, 0.0).astype(BF16)
                    dp = _dot_nt(dom, vb)
                    delta = jnp.sum(jnp.where(hm, prod, 0.0), axis=1, keepdims=True)
                    ds = (p * (dp - delta) * SWA_SCALE).astype(BF16)
                    p_sink = jnp.exp(sink_ref[2 * j + hf] - lse_h)
                    d_sink = -jnp.sum(p_sink * delta, axis=0, keepdims=True)
                    dsink = dsink + jnp.where(lane8 == 2 * j + hf, d_sink, 0.0)
                    dvb = dvb + _dot_tn(p.astype(BF16), dom)
                    dkb = dkb + _dot_tn(ds, qh)
                    dqs.append(_dot(ds, kb))
                dq_ref[:, sl] = jnp.where(lo, dqs[0], dqs[1])
            dkp_ref[:, gs] = dkb[:WINDOW]
            dkc_ref[:, gs] = dkb[WINDOW:]
            dvp_ref[:, gs] = dvb[:WINDOW]
            dvc_ref[:, gs] = dvb[WINDOW:]
        dsink_ref[...] += dsink

    return _pcall(body, name="swa_bwd", grid=(nb,),
                  in_specs=[pl.BlockSpec(memory_space=pltpu.SMEM), cur(D), cur(256), prev(256), cur(256), prev(256),
                            cur(D), cur(D), cur(D)],
                  out_specs=[cur(D), cur(256), cur(256), cur(256), cur(256), _full((8, LANES))],
                  out_shape=[SDS((T, D), F32), SDS((T, 256), F32), SDS((T, 256), F32), SDS((T, 256), F32), SDS((T, 256), F32),
                             SDS((8, LANES), F32)],
                  sem=("arbitrary",))(sinks, qs, ks, ks, vs, vs, do, o, lse)


def _bwd_qkv(dqm, dkm, dvm, dqs, dkc, dkp, dvc, dvp, z, gq, gkv, wqb, wkn, wv, tab_m, tab_s):
    T = z.shape[0]
    tm = WINDOW
    nb = T // tm
    per = dqm.shape[2] // tm

    def body(dqm_ref, dkm_ref, dvm_ref, dqs_ref, dkc_ref, dkp_ref, dvc_ref, dvp_ref, qa_ref, kva_ref, gq_ref, gkv_ref,
             wqb_ref, wkn_ref, wv_ref, cm_ref, am_ref, bm_ref, cs_ref, as_ref, bs_ref,
             dq_out, dkn_out, dv_out, dsq_ref, drest_ref, dgq_ref, dgkv_ref):
        i = pl.program_id(0)

        @pl.when(i == 0)
        def _():
            dgq_ref[...] = jnp.zeros_like(dgq_ref)
            dgkv_ref[...] = jnp.zeros_like(dgkv_ref)

        cm, am, bm = cm_ref[...], -am_ref[...], -bm_ref[...]
        cs, as_, bs = cs_ref[...], -as_ref[...], -bs_ref[...]
        lane = lax.broadcasted_iota(jnp.int32, (tm, LANES), 1)
        nope = lane < MLA_NOPE
        roped = jnp.logical_and(lane >= MLA_NOPE, lane < MLA_NOPE + MLA_ROPE)
        dkr = jnp.zeros((tm, LANES), F32)
        for h in range(MLA_HEADS):
            sl = slice(LANES * h, LANES * (h + 1))
            dq_out[:, sl] = _rope(dqm_ref[0, sl, :].T, cm, am, bm, MLA_ROPE // 2).astype(BF16)
            dk_h = dkm_ref[0, sl, :].T
            dkn_out[:, sl] = jnp.where(nope, dk_h, 0.0).astype(BF16)
            dkr = dkr + jnp.where(roped, dk_h, 0.0)
        for j in range(D // LANES):
            sl = slice(LANES * j, LANES * (j + 1))
            dv_out[:, sl] = dvm_ref[0, sl, :].T.astype(BF16)
        dqn = _dot_nt(dq_out[...], wqb_ref[...])
        dkvn = _dot_nt(dkn_out[...], wkn_ref[...]) + _dot_nt(dv_out[...], wv_ref[...])
        dqa, dgq = _rms_bwd(dqn, qa_ref[...], gq_ref[...])
        dkva, dgkv = _rms_bwd(dkvn, kva_ref[...], gkv_ref[...])
        dgq_ref[...] += dgq
        dgkv_ref[...] += dgkv
        for j in range(D // LANES):
            sl = slice(LANES * j, LANES * (j + 1))
            dsq_ref[:, sl] = _rope(dqs_ref[:, sl], cs, as_, bs, SWA_HD // 2).astype(BF16)
        keep = (i < nb - 1).astype(F32)
        drest_ref[:, 0:256] = dqa.astype(BF16)
        for j in range(2):
            sl = slice(LANES * j, LANES * (j + 1))
            dk = dkc_ref[:, sl] + keep * dkp_ref[:, sl]
            drest_ref[:, 256 + LANES * j:256 + LANES * (j + 1)] = _rope(dk, cs, as_, bs, SWA_HD // 2).astype(BF16)
        drest_ref[:, 512:768] = (dvc_ref[...] + keep * dvp_ref[...]).astype(BF16)
        drest_ref[:, 768:896] = dkva.astype(BF16)
        drest_ref[:, 896:1024] = _rope(dkr, cm, am, bm, MLA_ROPE // 2).astype(BF16)

    nxt = pl.BlockSpec((tm, 256), lambda i: (jnp.minimum(i + 1, nb - 1), 0))
    tab = [_rows(tm, LANES)] * 6
    return _pcall(body, name="bwd_qkv", grid=(nb,),
                  in_specs=[pl.BlockSpec((1, 2048, tm), lambda i: (i // per, 0, i % per)),
                            pl.BlockSpec((1, 2048, tm), lambda i: (i // per, 0, i % per)),
                            pl.BlockSpec((1, 1024, tm), lambda i: (i // per, 0, i % per)), _rows(tm, 1024), _rows(tm, 256), nxt,
                            _rows(tm, 256), nxt, _rows(tm, 256, 12), _rows(tm, 128, 30), _full((1, Q_LORA)), _full((1, KV_LORA)),
                            _full((Q_LORA, 2048)), _full((KV_LORA, 2048)), _full((KV_LORA, 1024))] + tab,
                  out_specs=[_rows(tm, 2048), _rows(tm, 2048), _rows(tm, 1024), _rows(tm, 1024), _rows(tm, 1024),
                             _full((1, Q_LORA)), _full((1, KV_LORA))],
                  out_shape=[SDS((T, 2048), BF16), SDS((T, 2048), BF16), SDS((T, 1024), BF16), SDS((T, 1024), BF16),
                             SDS((T, 1024), BF16), SDS((1, Q_LORA), F32), SDS((1, KV_LORA), F32)],
                  sem=("arbitrary",))(dqm, dkm, dvm, dqs, dkc, dkp, dvc, dvp, z, z, gq, gkv, wqb, wkn, wv, *tab_m, *tab_s)


def _bwd_in(dsq, dga, dgb, drest, w_in_p, x, g1, dx1, tm):
    T = x.shape[0]

    def body(a_ref, b_ref, c_ref, d_ref, w_ref, x_ref, g_ref, dx1_ref, dx_ref, dg_ref):
        @pl.when(pl.program_id(0) == 0)
        def _():
            dg_ref[...] = jnp.zeros_like(dg_ref)

        dh = (_dot_nt(a_ref[...], w_ref[:, 0:1024]) + _dot_nt(b_ref[...], w_ref[:, 1024:2048])
              + _dot_nt(c_ref[...], w_ref[:, 2048:3072]) + _dot_nt(d_ref[...], w_ref[:, 3072:4096]))
        dx, dg = _rms_bwd(dh, x_ref[...], g_ref[...])
        dg_ref[...] += dg
        dx_ref[...] = dx1_ref[...] + dx

    r = _rows(tm, D)
    return _pcall(body, name="bwd_in", grid=(T // tm,),
                  in_specs=[r, r, r, r, _full((D, NZ)), r, _full((1, D)), r],
                  out_specs=[r, _full((1, D))], out_shape=[SDS((T, D), F32), SDS((1, D), F32)],
                  sem=("arbitrary",))(dsq, dga, dgb, drest, w_in_p, x, g1, dx1)


def _wgrad(a, g, name, out_dtype=F32, by_column_block=False):
    T, K = a.shape
    N = g.shape[1]
    tk, tn, tt = min(K, 1024), min(N, 1024), min(T, 1024)
    assert K % tk == 0 and N % tn == 0 and T % tt == 0, (a.shape, g.shape)
    steps = T // tt

    def body(a_ref, g_ref, o_ref, acc_ref):
        t = pl.program_id(2)

        @pl.when(t == 0)
        def _():
            acc_ref[...] = jnp.zeros_like(acc_ref)

        acc_ref[...] += _dot_tn(a_ref[...].astype(BF16), g_ref[...].astype(BF16))

        @pl.when(t == steps - 1)
        def _():
            o_ref[...] = acc_ref[...].astype(out_dtype).reshape(o_ref.shape)

    if by_column_block:
        out_spec, out_shape = pl.BlockSpec((1, tk, tn), lambda k, n, t: (n, k, 0)), SDS((N // tn, K, tn), out_dtype)
    else:
        out_spec, out_shape = pl.BlockSpec((tk, tn), lambda k, n, t: (k, n)), SDS((K, N), out_dtype)
    return _pcall(body, name=name, grid=(K // tk, N // tn, steps),
                  in_specs=[pl.BlockSpec((tt, tk), lambda k, n, t: (t, k)), pl.BlockSpec((tt, tn), lambda k, n, t: (t, n))],
                  out_specs=out_spec, out_shape=out_shape, scratch=[pltpu.VMEM((tk, tn), F32)],
                  sem=("parallel", "parallel", "arbitrary"))(a, g)


def _adamw(w, packed_g, m, v, name):
    _, R, C = w.shape
    _, row0, lane0 = PACK_AT[name]
    tr = min(R, 256 if row0 % 256 == 0 else 128)
    assert row0 % tr == 0 and R % tr == 0

    def body(w_ref, g_ref, m_ref, v_ref, go_ref, d_ref, m2_ref, v2_ref):
        g_ = g_ref[:, lane0:lane0 + C]
        go_ref[0] = g_
        m2 = ADAM_B1 * m_ref[0] + (1.0 - ADAM_B1) * g_
        v2 = ADAM_B2 * v_ref[0] + (1.0 - ADAM_B2) * jnp.square(g_)
        m_hat = m2 / (1.0 - ADAM_B1 ** ADAM_STEP)
        v_hat = v2 / (1.0 - ADAM_B2 ** ADAM_STEP)
        d_ref[0] = -ADAM_LR * (m_hat / (jnp.sqrt(v_hat) + ADAM_EPS) + ADAM_WD * w_ref[0])
        m2_ref[0] = m2
        v2_ref[0] = v2

    r = pl.BlockSpec((1, tr, C), lambda i: (0, i, 0))
    return _pcall(body, name="adamw_" + name, grid=(R // tr,),
                  in_specs=[r, pl.BlockSpec((tr, D), lambda i: (row0 // tr + i, 0)), r, r], out_specs=[r] * 4,
                  out_shape=[SDS((1, R, C), F32)] * 4, sem=("parallel",))(w, packed_g, m, v)


def _adamw_small(w, parts, m, v):
    def body(w_ref, p_ref, m_ref, v_ref, g_ref, d_ref, m2_ref, v2_ref):
        g_ = p_ref[0]
        for k in range(1, N_DEV):
            g_ = g_ + p_ref[k]
        g_ref[...] = g_
        m2 = ADAM_B1 * m_ref[...] + (1.0 - ADAM_B1) * g_
        v2 = ADAM_B2 * v_ref[...] + (1.0 - ADAM_B2) * jnp.square(g_)
        m_hat = m2 / (1.0 - ADAM_B1 ** ADAM_STEP)
        v_hat = v2 / (1.0 - ADAM_B2 ** ADAM_STEP)
        d_ref[...] = -ADAM_LR * (m_hat / (jnp.sqrt(v_hat) + ADAM_EPS) + ADAM_WD * w_ref[...])
        m2_ref[...] = m2
        v2_ref[...] = v2

    s = _full((8, D))
    return _pcall(body, name="adamw_small", grid=(1,), in_specs=[s, _full((N_DEV, 8, D)), s, s], out_specs=[s] * 4,
                  out_shape=[SDS((8, D), F32)] * 4, sem=("arbitrary",))(w, parts, m, v)


ANY = pl.BlockSpec(memory_space=pl.ANY)


def _place():
    x, y, c = lax.axis_index("x"), lax.axis_index("y"), lax.axis_index("c")
    chips = [(1 - x, y), (x, 1 - y), (1 - x, 1 - y)]
    return x, y, c, chips


def _all_gather(wpk):
    rows = wpk.shape[0]
    HALF = rows // 2
    assert HALF % 16 == 0

    def body(in_ref, out_ref, send_sems, recv_sems):
        x, y, c, chips = _place()
        half = pl.ds(pl.multiple_of(c * HALF, 16), HALF)
        other = pl.ds(pl.multiple_of((1 - c) * HALF, 16), HALF)

        def copy(k, src, dst, to):
            return pltpu.make_async_remote_copy(src_ref=src, dst_ref=dst, send_sem=send_sems.at[k], recv_sem=recv_sems.at[k],
                                                device_id=to, device_id_type=MESH)

        first = [copy(k, in_ref.at[half], out_ref.at[2 * x + y, half], (cx, cy, c)) for k, (cx, cy) in enumerate(chips)]
        for cp in first:
            cp.start()
        passed = []
        for k, (cx, cy) in enumerate(chips):
            slot = out_ref.at[2 * cx + cy, half]
            copy(k, slot, slot, (x, y, c)).wait_recv()
            fwd = copy(3 + k, slot, slot, (x, y, 1 - c))
            fwd.start()
            passed.append(fwd)
        for k, (cx, cy) in enumerate(chips):
            slot = out_ref.at[2 * cx + cy, other]
            copy(3 + k, slot, slot, (x, y, c)).wait_recv()
        for cp in first + passed:
            cp.wait_send()

    return _pcall(body, name="all_gather_weights", in_specs=[ANY], out_specs=ANY,
                  out_shape=SDS((N_CHIPS, rows, D), BF16),
                  scratch=[pltpu.SemaphoreType.DMA((6,)), pltpu.SemaphoreType.DMA((6,))])(wpk)


HBM = pl.BlockSpec(memory_space=pltpu.HBM)
SEM = pl.BlockSpec(memory_space=pltpu.SEMAPHORE)
DATAFLOW = pltpu.SideEffectType.DATAFLOW_SIDE_EFFECTING


def _in_hbm(a):
    return pltpu.with_memory_space_constraint(a, pltpu.HBM)


def _gather_late_start(wpk, after):
    rows = wpk.shape[0]

    def body(in_ref, land_ref, after_ref, send_sems, recv_sems, in_thru, land_thru, token):
        x, y, c, chips = _place()
        for k, (cx, cy) in enumerate(chips):
            pltpu.make_async_remote_copy(src_ref=in_ref, dst_ref=land_ref.at[2 * x + y], send_sem=send_sems.at[k],
                                         recv_sem=recv_sems.at[k], device_id=(cx, cy, c), device_id_type=MESH).start()
        token[...] = jnp.zeros_like(token)

    return pl.pallas_call(
        body, name="gather_late_start",
        out_shape=(pltpu.SemaphoreType.DMA((3,)), pltpu.SemaphoreType.DMA((3,)), pltpu.HBM(wpk.shape, wpk.dtype),
                   pltpu.HBM((N_CHIPS, rows, D), wpk.dtype), SDS((8, LANES), F32)),
        in_specs=(HBM, HBM, ANY), out_specs=(SEM, SEM, HBM, HBM, pl.BlockSpec(memory_space=pltpu.VMEM)),
        input_output_aliases={0: 2, 1: 3}, compiler_params=pltpu.CompilerParams(has_side_effects=DATAFLOW),
    )(_in_hbm(wpk), _in_hbm(lax.empty((N_CHIPS, rows, D), wpk.dtype)), after)


def _gather_late_wait(send_sems, recv_sems, in_thru, land_thru, after):
    def body(in_ref, land_ref, send_sems, recv_sems, after_ref, after2_ref, in_dead, got_ref):
        x, y, c, chips = _place()
        for k, (cx, cy) in enumerate(chips):
            cp = pltpu.make_async_remote_copy(src_ref=in_ref, dst_ref=land_ref.at[2 * cx + cy], send_sem=send_sems.at[k],
                                              recv_sem=recv_sems.at[k], device_id=(cx, cy, c), device_id_type=MESH)
            cp.wait_send()
            cp.wait_recv()

    return pl.pallas_call(
        body, name="gather_late_wait",
        out_shape=(pltpu.HBM(in_thru.shape, in_thru.dtype), pltpu.HBM(land_thru.shape, land_thru.dtype)),
        in_specs=(HBM, HBM, SEM, SEM, ANY, ANY), out_specs=(HBM, HBM), input_output_aliases={0: 0, 1: 1},
        compiler_params=pltpu.CompilerParams(has_side_effects=DATAFLOW),
    )(in_thru, land_thru, send_sems, recv_sems, *after)[1]


def _rs_sibling(gpk):
    HALF = gpk.shape[1] // 2

    def body(in_ref, out_ref, send_sem, recv_sem):
        x, y, c, _ = _place()
        theirs = pl.ds(pl.multiple_of((1 - c) * HALF, 8), HALF)
        cp = pltpu.make_async_remote_copy(src_ref=in_ref.at[:, theirs], dst_ref=out_ref, send_sem=send_sem, recv_sem=recv_sem,
                                          device_id=(x, y, 1 - c), device_id_type=MESH)
        cp.start()
        cp.wait()

    return _pcall(body, name="rs_sibling", in_specs=[ANY], out_specs=ANY, out_shape=SDS((N_CHIPS, HALF, D), F32),
                  scratch=[pltpu.SemaphoreType.DMA, pltpu.SemaphoreType.DMA])(gpk)


def _rs_add_sibling(cidx, gpk, got):
    HALF = got.shape[1]
    th = HALF // 4
    nh = HALF // th
    assert th % 16 == 0

    def body(c_ref, a_ref, b_ref, o_ref):
        o_ref[...] = (a_ref[...] + b_ref[...]).astype(BF16)

    gs = pltpu.PrefetchScalarGridSpec(
        num_scalar_prefetch=1, grid=(N_CHIPS, nh),
        in_specs=[pl.BlockSpec((1, th, D), lambda j, i, c: (j, c[0] * nh + i, 0)), pl.BlockSpec((1, th, D), lambda j, i, c: (j, i, 0))],
        out_specs=pl.BlockSpec((1, th, D), lambda j, i, c: (j, i, 0)))
    return pl.pallas_call(body, name="rs_add_sibling", grid_spec=gs, out_shape=SDS((N_CHIPS, HALF, D), BF16),
                          compiler_params=pltpu.CompilerParams(dimension_semantics=("parallel", "parallel"),
                                                               vmem_limit_bytes=48 << 20))(cidx, gpk, got)


def _rs_chips(part, small):
    def body(p_ref, s_ref, o_ref, so_ref, send_sems, recv_sems, ssend_sems, srecv_sems, local_sem):
        x, y, c, chips = _place()
        me = 2 * x + y
        mine_s = pltpu.make_async_copy(s_ref, so_ref.at[4 * x + 2 * y + c], local_sem)
        mine_s.start()
        sends = []
        for k, (cx, cy) in enumerate(chips):
            sends.append(pltpu.make_async_remote_copy(src_ref=p_ref.at[2 * cx + cy], dst_ref=o_ref.at[me], send_sem=send_sems.at[k],
                                                      recv_sem=recv_sems.at[k], device_id=(cx, cy, c), device_id_type=MESH))
        peers = [(x, y, 1 - c)] + [(cx, cy, c) for cx, cy in chips] + [(cx, cy, 1 - c) for cx, cy in chips]
        for k, to in enumerate(peers):
            sends.append(pltpu.make_async_remote_copy(src_ref=s_ref, dst_ref=so_ref.at[4 * x + 2 * y + c], send_sem=ssend_sems.at[k],
                                                      recv_sem=srecv_sems.at[k], device_id=to, device_id_type=MESH))
        for cp in sends:
            cp.start()
        for k, (cx, cy) in enumerate(chips):
            slot = o_ref.at[2 * cx + cy]
            pltpu.make_async_remote_copy(src_ref=slot, dst_ref=slot, send_sem=send_sems.at[k], recv_sem=recv_sems.at[k],
                                         device_id=(x, y, c), device_id_type=MESH).wait_recv()
        for k, (px, py, pc) in enumerate(peers):
            slot = so_ref.at[4 * px + 2 * py + pc]
            pltpu.make_async_remote_copy(src_ref=slot, dst_ref=slot, send_sem=ssend_sems.at[k], recv_sem=srecv_sems.at[k],
                                         device_id=(x, y, c), device_id_type=MESH).wait_recv()
        for cp in sends:
            cp.wait_send()
        mine_s.wait()

    return _pcall(body, name="rs_chips", in_specs=[ANY, ANY], out_specs=[ANY, ANY],
                  out_shape=[SDS(part.shape, part.dtype), SDS((N_DEV, 8, D), F32)],
                  scratch=[pltpu.SemaphoreType.DMA((3,)), pltpu.SemaphoreType.DMA((3,)), pltpu.SemaphoreType.DMA((7,)),
                           pltpu.SemaphoreType.DMA((7,)), pltpu.SemaphoreType.DMA])(part, small)


def _rs_add_chips(qidx, part, parts):
    HALF = part.shape[1]
    th = HALF // 4
    assert th % 16 == 0

    def body(q_ref, own_ref, p_ref, o_ref):
        for me in range(N_CHIPS):
            @pl.when(q_ref[0] == me)
            def _(me=me):
                t = [(own_ref[0] if j == me else p_ref[j]).astype(F32) for j in range(N_CHIPS)]
                o_ref[...] = ((t[0] + t[1]) + t[2]) + t[3]

    gs = pltpu.PrefetchScalarGridSpec(
        num_scalar_prefetch=1, grid=(HALF // th,),
        in_specs=[pl.BlockSpec((1, th, D), lambda i, q: (q[0], i, 0)), pl.BlockSpec((N_CHIPS, th, D), lambda i, q: (0, i, 0))],
        out_specs=pl.BlockSpec((th, D), lambda i, q: (i, 0)))
    return pl.pallas_call(body, name="rs_add_chips", grid_spec=gs, out_shape=SDS((HALF, D), F32),
                          compiler_params=pltpu.CompilerParams(dimension_semantics=("parallel",),
                                                               vmem_limit_bytes=48 << 20))(qidx, part, parts)


def _rs_join(early, late):
    def body(e_ref, l_ref, eo_ref, lo_ref, send_sems, recv_sems):
        x, y, c, _ = _place()
        cps = [pltpu.make_async_remote_copy(src_ref=src, dst_ref=dst, send_sem=send_sems.at[k], recv_sem=recv_sems.at[k],
                                            device_id=(x, y, 1 - c), device_id_type=MESH)
               for k, (src, dst) in enumerate([(e_ref, eo_ref), (l_ref, lo_ref)])]
        for cp in cps:
            cp.start()
        for cp in cps:
            cp.wait()

    return _pcall(body, name="rs_join", in_specs=[ANY, ANY], out_specs=[ANY, ANY],
                  out_shape=[SDS(early.shape, F32), SDS(late.shape, F32)],
                  scratch=[pltpu.SemaphoreType.DMA((2,)), pltpu.SemaphoreType.DMA((2,))])(early, late)


def _reduce_late_start(gpk, after):
    rows = gpk.shape[1]
    HALF = rows // 2
    assert HALF % 16 == 0

    def body(in_ref, land_ref, after_ref, send_sems, recv_sems, in_thru, land_thru, token):
        x, y, c, chips = _place()
        me = 4 * x + 2 * y + c
        peers = [(x, y, 1 - c)] + [(cx, cy, c) for cx, cy in chips] + [(cx, cy, 1 - c) for cx, cy in chips]
        for k, (px, py, pc) in enumerate(peers):
            src = in_ref.at[2 * px + py, pl.ds(pl.multiple_of(pc * HALF, 16), HALF)]
            pltpu.make_async_remote_copy(src_ref=src, dst_ref=land_ref.at[me], send_sem=send_sems.at[k], recv_sem=recv_sems.at[k],
                                         device_id=(px, py, pc), device_id_type=MESH).start()
        token[...] = jnp.zeros_like(token)

    return pl.pallas_call(
        body, name="reduce_late_start",
        out_shape=(pltpu.SemaphoreType.DMA((7,)), pltpu.SemaphoreType.DMA((7,)), pltpu.HBM(gpk.shape, gpk.dtype),
                   pltpu.HBM((N_DEV, HALF, D), gpk.dtype), SDS((8, LANES), F32)),
        in_specs=(HBM, HBM, ANY), out_specs=(SEM, SEM, HBM, HBM, pl.BlockSpec(memory_space=pltpu.VMEM)),
        input_output_aliases={0: 2, 1: 3}, compiler_params=pltpu.CompilerParams(has_side_effects=DATAFLOW),
    )(_in_hbm(gpk), _in_hbm(lax.empty((N_DEV, HALF, D), gpk.dtype)), after)


def _reduce_late_wait(send_sems, recv_sems, in_thru, land_thru, after):
    def body(in_ref, land_ref, send_sems, recv_sems, after_ref, in_out, got_ref):
        x, y, c, chips = _place()
        peers = [(x, y, 1 - c)] + [(cx, cy, c) for cx, cy in chips] + [(cx, cy, 1 - c) for cx, cy in chips]
        for k, (px, py, pc) in enumerate(peers):
            cp = pltpu.make_async_remote_copy(src_ref=land_ref.at[0], dst_ref=land_ref.at[4 * px + 2 * py + pc],
                                              send_sem=send_sems.at[k], recv_sem=recv_sems.at[k],
                                              device_id=(px, py, pc), device_id_type=MESH)
            cp.wait_send()
            cp.wait_recv()

    return pl.pallas_call(
        body, name="reduce_late_wait",
        out_shape=(pltpu.HBM(in_thru.shape, in_thru.dtype), pltpu.HBM(land_thru.shape, land_thru.dtype)),
        in_specs=(HBM, HBM, SEM, SEM, ANY), out_specs=(HBM, HBM), input_output_aliases={0: 0, 1: 1},
        compiler_params=pltpu.CompilerParams(has_side_effects=DATAFLOW),
    )(in_thru, land_thru, send_sems, recv_sems, after)


def _reduce_late_add(didx, gpk, parts):
    HALF = parts.shape[1]
    th = HALF // 4
    nh = HALF // th
    assert th % 16 == 0

    def body(d_ref, own_ref, p_ref, o_ref):
        for me in range(N_DEV):
            @pl.when(d_ref[0] == me)
            def _(me=me):
                t = [(own_ref[0] if j == me else p_ref[j]).astype(F32) for j in range(N_DEV)]
                o_ref[...] = ((((((t[0] + t[1]) + t[2]) + t[3]) + t[4]) + t[5]) + t[6]) + t[7]

    gs = pltpu.PrefetchScalarGridSpec(
        num_scalar_prefetch=1, grid=(nh,),
        in_specs=[pl.BlockSpec((1, th, D), lambda i, d: (d[1], d[2] * nh + i, 0)), pl.BlockSpec((N_DEV, th, D), lambda i, d: (0, i, 0))],
        out_specs=pl.BlockSpec((th, D), lambda i, d: (i, 0)))
    return pl.pallas_call(body, name="reduce_late_add", grid_spec=gs, out_shape=SDS((HALF, D), F32),
                          compiler_params=pltpu.CompilerParams(dimension_semantics=("parallel",),
                                                               vmem_limit_bytes=48 << 20))(didx, gpk, parts)


def _pack_early(b, dtype):
    lanes = lambda a: jnp.pad(a.astype(dtype), ((0, 0), (0, D - a.shape[1])))
    pair = jnp.concatenate([b["w_q_b"].astype(dtype), b["w_ple"].astype(dtype), jnp.zeros((256, D - 640), dtype)], axis=1)
    return jnp.concatenate([lanes(b["w_in"]), pair, lanes(b["w_kv_b"])], axis=0)


def _pack_late(b, dtype):
    return jnp.concatenate([b[n].astype(dtype) for n in ("w_mla_up", "w_swa_up", "w_out", "w_ple_gate", "w_mlp_up", "w_mlp_down")],
                           axis=0)


def _unpack_shards(pk, which):
    return {n: pk[PACK_AT[n][1]:PACK_AT[n][1] + r, PACK_AT[n][2]:PACK_AT[n][2] + c] for n, r, c in BIG if PACK_AT[n][0] == which}


def _full_weights(gathered, own, chip, which):
    own_b = _unpack_shards(own, which)
    per_chip = [{n: jnp.where(chip == j, own_b[n], blk) for n, blk in _unpack_shards(gathered[j], which).items()}
                for j in range(N_CHIPS)]
    out = {}
    for n in own_b:
        shards = [pc[n] for pc in per_chip]
        if n == "w_in":
            out["w_in_p"] = _w_in_internal(shards)
        else:
            out[n] = jnp.concatenate(shards, axis=1 if n in COL_SHARDED else 0)
    return out


def _split_full_grads(grads, pack, dtype):
    shard = {n: (r, c) for n, r, c in BIG}
    chunks = []
    for j in range(N_CHIPS):
        blocks = {}
        for n, g in grads.items():
            if n == "w_in_p":
                blocks["w_in"] = _w_in_grad_shard(g, j)
                continue
            r, c = shard[n]
            blocks[n] = g[:, j * c:(j + 1) * c] if n in COL_SHARDED else g[j * r:(j + 1) * r]
        chunks.append(pack(blocks, dtype))
    return jnp.stack(chunks)


W_IN_SHARD = 936
W_IN_SEGMENTS = ((0, 256, (3072,)), (256, 384, (3840,)), (384, 416, (4032,)), (416, 1440, (0,)), (1440, 1504, (3328, 3392)),
                 (1504, 1568, (3456, 3520)), (1568, 1632, (3584, 3648)), (1632, 1696, (3712, 3776)), (1696, 3744, (1024,)))


def _w_in_internal(shards):
    def cols(a, b):
        out = []
        for j, s in enumerate(shards):
            lo, hi = max(a, W_IN_SHARD * j), min(b, W_IN_SHARD * (j + 1))
            if lo < hi:
                out.append(s[:, lo - W_IN_SHARD * j:hi - W_IN_SHARD * j])
        return out

    pieces = {}
    for a, b, places in W_IN_SEGMENTS:
        for at in places:
            pieces[at] = cols(a, b)
    zeros = lambda n: [jnp.zeros((D, n), shards[0].dtype)]
    pieces[3968] = zeros(64)
    pieces[4064] = zeros(32)
    return jnp.concatenate([piece for at in sorted(pieces) for piece in pieces[at]], axis=1)


def _w_in_grad_shard(g, j):
    def internal(a, b):
        out = []
        while a < b:
            end = min(b, (a // D + 1) * D)
            out.append(g[a // D][:, a % D:a % D + end - a])
            a = end
        return out

    out = []
    for a, b, places in W_IN_SEGMENTS:
        lo, hi = max(a, W_IN_SHARD * j), min(b, W_IN_SHARD * (j + 1))
        if lo < hi:
            parts = [internal(at + lo - a, at + hi - a) for at in places]
            if len(parts) == 1:
                out += parts[0]
            else:
                assert len(parts[0]) == len(parts[1]) == 1
                out.append(parts[0][0] + parts[1][0])
    return jnp.concatenate(out, axis=1)


def _local_step(x, p, tgt, w, small, late_weights, late_grads_out):
    T = x.shape[0]
    tm = 256
    tb = 256
    w_in_p = w["w_in_p"]
    wqb = jnp.pad(w["w_q_b"].reshape(Q_LORA, MLA_HEADS, 96), ((0, 0), (0, 0), (0, 32))).reshape(Q_LORA, 2048)
    wkv = w["w_kv_b"].reshape(KV_LORA, MLA_HEADS, 128)
    wkn = jnp.pad(wkv[:, :, :64], ((0, 0), (0, 0), (0, 64))).reshape(KV_LORA, 2048)
    wv = wkv[:, :, 64:].reshape(KV_LORA, 1024)
    tab_m = _rope_tables(T, "mla")
    tab_s = _rope_tables(T, "swa")
    g1, gq, gkv, sinks = small["g_mix_pre"], small["g_q_a"], small["g_kv_a"], small["sinks"]
    g2, g3, g4, g5 = small["g_mix_post"], small["g_mlp_pre"], small["g_mlp_post"], small["g_ple"]
    sink_vec = sinks.reshape(SWA_HEADS)

    z, h1 = _fwd_in(x, g1, w_in_p, tm)
    qn, kvn, qm, km, vm, qt, kt, vt, qs, ks, vs = _fwd_qkv(z, gq, gkv, wqb, wkn, wv, tab_m, tab_s, tb)
    om, lse_m = _mla_fwd(qm, km, vt, tb)
    os_, lse_s = _swa_fwd(sink_vec, qs, ks, vs)
    w = {**w, **late_weights((om, os_))}
    y, yo, au, bu, x1 = _fwd_mix(om, os_, z, x, w["w_mla_up"], w["w_swa_up"], w["w_out"], g2, tm)
    h2, a, u = _fwd_mlp_up(x1, g3, w["w_mlp_up"], tm)
    d, x2 = _fwd_mlp_down(u, w["w_mlp_down"], x1, g4, tm)
    loss, dx2, dgt, de0, dg5 = _ple_fwd_bwd(p, x2, tgt, w["w_ple"], g5, w["w_ple_gate"], tm)

    dd, da, dg4 = _bwd_mlp_down(dx2, d, g4, w["w_mlp_down"], a, tm)
    dx1, dg3 = _bwd_mlp_up(da, w["w_mlp_up"], x1, g3, dx2, tm)
    dyo, dg2, dau, dbu, dga, dgb, dos, delta_m, dom_t = _bwd_mix(dx1, yo, g2, w["w_out"], z, au, bu, w["w_mla_up"],
                                                                w["w_swa_up"], om, tb)
    token = late_grads_out({
        "w_mla_up": _wgrad(om, dau, "wgrad_mla_up", BF16),
        "w_swa_up": _wgrad(os_, dbu, "wgrad_swa_up", BF16),
        "w_out": _wgrad(y, dyo, "wgrad_out", BF16),
        "w_ple_gate": _wgrad(x2, dgt, "wgrad_ple_gate", BF16),
        "w_mlp_up": _wgrad(h2, da, "wgrad_mlp_up", BF16, by_column_block=True),
        "w_mlp_down": _wgrad(u, dd, "wgrad_mlp_down", BF16),
    })
    delta_m = delta_m + token[0, 0]
    dqm, dkm, dvm = _mla_bwd(qm, qt, km, kt, vm, dom_t, lse_m, delta_m, tb)
    dqs, dkc, dkp, dvc, dvp, dsink = _swa_bwd(sink_vec, qs, ks, vs, dos, os_, lse_s)
    dqb, dknb, dvb, dsq, drest, dgq, dgkv = _bwd_qkv(dqm, dkm, dvm, dqs, dkc, dkp, dvc, dvp, z, gq, gkv, wqb, wkn, wv,
                                                      tab_m, tab_s)
    gx, dg1 = _bwd_in(dsq, dga, dgb, drest, w_in_p, x, g1, dx1, tm)

    g_in_p = [_wgrad(h1, dsq, "wgrad_in_sq"), _wgrad(h1, dga, "wgrad_in_ga"), _wgrad(h1, dgb, "wgrad_in_gb"),
              _wgrad(h1, drest, "wgrad_in_rest")]
    g_qb_p = _wgrad(qn, dqb, "wgrad_q_b")
    g_kn_p = _wgrad(kvn, dknb, "wgrad_kv_b_nope")
    g_v_p = _wgrad(kvn, dvb, "wgrad_kv_b_v")
    grads = {
        "w_in_p": g_in_p,
        "w_q_b": g_qb_p.reshape(Q_LORA, MLA_HEADS, 128)[:, :, :96].reshape(Q_LORA, 1536),
        "w_kv_b": jnp.concatenate([g_kn_p.reshape(KV_LORA, MLA_HEADS, 128)[:, :, :64], g_v_p.reshape(KV_LORA, MLA_HEADS, 64)],
                                  axis=2).reshape(KV_LORA, 2048),
        "w_ple": _wgrad(p, de0, "wgrad_ple"),
    }
    small_grads = {"g_mix_pre": dg1, "g_q_a": dgq, "g_kv_a": dgkv, "sinks": dsink[0:1, 0:SWA_HEADS], "g_mix_post": dg2,
                   "g_mlp_pre": dg3, "g_mlp_post": dg4, "g_ple": dg5}
    return loss, gx, grads, small_grads


def _pack_small(vals, fill, scalar=None):
    wide = [vals[n] for n, k in SMALL if k == D]
    narrow = [vals[n] for n, k in SMALL if k != D]
    used = sum(k for _, k in SMALL if k != D)
    last = jnp.concatenate(narrow + [jnp.full((1, D - used), fill, F32)], axis=1)
    rest = jnp.full((2, D), fill, F32)
    if scalar is not None:
        rest = jnp.concatenate([jnp.concatenate([scalar, rest[0:1, 1:]], axis=1), rest[1:2]], axis=0)
    return jnp.concatenate(wide + [last, rest], axis=0)


def _unpack_small(pk):
    out, row, off = {}, 0, 0
    for n, k in SMALL:
        if k == D:
            out[n] = pk[row:row + 1]
            row += 1
    for n, k in SMALL:
        if k != D:
            out[n] = pk[5:6, off:off + k]
            off += k
    return out


def kernel(x, p, g_mix_pre, w_in, g_q_a, w_q_b, g_kv_a, w_kv_b, sinks, w_mla_up, w_swa_up, w_out, g_mix_post, g_mlp_pre, w_mlp_up, w_mlp_down, g_mlp_post, w_ple, g_ple, w_ple_gate, loss_target, m_g_mix_pre, m_w_in, m_g_q_a, m_w_q_b, m_g_kv_a, m_w_kv_b, m_sinks, m_w_mla_up, m_w_swa_up, m_w_out, m_g_mix_post, m_g_mlp_pre, m_w_mlp_up, m_w_mlp_down, m_g_mlp_post, m_w_ple, m_g_ple, m_w_ple_gate, v_g_mix_pre, v_w_in, v_g_q_a, v_w_q_b, v_g_kv_a, v_w_kv_b, v_sinks, v_w_mla_up, v_w_swa_up, v_w_out, v_g_mix_post, v_g_mlp_pre, v_w_mlp_up, v_w_mlp_down, v_g_mlp_post, v_w_ple, v_g_ple, v_w_ple_gate):
    given = dict(locals())
    big_w = {n: given[n][0] for n, _, _ in BIG}
    small_w = {n: given[n] for n, _ in SMALL}
    small_m = {n: given["m_" + n] for n, _ in SMALL}
    small_v = {n: given["v_" + n] for n, _ in SMALL}

    core = lax.axis_index("c")
    chip = 2 * lax.axis_index("x") + lax.axis_index("y")
    core_i = core.astype(jnp.int32).reshape(1)
    chip_i = chip.astype(jnp.int32).reshape(1)
    dev_i = jnp.stack([2 * chip + core, chip, core]).astype(jnp.int32)

    own_early = _pack_early(big_w, BF16)
    own_late = _pack_late(big_w, BF16)
    got_early = _all_gather(own_early)
    late_flight = _gather_late_start(own_late, got_early)
    weights = _full_weights(got_early, own_early, chip, "early")
    step_small = {**small_w, "g_mix_pre": small_w["g_mix_pre"] + late_flight[4][0, 0]}

    def late_weights(after):
        return _full_weights(_gather_late_wait(*late_flight[:4], after), own_late, chip, "late")

    flight = {}

    def late_grads_out(grads):
        order = sorted((n for n in grads), key=lambda n: PACK_AT[n][1])
        gpk_late = jnp.concatenate([grads[n] if grads[n].ndim == 3 else grads[n].reshape(N_CHIPS, -1, D) for n in order], axis=1)
        flight["late"] = _reduce_late_start(gpk_late, dev_i)
        return flight["late"][4]

    loss_blk, gx, grads, small_grads = _local_step(x[0], p[0, 0], loss_target[0], weights, step_small, late_weights,
                                                   late_grads_out)

    gpk = _split_full_grads(grads, _pack_early, F32)
    got = _rs_sibling(gpk)
    part = _rs_add_sibling(core_i, gpk, got)
    parts, small_parts = _rs_chips(part, _pack_small(small_grads, 0.0, loss_blk[0:1, 0:1]))
    mine_early = _rs_add_chips(chip_i, part, parts)
    gpk_late, parts_late = _reduce_late_wait(*flight["late"][:4], mine_early)
    mine_late = _reduce_late_add(dev_i, gpk_late, parts_late)
    theirs_early, theirs_late = _rs_join(mine_early, mine_late)
    joined = {"early": jnp.where(core == 0, jnp.concatenate([mine_early, theirs_early]), jnp.concatenate([theirs_early, mine_early])),
              "late": jnp.where(core == 0, jnp.concatenate([mine_late, theirs_late]), jnp.concatenate([theirs_late, mine_late]))}

    g_small_pk, d_small_pk, m_small_pk, v_small_pk = _adamw_small(
        _pack_small(small_w, 0.0), small_parts, _pack_small(small_m, 0.0), _pack_small(small_v, 1.0))
    loss = g_small_pk[6, 0]
    g_small, d_small = _unpack_small(g_small_pk), _unpack_small(d_small_pk)
    m_small, v_small = _unpack_small(m_small_pk), _unpack_small(v_small_pk)

    out_g, out_d, out_m, out_v = dict(g_small), dict(d_small), dict(m_small), dict(v_small)
    for n, _, _ in BIG:
        out_g[n], out_d[n], out_m[n], out_v[n] = _adamw(given[n], joined[PACK_AT[n][0]], given["m_" + n], given["v_" + n], n)
    order = ["g_mix_pre", "w_in", "g_q_a", "w_q_b", "g_kv_a", "w_kv_b", "sinks", "w_mla_up", "w_swa_up", "w_out", "g_mix_post",
             "g_mlp_pre", "w_mlp_up", "w_mlp_down", "g_mlp_post", "w_ple", "g_ple", "w_ple_gate"]
    return (loss, gx[None], *[out_g[n] for n in order], *[out_d[n] for n in order], *[out_m[n] for n in order],
            *[out_v[n] for n in order])
```

```python
import math

import jax
import jax.numpy as jnp
from jax import lax
from jax.experimental import pallas as pl
from jax.experimental.pallas import tpu as pltpu

F32 = jnp.float32
BF16 = jnp.bfloat16
SDS = jax.ShapeDtypeStruct

D = 1024
D_FF = 4096
PLE = 256
Q_LORA = 256
KV_LORA = 128
MLA_HEADS = 16
MLA_NOPE = 64
MLA_ROPE = 32
SWA_HEADS = 16
SWA_HD = 64
WINDOW = 128
ROPE_THETA = 10000.0
EPS = 1e-6
NEG = -1e30
NZ = 4096
MLA_SCALE = (MLA_NOPE + MLA_ROPE) ** -0.5
LOG2_E = math.log2(math.e)
MLA_LOG2_SCALE = MLA_SCALE * LOG2_E
SWA_SCALE = SWA_HD ** -0.5

ADAM_LR = 0.001
ADAM_B1 = 0.9
ADAM_B2 = 0.999
ADAM_EPS = 1e-08
ADAM_WD = 0.01
ADAM_STEP = 10

LANES = 128
ATT_COLS = 128
N_CHIPS = 4
N_DEV = 8
MESH = pl.DeviceIdType.MESH

NT = (((1,), (1,)), ((), ()))
TN = (((0,), (0,)), ((), ()))

BIG = (("w_in", 1024, 936), ("w_q_b", 256, 384), ("w_kv_b", 128, 512), ("w_mla_up", 256, 1024),
       ("w_swa_up", 256, 1024), ("w_out", 256, 1024), ("w_mlp_up", 1024, 1024), ("w_mlp_down", 1024, 1024),
       ("w_ple", 256, 256), ("w_ple_gate", 256, 1024))
COL_SHARDED = ("w_in", "w_q_b", "w_kv_b", "w_mlp_up", "w_ple")
PACK_AT = {"w_in": ("early", 0, 0), "w_q_b": ("early", 1024, 0), "w_ple": ("early", 1024, 384), "w_kv_b": ("early", 1280, 0),
           "w_mla_up": ("late", 0, 0), "w_swa_up": ("late", 256, 0), "w_out": ("late", 512, 0), "w_ple_gate": ("late", 768, 0),
           "w_mlp_up": ("late", 1024, 0), "w_mlp_down": ("late", 2048, 0)}
PACK_ROWS = {"early": 1408, "late": 3072}
SMALL = (("g_mix_pre", 1024), ("g_q_a", 256), ("g_kv_a", 128), ("sinks", 16), ("g_mix_post", 1024),
         ("g_mlp_pre", 1024), ("g_mlp_post", 1024), ("g_ple", 1024))


def _dot(a, b):
    return jnp.dot(a, b, preferred_element_type=F32)


def _dot_nt(a, b):
    return lax.dot_general(a, b, NT, preferred_element_type=F32)


def _dot_tn(a, b):
    return lax.dot_general(a, b, TN, preferred_element_type=F32)


def _pcall(body, *, name, out_shape, grid=(), in_specs=None, out_specs=None, scratch=(), sem=None, vmem_mb=48, aliases=None):
    params = dict(vmem_limit_bytes=vmem_mb << 20)
    if sem is not None:
        params["dimension_semantics"] = sem
    return pl.pallas_call(body, name=name, grid=grid, in_specs=in_specs, out_specs=out_specs, out_shape=out_shape,
                          scratch_shapes=list(scratch), input_output_aliases=aliases or {},
                          compiler_params=pltpu.CompilerParams(**params))


def _rows(tm, n, col=0):
    return pl.BlockSpec((tm, n), lambda i: (i, col))


def _full(shape):
    return pl.BlockSpec(shape, lambda i: (0,) * len(shape))


def _rms(x, g):
    r = lax.rsqrt(jnp.mean(x * x, axis=-1, keepdims=True) + EPS)
    return x * r * g


def _rms_bwd(dy, x, g):
    r = lax.rsqrt(jnp.mean(x * x, axis=-1, keepdims=True) + EPS)
    xn = x * r
    dn = dy * g
    dx = r * (dn - xn * jnp.mean(dn * xn, axis=-1, keepdims=True))
    return dx, jnp.sum(dy * xn, axis=0, keepdims=True)


def _sigmoid(x):
    return 1.0 / (1.0 + jnp.exp(-x))


def _rope(x, c, a, b, half):
    return x * c + pltpu.roll(x, LANES - half, 1) * a + pltpu.roll(x, half, 1) * b


def _rope_tables(T, kind):
    lane = jnp.arange(LANES)
    if kind == "mla":
        half = MLA_ROPE // 2
        rel = lane - MLA_NOPE
        on = (rel >= 0) & (rel < MLA_ROPE)
        d = MLA_ROPE
    else:
        half = SWA_HD // 2
        rel = lane % SWA_HD
        on = jnp.ones((LANES,), bool)
        d = SWA_HD
    first = on & (rel < half)
    second = on & (rel >= half)
    f = jnp.where(first, rel, rel - half).astype(F32)
    inv = jnp.exp(-math.log(ROPE_THETA) * f * (2.0 / d))
    ang = jnp.arange(T, dtype=F32)[:, None] * inv[None, :]
    cos, sin = jnp.cos(ang), jnp.sin(ang)
    c = jnp.where(on[None], cos, 1.0)
    a = jnp.where(first[None], -sin, 0.0)
    b = jnp.where(second[None], sin, 0.0)
    return c, a, b


def _fwd_in(x, g1, w_in_p, tm):
    T = x.shape[0]

    def body(x_ref, g_ref, w_ref, z_ref, h_ref):
        h = _rms(x_ref[...], g_ref[...]).astype(BF16)
        h_ref[...] = h
        z_ref[...] = _dot(h, w_ref[...])

    return _pcall(body, name="fwd_in", grid=(T // tm,),
                  in_specs=[_rows(tm, D), _full((1, D)), _full((D, NZ))],
                  out_specs=[_rows(tm, NZ), _rows(tm, D)],
                  out_shape=[SDS((T, NZ), F32), SDS((T, D), BF16)], sem=("parallel",))(x, g1, w_in_p)


def _fwd_qkv(z, gq, gkv, wqb, wkn, wv, tab_m, tab_s, tm):
    T = z.shape[0]

    def body(qa_ref, sq_ref, skd_ref, svd_ref, kva_ref, kr_ref, gq_ref, gkv_ref, wqb_ref, wkn_ref, wv_ref,
             cm_ref, am_ref, bm_ref, cs_ref, as_ref, bs_ref,
             qn_ref, kvn_ref, qm_ref, km_ref, vm_ref, qt_ref, kt_ref, vt_ref, qs_ref, ks_ref, vs_ref):
        qn = _rms(qa_ref[...], gq_ref[...]).astype(BF16)
        qn_ref[...] = qn
        kvn = _rms(kva_ref[...], gkv_ref[...]).astype(BF16)
        kvn_ref[...] = kvn
        cm, am, bm = cm_ref[...], am_ref[...], bm_ref[...]
        cs, as_, bs = cs_ref[...], as_ref[...], bs_ref[...]
        k_rope = _rope(kr_ref[...], cm, am, bm, MLA_ROPE // 2)
        vt_row = lax.broadcasted_iota(jnp.int32, (LANES, tm), 0)
        v_all = _dot(kvn, wv_ref[...])
        q_all = _dot(qn, wqb_ref[...])
        k_all = _dot(kvn, wkn_ref[...])
        for j in range(D // LANES):
            sl = slice(LANES * j, LANES * (j + 1))
            v = v_all[:, sl]
            vm_ref[:, sl] = v.astype(BF16)
            v_t = v.T
            for hh, rows64 in enumerate((v_t, pltpu.roll(v_t, 64, 0))):
                blk = jnp.where(vt_row < 64, rows64, jnp.where(vt_row == 64, 1.0, 0.0))
                vt_ref[0, LANES * (2 * j + hh):LANES * (2 * j + hh + 1), :] = blk.astype(BF16)
        for h in range(MLA_HEADS):
            sl = slice(LANES * h, LANES * (h + 1))
            qh = _rope(q_all[:, sl], cm, am, bm, MLA_ROPE // 2)
            qm_ref[:, sl] = qh.astype(BF16)
            qt_ref[0, sl, :] = qh.T.astype(BF16)
            k = k_all[:, sl] + k_rope
            km_ref[:, sl] = k.astype(BF16)
            kt_ref[0, sl, :] = k.T.astype(BF16)
        for j in range(D // LANES):
            sl = slice(LANES * j, LANES * (j + 1))
            qs_ref[:, sl] = _rope(sq_ref[:, sl], cs, as_, bs, SWA_HD // 2).astype(BF16)
        for j in range(2):
            sl = slice(LANES * j, LANES * (j + 1))
            ks_ref[:, sl] = _rope(skd_ref[:, sl], cs, as_, bs, SWA_HD // 2).astype(BF16)
        vs_ref[...] = svd_ref[...].astype(BF16)

    tab = [_rows(tm, LANES)] * 6
    return _pcall(body, name="fwd_qkv", grid=(T // tm,),
                  in_specs=[_rows(tm, 256, 12), _rows(tm, 1024, 0), _rows(tm, 256, 13), _rows(tm, 256, 14),
                            _rows(tm, 128, 30), _rows(tm, 128, 31), _full((1, Q_LORA)), _full((1, KV_LORA)),
                            _full((Q_LORA, 2048)), _full((KV_LORA, 2048)), _full((KV_LORA, 1024))] + tab,
                  out_specs=[_rows(tm, Q_LORA), _rows(tm, KV_LORA), _rows(tm, 2048), _rows(tm, 2048), _rows(tm, 1024),
                             pl.BlockSpec((1, 2048, tm), lambda i: (i, 0, 0)), pl.BlockSpec((1, 2048, tm), lambda i: (i, 0, 0)),
                             pl.BlockSpec((1, 2048, tm), lambda i: (i, 0, 0)),
                             _rows(tm, 1024), _rows(tm, 256), _rows(tm, 256)],
                  out_shape=[SDS((T, Q_LORA), BF16), SDS((T, KV_LORA), BF16), SDS((T, 2048), BF16), SDS((T, 2048), BF16),
                             SDS((T, 1024), BF16), SDS((T // tm, 2048, tm), BF16), SDS((T // tm, 2048, tm), BF16),
                             SDS((T // tm, 2048, tm), BF16),
                             SDS((T, 1024), BF16), SDS((T, 256), BF16), SDS((T, 256), BF16)],
                  sem=("parallel",))(z, z, z, z, z, z, gq, gkv, wqb, wkn, wv, *tab_m, *tab_s)


def _mla_fwd(qm, km, vt, tb):
    T = qm.shape[0]
    nb = T // tb
    cc = ATT_COLS

    def body(q_ref, k_ref, vt_ref, o_ref, l_ref, s_ref, p_ref, al_ref, m_ref, acc_ref):
        i = pl.program_id(1)
        m_ref[...] = jnp.full(m_ref.shape, NEG, F32)
        acc_ref[...] = jnp.zeros_like(acc_ref)
        p_ref[1] = jnp.zeros(p_ref.shape[1:], BF16)
        al_ref[1] = jnp.ones(al_ref.shape[1:], F32)
        key = lax.broadcasted_iota(jnp.int32, (tb, cc), 0)
        qry = lax.broadcasted_iota(jnp.int32, (tb, cc), 1)

        def scores(j, slot):
            off = pl.multiple_of(j * tb, tb)
            for hh in range(2):
                sl = slice(LANES * hh, LANES * (hh + 1))
                s_ref[slot, hh] = _dot_nt(k_ref[pl.ds(off, tb), sl], q_ref[:, sl])

        def softmax(slot, diagonal):
            chains = [(hh, slice(cc * c, cc * (c + 1)), c) for hh in range(2) for c in range(tb // cc)]

            def scaled(hh, cols, c):
                t = s_ref[slot, hh, :, cols] * MLA_LOG2_SCALE
                return jnp.where(key <= qry + cc * c, t, NEG) if diagonal else t

            tops = []
            for hh, cols, c in chains:
                if diagonal:
                    top = jnp.max(scaled(hh, cols, c), axis=0, keepdims=True)
                else:
                    top = jnp.max(s_ref[slot, hh, :, cols], axis=0, keepdims=True) * MLA_LOG2_SCALE
                m_old = m_ref[hh, :, cols]
                mn = jnp.maximum(m_old, top)
                m_ref[hh, :, cols] = mn
                al_ref[slot, hh, :, cols] = jnp.exp2(m_old - mn)
                tops.append(mn)
            for (hh, cols, c), mn in zip(chains, tops):
                p_ref[slot, hh, :, cols] = jnp.exp2(scaled(hh, cols, c) - mn).astype(BF16)

        def accumulate(j, slot):
            for hh in range(2):
                acc_ref[hh] = al_ref[slot, hh] * acc_ref[hh] + _dot(vt_ref[j, LANES * hh:LANES * (hh + 1), :], p_ref[slot, hh])

        def step(t, carry):
            scores(2 * t + 1, 1)
            accumulate(jnp.maximum(2 * t - 1, 0), 1)
            softmax(0, False)
            scores(2 * t + 2, 0)
            accumulate(2 * t, 0)
            softmax(1, False)
            return carry

        scores(0, 0)
        lax.fori_loop(0, i // 2, step, 0)

        @pl.when(i % 2 == 1)
        def _():
            scores(i, 1)
            accumulate(jnp.maximum(i - 2, 0), 1)
            softmax(0, False)
            accumulate(i - 1, 0)
            softmax(1, True)
            accumulate(i, 1)

        @pl.when(i % 2 == 0)
        def _():
            accumulate(jnp.maximum(i - 1, 0), 1)
            softmax(0, True)
            accumulate(i, 0)
        den = [acc_ref[hh, 64:65, :] for hh in range(2)]
        o_ref[...] = jnp.concatenate([acc_ref[hh, 0:64, :] / den[hh] for hh in range(2)], axis=0).T
        sub = lax.broadcasted_iota(jnp.int32, (8, tb), 0)
        lse = [m_ref[hh] + jnp.log(den[hh]) * LOG2_E for hh in range(2)]
        l_ref[0, 0] = jnp.where(sub == 0, lse[0], jnp.where(sub == 1, lse[1], 0.0))

    return _pcall(body, name="mla_fwd", grid=(MLA_HEADS // 2, nb),
                  in_specs=[pl.BlockSpec((tb, 256), lambda p, i: (i, p)), pl.BlockSpec((T, 256), lambda p, i: (0, p)),
                            pl.BlockSpec((nb, 2 * LANES, tb), lambda p, i: (0, p, 0))],
                  out_specs=[pl.BlockSpec((tb, LANES), lambda p, i: (i, p)),
                             pl.BlockSpec((1, 1, 8, tb), lambda p, i: (p, i, 0, 0))],
                  out_shape=[SDS((T, D), F32), SDS((MLA_HEADS // 2, nb, 8, tb), F32)],
                  scratch=[pltpu.VMEM((2, 2, tb, tb), F32), pltpu.VMEM((2, 2, tb, tb), BF16), pltpu.VMEM((2, 2, 1, tb), F32),
                           pltpu.VMEM((2, 1, tb), F32), pltpu.VMEM((2, LANES, tb), F32)],
                  sem=("parallel", "arbitrary"))(qm, km, vt)


def _swa_mask(n):
    row = lax.broadcasted_iota(jnp.int32, (WINDOW, 2 * WINDOW), 0)
    col = lax.broadcasted_iota(jnp.int32, (WINDOW, 2 * WINDOW), 1)
    rel = row - col + WINDOW
    return (rel >= 0) & (rel < WINDOW) & ((col >= WINDOW) | (n > 0))


def _swa_specs(T):
    nb = T // WINDOW
    cur = lambda w: pl.BlockSpec((WINDOW, w), lambda n: (n, 0))
    prev = lambda w: pl.BlockSpec((WINDOW, w), lambda n: (jnp.maximum(n - 1, 0), 0))
    return nb, cur, prev


def _swa_fwd(sinks, qs, ks, vs):
    T = qs.shape[0]
    nb, cur, prev = _swa_specs(T)

    def body(sink_ref, q_ref, kc_ref, kp_ref, vc_ref, vp_ref, o_ref, l_ref):
        n = pl.program_id(0)
        mask = _swa_mask(n)
        lo = lax.broadcasted_iota(jnp.int32, (WINDOW, LANES), 1) < 64
        for g in range(2):
            gs = slice(LANES * g, LANES * (g + 1))
            kb = jnp.concatenate([kp_ref[:, gs], kc_ref[:, gs]], axis=0)
            vb = jnp.concatenate([vp_ref[:, gs], vc_ref[:, gs]], axis=0)
            for jj in range(4):
                j = 4 * g + jj
                sl = slice(LANES * j, LANES * (j + 1))
                qp = q_ref[:, sl]
                outs, lses = [], []
                for hf in range(2):
                    hm = lo if hf == 0 else jnp.logical_not(lo)
                    qh = jnp.where(hm, qp, jnp.zeros_like(qp))
                    s = jnp.where(mask, _dot_nt(qh, kb) * SWA_SCALE, NEG)
                    sk = sink_ref[2 * j + hf]
                    m = jnp.maximum(jnp.max(s, axis=1, keepdims=True), sk)
                    e = jnp.exp(s - m)
                    den = jnp.sum(e, axis=1, keepdims=True) + jnp.exp(sk - m)
                    p = e / den
                    outs.append(_dot(p.astype(BF16), vb))
                    lses.append(jnp.broadcast_to(m + jnp.log(den), (WINDOW, LANES)))
                o_ref[:, sl] = jnp.where(lo, outs[0], outs[1])
                l_ref[:, sl] = jnp.where(lo, lses[0], lses[1])

    return _pcall(body, name="swa_fwd", grid=(nb,),
                  in_specs=[pl.BlockSpec(memory_space=pltpu.SMEM), cur(D), cur(256), prev(256), cur(256), prev(256)],
                  out_specs=[cur(D), cur(D)], out_shape=[SDS((T, D), F32)] * 2,
                  sem=("parallel",))(sinks, qs, ks, ks, vs, vs)


def _fwd_mix(om, os_, z, x, wmu, wsu, wo, g2, tm):
    T = x.shape[0]

    def body(om_ref, os_ref, ga_ref, gb_ref, x_ref, wmu_ref, wsu_ref, wo_ref, g2_ref,
             y_ref, yo_ref, au_ref, bu_ref, x1_ref):
        au = _dot(om_ref[...].astype(BF16), wmu_ref[...])
        bu = _dot(os_ref[...].astype(BF16), wsu_ref[...])
        au_ref[...] = au
        bu_ref[...] = bu
        y = (_sigmoid(ga_ref[...]) * au + _sigmoid(gb_ref[...]) * bu).astype(BF16)
        y_ref[...] = y
        yo = _dot(y, wo_ref[...])
        yo_ref[...] = yo
        x1_ref[...] = x_ref[...] + _rms(yo, g2_ref[...])

    r = _rows(tm, D)
    w = _full((D, D))
    return _pcall(body, name="fwd_mix", grid=(T // tm,),
                  in_specs=[r, r, _rows(tm, D, 1), _rows(tm, D, 2), r, w, w, w, _full((1, D))],
                  out_specs=[r] * 5,
                  out_shape=[SDS((T, D), BF16), SDS((T, D), F32), SDS((T, D), F32), SDS((T, D), F32), SDS((T, D), F32)],
                  sem=("parallel",))(om, os_, z, z, x, wmu, wsu, wo, g2)


def _fwd_mlp_up(x1, g3, w1, tm):
    T = x1.shape[0]

    def body(x_ref, g_ref, w_ref, h_ref, a_ref, u_ref):
        h = _rms(x_ref[...], g_ref[...]).astype(BF16)
        h_ref[...] = h
        a = _dot(h, w_ref[...])
        a_ref[...] = a
        u_ref[...] = jnp.square(jnp.maximum(a, 0.0)).astype(BF16)

    return _pcall(body, name="fwd_mlp_up", grid=(T // tm,),
                  in_specs=[_rows(tm, D), _full((1, D)), _full((D, D_FF))],
                  out_specs=[_rows(tm, D), _rows(tm, D_FF), _rows(tm, D_FF)],
                  out_shape=[SDS((T, D), BF16), SDS((T, D_FF), F32), SDS((T, D_FF), BF16)],
                  sem=("parallel",))(x1, g3, w1)


def _fwd_mlp_down(u, w2, x1, g4, tm):
    T = x1.shape[0]

    def body(u_ref, w_ref, x_ref, g_ref, d_ref, x2_ref):
        d = _dot(u_ref[...], w_ref[...])
        d_ref[...] = d
        x2_ref[...] = x_ref[...] + _rms(d, g_ref[...])

    return _pcall(body, name="fwd_mlp_down", grid=(T // tm,),
                  in_specs=[_rows(tm, D_FF), _full((D_FF, D)), _rows(tm, D), _full((1, D))],
                  out_specs=[_rows(tm, D), _rows(tm, D)], out_shape=[SDS((T, D), F32)] * 2,
                  sem=("parallel",))(u, w2, x1, g4)


def _ple_fwd_bwd(p, x2, tgt, wple, g5, wpg, tm):
    T = x2.shape[0]

    def body(p_ref, x2_ref, t_ref, wple_ref, g5_ref, wpg_ref, loss_ref, dx2_ref, dgt_ref, de0_ref, dg5_ref):
        @pl.when(pl.program_id(0) == 0)
        def _():
            loss_ref[...] = jnp.zeros_like(loss_ref)
            dg5_ref[...] = jnp.zeros_like(dg5_ref)

        e0 = _dot(p_ref[...].astype(BF16), wple_ref[...])
        g5 = g5_ref[...]
        r = lax.rsqrt(jnp.mean(e0 * e0, axis=-1, keepdims=True) + EPS)
        en = e0 * r
        e = en * g5
        x2 = x2_ref[...]
        s = _sigmoid(_dot(x2.astype(BF16), wpg_ref[...]))
        diff = x2 + s * e - t_ref[...]
        sq = jnp.sum(jnp.sum(diff * diff, axis=1, keepdims=True), axis=0, keepdims=True)
        loss_ref[...] += jnp.broadcast_to(sq * (0.5 / D), loss_ref.shape)
        dx3 = diff * (1.0 / D)
        de = dx3 * s
        dgt = (dx3 * e * s * (1.0 - s)).astype(BF16)
        dgt_ref[...] = dgt
        dn = de * g5
        de0_ref[...] = (r * (dn - en * jnp.mean(dn * en, axis=-1, keepdims=True))).astype(BF16)
        dg5_ref[...] += jnp.sum(de * en, axis=0, keepdims=True)
        dx2_ref[...] = dx3 + _dot_nt(dgt, wpg_ref[...])

    r = _rows(tm, D)
    return _pcall(body, name="ple_fwd_bwd", grid=(T // tm,),
                  in_specs=[_rows(tm, PLE), r, r, _full((PLE, D)), _full((1, D)), _full((D, D))],
                  out_specs=[_full((8, LANES)), r, r, r, _full((1, D))],
                  out_shape=[SDS((8, LANES), F32), SDS((T, D), F32), SDS((T, D), BF16), SDS((T, D), BF16), SDS((1, D), F32)],
                  sem=("arbitrary",))(p, x2, tgt, wple, g5, wpg)


def _bwd_mlp_down(dx2, d, g4, w2, a, tm):
    T = dx2.shape[0]

    def body(dx_ref, d_ref, g_ref, w_ref, a_ref, dd_ref, da_ref, dg_ref):
        @pl.when(pl.program_id(0) == 0)
        def _():
            dg_ref[...] = jnp.zeros_like(dg_ref)

        dd, dg = _rms_bwd(dx_ref[...], d_ref[...], g_ref[...])
        dg_ref[...] += dg
        ddb = dd.astype(BF16)
        dd_ref[...] = ddb
        du = _dot_nt(ddb, w_ref[...])
        da_ref[...] = (du * (2.0 * jnp.maximum(a_ref[...], 0.0))).astype(BF16)

    return _pcall(body, name="bwd_mlp_down", grid=(T // tm,),
                  in_specs=[_rows(tm, D), _rows(tm, D), _full((1, D)), _full((D_FF, D)), _rows(tm, D_FF)],
                  out_specs=[_rows(tm, D), _rows(tm, D_FF), _full((1, D))],
                  out_shape=[SDS((T, D), BF16), SDS((T, D_FF), BF16), SDS((1, D), F32)],
                  sem=("arbitrary",))(dx2, d, g4, w2, a)


def _bwd_mlp_up(da, w1, x1, g3, dx2, tm):
    T = dx2.shape[0]

    def body(da_ref, w_ref, x_ref, g_ref, dx2_ref, dx1_ref, dg_ref):
        @pl.when(pl.program_id(0) == 0)
        def _():
            dg_ref[...] = jnp.zeros_like(dg_ref)

        dh = _dot_nt(da_ref[...], w_ref[...])
        dx, dg = _rms_bwd(dh, x_ref[...], g_ref[...])
        dg_ref[...] += dg
        dx1_ref[...] = dx2_ref[...] + dx

    return _pcall(body, name="bwd_mlp_up", grid=(T // tm,),
                  in_specs=[_rows(tm, D_FF), _full((D, D_FF)), _rows(tm, D), _full((1, D)), _rows(tm, D)],
                  out_specs=[_rows(tm, D), _full((1, D))],
                  out_shape=[SDS((T, D), F32), SDS((1, D), F32)], sem=("arbitrary",))(da, w1, x1, g3, dx2)


def _bwd_mix(dx1, yo, g2, wo, z, au, bu, wmu, wsu, om, tm):
    T = dx1.shape[0]

    def body(dx_ref, yo_ref, g_ref, wo_ref, ga_ref, gb_ref, au_ref, bu_ref, wmu_ref, wsu_ref, om_ref,
             dyo_ref, dg_ref, dau_ref, dbu_ref, dga_ref, dgb_ref, dos_ref, dl_ref, dot_ref):
        @pl.when(pl.program_id(0) == 0)
        def _():
            dg_ref[...] = jnp.zeros_like(dg_ref)

        dyo, dg = _rms_bwd(dx_ref[...], yo_ref[...], g_ref[...])
        dg_ref[...] += dg
        dyob = dyo.astype(BF16)
        dyo_ref[...] = dyob
        dy = _dot_nt(dyob, wo_ref[...])
        sa = _sigmoid(ga_ref[...])
        sb = _sigmoid(gb_ref[...])
        dau = (dy * sa).astype(BF16)
        dbu = (dy * sb).astype(BF16)
        dau_ref[...] = dau
        dbu_ref[...] = dbu
        dga_ref[...] = (dy * au_ref[...] * sa * (1.0 - sa)).astype(BF16)
        dgb_ref[...] = (dy * bu_ref[...] * sb * (1.0 - sb)).astype(BF16)
        dom = _dot_nt(dau, wmu_ref[...])
        dos_ref[...] = _dot_nt(dbu, wsu_ref[...])
        prod = dom * om_ref[...]
        sub = lax.broadcasted_iota(jnp.int32, (8, tm), 0)
        for pr in range(MLA_HEADS // 2):
            sl = slice(LANES * pr, LANES * (pr + 1))
            pt = prod[:, sl].T
            d0 = jnp.sum(pt[0:64], axis=0, keepdims=True)
            d1 = jnp.sum(pt[64:128], axis=0, keepdims=True)
            dl_ref[pr, 0] = jnp.where(sub == 0, d0, jnp.where(sub == 1, d1, 0.0))
            dot_ref[0, sl, :] = dom[:, sl].T.astype(BF16)

    r = _rows(tm, D)
    w = _full((D, D))
    return _pcall(body, name="bwd_mix", grid=(T // tm,),
                  in_specs=[r, r, _full((1, D)), w, _rows(tm, D, 1), _rows(tm, D, 2), r, r, w, w, r],
                  out_specs=[r, _full((1, D)), r, r, r, r, r, pl.BlockSpec((MLA_HEADS // 2, 1, 8, tm), lambda i: (0, i, 0, 0)),
                             pl.BlockSpec((1, D, tm), lambda i: (i, 0, 0))],
                  out_shape=[SDS((T, D), BF16), SDS((1, D), F32), SDS((T, D), BF16), SDS((T, D), BF16), SDS((T, D), BF16),
                             SDS((T, D), BF16), SDS((T, D), F32), SDS((MLA_HEADS // 2, T // tm, 8, tm), F32),
                             SDS((T // tm, D, tm), BF16)],
                  sem=("arbitrary",))(dx1, yo, g2, wo, z, z, au, bu, wmu, wsu, om)


def _mla_bwd(qm, qt, km, kt, vm, dot, lse, delta, tb):
    T = qm.shape[0]
    nb = T // tb
    cc = ATT_COLS

    def body(q_ref, qt_ref, k_ref, kt_ref, v_ref, dot_ref, l_ref, dl_ref, dqt_ref, dkt_ref, dvt_ref,
             s_ref, dp_ref, p_ref, ds_ref, vh_ref):
        j = pl.program_id(1)

        @pl.when(j == 0)
        def _():
            dqt_ref[...] = jnp.zeros_like(dqt_ref)

        dkt_ref[...] = jnp.zeros_like(dkt_ref)
        dvt_ref[...] = jnp.zeros_like(dvt_ref)
        lo = lax.broadcasted_iota(jnp.int32, (tb, LANES), 1) < 64
        key = lax.broadcasted_iota(jnp.int32, (tb, cc), 0)
        qry = lax.broadcasted_iota(jnp.int32, (tb, cc), 1)
        v = v_ref[...]
        vh_ref[0] = jnp.where(lo, v, jnp.zeros_like(v))
        vh_ref[1] = jnp.where(lo, jnp.zeros_like(v), v)

        def scores(i, slot):
            rows_i = pl.ds(pl.multiple_of(i * tb, tb), tb)
            for hh in range(2):
                sl = slice(LANES * hh, LANES * (hh + 1))
                s_ref[slot, hh] = _dot_nt(k_ref[:, sl], q_ref[rows_i, sl])
                dp_ref[slot, hh] = _dot(vh_ref[hh], dot_ref[i])

        def grads(i, slot, diagonal):
            lse_i = l_ref[0, i]
            delta_i = dl_ref[0, i]
            for hh in range(2):
                for c in range(tb // cc):
                    cols = slice(cc * c, cc * (c + 1))
                    p = jnp.exp2(s_ref[slot, hh, :, cols] * MLA_LOG2_SCALE - lse_i[hh:hh + 1, cols])
                    if diagonal:
                        p = jnp.where(key <= qry + cc * c, p, 0.0)
                    p_ref[hh, :, cols] = p.astype(BF16)
                    ds_ref[hh, :, cols] = (p * (dp_ref[slot, hh, :, cols] - delta_i[hh:hh + 1, cols]) * MLA_SCALE).astype(BF16)
            for hh in range(2):
                sl = slice(LANES * hh, LANES * (hh + 1))
                half = slice(64 * hh, 64 * (hh + 1))
                dvt_ref[0, half, :] += _dot_nt(dot_ref[i, half, :], p_ref[hh])
                dkt_ref[0, sl, :] += _dot_nt(qt_ref[i, sl, :], ds_ref[hh])
                dqt_ref[i, sl, :] += _dot(kt_ref[0, sl, :], ds_ref[hh])

        n_off = nb - 1 - j

        def step(u, carry):
            i0 = j + 1 + 2 * u
            scores(i0 + 1, 1)
            grads(i0, 0, False)
            scores(jnp.where(i0 + 2 < nb, i0 + 2, j), 0)
            grads(i0 + 1, 1, False)
            return carry

        scores(jnp.where(n_off > 0, j + 1, j), 0)
        lax.fori_loop(0, n_off // 2, step, 0)

        @pl.when(n_off % 2 == 1)
        def _():
            scores(j, 1)
            grads(nb - 1, 0, False)
            grads(j, 1, True)

        @pl.when(n_off % 2 == 0)
        def _():
            grads(j, 0, True)

    blk = lambda w: pl.BlockSpec((tb, w), lambda p, j: (j, p))
    stat = pl.BlockSpec((1, nb, 8, tb), lambda p, j: (p, 0, 0, 0))
    pair_t = lambda w: pl.BlockSpec((nb, w, tb), lambda p, j: (0, p, 0))
    blk_t = lambda w: pl.BlockSpec((1, w, tb), lambda p, j: (j, p, 0))
    return _pcall(body, name="mla_bwd", grid=(MLA_HEADS // 2, nb),
                  in_specs=[pl.BlockSpec((T, 256), lambda p, j: (0, p)), pair_t(256), blk(256), blk_t(256), blk(LANES),
                            pair_t(LANES), stat, stat],
                  out_specs=[pair_t(256), blk_t(256), blk_t(LANES)],
                  out_shape=[SDS((nb, 2048, tb), F32), SDS((nb, 2048, tb), F32), SDS((nb, D, tb), F32)],
                  scratch=[pltpu.VMEM((2, 2, tb, tb), F32), pltpu.VMEM((2, 2, tb, tb), F32), pltpu.VMEM((2, tb, tb), BF16),
                           pltpu.VMEM((2, tb, tb), BF16), pltpu.VMEM((2, tb, LANES), BF16)],
                  sem=("parallel", "arbitrary"))(qm, qt, km, kt, vm, dot, lse, delta)


def _swa_bwd(sinks, qs, ks, vs, do, o, lse):
    T = qs.shape[0]
    nb, cur, prev = _swa_specs(T)

    def body(sink_ref, q_ref, kc_ref, kp_ref, vc_ref, vp_ref, do_ref, o_ref, l_ref,
             dq_ref, dkc_ref, dkp_ref, dvc_ref, dvp_ref, dsink_ref):
        n = pl.program_id(0)

        @pl.when(n == 0)
        def _():
            dsink_ref[...] = jnp.zeros_like(dsink_ref)

        mask = _swa_mask(n)
        lo = lax.broadcasted_iota(jnp.int32, (WINDOW, LANES), 1) < 64
        lane8 = lax.broadcasted_iota(jnp.int32, (8, LANES), 1)
        dsink = jnp.zeros((8, LANES), F32)
        for g in range(2):
            gs = slice(LANES * g, LANES * (g + 1))
            kb = jnp.concatenate([kp_ref[:, gs], kc_ref[:, gs]], axis=0)
            vb = jnp.concatenate([vp_ref[:, gs], vc_ref[:, gs]], axis=0)
            dkb = jnp.zeros((2 * WINDOW, LANES), F32)
            dvb = jnp.zeros((2 * WINDOW, LANES), F32)
            for jj in range(4):
                j = 4 * g + jj
                sl = slice(LANES * j, LANES * (j + 1))
                qp = q_ref[:, sl]
                d_o = do_ref[:, sl]
                prod = d_o * o_ref[:, sl]
                lse_b = l_ref[:, sl]
                dqs = []
                for hf in range(2):
                    hm = lo if hf == 0 else jnp.logical_not(lo)
                    qh = jnp.where(hm, qp, jnp.zeros_like(qp))
                    s = jnp.where(mask, _dot_nt(qh, kb) * SWA_SCALE, NEG)
                    lse_h = jnp.max(jnp.where(hm, lse_b, -jnp.inf), axis=1, keepdims=True)
                    p = jnp.exp(s - lse_h)
                    dom = jnp.where(hm, d_o, 0.0).astype(BF16)
                    dp = _dot_nt(dom, vb)
                    delta = jnp.sum(jnp.where(hm, prod, 0.0), axis=1, keepdims=True)
                    ds = (p * (dp - delta) * SWA_SCALE).astype(BF16)
                    p_sink = jnp.exp(sink_ref[2 * j + hf] - lse_h)
                    d_sink = -jnp.sum(p_sink * delta, axis=0, keepdims=True)
                    dsink = dsink + jnp.where(lane8 == 2 * j + hf, d_sink, 0.0)
                    dvb = dvb + _dot_tn(p.astype(BF16), dom)
                    dkb = dkb + _dot_tn(ds, qh)
                    dqs.append(_dot(ds, kb))
                dq_ref[:, sl] = jnp.where(lo, dqs[0], dqs[1])
            dkp_ref[:, gs] = dkb[:WINDOW]
            dkc_ref[:, gs] = dkb[WINDOW:]
            dvp_ref[:, gs] = dvb[:WINDOW]
            dvc_ref[:, gs] = dvb[WINDOW:]
        dsink_ref[...] += dsink

    return _pcall(body, name="swa_bwd", grid=(nb,),
                  in_specs=[pl.BlockSpec(memory_space=pltpu.SMEM), cur(D), cur(256), prev(256), cur(256), prev(256),
                            cur(D), cur(D), cur(D)],
                  out_specs=[cur(D), cur(256), cur(256), cur(256), cur(256), _full((8, LANES))],
                  out_shape=[SDS((T, D), F32), SDS((T, 256), F32), SDS((T, 256), F32), SDS((T, 256), F32), SDS((T, 256), F32),
                             SDS((8, LANES), F32)],
                  sem=("arbitrary",))(sinks, qs, ks, ks, vs, vs, do, o, lse)


def _bwd_qkv(dqm, dkm, dvm, dqs, dkc, dkp, dvc, dvp, z, gq, gkv, wqb, wkn, wv, tab_m, tab_s):
    T = z.shape[0]
    tm = WINDOW
    nb = T // tm
    per = dqm.shape[2] // tm

    def body(dqm_ref, dkm_ref, dvm_ref, dqs_ref, dkc_ref, dkp_ref, dvc_ref, dvp_ref, qa_ref, kva_ref, gq_ref, gkv_ref,
             wqb_ref, wkn_ref, wv_ref, cm_ref, am_ref, bm_ref, cs_ref, as_ref, bs_ref,
             dq_out, dkn_out, dv_out, dsq_ref, drest_ref, dgq_ref, dgkv_ref):
        i = pl.program_id(0)

        @pl.when(i == 0)
        def _():
            dgq_ref[...] = jnp.zeros_like(dgq_ref)
            dgkv_ref[...] = jnp.zeros_like(dgkv_ref)

        cm, am, bm = cm_ref[...], -am_ref[...], -bm_ref[...]
        cs, as_, bs = cs_ref[...], -as_ref[...], -bs_ref[...]
        lane = lax.broadcasted_iota(jnp.int32, (tm, LANES), 1)
        nope = lane < MLA_NOPE
        roped = jnp.logical_and(lane >= MLA_NOPE, lane < MLA_NOPE + MLA_ROPE)
        dkr = jnp.zeros((tm, LANES), F32)
        for h in range(MLA_HEADS):
            sl = slice(LANES * h, LANES * (h + 1))
            dq_out[:, sl] = _rope(dqm_ref[0, sl, :].T, cm, am, bm, MLA_ROPE // 2).astype(BF16)
            dk_h = dkm_ref[0, sl, :].T
            dkn_out[:, sl] = jnp.where(nope, dk_h, 0.0).astype(BF16)
            dkr = dkr + jnp.where(roped, dk_h, 0.0)
        for j in range(D // LANES):
            sl = slice(LANES * j, LANES * (j + 1))
            dv_out[:, sl] = dvm_ref[0, sl, :].T.astype(BF16)
        dqn = _dot_nt(dq_out[...], wqb_ref[...])
        dkvn = _dot_nt(dkn_out[...], wkn_ref[...]) + _dot_nt(dv_out[...], wv_ref[...])
        dqa, dgq = _rms_bwd(dqn, qa_ref[...], gq_ref[...])
        dkva, dgkv = _rms_bwd(dkvn, kva_ref[...], gkv_ref[...])
        dgq_ref[...] += dgq
        dgkv_ref[...] += dgkv
        for j in range(D // LANES):
            sl = slice(LANES * j, LANES * (j + 1))
            dsq_ref[:, sl] = _rope(dqs_ref[:, sl], cs, as_, bs, SWA_HD // 2).astype(BF16)
        keep = (i < nb - 1).astype(F32)
        drest_ref[:, 0:256] = dqa.astype(BF16)
        for j in range(2):
            sl = slice(LANES * j, LANES * (j + 1))
            dk = dkc_ref[:, sl] + keep * dkp_ref[:, sl]
            drest_ref[:, 256 + LANES * j:256 + LANES * (j + 1)] = _rope(dk, cs, as_, bs, SWA_HD // 2).astype(BF16)
        drest_ref[:, 512:768] = (dvc_ref[...] + keep * dvp_ref[...]).astype(BF16)
        drest_ref[:, 768:896] = dkva.astype(BF16)
        drest_ref[:, 896:1024] = _rope(dkr, cm, am, bm, MLA_ROPE // 2).astype(BF16)

    nxt = pl.BlockSpec((tm, 256), lambda i: (jnp.minimum(i + 1, nb - 1), 0))
    tab = [_rows(tm, LANES)] * 6
    return _pcall(body, name="bwd_qkv", grid=(nb,),
                  in_specs=[pl.BlockSpec((1, 2048, tm), lambda i: (i // per, 0, i % per)),
                            pl.BlockSpec((1, 2048, tm), lambda i: (i // per, 0, i % per)),
                            pl.BlockSpec((1, 1024, tm), lambda i: (i // per, 0, i % per)), _rows(tm, 1024), _rows(tm, 256), nxt,
                            _rows(tm, 256), nxt, _rows(tm, 256, 12), _rows(tm, 128, 30), _full((1, Q_LORA)), _full((1, KV_LORA)),
                            _full((Q_LORA, 2048)), _full((KV_LORA, 2048)), _full((KV_LORA, 1024))] + tab,
                  out_specs=[_rows(tm, 2048), _rows(tm, 2048), _rows(tm, 1024), _rows(tm, 1024), _rows(tm, 1024),
                             _full((1, Q_LORA)), _full((1, KV_LORA))],
                  out_shape=[SDS((T, 2048), BF16), SDS((T, 2048), BF16), SDS((T, 1024), BF16), SDS((T, 1024), BF16),
                             SDS((T, 1024), BF16), SDS((1, Q_LORA), F32), SDS((1, KV_LORA), F32)],
                  sem=("arbitrary",))(dqm, dkm, dvm, dqs, dkc, dkp, dvc, dvp, z, z, gq, gkv, wqb, wkn, wv, *tab_m, *tab_s)


def _bwd_in(dsq, dga, dgb, drest, w_in_p, x, g1, dx1, tm):
    T = x.shape[0]

    def body(a_ref, b_ref, c_ref, d_ref, w_ref, x_ref, g_ref, dx1_ref, dx_ref, dg_ref):
        @pl.when(pl.program_id(0) == 0)
        def _():
            dg_ref[...] = jnp.zeros_like(dg_ref)

        dh = (_dot_nt(a_ref[...], w_ref[:, 0:1024]) + _dot_nt(b_ref[...], w_ref[:, 1024:2048])
              + _dot_nt(c_ref[...], w_ref[:, 2048:3072]) + _dot_nt(d_ref[...], w_ref[:, 3072:4096]))
        dx, dg = _rms_bwd(dh, x_ref[...], g_ref[...])
        dg_ref[...] += dg
        dx_ref[...] = dx1_ref[...] + dx

    r = _rows(tm, D)
    return _pcall(body, name="bwd_in", grid=(T // tm,),
                  in_specs=[r, r, r, r, _full((D, NZ)), r, _full((1, D)), r],
                  out_specs=[r, _full((1, D))], out_shape=[SDS((T, D), F32), SDS((1, D), F32)],
                  sem=("arbitrary",))(dsq, dga, dgb, drest, w_in_p, x, g1, dx1)


def _wgrad(a, g, name, into=None):
    T, K = a.shape
    N = g.shape[1]
    tk, tn, tt = min(K, 1024), min(N, 1024), min(T, 1024)
    if into is not None:
        buf, weight = into
        _, row0, lane0 = PACK_AT[weight]
        shard = {n: (r, c) for n, r, c in BIG}[weight]
        assert lane0 == 0 and shard[1] == D
        tk = shard[0]
    assert K % tk == 0 and N % tn == 0 and T % tt == 0, (a.shape, g.shape)
    steps = T // tt

    def body(a_ref, g_ref, *rest):
        o_ref, acc_ref = rest[-2:]
        t = pl.program_id(2)

        @pl.when(t == 0)
        def _():
            acc_ref[...] = jnp.zeros_like(acc_ref)

        acc_ref[...] += _dot_tn(a_ref[...].astype(BF16), g_ref[...].astype(BF16))

        @pl.when(t == steps - 1)
        def _():
            o_ref[...] = acc_ref[...].astype(o_ref.dtype).reshape(o_ref.shape)

    in_specs = [pl.BlockSpec((tt, tk), lambda k, n, t: (t, k)), pl.BlockSpec((tt, tn), lambda k, n, t: (t, n))]
    if into is None:
        return _pcall(body, name=name, grid=(K // tk, N // tn, steps), in_specs=in_specs,
                      out_specs=pl.BlockSpec((tk, tn), lambda k, n, t: (k, n)), out_shape=SDS((K, N), F32),
                      scratch=[pltpu.VMEM((tk, tn), F32)], sem=("parallel", "parallel", "arbitrary"))(a, g)
    assert row0 % tk == 0 and (K // tk) * (N // tn) == N_CHIPS
    return _pcall(body, name=name, grid=(K // tk, N // tn, steps), in_specs=in_specs + [ANY],
                  out_specs=pl.BlockSpec((1, tk, tn), lambda k, n, t: (k + n, row0 // tk, 0)), out_shape=SDS(buf.shape, buf.dtype),
                  scratch=[pltpu.VMEM((tk, tn), F32)], sem=("parallel", "parallel", "arbitrary"), aliases={2: 0})(a, g, buf)


def _adamw(w, packed_g, m, v, name):
    _, R, C = w.shape
    _, row0, lane0 = PACK_AT[name]
    tr = min(R, 256 if row0 % 256 == 0 else 128)
    assert row0 % tr == 0 and R % tr == 0

    def body(w_ref, g_ref, m_ref, v_ref, go_ref, d_ref, m2_ref, v2_ref):
        g_ = g_ref[:, lane0:lane0 + C]
        go_ref[0] = g_
        m2 = ADAM_B1 * m_ref[0] + (1.0 - ADAM_B1) * g_
        v2 = ADAM_B2 * v_ref[0] + (1.0 - ADAM_B2) * jnp.square(g_)
        m_hat = m2 / (1.0 - ADAM_B1 ** ADAM_STEP)
        v_hat = v2 / (1.0 - ADAM_B2 ** ADAM_STEP)
        d_ref[0] = -ADAM_LR * (m_hat / (jnp.sqrt(v_hat) + ADAM_EPS) + ADAM_WD * w_ref[0])
        m2_ref[0] = m2
        v2_ref[0] = v2

    r = pl.BlockSpec((1, tr, C), lambda i: (0, i, 0))
    return _pcall(body, name="adamw_" + name, grid=(R // tr,),
                  in_specs=[r, pl.BlockSpec((tr, D), lambda i: (row0 // tr + i, 0)), r, r], out_specs=[r] * 4,
                  out_shape=[SDS((1, R, C), F32)] * 4, sem=("parallel",))(w, packed_g, m, v)


def _adamw_small(w, parts, m, v):
    def body(w_ref, p_ref, m_ref, v_ref, g_ref, d_ref, m2_ref, v2_ref):
        g_ = p_ref[0]
        for k in range(1, N_DEV):
            g_ = g_ + p_ref[k]
        g_ref[...] = g_
        m2 = ADAM_B1 * m_ref[...] + (1.0 - ADAM_B1) * g_
        v2 = ADAM_B2 * v_ref[...] + (1.0 - ADAM_B2) * jnp.square(g_)
        m_hat = m2 / (1.0 - ADAM_B1 ** ADAM_STEP)
        v_hat = v2 / (1.0 - ADAM_B2 ** ADAM_STEP)
        d_ref[...] = -ADAM_LR * (m_hat / (jnp.sqrt(v_hat) + ADAM_EPS) + ADAM_WD * w_ref[...])
        m2_ref[...] = m2
        v2_ref[...] = v2

    s = _full((8, D))
    return _pcall(body, name="adamw_small", grid=(1,), in_specs=[s, _full((N_DEV, 8, D)), s, s], out_specs=[s] * 4,
                  out_shape=[SDS((8, D), F32)] * 4, sem=("arbitrary",))(w, parts, m, v)


ANY = pl.BlockSpec(memory_space=pl.ANY)


def _place():
    x, y, c = lax.axis_index("x"), lax.axis_index("y"), lax.axis_index("c")
    chips = [(1 - x, y), (x, 1 - y), (1 - x, 1 - y)]
    return x, y, c, chips


def _all_gather(wpk):
    rows = wpk.shape[0]
    HALF = rows // 2
    assert HALF % 16 == 0

    def body(in_ref, out_ref, send_sems, recv_sems):
        x, y, c, chips = _place()
        half = pl.ds(pl.multiple_of(c * HALF, 16), HALF)
        other = pl.ds(pl.multiple_of((1 - c) * HALF, 16), HALF)

        def copy(k, src, dst, to):
            return pltpu.make_async_remote_copy(src_ref=src, dst_ref=dst, send_sem=send_sems.at[k], recv_sem=recv_sems.at[k],
                                                device_id=to, device_id_type=MESH)

        first = [copy(k, in_ref.at[half], out_ref.at[2 * x + y, half], (cx, cy, c)) for k, (cx, cy) in enumerate(chips)]
        for cp in first:
            cp.start()
        passed = []
        for k, (cx, cy) in enumerate(chips):
            slot = out_ref.at[2 * cx + cy, half]
            copy(k, slot, slot, (x, y, c)).wait_recv()
            fwd = copy(3 + k, slot, slot, (x, y, 1 - c))
            fwd.start()
            passed.append(fwd)
        for k, (cx, cy) in enumerate(chips):
            slot = out_ref.at[2 * cx + cy, other]
            copy(3 + k, slot, slot, (x, y, c)).wait_recv()
        for cp in first + passed:
            cp.wait_send()

    return _pcall(body, name="all_gather_weights", in_specs=[ANY], out_specs=ANY,
                  out_shape=SDS((N_CHIPS, rows, D), BF16),
                  scratch=[pltpu.SemaphoreType.DMA((6,)), pltpu.SemaphoreType.DMA((6,))])(wpk)


HBM = pl.BlockSpec(memory_space=pltpu.HBM)
SEM = pl.BlockSpec(memory_space=pltpu.SEMAPHORE)
DATAFLOW = pltpu.SideEffectType.DATAFLOW_SIDE_EFFECTING


def _in_hbm(a):
    return pltpu.with_memory_space_constraint(a, pltpu.HBM)


def _gather_late_start(wpk, after):
    rows = wpk.shape[0]

    def body(in_ref, land_ref, after_ref, send_sems, recv_sems, in_thru, land_thru, token):
        x, y, c, chips = _place()
        for k, (cx, cy) in enumerate(chips):
            pltpu.make_async_remote_copy(src_ref=in_ref, dst_ref=land_ref.at[2 * x + y], send_sem=send_sems.at[k],
                                         recv_sem=recv_sems.at[k], device_id=(cx, cy, c), device_id_type=MESH).start()
        token[...] = jnp.zeros_like(token)

    return pl.pallas_call(
        body, name="gather_late_start",
        out_shape=(pltpu.SemaphoreType.DMA((3,)), pltpu.SemaphoreType.DMA((3,)), pltpu.HBM(wpk.shape, wpk.dtype),
                   pltpu.HBM((N_CHIPS, rows, D), wpk.dtype), SDS((8, LANES), F32)),
        in_specs=(HBM, HBM, ANY), out_specs=(SEM, SEM, HBM, HBM, pl.BlockSpec(memory_space=pltpu.VMEM)),
        input_output_aliases={0: 2, 1: 3}, compiler_params=pltpu.CompilerParams(has_side_effects=DATAFLOW),
    )(_in_hbm(wpk), _in_hbm(lax.empty((N_CHIPS, rows, D), wpk.dtype)), after)


def _gather_late_wait(send_sems, recv_sems, in_thru, land_thru, after):
    def body(in_ref, land_ref, send_sems, recv_sems, after_ref, after2_ref, in_dead, got_ref):
        x, y, c, chips = _place()
        for k, (cx, cy) in enumerate(chips):
            cp = pltpu.make_async_remote_copy(src_ref=in_ref, dst_ref=land_ref.at[2 * cx + cy], send_sem=send_sems.at[k],
                                              recv_sem=recv_sems.at[k], device_id=(cx, cy, c), device_id_type=MESH)
            cp.wait_send()
            cp.wait_recv()

    return pl.pallas_call(
        body, name="gather_late_wait",
        out_shape=(pltpu.HBM(in_thru.shape, in_thru.dtype), pltpu.HBM(land_thru.shape, land_thru.dtype)),
        in_specs=(HBM, HBM, SEM, SEM, ANY, ANY), out_specs=(HBM, HBM), input_output_aliases={0: 0, 1: 1},
        compiler_params=pltpu.CompilerParams(has_side_effects=DATAFLOW),
    )(in_thru, land_thru, send_sems, recv_sems, *after)[1]


def _rs_sibling(gpk):
    HALF = gpk.shape[1] // 2

    def body(in_ref, out_ref, send_sem, recv_sem):
        x, y, c, _ = _place()
        theirs = pl.ds(pl.multiple_of((1 - c) * HALF, 8), HALF)
        cp = pltpu.make_async_remote_copy(src_ref=in_ref.at[:, theirs], dst_ref=out_ref, send_sem=send_sem, recv_sem=recv_sem,
                                          device_id=(x, y, 1 - c), device_id_type=MESH)
        cp.start()
        cp.wait()

    return _pcall(body, name="rs_sibling", in_specs=[ANY], out_specs=ANY, out_shape=SDS((N_CHIPS, HALF, D), F32),
                  scratch=[pltpu.SemaphoreType.DMA, pltpu.SemaphoreType.DMA])(gpk)


def _rs_add_sibling(cidx, gpk, got):
    HALF = got.shape[1]
    th = HALF // 4
    nh = HALF // th
    assert th % 16 == 0

    def body(c_ref, a_ref, b_ref, o_ref):
        o_ref[...] = (a_ref[...] + b_ref[...]).astype(BF16)

    gs = pltpu.PrefetchScalarGridSpec(
        num_scalar_prefetch=1, grid=(N_CHIPS, nh),
        in_specs=[pl.BlockSpec((1, th, D), lambda j, i, c: (j, c[0] * nh + i, 0)), pl.BlockSpec((1, th, D), lambda j, i, c: (j, i, 0))],
        out_specs=pl.BlockSpec((1, th, D), lambda j, i, c: (j, i, 0)))
    return pl.pallas_call(body, name="rs_add_sibling", grid_spec=gs, out_shape=SDS((N_CHIPS, HALF, D), BF16),
                          compiler_params=pltpu.CompilerParams(dimension_semantics=("parallel", "parallel"),
                                                               vmem_limit_bytes=48 << 20))(cidx, gpk, got)


def _rs_chips(part, small):
    def body(p_ref, s_ref, o_ref, so_ref, send_sems, recv_sems, ssend_sems, srecv_sems, local_sem):
        x, y, c, chips = _place()
        me = 2 * x + y
        mine_s = pltpu.make_async_copy(s_ref, so_ref.at[4 * x + 2 * y + c], local_sem)
        mine_s.start()
        sends = []
        for k, (cx, cy) in enumerate(chips):
            sends.append(pltpu.make_async_remote_copy(src_ref=p_ref.at[2 * cx + cy], dst_ref=o_ref.at[me], send_sem=send_sems.at[k],
                                                      recv_sem=recv_sems.at[k], device_id=(cx, cy, c), device_id_type=MESH))
        peers = [(x, y, 1 - c)] + [(cx, cy, c) for cx, cy in chips] + [(cx, cy, 1 - c) for cx, cy in chips]
        for k, to in enumerate(peers):
            sends.append(pltpu.make_async_remote_copy(src_ref=s_ref, dst_ref=so_ref.at[4 * x + 2 * y + c], send_sem=ssend_sems.at[k],
                                                      recv_sem=srecv_sems.at[k], device_id=to, device_id_type=MESH))
        for cp in sends:
            cp.start()
        for k, (cx, cy) in enumerate(chips):
            slot = o_ref.at[2 * cx + cy]
            pltpu.make_async_remote_copy(src_ref=slot, dst_ref=slot, send_sem=send_sems.at[k], recv_sem=recv_sems.at[k],
                                         device_id=(x, y, c), device_id_type=MESH).wait_recv()
        for k, (px, py, pc) in enumerate(peers):
            slot = so_ref.at[4 * px + 2 * py + pc]
            pltpu.make_async_remote_copy(src_ref=slot, dst_ref=slot, send_sem=ssend_sems.at[k], recv_sem=srecv_sems.at[k],
                                         device_id=(x, y, c), device_id_type=MESH).wait_recv()
        for cp in sends:
            cp.wait_send()
        mine_s.wait()

    return _pcall(body, name="rs_chips", in_specs=[ANY, ANY], out_specs=[ANY, ANY],
                  out_shape=[SDS(part.shape, part.dtype), SDS((N_DEV, 8, D), F32)],
                  scratch=[pltpu.SemaphoreType.DMA((3,)), pltpu.SemaphoreType.DMA((3,)), pltpu.SemaphoreType.DMA((7,)),
                           pltpu.SemaphoreType.DMA((7,)), pltpu.SemaphoreType.DMA])(part, small)


def _rs_add_chips(qidx, part, parts):
    HALF = part.shape[1]
    th = HALF // 4
    assert th % 16 == 0

    def body(q_ref, own_ref, p_ref, o_ref):
        for me in range(N_CHIPS):
            @pl.when(q_ref[0] == me)
            def _(me=me):
                t = [(own_ref[0] if j == me else p_ref[j]).astype(F32) for j in range(N_CHIPS)]
                o_ref[...] = ((t[0] + t[1]) + t[2]) + t[3]

    gs = pltpu.PrefetchScalarGridSpec(
        num_scalar_prefetch=1, grid=(HALF // th,),
        in_specs=[pl.BlockSpec((1, th, D), lambda i, q: (q[0], i, 0)), pl.BlockSpec((N_CHIPS, th, D), lambda i, q: (0, i, 0))],
        out_specs=pl.BlockSpec((th, D), lambda i, q: (i, 0)))
    return pl.pallas_call(body, name="rs_add_chips", grid_spec=gs, out_shape=SDS((HALF, D), F32),
                          compiler_params=pltpu.CompilerParams(dimension_semantics=("parallel",),
                                                               vmem_limit_bytes=48 << 20))(qidx, part, parts)


def _rs_join(early, late):
    def body(e_ref, l_ref, eo_ref, lo_ref, send_sems, recv_sems):
        x, y, c, _ = _place()
        cps = [pltpu.make_async_remote_copy(src_ref=src, dst_ref=dst, send_sem=send_sems.at[k], recv_sem=recv_sems.at[k],
                                            device_id=(x, y, 1 - c), device_id_type=MESH)
               for k, (src, dst) in enumerate([(e_ref, eo_ref), (l_ref, lo_ref)])]
        for cp in cps:
            cp.start()
        for cp in cps:
            cp.wait()

    return _pcall(body, name="rs_join", in_specs=[ANY, ANY], out_specs=[ANY, ANY],
                  out_shape=[SDS(early.shape, F32), SDS(late.shape, F32)],
                  scratch=[pltpu.SemaphoreType.DMA((2,)), pltpu.SemaphoreType.DMA((2,))])(early, late)


def _reduce_late_start(gpk, after):
    rows = gpk.shape[1]
    HALF = rows // 2
    assert HALF % 16 == 0

    def body(in_ref, land_ref, after_ref, send_sems, recv_sems, in_thru, land_thru, token):
        x, y, c, chips = _place()
        me = 4 * x + 2 * y + c
        peers = [(x, y, 1 - c)] + [(cx, cy, c) for cx, cy in chips] + [(cx, cy, 1 - c) for cx, cy in chips]
        for k, (px, py, pc) in enumerate(peers):
            src = in_ref.at[2 * px + py, pl.ds(pl.multiple_of(pc * HALF, 16), HALF)]
            pltpu.make_async_remote_copy(src_ref=src, dst_ref=land_ref.at[me], send_sem=send_sems.at[k], recv_sem=recv_sems.at[k],
                                         device_id=(px, py, pc), device_id_type=MESH).start()
        token[...] = jnp.zeros_like(token)

    return pl.pallas_call(
        body, name="reduce_late_start",
        out_shape=(pltpu.SemaphoreType.DMA((7,)), pltpu.SemaphoreType.DMA((7,)), pltpu.HBM(gpk.shape, gpk.dtype),
                   pltpu.HBM((N_DEV, HALF, D), gpk.dtype), SDS((8, LANES), F32)),
        in_specs=(HBM, HBM, ANY), out_specs=(SEM, SEM, HBM, HBM, pl.BlockSpec(memory_space=pltpu.VMEM)),
        input_output_aliases={0: 2, 1: 3}, compiler_params=pltpu.CompilerParams(has_side_effects=DATAFLOW),
    )(_in_hbm(gpk), _in_hbm(lax.empty((N_DEV, HALF, D), gpk.dtype)), after)


def _reduce_late_wait(send_sems, recv_sems, in_thru, land_thru, after):
    def body(in_ref, land_ref, send_sems, recv_sems, after_ref, in_out, got_ref):
        x, y, c, chips = _place()
        peers = [(x, y, 1 - c)] + [(cx, cy, c) for cx, cy in chips] + [(cx, cy, 1 - c) for cx, cy in chips]
        for k, (px, py, pc) in enumerate(peers):
            cp = pltpu.make_async_remote_copy(src_ref=land_ref.at[0], dst_ref=land_ref.at[4 * px + 2 * py + pc],
                                              send_sem=send_sems.at[k], recv_sem=recv_sems.at[k],
                                              device_id=(px, py, pc), device_id_type=MESH)
            cp.wait_send()
            cp.wait_recv()

    return pl.pallas_call(
        body, name="reduce_late_wait",
        out_shape=(pltpu.HBM(in_thru.shape, in_thru.dtype), pltpu.HBM(land_thru.shape, land_thru.dtype)),
        in_specs=(HBM, HBM, SEM, SEM, ANY), out_specs=(HBM, HBM), input_output_aliases={0: 0, 1: 1},
        compiler_params=pltpu.CompilerParams(has_side_effects=DATAFLOW),
    )(in_thru, land_thru, send_sems, recv_sems, after)


def _reduce_late_add(didx, gpk, parts):
    HALF = parts.shape[1]
    th = HALF // 4
    nh = HALF // th
    assert th % 16 == 0

    def body(d_ref, own_ref, p_ref, o_ref):
        for me in range(N_DEV):
            @pl.when(d_ref[0] == me)
            def _(me=me):
                t = [(own_ref[0] if j == me else p_ref[j]).astype(F32) for j in range(N_DEV)]
                o_ref[...] = ((((((t[0] + t[1]) + t[2]) + t[3]) + t[4]) + t[5]) + t[6]) + t[7]

    gs = pltpu.PrefetchScalarGridSpec(
        num_scalar_prefetch=1, grid=(nh,),
        in_specs=[pl.BlockSpec((1, th, D), lambda i, d: (d[1], d[2] * nh + i, 0)), pl.BlockSpec((N_DEV, th, D), lambda i, d: (0, i, 0))],
        out_specs=pl.BlockSpec((th, D), lambda i, d: (i, 0)))
    return pl.pallas_call(body, name="reduce_late_add", grid_spec=gs, out_shape=SDS((HALF, D), F32),
                          compiler_params=pltpu.CompilerParams(dimension_semantics=("parallel",),
                                                               vmem_limit_bytes=48 << 20))(didx, gpk, parts)


def _pack_early(b, dtype):
    lanes = lambda a: jnp.pad(a.astype(dtype), ((0, 0), (0, D - a.shape[1])))
    pair = jnp.concatenate([b["w_q_b"].astype(dtype), b["w_ple"].astype(dtype), jnp.zeros((256, D - 640), dtype)], axis=1)
    return jnp.concatenate([lanes(b["w_in"]), pair, lanes(b["w_kv_b"])], axis=0)


def _pack_late(b, dtype):
    return jnp.concatenate([b[n].astype(dtype) for n in ("w_mla_up", "w_swa_up", "w_out", "w_ple_gate", "w_mlp_up", "w_mlp_down")],
                           axis=0)


def _unpack_shards(pk, which):
    return {n: pk[PACK_AT[n][1]:PACK_AT[n][1] + r, PACK_AT[n][2]:PACK_AT[n][2] + c] for n, r, c in BIG if PACK_AT[n][0] == which}


def _full_weights(gathered, own, chip, which):
    own_b = _unpack_shards(own, which)
    per_chip = [{n: jnp.where(chip == j, own_b[n], blk) for n, blk in _unpack_shards(gathered[j], which).items()}
                for j in range(N_CHIPS)]
    out = {}
    for n in own_b:
        shards = [pc[n] for pc in per_chip]
        if n == "w_in":
            out["w_in_p"] = _w_in_internal(shards)
        else:
            out[n] = jnp.concatenate(shards, axis=1 if n in COL_SHARDED else 0)
    return out


def _split_full_grads(grads, pack, dtype):
    shard = {n: (r, c) for n, r, c in BIG}
    chunks = []
    for j in range(N_CHIPS):
        blocks = {}
        for n, g in grads.items():
            if n == "w_in_p":
                blocks["w_in"] = _w_in_grad_shard(g, j)
                continue
            r, c = shard[n]
            blocks[n] = g[:, j * c:(j + 1) * c] if n in COL_SHARDED else g[j * r:(j + 1) * r]
        chunks.append(pack(blocks, dtype))
    return jnp.stack(chunks)


W_IN_SHARD = 936
W_IN_SEGMENTS = ((0, 256, (3072,)), (256, 384, (3840,)), (384, 416, (4032,)), (416, 1440, (0,)), (1440, 1504, (3328, 3392)),
                 (1504, 1568, (3456, 3520)), (1568, 1632, (3584, 3648)), (1632, 1696, (3712, 3776)), (1696, 3744, (1024,)))


def _w_in_internal(shards):
    def cols(a, b):
        out = []
        for j, s in enumerate(shards):
            lo, hi = max(a, W_IN_SHARD * j), min(b, W_IN_SHARD * (j + 1))
            if lo < hi:
                out.append(s[:, lo - W_IN_SHARD * j:hi - W_IN_SHARD * j])
        return out

    pieces = {}
    for a, b, places in W_IN_SEGMENTS:
        for at in places:
            pieces[at] = cols(a, b)
    zeros = lambda n: [jnp.zeros((D, n), shards[0].dtype)]
    pieces[3968] = zeros(64)
    pieces[4064] = zeros(32)
    return jnp.concatenate([piece for at in sorted(pieces) for piece in pieces[at]], axis=1)


def _w_in_grad_shard(g, j):
    def internal(a, b):
        out = []
        while a < b:
            end = min(b, (a // D + 1) * D)
            out.append(g[a // D][:, a % D:a % D + end - a])
            a = end
        return out

    out = []
    for a, b, places in W_IN_SEGMENTS:
        lo, hi = max(a, W_IN_SHARD * j), min(b, W_IN_SHARD * (j + 1))
        if lo < hi:
            parts = [internal(at + lo - a, at + hi - a) for at in places]
            if len(parts) == 1:
                out += parts[0]
            else:
                assert len(parts[0]) == len(parts[1]) == 1
                out.append(parts[0][0] + parts[1][0])
    return jnp.concatenate(out, axis=1)


def _local_step(x, p, tgt, w, small, late_weights, late_grads_out):
    T = x.shape[0]
    tm = 256
    tb = 256
    w_in_p = w["w_in_p"]
    wqb = jnp.pad(w["w_q_b"].reshape(Q_LORA, MLA_HEADS, 96), ((0, 0), (0, 0), (0, 32))).reshape(Q_LORA, 2048)
    wkv = w["w_kv_b"].reshape(KV_LORA, MLA_HEADS, 128)
    wkn = jnp.pad(wkv[:, :, :64], ((0, 0), (0, 0), (0, 64))).reshape(KV_LORA, 2048)
    wv = wkv[:, :, 64:].reshape(KV_LORA, 1024)
    tab_m = _rope_tables(T, "mla")
    tab_s = _rope_tables(T, "swa")
    g1, gq, gkv, sinks = small["g_mix_pre"], small["g_q_a"], small["g_kv_a"], small["sinks"]
    g2, g3, g4, g5 = small["g_mix_post"], small["g_mlp_pre"], small["g_mlp_post"], small["g_ple"]
    sink_vec = sinks.reshape(SWA_HEADS)

    z, h1 = _fwd_in(x, g1, w_in_p, tm)
    qn, kvn, qm, km, vm, qt, kt, vt, qs, ks, vs = _fwd_qkv(z, gq, gkv, wqb, wkn, wv, tab_m, tab_s, tb)
    om, lse_m = _mla_fwd(qm, km, vt, tb)
    os_, lse_s = _swa_fwd(sink_vec, qs, ks, vs)
    w = {**w, **late_weights((om, os_))}
    y, yo, au, bu, x1 = _fwd_mix(om, os_, z, x, w["w_mla_up"], w["w_swa_up"], w["w_out"], g2, tm)
    h2, a, u = _fwd_mlp_up(x1, g3, w["w_mlp_up"], tm)
    d, x2 = _fwd_mlp_down(u, w["w_mlp_down"], x1, g4, tm)
    loss, dx2, dgt, de0, dg5 = _ple_fwd_bwd(p, x2, tgt, w["w_ple"], g5, w["w_ple_gate"], tm)

    dd, da, dg4 = _bwd_mlp_down(dx2, d, g4, w["w_mlp_down"], a, tm)
    dx1, dg3 = _bwd_mlp_up(da, w["w_mlp_up"], x1, g3, dx2, tm)
    dyo, dg2, dau, dbu, dga, dgb, dos, delta_m, dom_t = _bwd_mix(dx1, yo, g2, w["w_out"], z, au, bu, w["w_mla_up"],
                                                                w["w_swa_up"], om, tb)
    gpk_late = lax.empty((N_CHIPS, PACK_ROWS["late"], D), BF16)
    for weight, a_, g_ in (("w_mla_up", om, dau), ("w_swa_up", os_, dbu), ("w_out", y, dyo), ("w_ple_gate", x2, dgt),
                           ("w_mlp_up", h2, da), ("w_mlp_down", u, dd)):
        gpk_late = _wgrad(a_, g_, "wgrad_" + weight[2:], into=(gpk_late, weight))
    token = late_grads_out(gpk_late)
    delta_m = delta_m + token[0, 0]
    dqm, dkm, dvm = _mla_bwd(qm, qt, km, kt, vm, dom_t, lse_m, delta_m, tb)
    dqs, dkc, dkp, dvc, dvp, dsink = _swa_bwd(sink_vec, qs, ks, vs, dos, os_, lse_s)
    dqb, dknb, dvb, dsq, drest, dgq, dgkv = _bwd_qkv(dqm, dkm, dvm, dqs, dkc, dkp, dvc, dvp, z, gq, gkv, wqb, wkn, wv,
                                                      tab_m, tab_s)
    gx, dg1 = _bwd_in(dsq, dga, dgb, drest, w_in_p, x, g1, dx1, tm)

    g_in_p = [_wgrad(h1, dsq, "wgrad_in_sq"), _wgrad(h1, dga, "wgrad_in_ga"), _wgrad(h1, dgb, "wgrad_in_gb"),
              _wgrad(h1, drest, "wgrad_in_rest")]
    g_qb_p = _wgrad(qn, dqb, "wgrad_q_b")
    g_kn_p = _wgrad(kvn, dknb, "wgrad_kv_b_nope")
    g_v_p = _wgrad(kvn, dvb, "wgrad_kv_b_v")
    grads = {
        "w_in_p": g_in_p,
        "w_q_b": g_qb_p.reshape(Q_LORA, MLA_HEADS, 128)[:, :, :96].reshape(Q_LORA, 1536),
        "w_kv_b": jnp.concatenate([g_kn_p.reshape(KV_LORA, MLA_HEADS, 128)[:, :, :64], g_v_p.reshape(KV_LORA, MLA_HEADS, 64)],
                                  axis=2).reshape(KV_LORA, 2048),
        "w_ple": _wgrad(p, de0, "wgrad_ple"),
    }
    small_grads = {"g_mix_pre": dg1, "g_q_a": dgq, "g_kv_a": dgkv, "sinks": dsink[0:1, 0:SWA_HEADS], "g_mix_post": dg2,
                   "g_mlp_pre": dg3, "g_mlp_post": dg4, "g_ple": dg5}
    return loss, gx, grads, small_grads


def _pack_small(vals, fill, scalar=None):
    wide = [vals[n] for n, k in SMALL if k == D]
    narrow = [vals[n] for n, k in SMALL if k != D]
    used = sum(k for _, k in SMALL if k != D)
    last = jnp.concatenate(narrow + [jnp.full((1, D - used), fill, F32)], axis=1)
    rest = jnp.full((2, D), fill, F32)
    if scalar is not None:
        rest = jnp.concatenate([jnp.concatenate([scalar, rest[0:1, 1:]], axis=1), rest[1:2]], axis=0)
    return jnp.concatenate(wide + [last, rest], axis=0)


def _unpack_small(pk):
    out, row, off = {}, 0, 0
    for n, k in SMALL:
        if k == D:
            out[n] = pk[row:row + 1]
            row += 1
    for n, k in SMALL:
        if k != D:
            out[n] = pk[5:6, off:off + k]
            off += k
    return out


def kernel(x, p, g_mix_pre, w_in, g_q_a, w_q_b, g_kv_a, w_kv_b, sinks, w_mla_up, w_swa_up, w_out, g_mix_post, g_mlp_pre, w_mlp_up, w_mlp_down, g_mlp_post, w_ple, g_ple, w_ple_gate, loss_target, m_g_mix_pre, m_w_in, m_g_q_a, m_w_q_b, m_g_kv_a, m_w_kv_b, m_sinks, m_w_mla_up, m_w_swa_up, m_w_out, m_g_mix_post, m_g_mlp_pre, m_w_mlp_up, m_w_mlp_down, m_g_mlp_post, m_w_ple, m_g_ple, m_w_ple_gate, v_g_mix_pre, v_w_in, v_g_q_a, v_w_q_b, v_g_kv_a, v_w_kv_b, v_sinks, v_w_mla_up, v_w_swa_up, v_w_out, v_g_mix_post, v_g_mlp_pre, v_w_mlp_up, v_w_mlp_down, v_g_mlp_post, v_w_ple, v_g_ple, v_w_ple_gate):
    given = dict(locals())
    big_w = {n: given[n][0] for n, _, _ in BIG}
    small_w = {n: given[n] for n, _ in SMALL}
    small_m = {n: given["m_" + n] for n, _ in SMALL}
    small_v = {n: given["v_" + n] for n, _ in SMALL}

    core = lax.axis_index("c")
    chip = 2 * lax.axis_index("x") + lax.axis_index("y")
    core_i = core.astype(jnp.int32).reshape(1)
    chip_i = chip.astype(jnp.int32).reshape(1)
    dev_i = jnp.stack([2 * chip + core, chip, core]).astype(jnp.int32)

    own_early = _pack_early(big_w, BF16)
    own_late = _pack_late(big_w, BF16)
    got_early = _all_gather(own_early)
    late_flight = _gather_late_start(own_late, got_early)
    weights = _full_weights(got_early, own_early, chip, "early")
    step_small = {**small_w, "g_mix_pre": small_w["g_mix_pre"] + late_flight[4][0, 0]}

    def late_weights(after):
        return _full_weights(_gather_late_wait(*late_flight[:4], after), own_late, chip, "late")

    flight = {}

    def late_grads_out(gpk_late):
        flight["late"] = _reduce_late_start(gpk_late, dev_i)
        return flight["late"][4]

    loss_blk, gx, grads, small_grads = _local_step(x[0], p[0, 0], loss_target[0], weights, step_small, late_weights,
                                                   late_grads_out)

    gpk = _split_full_grads(grads, _pack_early, F32)
    got = _rs_sibling(gpk)
    part = _rs_add_sibling(core_i, gpk, got)
    parts, small_parts = _rs_chips(part, _pack_small(small_grads, 0.0, loss_blk[0:1, 0:1]))
    mine_early = _rs_add_chips(chip_i, part, parts)
    gpk_late, parts_late = _reduce_late_wait(*flight["late"][:4], mine_early)
    mine_late = _reduce_late_add(dev_i, gpk_late, parts_late)
    theirs_early, theirs_late = _rs_join(mine_early, mine_late)
    joined = {"early": jnp.where(core == 0, jnp.concatenate([mine_early, theirs_early]), jnp.concatenate([theirs_early, mine_early])),
              "late": jnp.where(core == 0, jnp.concatenate([mine_late, theirs_late]), jnp.concatenate([theirs_late, mine_late]))}

    g_small_pk, d_small_pk, m_small_pk, v_small_pk = _adamw_small(
        _pack_small(small_w, 0.0), small_parts, _pack_small(small_m, 0.0), _pack_small(small_v, 1.0))
    loss = g_small_pk[6, 0]
    g_small, d_small = _unpack_small(g_small_pk), _unpack_small(d_small_pk)
    m_small, v_small = _unpack_small(m_small_pk), _unpack_small(v_small_pk)

    out_g, out_d, out_m, out_v = dict(g_small), dict(d_small), dict(m_small), dict(v_small)
    for n, _, _ in BIG:
        out_g[n], out_d[n], out_m[n], out_v[n] = _adamw(given[n], joined[PACK_AT[n][0]], given["m_" + n], given["v_" + n], n)
    order = ["g_mix_pre", "w_in", "g_q_a", "w_q_b", "g_kv_a", "w_kv_b", "sinks", "w_mla_up", "w_swa_up", "w_out", "g_mix_post",
             "g_mlp_pre", "w_mlp_up", "w_mlp_down", "g_mlp_post", "w_ple", "g_ple", "w_ple_gate"]
    return (loss, gx[None], *[out_g[n] for n in order], *[out_d[n] for n in order], *[out_m[n] for n in order],
            *[out_v[n] for n in order])
```

```python
import math

import jax
import jax.numpy as jnp
from jax import lax
from jax.experimental import pallas as pl
from jax.experimental.pallas import tpu as pltpu

F32 = jnp.float32
BF16 = jnp.bfloat16
SDS = jax.ShapeDtypeStruct

D = 1024
D_FF = 4096
PLE = 256
Q_LORA = 256
KV_LORA = 128
MLA_HEADS = 16
MLA_NOPE = 64
MLA_ROPE = 32
SWA_HEADS = 16
SWA_HD = 64
WINDOW = 128
ROPE_THETA = 10000.0
EPS = 1e-6
NEG = -1e30
NZ = 4096
MLA_SCALE = (MLA_NOPE + MLA_ROPE) ** -0.5
LOG2_E = math.log2(math.e)
MLA_LOG2_SCALE = MLA_SCALE * LOG2_E
SWA_SCALE = SWA_HD ** -0.5

ADAM_LR = 0.001
ADAM_B1 = 0.9
ADAM_B2 = 0.999
ADAM_EPS = 1e-08
ADAM_WD = 0.01
ADAM_STEP = 10

LANES = 128
ATT_COLS = 128
N_CHIPS = 4
N_DEV = 8
MESH = pl.DeviceIdType.MESH

NT = (((1,), (1,)), ((), ()))
TN = (((0,), (0,)), ((), ()))

BIG = (("w_in", 1024, 936), ("w_q_b", 256, 384), ("w_kv_b", 128, 512), ("w_mla_up", 256, 1024),
       ("w_swa_up", 256, 1024), ("w_out", 256, 1024), ("w_mlp_up", 1024, 1024), ("w_mlp_down", 1024, 1024),
       ("w_ple", 256, 256), ("w_ple_gate", 256, 1024))
COL_SHARDED = ("w_in", "w_q_b", "w_kv_b", "w_mlp_up", "w_ple")
PACK_AT = {"w_in": ("early", 0, 0), "w_q_b": ("early", 1024, 0), "w_ple": ("early", 1024, 384), "w_kv_b": ("early", 1280, 0),
           "w_mla_up": ("late", 0, 0), "w_swa_up": ("late", 256, 0), "w_out": ("late", 512, 0), "w_ple_gate": ("late", 768, 0),
           "w_mlp_up": ("late", 1024, 0), "w_mlp_down": ("late", 2048, 0)}
PACK_ROWS = {"early": 1408, "late": 3072}
SMALL = (("g_mix_pre", 1024), ("g_q_a", 256), ("g_kv_a", 128), ("sinks", 16), ("g_mix_post", 1024),
         ("g_mlp_pre", 1024), ("g_mlp_post", 1024), ("g_ple", 1024))


def _dot(a, b):
    return jnp.dot(a, b, preferred_element_type=F32)


def _dot_nt(a, b):
    return lax.dot_general(a, b, NT, preferred_element_type=F32)


def _dot_tn(a, b):
    return lax.dot_general(a, b, TN, preferred_element_type=F32)


def _pcall(body, *, name, out_shape, grid=(), in_specs=None, out_specs=None, scratch=(), sem=None, vmem_mb=48, aliases=None):
    params = dict(vmem_limit_bytes=vmem_mb << 20)
    if sem is not None:
        params["dimension_semantics"] = sem
    return pl.pallas_call(body, name=name, grid=grid, in_specs=in_specs, out_specs=out_specs, out_shape=out_shape,
                          scratch_shapes=list(scratch), input_output_aliases=aliases or {},
                          compiler_params=pltpu.CompilerParams(**params))


def _rows(tm, n, col=0):
    return pl.BlockSpec((tm, n), lambda i: (i, col))


def _full(shape):
    return pl.BlockSpec(shape, lambda i: (0,) * len(shape))


def _rms(x, g):
    r = lax.rsqrt(jnp.mean(x * x, axis=-1, keepdims=True) + EPS)
    return x * r * g


def _rms_bwd(dy, x, g):
    r = lax.rsqrt(jnp.mean(x * x, axis=-1, keepdims=True) + EPS)
    xn = x * r
    dn = dy * g
    dx = r * (dn - xn * jnp.mean(dn * xn, axis=-1, keepdims=True))
    return dx, jnp.sum(dy * xn, axis=0, keepdims=True)


def _sigmoid(x):
    return 1.0 / (1.0 + jnp.exp(-x))


def _rope(x, c, a, b, half):
    return x * c + pltpu.roll(x, LANES - half, 1) * a + pltpu.roll(x, half, 1) * b


def _rope_tables(T, kind):
    lane = jnp.arange(LANES)
    if kind == "mla":
        half = MLA_ROPE // 2
        rel = lane - MLA_NOPE
        on = (rel >= 0) & (rel < MLA_ROPE)
        d = MLA_ROPE
    else:
        half = SWA_HD // 2
        rel = lane % SWA_HD
        on = jnp.ones((LANES,), bool)
        d = SWA_HD
    first = on & (rel < half)
    second = on & (rel >= half)
    f = jnp.where(first, rel, rel - half).astype(F32)
    inv = jnp.exp(-math.log(ROPE_THETA) * f * (2.0 / d))
    ang = jnp.arange(T, dtype=F32)[:, None] * inv[None, :]
    cos, sin = jnp.cos(ang), jnp.sin(ang)
    c = jnp.where(on[None], cos, 1.0)
    a = jnp.where(first[None], -sin, 0.0)
    b = jnp.where(second[None], sin, 0.0)
    return c, a, b


def _fwd_in(x, g1, w_in_p, tm):
    T = x.shape[0]

    def body(x_ref, g_ref, w_ref, z_ref, h_ref):
        h = _rms(x_ref[...], g_ref[...]).astype(BF16)
        h_ref[...] = h
        z_ref[...] = _dot(h, w_ref[...])

    return _pcall(body, name="fwd_in", grid=(T // tm,),
                  in_specs=[_rows(tm, D), _full((1, D)), _full((D, NZ))],
                  out_specs=[_rows(tm, NZ), _rows(tm, D)],
                  out_shape=[SDS((T, NZ), F32), SDS((T, D), BF16)], sem=("parallel",))(x, g1, w_in_p)


def _fwd_qkv(z, gq, gkv, wqb, wkn, wv, tab_m, tab_s, tm):
    T = z.shape[0]

    def body(qa_ref, sq_ref, skd_ref, svd_ref, kva_ref, kr_ref, gq_ref, gkv_ref, wqb_ref, wkn_ref, wv_ref,
             cm_ref, am_ref, bm_ref, cs_ref, as_ref, bs_ref,
             qn_ref, kvn_ref, qm_ref, km_ref, vm_ref, qt_ref, kt_ref, vt_ref, qs_ref, ks_ref, vs_ref):
        qn = _rms(qa_ref[...], gq_ref[...]).astype(BF16)
        qn_ref[...] = qn
        kvn = _rms(kva_ref[...], gkv_ref[...]).astype(BF16)
        kvn_ref[...] = kvn
        cm, am, bm = cm_ref[...], am_ref[...], bm_ref[...]
        cs, as_, bs = cs_ref[...], as_ref[...], bs_ref[...]
        k_rope = _rope(kr_ref[...], cm, am, bm, MLA_ROPE // 2)
        vt_row = lax.broadcasted_iota(jnp.int32, (LANES, tm), 0)
        v_all = _dot(kvn, wv_ref[...])
        q_all = _dot(qn, wqb_ref[...])
        k_all = _dot(kvn, wkn_ref[...])
        for j in range(D // LANES):
            sl = slice(LANES * j, LANES * (j + 1))
            v = v_all[:, sl]
            vm_ref[:, sl] = v.astype(BF16)
            v_t = v.T
            for hh, rows64 in enumerate((v_t, pltpu.roll(v_t, 64, 0))):
                blk = jnp.where(vt_row < 64, rows64, jnp.where(vt_row == 64, 1.0, 0.0))
                vt_ref[0, LANES * (2 * j + hh):LANES * (2 * j + hh + 1), :] = blk.astype(BF16)
        for h in range(MLA_HEADS):
            sl = slice(LANES * h, LANES * (h + 1))
            qh = _rope(q_all[:, sl], cm, am, bm, MLA_ROPE // 2)
            qm_ref[:, sl] = qh.astype(BF16)
            qt_ref[0, sl, :] = qh.T.astype(BF16)
            k = k_all[:, sl] + k_rope
            km_ref[:, sl] = k.astype(BF16)
            kt_ref[0, sl, :] = k.T.astype(BF16)
        for j in range(D // LANES):
            sl = slice(LANES * j, LANES * (j + 1))
            qs_ref[:, sl] = _rope(sq_ref[:, sl], cs, as_, bs, SWA_HD // 2).astype(BF16)
        for j in range(2):
            sl = slice(LANES * j, LANES * (j + 1))
            ks_ref[:, sl] = _rope(skd_ref[:, sl], cs, as_, bs, SWA_HD // 2).astype(BF16)
        vs_ref[...] = svd_ref[...].astype(BF16)

    tab = [_rows(tm, LANES)] * 6
    return _pcall(body, name="fwd_qkv", grid=(T // tm,),
                  in_specs=[_rows(tm, 256, 12), _rows(tm, 1024, 0), _rows(tm, 256, 13), _rows(tm, 256, 14),
                            _rows(tm, 128, 30), _rows(tm, 128, 31), _full((1, Q_LORA)), _full((1, KV_LORA)),
                            _full((Q_LORA, 2048)), _full((KV_LORA, 2048)), _full((KV_LORA, 1024))] + tab,
                  out_specs=[_rows(tm, Q_LORA), _rows(tm, KV_LORA), _rows(tm, 2048), _rows(tm, 2048), _rows(tm, 1024),
                             pl.BlockSpec((1, 2048, tm), lambda i: (i, 0, 0)), pl.BlockSpec((1, 2048, tm), lambda i: (i, 0, 0)),
                             pl.BlockSpec((1, 2048, tm), lambda i: (i, 0, 0)),
                             _rows(tm, 1024), _rows(tm, 256), _rows(tm, 256)],
                  out_shape=[SDS((T, Q_LORA), BF16), SDS((T, KV_LORA), BF16), SDS((T, 2048), BF16), SDS((T, 2048), BF16),
                             SDS((T, 1024), BF16), SDS((T // tm, 2048, tm), BF16), SDS((T // tm, 2048, tm), BF16),
                             SDS((T // tm, 2048, tm), BF16),
                             SDS((T, 1024), BF16), SDS((T, 256), BF16), SDS((T, 256), BF16)],
                  sem=("parallel",))(z, z, z, z, z, z, gq, gkv, wqb, wkn, wv, *tab_m, *tab_s)


def _mla_fwd(qm, km, vt, tb):
    T = qm.shape[0]
    nb = T // tb
    cc = ATT_COLS

    def body(q_ref, k_ref, vt_ref, o_ref, l_ref, s_ref, p_ref, al_ref, m_ref, acc_ref):
        i = pl.program_id(1)
        m_ref[...] = jnp.full(m_ref.shape, NEG, F32)
        acc_ref[...] = jnp.zeros_like(acc_ref)
        p_ref[1] = jnp.zeros(p_ref.shape[1:], BF16)
        al_ref[1] = jnp.ones(al_ref.shape[1:], F32)
        key = lax.broadcasted_iota(jnp.int32, (tb, cc), 0)
        qry = lax.broadcasted_iota(jnp.int32, (tb, cc), 1)

        def scores(j, slot):
            off = pl.multiple_of(j * tb, tb)
            for hh in range(2):
                sl = slice(LANES * hh, LANES * (hh + 1))
                s_ref[slot, hh] = _dot_nt(k_ref[pl.ds(off, tb), sl], q_ref[:, sl])

        def softmax(slot, diagonal):
            chains = [(hh, slice(cc * c, cc * (c + 1)), c) for hh in range(2) for c in range(tb // cc)]

            def scaled(hh, cols, c):
                t = s_ref[slot, hh, :, cols] * MLA_LOG2_SCALE
                return jnp.where(key <= qry + cc * c, t, NEG) if diagonal else t

            tops = []
            for hh, cols, c in chains:
                if diagonal:
                    top = jnp.max(scaled(hh, cols, c), axis=0, keepdims=True)
                else:
                    top = jnp.max(s_ref[slot, hh, :, cols], axis=0, keepdims=True) * MLA_LOG2_SCALE
                m_old = m_ref[hh, :, cols]
                mn = jnp.maximum(m_old, top)
                m_ref[hh, :, cols] = mn
                al_ref[slot, hh, :, cols] = jnp.exp2(m_old - mn)
                tops.append(mn)
            for (hh, cols, c), mn in zip(chains, tops):
                p_ref[slot, hh, :, cols] = jnp.exp2(scaled(hh, cols, c) - mn).astype(BF16)

        def accumulate(j, slot):
            for hh in range(2):
                acc_ref[hh] = al_ref[slot, hh] * acc_ref[hh] + _dot(vt_ref[j, LANES * hh:LANES * (hh + 1), :], p_ref[slot, hh])

        def step(t, carry):
            scores(2 * t + 1, 1)
            accumulate(jnp.maximum(2 * t - 1, 0), 1)
            softmax(0, False)
            scores(2 * t + 2, 0)
            accumulate(2 * t, 0)
            softmax(1, False)
            return carry

        scores(0, 0)
        lax.fori_loop(0, i // 2, step, 0)

        @pl.when(i % 2 == 1)
        def _():
            scores(i, 1)
            accumulate(jnp.maximum(i - 2, 0), 1)
            softmax(0, False)
            accumulate(i - 1, 0)
            softmax(1, True)
            accumulate(i, 1)

        @pl.when(i % 2 == 0)
        def _():
            accumulate(jnp.maximum(i - 1, 0), 1)
            softmax(0, True)
            accumulate(i, 0)
        den = [acc_ref[hh, 64:65, :] for hh in range(2)]
        o_ref[...] = jnp.concatenate([acc_ref[hh, 0:64, :] / den[hh] for hh in range(2)], axis=0).T
        sub = lax.broadcasted_iota(jnp.int32, (8, tb), 0)
        lse = [m_ref[hh] + jnp.log(den[hh]) * LOG2_E for hh in range(2)]
        l_ref[0, 0] = jnp.where(sub == 0, lse[0], jnp.where(sub == 1, lse[1], 0.0))

    return _pcall(body, name="mla_fwd", grid=(MLA_HEADS // 2, nb),
                  in_specs=[pl.BlockSpec((tb, 256), lambda p, i: (i, p)), pl.BlockSpec((T, 256), lambda p, i: (0, p)),
                            pl.BlockSpec((nb, 2 * LANES, tb), lambda p, i: (0, p, 0))],
                  out_specs=[pl.BlockSpec((tb, LANES), lambda p, i: (i, p)),
                             pl.BlockSpec((1, 1, 8, tb), lambda p, i: (p, i, 0, 0))],
                  out_shape=[SDS((T, D), F32), SDS((MLA_HEADS // 2, nb, 8, tb), F32)],
                  scratch=[pltpu.VMEM((2, 2, tb, tb), F32), pltpu.VMEM((2, 2, tb, tb), BF16), pltpu.VMEM((2, 2, 1, tb), F32),
                           pltpu.VMEM((2, 1, tb), F32), pltpu.VMEM((2, LANES, tb), F32)],
                  sem=("parallel", "arbitrary"))(qm, km, vt)


def _swa_mask(n):
    row = lax.broadcasted_iota(jnp.int32, (WINDOW, 2 * WINDOW), 0)
    col = lax.broadcasted_iota(jnp.int32, (WINDOW, 2 * WINDOW), 1)
    rel = row - col + WINDOW
    return (rel >= 0) & (rel < WINDOW) & ((col >= WINDOW) | (n > 0))


def _swa_specs(T):
    nb = T // WINDOW
    cur = lambda w: pl.BlockSpec((WINDOW, w), lambda n: (n, 0))
    prev = lambda w: pl.BlockSpec((WINDOW, w), lambda n: (jnp.maximum(n - 1, 0), 0))
    return nb, cur, prev


def _swa_fwd(sinks, qs, ks, vs):
    T = qs.shape[0]
    nb, cur, prev = _swa_specs(T)

    def body(sink_ref, q_ref, kc_ref, kp_ref, vc_ref, vp_ref, o_ref, l_ref, kb_ref, vb_ref, s_ref, p_ref):
        n = pl.program_id(0)
        mask = _swa_mask(n)
        lo = lax.broadcasted_iota(jnp.int32, (WINDOW, LANES), 1) < 64
        hi = jnp.logical_not(lo)
        for g in range(2):
            gs = slice(LANES * g, LANES * (g + 1))
            kb_ref[g] = jnp.concatenate([kp_ref[:, gs], kc_ref[:, gs]], axis=0)
            vb_ref[g] = jnp.concatenate([vp_ref[:, gs], vc_ref[:, gs]], axis=0)
        for h in range(SWA_HEADS):
            qp = q_ref[:, LANES * (h // 2):LANES * (h // 2 + 1)]
            qh = jnp.where(lo if h % 2 == 0 else hi, qp, jnp.zeros_like(qp))
            s_ref[h] = _dot_nt(qh, kb_ref[h // 8])
        for j in range(SWA_HEADS // 2):
            sl = slice(LANES * j, LANES * (j + 1))
            lses = []
            for h in (2 * j, 2 * j + 1):
                s = jnp.where(mask, s_ref[h] * SWA_SCALE, NEG)
                sk = sink_ref[h]
                m = jnp.maximum(jnp.max(s, axis=1, keepdims=True), sk)
                e = jnp.exp(s - m)
                den = jnp.sum(e, axis=1, keepdims=True) + jnp.exp(sk - m)
                p_ref[h] = (e / den).astype(BF16)
                lses.append(jnp.broadcast_to(m + jnp.log(den), (WINDOW, LANES)))
            l_ref[:, sl] = jnp.where(lo, lses[0], lses[1])
        for j in range(SWA_HEADS // 2):
            vb = vb_ref[j // 4]
            o_ref[:, LANES * j:LANES * (j + 1)] = jnp.where(lo, _dot(p_ref[2 * j], vb), _dot(p_ref[2 * j + 1], vb))

    return _pcall(body, name="swa_fwd", grid=(nb,),
                  in_specs=[pl.BlockSpec(memory_space=pltpu.SMEM), cur(D), cur(256), prev(256), cur(256), prev(256)],
                  out_specs=[cur(D), cur(D)], out_shape=[SDS((T, D), F32)] * 2,
                  scratch=[pltpu.VMEM((2, 2 * WINDOW, LANES), BF16), pltpu.VMEM((2, 2 * WINDOW, LANES), BF16),
                           pltpu.VMEM((SWA_HEADS, WINDOW, 2 * WINDOW), F32), pltpu.VMEM((SWA_HEADS, WINDOW, 2 * WINDOW), BF16)],
                  sem=("parallel",))(sinks, qs, ks, ks, vs, vs)


def _fwd_mix(om, os_, z, x, wmu, wsu, wo, g2, tm):
    T = x.shape[0]

    def body(om_ref, os_ref, ga_ref, gb_ref, x_ref, wmu_ref, wsu_ref, wo_ref, g2_ref,
             y_ref, yo_ref, au_ref, bu_ref, x1_ref):
        au = _dot(om_ref[...].astype(BF16), wmu_ref[...])
        bu = _dot(os_ref[...].astype(BF16), wsu_ref[...])
        au_ref[...] = au
        bu_ref[...] = bu
        y = (_sigmoid(ga_ref[...]) * au + _sigmoid(gb_ref[...]) * bu).astype(BF16)
        y_ref[...] = y
        yo = _dot(y, wo_ref[...])
        yo_ref[...] = yo
        x1_ref[...] = x_ref[...] + _rms(yo, g2_ref[...])

    r = _rows(tm, D)
    w = _full((D, D))
    return _pcall(body, name="fwd_mix", grid=(T // tm,),
                  in_specs=[r, r, _rows(tm, D, 1), _rows(tm, D, 2), r, w, w, w, _full((1, D))],
                  out_specs=[r] * 5,
                  out_shape=[SDS((T, D), BF16), SDS((T, D), F32), SDS((T, D), F32), SDS((T, D), F32), SDS((T, D), F32)],
                  sem=("parallel",))(om, os_, z, z, x, wmu, wsu, wo, g2)


def _fwd_mlp_up(x1, g3, w1, tm):
    T = x1.shape[0]

    def body(x_ref, g_ref, w_ref, h_ref, a_ref, u_ref):
        h = _rms(x_ref[...], g_ref[...]).astype(BF16)
        h_ref[...] = h
        a = _dot(h, w_ref[...])
        a_ref[...] = a
        u_ref[...] = jnp.square(jnp.maximum(a, 0.0)).astype(BF16)

    return _pcall(body, name="fwd_mlp_up", grid=(T // tm,),
                  in_specs=[_rows(tm, D), _full((1, D)), _full((D, D_FF))],
                  out_specs=[_rows(tm, D), _rows(tm, D_FF), _rows(tm, D_FF)],
                  out_shape=[SDS((T, D), BF16), SDS((T, D_FF), F32), SDS((T, D_FF), BF16)],
                  sem=("parallel",))(x1, g3, w1)


def _fwd_mlp_down(u, w2, x1, g4, tm):
    T = x1.shape[0]

    def body(u_ref, w_ref, x_ref, g_ref, d_ref, x2_ref):
        d = _dot(u_ref[...], w_ref[...])
        d_ref[...] = d
        x2_ref[...] = x_ref[...] + _rms(d, g_ref[...])

    return _pcall(body, name="fwd_mlp_down", grid=(T // tm,),
                  in_specs=[_rows(tm, D_FF), _full((D_FF, D)), _rows(tm, D), _full((1, D))],
                  out_specs=[_rows(tm, D), _rows(tm, D)], out_shape=[SDS((T, D), F32)] * 2,
                  sem=("parallel",))(u, w2, x1, g4)


def _ple_fwd_bwd(p, x2, tgt, wple, g5, wpg, tm):
    T = x2.shape[0]

    def body(p_ref, x2_ref, t_ref, wple_ref, g5_ref, wpg_ref, loss_ref, dx2_ref, dgt_ref, de0_ref, dg5_ref):
        @pl.when(pl.program_id(0) == 0)
        def _():
            loss_ref[...] = jnp.zeros_like(loss_ref)
            dg5_ref[...] = jnp.zeros_like(dg5_ref)

        e0 = _dot(p_ref[...].astype(BF16), wple_ref[...])
        g5 = g5_ref[...]
        r = lax.rsqrt(jnp.mean(e0 * e0, axis=-1, keepdims=True) + EPS)
        en = e0 * r
        e = en * g5
        x2 = x2_ref[...]
        s = _sigmoid(_dot(x2.astype(BF16), wpg_ref[...]))
        diff = x2 + s * e - t_ref[...]
        sq = jnp.sum(jnp.sum(diff * diff, axis=1, keepdims=True), axis=0, keepdims=True)
        loss_ref[...] += jnp.broadcast_to(sq * (0.5 / D), loss_ref.shape)
        dx3 = diff * (1.0 / D)
        de = dx3 * s
        dgt = (dx3 * e * s * (1.0 - s)).astype(BF16)
        dgt_ref[...] = dgt
        dn = de * g5
        de0_ref[...] = (r * (dn - en * jnp.mean(dn * en, axis=-1, keepdims=True))).astype(BF16)
        dg5_ref[...] += jnp.sum(de * en, axis=0, keepdims=True)
        dx2_ref[...] = dx3 + _dot_nt(dgt, wpg_ref[...])

    r = _rows(tm, D)
    return _pcall(body, name="ple_fwd_bwd", grid=(T // tm,),
                  in_specs=[_rows(tm, PLE), r, r, _full((PLE, D)), _full((1, D)), _full((D, D))],
                  out_specs=[_full((8, LANES)), r, r, r, _full((1, D))],
                  out_shape=[SDS((8, LANES), F32), SDS((T, D), F32), SDS((T, D), BF16), SDS((T, D), BF16), SDS((1, D), F32)],
                  sem=("arbitrary",))(p, x2, tgt, wple, g5, wpg)


def _bwd_mlp_down(dx2, d, g4, w2, a, tm):
    T = dx2.shape[0]

    def body(dx_ref, d_ref, g_ref, w_ref, a_ref, dd_ref, da_ref, dg_ref):
        @pl.when(pl.program_id(0) == 0)
        def _():
            dg_ref[...] = jnp.zeros_like(dg_ref)

        dd, dg = _rms_bwd(dx_ref[...], d_ref[...], g_ref[...])
        dg_ref[...] += dg
        ddb = dd.astype(BF16)
        dd_ref[...] = ddb
        du = _dot_nt(ddb, w_ref[...])
        da_ref[...] = (du * (2.0 * jnp.maximum(a_ref[...], 0.0))).astype(BF16)

    return _pcall(body, name="bwd_mlp_down", grid=(T // tm,),
                  in_specs=[_rows(tm, D), _rows(tm, D), _full((1, D)), _full((D_FF, D)), _rows(tm, D_FF)],
                  out_specs=[_rows(tm, D), _rows(tm, D_FF), _full((1, D))],
                  out_shape=[SDS((T, D), BF16), SDS((T, D_FF), BF16), SDS((1, D), F32)],
                  sem=("arbitrary",))(dx2, d, g4, w2, a)


def _bwd_mlp_up(da, w1, x1, g3, dx2, tm):
    T = dx2.shape[0]

    def body(da_ref, w_ref, x_ref, g_ref, dx2_ref, dx1_ref, dg_ref):
        @pl.when(pl.program_id(0) == 0)
        def _():
            dg_ref[...] = jnp.zeros_like(dg_ref)

        dh = _dot_nt(da_ref[...], w_ref[...])
        dx, dg = _rms_bwd(dh, x_ref[...], g_ref[...])
        dg_ref[...] += dg
        dx1_ref[...] = dx2_ref[...] + dx

    return _pcall(body, name="bwd_mlp_up", grid=(T // tm,),
                  in_specs=[_rows(tm, D_FF), _full((D, D_FF)), _rows(tm, D), _full((1, D)), _rows(tm, D)],
                  out_specs=[_rows(tm, D), _full((1, D))],
                  out_shape=[SDS((T, D), F32), SDS((1, D), F32)], sem=("arbitrary",))(da, w1, x1, g3, dx2)


def _bwd_mix(dx1, yo, g2, wo, z, au, bu, wmu, wsu, om, tm):
    T = dx1.shape[0]

    def body(dx_ref, yo_ref, g_ref, wo_ref, ga_ref, gb_ref, au_ref, bu_ref, wmu_ref, wsu_ref, om_ref,
             dyo_ref, dg_ref, dau_ref, dbu_ref, dga_ref, dgb_ref, dos_ref, dl_ref, dot_ref):
        @pl.when(pl.program_id(0) == 0)
        def _():
            dg_ref[...] = jnp.zeros_like(dg_ref)

        dyo, dg = _rms_bwd(dx_ref[...], yo_ref[...], g_ref[...])
        dg_ref[...] += dg
        dyob = dyo.astype(BF16)
        dyo_ref[...] = dyob
        dy = _dot_nt(dyob, wo_ref[...])
        sa = _sigmoid(ga_ref[...])
        sb = _sigmoid(gb_ref[...])
        dau = (dy * sa).astype(BF16)
        dbu = (dy * sb).astype(BF16)
        dau_ref[...] = dau
        dbu_ref[...] = dbu
        dga_ref[...] = (dy * au_ref[...] * sa * (1.0 - sa)).astype(BF16)
        dgb_ref[...] = (dy * bu_ref[...] * sb * (1.0 - sb)).astype(BF16)
        dom = _dot_nt(dau, wmu_ref[...])
        dos_ref[...] = _dot_nt(dbu, wsu_ref[...])
        prod = dom * om_ref[...]
        sub = lax.broadcasted_iota(jnp.int32, (8, tm), 0)
        for pr in range(MLA_HEADS // 2):
            sl = slice(LANES * pr, LANES * (pr + 1))
            pt = prod[:, sl].T
            d0 = jnp.sum(pt[0:64], axis=0, keepdims=True)
            d1 = jnp.sum(pt[64:128], axis=0, keepdims=True)
            dl_ref[pr, 0] = jnp.where(sub == 0, d0, jnp.where(sub == 1, d1, 0.0))
            dot_ref[0, sl, :] = dom[:, sl].T.astype(BF16)

    r = _rows(tm, D)
    w = _full((D, D))
    return _pcall(body, name="bwd_mix", grid=(T // tm,),
                  in_specs=[r, r, _full((1, D)), w, _rows(tm, D, 1), _rows(tm, D, 2), r, r, w, w, r],
                  out_specs=[r, _full((1, D)), r, r, r, r, r, pl.BlockSpec((MLA_HEADS // 2, 1, 8, tm), lambda i: (0, i, 0, 0)),
                             pl.BlockSpec((1, D, tm), lambda i: (i, 0, 0))],
                  out_shape=[SDS((T, D), BF16), SDS((1, D), F32), SDS((T, D), BF16), SDS((T, D), BF16), SDS((T, D), BF16),
                             SDS((T, D), BF16), SDS((T, D), F32), SDS((MLA_HEADS // 2, T // tm, 8, tm), F32),
                             SDS((T // tm, D, tm), BF16)],
                  sem=("arbitrary",))(dx1, yo, g2, wo, z, z, au, bu, wmu, wsu, om)


def _mla_bwd(qm, qt, km, kt, vm, dot, lse, delta, tb):
    T = qm.shape[0]
    nb = T // tb
    cc = ATT_COLS

    def body(q_ref, qt_ref, k_ref, kt_ref, v_ref, dot_ref, l_ref, dl_ref, dqt_ref, dkt_ref, dvt_ref,
             s_ref, dp_ref, p_ref, ds_ref, vh_ref):
        j = pl.program_id(1)

        @pl.when(j == 0)
        def _():
            dqt_ref[...] = jnp.zeros_like(dqt_ref)

        dkt_ref[...] = jnp.zeros_like(dkt_ref)
        dvt_ref[...] = jnp.zeros_like(dvt_ref)
        lo = lax.broadcasted_iota(jnp.int32, (tb, LANES), 1) < 64
        key = lax.broadcasted_iota(jnp.int32, (tb, cc), 0)
        qry = lax.broadcasted_iota(jnp.int32, (tb, cc), 1)
        v = v_ref[...]
        vh_ref[0] = jnp.where(lo, v, jnp.zeros_like(v))
        vh_ref[1] = jnp.where(lo, jnp.zeros_like(v), v)

        def scores(i, slot):
            rows_i = pl.ds(pl.multiple_of(i * tb, tb), tb)
            for hh in range(2):
                sl = slice(LANES * hh, LANES * (hh + 1))
                s_ref[slot, hh] = _dot_nt(k_ref[:, sl], q_ref[rows_i, sl])
                dp_ref[slot, hh] = _dot(vh_ref[hh], dot_ref[i])

        def grads(i, slot, diagonal):
            lse_i = l_ref[0, i]
            delta_i = dl_ref[0, i]
            for hh in range(2):
                for c in range(tb // cc):
                    cols = slice(cc * c, cc * (c + 1))
                    p = jnp.exp2(s_ref[slot, hh, :, cols] * MLA_LOG2_SCALE - lse_i[hh:hh + 1, cols])
                    if diagonal:
                        p = jnp.where(key <= qry + cc * c, p, 0.0)
                    p_ref[hh, :, cols] = p.astype(BF16)
                    ds_ref[hh, :, cols] = (p * (dp_ref[slot, hh, :, cols] - delta_i[hh:hh + 1, cols]) * MLA_SCALE).astype(BF16)
            for hh in range(2):
                sl = slice(LANES * hh, LANES * (hh + 1))
                half = slice(64 * hh, 64 * (hh + 1))
                dvt_ref[0, half, :] += _dot_nt(dot_ref[i, half, :], p_ref[hh])
                dkt_ref[0, sl, :] += _dot_nt(qt_ref[i, sl, :], ds_ref[hh])
                dqt_ref[i, sl, :] += _dot(kt_ref[0, sl, :], ds_ref[hh])

        n_off = nb - 1 - j

        def step(u, carry):
            i0 = j + 1 + 2 * u
            scores(i0 + 1, 1)
            grads(i0, 0, False)
            scores(jnp.where(i0 + 2 < nb, i0 + 2, j), 0)
            grads(i0 + 1, 1, False)
            return carry

        scores(jnp.where(n_off > 0, j + 1, j), 0)
        lax.fori_loop(0, n_off // 2, step, 0)

        @pl.when(n_off % 2 == 1)
        def _():
            scores(j, 1)
            grads(nb - 1, 0, False)
            grads(j, 1, True)

        @pl.when(n_off % 2 == 0)
        def _():
            grads(j, 0, True)

    blk = lambda w: pl.BlockSpec((tb, w), lambda p, j: (j, p))
    stat = pl.BlockSpec((1, nb, 8, tb), lambda p, j: (p, 0, 0, 0))
    pair_t = lambda w: pl.BlockSpec((nb, w, tb), lambda p, j: (0, p, 0))
    blk_t = lambda w: pl.BlockSpec((1, w, tb), lambda p, j: (j, p, 0))
    return _pcall(body, name="mla_bwd", grid=(MLA_HEADS // 2, nb),
                  in_specs=[pl.BlockSpec((T, 256), lambda p, j: (0, p)), pair_t(256), blk(256), blk_t(256), blk(LANES),
                            pair_t(LANES), stat, stat],
                  out_specs=[pair_t(256), blk_t(256), blk_t(LANES)],
                  out_shape=[SDS((nb, 2048, tb), F32), SDS((nb, 2048, tb), F32), SDS((nb, D, tb), F32)],
                  scratch=[pltpu.VMEM((2, 2, tb, tb), F32), pltpu.VMEM((2, 2, tb, tb), F32), pltpu.VMEM((2, tb, tb), BF16),
                           pltpu.VMEM((2, tb, tb), BF16), pltpu.VMEM((2, tb, LANES), BF16)],
                  sem=("parallel", "arbitrary"))(qm, qt, km, kt, vm, dot, lse, delta)


def _swa_bwd(sinks, qs, ks, vs, do, o, lse):
    T = qs.shape[0]
    nb, cur, prev = _swa_specs(T)

    def body(sink_ref, q_ref, kc_ref, kp_ref, vc_ref, vp_ref, do_ref, o_ref, l_ref,
             dq_ref, dkc_ref, dkp_ref, dvc_ref, dvp_ref, dsink_ref, kb_ref, vb_ref, s_ref, dp_ref, p_ref, ds_ref):
        n = pl.program_id(0)

        @pl.when(n == 0)
        def _():
            dsink_ref[...] = jnp.zeros_like(dsink_ref)

        mask = _swa_mask(n)
        lo = lax.broadcasted_iota(jnp.int32, (WINDOW, LANES), 1) < 64
        hi = jnp.logical_not(lo)
        lane8 = lax.broadcasted_iota(jnp.int32, (8, LANES), 1)
        for g in range(2):
            gs = slice(LANES * g, LANES * (g + 1))
            kb_ref[g] = jnp.concatenate([kp_ref[:, gs], kc_ref[:, gs]], axis=0)
            vb_ref[g] = jnp.concatenate([vp_ref[:, gs], vc_ref[:, gs]], axis=0)

        def head(h):
            sl = slice(LANES * (h // 2), LANES * (h // 2 + 1))
            hm = lo if h % 2 == 0 else hi
            qp = q_ref[:, sl]
            return hm, sl, jnp.where(hm, qp, jnp.zeros_like(qp)), jnp.where(hm, do_ref[:, sl], 0.0).astype(BF16)

        for h in range(SWA_HEADS):
            _, _, qh, dom = head(h)
            s_ref[h] = _dot_nt(qh, kb_ref[h // 8])
            dp_ref[h] = _dot_nt(dom, vb_ref[h // 8])
        dsink = jnp.zeros((8, LANES), F32)
        for h in range(SWA_HEADS):
            hm, sl, _, _ = head(h)
            lse_h = jnp.max(jnp.where(hm, l_ref[:, sl], -jnp.inf), axis=1, keepdims=True)
            delta = jnp.sum(jnp.where(hm, do_ref[:, sl] * o_ref[:, sl], 0.0), axis=1, keepdims=True)
            p = jnp.exp(jnp.where(mask, s_ref[h] * SWA_SCALE, NEG) - lse_h)
            p_ref[h] = p.astype(BF16)
            ds_ref[h] = (p * (dp_ref[h] - delta) * SWA_SCALE).astype(BF16)
            d_sink = -jnp.sum(jnp.exp(sink_ref[h] - lse_h) * delta, axis=0, keepdims=True)
            dsink = dsink + jnp.where(lane8 == h, d_sink, 0.0)
        dsink_ref[...] += dsink
        for g in range(2):
            gs = slice(LANES * g, LANES * (g + 1))
            dkb = jnp.zeros((2 * WINDOW, LANES), F32)
            dvb = jnp.zeros((2 * WINDOW, LANES), F32)
            for j in range(4 * g, 4 * g + 4):
                dqs = []
                for h in (2 * j, 2 * j + 1):
                    _, _, qh, dom = head(h)
                    dvb = dvb + _dot_tn(p_ref[h], dom)
                    dkb = dkb + _dot_tn(ds_ref[h], qh)
                    dqs.append(_dot(ds_ref[h], kb_ref[g]))
                dq_ref[:, LANES * j:LANES * (j + 1)] = jnp.where(lo, dqs[0], dqs[1])
            dkp_ref[:, gs] = dkb[:WINDOW]
            dkc_ref[:, gs] = dkb[WINDOW:]
            dvp_ref[:, gs] = dvb[:WINDOW]
            dvc_ref[:, gs] = dvb[WINDOW:]

    band = pltpu.VMEM((2, 2 * WINDOW, LANES), BF16)
    return _pcall(body, name="swa_bwd", grid=(nb,),
                  in_specs=[pl.BlockSpec(memory_space=pltpu.SMEM), cur(D), cur(256), prev(256), cur(256), prev(256),
                            cur(D), cur(D), cur(D)],
                  out_specs=[cur(D), cur(256), cur(256), cur(256), cur(256), _full((8, LANES))],
                  out_shape=[SDS((T, D), F32), SDS((T, 256), F32), SDS((T, 256), F32), SDS((T, 256), F32), SDS((T, 256), F32),
                             SDS((8, LANES), F32)],
                  scratch=[band, band, pltpu.VMEM((SWA_HEADS, WINDOW, 2 * WINDOW), F32),
                           pltpu.VMEM((SWA_HEADS, WINDOW, 2 * WINDOW), F32), pltpu.VMEM((SWA_HEADS, WINDOW, 2 * WINDOW), BF16),
                           pltpu.VMEM((SWA_HEADS, WINDOW, 2 * WINDOW), BF16)],
                  sem=("arbitrary",))(sinks, qs, ks, ks, vs, vs, do, o, lse)


def _bwd_qkv(dqm, dkm, dvm, dqs, dkc, dkp, dvc, dvp, z, gq, gkv, wqb, wkn, wv, tab_m, tab_s):
    T = z.shape[0]
    tm = WINDOW
    nb = T // tm
    per = dqm.shape[2] // tm

    def body(dqm_ref, dkm_ref, dvm_ref, dqs_ref, dkc_ref, dkp_ref, dvc_ref, dvp_ref, qa_ref, kva_ref, gq_ref, gkv_ref,
             wqb_ref, wkn_ref, wv_ref, cm_ref, am_ref, bm_ref, cs_ref, as_ref, bs_ref,
             dq_out, dkn_out, dv_out, dsq_ref, drest_ref, dgq_ref, dgkv_ref):
        i = pl.program_id(0)

        @pl.when(i == 0)
        def _():
            dgq_ref[...] = jnp.zeros_like(dgq_ref)
            dgkv_ref[...] = jnp.zeros_like(dgkv_ref)

        cm, am, bm = cm_ref[...], -am_ref[...], -bm_ref[...]
        cs, as_, bs = cs_ref[...], -as_ref[...], -bs_ref[...]
        lane = lax.broadcasted_iota(jnp.int32, (tm, LANES), 1)
        nope = lane < MLA_NOPE
        roped = jnp.logical_and(lane >= MLA_NOPE, lane < MLA_NOPE + MLA_ROPE)
        dkr = jnp.zeros((tm, LANES), F32)
        for h in range(MLA_HEADS):
            sl = slice(LANES * h, LANES * (h + 1))
            dq_out[:, sl] = _rope(dqm_ref[0, sl, :].T, cm, am, bm, MLA_ROPE // 2).astype(BF16)
            dk_h = dkm_ref[0, sl, :].T
            dkn_out[:, sl] = jnp.where(nope, dk_h, 0.0).astype(BF16)
            dkr = dkr + jnp.where(roped, dk_h, 0.0)
        for j in range(D // LANES):
            sl = slice(LANES * j, LANES * (j + 1))
            dv_out[:, sl] = dvm_ref[0, sl, :].T.astype(BF16)
        dqn = _dot_nt(dq_out[...], wqb_ref[...])
        dkvn = _dot_nt(dkn_out[...], wkn_ref[...]) + _dot_nt(dv_out[...], wv_ref[...])
        dqa, dgq = _rms_bwd(dqn, qa_ref[...], gq_ref[...])
        dkva, dgkv = _rms_bwd(dkvn, kva_ref[...], gkv_ref[...])
        dgq_ref[...] += dgq
        dgkv_ref[...] += dgkv
        for j in range(D // LANES):
            sl = slice(LANES * j, LANES * (j + 1))
            dsq_ref[:, sl] = _rope(dqs_ref[:, sl], cs, as_, bs, SWA_HD // 2).astype(BF16)
        keep = (i < nb - 1).astype(F32)
        drest_ref[:, 0:256] = dqa.astype(BF16)
        for j in range(2):
            sl = slice(LANES * j, LANES * (j + 1))
            dk = dkc_ref[:, sl] + keep * dkp_ref[:, sl]
            drest_ref[:, 256 + LANES * j:256 + LANES * (j + 1)] = _rope(dk, cs, as_, bs, SWA_HD // 2).astype(BF16)
        drest_ref[:, 512:768] = (dvc_ref[...] + keep * dvp_ref[...]).astype(BF16)
        drest_ref[:, 768:896] = dkva.astype(BF16)
        drest_ref[:, 896:1024] = _rope(dkr, cm, am, bm, MLA_ROPE // 2).astype(BF16)

    nxt = pl.BlockSpec((tm, 256), lambda i: (jnp.minimum(i + 1, nb - 1), 0))
    tab = [_rows(tm, LANES)] * 6
    return _pcall(body, name="bwd_qkv", grid=(nb,),
                  in_specs=[pl.BlockSpec((1, 2048, tm), lambda i: (i // per, 0, i % per)),
                            pl.BlockSpec((1, 2048, tm), lambda i: (i // per, 0, i % per)),
                            pl.BlockSpec((1, 1024, tm), lambda i: (i // per, 0, i % per)), _rows(tm, 1024), _rows(tm, 256), nxt,
                            _rows(tm, 256), nxt, _rows(tm, 256, 12), _rows(tm, 128, 30), _full((1, Q_LORA)), _full((1, KV_LORA)),
                            _full((Q_LORA, 2048)), _full((KV_LORA, 2048)), _full((KV_LORA, 1024))] + tab,
                  out_specs=[_rows(tm, 2048), _rows(tm, 2048), _rows(tm, 1024), _rows(tm, 1024), _rows(tm, 1024),
                             _full((1, Q_LORA)), _full((1, KV_LORA))],
                  out_shape=[SDS((T, 2048), BF16), SDS((T, 2048), BF16), SDS((T, 1024), BF16), SDS((T, 1024), BF16),
                             SDS((T, 1024), BF16), SDS((1, Q_LORA), F32), SDS((1, KV_LORA), F32)],
                  sem=("arbitrary",))(dqm, dkm, dvm, dqs, dkc, dkp, dvc, dvp, z, z, gq, gkv, wqb, wkn, wv, *tab_m, *tab_s)


def _bwd_in(dsq, dga, dgb, drest, w_in_p, x, g1, dx1, tm):
    T = x.shape[0]

    def body(a_ref, b_ref, c_ref, d_ref, w_ref, x_ref, g_ref, dx1_ref, dx_ref, dg_ref):
        @pl.when(pl.program_id(0) == 0)
        def _():
            dg_ref[...] = jnp.zeros_like(dg_ref)

        dh = (_dot_nt(a_ref[...], w_ref[:, 0:1024]) + _dot_nt(b_ref[...], w_ref[:, 1024:2048])
              + _dot_nt(c_ref[...], w_ref[:, 2048:3072]) + _dot_nt(d_ref[...], w_ref[:, 3072:4096]))
        dx, dg = _rms_bwd(dh, x_ref[...], g_ref[...])
        dg_ref[...] += dg
        dx_ref[...] = dx1_ref[...] + dx

    r = _rows(tm, D)
    return _pcall(body, name="bwd_in", grid=(T // tm,),
                  in_specs=[r, r, r, r, _full((D, NZ)), r, _full((1, D)), r],
                  out_specs=[r, _full((1, D))], out_shape=[SDS((T, D), F32), SDS((1, D), F32)],
                  sem=("arbitrary",))(dsq, dga, dgb, drest, w_in_p, x, g1, dx1)


def _wgrad(a, g, name, into=None):
    T, K = a.shape
    N = g.shape[1]
    tk, tn, tt = min(K, 1024), min(N, 1024), min(T, 1024)
    if into is not None:
        buf, weight = into
        _, row0, lane0 = PACK_AT[weight]
        shard = {n: (r, c) for n, r, c in BIG}[weight]
        assert lane0 == 0 and shard[1] == D and tk % shard[0] == 0
        per_step = tk // shard[0]
    assert K % tk == 0 and N % tn == 0 and T % tt == 0, (a.shape, g.shape)
    steps = T // tt

    def body(a_ref, g_ref, *rest):
        o_ref, acc_ref = rest[-2:]
        t = pl.program_id(2)

        @pl.when(t == 0)
        def _():
            acc_ref[...] = jnp.zeros_like(acc_ref)

        acc_ref[...] += _dot_tn(a_ref[...].astype(BF16), g_ref[...].astype(BF16))

        @pl.when(t == steps - 1)
        def _():
            o_ref[...] = acc_ref[...].astype(o_ref.dtype).reshape(o_ref.shape)

    in_specs = [pl.BlockSpec((tt, tk), lambda k, n, t: (t, k)), pl.BlockSpec((tt, tn), lambda k, n, t: (t, n))]
    if into is None:
        return _pcall(body, name=name, grid=(K // tk, N // tn, steps), in_specs=in_specs,
                      out_specs=pl.BlockSpec((tk, tn), lambda k, n, t: (k, n)), out_shape=SDS((K, N), F32),
                      scratch=[pltpu.VMEM((tk, tn), F32)], sem=("parallel", "parallel", "arbitrary"))(a, g)
    assert row0 % shard[0] == 0 and (K // tk) * (N // tn) * per_step == N_CHIPS
    return _pcall(body, name=name, grid=(K // tk, N // tn, steps), in_specs=in_specs + [ANY],
                  out_specs=pl.BlockSpec((per_step, shard[0], tn), lambda k, n, t: (k + n, row0 // shard[0], 0)),
                  out_shape=SDS(buf.shape, buf.dtype),
                  scratch=[pltpu.VMEM((tk, tn), F32)], sem=("parallel", "parallel", "arbitrary"), aliases={2: 0})(a, g, buf)


def _adamw(w, packed_g, m, v, name):
    _, R, C = w.shape
    _, row0, lane0 = PACK_AT[name]
    tr = min(R, 256 if row0 % 256 == 0 else 128)
    assert row0 % tr == 0 and R % tr == 0

    def body(w_ref, g_ref, m_ref, v_ref, go_ref, d_ref, m2_ref, v2_ref):
        g_ = g_ref[:, lane0:lane0 + C]
        go_ref[0] = g_
        m2 = ADAM_B1 * m_ref[0] + (1.0 - ADAM_B1) * g_
        v2 = ADAM_B2 * v_ref[0] + (1.0 - ADAM_B2) * jnp.square(g_)
        m_hat = m2 / (1.0 - ADAM_B1 ** ADAM_STEP)
        v_hat = v2 / (1.0 - ADAM_B2 ** ADAM_STEP)
        d_ref[0] = -ADAM_LR * (m_hat / (jnp.sqrt(v_hat) + ADAM_EPS) + ADAM_WD * w_ref[0])
        m2_ref[0] = m2
        v2_ref[0] = v2

    r = pl.BlockSpec((1, tr, C), lambda i: (0, i, 0))
    return _pcall(body, name="adamw_" + name, grid=(R // tr,),
                  in_specs=[r, pl.BlockSpec((tr, D), lambda i: (row0 // tr + i, 0)), r, r], out_specs=[r] * 4,
                  out_shape=[SDS((1, R, C), F32)] * 4, sem=("parallel",))(w, packed_g, m, v)


def _adamw_small(w, parts, m, v):
    def body(w_ref, p_ref, m_ref, v_ref, g_ref, d_ref, m2_ref, v2_ref):
        g_ = p_ref[0]
        for k in range(1, N_DEV):
            g_ = g_ + p_ref[k]
        g_ref[...] = g_
        m2 = ADAM_B1 * m_ref[...] + (1.0 - ADAM_B1) * g_
        v2 = ADAM_B2 * v_ref[...] + (1.0 - ADAM_B2) * jnp.square(g_)
        m_hat = m2 / (1.0 - ADAM_B1 ** ADAM_STEP)
        v_hat = v2 / (1.0 - ADAM_B2 ** ADAM_STEP)
        d_ref[...] = -ADAM_LR * (m_hat / (jnp.sqrt(v_hat) + ADAM_EPS) + ADAM_WD * w_ref[...])
        m2_ref[...] = m2
        v2_ref[...] = v2

    s = _full((8, D))
    return _pcall(body, name="adamw_small", grid=(1,), in_specs=[s, _full((N_DEV, 8, D)), s, s], out_specs=[s] * 4,
                  out_shape=[SDS((8, D), F32)] * 4, sem=("arbitrary",))(w, parts, m, v)


ANY = pl.BlockSpec(memory_space=pl.ANY)


def _place():
    x, y, c = lax.axis_index("x"), lax.axis_index("y"), lax.axis_index("c")
    chips = [(1 - x, y), (x, 1 - y), (1 - x, 1 - y)]
    return x, y, c, chips


def _all_gather(wpk):
    rows = wpk.shape[0]
    HALF = rows // 2
    assert HALF % 16 == 0

    def body(in_ref, out_ref, send_sems, recv_sems):
        x, y, c, chips = _place()
        half = pl.ds(pl.multiple_of(c * HALF, 16), HALF)
        other = pl.ds(pl.multiple_of((1 - c) * HALF, 16), HALF)

        def copy(k, src, dst, to):
            return pltpu.make_async_remote_copy(src_ref=src, dst_ref=dst, send_sem=send_sems.at[k], recv_sem=recv_sems.at[k],
                                                device_id=to, device_id_type=MESH)

        first = [copy(k, in_ref.at[half], out_ref.at[2 * x + y, half], (cx, cy, c)) for k, (cx, cy) in enumerate(chips)]
        for cp in first:
            cp.start()
        passed = []
        for k, (cx, cy) in enumerate(chips):
            slot = out_ref.at[2 * cx + cy, half]
            copy(k, slot, slot, (x, y, c)).wait_recv()
            fwd = copy(3 + k, slot, slot, (x, y, 1 - c))
            fwd.start()
            passed.append(fwd)
        for k, (cx, cy) in enumerate(chips):
            slot = out_ref.at[2 * cx + cy, other]
            copy(3 + k, slot, slot, (x, y, c)).wait_recv()
        for cp in first + passed:
            cp.wait_send()

    return _pcall(body, name="all_gather_weights", in_specs=[ANY], out_specs=ANY,
                  out_shape=SDS((N_CHIPS, rows, D), BF16),
                  scratch=[pltpu.SemaphoreType.DMA((6,)), pltpu.SemaphoreType.DMA((6,))])(wpk)


HBM = pl.BlockSpec(memory_space=pltpu.HBM)
SEM = pl.BlockSpec(memory_space=pltpu.SEMAPHORE)
DATAFLOW = pltpu.SideEffectType.DATAFLOW_SIDE_EFFECTING


def _in_hbm(a):
    return pltpu.with_memory_space_constraint(a, pltpu.HBM)


def _gather_late_start(wpk, after):
    rows = wpk.shape[0]

    def body(in_ref, land_ref, after_ref, send_sems, recv_sems, in_thru, land_thru, token):
        x, y, c, chips = _place()
        for k, (cx, cy) in enumerate(chips):
            pltpu.make_async_remote_copy(src_ref=in_ref, dst_ref=land_ref.at[2 * x + y], send_sem=send_sems.at[k],
                                         recv_sem=recv_sems.at[k], device_id=(cx, cy, c), device_id_type=MESH).start()
        token[...] = jnp.zeros_like(token)

    return pl.pallas_call(
        body, name="gather_late_start",
        out_shape=(pltpu.SemaphoreType.DMA((3,)), pltpu.SemaphoreType.DMA((3,)), pltpu.HBM(wpk.shape, wpk.dtype),
                   pltpu.HBM((N_CHIPS, rows, D), wpk.dtype), SDS((8, LANES), F32)),
        in_specs=(HBM, HBM, ANY), out_specs=(SEM, SEM, HBM, HBM, pl.BlockSpec(memory_space=pltpu.VMEM)),
        input_output_aliases={0: 2, 1: 3}, compiler_params=pltpu.CompilerParams(has_side_effects=DATAFLOW),
    )(_in_hbm(wpk), _in_hbm(lax.empty((N_CHIPS, rows, D), wpk.dtype)), after)


def _gather_late_wait(send_sems, recv_sems, in_thru, land_thru, after):
    def body(in_ref, land_ref, send_sems, recv_sems, after_ref, after2_ref, in_dead, got_ref):
        x, y, c, chips = _place()
        for k, (cx, cy) in enumerate(chips):
            cp = pltpu.make_async_remote_copy(src_ref=in_ref, dst_ref=land_ref.at[2 * cx + cy], send_sem=send_sems.at[k],
                                              recv_sem=recv_sems.at[k], device_id=(cx, cy, c), device_id_type=MESH)
            cp.wait_send()
            cp.wait_recv()

    return pl.pallas_call(
        body, name="gather_late_wait",
        out_shape=(pltpu.HBM(in_thru.shape, in_thru.dtype), pltpu.HBM(land_thru.shape, land_thru.dtype)),
        in_specs=(HBM, HBM, SEM, SEM, ANY, ANY), out_specs=(HBM, HBM), input_output_aliases={0: 0, 1: 1},
        compiler_params=pltpu.CompilerParams(has_side_effects=DATAFLOW),
    )(in_thru, land_thru, send_sems, recv_sems, *after)[1]


def _rs_sibling(gpk):
    HALF = gpk.shape[1] // 2

    def body(in_ref, out_ref, send_sem, recv_sem):
        x, y, c, _ = _place()
        theirs = pl.ds(pl.multiple_of((1 - c) * HALF, 8), HALF)
        cp = pltpu.make_async_remote_copy(src_ref=in_ref.at[:, theirs], dst_ref=out_ref, send_sem=send_sem, recv_sem=recv_sem,
                                          device_id=(x, y, 1 - c), device_id_type=MESH)
        cp.start()
        cp.wait()

    return _pcall(body, name="rs_sibling", in_specs=[ANY], out_specs=ANY, out_shape=SDS((N_CHIPS, HALF, D), F32),
                  scratch=[pltpu.SemaphoreType.DMA, pltpu.SemaphoreType.DMA])(gpk)


def _rs_add_sibling(cidx, gpk, got):
    HALF = got.shape[1]
    th = HALF // 4
    nh = HALF // th
    assert th % 16 == 0

    def body(c_ref, a_ref, b_ref, o_ref):
        o_ref[...] = (a_ref[...] + b_ref[...]).astype(BF16)

    gs = pltpu.PrefetchScalarGridSpec(
        num_scalar_prefetch=1, grid=(N_CHIPS, nh),
        in_specs=[pl.BlockSpec((1, th, D), lambda j, i, c: (j, c[0] * nh + i, 0)), pl.BlockSpec((1, th, D), lambda j, i, c: (j, i, 0))],
        out_specs=pl.BlockSpec((1, th, D), lambda j, i, c: (j, i, 0)))
    return pl.pallas_call(body, name="rs_add_sibling", grid_spec=gs, out_shape=SDS((N_CHIPS, HALF, D), BF16),
                          compiler_params=pltpu.CompilerParams(dimension_semantics=("parallel", "parallel"),
                                                               vmem_limit_bytes=48 << 20))(cidx, gpk, got)


def _rs_chips(part, small):
    def body(p_ref, s_ref, o_ref, so_ref, send_sems, recv_sems, ssend_sems, srecv_sems, local_sem):
        x, y, c, chips = _place()
        me = 2 * x + y
        mine_s = pltpu.make_async_copy(s_ref, so_ref.at[4 * x + 2 * y + c], local_sem)
        mine_s.start()
        sends = []
        for k, (cx, cy) in enumerate(chips):
            sends.append(pltpu.make_async_remote_copy(src_ref=p_ref.at[2 * cx + cy], dst_ref=o_ref.at[me], send_sem=send_sems.at[k],
                                                      recv_sem=recv_sems.at[k], device_id=(cx, cy, c), device_id_type=MESH))
        peers = [(x, y, 1 - c)] + [(cx, cy, c) for cx, cy in chips] + [(cx, cy, 1 - c) for cx, cy in chips]
        for k, to in enumerate(peers):
            sends.append(pltpu.make_async_remote_copy(src_ref=s_ref, dst_ref=so_ref.at[4 * x + 2 * y + c], send_sem=ssend_sems.at[k],
                                                      recv_sem=srecv_sems.at[k], device_id=to, device_id_type=MESH))
        for cp in sends:
            cp.start()
        for k, (cx, cy) in enumerate(chips):
            slot = o_ref.at[2 * cx + cy]
            pltpu.make_async_remote_copy(src_ref=slot, dst_ref=slot, send_sem=send_sems.at[k], recv_sem=recv_sems.at[k],
                                         device_id=(x, y, c), device_id_type=MESH).wait_recv()
        for k, (px, py, pc) in enumerate(peers):
            slot = so_ref.at[4 * px + 2 * py + pc]
            pltpu.make_async_remote_copy(src_ref=slot, dst_ref=slot, send_sem=ssend_sems.at[k], recv_sem=srecv_sems.at[k],
                                         device_id=(x, y, c), device_id_type=MESH).wait_recv()
        for cp in sends:
            cp.wait_send()
        mine_s.wait()

    return _pcall(body, name="rs_chips", in_specs=[ANY, ANY], out_specs=[ANY, ANY],
                  out_shape=[SDS(part.shape, part.dtype), SDS((N_DEV, 8, D), F32)],
                  scratch=[pltpu.SemaphoreType.DMA((3,)), pltpu.SemaphoreType.DMA((3,)), pltpu.SemaphoreType.DMA((7,)),
                           pltpu.SemaphoreType.DMA((7,)), pltpu.SemaphoreType.DMA])(part, small)


def _rs_add_chips(qidx, part, parts):
    HALF = part.shape[1]
    th = HALF // 4
    assert th % 16 == 0

    def body(q_ref, own_ref, p_ref, o_ref):
        for me in range(N_CHIPS):
            @pl.when(q_ref[0] == me)
            def _(me=me):
                t = [(own_ref[0] if j == me else p_ref[j]).astype(F32) for j in range(N_CHIPS)]
                o_ref[...] = ((t[0] + t[1]) + t[2]) + t[3]

    gs = pltpu.PrefetchScalarGridSpec(
        num_scalar_prefetch=1, grid=(HALF // th,),
        in_specs=[pl.BlockSpec((1, th, D), lambda i, q: (q[0], i, 0)), pl.BlockSpec((N_CHIPS, th, D), lambda i, q: (0, i, 0))],
        out_specs=pl.BlockSpec((th, D), lambda i, q: (i, 0)))
    return pl.pallas_call(body, name="rs_add_chips", grid_spec=gs, out_shape=SDS((HALF, D), F32),
                          compiler_params=pltpu.CompilerParams(dimension_semantics=("parallel",),
                                                               vmem_limit_bytes=48 << 20))(qidx, part, parts)


def _rs_join(early, late):
    def body(e_ref, l_ref, eo_ref, lo_ref, send_sems, recv_sems):
        x, y, c, _ = _place()
        cps = [pltpu.make_async_remote_copy(src_ref=src, dst_ref=dst, send_sem=send_sems.at[k], recv_sem=recv_sems.at[k],
                                            device_id=(x, y, 1 - c), device_id_type=MESH)
               for k, (src, dst) in enumerate([(e_ref, eo_ref), (l_ref, lo_ref)])]
        for cp in cps:
            cp.start()
        for cp in cps:
            cp.wait()

    return _pcall(body, name="rs_join", in_specs=[ANY, ANY], out_specs=[ANY, ANY],
                  out_shape=[SDS(early.shape, F32), SDS(late.shape, F32)],
                  scratch=[pltpu.SemaphoreType.DMA((2,)), pltpu.SemaphoreType.DMA((2,))])(early, late)


def _reduce_late_start(gpk, after):
    rows = gpk.shape[1]
    HALF = rows // 2
    assert HALF % 16 == 0

    def body(in_ref, land_ref, after_ref, send_sems, recv_sems, in_thru, land_thru, token):
        x, y, c, chips = _place()
        me = 4 * x + 2 * y + c
        peers = [(x, y, 1 - c)] + [(cx, cy, c) for cx, cy in chips] + [(cx, cy, 1 - c) for cx, cy in chips]
        for k, (px, py, pc) in enumerate(peers):
            src = in_ref.at[2 * px + py, pl.ds(pl.multiple_of(pc * HALF, 16), HALF)]
            pltpu.make_async_remote_copy(src_ref=src, dst_ref=land_ref.at[me], send_sem=send_sems.at[k], recv_sem=recv_sems.at[k],
                                         device_id=(px, py, pc), device_id_type=MESH).start()
        token[...] = jnp.zeros_like(token)

    return pl.pallas_call(
        body, name="reduce_late_start",
        out_shape=(pltpu.SemaphoreType.DMA((7,)), pltpu.SemaphoreType.DMA((7,)), pltpu.HBM(gpk.shape, gpk.dtype),
                   pltpu.HBM((N_DEV, HALF, D), gpk.dtype), SDS((8, LANES), F32)),
        in_specs=(HBM, HBM, ANY), out_specs=(SEM, SEM, HBM, HBM, pl.BlockSpec(memory_space=pltpu.VMEM)),
        input_output_aliases={0: 2, 1: 3}, compiler_params=pltpu.CompilerParams(has_side_effects=DATAFLOW),
    )(_in_hbm(gpk), _in_hbm(lax.empty((N_DEV, HALF, D), gpk.dtype)), after)


def _reduce_late_wait(send_sems, recv_sems, in_thru, land_thru, after):
    def body(in_ref, land_ref, send_sems, recv_sems, after_ref, in_out, got_ref):
        x, y, c, chips = _place()
        peers = [(x, y, 1 - c)] + [(cx, cy, c) for cx, cy in chips] + [(cx, cy, 1 - c) for cx, cy in chips]
        for k, (px, py, pc) in enumerate(peers):
            cp = pltpu.make_async_remote_copy(src_ref=land_ref.at[0], dst_ref=land_ref.at[4 * px + 2 * py + pc],
                                              send_sem=send_sems.at[k], recv_sem=recv_sems.at[k],
                                              device_id=(px, py, pc), device_id_type=MESH)
            cp.wait_send()
            cp.wait_recv()

    return pl.pallas_call(
        body, name="reduce_late_wait",
        out_shape=(pltpu.HBM(in_thru.shape, in_thru.dtype), pltpu.HBM(land_thru.shape, land_thru.dtype)),
        in_specs=(HBM, HBM, SEM, SEM, ANY), out_specs=(HBM, HBM), input_output_aliases={0: 0, 1: 1},
        compiler_params=pltpu.CompilerParams(has_side_effects=DATAFLOW),
    )(in_thru, land_thru, send_sems, recv_sems, after)


def _reduce_late_add(didx, gpk, parts):
    HALF = parts.shape[1]
    th = HALF // 4
    nh = HALF // th
    assert th % 16 == 0

    def body(d_ref, own_ref, p_ref, o_ref):
        for me in range(N_DEV):
            @pl.when(d_ref[0] == me)
            def _(me=me):
                t = [(own_ref[0] if j == me else p_ref[j]).astype(F32) for j in range(N_DEV)]
                o_ref[...] = ((((((t[0] + t[1]) + t[2]) + t[3]) + t[4]) + t[5]) + t[6]) + t[7]

    gs = pltpu.PrefetchScalarGridSpec(
        num_scalar_prefetch=1, grid=(nh,),
        in_specs=[pl.BlockSpec((1, th, D), lambda i, d: (d[1], d[2] * nh + i, 0)), pl.BlockSpec((N_DEV, th, D), lambda i, d: (0, i, 0))],
        out_specs=pl.BlockSpec((th, D), lambda i, d: (i, 0)))
    return pl.pallas_call(body, name="reduce_late_add", grid_spec=gs, out_shape=SDS((HALF, D), F32),
                          compiler_params=pltpu.CompilerParams(dimension_semantics=("parallel",),
                                                               vmem_limit_bytes=48 << 20))(didx, gpk, parts)


def _pack_early(b, dtype):
    lanes = lambda a: jnp.pad(a.astype(dtype), ((0, 0), (0, D - a.shape[1])))
    pair = jnp.concatenate([b["w_q_b"].astype(dtype), b["w_ple"].astype(dtype), jnp.zeros((256, D - 640), dtype)], axis=1)
    return jnp.concatenate([lanes(b["w_in"]), pair, lanes(b["w_kv_b"])], axis=0)


def _pack_late(b, dtype):
    return jnp.concatenate([b[n].astype(dtype) for n in ("w_mla_up", "w_swa_up", "w_out", "w_ple_gate", "w_mlp_up", "w_mlp_down")],
                           axis=0)


def _unpack_shards(pk, which):
    return {n: pk[PACK_AT[n][1]:PACK_AT[n][1] + r, PACK_AT[n][2]:PACK_AT[n][2] + c] for n, r, c in BIG if PACK_AT[n][0] == which}


def _full_weights(gathered, own, chip, which):
    own_b = _unpack_shards(own, which)
    per_chip = [{n: jnp.where(chip == j, own_b[n], blk) for n, blk in _unpack_shards(gathered[j], which).items()}
                for j in range(N_CHIPS)]
    out = {}
    for n in own_b:
        shards = [pc[n] for pc in per_chip]
        if n == "w_in":
            out["w_in_p"] = _w_in_internal(shards)
        else:
            out[n] = jnp.concatenate(shards, axis=1 if n in COL_SHARDED else 0)
    return out


def _split_full_grads(grads, pack, dtype):
    shard = {n: (r, c) for n, r, c in BIG}
    chunks = []
    for j in range(N_CHIPS):
        blocks = {}
        for n, g in grads.items():
            if n == "w_in_p":
                blocks["w_in"] = _w_in_grad_shard(g, j)
                continue
            r, c = shard[n]
            blocks[n] = g[:, j * c:(j + 1) * c] if n in COL_SHARDED else g[j * r:(j + 1) * r]
        chunks.append(pack(blocks, dtype))
    return jnp.stack(chunks)


W_IN_SHARD = 936
W_IN_SEGMENTS = ((0, 256, (3072,)), (256, 384, (3840,)), (384, 416, (4032,)), (416, 1440, (0,)), (1440, 1504, (3328, 3392)),
                 (1504, 1568, (3456, 3520)), (1568, 1632, (3584, 3648)), (1632, 1696, (3712, 3776)), (1696, 3744, (1024,)))


def _w_in_internal(shards):
    def cols(a, b):
        out = []
        for j, s in enumerate(shards):
            lo, hi = max(a, W_IN_SHARD * j), min(b, W_IN_SHARD * (j + 1))
            if lo < hi:
                out.append(s[:, lo - W_IN_SHARD * j:hi - W_IN_SHARD * j])
        return out

    pieces = {}
    for a, b, places in W_IN_SEGMENTS:
        for at in places:
            pieces[at] = cols(a, b)
    zeros = lambda n: [jnp.zeros((D, n), shards[0].dtype)]
    pieces[3968] = zeros(64)
    pieces[4064] = zeros(32)
    return jnp.concatenate([piece for at in sorted(pieces) for piece in pieces[at]], axis=1)


def _w_in_grad_shard(g, j):
    def internal(a, b):
        out = []
        while a < b:
            end = min(b, (a // D + 1) * D)
            out.append(g[a // D][:, a % D:a % D + end - a])
            a = end
        return out

    out = []
    for a, b, places in W_IN_SEGMENTS:
        lo, hi = max(a, W_IN_SHARD * j), min(b, W_IN_SHARD * (j + 1))
        if lo < hi:
            parts = [internal(at + lo - a, at + hi - a) for at in places]
            if len(parts) == 1:
                out += parts[0]
            else:
                assert len(parts[0]) == len(parts[1]) == 1
                out.append(parts[0][0] + parts[1][0])
    return jnp.concatenate(out, axis=1)


def _local_step(x, p, tgt, w, small, late_weights, late_grads_out):
    T = x.shape[0]
    tm = 256
    tb = 256
    w_in_p = w["w_in_p"]
    wqb = jnp.pad(w["w_q_b"].reshape(Q_LORA, MLA_HEADS, 96), ((0, 0), (0, 0), (0, 32))).reshape(Q_LORA, 2048)
    wkv = w["w_kv_b"].reshape(KV_LORA, MLA_HEADS, 128)
    wkn = jnp.pad(wkv[:, :, :64], ((0, 0), (0, 0), (0, 64))).reshape(KV_LORA, 2048)
    wv = wkv[:, :, 64:].reshape(KV_LORA, 1024)
    tab_m = _rope_tables(T, "mla")
    tab_s = _rope_tables(T, "swa")
    g1, gq, gkv, sinks = small["g_mix_pre"], small["g_q_a"], small["g_kv_a"], small["sinks"]
    g2, g3, g4, g5 = small["g_mix_post"], small["g_mlp_pre"], small["g_mlp_post"], small["g_ple"]
    sink_vec = sinks.reshape(SWA_HEADS)

    z, h1 = _fwd_in(x, g1, w_in_p, tm)
    qn, kvn, qm, km, vm, qt, kt, vt, qs, ks, vs = _fwd_qkv(z, gq, gkv, wqb, wkn, wv, tab_m, tab_s, tb)
    om, lse_m = _mla_fwd(qm, km, vt, tb)
    os_, lse_s = _swa_fwd(sink_vec, qs, ks, vs)
    w = {**w, **late_weights((om, os_))}
    y, yo, au, bu, x1 = _fwd_mix(om, os_, z, x, w["w_mla_up"], w["w_swa_up"], w["w_out"], g2, tm)
    h2, a, u = _fwd_mlp_up(x1, g3, w["w_mlp_up"], tm)
    d, x2 = _fwd_mlp_down(u, w["w_mlp_down"], x1, g4, tm)
    loss, dx2, dgt, de0, dg5 = _ple_fwd_bwd(p, x2, tgt, w["w_ple"], g5, w["w_ple_gate"], tm)

    dd, da, dg4 = _bwd_mlp_down(dx2, d, g4, w["w_mlp_down"], a, tm)
    dx1, dg3 = _bwd_mlp_up(da, w["w_mlp_up"], x1, g3, dx2, tm)
    dyo, dg2, dau, dbu, dga, dgb, dos, delta_m, dom_t = _bwd_mix(dx1, yo, g2, w["w_out"], z, au, bu, w["w_mla_up"],
                                                                w["w_swa_up"], om, tb)
    gpk_late = lax.empty((N_CHIPS, PACK_ROWS["late"], D), BF16)
    for weight, a_, g_ in (("w_mla_up", om, dau), ("w_swa_up", os_, dbu), ("w_out", y, dyo), ("w_ple_gate", x2, dgt),
                           ("w_mlp_up", h2, da), ("w_mlp_down", u, dd)):
        gpk_late = _wgrad(a_, g_, "wgrad_" + weight[2:], into=(gpk_late, weight))
    token = late_grads_out(gpk_late)
    delta_m = delta_m + token[0, 0]
    dqm, dkm, dvm = _mla_bwd(qm, qt, km, kt, vm, dom_t, lse_m, delta_m, tb)
    dqs, dkc, dkp, dvc, dvp, dsink = _swa_bwd(sink_vec, qs, ks, vs, dos, os_, lse_s)
    dqb, dknb, dvb, dsq, drest, dgq, dgkv = _bwd_qkv(dqm, dkm, dvm, dqs, dkc, dkp, dvc, dvp, z, gq, gkv, wqb, wkn, wv,
                                                      tab_m, tab_s)
    gx, dg1 = _bwd_in(dsq, dga, dgb, drest, w_in_p, x, g1, dx1, tm)

    g_in_p = [_wgrad(h1, dsq, "wgrad_in_sq"), _wgrad(h1, dga, "wgrad_in_ga"), _wgrad(h1, dgb, "wgrad_in_gb"),
              _wgrad(h1, drest, "wgrad_in_rest")]
    g_qb_p = _wgrad(qn, dqb, "wgrad_q_b")
    g_kn_p = _wgrad(kvn, dknb, "wgrad_kv_b_nope")
    g_v_p = _wgrad(kvn, dvb, "wgrad_kv_b_v")
    grads = {
        "w_in_p": g_in_p,
        "w_q_b": g_qb_p.reshape(Q_LORA, MLA_HEADS, 128)[:, :, :96].reshape(Q_LORA, 1536),
        "w_kv_b": jnp.concatenate([g_kn_p.reshape(KV_LORA, MLA_HEADS, 128)[:, :, :64], g_v_p.reshape(KV_LORA, MLA_HEADS, 64)],
                                  axis=2).reshape(KV_LORA, 2048),
        "w_ple": _wgrad(p, de0, "wgrad_ple"),
    }
    small_grads = {"g_mix_pre": dg1, "g_q_a": dgq, "g_kv_a": dgkv, "sinks": dsink[0:1, 0:SWA_HEADS], "g_mix_post": dg2,
                   "g_mlp_pre": dg3, "g_mlp_post": dg4, "g_ple": dg5}
    return loss, gx, grads, small_grads


def _pack_small(vals, fill, scalar=None):
    wide = [vals[n] for n, k in SMALL if k == D]
    narrow = [vals[n] for n, k in SMALL if k != D]
    used = sum(k for _, k in SMALL if k != D)
    last = jnp.concatenate(narrow + [jnp.full((1, D - used), fill, F32)], axis=1)
    rest = jnp.full((2, D), fill, F32)
    if scalar is not None:
        rest = jnp.concatenate([jnp.concatenate([scalar, rest[0:1, 1:]], axis=1), rest[1:2]], axis=0)
    return jnp.concatenate(wide + [last, rest], axis=0)


def _unpack_small(pk):
    out, row, off = {}, 0, 0
    for n, k in SMALL:
        if k == D:
            out[n] = pk[row:row + 1]
            row += 1
    for n, k in SMALL:
        if k != D:
            out[n] = pk[5:6, off:off + k]
            off += k
    return out


def kernel(x, p, g_mix_pre, w_in, g_q_a, w_q_b, g_kv_a, w_kv_b, sinks, w_mla_up, w_swa_up, w_out, g_mix_post, g_mlp_pre, w_mlp_up, w_mlp_down, g_mlp_post, w_ple, g_ple, w_ple_gate, loss_target, m_g_mix_pre, m_w_in, m_g_q_a, m_w_q_b, m_g_kv_a, m_w_kv_b, m_sinks, m_w_mla_up, m_w_swa_up, m_w_out, m_g_mix_post, m_g_mlp_pre, m_w_mlp_up, m_w_mlp_down, m_g_mlp_post, m_w_ple, m_g_ple, m_w_ple_gate, v_g_mix_pre, v_w_in, v_g_q_a, v_w_q_b, v_g_kv_a, v_w_kv_b, v_sinks, v_w_mla_up, v_w_swa_up, v_w_out, v_g_mix_post, v_g_mlp_pre, v_w_mlp_up, v_w_mlp_down, v_g_mlp_post, v_w_ple, v_g_ple, v_w_ple_gate):
    given = dict(locals())
    big_w = {n: given[n][0] for n, _, _ in BIG}
    small_w = {n: given[n] for n, _ in SMALL}
    small_m = {n: given["m_" + n] for n, _ in SMALL}
    small_v = {n: given["v_" + n] for n, _ in SMALL}

    core = lax.axis_index("c")
    chip = 2 * lax.axis_index("x") + lax.axis_index("y")
    core_i = core.astype(jnp.int32).reshape(1)
    chip_i = chip.astype(jnp.int32).reshape(1)
    dev_i = jnp.stack([2 * chip + core, chip, core]).astype(jnp.int32)

    own_early = _pack_early(big_w, BF16)
    own_late = _pack_late(big_w, BF16)
    got_early = _all_gather(own_early)
    late_flight = _gather_late_start(own_late, got_early)
    weights = _full_weights(got_early, own_early, chip, "early")
    step_small = {**small_w, "g_mix_pre": small_w["g_mix_pre"] + late_flight[4][0, 0]}

    def late_weights(after):
        return _full_weights(_gather_late_wait(*late_flight[:4], after), own_late, chip, "late")

    flight = {}

    def late_grads_out(gpk_late):
        flight["late"] = _reduce_late_start(gpk_late, dev_i)
        return flight["late"][4]

    loss_blk, gx, grads, small_grads = _local_step(x[0], p[0, 0], loss_target[0], weights, step_small, late_weights,
                                                   late_grads_out)

    gpk = _split_full_grads(grads, _pack_early, F32)
    got = _rs_sibling(gpk)
    part = _rs_add_sibling(core_i, gpk, got)
    parts, small_parts = _rs_chips(part, _pack_small(small_grads, 0.0, loss_blk[0:1, 0:1]))
    mine_early = _rs_add_chips(chip_i, part, parts)
    gpk_late, parts_late = _reduce_late_wait(*flight["late"][:4], mine_early)
    mine_late = _reduce_late_add(dev_i, gpk_late, parts_late)
    theirs_early, theirs_late = _rs_join(mine_early, mine_late)
    joined = {"early": jnp.where(core == 0, jnp.concatenate([mine_early, theirs_early]), jnp.concatenate([theirs_early, mine_early])),
              "late": jnp.where(core == 0, jnp.concatenate([mine_late, theirs_late]), jnp.concatenate([theirs_late, mine_late]))}

    g_small_pk, d_small_pk, m_small_pk, v_small_pk = _adamw_small(
        _pack_small(small_w, 0.0), small_parts, _pack_small(small_m, 0.0), _pack_small(small_v, 1.0))
    loss = g_small_pk[6, 0]
    g_small, d_small = _unpack_small(g_small_pk), _unpack_small(d_small_pk)
    m_small, v_small = _unpack_small(m_small_pk), _unpack_small(v_small_pk)

    out_g, out_d, out_m, out_v = dict(g_small), dict(d_small), dict(m_small), dict(v_small)
    for n, _, _ in BIG:
        out_g[n], out_d[n], out_m[n], out_v[n] = _adamw(given[n], joined[PACK_AT[n][0]], given["m_" + n], given["v_" + n], n)
    order = ["g_mix_pre", "w_in", "g_q_a", "w_q_b", "g_kv_a", "w_kv_b", "sinks", "w_mla_up", "w_swa_up", "w_out", "g_mix_post",
             "g_mlp_pre", "w_mlp_up", "w_mlp_down", "g_mlp_post", "w_ple", "g_ple", "w_ple_gate"]
    return (loss, gx[None], *[out_g[n] for n in order], *[out_d[n] for n in order], *[out_m[n] for n in order],
            *[out_v[n] for n in order])
```

```python
import math

import jax
import jax.numpy as jnp
from jax import lax
from jax.experimental import pallas as pl
from jax.experimental.pallas import tpu as pltpu

F32 = jnp.float32
BF16 = jnp.bfloat16
SDS = jax.ShapeDtypeStruct

D = 1024
D_FF = 4096
PLE = 256
Q_LORA = 256
KV_LORA = 128
MLA_HEADS = 16
MLA_NOPE = 64
MLA_ROPE = 32
SWA_HEADS = 16
SWA_HD = 64
WINDOW = 128
ROPE_THETA = 10000.0
EPS = 1e-6
NEG = -1e30
NZ = 4096
MLA_SCALE = (MLA_NOPE + MLA_ROPE) ** -0.5
LOG2_E = math.log2(math.e)
MLA_LOG2_SCALE = MLA_SCALE * LOG2_E
SWA_SCALE = SWA_HD ** -0.5

ADAM_LR = 0.001
ADAM_B1 = 0.9
ADAM_B2 = 0.999
ADAM_EPS = 1e-08
ADAM_WD = 0.01
ADAM_STEP = 10

LANES = 128
ATT_COLS = 128
N_CHIPS = 4
N_DEV = 8
MESH = pl.DeviceIdType.MESH

NT = (((1,), (1,)), ((), ()))
TN = (((0,), (0,)), ((), ()))

BIG = (("w_in", 1024, 936), ("w_q_b", 256, 384), ("w_kv_b", 128, 512), ("w_mla_up", 256, 1024),
       ("w_swa_up", 256, 1024), ("w_out", 256, 1024), ("w_mlp_up", 1024, 1024), ("w_mlp_down", 1024, 1024),
       ("w_ple", 256, 256), ("w_ple_gate", 256, 1024))
COL_SHARDED = ("w_in", "w_q_b", "w_kv_b", "w_mlp_up", "w_ple")
PACK_AT = {"w_in": ("early", 0, 0), "w_q_b": ("early", 1024, 0), "w_ple": ("early", 1024, 384), "w_kv_b": ("early", 1280, 0),
           "w_mla_up": ("late", 0, 0), "w_swa_up": ("late", 256, 0), "w_out": ("late", 512, 0), "w_ple_gate": ("late", 768, 0),
           "w_mlp_up": ("late", 1024, 0), "w_mlp_down": ("late", 2048, 0)}
PACK_ROWS = {"early": 1408, "late": 3072}
SMALL = (("g_mix_pre", 1024), ("g_q_a", 256), ("g_kv_a", 128), ("sinks", 16), ("g_mix_post", 1024),
         ("g_mlp_pre", 1024), ("g_mlp_post", 1024), ("g_ple", 1024))


def _dot(a, b):
    return jnp.dot(a, b, preferred_element_type=F32)


def _dot_nt(a, b):
    return lax.dot_general(a, b, NT, preferred_element_type=F32)


def _dot_tn(a, b):
    return lax.dot_general(a, b, TN, preferred_element_type=F32)


def _pcall(body, *, name, out_shape, grid=(), in_specs=None, out_specs=None, scratch=(), sem=None, vmem_mb=48, aliases=None):
    params = dict(vmem_limit_bytes=vmem_mb << 20)
    if sem is not None:
        params["dimension_semantics"] = sem
    return pl.pallas_call(body, name=name, grid=grid, in_specs=in_specs, out_specs=out_specs, out_shape=out_shape,
                          scratch_shapes=list(scratch), input_output_aliases=aliases or {},
                          compiler_params=pltpu.CompilerParams(**params))


def _rows(tm, n, col=0):
    return pl.BlockSpec((tm, n), lambda i: (i, col))


def _full(shape):
    return pl.BlockSpec(shape, lambda i: (0,) * len(shape))


def _rms(x, g):
    r = lax.rsqrt(jnp.mean(x * x, axis=-1, keepdims=True) + EPS)
    return x * r * g


def _rms_bwd(dy, x, g):
    r = lax.rsqrt(jnp.mean(x * x, axis=-1, keepdims=True) + EPS)
    xn = x * r
    dn = dy * g
    dx = r * (dn - xn * jnp.mean(dn * xn, axis=-1, keepdims=True))
    return dx, jnp.sum(dy * xn, axis=0, keepdims=True)


def _sigmoid(x):
    return 1.0 / (1.0 + jnp.exp(-x))


def _rope(x, c, a, b, half):
    return x * c + pltpu.roll(x, LANES - half, 1) * a + pltpu.roll(x, half, 1) * b


def _rope_tables(T, kind):
    lane = jnp.arange(LANES)
    if kind == "mla":
        half = MLA_ROPE // 2
        rel = lane - MLA_NOPE
        on = (rel >= 0) & (rel < MLA_ROPE)
        d = MLA_ROPE
    else:
        half = SWA_HD // 2
        rel = lane % SWA_HD
        on = jnp.ones((LANES,), bool)
        d = SWA_HD
    first = on & (rel < half)
    second = on & (rel >= half)
    f = jnp.where(first, rel, rel - half).astype(F32)
    inv = jnp.exp(-math.log(ROPE_THETA) * f * (2.0 / d))
    ang = jnp.arange(T, dtype=F32)[:, None] * inv[None, :]
    cos, sin = jnp.cos(ang), jnp.sin(ang)
    c = jnp.where(on[None], cos, 1.0)
    a = jnp.where(first[None], -sin, 0.0)
    b = jnp.where(second[None], sin, 0.0)
    return c, a, b


def _fwd_in(x, g1, w_in_p, tm):
    T = x.shape[0]

    def body(x_ref, g_ref, w_ref, z_ref, h_ref):
        h = _rms(x_ref[...], g_ref[...]).astype(BF16)
        h_ref[...] = h
        z_ref[...] = _dot(h, w_ref[...])

    return _pcall(body, name="fwd_in", grid=(T // tm,),
                  in_specs=[_rows(tm, D), _full((1, D)), _full((D, NZ))],
                  out_specs=[_rows(tm, NZ), _rows(tm, D)],
                  out_shape=[SDS((T, NZ), F32), SDS((T, D), BF16)], sem=("parallel",))(x, g1, w_in_p)


def _fwd_qkv(z, gq, gkv, wqb, wkn, wv, tab_m, tab_s, tm):
    T = z.shape[0]

    def body(qa_ref, sq_ref, skd_ref, svd_ref, kva_ref, kr_ref, gq_ref, gkv_ref, wqb_ref, wkn_ref, wv_ref,
             cm_ref, am_ref, bm_ref, cs_ref, as_ref, bs_ref,
             qn_ref, kvn_ref, qm_ref, km_ref, vm_ref, qt_ref, kt_ref, vt_ref, qs_ref, ks_ref, vs_ref):
        qn = _rms(qa_ref[...], gq_ref[...]).astype(BF16)
        qn_ref[...] = qn
        kvn = _rms(kva_ref[...], gkv_ref[...]).astype(BF16)
        kvn_ref[...] = kvn
        cm, am, bm = cm_ref[...], am_ref[...], bm_ref[...]
        cs, as_, bs = cs_ref[...], as_ref[...], bs_ref[...]
        k_rope = _rope(kr_ref[...], cm, am, bm, MLA_ROPE // 2)
        vt_row = lax.broadcasted_iota(jnp.int32, (LANES, tm), 0)
        v_all = _dot(kvn, wv_ref[...])
        q_all = _dot(qn, wqb_ref[...])
        k_all = _dot(kvn, wkn_ref[...])
        for j in range(D // LANES):
            sl = slice(LANES * j, LANES * (j + 1))
            v = v_all[:, sl]
            vm_ref[:, sl] = v.astype(BF16)
            v_t = v.T
            for hh, rows64 in enumerate((v_t, pltpu.roll(v_t, 64, 0))):
                blk = jnp.where(vt_row < 64, rows64, jnp.where(vt_row == 64, 1.0, 0.0))
                vt_ref[0, LANES * (2 * j + hh):LANES * (2 * j + hh + 1), :] = blk.astype(BF16)
        for h in range(MLA_HEADS):
            sl = slice(LANES * h, LANES * (h + 1))
            qh = _rope(q_all[:, sl], cm, am, bm, MLA_ROPE // 2)
            qm_ref[:, sl] = qh.astype(BF16)
            qt_ref[0, sl, :] = qh.T.astype(BF16)
            k = k_all[:, sl] + k_rope
            km_ref[:, sl] = k.astype(BF16)
            kt_ref[0, sl, :] = k.T.astype(BF16)
        for j in range(D // LANES):
            sl = slice(LANES * j, LANES * (j + 1))
            qs_ref[:, sl] = _rope(sq_ref[:, sl], cs, as_, bs, SWA_HD // 2).astype(BF16)
        for j in range(2):
            sl = slice(LANES * j, LANES * (j + 1))
            ks_ref[:, sl] = _rope(skd_ref[:, sl], cs, as_, bs, SWA_HD // 2).astype(BF16)
        vs_ref[...] = svd_ref[...].astype(BF16)

    tab = [_rows(tm, LANES)] * 6
    return _pcall(body, name="fwd_qkv", grid=(T // tm,),
                  in_specs=[_rows(tm, 256, 12), _rows(tm, 1024, 0), _rows(tm, 256, 13), _rows(tm, 256, 14),
                            _rows(tm, 128, 30), _rows(tm, 128, 31), _full((1, Q_LORA)), _full((1, KV_LORA)),
                            _full((Q_LORA, 2048)), _full((KV_LORA, 2048)), _full((KV_LORA, 1024))] + tab,
                  out_specs=[_rows(tm, Q_LORA), _rows(tm, KV_LORA), _rows(tm, 2048), _rows(tm, 2048), _rows(tm, 1024),
                             pl.BlockSpec((1, 2048, tm), lambda i: (i, 0, 0)), pl.BlockSpec((1, 2048, tm), lambda i: (i, 0, 0)),
                             pl.BlockSpec((1, 2048, tm), lambda i: (i, 0, 0)),
                             _rows(tm, 1024), _rows(tm, 256), _rows(tm, 256)],
                  out_shape=[SDS((T, Q_LORA), BF16), SDS((T, KV_LORA), BF16), SDS((T, 2048), BF16), SDS((T, 2048), BF16),
                             SDS((T, 1024), BF16), SDS((T // tm, 2048, tm), BF16), SDS((T // tm, 2048, tm), BF16),
                             SDS((T // tm, 2048, tm), BF16),
                             SDS((T, 1024), BF16), SDS((T, 256), BF16), SDS((T, 256), BF16)],
                  sem=("parallel",))(z, z, z, z, z, z, gq, gkv, wqb, wkn, wv, *tab_m, *tab_s)


def _mla_fwd(qm, km, vt, tb):
    T = qm.shape[0]
    nb = T // tb
    cc = ATT_COLS

    def body(q_ref, k_ref, vt_ref, o_ref, l_ref, s_ref, p_ref, al_ref, m_ref, acc_ref):
        i = pl.program_id(1)
        m_ref[...] = jnp.full(m_ref.shape, NEG, F32)
        acc_ref[...] = jnp.zeros_like(acc_ref)
        p_ref[1] = jnp.zeros(p_ref.shape[1:], BF16)
        al_ref[1] = jnp.ones(al_ref.shape[1:], F32)
        key = lax.broadcasted_iota(jnp.int32, (tb, cc), 0)
        qry = lax.broadcasted_iota(jnp.int32, (tb, cc), 1)

        def scores(j, slot):
            off = pl.multiple_of(j * tb, tb)
            for hh in range(2):
                sl = slice(LANES * hh, LANES * (hh + 1))
                s_ref[slot, hh] = _dot_nt(k_ref[pl.ds(off, tb), sl], q_ref[:, sl])

        def softmax(slot, diagonal):
            chains = [(hh, slice(cc * c, cc * (c + 1)), c) for hh in range(2) for c in range(tb // cc)]

            def scaled(hh, cols, c):
                t = s_ref[slot, hh, :, cols] * MLA_LOG2_SCALE
                return jnp.where(key <= qry + cc * c, t, NEG) if diagonal else t

            tops = []
            for hh, cols, c in chains:
                if diagonal:
                    top = jnp.max(scaled(hh, cols, c), axis=0, keepdims=True)
                else:
                    top = jnp.max(s_ref[slot, hh, :, cols], axis=0, keepdims=True) * MLA_LOG2_SCALE
                m_old = m_ref[hh, :, cols]
                mn = jnp.maximum(m_old, top)
                m_ref[hh, :, cols] = mn
                al_ref[slot, hh, :, cols] = jnp.exp2(m_old - mn)
                tops.append(mn)
            for (hh, cols, c), mn in zip(chains, tops):
                p_ref[slot, hh, :, cols] = jnp.exp2(scaled(hh, cols, c) - mn).astype(BF16)

        def accumulate(j, slot):
            for hh in range(2):
                acc_ref[hh] = al_ref[slot, hh] * acc_ref[hh] + _dot(vt_ref[j, LANES * hh:LANES * (hh + 1), :], p_ref[slot, hh])

        def step(t, carry):
            scores(2 * t + 1, 1)
            accumulate(jnp.maximum(2 * t - 1, 0), 1)
            softmax(0, False)
            scores(2 * t + 2, 0)
            accumulate(2 * t, 0)
            softmax(1, False)
            return carry

        scores(0, 0)
        lax.fori_loop(0, i // 2, step, 0)

        @pl.when(i % 2 == 1)
        def _():
            scores(i, 1)
            accumulate(jnp.maximum(i - 2, 0), 1)
            softmax(0, False)
            accumulate(i - 1, 0)
            softmax(1, True)
            accumulate(i, 1)

        @pl.when(i % 2 == 0)
        def _():
            accumulate(jnp.maximum(i - 1, 0), 1)
            softmax(0, True)
            accumulate(i, 0)
        den = [acc_ref[hh, 64:65, :] for hh in range(2)]
        o_ref[...] = jnp.concatenate([acc_ref[hh, 0:64, :] / den[hh] for hh in range(2)], axis=0).T
        sub = lax.broadcasted_iota(jnp.int32, (8, tb), 0)
        lse = [m_ref[hh] + jnp.log(den[hh]) * LOG2_E for hh in range(2)]
        l_ref[0, 0] = jnp.where(sub == 0, lse[0], jnp.where(sub == 1, lse[1], 0.0))

    return _pcall(body, name="mla_fwd", grid=(MLA_HEADS // 2, nb),
                  in_specs=[pl.BlockSpec((tb, 256), lambda p, i: (i, p)), pl.BlockSpec((T, 256), lambda p, i: (0, p)),
                            pl.BlockSpec((nb, 2 * LANES, tb), lambda p, i: (0, p, 0))],
                  out_specs=[pl.BlockSpec((tb, LANES), lambda p, i: (i, p)),
                             pl.BlockSpec((1, 1, 8, tb), lambda p, i: (p, i, 0, 0))],
                  out_shape=[SDS((T, D), F32), SDS((MLA_HEADS // 2, nb, 8, tb), F32)],
                  scratch=[pltpu.VMEM((2, 2, tb, tb), F32), pltpu.VMEM((2, 2, tb, tb), BF16), pltpu.VMEM((2, 2, 1, tb), F32),
                           pltpu.VMEM((2, 1, tb), F32), pltpu.VMEM((2, LANES, tb), F32)],
                  sem=("parallel", "arbitrary"))(qm, km, vt)


def _swa_mask(n):
    row = lax.broadcasted_iota(jnp.int32, (WINDOW, 2 * WINDOW), 0)
    col = lax.broadcasted_iota(jnp.int32, (WINDOW, 2 * WINDOW), 1)
    rel = row - col + WINDOW
    return (rel >= 0) & (rel < WINDOW) & ((col >= WINDOW) | (n > 0))


def _swa_specs(T):
    nb = T // WINDOW
    cur = lambda w: pl.BlockSpec((WINDOW, w), lambda n: (n, 0))
    prev = lambda w: pl.BlockSpec((WINDOW, w), lambda n: (jnp.maximum(n - 1, 0), 0))
    return nb, cur, prev


def _swa_fwd(sinks, qs, ks, vs):
    T = qs.shape[0]
    nb, cur, prev = _swa_specs(T)

    def body(sink_ref, q_ref, kc_ref, kp_ref, vc_ref, vp_ref, o_ref, l_ref, kb_ref, vb_ref, s_ref, p_ref):
        n = pl.program_id(0)
        mask = _swa_mask(n)
        lo = lax.broadcasted_iota(jnp.int32, (WINDOW, LANES), 1) < 64
        hi = jnp.logical_not(lo)
        for g in range(2):
            gs = slice(LANES * g, LANES * (g + 1))
            kb_ref[g] = jnp.concatenate([kp_ref[:, gs], kc_ref[:, gs]], axis=0)
            vb_ref[g] = jnp.concatenate([vp_ref[:, gs], vc_ref[:, gs]], axis=0)
        for h in range(SWA_HEADS):
            qp = q_ref[:, LANES * (h // 2):LANES * (h // 2 + 1)]
            qh = jnp.where(lo if h % 2 == 0 else hi, qp, jnp.zeros_like(qp))
            s_ref[h] = _dot_nt(qh, kb_ref[h // 8])
        for j in range(SWA_HEADS // 2):
            sl = slice(LANES * j, LANES * (j + 1))
            lses = []
            for h in (2 * j, 2 * j + 1):
                s = jnp.where(mask, s_ref[h] * SWA_SCALE, NEG)
                sk = sink_ref[h]
                m = jnp.maximum(jnp.max(s, axis=1, keepdims=True), sk)
                e = jnp.exp(s - m)
                den = jnp.sum(e, axis=1, keepdims=True) + jnp.exp(sk - m)
                p_ref[h] = (e / den).astype(BF16)
                lses.append(jnp.broadcast_to(m + jnp.log(den), (WINDOW, LANES)))
            l_ref[:, sl] = jnp.where(lo, lses[0], lses[1])
        for j in range(SWA_HEADS // 2):
            vb = vb_ref[j // 4]
            o_ref[:, LANES * j:LANES * (j + 1)] = jnp.where(lo, _dot(p_ref[2 * j], vb), _dot(p_ref[2 * j + 1], vb))

    return _pcall(body, name="swa_fwd", grid=(nb,),
                  in_specs=[pl.BlockSpec(memory_space=pltpu.SMEM), cur(D), cur(256), prev(256), cur(256), prev(256)],
                  out_specs=[cur(D), cur(D)], out_shape=[SDS((T, D), F32)] * 2,
                  scratch=[pltpu.VMEM((2, 2 * WINDOW, LANES), BF16), pltpu.VMEM((2, 2 * WINDOW, LANES), BF16),
                           pltpu.VMEM((SWA_HEADS, WINDOW, 2 * WINDOW), F32), pltpu.VMEM((SWA_HEADS, WINDOW, 2 * WINDOW), BF16)],
                  sem=("parallel",))(sinks, qs, ks, ks, vs, vs)


def _fwd_mix(om, os_, z, x, wmu, wsu, wo, g2, tm):
    T = x.shape[0]

    def body(om_ref, os_ref, ga_ref, gb_ref, x_ref, wmu_ref, wsu_ref, wo_ref, g2_ref,
             y_ref, yo_ref, au_ref, bu_ref, x1_ref):
        au = _dot(om_ref[...].astype(BF16), wmu_ref[...])
        bu = _dot(os_ref[...].astype(BF16), wsu_ref[...])
        au_ref[...] = au
        bu_ref[...] = bu
        y = (_sigmoid(ga_ref[...]) * au + _sigmoid(gb_ref[...]) * bu).astype(BF16)
        y_ref[...] = y
        yo = _dot(y, wo_ref[...])
        yo_ref[...] = yo
        x1_ref[...] = x_ref[...] + _rms(yo, g2_ref[...])

    r = _rows(tm, D)
    w = _full((D, D))
    return _pcall(body, name="fwd_mix", grid=(T // tm,),
                  in_specs=[r, r, _rows(tm, D, 1), _rows(tm, D, 2), r, w, w, w, _full((1, D))],
                  out_specs=[r] * 5,
                  out_shape=[SDS((T, D), BF16), SDS((T, D), F32), SDS((T, D), F32), SDS((T, D), F32), SDS((T, D), F32)],
                  sem=("parallel",))(om, os_, z, z, x, wmu, wsu, wo, g2)


def _fwd_mlp_up(x1, g3, w1, tm):
    T = x1.shape[0]

    def body(x_ref, g_ref, w_ref, h_ref, u_ref):
        h = _rms(x_ref[...], g_ref[...]).astype(BF16)
        h_ref[...] = h
        u_ref[...] = jnp.square(jnp.maximum(_dot(h, w_ref[...]), 0.0)).astype(BF16)

    return _pcall(body, name="fwd_mlp_up", grid=(T // tm,),
                  in_specs=[_rows(tm, D), _full((1, D)), _full((D, D_FF))],
                  out_specs=[_rows(tm, D), _rows(tm, D_FF)],
                  out_shape=[SDS((T, D), BF16), SDS((T, D_FF), BF16)],
                  sem=("parallel",))(x1, g3, w1)


def _fwd_mlp_down(u, w2, x1, g4, tm):
    T = x1.shape[0]

    def body(u_ref, w_ref, x_ref, g_ref, d_ref, x2_ref):
        d = _dot(u_ref[...], w_ref[...])
        d_ref[...] = d
        x2_ref[...] = x_ref[...] + _rms(d, g_ref[...])

    return _pcall(body, name="fwd_mlp_down", grid=(T // tm,),
                  in_specs=[_rows(tm, D_FF), _full((D_FF, D)), _rows(tm, D), _full((1, D))],
                  out_specs=[_rows(tm, D), _rows(tm, D)], out_shape=[SDS((T, D), F32)] * 2,
                  sem=("parallel",))(u, w2, x1, g4)


def _ple_fwd_bwd(p, x2, tgt, wple, g5, wpg, tm):
    T = x2.shape[0]

    def body(p_ref, x2_ref, t_ref, wple_ref, g5_ref, wpg_ref, loss_ref, dx2_ref, dgt_ref, de0_ref, dg5_ref):
        @pl.when(pl.program_id(0) == 0)
        def _():
            loss_ref[...] = jnp.zeros_like(loss_ref)
            dg5_ref[...] = jnp.zeros_like(dg5_ref)

        e0 = _dot(p_ref[...].astype(BF16), wple_ref[...])
        g5 = g5_ref[...]
        r = lax.rsqrt(jnp.mean(e0 * e0, axis=-1, keepdims=True) + EPS)
        en = e0 * r
        e = en * g5
        x2 = x2_ref[...]
        s = _sigmoid(_dot(x2.astype(BF16), wpg_ref[...]))
        diff = x2 + s * e - t_ref[...]
        sq = jnp.sum(jnp.sum(diff * diff, axis=1, keepdims=True), axis=0, keepdims=True)
        loss_ref[...] += jnp.broadcast_to(sq * (0.5 / D), loss_ref.shape)
        dx3 = diff * (1.0 / D)
        de = dx3 * s
        dgt = (dx3 * e * s * (1.0 - s)).astype(BF16)
        dgt_ref[...] = dgt
        dn = de * g5
        de0_ref[...] = (r * (dn - en * jnp.mean(dn * en, axis=-1, keepdims=True))).astype(BF16)
        dg5_ref[...] += jnp.sum(de * en, axis=0, keepdims=True)
        dx2_ref[...] = dx3 + _dot_nt(dgt, wpg_ref[...])

    r = _rows(tm, D)
    return _pcall(body, name="ple_fwd_bwd", grid=(T // tm,),
                  in_specs=[_rows(tm, PLE), r, r, _full((PLE, D)), _full((1, D)), _full((D, D))],
                  out_specs=[_full((8, LANES)), r, r, r, _full((1, D))],
                  out_shape=[SDS((8, LANES), F32), SDS((T, D), F32), SDS((T, D), BF16), SDS((T, D), BF16), SDS((1, D), F32)],
                  sem=("arbitrary",))(p, x2, tgt, wple, g5, wpg)


def _bwd_mlp_down(dx2, d, g4, w2, u, tm):
    T = dx2.shape[0]

    def body(dx_ref, d_ref, g_ref, w_ref, u_ref, dd_ref, da_ref, dg_ref):
        @pl.when(pl.program_id(0) == 0)
        def _():
            dg_ref[...] = jnp.zeros_like(dg_ref)

        dd, dg = _rms_bwd(dx_ref[...], d_ref[...], g_ref[...])
        dg_ref[...] += dg
        ddb = dd.astype(BF16)
        dd_ref[...] = ddb
        du = _dot_nt(ddb, w_ref[...])
        da_ref[...] = (du * (2.0 * jnp.sqrt(u_ref[...].astype(F32)))).astype(BF16)

    return _pcall(body, name="bwd_mlp_down", grid=(T // tm,),
                  in_specs=[_rows(tm, D), _rows(tm, D), _full((1, D)), _full((D_FF, D)), _rows(tm, D_FF)],
                  out_specs=[_rows(tm, D), _rows(tm, D_FF), _full((1, D))],
                  out_shape=[SDS((T, D), BF16), SDS((T, D_FF), BF16), SDS((1, D), F32)],
                  sem=("arbitrary",))(dx2, d, g4, w2, u)


def _bwd_mlp_up(da, w1, x1, g3, dx2, tm):
    T = dx2.shape[0]

    def body(da_ref, w_ref, x_ref, g_ref, dx2_ref, dx1_ref, dg_ref):
        @pl.when(pl.program_id(0) == 0)
        def _():
            dg_ref[...] = jnp.zeros_like(dg_ref)

        dh = _dot_nt(da_ref[...], w_ref[...])
        dx, dg = _rms_bwd(dh, x_ref[...], g_ref[...])
        dg_ref[...] += dg
        dx1_ref[...] = dx2_ref[...] + dx

    return _pcall(body, name="bwd_mlp_up", grid=(T // tm,),
                  in_specs=[_rows(tm, D_FF), _full((D, D_FF)), _rows(tm, D), _full((1, D)), _rows(tm, D)],
                  out_specs=[_rows(tm, D), _full((1, D))],
                  out_shape=[SDS((T, D), F32), SDS((1, D), F32)], sem=("arbitrary",))(da, w1, x1, g3, dx2)


def _bwd_mix(dx1, yo, g2, wo, z, au, bu, wmu, wsu, om, tm):
    T = dx1.shape[0]

    def body(dx_ref, yo_ref, g_ref, wo_ref, ga_ref, gb_ref, au_ref, bu_ref, wmu_ref, wsu_ref, om_ref,
             dyo_ref, dg_ref, dau_ref, dbu_ref, dga_ref, dgb_ref, dos_ref, dl_ref, dot_ref):
        @pl.when(pl.program_id(0) == 0)
        def _():
            dg_ref[...] = jnp.zeros_like(dg_ref)

        dyo, dg = _rms_bwd(dx_ref[...], yo_ref[...], g_ref[...])
        dg_ref[...] += dg
        dyob = dyo.astype(BF16)
        dyo_ref[...] = dyob
        dy = _dot_nt(dyob, wo_ref[...])
        sa = _sigmoid(ga_ref[...])
        sb = _sigmoid(gb_ref[...])
        dau = (dy * sa).astype(BF16)
        dbu = (dy * sb).astype(BF16)
        dau_ref[...] = dau
        dbu_ref[...] = dbu
        dga_ref[...] = (dy * au_ref[...] * sa * (1.0 - sa)).astype(BF16)
        dgb_ref[...] = (dy * bu_ref[...] * sb * (1.0 - sb)).astype(BF16)
        dom = _dot_nt(dau, wmu_ref[...])
        dos_ref[...] = _dot_nt(dbu, wsu_ref[...])
        prod = dom * om_ref[...]
        sub = lax.broadcasted_iota(jnp.int32, (8, tm), 0)
        for pr in range(MLA_HEADS // 2):
            sl = slice(LANES * pr, LANES * (pr + 1))
            pt = prod[:, sl].T
            d0 = jnp.sum(pt[0:64], axis=0, keepdims=True)
            d1 = jnp.sum(pt[64:128], axis=0, keepdims=True)
            dl_ref[pr, 0] = jnp.where(sub == 0, d0, jnp.where(sub == 1, d1, 0.0))
            dot_ref[0, sl, :] = dom[:, sl].T.astype(BF16)

    r = _rows(tm, D)
    w = _full((D, D))
    return _pcall(body, name="bwd_mix", grid=(T // tm,),
                  in_specs=[r, r, _full((1, D)), w, _rows(tm, D, 1), _rows(tm, D, 2), r, r, w, w, r],
                  out_specs=[r, _full((1, D)), r, r, r, r, r, pl.BlockSpec((MLA_HEADS // 2, 1, 8, tm), lambda i: (0, i, 0, 0)),
                             pl.BlockSpec((1, D, tm), lambda i: (i, 0, 0))],
                  out_shape=[SDS((T, D), BF16), SDS((1, D), F32), SDS((T, D), BF16), SDS((T, D), BF16), SDS((T, D), BF16),
                             SDS((T, D), BF16), SDS((T, D), F32), SDS((MLA_HEADS // 2, T // tm, 8, tm), F32),
                             SDS((T // tm, D, tm), BF16)],
                  sem=("arbitrary",))(dx1, yo, g2, wo, z, z, au, bu, wmu, wsu, om)


def _mla_bwd(qm, qt, km, kt, vm, dot, lse, delta, tb):
    T = qm.shape[0]
    nb = T // tb
    cc = ATT_COLS

    def body(q_ref, qt_ref, k_ref, kt_ref, v_ref, dot_ref, l_ref, dl_ref, dqt_ref, dkt_ref, dvt_ref,
             s_ref, dp_ref, p_ref, ds_ref, vh_ref):
        j = pl.program_id(1)

        @pl.when(j == 0)
        def _():
            dqt_ref[...] = jnp.zeros_like(dqt_ref)

        dkt_ref[...] = jnp.zeros_like(dkt_ref)
        dvt_ref[...] = jnp.zeros_like(dvt_ref)
        lo = lax.broadcasted_iota(jnp.int32, (tb, LANES), 1) < 64
        key = lax.broadcasted_iota(jnp.int32, (tb, cc), 0)
        qry = lax.broadcasted_iota(jnp.int32, (tb, cc), 1)
        v = v_ref[...]
        vh_ref[0] = jnp.where(lo, v, jnp.zeros_like(v))
        vh_ref[1] = jnp.where(lo, jnp.zeros_like(v), v)

        def scores(i, slot):
            rows_i = pl.ds(pl.multiple_of(i * tb, tb), tb)
            for hh in range(2):
                sl = slice(LANES * hh, LANES * (hh + 1))
                s_ref[slot, hh] = _dot_nt(k_ref[:, sl], q_ref[rows_i, sl])
                dp_ref[slot, hh] = _dot(vh_ref[hh], dot_ref[i])

        def grads(i, slot, diagonal):
            lse_i = l_ref[0, i]
            delta_i = dl_ref[0, i]
            for hh in range(2):
                for c in range(tb // cc):
                    cols = slice(cc * c, cc * (c + 1))
                    p = jnp.exp2(s_ref[slot, hh, :, cols] * MLA_LOG2_SCALE - lse_i[hh:hh + 1, cols])
                    if diagonal:
                        p = jnp.where(key <= qry + cc * c, p, 0.0)
                    p_ref[hh, :, cols] = p.astype(BF16)
                    ds_ref[hh, :, cols] = (p * (dp_ref[slot, hh, :, cols] - delta_i[hh:hh + 1, cols]) * MLA_SCALE).astype(BF16)
            for hh in range(2):
                sl = slice(LANES * hh, LANES * (hh + 1))
                half = slice(64 * hh, 64 * (hh + 1))
                dvt_ref[0, half, :] += _dot_nt(dot_ref[i, half, :], p_ref[hh])
                dkt_ref[0, sl, :] += _dot_nt(qt_ref[i, sl, :], ds_ref[hh])
                dqt_ref[i, sl, :] += _dot(kt_ref[0, sl, :], ds_ref[hh])

        n_off = nb - 1 - j

        def step(u, carry):
            i0 = j + 1 + 2 * u
            scores(i0 + 1, 1)
            grads(i0, 0, False)
            scores(jnp.where(i0 + 2 < nb, i0 + 2, j), 0)
            grads(i0 + 1, 1, False)
            return carry

        scores(jnp.where(n_off > 0, j + 1, j), 0)
        lax.fori_loop(0, n_off // 2, step, 0)

        @pl.when(n_off % 2 == 1)
        def _():
            scores(j, 1)
            grads(nb - 1, 0, False)
            grads(j, 1, True)

        @pl.when(n_off % 2 == 0)
        def _():
            grads(j, 0, True)

    blk = lambda w: pl.BlockSpec((tb, w), lambda p, j: (j, p))
    stat = pl.BlockSpec((1, nb, 8, tb), lambda p, j: (p, 0, 0, 0))
    pair_t = lambda w: pl.BlockSpec((nb, w, tb), lambda p, j: (0, p, 0))
    blk_t = lambda w: pl.BlockSpec((1, w, tb), lambda p, j: (j, p, 0))
    return _pcall(body, name="mla_bwd", grid=(MLA_HEADS // 2, nb),
                  in_specs=[pl.BlockSpec((T, 256), lambda p, j: (0, p)), pair_t(256), blk(256), blk_t(256), blk(LANES),
                            pair_t(LANES), stat, stat],
                  out_specs=[pair_t(256), blk_t(256), blk_t(LANES)],
                  out_shape=[SDS((nb, 2048, tb), F32), SDS((nb, 2048, tb), F32), SDS((nb, D, tb), F32)],
                  scratch=[pltpu.VMEM((2, 2, tb, tb), F32), pltpu.VMEM((2, 2, tb, tb), F32), pltpu.VMEM((2, tb, tb), BF16),
                           pltpu.VMEM((2, tb, tb), BF16), pltpu.VMEM((2, tb, LANES), BF16)],
                  sem=("parallel", "arbitrary"))(qm, qt, km, kt, vm, dot, lse, delta)


def _swa_bwd(sinks, qs, ks, vs, do, o, lse):
    T = qs.shape[0]
    nb, cur, prev = _swa_specs(T)

    def body(sink_ref, q_ref, kc_ref, kp_ref, vc_ref, vp_ref, do_ref, o_ref, l_ref,
             dq_ref, dkc_ref, dkp_ref, dvc_ref, dvp_ref, dsink_ref, kb_ref, vb_ref, s_ref, dp_ref, p_ref, ds_ref):
        n = pl.program_id(0)

        @pl.when(n == 0)
        def _():
            dsink_ref[...] = jnp.zeros_like(dsink_ref)

        mask = _swa_mask(n)
        lo = lax.broadcasted_iota(jnp.int32, (WINDOW, LANES), 1) < 64
        hi = jnp.logical_not(lo)
        lane8 = lax.broadcasted_iota(jnp.int32, (8, LANES), 1)
        for g in range(2):
            gs = slice(LANES * g, LANES * (g + 1))
            kb_ref[g] = jnp.concatenate([kp_ref[:, gs], kc_ref[:, gs]], axis=0)
            vb_ref[g] = jnp.concatenate([vp_ref[:, gs], vc_ref[:, gs]], axis=0)

        def head(h):
            sl = slice(LANES * (h // 2), LANES * (h // 2 + 1))
            hm = lo if h % 2 == 0 else hi
            qp = q_ref[:, sl]
            return hm, sl, jnp.where(hm, qp, jnp.zeros_like(qp)), jnp.where(hm, do_ref[:, sl], 0.0).astype(BF16)

        for h in range(SWA_HEADS):
            _, _, qh, dom = head(h)
            s_ref[h] = _dot_nt(qh, kb_ref[h // 8])
            dp_ref[h] = _dot_nt(dom, vb_ref[h // 8])
        dsink = jnp.zeros((8, LANES), F32)
        for h in range(SWA_HEADS):
            hm, sl, _, _ = head(h)
            lse_h = jnp.max(jnp.where(hm, l_ref[:, sl], -jnp.inf), axis=1, keepdims=True)
            delta = jnp.sum(jnp.where(hm, do_ref[:, sl] * o_ref[:, sl], 0.0), axis=1, keepdims=True)
            p = jnp.exp(jnp.where(mask, s_ref[h] * SWA_SCALE, NEG) - lse_h)
            p_ref[h] = p.astype(BF16)
            ds_ref[h] = (p * (dp_ref[h] - delta) * SWA_SCALE).astype(BF16)
            d_sink = -jnp.sum(jnp.exp(sink_ref[h] - lse_h) * delta, axis=0, keepdims=True)
            dsink = dsink + jnp.where(lane8 == h, d_sink, 0.0)
        dsink_ref[...] += dsink
        for g in range(2):
            gs = slice(LANES * g, LANES * (g + 1))
            dkb = jnp.zeros((2 * WINDOW, LANES), F32)
            dvb = jnp.zeros((2 * WINDOW, LANES), F32)
            for j in range(4 * g, 4 * g + 4):
                dqs = []
                for h in (2 * j, 2 * j + 1):
                    _, _, qh, dom = head(h)
                    dvb = dvb + _dot_tn(p_ref[h], dom)
                    dkb = dkb + _dot_tn(ds_ref[h], qh)
                    dqs.append(_dot(ds_ref[h], kb_ref[g]))
                dq_ref[:, LANES * j:LANES * (j + 1)] = jnp.where(lo, dqs[0], dqs[1])
            dkp_ref[:, gs] = dkb[:WINDOW]
            dkc_ref[:, gs] = dkb[WINDOW:]
            dvp_ref[:, gs] = dvb[:WINDOW]
            dvc_ref[:, gs] = dvb[WINDOW:]

    band = pltpu.VMEM((2, 2 * WINDOW, LANES), BF16)
    return _pcall(body, name="swa_bwd", grid=(nb,),
                  in_specs=[pl.BlockSpec(memory_space=pltpu.SMEM), cur(D), cur(256), prev(256), cur(256), prev(256),
                            cur(D), cur(D), cur(D)],
                  out_specs=[cur(D), cur(256), cur(256), cur(256), cur(256), _full((8, LANES))],
                  out_shape=[SDS((T, D), F32), SDS((T, 256), F32), SDS((T, 256), F32), SDS((T, 256), F32), SDS((T, 256), F32),
                             SDS((8, LANES), F32)],
                  scratch=[band, band, pltpu.VMEM((SWA_HEADS, WINDOW, 2 * WINDOW), F32),
                           pltpu.VMEM((SWA_HEADS, WINDOW, 2 * WINDOW), F32), pltpu.VMEM((SWA_HEADS, WINDOW, 2 * WINDOW), BF16),
                           pltpu.VMEM((SWA_HEADS, WINDOW, 2 * WINDOW), BF16)],
                  sem=("arbitrary",))(sinks, qs, ks, ks, vs, vs, do, o, lse)


def _bwd_qkv(dqm, dkm, dvm, dqs, dkc, dkp, dvc, dvp, z, gq, gkv, wqb, wkn, wv, tab_m, tab_s):
    T = z.shape[0]
    tm = WINDOW
    nb = T // tm
    per = dqm.shape[2] // tm

    def body(dqm_ref, dkm_ref, dvm_ref, dqs_ref, dkc_ref, dkp_ref, dvc_ref, dvp_ref, qa_ref, kva_ref, gq_ref, gkv_ref,
             wqb_ref, wkn_ref, wv_ref, cm_ref, am_ref, bm_ref, cs_ref, as_ref, bs_ref,
             dq_out, dkn_out, dv_out, dsq_ref, drest_ref, dgq_ref, dgkv_ref):
        i = pl.program_id(0)

        @pl.when(i == 0)
        def _():
            dgq_ref[...] = jnp.zeros_like(dgq_ref)
            dgkv_ref[...] = jnp.zeros_like(dgkv_ref)

        cm, am, bm = cm_ref[...], -am_ref[...], -bm_ref[...]
        cs, as_, bs = cs_ref[...], -as_ref[...], -bs_ref[...]
        lane = lax.broadcasted_iota(jnp.int32, (tm, LANES), 1)
        nope = lane < MLA_NOPE
        roped = jnp.logical_and(lane >= MLA_NOPE, lane < MLA_NOPE + MLA_ROPE)
        dkr = jnp.zeros((tm, LANES), F32)
        for h in range(MLA_HEADS):
            sl = slice(LANES * h, LANES * (h + 1))
            dq_out[:, sl] = _rope(dqm_ref[0, sl, :].T, cm, am, bm, MLA_ROPE // 2).astype(BF16)
            dk_h = dkm_ref[0, sl, :].T
            dkn_out[:, sl] = jnp.where(nope, dk_h, 0.0).astype(BF16)
            dkr = dkr + jnp.where(roped, dk_h, 0.0)
        for j in range(D // LANES):
            sl = slice(LANES * j, LANES * (j + 1))
            dv_out[:, sl] = dvm_ref[0, sl, :].T.astype(BF16)
        dqn = _dot_nt(dq_out[...], wqb_ref[...])
        dkvn = _dot_nt(dkn_out[...], wkn_ref[...]) + _dot_nt(dv_out[...], wv_ref[...])
        dqa, dgq = _rms_bwd(dqn, qa_ref[...], gq_ref[...])
        dkva, dgkv = _rms_bwd(dkvn, kva_ref[...], gkv_ref[...])
        dgq_ref[...] += dgq
        dgkv_ref[...] += dgkv
        for j in range(D // LANES):
            sl = slice(LANES * j, LANES * (j + 1))
            dsq_ref[:, sl] = _rope(dqs_ref[:, sl], cs, as_, bs, SWA_HD // 2).astype(BF16)
        keep = (i < nb - 1).astype(F32)
        drest_ref[:, 0:256] = dqa.astype(BF16)
        for j in range(2):
            sl = slice(LANES * j, LANES * (j + 1))
            dk = dkc_ref[:, sl] + keep * dkp_ref[:, sl]
            drest_ref[:, 256 + LANES * j:256 + LANES * (j + 1)] = _rope(dk, cs, as_, bs, SWA_HD // 2).astype(BF16)
        drest_ref[:, 512:768] = (dvc_ref[...] + keep * dvp_ref[...]).astype(BF16)
        drest_ref[:, 768:896] = dkva.astype(BF16)
        drest_ref[:, 896:1024] = _rope(dkr, cm, am, bm, MLA_ROPE // 2).astype(BF16)

    nxt = pl.BlockSpec((tm, 256), lambda i: (jnp.minimum(i + 1, nb - 1), 0))
    tab = [_rows(tm, LANES)] * 6
    return _pcall(body, name="bwd_qkv", grid=(nb,),
                  in_specs=[pl.BlockSpec((1, 2048, tm), lambda i: (i // per, 0, i % per)),
                            pl.BlockSpec((1, 2048, tm), lambda i: (i // per, 0, i % per)),
                            pl.BlockSpec((1, 1024, tm), lambda i: (i // per, 0, i % per)), _rows(tm, 1024), _rows(tm, 256), nxt,
                            _rows(tm, 256), nxt, _rows(tm, 256, 12), _rows(tm, 128, 30), _full((1, Q_LORA)), _full((1, KV_LORA)),
                            _full((Q_LORA, 2048)), _full((KV_LORA, 2048)), _full((KV_LORA, 1024))] + tab,
                  out_specs=[_rows(tm, 2048), _rows(tm, 2048), _rows(tm, 1024), _rows(tm, 1024), _rows(tm, 1024),
                             _full((1, Q_LORA)), _full((1, KV_LORA))],
                  out_shape=[SDS((T, 2048), BF16), SDS((T, 2048), BF16), SDS((T, 1024), BF16), SDS((T, 1024), BF16),
                             SDS((T, 1024), BF16), SDS((1, Q_LORA), F32), SDS((1, KV_LORA), F32)],
                  sem=("arbitrary",))(dqm, dkm, dvm, dqs, dkc, dkp, dvc, dvp, z, z, gq, gkv, wqb, wkn, wv, *tab_m, *tab_s)


def _bwd_in(dsq, dga, dgb, drest, w_in_p, x, g1, dx1, tm):
    T = x.shape[0]

    def body(a_ref, b_ref, c_ref, d_ref, w_ref, x_ref, g_ref, dx1_ref, dx_ref, dg_ref):
        @pl.when(pl.program_id(0) == 0)
        def _():
            dg_ref[...] = jnp.zeros_like(dg_ref)

        dh = (_dot_nt(a_ref[...], w_ref[:, 0:1024]) + _dot_nt(b_ref[...], w_ref[:, 1024:2048])
              + _dot_nt(c_ref[...], w_ref[:, 2048:3072]) + _dot_nt(d_ref[...], w_ref[:, 3072:4096]))
        dx, dg = _rms_bwd(dh, x_ref[...], g_ref[...])
        dg_ref[...] += dg
        dx_ref[...] = dx1_ref[...] + dx

    r = _rows(tm, D)
    return _pcall(body, name="bwd_in", grid=(T // tm,),
                  in_specs=[r, r, r, r, _full((D, NZ)), r, _full((1, D)), r],
                  out_specs=[r, _full((1, D))], out_shape=[SDS((T, D), F32), SDS((1, D), F32)],
                  sem=("arbitrary",))(dsq, dga, dgb, drest, w_in_p, x, g1, dx1)


def _wgrad(a, g, name, into=None):
    T, K = a.shape
    N = g.shape[1]
    tk, tn, tt = min(K, 1024), min(N, 1024), min(T, 1024)
    if into is not None:
        buf, weight = into
        _, row0, lane0 = PACK_AT[weight]
        shard = {n: (r, c) for n, r, c in BIG}[weight]
        assert lane0 == 0 and shard[1] == D and tk % shard[0] == 0
        per_step = tk // shard[0]
    assert K % tk == 0 and N % tn == 0 and T % tt == 0, (a.shape, g.shape)
    steps = T // tt

    def body(a_ref, g_ref, *rest):
        o_ref, acc_ref = rest[-2:]
        t = pl.program_id(2)

        @pl.when(t == 0)
        def _():
            acc_ref[...] = jnp.zeros_like(acc_ref)

        acc_ref[...] += _dot_tn(a_ref[...].astype(BF16), g_ref[...].astype(BF16))

        @pl.when(t == steps - 1)
        def _():
            o_ref[...] = acc_ref[...].astype(o_ref.dtype).reshape(o_ref.shape)

    in_specs = [pl.BlockSpec((tt, tk), lambda k, n, t: (t, k)), pl.BlockSpec((tt, tn), lambda k, n, t: (t, n))]
    if into is None:
        return _pcall(body, name=name, grid=(K // tk, N // tn, steps), in_specs=in_specs,
                      out_specs=pl.BlockSpec((tk, tn), lambda k, n, t: (k, n)), out_shape=SDS((K, N), F32),
                      scratch=[pltpu.VMEM((tk, tn), F32)], sem=("parallel", "parallel", "arbitrary"))(a, g)
    assert row0 % shard[0] == 0 and (K // tk) * (N // tn) * per_step == N_CHIPS
    return _pcall(body, name=name, grid=(K // tk, N // tn, steps), in_specs=in_specs + [ANY],
                  out_specs=pl.BlockSpec((per_step, shard[0], tn), lambda k, n, t: (k + n, row0 // shard[0], 0)),
                  out_shape=SDS(buf.shape, buf.dtype),
                  scratch=[pltpu.VMEM((tk, tn), F32)], sem=("parallel", "parallel", "arbitrary"), aliases={2: 0})(a, g, buf)


def _adamw(w, packed_g, m, v, name):
    _, R, C = w.shape
    _, row0, lane0 = PACK_AT[name]
    tr = min(R, 256 if row0 % 256 == 0 else 128)
    assert row0 % tr == 0 and R % tr == 0

    def body(w_ref, g_ref, m_ref, v_ref, go_ref, d_ref, m2_ref, v2_ref):
        g_ = g_ref[:, lane0:lane0 + C]
        go_ref[0] = g_
        m2 = ADAM_B1 * m_ref[0] + (1.0 - ADAM_B1) * g_
        v2 = ADAM_B2 * v_ref[0] + (1.0 - ADAM_B2) * jnp.square(g_)
        m_hat = m2 / (1.0 - ADAM_B1 ** ADAM_STEP)
        v_hat = v2 / (1.0 - ADAM_B2 ** ADAM_STEP)
        d_ref[0] = -ADAM_LR * (m_hat / (jnp.sqrt(v_hat) + ADAM_EPS) + ADAM_WD * w_ref[0])
        m2_ref[0] = m2
        v2_ref[0] = v2

    r = pl.BlockSpec((1, tr, C), lambda i: (0, i, 0))
    return _pcall(body, name="adamw_" + name, grid=(R // tr,),
                  in_specs=[r, pl.BlockSpec((tr, D), lambda i: (row0 // tr + i, 0)), r, r], out_specs=[r] * 4,
                  out_shape=[SDS((1, R, C), F32)] * 4, sem=("parallel",))(w, packed_g, m, v)


def _adamw_small(w, parts, m, v):
    def body(w_ref, p_ref, m_ref, v_ref, g_ref, d_ref, m2_ref, v2_ref):
        g_ = p_ref[0]
        for k in range(1, N_DEV):
            g_ = g_ + p_ref[k]
        g_ref[...] = g_
        m2 = ADAM_B1 * m_ref[...] + (1.0 - ADAM_B1) * g_
        v2 = ADAM_B2 * v_ref[...] + (1.0 - ADAM_B2) * jnp.square(g_)
        m_hat = m2 / (1.0 - ADAM_B1 ** ADAM_STEP)
        v_hat = v2 / (1.0 - ADAM_B2 ** ADAM_STEP)
        d_ref[...] = -ADAM_LR * (m_hat / (jnp.sqrt(v_hat) + ADAM_EPS) + ADAM_WD * w_ref[...])
        m2_ref[...] = m2
        v2_ref[...] = v2

    s = _full((8, D))
    return _pcall(body, name="adamw_small", grid=(1,), in_specs=[s, _full((N_DEV, 8, D)), s, s], out_specs=[s] * 4,
                  out_shape=[SDS((8, D), F32)] * 4, sem=("arbitrary",))(w, parts, m, v)


ANY = pl.BlockSpec(memory_space=pl.ANY)


def _place():
    x, y, c = lax.axis_index("x"), lax.axis_index("y"), lax.axis_index("c")
    chips = [(1 - x, y), (x, 1 - y), (1 - x, 1 - y)]
    return x, y, c, chips


def _all_gather(wpk):
    rows = wpk.shape[0]
    HALF = rows // 2
    assert HALF % 16 == 0

    def body(in_ref, out_ref, send_sems, recv_sems):
        x, y, c, chips = _place()
        half = pl.ds(pl.multiple_of(c * HALF, 16), HALF)
        other = pl.ds(pl.multiple_of((1 - c) * HALF, 16), HALF)

        def copy(k, src, dst, to):
            return pltpu.make_async_remote_copy(src_ref=src, dst_ref=dst, send_sem=send_sems.at[k], recv_sem=recv_sems.at[k],
                                                device_id=to, device_id_type=MESH)

        first = [copy(k, in_ref.at[half], out_ref.at[2 * x + y, half], (cx, cy, c)) for k, (cx, cy) in enumerate(chips)]
        for cp in first:
            cp.start()
        passed = []
        for k, (cx, cy) in enumerate(chips):
            slot = out_ref.at[2 * cx + cy, half]
            copy(k, slot, slot, (x, y, c)).wait_recv()
            fwd = copy(3 + k, slot, slot, (x, y, 1 - c))
            fwd.start()
            passed.append(fwd)
        for k, (cx, cy) in enumerate(chips):
            slot = out_ref.at[2 * cx + cy, other]
            copy(3 + k, slot, slot, (x, y, c)).wait_recv()
        for cp in first + passed:
            cp.wait_send()

    return _pcall(body, name="all_gather_weights", in_specs=[ANY], out_specs=ANY,
                  out_shape=SDS((N_CHIPS, rows, D), BF16),
                  scratch=[pltpu.SemaphoreType.DMA((6,)), pltpu.SemaphoreType.DMA((6,))])(wpk)


HBM = pl.BlockSpec(memory_space=pltpu.HBM)
SEM = pl.BlockSpec(memory_space=pltpu.SEMAPHORE)
DATAFLOW = pltpu.SideEffectType.DATAFLOW_SIDE_EFFECTING


def _in_hbm(a):
    return pltpu.with_memory_space_constraint(a, pltpu.HBM)


def _gather_late_start(wpk, after):
    rows = wpk.shape[0]

    def body(in_ref, land_ref, after_ref, send_sems, recv_sems, in_thru, land_thru, token):
        x, y, c, chips = _place()
        for k, (cx, cy) in enumerate(chips):
            pltpu.make_async_remote_copy(src_ref=in_ref, dst_ref=land_ref.at[2 * x + y], send_sem=send_sems.at[k],
                                         recv_sem=recv_sems.at[k], device_id=(cx, cy, c), device_id_type=MESH).start()
        token[...] = jnp.zeros_like(token)

    return pl.pallas_call(
        body, name="gather_late_start",
        out_shape=(pltpu.SemaphoreType.DMA((3,)), pltpu.SemaphoreType.DMA((3,)), pltpu.HBM(wpk.shape, wpk.dtype),
                   pltpu.HBM((N_CHIPS, rows, D), wpk.dtype), SDS((8, LANES), F32)),
        in_specs=(HBM, HBM, ANY), out_specs=(SEM, SEM, HBM, HBM, pl.BlockSpec(memory_space=pltpu.VMEM)),
        input_output_aliases={0: 2, 1: 3}, compiler_params=pltpu.CompilerParams(has_side_effects=DATAFLOW),
    )(_in_hbm(wpk), _in_hbm(lax.empty((N_CHIPS, rows, D), wpk.dtype)), after)


def _gather_late_wait(send_sems, recv_sems, in_thru, land_thru, after):
    def body(in_ref, land_ref, send_sems, recv_sems, after_ref, after2_ref, in_dead, got_ref):
        x, y, c, chips = _place()
        for k, (cx, cy) in enumerate(chips):
            cp = pltpu.make_async_remote_copy(src_ref=in_ref, dst_ref=land_ref.at[2 * cx + cy], send_sem=send_sems.at[k],
                                              recv_sem=recv_sems.at[k], device_id=(cx, cy, c), device_id_type=MESH)
            cp.wait_send()
            cp.wait_recv()

    return pl.pallas_call(
        body, name="gather_late_wait",
        out_shape=(pltpu.HBM(in_thru.shape, in_thru.dtype), pltpu.HBM(land_thru.shape, land_thru.dtype)),
        in_specs=(HBM, HBM, SEM, SEM, ANY, ANY), out_specs=(HBM, HBM), input_output_aliases={0: 0, 1: 1},
        compiler_params=pltpu.CompilerParams(has_side_effects=DATAFLOW),
    )(in_thru, land_thru, send_sems, recv_sems, *after)[1]


def _rs_sibling(gpk):
    HALF = gpk.shape[1] // 2

    def body(in_ref, out_ref, send_sem, recv_sem):
        x, y, c, _ = _place()
        theirs = pl.ds(pl.multiple_of((1 - c) * HALF, 8), HALF)
        cp = pltpu.make_async_remote_copy(src_ref=in_ref.at[:, theirs], dst_ref=out_ref, send_sem=send_sem, recv_sem=recv_sem,
                                          device_id=(x, y, 1 - c), device_id_type=MESH)
        cp.start()
        cp.wait()

    return _pcall(body, name="rs_sibling", in_specs=[ANY], out_specs=ANY, out_shape=SDS((N_CHIPS, HALF, D), F32),
                  scratch=[pltpu.SemaphoreType.DMA, pltpu.SemaphoreType.DMA])(gpk)


def _rs_add_sibling(cidx, gpk, got):
    HALF = got.shape[1]
    th = HALF // 4
    nh = HALF // th
    assert th % 16 == 0

    def body(c_ref, a_ref, b_ref, o_ref):
        o_ref[...] = (a_ref[...] + b_ref[...]).astype(BF16)

    gs = pltpu.PrefetchScalarGridSpec(
        num_scalar_prefetch=1, grid=(N_CHIPS, nh),
        in_specs=[pl.BlockSpec((1, th, D), lambda j, i, c: (j, c[0] * nh + i, 0)), pl.BlockSpec((1, th, D), lambda j, i, c: (j, i, 0))],
        out_specs=pl.BlockSpec((1, th, D), lambda j, i, c: (j, i, 0)))
    return pl.pallas_call(body, name="rs_add_sibling", grid_spec=gs, out_shape=SDS((N_CHIPS, HALF, D), BF16),
                          compiler_params=pltpu.CompilerParams(dimension_semantics=("parallel", "parallel"),
                                                               vmem_limit_bytes=48 << 20))(cidx, gpk, got)


def _rs_chips(part, small):
    def body(p_ref, s_ref, o_ref, so_ref, send_sems, recv_sems, ssend_sems, srecv_sems, local_sem):
        x, y, c, chips = _place()
        me = 2 * x + y
        mine_s = pltpu.make_async_copy(s_ref, so_ref.at[4 * x + 2 * y + c], local_sem)
        mine_s.start()
        sends = []
        for k, (cx, cy) in enumerate(chips):
            sends.append(pltpu.make_async_remote_copy(src_ref=p_ref.at[2 * cx + cy], dst_ref=o_ref.at[me], send_sem=send_sems.at[k],
                                                      recv_sem=recv_sems.at[k], device_id=(cx, cy, c), device_id_type=MESH))
        peers = [(x, y, 1 - c)] + [(cx, cy, c) for cx, cy in chips] + [(cx, cy, 1 - c) for cx, cy in chips]
        for k, to in enumerate(peers):
            sends.append(pltpu.make_async_remote_copy(src_ref=s_ref, dst_ref=so_ref.at[4 * x + 2 * y + c], send_sem=ssend_sems.at[k],
                                                      recv_sem=srecv_sems.at[k], device_id=to, device_id_type=MESH))
        for cp in sends:
            cp.start()
        for k, (cx, cy) in enumerate(chips):
            slot = o_ref.at[2 * cx + cy]
            pltpu.make_async_remote_copy(src_ref=slot, dst_ref=slot, send_sem=send_sems.at[k], recv_sem=recv_sems.at[k],
                                         device_id=(x, y, c), device_id_type=MESH).wait_recv()
        for k, (px, py, pc) in enumerate(peers):
            slot = so_ref.at[4 * px + 2 * py + pc]
            pltpu.make_async_remote_copy(src_ref=slot, dst_ref=slot, send_sem=ssend_sems.at[k], recv_sem=srecv_sems.at[k],
                                         device_id=(x, y, c), device_id_type=MESH).wait_recv()
        for cp in sends:
            cp.wait_send()
        mine_s.wait()

    return _pcall(body, name="rs_chips", in_specs=[ANY, ANY], out_specs=[ANY, ANY],
                  out_shape=[SDS(part.shape, part.dtype), SDS((N_DEV, 8, D), F32)],
                  scratch=[pltpu.SemaphoreType.DMA((3,)), pltpu.SemaphoreType.DMA((3,)), pltpu.SemaphoreType.DMA((7,)),
                           pltpu.SemaphoreType.DMA((7,)), pltpu.SemaphoreType.DMA])(part, small)


def _rs_add_chips(qidx, part, parts):
    HALF = part.shape[1]
    th = HALF // 4
    assert th % 16 == 0

    def body(q_ref, own_ref, p_ref, o_ref):
        for me in range(N_CHIPS):
            @pl.when(q_ref[0] == me)
            def _(me=me):
                t = [(own_ref[0] if j == me else p_ref[j]).astype(F32) for j in range(N_CHIPS)]
                o_ref[...] = ((t[0] + t[1]) + t[2]) + t[3]

    gs = pltpu.PrefetchScalarGridSpec(
        num_scalar_prefetch=1, grid=(HALF // th,),
        in_specs=[pl.BlockSpec((1, th, D), lambda i, q: (q[0], i, 0)), pl.BlockSpec((N_CHIPS, th, D), lambda i, q: (0, i, 0))],
        out_specs=pl.BlockSpec((th, D), lambda i, q: (i, 0)))
    return pl.pallas_call(body, name="rs_add_chips", grid_spec=gs, out_shape=SDS((HALF, D), F32),
                          compiler_params=pltpu.CompilerParams(dimension_semantics=("parallel",),
                                                               vmem_limit_bytes=48 << 20))(qidx, part, parts)


def _rs_join(early, late):
    def body(e_ref, l_ref, eo_ref, lo_ref, send_sems, recv_sems):
        x, y, c, _ = _place()
        cps = [pltpu.make_async_remote_copy(src_ref=src, dst_ref=dst, send_sem=send_sems.at[k], recv_sem=recv_sems.at[k],
                                            device_id=(x, y, 1 - c), device_id_type=MESH)
               for k, (src, dst) in enumerate([(e_ref, eo_ref), (l_ref, lo_ref)])]
        for cp in cps:
            cp.start()
        for cp in cps:
            cp.wait()

    return _pcall(body, name="rs_join", in_specs=[ANY, ANY], out_specs=[ANY, ANY],
                  out_shape=[SDS(early.shape, F32), SDS(late.shape, F32)],
                  scratch=[pltpu.SemaphoreType.DMA((2,)), pltpu.SemaphoreType.DMA((2,))])(early, late)


def _reduce_late_start(gpk, after):
    rows = gpk.shape[1]
    HALF = rows // 2
    assert HALF % 16 == 0

    def body(in_ref, land_ref, after_ref, send_sems, recv_sems, in_thru, land_thru, token):
        x, y, c, chips = _place()
        me = 4 * x + 2 * y + c
        peers = [(x, y, 1 - c)] + [(cx, cy, c) for cx, cy in chips] + [(cx, cy, 1 - c) for cx, cy in chips]
        for k, (px, py, pc) in enumerate(peers):
            src = in_ref.at[2 * px + py, pl.ds(pl.multiple_of(pc * HALF, 16), HALF)]
            pltpu.make_async_remote_copy(src_ref=src, dst_ref=land_ref.at[me], send_sem=send_sems.at[k], recv_sem=recv_sems.at[k],
                                         device_id=(px, py, pc), device_id_type=MESH).start()
        token[...] = jnp.zeros_like(token)

    return pl.pallas_call(
        body, name="reduce_late_start",
        out_shape=(pltpu.SemaphoreType.DMA((7,)), pltpu.SemaphoreType.DMA((7,)), pltpu.HBM(gpk.shape, gpk.dtype),
                   pltpu.HBM((N_DEV, HALF, D), gpk.dtype), SDS((8, LANES), F32)),
        in_specs=(HBM, HBM, ANY), out_specs=(SEM, SEM, HBM, HBM, pl.BlockSpec(memory_space=pltpu.VMEM)),
        input_output_aliases={0: 2, 1: 3}, compiler_params=pltpu.CompilerParams(has_side_effects=DATAFLOW),
    )(_in_hbm(gpk), _in_hbm(lax.empty((N_DEV, HALF, D), gpk.dtype)), after)


def _reduce_late_wait(send_sems, recv_sems, in_thru, land_thru, after):
    def body(in_ref, land_ref, send_sems, recv_sems, after_ref, in_out, got_ref):
        x, y, c, chips = _place()
        peers = [(x, y, 1 - c)] + [(cx, cy, c) for cx, cy in chips] + [(cx, cy, 1 - c) for cx, cy in chips]
        for k, (px, py, pc) in enumerate(peers):
            cp = pltpu.make_async_remote_copy(src_ref=land_ref.at[0], dst_ref=land_ref.at[4 * px + 2 * py + pc],
                                              send_sem=send_sems.at[k], recv_sem=recv_sems.at[k],
                                              device_id=(px, py, pc), device_id_type=MESH)
            cp.wait_send()
            cp.wait_recv()

    return pl.pallas_call(
        body, name="reduce_late_wait",
        out_shape=(pltpu.HBM(in_thru.shape, in_thru.dtype), pltpu.HBM(land_thru.shape, land_thru.dtype)),
        in_specs=(HBM, HBM, SEM, SEM, ANY), out_specs=(HBM, HBM), input_output_aliases={0: 0, 1: 1},
        compiler_params=pltpu.CompilerParams(has_side_effects=DATAFLOW),
    )(in_thru, land_thru, send_sems, recv_sems, after)


def _reduce_late_add(didx, gpk, parts):
    HALF = parts.shape[1]
    th = HALF // 4
    nh = HALF // th
    assert th % 16 == 0

    def body(d_ref, own_ref, p_ref, o_ref):
        for me in range(N_DEV):
            @pl.when(d_ref[0] == me)
            def _(me=me):
                t = [(own_ref[0] if j == me else p_ref[j]).astype(F32) for j in range(N_DEV)]
                o_ref[...] = ((((((t[0] + t[1]) + t[2]) + t[3]) + t[4]) + t[5]) + t[6]) + t[7]

    gs = pltpu.PrefetchScalarGridSpec(
        num_scalar_prefetch=1, grid=(nh,),
        in_specs=[pl.BlockSpec((1, th, D), lambda i, d: (d[1], d[2] * nh + i, 0)), pl.BlockSpec((N_DEV, th, D), lambda i, d: (0, i, 0))],
        out_specs=pl.BlockSpec((th, D), lambda i, d: (i, 0)))
    return pl.pallas_call(body, name="reduce_late_add", grid_spec=gs, out_shape=SDS((HALF, D), F32),
                          compiler_params=pltpu.CompilerParams(dimension_semantics=("parallel",),
                                                               vmem_limit_bytes=48 << 20))(didx, gpk, parts)


def _pack_early(b, dtype):
    lanes = lambda a: jnp.pad(a.astype(dtype), ((0, 0), (0, D - a.shape[1])))
    pair = jnp.concatenate([b["w_q_b"].astype(dtype), b["w_ple"].astype(dtype), jnp.zeros((256, D - 640), dtype)], axis=1)
    return jnp.concatenate([lanes(b["w_in"]), pair, lanes(b["w_kv_b"])], axis=0)


def _pack_late(b, dtype):
    return jnp.concatenate([b[n].astype(dtype) for n in ("w_mla_up", "w_swa_up", "w_out", "w_ple_gate", "w_mlp_up", "w_mlp_down")],
                           axis=0)


def _unpack_shards(pk, which):
    return {n: pk[PACK_AT[n][1]:PACK_AT[n][1] + r, PACK_AT[n][2]:PACK_AT[n][2] + c] for n, r, c in BIG if PACK_AT[n][0] == which}


def _full_weights(gathered, own, chip, which):
    own_b = _unpack_shards(own, which)
    per_chip = [{n: jnp.where(chip == j, own_b[n], blk) for n, blk in _unpack_shards(gathered[j], which).items()}
                for j in range(N_CHIPS)]
    out = {}
    for n in own_b:
        shards = [pc[n] for pc in per_chip]
        if n == "w_in":
            out["w_in_p"] = _w_in_internal(shards)
        else:
            out[n] = jnp.concatenate(shards, axis=1 if n in COL_SHARDED else 0)
    return out


def _split_full_grads(grads, pack, dtype):
    shard = {n: (r, c) for n, r, c in BIG}
    chunks = []
    for j in range(N_CHIPS):
        blocks = {}
        for n, g in grads.items():
            if n == "w_in_p":
                blocks["w_in"] = _w_in_grad_shard(g, j)
                continue
            r, c = shard[n]
            blocks[n] = g[:, j * c:(j + 1) * c] if n in COL_SHARDED else g[j * r:(j + 1) * r]
        chunks.append(pack(blocks, dtype))
    return jnp.stack(chunks)


W_IN_SHARD = 936
W_IN_SEGMENTS = ((0, 256, (3072,)), (256, 384, (3840,)), (384, 416, (4032,)), (416, 1440, (0,)), (1440, 1504, (3328, 3392)),
                 (1504, 1568, (3456, 3520)), (1568, 1632, (3584, 3648)), (1632, 1696, (3712, 3776)), (1696, 3744, (1024,)))


def _w_in_internal(shards):
    def cols(a, b):
        out = []
        for j, s in enumerate(shards):
            lo, hi = max(a, W_IN_SHARD * j), min(b, W_IN_SHARD * (j + 1))
            if lo < hi:
                out.append(s[:, lo - W_IN_SHARD * j:hi - W_IN_SHARD * j])
        return out

    pieces = {}
    for a, b, places in W_IN_SEGMENTS:
        for at in places:
            pieces[at] = cols(a, b)
    zeros = lambda n: [jnp.zeros((D, n), shards[0].dtype)]
    pieces[3968] = zeros(64)
    pieces[4064] = zeros(32)
    return jnp.concatenate([piece for at in sorted(pieces) for piece in pieces[at]], axis=1)


def _w_in_grad_shard(g, j):
    def internal(a, b):
        out = []
        while a < b:
            end = min(b, (a // D + 1) * D)
            out.append(g[a // D][:, a % D:a % D + end - a])
            a = end
        return out

    out = []
    for a, b, places in W_IN_SEGMENTS:
        lo, hi = max(a, W_IN_SHARD * j), min(b, W_IN_SHARD * (j + 1))
        if lo < hi:
            parts = [internal(at + lo - a, at + hi - a) for at in places]
            if len(parts) == 1:
                out += parts[0]
            else:
                assert len(parts[0]) == len(parts[1]) == 1
                out.append(parts[0][0] + parts[1][0])
    return jnp.concatenate(out, axis=1)


def _local_step(x, p, tgt, w, small, late_weights, late_grads_out):
    T = x.shape[0]
    tm = 256
    tb = 256
    w_in_p = w["w_in_p"]
    wqb = jnp.pad(w["w_q_b"].reshape(Q_LORA, MLA_HEADS, 96), ((0, 0), (0, 0), (0, 32))).reshape(Q_LORA, 2048)
    wkv = w["w_kv_b"].reshape(KV_LORA, MLA_HEADS, 128)
    wkn = jnp.pad(wkv[:, :, :64], ((0, 0), (0, 0), (0, 64))).reshape(KV_LORA, 2048)
    wv = wkv[:, :, 64:].reshape(KV_LORA, 1024)
    tab_m = _rope_tables(T, "mla")
    tab_s = _rope_tables(T, "swa")
    g1, gq, gkv, sinks = small["g_mix_pre"], small["g_q_a"], small["g_kv_a"], small["sinks"]
    g2, g3, g4, g5 = small["g_mix_post"], small["g_mlp_pre"], small["g_mlp_post"], small["g_ple"]
    sink_vec = sinks.reshape(SWA_HEADS)

    z, h1 = _fwd_in(x, g1, w_in_p, tm)
    qn, kvn, qm, km, vm, qt, kt, vt, qs, ks, vs = _fwd_qkv(z, gq, gkv, wqb, wkn, wv, tab_m, tab_s, tb)
    om, lse_m = _mla_fwd(qm, km, vt, tb)
    os_, lse_s = _swa_fwd(sink_vec, qs, ks, vs)
    w = {**w, **late_weights((om, os_))}
    y, yo, au, bu, x1 = _fwd_mix(om, os_, z, x, w["w_mla_up"], w["w_swa_up"], w["w_out"], g2, tm)
    h2, u = _fwd_mlp_up(x1, g3, w["w_mlp_up"], tm)
    d, x2 = _fwd_mlp_down(u, w["w_mlp_down"], x1, g4, tm)
    loss, dx2, dgt, de0, dg5 = _ple_fwd_bwd(p, x2, tgt, w["w_ple"], g5, w["w_ple_gate"], tm)

    dd, da, dg4 = _bwd_mlp_down(dx2, d, g4, w["w_mlp_down"], u, tm)
    dx1, dg3 = _bwd_mlp_up(da, w["w_mlp_up"], x1, g3, dx2, tm)
    dyo, dg2, dau, dbu, dga, dgb, dos, delta_m, dom_t = _bwd_mix(dx1, yo, g2, w["w_out"], z, au, bu, w["w_mla_up"],
                                                                w["w_swa_up"], om, tb)
    gpk_late = lax.empty((N_CHIPS, PACK_ROWS["late"], D), BF16)
    for weight, a_, g_ in (("w_mla_up", om, dau), ("w_swa_up", os_, dbu), ("w_out", y, dyo), ("w_ple_gate", x2, dgt),
                           ("w_mlp_up", h2, da), ("w_mlp_down", u, dd)):
        gpk_late = _wgrad(a_, g_, "wgrad_" + weight[2:], into=(gpk_late, weight))
    token = late_grads_out(gpk_late)
    delta_m = delta_m + token[0, 0]
    dqm, dkm, dvm = _mla_bwd(qm, qt, km, kt, vm, dom_t, lse_m, delta_m, tb)
    dqs, dkc, dkp, dvc, dvp, dsink = _swa_bwd(sink_vec, qs, ks, vs, dos, os_, lse_s)
    dqb, dknb, dvb, dsq, drest, dgq, dgkv = _bwd_qkv(dqm, dkm, dvm, dqs, dkc, dkp, dvc, dvp, z, gq, gkv, wqb, wkn, wv,
                                                      tab_m, tab_s)
    gx, dg1 = _bwd_in(dsq, dga, dgb, drest, w_in_p, x, g1, dx1, tm)

    g_in_p = [_wgrad(h1, dsq, "wgrad_in_sq"), _wgrad(h1, dga, "wgrad_in_ga"), _wgrad(h1, dgb, "wgrad_in_gb"),
              _wgrad(h1, drest, "wgrad_in_rest")]
    g_qb_p = _wgrad(qn, dqb, "wgrad_q_b")
    g_kn_p = _wgrad(kvn, dknb, "wgrad_kv_b_nope")
    g_v_p = _wgrad(kvn, dvb, "wgrad_kv_b_v")
    grads = {
        "w_in_p": g_in_p,
        "w_q_b": g_qb_p.reshape(Q_LORA, MLA_HEADS, 128)[:, :, :96].reshape(Q_LORA, 1536),
        "w_kv_b": jnp.concatenate([g_kn_p.reshape(KV_LORA, MLA_HEADS, 128)[:, :, :64], g_v_p.reshape(KV_LORA, MLA_HEADS, 64)],
                                  axis=2).reshape(KV_LORA, 2048),
        "w_ple": _wgrad(p, de0, "wgrad_ple"),
    }
    small_grads = {"g_mix_pre": dg1, "g_q_a": dgq, "g_kv_a": dgkv, "sinks": dsink[0:1, 0:SWA_HEADS], "g_mix_post": dg2,
                   "g_mlp_pre": dg3, "g_mlp_post": dg4, "g_ple": dg5}
    return loss, gx, grads, small_grads


def _pack_small(vals, fill, scalar=None):
    wide = [vals[n] for n, k in SMALL if k == D]
    narrow = [vals[n] for n, k in SMALL if k != D]
    used = sum(k for _, k in SMALL if k != D)
    last = jnp.concatenate(narrow + [jnp.full((1, D - used), fill, F32)], axis=1)
    rest = jnp.full((2, D), fill, F32)
    if scalar is not None:
        rest = jnp.concatenate([jnp.concatenate([scalar, rest[0:1, 1:]], axis=1), rest[1:2]], axis=0)
    return jnp.concatenate(wide + [last, rest], axis=0)


def _unpack_small(pk):
    out, row, off = {}, 0, 0
    for n, k in SMALL:
        if k == D:
            out[n] = pk[row:row + 1]
            row += 1
    for n, k in SMALL:
        if k != D:
            out[n] = pk[5:6, off:off + k]
            off += k
    return out


def kernel(x, p, g_mix_pre, w_in, g_q_a, w_q_b, g_kv_a, w_kv_b, sinks, w_mla_up, w_swa_up, w_out, g_mix_post, g_mlp_pre, w_mlp_up, w_mlp_down, g_mlp_post, w_ple, g_ple, w_ple_gate, loss_target, m_g_mix_pre, m_w_in, m_g_q_a, m_w_q_b, m_g_kv_a, m_w_kv_b, m_sinks, m_w_mla_up, m_w_swa_up, m_w_out, m_g_mix_post, m_g_mlp_pre, m_w_mlp_up, m_w_mlp_down, m_g_mlp_post, m_w_ple, m_g_ple, m_w_ple_gate, v_g_mix_pre, v_w_in, v_g_q_a, v_w_q_b, v_g_kv_a, v_w_kv_b, v_sinks, v_w_mla_up, v_w_swa_up, v_w_out, v_g_mix_post, v_g_mlp_pre, v_w_mlp_up, v_w_mlp_down, v_g_mlp_post, v_w_ple, v_g_ple, v_w_ple_gate):
    given = dict(locals())
    big_w = {n: given[n][0] for n, _, _ in BIG}
    small_w = {n: given[n] for n, _ in SMALL}
    small_m = {n: given["m_" + n] for n, _ in SMALL}
    small_v = {n: given["v_" + n] for n, _ in SMALL}

    core = lax.axis_index("c")
    chip = 2 * lax.axis_index("x") + lax.axis_index("y")
    core_i = core.astype(jnp.int32).reshape(1)
    chip_i = chip.astype(jnp.int32).reshape(1)
    dev_i = jnp.stack([2 * chip + core, chip, core]).astype(jnp.int32)

    own_early = _pack_early(big_w, BF16)
    own_late = _pack_late(big_w, BF16)
    got_early = _all_gather(own_early)
    late_flight = _gather_late_start(own_late, got_early)
    weights = _full_weights(got_early, own_early, chip, "early")
    step_small = {**small_w, "g_mix_pre": small_w["g_mix_pre"] + late_flight[4][0, 0]}

    def late_weights(after):
        return _full_weights(_gather_late_wait(*late_flight[:4], after), own_late, chip, "late")

    flight = {}

    def late_grads_out(gpk_late):
        flight["late"] = _reduce_late_start(gpk_late, dev_i)
        return flight["late"][4]

    loss_blk, gx, grads, small_grads = _local_step(x[0], p[0, 0], loss_target[0], weights, step_small, late_weights,
                                                   late_grads_out)

    gpk = _split_full_grads(grads, _pack_early, F32)
    got = _rs_sibling(gpk)
    part = _rs_add_sibling(core_i, gpk, got)
    parts, small_parts = _rs_chips(part, _pack_small(small_grads, 0.0, loss_blk[0:1, 0:1]))
    mine_early = _rs_add_chips(chip_i, part, parts)
    gpk_late, parts_late = _reduce_late_wait(*flight["late"][:4], mine_early)
    mine_late = _reduce_late_add(dev_i, gpk_late, parts_late)
    theirs_early, theirs_late = _rs_join(mine_early, mine_late)
    joined = {"early": jnp.where(core == 0, jnp.concatenate([mine_early, theirs_early]), jnp.concatenate([theirs_early, mine_early])),
              "late": jnp.where(core == 0, jnp.concatenate([mine_late, theirs_late]), jnp.concatenate([theirs_late, mine_late]))}

    g_small_pk, d_small_pk, m_small_pk, v_small_pk = _adamw_small(
        _pack_small(small_w, 0.0), small_parts, _pack_small(small_m, 0.0), _pack_small(small_v, 1.0))
    loss = g_small_pk[6, 0]
    g_small, d_small = _unpack_small(g_small_pk), _unpack_small(d_small_pk)
    m_small, v_small = _unpack_small(m_small_pk), _unpack_small(v_small_pk)

    out_g, out_d, out_m, out_v = dict(g_small), dict(d_small), dict(m_small), dict(v_small)
    for n, _, _ in BIG:
        out_g[n], out_d[n], out_m[n], out_v[n] = _adamw(given[n], joined[PACK_AT[n][0]], given["m_" + n], given["v_" + n], n)
    order = ["g_mix_pre", "w_in", "g_q_a", "w_q_b", "g_kv_a", "w_kv_b", "sinks", "w_mla_up", "w_swa_up", "w_out", "g_mix_post",
             "g_mlp_pre", "w_mlp_up", "w_mlp_down", "g_mlp_post", "w_ple", "g_ple", "w_ple_gate"]
    return (loss, gx[None], *[out_g[n] for n in order], *[out_d[n] for n in order], *[out_m[n] for n in order],
            *[out_v[n] for n in order])
```

```python
import math

import jax
import jax.numpy as jnp
from jax import lax
from jax.experimental import pallas as pl
from jax.experimental.pallas import tpu as pltpu

F32 = jnp.float32
BF16 = jnp.bfloat16
SDS = jax.ShapeDtypeStruct

D = 1024
D_FF = 4096
PLE = 256
Q_LORA = 256
KV_LORA = 128
MLA_HEADS = 16
MLA_NOPE = 64
MLA_ROPE = 32
SWA_HEADS = 16
SWA_HD = 64
WINDOW = 128
ROPE_THETA = 10000.0
EPS = 1e-6
NEG = -1e30
NZ = 4096
MLA_SCALE = (MLA_NOPE + MLA_ROPE) ** -0.5
LOG2_E = math.log2(math.e)
MLA_LOG2_SCALE = MLA_SCALE * LOG2_E
SWA_SCALE = SWA_HD ** -0.5

ADAM_LR = 0.001
ADAM_B1 = 0.9
ADAM_B2 = 0.999
ADAM_EPS = 1e-08
ADAM_WD = 0.01
ADAM_STEP = 10

LANES = 128
ATT_COLS = 128
N_CHIPS = 4
N_DEV = 8
MESH = pl.DeviceIdType.MESH

NT = (((1,), (1,)), ((), ()))
TN = (((0,), (0,)), ((), ()))

BIG = (("w_in", 1024, 936), ("w_q_b", 256, 384), ("w_kv_b", 128, 512), ("w_mla_up", 256, 1024),
       ("w_swa_up", 256, 1024), ("w_out", 256, 1024), ("w_mlp_up", 1024, 1024), ("w_mlp_down", 1024, 1024),
       ("w_ple", 256, 256), ("w_ple_gate", 256, 1024))
COL_SHARDED = ("w_in", "w_q_b", "w_kv_b", "w_mlp_up", "w_ple")
PACK_AT = {"w_in": ("early", 0, 0), "w_q_b": ("early", 1024, 0), "w_ple": ("early", 1024, 384), "w_kv_b": ("early", 1280, 0),
           "w_mla_up": ("late", 0, 0), "w_swa_up": ("late", 256, 0), "w_out": ("late", 512, 0), "w_ple_gate": ("late", 768, 0),
           "w_mlp_up": ("late", 1024, 0), "w_mlp_down": ("late", 2048, 0)}
PACK_ROWS = {"early": 1408, "late": 3072}
SMALL = (("g_mix_pre", 1024), ("g_q_a", 256), ("g_kv_a", 128), ("sinks", 16), ("g_mix_post", 1024),
         ("g_mlp_pre", 1024), ("g_mlp_post", 1024), ("g_ple", 1024))


def _dot(a, b):
    return jnp.dot(a, b, preferred_element_type=F32)


def _dot_nt(a, b):
    return lax.dot_general(a, b, NT, preferred_element_type=F32)


def _dot_tn(a, b):
    return lax.dot_general(a, b, TN, preferred_element_type=F32)


def _pcall(body, *, name, out_shape, grid=(), in_specs=None, out_specs=None, scratch=(), sem=None, vmem_mb=48, aliases=None):
    params = dict(vmem_limit_bytes=vmem_mb << 20)
    if sem is not None:
        params["dimension_semantics"] = sem
    return pl.pallas_call(body, name=name, grid=grid, in_specs=in_specs, out_specs=out_specs, out_shape=out_shape,
                          scratch_shapes=list(scratch), input_output_aliases=aliases or {},
                          compiler_params=pltpu.CompilerParams(**params))


def _rows(tm, n, col=0):
    return pl.BlockSpec((tm, n), lambda i: (i, col))


def _full(shape):
    return pl.BlockSpec(shape, lambda i: (0,) * len(shape))


def _rms(x, g):
    r = lax.rsqrt(jnp.mean(x * x, axis=-1, keepdims=True) + EPS)
    return x * r * g


def _rms_bwd(dy, x, g):
    r = lax.rsqrt(jnp.mean(x * x, axis=-1, keepdims=True) + EPS)
    xn = x * r
    dn = dy * g
    dx = r * (dn - xn * jnp.mean(dn * xn, axis=-1, keepdims=True))
    return dx, jnp.sum(dy * xn, axis=0, keepdims=True)


def _sigmoid(x):
    return 1.0 / (1.0 + jnp.exp(-x))


def _rope(x, c, a, b, half):
    return x * c + pltpu.roll(x, LANES - half, 1) * a + pltpu.roll(x, half, 1) * b


def _rope_tables(T, kind):
    lane = jnp.arange(LANES)
    if kind == "mla":
        half = MLA_ROPE // 2
        rel = lane - MLA_NOPE
        on = (rel >= 0) & (rel < MLA_ROPE)
        d = MLA_ROPE
    else:
        half = SWA_HD // 2
        rel = lane % SWA_HD
        on = jnp.ones((LANES,), bool)
        d = SWA_HD
    first = on & (rel < half)
    second = on & (rel >= half)
    f = jnp.where(first, rel, rel - half).astype(F32)
    inv = jnp.exp(-math.log(ROPE_THETA) * f * (2.0 / d))
    ang = jnp.arange(T, dtype=F32)[:, None] * inv[None, :]
    cos, sin = jnp.cos(ang), jnp.sin(ang)
    c = jnp.where(on[None], cos, 1.0)
    a = jnp.where(first[None], -sin, 0.0)
    b = jnp.where(second[None], sin, 0.0)
    return c, a, b


def _fwd_in(x, g1, w_in_p, tm):
    T = x.shape[0]

    def body(x_ref, g_ref, w_ref, z_ref, h_ref):
        h = _rms(x_ref[...], g_ref[...]).astype(BF16)
        h_ref[...] = h
        z_ref[...] = _dot(h, w_ref[...])

    return _pcall(body, name="fwd_in", grid=(T // tm,),
                  in_specs=[_rows(tm, D), _full((1, D)), _full((D, NZ))],
                  out_specs=[_rows(tm, NZ), _rows(tm, D)],
                  out_shape=[SDS((T, NZ), F32), SDS((T, D), BF16)], sem=("parallel",))(x, g1, w_in_p)


def _fwd_qkv(z, gq, gkv, wqb, wkn, wv, tab_m, tab_s, tm):
    T = z.shape[0]

    def body(qa_ref, sq_ref, skd_ref, svd_ref, kva_ref, kr_ref, gq_ref, gkv_ref, wqb_ref, wkn_ref, wv_ref,
             cm_ref, am_ref, bm_ref, cs_ref, as_ref, bs_ref,
             qn_ref, kvn_ref, qm_ref, km_ref, vm_ref, qt_ref, kt_ref, vt_ref, qs_ref, ks_ref, vs_ref):
        qn = _rms(qa_ref[...], gq_ref[...]).astype(BF16)
        qn_ref[...] = qn
        kvn = _rms(kva_ref[...], gkv_ref[...]).astype(BF16)
        kvn_ref[...] = kvn
        cm, am, bm = cm_ref[...], am_ref[...], bm_ref[...]
        cs, as_, bs = cs_ref[...], as_ref[...], bs_ref[...]
        k_rope = _rope(kr_ref[...], cm, am, bm, MLA_ROPE // 2)
        vt_row = lax.broadcasted_iota(jnp.int32, (LANES, tm), 0)
        v_all = _dot(kvn, wv_ref[...])
        q_all = _dot(qn, wqb_ref[...])
        k_all = _dot(kvn, wkn_ref[...])
        for j in range(D // LANES):
            sl = slice(LANES * j, LANES * (j + 1))
            v = v_all[:, sl]
            vm_ref[:, sl] = v.astype(BF16)
            v_t = v.T
            for hh, rows64 in enumerate((v_t, pltpu.roll(v_t, 64, 0))):
                blk = jnp.where(vt_row < 64, rows64, jnp.where(vt_row == 64, 1.0, 0.0))
                vt_ref[0, LANES * (2 * j + hh):LANES * (2 * j + hh + 1), :] = blk.astype(BF16)
        for h in range(MLA_HEADS):
            sl = slice(LANES * h, LANES * (h + 1))
            qh = _rope(q_all[:, sl], cm, am, bm, MLA_ROPE // 2)
            qm_ref[:, sl] = qh.astype(BF16)
            qt_ref[0, sl, :] = qh.T.astype(BF16)
            k = k_all[:, sl] + k_rope
            km_ref[:, sl] = k.astype(BF16)
            kt_ref[0, sl, :] = k.T.astype(BF16)
        for j in range(D // LANES):
            sl = slice(LANES * j, LANES * (j + 1))
            qs_ref[:, sl] = _rope(sq_ref[:, sl], cs, as_, bs, SWA_HD // 2).astype(BF16)
        for j in range(2):
            sl = slice(LANES * j, LANES * (j + 1))
            ks_ref[:, sl] = _rope(skd_ref[:, sl], cs, as_, bs, SWA_HD // 2).astype(BF16)
        vs_ref[...] = svd_ref[...].astype(BF16)

    tab = [_rows(tm, LANES)] * 6
    return _pcall(body, name="fwd_qkv", grid=(T // tm,),
                  in_specs=[_rows(tm, 256, 12), _rows(tm, 1024, 0), _rows(tm, 256, 13), _rows(tm, 256, 14),
                            _rows(tm, 128, 30), _rows(tm, 128, 31), _full((1, Q_LORA)), _full((1, KV_LORA)),
                            _full((Q_LORA, 2048)), _full((KV_LORA, 2048)), _full((KV_LORA, 1024))] + tab,
                  out_specs=[_rows(tm, Q_LORA), _rows(tm, KV_LORA), _rows(tm, 2048), _rows(tm, 2048), _rows(tm, 1024),
                             pl.BlockSpec((1, 2048, tm), lambda i: (i, 0, 0)), pl.BlockSpec((1, 2048, tm), lambda i: (i, 0, 0)),
                             pl.BlockSpec((1, 2048, tm), lambda i: (i, 0, 0)),
                             _rows(tm, 1024), _rows(tm, 256), _rows(tm, 256)],
                  out_shape=[SDS((T, Q_LORA), BF16), SDS((T, KV_LORA), BF16), SDS((T, 2048), BF16), SDS((T, 2048), BF16),
                             SDS((T, 1024), BF16), SDS((T // tm, 2048, tm), BF16), SDS((T // tm, 2048, tm), BF16),
                             SDS((T // tm, 2048, tm), BF16),
                             SDS((T, 1024), BF16), SDS((T, 256), BF16), SDS((T, 256), BF16)],
                  sem=("parallel",))(z, z, z, z, z, z, gq, gkv, wqb, wkn, wv, *tab_m, *tab_s)


def _mla_fwd(qm, km, vt, tb):
    T = qm.shape[0]
    nb = T // tb
    cc = ATT_COLS

    def body(q_ref, k_ref, vt_ref, o_ref, l_ref, s_ref, p_ref, al_ref, m_ref, acc_ref):
        i = pl.program_id(1)
        m_ref[...] = jnp.full(m_ref.shape, NEG, F32)
        acc_ref[...] = jnp.zeros_like(acc_ref)
        p_ref[1] = jnp.zeros(p_ref.shape[1:], BF16)
        al_ref[1] = jnp.ones(al_ref.shape[1:], F32)
        key = lax.broadcasted_iota(jnp.int32, (tb, cc), 0)
        qry = lax.broadcasted_iota(jnp.int32, (tb, cc), 1)

        def scores(j, slot):
            off = pl.multiple_of(j * tb, tb)
            for hh in range(2):
                sl = slice(LANES * hh, LANES * (hh + 1))
                s_ref[slot, hh] = _dot_nt(k_ref[pl.ds(off, tb), sl], q_ref[:, sl])

        def softmax(slot, diagonal):
            chains = [(hh, slice(cc * c, cc * (c + 1)), c) for hh in range(2) for c in range(tb // cc)]

            def scaled(hh, cols, c):
                t = s_ref[slot, hh, :, cols] * MLA_LOG2_SCALE
                return jnp.where(key <= qry + cc * c, t, NEG) if diagonal else t

            tops = []
            for hh, cols, c in chains:
                if diagonal:
                    top = jnp.max(scaled(hh, cols, c), axis=0, keepdims=True)
                else:
                    top = jnp.max(s_ref[slot, hh, :, cols], axis=0, keepdims=True) * MLA_LOG2_SCALE
                m_old = m_ref[hh, :, cols]
                mn = jnp.maximum(m_old, top)
                m_ref[hh, :, cols] = mn
                al_ref[slot, hh, :, cols] = jnp.exp2(m_old - mn)
                tops.append(mn)
            for (hh, cols, c), mn in zip(chains, tops):
                p_ref[slot, hh, :, cols] = jnp.exp2(scaled(hh, cols, c) - mn).astype(BF16)

        def accumulate(j, slot):
            for hh in range(2):
                acc_ref[hh] = al_ref[slot, hh] * acc_ref[hh] + _dot(vt_ref[j, LANES * hh:LANES * (hh + 1), :], p_ref[slot, hh])

        def step(t, carry):
            scores(2 * t + 1, 1)
            accumulate(jnp.maximum(2 * t - 1, 0), 1)
            softmax(0, False)
            scores(2 * t + 2, 0)
            accumulate(2 * t, 0)
            softmax(1, False)
            return carry

        scores(0, 0)
        lax.fori_loop(0, i // 2, step, 0)

        @pl.when(i % 2 == 1)
        def _():
            scores(i, 1)
            accumulate(jnp.maximum(i - 2, 0), 1)
            softmax(0, False)
            accumulate(i - 1, 0)
            softmax(1, True)
            accumulate(i, 1)

        @pl.when(i % 2 == 0)
        def _():
            accumulate(jnp.maximum(i - 1, 0), 1)
            softmax(0, True)
            accumulate(i, 0)
        den = [acc_ref[hh, 64:65, :] for hh in range(2)]
        o_ref[...] = jnp.concatenate([acc_ref[hh, 0:64, :] / den[hh] for hh in range(2)], axis=0).T
        sub = lax.broadcasted_iota(jnp.int32, (8, tb), 0)
        lse = [m_ref[hh] + jnp.log(den[hh]) * LOG2_E for hh in range(2)]
        l_ref[0, 0] = jnp.where(sub == 0, lse[0], jnp.where(sub == 1, lse[1], 0.0))

    return _pcall(body, name="mla_fwd", grid=(MLA_HEADS // 2, nb),
                  in_specs=[pl.BlockSpec((tb, 256), lambda p, i: (i, p)), pl.BlockSpec((T, 256), lambda p, i: (0, p)),
                            pl.BlockSpec((nb, 2 * LANES, tb), lambda p, i: (0, p, 0))],
                  out_specs=[pl.BlockSpec((tb, LANES), lambda p, i: (i, p)),
                             pl.BlockSpec((1, 1, 8, tb), lambda p, i: (p, i, 0, 0))],
                  out_shape=[SDS((T, D), F32), SDS((MLA_HEADS // 2, nb, 8, tb), F32)],
                  scratch=[pltpu.VMEM((2, 2, tb, tb), F32), pltpu.VMEM((2, 2, tb, tb), BF16), pltpu.VMEM((2, 2, 1, tb), F32),
                           pltpu.VMEM((2, 1, tb), F32), pltpu.VMEM((2, LANES, tb), F32)],
                  sem=("parallel", "arbitrary"))(qm, km, vt)


def _swa_mask(n):
    row = lax.broadcasted_iota(jnp.int32, (WINDOW, 2 * WINDOW), 0)
    col = lax.broadcasted_iota(jnp.int32, (WINDOW, 2 * WINDOW), 1)
    rel = row - col + WINDOW
    return (rel >= 0) & (rel < WINDOW) & ((col >= WINDOW) | (n > 0))


def _swa_specs(T):
    nb = T // WINDOW
    cur = lambda w: pl.BlockSpec((WINDOW, w), lambda n: (n, 0))
    prev = lambda w: pl.BlockSpec((WINDOW, w), lambda n: (jnp.maximum(n - 1, 0), 0))
    return nb, cur, prev


def _swa_fwd(sinks, qs, ks, vs):
    T = qs.shape[0]
    nb, cur, prev = _swa_specs(T)

    def body(sink_ref, q_ref, kc_ref, kp_ref, vc_ref, vp_ref, o_ref, l_ref, kb_ref, vb_ref, s_ref, p_ref):
        n = pl.program_id(0)
        mask = _swa_mask(n)
        lo = lax.broadcasted_iota(jnp.int32, (WINDOW, LANES), 1) < 64
        hi = jnp.logical_not(lo)
        for g in range(2):
            gs = slice(LANES * g, LANES * (g + 1))
            kb_ref[g] = jnp.concatenate([kp_ref[:, gs], kc_ref[:, gs]], axis=0)
            vb_ref[g] = jnp.concatenate([vp_ref[:, gs], vc_ref[:, gs]], axis=0)
        for h in range(SWA_HEADS):
            qp = q_ref[:, LANES * (h // 2):LANES * (h // 2 + 1)]
            qh = jnp.where(lo if h % 2 == 0 else hi, qp, jnp.zeros_like(qp))
            s_ref[h] = _dot_nt(qh, kb_ref[h // 8])
        for j in range(SWA_HEADS // 2):
            sl = slice(LANES * j, LANES * (j + 1))
            lses = []
            for h in (2 * j, 2 * j + 1):
                s = jnp.where(mask, s_ref[h] * SWA_SCALE, NEG)
                sk = sink_ref[h]
                m = jnp.maximum(jnp.max(s, axis=1, keepdims=True), sk)
                e = jnp.exp(s - m)
                den = jnp.sum(e, axis=1, keepdims=True) + jnp.exp(sk - m)
                p_ref[h] = (e / den).astype(BF16)
                lses.append(jnp.broadcast_to(m + jnp.log(den), (WINDOW, LANES)))
            l_ref[:, sl] = jnp.where(lo, lses[0], lses[1])
        for j in range(SWA_HEADS // 2):
            vb = vb_ref[j // 4]
            o_ref[:, LANES * j:LANES * (j + 1)] = jnp.where(lo, _dot(p_ref[2 * j], vb), _dot(p_ref[2 * j + 1], vb))

    return _pcall(body, name="swa_fwd", grid=(nb,),
                  in_specs=[pl.BlockSpec(memory_space=pltpu.SMEM), cur(D), cur(256), prev(256), cur(256), prev(256)],
                  out_specs=[cur(D), cur(D)], out_shape=[SDS((T, D), F32)] * 2,
                  scratch=[pltpu.VMEM((2, 2 * WINDOW, LANES), BF16), pltpu.VMEM((2, 2 * WINDOW, LANES), BF16),
                           pltpu.VMEM((SWA_HEADS, WINDOW, 2 * WINDOW), F32), pltpu.VMEM((SWA_HEADS, WINDOW, 2 * WINDOW), BF16)],
                  sem=("parallel",))(sinks, qs, ks, ks, vs, vs)


def _fwd_mix(om, os_, z, x, wmu, wsu, wo, g2, tm):
    T = x.shape[0]

    def body(om_ref, os_ref, ga_ref, gb_ref, x_ref, wmu_ref, wsu_ref, wo_ref, g2_ref,
             y_ref, yo_ref, au_ref, bu_ref, x1_ref):
        au = _dot(om_ref[...].astype(BF16), wmu_ref[...])
        bu = _dot(os_ref[...].astype(BF16), wsu_ref[...])
        au_ref[...] = au
        bu_ref[...] = bu
        y = (_sigmoid(ga_ref[...]) * au + _sigmoid(gb_ref[...]) * bu).astype(BF16)
        y_ref[...] = y
        yo = _dot(y, wo_ref[...])
        yo_ref[...] = yo
        x1_ref[...] = x_ref[...] + _rms(yo, g2_ref[...])

    r = _rows(tm, D)
    w = _full((D, D))
    return _pcall(body, name="fwd_mix", grid=(T // tm,),
                  in_specs=[r, r, _rows(tm, D, 1), _rows(tm, D, 2), r, w, w, w, _full((1, D))],
                  out_specs=[r] * 5,
                  out_shape=[SDS((T, D), BF16), SDS((T, D), F32), SDS((T, D), F32), SDS((T, D), F32), SDS((T, D), F32)],
                  sem=("parallel",))(om, os_, z, z, x, wmu, wsu, wo, g2)


def _fwd_mlp_up(x1, g3, w1, tm):
    T = x1.shape[0]

    def body(x_ref, g_ref, w_ref, h_ref, u_ref):
        h = _rms(x_ref[...], g_ref[...]).astype(BF16)
        h_ref[...] = h
        u_ref[...] = jnp.square(jnp.maximum(_dot(h, w_ref[...]), 0.0)).astype(BF16)

    return _pcall(body, name="fwd_mlp_up", grid=(T // tm,),
                  in_specs=[_rows(tm, D), _full((1, D)), _full((D, D_FF))],
                  out_specs=[_rows(tm, D), _rows(tm, D_FF)],
                  out_shape=[SDS((T, D), BF16), SDS((T, D_FF), BF16)],
                  sem=("parallel",))(x1, g3, w1)


def _fwd_mlp_down(u, w2, x1, g4, tm):
    T = x1.shape[0]

    def body(u_ref, w_ref, x_ref, g_ref, d_ref, x2_ref):
        d = _dot(u_ref[...], w_ref[...])
        d_ref[...] = d
        x2_ref[...] = x_ref[...] + _rms(d, g_ref[...])

    return _pcall(body, name="fwd_mlp_down", grid=(T // tm,),
                  in_specs=[_rows(tm, D_FF), _full((D_FF, D)), _rows(tm, D), _full((1, D))],
                  out_specs=[_rows(tm, D), _rows(tm, D)], out_shape=[SDS((T, D), F32)] * 2,
                  sem=("parallel",))(u, w2, x1, g4)


def _ple_fwd_bwd(p, x2, tgt, wple, g5, wpg, tm):
    T = x2.shape[0]

    def body(p_ref, x2_ref, t_ref, wple_ref, g5_ref, wpg_ref, loss_ref, dx2_ref, dgt_ref, de0_ref, dg5_ref):
        @pl.when(pl.program_id(0) == 0)
        def _():
            loss_ref[...] = jnp.zeros_like(loss_ref)
            dg5_ref[...] = jnp.zeros_like(dg5_ref)

        e0 = _dot(p_ref[...].astype(BF16), wple_ref[...])
        g5 = g5_ref[...]
        r = lax.rsqrt(jnp.mean(e0 * e0, axis=-1, keepdims=True) + EPS)
        en = e0 * r
        e = en * g5
        x2 = x2_ref[...]
        s = _sigmoid(_dot(x2.astype(BF16), wpg_ref[...]))
        diff = x2 + s * e - t_ref[...]
        sq = jnp.sum(jnp.sum(diff * diff, axis=1, keepdims=True), axis=0, keepdims=True)
        loss_ref[...] += jnp.broadcast_to(sq * (0.5 / D), loss_ref.shape)
        dx3 = diff * (1.0 / D)
        de = dx3 * s
        dgt = (dx3 * e * s * (1.0 - s)).astype(BF16)
        dgt_ref[...] = dgt
        dn = de * g5
        de0_ref[...] = (r * (dn - en * jnp.mean(dn * en, axis=-1, keepdims=True))).astype(BF16)
        dg5_ref[...] += jnp.sum(de * en, axis=0, keepdims=True)
        dx2_ref[...] = dx3 + _dot_nt(dgt, wpg_ref[...])

    r = _rows(tm, D)
    return _pcall(body, name="ple_fwd_bwd", grid=(T // tm,),
                  in_specs=[_rows(tm, PLE), r, r, _full((PLE, D)), _full((1, D)), _full((D, D))],
                  out_specs=[_full((8, LANES)), r, r, r, _full((1, D))],
                  out_shape=[SDS((8, LANES), F32), SDS((T, D), F32), SDS((T, D), BF16), SDS((T, D), BF16), SDS((1, D), F32)],
                  sem=("arbitrary",))(p, x2, tgt, wple, g5, wpg)


def _bwd_mlp_down(dx2, d, g4, w2, u, tm):
    T = dx2.shape[0]

    def body(dx_ref, d_ref, g_ref, w_ref, u_ref, dd_ref, da_ref, dg_ref):
        @pl.when(pl.program_id(0) == 0)
        def _():
            dg_ref[...] = jnp.zeros_like(dg_ref)

        dd, dg = _rms_bwd(dx_ref[...], d_ref[...], g_ref[...])
        dg_ref[...] += dg
        ddb = dd.astype(BF16)
        dd_ref[...] = ddb
        du = _dot_nt(ddb, w_ref[...])
        da_ref[...] = (du * (2.0 * jnp.sqrt(u_ref[...].astype(F32)))).astype(BF16)

    return _pcall(body, name="bwd_mlp_down", grid=(T // tm,),
                  in_specs=[_rows(tm, D), _rows(tm, D), _full((1, D)), _full((D_FF, D)), _rows(tm, D_FF)],
                  out_specs=[_rows(tm, D), _rows(tm, D_FF), _full((1, D))],
                  out_shape=[SDS((T, D), BF16), SDS((T, D_FF), BF16), SDS((1, D), F32)],
                  sem=("arbitrary",))(dx2, d, g4, w2, u)


def _bwd_mlp_up(da, w1, x1, g3, dx2, tm):
    T = dx2.shape[0]

    def body(da_ref, w_ref, x_ref, g_ref, dx2_ref, dx1_ref, dg_ref):
        @pl.when(pl.program_id(0) == 0)
        def _():
            dg_ref[...] = jnp.zeros_like(dg_ref)

        dh = _dot_nt(da_ref[...], w_ref[...])
        dx, dg = _rms_bwd(dh, x_ref[...], g_ref[...])
        dg_ref[...] += dg
        dx1_ref[...] = dx2_ref[...] + dx

    return _pcall(body, name="bwd_mlp_up", grid=(T // tm,),
                  in_specs=[_rows(tm, D_FF), _full((D, D_FF)), _rows(tm, D), _full((1, D)), _rows(tm, D)],
                  out_specs=[_rows(tm, D), _full((1, D))],
                  out_shape=[SDS((T, D), F32), SDS((1, D), F32)], sem=("arbitrary",))(da, w1, x1, g3, dx2)


def _bwd_mix(dx1, yo, g2, wo, z, au, bu, wmu, wsu, om, tm):
    T = dx1.shape[0]

    def body(dx_ref, yo_ref, g_ref, wo_ref, ga_ref, gb_ref, au_ref, bu_ref, wmu_ref, wsu_ref, om_ref,
             dyo_ref, dg_ref, dau_ref, dbu_ref, dga_ref, dgb_ref, dos_ref, dl_ref, dot_ref):
        @pl.when(pl.program_id(0) == 0)
        def _():
            dg_ref[...] = jnp.zeros_like(dg_ref)

        dyo, dg = _rms_bwd(dx_ref[...], yo_ref[...], g_ref[...])
        dg_ref[...] += dg
        dyob = dyo.astype(BF16)
        dyo_ref[...] = dyob
        dy = _dot_nt(dyob, wo_ref[...])
        sa = _sigmoid(ga_ref[...])
        sb = _sigmoid(gb_ref[...])
        dau = (dy * sa).astype(BF16)
        dbu = (dy * sb).astype(BF16)
        dau_ref[...] = dau
        dbu_ref[...] = dbu
        dga_ref[...] = (dy * au_ref[...] * sa * (1.0 - sa)).astype(BF16)
        dgb_ref[...] = (dy * bu_ref[...] * sb * (1.0 - sb)).astype(BF16)
        dom = _dot_nt(dau, wmu_ref[...])
        dos_ref[...] = _dot_nt(dbu, wsu_ref[...])
        prod = dom * om_ref[...]
        sub = lax.broadcasted_iota(jnp.int32, (8, tm), 0)
        for pr in range(MLA_HEADS // 2):
            sl = slice(LANES * pr, LANES * (pr + 1))
            pt = prod[:, sl].T
            d0 = jnp.sum(pt[0:64], axis=0, keepdims=True)
            d1 = jnp.sum(pt[64:128], axis=0, keepdims=True)
            dl_ref[pr, 0] = jnp.where(sub == 0, d0, jnp.where(sub == 1, d1, 0.0))
            dot_ref[0, sl, :] = dom[:, sl].T.astype(BF16)

    r = _rows(tm, D)
    w = _full((D, D))
    return _pcall(body, name="bwd_mix", grid=(T // tm,),
                  in_specs=[r, r, _full((1, D)), w, _rows(tm, D, 1), _rows(tm, D, 2), r, r, w, w, r],
                  out_specs=[r, _full((1, D)), r, r, r, r, r, pl.BlockSpec((MLA_HEADS // 2, 1, 8, tm), lambda i: (0, i, 0, 0)),
                             pl.BlockSpec((1, D, tm), lambda i: (i, 0, 0))],
                  out_shape=[SDS((T, D), BF16), SDS((1, D), F32), SDS((T, D), BF16), SDS((T, D), BF16), SDS((T, D), BF16),
                             SDS((T, D), BF16), SDS((T, D), F32), SDS((MLA_HEADS // 2, T // tm, 8, tm), F32),
                             SDS((T // tm, D, tm), BF16)],
                  sem=("arbitrary",))(dx1, yo, g2, wo, z, z, au, bu, wmu, wsu, om)


def _mla_bwd(qm, qt, km, kt, vm, dot, lse, delta, tb):
    T = qm.shape[0]
    nb = T // tb
    cc = ATT_COLS

    def body(q_ref, qt_ref, k_ref, kt_ref, v_ref, dot_ref, l_ref, dl_ref, dqt_ref, dkt_ref, dvt_ref,
             s_ref, dp_ref, p_ref, ds_ref, vh_ref):
        j = pl.program_id(1)

        @pl.when(j == 0)
        def _():
            dqt_ref[...] = jnp.zeros_like(dqt_ref)

        dkt_ref[...] = jnp.zeros_like(dkt_ref)
        dvt_ref[...] = jnp.zeros_like(dvt_ref)
        lo = lax.broadcasted_iota(jnp.int32, (tb, LANES), 1) < 64
        key = lax.broadcasted_iota(jnp.int32, (tb, cc), 0)
        qry = lax.broadcasted_iota(jnp.int32, (tb, cc), 1)
        v = v_ref[...]
        vh_ref[0] = jnp.where(lo, v, jnp.zeros_like(v))
        vh_ref[1] = jnp.where(lo, jnp.zeros_like(v), v)

        def scores(i, slot):
            rows_i = pl.ds(pl.multiple_of(i * tb, tb), tb)
            for hh in range(2):
                sl = slice(LANES * hh, LANES * (hh + 1))
                s_ref[slot, hh] = _dot_nt(k_ref[:, sl], q_ref[rows_i, sl])
                dp_ref[slot, hh] = _dot(vh_ref[hh], dot_ref[i])

        def grads(i, slot, diagonal):
            lse_i = l_ref[0, i]
            delta_i = dl_ref[0, i]
            for hh in range(2):
                for c in range(tb // cc):
                    cols = slice(cc * c, cc * (c + 1))
                    p = jnp.exp2(s_ref[slot, hh, :, cols] * MLA_LOG2_SCALE - lse_i[hh:hh + 1, cols])
                    if diagonal:
                        p = jnp.where(key <= qry + cc * c, p, 0.0)
                    p_ref[hh, :, cols] = p.astype(BF16)
                    ds_ref[hh, :, cols] = (p * (dp_ref[slot, hh, :, cols] - delta_i[hh:hh + 1, cols]) * MLA_SCALE).astype(BF16)
            for hh in range(2):
                sl = slice(LANES * hh, LANES * (hh + 1))
                half = slice(64 * hh, 64 * (hh + 1))
                dvt_ref[0, half, :] += _dot_nt(dot_ref[i, half, :], p_ref[hh])
                dkt_ref[0, sl, :] += _dot_nt(qt_ref[i, sl, :], ds_ref[hh])
                dqt_ref[i, sl, :] += _dot(kt_ref[0, sl, :], ds_ref[hh])

        n_off = nb - 1 - j

        def step(u, carry):
            i0 = j + 1 + 2 * u
            scores(i0 + 1, 1)
            grads(i0, 0, False)
            scores(jnp.where(i0 + 2 < nb, i0 + 2, j), 0)
            grads(i0 + 1, 1, False)
            return carry

        scores(jnp.where(n_off > 0, j + 1, j), 0)
        lax.fori_loop(0, n_off // 2, step, 0)

        @pl.when(n_off % 2 == 1)
        def _():
            scores(j, 1)
            grads(nb - 1, 0, False)
            grads(j, 1, True)

        @pl.when(n_off % 2 == 0)
        def _():
            grads(j, 0, True)

    blk = lambda w: pl.BlockSpec((tb, w), lambda p, j: (j, p))
    stat = pl.BlockSpec((1, nb, 8, tb), lambda p, j: (p, 0, 0, 0))
    pair_t = lambda w: pl.BlockSpec((nb, w, tb), lambda p, j: (0, p, 0))
    blk_t = lambda w: pl.BlockSpec((1, w, tb), lambda p, j: (j, p, 0))
    return _pcall(body, name="mla_bwd", grid=(MLA_HEADS // 2, nb),
                  in_specs=[pl.BlockSpec((T, 256), lambda p, j: (0, p)), pair_t(256), blk(256), blk_t(256), blk(LANES),
                            pair_t(LANES), stat, stat],
                  out_specs=[pair_t(256), blk_t(256), blk_t(LANES)],
                  out_shape=[SDS((nb, 2048, tb), F32), SDS((nb, 2048, tb), F32), SDS((nb, D, tb), F32)],
                  scratch=[pltpu.VMEM((2, 2, tb, tb), F32), pltpu.VMEM((2, 2, tb, tb), F32), pltpu.VMEM((2, tb, tb), BF16),
                           pltpu.VMEM((2, tb, tb), BF16), pltpu.VMEM((2, tb, LANES), BF16)],
                  sem=("parallel", "arbitrary"))(qm, qt, km, kt, vm, dot, lse, delta)


def _swa_bwd(sinks, qs, ks, vs, do, o, lse):
    T = qs.shape[0]
    nb, cur, prev = _swa_specs(T)

    def body(sink_ref, q_ref, kc_ref, kp_ref, vc_ref, vp_ref, do_ref, o_ref, l_ref,
             dq_ref, dkc_ref, dkp_ref, dvc_ref, dvp_ref, dsink_ref, kb_ref, vb_ref, s_ref, dp_ref, p_ref, ds_ref):
        n = pl.program_id(0)

        @pl.when(n == 0)
        def _():
            dsink_ref[...] = jnp.zeros_like(dsink_ref)

        mask = _swa_mask(n)
        lo = lax.broadcasted_iota(jnp.int32, (WINDOW, LANES), 1) < 64
        hi = jnp.logical_not(lo)
        lane8 = lax.broadcasted_iota(jnp.int32, (8, LANES), 1)
        for g in range(2):
            gs = slice(LANES * g, LANES * (g + 1))
            kb_ref[g] = jnp.concatenate([kp_ref[:, gs], kc_ref[:, gs]], axis=0)
            vb_ref[g] = jnp.concatenate([vp_ref[:, gs], vc_ref[:, gs]], axis=0)

        def head(h):
            sl = slice(LANES * (h // 2), LANES * (h // 2 + 1))
            hm = lo if h % 2 == 0 else hi
            qp = q_ref[:, sl]
            return hm, sl, jnp.where(hm, qp, jnp.zeros_like(qp)), jnp.where(hm, do_ref[:, sl], 0.0).astype(BF16)

        for h in range(SWA_HEADS):
            _, _, qh, dom = head(h)
            s_ref[h] = _dot_nt(qh, kb_ref[h // 8])
            dp_ref[h] = _dot_nt(dom, vb_ref[h // 8])
        dsink = jnp.zeros((8, LANES), F32)
        for h in range(SWA_HEADS):
            hm, sl, _, _ = head(h)
            lse_h = jnp.max(jnp.where(hm, l_ref[:, sl], -jnp.inf), axis=1, keepdims=True)
            delta = jnp.sum(jnp.where(hm, do_ref[:, sl] * o_ref[:, sl], 0.0), axis=1, keepdims=True)
            p = jnp.exp(jnp.where(mask, s_ref[h] * SWA_SCALE, NEG) - lse_h)
            p_ref[h] = p.astype(BF16)
            ds_ref[h] = (p * (dp_ref[h] - delta) * SWA_SCALE).astype(BF16)
            d_sink = -jnp.sum(jnp.exp(sink_ref[h] - lse_h) * delta, axis=0, keepdims=True)
            dsink = dsink + jnp.where(lane8 == h, d_sink, 0.0)
        dsink_ref[...] += dsink
        for g in range(2):
            gs = slice(LANES * g, LANES * (g + 1))
            dkb = jnp.zeros((2 * WINDOW, LANES), F32)
            dvb = jnp.zeros((2 * WINDOW, LANES), F32)
            for j in range(4 * g, 4 * g + 4):
                dqs = []
                for h in (2 * j, 2 * j + 1):
                    _, _, qh, dom = head(h)
                    dvb = dvb + _dot_tn(p_ref[h], dom)
                    dkb = dkb + _dot_tn(ds_ref[h], qh)
                    dqs.append(_dot(ds_ref[h], kb_ref[g]))
                dq_ref[:, LANES * j:LANES * (j + 1)] = jnp.where(lo, dqs[0], dqs[1])
            dkp_ref[:, gs] = dkb[:WINDOW]
            dkc_ref[:, gs] = dkb[WINDOW:]
            dvp_ref[:, gs] = dvb[:WINDOW]
            dvc_ref[:, gs] = dvb[WINDOW:]

    band = pltpu.VMEM((2, 2 * WINDOW, LANES), BF16)
    return _pcall(body, name="swa_bwd", grid=(nb,),
                  in_specs=[pl.BlockSpec(memory_space=pltpu.SMEM), cur(D), cur(256), prev(256), cur(256), prev(256),
                            cur(D), cur(D), cur(D)],
                  out_specs=[cur(D), cur(256), cur(256), cur(256), cur(256), _full((8, LANES))],
                  out_shape=[SDS((T, D), F32), SDS((T, 256), F32), SDS((T, 256), F32), SDS((T, 256), F32), SDS((T, 256), F32),
                             SDS((8, LANES), F32)],
                  scratch=[band, band, pltpu.VMEM((SWA_HEADS, WINDOW, 2 * WINDOW), F32),
                           pltpu.VMEM((SWA_HEADS, WINDOW, 2 * WINDOW), F32), pltpu.VMEM((SWA_HEADS, WINDOW, 2 * WINDOW), BF16),
                           pltpu.VMEM((SWA_HEADS, WINDOW, 2 * WINDOW), BF16)],
                  sem=("arbitrary",))(sinks, qs, ks, ks, vs, vs, do, o, lse)


def _bwd_qkv(dqm, dkm, dvm, dqs, dkc, dkp, dvc, dvp, z, gq, gkv, wqb, wkn, wv, tab_m, tab_s):
    T = z.shape[0]
    tm = WINDOW
    nb = T // tm
    per = dqm.shape[2] // tm

    def body(dqm_ref, dkm_ref, dvm_ref, dqs_ref, dkc_ref, dkp_ref, dvc_ref, dvp_ref, qa_ref, kva_ref, gq_ref, gkv_ref,
             wqb_ref, wkn_ref, wv_ref, cm_ref, am_ref, bm_ref, cs_ref, as_ref, bs_ref,
             dq_out, dkn_out, dv_out, dsq_ref, drest_ref, dgq_ref, dgkv_ref):
        i = pl.program_id(0)

        @pl.when(i == 0)
        def _():
            dgq_ref[...] = jnp.zeros_like(dgq_ref)
            dgkv_ref[...] = jnp.zeros_like(dgkv_ref)

        cm, am, bm = cm_ref[...], -am_ref[...], -bm_ref[...]
        cs, as_, bs = cs_ref[...], -as_ref[...], -bs_ref[...]
        lane = lax.broadcasted_iota(jnp.int32, (tm, LANES), 1)
        nope = lane < MLA_NOPE
        roped = jnp.logical_and(lane >= MLA_NOPE, lane < MLA_NOPE + MLA_ROPE)
        dkr = jnp.zeros((tm, LANES), F32)
        for h in range(MLA_HEADS):
            sl = slice(LANES * h, LANES * (h + 1))
            dq_out[:, sl] = _rope(dqm_ref[0, sl, :].T, cm, am, bm, MLA_ROPE // 2).astype(BF16)
            dk_h = dkm_ref[0, sl, :].T
            dkn_out[:, sl] = jnp.where(nope, dk_h, 0.0).astype(BF16)
            dkr = dkr + jnp.where(roped, dk_h, 0.0)
        for j in range(D // LANES):
            sl = slice(LANES * j, LANES * (j + 1))
            dv_out[:, sl] = dvm_ref[0, sl, :].T.astype(BF16)
        dqn = _dot_nt(dq_out[...], wqb_ref[...])
        dkvn = _dot_nt(dkn_out[...], wkn_ref[...]) + _dot_nt(dv_out[...], wv_ref[...])
        dqa, dgq = _rms_bwd(dqn, qa_ref[...], gq_ref[...])
        dkva, dgkv = _rms_bwd(dkvn, kva_ref[...], gkv_ref[...])
        dgq_ref[...] += dgq
        dgkv_ref[...] += dgkv
        for j in range(D // LANES):
            sl = slice(LANES * j, LANES * (j + 1))
            dsq_ref[:, sl] = _rope(dqs_ref[:, sl], cs, as_, bs, SWA_HD // 2).astype(BF16)
        keep = (i < nb - 1).astype(F32)
        drest_ref[:, 0:256] = dqa.astype(BF16)
        for j in range(2):
            sl = slice(LANES * j, LANES * (j + 1))
            dk = dkc_ref[:, sl] + keep * dkp_ref[:, sl]
            drest_ref[:, 256 + LANES * j:256 + LANES * (j + 1)] = _rope(dk, cs, as_, bs, SWA_HD // 2).astype(BF16)
        drest_ref[:, 512:768] = (dvc_ref[...] + keep * dvp_ref[...]).astype(BF16)
        drest_ref[:, 768:896] = dkva.astype(BF16)
        drest_ref[:, 896:1024] = _rope(dkr, cm, am, bm, MLA_ROPE // 2).astype(BF16)

    nxt = pl.BlockSpec((tm, 256), lambda i: (jnp.minimum(i + 1, nb - 1), 0))
    tab = [_rows(tm, LANES)] * 6
    return _pcall(body, name="bwd_qkv", grid=(nb,),
                  in_specs=[pl.BlockSpec((1, 2048, tm), lambda i: (i // per, 0, i % per)),
                            pl.BlockSpec((1, 2048, tm), lambda i: (i // per, 0, i % per)),
                            pl.BlockSpec((1, 1024, tm), lambda i: (i // per, 0, i % per)), _rows(tm, 1024), _rows(tm, 256), nxt,
                            _rows(tm, 256), nxt, _rows(tm, 256, 12), _rows(tm, 128, 30), _full((1, Q_LORA)), _full((1, KV_LORA)),
                            _full((Q_LORA, 2048)), _full((KV_LORA, 2048)), _full((KV_LORA, 1024))] + tab,
                  out_specs=[_rows(tm, 2048), _rows(tm, 2048), _rows(tm, 1024), _rows(tm, 1024), _rows(tm, 1024),
                             _full((1, Q_LORA)), _full((1, KV_LORA))],
                  out_shape=[SDS((T, 2048), BF16), SDS((T, 2048), BF16), SDS((T, 1024), BF16), SDS((T, 1024), BF16),
                             SDS((T, 1024), BF16), SDS((1, Q_LORA), F32), SDS((1, KV_LORA), F32)],
                  sem=("arbitrary",))(dqm, dkm, dvm, dqs, dkc, dkp, dvc, dvp, z, z, gq, gkv, wqb, wkn, wv, *tab_m, *tab_s)


def _bwd_in(dsq, dga, dgb, drest, w_in_p, x, g1, dx1, tm):
    T = x.shape[0]

    def body(a_ref, b_ref, c_ref, d_ref, w_ref, x_ref, g_ref, dx1_ref, dx_ref, dg_ref):
        @pl.when(pl.program_id(0) == 0)
        def _():
            dg_ref[...] = jnp.zeros_like(dg_ref)

        dh = (_dot_nt(a_ref[...], w_ref[:, 0:1024]) + _dot_nt(b_ref[...], w_ref[:, 1024:2048])
              + _dot_nt(c_ref[...], w_ref[:, 2048:3072]) + _dot_nt(d_ref[...], w_ref[:, 3072:4096]))
        dx, dg = _rms_bwd(dh, x_ref[...], g_ref[...])
        dg_ref[...] += dg
        dx_ref[...] = dx1_ref[...] + dx

    r = _rows(tm, D)
    return _pcall(body, name="bwd_in", grid=(T // tm,),
                  in_specs=[r, r, r, r, _full((D, NZ)), r, _full((1, D)), r],
                  out_specs=[r, _full((1, D))], out_shape=[SDS((T, D), F32), SDS((1, D), F32)],
                  sem=("arbitrary",))(dsq, dga, dgb, drest, w_in_p, x, g1, dx1)


def _wgrad(a, g, name, into=None):
    T, K = a.shape
    N = g.shape[1]
    tk, tn, tt = min(K, 1024), min(N, 1024), min(T, 1024)
    if into is not None:
        buf, weight = into
        _, row0, lane0 = PACK_AT[weight]
        shard = {n: (r, c) for n, r, c in BIG}[weight]
        assert lane0 == 0 and shard[1] == D and tk % shard[0] == 0
        per_step = tk // shard[0]
    assert K % tk == 0 and N % tn == 0 and T % tt == 0, (a.shape, g.shape)
    steps = T // tt

    def body(a_ref, g_ref, *rest):
        o_ref, acc_ref = rest[-2:]
        t = pl.program_id(2)

        @pl.when(t == 0)
        def _():
            acc_ref[...] = jnp.zeros_like(acc_ref)

        acc_ref[...] += _dot_tn(a_ref[...].astype(BF16), g_ref[...].astype(BF16))

        @pl.when(t == steps - 1)
        def _():
            o_ref[...] = acc_ref[...].astype(o_ref.dtype).reshape(o_ref.shape)

    in_specs = [pl.BlockSpec((tt, tk), lambda k, n, t: (t, k)), pl.BlockSpec((tt, tn), lambda k, n, t: (t, n))]
    if into is None:
        return _pcall(body, name=name, grid=(K // tk, N // tn, steps), in_specs=in_specs,
                      out_specs=pl.BlockSpec((tk, tn), lambda k, n, t: (k, n)), out_shape=SDS((K, N), F32),
                      scratch=[pltpu.VMEM((tk, tn), F32)], sem=("parallel", "parallel", "arbitrary"))(a, g)
    assert row0 % shard[0] == 0 and (K // tk) * (N // tn) * per_step == N_CHIPS
    return _pcall(body, name=name, grid=(K // tk, N // tn, steps), in_specs=in_specs + [ANY],
                  out_specs=pl.BlockSpec((per_step, shard[0], tn), lambda k, n, t: (k + n, row0 // shard[0], 0)),
                  out_shape=SDS(buf.shape, buf.dtype),
                  scratch=[pltpu.VMEM((tk, tn), F32)], sem=("parallel", "parallel", "arbitrary"), aliases={2: 0})(a, g, buf)


def _adamw(w, packed_g, m, v, name):
    _, R, C = w.shape
    _, row0, lane0 = PACK_AT[name]
    tr = min(R, 256 if row0 % 256 == 0 else 128)
    assert row0 % tr == 0 and R % tr == 0

    def body(w_ref, g_ref, m_ref, v_ref, go_ref, d_ref, m2_ref, v2_ref):
        g_ = g_ref[:, lane0:lane0 + C]
        go_ref[0] = g_
        m2 = ADAM_B1 * m_ref[0] + (1.0 - ADAM_B1) * g_
        v2 = ADAM_B2 * v_ref[0] + (1.0 - ADAM_B2) * jnp.square(g_)
        m_hat = m2 / (1.0 - ADAM_B1 ** ADAM_STEP)
        v_hat = v2 / (1.0 - ADAM_B2 ** ADAM_STEP)
        d_ref[0] = -ADAM_LR * (m_hat / (jnp.sqrt(v_hat) + ADAM_EPS) + ADAM_WD * w_ref[0])
        m2_ref[0] = m2
        v2_ref[0] = v2

    r = pl.BlockSpec((1, tr, C), lambda i: (0, i, 0))
    return _pcall(body, name="adamw_" + name, grid=(R // tr,),
                  in_specs=[r, pl.BlockSpec((tr, D), lambda i: (row0 // tr + i, 0)), r, r], out_specs=[r] * 4,
                  out_shape=[SDS((1, R, C), F32)] * 4, sem=("parallel",))(w, packed_g, m, v)


def _adamw_small(w, parts, m, v):
    def body(w_ref, p_ref, m_ref, v_ref, g_ref, d_ref, m2_ref, v2_ref):
        g_ = p_ref[0]
        for k in range(1, N_DEV):
            g_ = g_ + p_ref[k]
        g_ref[...] = g_
        m2 = ADAM_B1 * m_ref[...] + (1.0 - ADAM_B1) * g_
        v2 = ADAM_B2 * v_ref[...] + (1.0 - ADAM_B2) * jnp.square(g_)
        m_hat = m2 / (1.0 - ADAM_B1 ** ADAM_STEP)
        v_hat = v2 / (1.0 - ADAM_B2 ** ADAM_STEP)
        d_ref[...] = -ADAM_LR * (m_hat / (jnp.sqrt(v_hat) + ADAM_EPS) + ADAM_WD * w_ref[...])
        m2_ref[...] = m2
        v2_ref[...] = v2

    s = _full((8, D))
    return _pcall(body, name="adamw_small", grid=(1,), in_specs=[s, _full((N_DEV, 8, D)), s, s], out_specs=[s] * 4,
                  out_shape=[SDS((8, D), F32)] * 4, sem=("arbitrary",))(w, parts, m, v)


ANY = pl.BlockSpec(memory_space=pl.ANY)


def _place():
    x, y, c = lax.axis_index("x"), lax.axis_index("y"), lax.axis_index("c")
    chips = [(1 - x, y), (x, 1 - y), (1 - x, 1 - y)]
    return x, y, c, chips


def _all_gather(wpk):
    rows = wpk.shape[0]
    HALF = rows // 2
    assert HALF % 16 == 0

    def body(in_ref, out_ref, send_sems, recv_sems):
        x, y, c, chips = _place()
        half = pl.ds(pl.multiple_of(c * HALF, 16), HALF)
        other = pl.ds(pl.multiple_of((1 - c) * HALF, 16), HALF)

        def copy(k, src, dst, to):
            return pltpu.make_async_remote_copy(src_ref=src, dst_ref=dst, send_sem=send_sems.at[k], recv_sem=recv_sems.at[k],
                                                device_id=to, device_id_type=MESH)

        first = [copy(k, in_ref.at[half], out_ref.at[2 * x + y, half], (cx, cy, c)) for k, (cx, cy) in enumerate(chips)]
        for cp in first:
            cp.start()
        passed = []
        for k, (cx, cy) in enumerate(chips):
            slot = out_ref.at[2 * cx + cy, half]
            copy(k, slot, slot, (x, y, c)).wait_recv()
            fwd = copy(3 + k, slot, slot, (x, y, 1 - c))
            fwd.start()
            passed.append(fwd)
        for k, (cx, cy) in enumerate(chips):
            slot = out_ref.at[2 * cx + cy, other]
            copy(3 + k, slot, slot, (x, y, c)).wait_recv()
        for cp in first + passed:
            cp.wait_send()

    return _pcall(body, name="all_gather_weights", in_specs=[ANY], out_specs=ANY,
                  out_shape=SDS((N_CHIPS, rows, D), BF16),
                  scratch=[pltpu.SemaphoreType.DMA((6,)), pltpu.SemaphoreType.DMA((6,))])(wpk)


HBM = pl.BlockSpec(memory_space=pltpu.HBM)
SEM = pl.BlockSpec(memory_space=pltpu.SEMAPHORE)
DATAFLOW = pltpu.SideEffectType.DATAFLOW_SIDE_EFFECTING


def _in_hbm(a):
    return pltpu.with_memory_space_constraint(a, pltpu.HBM)


def _gather_late_start(wpk, after):
    rows = wpk.shape[0]

    def body(in_ref, land_ref, after_ref, send_sems, recv_sems, in_thru, land_thru, token):
        x, y, c, chips = _place()
        for k, (cx, cy) in enumerate(chips):
            pltpu.make_async_remote_copy(src_ref=in_ref, dst_ref=land_ref.at[2 * x + y], send_sem=send_sems.at[k],
                                         recv_sem=recv_sems.at[k], device_id=(cx, cy, c), device_id_type=MESH).start()
        token[...] = jnp.zeros_like(token)

    return pl.pallas_call(
        body, name="gather_late_start",
        out_shape=(pltpu.SemaphoreType.DMA((3,)), pltpu.SemaphoreType.DMA((3,)), pltpu.HBM(wpk.shape, wpk.dtype),
                   pltpu.HBM((N_CHIPS, rows, D), wpk.dtype), SDS((8, LANES), F32)),
        in_specs=(HBM, HBM, ANY), out_specs=(SEM, SEM, HBM, HBM, pl.BlockSpec(memory_space=pltpu.VMEM)),
        input_output_aliases={0: 2, 1: 3}, compiler_params=pltpu.CompilerParams(has_side_effects=DATAFLOW),
    )(_in_hbm(wpk), _in_hbm(lax.empty((N_CHIPS, rows, D), wpk.dtype)), after)


def _gather_late_wait(send_sems, recv_sems, in_thru, land_thru, after):
    def body(in_ref, land_ref, send_sems, recv_sems, after_ref, after2_ref, in_dead, got_ref):
        x, y, c, chips = _place()
        for k, (cx, cy) in enumerate(chips):
            cp = pltpu.make_async_remote_copy(src_ref=in_ref, dst_ref=land_ref.at[2 * cx + cy], send_sem=send_sems.at[k],
                                              recv_sem=recv_sems.at[k], device_id=(cx, cy, c), device_id_type=MESH)
            cp.wait_send()
            cp.wait_recv()

    return pl.pallas_call(
        body, name="gather_late_wait",
        out_shape=(pltpu.HBM(in_thru.shape, in_thru.dtype), pltpu.HBM(land_thru.shape, land_thru.dtype)),
        in_specs=(HBM, HBM, SEM, SEM, ANY, ANY), out_specs=(HBM, HBM), input_output_aliases={0: 0, 1: 1},
        compiler_params=pltpu.CompilerParams(has_side_effects=DATAFLOW),
    )(in_thru, land_thru, send_sems, recv_sems, *after)[1]


def _rs_sibling(gpk):
    HALF = gpk.shape[1] // 2

    def body(in_ref, out_ref, send_sem, recv_sem):
        x, y, c, _ = _place()
        theirs = pl.ds(pl.multiple_of((1 - c) * HALF, 8), HALF)
        cp = pltpu.make_async_remote_copy(src_ref=in_ref.at[:, theirs], dst_ref=out_ref, send_sem=send_sem, recv_sem=recv_sem,
                                          device_id=(x, y, 1 - c), device_id_type=MESH)
        cp.start()
        cp.wait()

    return _pcall(body, name="rs_sibling", in_specs=[ANY], out_specs=ANY, out_shape=SDS((N_CHIPS, HALF, D), F32),
                  scratch=[pltpu.SemaphoreType.DMA, pltpu.SemaphoreType.DMA])(gpk)


def _rs_add_sibling(cidx, gpk, got):
    HALF = got.shape[1]
    th = HALF // 4
    nh = HALF // th
    assert th % 16 == 0

    def body(c_ref, a_ref, b_ref, o_ref):
        o_ref[...] = (a_ref[...] + b_ref[...]).astype(BF16)

    gs = pltpu.PrefetchScalarGridSpec(
        num_scalar_prefetch=1, grid=(N_CHIPS, nh),
        in_specs=[pl.BlockSpec((1, th, D), lambda j, i, c: (j, c[0] * nh + i, 0)), pl.BlockSpec((1, th, D), lambda j, i, c: (j, i, 0))],
        out_specs=pl.BlockSpec((1, th, D), lambda j, i, c: (j, i, 0)))
    return pl.pallas_call(body, name="rs_add_sibling", grid_spec=gs, out_shape=SDS((N_CHIPS, HALF, D), BF16),
                          compiler_params=pltpu.CompilerParams(dimension_semantics=("parallel", "parallel"),
                                                               vmem_limit_bytes=48 << 20))(cidx, gpk, got)


def _rs_chips_start(part, small, after):
    def body(p_ref, s_ref, land_ref, sland_ref, after_ref, send_sems, recv_sems, p_thru, s_thru, land_thru, sland_thru, token):
        x, y, c, chips = _place()
        for k, (cx, cy) in enumerate(chips):
            pltpu.make_async_remote_copy(src_ref=p_ref.at[2 * cx + cy], dst_ref=land_ref.at[2 * x + y], send_sem=send_sems.at[k],
                                         recv_sem=recv_sems.at[k], device_id=(cx, cy, c), device_id_type=MESH).start()
        peers = [(x, y, 1 - c)] + [(cx, cy, c) for cx, cy in chips] + [(cx, cy, 1 - c) for cx, cy in chips]
        for k, to in enumerate(peers):
            pltpu.make_async_remote_copy(src_ref=s_ref, dst_ref=sland_ref.at[4 * x + 2 * y + c], send_sem=send_sems.at[3 + k],
                                         recv_sem=recv_sems.at[3 + k], device_id=to, device_id_type=MESH).start()
        token[...] = jnp.zeros_like(token)

    return pl.pallas_call(
        body, name="rs_chips_start",
        out_shape=(pltpu.SemaphoreType.DMA((10,)), pltpu.SemaphoreType.DMA((10,)), pltpu.HBM(part.shape, part.dtype),
                   pltpu.HBM(small.shape, small.dtype), pltpu.HBM(part.shape, part.dtype), pltpu.HBM((N_DEV, 8, D), F32),
                   SDS((8, LANES), F32)),
        in_specs=(HBM, HBM, HBM, HBM, ANY), out_specs=(SEM, SEM, HBM, HBM, HBM, HBM, pl.BlockSpec(memory_space=pltpu.VMEM)),
        input_output_aliases={0: 2, 1: 3, 2: 4, 3: 5}, compiler_params=pltpu.CompilerParams(has_side_effects=DATAFLOW),
    )(_in_hbm(part), _in_hbm(small), _in_hbm(lax.empty(part.shape, part.dtype)), _in_hbm(lax.empty((N_DEV, 8, D), F32)), after)


def _rs_chips_wait(send_sems, recv_sems, p_thru, s_thru, land_thru, sland_thru, after):
    def body(p_ref, s_ref, land_ref, sland_ref, send_sems, recv_sems, *after_and_outputs):
        x, y, c, chips = _place()
        for k, (cx, cy) in enumerate(chips):
            cp = pltpu.make_async_remote_copy(src_ref=p_ref.at[0], dst_ref=land_ref.at[2 * cx + cy], send_sem=send_sems.at[k],
                                              recv_sem=recv_sems.at[k], device_id=(cx, cy, c), device_id_type=MESH)
            cp.wait_send()
            cp.wait_recv()
        peers = [(x, y, 1 - c)] + [(cx, cy, c) for cx, cy in chips] + [(cx, cy, 1 - c) for cx, cy in chips]
        for k, (px, py, pc) in enumerate(peers):
            cp = pltpu.make_async_remote_copy(src_ref=s_ref, dst_ref=sland_ref.at[4 * px + 2 * py + pc], send_sem=send_sems.at[3 + k],
                                              recv_sem=recv_sems.at[3 + k], device_id=(px, py, pc), device_id_type=MESH)
            cp.wait_send()
            cp.wait_recv()

    hbm = lambda a: pltpu.HBM(a.shape, a.dtype)
    outs = pl.pallas_call(
        body, name="rs_chips_wait", out_shape=(hbm(p_thru), hbm(s_thru), hbm(land_thru), hbm(sland_thru)),
        in_specs=(HBM, HBM, HBM, HBM, SEM, SEM) + (ANY,) * len(after), out_specs=(HBM, HBM, HBM, HBM),
        input_output_aliases={0: 0, 1: 1, 2: 2, 3: 3}, compiler_params=pltpu.CompilerParams(has_side_effects=DATAFLOW),
    )(p_thru, s_thru, land_thru, sland_thru, send_sems, recv_sems, *after)
    return outs[0], outs[2], outs[3]


def _rs_add_chips(qidx, part, parts):
    HALF = part.shape[1]
    th = HALF // 4
    assert th % 16 == 0

    def body(q_ref, own_ref, p_ref, o_ref):
        for me in range(N_CHIPS):
            @pl.when(q_ref[0] == me)
            def _(me=me):
                t = [(own_ref[0] if j == me else p_ref[j]).astype(F32) for j in range(N_CHIPS)]
                o_ref[...] = ((t[0] + t[1]) + t[2]) + t[3]

    gs = pltpu.PrefetchScalarGridSpec(
        num_scalar_prefetch=1, grid=(HALF // th,),
        in_specs=[pl.BlockSpec((1, th, D), lambda i, q: (q[0], i, 0)), pl.BlockSpec((N_CHIPS, th, D), lambda i, q: (0, i, 0))],
        out_specs=pl.BlockSpec((th, D), lambda i, q: (i, 0)))
    return pl.pallas_call(body, name="rs_add_chips", grid_spec=gs, out_shape=SDS((HALF, D), F32),
                          compiler_params=pltpu.CompilerParams(dimension_semantics=("parallel",),
                                                               vmem_limit_bytes=48 << 20))(qidx, part, parts)


def _rs_join(mine, core, name):
    def body(in_ref, out_ref, send_sem, recv_sem):
        x, y, c, _ = _place()
        cp = pltpu.make_async_remote_copy(src_ref=in_ref, dst_ref=out_ref, send_sem=send_sem, recv_sem=recv_sem,
                                          device_id=(x, y, 1 - c), device_id_type=MESH)
        cp.start()
        cp.wait()

    theirs = _pcall(body, name=name, in_specs=[ANY], out_specs=ANY, out_shape=SDS(mine.shape, F32),
                    scratch=[pltpu.SemaphoreType.DMA, pltpu.SemaphoreType.DMA])(mine)
    return jnp.where(core == 0, jnp.concatenate([mine, theirs]), jnp.concatenate([theirs, mine]))


def _reduce_late_start(gpk, after):
    rows = gpk.shape[1]
    HALF = rows // 2
    assert HALF % 16 == 0

    def body(in_ref, land_ref, after_ref, send_sems, recv_sems, in_thru, land_thru, token):
        x, y, c, chips = _place()
        me = 4 * x + 2 * y + c
        peers = [(x, y, 1 - c)] + [(cx, cy, c) for cx, cy in chips] + [(cx, cy, 1 - c) for cx, cy in chips]
        for k, (px, py, pc) in enumerate(peers):
            src = in_ref.at[2 * px + py, pl.ds(pl.multiple_of(pc * HALF, 16), HALF)]
            pltpu.make_async_remote_copy(src_ref=src, dst_ref=land_ref.at[me], send_sem=send_sems.at[k], recv_sem=recv_sems.at[k],
                                         device_id=(px, py, pc), device_id_type=MESH).start()
        token[...] = jnp.zeros_like(token)

    return pl.pallas_call(
        body, name="reduce_late_start",
        out_shape=(pltpu.SemaphoreType.DMA((7,)), pltpu.SemaphoreType.DMA((7,)), pltpu.HBM(gpk.shape, gpk.dtype),
                   pltpu.HBM((N_DEV, HALF, D), gpk.dtype), SDS((8, LANES), F32)),
        in_specs=(HBM, HBM, ANY), out_specs=(SEM, SEM, HBM, HBM, pl.BlockSpec(memory_space=pltpu.VMEM)),
        input_output_aliases={0: 2, 1: 3}, compiler_params=pltpu.CompilerParams(has_side_effects=DATAFLOW),
    )(_in_hbm(gpk), _in_hbm(lax.empty((N_DEV, HALF, D), gpk.dtype)), after)


def _reduce_late_wait(send_sems, recv_sems, in_thru, land_thru, after):
    def body(in_ref, land_ref, send_sems, recv_sems, after_ref, in_out, got_ref):
        x, y, c, chips = _place()
        peers = [(x, y, 1 - c)] + [(cx, cy, c) for cx, cy in chips] + [(cx, cy, 1 - c) for cx, cy in chips]
        for k, (px, py, pc) in enumerate(peers):
            cp = pltpu.make_async_remote_copy(src_ref=land_ref.at[0], dst_ref=land_ref.at[4 * px + 2 * py + pc],
                                              send_sem=send_sems.at[k], recv_sem=recv_sems.at[k],
                                              device_id=(px, py, pc), device_id_type=MESH)
            cp.wait_send()
            cp.wait_recv()

    return pl.pallas_call(
        body, name="reduce_late_wait",
        out_shape=(pltpu.HBM(in_thru.shape, in_thru.dtype), pltpu.HBM(land_thru.shape, land_thru.dtype)),
        in_specs=(HBM, HBM, SEM, SEM, ANY), out_specs=(HBM, HBM), input_output_aliases={0: 0, 1: 1},
        compiler_params=pltpu.CompilerParams(has_side_effects=DATAFLOW),
    )(in_thru, land_thru, send_sems, recv_sems, after)


def _reduce_late_add(didx, gpk, parts):
    HALF = parts.shape[1]
    th = HALF // 4
    nh = HALF // th
    assert th % 16 == 0

    def body(d_ref, own_ref, p_ref, o_ref):
        for me in range(N_DEV):
            @pl.when(d_ref[0] == me)
            def _(me=me):
                t = [(own_ref[0] if j == me else p_ref[j]).astype(F32) for j in range(N_DEV)]
                o_ref[...] = ((((((t[0] + t[1]) + t[2]) + t[3]) + t[4]) + t[5]) + t[6]) + t[7]

    gs = pltpu.PrefetchScalarGridSpec(
        num_scalar_prefetch=1, grid=(nh,),
        in_specs=[pl.BlockSpec((1, th, D), lambda i, d: (d[1], d[2] * nh + i, 0)), pl.BlockSpec((N_DEV, th, D), lambda i, d: (0, i, 0))],
        out_specs=pl.BlockSpec((th, D), lambda i, d: (i, 0)))
    return pl.pallas_call(body, name="reduce_late_add", grid_spec=gs, out_shape=SDS((HALF, D), F32),
                          compiler_params=pltpu.CompilerParams(dimension_semantics=("parallel",),
                                                               vmem_limit_bytes=48 << 20))(didx, gpk, parts)


def _pack_early(b, dtype):
    lanes = lambda a: jnp.pad(a.astype(dtype), ((0, 0), (0, D - a.shape[1])))
    pair = jnp.concatenate([b["w_q_b"].astype(dtype), b["w_ple"].astype(dtype), jnp.zeros((256, D - 640), dtype)], axis=1)
    return jnp.concatenate([lanes(b["w_in"]), pair, lanes(b["w_kv_b"])], axis=0)


def _pack_late(b, dtype):
    return jnp.concatenate([b[n].astype(dtype) for n in ("w_mla_up", "w_swa_up", "w_out", "w_ple_gate", "w_mlp_up", "w_mlp_down")],
                           axis=0)


def _unpack_shards(pk, which):
    return {n: pk[PACK_AT[n][1]:PACK_AT[n][1] + r, PACK_AT[n][2]:PACK_AT[n][2] + c] for n, r, c in BIG if PACK_AT[n][0] == which}


def _full_weights(gathered, own, chip, which):
    own_b = _unpack_shards(own, which)
    per_chip = [{n: jnp.where(chip == j, own_b[n], blk) for n, blk in _unpack_shards(gathered[j], which).items()}
                for j in range(N_CHIPS)]
    out = {}
    for n in own_b:
        shards = [pc[n] for pc in per_chip]
        if n == "w_in":
            out["w_in_p"] = _w_in_internal(shards)
        else:
            out[n] = jnp.concatenate(shards, axis=1 if n in COL_SHARDED else 0)
    return out


def _split_full_grads(grads, pack, dtype):
    shard = {n: (r, c) for n, r, c in BIG}
    chunks = []
    for j in range(N_CHIPS):
        blocks = {}
        for n, g in grads.items():
            if n == "w_in_p":
                blocks["w_in"] = _w_in_grad_shard(g, j)
                continue
            r, c = shard[n]
            blocks[n] = g[:, j * c:(j + 1) * c] if n in COL_SHARDED else g[j * r:(j + 1) * r]
        chunks.append(pack(blocks, dtype))
    return jnp.stack(chunks)


W_IN_SHARD = 936
W_IN_SEGMENTS = ((0, 256, (3072,)), (256, 384, (3840,)), (384, 416, (4032,)), (416, 1440, (0,)), (1440, 1504, (3328, 3392)),
                 (1504, 1568, (3456, 3520)), (1568, 1632, (3584, 3648)), (1632, 1696, (3712, 3776)), (1696, 3744, (1024,)))


def _w_in_internal(shards):
    def cols(a, b):
        out = []
        for j, s in enumerate(shards):
            lo, hi = max(a, W_IN_SHARD * j), min(b, W_IN_SHARD * (j + 1))
            if lo < hi:
                out.append(s[:, lo - W_IN_SHARD * j:hi - W_IN_SHARD * j])
        return out

    pieces = {}
    for a, b, places in W_IN_SEGMENTS:
        for at in places:
            pieces[at] = cols(a, b)
    zeros = lambda n: [jnp.zeros((D, n), shards[0].dtype)]
    pieces[3968] = zeros(64)
    pieces[4064] = zeros(32)
    return jnp.concatenate([piece for at in sorted(pieces) for piece in pieces[at]], axis=1)


def _w_in_grad_shard(g, j):
    def internal(a, b):
        out = []
        while a < b:
            end = min(b, (a // D + 1) * D)
            out.append(g[a // D][:, a % D:a % D + end - a])
            a = end
        return out

    out = []
    for a, b, places in W_IN_SEGMENTS:
        lo, hi = max(a, W_IN_SHARD * j), min(b, W_IN_SHARD * (j + 1))
        if lo < hi:
            parts = [internal(at + lo - a, at + hi - a) for at in places]
            if len(parts) == 1:
                out += parts[0]
            else:
                assert len(parts[0]) == len(parts[1]) == 1
                out.append(parts[0][0] + parts[1][0])
    return jnp.concatenate(out, axis=1)


def _local_step(x, p, tgt, w, small, late_weights, late_grads_out):
    T = x.shape[0]
    tm = 256
    tb = 256
    w_in_p = w["w_in_p"]
    wqb = jnp.pad(w["w_q_b"].reshape(Q_LORA, MLA_HEADS, 96), ((0, 0), (0, 0), (0, 32))).reshape(Q_LORA, 2048)
    wkv = w["w_kv_b"].reshape(KV_LORA, MLA_HEADS, 128)
    wkn = jnp.pad(wkv[:, :, :64], ((0, 0), (0, 0), (0, 64))).reshape(KV_LORA, 2048)
    wv = wkv[:, :, 64:].reshape(KV_LORA, 1024)
    tab_m = _rope_tables(T, "mla")
    tab_s = _rope_tables(T, "swa")
    g1, gq, gkv, sinks = small["g_mix_pre"], small["g_q_a"], small["g_kv_a"], small["sinks"]
    g2, g3, g4, g5 = small["g_mix_post"], small["g_mlp_pre"], small["g_mlp_post"], small["g_ple"]
    sink_vec = sinks.reshape(SWA_HEADS)

    z, h1 = _fwd_in(x, g1, w_in_p, tm)
    qn, kvn, qm, km, vm, qt, kt, vt, qs, ks, vs = _fwd_qkv(z, gq, gkv, wqb, wkn, wv, tab_m, tab_s, tb)
    om, lse_m = _mla_fwd(qm, km, vt, tb)
    os_, lse_s = _swa_fwd(sink_vec, qs, ks, vs)
    w = {**w, **late_weights((om, os_))}
    y, yo, au, bu, x1 = _fwd_mix(om, os_, z, x, w["w_mla_up"], w["w_swa_up"], w["w_out"], g2, tm)
    h2, u = _fwd_mlp_up(x1, g3, w["w_mlp_up"], tm)
    d, x2 = _fwd_mlp_down(u, w["w_mlp_down"], x1, g4, tm)
    loss, dx2, dgt, de0, dg5 = _ple_fwd_bwd(p, x2, tgt, w["w_ple"], g5, w["w_ple_gate"], tm)

    dd, da, dg4 = _bwd_mlp_down(dx2, d, g4, w["w_mlp_down"], u, tm)
    dx1, dg3 = _bwd_mlp_up(da, w["w_mlp_up"], x1, g3, dx2, tm)
    dyo, dg2, dau, dbu, dga, dgb, dos, delta_m, dom_t = _bwd_mix(dx1, yo, g2, w["w_out"], z, au, bu, w["w_mla_up"],
                                                                w["w_swa_up"], om, tb)
    gpk_late = lax.empty((N_CHIPS, PACK_ROWS["late"], D), BF16)
    for weight, a_, g_ in (("w_mla_up", om, dau), ("w_swa_up", os_, dbu), ("w_out", y, dyo), ("w_ple_gate", x2, dgt),
                           ("w_mlp_up", h2, da), ("w_mlp_down", u, dd)):
        gpk_late = _wgrad(a_, g_, "wgrad_" + weight[2:], into=(gpk_late, weight))
    token = late_grads_out(gpk_late)
    delta_m = delta_m + token[0, 0]
    dqm, dkm, dvm = _mla_bwd(qm, qt, km, kt, vm, dom_t, lse_m, delta_m, tb)
    dqs, dkc, dkp, dvc, dvp, dsink = _swa_bwd(sink_vec, qs, ks, vs, dos, os_, lse_s)
    dqb, dknb, dvb, dsq, drest, dgq, dgkv = _bwd_qkv(dqm, dkm, dvm, dqs, dkc, dkp, dvc, dvp, z, gq, gkv, wqb, wkn, wv,
                                                      tab_m, tab_s)
    gx, dg1 = _bwd_in(dsq, dga, dgb, drest, w_in_p, x, g1, dx1, tm)

    g_in_p = [_wgrad(h1, dsq, "wgrad_in_sq"), _wgrad(h1, dga, "wgrad_in_ga"), _wgrad(h1, dgb, "wgrad_in_gb"),
              _wgrad(h1, drest, "wgrad_in_rest")]
    g_qb_p = _wgrad(qn, dqb, "wgrad_q_b")
    g_kn_p = _wgrad(kvn, dknb, "wgrad_kv_b_nope")
    g_v_p = _wgrad(kvn, dvb, "wgrad_kv_b_v")
    grads = {
        "w_in_p": g_in_p,
        "w_q_b": g_qb_p.reshape(Q_LORA, MLA_HEADS, 128)[:, :, :96].reshape(Q_LORA, 1536),
        "w_kv_b": jnp.concatenate([g_kn_p.reshape(KV_LORA, MLA_HEADS, 128)[:, :, :64], g_v_p.reshape(KV_LORA, MLA_HEADS, 64)],
                                  axis=2).reshape(KV_LORA, 2048),
        "w_ple": _wgrad(p, de0, "wgrad_ple"),
    }
    small_grads = {"g_mix_pre": dg1, "g_q_a": dgq, "g_kv_a": dgkv, "sinks": dsink[0:1, 0:SWA_HEADS], "g_mix_post": dg2,
                   "g_mlp_pre": dg3, "g_mlp_post": dg4, "g_ple": dg5}
    return loss, gx, grads, small_grads


def _pack_small(vals, fill, scalar=None):
    wide = [vals[n] for n, k in SMALL if k == D]
    narrow = [vals[n] for n, k in SMALL if k != D]
    used = sum(k for _, k in SMALL if k != D)
    last = jnp.concatenate(narrow + [jnp.full((1, D - used), fill, F32)], axis=1)
    rest = jnp.full((2, D), fill, F32)
    if scalar is not None:
        rest = jnp.concatenate([jnp.concatenate([scalar, rest[0:1, 1:]], axis=1), rest[1:2]], axis=0)
    return jnp.concatenate(wide + [last, rest], axis=0)


def _unpack_small(pk):
    out, row, off = {}, 0, 0
    for n, k in SMALL:
        if k == D:
            out[n] = pk[row:row + 1]
            row += 1
    for n, k in SMALL:
        if k != D:
            out[n] = pk[5:6, off:off + k]
            off += k
    return out


def kernel(x, p, g_mix_pre, w_in, g_q_a, w_q_b, g_kv_a, w_kv_b, sinks, w_mla_up, w_swa_up, w_out, g_mix_post, g_mlp_pre, w_mlp_up, w_mlp_down, g_mlp_post, w_ple, g_ple, w_ple_gate, loss_target, m_g_mix_pre, m_w_in, m_g_q_a, m_w_q_b, m_g_kv_a, m_w_kv_b, m_sinks, m_w_mla_up, m_w_swa_up, m_w_out, m_g_mix_post, m_g_mlp_pre, m_w_mlp_up, m_w_mlp_down, m_g_mlp_post, m_w_ple, m_g_ple, m_w_ple_gate, v_g_mix_pre, v_w_in, v_g_q_a, v_w_q_b, v_g_kv_a, v_w_kv_b, v_sinks, v_w_mla_up, v_w_swa_up, v_w_out, v_g_mix_post, v_g_mlp_pre, v_w_mlp_up, v_w_mlp_down, v_g_mlp_post, v_w_ple, v_g_ple, v_w_ple_gate):
    given = dict(locals())
    big_w = {n: given[n][0] for n, _, _ in BIG}
    small_w = {n: given[n] for n, _ in SMALL}
    small_m = {n: given["m_" + n] for n, _ in SMALL}
    small_v = {n: given["v_" + n] for n, _ in SMALL}

    core = lax.axis_index("c")
    chip = 2 * lax.axis_index("x") + lax.axis_index("y")
    core_i = core.astype(jnp.int32).reshape(1)
    chip_i = chip.astype(jnp.int32).reshape(1)
    dev_i = jnp.stack([2 * chip + core, chip, core]).astype(jnp.int32)

    own_early = _pack_early(big_w, BF16)
    own_late = _pack_late(big_w, BF16)
    got_early = _all_gather(own_early)
    late_flight = _gather_late_start(own_late, got_early)
    weights = _full_weights(got_early, own_early, chip, "early")
    step_small = {**small_w, "g_mix_pre": small_w["g_mix_pre"] + late_flight[4][0, 0]}

    def late_weights(after):
        return _full_weights(_gather_late_wait(*late_flight[:4], after), own_late, chip, "late")

    flight = {}

    def late_grads_out(gpk_late):
        flight["late"] = _reduce_late_start(gpk_late, dev_i)
        return flight["late"][4]

    loss_blk, gx, grads, small_grads = _local_step(x[0], p[0, 0], loss_target[0], weights, step_small, late_weights,
                                                   late_grads_out)

    gpk = _split_full_grads(grads, _pack_early, F32)
    got = _rs_sibling(gpk)
    part = _rs_add_sibling(core_i, gpk, got)
    small_own = _pack_small(small_grads, 0.0, loss_blk[0:1, 0:1])
    early_flight = _rs_chips_start(part, small_own, dev_i)

    out_g, out_d, out_m, out_v = {}, {}, {}, {}
    gpk_late, parts_late = _reduce_late_wait(*flight["late"][:4], early_flight[6])
    joined_late = _rs_join(_reduce_late_add(dev_i, gpk_late, parts_late), core, "rs_join_late")
    for n, _, _ in BIG:
        if PACK_AT[n][0] == "late":
            out_g[n], out_d[n], out_m[n], out_v[n] = _adamw(given[n], joined_late, given["m_" + n], given["v_" + n], n)

    part, parts, small_parts = _rs_chips_wait(*early_flight[:6], [out_d[n] for n in out_d])
    joined_early = _rs_join(_rs_add_chips(chip_i, part, parts), core, "rs_join_early")
    for n, _, _ in BIG:
        if PACK_AT[n][0] == "early":
            out_g[n], out_d[n], out_m[n], out_v[n] = _adamw(given[n], joined_early, given["m_" + n], given["v_" + n], n)

    mine = (lax.broadcasted_iota(jnp.int32, (N_DEV, 1, 1), 0) == dev_i[0])
    g_small_pk, d_small_pk, m_small_pk, v_small_pk = _adamw_small(
        _pack_small(small_w, 0.0), jnp.where(mine, small_own[None], small_parts), _pack_small(small_m, 0.0),
        _pack_small(small_v, 1.0))
    loss = g_small_pk[6, 0]
    for out, pk in ((out_g, g_small_pk), (out_d, d_small_pk), (out_m, m_small_pk), (out_v, v_small_pk)):
        out.update(_unpack_small(pk))
    order = ["g_mix_pre", "w_in", "g_q_a", "w_q_b", "g_kv_a", "w_kv_b", "sinks", "w_mla_up", "w_swa_up", "w_out", "g_mix_post",
             "g_mlp_pre", "w_mlp_up", "w_mlp_down", "g_mlp_post", "w_ple", "g_ple", "w_ple_gate"]
    return (loss, gx[None], *[out_g[n] for n in order], *[out_d[n] for n in order], *[out_m[n] for n in order],
            *[out_v[n] for n in order])
```

```python
import math

import jax
import jax.numpy as jnp
from jax import lax
from jax.experimental import pallas as pl
from jax.experimental.pallas import tpu as pltpu

F32 = jnp.float32
BF16 = jnp.bfloat16
SDS = jax.ShapeDtypeStruct

D = 1024
D_FF = 4096
PLE = 256
Q_LORA = 256
KV_LORA = 128
MLA_HEADS = 16
MLA_NOPE = 64
MLA_ROPE = 32
SWA_HEADS = 16
SWA_HD = 64
WINDOW = 128
ROPE_THETA = 10000.0
EPS = 1e-6
NEG = -1e30
NZ = 4096
MLA_SCALE = (MLA_NOPE + MLA_ROPE) ** -0.5
LOG2_E = math.log2(math.e)
MLA_LOG2_SCALE = MLA_SCALE * LOG2_E
SWA_SCALE = SWA_HD ** -0.5

ADAM_LR = 0.001
ADAM_B1 = 0.9
ADAM_B2 = 0.999
ADAM_EPS = 1e-08
ADAM_WD = 0.01
ADAM_STEP = 10

LANES = 128
ATT_COLS = 128
N_CHIPS = 4
N_DEV = 8
MESH = pl.DeviceIdType.MESH

NT = (((1,), (1,)), ((), ()))
TN = (((0,), (0,)), ((), ()))

BIG = (("w_in", 1024, 936), ("w_q_b", 256, 384), ("w_kv_b", 128, 512), ("w_mla_up", 256, 1024),
       ("w_swa_up", 256, 1024), ("w_out", 256, 1024), ("w_mlp_up", 1024, 1024), ("w_mlp_down", 1024, 1024),
       ("w_ple", 256, 256), ("w_ple_gate", 256, 1024))
COL_SHARDED = ("w_in", "w_q_b", "w_kv_b", "w_mlp_up", "w_ple")
PACK_AT = {"w_in": ("early", 0, 0), "w_q_b": ("early", 1024, 0), "w_ple": ("early", 1024, 384), "w_kv_b": ("early", 1280, 0),
           "w_mla_up": ("late", 0, 0), "w_swa_up": ("late", 256, 0), "w_out": ("late", 512, 0), "w_ple_gate": ("late", 768, 0),
           "w_mlp_up": ("late", 1024, 0), "w_mlp_down": ("late", 2048, 0)}
PACK_ROWS = {"early": 1408, "late": 3072}
SMALL = (("g_mix_pre", 1024), ("g_q_a", 256), ("g_kv_a", 128), ("sinks", 16), ("g_mix_post", 1024),
         ("g_mlp_pre", 1024), ("g_mlp_post", 1024), ("g_ple", 1024))


def _dot(a, b):
    return jnp.dot(a, b, preferred_element_type=F32)


def _dot_nt(a, b):
    return lax.dot_general(a, b, NT, preferred_element_type=F32)


def _dot_tn(a, b):
    return lax.dot_general(a, b, TN, preferred_element_type=F32)


def _pcall(body, *, name, out_shape, grid=(), in_specs=None, out_specs=None, scratch=(), sem=None, vmem_mb=48, aliases=None):
    params = dict(vmem_limit_bytes=vmem_mb << 20)
    if sem is not None:
        params["dimension_semantics"] = sem
    return pl.pallas_call(body, name=name, grid=grid, in_specs=in_specs, out_specs=out_specs, out_shape=out_shape,
                          scratch_shapes=list(scratch), input_output_aliases=aliases or {},
                          compiler_params=pltpu.CompilerParams(**params))


def _rows(tm, n, col=0):
    return pl.BlockSpec((tm, n), lambda i: (i, col))


def _full(shape):
    return pl.BlockSpec(shape, lambda i: (0,) * len(shape))


def _rms(x, g):
    r = lax.rsqrt(jnp.mean(x * x, axis=-1, keepdims=True) + EPS)
    return x * r * g


def _rms_bwd(dy, x, g):
    r = lax.rsqrt(jnp.mean(x * x, axis=-1, keepdims=True) + EPS)
    xn = x * r
    dn = dy * g
    dx = r * (dn - xn * jnp.mean(dn * xn, axis=-1, keepdims=True))
    return dx, jnp.sum(dy * xn, axis=0, keepdims=True)


def _sigmoid(x):
    return 1.0 / (1.0 + jnp.exp(-x))


def _rope(x, c, a, b, half):
    return x * c + pltpu.roll(x, LANES - half, 1) * a + pltpu.roll(x, half, 1) * b


def _rope_tables(T, kind):
    lane = jnp.arange(LANES)
    if kind == "mla":
        half = MLA_ROPE // 2
        rel = lane - MLA_NOPE
        on = (rel >= 0) & (rel < MLA_ROPE)
        d = MLA_ROPE
    else:
        half = SWA_HD // 2
        rel = lane % SWA_HD
        on = jnp.ones((LANES,), bool)
        d = SWA_HD
    first = on & (rel < half)
    second = on & (rel >= half)
    f = jnp.where(first, rel, rel - half).astype(F32)
    inv = jnp.exp(-math.log(ROPE_THETA) * f * (2.0 / d))
    ang = jnp.arange(T, dtype=F32)[:, None] * inv[None, :]
    cos, sin = jnp.cos(ang), jnp.sin(ang)
    c = jnp.where(on[None], cos, 1.0)
    a = jnp.where(first[None], -sin, 0.0)
    b = jnp.where(second[None], sin, 0.0)
    return c, a, b


def _fwd_in(x, g1, w_in_p, tm):
    T = x.shape[0]

    def body(x_ref, g_ref, w_ref, z_ref, h_ref):
        h = _rms(x_ref[...], g_ref[...]).astype(BF16)
        h_ref[...] = h
        z_ref[...] = _dot(h, w_ref[...])

    return _pcall(body, name="fwd_in", grid=(T // tm,),
                  in_specs=[_rows(tm, D), _full((1, D)), _full((D, NZ))],
                  out_specs=[_rows(tm, NZ), _rows(tm, D)],
                  out_shape=[SDS((T, NZ), F32), SDS((T, D), BF16)], sem=("parallel",))(x, g1, w_in_p)


def _fwd_qkv(z, gq, gkv, wqb, wkn, wv, tab_m, tab_s, tm):
    T = z.shape[0]

    def body(qa_ref, sq_ref, skd_ref, svd_ref, kva_ref, kr_ref, gq_ref, gkv_ref, wqb_ref, wkn_ref, wv_ref,
             cm_ref, am_ref, bm_ref, cs_ref, as_ref, bs_ref,
             qn_ref, kvn_ref, qm_ref, km_ref, vm_ref, qt_ref, kt_ref, vt_ref, qs_ref, ks_ref, vs_ref):
        qn = _rms(qa_ref[...], gq_ref[...]).astype(BF16)
        qn_ref[...] = qn
        kvn = _rms(kva_ref[...], gkv_ref[...]).astype(BF16)
        kvn_ref[...] = kvn
        cm, am, bm = cm_ref[...], am_ref[...], bm_ref[...]
        cs, as_, bs = cs_ref[...], as_ref[...], bs_ref[...]
        k_rope = _rope(kr_ref[...], cm, am, bm, MLA_ROPE // 2)
        vt_row = lax.broadcasted_iota(jnp.int32, (LANES, tm), 0)
        v_all = _dot(kvn, wv_ref[...])
        q_all = _dot(qn, wqb_ref[...])
        k_all = _dot(kvn, wkn_ref[...])
        for j in range(D // LANES):
            sl = slice(LANES * j, LANES * (j + 1))
            v = v_all[:, sl]
            vm_ref[:, sl] = v.astype(BF16)
            v_t = v.T
            for hh, rows64 in enumerate((v_t, pltpu.roll(v_t, 64, 0))):
                blk = jnp.where(vt_row < 64, rows64, jnp.where(vt_row == 64, 1.0, 0.0))
                vt_ref[0, LANES * (2 * j + hh):LANES * (2 * j + hh + 1), :] = blk.astype(BF16)
        for h in range(MLA_HEADS):
            sl = slice(LANES * h, LANES * (h + 1))
            qh = _rope(q_all[:, sl], cm, am, bm, MLA_ROPE // 2)
            qm_ref[:, sl] = qh.astype(BF16)
            qt_ref[0, sl, :] = qh.T.astype(BF16)
            k = k_all[:, sl] + k_rope
            km_ref[:, sl] = k.astype(BF16)
            kt_ref[0, sl, :] = k.T.astype(BF16)
        for j in range(D // LANES):
            sl = slice(LANES * j, LANES * (j + 1))
            qs_ref[:, sl] = _rope(sq_ref[:, sl], cs, as_, bs, SWA_HD // 2).astype(BF16)
        for j in range(2):
            sl = slice(LANES * j, LANES * (j + 1))
            ks_ref[:, sl] = _rope(skd_ref[:, sl], cs, as_, bs, SWA_HD // 2).astype(BF16)
        vs_ref[...] = svd_ref[...].astype(BF16)

    tab = [_rows(tm, LANES)] * 6
    return _pcall(body, name="fwd_qkv", grid=(T // tm,),
                  in_specs=[_rows(tm, 256, 12), _rows(tm, 1024, 0), _rows(tm, 256, 13), _rows(tm, 256, 14),
                            _rows(tm, 128, 30), _rows(tm, 128, 31), _full((1, Q_LORA)), _full((1, KV_LORA)),
                            _full((Q_LORA, 2048)), _full((KV_LORA, 2048)), _full((KV_LORA, 1024))] + tab,
                  out_specs=[_rows(tm, Q_LORA), _rows(tm, KV_LORA), _rows(tm, 2048), _rows(tm, 2048), _rows(tm, 1024),
                             pl.BlockSpec((1, 2048, tm), lambda i: (i, 0, 0)), pl.BlockSpec((1, 2048, tm), lambda i: (i, 0, 0)),
                             pl.BlockSpec((1, 2048, tm), lambda i: (i, 0, 0)),
                             _rows(tm, 1024), _rows(tm, 256), _rows(tm, 256)],
                  out_shape=[SDS((T, Q_LORA), BF16), SDS((T, KV_LORA), BF16), SDS((T, 2048), BF16), SDS((T, 2048), BF16),
                             SDS((T, 1024), BF16), SDS((T // tm, 2048, tm), BF16), SDS((T // tm, 2048, tm), BF16),
                             SDS((T // tm, 2048, tm), BF16),
                             SDS((T, 1024), BF16), SDS((T, 256), BF16), SDS((T, 256), BF16)],
                  sem=("parallel",))(z, z, z, z, z, z, gq, gkv, wqb, wkn, wv, *tab_m, *tab_s)


def _mla_fwd(qm, km, vt, tb):
    T = qm.shape[0]
    nb = T // tb
    cc = ATT_COLS

    def body(q_ref, k_ref, vt_ref, o_ref, l_ref, s_ref, p_ref, al_ref, m_ref, acc_ref):
        i = pl.program_id(1)
        m_ref[...] = jnp.full(m_ref.shape, NEG, F32)
        acc_ref[...] = jnp.zeros_like(acc_ref)
        p_ref[1] = jnp.zeros(p_ref.shape[1:], BF16)
        al_ref[1] = jnp.ones(al_ref.shape[1:], F32)
        key = lax.broadcasted_iota(jnp.int32, (tb, cc), 0)
        qry = lax.broadcasted_iota(jnp.int32, (tb, cc), 1)

        def scores(j, slot):
            off = pl.multiple_of(j * tb, tb)
            for hh in range(2):
                sl = slice(LANES * hh, LANES * (hh + 1))
                s_ref[slot, hh] = _dot_nt(k_ref[pl.ds(off, tb), sl], q_ref[:, sl])

        def softmax(slot, diagonal):
            chains = [(hh, slice(cc * c, cc * (c + 1)), c) for hh in range(2) for c in range(tb // cc)]

            def scaled(hh, cols, c):
                t = s_ref[slot, hh, :, cols] * MLA_LOG2_SCALE
                return jnp.where(key <= qry + cc * c, t, NEG) if diagonal else t

            tops = []
            for hh, cols, c in chains:
                if diagonal:
                    top = jnp.max(scaled(hh, cols, c), axis=0, keepdims=True)
                else:
                    top = jnp.max(s_ref[slot, hh, :, cols], axis=0, keepdims=True) * MLA_LOG2_SCALE
                m_old = m_ref[hh, :, cols]
                mn = jnp.maximum(m_old, top)
                m_ref[hh, :, cols] = mn
                al_ref[slot, hh, :, cols] = jnp.exp2(m_old - mn)
                tops.append(mn)
            for (hh, cols, c), mn in zip(chains, tops):
                p_ref[slot, hh, :, cols] = jnp.exp2(scaled(hh, cols, c) - mn).astype(BF16)

        def accumulate(j, slot):
            for hh in range(2):
                acc_ref[hh] = al_ref[slot, hh] * acc_ref[hh] + _dot(vt_ref[j, LANES * hh:LANES * (hh + 1), :], p_ref[slot, hh])

        def step(t, carry):
            scores(2 * t + 1, 1)
            accumulate(jnp.maximum(2 * t - 1, 0), 1)
            softmax(0, False)
            scores(2 * t + 2, 0)
            accumulate(2 * t, 0)
            softmax(1, False)
            return carry

        scores(0, 0)
        lax.fori_loop(0, i // 2, step, 0)

        @pl.when(i % 2 == 1)
        def _():
            scores(i, 1)
            accumulate(jnp.maximum(i - 2, 0), 1)
            softmax(0, False)
            accumulate(i - 1, 0)
            softmax(1, True)
            accumulate(i, 1)

        @pl.when(i % 2 == 0)
        def _():
            accumulate(jnp.maximum(i - 1, 0), 1)
            softmax(0, True)
            accumulate(i, 0)
        den = [acc_ref[hh, 64:65, :] for hh in range(2)]
        o_ref[...] = jnp.concatenate([acc_ref[hh, 0:64, :] / den[hh] for hh in range(2)], axis=0).T
        sub = lax.broadcasted_iota(jnp.int32, (8, tb), 0)
        lse = [m_ref[hh] + jnp.log(den[hh]) * LOG2_E for hh in range(2)]
        l_ref[0, 0] = jnp.where(sub == 0, lse[0], jnp.where(sub == 1, lse[1], 0.0))

    return _pcall(body, name="mla_fwd", grid=(MLA_HEADS // 2, nb),
                  in_specs=[pl.BlockSpec((tb, 256), lambda p, i: (i, p)), pl.BlockSpec((T, 256), lambda p, i: (0, p)),
                            pl.BlockSpec((nb, 2 * LANES, tb), lambda p, i: (0, p, 0))],
                  out_specs=[pl.BlockSpec((tb, LANES), lambda p, i: (i, p)),
                             pl.BlockSpec((1, 1, 8, tb), lambda p, i: (p, i, 0, 0))],
                  out_shape=[SDS((T, D), F32), SDS((MLA_HEADS // 2, nb, 8, tb), F32)],
                  scratch=[pltpu.VMEM((2, 2, tb, tb), F32), pltpu.VMEM((2, 2, tb, tb), BF16), pltpu.VMEM((2, 2, 1, tb), F32),
                           pltpu.VMEM((2, 1, tb), F32), pltpu.VMEM((2, LANES, tb), F32)],
                  sem=("parallel", "arbitrary"))(qm, km, vt)


def _swa_mask(n):
    row = lax.broadcasted_iota(jnp.int32, (WINDOW, 2 * WINDOW), 0)
    col = lax.broadcasted_iota(jnp.int32, (WINDOW, 2 * WINDOW), 1)
    rel = row - col + WINDOW
    return (rel >= 0) & (rel < WINDOW) & ((col >= WINDOW) | (n > 0))


def _swa_specs(T):
    nb = T // WINDOW
    cur = lambda w: pl.BlockSpec((WINDOW, w), lambda n: (n, 0))
    prev = lambda w: pl.BlockSpec((WINDOW, w), lambda n: (jnp.maximum(n - 1, 0), 0))
    return nb, cur, prev


def _swa_fwd(sinks, qs, ks, vs):
    T = qs.shape[0]
    nb, cur, prev = _swa_specs(T)

    def body(sink_ref, q_ref, kc_ref, kp_ref, vc_ref, vp_ref, o_ref, l_ref, kb_ref, vb_ref, s_ref, p_ref):
        n = pl.program_id(0)
        mask = _swa_mask(n)
        lo = lax.broadcasted_iota(jnp.int32, (WINDOW, LANES), 1) < 64
        hi = jnp.logical_not(lo)
        for g in range(2):
            gs = slice(LANES * g, LANES * (g + 1))
            kb_ref[g] = jnp.concatenate([kp_ref[:, gs], kc_ref[:, gs]], axis=0)
            vb_ref[g] = jnp.concatenate([vp_ref[:, gs], vc_ref[:, gs]], axis=0)
        for h in range(SWA_HEADS):
            qp = q_ref[:, LANES * (h // 2):LANES * (h // 2 + 1)]
            qh = jnp.where(lo if h % 2 == 0 else hi, qp, jnp.zeros_like(qp))
            s_ref[h] = _dot_nt(qh, kb_ref[h // 8])
        for j in range(SWA_HEADS // 2):
            sl = slice(LANES * j, LANES * (j + 1))
            lses = []
            for h in (2 * j, 2 * j + 1):
                s = jnp.where(mask, s_ref[h] * SWA_SCALE, NEG)
                sk = sink_ref[h]
                m = jnp.maximum(jnp.max(s, axis=1, keepdims=True), sk)
                e = jnp.exp(s - m)
                den = jnp.sum(e, axis=1, keepdims=True) + jnp.exp(sk - m)
                p_ref[h] = (e / den).astype(BF16)
                lses.append(jnp.broadcast_to(m + jnp.log(den), (WINDOW, LANES)))
            l_ref[:, sl] = jnp.where(lo, lses[0], lses[1])
        for j in range(SWA_HEADS // 2):
            vb = vb_ref[j // 4]
            o_ref[:, LANES * j:LANES * (j + 1)] = jnp.where(lo, _dot(p_ref[2 * j], vb), _dot(p_ref[2 * j + 1], vb))

    return _pcall(body, name="swa_fwd", grid=(nb,),
                  in_specs=[pl.BlockSpec(memory_space=pltpu.SMEM), cur(D), cur(256), prev(256), cur(256), prev(256)],
                  out_specs=[cur(D), cur(D)], out_shape=[SDS((T, D), F32)] * 2,
                  scratch=[pltpu.VMEM((2, 2 * WINDOW, LANES), BF16), pltpu.VMEM((2, 2 * WINDOW, LANES), BF16),
                           pltpu.VMEM((SWA_HEADS, WINDOW, 2 * WINDOW), F32), pltpu.VMEM((SWA_HEADS, WINDOW, 2 * WINDOW), BF16)],
                  sem=("parallel",))(sinks, qs, ks, ks, vs, vs)


def _fwd_mix(om, os_, z, x, wmu, wsu, wo, g2, tm):
    T = x.shape[0]

    def body(om_ref, os_ref, ga_ref, gb_ref, x_ref, wmu_ref, wsu_ref, wo_ref, g2_ref,
             y_ref, yo_ref, au_ref, bu_ref, x1_ref):
        au = _dot(om_ref[...].astype(BF16), wmu_ref[...])
        bu = _dot(os_ref[...].astype(BF16), wsu_ref[...])
        au_ref[...] = au
        bu_ref[...] = bu
        y = (_sigmoid(ga_ref[...]) * au + _sigmoid(gb_ref[...]) * bu).astype(BF16)
        y_ref[...] = y
        yo = _dot(y, wo_ref[...])
        yo_ref[...] = yo
        x1_ref[...] = x_ref[...] + _rms(yo, g2_ref[...])

    r = _rows(tm, D)
    w = _full((D, D))
    return _pcall(body, name="fwd_mix", grid=(T // tm,),
                  in_specs=[r, r, _rows(tm, D, 1), _rows(tm, D, 2), r, w, w, w, _full((1, D))],
                  out_specs=[r] * 5,
                  out_shape=[SDS((T, D), BF16), SDS((T, D), F32), SDS((T, D), F32), SDS((T, D), F32), SDS((T, D), F32)],
                  sem=("parallel",))(om, os_, z, z, x, wmu, wsu, wo, g2)


def _fwd_mlp_up(x1, g3, w1, tm):
    T = x1.shape[0]

    def body(x_ref, g_ref, w_ref, h_ref, u_ref):
        h = _rms(x_ref[...], g_ref[...]).astype(BF16)
        h_ref[...] = h
        u_ref[...] = jnp.square(jnp.maximum(_dot(h, w_ref[...]), 0.0)).astype(BF16)

    return _pcall(body, name="fwd_mlp_up", grid=(T // tm,),
                  in_specs=[_rows(tm, D), _full((1, D)), _full((D, D_FF))],
                  out_specs=[_rows(tm, D), _rows(tm, D_FF)],
                  out_shape=[SDS((T, D), BF16), SDS((T, D_FF), BF16)],
                  sem=("parallel",))(x1, g3, w1)


def _fwd_mlp_down(u, w2, x1, g4, tm):
    T = x1.shape[0]

    def body(u_ref, w_ref, x_ref, g_ref, d_ref, x2_ref):
        d = _dot(u_ref[...], w_ref[...])
        d_ref[...] = d
        x2_ref[...] = x_ref[...] + _rms(d, g_ref[...])

    return _pcall(body, name="fwd_mlp_down", grid=(T // tm,),
                  in_specs=[_rows(tm, D_FF), _full((D_FF, D)), _rows(tm, D), _full((1, D))],
                  out_specs=[_rows(tm, D), _rows(tm, D)], out_shape=[SDS((T, D), F32)] * 2,
                  sem=("parallel",))(u, w2, x1, g4)


def _ple_fwd_bwd(p, x2, tgt, wple, g5, wpg, tm):
    T = x2.shape[0]

    def body(p_ref, x2_ref, t_ref, wple_ref, g5_ref, wpg_ref, loss_ref, dx2_ref, dgt_ref, de0_ref, dg5_ref):
        @pl.when(pl.program_id(0) == 0)
        def _():
            loss_ref[...] = jnp.zeros_like(loss_ref)
            dg5_ref[...] = jnp.zeros_like(dg5_ref)

        e0 = _dot(p_ref[...].astype(BF16), wple_ref[...])
        g5 = g5_ref[...]
        r = lax.rsqrt(jnp.mean(e0 * e0, axis=-1, keepdims=True) + EPS)
        en = e0 * r
        e = en * g5
        x2 = x2_ref[...]
        s = _sigmoid(_dot(x2.astype(BF16), wpg_ref[...]))
        diff = x2 + s * e - t_ref[...]
        sq = jnp.sum(jnp.sum(diff * diff, axis=1, keepdims=True), axis=0, keepdims=True)
        loss_ref[...] += jnp.broadcast_to(sq * (0.5 / D), loss_ref.shape)
        dx3 = diff * (1.0 / D)
        de = dx3 * s
        dgt = (dx3 * e * s * (1.0 - s)).astype(BF16)
        dgt_ref[...] = dgt
        dn = de * g5
        de0_ref[...] = (r * (dn - en * jnp.mean(dn * en, axis=-1, keepdims=True))).astype(BF16)
        dg5_ref[...] += jnp.sum(de * en, axis=0, keepdims=True)
        dx2_ref[...] = dx3 + _dot_nt(dgt, wpg_ref[...])

    r = _rows(tm, D)
    return _pcall(body, name="ple_fwd_bwd", grid=(T // tm,),
                  in_specs=[_rows(tm, PLE), r, r, _full((PLE, D)), _full((1, D)), _full((D, D))],
                  out_specs=[_full((8, LANES)), r, r, r, _full((1, D))],
                  out_shape=[SDS((8, LANES), F32), SDS((T, D), F32), SDS((T, D), BF16), SDS((T, D), BF16), SDS((1, D), F32)],
                  sem=("arbitrary",))(p, x2, tgt, wple, g5, wpg)


def _bwd_mlp_down(dx2, d, g4, w2, u, tm):
    T = dx2.shape[0]

    def body(dx_ref, d_ref, g_ref, w_ref, u_ref, dd_ref, da_ref, dg_ref):
        @pl.when(pl.program_id(0) == 0)
        def _():
            dg_ref[...] = jnp.zeros_like(dg_ref)

        dd, dg = _rms_bwd(dx_ref[...], d_ref[...], g_ref[...])
        dg_ref[...] += dg
        ddb = dd.astype(BF16)
        dd_ref[...] = ddb
        du = _dot_nt(ddb, w_ref[...])
        da_ref[...] = (du * (2.0 * jnp.sqrt(u_ref[...].astype(F32)))).astype(BF16)

    return _pcall(body, name="bwd_mlp_down", grid=(T // tm,),
                  in_specs=[_rows(tm, D), _rows(tm, D), _full((1, D)), _full((D_FF, D)), _rows(tm, D_FF)],
                  out_specs=[_rows(tm, D), _rows(tm, D_FF), _full((1, D))],
                  out_shape=[SDS((T, D), BF16), SDS((T, D_FF), BF16), SDS((1, D), F32)],
                  sem=("arbitrary",))(dx2, d, g4, w2, u)


def _bwd_mlp_up(da, w1, x1, g3, dx2, tm):
    T = dx2.shape[0]

    def body(da_ref, w_ref, x_ref, g_ref, dx2_ref, dx1_ref, dg_ref):
        @pl.when(pl.program_id(0) == 0)
        def _():
            dg_ref[...] = jnp.zeros_like(dg_ref)

        dh = _dot_nt(da_ref[...], w_ref[...])
        dx, dg = _rms_bwd(dh, x_ref[...], g_ref[...])
        dg_ref[...] += dg
        dx1_ref[...] = dx2_ref[...] + dx

    return _pcall(body, name="bwd_mlp_up", grid=(T // tm,),
                  in_specs=[_rows(tm, D_FF), _full((D, D_FF)), _rows(tm, D), _full((1, D)), _rows(tm, D)],
                  out_specs=[_rows(tm, D), _full((1, D))],
                  out_shape=[SDS((T, D), F32), SDS((1, D), F32)], sem=("arbitrary",))(da, w1, x1, g3, dx2)


def _bwd_mix(dx1, yo, g2, wo, z, au, bu, wmu, wsu, om, tm):
    T = dx1.shape[0]

    def body(dx_ref, yo_ref, g_ref, wo_ref, ga_ref, gb_ref, au_ref, bu_ref, wmu_ref, wsu_ref, om_ref,
             dyo_ref, dg_ref, dau_ref, dbu_ref, dga_ref, dgb_ref, dos_ref, dl_ref, dot_ref):
        @pl.when(pl.program_id(0) == 0)
        def _():
            dg_ref[...] = jnp.zeros_like(dg_ref)

        dyo, dg = _rms_bwd(dx_ref[...], yo_ref[...], g_ref[...])
        dg_ref[...] += dg
        dyob = dyo.astype(BF16)
        dyo_ref[...] = dyob
        dy = _dot_nt(dyob, wo_ref[...])
        sa = _sigmoid(ga_ref[...])
        sb = _sigmoid(gb_ref[...])
        dau = (dy * sa).astype(BF16)
        dbu = (dy * sb).astype(BF16)
        dau_ref[...] = dau
        dbu_ref[...] = dbu
        dga_ref[...] = (dy * au_ref[...] * sa * (1.0 - sa)).astype(BF16)
        dgb_ref[...] = (dy * bu_ref[...] * sb * (1.0 - sb)).astype(BF16)
        dom = _dot_nt(dau, wmu_ref[...])
        dos_ref[...] = _dot_nt(dbu, wsu_ref[...])
        prod = dom * om_ref[...]
        sub = lax.broadcasted_iota(jnp.int32, (8, tm), 0)
        for pr in range(MLA_HEADS // 2):
            sl = slice(LANES * pr, LANES * (pr + 1))
            pt = prod[:, sl].T
            d0 = jnp.sum(pt[0:64], axis=0, keepdims=True)
            d1 = jnp.sum(pt[64:128], axis=0, keepdims=True)
            dl_ref[pr, 0] = jnp.where(sub == 0, d0, jnp.where(sub == 1, d1, 0.0))
            dot_ref[0, sl, :] = dom[:, sl].T.astype(BF16)

    r = _rows(tm, D)
    w = _full((D, D))
    return _pcall(body, name="bwd_mix", grid=(T // tm,),
                  in_specs=[r, r, _full((1, D)), w, _rows(tm, D, 1), _rows(tm, D, 2), r, r, w, w, r],
                  out_specs=[r, _full((1, D)), r, r, r, r, r, pl.BlockSpec((MLA_HEADS // 2, 1, 8, tm), lambda i: (0, i, 0, 0)),
                             pl.BlockSpec((1, D, tm), lambda i: (i, 0, 0))],
                  out_shape=[SDS((T, D), BF16), SDS((1, D), F32), SDS((T, D), BF16), SDS((T, D), BF16), SDS((T, D), BF16),
                             SDS((T, D), BF16), SDS((T, D), F32), SDS((MLA_HEADS // 2, T // tm, 8, tm), F32),
                             SDS((T // tm, D, tm), BF16)],
                  sem=("arbitrary",))(dx1, yo, g2, wo, z, z, au, bu, wmu, wsu, om)


def _mla_bwd(qm, qt, km, kt, vm, dot, lse, delta, tb):
    T = qm.shape[0]
    nb = T // tb
    cc = ATT_COLS

    n_tiles = nb * (nb + 1) // 2
    assert n_tiles % 2 == 0, nb

    def body(q_ref, qt_ref, k_ref, kt_ref, v_ref, dot_ref, l_ref, dl_ref, dqt_ref, dkt_ref, dvt_ref,
             s_ref, dp_ref, p_ref, ds_ref, vh_ref):
        dqt_ref[...] = jnp.zeros_like(dqt_ref)
        dkt_ref[...] = jnp.zeros_like(dkt_ref)
        dvt_ref[...] = jnp.zeros_like(dvt_ref)
        lo = lax.broadcasted_iota(jnp.int32, (tb, LANES), 1) < 64
        ahead = [lax.broadcasted_iota(jnp.int32, (tb, cc), 0) - lax.broadcasted_iota(jnp.int32, (tb, cc), 1) - cc * c
                 for c in range(tb // cc)]

        def scores(j, i, slot):
            rows_j = pl.ds(pl.multiple_of(j * tb, tb), tb)
            rows_i = pl.ds(pl.multiple_of(i * tb, tb), tb)
            v = v_ref[rows_j, :]
            vh_ref[slot, 0] = jnp.where(lo, v, jnp.zeros_like(v))
            vh_ref[slot, 1] = jnp.where(lo, jnp.zeros_like(v), v)
            for hh in range(2):
                sl = slice(LANES * hh, LANES * (hh + 1))
                s_ref[slot, hh] = _dot_nt(k_ref[rows_j, sl], q_ref[rows_i, sl])
                dp_ref[slot, hh] = _dot(vh_ref[slot, hh], dot_ref[i])

        def grads(j, i, slot):
            lse_i = l_ref[0, i]
            delta_i = dl_ref[0, i]
            room = (i - j) * tb
            for hh in range(2):
                for c in range(tb // cc):
                    cols = slice(cc * c, cc * (c + 1))
                    p = jnp.exp2(s_ref[slot, hh, :, cols] * MLA_LOG2_SCALE - lse_i[hh:hh + 1, cols])
                    p = jnp.where(ahead[c] <= room, p, 0.0)
                    p_ref[hh, :, cols] = p.astype(BF16)
                    ds_ref[hh, :, cols] = (p * (dp_ref[slot, hh, :, cols] - delta_i[hh:hh + 1, cols]) * MLA_SCALE).astype(BF16)
            for hh in range(2):
                sl = slice(LANES * hh, LANES * (hh + 1))
                half = slice(64 * hh, 64 * (hh + 1))
                dvt_ref[j, half, :] += _dot_nt(dot_ref[i, half, :], p_ref[hh])
                dkt_ref[j, sl, :] += _dot_nt(qt_ref[i, sl, :], ds_ref[hh])
                dqt_ref[i, sl, :] += _dot(kt_ref[j, sl, :], ds_ref[hh])

        def following(j, i):
            last = i + 1 >= nb
            j2 = jnp.where(last, jnp.minimum(j + 1, nb - 1), j)
            return j2, jnp.where(last, j2, i + 1)

        def step(t, tile):
            ja, ia = tile
            jb, ib = following(ja, ia)
            scores(jb, ib, 1)
            grads(ja, ia, 0)
            jn, i_n = following(jb, ib)
            scores(jn, i_n, 0)
            grads(jb, ib, 1)
            return jn, i_n

        first = (jnp.int32(0), jnp.int32(0))
        scores(*first, 0)
        lax.fori_loop(0, n_tiles // 2, step, first)

    stat = pl.BlockSpec((1, nb, 8, tb), lambda p: (p, 0, 0, 0))
    pair = lambda w: pl.BlockSpec((T, w), lambda p: (0, p))
    pair_t = lambda w: pl.BlockSpec((nb, w, tb), lambda p: (0, p, 0))
    return _pcall(body, name="mla_bwd", grid=(MLA_HEADS // 2,),
                  in_specs=[pair(256), pair_t(256), pair(256), pair_t(256), pair(LANES), pair_t(LANES), stat, stat],
                  out_specs=[pair_t(256), pair_t(256), pair_t(LANES)],
                  out_shape=[SDS((nb, 2048, tb), F32), SDS((nb, 2048, tb), F32), SDS((nb, D, tb), F32)],
                  scratch=[pltpu.VMEM((2, 2, tb, tb), F32), pltpu.VMEM((2, 2, tb, tb), F32), pltpu.VMEM((2, tb, tb), BF16),
                           pltpu.VMEM((2, tb, tb), BF16), pltpu.VMEM((2, 2, tb, LANES), BF16)],
                  sem=("parallel",), vmem_mb=56)(qm, qt, km, kt, vm, dot, lse, delta)


def _swa_bwd(sinks, qs, ks, vs, do, o, lse):
    T = qs.shape[0]
    nb, cur, prev = _swa_specs(T)

    def body(sink_ref, q_ref, kc_ref, kp_ref, vc_ref, vp_ref, do_ref, o_ref, l_ref,
             dq_ref, dkc_ref, dkp_ref, dvc_ref, dvp_ref, dsink_ref, kb_ref, vb_ref, s_ref, dp_ref, p_ref, ds_ref):
        n = pl.program_id(0)

        @pl.when(n == 0)
        def _():
            dsink_ref[...] = jnp.zeros_like(dsink_ref)

        mask = _swa_mask(n)
        lo = lax.broadcasted_iota(jnp.int32, (WINDOW, LANES), 1) < 64
        hi = jnp.logical_not(lo)
        lane8 = lax.broadcasted_iota(jnp.int32, (8, LANES), 1)
        for g in range(2):
            gs = slice(LANES * g, LANES * (g + 1))
            kb_ref[g] = jnp.concatenate([kp_ref[:, gs], kc_ref[:, gs]], axis=0)
            vb_ref[g] = jnp.concatenate([vp_ref[:, gs], vc_ref[:, gs]], axis=0)

        def head(h):
            sl = slice(LANES * (h // 2), LANES * (h // 2 + 1))
            hm = lo if h % 2 == 0 else hi
            qp = q_ref[:, sl]
            return hm, sl, jnp.where(hm, qp, jnp.zeros_like(qp)), jnp.where(hm, do_ref[:, sl], 0.0).astype(BF16)

        for h in range(SWA_HEADS):
            _, _, qh, dom = head(h)
            s_ref[h] = _dot_nt(qh, kb_ref[h // 8])
            dp_ref[h] = _dot_nt(dom, vb_ref[h // 8])
        dsink = jnp.zeros((8, LANES), F32)
        for h in range(SWA_HEADS):
            hm, sl, _, _ = head(h)
            lse_h = jnp.max(jnp.where(hm, l_ref[:, sl], -jnp.inf), axis=1, keepdims=True)
            delta = jnp.sum(jnp.where(hm, do_ref[:, sl] * o_ref[:, sl], 0.0), axis=1, keepdims=True)
            p = jnp.exp(jnp.where(mask, s_ref[h] * SWA_SCALE, NEG) - lse_h)
            p_ref[h] = p.astype(BF16)
            ds_ref[h] = (p * (dp_ref[h] - delta) * SWA_SCALE).astype(BF16)
            d_sink = -jnp.sum(jnp.exp(sink_ref[h] - lse_h) * delta, axis=0, keepdims=True)
            dsink = dsink + jnp.where(lane8 == h, d_sink, 0.0)
        dsink_ref[...] += dsink
        for g in range(2):
            gs = slice(LANES * g, LANES * (g + 1))
            dkb = jnp.zeros((2 * WINDOW, LANES), F32)
            dvb = jnp.zeros((2 * WINDOW, LANES), F32)
            for j in range(4 * g, 4 * g + 4):
                dqs = []
                for h in (2 * j, 2 * j + 1):
                    _, _, qh, dom = head(h)
                    dvb = dvb + _dot_tn(p_ref[h], dom)
                    dkb = dkb + _dot_tn(ds_ref[h], qh)
                    dqs.append(_dot(ds_ref[h], kb_ref[g]))
                dq_ref[:, LANES * j:LANES * (j + 1)] = jnp.where(lo, dqs[0], dqs[1])
            dkp_ref[:, gs] = dkb[:WINDOW]
            dkc_ref[:, gs] = dkb[WINDOW:]
            dvp_ref[:, gs] = dvb[:WINDOW]
            dvc_ref[:, gs] = dvb[WINDOW:]

    band = pltpu.VMEM((2, 2 * WINDOW, LANES), BF16)
    return _pcall(body, name="swa_bwd", grid=(nb,),
                  in_specs=[pl.BlockSpec(memory_space=pltpu.SMEM), cur(D), cur(256), prev(256), cur(256), prev(256),
                            cur(D), cur(D), cur(D)],
                  out_specs=[cur(D), cur(256), cur(256), cur(256), cur(256), _full((8, LANES))],
                  out_shape=[SDS((T, D), F32), SDS((T, 256), F32), SDS((T, 256), F32), SDS((T, 256), F32), SDS((T, 256), F32),
                             SDS((8, LANES), F32)],
                  scratch=[band, band, pltpu.VMEM((SWA_HEADS, WINDOW, 2 * WINDOW), F32),
                           pltpu.VMEM((SWA_HEADS, WINDOW, 2 * WINDOW), F32), pltpu.VMEM((SWA_HEADS, WINDOW, 2 * WINDOW), BF16),
                           pltpu.VMEM((SWA_HEADS, WINDOW, 2 * WINDOW), BF16)],
                  sem=("arbitrary",))(sinks, qs, ks, ks, vs, vs, do, o, lse)


def _bwd_qkv(dqm, dkm, dvm, dqs, dkc, dkp, dvc, dvp, z, gq, gkv, wqb, wkn, wv, tab_m, tab_s):
    T = z.shape[0]
    tm = WINDOW
    nb = T // tm
    per = dqm.shape[2] // tm

    def body(dqm_ref, dkm_ref, dvm_ref, dqs_ref, dkc_ref, dkp_ref, dvc_ref, dvp_ref, qa_ref, kva_ref, gq_ref, gkv_ref,
             wqb_ref, wkn_ref, wv_ref, cm_ref, am_ref, bm_ref, cs_ref, as_ref, bs_ref,
             dq_out, dkn_out, dv_out, dsq_ref, drest_ref, dgq_ref, dgkv_ref):
        i = pl.program_id(0)

        @pl.when(i == 0)
        def _():
            dgq_ref[...] = jnp.zeros_like(dgq_ref)
            dgkv_ref[...] = jnp.zeros_like(dgkv_ref)

        cm, am, bm = cm_ref[...], -am_ref[...], -bm_ref[...]
        cs, as_, bs = cs_ref[...], -as_ref[...], -bs_ref[...]
        lane = lax.broadcasted_iota(jnp.int32, (tm, LANES), 1)
        nope = lane < MLA_NOPE
        roped = jnp.logical_and(lane >= MLA_NOPE, lane < MLA_NOPE + MLA_ROPE)
        dkr = jnp.zeros((tm, LANES), F32)
        for h in range(MLA_HEADS):
            sl = slice(LANES * h, LANES * (h + 1))
            dq_out[:, sl] = _rope(dqm_ref[0, sl, :].T, cm, am, bm, MLA_ROPE // 2).astype(BF16)
            dk_h = dkm_ref[0, sl, :].T
            dkn_out[:, sl] = jnp.where(nope, dk_h, 0.0).astype(BF16)
            dkr = dkr + jnp.where(roped, dk_h, 0.0)
        for j in range(D // LANES):
            sl = slice(LANES * j, LANES * (j + 1))
            dv_out[:, sl] = dvm_ref[0, sl, :].T.astype(BF16)
        dqn = _dot_nt(dq_out[...], wqb_ref[...])
        dkvn = _dot_nt(dkn_out[...], wkn_ref[...]) + _dot_nt(dv_out[...], wv_ref[...])
        dqa, dgq = _rms_bwd(dqn, qa_ref[...], gq_ref[...])
        dkva, dgkv = _rms_bwd(dkvn, kva_ref[...], gkv_ref[...])
        dgq_ref[...] += dgq
        dgkv_ref[...] += dgkv
        for j in range(D // LANES):
            sl = slice(LANES * j, LANES * (j + 1))
            dsq_ref[:, sl] = _rope(dqs_ref[:, sl], cs, as_, bs, SWA_HD // 2).astype(BF16)
        keep = (i < nb - 1).astype(F32)
        drest_ref[:, 0:256] = dqa.astype(BF16)
        for j in range(2):
            sl = slice(LANES * j, LANES * (j + 1))
            dk = dkc_ref[:, sl] + keep * dkp_ref[:, sl]
            drest_ref[:, 256 + LANES * j:256 + LANES * (j + 1)] = _rope(dk, cs, as_, bs, SWA_HD // 2).astype(BF16)
        drest_ref[:, 512:768] = (dvc_ref[...] + keep * dvp_ref[...]).astype(BF16)
        drest_ref[:, 768:896] = dkva.astype(BF16)
        drest_ref[:, 896:1024] = _rope(dkr, cm, am, bm, MLA_ROPE // 2).astype(BF16)

    nxt = pl.BlockSpec((tm, 256), lambda i: (jnp.minimum(i + 1, nb - 1), 0))
    tab = [_rows(tm, LANES)] * 6
    return _pcall(body, name="bwd_qkv", grid=(nb,),
                  in_specs=[pl.BlockSpec((1, 2048, tm), lambda i: (i // per, 0, i % per)),
                            pl.BlockSpec((1, 2048, tm), lambda i: (i // per, 0, i % per)),
                            pl.BlockSpec((1, 1024, tm), lambda i: (i // per, 0, i % per)), _rows(tm, 1024), _rows(tm, 256), nxt,
                            _rows(tm, 256), nxt, _rows(tm, 256, 12), _rows(tm, 128, 30), _full((1, Q_LORA)), _full((1, KV_LORA)),
                            _full((Q_LORA, 2048)), _full((KV_LORA, 2048)), _full((KV_LORA, 1024))] + tab,
                  out_specs=[_rows(tm, 2048), _rows(tm, 2048), _rows(tm, 1024), _rows(tm, 1024), _rows(tm, 1024),
                             _full((1, Q_LORA)), _full((1, KV_LORA))],
                  out_shape=[SDS((T, 2048), BF16), SDS((T, 2048), BF16), SDS((T, 1024), BF16), SDS((T, 1024), BF16),
                             SDS((T, 1024), BF16), SDS((1, Q_LORA), F32), SDS((1, KV_LORA), F32)],
                  sem=("arbitrary",))(dqm, dkm, dvm, dqs, dkc, dkp, dvc, dvp, z, z, gq, gkv, wqb, wkn, wv, *tab_m, *tab_s)


def _bwd_in(dsq, dga, dgb, drest, w_in_p, x, g1, dx1, tm):
    T = x.shape[0]

    def body(a_ref, b_ref, c_ref, d_ref, w_ref, x_ref, g_ref, dx1_ref, dx_ref, dg_ref):
        @pl.when(pl.program_id(0) == 0)
        def _():
            dg_ref[...] = jnp.zeros_like(dg_ref)

        dh = (_dot_nt(a_ref[...], w_ref[:, 0:1024]) + _dot_nt(b_ref[...], w_ref[:, 1024:2048])
              + _dot_nt(c_ref[...], w_ref[:, 2048:3072]) + _dot_nt(d_ref[...], w_ref[:, 3072:4096]))
        dx, dg = _rms_bwd(dh, x_ref[...], g_ref[...])
        dg_ref[...] += dg
        dx_ref[...] = dx1_ref[...] + dx

    r = _rows(tm, D)
    return _pcall(body, name="bwd_in", grid=(T // tm,),
                  in_specs=[r, r, r, r, _full((D, NZ)), r, _full((1, D)), r],
                  out_specs=[r, _full((1, D))], out_shape=[SDS((T, D), F32), SDS((1, D), F32)],
                  sem=("arbitrary",))(dsq, dga, dgb, drest, w_in_p, x, g1, dx1)


def _wgrad(a, g, name, into=None):
    T, K = a.shape
    N = g.shape[1]
    tk, tn, tt = min(K, 1024), min(N, 1024), min(T, 1024)
    if into is not None:
        buf, weight = into
        _, row0, lane0 = PACK_AT[weight]
        shard = {n: (r, c) for n, r, c in BIG}[weight]
        assert lane0 == 0 and shard[1] == D and tk % shard[0] == 0
        per_step = tk // shard[0]
    assert K % tk == 0 and N % tn == 0 and T % tt == 0, (a.shape, g.shape)
    steps = T // tt

    def body(a_ref, g_ref, *rest):
        o_ref, acc_ref = rest[-2:]
        t = pl.program_id(2)

        @pl.when(t == 0)
        def _():
            acc_ref[...] = jnp.zeros_like(acc_ref)

        acc_ref[...] += _dot_tn(a_ref[...].astype(BF16), g_ref[...].astype(BF16))

        @pl.when(t == steps - 1)
        def _():
            o_ref[...] = acc_ref[...].astype(o_ref.dtype).reshape(o_ref.shape)

    in_specs = [pl.BlockSpec((tt, tk), lambda k, n, t: (t, k)), pl.BlockSpec((tt, tn), lambda k, n, t: (t, n))]
    if into is None:
        return _pcall(body, name=name, grid=(K // tk, N // tn, steps), in_specs=in_specs,
                      out_specs=pl.BlockSpec((tk, tn), lambda k, n, t: (k, n)), out_shape=SDS((K, N), F32),
                      scratch=[pltpu.VMEM((tk, tn), F32)], sem=("parallel", "parallel", "arbitrary"))(a, g)
    assert row0 % shard[0] == 0 and (K // tk) * (N // tn) * per_step == N_CHIPS
    return _pcall(body, name=name, grid=(K // tk, N // tn, steps), in_specs=in_specs + [ANY],
                  out_specs=pl.BlockSpec((per_step, shard[0], tn), lambda k, n, t: (k + n, row0 // shard[0], 0)),
                  out_shape=SDS(buf.shape, buf.dtype),
                  scratch=[pltpu.VMEM((tk, tn), F32)], sem=("parallel", "parallel", "arbitrary"), aliases={2: 0})(a, g, buf)


def _adamw(w, packed_g, m, v, name):
    _, R, C = w.shape
    _, row0, lane0 = PACK_AT[name]
    tr = min(R, 256 if row0 % 256 == 0 else 128)
    assert row0 % tr == 0 and R % tr == 0

    def body(w_ref, g_ref, m_ref, v_ref, go_ref, d_ref, m2_ref, v2_ref):
        g_ = g_ref[:, lane0:lane0 + C]
        go_ref[0] = g_
        m2 = ADAM_B1 * m_ref[0] + (1.0 - ADAM_B1) * g_
        v2 = ADAM_B2 * v_ref[0] + (1.0 - ADAM_B2) * jnp.square(g_)
        m_hat = m2 / (1.0 - ADAM_B1 ** ADAM_STEP)
        v_hat = v2 / (1.0 - ADAM_B2 ** ADAM_STEP)
        d_ref[0] = -ADAM_LR * (m_hat / (jnp.sqrt(v_hat) + ADAM_EPS) + ADAM_WD * w_ref[0])
        m2_ref[0] = m2
        v2_ref[0] = v2

    r = pl.BlockSpec((1, tr, C), lambda i: (0, i, 0))
    return _pcall(body, name="adamw_" + name, grid=(R // tr,),
                  in_specs=[r, pl.BlockSpec((tr, D), lambda i: (row0 // tr + i, 0)), r, r], out_specs=[r] * 4,
                  out_shape=[SDS((1, R, C), F32)] * 4, sem=("parallel",))(w, packed_g, m, v)


def _adamw_small(w, parts, m, v):
    def body(w_ref, p_ref, m_ref, v_ref, g_ref, d_ref, m2_ref, v2_ref):
        g_ = p_ref[0]
        for k in range(1, N_DEV):
            g_ = g_ + p_ref[k]
        g_ref[...] = g_
        m2 = ADAM_B1 * m_ref[...] + (1.0 - ADAM_B1) * g_
        v2 = ADAM_B2 * v_ref[...] + (1.0 - ADAM_B2) * jnp.square(g_)
        m_hat = m2 / (1.0 - ADAM_B1 ** ADAM_STEP)
        v_hat = v2 / (1.0 - ADAM_B2 ** ADAM_STEP)
        d_ref[...] = -ADAM_LR * (m_hat / (jnp.sqrt(v_hat) + ADAM_EPS) + ADAM_WD * w_ref[...])
        m2_ref[...] = m2
        v2_ref[...] = v2

    s = _full((8, D))
    return _pcall(body, name="adamw_small", grid=(1,), in_specs=[s, _full((N_DEV, 8, D)), s, s], out_specs=[s] * 4,
                  out_shape=[SDS((8, D), F32)] * 4, sem=("arbitrary",))(w, parts, m, v)


ANY = pl.BlockSpec(memory_space=pl.ANY)


def _place():
    x, y, c = lax.axis_index("x"), lax.axis_index("y"), lax.axis_index("c")
    chips = [(1 - x, y), (x, 1 - y), (1 - x, 1 - y)]
    return x, y, c, chips


def _all_gather(wpk):
    rows = wpk.shape[0]
    HALF = rows // 2
    assert HALF % 16 == 0

    def body(in_ref, out_ref, send_sems, recv_sems):
        x, y, c, chips = _place()
        half = pl.ds(pl.multiple_of(c * HALF, 16), HALF)
        other = pl.ds(pl.multiple_of((1 - c) * HALF, 16), HALF)

        def copy(k, src, dst, to):
            return pltpu.make_async_remote_copy(src_ref=src, dst_ref=dst, send_sem=send_sems.at[k], recv_sem=recv_sems.at[k],
                                                device_id=to, device_id_type=MESH)

        first = [copy(k, in_ref.at[half], out_ref.at[2 * x + y, half], (cx, cy, c)) for k, (cx, cy) in enumerate(chips)]
        for cp in first:
            cp.start()
        passed = []
        for k, (cx, cy) in enumerate(chips):
            slot = out_ref.at[2 * cx + cy, half]
            copy(k, slot, slot, (x, y, c)).wait_recv()
            fwd = copy(3 + k, slot, slot, (x, y, 1 - c))
            fwd.start()
            passed.append(fwd)
        for k, (cx, cy) in enumerate(chips):
            slot = out_ref.at[2 * cx + cy, other]
            copy(3 + k, slot, slot, (x, y, c)).wait_recv()
        for cp in first + passed:
            cp.wait_send()

    return _pcall(body, name="all_gather_weights", in_specs=[ANY], out_specs=ANY,
                  out_shape=SDS((N_CHIPS, rows, D), BF16),
                  scratch=[pltpu.SemaphoreType.DMA((6,)), pltpu.SemaphoreType.DMA((6,))])(wpk)


HBM = pl.BlockSpec(memory_space=pltpu.HBM)
SEM = pl.BlockSpec(memory_space=pltpu.SEMAPHORE)
DATAFLOW = pltpu.SideEffectType.DATAFLOW_SIDE_EFFECTING


def _in_hbm(a):
    return pltpu.with_memory_space_constraint(a, pltpu.HBM)


def _gather_late_start(wpk, after):
    rows = wpk.shape[0]

    def body(in_ref, land_ref, after_ref, send_sems, recv_sems, in_thru, land_thru, token):
        x, y, c, chips = _place()
        for k, (cx, cy) in enumerate(chips):
            pltpu.make_async_remote_copy(src_ref=in_ref, dst_ref=land_ref.at[2 * x + y], send_sem=send_sems.at[k],
                                         recv_sem=recv_sems.at[k], device_id=(cx, cy, c), device_id_type=MESH).start()
        token[...] = jnp.zeros_like(token)

    return pl.pallas_call(
        body, name="gather_late_start",
        out_shape=(pltpu.SemaphoreType.DMA((3,)), pltpu.SemaphoreType.DMA((3,)), pltpu.HBM(wpk.shape, wpk.dtype),
                   pltpu.HBM((N_CHIPS, rows, D), wpk.dtype), SDS((8, LANES), F32)),
        in_specs=(HBM, HBM, ANY), out_specs=(SEM, SEM, HBM, HBM, pl.BlockSpec(memory_space=pltpu.VMEM)),
        input_output_aliases={0: 2, 1: 3}, compiler_params=pltpu.CompilerParams(has_side_effects=DATAFLOW),
    )(_in_hbm(wpk), _in_hbm(lax.empty((N_CHIPS, rows, D), wpk.dtype)), after)


def _gather_late_wait(send_sems, recv_sems, in_thru, land_thru, after):
    def body(in_ref, land_ref, send_sems, recv_sems, after_ref, after2_ref, in_dead, got_ref):
        x, y, c, chips = _place()
        for k, (cx, cy) in enumerate(chips):
            cp = pltpu.make_async_remote_copy(src_ref=in_ref, dst_ref=land_ref.at[2 * cx + cy], send_sem=send_sems.at[k],
                                              recv_sem=recv_sems.at[k], device_id=(cx, cy, c), device_id_type=MESH)
            cp.wait_send()
            cp.wait_recv()

    return pl.pallas_call(
        body, name="gather_late_wait",
        out_shape=(pltpu.HBM(in_thru.shape, in_thru.dtype), pltpu.HBM(land_thru.shape, land_thru.dtype)),
        in_specs=(HBM, HBM, SEM, SEM, ANY, ANY), out_specs=(HBM, HBM), input_output_aliases={0: 0, 1: 1},
        compiler_params=pltpu.CompilerParams(has_side_effects=DATAFLOW),
    )(in_thru, land_thru, send_sems, recv_sems, *after)[1]


def _rs_sibling(gpk):
    HALF = gpk.shape[1] // 2

    def body(in_ref, out_ref, send_sem, recv_sem):
        x, y, c, _ = _place()
        theirs = pl.ds(pl.multiple_of((1 - c) * HALF, 8), HALF)
        cp = pltpu.make_async_remote_copy(src_ref=in_ref.at[:, theirs], dst_ref=out_ref, send_sem=send_sem, recv_sem=recv_sem,
                                          device_id=(x, y, 1 - c), device_id_type=MESH)
        cp.start()
        cp.wait()

    return _pcall(body, name="rs_sibling", in_specs=[ANY], out_specs=ANY, out_shape=SDS((N_CHIPS, HALF, D), F32),
                  scratch=[pltpu.SemaphoreType.DMA, pltpu.SemaphoreType.DMA])(gpk)


def _rs_add_sibling(cidx, gpk, got):
    HALF = got.shape[1]
    th = HALF // 4
    nh = HALF // th
    assert th % 16 == 0

    def body(c_ref, a_ref, b_ref, o_ref):
        o_ref[...] = (a_ref[...] + b_ref[...]).astype(BF16)

    gs = pltpu.PrefetchScalarGridSpec(
        num_scalar_prefetch=1, grid=(N_CHIPS, nh),
        in_specs=[pl.BlockSpec((1, th, D), lambda j, i, c: (j, c[0] * nh + i, 0)), pl.BlockSpec((1, th, D), lambda j, i, c: (j, i, 0))],
        out_specs=pl.BlockSpec((1, th, D), lambda j, i, c: (j, i, 0)))
    return pl.pallas_call(body, name="rs_add_sibling", grid_spec=gs, out_shape=SDS((N_CHIPS, HALF, D), BF16),
                          compiler_params=pltpu.CompilerParams(dimension_semantics=("parallel", "parallel"),
                                                               vmem_limit_bytes=48 << 20))(cidx, gpk, got)


def _rs_chips_start(part, small, after):
    def body(p_ref, s_ref, land_ref, sland_ref, after_ref, send_sems, recv_sems, p_thru, s_thru, land_thru, sland_thru, token):
        x, y, c, chips = _place()
        for k, (cx, cy) in enumerate(chips):
            pltpu.make_async_remote_copy(src_ref=p_ref.at[2 * cx + cy], dst_ref=land_ref.at[2 * x + y], send_sem=send_sems.at[k],
                                         recv_sem=recv_sems.at[k], device_id=(cx, cy, c), device_id_type=MESH).start()
        peers = [(x, y, 1 - c)] + [(cx, cy, c) for cx, cy in chips] + [(cx, cy, 1 - c) for cx, cy in chips]
        for k, to in enumerate(peers):
            pltpu.make_async_remote_copy(src_ref=s_ref, dst_ref=sland_ref.at[4 * x + 2 * y + c], send_sem=send_sems.at[3 + k],
                                         recv_sem=recv_sems.at[3 + k], device_id=to, device_id_type=MESH).start()
        token[...] = jnp.zeros_like(token)

    return pl.pallas_call(
        body, name="rs_chips_start",
        out_shape=(pltpu.SemaphoreType.DMA((10,)), pltpu.SemaphoreType.DMA((10,)), pltpu.HBM(part.shape, part.dtype),
                   pltpu.HBM(small.shape, small.dtype), pltpu.HBM(part.shape, part.dtype), pltpu.HBM((N_DEV, 8, D), F32),
                   SDS((8, LANES), F32)),
        in_specs=(HBM, HBM, HBM, HBM, ANY), out_specs=(SEM, SEM, HBM, HBM, HBM, HBM, pl.BlockSpec(memory_space=pltpu.VMEM)),
        input_output_aliases={0: 2, 1: 3, 2: 4, 3: 5}, compiler_params=pltpu.CompilerParams(has_side_effects=DATAFLOW),
    )(_in_hbm(part), _in_hbm(small), _in_hbm(lax.empty(part.shape, part.dtype)), _in_hbm(lax.empty((N_DEV, 8, D), F32)), after)


def _rs_chips_wait(send_sems, recv_sems, p_thru, s_thru, land_thru, sland_thru, after):
    def body(p_ref, s_ref, land_ref, sland_ref, send_sems, recv_sems, *after_and_outputs):
        x, y, c, chips = _place()
        for k, (cx, cy) in enumerate(chips):
            cp = pltpu.make_async_remote_copy(src_ref=p_ref.at[0], dst_ref=land_ref.at[2 * cx + cy], send_sem=send_sems.at[k],
                                              recv_sem=recv_sems.at[k], device_id=(cx, cy, c), device_id_type=MESH)
            cp.wait_send()
            cp.wait_recv()
        peers = [(x, y, 1 - c)] + [(cx, cy, c) for cx, cy in chips] + [(cx, cy, 1 - c) for cx, cy in chips]
        for k, (px, py, pc) in enumerate(peers):
            cp = pltpu.make_async_remote_copy(src_ref=s_ref, dst_ref=sland_ref.at[4 * px + 2 * py + pc], send_sem=send_sems.at[3 + k],
                                              recv_sem=recv_sems.at[3 + k], device_id=(px, py, pc), device_id_type=MESH)
            cp.wait_send()
            cp.wait_recv()

    hbm = lambda a: pltpu.HBM(a.shape, a.dtype)
    outs = pl.pallas_call(
        body, name="rs_chips_wait", out_shape=(hbm(p_thru), hbm(s_thru), hbm(land_thru), hbm(sland_thru)),
        in_specs=(HBM, HBM, HBM, HBM, SEM, SEM) + (ANY,) * len(after), out_specs=(HBM, HBM, HBM, HBM),
        input_output_aliases={0: 0, 1: 1, 2: 2, 3: 3}, compiler_params=pltpu.CompilerParams(has_side_effects=DATAFLOW),
    )(p_thru, s_thru, land_thru, sland_thru, send_sems, recv_sems, *after)
    return outs[0], outs[2], outs[3]


def _rs_add_chips(qidx, part, parts):
    HALF = part.shape[1]
    th = HALF // 4
    assert th % 16 == 0

    def body(q_ref, own_ref, p_ref, o_ref):
        for me in range(N_CHIPS):
            @pl.when(q_ref[0] == me)
            def _(me=me):
                t = [(own_ref[0] if j == me else p_ref[j]).astype(F32) for j in range(N_CHIPS)]
                o_ref[...] = ((t[0] + t[1]) + t[2]) + t[3]

    gs = pltpu.PrefetchScalarGridSpec(
        num_scalar_prefetch=1, grid=(HALF // th,),
        in_specs=[pl.BlockSpec((1, th, D), lambda i, q: (q[0], i, 0)), pl.BlockSpec((N_CHIPS, th, D), lambda i, q: (0, i, 0))],
        out_specs=pl.BlockSpec((th, D), lambda i, q: (i, 0)))
    return pl.pallas_call(body, name="rs_add_chips", grid_spec=gs, out_shape=SDS((HALF, D), F32),
                          compiler_params=pltpu.CompilerParams(dimension_semantics=("parallel",),
                                                               vmem_limit_bytes=48 << 20))(qidx, part, parts)


def _rs_join(mine, core, name):
    def body(in_ref, out_ref, send_sem, recv_sem):
        x, y, c, _ = _place()
        cp = pltpu.make_async_remote_copy(src_ref=in_ref, dst_ref=out_ref, send_sem=send_sem, recv_sem=recv_sem,
                                          device_id=(x, y, 1 - c), device_id_type=MESH)
        cp.start()
        cp.wait()

    theirs = _pcall(body, name=name, in_specs=[ANY], out_specs=ANY, out_shape=SDS(mine.shape, F32),
                    scratch=[pltpu.SemaphoreType.DMA, pltpu.SemaphoreType.DMA])(mine)
    return jnp.where(core == 0, jnp.concatenate([mine, theirs]), jnp.concatenate([theirs, mine]))


def _reduce_late_start(gpk, after):
    rows = gpk.shape[1]
    HALF = rows // 2
    assert HALF % 16 == 0

    def body(in_ref, land_ref, after_ref, send_sems, recv_sems, in_thru, land_thru, token):
        x, y, c, chips = _place()
        me = 4 * x + 2 * y + c
        peers = [(x, y, 1 - c)] + [(cx, cy, c) for cx, cy in chips] + [(cx, cy, 1 - c) for cx, cy in chips]
        for k, (px, py, pc) in enumerate(peers):
            src = in_ref.at[2 * px + py, pl.ds(pl.multiple_of(pc * HALF, 16), HALF)]
            pltpu.make_async_remote_copy(src_ref=src, dst_ref=land_ref.at[me], send_sem=send_sems.at[k], recv_sem=recv_sems.at[k],
                                         device_id=(px, py, pc), device_id_type=MESH).start()
        token[...] = jnp.zeros_like(token)

    return pl.pallas_call(
        body, name="reduce_late_start",
        out_shape=(pltpu.SemaphoreType.DMA((7,)), pltpu.SemaphoreType.DMA((7,)), pltpu.HBM(gpk.shape, gpk.dtype),
                   pltpu.HBM((N_DEV, HALF, D), gpk.dtype), SDS((8, LANES), F32)),
        in_specs=(HBM, HBM, ANY), out_specs=(SEM, SEM, HBM, HBM, pl.BlockSpec(memory_space=pltpu.VMEM)),
        input_output_aliases={0: 2, 1: 3}, compiler_params=pltpu.CompilerParams(has_side_effects=DATAFLOW),
    )(_in_hbm(gpk), _in_hbm(lax.empty((N_DEV, HALF, D), gpk.dtype)), after)


def _reduce_late_wait(send_sems, recv_sems, in_thru, land_thru, after):
    def body(in_ref, land_ref, send_sems, recv_sems, after_ref, in_out, got_ref):
        x, y, c, chips = _place()
        peers = [(x, y, 1 - c)] + [(cx, cy, c) for cx, cy in chips] + [(cx, cy, 1 - c) for cx, cy in chips]
        for k, (px, py, pc) in enumerate(peers):
            cp = pltpu.make_async_remote_copy(src_ref=land_ref.at[0], dst_ref=land_ref.at[4 * px + 2 * py + pc],
                                              send_sem=send_sems.at[k], recv_sem=recv_sems.at[k],
                                              device_id=(px, py, pc), device_id_type=MESH)
            cp.wait_send()
            cp.wait_recv()

    return pl.pallas_call(
        body, name="reduce_late_wait",
        out_shape=(pltpu.HBM(in_thru.shape, in_thru.dtype), pltpu.HBM(land_thru.shape, land_thru.dtype)),
        in_specs=(HBM, HBM, SEM, SEM, ANY), out_specs=(HBM, HBM), input_output_aliases={0: 0, 1: 1},
        compiler_params=pltpu.CompilerParams(has_side_effects=DATAFLOW),
    )(in_thru, land_thru, send_sems, recv_sems, after)


def _reduce_late_add(didx, gpk, parts):
    HALF = parts.shape[1]
    th = HALF // 4
    nh = HALF // th
    assert th % 16 == 0

    def body(d_ref, own_ref, p_ref, o_ref):
        for me in range(N_DEV):
            @pl.when(d_ref[0] == me)
            def _(me=me):
                t = [(own_ref[0] if j == me else p_ref[j]).astype(F32) for j in range(N_DEV)]
                o_ref[...] = ((((((t[0] + t[1]) + t[2]) + t[3]) + t[4]) + t[5]) + t[6]) + t[7]

    gs = pltpu.PrefetchScalarGridSpec(
        num_scalar_prefetch=1, grid=(nh,),
        in_specs=[pl.BlockSpec((1, th, D), lambda i, d: (d[1], d[2] * nh + i, 0)), pl.BlockSpec((N_DEV, th, D), lambda i, d: (0, i, 0))],
        out_specs=pl.BlockSpec((th, D), lambda i, d: (i, 0)))
    return pl.pallas_call(body, name="reduce_late_add", grid_spec=gs, out_shape=SDS((HALF, D), F32),
                          compiler_params=pltpu.CompilerParams(dimension_semantics=("parallel",),
                                                               vmem_limit_bytes=48 << 20))(didx, gpk, parts)


def _pack_early(b, dtype):
    lanes = lambda a: jnp.pad(a.astype(dtype), ((0, 0), (0, D - a.shape[1])))
    pair = jnp.concatenate([b["w_q_b"].astype(dtype), b["w_ple"].astype(dtype), jnp.zeros((256, D - 640), dtype)], axis=1)
    return jnp.concatenate([lanes(b["w_in"]), pair, lanes(b["w_kv_b"])], axis=0)


def _pack_late(b, dtype):
    return jnp.concatenate([b[n].astype(dtype) for n in ("w_mla_up", "w_swa_up", "w_out", "w_ple_gate", "w_mlp_up", "w_mlp_down")],
                           axis=0)


def _unpack_shards(pk, which):
    return {n: pk[PACK_AT[n][1]:PACK_AT[n][1] + r, PACK_AT[n][2]:PACK_AT[n][2] + c] for n, r, c in BIG if PACK_AT[n][0] == which}


def _full_weights(gathered, own, chip, which):
    own_b = _unpack_shards(own, which)
    per_chip = [{n: jnp.where(chip == j, own_b[n], blk) for n, blk in _unpack_shards(gathered[j], which).items()}
                for j in range(N_CHIPS)]
    out = {}
    for n in own_b:
        shards = [pc[n] for pc in per_chip]
        if n == "w_in":
            out["w_in_p"] = _w_in_internal(shards)
        else:
            out[n] = jnp.concatenate(shards, axis=1 if n in COL_SHARDED else 0)
    return out


def _split_full_grads(grads, pack, dtype):
    shard = {n: (r, c) for n, r, c in BIG}
    chunks = []
    for j in range(N_CHIPS):
        blocks = {}
        for n, g in grads.items():
            if n == "w_in_p":
                blocks["w_in"] = _w_in_grad_shard(g, j)
                continue
            r, c = shard[n]
            blocks[n] = g[:, j * c:(j + 1) * c] if n in COL_SHARDED else g[j * r:(j + 1) * r]
        chunks.append(pack(blocks, dtype))
    return jnp.stack(chunks)


W_IN_SHARD = 936
W_IN_SEGMENTS = ((0, 256, (3072,)), (256, 384, (3840,)), (384, 416, (4032,)), (416, 1440, (0,)), (1440, 1504, (3328, 3392)),
                 (1504, 1568, (3456, 3520)), (1568, 1632, (3584, 3648)), (1632, 1696, (3712, 3776)), (1696, 3744, (1024,)))


def _w_in_internal(shards):
    def cols(a, b):
        out = []
        for j, s in enumerate(shards):
            lo, hi = max(a, W_IN_SHARD * j), min(b, W_IN_SHARD * (j + 1))
            if lo < hi:
                out.append(s[:, lo - W_IN_SHARD * j:hi - W_IN_SHARD * j])
        return out

    pieces = {}
    for a, b, places in W_IN_SEGMENTS:
        for at in places:
            pieces[at] = cols(a, b)
    zeros = lambda n: [jnp.zeros((D, n), shards[0].dtype)]
    pieces[3968] = zeros(64)
    pieces[4064] = zeros(32)
    return jnp.concatenate([piece for at in sorted(pieces) for piece in pieces[at]], axis=1)


def _w_in_grad_shard(g, j):
    def internal(a, b):
        out = []
        while a < b:
            end = min(b, (a // D + 1) * D)
            out.append(g[a // D][:, a % D:a % D + end - a])
            a = end
        return out

    out = []
    for a, b, places in W_IN_SEGMENTS:
        lo, hi = max(a, W_IN_SHARD * j), min(b, W_IN_SHARD * (j + 1))
        if lo < hi:
            parts = [internal(at + lo - a, at + hi - a) for at in places]
            if len(parts) == 1:
                out += parts[0]
            else:
                assert len(parts[0]) == len(parts[1]) == 1
                out.append(parts[0][0] + parts[1][0])
    return jnp.concatenate(out, axis=1)


def _local_step(x, p, tgt, w, small, late_weights, late_grads_out):
    T = x.shape[0]
    tm = 256
    tb = 256
    w_in_p = w["w_in_p"]
    wqb = jnp.pad(w["w_q_b"].reshape(Q_LORA, MLA_HEADS, 96), ((0, 0), (0, 0), (0, 32))).reshape(Q_LORA, 2048)
    wkv = w["w_kv_b"].reshape(KV_LORA, MLA_HEADS, 128)
    wkn = jnp.pad(wkv[:, :, :64], ((0, 0), (0, 0), (0, 64))).reshape(KV_LORA, 2048)
    wv = wkv[:, :, 64:].reshape(KV_LORA, 1024)
    tab_m = _rope_tables(T, "mla")
    tab_s = _rope_tables(T, "swa")
    g1, gq, gkv, sinks = small["g_mix_pre"], small["g_q_a"], small["g_kv_a"], small["sinks"]
    g2, g3, g4, g5 = small["g_mix_post"], small["g_mlp_pre"], small["g_mlp_post"], small["g_ple"]
    sink_vec = sinks.reshape(SWA_HEADS)

    z, h1 = _fwd_in(x, g1, w_in_p, tm)
    qn, kvn, qm, km, vm, qt, kt, vt, qs, ks, vs = _fwd_qkv(z, gq, gkv, wqb, wkn, wv, tab_m, tab_s, tb)
    om, lse_m = _mla_fwd(qm, km, vt, tb)
    os_, lse_s = _swa_fwd(sink_vec, qs, ks, vs)
    w = {**w, **late_weights((om, os_))}
    y, yo, au, bu, x1 = _fwd_mix(om, os_, z, x, w["w_mla_up"], w["w_swa_up"], w["w_out"], g2, tm)
    h2, u = _fwd_mlp_up(x1, g3, w["w_mlp_up"], tm)
    d, x2 = _fwd_mlp_down(u, w["w_mlp_down"], x1, g4, tm)
    loss, dx2, dgt, de0, dg5 = _ple_fwd_bwd(p, x2, tgt, w["w_ple"], g5, w["w_ple_gate"], tm)

    dd, da, dg4 = _bwd_mlp_down(dx2, d, g4, w["w_mlp_down"], u, tm)
    dx1, dg3 = _bwd_mlp_up(da, w["w_mlp_up"], x1, g3, dx2, tm)
    dyo, dg2, dau, dbu, dga, dgb, dos, delta_m, dom_t = _bwd_mix(dx1, yo, g2, w["w_out"], z, au, bu, w["w_mla_up"],
                                                                w["w_swa_up"], om, tb)
    gpk_late = lax.empty((N_CHIPS, PACK_ROWS["late"], D), BF16)
    for weight, a_, g_ in (("w_mla_up", om, dau), ("w_swa_up", os_, dbu), ("w_out", y, dyo), ("w_ple_gate", x2, dgt),
                           ("w_mlp_up", h2, da), ("w_mlp_down", u, dd)):
        gpk_late = _wgrad(a_, g_, "wgrad_" + weight[2:], into=(gpk_late, weight))
    token = late_grads_out(gpk_late)
    delta_m = delta_m + token[0, 0]
    dqm, dkm, dvm = _mla_bwd(qm, qt, km, kt, vm, dom_t, lse_m, delta_m, tb)
    dqs, dkc, dkp, dvc, dvp, dsink = _swa_bwd(sink_vec, qs, ks, vs, dos, os_, lse_s)
    dqb, dknb, dvb, dsq, drest, dgq, dgkv = _bwd_qkv(dqm, dkm, dvm, dqs, dkc, dkp, dvc, dvp, z, gq, gkv, wqb, wkn, wv,
                                                      tab_m, tab_s)
    gx, dg1 = _bwd_in(dsq, dga, dgb, drest, w_in_p, x, g1, dx1, tm)

    g_in_p = [_wgrad(h1, dsq, "wgrad_in_sq"), _wgrad(h1, dga, "wgrad_in_ga"), _wgrad(h1, dgb, "wgrad_in_gb"),
              _wgrad(h1, drest, "wgrad_in_rest")]
    g_qb_p = _wgrad(qn, dqb, "wgrad_q_b")
    g_kn_p = _wgrad(kvn, dknb, "wgrad_kv_b_nope")
    g_v_p = _wgrad(kvn, dvb, "wgrad_kv_b_v")
    grads = {
        "w_in_p": g_in_p,
        "w_q_b": g_qb_p.reshape(Q_LORA, MLA_HEADS, 128)[:, :, :96].reshape(Q_LORA, 1536),
        "w_kv_b": jnp.concatenate([g_kn_p.reshape(KV_LORA, MLA_HEADS, 128)[:, :, :64], g_v_p.reshape(KV_LORA, MLA_HEADS, 64)],
                                  axis=2).reshape(KV_LORA, 2048),
        "w_ple": _wgrad(p, de0, "wgrad_ple"),
    }
    small_grads = {"g_mix_pre": dg1, "g_q_a": dgq, "g_kv_a": dgkv, "sinks": dsink[0:1, 0:SWA_HEADS], "g_mix_post": dg2,
                   "g_mlp_pre": dg3, "g_mlp_post": dg4, "g_ple": dg5}
    return loss, gx, grads, small_grads


def _pack_small(vals, fill, scalar=None):
    wide = [vals[n] for n, k in SMALL if k == D]
    narrow = [vals[n] for n, k in SMALL if k != D]
    used = sum(k for _, k in SMALL if k != D)
    last = jnp.concatenate(narrow + [jnp.full((1, D - used), fill, F32)], axis=1)
    rest = jnp.full((2, D), fill, F32)
    if scalar is not None:
        rest = jnp.concatenate([jnp.concatenate([scalar, rest[0:1, 1:]], axis=1), rest[1:2]], axis=0)
    return jnp.concatenate(wide + [last, rest], axis=0)


def _unpack_small(pk):
    out, row, off = {}, 0, 0
    for n, k in SMALL:
        if k == D:
            out[n] = pk[row:row + 1]
            row += 1
    for n, k in SMALL:
        if k != D:
            out[n] = pk[5:6, off:off + k]
            off += k
    return out


def kernel(x, p, g_mix_pre, w_in, g_q_a, w_q_b, g_kv_a, w_kv_b, sinks, w_mla_up, w_swa_up, w_out, g_mix_post, g_mlp_pre, w_mlp_up, w_mlp_down, g_mlp_post, w_ple, g_ple, w_ple_gate, loss_target, m_g_mix_pre, m_w_in, m_g_q_a, m_w_q_b, m_g_kv_a, m_w_kv_b, m_sinks, m_w_mla_up, m_w_swa_up, m_w_out, m_g_mix_post, m_g_mlp_pre, m_w_mlp_up, m_w_mlp_down, m_g_mlp_post, m_w_ple, m_g_ple, m_w_ple_gate, v_g_mix_pre, v_w_in, v_g_q_a, v_w_q_b, v_g_kv_a, v_w_kv_b, v_sinks, v_w_mla_up, v_w_swa_up, v_w_out, v_g_mix_post, v_g_mlp_pre, v_w_mlp_up, v_w_mlp_down, v_g_mlp_post, v_w_ple, v_g_ple, v_w_ple_gate):
    given = dict(locals())
    big_w = {n: given[n][0] for n, _, _ in BIG}
    small_w = {n: given[n] for n, _ in SMALL}
    small_m = {n: given["m_" + n] for n, _ in SMALL}
    small_v = {n: given["v_" + n] for n, _ in SMALL}

    core = lax.axis_index("c")
    chip = 2 * lax.axis_index("x") + lax.axis_index("y")
    core_i = core.astype(jnp.int32).reshape(1)
    chip_i = chip.astype(jnp.int32).reshape(1)
    dev_i = jnp.stack([2 * chip + core, chip, core]).astype(jnp.int32)

    own_early = _pack_early(big_w, BF16)
    own_late = _pack_late(big_w, BF16)
    got_early = _all_gather(own_early)
    late_flight = _gather_late_start(own_late, got_early)
    weights = _full_weights(got_early, own_early, chip, "early")
    step_small = {**small_w, "g_mix_pre": small_w["g_mix_pre"] + late_flight[4][0, 0]}

    def late_weights(after):
        return _full_weights(_gather_late_wait(*late_flight[:4], after), own_late, chip, "late")

    flight = {}

    def late_grads_out(gpk_late):
        flight["late"] = _reduce_late_start(gpk_late, dev_i)
        return flight["late"][4]

    loss_blk, gx, grads, small_grads = _local_step(x[0], p[0, 0], loss_target[0], weights, step_small, late_weights,
                                                   late_grads_out)

    gpk = _split_full_grads(grads, _pack_early, F32)
    got = _rs_sibling(gpk)
    part = _rs_add_sibling(core_i, gpk, got)
    small_own = _pack_small(small_grads, 0.0, loss_blk[0:1, 0:1])
    early_flight = _rs_chips_start(part, small_own, dev_i)

    out_g, out_d, out_m, out_v = {}, {}, {}, {}
    gpk_late, parts_late = _reduce_late_wait(*flight["late"][:4], early_flight[6])
    joined_late = _rs_join(_reduce_late_add(dev_i, gpk_late, parts_late), core, "rs_join_late")
    for n, _, _ in BIG:
        if PACK_AT[n][0] == "late":
            out_g[n], out_d[n], out_m[n], out_v[n] = _adamw(given[n], joined_late, given["m_" + n], given["v_" + n], n)

    part, parts, small_parts = _rs_chips_wait(*early_flight[:6], [out_d[n] for n in out_d])
    joined_early = _rs_join(_rs_add_chips(chip_i, part, parts), core, "rs_join_early")
    for n, _, _ in BIG:
        if PACK_AT[n][0] == "early":
            out_g[n], out_d[n], out_m[n], out_v[n] = _adamw(given[n], joined_early, given["m_" + n], given["v_" + n], n)

    mine = (lax.broadcasted_iota(jnp.int32, (N_DEV, 1, 1), 0) == dev_i[0])
    g_small_pk, d_small_pk, m_small_pk, v_small_pk = _adamw_small(
        _pack_small(small_w, 0.0), jnp.where(mine, small_own[None], small_parts), _pack_small(small_m, 0.0),
        _pack_small(small_v, 1.0))
    loss = g_small_pk[6, 0]
    for out, pk in ((out_g, g_small_pk), (out_d, d_small_pk), (out_m, m_small_pk), (out_v, v_small_pk)):
        out.update(_unpack_small(pk))
    order = ["g_mix_pre", "w_in", "g_q_a", "w_q_b", "g_kv_a", "w_kv_b", "sinks", "w_mla_up", "w_swa_up", "w_out", "g_mix_post",
             "g_mlp_pre", "w_mlp_up", "w_mlp_down", "g_mlp_post", "w_ple", "g_ple", "w_ple_gate"]
    return (loss, gx[None], *[out_g[n] for n in order], *[out_d[n] for n in order], *[out_m[n] for n in order],
            *[out_v[n] for n in order])
```

```python
import math

import jax
import jax.numpy as jnp
from jax import lax
from jax.experimental import pallas as pl
from jax.experimental.pallas import tpu as pltpu

F32 = jnp.float32
BF16 = jnp.bfloat16
SDS = jax.ShapeDtypeStruct

D = 1024
D_FF = 4096
PLE = 256
Q_LORA = 256
KV_LORA = 128
MLA_HEADS = 16
MLA_NOPE = 64
MLA_ROPE = 32
SWA_HEADS = 16
SWA_HD = 64
WINDOW = 128
ROPE_THETA = 10000.0
EPS = 1e-6
NEG = -1e30
NZ = 4096
MLA_SCALE = (MLA_NOPE + MLA_ROPE) ** -0.5
LOG2_E = math.log2(math.e)
MLA_LOG2_SCALE = MLA_SCALE * LOG2_E
SWA_SCALE = SWA_HD ** -0.5

ADAM_LR = 0.001
ADAM_B1 = 0.9
ADAM_B2 = 0.999
ADAM_EPS = 1e-08
ADAM_WD = 0.01
ADAM_STEP = 10

LANES = 128
ATT_COLS = 128
N_CHIPS = 4
N_DEV = 8
MESH = pl.DeviceIdType.MESH

NT = (((1,), (1,)), ((), ()))
TN = (((0,), (0,)), ((), ()))

BIG = (("w_in", 1024, 936), ("w_q_b", 256, 384), ("w_kv_b", 128, 512), ("w_mla_up", 256, 1024),
       ("w_swa_up", 256, 1024), ("w_out", 256, 1024), ("w_mlp_up", 1024, 1024), ("w_mlp_down", 1024, 1024),
       ("w_ple", 256, 256), ("w_ple_gate", 256, 1024))
COL_SHARDED = ("w_in", "w_q_b", "w_kv_b", "w_mlp_up", "w_ple")
PACK_AT = {"w_in": ("early", 0, 0), "w_q_b": ("early", 1024, 0), "w_ple": ("early", 1024, 384), "w_kv_b": ("early", 1280, 0),
           "w_mla_up": ("late", 0, 0), "w_swa_up": ("late", 256, 0), "w_out": ("late", 512, 0), "w_ple_gate": ("late", 768, 0),
           "w_mlp_up": ("late", 1024, 0), "w_mlp_down": ("late", 2048, 0)}
PACK_ROWS = {"early": 1408, "late": 3072}
SMALL = (("g_mix_pre", 1024), ("g_q_a", 256), ("g_kv_a", 128), ("sinks", 16), ("g_mix_post", 1024),
         ("g_mlp_pre", 1024), ("g_mlp_post", 1024), ("g_ple", 1024))


def _dot(a, b):
    return jnp.dot(a, b, preferred_element_type=F32)


def _dot_nt(a, b):
    return lax.dot_general(a, b, NT, preferred_element_type=F32)


def _dot_tn(a, b):
    return lax.dot_general(a, b, TN, preferred_element_type=F32)


def _pcall(body, *, name, out_shape, grid=(), in_specs=None, out_specs=None, scratch=(), sem=None, vmem_mb=48, aliases=None):
    params = dict(vmem_limit_bytes=vmem_mb << 20)
    if sem is not None:
        params["dimension_semantics"] = sem
    return pl.pallas_call(body, name=name, grid=grid, in_specs=in_specs, out_specs=out_specs, out_shape=out_shape,
                          scratch_shapes=list(scratch), input_output_aliases=aliases or {},
                          compiler_params=pltpu.CompilerParams(**params))


def _rows(tm, n, col=0):
    return pl.BlockSpec((tm, n), lambda i: (i, col))


def _full(shape):
    return pl.BlockSpec(shape, lambda i: (0,) * len(shape))


def _rms(x, g):
    r = lax.rsqrt(jnp.mean(x * x, axis=-1, keepdims=True) + EPS)
    return x * r * g


def _rms_bwd(dy, x, g):
    r = lax.rsqrt(jnp.mean(x * x, axis=-1, keepdims=True) + EPS)
    xn = x * r
    dn = dy * g
    dx = r * (dn - xn * jnp.mean(dn * xn, axis=-1, keepdims=True))
    return dx, jnp.sum(dy * xn, axis=0, keepdims=True)


def _sigmoid(x):
    return 1.0 / (1.0 + jnp.exp(-x))


def _rope(x, c, a, b, half):
    return x * c + pltpu.roll(x, LANES - half, 1) * a + pltpu.roll(x, half, 1) * b


def _rope_tables(T, kind):
    lane = jnp.arange(LANES)
    if kind == "mla":
        half = MLA_ROPE // 2
        rel = lane - MLA_NOPE
        on = (rel >= 0) & (rel < MLA_ROPE)
        d = MLA_ROPE
    else:
        half = SWA_HD // 2
        rel = lane % SWA_HD
        on = jnp.ones((LANES,), bool)
        d = SWA_HD
    first = on & (rel < half)
    second = on & (rel >= half)
    f = jnp.where(first, rel, rel - half).astype(F32)
    inv = jnp.exp(-math.log(ROPE_THETA) * f * (2.0 / d))
    ang = jnp.arange(T, dtype=F32)[:, None] * inv[None, :]
    cos, sin = jnp.cos(ang), jnp.sin(ang)
    c = jnp.where(on[None], cos, 1.0)
    a = jnp.where(first[None], -sin, 0.0)
    b = jnp.where(second[None], sin, 0.0)
    return c, a, b


def _fwd_in(x, g1, w_in_p, tm):
    T = x.shape[0]

    def body(x_ref, g_ref, w_ref, z_ref, h_ref):
        h = _rms(x_ref[...], g_ref[...]).astype(BF16)
        h_ref[...] = h
        z_ref[...] = _dot(h, w_ref[...])

    return _pcall(body, name="fwd_in", grid=(T // tm,),
                  in_specs=[_rows(tm, D), _full((1, D)), _full((D, NZ))],
                  out_specs=[_rows(tm, NZ), _rows(tm, D)],
                  out_shape=[SDS((T, NZ), F32), SDS((T, D), BF16)], sem=("parallel",))(x, g1, w_in_p)


def _fwd_qkv(z, gq, gkv, wqb, wkn, wv, tab_m, tab_s, tm):
    T = z.shape[0]

    def body(qa_ref, sq_ref, skd_ref, svd_ref, kva_ref, kr_ref, gq_ref, gkv_ref, wqb_ref, wkn_ref, wv_ref,
             cm_ref, am_ref, bm_ref, cs_ref, as_ref, bs_ref,
             qn_ref, kvn_ref, km_ref, vm_ref, qt_ref, kt_ref, vt_ref, qs_ref, ks_ref, vs_ref):
        qn = _rms(qa_ref[...], gq_ref[...]).astype(BF16)
        qn_ref[...] = qn
        kvn = _rms(kva_ref[...], gkv_ref[...]).astype(BF16)
        kvn_ref[...] = kvn
        cm, am, bm = cm_ref[...], am_ref[...], bm_ref[...]
        cs, as_, bs = cs_ref[...], as_ref[...], bs_ref[...]
        k_rope = _rope(kr_ref[...], cm, am, bm, MLA_ROPE // 2)
        vt_row = lax.broadcasted_iota(jnp.int32, (LANES, tm), 0)
        v_all = _dot(kvn, wv_ref[...])
        q_all = _dot(qn, wqb_ref[...])
        k_all = _dot(kvn, wkn_ref[...])
        for j in range(D // LANES):
            sl = slice(LANES * j, LANES * (j + 1))
            v = v_all[:, sl]
            vm_ref[:, sl] = v.astype(BF16)
            v_t = v.T
            for hh, rows64 in enumerate((v_t, pltpu.roll(v_t, 64, 0))):
                blk = jnp.where(vt_row < 64, rows64, jnp.where(vt_row == 64, 1.0, 0.0))
                vt_ref[0, LANES * (2 * j + hh):LANES * (2 * j + hh + 1), :] = blk.astype(BF16)
        for h in range(MLA_HEADS):
            sl = slice(LANES * h, LANES * (h + 1))
            qh = _rope(q_all[:, sl], cm, am, bm, MLA_ROPE // 2)
            qt_ref[0, sl, :] = qh.T.astype(BF16)
            k = k_all[:, sl] + k_rope
            km_ref[:, sl] = k.astype(BF16)
            kt_ref[0, sl, :] = k.T.astype(BF16)
        for j in range(D // LANES):
            sl = slice(LANES * j, LANES * (j + 1))
            qs_ref[:, sl] = _rope(sq_ref[:, sl], cs, as_, bs, SWA_HD // 2).astype(BF16)
        for j in range(2):
            sl = slice(LANES * j, LANES * (j + 1))
            ks_ref[:, sl] = _rope(skd_ref[:, sl], cs, as_, bs, SWA_HD // 2).astype(BF16)
        vs_ref[...] = svd_ref[...].astype(BF16)

    tab = [_rows(tm, LANES)] * 6
    return _pcall(body, name="fwd_qkv", grid=(T // tm,),
                  in_specs=[_rows(tm, 256, 12), _rows(tm, 1024, 0), _rows(tm, 256, 13), _rows(tm, 256, 14),
                            _rows(tm, 128, 30), _rows(tm, 128, 31), _full((1, Q_LORA)), _full((1, KV_LORA)),
                            _full((Q_LORA, 2048)), _full((KV_LORA, 2048)), _full((KV_LORA, 1024))] + tab,
                  out_specs=[_rows(tm, Q_LORA), _rows(tm, KV_LORA), _rows(tm, 2048), _rows(tm, 1024),
                             pl.BlockSpec((1, 2048, tm), lambda i: (i, 0, 0)), pl.BlockSpec((1, 2048, tm), lambda i: (i, 0, 0)),
                             pl.BlockSpec((1, 2048, tm), lambda i: (i, 0, 0)),
                             _rows(tm, 1024), _rows(tm, 256), _rows(tm, 256)],
                  out_shape=[SDS((T, Q_LORA), BF16), SDS((T, KV_LORA), BF16), SDS((T, 2048), BF16),
                             SDS((T, 1024), BF16), SDS((T // tm, 2048, tm), BF16), SDS((T // tm, 2048, tm), BF16),
                             SDS((T // tm, 2048, tm), BF16),
                             SDS((T, 1024), BF16), SDS((T, 256), BF16), SDS((T, 256), BF16)],
                  sem=("parallel",))(z, z, z, z, z, z, gq, gkv, wqb, wkn, wv, *tab_m, *tab_s)


def _mla_fwd(qt, km, vt, tb):
    T = km.shape[0]
    nb = T // tb
    cc = ATT_COLS

    def body(q_ref, k_ref, vt_ref, o_ref, l_ref, s_ref, p_ref, al_ref, m_ref, acc_ref):
        i = pl.program_id(1)
        m_ref[...] = jnp.full(m_ref.shape, NEG, F32)
        acc_ref[...] = jnp.zeros_like(acc_ref)
        p_ref[1] = jnp.zeros(p_ref.shape[1:], BF16)
        al_ref[1] = jnp.ones(al_ref.shape[1:], F32)
        key = lax.broadcasted_iota(jnp.int32, (tb, cc), 0)
        qry = lax.broadcasted_iota(jnp.int32, (tb, cc), 1)

        def scores(j, slot):
            off = pl.multiple_of(j * tb, tb)
            for hh in range(2):
                sl = slice(LANES * hh, LANES * (hh + 1))
                s_ref[slot, hh] = _dot(k_ref[pl.ds(off, tb), sl], q_ref[0, sl, :])

        def softmax(slot, diagonal):
            chains = [(hh, slice(cc * c, cc * (c + 1)), c) for hh in range(2) for c in range(tb // cc)]

            def scaled(hh, cols, c):
                t = s_ref[slot, hh, :, cols] * MLA_LOG2_SCALE
                return jnp.where(key <= qry + cc * c, t, NEG) if diagonal else t

            tops = []
            for hh, cols, c in chains:
                if diagonal:
                    top = jnp.max(scaled(hh, cols, c), axis=0, keepdims=True)
                else:
                    top = jnp.max(s_ref[slot, hh, :, cols], axis=0, keepdims=True) * MLA_LOG2_SCALE
                m_old = m_ref[hh, :, cols]
                mn = jnp.maximum(m_old, top)
                m_ref[hh, :, cols] = mn
                al_ref[slot, hh, :, cols] = jnp.exp2(m_old - mn)
                tops.append(mn)
            for (hh, cols, c), mn in zip(chains, tops):
                p_ref[slot, hh, :, cols] = jnp.exp2(scaled(hh, cols, c) - mn).astype(BF16)

        def accumulate(j, slot):
            for hh in range(2):
                acc_ref[hh] = al_ref[slot, hh] * acc_ref[hh] + _dot(vt_ref[j, LANES * hh:LANES * (hh + 1), :], p_ref[slot, hh])

        def step(t, carry):
            scores(2 * t + 1, 1)
            accumulate(jnp.maximum(2 * t - 1, 0), 1)
            softmax(0, False)
            scores(2 * t + 2, 0)
            accumulate(2 * t, 0)
            softmax(1, False)
            return carry

        scores(0, 0)
        lax.fori_loop(0, i // 2, step, 0)

        @pl.when(i % 2 == 1)
        def _():
            scores(i, 1)
            accumulate(jnp.maximum(i - 2, 0), 1)
            softmax(0, False)
            accumulate(i - 1, 0)
            softmax(1, True)
            accumulate(i, 1)

        @pl.when(i % 2 == 0)
        def _():
            accumulate(jnp.maximum(i - 1, 0), 1)
            softmax(0, True)
            accumulate(i, 0)
        den = [acc_ref[hh, 64:65, :] for hh in range(2)]
        o_ref[...] = jnp.concatenate([acc_ref[hh, 0:64, :] / den[hh] for hh in range(2)], axis=0).T
        sub = lax.broadcasted_iota(jnp.int32, (8, tb), 0)
        lse = [m_ref[hh] + jnp.log(den[hh]) * LOG2_E for hh in range(2)]
        l_ref[0, 0] = jnp.where(sub == 0, lse[0], jnp.where(sub == 1, lse[1], 0.0))

    return _pcall(body, name="mla_fwd", grid=(MLA_HEADS // 2, nb),
                  in_specs=[pl.BlockSpec((1, 256, tb), lambda p, i: (i, p, 0)), pl.BlockSpec((T, 256), lambda p, i: (0, p)),
                            pl.BlockSpec((nb, 2 * LANES, tb), lambda p, i: (0, p, 0))],
                  out_specs=[pl.BlockSpec((tb, LANES), lambda p, i: (i, p)),
                             pl.BlockSpec((1, 1, 8, tb), lambda p, i: (p, i, 0, 0))],
                  out_shape=[SDS((T, D), F32), SDS((MLA_HEADS // 2, nb, 8, tb), F32)],
                  scratch=[pltpu.VMEM((2, 2, tb, tb), F32), pltpu.VMEM((2, 2, tb, tb), BF16), pltpu.VMEM((2, 2, 1, tb), F32),
                           pltpu.VMEM((2, 1, tb), F32), pltpu.VMEM((2, LANES, tb), F32)],
                  sem=("parallel", "arbitrary"))(qt, km, vt)


def _swa_mask(n):
    row = lax.broadcasted_iota(jnp.int32, (WINDOW, 2 * WINDOW), 0)
    col = lax.broadcasted_iota(jnp.int32, (WINDOW, 2 * WINDOW), 1)
    rel = row - col + WINDOW
    return (rel >= 0) & (rel < WINDOW) & ((col >= WINDOW) | (n > 0))


def _swa_specs(T):
    nb = T // WINDOW
    cur = lambda w: pl.BlockSpec((WINDOW, w), lambda n: (n, 0))
    prev = lambda w: pl.BlockSpec((WINDOW, w), lambda n: (jnp.maximum(n - 1, 0), 0))
    return nb, cur, prev


def _swa_fwd(sinks, qs, ks, vs):
    T = qs.shape[0]
    nb, cur, prev = _swa_specs(T)

    def body(sink_ref, q_ref, kc_ref, kp_ref, vc_ref, vp_ref, o_ref, l_ref, kb_ref, vb_ref, s_ref, p_ref):
        n = pl.program_id(0)
        mask = _swa_mask(n)
        lo = lax.broadcasted_iota(jnp.int32, (WINDOW, LANES), 1) < 64
        hi = jnp.logical_not(lo)
        for g in range(2):
            gs = slice(LANES * g, LANES * (g + 1))
            kb_ref[g] = jnp.concatenate([kp_ref[:, gs], kc_ref[:, gs]], axis=0)
            vb_ref[g] = jnp.concatenate([vp_ref[:, gs], vc_ref[:, gs]], axis=0)
        for h in range(SWA_HEADS):
            qp = q_ref[:, LANES * (h // 2):LANES * (h // 2 + 1)]
            qh = jnp.where(lo if h % 2 == 0 else hi, qp, jnp.zeros_like(qp))
            s_ref[h] = _dot_nt(qh, kb_ref[h // 8])
        for j in range(SWA_HEADS // 2):
            sl = slice(LANES * j, LANES * (j + 1))
            lses = []
            for h in (2 * j, 2 * j + 1):
                s = jnp.where(mask, s_ref[h] * SWA_SCALE, NEG)
                sk = sink_ref[h]
                m = jnp.maximum(jnp.max(s, axis=1, keepdims=True), sk)
                e = jnp.exp(s - m)
                den = jnp.sum(e, axis=1, keepdims=True) + jnp.exp(sk - m)
                p_ref[h] = (e / den).astype(BF16)
                lses.append(jnp.broadcast_to(m + jnp.log(den), (WINDOW, LANES)))
            l_ref[:, sl] = jnp.where(lo, lses[0], lses[1])
        for j in range(SWA_HEADS // 2):
            vb = vb_ref[j // 4]
            o_ref[:, LANES * j:LANES * (j + 1)] = jnp.where(lo, _dot(p_ref[2 * j], vb), _dot(p_ref[2 * j + 1], vb))

    return _pcall(body, name="swa_fwd", grid=(nb,),
                  in_specs=[pl.BlockSpec(memory_space=pltpu.SMEM), cur(D), cur(256), prev(256), cur(256), prev(256)],
                  out_specs=[cur(D), cur(D)], out_shape=[SDS((T, D), F32)] * 2,
                  scratch=[pltpu.VMEM((2, 2 * WINDOW, LANES), BF16), pltpu.VMEM((2, 2 * WINDOW, LANES), BF16),
                           pltpu.VMEM((SWA_HEADS, WINDOW, 2 * WINDOW), F32), pltpu.VMEM((SWA_HEADS, WINDOW, 2 * WINDOW), BF16)],
                  sem=("parallel",))(sinks, qs, ks, ks, vs, vs)


def _fwd_mix(om, os_, z, x, wmu, wsu, wo, g2, tm):
    T = x.shape[0]

    def body(om_ref, os_ref, ga_ref, gb_ref, x_ref, wmu_ref, wsu_ref, wo_ref, g2_ref,
             y_ref, yo_ref, au_ref, bu_ref, x1_ref):
        au = _dot(om_ref[...].astype(BF16), wmu_ref[...])
        bu = _dot(os_ref[...].astype(BF16), wsu_ref[...])
        au_ref[...] = au
        bu_ref[...] = bu
        y = (_sigmoid(ga_ref[...]) * au + _sigmoid(gb_ref[...]) * bu).astype(BF16)
        y_ref[...] = y
        yo = _dot(y, wo_ref[...])
        yo_ref[...] = yo
        x1_ref[...] = x_ref[...] + _rms(yo, g2_ref[...])

    r = _rows(tm, D)
    w = _full((D, D))
    return _pcall(body, name="fwd_mix", grid=(T // tm,),
                  in_specs=[r, r, _rows(tm, D, 1), _rows(tm, D, 2), r, w, w, w, _full((1, D))],
                  out_specs=[r] * 5,
                  out_shape=[SDS((T, D), BF16), SDS((T, D), F32), SDS((T, D), F32), SDS((T, D), F32), SDS((T, D), F32)],
                  sem=("parallel",))(om, os_, z, z, x, wmu, wsu, wo, g2)


def _fwd_mlp_up(x1, g3, w1, tm):
    T = x1.shape[0]

    def body(x_ref, g_ref, w_ref, h_ref, u_ref):
        h = _rms(x_ref[...], g_ref[...]).astype(BF16)
        h_ref[...] = h
        u_ref[...] = jnp.square(jnp.maximum(_dot(h, w_ref[...]), 0.0)).astype(BF16)

    return _pcall(body, name="fwd_mlp_up", grid=(T // tm,),
                  in_specs=[_rows(tm, D), _full((1, D)), _full((D, D_FF))],
                  out_specs=[_rows(tm, D), _rows(tm, D_FF)],
                  out_shape=[SDS((T, D), BF16), SDS((T, D_FF), BF16)],
                  sem=("parallel",))(x1, g3, w1)


def _fwd_mlp_down(u, w2, x1, g4, tm):
    T = x1.shape[0]

    def body(u_ref, w_ref, x_ref, g_ref, d_ref, x2_ref):
        d = _dot(u_ref[...], w_ref[...])
        d_ref[...] = d
        x2_ref[...] = x_ref[...] + _rms(d, g_ref[...])

    return _pcall(body, name="fwd_mlp_down", grid=(T // tm,),
                  in_specs=[_rows(tm, D_FF), _full((D_FF, D)), _rows(tm, D), _full((1, D))],
                  out_specs=[_rows(tm, D), _rows(tm, D)], out_shape=[SDS((T, D), F32)] * 2,
                  sem=("parallel",))(u, w2, x1, g4)


def _ple_fwd_bwd(p, x2, tgt, wple, g5, wpg, tm):
    T = x2.shape[0]

    def body(p_ref, x2_ref, t_ref, wple_ref, g5_ref, wpg_ref, loss_ref, dx2_ref, dgt_ref, de0_ref, dg5_ref):
        @pl.when(pl.program_id(0) == 0)
        def _():
            loss_ref[...] = jnp.zeros_like(loss_ref)
            dg5_ref[...] = jnp.zeros_like(dg5_ref)

        e0 = _dot(p_ref[...].astype(BF16), wple_ref[...])
        g5 = g5_ref[...]
        r = lax.rsqrt(jnp.mean(e0 * e0, axis=-1, keepdims=True) + EPS)
        en = e0 * r
        e = en * g5
        x2 = x2_ref[...]
        s = _sigmoid(_dot(x2.astype(BF16), wpg_ref[...]))
        diff = x2 + s * e - t_ref[...]
        sq = jnp.sum(jnp.sum(diff * diff, axis=1, keepdims=True), axis=0, keepdims=True)
        loss_ref[...] += jnp.broadcast_to(sq * (0.5 / D), loss_ref.shape)
        dx3 = diff * (1.0 / D)
        de = dx3 * s
        dgt = (dx3 * e * s * (1.0 - s)).astype(BF16)
        dgt_ref[...] = dgt
        dn = de * g5
        de0_ref[...] = (r * (dn - en * jnp.mean(dn * en, axis=-1, keepdims=True))).astype(BF16)
        dg5_ref[...] += jnp.sum(de * en, axis=0, keepdims=True)
        dx2_ref[...] = dx3 + _dot_nt(dgt, wpg_ref[...])

    r = _rows(tm, D)
    return _pcall(body, name="ple_fwd_bwd", grid=(T // tm,),
                  in_specs=[_rows(tm, PLE), r, r, _full((PLE, D)), _full((1, D)), _full((D, D))],
                  out_specs=[_full((8, LANES)), r, r, r, _full((1, D))],
                  out_shape=[SDS((8, LANES), F32), SDS((T, D), F32), SDS((T, D), BF16), SDS((T, D), BF16), SDS((1, D), F32)],
                  sem=("arbitrary",))(p, x2, tgt, wple, g5, wpg)


def _bwd_mlp_down(dx2, d, g4, w2, u, tm):
    T = dx2.shape[0]

    def body(dx_ref, d_ref, g_ref, w_ref, u_ref, dd_ref, da_ref, dg_ref):
        @pl.when(pl.program_id(0) == 0)
        def _():
            dg_ref[...] = jnp.zeros_like(dg_ref)

        dd, dg = _rms_bwd(dx_ref[...], d_ref[...], g_ref[...])
        dg_ref[...] += dg
        ddb = dd.astype(BF16)
        dd_ref[...] = ddb
        du = _dot_nt(ddb, w_ref[...])
        da_ref[...] = (du * (2.0 * jnp.sqrt(u_ref[...].astype(F32)))).astype(BF16)

    return _pcall(body, name="bwd_mlp_down", grid=(T // tm,),
                  in_specs=[_rows(tm, D), _rows(tm, D), _full((1, D)), _full((D_FF, D)), _rows(tm, D_FF)],
                  out_specs=[_rows(tm, D), _rows(tm, D_FF), _full((1, D))],
                  out_shape=[SDS((T, D), BF16), SDS((T, D_FF), BF16), SDS((1, D), F32)],
                  sem=("arbitrary",))(dx2, d, g4, w2, u)


def _bwd_mlp_up(da, w1, x1, g3, dx2, tm):
    T = dx2.shape[0]

    def body(da_ref, w_ref, x_ref, g_ref, dx2_ref, dx1_ref, dg_ref):
        @pl.when(pl.program_id(0) == 0)
        def _():
            dg_ref[...] = jnp.zeros_like(dg_ref)

        dh = _dot_nt(da_ref[...], w_ref[...])
        dx, dg = _rms_bwd(dh, x_ref[...], g_ref[...])
        dg_ref[...] += dg
        dx1_ref[...] = dx2_ref[...] + dx

    return _pcall(body, name="bwd_mlp_up", grid=(T // tm,),
                  in_specs=[_rows(tm, D_FF), _full((D, D_FF)), _rows(tm, D), _full((1, D)), _rows(tm, D)],
                  out_specs=[_rows(tm, D), _full((1, D))],
                  out_shape=[SDS((T, D), F32), SDS((1, D), F32)], sem=("arbitrary",))(da, w1, x1, g3, dx2)


def _bwd_mix(dx1, yo, g2, wo, z, au, bu, wmu, wsu, om, tm):
    T = dx1.shape[0]

    def body(dx_ref, yo_ref, g_ref, wo_ref, ga_ref, gb_ref, au_ref, bu_ref, wmu_ref, wsu_ref, om_ref,
             dyo_ref, dg_ref, dau_ref, dbu_ref, dga_ref, dgb_ref, dos_ref, dl_ref, dot_ref):
        @pl.when(pl.program_id(0) == 0)
        def _():
            dg_ref[...] = jnp.zeros_like(dg_ref)

        dyo, dg = _rms_bwd(dx_ref[...], yo_ref[...], g_ref[...])
        dg_ref[...] += dg
        dyob = dyo.astype(BF16)
        dyo_ref[...] = dyob
        dy = _dot_nt(dyob, wo_ref[...])
        sa = _sigmoid(ga_ref[...])
        sb = _sigmoid(gb_ref[...])
        dau = (dy * sa).astype(BF16)
        dbu = (dy * sb).astype(BF16)
        dau_ref[...] = dau
        dbu_ref[...] = dbu
        dga_ref[...] = (dy * au_ref[...] * sa * (1.0 - sa)).astype(BF16)
        dgb_ref[...] = (dy * bu_ref[...] * sb * (1.0 - sb)).astype(BF16)
        dom = _dot_nt(dau, wmu_ref[...])
        dos_ref[...] = _dot_nt(dbu, wsu_ref[...])
        prod = dom * om_ref[...]
        sub = lax.broadcasted_iota(jnp.int32, (8, tm), 0)
        for pr in range(MLA_HEADS // 2):
            sl = slice(LANES * pr, LANES * (pr + 1))
            pt = prod[:, sl].T
            d0 = jnp.sum(pt[0:64], axis=0, keepdims=True)
            d1 = jnp.sum(pt[64:128], axis=0, keepdims=True)
            dl_ref[pr, 0] = jnp.where(sub == 0, d0, jnp.where(sub == 1, d1, 0.0))
            dot_ref[0, sl, :] = dom[:, sl].T.astype(BF16)

    r = _rows(tm, D)
    w = _full((D, D))
    return _pcall(body, name="bwd_mix", grid=(T // tm,),
                  in_specs=[r, r, _full((1, D)), w, _rows(tm, D, 1), _rows(tm, D, 2), r, r, w, w, r],
                  out_specs=[r, _full((1, D)), r, r, r, r, r, pl.BlockSpec((MLA_HEADS // 2, 1, 8, tm), lambda i: (0, i, 0, 0)),
                             pl.BlockSpec((1, D, tm), lambda i: (i, 0, 0))],
                  out_shape=[SDS((T, D), BF16), SDS((1, D), F32), SDS((T, D), BF16), SDS((T, D), BF16), SDS((T, D), BF16),
                             SDS((T, D), BF16), SDS((T, D), F32), SDS((MLA_HEADS // 2, T // tm, 8, tm), F32),
                             SDS((T // tm, D, tm), BF16)],
                  sem=("arbitrary",))(dx1, yo, g2, wo, z, z, au, bu, wmu, wsu, om)


def _mla_bwd(qt, km, kt, vm, dot, lse, delta, tb):
    T = km.shape[0]
    nb = T // tb
    cc = ATT_COLS

    def body(qt_ref, k_ref, kt_ref, v_ref, dot_ref, l_ref, dl_ref, dqt_ref, dkt_ref, dvt_ref,
             s_ref, dp_ref, p_ref, ds_ref, vh_ref):
        j = pl.program_id(1)

        @pl.when(j == 0)
        def _():
            dqt_ref[...] = jnp.zeros_like(dqt_ref)

        dkt_ref[...] = jnp.zeros_like(dkt_ref)
        dvt_ref[...] = jnp.zeros_like(dvt_ref)
        lo = lax.broadcasted_iota(jnp.int32, (tb, LANES), 1) < 64
        key = lax.broadcasted_iota(jnp.int32, (tb, cc), 0)
        qry = lax.broadcasted_iota(jnp.int32, (tb, cc), 1)
        v = v_ref[...]
        vh_ref[0] = jnp.where(lo, v, jnp.zeros_like(v))
        vh_ref[1] = jnp.where(lo, jnp.zeros_like(v), v)

        def scores(i, slot):
            for hh in range(2):
                sl = slice(LANES * hh, LANES * (hh + 1))
                s_ref[slot, hh] = _dot(k_ref[:, sl], qt_ref[i, sl, :])
                dp_ref[slot, hh] = _dot(vh_ref[hh], dot_ref[i])

        def grads(i, slot, diagonal):
            lse_i = l_ref[0, i]
            delta_i = dl_ref[0, i]
            for hh in range(2):
                for c in range(tb // cc):
                    cols = slice(cc * c, cc * (c + 1))
                    p = jnp.exp2(s_ref[slot, hh, :, cols] * MLA_LOG2_SCALE - lse_i[hh:hh + 1, cols])
                    if diagonal:
                        p = jnp.where(key <= qry + cc * c, p, 0.0)
                    p_ref[hh, :, cols] = p.astype(BF16)
                    ds_ref[hh, :, cols] = (p * (dp_ref[slot, hh, :, cols] - delta_i[hh:hh + 1, cols]) * MLA_SCALE).astype(BF16)
            for hh in range(2):
                sl = slice(LANES * hh, LANES * (hh + 1))
                half = slice(64 * hh, 64 * (hh + 1))
                dvt_ref[0, half, :] += _dot_nt(dot_ref[i, half, :], p_ref[hh])
                dkt_ref[0, sl, :] += _dot_nt(qt_ref[i, sl, :], ds_ref[hh])
                dqt_ref[i, sl, :] += _dot(kt_ref[0, sl, :], ds_ref[hh])

        n_off = nb - 1 - j

        def step(u, carry):
            i0 = j + 1 + 2 * u
            scores(i0 + 1, 1)
            grads(i0, 0, False)
            scores(jnp.where(i0 + 2 < nb, i0 + 2, j), 0)
            grads(i0 + 1, 1, False)
            return carry

        scores(jnp.where(n_off > 0, j + 1, j), 0)
        lax.fori_loop(0, n_off // 2, step, 0)

        @pl.when(n_off % 2 == 1)
        def _():
            scores(j, 1)
            grads(nb - 1, 0, False)
            grads(j, 1, True)

        @pl.when(n_off % 2 == 0)
        def _():
            grads(j, 0, True)

    blk = lambda w: pl.BlockSpec((tb, w), lambda p, j: (j, p))
    stat = pl.BlockSpec((1, nb, 8, tb), lambda p, j: (p, 0, 0, 0))
    pair_t = lambda w: pl.BlockSpec((nb, w, tb), lambda p, j: (0, p, 0))
    blk_t = lambda w: pl.BlockSpec((1, w, tb), lambda p, j: (j, p, 0))
    return _pcall(body, name="mla_bwd", grid=(MLA_HEADS // 2, nb),
                  in_specs=[pair_t(256), blk(256), blk_t(256), blk(LANES), pair_t(LANES), stat, stat],
                  out_specs=[pair_t(256), blk_t(256), blk_t(LANES)],
                  out_shape=[SDS((nb, 2048, tb), F32), SDS((nb, 2048, tb), F32), SDS((nb, D, tb), F32)],
                  scratch=[pltpu.VMEM((2, 2, tb, tb), F32), pltpu.VMEM((2, 2, tb, tb), F32), pltpu.VMEM((2, tb, tb), BF16),
                           pltpu.VMEM((2, tb, tb), BF16), pltpu.VMEM((2, tb, LANES), BF16)],
                  sem=("parallel", "arbitrary"))(qt, km, kt, vm, dot, lse, delta)


def _swa_bwd(sinks, qs, ks, vs, do, o, lse):
    T = qs.shape[0]
    nb, cur, prev = _swa_specs(T)

    def body(sink_ref, q_ref, kc_ref, kp_ref, vc_ref, vp_ref, do_ref, o_ref, l_ref,
             dq_ref, dkc_ref, dkp_ref, dvc_ref, dvp_ref, dsink_ref, kb_ref, vb_ref, s_ref, dp_ref, p_ref, ds_ref):
        n = pl.program_id(0)

        @pl.when(n == 0)
        def _():
            dsink_ref[...] = jnp.zeros_like(dsink_ref)

        mask = _swa_mask(n)
        lo = lax.broadcasted_iota(jnp.int32, (WINDOW, LANES), 1) < 64
        hi = jnp.logical_not(lo)
        lane8 = lax.broadcasted_iota(jnp.int32, (8, LANES), 1)
        for g in range(2):
            gs = slice(LANES * g, LANES * (g + 1))
            kb_ref[g] = jnp.concatenate([kp_ref[:, gs], kc_ref[:, gs]], axis=0)
            vb_ref[g] = jnp.concatenate([vp_ref[:, gs], vc_ref[:, gs]], axis=0)

        def head(h):
            sl = slice(LANES * (h // 2), LANES * (h // 2 + 1))
            hm = lo if h % 2 == 0 else hi
            qp = q_ref[:, sl]
            return hm, sl, jnp.where(hm, qp, jnp.zeros_like(qp)), jnp.where(hm, do_ref[:, sl], 0.0).astype(BF16)

        for h in range(SWA_HEADS):
            _, _, qh, dom = head(h)
            s_ref[h] = _dot_nt(qh, kb_ref[h // 8])
            dp_ref[h] = _dot_nt(dom, vb_ref[h // 8])
        dsink = jnp.zeros((8, LANES), F32)
        for h in range(SWA_HEADS):
            hm, sl, _, _ = head(h)
            lse_h = jnp.max(jnp.where(hm, l_ref[:, sl], -jnp.inf), axis=1, keepdims=True)
            delta = jnp.sum(jnp.where(hm, do_ref[:, sl] * o_ref[:, sl], 0.0), axis=1, keepdims=True)
            p = jnp.exp(jnp.where(mask, s_ref[h] * SWA_SCALE, NEG) - lse_h)
            p_ref[h] = p.astype(BF16)
            ds_ref[h] = (p * (dp_ref[h] - delta) * SWA_SCALE).astype(BF16)
            d_sink = -jnp.sum(jnp.exp(sink_ref[h] - lse_h) * delta, axis=0, keepdims=True)
            dsink = dsink + jnp.where(lane8 == h, d_sink, 0.0)
        dsink_ref[...] += dsink
        for g in range(2):
            gs = slice(LANES * g, LANES * (g + 1))
            dkb = jnp.zeros((2 * WINDOW, LANES), F32)
            dvb = jnp.zeros((2 * WINDOW, LANES), F32)
            for j in range(4 * g, 4 * g + 4):
                dqs = []
                for h in (2 * j, 2 * j + 1):
                    _, _, qh, dom = head(h)
                    dvb = dvb + _dot_tn(p_ref[h], dom)
                    dkb = dkb + _dot_tn(ds_ref[h], qh)
                    dqs.append(_dot(ds_ref[h], kb_ref[g]))
                dq_ref[:, LANES * j:LANES * (j + 1)] = jnp.where(lo, dqs[0], dqs[1])
            dkp_ref[:, gs] = dkb[:WINDOW]
            dkc_ref[:, gs] = dkb[WINDOW:]
            dvp_ref[:, gs] = dvb[:WINDOW]
            dvc_ref[:, gs] = dvb[WINDOW:]

    band = pltpu.VMEM((2, 2 * WINDOW, LANES), BF16)
    return _pcall(body, name="swa_bwd", grid=(nb,),
                  in_specs=[pl.BlockSpec(memory_space=pltpu.SMEM), cur(D), cur(256), prev(256), cur(256), prev(256),
                            cur(D), cur(D), cur(D)],
                  out_specs=[cur(D), cur(256), cur(256), cur(256), cur(256), _full((8, LANES))],
                  out_shape=[SDS((T, D), F32), SDS((T, 256), F32), SDS((T, 256), F32), SDS((T, 256), F32), SDS((T, 256), F32),
                             SDS((8, LANES), F32)],
                  scratch=[band, band, pltpu.VMEM((SWA_HEADS, WINDOW, 2 * WINDOW), F32),
                           pltpu.VMEM((SWA_HEADS, WINDOW, 2 * WINDOW), F32), pltpu.VMEM((SWA_HEADS, WINDOW, 2 * WINDOW), BF16),
                           pltpu.VMEM((SWA_HEADS, WINDOW, 2 * WINDOW), BF16)],
                  sem=("arbitrary",))(sinks, qs, ks, ks, vs, vs, do, o, lse)


def _bwd_qkv(dqm, dkm, dvm, dqs, dkc, dkp, dvc, dvp, z, gq, gkv, wqb, wkn, wv, tab_m, tab_s):
    T = z.shape[0]
    tm = WINDOW
    nb = T // tm
    per = dqm.shape[2] // tm

    def body(dqm_ref, dkm_ref, dvm_ref, dqs_ref, dkc_ref, dkp_ref, dvc_ref, dvp_ref, qa_ref, kva_ref, gq_ref, gkv_ref,
             wqb_ref, wkn_ref, wv_ref, cm_ref, am_ref, bm_ref, cs_ref, as_ref, bs_ref,
             dq_out, dkn_out, dv_out, dsq_ref, drest_ref, dgq_ref, dgkv_ref):
        i = pl.program_id(0)

        @pl.when(i == 0)
        def _():
            dgq_ref[...] = jnp.zeros_like(dgq_ref)
            dgkv_ref[...] = jnp.zeros_like(dgkv_ref)

        cm, am, bm = cm_ref[...], -am_ref[...], -bm_ref[...]
        cs, as_, bs = cs_ref[...], -as_ref[...], -bs_ref[...]
        lane = lax.broadcasted_iota(jnp.int32, (tm, LANES), 1)
        nope = lane < MLA_NOPE
        roped = jnp.logical_and(lane >= MLA_NOPE, lane < MLA_NOPE + MLA_ROPE)
        dkr = jnp.zeros((tm, LANES), F32)
        for h in range(MLA_HEADS):
            sl = slice(LANES * h, LANES * (h + 1))
            dq_out[:, sl] = _rope(dqm_ref[0, sl, :].T, cm, am, bm, MLA_ROPE // 2).astype(BF16)
            dk_h = dkm_ref[0, sl, :].T
            dkn_out[:, sl] = jnp.where(nope, dk_h, 0.0).astype(BF16)
            dkr = dkr + jnp.where(roped, dk_h, 0.0)
        for j in range(D // LANES):
            sl = slice(LANES * j, LANES * (j + 1))
            dv_out[:, sl] = dvm_ref[0, sl, :].T.astype(BF16)
        dqn = _dot_nt(dq_out[...], wqb_ref[...])
        dkvn = _dot_nt(dkn_out[...], wkn_ref[...]) + _dot_nt(dv_out[...], wv_ref[...])
        dqa, dgq = _rms_bwd(dqn, qa_ref[...], gq_ref[...])
        dkva, dgkv = _rms_bwd(dkvn, kva_ref[...], gkv_ref[...])
        dgq_ref[...] += dgq
        dgkv_ref[...] += dgkv
        for j in range(D // LANES):
            sl = slice(LANES * j, LANES * (j + 1))
            dsq_ref[:, sl] = _rope(dqs_ref[:, sl], cs, as_, bs, SWA_HD // 2).astype(BF16)
        keep = (i < nb - 1).astype(F32)
        drest_ref[:, 0:256] = dqa.astype(BF16)
        for j in range(2):
            sl = slice(LANES * j, LANES * (j + 1))
            dk = dkc_ref[:, sl] + keep * dkp_ref[:, sl]
            drest_ref[:, 256 + LANES * j:256 + LANES * (j + 1)] = _rope(dk, cs, as_, bs, SWA_HD // 2).astype(BF16)
        drest_ref[:, 512:768] = (dvc_ref[...] + keep * dvp_ref[...]).astype(BF16)
        drest_ref[:, 768:896] = dkva.astype(BF16)
        drest_ref[:, 896:1024] = _rope(dkr, cm, am, bm, MLA_ROPE // 2).astype(BF16)

    nxt = pl.BlockSpec((tm, 256), lambda i: (jnp.minimum(i + 1, nb - 1), 0))
    tab = [_rows(tm, LANES)] * 6
    return _pcall(body, name="bwd_qkv", grid=(nb,),
                  in_specs=[pl.BlockSpec((1, 2048, tm), lambda i: (i // per, 0, i % per)),
                            pl.BlockSpec((1, 2048, tm), lambda i: (i // per, 0, i % per)),
                            pl.BlockSpec((1, 1024, tm), lambda i: (i // per, 0, i % per)), _rows(tm, 1024), _rows(tm, 256), nxt,
                            _rows(tm, 256), nxt, _rows(tm, 256, 12), _rows(tm, 128, 30), _full((1, Q_LORA)), _full((1, KV_LORA)),
                            _full((Q_LORA, 2048)), _full((KV_LORA, 2048)), _full((KV_LORA, 1024))] + tab,
                  out_specs=[_rows(tm, 2048), _rows(tm, 2048), _rows(tm, 1024), _rows(tm, 1024), _rows(tm, 1024),
                             _full((1, Q_LORA)), _full((1, KV_LORA))],
                  out_shape=[SDS((T, 2048), BF16), SDS((T, 2048), BF16), SDS((T, 1024), BF16), SDS((T, 1024), BF16),
                             SDS((T, 1024), BF16), SDS((1, Q_LORA), F32), SDS((1, KV_LORA), F32)],
                  sem=("arbitrary",))(dqm, dkm, dvm, dqs, dkc, dkp, dvc, dvp, z, z, gq, gkv, wqb, wkn, wv, *tab_m, *tab_s)


def _bwd_in(dsq, dga, dgb, drest, w_in_p, x, g1, dx1, tm):
    T = x.shape[0]

    def body(a_ref, b_ref, c_ref, d_ref, w_ref, x_ref, g_ref, dx1_ref, dx_ref, dg_ref):
        @pl.when(pl.program_id(0) == 0)
        def _():
            dg_ref[...] = jnp.zeros_like(dg_ref)

        dh = (_dot_nt(a_ref[...], w_ref[:, 0:1024]) + _dot_nt(b_ref[...], w_ref[:, 1024:2048])
              + _dot_nt(c_ref[...], w_ref[:, 2048:3072]) + _dot_nt(d_ref[...], w_ref[:, 3072:4096]))
        dx, dg = _rms_bwd(dh, x_ref[...], g_ref[...])
        dg_ref[...] += dg
        dx_ref[...] = dx1_ref[...] + dx

    r = _rows(tm, D)
    return _pcall(body, name="bwd_in", grid=(T // tm,),
                  in_specs=[r, r, r, r, _full((D, NZ)), r, _full((1, D)), r],
                  out_specs=[r, _full((1, D))], out_shape=[SDS((T, D), F32), SDS((1, D), F32)],
                  sem=("arbitrary",))(dsq, dga, dgb, drest, w_in_p, x, g1, dx1)


def _wgrad(a, g, name, into=None):
    T, K = a.shape
    N = g.shape[1]
    tk, tn, tt = min(K, 1024), min(N, 1024), min(T, 1024)
    if into is not None:
        buf, weight = into
        _, row0, lane0 = PACK_AT[weight]
        shard = {n: (r, c) for n, r, c in BIG}[weight]
        assert lane0 == 0 and shard[1] == D and tk % shard[0] == 0
        per_step = tk // shard[0]
    assert K % tk == 0 and N % tn == 0 and T % tt == 0, (a.shape, g.shape)
    steps = T // tt

    def body(a_ref, g_ref, *rest):
        o_ref, acc_ref = rest[-2:]
        t = pl.program_id(2)

        @pl.when(t == 0)
        def _():
            acc_ref[...] = jnp.zeros_like(acc_ref)

        acc_ref[...] += _dot_tn(a_ref[...].astype(BF16), g_ref[...].astype(BF16))

        @pl.when(t == steps - 1)
        def _():
            o_ref[...] = acc_ref[...].astype(o_ref.dtype).reshape(o_ref.shape)

    in_specs = [pl.BlockSpec((tt, tk), lambda k, n, t: (t, k)), pl.BlockSpec((tt, tn), lambda k, n, t: (t, n))]
    if into is None:
        return _pcall(body, name=name, grid=(K // tk, N // tn, steps), in_specs=in_specs,
                      out_specs=pl.BlockSpec((tk, tn), lambda k, n, t: (k, n)), out_shape=SDS((K, N), F32),
                      scratch=[pltpu.VMEM((tk, tn), F32)], sem=("parallel", "parallel", "arbitrary"))(a, g)
    assert row0 % shard[0] == 0 and (K // tk) * (N // tn) * per_step == N_CHIPS
    return _pcall(body, name=name, grid=(K // tk, N // tn, steps), in_specs=in_specs + [ANY],
                  out_specs=pl.BlockSpec((per_step, shard[0], tn), lambda k, n, t: (k + n, row0 // shard[0], 0)),
                  out_shape=SDS(buf.shape, buf.dtype),
                  scratch=[pltpu.VMEM((tk, tn), F32)], sem=("parallel", "parallel", "arbitrary"), aliases={2: 0})(a, g, buf)


def _adamw(w, packed_g, m, v, name):
    _, R, C = w.shape
    _, row0, lane0 = PACK_AT[name]
    tr = min(R, 256 if row0 % 256 == 0 else 128)
    assert row0 % tr == 0 and R % tr == 0

    def body(w_ref, g_ref, m_ref, v_ref, go_ref, d_ref, m2_ref, v2_ref):
        g_ = g_ref[:, lane0:lane0 + C]
        go_ref[0] = g_
        m2 = ADAM_B1 * m_ref[0] + (1.0 - ADAM_B1) * g_
        v2 = ADAM_B2 * v_ref[0] + (1.0 - ADAM_B2) * jnp.square(g_)
        m_hat = m2 / (1.0 - ADAM_B1 ** ADAM_STEP)
        v_hat = v2 / (1.0 - ADAM_B2 ** ADAM_STEP)
        d_ref[0] = -ADAM_LR * (m_hat / (jnp.sqrt(v_hat) + ADAM_EPS) + ADAM_WD * w_ref[0])
        m2_ref[0] = m2
        v2_ref[0] = v2

    r = pl.BlockSpec((1, tr, C), lambda i: (0, i, 0))
    return _pcall(body, name="adamw_" + name, grid=(R // tr,),
                  in_specs=[r, pl.BlockSpec((tr, D), lambda i: (row0 // tr + i, 0)), r, r], out_specs=[r] * 4,
                  out_shape=[SDS((1, R, C), F32)] * 4, sem=("parallel",))(w, packed_g, m, v)


def _adamw_small(w, parts, m, v):
    def body(w_ref, p_ref, m_ref, v_ref, g_ref, d_ref, m2_ref, v2_ref):
        g_ = p_ref[0]
        for k in range(1, N_DEV):
            g_ = g_ + p_ref[k]
        g_ref[...] = g_
        m2 = ADAM_B1 * m_ref[...] + (1.0 - ADAM_B1) * g_
        v2 = ADAM_B2 * v_ref[...] + (1.0 - ADAM_B2) * jnp.square(g_)
        m_hat = m2 / (1.0 - ADAM_B1 ** ADAM_STEP)
        v_hat = v2 / (1.0 - ADAM_B2 ** ADAM_STEP)
        d_ref[...] = -ADAM_LR * (m_hat / (jnp.sqrt(v_hat) + ADAM_EPS) + ADAM_WD * w_ref[...])
        m2_ref[...] = m2
        v2_ref[...] = v2

    s = _full((8, D))
    return _pcall(body, name="adamw_small", grid=(1,), in_specs=[s, _full((N_DEV, 8, D)), s, s], out_specs=[s] * 4,
                  out_shape=[SDS((8, D), F32)] * 4, sem=("arbitrary",))(w, parts, m, v)


ANY = pl.BlockSpec(memory_space=pl.ANY)


def _place():
    x, y, c = lax.axis_index("x"), lax.axis_index("y"), lax.axis_index("c")
    chips = [(1 - x, y), (x, 1 - y), (1 - x, 1 - y)]
    return x, y, c, chips


def _all_gather(wpk):
    rows = wpk.shape[0]
    HALF = rows // 2
    assert HALF % 16 == 0

    def body(in_ref, out_ref, send_sems, recv_sems):
        x, y, c, chips = _place()
        half = pl.ds(pl.multiple_of(c * HALF, 16), HALF)
        other = pl.ds(pl.multiple_of((1 - c) * HALF, 16), HALF)

        def copy(k, src, dst, to):
            return pltpu.make_async_remote_copy(src_ref=src, dst_ref=dst, send_sem=send_sems.at[k], recv_sem=recv_sems.at[k],
                                                device_id=to, device_id_type=MESH)

        first = [copy(k, in_ref.at[half], out_ref.at[2 * x + y, half], (cx, cy, c)) for k, (cx, cy) in enumerate(chips)]
        for cp in first:
            cp.start()
        passed = []
        for k, (cx, cy) in enumerate(chips):
            slot = out_ref.at[2 * cx + cy, half]
            copy(k, slot, slot, (x, y, c)).wait_recv()
            fwd = copy(3 + k, slot, slot, (x, y, 1 - c))
            fwd.start()
            passed.append(fwd)
        for k, (cx, cy) in enumerate(chips):
            slot = out_ref.at[2 * cx + cy, other]
            copy(3 + k, slot, slot, (x, y, c)).wait_recv()
        for cp in first + passed:
            cp.wait_send()

    return _pcall(body, name="all_gather_weights", in_specs=[ANY], out_specs=ANY,
                  out_shape=SDS((N_CHIPS, rows, D), BF16),
                  scratch=[pltpu.SemaphoreType.DMA((6,)), pltpu.SemaphoreType.DMA((6,))])(wpk)


HBM = pl.BlockSpec(memory_space=pltpu.HBM)
SEM = pl.BlockSpec(memory_space=pltpu.SEMAPHORE)
DATAFLOW = pltpu.SideEffectType.DATAFLOW_SIDE_EFFECTING


def _in_hbm(a):
    return pltpu.with_memory_space_constraint(a, pltpu.HBM)


def _gather_late_start(wpk, after):
    rows = wpk.shape[0]

    def body(in_ref, land_ref, after_ref, send_sems, recv_sems, in_thru, land_thru, token):
        x, y, c, chips = _place()
        for k, (cx, cy) in enumerate(chips):
            pltpu.make_async_remote_copy(src_ref=in_ref, dst_ref=land_ref.at[2 * x + y], send_sem=send_sems.at[k],
                                         recv_sem=recv_sems.at[k], device_id=(cx, cy, c), device_id_type=MESH).start()
        token[...] = jnp.zeros_like(token)

    return pl.pallas_call(
        body, name="gather_late_start",
        out_shape=(pltpu.SemaphoreType.DMA((3,)), pltpu.SemaphoreType.DMA((3,)), pltpu.HBM(wpk.shape, wpk.dtype),
                   pltpu.HBM((N_CHIPS, rows, D), wpk.dtype), SDS((8, LANES), F32)),
        in_specs=(HBM, HBM, ANY), out_specs=(SEM, SEM, HBM, HBM, pl.BlockSpec(memory_space=pltpu.VMEM)),
        input_output_aliases={0: 2, 1: 3}, compiler_params=pltpu.CompilerParams(has_side_effects=DATAFLOW),
    )(_in_hbm(wpk), _in_hbm(lax.empty((N_CHIPS, rows, D), wpk.dtype)), after)


def _gather_late_wait(send_sems, recv_sems, in_thru, land_thru, after):
    def body(in_ref, land_ref, send_sems, recv_sems, after_ref, after2_ref, in_dead, got_ref):
        x, y, c, chips = _place()
        for k, (cx, cy) in enumerate(chips):
            cp = pltpu.make_async_remote_copy(src_ref=in_ref, dst_ref=land_ref.at[2 * cx + cy], send_sem=send_sems.at[k],
                                              recv_sem=recv_sems.at[k], device_id=(cx, cy, c), device_id_type=MESH)
            cp.wait_send()
            cp.wait_recv()

    return pl.pallas_call(
        body, name="gather_late_wait",
        out_shape=(pltpu.HBM(in_thru.shape, in_thru.dtype), pltpu.HBM(land_thru.shape, land_thru.dtype)),
        in_specs=(HBM, HBM, SEM, SEM, ANY, ANY), out_specs=(HBM, HBM), input_output_aliases={0: 0, 1: 1},
        compiler_params=pltpu.CompilerParams(has_side_effects=DATAFLOW),
    )(in_thru, land_thru, send_sems, recv_sems, *after)[1]


def _rs_sibling(gpk):
    HALF = gpk.shape[1] // 2

    def body(in_ref, out_ref, send_sem, recv_sem):
        x, y, c, _ = _place()
        theirs = pl.ds(pl.multiple_of((1 - c) * HALF, 8), HALF)
        cp = pltpu.make_async_remote_copy(src_ref=in_ref.at[:, theirs], dst_ref=out_ref, send_sem=send_sem, recv_sem=recv_sem,
                                          device_id=(x, y, 1 - c), device_id_type=MESH)
        cp.start()
        cp.wait()

    return _pcall(body, name="rs_sibling", in_specs=[ANY], out_specs=ANY, out_shape=SDS((N_CHIPS, HALF, D), F32),
                  scratch=[pltpu.SemaphoreType.DMA, pltpu.SemaphoreType.DMA])(gpk)


def _rs_add_sibling(cidx, gpk, got):
    HALF = got.shape[1]
    th = HALF // 4
    nh = HALF // th
    assert th % 16 == 0

    def body(c_ref, a_ref, b_ref, o_ref):
        o_ref[...] = (a_ref[...] + b_ref[...]).astype(BF16)

    gs = pltpu.PrefetchScalarGridSpec(
        num_scalar_prefetch=1, grid=(N_CHIPS, nh),
        in_specs=[pl.BlockSpec((1, th, D), lambda j, i, c: (j, c[0] * nh + i, 0)), pl.BlockSpec((1, th, D), lambda j, i, c: (j, i, 0))],
        out_specs=pl.BlockSpec((1, th, D), lambda j, i, c: (j, i, 0)))
    return pl.pallas_call(body, name="rs_add_sibling", grid_spec=gs, out_shape=SDS((N_CHIPS, HALF, D), BF16),
                          compiler_params=pltpu.CompilerParams(dimension_semantics=("parallel", "parallel"),
                                                               vmem_limit_bytes=48 << 20))(cidx, gpk, got)


def _rs_chips_start(part, small, after):
    def body(p_ref, s_ref, land_ref, sland_ref, after_ref, send_sems, recv_sems, p_thru, s_thru, land_thru, sland_thru, token):
        x, y, c, chips = _place()
        for k, (cx, cy) in enumerate(chips):
            pltpu.make_async_remote_copy(src_ref=p_ref.at[2 * cx + cy], dst_ref=land_ref.at[2 * x + y], send_sem=send_sems.at[k],
                                         recv_sem=recv_sems.at[k], device_id=(cx, cy, c), device_id_type=MESH).start()
        peers = [(x, y, 1 - c)] + [(cx, cy, c) for cx, cy in chips] + [(cx, cy, 1 - c) for cx, cy in chips]
        for k, to in enumerate(peers):
            pltpu.make_async_remote_copy(src_ref=s_ref, dst_ref=sland_ref.at[4 * x + 2 * y + c], send_sem=send_sems.at[3 + k],
                                         recv_sem=recv_sems.at[3 + k], device_id=to, device_id_type=MESH).start()
        token[...] = jnp.zeros_like(token)

    return pl.pallas_call(
        body, name="rs_chips_start",
        out_shape=(pltpu.SemaphoreType.DMA((10,)), pltpu.SemaphoreType.DMA((10,)), pltpu.HBM(part.shape, part.dtype),
                   pltpu.HBM(small.shape, small.dtype), pltpu.HBM(part.shape, part.dtype), pltpu.HBM((N_DEV, 8, D), F32),
                   SDS((8, LANES), F32)),
        in_specs=(HBM, HBM, HBM, HBM, ANY), out_specs=(SEM, SEM, HBM, HBM, HBM, HBM, pl.BlockSpec(memory_space=pltpu.VMEM)),
        input_output_aliases={0: 2, 1: 3, 2: 4, 3: 5}, compiler_params=pltpu.CompilerParams(has_side_effects=DATAFLOW),
    )(_in_hbm(part), _in_hbm(small), _in_hbm(lax.empty(part.shape, part.dtype)), _in_hbm(lax.empty((N_DEV, 8, D), F32)), after)


def _rs_chips_wait(send_sems, recv_sems, p_thru, s_thru, land_thru, sland_thru, after):
    def body(p_ref, s_ref, land_ref, sland_ref, send_sems, recv_sems, *after_and_outputs):
        x, y, c, chips = _place()
        for k, (cx, cy) in enumerate(chips):
            cp = pltpu.make_async_remote_copy(src_ref=p_ref.at[0], dst_ref=land_ref.at[2 * cx + cy], send_sem=send_sems.at[k],
                                              recv_sem=recv_sems.at[k], device_id=(cx, cy, c), device_id_type=MESH)
            cp.wait_send()
            cp.wait_recv()
        peers = [(x, y, 1 - c)] + [(cx, cy, c) for cx, cy in chips] + [(cx, cy, 1 - c) for cx, cy in chips]
        for k, (px, py, pc) in enumerate(peers):
            cp = pltpu.make_async_remote_copy(src_ref=s_ref, dst_ref=sland_ref.at[4 * px + 2 * py + pc], send_sem=send_sems.at[3 + k],
                                              recv_sem=recv_sems.at[3 + k], device_id=(px, py, pc), device_id_type=MESH)
            cp.wait_send()
            cp.wait_recv()

    hbm = lambda a: pltpu.HBM(a.shape, a.dtype)
    outs = pl.pallas_call(
        body, name="rs_chips_wait", out_shape=(hbm(p_thru), hbm(s_thru), hbm(land_thru), hbm(sland_thru)),
        in_specs=(HBM, HBM, HBM, HBM, SEM, SEM) + (ANY,) * len(after), out_specs=(HBM, HBM, HBM, HBM),
        input_output_aliases={0: 0, 1: 1, 2: 2, 3: 3}, compiler_params=pltpu.CompilerParams(has_side_effects=DATAFLOW),
    )(p_thru, s_thru, land_thru, sland_thru, send_sems, recv_sems, *after)
    return outs[0], outs[2], outs[3]


def _rs_add_chips(qidx, part, parts):
    HALF = part.shape[1]
    th = HALF // 4
    assert th % 16 == 0

    def body(q_ref, own_ref, p_ref, o_ref):
        for me in range(N_CHIPS):
            @pl.when(q_ref[0] == me)
            def _(me=me):
                t = [(own_ref[0] if j == me else p_ref[j]).astype(F32) for j in range(N_CHIPS)]
                o_ref[...] = ((t[0] + t[1]) + t[2]) + t[3]

    gs = pltpu.PrefetchScalarGridSpec(
        num_scalar_prefetch=1, grid=(HALF // th,),
        in_specs=[pl.BlockSpec((1, th, D), lambda i, q: (q[0], i, 0)), pl.BlockSpec((N_CHIPS, th, D), lambda i, q: (0, i, 0))],
        out_specs=pl.BlockSpec((th, D), lambda i, q: (i, 0)))
    return pl.pallas_call(body, name="rs_add_chips", grid_spec=gs, out_shape=SDS((HALF, D), F32),
                          compiler_params=pltpu.CompilerParams(dimension_semantics=("parallel",),
                                                               vmem_limit_bytes=48 << 20))(qidx, part, parts)


def _rs_join(mine, core, name):
    def body(in_ref, out_ref, send_sem, recv_sem):
        x, y, c, _ = _place()
        cp = pltpu.make_async_remote_copy(src_ref=in_ref, dst_ref=out_ref, send_sem=send_sem, recv_sem=recv_sem,
                                          device_id=(x, y, 1 - c), device_id_type=MESH)
        cp.start()
        cp.wait()

    theirs = _pcall(body, name=name, in_specs=[ANY], out_specs=ANY, out_shape=SDS(mine.shape, F32),
                    scratch=[pltpu.SemaphoreType.DMA, pltpu.SemaphoreType.DMA])(mine)
    return jnp.where(core == 0, jnp.concatenate([mine, theirs]), jnp.concatenate([theirs, mine]))


def _reduce_late_start(gpk, after):
    rows = gpk.shape[1]
    HALF = rows // 2
    assert HALF % 16 == 0

    def body(in_ref, land_ref, after_ref, send_sems, recv_sems, in_thru, land_thru, token):
        x, y, c, chips = _place()
        me = 4 * x + 2 * y + c
        peers = [(x, y, 1 - c)] + [(cx, cy, c) for cx, cy in chips] + [(cx, cy, 1 - c) for cx, cy in chips]
        for k, (px, py, pc) in enumerate(peers):
            src = in_ref.at[2 * px + py, pl.ds(pl.multiple_of(pc * HALF, 16), HALF)]
            pltpu.make_async_remote_copy(src_ref=src, dst_ref=land_ref.at[me], send_sem=send_sems.at[k], recv_sem=recv_sems.at[k],
                                         device_id=(px, py, pc), device_id_type=MESH).start()
        token[...] = jnp.zeros_like(token)

    return pl.pallas_call(
        body, name="reduce_late_start",
        out_shape=(pltpu.SemaphoreType.DMA((7,)), pltpu.SemaphoreType.DMA((7,)), pltpu.HBM(gpk.shape, gpk.dtype),
                   pltpu.HBM((N_DEV, HALF, D), gpk.dtype), SDS((8, LANES), F32)),
        in_specs=(HBM, HBM, ANY), out_specs=(SEM, SEM, HBM, HBM, pl.BlockSpec(memory_space=pltpu.VMEM)),
        input_output_aliases={0: 2, 1: 3}, compiler_params=pltpu.CompilerParams(has_side_effects=DATAFLOW),
    )(_in_hbm(gpk), _in_hbm(lax.empty((N_DEV, HALF, D), gpk.dtype)), after)


def _reduce_late_wait(send_sems, recv_sems, in_thru, land_thru, after):
    def body(in_ref, land_ref, send_sems, recv_sems, after_ref, in_out, got_ref):
        x, y, c, chips = _place()
        peers = [(x, y, 1 - c)] + [(cx, cy, c) for cx, cy in chips] + [(cx, cy, 1 - c) for cx, cy in chips]
        for k, (px, py, pc) in enumerate(peers):
            cp = pltpu.make_async_remote_copy(src_ref=land_ref.at[0], dst_ref=land_ref.at[4 * px + 2 * py + pc],
                                              send_sem=send_sems.at[k], recv_sem=recv_sems.at[k],
                                              device_id=(px, py, pc), device_id_type=MESH)
            cp.wait_send()
            cp.wait_recv()

    return pl.pallas_call(
        body, name="reduce_late_wait",
        out_shape=(pltpu.HBM(in_thru.shape, in_thru.dtype), pltpu.HBM(land_thru.shape, land_thru.dtype)),
        in_specs=(HBM, HBM, SEM, SEM, ANY), out_specs=(HBM, HBM), input_output_aliases={0: 0, 1: 1},
        compiler_params=pltpu.CompilerParams(has_side_effects=DATAFLOW),
    )(in_thru, land_thru, send_sems, recv_sems, after)


def _reduce_late_add(didx, gpk, parts):
    HALF = parts.shape[1]
    th = HALF // 4
    nh = HALF // th
    assert th % 16 == 0

    def body(d_ref, own_ref, p_ref, o_ref):
        for me in range(N_DEV):
            @pl.when(d_ref[0] == me)
            def _(me=me):
                t = [(own_ref[0] if j == me else p_ref[j]).astype(F32) for j in range(N_DEV)]
                o_ref[...] = ((((((t[0] + t[1]) + t[2]) + t[3]) + t[4]) + t[5]) + t[6]) + t[7]

    gs = pltpu.PrefetchScalarGridSpec(
        num_scalar_prefetch=1, grid=(nh,),
        in_specs=[pl.BlockSpec((1, th, D), lambda i, d: (d[1], d[2] * nh + i, 0)), pl.BlockSpec((N_DEV, th, D), lambda i, d: (0, i, 0))],
        out_specs=pl.BlockSpec((th, D), lambda i, d: (i, 0)))
    return pl.pallas_call(body, name="reduce_late_add", grid_spec=gs, out_shape=SDS((HALF, D), F32),
                          compiler_params=pltpu.CompilerParams(dimension_semantics=("parallel",),
                                                               vmem_limit_bytes=48 << 20))(didx, gpk, parts)


def _pack_early(b, dtype):
    lanes = lambda a: jnp.pad(a.astype(dtype), ((0, 0), (0, D - a.shape[1])))
    pair = jnp.concatenate([b["w_q_b"].astype(dtype), b["w_ple"].astype(dtype), jnp.zeros((256, D - 640), dtype)], axis=1)
    return jnp.concatenate([lanes(b["w_in"]), pair, lanes(b["w_kv_b"])], axis=0)


def _pack_late(b, dtype):
    return jnp.concatenate([b[n].astype(dtype) for n in ("w_mla_up", "w_swa_up", "w_out", "w_ple_gate", "w_mlp_up", "w_mlp_down")],
                           axis=0)


def _unpack_shards(pk, which):
    return {n: pk[PACK_AT[n][1]:PACK_AT[n][1] + r, PACK_AT[n][2]:PACK_AT[n][2] + c] for n, r, c in BIG if PACK_AT[n][0] == which}


def _full_weights(gathered, own, chip, which):
    own_b = _unpack_shards(own, which)
    per_chip = [{n: jnp.where(chip == j, own_b[n], blk) for n, blk in _unpack_shards(gathered[j], which).items()}
                for j in range(N_CHIPS)]
    out = {}
    for n in own_b:
        shards = [pc[n] for pc in per_chip]
        if n == "w_in":
            out["w_in_p"] = _w_in_internal(shards)
        else:
            out[n] = jnp.concatenate(shards, axis=1 if n in COL_SHARDED else 0)
    return out


def _split_full_grads(grads, pack, dtype):
    shard = {n: (r, c) for n, r, c in BIG}
    chunks = []
    for j in range(N_CHIPS):
        blocks = {}
        for n, g in grads.items():
            if n == "w_in_p":
                blocks["w_in"] = _w_in_grad_shard(g, j)
                continue
            r, c = shard[n]
            blocks[n] = g[:, j * c:(j + 1) * c] if n in COL_SHARDED else g[j * r:(j + 1) * r]
        chunks.append(pack(blocks, dtype))
    return jnp.stack(chunks)


W_IN_SHARD = 936
W_IN_SEGMENTS = ((0, 256, (3072,)), (256, 384, (3840,)), (384, 416, (4032,)), (416, 1440, (0,)), (1440, 1504, (3328, 3392)),
                 (1504, 1568, (3456, 3520)), (1568, 1632, (3584, 3648)), (1632, 1696, (3712, 3776)), (1696, 3744, (1024,)))


def _w_in_internal(shards):
    def cols(a, b):
        out = []
        for j, s in enumerate(shards):
            lo, hi = max(a, W_IN_SHARD * j), min(b, W_IN_SHARD * (j + 1))
            if lo < hi:
                out.append(s[:, lo - W_IN_SHARD * j:hi - W_IN_SHARD * j])
        return out

    pieces = {}
    for a, b, places in W_IN_SEGMENTS:
        for at in places:
            pieces[at] = cols(a, b)
    zeros = lambda n: [jnp.zeros((D, n), shards[0].dtype)]
    pieces[3968] = zeros(64)
    pieces[4064] = zeros(32)
    return jnp.concatenate([piece for at in sorted(pieces) for piece in pieces[at]], axis=1)


def _w_in_grad_shard(g, j):
    def internal(a, b):
        out = []
        while a < b:
            end = min(b, (a // D + 1) * D)
            out.append(g[a // D][:, a % D:a % D + end - a])
            a = end
        return out

    out = []
    for a, b, places in W_IN_SEGMENTS:
        lo, hi = max(a, W_IN_SHARD * j), min(b, W_IN_SHARD * (j + 1))
        if lo < hi:
            parts = [internal(at + lo - a, at + hi - a) for at in places]
            if len(parts) == 1:
                out += parts[0]
            else:
                assert len(parts[0]) == len(parts[1]) == 1
                out.append(parts[0][0] + parts[1][0])
    return jnp.concatenate(out, axis=1)


def _local_step(x, p, tgt, w, small, late_weights, late_grads_out):
    T = x.shape[0]
    tm = 256
    tb = 256
    w_in_p = w["w_in_p"]
    wqb = jnp.pad(w["w_q_b"].reshape(Q_LORA, MLA_HEADS, 96), ((0, 0), (0, 0), (0, 32))).reshape(Q_LORA, 2048)
    wkv = w["w_kv_b"].reshape(KV_LORA, MLA_HEADS, 128)
    wkn = jnp.pad(wkv[:, :, :64], ((0, 0), (0, 0), (0, 64))).reshape(KV_LORA, 2048)
    wv = wkv[:, :, 64:].reshape(KV_LORA, 1024)
    tab_m = _rope_tables(T, "mla")
    tab_s = _rope_tables(T, "swa")
    g1, gq, gkv, sinks = small["g_mix_pre"], small["g_q_a"], small["g_kv_a"], small["sinks"]
    g2, g3, g4, g5 = small["g_mix_post"], small["g_mlp_pre"], small["g_mlp_post"], small["g_ple"]
    sink_vec = sinks.reshape(SWA_HEADS)

    z, h1 = _fwd_in(x, g1, w_in_p, tm)
    qn, kvn, km, vm, qt, kt, vt, qs, ks, vs = _fwd_qkv(z, gq, gkv, wqb, wkn, wv, tab_m, tab_s, tb)
    om, lse_m = _mla_fwd(qt, km, vt, tb)
    os_, lse_s = _swa_fwd(sink_vec, qs, ks, vs)
    w = {**w, **late_weights((om, os_))}
    y, yo, au, bu, x1 = _fwd_mix(om, os_, z, x, w["w_mla_up"], w["w_swa_up"], w["w_out"], g2, tm)
    h2, u = _fwd_mlp_up(x1, g3, w["w_mlp_up"], tm)
    d, x2 = _fwd_mlp_down(u, w["w_mlp_down"], x1, g4, tm)
    loss, dx2, dgt, de0, dg5 = _ple_fwd_bwd(p, x2, tgt, w["w_ple"], g5, w["w_ple_gate"], tm)

    dd, da, dg4 = _bwd_mlp_down(dx2, d, g4, w["w_mlp_down"], u, tm)
    dx1, dg3 = _bwd_mlp_up(da, w["w_mlp_up"], x1, g3, dx2, tm)
    dyo, dg2, dau, dbu, dga, dgb, dos, delta_m, dom_t = _bwd_mix(dx1, yo, g2, w["w_out"], z, au, bu, w["w_mla_up"],
                                                                w["w_swa_up"], om, tb)
    gpk_late = lax.empty((N_CHIPS, PACK_ROWS["late"], D), BF16)
    for weight, a_, g_ in (("w_mla_up", om, dau), ("w_swa_up", os_, dbu), ("w_out", y, dyo), ("w_ple_gate", x2, dgt),
                           ("w_mlp_up", h2, da), ("w_mlp_down", u, dd)):
        gpk_late = _wgrad(a_, g_, "wgrad_" + weight[2:], into=(gpk_late, weight))
    token = late_grads_out(gpk_late)
    delta_m = delta_m + token[0, 0]
    dqm, dkm, dvm = _mla_bwd(qt, km, kt, vm, dom_t, lse_m, delta_m, tb)
    dqs, dkc, dkp, dvc, dvp, dsink = _swa_bwd(sink_vec, qs, ks, vs, dos, os_, lse_s)
    dqb, dknb, dvb, dsq, drest, dgq, dgkv = _bwd_qkv(dqm, dkm, dvm, dqs, dkc, dkp, dvc, dvp, z, gq, gkv, wqb, wkn, wv,
                                                      tab_m, tab_s)
    gx, dg1 = _bwd_in(dsq, dga, dgb, drest, w_in_p, x, g1, dx1, tm)

    g_in_p = [_wgrad(h1, dsq, "wgrad_in_sq"), _wgrad(h1, dga, "wgrad_in_ga"), _wgrad(h1, dgb, "wgrad_in_gb"),
              _wgrad(h1, drest, "wgrad_in_rest")]
    g_qb_p = _wgrad(qn, dqb, "wgrad_q_b")
    g_kn_p = _wgrad(kvn, dknb, "wgrad_kv_b_nope")
    g_v_p = _wgrad(kvn, dvb, "wgrad_kv_b_v")
    grads = {
        "w_in_p": g_in_p,
        "w_q_b": g_qb_p.reshape(Q_LORA, MLA_HEADS, 128)[:, :, :96].reshape(Q_LORA, 1536),
        "w_kv_b": jnp.concatenate([g_kn_p.reshape(KV_LORA, MLA_HEADS, 128)[:, :, :64], g_v_p.reshape(KV_LORA, MLA_HEADS, 64)],
                                  axis=2).reshape(KV_LORA, 2048),
        "w_ple": _wgrad(p, de0, "wgrad_ple"),
    }
    small_grads = {"g_mix_pre": dg1, "g_q_a": dgq, "g_kv_a": dgkv, "sinks": dsink[0:1, 0:SWA_HEADS], "g_mix_post": dg2,
                   "g_mlp_pre": dg3, "g_mlp_post": dg4, "g_ple": dg5}
    return loss, gx, grads, small_grads


def _pack_small(vals, fill, scalar=None):
    wide = [vals[n] for n, k in SMALL if k == D]
    narrow = [vals[n] for n, k in SMALL if k != D]
    used = sum(k for _, k in SMALL if k != D)
    last = jnp.concatenate(narrow + [jnp.full((1, D - used), fill, F32)], axis=1)
    rest = jnp.full((2, D), fill, F32)
    if scalar is not None:
        rest = jnp.concatenate([jnp.concatenate([scalar, rest[0:1, 1:]], axis=1), rest[1:2]], axis=0)
    return jnp.concatenate(wide + [last, rest], axis=0)


def _unpack_small(pk):
    out, row, off = {}, 0, 0
    for n, k in SMALL:
        if k == D:
            out[n] = pk[row:row + 1]
            row += 1
    for n, k in SMALL:
        if k != D:
            out[n] = pk[5:6, off:off + k]
            off += k
    return out


def kernel(x, p, g_mix_pre, w_in, g_q_a, w_q_b, g_kv_a, w_kv_b, sinks, w_mla_up, w_swa_up, w_out, g_mix_post, g_mlp_pre, w_mlp_up, w_mlp_down, g_mlp_post, w_ple, g_ple, w_ple_gate, loss_target, m_g_mix_pre, m_w_in, m_g_q_a, m_w_q_b, m_g_kv_a, m_w_kv_b, m_sinks, m_w_mla_up, m_w_swa_up, m_w_out, m_g_mix_post, m_g_mlp_pre, m_w_mlp_up, m_w_mlp_down, m_g_mlp_post, m_w_ple, m_g_ple, m_w_ple_gate, v_g_mix_pre, v_w_in, v_g_q_a, v_w_q_b, v_g_kv_a, v_w_kv_b, v_sinks, v_w_mla_up, v_w_swa_up, v_w_out, v_g_mix_post, v_g_mlp_pre, v_w_mlp_up, v_w_mlp_down, v_g_mlp_post, v_w_ple, v_g_ple, v_w_ple_gate):
    given = dict(locals())
    big_w = {n: given[n][0] for n, _, _ in BIG}
    small_w = {n: given[n] for n, _ in SMALL}
    small_m = {n: given["m_" + n] for n, _ in SMALL}
    small_v = {n: given["v_" + n] for n, _ in SMALL}

    core = lax.axis_index("c")
    chip = 2 * lax.axis_index("x") + lax.axis_index("y")
    core_i = core.astype(jnp.int32).reshape(1)
    chip_i = chip.astype(jnp.int32).reshape(1)
    dev_i = jnp.stack([2 * chip + core, chip, core]).astype(jnp.int32)

    own_early = _pack_early(big_w, BF16)
    own_late = _pack_late(big_w, BF16)
    got_early = _all_gather(own_early)
    late_flight = _gather_late_start(own_late, got_early)
    weights = _full_weights(got_early, own_early, chip, "early")
    step_small = {**small_w, "g_mix_pre": small_w["g_mix_pre"] + late_flight[4][0, 0]}

    def late_weights(after):
        return _full_weights(_gather_late_wait(*late_flight[:4], after), own_late, chip, "late")

    flight = {}

    def late_grads_out(gpk_late):
        flight["late"] = _reduce_late_start(gpk_late, dev_i)
        return flight["late"][4]

    loss_blk, gx, grads, small_grads = _local_step(x[0], p[0, 0], loss_target[0], weights, step_small, late_weights,
                                                   late_grads_out)

    gpk = _split_full_grads(grads, _pack_early, F32)
    got = _rs_sibling(gpk)
    part = _rs_add_sibling(core_i, gpk, got)
    small_own = _pack_small(small_grads, 0.0, loss_blk[0:1, 0:1])
    early_flight = _rs_chips_start(part, small_own, dev_i)

    out_g, out_d, out_m, out_v = {}, {}, {}, {}
    gpk_late, parts_late = _reduce_late_wait(*flight["late"][:4], early_flight[6])
    joined_late = _rs_join(_reduce_late_add(dev_i, gpk_late, parts_late), core, "rs_join_late")
    for n, _, _ in BIG:
        if PACK_AT[n][0] == "late":
            out_g[n], out_d[n], out_m[n], out_v[n] = _adamw(given[n], joined_late, given["m_" + n], given["v_" + n], n)

    part, parts, small_parts = _rs_chips_wait(*early_flight[:6], [out_d[n] for n in out_d])
    joined_early = _rs_join(_rs_add_chips(chip_i, part, parts), core, "rs_join_early")
    for n, _, _ in BIG:
        if PACK_AT[n][0] == "early":
            out_g[n], out_d[n], out_m[n], out_v[n] = _adamw(given[n], joined_early, given["m_" + n], given["v_" + n], n)

    mine = (lax.broadcasted_iota(jnp.int32, (N_DEV, 1, 1), 0) == dev_i[0])
    g_small_pk, d_small_pk, m_small_pk, v_small_pk = _adamw_small(
        _pack_small(small_w, 0.0), jnp.where(mine, small_own[None], small_parts), _pack_small(small_m, 0.0),
        _pack_small(small_v, 1.0))
    loss = g_small_pk[6, 0]
    for out, pk in ((out_g, g_small_pk), (out_d, d_small_pk), (out_m, m_small_pk), (out_v, v_small_pk)):
        out.update(_unpack_small(pk))
    order = ["g_mix_pre", "w_in", "g_q_a", "w_q_b", "g_kv_a", "w_kv_b", "sinks", "w_mla_up", "w_swa_up", "w_out", "g_mix_post",
             "g_mlp_pre", "w_mlp_up", "w_mlp_down", "g_mlp_post", "w_ple", "g_ple", "w_ple_gate"]
    return (loss, gx[None], *[out_g[n] for n in order], *[out_d[n] for n in order], *[out_m[n] for n in order],
            *[out_v[n] for n in order])
```

```python
import math

import jax
import jax.numpy as jnp
from jax import lax
from jax.experimental import pallas as pl
from jax.experimental.pallas import tpu as pltpu

F32 = jnp.float32
BF16 = jnp.bfloat16
SDS = jax.ShapeDtypeStruct

D = 1024
D_FF = 4096
PLE = 256
Q_LORA = 256
KV_LORA = 128
MLA_HEADS = 16
MLA_NOPE = 64
MLA_ROPE = 32
SWA_HEADS = 16
SWA_HD = 64
WINDOW = 128
ROPE_THETA = 10000.0
EPS = 1e-6
NEG = -1e30
NZ = 4096
MLA_SCALE = (MLA_NOPE + MLA_ROPE) ** -0.5
LOG2_E = math.log2(math.e)
MLA_LOG2_SCALE = MLA_SCALE * LOG2_E
SWA_SCALE = SWA_HD ** -0.5

ADAM_LR = 0.001
ADAM_B1 = 0.9
ADAM_B2 = 0.999
ADAM_EPS = 1e-08
ADAM_WD = 0.01
ADAM_STEP = 10

LANES = 128
ATT_COLS = 128
N_CHIPS = 4
N_DEV = 8
MESH = pl.DeviceIdType.MESH

NT = (((1,), (1,)), ((), ()))
TN = (((0,), (0,)), ((), ()))

BIG = (("w_in", 1024, 936), ("w_q_b", 256, 384), ("w_kv_b", 128, 512), ("w_mla_up", 256, 1024),
       ("w_swa_up", 256, 1024), ("w_out", 256, 1024), ("w_mlp_up", 1024, 1024), ("w_mlp_down", 1024, 1024),
       ("w_ple", 256, 256), ("w_ple_gate", 256, 1024))
COL_SHARDED = ("w_in", "w_q_b", "w_kv_b", "w_mlp_up", "w_ple")
PACK_AT = {"w_in": ("early", 0, 0), "w_q_b": ("early", 1024, 0), "w_ple": ("early", 1024, 384), "w_kv_b": ("early", 1280, 0),
           "w_mla_up": ("late", 0, 0), "w_swa_up": ("late", 256, 0), "w_out": ("late", 512, 0), "w_ple_gate": ("late", 768, 0),
           "w_mlp_up": ("late", 1024, 0), "w_mlp_down": ("late", 2048, 0)}
PACK_ROWS = {"early": 1408, "late": 3072}
SMALL = (("g_mix_pre", 1024), ("g_q_a", 256), ("g_kv_a", 128), ("sinks", 16), ("g_mix_post", 1024),
         ("g_mlp_pre", 1024), ("g_mlp_post", 1024), ("g_ple", 1024))


def _dot(a, b):
    return jnp.dot(a, b, preferred_element_type=F32)


def _dot_nt(a, b):
    return lax.dot_general(a, b, NT, preferred_element_type=F32)


def _dot_tn(a, b):
    return lax.dot_general(a, b, TN, preferred_element_type=F32)


def _pcall(body, *, name, out_shape, grid=(), in_specs=None, out_specs=None, scratch=(), sem=None, vmem_mb=56, aliases=None):
    params = dict(vmem_limit_bytes=vmem_mb << 20)
    if sem is not None:
        params["dimension_semantics"] = sem
    return pl.pallas_call(body, name=name, grid=grid, in_specs=in_specs, out_specs=out_specs, out_shape=out_shape,
                          scratch_shapes=list(scratch), input_output_aliases=aliases or {},
                          compiler_params=pltpu.CompilerParams(**params))


def _rows(tm, n, col=0):
    return pl.BlockSpec((tm, n), lambda i: (i, col))


def _full(shape):
    return pl.BlockSpec(shape, lambda i: (0,) * len(shape))


def _rms(x, g):
    r = lax.rsqrt(jnp.mean(x * x, axis=-1, keepdims=True) + EPS)
    return x * r * g


def _rms_bwd(dy, x, g):
    r = lax.rsqrt(jnp.mean(x * x, axis=-1, keepdims=True) + EPS)
    xn = x * r
    dn = dy * g
    dx = r * (dn - xn * jnp.mean(dn * xn, axis=-1, keepdims=True))
    return dx, jnp.sum(dy * xn, axis=0, keepdims=True)


def _sigmoid(x):
    return 1.0 / (1.0 + jnp.exp(-x))


def _rope(x, c, a, b, half):
    return x * c + pltpu.roll(x, LANES - half, 1) * a + pltpu.roll(x, half, 1) * b


def _rope_tables(T, kind):
    lane = jnp.arange(LANES)
    if kind == "mla":
        half = MLA_ROPE // 2
        rel = lane - MLA_NOPE
        on = (rel >= 0) & (rel < MLA_ROPE)
        d = MLA_ROPE
    else:
        half = SWA_HD // 2
        rel = lane % SWA_HD
        on = jnp.ones((LANES,), bool)
        d = SWA_HD
    first = on & (rel < half)
    second = on & (rel >= half)
    f = jnp.where(first, rel, rel - half).astype(F32)
    inv = jnp.exp(-math.log(ROPE_THETA) * f * (2.0 / d))
    ang = jnp.arange(T, dtype=F32)[:, None] * inv[None, :]
    cos, sin = jnp.cos(ang), jnp.sin(ang)
    c = jnp.where(on[None], cos, 1.0)
    a = jnp.where(first[None], -sin, 0.0)
    b = jnp.where(second[None], sin, 0.0)
    return c, a, b


def _fwd_in(x, g1, w_in_p, tm):
    T = x.shape[0]

    def body(x_ref, g_ref, w_ref, z_ref, h_ref):
        h = _rms(x_ref[...], g_ref[...]).astype(BF16)
        h_ref[...] = h
        z_ref[...] = _dot(h, w_ref[...])

    return _pcall(body, name="fwd_in", grid=(T // tm,),
                  in_specs=[_rows(tm, D), _full((1, D)), _full((D, NZ))],
                  out_specs=[_rows(tm, NZ), _rows(tm, D)],
                  out_shape=[SDS((T, NZ), F32), SDS((T, D), BF16)], sem=("parallel",))(x, g1, w_in_p)


def _fwd_qkv(z, gq, gkv, wqb, wkn, wv, tab_m, tab_s, tm):
    T = z.shape[0]

    def body(qa_ref, sq_ref, skd_ref, svd_ref, kva_ref, kr_ref, gq_ref, gkv_ref, wqb_ref, wkn_ref, wv_ref,
             cm_ref, am_ref, bm_ref, cs_ref, as_ref, bs_ref,
             qn_ref, kvn_ref, km_ref, vm_ref, qt_ref, kt_ref, vt_ref, qs_ref, ks_ref, vs_ref):
        qn = _rms(qa_ref[...], gq_ref[...]).astype(BF16)
        qn_ref[...] = qn
        kvn = _rms(kva_ref[...], gkv_ref[...]).astype(BF16)
        kvn_ref[...] = kvn
        cm, am, bm = cm_ref[...], am_ref[...], bm_ref[...]
        cs, as_, bs = cs_ref[...], as_ref[...], bs_ref[...]
        k_rope = _rope(kr_ref[...], cm, am, bm, MLA_ROPE // 2)
        vt_row = lax.broadcasted_iota(jnp.int32, (LANES, tm), 0)
        v_all = _dot(kvn, wv_ref[...])
        q_all = _dot(qn, wqb_ref[...])
        k_all = _dot(kvn, wkn_ref[...])
        for j in range(D // LANES):
            sl = slice(LANES * j, LANES * (j + 1))
            v = v_all[:, sl]
            vm_ref[:, sl] = v.astype(BF16)
            v_t = v.T
            for hh, rows64 in enumerate((v_t, pltpu.roll(v_t, 64, 0))):
                blk = jnp.where(vt_row < 64, rows64, jnp.where(vt_row == 64, 1.0, 0.0))
                vt_ref[0, LANES * (2 * j + hh):LANES * (2 * j + hh + 1), :] = blk.astype(BF16)
        for h in range(MLA_HEADS):
            sl = slice(LANES * h, LANES * (h + 1))
            qh = _rope(q_all[:, sl], cm, am, bm, MLA_ROPE // 2)
            qt_ref[0, sl, :] = qh.T.astype(BF16)
            k = k_all[:, sl] + k_rope
            km_ref[:, sl] = k.astype(BF16)
            kt_ref[0, sl, :] = k.T.astype(BF16)
        for j in range(D // LANES):
            sl = slice(LANES * j, LANES * (j + 1))
            qs_ref[:, sl] = _rope(sq_ref[:, sl], cs, as_, bs, SWA_HD // 2).astype(BF16)
        for j in range(2):
            sl = slice(LANES * j, LANES * (j + 1))
            ks_ref[:, sl] = _rope(skd_ref[:, sl], cs, as_, bs, SWA_HD // 2).astype(BF16)
        vs_ref[...] = svd_ref[...].astype(BF16)

    tab = [_rows(tm, LANES)] * 6
    return _pcall(body, name="fwd_qkv", grid=(T // tm,),
                  in_specs=[_rows(tm, 256, 12), _rows(tm, 1024, 0), _rows(tm, 256, 13), _rows(tm, 256, 14),
                            _rows(tm, 128, 30), _rows(tm, 128, 31), _full((1, Q_LORA)), _full((1, KV_LORA)),
                            _full((Q_LORA, 2048)), _full((KV_LORA, 2048)), _full((KV_LORA, 1024))] + tab,
                  out_specs=[_rows(tm, Q_LORA), _rows(tm, KV_LORA), _rows(tm, 2048), _rows(tm, 1024),
                             pl.BlockSpec((1, 2048, tm), lambda i: (i, 0, 0)), pl.BlockSpec((1, 2048, tm), lambda i: (i, 0, 0)),
                             pl.BlockSpec((1, 2048, tm), lambda i: (i, 0, 0)),
                             _rows(tm, 1024), _rows(tm, 256), _rows(tm, 256)],
                  out_shape=[SDS((T, Q_LORA), BF16), SDS((T, KV_LORA), BF16), SDS((T, 2048), BF16),
                             SDS((T, 1024), BF16), SDS((T // tm, 2048, tm), BF16), SDS((T // tm, 2048, tm), BF16),
                             SDS((T // tm, 2048, tm), BF16),
                             SDS((T, 1024), BF16), SDS((T, 256), BF16), SDS((T, 256), BF16)],
                  sem=("parallel",))(z, z, z, z, z, z, gq, gkv, wqb, wkn, wv, *tab_m, *tab_s)


def _mla_fwd(qt, km, vt, tb):
    T = km.shape[0]
    nb = T // tb
    cc = ATT_COLS

    def body(q_ref, k_ref, vt_ref, o_ref, l_ref, s_ref, p_ref, al_ref, m_ref, acc_ref):
        i = pl.program_id(1)
        m_ref[...] = jnp.full(m_ref.shape, NEG, F32)
        acc_ref[...] = jnp.zeros_like(acc_ref)
        p_ref[1] = jnp.zeros(p_ref.shape[1:], BF16)
        al_ref[1] = jnp.ones(al_ref.shape[1:], F32)
        key = lax.broadcasted_iota(jnp.int32, (tb, cc), 0)
        qry = lax.broadcasted_iota(jnp.int32, (tb, cc), 1)

        def scores(j, slot):
            off = pl.multiple_of(j * tb, tb)
            for hh in range(2):
                sl = slice(LANES * hh, LANES * (hh + 1))
                s_ref[slot, hh] = _dot(k_ref[pl.ds(off, tb), sl], q_ref[0, sl, :])

        def softmax(slot, diagonal):
            chains = [(hh, slice(cc * c, cc * (c + 1)), c) for hh in range(2) for c in range(tb // cc)]

            def scaled(hh, cols, c):
                t = s_ref[slot, hh, :, cols] * MLA_LOG2_SCALE
                return jnp.where(key <= qry + cc * c, t, NEG) if diagonal else t

            tops = []
            for hh, cols, c in chains:
                if diagonal:
                    top = jnp.max(scaled(hh, cols, c), axis=0, keepdims=True)
                else:
                    top = jnp.max(s_ref[slot, hh, :, cols], axis=0, keepdims=True) * MLA_LOG2_SCALE
                m_old = m_ref[hh, :, cols]
                mn = jnp.maximum(m_old, top)
                m_ref[hh, :, cols] = mn
                al_ref[slot, hh, :, cols] = jnp.exp2(m_old - mn)
                tops.append(mn)
            for (hh, cols, c), mn in zip(chains, tops):
                p_ref[slot, hh, :, cols] = jnp.exp2(scaled(hh, cols, c) - mn).astype(BF16)

        def accumulate(j, slot):
            for hh in range(2):
                acc_ref[hh] = al_ref[slot, hh] * acc_ref[hh] + _dot(vt_ref[j, LANES * hh:LANES * (hh + 1), :], p_ref[slot, hh])

        def step(t, carry):
            scores(2 * t + 1, 1)
            accumulate(jnp.maximum(2 * t - 1, 0), 1)
            softmax(0, False)
            scores(2 * t + 2, 0)
            accumulate(2 * t, 0)
            softmax(1, False)
            return carry

        scores(0, 0)
        lax.fori_loop(0, i // 2, step, 0)

        @pl.when(i % 2 == 1)
        def _():
            scores(i, 1)
            accumulate(jnp.maximum(i - 2, 0), 1)
            softmax(0, False)
            accumulate(i - 1, 0)
            softmax(1, True)
            accumulate(i, 1)

        @pl.when(i % 2 == 0)
        def _():
            accumulate(jnp.maximum(i - 1, 0), 1)
            softmax(0, True)
            accumulate(i, 0)
        den = [acc_ref[hh, 64:65, :] for hh in range(2)]
        o_ref[...] = jnp.concatenate([acc_ref[hh, 0:64, :] / den[hh] for hh in range(2)], axis=0).T
        sub = lax.broadcasted_iota(jnp.int32, (8, tb), 0)
        lse = [m_ref[hh] + jnp.log(den[hh]) * LOG2_E for hh in range(2)]
        l_ref[0, 0] = jnp.where(sub == 0, lse[0], jnp.where(sub == 1, lse[1], 0.0))

    return _pcall(body, name="mla_fwd", grid=(MLA_HEADS // 2, nb),
                  in_specs=[pl.BlockSpec((1, 256, tb), lambda p, i: (i, p, 0)), pl.BlockSpec((T, 256), lambda p, i: (0, p)),
                            pl.BlockSpec((nb, 2 * LANES, tb), lambda p, i: (0, p, 0))],
                  out_specs=[pl.BlockSpec((tb, LANES), lambda p, i: (i, p)),
                             pl.BlockSpec((1, 1, 8, tb), lambda p, i: (p, i, 0, 0))],
                  out_shape=[SDS((T, D), F32), SDS((MLA_HEADS // 2, nb, 8, tb), F32)],
                  scratch=[pltpu.VMEM((2, 2, tb, tb), F32), pltpu.VMEM((2, 2, tb, tb), BF16), pltpu.VMEM((2, 2, 1, tb), F32),
                           pltpu.VMEM((2, 1, tb), F32), pltpu.VMEM((2, LANES, tb), F32)],
                  sem=("parallel", "arbitrary"))(qt, km, vt)


def _swa_mask(n):
    row = lax.broadcasted_iota(jnp.int32, (WINDOW, 2 * WINDOW), 0)
    col = lax.broadcasted_iota(jnp.int32, (WINDOW, 2 * WINDOW), 1)
    rel = row - col + WINDOW
    return (rel >= 0) & (rel < WINDOW) & ((col >= WINDOW) | (n > 0))


def _swa_specs(T):
    nb = T // WINDOW
    cur = lambda w: pl.BlockSpec((WINDOW, w), lambda n: (n, 0))
    prev = lambda w: pl.BlockSpec((WINDOW, w), lambda n: (jnp.maximum(n - 1, 0), 0))
    return nb, cur, prev


def _swa_fwd(sinks, qs, ks, vs):
    T = qs.shape[0]
    nb, cur, prev = _swa_specs(T)

    def body(sink_ref, q_ref, kc_ref, kp_ref, vc_ref, vp_ref, o_ref, l_ref, kb_ref, vb_ref, s_ref, p_ref):
        n = pl.program_id(0)
        mask = _swa_mask(n)
        lo = lax.broadcasted_iota(jnp.int32, (WINDOW, LANES), 1) < 64
        hi = jnp.logical_not(lo)
        for g in range(2):
            gs = slice(LANES * g, LANES * (g + 1))
            kb_ref[g] = jnp.concatenate([kp_ref[:, gs], kc_ref[:, gs]], axis=0)
            vb_ref[g] = jnp.concatenate([vp_ref[:, gs], vc_ref[:, gs]], axis=0)
        for h in range(SWA_HEADS):
            qp = q_ref[:, LANES * (h // 2):LANES * (h // 2 + 1)]
            qh = jnp.where(lo if h % 2 == 0 else hi, qp, jnp.zeros_like(qp))
            s_ref[h] = _dot_nt(qh, kb_ref[h // 8])
        for j in range(SWA_HEADS // 2):
            sl = slice(LANES * j, LANES * (j + 1))
            lses = []
            for h in (2 * j, 2 * j + 1):
                s = jnp.where(mask, s_ref[h] * SWA_SCALE, NEG)
                sk = sink_ref[h]
                m = jnp.maximum(jnp.max(s, axis=1, keepdims=True), sk)
                e = jnp.exp(s - m)
                den = jnp.sum(e, axis=1, keepdims=True) + jnp.exp(sk - m)
                p_ref[h] = (e / den).astype(BF16)
                lses.append(jnp.broadcast_to(m + jnp.log(den), (WINDOW, LANES)))
            l_ref[:, sl] = jnp.where(lo, lses[0], lses[1])
        for j in range(SWA_HEADS // 2):
            vb = vb_ref[j // 4]
            o_ref[:, LANES * j:LANES * (j + 1)] = jnp.where(lo, _dot(p_ref[2 * j], vb), _dot(p_ref[2 * j + 1], vb))

    return _pcall(body, name="swa_fwd", grid=(nb,),
                  in_specs=[pl.BlockSpec(memory_space=pltpu.SMEM), cur(D), cur(256), prev(256), cur(256), prev(256)],
                  out_specs=[cur(D), cur(D)], out_shape=[SDS((T, D), F32)] * 2,
                  scratch=[pltpu.VMEM((2, 2 * WINDOW, LANES), BF16), pltpu.VMEM((2, 2 * WINDOW, LANES), BF16),
                           pltpu.VMEM((SWA_HEADS, WINDOW, 2 * WINDOW), F32), pltpu.VMEM((SWA_HEADS, WINDOW, 2 * WINDOW), BF16)],
                  sem=("parallel",))(sinks, qs, ks, ks, vs, vs)


def _fwd_mix(om, os_, z, x, wmu, wsu, wo, g2, tm):
    T = x.shape[0]

    def body(om_ref, os_ref, ga_ref, gb_ref, x_ref, wmu_ref, wsu_ref, wo_ref, g2_ref,
             y_ref, yo_ref, au_ref, bu_ref, x1_ref):
        au = _dot(om_ref[...].astype(BF16), wmu_ref[...])
        bu = _dot(os_ref[...].astype(BF16), wsu_ref[...])
        au_ref[...] = au
        bu_ref[...] = bu
        y = (_sigmoid(ga_ref[...]) * au + _sigmoid(gb_ref[...]) * bu).astype(BF16)
        y_ref[...] = y
        yo = _dot(y, wo_ref[...])
        yo_ref[...] = yo
        x1_ref[...] = x_ref[...] + _rms(yo, g2_ref[...])

    r = _rows(tm, D)
    w = _full((D, D))
    return _pcall(body, name="fwd_mix", grid=(T // tm,),
                  in_specs=[r, r, _rows(tm, D, 1), _rows(tm, D, 2), r, w, w, w, _full((1, D))],
                  out_specs=[r] * 5,
                  out_shape=[SDS((T, D), BF16), SDS((T, D), F32), SDS((T, D), F32), SDS((T, D), F32), SDS((T, D), F32)],
                  sem=("parallel",))(om, os_, z, z, x, wmu, wsu, wo, g2)


def _fwd_mlp_up(x1, g3, w1, tm):
    T = x1.shape[0]

    def body(x_ref, g_ref, w_ref, h_ref, u_ref):
        h = _rms(x_ref[...], g_ref[...]).astype(BF16)
        h_ref[...] = h
        u_ref[...] = jnp.square(jnp.maximum(_dot(h, w_ref[...]), 0.0)).astype(BF16)

    return _pcall(body, name="fwd_mlp_up", grid=(T // tm,),
                  in_specs=[_rows(tm, D), _full((1, D)), _full((D, D_FF))],
                  out_specs=[_rows(tm, D), _rows(tm, D_FF)],
                  out_shape=[SDS((T, D), BF16), SDS((T, D_FF), BF16)],
                  sem=("parallel",))(x1, g3, w1)


def _fwd_mlp_down(u, w2, x1, g4, tm):
    T = x1.shape[0]

    def body(u_ref, w_ref, x_ref, g_ref, d_ref, x2_ref):
        d = _dot(u_ref[...], w_ref[...])
        d_ref[...] = d
        x2_ref[...] = x_ref[...] + _rms(d, g_ref[...])

    return _pcall(body, name="fwd_mlp_down", grid=(T // tm,),
                  in_specs=[_rows(tm, D_FF), _full((D_FF, D)), _rows(tm, D), _full((1, D))],
                  out_specs=[_rows(tm, D), _rows(tm, D)], out_shape=[SDS((T, D), F32)] * 2,
                  sem=("parallel",))(u, w2, x1, g4)


def _ple_fwd_bwd(p, x2, tgt, wple, g5, wpg, tm):
    T = x2.shape[0]

    def body(p_ref, x2_ref, t_ref, wple_ref, g5_ref, wpg_ref, loss_ref, dx2_ref, dgt_ref, de0_ref, dg5_ref):
        @pl.when(pl.program_id(0) == 0)
        def _():
            loss_ref[...] = jnp.zeros_like(loss_ref)
            dg5_ref[...] = jnp.zeros_like(dg5_ref)

        e0 = _dot(p_ref[...].astype(BF16), wple_ref[...])
        g5 = g5_ref[...]
        r = lax.rsqrt(jnp.mean(e0 * e0, axis=-1, keepdims=True) + EPS)
        en = e0 * r
        e = en * g5
        x2 = x2_ref[...]
        s = _sigmoid(_dot(x2.astype(BF16), wpg_ref[...]))
        diff = x2 + s * e - t_ref[...]
        sq = jnp.sum(jnp.sum(diff * diff, axis=1, keepdims=True), axis=0, keepdims=True)
        loss_ref[...] += jnp.broadcast_to(sq * (0.5 / D), loss_ref.shape)
        dx3 = diff * (1.0 / D)
        de = dx3 * s
        dgt = (dx3 * e * s * (1.0 - s)).astype(BF16)
        dgt_ref[...] = dgt
        dn = de * g5
        de0_ref[...] = (r * (dn - en * jnp.mean(dn * en, axis=-1, keepdims=True))).astype(BF16)
        dg5_ref[...] += jnp.sum(de * en, axis=0, keepdims=True)
        dx2_ref[...] = dx3 + _dot_nt(dgt, wpg_ref[...])

    r = _rows(tm, D)
    return _pcall(body, name="ple_fwd_bwd", grid=(T // tm,),
                  in_specs=[_rows(tm, PLE), r, r, _full((PLE, D)), _full((1, D)), _full((D, D))],
                  out_specs=[_full((8, LANES)), r, r, r, _full((1, D))],
                  out_shape=[SDS((8, LANES), F32), SDS((T, D), F32), SDS((T, D), BF16), SDS((T, D), BF16), SDS((1, D), F32)],
                  sem=("arbitrary",))(p, x2, tgt, wple, g5, wpg)


def _bwd_mlp_down(dx2, d, g4, w2, u, tm):
    T = dx2.shape[0]

    def body(dx_ref, d_ref, g_ref, w_ref, u_ref, dd_ref, da_ref, dg_ref):
        @pl.when(pl.program_id(0) == 0)
        def _():
            dg_ref[...] = jnp.zeros_like(dg_ref)

        dd, dg = _rms_bwd(dx_ref[...], d_ref[...], g_ref[...])
        dg_ref[...] += dg
        ddb = dd.astype(BF16)
        dd_ref[...] = ddb
        du = _dot_nt(ddb, w_ref[...])
        da_ref[...] = (du * (2.0 * jnp.sqrt(u_ref[...].astype(F32)))).astype(BF16)

    return _pcall(body, name="bwd_mlp_down", grid=(T // tm,),
                  in_specs=[_rows(tm, D), _rows(tm, D), _full((1, D)), _full((D_FF, D)), _rows(tm, D_FF)],
                  out_specs=[_rows(tm, D), _rows(tm, D_FF), _full((1, D))],
                  out_shape=[SDS((T, D), BF16), SDS((T, D_FF), BF16), SDS((1, D), F32)],
                  sem=("arbitrary",))(dx2, d, g4, w2, u)


def _bwd_mlp_up(da, w1, x1, g3, dx2, tm):
    T = dx2.shape[0]

    def body(da_ref, w_ref, x_ref, g_ref, dx2_ref, dx1_ref, dg_ref):
        @pl.when(pl.program_id(0) == 0)
        def _():
            dg_ref[...] = jnp.zeros_like(dg_ref)

        dh = _dot_nt(da_ref[...], w_ref[...])
        dx, dg = _rms_bwd(dh, x_ref[...], g_ref[...])
        dg_ref[...] += dg
        dx1_ref[...] = dx2_ref[...] + dx

    return _pcall(body, name="bwd_mlp_up", grid=(T // tm,),
                  in_specs=[_rows(tm, D_FF), _full((D, D_FF)), _rows(tm, D), _full((1, D)), _rows(tm, D)],
                  out_specs=[_rows(tm, D), _full((1, D))],
                  out_shape=[SDS((T, D), F32), SDS((1, D), F32)], sem=("arbitrary",))(da, w1, x1, g3, dx2)


def _bwd_mix(dx1, yo, g2, wo, z, au, bu, wmu, wsu, om, tm):
    T = dx1.shape[0]

    def body(dx_ref, yo_ref, g_ref, wo_ref, ga_ref, gb_ref, au_ref, bu_ref, wmu_ref, wsu_ref, om_ref,
             dyo_ref, dg_ref, dau_ref, dbu_ref, dga_ref, dgb_ref, dos_ref, dl_ref, dot_ref):
        @pl.when(pl.program_id(0) == 0)
        def _():
            dg_ref[...] = jnp.zeros_like(dg_ref)

        dyo, dg = _rms_bwd(dx_ref[...], yo_ref[...], g_ref[...])
        dg_ref[...] += dg
        dyob = dyo.astype(BF16)
        dyo_ref[...] = dyob
        dy = _dot_nt(dyob, wo_ref[...])
        sa = _sigmoid(ga_ref[...])
        sb = _sigmoid(gb_ref[...])
        dau = (dy * sa).astype(BF16)
        dbu = (dy * sb).astype(BF16)
        dau_ref[...] = dau
        dbu_ref[...] = dbu
        dga_ref[...] = (dy * au_ref[...] * sa * (1.0 - sa)).astype(BF16)
        dgb_ref[...] = (dy * bu_ref[...] * sb * (1.0 - sb)).astype(BF16)
        dom = _dot_nt(dau, wmu_ref[...])
        dos_ref[...] = _dot_nt(dbu, wsu_ref[...])
        prod = dom * om_ref[...]
        sub = lax.broadcasted_iota(jnp.int32, (8, tm), 0)
        for pr in range(MLA_HEADS // 2):
            sl = slice(LANES * pr, LANES * (pr + 1))
            pt = prod[:, sl].T
            d0 = jnp.sum(pt[0:64], axis=0, keepdims=True)
            d1 = jnp.sum(pt[64:128], axis=0, keepdims=True)
            dl_ref[pr, 0] = jnp.where(sub == 0, d0, jnp.where(sub == 1, d1, 0.0))
            dot_ref[0, sl, :] = dom[:, sl].T.astype(BF16)

    r = _rows(tm, D)
    w = _full((D, D))
    return _pcall(body, name="bwd_mix", grid=(T // tm,),
                  in_specs=[r, r, _full((1, D)), w, _rows(tm, D, 1), _rows(tm, D, 2), r, r, w, w, r],
                  out_specs=[r, _full((1, D)), r, r, r, r, r, pl.BlockSpec((MLA_HEADS // 2, 1, 8, tm), lambda i: (0, i, 0, 0)),
                             pl.BlockSpec((1, D, tm), lambda i: (i, 0, 0))],
                  out_shape=[SDS((T, D), BF16), SDS((1, D), F32), SDS((T, D), BF16), SDS((T, D), BF16), SDS((T, D), BF16),
                             SDS((T, D), BF16), SDS((T, D), F32), SDS((MLA_HEADS // 2, T // tm, 8, tm), F32),
                             SDS((T // tm, D, tm), BF16)],
                  sem=("arbitrary",))(dx1, yo, g2, wo, z, z, au, bu, wmu, wsu, om)


def _mla_bwd(qt, km, kt, vm, dot, lse, delta, tb):
    T = km.shape[0]
    nb = T // tb
    cc = ATT_COLS

    def body(qt_ref, k_ref, kt_ref, v_ref, dot_ref, l_ref, dl_ref, dqt_ref, dkt_ref, dvt_ref,
             s_ref, dp_ref, p_ref, ds_ref, vh_ref):
        j = pl.program_id(1)

        @pl.when(j == 0)
        def _():
            dqt_ref[...] = jnp.zeros_like(dqt_ref)

        dkt_ref[...] = jnp.zeros_like(dkt_ref)
        dvt_ref[...] = jnp.zeros_like(dvt_ref)
        lo = lax.broadcasted_iota(jnp.int32, (tb, LANES), 1) < 64
        key = lax.broadcasted_iota(jnp.int32, (tb, cc), 0)
        qry = lax.broadcasted_iota(jnp.int32, (tb, cc), 1)
        v = v_ref[...]
        vh_ref[0] = jnp.where(lo, v, jnp.zeros_like(v))
        vh_ref[1] = jnp.where(lo, jnp.zeros_like(v), v)

        def scores(i, slot):
            for hh in range(2):
                sl = slice(LANES * hh, LANES * (hh + 1))
                s_ref[slot, hh] = _dot(k_ref[:, sl], qt_ref[i, sl, :])
                dp_ref[slot, hh] = _dot(vh_ref[hh], dot_ref[i])

        def grads(i, slot, diagonal):
            lse_i = l_ref[0, i]
            delta_i = dl_ref[0, i]
            for hh in range(2):
                for c in range(tb // cc):
                    cols = slice(cc * c, cc * (c + 1))
                    p = jnp.exp2(s_ref[slot, hh, :, cols] * MLA_LOG2_SCALE - lse_i[hh:hh + 1, cols])
                    if diagonal:
                        p = jnp.where(key <= qry + cc * c, p, 0.0)
                    p_ref[hh, :, cols] = p.astype(BF16)
                    ds_ref[hh, :, cols] = (p * (dp_ref[slot, hh, :, cols] - delta_i[hh:hh + 1, cols]) * MLA_SCALE).astype(BF16)
            for hh in range(2):
                sl = slice(LANES * hh, LANES * (hh + 1))
                half = slice(64 * hh, 64 * (hh + 1))
                dvt_ref[0, half, :] += _dot_nt(dot_ref[i, half, :], p_ref[hh])
                dkt_ref[0, sl, :] += _dot_nt(qt_ref[i, sl, :], ds_ref[hh])
                dqt_ref[i, sl, :] += _dot(kt_ref[0, sl, :], ds_ref[hh])

        n_off = nb - 1 - j

        def step(u, carry):
            i0 = j + 1 + 2 * u
            scores(i0 + 1, 1)
            grads(i0, 0, False)
            scores(jnp.where(i0 + 2 < nb, i0 + 2, j), 0)
            grads(i0 + 1, 1, False)
            return carry

        scores(jnp.where(n_off > 0, j + 1, j), 0)
        lax.fori_loop(0, n_off // 2, step, 0)

        @pl.when(n_off % 2 == 1)
        def _():
            scores(j, 1)
            grads(nb - 1, 0, False)
            grads(j, 1, True)

        @pl.when(n_off % 2 == 0)
        def _():
            grads(j, 0, True)

    blk = lambda w: pl.BlockSpec((tb, w), lambda p, j: (j, p))
    stat = pl.BlockSpec((1, nb, 8, tb), lambda p, j: (p, 0, 0, 0))
    pair_t = lambda w: pl.BlockSpec((nb, w, tb), lambda p, j: (0, p, 0))
    blk_t = lambda w: pl.BlockSpec((1, w, tb), lambda p, j: (j, p, 0))
    return _pcall(body, name="mla_bwd", grid=(MLA_HEADS // 2, nb),
                  in_specs=[pair_t(256), blk(256), blk_t(256), blk(LANES), pair_t(LANES), stat, stat],
                  out_specs=[pair_t(256), blk_t(256), blk_t(LANES)],
                  out_shape=[SDS((nb, 2048, tb), F32), SDS((nb, 2048, tb), F32), SDS((nb, D, tb), F32)],
                  scratch=[pltpu.VMEM((2, 2, tb, tb), F32), pltpu.VMEM((2, 2, tb, tb), F32), pltpu.VMEM((2, tb, tb), BF16),
                           pltpu.VMEM((2, tb, tb), BF16), pltpu.VMEM((2, tb, LANES), BF16)],
                  sem=("parallel", "arbitrary"))(qt, km, kt, vm, dot, lse, delta)


def _swa_bwd(sinks, qs, ks, vs, do, o, lse):
    T = qs.shape[0]
    nb, cur, prev = _swa_specs(T)

    def body(sink_ref, q_ref, kc_ref, kp_ref, vc_ref, vp_ref, do_ref, o_ref, l_ref,
             dq_ref, dkc_ref, dkp_ref, dvc_ref, dvp_ref, dsink_ref, kb_ref, vb_ref, s_ref, dp_ref, p_ref, ds_ref):
        n = pl.program_id(0)

        @pl.when(n == 0)
        def _():
            dsink_ref[...] = jnp.zeros_like(dsink_ref)

        mask = _swa_mask(n)
        lo = lax.broadcasted_iota(jnp.int32, (WINDOW, LANES), 1) < 64
        hi = jnp.logical_not(lo)
        lane8 = lax.broadcasted_iota(jnp.int32, (8, LANES), 1)
        for g in range(2):
            gs = slice(LANES * g, LANES * (g + 1))
            kb_ref[g] = jnp.concatenate([kp_ref[:, gs], kc_ref[:, gs]], axis=0)
            vb_ref[g] = jnp.concatenate([vp_ref[:, gs], vc_ref[:, gs]], axis=0)

        def head(h):
            sl = slice(LANES * (h // 2), LANES * (h // 2 + 1))
            hm = lo if h % 2 == 0 else hi
            qp = q_ref[:, sl]
            return hm, sl, jnp.where(hm, qp, jnp.zeros_like(qp)), jnp.where(hm, do_ref[:, sl], 0.0).astype(BF16)

        for h in range(SWA_HEADS):
            _, _, qh, dom = head(h)
            s_ref[h] = _dot_nt(qh, kb_ref[h // 8])
            dp_ref[h] = _dot_nt(dom, vb_ref[h // 8])
        dsink = jnp.zeros((8, LANES), F32)
        for h in range(SWA_HEADS):
            hm, sl, _, _ = head(h)
            lse_h = jnp.max(jnp.where(hm, l_ref[:, sl], -jnp.inf), axis=1, keepdims=True)
            delta = jnp.sum(jnp.where(hm, do_ref[:, sl] * o_ref[:, sl], 0.0), axis=1, keepdims=True)
            p = jnp.exp(jnp.where(mask, s_ref[h] * SWA_SCALE, NEG) - lse_h)
            p_ref[h] = p.astype(BF16)
            ds_ref[h] = (p * (dp_ref[h] - delta) * SWA_SCALE).astype(BF16)
            d_sink = -jnp.sum(jnp.exp(sink_ref[h] - lse_h) * delta, axis=0, keepdims=True)
            dsink = dsink + jnp.where(lane8 == h, d_sink, 0.0)
        dsink_ref[...] += dsink
        for g in range(2):
            gs = slice(LANES * g, LANES * (g + 1))
            dkb = jnp.zeros((2 * WINDOW, LANES), F32)
            dvb = jnp.zeros((2 * WINDOW, LANES), F32)
            for j in range(4 * g, 4 * g + 4):
                dqs = []
                for h in (2 * j, 2 * j + 1):
                    _, _, qh, dom = head(h)
                    dvb = dvb + _dot_tn(p_ref[h], dom)
                    dkb = dkb + _dot_tn(ds_ref[h], qh)
                    dqs.append(_dot(ds_ref[h], kb_ref[g]))
                dq_ref[:, LANES * j:LANES * (j + 1)] = jnp.where(lo, dqs[0], dqs[1])
            dkp_ref[:, gs] = dkb[:WINDOW]
            dkc_ref[:, gs] = dkb[WINDOW:]
            dvp_ref[:, gs] = dvb[:WINDOW]
            dvc_ref[:, gs] = dvb[WINDOW:]

    band = pltpu.VMEM((2, 2 * WINDOW, LANES), BF16)
    return _pcall(body, name="swa_bwd", grid=(nb,),
                  in_specs=[pl.BlockSpec(memory_space=pltpu.SMEM), cur(D), cur(256), prev(256), cur(256), prev(256),
                            cur(D), cur(D), cur(D)],
                  out_specs=[cur(D), cur(256), cur(256), cur(256), cur(256), _full((8, LANES))],
                  out_shape=[SDS((T, D), F32), SDS((T, 256), F32), SDS((T, 256), F32), SDS((T, 256), F32), SDS((T, 256), F32),
                             SDS((8, LANES), F32)],
                  scratch=[band, band, pltpu.VMEM((SWA_HEADS, WINDOW, 2 * WINDOW), F32),
                           pltpu.VMEM((SWA_HEADS, WINDOW, 2 * WINDOW), F32), pltpu.VMEM((SWA_HEADS, WINDOW, 2 * WINDOW), BF16),
                           pltpu.VMEM((SWA_HEADS, WINDOW, 2 * WINDOW), BF16)],
                  sem=("arbitrary",))(sinks, qs, ks, ks, vs, vs, do, o, lse)


def _bwd_qkv(dqm, dkm, dvm, dqs, dkc, dkp, dvc, dvp, z, gq, gkv, wqb, wkn, wv, tab_m, tab_s):
    T = z.shape[0]
    tm = WINDOW
    nb = T // tm
    per = dqm.shape[2] // tm

    def body(dqm_ref, dkm_ref, dvm_ref, dqs_ref, dkc_ref, dkp_ref, dvc_ref, dvp_ref, qa_ref, kva_ref, gq_ref, gkv_ref,
             wqb_ref, wkn_ref, wv_ref, cm_ref, am_ref, bm_ref, cs_ref, as_ref, bs_ref,
             dq_out, dkn_out, dv_out, dsq_ref, drest_ref, dgq_ref, dgkv_ref):
        i = pl.program_id(0)

        @pl.when(i == 0)
        def _():
            dgq_ref[...] = jnp.zeros_like(dgq_ref)
            dgkv_ref[...] = jnp.zeros_like(dgkv_ref)

        cm, am, bm = cm_ref[...], -am_ref[...], -bm_ref[...]
        cs, as_, bs = cs_ref[...], -as_ref[...], -bs_ref[...]
        lane = lax.broadcasted_iota(jnp.int32, (tm, LANES), 1)
        nope = lane < MLA_NOPE
        roped = jnp.logical_and(lane >= MLA_NOPE, lane < MLA_NOPE + MLA_ROPE)
        dkr = jnp.zeros((tm, LANES), F32)
        for h in range(MLA_HEADS):
            sl = slice(LANES * h, LANES * (h + 1))
            dq_out[:, sl] = _rope(dqm_ref[0, sl, :].T, cm, am, bm, MLA_ROPE // 2).astype(BF16)
            dk_h = dkm_ref[0, sl, :].T
            dkn_out[:, sl] = jnp.where(nope, dk_h, 0.0).astype(BF16)
            dkr = dkr + jnp.where(roped, dk_h, 0.0)
        for j in range(D // LANES):
            sl = slice(LANES * j, LANES * (j + 1))
            dv_out[:, sl] = dvm_ref[0, sl, :].T.astype(BF16)
        dqn = _dot_nt(dq_out[...], wqb_ref[...])
        dkvn = _dot_nt(dkn_out[...], wkn_ref[...]) + _dot_nt(dv_out[...], wv_ref[...])
        dqa, dgq = _rms_bwd(dqn, qa_ref[...], gq_ref[...])
        dkva, dgkv = _rms_bwd(dkvn, kva_ref[...], gkv_ref[...])
        dgq_ref[...] += dgq
        dgkv_ref[...] += dgkv
        for j in range(D // LANES):
            sl = slice(LANES * j, LANES * (j + 1))
            dsq_ref[:, sl] = _rope(dqs_ref[:, sl], cs, as_, bs, SWA_HD // 2).astype(BF16)
        keep = (i < nb - 1).astype(F32)
        drest_ref[:, 0:256] = dqa.astype(BF16)
        for j in range(2):
            sl = slice(LANES * j, LANES * (j + 1))
            dk = dkc_ref[:, sl] + keep * dkp_ref[:, sl]
            drest_ref[:, 256 + LANES * j:256 + LANES * (j + 1)] = _rope(dk, cs, as_, bs, SWA_HD // 2).astype(BF16)
        drest_ref[:, 512:768] = (dvc_ref[...] + keep * dvp_ref[...]).astype(BF16)
        drest_ref[:, 768:896] = dkva.astype(BF16)
        drest_ref[:, 896:1024] = _rope(dkr, cm, am, bm, MLA_ROPE // 2).astype(BF16)

    nxt = pl.BlockSpec((tm, 256), lambda i: (jnp.minimum(i + 1, nb - 1), 0))
    tab = [_rows(tm, LANES)] * 6
    return _pcall(body, name="bwd_qkv", grid=(nb,),
                  in_specs=[pl.BlockSpec((1, 2048, tm), lambda i: (i // per, 0, i % per)),
                            pl.BlockSpec((1, 2048, tm), lambda i: (i // per, 0, i % per)),
                            pl.BlockSpec((1, 1024, tm), lambda i: (i // per, 0, i % per)), _rows(tm, 1024), _rows(tm, 256), nxt,
                            _rows(tm, 256), nxt, _rows(tm, 256, 12), _rows(tm, 128, 30), _full((1, Q_LORA)), _full((1, KV_LORA)),
                            _full((Q_LORA, 2048)), _full((KV_LORA, 2048)), _full((KV_LORA, 1024))] + tab,
                  out_specs=[_rows(tm, 2048), _rows(tm, 2048), _rows(tm, 1024), _rows(tm, 1024), _rows(tm, 1024),
                             _full((1, Q_LORA)), _full((1, KV_LORA))],
                  out_shape=[SDS((T, 2048), BF16), SDS((T, 2048), BF16), SDS((T, 1024), BF16), SDS((T, 1024), BF16),
                             SDS((T, 1024), BF16), SDS((1, Q_LORA), F32), SDS((1, KV_LORA), F32)],
                  sem=("arbitrary",))(dqm, dkm, dvm, dqs, dkc, dkp, dvc, dvp, z, z, gq, gkv, wqb, wkn, wv, *tab_m, *tab_s)


def _bwd_in(dsq, dga, dgb, drest, w_in_p, x, g1, dx1, tm):
    T = x.shape[0]

    def body(a_ref, b_ref, c_ref, d_ref, w_ref, x_ref, g_ref, dx1_ref, dx_ref, dg_ref):
        @pl.when(pl.program_id(0) == 0)
        def _():
            dg_ref[...] = jnp.zeros_like(dg_ref)

        dh = (_dot_nt(a_ref[...], w_ref[:, 0:1024]) + _dot_nt(b_ref[...], w_ref[:, 1024:2048])
              + _dot_nt(c_ref[...], w_ref[:, 2048:3072]) + _dot_nt(d_ref[...], w_ref[:, 3072:4096]))
        dx, dg = _rms_bwd(dh, x_ref[...], g_ref[...])
        dg_ref[...] += dg
        dx_ref[...] = dx1_ref[...] + dx

    r = _rows(tm, D)
    return _pcall(body, name="bwd_in", grid=(T // tm,),
                  in_specs=[r, r, r, r, _full((D, NZ)), r, _full((1, D)), r],
                  out_specs=[r, _full((1, D))], out_shape=[SDS((T, D), F32), SDS((1, D), F32)],
                  sem=("arbitrary",))(dsq, dga, dgb, drest, w_in_p, x, g1, dx1)


def _wgrad(a, g, name, into=None):
    T, K = a.shape
    N = g.shape[1]
    tk, tn, tt = min(K, 1024), min(N, 1024), min(T, 1024)
    if into is not None:
        buf, weight = into
        _, row0, lane0 = PACK_AT[weight]
        shard = {n: (r, c) for n, r, c in BIG}[weight]
        assert lane0 == 0 and shard[1] == D and tk % shard[0] == 0
        per_step = tk // shard[0]
    assert K % tk == 0 and N % tn == 0 and T % tt == 0, (a.shape, g.shape)
    steps = T // tt

    def body(a_ref, g_ref, *rest):
        o_ref, acc_ref = rest[-2:]
        t = pl.program_id(2)

        @pl.when(t == 0)
        def _():
            acc_ref[...] = jnp.zeros_like(acc_ref)

        acc_ref[...] += _dot_tn(a_ref[...].astype(BF16), g_ref[...].astype(BF16))

        @pl.when(t == steps - 1)
        def _():
            o_ref[...] = acc_ref[...].astype(o_ref.dtype).reshape(o_ref.shape)

    in_specs = [pl.BlockSpec((tt, tk), lambda k, n, t: (t, k)), pl.BlockSpec((tt, tn), lambda k, n, t: (t, n))]
    if into is None:
        return _pcall(body, name=name, grid=(K // tk, N // tn, steps), in_specs=in_specs,
                      out_specs=pl.BlockSpec((tk, tn), lambda k, n, t: (k, n)), out_shape=SDS((K, N), F32),
                      scratch=[pltpu.VMEM((tk, tn), F32)], sem=("parallel", "parallel", "arbitrary"))(a, g)
    assert row0 % shard[0] == 0 and (K // tk) * (N // tn) * per_step == N_CHIPS
    return _pcall(body, name=name, grid=(K // tk, N // tn, steps), in_specs=in_specs + [ANY],
                  out_specs=pl.BlockSpec((per_step, shard[0], tn), lambda k, n, t: (k + n, row0 // shard[0], 0)),
                  out_shape=SDS(buf.shape, buf.dtype),
                  scratch=[pltpu.VMEM((tk, tn), F32)], sem=("parallel", "parallel", "arbitrary"), aliases={2: 0})(a, g, buf)


def _adamw(w, packed_g, m, v, name):
    _, R, C = w.shape
    _, row0, lane0 = PACK_AT[name]
    tr = min(R, 256 if row0 % 256 == 0 else 128)
    assert row0 % tr == 0 and R % tr == 0

    def body(w_ref, g_ref, m_ref, v_ref, go_ref, d_ref, m2_ref, v2_ref):
        g_ = g_ref[:, lane0:lane0 + C]
        go_ref[0] = g_
        m2 = ADAM_B1 * m_ref[0] + (1.0 - ADAM_B1) * g_
        v2 = ADAM_B2 * v_ref[0] + (1.0 - ADAM_B2) * jnp.square(g_)
        m_hat = m2 / (1.0 - ADAM_B1 ** ADAM_STEP)
        v_hat = v2 / (1.0 - ADAM_B2 ** ADAM_STEP)
        d_ref[0] = -ADAM_LR * (m_hat / (jnp.sqrt(v_hat) + ADAM_EPS) + ADAM_WD * w_ref[0])
        m2_ref[0] = m2
        v2_ref[0] = v2

    r = pl.BlockSpec((1, tr, C), lambda i: (0, i, 0))
    return _pcall(body, name="adamw_" + name, grid=(R // tr,),
                  in_specs=[r, pl.BlockSpec((tr, D), lambda i: (row0 // tr + i, 0)), r, r], out_specs=[r] * 4,
                  out_shape=[SDS((1, R, C), F32)] * 4, sem=("parallel",))(w, packed_g, m, v)


def _adamw_small(w, parts, m, v):
    def body(w_ref, p_ref, m_ref, v_ref, g_ref, d_ref, m2_ref, v2_ref):
        g_ = p_ref[0]
        for k in range(1, N_DEV):
            g_ = g_ + p_ref[k]
        g_ref[...] = g_
        m2 = ADAM_B1 * m_ref[...] + (1.0 - ADAM_B1) * g_
        v2 = ADAM_B2 * v_ref[...] + (1.0 - ADAM_B2) * jnp.square(g_)
        m_hat = m2 / (1.0 - ADAM_B1 ** ADAM_STEP)
        v_hat = v2 / (1.0 - ADAM_B2 ** ADAM_STEP)
        d_ref[...] = -ADAM_LR * (m_hat / (jnp.sqrt(v_hat) + ADAM_EPS) + ADAM_WD * w_ref[...])
        m2_ref[...] = m2
        v2_ref[...] = v2

    s = _full((8, D))
    return _pcall(body, name="adamw_small", grid=(1,), in_specs=[s, _full((N_DEV, 8, D)), s, s], out_specs=[s] * 4,
                  out_shape=[SDS((8, D), F32)] * 4, sem=("arbitrary",))(w, parts, m, v)


ANY = pl.BlockSpec(memory_space=pl.ANY)


def _place():
    x, y, c = lax.axis_index("x"), lax.axis_index("y"), lax.axis_index("c")
    chips = [(1 - x, y), (x, 1 - y), (1 - x, 1 - y)]
    return x, y, c, chips


def _all_gather(wpk):
    rows = wpk.shape[0]
    HALF = rows // 2
    assert HALF % 16 == 0

    def body(in_ref, out_ref, send_sems, recv_sems):
        x, y, c, chips = _place()
        half = pl.ds(pl.multiple_of(c * HALF, 16), HALF)
        other = pl.ds(pl.multiple_of((1 - c) * HALF, 16), HALF)

        def copy(k, src, dst, to):
            return pltpu.make_async_remote_copy(src_ref=src, dst_ref=dst, send_sem=send_sems.at[k], recv_sem=recv_sems.at[k],
                                                device_id=to, device_id_type=MESH)

        first = [copy(k, in_ref.at[half], out_ref.at[2 * x + y, half], (cx, cy, c)) for k, (cx, cy) in enumerate(chips)]
        for cp in first:
            cp.start()
        passed = []
        for k, (cx, cy) in enumerate(chips):
            slot = out_ref.at[2 * cx + cy, half]
            copy(k, slot, slot, (x, y, c)).wait_recv()
            fwd = copy(3 + k, slot, slot, (x, y, 1 - c))
            fwd.start()
            passed.append(fwd)
        for k, (cx, cy) in enumerate(chips):
            slot = out_ref.at[2 * cx + cy, other]
            copy(3 + k, slot, slot, (x, y, c)).wait_recv()
        for cp in first + passed:
            cp.wait_send()

    return _pcall(body, name="all_gather_weights", in_specs=[ANY], out_specs=ANY,
                  out_shape=SDS((N_CHIPS, rows, D), BF16),
                  scratch=[pltpu.SemaphoreType.DMA((6,)), pltpu.SemaphoreType.DMA((6,))])(wpk)


HBM = pl.BlockSpec(memory_space=pltpu.HBM)
SEM = pl.BlockSpec(memory_space=pltpu.SEMAPHORE)
DATAFLOW = pltpu.SideEffectType.DATAFLOW_SIDE_EFFECTING


def _in_hbm(a):
    return pltpu.with_memory_space_constraint(a, pltpu.HBM)


def _gather_late_start(wpk, after):
    rows = wpk.shape[0]

    def body(in_ref, land_ref, after_ref, send_sems, recv_sems, in_thru, land_thru, token):
        x, y, c, chips = _place()
        for k, (cx, cy) in enumerate(chips):
            pltpu.make_async_remote_copy(src_ref=in_ref, dst_ref=land_ref.at[2 * x + y], send_sem=send_sems.at[k],
                                         recv_sem=recv_sems.at[k], device_id=(cx, cy, c), device_id_type=MESH).start()
        token[...] = jnp.zeros_like(token)

    return pl.pallas_call(
        body, name="gather_late_start",
        out_shape=(pltpu.SemaphoreType.DMA((3,)), pltpu.SemaphoreType.DMA((3,)), pltpu.HBM(wpk.shape, wpk.dtype),
                   pltpu.HBM((N_CHIPS, rows, D), wpk.dtype), SDS((8, LANES), F32)),
        in_specs=(HBM, HBM, ANY), out_specs=(SEM, SEM, HBM, HBM, pl.BlockSpec(memory_space=pltpu.VMEM)),
        input_output_aliases={0: 2, 1: 3}, compiler_params=pltpu.CompilerParams(has_side_effects=DATAFLOW),
    )(_in_hbm(wpk), _in_hbm(lax.empty((N_CHIPS, rows, D), wpk.dtype)), after)


def _gather_late_wait(send_sems, recv_sems, in_thru, land_thru, after):
    def body(in_ref, land_ref, send_sems, recv_sems, after_ref, after2_ref, in_dead, got_ref):
        x, y, c, chips = _place()
        for k, (cx, cy) in enumerate(chips):
            cp = pltpu.make_async_remote_copy(src_ref=in_ref, dst_ref=land_ref.at[2 * cx + cy], send_sem=send_sems.at[k],
                                              recv_sem=recv_sems.at[k], device_id=(cx, cy, c), device_id_type=MESH)
            cp.wait_send()
            cp.wait_recv()

    return pl.pallas_call(
        body, name="gather_late_wait",
        out_shape=(pltpu.HBM(in_thru.shape, in_thru.dtype), pltpu.HBM(land_thru.shape, land_thru.dtype)),
        in_specs=(HBM, HBM, SEM, SEM, ANY, ANY), out_specs=(HBM, HBM), input_output_aliases={0: 0, 1: 1},
        compiler_params=pltpu.CompilerParams(has_side_effects=DATAFLOW),
    )(in_thru, land_thru, send_sems, recv_sems, *after)[1]


def _rs_sibling(gpk):
    HALF = gpk.shape[1] // 2

    def body(in_ref, out_ref, send_sem, recv_sem):
        x, y, c, _ = _place()
        theirs = pl.ds(pl.multiple_of((1 - c) * HALF, 8), HALF)
        cp = pltpu.make_async_remote_copy(src_ref=in_ref.at[:, theirs], dst_ref=out_ref, send_sem=send_sem, recv_sem=recv_sem,
                                          device_id=(x, y, 1 - c), device_id_type=MESH)
        cp.start()
        cp.wait()

    return _pcall(body, name="rs_sibling", in_specs=[ANY], out_specs=ANY, out_shape=SDS((N_CHIPS, HALF, D), F32),
                  scratch=[pltpu.SemaphoreType.DMA, pltpu.SemaphoreType.DMA])(gpk)


def _rs_add_sibling(cidx, gpk, got):
    HALF = got.shape[1]
    th = HALF // 4
    nh = HALF // th
    assert th % 16 == 0

    def body(c_ref, a_ref, b_ref, o_ref):
        o_ref[...] = (a_ref[...] + b_ref[...]).astype(BF16)

    gs = pltpu.PrefetchScalarGridSpec(
        num_scalar_prefetch=1, grid=(N_CHIPS, nh),
        in_specs=[pl.BlockSpec((1, th, D), lambda j, i, c: (j, c[0] * nh + i, 0)), pl.BlockSpec((1, th, D), lambda j, i, c: (j, i, 0))],
        out_specs=pl.BlockSpec((1, th, D), lambda j, i, c: (j, i, 0)))
    return pl.pallas_call(body, name="rs_add_sibling", grid_spec=gs, out_shape=SDS((N_CHIPS, HALF, D), BF16),
                          compiler_params=pltpu.CompilerParams(dimension_semantics=("parallel", "parallel"),
                                                               vmem_limit_bytes=48 << 20))(cidx, gpk, got)


def _rs_chips_start(part, small, after):
    def body(p_ref, s_ref, land_ref, sland_ref, after_ref, send_sems, recv_sems, p_thru, s_thru, land_thru, sland_thru, token):
        x, y, c, chips = _place()
        for k, (cx, cy) in enumerate(chips):
            pltpu.make_async_remote_copy(src_ref=p_ref.at[2 * cx + cy], dst_ref=land_ref.at[2 * x + y], send_sem=send_sems.at[k],
                                         recv_sem=recv_sems.at[k], device_id=(cx, cy, c), device_id_type=MESH).start()
        peers = [(x, y, 1 - c)] + [(cx, cy, c) for cx, cy in chips] + [(cx, cy, 1 - c) for cx, cy in chips]
        for k, to in enumerate(peers):
            pltpu.make_async_remote_copy(src_ref=s_ref, dst_ref=sland_ref.at[4 * x + 2 * y + c], send_sem=send_sems.at[3 + k],
                                         recv_sem=recv_sems.at[3 + k], device_id=to, device_id_type=MESH).start()
        token[...] = jnp.zeros_like(token)

    return pl.pallas_call(
        body, name="rs_chips_start",
        out_shape=(pltpu.SemaphoreType.DMA((10,)), pltpu.SemaphoreType.DMA((10,)), pltpu.HBM(part.shape, part.dtype),
                   pltpu.HBM(small.shape, small.dtype), pltpu.HBM(part.shape, part.dtype), pltpu.HBM((N_DEV, 8, D), F32),
                   SDS((8, LANES), F32)),
        in_specs=(HBM, HBM, HBM, HBM, ANY), out_specs=(SEM, SEM, HBM, HBM, HBM, HBM, pl.BlockSpec(memory_space=pltpu.VMEM)),
        input_output_aliases={0: 2, 1: 3, 2: 4, 3: 5}, compiler_params=pltpu.CompilerParams(has_side_effects=DATAFLOW),
    )(_in_hbm(part), _in_hbm(small), _in_hbm(lax.empty(part.shape, part.dtype)), _in_hbm(lax.empty((N_DEV, 8, D), F32)), after)


def _rs_chips_wait(send_sems, recv_sems, p_thru, s_thru, land_thru, sland_thru, after):
    def body(p_ref, s_ref, land_ref, sland_ref, send_sems, recv_sems, *after_and_outputs):
        x, y, c, chips = _place()
        for k, (cx, cy) in enumerate(chips):
            cp = pltpu.make_async_remote_copy(src_ref=p_ref.at[0], dst_ref=land_ref.at[2 * cx + cy], send_sem=send_sems.at[k],
                                              recv_sem=recv_sems.at[k], device_id=(cx, cy, c), device_id_type=MESH)
            cp.wait_send()
            cp.wait_recv()
        peers = [(x, y, 1 - c)] + [(cx, cy, c) for cx, cy in chips] + [(cx, cy, 1 - c) for cx, cy in chips]
        for k, (px, py, pc) in enumerate(peers):
            cp = pltpu.make_async_remote_copy(src_ref=s_ref, dst_ref=sland_ref.at[4 * px + 2 * py + pc], send_sem=send_sems.at[3 + k],
                                              recv_sem=recv_sems.at[3 + k], device_id=(px, py, pc), device_id_type=MESH)
            cp.wait_send()
            cp.wait_recv()

    hbm = lambda a: pltpu.HBM(a.shape, a.dtype)
    outs = pl.pallas_call(
        body, name="rs_chips_wait", out_shape=(hbm(p_thru), hbm(s_thru), hbm(land_thru), hbm(sland_thru)),
        in_specs=(HBM, HBM, HBM, HBM, SEM, SEM) + (ANY,) * len(after), out_specs=(HBM, HBM, HBM, HBM),
        input_output_aliases={0: 0, 1: 1, 2: 2, 3: 3}, compiler_params=pltpu.CompilerParams(has_side_effects=DATAFLOW),
    )(p_thru, s_thru, land_thru, sland_thru, send_sems, recv_sems, *after)
    return outs[0], outs[2], outs[3]


def _rs_add_chips(qidx, part, parts):
    HALF = part.shape[1]
    th = HALF // 4
    assert th % 16 == 0

    def body(q_ref, own_ref, p_ref, o_ref):
        for me in range(N_CHIPS):
            @pl.when(q_ref[0] == me)
            def _(me=me):
                t = [(own_ref[0] if j == me else p_ref[j]).astype(F32) for j in range(N_CHIPS)]
                o_ref[...] = ((t[0] + t[1]) + t[2]) + t[3]

    gs = pltpu.PrefetchScalarGridSpec(
        num_scalar_prefetch=1, grid=(HALF // th,),
        in_specs=[pl.BlockSpec((1, th, D), lambda i, q: (q[0], i, 0)), pl.BlockSpec((N_CHIPS, th, D), lambda i, q: (0, i, 0))],
        out_specs=pl.BlockSpec((th, D), lambda i, q: (i, 0)))
    return pl.pallas_call(body, name="rs_add_chips", grid_spec=gs, out_shape=SDS((HALF, D), F32),
                          compiler_params=pltpu.CompilerParams(dimension_semantics=("parallel",),
                                                               vmem_limit_bytes=48 << 20))(qidx, part, parts)


def _rs_join(mine, core, name):
    def body(in_ref, out_ref, send_sem, recv_sem):
        x, y, c, _ = _place()
        cp = pltpu.make_async_remote_copy(src_ref=in_ref, dst_ref=out_ref, send_sem=send_sem, recv_sem=recv_sem,
                                          device_id=(x, y, 1 - c), device_id_type=MESH)
        cp.start()
        cp.wait()

    theirs = _pcall(body, name=name, in_specs=[ANY], out_specs=ANY, out_shape=SDS(mine.shape, F32),
                    scratch=[pltpu.SemaphoreType.DMA, pltpu.SemaphoreType.DMA])(mine)
    return jnp.where(core == 0, jnp.concatenate([mine, theirs]), jnp.concatenate([theirs, mine]))


def _reduce_late_start(gpk, after):
    rows = gpk.shape[1]
    HALF = rows // 2
    assert HALF % 16 == 0

    def body(in_ref, land_ref, after_ref, send_sems, recv_sems, in_thru, land_thru, token):
        x, y, c, chips = _place()
        me = 4 * x + 2 * y + c
        peers = [(x, y, 1 - c)] + [(cx, cy, c) for cx, cy in chips] + [(cx, cy, 1 - c) for cx, cy in chips]
        for k, (px, py, pc) in enumerate(peers):
            src = in_ref.at[2 * px + py, pl.ds(pl.multiple_of(pc * HALF, 16), HALF)]
            pltpu.make_async_remote_copy(src_ref=src, dst_ref=land_ref.at[me], send_sem=send_sems.at[k], recv_sem=recv_sems.at[k],
                                         device_id=(px, py, pc), device_id_type=MESH).start()
        token[...] = jnp.zeros_like(token)

    return pl.pallas_call(
        body, name="reduce_late_start",
        out_shape=(pltpu.SemaphoreType.DMA((7,)), pltpu.SemaphoreType.DMA((7,)), pltpu.HBM(gpk.shape, gpk.dtype),
                   pltpu.HBM((N_DEV, HALF, D), gpk.dtype), SDS((8, LANES), F32)),
        in_specs=(HBM, HBM, ANY), out_specs=(SEM, SEM, HBM, HBM, pl.BlockSpec(memory_space=pltpu.VMEM)),
        input_output_aliases={0: 2, 1: 3}, compiler_params=pltpu.CompilerParams(has_side_effects=DATAFLOW),
    )(_in_hbm(gpk), _in_hbm(lax.empty((N_DEV, HALF, D), gpk.dtype)), after)


def _reduce_late_wait(send_sems, recv_sems, in_thru, land_thru, after):
    def body(in_ref, land_ref, send_sems, recv_sems, after_ref, in_out, got_ref):
        x, y, c, chips = _place()
        peers = [(x, y, 1 - c)] + [(cx, cy, c) for cx, cy in chips] + [(cx, cy, 1 - c) for cx, cy in chips]
        for k, (px, py, pc) in enumerate(peers):
            cp = pltpu.make_async_remote_copy(src_ref=land_ref.at[0], dst_ref=land_ref.at[4 * px + 2 * py + pc],
                                              send_sem=send_sems.at[k], recv_sem=recv_sems.at[k],
                                              device_id=(px, py, pc), device_id_type=MESH)
            cp.wait_send()
            cp.wait_recv()

    return pl.pallas_call(
        body, name="reduce_late_wait",
        out_shape=(pltpu.HBM(in_thru.shape, in_thru.dtype), pltpu.HBM(land_thru.shape, land_thru.dtype)),
        in_specs=(HBM, HBM, SEM, SEM, ANY), out_specs=(HBM, HBM), input_output_aliases={0: 0, 1: 1},
        compiler_params=pltpu.CompilerParams(has_side_effects=DATAFLOW),
    )(in_thru, land_thru, send_sems, recv_sems, after)


def _reduce_late_add(didx, gpk, parts):
    HALF = parts.shape[1]
    th = HALF // 4
    nh = HALF // th
    assert th % 16 == 0

    def body(d_ref, own_ref, p_ref, o_ref):
        for me in range(N_DEV):
            @pl.when(d_ref[0] == me)
            def _(me=me):
                t = [(own_ref[0] if j == me else p_ref[j]).astype(F32) for j in range(N_DEV)]
                o_ref[...] = ((((((t[0] + t[1]) + t[2]) + t[3]) + t[4]) + t[5]) + t[6]) + t[7]

    gs = pltpu.PrefetchScalarGridSpec(
        num_scalar_prefetch=1, grid=(nh,),
        in_specs=[pl.BlockSpec((1, th, D), lambda i, d: (d[1], d[2] * nh + i, 0)), pl.BlockSpec((N_DEV, th, D), lambda i, d: (0, i, 0))],
        out_specs=pl.BlockSpec((th, D), lambda i, d: (i, 0)))
    return pl.pallas_call(body, name="reduce_late_add", grid_spec=gs, out_shape=SDS((HALF, D), F32),
                          compiler_params=pltpu.CompilerParams(dimension_semantics=("parallel",),
                                                               vmem_limit_bytes=48 << 20))(didx, gpk, parts)


def _pack_early(b, dtype):
    lanes = lambda a: jnp.pad(a.astype(dtype), ((0, 0), (0, D - a.shape[1])))
    pair = jnp.concatenate([b["w_q_b"].astype(dtype), b["w_ple"].astype(dtype), jnp.zeros((256, D - 640), dtype)], axis=1)
    return jnp.concatenate([lanes(b["w_in"]), pair, lanes(b["w_kv_b"])], axis=0)


def _pack_late(b, dtype):
    return jnp.concatenate([b[n].astype(dtype) for n in ("w_mla_up", "w_swa_up", "w_out", "w_ple_gate", "w_mlp_up", "w_mlp_down")],
                           axis=0)


def _unpack_shards(pk, which):
    return {n: pk[PACK_AT[n][1]:PACK_AT[n][1] + r, PACK_AT[n][2]:PACK_AT[n][2] + c] for n, r, c in BIG if PACK_AT[n][0] == which}


def _full_weights(gathered, own, chip, which):
    own_b = _unpack_shards(own, which)
    per_chip = [{n: jnp.where(chip == j, own_b[n], blk) for n, blk in _unpack_shards(gathered[j], which).items()}
                for j in range(N_CHIPS)]
    out = {}
    for n in own_b:
        shards = [pc[n] for pc in per_chip]
        if n == "w_in":
            out["w_in_p"] = _w_in_internal(shards)
        else:
            out[n] = jnp.concatenate(shards, axis=1 if n in COL_SHARDED else 0)
    return out


def _split_full_grads(grads, pack, dtype):
    shard = {n: (r, c) for n, r, c in BIG}
    chunks = []
    for j in range(N_CHIPS):
        blocks = {}
        for n, g in grads.items():
            if n == "w_in_p":
                blocks["w_in"] = _w_in_grad_shard(g, j)
                continue
            r, c = shard[n]
            blocks[n] = g[:, j * c:(j + 1) * c] if n in COL_SHARDED else g[j * r:(j + 1) * r]
        chunks.append(pack(blocks, dtype))
    return jnp.stack(chunks)


W_IN_SHARD = 936
W_IN_SEGMENTS = ((0, 256, (3072,)), (256, 384, (3840,)), (384, 416, (4032,)), (416, 1440, (0,)), (1440, 1504, (3328, 3392)),
                 (1504, 1568, (3456, 3520)), (1568, 1632, (3584, 3648)), (1632, 1696, (3712, 3776)), (1696, 3744, (1024,)))


def _w_in_internal(shards):
    def cols(a, b):
        out = []
        for j, s in enumerate(shards):
            lo, hi = max(a, W_IN_SHARD * j), min(b, W_IN_SHARD * (j + 1))
            if lo < hi:
                out.append(s[:, lo - W_IN_SHARD * j:hi - W_IN_SHARD * j])
        return out

    pieces = {}
    for a, b, places in W_IN_SEGMENTS:
        for at in places:
            pieces[at] = cols(a, b)
    zeros = lambda n: [jnp.zeros((D, n), shards[0].dtype)]
    pieces[3968] = zeros(64)
    pieces[4064] = zeros(32)
    return jnp.concatenate([piece for at in sorted(pieces) for piece in pieces[at]], axis=1)


def _w_in_grad_shard(g, j):
    def internal(a, b):
        out = []
        while a < b:
            end = min(b, (a // D + 1) * D)
            out.append(g[a // D][:, a % D:a % D + end - a])
            a = end
        return out

    out = []
    for a, b, places in W_IN_SEGMENTS:
        lo, hi = max(a, W_IN_SHARD * j), min(b, W_IN_SHARD * (j + 1))
        if lo < hi:
            parts = [internal(at + lo - a, at + hi - a) for at in places]
            if len(parts) == 1:
                out += parts[0]
            else:
                assert len(parts[0]) == len(parts[1]) == 1
                out.append(parts[0][0] + parts[1][0])
    return jnp.concatenate(out, axis=1)


def _local_step(x, p, tgt, w, small, late_weights, late_grads_out):
    T = x.shape[0]
    tm = 256
    tw = 512 if T % 512 == 0 else tm
    tb = 256
    w_in_p = w["w_in_p"]
    wqb = jnp.pad(w["w_q_b"].reshape(Q_LORA, MLA_HEADS, 96), ((0, 0), (0, 0), (0, 32))).reshape(Q_LORA, 2048)
    wkv = w["w_kv_b"].reshape(KV_LORA, MLA_HEADS, 128)
    wkn = jnp.pad(wkv[:, :, :64], ((0, 0), (0, 0), (0, 64))).reshape(KV_LORA, 2048)
    wv = wkv[:, :, 64:].reshape(KV_LORA, 1024)
    tab_m = _rope_tables(T, "mla")
    tab_s = _rope_tables(T, "swa")
    g1, gq, gkv, sinks = small["g_mix_pre"], small["g_q_a"], small["g_kv_a"], small["sinks"]
    g2, g3, g4, g5 = small["g_mix_post"], small["g_mlp_pre"], small["g_mlp_post"], small["g_ple"]
    sink_vec = sinks.reshape(SWA_HEADS)

    z, h1 = _fwd_in(x, g1, w_in_p, tw)
    qn, kvn, km, vm, qt, kt, vt, qs, ks, vs = _fwd_qkv(z, gq, gkv, wqb, wkn, wv, tab_m, tab_s, tb)
    om, lse_m = _mla_fwd(qt, km, vt, tb)
    os_, lse_s = _swa_fwd(sink_vec, qs, ks, vs)
    w = {**w, **late_weights((om, os_))}
    y, yo, au, bu, x1 = _fwd_mix(om, os_, z, x, w["w_mla_up"], w["w_swa_up"], w["w_out"], g2, tm)
    h2, u = _fwd_mlp_up(x1, g3, w["w_mlp_up"], tw)
    d, x2 = _fwd_mlp_down(u, w["w_mlp_down"], x1, g4, tw)
    loss, dx2, dgt, de0, dg5 = _ple_fwd_bwd(p, x2, tgt, w["w_ple"], g5, w["w_ple_gate"], tw)

    dd, da, dg4 = _bwd_mlp_down(dx2, d, g4, w["w_mlp_down"], u, tw)
    dx1, dg3 = _bwd_mlp_up(da, w["w_mlp_up"], x1, g3, dx2, tw)
    dyo, dg2, dau, dbu, dga, dgb, dos, delta_m, dom_t = _bwd_mix(dx1, yo, g2, w["w_out"], z, au, bu, w["w_mla_up"],
                                                                w["w_swa_up"], om, tb)
    gpk_late = lax.empty((N_CHIPS, PACK_ROWS["late"], D), BF16)
    for weight, a_, g_ in (("w_mla_up", om, dau), ("w_swa_up", os_, dbu), ("w_out", y, dyo), ("w_ple_gate", x2, dgt),
                           ("w_mlp_up", h2, da), ("w_mlp_down", u, dd)):
        gpk_late = _wgrad(a_, g_, "wgrad_" + weight[2:], into=(gpk_late, weight))
    token = late_grads_out(gpk_late)
    delta_m = delta_m + token[0, 0]
    dqm, dkm, dvm = _mla_bwd(qt, km, kt, vm, dom_t, lse_m, delta_m, tb)
    dqs, dkc, dkp, dvc, dvp, dsink = _swa_bwd(sink_vec, qs, ks, vs, dos, os_, lse_s)
    dqb, dknb, dvb, dsq, drest, dgq, dgkv = _bwd_qkv(dqm, dkm, dvm, dqs, dkc, dkp, dvc, dvp, z, gq, gkv, wqb, wkn, wv,
                                                      tab_m, tab_s)
    gx, dg1 = _bwd_in(dsq, dga, dgb, drest, w_in_p, x, g1, dx1, tw)

    g_in_p = [_wgrad(h1, dsq, "wgrad_in_sq"), _wgrad(h1, dga, "wgrad_in_ga"), _wgrad(h1, dgb, "wgrad_in_gb"),
              _wgrad(h1, drest, "wgrad_in_rest")]
    g_qb_p = _wgrad(qn, dqb, "wgrad_q_b")
    g_kn_p = _wgrad(kvn, dknb, "wgrad_kv_b_nope")
    g_v_p = _wgrad(kvn, dvb, "wgrad_kv_b_v")
    grads = {
        "w_in_p": g_in_p,
        "w_q_b": g_qb_p.reshape(Q_LORA, MLA_HEADS, 128)[:, :, :96].reshape(Q_LORA, 1536),
        "w_kv_b": jnp.concatenate([g_kn_p.reshape(KV_LORA, MLA_HEADS, 128)[:, :, :64], g_v_p.reshape(KV_LORA, MLA_HEADS, 64)],
                                  axis=2).reshape(KV_LORA, 2048),
        "w_ple": _wgrad(p, de0, "wgrad_ple"),
    }
    small_grads = {"g_mix_pre": dg1, "g_q_a": dgq, "g_kv_a": dgkv, "sinks": dsink[0:1, 0:SWA_HEADS], "g_mix_post": dg2,
                   "g_mlp_pre": dg3, "g_mlp_post": dg4, "g_ple": dg5}
    return loss, gx, grads, small_grads


def _pack_small(vals, fill, scalar=None):
    wide = [vals[n] for n, k in SMALL if k == D]
    narrow = [vals[n] for n, k in SMALL if k != D]
    used = sum(k for _, k in SMALL if k != D)
    last = jnp.concatenate(narrow + [jnp.full((1, D - used), fill, F32)], axis=1)
    rest = jnp.full((2, D), fill, F32)
    if scalar is not None:
        rest = jnp.concatenate([jnp.concatenate([scalar, rest[0:1, 1:]], axis=1), rest[1:2]], axis=0)
    return jnp.concatenate(wide + [last, rest], axis=0)


def _unpack_small(pk):
    out, row, off = {}, 0, 0
    for n, k in SMALL:
        if k == D:
            out[n] = pk[row:row + 1]
            row += 1
    for n, k in SMALL:
        if k != D:
            out[n] = pk[5:6, off:off + k]
            off += k
    return out


def kernel(x, p, g_mix_pre, w_in, g_q_a, w_q_b, g_kv_a, w_kv_b, sinks, w_mla_up, w_swa_up, w_out, g_mix_post, g_mlp_pre, w_mlp_up, w_mlp_down, g_mlp_post, w_ple, g_ple, w_ple_gate, loss_target, m_g_mix_pre, m_w_in, m_g_q_a, m_w_q_b, m_g_kv_a, m_w_kv_b, m_sinks, m_w_mla_up, m_w_swa_up, m_w_out, m_g_mix_post, m_g_mlp_pre, m_w_mlp_up, m_w_mlp_down, m_g_mlp_post, m_w_ple, m_g_ple, m_w_ple_gate, v_g_mix_pre, v_w_in, v_g_q_a, v_w_q_b, v_g_kv_a, v_w_kv_b, v_sinks, v_w_mla_up, v_w_swa_up, v_w_out, v_g_mix_post, v_g_mlp_pre, v_w_mlp_up, v_w_mlp_down, v_g_mlp_post, v_w_ple, v_g_ple, v_w_ple_gate):
    given = dict(locals())
    big_w = {n: given[n][0] for n, _, _ in BIG}
    small_w = {n: given[n] for n, _ in SMALL}
    small_m = {n: given["m_" + n] for n, _ in SMALL}
    small_v = {n: given["v_" + n] for n, _ in SMALL}

    core = lax.axis_index("c")
    chip = 2 * lax.axis_index("x") + lax.axis_index("y")
    core_i = core.astype(jnp.int32).reshape(1)
    chip_i = chip.astype(jnp.int32).reshape(1)
    dev_i = jnp.stack([2 * chip + core, chip, core]).astype(jnp.int32)

    own_early = _pack_early(big_w, BF16)
    own_late = _pack_late(big_w, BF16)
    got_early = _all_gather(own_early)
    late_flight = _gather_late_start(own_late, got_early)
    weights = _full_weights(got_early, own_early, chip, "early")
    step_small = {**small_w, "g_mix_pre": small_w["g_mix_pre"] + late_flight[4][0, 0]}

    def late_weights(after):
        return _full_weights(_gather_late_wait(*late_flight[:4], after), own_late, chip, "late")

    flight = {}

    def late_grads_out(gpk_late):
        flight["late"] = _reduce_late_start(gpk_late, dev_i)
        return flight["late"][4]

    loss_blk, gx, grads, small_grads = _local_step(x[0], p[0, 0], loss_target[0], weights, step_small, late_weights,
                                                   late_grads_out)

    gpk = _split_full_grads(grads, _pack_early, F32)
    got = _rs_sibling(gpk)
    part = _rs_add_sibling(core_i, gpk, got)
    small_own = _pack_small(small_grads, 0.0, loss_blk[0:1, 0:1])
    early_flight = _rs_chips_start(part, small_own, dev_i)

    out_g, out_d, out_m, out_v = {}, {}, {}, {}
    gpk_late, parts_late = _reduce_late_wait(*flight["late"][:4], early_flight[6])
    joined_late = _rs_join(_reduce_late_add(dev_i, gpk_late, parts_late), core, "rs_join_late")
    for n, _, _ in BIG:
        if PACK_AT[n][0] == "late":
            out_g[n], out_d[n], out_m[n], out_v[n] = _adamw(given[n], joined_late, given["m_" + n], given["v_" + n], n)

    part, parts, small_parts = _rs_chips_wait(*early_flight[:6], [out_d[n] for n in out_d])
    joined_early = _rs_join(_rs_add_chips(chip_i, part, parts), core, "rs_join_early")
    for n, _, _ in BIG:
        if PACK_AT[n][0] == "early":
            out_g[n], out_d[n], out_m[n], out_v[n] = _adamw(given[n], joined_early, given["m_" + n], given["v_" + n], n)

    mine = (lax.broadcasted_iota(jnp.int32, (N_DEV, 1, 1), 0) == dev_i[0])
    g_small_pk, d_small_pk, m_small_pk, v_small_pk = _adamw_small(
        _pack_small(small_w, 0.0), jnp.where(mine, small_own[None], small_parts), _pack_small(small_m, 0.0),
        _pack_small(small_v, 1.0))
    loss = g_small_pk[6, 0]
    for out, pk in ((out_g, g_small_pk), (out_d, d_small_pk), (out_m, m_small_pk), (out_v, v_small_pk)):
        out.update(_unpack_small(pk))
    order = ["g_mix_pre", "w_in", "g_q_a", "w_q_b", "g_kv_a", "w_kv_b", "sinks", "w_mla_up", "w_swa_up", "w_out", "g_mix_post",
             "g_mlp_pre", "w_mlp_up", "w_mlp_down", "g_mlp_post", "w_ple", "g_ple", "w_ple_gate"]
    return (loss, gx[None], *[out_g[n] for n in order], *[out_d[n] for n in order], *[out_m[n] for n in order],
            *[out_v[n] for n in order])
```

```python
import math

import jax
import jax.numpy as jnp
from jax import lax
from jax.experimental import pallas as pl
from jax.experimental.pallas import tpu as pltpu

F32 = jnp.float32
BF16 = jnp.bfloat16
SDS = jax.ShapeDtypeStruct

D = 1024
D_FF = 4096
PLE = 256
Q_LORA = 256
KV_LORA = 128
MLA_HEADS = 16
MLA_NOPE = 64
MLA_ROPE = 32
SWA_HEADS = 16
SWA_HD = 64
WINDOW = 128
ROPE_THETA = 10000.0
EPS = 1e-6
NEG = -1e30
NZ = 4096
MLA_SCALE = (MLA_NOPE + MLA_ROPE) ** -0.5
LOG2_E = math.log2(math.e)
MLA_LOG2_SCALE = MLA_SCALE * LOG2_E
SWA_SCALE = SWA_HD ** -0.5

ADAM_LR = 0.001
ADAM_B1 = 0.9
ADAM_B2 = 0.999
ADAM_EPS = 1e-08
ADAM_WD = 0.01
ADAM_STEP = 10

LANES = 128
ATT_COLS = 128
N_CHIPS = 4
N_DEV = 8
MESH = pl.DeviceIdType.MESH

NT = (((1,), (1,)), ((), ()))
TN = (((0,), (0,)), ((), ()))

BIG = (("w_in", 1024, 936), ("w_q_b", 256, 384), ("w_kv_b", 128, 512), ("w_mla_up", 256, 1024),
       ("w_swa_up", 256, 1024), ("w_out", 256, 1024), ("w_mlp_up", 1024, 1024), ("w_mlp_down", 1024, 1024),
       ("w_ple", 256, 256), ("w_ple_gate", 256, 1024))
COL_SHARDED = ("w_in", "w_q_b", "w_kv_b", "w_mlp_up", "w_ple")
PACK_AT = {"w_in": ("early", 0, 0), "w_q_b": ("early", 1024, 0), "w_ple": ("early", 1024, 384), "w_kv_b": ("early", 1280, 0),
           "w_mla_up": ("late", 0, 0), "w_swa_up": ("late", 256, 0), "w_out": ("late", 512, 0), "w_ple_gate": ("late", 768, 0),
           "w_mlp_up": ("late", 1024, 0), "w_mlp_down": ("late", 2048, 0)}
PACK_ROWS = {"early": 1408, "late": 3072}
SMALL = (("g_mix_pre", 1024), ("g_q_a", 256), ("g_kv_a", 128), ("sinks", 16), ("g_mix_post", 1024),
         ("g_mlp_pre", 1024), ("g_mlp_post", 1024), ("g_ple", 1024))


def _dot(a, b):
    return jnp.dot(a, b, preferred_element_type=F32)


def _dot_nt(a, b):
    return lax.dot_general(a, b, NT, preferred_element_type=F32)


def _dot_tn(a, b):
    return lax.dot_general(a, b, TN, preferred_element_type=F32)


def _pcall(body, *, name, out_shape, grid=(), in_specs=None, out_specs=None, scratch=(), sem=None, vmem_mb=32, aliases=None):
    params = dict(vmem_limit_bytes=vmem_mb << 20)
    if sem is not None:
        params["dimension_semantics"] = sem
    return pl.pallas_call(body, name=name, grid=grid, in_specs=in_specs, out_specs=out_specs, out_shape=out_shape,
                          scratch_shapes=list(scratch), input_output_aliases=aliases or {},
                          compiler_params=pltpu.CompilerParams(**params))


def _rows(tm, n, col=0):
    return pl.BlockSpec((tm, n), lambda i: (i, col))


def _full(shape):
    return pl.BlockSpec(shape, lambda i: (0,) * len(shape))


def _rms(x, g):
    r = lax.rsqrt(jnp.mean(x * x, axis=-1, keepdims=True) + EPS)
    return x * r * g


def _rms_bwd(dy, x, g):
    r = lax.rsqrt(jnp.mean(x * x, axis=-1, keepdims=True) + EPS)
    xn = x * r
    dn = dy * g
    dx = r * (dn - xn * jnp.mean(dn * xn, axis=-1, keepdims=True))
    return dx, jnp.sum(dy * xn, axis=0, keepdims=True)


def _sigmoid(x):
    return 1.0 / (1.0 + jnp.exp(-x))


def _rope(x, c, a, b, half):
    return x * c + pltpu.roll(x, LANES - half, 1) * a + pltpu.roll(x, half, 1) * b


def _rope_tables(T, kind):
    lane = jnp.arange(LANES)
    if kind == "mla":
        half = MLA_ROPE // 2
        rel = lane - MLA_NOPE
        on = (rel >= 0) & (rel < MLA_ROPE)
        d = MLA_ROPE
    else:
        half = SWA_HD // 2
        rel = lane % SWA_HD
        on = jnp.ones((LANES,), bool)
        d = SWA_HD
    first = on & (rel < half)
    second = on & (rel >= half)
    f = jnp.where(first, rel, rel - half).astype(F32)
    inv = jnp.exp(-math.log(ROPE_THETA) * f * (2.0 / d))
    ang = jnp.arange(T, dtype=F32)[:, None] * inv[None, :]
    cos, sin = jnp.cos(ang), jnp.sin(ang)
    c = jnp.where(on[None], cos, 1.0)
    a = jnp.where(first[None], -sin, 0.0)
    b = jnp.where(second[None], sin, 0.0)
    return c, a, b


def _fwd_in(x, g1, w_in_p, tm):
    T = x.shape[0]

    def body(x_ref, g_ref, w_ref, z_ref, h_ref):
        h = _rms(x_ref[...], g_ref[...]).astype(BF16)
        h_ref[...] = h
        z_ref[...] = _dot(h, w_ref[...])

    return _pcall(body, name="fwd_in", grid=(T // tm,),
                  in_specs=[_rows(tm, D), _full((1, D)), _full((D, NZ))],
                  out_specs=[_rows(tm, NZ), _rows(tm, D)],
                  out_shape=[SDS((T, NZ), F32), SDS((T, D), BF16)], sem=("parallel",))(x, g1, w_in_p)


def _fwd_qkv(z, gq, gkv, wqb, wkn, wv, tab_m, tab_s, tm):
    T = z.shape[0]

    def body(qa_ref, sq_ref, skd_ref, svd_ref, kva_ref, kr_ref, gq_ref, gkv_ref, wqb_ref, wkn_ref, wv_ref,
             cm_ref, am_ref, bm_ref, cs_ref, as_ref, bs_ref,
             qn_ref, kvn_ref, km_ref, vm_ref, qt_ref, kt_ref, vt_ref, qs_ref, ks_ref, vs_ref):
        qn = _rms(qa_ref[...], gq_ref[...]).astype(BF16)
        qn_ref[...] = qn
        kvn = _rms(kva_ref[...], gkv_ref[...]).astype(BF16)
        kvn_ref[...] = kvn
        cm, am, bm = cm_ref[...], am_ref[...], bm_ref[...]
        cs, as_, bs = cs_ref[...], as_ref[...], bs_ref[...]
        k_rope = _rope(kr_ref[...], cm, am, bm, MLA_ROPE // 2)
        vt_row = lax.broadcasted_iota(jnp.int32, (LANES, tm), 0)
        v_all = _dot(kvn, wv_ref[...])
        q_all = _dot(qn, wqb_ref[...])
        k_all = _dot(kvn, wkn_ref[...])
        for j in range(D // LANES):
            sl = slice(LANES * j, LANES * (j + 1))
            v = v_all[:, sl]
            vm_ref[:, sl] = v.astype(BF16)
            v_t = v.T
            for hh, rows64 in enumerate((v_t, pltpu.roll(v_t, 64, 0))):
                blk = jnp.where(vt_row < 64, rows64, jnp.where(vt_row == 64, 1.0, 0.0))
                vt_ref[0, LANES * (2 * j + hh):LANES * (2 * j + hh + 1), :] = blk.astype(BF16)
        for h in range(MLA_HEADS):
            sl = slice(LANES * h, LANES * (h + 1))
            qh = _rope(q_all[:, sl], cm, am, bm, MLA_ROPE // 2)
            qt_ref[0, sl, :] = qh.T.astype(BF16)
            k = k_all[:, sl] + k_rope
            km_ref[:, sl] = k.astype(BF16)
            kt_ref[0, sl, :] = k.T.astype(BF16)
        for j in range(D // LANES):
            sl = slice(LANES * j, LANES * (j + 1))
            qs_ref[:, sl] = _rope(sq_ref[:, sl], cs, as_, bs, SWA_HD // 2).astype(BF16)
        for j in range(2):
            sl = slice(LANES * j, LANES * (j + 1))
            ks_ref[:, sl] = _rope(skd_ref[:, sl], cs, as_, bs, SWA_HD // 2).astype(BF16)
        vs_ref[...] = svd_ref[...].astype(BF16)

    tab = [_rows(tm, LANES)] * 6
    return _pcall(body, name="fwd_qkv", grid=(T // tm,),
                  in_specs=[_rows(tm, 256, 12), _rows(tm, 1024, 0), _rows(tm, 256, 13), _rows(tm, 256, 14),
                            _rows(tm, 128, 30), _rows(tm, 128, 31), _full((1, Q_LORA)), _full((1, KV_LORA)),
                            _full((Q_LORA, 2048)), _full((KV_LORA, 2048)), _full((KV_LORA, 1024))] + tab,
                  out_specs=[_rows(tm, Q_LORA), _rows(tm, KV_LORA), _rows(tm, 2048), _rows(tm, 1024),
                             pl.BlockSpec((1, 2048, tm), lambda i: (i, 0, 0)), pl.BlockSpec((1, 2048, tm), lambda i: (i, 0, 0)),
                             pl.BlockSpec((1, 2048, tm), lambda i: (i, 0, 0)),
                             _rows(tm, 1024), _rows(tm, 256), _rows(tm, 256)],
                  out_shape=[SDS((T, Q_LORA), BF16), SDS((T, KV_LORA), BF16), SDS((T, 2048), BF16),
                             SDS((T, 1024), BF16), SDS((T // tm, 2048, tm), BF16), SDS((T // tm, 2048, tm), BF16),
                             SDS((T // tm, 2048, tm), BF16),
                             SDS((T, 1024), BF16), SDS((T, 256), BF16), SDS((T, 256), BF16)],
                  sem=("parallel",))(z, z, z, z, z, z, gq, gkv, wqb, wkn, wv, *tab_m, *tab_s)


def _mla_fwd(qt, km, vt, tb):
    T = km.shape[0]
    nb = T // tb
    cc = ATT_COLS

    def body(q_ref, k_ref, vt_ref, o_ref, l_ref, s_ref, p_ref, al_ref, m_ref, acc_ref):
        i = pl.program_id(1)
        m_ref[...] = jnp.full(m_ref.shape, NEG, F32)
        acc_ref[...] = jnp.zeros_like(acc_ref)
        p_ref[1] = jnp.zeros(p_ref.shape[1:], BF16)
        al_ref[1] = jnp.ones(al_ref.shape[1:], F32)
        key = lax.broadcasted_iota(jnp.int32, (tb, cc), 0)
        qry = lax.broadcasted_iota(jnp.int32, (tb, cc), 1)

        def scores(j, slot):
            off = pl.multiple_of(j * tb, tb)
            for hh in range(2):
                sl = slice(LANES * hh, LANES * (hh + 1))
                s_ref[slot, hh] = _dot(k_ref[pl.ds(off, tb), sl], q_ref[0, sl, :])

        def softmax(slot, diagonal):
            chains = [(hh, slice(cc * c, cc * (c + 1)), c) for hh in range(2) for c in range(tb // cc)]

            def scaled(hh, cols, c):
                t = s_ref[slot, hh, :, cols] * MLA_LOG2_SCALE
                return jnp.where(key <= qry + cc * c, t, NEG) if diagonal else t

            tops = []
            for hh, cols, c in chains:
                if diagonal:
                    top = jnp.max(scaled(hh, cols, c), axis=0, keepdims=True)
                else:
                    top = jnp.max(s_ref[slot, hh, :, cols], axis=0, keepdims=True) * MLA_LOG2_SCALE
                m_old = m_ref[hh, :, cols]
                mn = jnp.maximum(m_old, top)
                m_ref[hh, :, cols] = mn
                al_ref[slot, hh, :, cols] = jnp.exp2(m_old - mn)
                tops.append(mn)
            for (hh, cols, c), mn in zip(chains, tops):
                p_ref[slot, hh, :, cols] = jnp.exp2(scaled(hh, cols, c) - mn).astype(BF16)

        def accumulate(j, slot):
            for hh in range(2):
                acc_ref[hh] = al_ref[slot, hh] * acc_ref[hh] + _dot(vt_ref[j, LANES * hh:LANES * (hh + 1), :], p_ref[slot, hh])

        def step(t, carry):
            scores(2 * t + 1, 1)
            accumulate(jnp.maximum(2 * t - 1, 0), 1)
            softmax(0, False)
            scores(2 * t + 2, 0)
            accumulate(2 * t, 0)
            softmax(1, False)
            return carry

        scores(0, 0)
        lax.fori_loop(0, i // 2, step, 0)

        @pl.when(i % 2 == 1)
        def _():
            scores(i, 1)
            accumulate(jnp.maximum(i - 2, 0), 1)
            softmax(0, False)
            accumulate(i - 1, 0)
            softmax(1, True)
            accumulate(i, 1)

        @pl.when(i % 2 == 0)
        def _():
            accumulate(jnp.maximum(i - 1, 0), 1)
            softmax(0, True)
            accumulate(i, 0)
        den = [acc_ref[hh, 64:65, :] for hh in range(2)]
        o_ref[...] = jnp.concatenate([acc_ref[hh, 0:64, :] / den[hh] for hh in range(2)], axis=0).T
        sub = lax.broadcasted_iota(jnp.int32, (8, tb), 0)
        lse = [m_ref[hh] + jnp.log(den[hh]) * LOG2_E for hh in range(2)]
        l_ref[0, 0] = jnp.where(sub == 0, lse[0], jnp.where(sub == 1, lse[1], 0.0))

    return _pcall(body, name="mla_fwd", grid=(MLA_HEADS // 2, nb),
                  in_specs=[pl.BlockSpec((1, 256, tb), lambda p, i: (i, p, 0)), pl.BlockSpec((T, 256), lambda p, i: (0, p)),
                            pl.BlockSpec((nb, 2 * LANES, tb), lambda p, i: (0, p, 0))],
                  out_specs=[pl.BlockSpec((tb, LANES), lambda p, i: (i, p)),
                             pl.BlockSpec((1, 1, 8, tb), lambda p, i: (p, i, 0, 0))],
                  out_shape=[SDS((T, D), F32), SDS((MLA_HEADS // 2, nb, 8, tb), F32)],
                  scratch=[pltpu.VMEM((2, 2, tb, tb), F32), pltpu.VMEM((2, 2, tb, tb), BF16), pltpu.VMEM((2, 2, 1, tb), F32),
                           pltpu.VMEM((2, 1, tb), F32), pltpu.VMEM((2, LANES, tb), F32)],
                  sem=("parallel", "arbitrary"))(qt, km, vt)


def _swa_mask(n):
    row = lax.broadcasted_iota(jnp.int32, (WINDOW, 2 * WINDOW), 0)
    col = lax.broadcasted_iota(jnp.int32, (WINDOW, 2 * WINDOW), 1)
    rel = row - col + WINDOW
    return (rel >= 0) & (rel < WINDOW) & ((col >= WINDOW) | (n > 0))


def _swa_specs(T):
    nb = T // WINDOW
    cur = lambda w: pl.BlockSpec((WINDOW, w), lambda n: (n, 0))
    prev = lambda w: pl.BlockSpec((WINDOW, w), lambda n: (jnp.maximum(n - 1, 0), 0))
    return nb, cur, prev


def _swa_fwd(sinks, qs, ks, vs):
    T = qs.shape[0]
    nb, cur, prev = _swa_specs(T)

    def body(sink_ref, q_ref, kc_ref, kp_ref, vc_ref, vp_ref, o_ref, l_ref, kb_ref, vb_ref, s_ref, p_ref):
        n = pl.program_id(0)
        mask = _swa_mask(n)
        lo = lax.broadcasted_iota(jnp.int32, (WINDOW, LANES), 1) < 64
        hi = jnp.logical_not(lo)
        for g in range(2):
            gs = slice(LANES * g, LANES * (g + 1))
            kb_ref[g] = jnp.concatenate([kp_ref[:, gs], kc_ref[:, gs]], axis=0)
            vb_ref[g] = jnp.concatenate([vp_ref[:, gs], vc_ref[:, gs]], axis=0)
        for h in range(SWA_HEADS):
            qp = q_ref[:, LANES * (h // 2):LANES * (h // 2 + 1)]
            qh = jnp.where(lo if h % 2 == 0 else hi, qp, jnp.zeros_like(qp))
            s_ref[h] = _dot_nt(qh, kb_ref[h // 8])
        for j in range(SWA_HEADS // 2):
            sl = slice(LANES * j, LANES * (j + 1))
            lses = []
            for h in (2 * j, 2 * j + 1):
                s = jnp.where(mask, s_ref[h] * SWA_SCALE, NEG)
                sk = sink_ref[h]
                m = jnp.maximum(jnp.max(s, axis=1, keepdims=True), sk)
                e = jnp.exp(s - m)
                den = jnp.sum(e, axis=1, keepdims=True) + jnp.exp(sk - m)
                p_ref[h] = (e / den).astype(BF16)
                lses.append(jnp.broadcast_to(m + jnp.log(den), (WINDOW, LANES)))
            l_ref[:, sl] = jnp.where(lo, lses[0], lses[1])
        for j in range(SWA_HEADS // 2):
            vb = vb_ref[j // 4]
            o_ref[:, LANES * j:LANES * (j + 1)] = jnp.where(lo, _dot(p_ref[2 * j], vb), _dot(p_ref[2 * j + 1], vb))

    return _pcall(body, name="swa_fwd", grid=(nb,),
                  in_specs=[pl.BlockSpec(memory_space=pltpu.SMEM), cur(D), cur(256), prev(256), cur(256), prev(256)],
                  out_specs=[cur(D), cur(D)], out_shape=[SDS((T, D), F32)] * 2,
                  scratch=[pltpu.VMEM((2, 2 * WINDOW, LANES), BF16), pltpu.VMEM((2, 2 * WINDOW, LANES), BF16),
                           pltpu.VMEM((SWA_HEADS, WINDOW, 2 * WINDOW), F32), pltpu.VMEM((SWA_HEADS, WINDOW, 2 * WINDOW), BF16)],
                  sem=("parallel",))(sinks, qs, ks, ks, vs, vs)


def _fwd_mix(om, os_, z, x, wmu, wsu, wo, g2, tm):
    T = x.shape[0]

    def body(om_ref, os_ref, ga_ref, gb_ref, x_ref, wmu_ref, wsu_ref, wo_ref, g2_ref,
             y_ref, yo_ref, au_ref, bu_ref, x1_ref):
        au = _dot(om_ref[...].astype(BF16), wmu_ref[...])
        bu = _dot(os_ref[...].astype(BF16), wsu_ref[...])
        au_ref[...] = au
        bu_ref[...] = bu
        y = (_sigmoid(ga_ref[...]) * au + _sigmoid(gb_ref[...]) * bu).astype(BF16)
        y_ref[...] = y
        yo = _dot(y, wo_ref[...])
        yo_ref[...] = yo
        x1_ref[...] = x_ref[...] + _rms(yo, g2_ref[...])

    r = _rows(tm, D)
    w = _full((D, D))
    return _pcall(body, name="fwd_mix", grid=(T // tm,),
                  in_specs=[r, r, _rows(tm, D, 1), _rows(tm, D, 2), r, w, w, w, _full((1, D))],
                  out_specs=[r] * 5,
                  out_shape=[SDS((T, D), BF16), SDS((T, D), F32), SDS((T, D), F32), SDS((T, D), F32), SDS((T, D), F32)],
                  sem=("parallel",))(om, os_, z, z, x, wmu, wsu, wo, g2)


def _fwd_mlp_up(x1, g3, w1, tm):
    T = x1.shape[0]

    def body(x_ref, g_ref, w_ref, h_ref, u_ref):
        h = _rms(x_ref[...], g_ref[...]).astype(BF16)
        h_ref[...] = h
        u_ref[...] = jnp.square(jnp.maximum(_dot(h, w_ref[...]), 0.0)).astype(BF16)

    return _pcall(body, name="fwd_mlp_up", grid=(T // tm,),
                  in_specs=[_rows(tm, D), _full((1, D)), _full((D, D_FF))],
                  out_specs=[_rows(tm, D), _rows(tm, D_FF)],
                  out_shape=[SDS((T, D), BF16), SDS((T, D_FF), BF16)],
                  sem=("parallel",))(x1, g3, w1)


def _fwd_mlp_down(u, w2, x1, g4, tm):
    T = x1.shape[0]

    def body(u_ref, w_ref, x_ref, g_ref, d_ref, x2_ref):
        d = _dot(u_ref[...], w_ref[...])
        d_ref[...] = d
        x2_ref[...] = x_ref[...] + _rms(d, g_ref[...])

    return _pcall(body, name="fwd_mlp_down", grid=(T // tm,),
                  in_specs=[_rows(tm, D_FF), _full((D_FF, D)), _rows(tm, D), _full((1, D))],
                  out_specs=[_rows(tm, D), _rows(tm, D)], out_shape=[SDS((T, D), F32)] * 2,
                  sem=("parallel",))(u, w2, x1, g4)


def _ple_fwd_bwd(p, x2, tgt, wple, g5, wpg, tm):
    T = x2.shape[0]

    def body(p_ref, x2_ref, t_ref, wple_ref, g5_ref, wpg_ref, loss_ref, dx2_ref, dgt_ref, de0_ref, dg5_ref):
        @pl.when(pl.program_id(0) == 0)
        def _():
            loss_ref[...] = jnp.zeros_like(loss_ref)
            dg5_ref[...] = jnp.zeros_like(dg5_ref)

        e0 = _dot(p_ref[...].astype(BF16), wple_ref[...])
        g5 = g5_ref[...]
        r = lax.rsqrt(jnp.mean(e0 * e0, axis=-1, keepdims=True) + EPS)
        en = e0 * r
        e = en * g5
        x2 = x2_ref[...]
        s = _sigmoid(_dot(x2.astype(BF16), wpg_ref[...]))
        diff = x2 + s * e - t_ref[...]
        sq = jnp.sum(jnp.sum(diff * diff, axis=1, keepdims=True), axis=0, keepdims=True)
        loss_ref[...] += jnp.broadcast_to(sq * (0.5 / D), loss_ref.shape)
        dx3 = diff * (1.0 / D)
        de = dx3 * s
        dgt = (dx3 * e * s * (1.0 - s)).astype(BF16)
        dgt_ref[...] = dgt
        dn = de * g5
        de0_ref[...] = (r * (dn - en * jnp.mean(dn * en, axis=-1, keepdims=True))).astype(BF16)
        dg5_ref[...] += jnp.sum(de * en, axis=0, keepdims=True)
        dx2_ref[...] = dx3 + _dot_nt(dgt, wpg_ref[...])

    r = _rows(tm, D)
    return _pcall(body, name="ple_fwd_bwd", grid=(T // tm,),
                  in_specs=[_rows(tm, PLE), r, r, _full((PLE, D)), _full((1, D)), _full((D, D))],
                  out_specs=[_full((8, LANES)), r, r, r, _full((1, D))],
                  out_shape=[SDS((8, LANES), F32), SDS((T, D), F32), SDS((T, D), BF16), SDS((T, D), BF16), SDS((1, D), F32)],
                  sem=("arbitrary",))(p, x2, tgt, wple, g5, wpg)


def _bwd_mlp_down(dx2, d, g4, w2, u, tm):
    T = dx2.shape[0]

    def body(dx_ref, d_ref, g_ref, w_ref, u_ref, dd_ref, da_ref, dg_ref):
        @pl.when(pl.program_id(0) == 0)
        def _():
            dg_ref[...] = jnp.zeros_like(dg_ref)

        dd, dg = _rms_bwd(dx_ref[...], d_ref[...], g_ref[...])
        dg_ref[...] += dg
        ddb = dd.astype(BF16)
        dd_ref[...] = ddb
        du = _dot_nt(ddb, w_ref[...])
        da_ref[...] = (du * (2.0 * jnp.sqrt(u_ref[...].astype(F32)))).astype(BF16)

    return _pcall(body, name="bwd_mlp_down", grid=(T // tm,),
                  in_specs=[_rows(tm, D), _rows(tm, D), _full((1, D)), _full((D_FF, D)), _rows(tm, D_FF)],
                  out_specs=[_rows(tm, D), _rows(tm, D_FF), _full((1, D))],
                  out_shape=[SDS((T, D), BF16), SDS((T, D_FF), BF16), SDS((1, D), F32)],
                  sem=("arbitrary",))(dx2, d, g4, w2, u)


def _bwd_mlp_up(da, w1, x1, g3, dx2, tm):
    T = dx2.shape[0]

    def body(da_ref, w_ref, x_ref, g_ref, dx2_ref, dx1_ref, dg_ref):
        @pl.when(pl.program_id(0) == 0)
        def _():
            dg_ref[...] = jnp.zeros_like(dg_ref)

        dh = _dot_nt(da_ref[...], w_ref[...])
        dx, dg = _rms_bwd(dh, x_ref[...], g_ref[...])
        dg_ref[...] += dg
        dx1_ref[...] = dx2_ref[...] + dx

    return _pcall(body, name="bwd_mlp_up", grid=(T // tm,),
                  in_specs=[_rows(tm, D_FF), _full((D, D_FF)), _rows(tm, D), _full((1, D)), _rows(tm, D)],
                  out_specs=[_rows(tm, D), _full((1, D))],
                  out_shape=[SDS((T, D), F32), SDS((1, D), F32)], sem=("arbitrary",))(da, w1, x1, g3, dx2)


def _bwd_mix(dx1, yo, g2, wo, z, au, bu, wmu, wsu, om, tm):
    T = dx1.shape[0]

    def body(dx_ref, yo_ref, g_ref, wo_ref, ga_ref, gb_ref, au_ref, bu_ref, wmu_ref, wsu_ref, om_ref,
             dyo_ref, dg_ref, dau_ref, dbu_ref, dga_ref, dgb_ref, dos_ref, dl_ref, dot_ref):
        @pl.when(pl.program_id(0) == 0)
        def _():
            dg_ref[...] = jnp.zeros_like(dg_ref)

        dyo, dg = _rms_bwd(dx_ref[...], yo_ref[...], g_ref[...])
        dg_ref[...] += dg
        dyob = dyo.astype(BF16)
        dyo_ref[...] = dyob
        dy = _dot_nt(dyob, wo_ref[...])
        sa = _sigmoid(ga_ref[...])
        sb = _sigmoid(gb_ref[...])
        dau = (dy * sa).astype(BF16)
        dbu = (dy * sb).astype(BF16)
        dau_ref[...] = dau
        dbu_ref[...] = dbu
        dga_ref[...] = (dy * au_ref[...] * sa * (1.0 - sa)).astype(BF16)
        dgb_ref[...] = (dy * bu_ref[...] * sb * (1.0 - sb)).astype(BF16)
        dom = _dot_nt(dau, wmu_ref[...])
        dos_ref[...] = _dot_nt(dbu, wsu_ref[...])
        prod = dom * om_ref[...]
        sub = lax.broadcasted_iota(jnp.int32, (8, tm), 0)
        for pr in range(MLA_HEADS // 2):
            sl = slice(LANES * pr, LANES * (pr + 1))
            pt = prod[:, sl].T
            d0 = jnp.sum(pt[0:64], axis=0, keepdims=True)
            d1 = jnp.sum(pt[64:128], axis=0, keepdims=True)
            dl_ref[pr, 0] = jnp.where(sub == 0, d0, jnp.where(sub == 1, d1, 0.0))
            dot_ref[0, sl, :] = dom[:, sl].T.astype(BF16)

    r = _rows(tm, D)
    w = _full((D, D))
    return _pcall(body, name="bwd_mix", grid=(T // tm,),
                  in_specs=[r, r, _full((1, D)), w, _rows(tm, D, 1), _rows(tm, D, 2), r, r, w, w, r],
                  out_specs=[r, _full((1, D)), r, r, r, r, r, pl.BlockSpec((MLA_HEADS // 2, 1, 8, tm), lambda i: (0, i, 0, 0)),
                             pl.BlockSpec((1, D, tm), lambda i: (i, 0, 0))],
                  out_shape=[SDS((T, D), BF16), SDS((1, D), F32), SDS((T, D), BF16), SDS((T, D), BF16), SDS((T, D), BF16),
                             SDS((T, D), BF16), SDS((T, D), F32), SDS((MLA_HEADS // 2, T // tm, 8, tm), F32),
                             SDS((T // tm, D, tm), BF16)],
                  sem=("arbitrary",))(dx1, yo, g2, wo, z, z, au, bu, wmu, wsu, om)


def _mla_bwd(qt, km, kt, vm, dot, lse, delta, tb):
    T = km.shape[0]
    nb = T // tb
    cc = ATT_COLS

    def body(qt_ref, k_ref, kt_ref, v_ref, dot_ref, l_ref, dl_ref, dqt_ref, dkt_ref, dvt_ref,
             s_ref, dp_ref, p_ref, ds_ref, vh_ref):
        j = pl.program_id(1)

        @pl.when(j == 0)
        def _():
            dqt_ref[...] = jnp.zeros_like(dqt_ref)

        dkt_ref[...] = jnp.zeros_like(dkt_ref)
        dvt_ref[...] = jnp.zeros_like(dvt_ref)
        lo = lax.broadcasted_iota(jnp.int32, (tb, LANES), 1) < 64
        key = lax.broadcasted_iota(jnp.int32, (tb, cc), 0)
        qry = lax.broadcasted_iota(jnp.int32, (tb, cc), 1)
        v = v_ref[...]
        vh_ref[0] = jnp.where(lo, v, jnp.zeros_like(v))
        vh_ref[1] = jnp.where(lo, jnp.zeros_like(v), v)

        def scores(i, slot):
            for hh in range(2):
                sl = slice(LANES * hh, LANES * (hh + 1))
                s_ref[slot, hh] = _dot(k_ref[:, sl], qt_ref[i, sl, :])
                dp_ref[slot, hh] = _dot(vh_ref[hh], dot_ref[i])

        def grads(i, slot, diagonal):
            lse_i = l_ref[0, i]
            delta_i = dl_ref[0, i]
            for hh in range(2):
                for c in range(tb // cc):
                    cols = slice(cc * c, cc * (c + 1))
                    p = jnp.exp2(s_ref[slot, hh, :, cols] * MLA_LOG2_SCALE - lse_i[hh:hh + 1, cols])
                    if diagonal:
                        p = jnp.where(key <= qry + cc * c, p, 0.0)
                    p_ref[hh, :, cols] = p.astype(BF16)
                    ds_ref[hh, :, cols] = (p * (dp_ref[slot, hh, :, cols] - delta_i[hh:hh + 1, cols]) * MLA_SCALE).astype(BF16)
            for hh in range(2):
                sl = slice(LANES * hh, LANES * (hh + 1))
                half = slice(64 * hh, 64 * (hh + 1))
                dvt_ref[0, half, :] += _dot_nt(dot_ref[i, half, :], p_ref[hh])
                dkt_ref[0, sl, :] += _dot_nt(qt_ref[i, sl, :], ds_ref[hh])
                dqt_ref[i, sl, :] += _dot(kt_ref[0, sl, :], ds_ref[hh])

        n_off = nb - 1 - j

        def step(u, carry):
            i0 = j + 1 + 2 * u
            scores(i0 + 1, 1)
            grads(i0, 0, False)
            scores(jnp.where(i0 + 2 < nb, i0 + 2, j), 0)
            grads(i0 + 1, 1, False)
            return carry

        scores(jnp.where(n_off > 0, j + 1, j), 0)
        lax.fori_loop(0, n_off // 2, step, 0)

        @pl.when(n_off % 2 == 1)
        def _():
            scores(j, 1)
            grads(nb - 1, 0, False)
            grads(j, 1, True)

        @pl.when(n_off % 2 == 0)
        def _():
            grads(j, 0, True)

    blk = lambda w: pl.BlockSpec((tb, w), lambda p, j: (j, p))
    stat = pl.BlockSpec((1, nb, 8, tb), lambda p, j: (p, 0, 0, 0))
    pair_t = lambda w: pl.BlockSpec((nb, w, tb), lambda p, j: (0, p, 0))
    blk_t = lambda w: pl.BlockSpec((1, w, tb), lambda p, j: (j, p, 0))
    return _pcall(body, name="mla_bwd", grid=(MLA_HEADS // 2, nb),
                  in_specs=[pair_t(256), blk(256), blk_t(256), blk(LANES), pair_t(LANES), stat, stat],
                  out_specs=[pair_t(256), blk_t(256), blk_t(LANES)],
                  out_shape=[SDS((nb, 2048, tb), F32), SDS((nb, 2048, tb), F32), SDS((nb, D, tb), F32)],
                  scratch=[pltpu.VMEM((2, 2, tb, tb), F32), pltpu.VMEM((2, 2, tb, tb), F32), pltpu.VMEM((2, tb, tb), BF16),
                           pltpu.VMEM((2, tb, tb), BF16), pltpu.VMEM((2, tb, LANES), BF16)],
                  sem=("parallel", "arbitrary"))(qt, km, kt, vm, dot, lse, delta)


def _swa_bwd(sinks, qs, ks, vs, do, o, lse):
    T = qs.shape[0]
    nb, cur, prev = _swa_specs(T)

    def body(sink_ref, q_ref, kc_ref, kp_ref, vc_ref, vp_ref, do_ref, o_ref, l_ref,
             dq_ref, dkc_ref, dkp_ref, dvc_ref, dvp_ref, dsink_ref, kb_ref, vb_ref, s_ref, dp_ref, p_ref, ds_ref):
        n = pl.program_id(0)

        @pl.when(n == 0)
        def _():
            dsink_ref[...] = jnp.zeros_like(dsink_ref)

        mask = _swa_mask(n)
        lo = lax.broadcasted_iota(jnp.int32, (WINDOW, LANES), 1) < 64
        hi = jnp.logical_not(lo)
        lane8 = lax.broadcasted_iota(jnp.int32, (8, LANES), 1)
        for g in range(2):
            gs = slice(LANES * g, LANES * (g + 1))
            kb_ref[g] = jnp.concatenate([kp_ref[:, gs], kc_ref[:, gs]], axis=0)
            vb_ref[g] = jnp.concatenate([vp_ref[:, gs], vc_ref[:, gs]], axis=0)

        def head(h):
            sl = slice(LANES * (h // 2), LANES * (h // 2 + 1))
            hm = lo if h % 2 == 0 else hi
            qp = q_ref[:, sl]
            return hm, sl, jnp.where(hm, qp, jnp.zeros_like(qp)), jnp.where(hm, do_ref[:, sl], 0.0).astype(BF16)

        for h in range(SWA_HEADS):
            _, _, qh, dom = head(h)
            s_ref[h] = _dot_nt(qh, kb_ref[h // 8])
            dp_ref[h] = _dot_nt(dom, vb_ref[h // 8])
        dsink = jnp.zeros((8, LANES), F32)
        for h in range(SWA_HEADS):
            hm, sl, _, _ = head(h)
            lse_h = jnp.max(jnp.where(hm, l_ref[:, sl], -jnp.inf), axis=1, keepdims=True)
            delta = jnp.sum(jnp.where(hm, do_ref[:, sl] * o_ref[:, sl], 0.0), axis=1, keepdims=True)
            p = jnp.exp(jnp.where(mask, s_ref[h] * SWA_SCALE, NEG) - lse_h)
            p_ref[h] = p.astype(BF16)
            ds_ref[h] = (p * (dp_ref[h] - delta) * SWA_SCALE).astype(BF16)
            d_sink = -jnp.sum(jnp.exp(sink_ref[h] - lse_h) * delta, axis=0, keepdims=True)
            dsink = dsink + jnp.where(lane8 == h, d_sink, 0.0)
        dsink_ref[...] += dsink
        for g in range(2):
            gs = slice(LANES * g, LANES * (g + 1))
            dkb = jnp.zeros((2 * WINDOW, LANES), F32)
            dvb = jnp.zeros((2 * WINDOW, LANES), F32)
            for j in range(4 * g, 4 * g + 4):
                dqs = []
                for h in (2 * j, 2 * j + 1):
                    _, _, qh, dom = head(h)
                    dvb = dvb + _dot_tn(p_ref[h], dom)
                    dkb = dkb + _dot_tn(ds_ref[h], qh)
                    dqs.append(_dot(ds_ref[h], kb_ref[g]))
                dq_ref[:, LANES * j:LANES * (j + 1)] = jnp.where(lo, dqs[0], dqs[1])
            dkp_ref[:, gs] = dkb[:WINDOW]
            dkc_ref[:, gs] = dkb[WINDOW:]
            dvp_ref[:, gs] = dvb[:WINDOW]
            dvc_ref[:, gs] = dvb[WINDOW:]

    band = pltpu.VMEM((2, 2 * WINDOW, LANES), BF16)
    return _pcall(body, name="swa_bwd", grid=(nb,),
                  in_specs=[pl.BlockSpec(memory_space=pltpu.SMEM), cur(D), cur(256), prev(256), cur(256), prev(256),
                            cur(D), cur(D), cur(D)],
                  out_specs=[cur(D), cur(256), cur(256), cur(256), cur(256), _full((8, LANES))],
                  out_shape=[SDS((T, D), F32), SDS((T, 256), F32), SDS((T, 256), F32), SDS((T, 256), F32), SDS((T, 256), F32),
                             SDS((8, LANES), F32)],
                  scratch=[band, band, pltpu.VMEM((SWA_HEADS, WINDOW, 2 * WINDOW), F32),
                           pltpu.VMEM((SWA_HEADS, WINDOW, 2 * WINDOW), F32), pltpu.VMEM((SWA_HEADS, WINDOW, 2 * WINDOW), BF16),
                           pltpu.VMEM((SWA_HEADS, WINDOW, 2 * WINDOW), BF16)],
                  sem=("arbitrary",))(sinks, qs, ks, ks, vs, vs, do, o, lse)


def _bwd_qkv(dqm, dkm, dvm, dqs, dkc, dkp, dvc, dvp, z, gq, gkv, wqb, wkn, wv, tab_m, tab_s):
    T = z.shape[0]
    tm = WINDOW
    nb = T // tm
    per = dqm.shape[2] // tm

    def body(dqm_ref, dkm_ref, dvm_ref, dqs_ref, dkc_ref, dkp_ref, dvc_ref, dvp_ref, qa_ref, kva_ref, gq_ref, gkv_ref,
             wqb_ref, wkn_ref, wv_ref, cm_ref, am_ref, bm_ref, cs_ref, as_ref, bs_ref,
             dq_out, dkn_out, dv_out, dsq_ref, drest_ref, dgq_ref, dgkv_ref):
        i = pl.program_id(0)

        @pl.when(i == 0)
        def _():
            dgq_ref[...] = jnp.zeros_like(dgq_ref)
            dgkv_ref[...] = jnp.zeros_like(dgkv_ref)

        cm, am, bm = cm_ref[...], -am_ref[...], -bm_ref[...]
        cs, as_, bs = cs_ref[...], -as_ref[...], -bs_ref[...]
        lane = lax.broadcasted_iota(jnp.int32, (tm, LANES), 1)
        nope = lane < MLA_NOPE
        roped = jnp.logical_and(lane >= MLA_NOPE, lane < MLA_NOPE + MLA_ROPE)
        dkr = jnp.zeros((tm, LANES), F32)
        for h in range(MLA_HEADS):
            sl = slice(LANES * h, LANES * (h + 1))
            dq_out[:, sl] = _rope(dqm_ref[0, sl, :].T, cm, am, bm, MLA_ROPE // 2).astype(BF16)
            dk_h = dkm_ref[0, sl, :].T
            dkn_out[:, sl] = jnp.where(nope, dk_h, 0.0).astype(BF16)
            dkr = dkr + jnp.where(roped, dk_h, 0.0)
        for j in range(D // LANES):
            sl = slice(LANES * j, LANES * (j + 1))
            dv_out[:, sl] = dvm_ref[0, sl, :].T.astype(BF16)
        dqn = _dot_nt(dq_out[...], wqb_ref[...])
        dkvn = _dot_nt(dkn_out[...], wkn_ref[...]) + _dot_nt(dv_out[...], wv_ref[...])
        dqa, dgq = _rms_bwd(dqn, qa_ref[...], gq_ref[...])
        dkva, dgkv = _rms_bwd(dkvn, kva_ref[...], gkv_ref[...])
        dgq_ref[...] += dgq
        dgkv_ref[...] += dgkv
        for j in range(D // LANES):
            sl = slice(LANES * j, LANES * (j + 1))
            dsq_ref[:, sl] = _rope(dqs_ref[:, sl], cs, as_, bs, SWA_HD // 2).astype(BF16)
        keep = (i < nb - 1).astype(F32)
        drest_ref[:, 0:256] = dqa.astype(BF16)
        for j in range(2):
            sl = slice(LANES * j, LANES * (j + 1))
            dk = dkc_ref[:, sl] + keep * dkp_ref[:, sl]
            drest_ref[:, 256 + LANES * j:256 + LANES * (j + 1)] = _rope(dk, cs, as_, bs, SWA_HD // 2).astype(BF16)
        drest_ref[:, 512:768] = (dvc_ref[...] + keep * dvp_ref[...]).astype(BF16)
        drest_ref[:, 768:896] = dkva.astype(BF16)
        drest_ref[:, 896:1024] = _rope(dkr, cm, am, bm, MLA_ROPE // 2).astype(BF16)

    nxt = pl.BlockSpec((tm, 256), lambda i: (jnp.minimum(i + 1, nb - 1), 0))
    tab = [_rows(tm, LANES)] * 6
    return _pcall(body, name="bwd_qkv", grid=(nb,),
                  in_specs=[pl.BlockSpec((1, 2048, tm), lambda i: (i // per, 0, i % per)),
                            pl.BlockSpec((1, 2048, tm), lambda i: (i // per, 0, i % per)),
                            pl.BlockSpec((1, 1024, tm), lambda i: (i // per, 0, i % per)), _rows(tm, 1024), _rows(tm, 256), nxt,
                            _rows(tm, 256), nxt, _rows(tm, 256, 12), _rows(tm, 128, 30), _full((1, Q_LORA)), _full((1, KV_LORA)),
                            _full((Q_LORA, 2048)), _full((KV_LORA, 2048)), _full((KV_LORA, 1024))] + tab,
                  out_specs=[_rows(tm, 2048), _rows(tm, 2048), _rows(tm, 1024), _rows(tm, 1024), _rows(tm, 1024),
                             _full((1, Q_LORA)), _full((1, KV_LORA))],
                  out_shape=[SDS((T, 2048), BF16), SDS((T, 2048), BF16), SDS((T, 1024), BF16), SDS((T, 1024), BF16),
                             SDS((T, 1024), BF16), SDS((1, Q_LORA), F32), SDS((1, KV_LORA), F32)],
                  sem=("arbitrary",))(dqm, dkm, dvm, dqs, dkc, dkp, dvc, dvp, z, z, gq, gkv, wqb, wkn, wv, *tab_m, *tab_s)


def _bwd_in(dsq, dga, dgb, drest, w_in_p, x, g1, dx1, tm):
    T = x.shape[0]

    def body(a_ref, b_ref, c_ref, d_ref, w_ref, x_ref, g_ref, dx1_ref, dx_ref, dg_ref):
        @pl.when(pl.program_id(0) == 0)
        def _():
            dg_ref[...] = jnp.zeros_like(dg_ref)

        dh = (_dot_nt(a_ref[...], w_ref[:, 0:1024]) + _dot_nt(b_ref[...], w_ref[:, 1024:2048])
              + _dot_nt(c_ref[...], w_ref[:, 2048:3072]) + _dot_nt(d_ref[...], w_ref[:, 3072:4096]))
        dx, dg = _rms_bwd(dh, x_ref[...], g_ref[...])
        dg_ref[...] += dg
        dx_ref[...] = dx1_ref[...] + dx

    r = _rows(tm, D)
    return _pcall(body, name="bwd_in", grid=(T // tm,),
                  in_specs=[r, r, r, r, _full((D, NZ)), r, _full((1, D)), r],
                  out_specs=[r, _full((1, D))], out_shape=[SDS((T, D), F32), SDS((1, D), F32)],
                  sem=("arbitrary",))(dsq, dga, dgb, drest, w_in_p, x, g1, dx1)


def _wgrad(a, g, name, into=None):
    T, K = a.shape
    N = g.shape[1]
    tk, tn, tt = min(K, 1024), min(N, 1024), min(T, 1024)
    if into is not None:
        buf, weight = into
        _, row0, lane0 = PACK_AT[weight]
        shard = {n: (r, c) for n, r, c in BIG}[weight]
        assert lane0 == 0 and shard[1] == D and tk % shard[0] == 0
        per_step = tk // shard[0]
    assert K % tk == 0 and N % tn == 0 and T % tt == 0, (a.shape, g.shape)
    steps = T // tt

    def body(a_ref, g_ref, *rest):
        o_ref, acc_ref = rest[-2:]
        t = pl.program_id(2)

        @pl.when(t == 0)
        def _():
            acc_ref[...] = jnp.zeros_like(acc_ref)

        acc_ref[...] += _dot_tn(a_ref[...].astype(BF16), g_ref[...].astype(BF16))

        @pl.when(t == steps - 1)
        def _():
            o_ref[...] = acc_ref[...].astype(o_ref.dtype).reshape(o_ref.shape)

    in_specs = [pl.BlockSpec((tt, tk), lambda k, n, t: (t, k)), pl.BlockSpec((tt, tn), lambda k, n, t: (t, n))]
    if into is None:
        return _pcall(body, name=name, grid=(K // tk, N // tn, steps), in_specs=in_specs,
                      out_specs=pl.BlockSpec((tk, tn), lambda k, n, t: (k, n)), out_shape=SDS((K, N), F32),
                      scratch=[pltpu.VMEM((tk, tn), F32)], sem=("parallel", "parallel", "arbitrary"))(a, g)
    assert row0 % shard[0] == 0 and (K // tk) * (N // tn) * per_step == N_CHIPS
    return _pcall(body, name=name, grid=(K // tk, N // tn, steps), in_specs=in_specs + [ANY],
                  out_specs=pl.BlockSpec((per_step, shard[0], tn), lambda k, n, t: (k + n, row0 // shard[0], 0)),
                  out_shape=SDS(buf.shape, buf.dtype),
                  scratch=[pltpu.VMEM((tk, tn), F32)], sem=("parallel", "parallel", "arbitrary"), aliases={2: 0})(a, g, buf)


def _adamw(w, packed_g, m, v, name):
    _, R, C = w.shape
    _, row0, lane0 = PACK_AT[name]
    tr = min(R, 256 if row0 % 256 == 0 else 128)
    assert row0 % tr == 0 and R % tr == 0

    def body(w_ref, g_ref, m_ref, v_ref, go_ref, d_ref, m2_ref, v2_ref):
        g_ = g_ref[:, lane0:lane0 + C]
        go_ref[0] = g_
        m2 = ADAM_B1 * m_ref[0] + (1.0 - ADAM_B1) * g_
        v2 = ADAM_B2 * v_ref[0] + (1.0 - ADAM_B2) * jnp.square(g_)
        m_hat = m2 / (1.0 - ADAM_B1 ** ADAM_STEP)
        v_hat = v2 / (1.0 - ADAM_B2 ** ADAM_STEP)
        d_ref[0] = -ADAM_LR * (m_hat / (jnp.sqrt(v_hat) + ADAM_EPS) + ADAM_WD * w_ref[0])
        m2_ref[0] = m2
        v2_ref[0] = v2

    r = pl.BlockSpec((1, tr, C), lambda i: (0, i, 0))
    return _pcall(body, name="adamw_" + name, grid=(R // tr,),
                  in_specs=[r, pl.BlockSpec((tr, D), lambda i: (row0 // tr + i, 0)), r, r], out_specs=[r] * 4,
                  out_shape=[SDS((1, R, C), F32)] * 4, sem=("parallel",))(w, packed_g, m, v)


def _adamw_small(w, parts, m, v):
    def body(w_ref, p_ref, m_ref, v_ref, g_ref, d_ref, m2_ref, v2_ref):
        g_ = p_ref[0]
        for k in range(1, N_DEV):
            g_ = g_ + p_ref[k]
        g_ref[...] = g_
        m2 = ADAM_B1 * m_ref[...] + (1.0 - ADAM_B1) * g_
        v2 = ADAM_B2 * v_ref[...] + (1.0 - ADAM_B2) * jnp.square(g_)
        m_hat = m2 / (1.0 - ADAM_B1 ** ADAM_STEP)
        v_hat = v2 / (1.0 - ADAM_B2 ** ADAM_STEP)
        d_ref[...] = -ADAM_LR * (m_hat / (jnp.sqrt(v_hat) + ADAM_EPS) + ADAM_WD * w_ref[...])
        m2_ref[...] = m2
        v2_ref[...] = v2

    s = _full((8, D))
    return _pcall(body, name="adamw_small", grid=(1,), in_specs=[s, _full((N_DEV, 8, D)), s, s], out_specs=[s] * 4,
                  out_shape=[SDS((8, D), F32)] * 4, sem=("arbitrary",))(w, parts, m, v)


ANY = pl.BlockSpec(memory_space=pl.ANY)


def _place():
    x, y, c = lax.axis_index("x"), lax.axis_index("y"), lax.axis_index("c")
    chips = [(1 - x, y), (x, 1 - y), (1 - x, 1 - y)]
    return x, y, c, chips


def _all_gather(wpk):
    rows = wpk.shape[0]
    HALF = rows // 2
    assert HALF % 16 == 0

    def body(in_ref, out_ref, send_sems, recv_sems):
        x, y, c, chips = _place()
        half = pl.ds(pl.multiple_of(c * HALF, 16), HALF)
        other = pl.ds(pl.multiple_of((1 - c) * HALF, 16), HALF)

        def copy(k, src, dst, to):
            return pltpu.make_async_remote_copy(src_ref=src, dst_ref=dst, send_sem=send_sems.at[k], recv_sem=recv_sems.at[k],
                                                device_id=to, device_id_type=MESH)

        first = [copy(k, in_ref.at[half], out_ref.at[2 * x + y, half], (cx, cy, c)) for k, (cx, cy) in enumerate(chips)]
        for cp in first:
            cp.start()
        passed = []
        for k, (cx, cy) in enumerate(chips):
            slot = out_ref.at[2 * cx + cy, half]
            copy(k, slot, slot, (x, y, c)).wait_recv()
            fwd = copy(3 + k, slot, slot, (x, y, 1 - c))
            fwd.start()
            passed.append(fwd)
        for k, (cx, cy) in enumerate(chips):
            slot = out_ref.at[2 * cx + cy, other]
            copy(3 + k, slot, slot, (x, y, c)).wait_recv()
        for cp in first + passed:
            cp.wait_send()

    return _pcall(body, name="all_gather_weights", in_specs=[ANY], out_specs=ANY,
                  out_shape=SDS((N_CHIPS, rows, D), BF16),
                  scratch=[pltpu.SemaphoreType.DMA((6,)), pltpu.SemaphoreType.DMA((6,))])(wpk)


HBM = pl.BlockSpec(memory_space=pltpu.HBM)
SEM = pl.BlockSpec(memory_space=pltpu.SEMAPHORE)
DATAFLOW = pltpu.SideEffectType.DATAFLOW_SIDE_EFFECTING


def _in_hbm(a):
    return pltpu.with_memory_space_constraint(a, pltpu.HBM)


def _gather_late_start(wpk, after):
    rows = wpk.shape[0]

    def body(in_ref, land_ref, after_ref, send_sems, recv_sems, in_thru, land_thru, token):
        x, y, c, chips = _place()
        for k, (cx, cy) in enumerate(chips):
            pltpu.make_async_remote_copy(src_ref=in_ref, dst_ref=land_ref.at[2 * x + y], send_sem=send_sems.at[k],
                                         recv_sem=recv_sems.at[k], device_id=(cx, cy, c), device_id_type=MESH).start()
        token[...] = jnp.zeros_like(token)

    return pl.pallas_call(
        body, name="gather_late_start",
        out_shape=(pltpu.SemaphoreType.DMA((3,)), pltpu.SemaphoreType.DMA((3,)), pltpu.HBM(wpk.shape, wpk.dtype),
                   pltpu.HBM((N_CHIPS, rows, D), wpk.dtype), SDS((8, LANES), F32)),
        in_specs=(HBM, HBM, ANY), out_specs=(SEM, SEM, HBM, HBM, pl.BlockSpec(memory_space=pltpu.VMEM)),
        input_output_aliases={0: 2, 1: 3}, compiler_params=pltpu.CompilerParams(has_side_effects=DATAFLOW),
    )(_in_hbm(wpk), _in_hbm(lax.empty((N_CHIPS, rows, D), wpk.dtype)), after)


def _gather_late_wait(send_sems, recv_sems, in_thru, land_thru, after):
    def body(in_ref, land_ref, send_sems, recv_sems, after_ref, after2_ref, in_dead, got_ref):
        x, y, c, chips = _place()
        for k, (cx, cy) in enumerate(chips):
            cp = pltpu.make_async_remote_copy(src_ref=in_ref, dst_ref=land_ref.at[2 * cx + cy], send_sem=send_sems.at[k],
                                              recv_sem=recv_sems.at[k], device_id=(cx, cy, c), device_id_type=MESH)
            cp.wait_send()
            cp.wait_recv()

    return pl.pallas_call(
        body, name="gather_late_wait",
        out_shape=(pltpu.HBM(in_thru.shape, in_thru.dtype), pltpu.HBM(land_thru.shape, land_thru.dtype)),
        in_specs=(HBM, HBM, SEM, SEM, ANY, ANY), out_specs=(HBM, HBM), input_output_aliases={0: 0, 1: 1},
        compiler_params=pltpu.CompilerParams(has_side_effects=DATAFLOW),
    )(in_thru, land_thru, send_sems, recv_sems, *after)[1]


def _rs_sibling(gpk):
    HALF = gpk.shape[1] // 2

    def body(in_ref, out_ref, send_sem, recv_sem):
        x, y, c, _ = _place()
        theirs = pl.ds(pl.multiple_of((1 - c) * HALF, 8), HALF)
        cp = pltpu.make_async_remote_copy(src_ref=in_ref.at[:, theirs], dst_ref=out_ref, send_sem=send_sem, recv_sem=recv_sem,
                                          device_id=(x, y, 1 - c), device_id_type=MESH)
        cp.start()
        cp.wait()

    return _pcall(body, name="rs_sibling", in_specs=[ANY], out_specs=ANY, out_shape=SDS((N_CHIPS, HALF, D), F32),
                  scratch=[pltpu.SemaphoreType.DMA, pltpu.SemaphoreType.DMA])(gpk)


def _rs_add_sibling(cidx, gpk, got):
    HALF = got.shape[1]
    th = HALF // 4
    nh = HALF // th
    assert th % 16 == 0

    def body(c_ref, a_ref, b_ref, o_ref):
        o_ref[...] = (a_ref[...] + b_ref[...]).astype(BF16)

    gs = pltpu.PrefetchScalarGridSpec(
        num_scalar_prefetch=1, grid=(N_CHIPS, nh),
        in_specs=[pl.BlockSpec((1, th, D), lambda j, i, c: (j, c[0] * nh + i, 0)), pl.BlockSpec((1, th, D), lambda j, i, c: (j, i, 0))],
        out_specs=pl.BlockSpec((1, th, D), lambda j, i, c: (j, i, 0)))
    return pl.pallas_call(body, name="rs_add_sibling", grid_spec=gs, out_shape=SDS((N_CHIPS, HALF, D), BF16),
                          compiler_params=pltpu.CompilerParams(dimension_semantics=("parallel", "parallel"),
                                                               vmem_limit_bytes=32 << 20))(cidx, gpk, got)


def _rs_chips_start(part, small, after):
    def body(p_ref, s_ref, land_ref, sland_ref, after_ref, send_sems, recv_sems, p_thru, s_thru, land_thru, sland_thru, token):
        x, y, c, chips = _place()
        for k, (cx, cy) in enumerate(chips):
            pltpu.make_async_remote_copy(src_ref=p_ref.at[2 * cx + cy], dst_ref=land_ref.at[2 * x + y], send_sem=send_sems.at[k],
                                         recv_sem=recv_sems.at[k], device_id=(cx, cy, c), device_id_type=MESH).start()
        peers = [(x, y, 1 - c)] + [(cx, cy, c) for cx, cy in chips] + [(cx, cy, 1 - c) for cx, cy in chips]
        for k, to in enumerate(peers):
            pltpu.make_async_remote_copy(src_ref=s_ref, dst_ref=sland_ref.at[4 * x + 2 * y + c], send_sem=send_sems.at[3 + k],
                                         recv_sem=recv_sems.at[3 + k], device_id=to, device_id_type=MESH).start()
        token[...] = jnp.zeros_like(token)

    return pl.pallas_call(
        body, name="rs_chips_start",
        out_shape=(pltpu.SemaphoreType.DMA((10,)), pltpu.SemaphoreType.DMA((10,)), pltpu.HBM(part.shape, part.dtype),
                   pltpu.HBM(small.shape, small.dtype), pltpu.HBM(part.shape, part.dtype), pltpu.HBM((N_DEV, 8, D), F32),
                   SDS((8, LANES), F32)),
        in_specs=(HBM, HBM, HBM, HBM, ANY), out_specs=(SEM, SEM, HBM, HBM, HBM, HBM, pl.BlockSpec(memory_space=pltpu.VMEM)),
        input_output_aliases={0: 2, 1: 3, 2: 4, 3: 5}, compiler_params=pltpu.CompilerParams(has_side_effects=DATAFLOW),
    )(_in_hbm(part), _in_hbm(small), _in_hbm(lax.empty(part.shape, part.dtype)), _in_hbm(lax.empty((N_DEV, 8, D), F32)), after)


def _rs_chips_wait(send_sems, recv_sems, p_thru, s_thru, land_thru, sland_thru, after):
    def body(p_ref, s_ref, land_ref, sland_ref, send_sems, recv_sems, *after_and_outputs):
        x, y, c, chips = _place()
        for k, (cx, cy) in enumerate(chips):
            cp = pltpu.make_async_remote_copy(src_ref=p_ref.at[0], dst_ref=land_ref.at[2 * cx + cy], send_sem=send_sems.at[k],
                                              recv_sem=recv_sems.at[k], device_id=(cx, cy, c), device_id_type=MESH)
            cp.wait_send()
            cp.wait_recv()
        peers = [(x, y, 1 - c)] + [(cx, cy, c) for cx, cy in chips] + [(cx, cy, 1 - c) for cx, cy in chips]
        for k, (px, py, pc) in enumerate(peers):
            cp = pltpu.make_async_remote_copy(src_ref=s_ref, dst_ref=sland_ref.at[4 * px + 2 * py + pc], send_sem=send_sems.at[3 + k],
                                              recv_sem=recv_sems.at[3 + k], device_id=(px, py, pc), device_id_type=MESH)
            cp.wait_send()
            cp.wait_recv()

    hbm = lambda a: pltpu.HBM(a.shape, a.dtype)
    outs = pl.pallas_call(
        body, name="rs_chips_wait", out_shape=(hbm(p_thru), hbm(s_thru), hbm(land_thru), hbm(sland_thru)),
        in_specs=(HBM, HBM, HBM, HBM, SEM, SEM) + (ANY,) * len(after), out_specs=(HBM, HBM, HBM, HBM),
        input_output_aliases={0: 0, 1: 1, 2: 2, 3: 3}, compiler_params=pltpu.CompilerParams(has_side_effects=DATAFLOW),
    )(p_thru, s_thru, land_thru, sland_thru, send_sems, recv_sems, *after)
    return outs[0], outs[2], outs[3]


def _rs_add_chips(qidx, part, parts):
    HALF = part.shape[1]
    th = HALF // 4
    assert th % 16 == 0

    def body(q_ref, own_ref, p_ref, o_ref):
        for me in range(N_CHIPS):
            @pl.when(q_ref[0] == me)
            def _(me=me):
                t = [(own_ref[0] if j == me else p_ref[j]).astype(F32) for j in range(N_CHIPS)]
                o_ref[...] = ((t[0] + t[1]) + t[2]) + t[3]

    gs = pltpu.PrefetchScalarGridSpec(
        num_scalar_prefetch=1, grid=(HALF // th,),
        in_specs=[pl.BlockSpec((1, th, D), lambda i, q: (q[0], i, 0)), pl.BlockSpec((N_CHIPS, th, D), lambda i, q: (0, i, 0))],
        out_specs=pl.BlockSpec((th, D), lambda i, q: (i, 0)))
    return pl.pallas_call(body, name="rs_add_chips", grid_spec=gs, out_shape=SDS((HALF, D), F32),
                          compiler_params=pltpu.CompilerParams(dimension_semantics=("parallel",),
                                                               vmem_limit_bytes=32 << 20))(qidx, part, parts)


def _rs_join(mine, core, name):
    def body(in_ref, out_ref, send_sem, recv_sem):
        x, y, c, _ = _place()
        cp = pltpu.make_async_remote_copy(src_ref=in_ref, dst_ref=out_ref, send_sem=send_sem, recv_sem=recv_sem,
                                          device_id=(x, y, 1 - c), device_id_type=MESH)
        cp.start()
        cp.wait()

    theirs = _pcall(body, name=name, in_specs=[ANY], out_specs=ANY, out_shape=SDS(mine.shape, F32),
                    scratch=[pltpu.SemaphoreType.DMA, pltpu.SemaphoreType.DMA])(mine)
    return jnp.where(core == 0, jnp.concatenate([mine, theirs]), jnp.concatenate([theirs, mine]))


def _reduce_late_start(gpk, after):
    rows = gpk.shape[1]
    HALF = rows // 2
    assert HALF % 16 == 0

    def body(in_ref, land_ref, after_ref, send_sems, recv_sems, in_thru, land_thru, token):
        x, y, c, chips = _place()
        me = 4 * x + 2 * y + c
        peers = [(x, y, 1 - c)] + [(cx, cy, c) for cx, cy in chips] + [(cx, cy, 1 - c) for cx, cy in chips]
        for k, (px, py, pc) in enumerate(peers):
            src = in_ref.at[2 * px + py, pl.ds(pl.multiple_of(pc * HALF, 16), HALF)]
            pltpu.make_async_remote_copy(src_ref=src, dst_ref=land_ref.at[me], send_sem=send_sems.at[k], recv_sem=recv_sems.at[k],
                                         device_id=(px, py, pc), device_id_type=MESH).start()
        token[...] = jnp.zeros_like(token)

    return pl.pallas_call(
        body, name="reduce_late_start",
        out_shape=(pltpu.SemaphoreType.DMA((7,)), pltpu.SemaphoreType.DMA((7,)), pltpu.HBM(gpk.shape, gpk.dtype),
                   pltpu.HBM((N_DEV, HALF, D), gpk.dtype), SDS((8, LANES), F32)),
        in_specs=(HBM, HBM, ANY), out_specs=(SEM, SEM, HBM, HBM, pl.BlockSpec(memory_space=pltpu.VMEM)),
        input_output_aliases={0: 2, 1: 3}, compiler_params=pltpu.CompilerParams(has_side_effects=DATAFLOW),
    )(_in_hbm(gpk), _in_hbm(lax.empty((N_DEV, HALF, D), gpk.dtype)), after)


def _reduce_late_wait(send_sems, recv_sems, in_thru, land_thru, after):
    def body(in_ref, land_ref, send_sems, recv_sems, after_ref, in_out, got_ref):
        x, y, c, chips = _place()
        peers = [(x, y, 1 - c)] + [(cx, cy, c) for cx, cy in chips] + [(cx, cy, 1 - c) for cx, cy in chips]
        for k, (px, py, pc) in enumerate(peers):
            cp = pltpu.make_async_remote_copy(src_ref=land_ref.at[0], dst_ref=land_ref.at[4 * px + 2 * py + pc],
                                              send_sem=send_sems.at[k], recv_sem=recv_sems.at[k],
                                              device_id=(px, py, pc), device_id_type=MESH)
            cp.wait_send()
            cp.wait_recv()

    return pl.pallas_call(
        body, name="reduce_late_wait",
        out_shape=(pltpu.HBM(in_thru.shape, in_thru.dtype), pltpu.HBM(land_thru.shape, land_thru.dtype)),
        in_specs=(HBM, HBM, SEM, SEM, ANY), out_specs=(HBM, HBM), input_output_aliases={0: 0, 1: 1},
        compiler_params=pltpu.CompilerParams(has_side_effects=DATAFLOW),
    )(in_thru, land_thru, send_sems, recv_sems, after)


def _reduce_late_add(didx, gpk, parts):
    HALF = parts.shape[1]
    th = HALF // 4
    nh = HALF // th
    assert th % 16 == 0

    def body(d_ref, own_ref, p_ref, o_ref):
        for me in range(N_DEV):
            @pl.when(d_ref[0] == me)
            def _(me=me):
                t = [(own_ref[0] if j == me else p_ref[j]).astype(F32) for j in range(N_DEV)]
                o_ref[...] = ((((((t[0] + t[1]) + t[2]) + t[3]) + t[4]) + t[5]) + t[6]) + t[7]

    gs = pltpu.PrefetchScalarGridSpec(
        num_scalar_prefetch=1, grid=(nh,),
        in_specs=[pl.BlockSpec((1, th, D), lambda i, d: (d[1], d[2] * nh + i, 0)), pl.BlockSpec((N_DEV, th, D), lambda i, d: (0, i, 0))],
        out_specs=pl.BlockSpec((th, D), lambda i, d: (i, 0)))
    return pl.pallas_call(body, name="reduce_late_add", grid_spec=gs, out_shape=SDS((HALF, D), F32),
                          compiler_params=pltpu.CompilerParams(dimension_semantics=("parallel",),
                                                               vmem_limit_bytes=32 << 20))(didx, gpk, parts)


def _pack_early(b, dtype):
    lanes = lambda a: jnp.pad(a.astype(dtype), ((0, 0), (0, D - a.shape[1])))
    pair = jnp.concatenate([b["w_q_b"].astype(dtype), b["w_ple"].astype(dtype), jnp.zeros((256, D - 640), dtype)], axis=1)
    return jnp.concatenate([lanes(b["w_in"]), pair, lanes(b["w_kv_b"])], axis=0)


def _pack_late(b, dtype):
    return jnp.concatenate([b[n].astype(dtype) for n in ("w_mla_up", "w_swa_up", "w_out", "w_ple_gate", "w_mlp_up", "w_mlp_down")],
                           axis=0)


def _unpack_shards(pk, which):
    return {n: pk[PACK_AT[n][1]:PACK_AT[n][1] + r, PACK_AT[n][2]:PACK_AT[n][2] + c] for n, r, c in BIG if PACK_AT[n][0] == which}


def _full_weights(gathered, own, chip, which):
    own_b = _unpack_shards(own, which)
    per_chip = [{n: jnp.where(chip == j, own_b[n], blk) for n, blk in _unpack_shards(gathered[j], which).items()}
                for j in range(N_CHIPS)]
    out = {}
    for n in own_b:
        shards = [pc[n] for pc in per_chip]
        if n == "w_in":
            out["w_in_p"] = _w_in_internal(shards)
        else:
            out[n] = jnp.concatenate(shards, axis=1 if n in COL_SHARDED else 0)
    return out


def _split_full_grads(grads, pack, dtype):
    shard = {n: (r, c) for n, r, c in BIG}
    chunks = []
    for j in range(N_CHIPS):
        blocks = {}
        for n, g in grads.items():
            if n == "w_in_p":
                blocks["w_in"] = _w_in_grad_shard(g, j)
                continue
            r, c = shard[n]
            blocks[n] = g[:, j * c:(j + 1) * c] if n in COL_SHARDED else g[j * r:(j + 1) * r]
        chunks.append(pack(blocks, dtype))
    return jnp.stack(chunks)


W_IN_SHARD = 936
W_IN_SEGMENTS = ((0, 256, (3072,)), (256, 384, (3840,)), (384, 416, (4032,)), (416, 1440, (0,)), (1440, 1504, (3328, 3392)),
                 (1504, 1568, (3456, 3520)), (1568, 1632, (3584, 3648)), (1632, 1696, (3712, 3776)), (1696, 3744, (1024,)))


def _w_in_internal(shards):
    def cols(a, b):
        out = []
        for j, s in enumerate(shards):
            lo, hi = max(a, W_IN_SHARD * j), min(b, W_IN_SHARD * (j + 1))
            if lo < hi:
                out.append(s[:, lo - W_IN_SHARD * j:hi - W_IN_SHARD * j])
        return out

    pieces = {}
    for a, b, places in W_IN_SEGMENTS:
        for at in places:
            pieces[at] = cols(a, b)
    zeros = lambda n: [jnp.zeros((D, n), shards[0].dtype)]
    pieces[3968] = zeros(64)
    pieces[4064] = zeros(32)
    return jnp.concatenate([piece for at in sorted(pieces) for piece in pieces[at]], axis=1)


def _w_in_grad_shard(g, j):
    def internal(a, b):
        out = []
        while a < b:
            end = min(b, (a // D + 1) * D)
            out.append(g[a // D][:, a % D:a % D + end - a])
            a = end
        return out

    out = []
    for a, b, places in W_IN_SEGMENTS:
        lo, hi = max(a, W_IN_SHARD * j), min(b, W_IN_SHARD * (j + 1))
        if lo < hi:
            parts = [internal(at + lo - a, at + hi - a) for at in places]
            if len(parts) == 1:
                out += parts[0]
            else:
                assert len(parts[0]) == len(parts[1]) == 1
                out.append(parts[0][0] + parts[1][0])
    return jnp.concatenate(out, axis=1)


def _local_step(x, p, tgt, w, small, late_weights, late_grads_out):
    T = x.shape[0]
    tm = 256
    tb = 256
    w_in_p = w["w_in_p"]
    wqb = jnp.pad(w["w_q_b"].reshape(Q_LORA, MLA_HEADS, 96), ((0, 0), (0, 0), (0, 32))).reshape(Q_LORA, 2048)
    wkv = w["w_kv_b"].reshape(KV_LORA, MLA_HEADS, 128)
    wkn = jnp.pad(wkv[:, :, :64], ((0, 0), (0, 0), (0, 64))).reshape(KV_LORA, 2048)
    wv = wkv[:, :, 64:].reshape(KV_LORA, 1024)
    tab_m = _rope_tables(T, "mla")
    tab_s = _rope_tables(T, "swa")
    g1, gq, gkv, sinks = small["g_mix_pre"], small["g_q_a"], small["g_kv_a"], small["sinks"]
    g2, g3, g4, g5 = small["g_mix_post"], small["g_mlp_pre"], small["g_mlp_post"], small["g_ple"]
    sink_vec = sinks.reshape(SWA_HEADS)

    z, h1 = _fwd_in(x, g1, w_in_p, tm)
    qn, kvn, km, vm, qt, kt, vt, qs, ks, vs = _fwd_qkv(z, gq, gkv, wqb, wkn, wv, tab_m, tab_s, tb)
    om, lse_m = _mla_fwd(qt, km, vt, tb)
    os_, lse_s = _swa_fwd(sink_vec, qs, ks, vs)
    w = {**w, **late_weights((om, os_))}
    y, yo, au, bu, x1 = _fwd_mix(om, os_, z, x, w["w_mla_up"], w["w_swa_up"], w["w_out"], g2, tm)
    h2, u = _fwd_mlp_up(x1, g3, w["w_mlp_up"], tm)
    d, x2 = _fwd_mlp_down(u, w["w_mlp_down"], x1, g4, tm)
    loss, dx2, dgt, de0, dg5 = _ple_fwd_bwd(p, x2, tgt, w["w_ple"], g5, w["w_ple_gate"], tm)

    dd, da, dg4 = _bwd_mlp_down(dx2, d, g4, w["w_mlp_down"], u, tm)
    dx1, dg3 = _bwd_mlp_up(da, w["w_mlp_up"], x1, g3, dx2, tm)
    dyo, dg2, dau, dbu, dga, dgb, dos, delta_m, dom_t = _bwd_mix(dx1, yo, g2, w["w_out"], z, au, bu, w["w_mla_up"],
                                                                w["w_swa_up"], om, tb)
    gpk_late = lax.empty((N_CHIPS, PACK_ROWS["late"], D), BF16)
    for weight, a_, g_ in (("w_mla_up", om, dau), ("w_swa_up", os_, dbu), ("w_out", y, dyo), ("w_ple_gate", x2, dgt),
                           ("w_mlp_up", h2, da), ("w_mlp_down", u, dd)):
        gpk_late = _wgrad(a_, g_, "wgrad_" + weight[2:], into=(gpk_late, weight))
    token = late_grads_out(gpk_late)
    delta_m = delta_m + token[0, 0]
    dqm, dkm, dvm = _mla_bwd(qt, km, kt, vm, dom_t, lse_m, delta_m, tb)
    dqs, dkc, dkp, dvc, dvp, dsink = _swa_bwd(sink_vec, qs, ks, vs, dos, os_, lse_s)
    dqb, dknb, dvb, dsq, drest, dgq, dgkv = _bwd_qkv(dqm, dkm, dvm, dqs, dkc, dkp, dvc, dvp, z, gq, gkv, wqb, wkn, wv,
                                                      tab_m, tab_s)
    gx, dg1 = _bwd_in(dsq, dga, dgb, drest, w_in_p, x, g1, dx1, tm)

    g_in_p = [_wgrad(h1, dsq, "wgrad_in_sq"), _wgrad(h1, dga, "wgrad_in_ga"), _wgrad(h1, dgb, "wgrad_in_gb"),
              _wgrad(h1, drest, "wgrad_in_rest")]
    g_qb_p = _wgrad(qn, dqb, "wgrad_q_b")
    g_kn_p = _wgrad(kvn, dknb, "wgrad_kv_b_nope")
    g_v_p = _wgrad(kvn, dvb, "wgrad_kv_b_v")
    grads = {
        "w_in_p": g_in_p,
        "w_q_b": g_qb_p.reshape(Q_LORA, MLA_HEADS, 128)[:, :, :96].reshape(Q_LORA, 1536),
        "w_kv_b": jnp.concatenate([g_kn_p.reshape(KV_LORA, MLA_HEADS, 128)[:, :, :64], g_v_p.reshape(KV_LORA, MLA_HEADS, 64)],
                                  axis=2).reshape(KV_LORA, 2048),
        "w_ple": _wgrad(p, de0, "wgrad_ple"),
    }
    small_grads = {"g_mix_pre": dg1, "g_q_a": dgq, "g_kv_a": dgkv, "sinks": dsink[0:1, 0:SWA_HEADS], "g_mix_post": dg2,
                   "g_mlp_pre": dg3, "g_mlp_post": dg4, "g_ple": dg5}
    return loss, gx, grads, small_grads


def _pack_small(vals, fill, scalar=None):
    wide = [vals[n] for n, k in SMALL if k == D]
    narrow = [vals[n] for n, k in SMALL if k != D]
    used = sum(k for _, k in SMALL if k != D)
    last = jnp.concatenate(narrow + [jnp.full((1, D - used), fill, F32)], axis=1)
    rest = jnp.full((2, D), fill, F32)
    if scalar is not None:
        rest = jnp.concatenate([jnp.concatenate([scalar, rest[0:1, 1:]], axis=1), rest[1:2]], axis=0)
    return jnp.concatenate(wide + [last, rest], axis=0)


def _unpack_small(pk):
    out, row, off = {}, 0, 0
    for n, k in SMALL:
        if k == D:
            out[n] = pk[row:row + 1]
            row += 1
    for n, k in SMALL:
        if k != D:
            out[n] = pk[5:6, off:off + k]
            off += k
    return out


def kernel(x, p, g_mix_pre, w_in, g_q_a, w_q_b, g_kv_a, w_kv_b, sinks, w_mla_up, w_swa_up, w_out, g_mix_post, g_mlp_pre, w_mlp_up, w_mlp_down, g_mlp_post, w_ple, g_ple, w_ple_gate, loss_target, m_g_mix_pre, m_w_in, m_g_q_a, m_w_q_b, m_g_kv_a, m_w_kv_b, m_sinks, m_w_mla_up, m_w_swa_up, m_w_out, m_g_mix_post, m_g_mlp_pre, m_w_mlp_up, m_w_mlp_down, m_g_mlp_post, m_w_ple, m_g_ple, m_w_ple_gate, v_g_mix_pre, v_w_in, v_g_q_a, v_w_q_b, v_g_kv_a, v_w_kv_b, v_sinks, v_w_mla_up, v_w_swa_up, v_w_out, v_g_mix_post, v_g_mlp_pre, v_w_mlp_up, v_w_mlp_down, v_g_mlp_post, v_w_ple, v_g_ple, v_w_ple_gate):
    given = dict(locals())
    big_w = {n: given[n][0] for n, _, _ in BIG}
    small_w = {n: given[n] for n, _ in SMALL}
    small_m = {n: given["m_" + n] for n, _ in SMALL}
    small_v = {n: given["v_" + n] for n, _ in SMALL}

    core = lax.axis_index("c")
    chip = 2 * lax.axis_index("x") + lax.axis_index("y")
    core_i = core.astype(jnp.int32).reshape(1)
    chip_i = chip.astype(jnp.int32).reshape(1)
    dev_i = jnp.stack([2 * chip + core, chip, core]).astype(jnp.int32)

    own_early = _pack_early(big_w, BF16)
    own_late = _pack_late(big_w, BF16)
    got_early = _all_gather(own_early)
    late_flight = _gather_late_start(own_late, got_early)
    weights = _full_weights(got_early, own_early, chip, "early")
    step_small = {**small_w, "g_mix_pre": small_w["g_mix_pre"] + late_flight[4][0, 0]}

    def late_weights(after):
        return _full_weights(_gather_late_wait(*late_flight[:4], after), own_late, chip, "late")

    flight = {}

    def late_grads_out(gpk_late):
        flight["late"] = _reduce_late_start(gpk_late, dev_i)
        return flight["late"][4]

    loss_blk, gx, grads, small_grads = _local_step(x[0], p[0, 0], loss_target[0], weights, step_small, late_weights,
                                                   late_grads_out)

    gpk = _split_full_grads(grads, _pack_early, F32)
    got = _rs_sibling(gpk)
    part = _rs_add_sibling(core_i, gpk, got)
    small_own = _pack_small(small_grads, 0.0, loss_blk[0:1, 0:1])
    early_flight = _rs_chips_start(part, small_own, dev_i)

    out_g, out_d, out_m, out_v = {}, {}, {}, {}
    gpk_late, parts_late = _reduce_late_wait(*flight["late"][:4], early_flight[6])
    joined_late = _rs_join(_reduce_late_add(dev_i, gpk_late, parts_late), core, "rs_join_late")
    for n, _, _ in BIG:
        if PACK_AT[n][0] == "late":
            out_g[n], out_d[n], out_m[n], out_v[n] = _adamw(given[n], joined_late, given["m_" + n], given["v_" + n], n)

    part, parts, small_parts = _rs_chips_wait(*early_flight[:6], [out_d[n] for n in out_d])
    joined_early = _rs_join(_rs_add_chips(chip_i, part, parts), core, "rs_join_early")
    for n, _, _ in BIG:
        if PACK_AT[n][0] == "early":
            out_g[n], out_d[n], out_m[n], out_v[n] = _adamw(given[n], joined_early, given["m_" + n], given["v_" + n], n)

    mine = (lax.broadcasted_iota(jnp.int32, (N_DEV, 1, 1), 0) == dev_i[0])
    g_small_pk, d_small_pk, m_small_pk, v_small_pk = _adamw_small(
        _pack_small(small_w, 0.0), jnp.where(mine, small_own[None], small_parts), _pack_small(small_m, 0.0),
        _pack_small(small_v, 1.0))
    loss = g_small_pk[6, 0]
    for out, pk in ((out_g, g_small_pk), (out_d, d_small_pk), (out_m, m_small_pk), (out_v, v_small_pk)):
        out.update(_unpack_small(pk))
    order = ["g_mix_pre", "w_in", "g_q_a", "w_q_b", "g_kv_a", "w_kv_b", "sinks", "w_mla_up", "w_swa_up", "w_out", "g_mix_post",
             "g_mlp_pre", "w_mlp_up", "w_mlp_down", "g_mlp_post", "w_ple", "g_ple", "w_ple_gate"]
    return (loss, gx[None], *[out_g[n] for n in order], *[out_d[n] for n in order], *[out_m[n] for n in order],
            *[out_v[n] for n in order])
```

```python
import math

import jax
import jax.numpy as jnp
from jax import lax
from jax.experimental import pallas as pl
from jax.experimental.pallas import tpu as pltpu

F32 = jnp.float32
BF16 = jnp.bfloat16
SDS = jax.ShapeDtypeStruct

D = 1024
D_FF = 4096
PLE = 256
Q_LORA = 256
KV_LORA = 128
MLA_HEADS = 16
MLA_NOPE = 64
MLA_ROPE = 32
SWA_HEADS = 16
SWA_HD = 64
WINDOW = 128
ROPE_THETA = 10000.0
EPS = 1e-6
NEG = -1e30
NZ = 4096
MLA_SCALE = (MLA_NOPE + MLA_ROPE) ** -0.5
LOG2_E = math.log2(math.e)
MLA_LOG2_SCALE = MLA_SCALE * LOG2_E
SWA_SCALE = SWA_HD ** -0.5

ADAM_LR = 0.001
ADAM_B1 = 0.9
ADAM_B2 = 0.999
ADAM_EPS = 1e-08
ADAM_WD = 0.01
ADAM_STEP = 10

LANES = 128
ATT_COLS = 128
N_CHIPS = 4
N_DEV = 8
MESH = pl.DeviceIdType.MESH

NT = (((1,), (1,)), ((), ()))
TN = (((0,), (0,)), ((), ()))

BIG = (("w_in", 1024, 936), ("w_q_b", 256, 384), ("w_kv_b", 128, 512), ("w_mla_up", 256, 1024),
       ("w_swa_up", 256, 1024), ("w_out", 256, 1024), ("w_mlp_up", 1024, 1024), ("w_mlp_down", 1024, 1024),
       ("w_ple", 256, 256), ("w_ple_gate", 256, 1024))
COL_SHARDED = ("w_in", "w_q_b", "w_kv_b", "w_mlp_up", "w_ple")
PACK_AT = {"w_in": ("early", 0, 0), "w_q_b": ("early", 1024, 0), "w_ple": ("early", 1024, 384), "w_kv_b": ("early", 1280, 0),
           "w_mla_up": ("late", 0, 0), "w_swa_up": ("late", 256, 0), "w_out": ("late", 512, 0), "w_ple_gate": ("late", 768, 0),
           "w_mlp_up": ("late", 1024, 0), "w_mlp_down": ("late", 2048, 0)}
PACK_ROWS = {"early": 1408, "late": 3072}
SMALL = (("g_mix_pre", 1024), ("g_q_a", 256), ("g_kv_a", 128), ("sinks", 16), ("g_mix_post", 1024),
         ("g_mlp_pre", 1024), ("g_mlp_post", 1024), ("g_ple", 1024))


def _dot(a, b):
    return jnp.dot(a, b, preferred_element_type=F32)


def _dot_nt(a, b):
    return lax.dot_general(a, b, NT, preferred_element_type=F32)


def _dot_tn(a, b):
    return lax.dot_general(a, b, TN, preferred_element_type=F32)


def _pcall(body, *, name, out_shape, grid=(), in_specs=None, out_specs=None, scratch=(), sem=None, vmem_mb=48, aliases=None):
    params = dict(vmem_limit_bytes=vmem_mb << 20)
    if sem is not None:
        params["dimension_semantics"] = sem
    return pl.pallas_call(body, name=name, grid=grid, in_specs=in_specs, out_specs=out_specs, out_shape=out_shape,
                          scratch_shapes=list(scratch), input_output_aliases=aliases or {},
                          compiler_params=pltpu.CompilerParams(**params))


def _rows(tm, n, col=0):
    return pl.BlockSpec((tm, n), lambda i: (i, col))


def _full(shape):
    return pl.BlockSpec(shape, lambda i: (0,) * len(shape))


def _rms(x, g):
    r = lax.rsqrt(jnp.mean(x * x, axis=-1, keepdims=True) + EPS)
    return x * r * g


def _rms_bwd(dy, x, g):
    r = lax.rsqrt(jnp.mean(x * x, axis=-1, keepdims=True) + EPS)
    xn = x * r
    dn = dy * g
    dx = r * (dn - xn * jnp.mean(dn * xn, axis=-1, keepdims=True))
    return dx, jnp.sum(dy * xn, axis=0, keepdims=True)


def _sigmoid(x):
    return 1.0 / (1.0 + jnp.exp(-x))


def _rope(x, c, a, b, half):
    return x * c + pltpu.roll(x, LANES - half, 1) * a + pltpu.roll(x, half, 1) * b


def _rope_tables(T, kind):
    lane = jnp.arange(LANES)
    if kind == "mla":
        half = MLA_ROPE // 2
        rel = lane - MLA_NOPE
        on = (rel >= 0) & (rel < MLA_ROPE)
        d = MLA_ROPE
    else:
        half = SWA_HD // 2
        rel = lane % SWA_HD
        on = jnp.ones((LANES,), bool)
        d = SWA_HD
    first = on & (rel < half)
    second = on & (rel >= half)
    f = jnp.where(first, rel, rel - half).astype(F32)
    inv = jnp.exp(-math.log(ROPE_THETA) * f * (2.0 / d))
    ang = jnp.arange(T, dtype=F32)[:, None] * inv[None, :]
    cos, sin = jnp.cos(ang), jnp.sin(ang)
    c = jnp.where(on[None], cos, 1.0)
    a = jnp.where(first[None], -sin, 0.0)
    b = jnp.where(second[None], sin, 0.0)
    return c, a, b


def _fwd_in(x, g1, w_in_p, tm):
    T = x.shape[0]

    def body(x_ref, g_ref, w_ref, z_ref, h_ref):
        h = _rms(x_ref[...], g_ref[...]).astype(BF16)
        h_ref[...] = h
        z_ref[...] = _dot(h, w_ref[...])

    return _pcall(body, name="fwd_in", grid=(T // tm,),
                  in_specs=[_rows(tm, D), _full((1, D)), _full((D, NZ))],
                  out_specs=[_rows(tm, NZ), _rows(tm, D)],
                  out_shape=[SDS((T, NZ), F32), SDS((T, D), BF16)], sem=("parallel",))(x, g1, w_in_p)


def _fwd_qkv(z, gq, gkv, wqb, wkn, wv, tab_m, tab_s, tm):
    T = z.shape[0]
    wqb_t, wkn_t, wv_t = wqb.T, wkn.T, wv.T
    tab_mt = [t.T for t in tab_m]

    def body(qa_ref, sq_ref, skd_ref, svd_ref, kva_ref, kr_ref, gq_ref, gkv_ref, wkn_ref, wv_ref, wqbt_ref, wknt_ref, wvt_ref,
             cm_ref, am_ref, bm_ref, cmt_ref, amt_ref, bmt_ref, cs_ref, as_ref, bs_ref,
             qn_ref, kvn_ref, km_ref, vm_ref, qt_ref, kt_ref, vt_ref, qs_ref, ks_ref, vs_ref):
        qn = _rms(qa_ref[...], gq_ref[...])
        qn_ref[...] = qn.astype(BF16)
        kvn = _rms(kva_ref[...], gkv_ref[...])
        kvn_b = kvn.astype(BF16)
        kvn_ref[...] = kvn_b
        qn_t = qn.T.astype(BF16)
        kvn_t = kvn.T.astype(BF16)
        cm, am, bm = cm_ref[...], am_ref[...], bm_ref[...]
        cmt, amt, bmt = cmt_ref[...], amt_ref[...], bmt_ref[...]
        cs, as_, bs = cs_ref[...], as_ref[...], bs_ref[...]
        k_rope = _rope(kr_ref[...], cm, am, bm, MLA_ROPE // 2)
        k_rope_t = k_rope.T
        half = MLA_ROPE // 2
        vm_ref[...] = _dot(kvn_b, wv_ref[...]).astype(BF16)
        km_all = _dot(kvn_b, wkn_ref[...])
        v_t = _dot(wvt_ref[...], kvn_t)
        q_t = _dot(wqbt_ref[...], qn_t)
        k_t = _dot(wknt_ref[...], kvn_t)
        ones_row = jnp.where(lax.broadcasted_iota(jnp.int32, (64, tm), 0) == 0, 1.0, 0.0)
        for h in range(MLA_HEADS):
            sl = slice(LANES * h, LANES * (h + 1))
            vt_ref[0, sl, :] = jnp.concatenate([v_t[64 * h:64 * (h + 1)], ones_row], axis=0).astype(BF16)
            qh = q_t[sl]
            qt_ref[0, sl, :] = (qh * cmt + pltpu.roll(qh, LANES - half, 0) * amt + pltpu.roll(qh, half, 0) * bmt).astype(BF16)
            km_ref[:, sl] = (km_all[:, sl] + k_rope).astype(BF16)
            kt_ref[0, sl, :] = (k_t[sl] + k_rope_t).astype(BF16)
        for j in range(D // LANES):
            sl = slice(LANES * j, LANES * (j + 1))
            qs_ref[:, sl] = _rope(sq_ref[:, sl], cs, as_, bs, SWA_HD // 2).astype(BF16)
        for j in range(2):
            sl = slice(LANES * j, LANES * (j + 1))
            ks_ref[:, sl] = _rope(skd_ref[:, sl], cs, as_, bs, SWA_HD // 2).astype(BF16)
        vs_ref[...] = svd_ref[...].astype(BF16)

    tab = [_rows(tm, LANES)] * 3
    tab_t = [pl.BlockSpec((LANES, tm), lambda i: (0, i))] * 3
    return _pcall(body, name="fwd_qkv", grid=(T // tm,),
                  in_specs=[_rows(tm, 256, 12), _rows(tm, 1024, 0), _rows(tm, 256, 13), _rows(tm, 256, 14),
                            _rows(tm, 128, 30), _rows(tm, 128, 31), _full((1, Q_LORA)), _full((1, KV_LORA)),
                            _full((KV_LORA, 2048)), _full((KV_LORA, 1024)), _full((2048, Q_LORA)), _full((2048, KV_LORA)),
                            _full((1024, KV_LORA))] + tab + tab_t + tab,
                  out_specs=[_rows(tm, Q_LORA), _rows(tm, KV_LORA), _rows(tm, 2048), _rows(tm, 1024),
                             pl.BlockSpec((1, 2048, tm), lambda i: (i, 0, 0)), pl.BlockSpec((1, 2048, tm), lambda i: (i, 0, 0)),
                             pl.BlockSpec((1, 2048, tm), lambda i: (i, 0, 0)),
                             _rows(tm, 1024), _rows(tm, 256), _rows(tm, 256)],
                  out_shape=[SDS((T, Q_LORA), BF16), SDS((T, KV_LORA), BF16), SDS((T, 2048), BF16),
                             SDS((T, 1024), BF16), SDS((T // tm, 2048, tm), BF16), SDS((T // tm, 2048, tm), BF16),
                             SDS((T // tm, 2048, tm), BF16),
                             SDS((T, 1024), BF16), SDS((T, 256), BF16), SDS((T, 256), BF16)],
                  sem=("parallel",))(z, z, z, z, z, z, gq, gkv, wkn, wv, wqb_t, wkn_t, wv_t, *tab_m, *tab_mt, *tab_s)


def _mla_fwd(qt, km, vt, tb):
    T = km.shape[0]
    nb = T // tb
    cc = ATT_COLS

    def body(q_ref, k_ref, vt_ref, o_ref, l_ref, s_ref, p_ref, al_ref, m_ref, acc_ref):
        i = pl.program_id(1)
        m_ref[...] = jnp.full(m_ref.shape, NEG, F32)
        acc_ref[...] = jnp.zeros_like(acc_ref)
        p_ref[1] = jnp.zeros(p_ref.shape[1:], BF16)
        al_ref[1] = jnp.ones(al_ref.shape[1:], F32)
        key = lax.broadcasted_iota(jnp.int32, (tb, cc), 0)
        qry = lax.broadcasted_iota(jnp.int32, (tb, cc), 1)

        def scores(j, slot):
            off = pl.multiple_of(j * tb, tb)
            for hh in range(2):
                sl = slice(LANES * hh, LANES * (hh + 1))
                s_ref[slot, hh] = _dot(k_ref[pl.ds(off, tb), sl], q_ref[0, sl, :])

        def softmax(slot, diagonal):
            chains = [(hh, slice(cc * c, cc * (c + 1)), c) for hh in range(2) for c in range(tb // cc)]

            def scaled(hh, cols, c):
                t = s_ref[slot, hh, :, cols] * MLA_LOG2_SCALE
                return jnp.where(key <= qry + cc * c, t, NEG) if diagonal else t

            tops = []
            for hh, cols, c in chains:
                if diagonal:
                    top = jnp.max(scaled(hh, cols, c), axis=0, keepdims=True)
                else:
                    top = jnp.max(s_ref[slot, hh, :, cols], axis=0, keepdims=True) * MLA_LOG2_SCALE
                m_old = m_ref[hh, :, cols]
                mn = jnp.maximum(m_old, top)
                m_ref[hh, :, cols] = mn
                al_ref[slot, hh, :, cols] = jnp.exp2(m_old - mn)
                tops.append(mn)
            for (hh, cols, c), mn in zip(chains, tops):
                p_ref[slot, hh, :, cols] = jnp.exp2(scaled(hh, cols, c) - mn).astype(BF16)

        def accumulate(j, slot):
            for hh in range(2):
                acc_ref[hh] = al_ref[slot, hh] * acc_ref[hh] + _dot(vt_ref[j, LANES * hh:LANES * (hh + 1), :], p_ref[slot, hh])

        def step(t, carry):
            scores(2 * t + 1, 1)
            accumulate(jnp.maximum(2 * t - 1, 0), 1)
            softmax(0, False)
            scores(2 * t + 2, 0)
            accumulate(2 * t, 0)
            softmax(1, False)
            return carry

        scores(0, 0)
        lax.fori_loop(0, i // 2, step, 0)

        @pl.when(i % 2 == 1)
        def _():
            scores(i, 1)
            accumulate(jnp.maximum(i - 2, 0), 1)
            softmax(0, False)
            accumulate(i - 1, 0)
            softmax(1, True)
            accumulate(i, 1)

        @pl.when(i % 2 == 0)
        def _():
            accumulate(jnp.maximum(i - 1, 0), 1)
            softmax(0, True)
            accumulate(i, 0)
        den = [acc_ref[hh, 64:65, :] for hh in range(2)]
        o_ref[...] = jnp.concatenate([acc_ref[hh, 0:64, :] / den[hh] for hh in range(2)], axis=0).T
        sub = lax.broadcasted_iota(jnp.int32, (8, tb), 0)
        lse = [m_ref[hh] + jnp.log(den[hh]) * LOG2_E for hh in range(2)]
        l_ref[0, 0] = jnp.where(sub == 0, lse[0], jnp.where(sub == 1, lse[1], 0.0))

    return _pcall(body, name="mla_fwd", grid=(MLA_HEADS // 2, nb),
                  in_specs=[pl.BlockSpec((1, 256, tb), lambda p, i: (i, p, 0)), pl.BlockSpec((T, 256), lambda p, i: (0, p)),
                            pl.BlockSpec((nb, 2 * LANES, tb), lambda p, i: (0, p, 0))],
                  out_specs=[pl.BlockSpec((tb, LANES), lambda p, i: (i, p)),
                             pl.BlockSpec((1, 1, 8, tb), lambda p, i: (p, i, 0, 0))],
                  out_shape=[SDS((T, D), F32), SDS((MLA_HEADS // 2, nb, 8, tb), F32)],
                  scratch=[pltpu.VMEM((2, 2, tb, tb), F32), pltpu.VMEM((2, 2, tb, tb), BF16), pltpu.VMEM((2, 2, 1, tb), F32),
                           pltpu.VMEM((2, 1, tb), F32), pltpu.VMEM((2, LANES, tb), F32)],
                  sem=("parallel", "arbitrary"))(qt, km, vt)


def _swa_mask(n):
    row = lax.broadcasted_iota(jnp.int32, (WINDOW, 2 * WINDOW), 0)
    col = lax.broadcasted_iota(jnp.int32, (WINDOW, 2 * WINDOW), 1)
    rel = row - col + WINDOW
    return (rel >= 0) & (rel < WINDOW) & ((col >= WINDOW) | (n > 0))


def _swa_specs(T):
    nb = T // WINDOW
    cur = lambda w: pl.BlockSpec((WINDOW, w), lambda n: (n, 0))
    prev = lambda w: pl.BlockSpec((WINDOW, w), lambda n: (jnp.maximum(n - 1, 0), 0))
    return nb, cur, prev


def _swa_fwd(sinks, qs, ks, vs):
    T = qs.shape[0]
    nb, cur, prev = _swa_specs(T)

    def body(sink_ref, q_ref, kc_ref, kp_ref, vc_ref, vp_ref, o_ref, l_ref, kb_ref, vb_ref, s_ref, p_ref):
        n = pl.program_id(0)
        mask = _swa_mask(n)
        lo = lax.broadcasted_iota(jnp.int32, (WINDOW, LANES), 1) < 64
        hi = jnp.logical_not(lo)
        for g in range(2):
            gs = slice(LANES * g, LANES * (g + 1))
            kb_ref[g] = jnp.concatenate([kp_ref[:, gs], kc_ref[:, gs]], axis=0)
            vb_ref[g] = jnp.concatenate([vp_ref[:, gs], vc_ref[:, gs]], axis=0)
        for h in range(SWA_HEADS):
            qp = q_ref[:, LANES * (h // 2):LANES * (h // 2 + 1)]
            qh = jnp.where(lo if h % 2 == 0 else hi, qp, jnp.zeros_like(qp))
            s_ref[h] = _dot_nt(qh, kb_ref[h // 8])
        for j in range(SWA_HEADS // 2):
            sl = slice(LANES * j, LANES * (j + 1))
            lses = []
            for h in (2 * j, 2 * j + 1):
                s = jnp.where(mask, s_ref[h] * SWA_SCALE, NEG)
                sk = sink_ref[h]
                m = jnp.maximum(jnp.max(s, axis=1, keepdims=True), sk)
                e = jnp.exp(s - m)
                den = jnp.sum(e, axis=1, keepdims=True) + jnp.exp(sk - m)
                p_ref[h] = (e / den).astype(BF16)
                lses.append(jnp.broadcast_to(m + jnp.log(den), (WINDOW, LANES)))
            l_ref[:, sl] = jnp.where(lo, lses[0], lses[1])
        for j in range(SWA_HEADS // 2):
            vb = vb_ref[j // 4]
            o_ref[:, LANES * j:LANES * (j + 1)] = jnp.where(lo, _dot(p_ref[2 * j], vb), _dot(p_ref[2 * j + 1], vb))

    return _pcall(body, name="swa_fwd", grid=(nb,),
                  in_specs=[pl.BlockSpec(memory_space=pltpu.SMEM), cur(D), cur(256), prev(256), cur(256), prev(256)],
                  out_specs=[cur(D), cur(D)], out_shape=[SDS((T, D), F32)] * 2,
                  scratch=[pltpu.VMEM((2, 2 * WINDOW, LANES), BF16), pltpu.VMEM((2, 2 * WINDOW, LANES), BF16),
                           pltpu.VMEM((SWA_HEADS, WINDOW, 2 * WINDOW), F32), pltpu.VMEM((SWA_HEADS, WINDOW, 2 * WINDOW), BF16)],
                  sem=("parallel",))(sinks, qs, ks, ks, vs, vs)


def _fwd_mix(om, os_, z, x, wmu, wsu, wo, g2, tm):
    T = x.shape[0]

    def body(om_ref, os_ref, ga_ref, gb_ref, x_ref, wmu_ref, wsu_ref, wo_ref, g2_ref,
             y_ref, yo_ref, au_ref, bu_ref, x1_ref):
        au = _dot(om_ref[...].astype(BF16), wmu_ref[...])
        bu = _dot(os_ref[...].astype(BF16), wsu_ref[...])
        au_ref[...] = au
        bu_ref[...] = bu
        y = (_sigmoid(ga_ref[...]) * au + _sigmoid(gb_ref[...]) * bu).astype(BF16)
        y_ref[...] = y
        yo = _dot(y, wo_ref[...])
        yo_ref[...] = yo
        x1_ref[...] = x_ref[...] + _rms(yo, g2_ref[...])

    r = _rows(tm, D)
    w = _full((D, D))
    return _pcall(body, name="fwd_mix", grid=(T // tm,),
                  in_specs=[r, r, _rows(tm, D, 1), _rows(tm, D, 2), r, w, w, w, _full((1, D))],
                  out_specs=[r] * 5,
                  out_shape=[SDS((T, D), BF16), SDS((T, D), F32), SDS((T, D), F32), SDS((T, D), F32), SDS((T, D), F32)],
                  sem=("parallel",))(om, os_, z, z, x, wmu, wsu, wo, g2)


def _fwd_mlp_up(x1, g3, w1, tm):
    T = x1.shape[0]

    def body(x_ref, g_ref, w_ref, h_ref, u_ref):
        h = _rms(x_ref[...], g_ref[...]).astype(BF16)
        h_ref[...] = h
        u_ref[...] = jnp.square(jnp.maximum(_dot(h, w_ref[...]), 0.0)).astype(BF16)

    return _pcall(body, name="fwd_mlp_up", grid=(T // tm,),
                  in_specs=[_rows(tm, D), _full((1, D)), _full((D, D_FF))],
                  out_specs=[_rows(tm, D), _rows(tm, D_FF)],
                  out_shape=[SDS((T, D), BF16), SDS((T, D_FF), BF16)],
                  sem=("parallel",))(x1, g3, w1)


def _fwd_mlp_down(u, w2, x1, g4, tm):
    T = x1.shape[0]

    def body(u_ref, w_ref, x_ref, g_ref, d_ref, x2_ref):
        d = _dot(u_ref[...], w_ref[...])
        d_ref[...] = d
        x2_ref[...] = x_ref[...] + _rms(d, g_ref[...])

    return _pcall(body, name="fwd_mlp_down", grid=(T // tm,),
                  in_specs=[_rows(tm, D_FF), _full((D_FF, D)), _rows(tm, D), _full((1, D))],
                  out_specs=[_rows(tm, D), _rows(tm, D)], out_shape=[SDS((T, D), F32)] * 2,
                  sem=("parallel",))(u, w2, x1, g4)


def _ple_fwd_bwd(p, x2, tgt, wple, g5, wpg, tm):
    T = x2.shape[0]

    def body(p_ref, x2_ref, t_ref, wple_ref, g5_ref, wpg_ref, loss_ref, dx2_ref, dgt_ref, de0_ref, dg5_ref):
        @pl.when(pl.program_id(0) == 0)
        def _():
            loss_ref[...] = jnp.zeros_like(loss_ref)
            dg5_ref[...] = jnp.zeros_like(dg5_ref)

        e0 = _dot(p_ref[...].astype(BF16), wple_ref[...])
        g5 = g5_ref[...]
        r = lax.rsqrt(jnp.mean(e0 * e0, axis=-1, keepdims=True) + EPS)
        en = e0 * r
        e = en * g5
        x2 = x2_ref[...]
        s = _sigmoid(_dot(x2.astype(BF16), wpg_ref[...]))
        diff = x2 + s * e - t_ref[...]
        sq = jnp.sum(jnp.sum(diff * diff, axis=1, keepdims=True), axis=0, keepdims=True)
        loss_ref[...] += jnp.broadcast_to(sq * (0.5 / D), loss_ref.shape)
        dx3 = diff * (1.0 / D)
        de = dx3 * s
        dgt = (dx3 * e * s * (1.0 - s)).astype(BF16)
        dgt_ref[...] = dgt
        dn = de * g5
        de0_ref[...] = (r * (dn - en * jnp.mean(dn * en, axis=-1, keepdims=True))).astype(BF16)
        dg5_ref[...] += jnp.sum(de * en, axis=0, keepdims=True)
        dx2_ref[...] = dx3 + _dot_nt(dgt, wpg_ref[...])

    r = _rows(tm, D)
    return _pcall(body, name="ple_fwd_bwd", grid=(T // tm,),
                  in_specs=[_rows(tm, PLE), r, r, _full((PLE, D)), _full((1, D)), _full((D, D))],
                  out_specs=[_full((8, LANES)), r, r, r, _full((1, D))],
                  out_shape=[SDS((8, LANES), F32), SDS((T, D), F32), SDS((T, D), BF16), SDS((T, D), BF16), SDS((1, D), F32)],
                  sem=("arbitrary",))(p, x2, tgt, wple, g5, wpg)


def _bwd_mlp_down(dx2, d, g4, w2, u, tm):
    T = dx2.shape[0]

    def body(dx_ref, d_ref, g_ref, w_ref, u_ref, dd_ref, da_ref, dg_ref):
        @pl.when(pl.program_id(0) == 0)
        def _():
            dg_ref[...] = jnp.zeros_like(dg_ref)

        dd, dg = _rms_bwd(dx_ref[...], d_ref[...], g_ref[...])
        dg_ref[...] += dg
        ddb = dd.astype(BF16)
        dd_ref[...] = ddb
        du = _dot_nt(ddb, w_ref[...])
        da_ref[...] = (du * (2.0 * jnp.sqrt(u_ref[...].astype(F32)))).astype(BF16)

    return _pcall(body, name="bwd_mlp_down", grid=(T // tm,),
                  in_specs=[_rows(tm, D), _rows(tm, D), _full((1, D)), _full((D_FF, D)), _rows(tm, D_FF)],
                  out_specs=[_rows(tm, D), _rows(tm, D_FF), _full((1, D))],
                  out_shape=[SDS((T, D), BF16), SDS((T, D_FF), BF16), SDS((1, D), F32)],
                  sem=("arbitrary",))(dx2, d, g4, w2, u)


def _bwd_mlp_up(da, w1, x1, g3, dx2, tm):
    T = dx2.shape[0]

    def body(da_ref, w_ref, x_ref, g_ref, dx2_ref, dx1_ref, dg_ref):
        @pl.when(pl.program_id(0) == 0)
        def _():
            dg_ref[...] = jnp.zeros_like(dg_ref)

        dh = _dot_nt(da_ref[...], w_ref[...])
        dx, dg = _rms_bwd(dh, x_ref[...], g_ref[...])
        dg_ref[...] += dg
        dx1_ref[...] = dx2_ref[...] + dx

    return _pcall(body, name="bwd_mlp_up", grid=(T // tm,),
                  in_specs=[_rows(tm, D_FF), _full((D, D_FF)), _rows(tm, D), _full((1, D)), _rows(tm, D)],
                  out_specs=[_rows(tm, D), _full((1, D))],
                  out_shape=[SDS((T, D), F32), SDS((1, D), F32)], sem=("arbitrary",))(da, w1, x1, g3, dx2)


def _bwd_mix(dx1, yo, g2, wo, z, au, bu, wmu, wsu, om, tm):
    T = dx1.shape[0]

    def body(dx_ref, yo_ref, g_ref, wo_ref, ga_ref, gb_ref, au_ref, bu_ref, wmu_ref, wsu_ref, om_ref,
             dyo_ref, dg_ref, dau_ref, dbu_ref, dga_ref, dgb_ref, dos_ref, dl_ref, dot_ref):
        @pl.when(pl.program_id(0) == 0)
        def _():
            dg_ref[...] = jnp.zeros_like(dg_ref)

        dyo, dg = _rms_bwd(dx_ref[...], yo_ref[...], g_ref[...])
        dg_ref[...] += dg
        dyob = dyo.astype(BF16)
        dyo_ref[...] = dyob
        dy = _dot_nt(dyob, wo_ref[...])
        sa = _sigmoid(ga_ref[...])
        sb = _sigmoid(gb_ref[...])
        dau = (dy * sa).astype(BF16)
        dbu = (dy * sb).astype(BF16)
        dau_ref[...] = dau
        dbu_ref[...] = dbu
        dga_ref[...] = (dy * au_ref[...] * sa * (1.0 - sa)).astype(BF16)
        dgb_ref[...] = (dy * bu_ref[...] * sb * (1.0 - sb)).astype(BF16)
        dom = _dot_nt(dau, wmu_ref[...])
        dos_ref[...] = _dot_nt(dbu, wsu_ref[...])
        prod = dom * om_ref[...]
        sub = lax.broadcasted_iota(jnp.int32, (8, tm), 0)
        for pr in range(MLA_HEADS // 2):
            sl = slice(LANES * pr, LANES * (pr + 1))
            pt = prod[:, sl].T
            d0 = jnp.sum(pt[0:64], axis=0, keepdims=True)
            d1 = jnp.sum(pt[64:128], axis=0, keepdims=True)
            dl_ref[pr, 0] = jnp.where(sub == 0, d0, jnp.where(sub == 1, d1, 0.0))
            dot_ref[0, sl, :] = dom[:, sl].T.astype(BF16)

    r = _rows(tm, D)
    w = _full((D, D))
    return _pcall(body, name="bwd_mix", grid=(T // tm,),
                  in_specs=[r, r, _full((1, D)), w, _rows(tm, D, 1), _rows(tm, D, 2), r, r, w, w, r],
                  out_specs=[r, _full((1, D)), r, r, r, r, r, pl.BlockSpec((MLA_HEADS // 2, 1, 8, tm), lambda i: (0, i, 0, 0)),
                             pl.BlockSpec((1, D, tm), lambda i: (i, 0, 0))],
                  out_shape=[SDS((T, D), BF16), SDS((1, D), F32), SDS((T, D), BF16), SDS((T, D), BF16), SDS((T, D), BF16),
                             SDS((T, D), BF16), SDS((T, D), F32), SDS((MLA_HEADS // 2, T // tm, 8, tm), F32),
                             SDS((T // tm, D, tm), BF16)],
                  sem=("arbitrary",))(dx1, yo, g2, wo, z, z, au, bu, wmu, wsu, om)


def _mla_bwd(qt, km, kt, vm, dot, lse, delta, tb):
    T = km.shape[0]
    nb = T // tb
    cc = ATT_COLS

    def body(qt_ref, k_ref, kt_ref, v_ref, dot_ref, l_ref, dl_ref, dqt_ref, dkt_ref, dvt_ref,
             s_ref, dp_ref, p_ref, ds_ref, vh_ref):
        j = pl.program_id(1)

        @pl.when(j == 0)
        def _():
            dqt_ref[...] = jnp.zeros_like(dqt_ref)

        dkt_ref[...] = jnp.zeros_like(dkt_ref)
        dvt_ref[...] = jnp.zeros_like(dvt_ref)
        lo = lax.broadcasted_iota(jnp.int32, (tb, LANES), 1) < 64
        key = lax.broadcasted_iota(jnp.int32, (tb, cc), 0)
        qry = lax.broadcasted_iota(jnp.int32, (tb, cc), 1)
        v = v_ref[...]
        vh_ref[0] = jnp.where(lo, v, jnp.zeros_like(v))
        vh_ref[1] = jnp.where(lo, jnp.zeros_like(v), v)

        def scores(i, slot):
            for hh in range(2):
                sl = slice(LANES * hh, LANES * (hh + 1))
                s_ref[slot, hh] = _dot(k_ref[:, sl], qt_ref[i, sl, :])
                dp_ref[slot, hh] = _dot(vh_ref[hh], dot_ref[i])

        def grads(i, slot, diagonal):
            lse_i = l_ref[0, i]
            delta_i = dl_ref[0, i]
            for hh in range(2):
                for c in range(tb // cc):
                    cols = slice(cc * c, cc * (c + 1))
                    p = jnp.exp2(s_ref[slot, hh, :, cols] * MLA_LOG2_SCALE - lse_i[hh:hh + 1, cols])
                    if diagonal:
                        p = jnp.where(key <= qry + cc * c, p, 0.0)
                    p_ref[hh, :, cols] = p.astype(BF16)
                    ds_ref[hh, :, cols] = (p * (dp_ref[slot, hh, :, cols] - delta_i[hh:hh + 1, cols]) * MLA_SCALE).astype(BF16)
            for hh in range(2):
                sl = slice(LANES * hh, LANES * (hh + 1))
                half = slice(64 * hh, 64 * (hh + 1))
                dvt_ref[0, half, :] += _dot_nt(dot_ref[i, half, :], p_ref[hh])
                dkt_ref[0, sl, :] += _dot_nt(qt_ref[i, sl, :], ds_ref[hh])
                dqt_ref[i, sl, :] += _dot(kt_ref[0, sl, :], ds_ref[hh])

        n_off = nb - 1 - j

        def step(u, carry):
            i0 = j + 1 + 2 * u
            scores(i0 + 1, 1)
            grads(i0, 0, False)
            scores(jnp.where(i0 + 2 < nb, i0 + 2, j), 0)
            grads(i0 + 1, 1, False)
            return carry

        scores(jnp.where(n_off > 0, j + 1, j), 0)
        lax.fori_loop(0, n_off // 2, step, 0)

        @pl.when(n_off % 2 == 1)
        def _():
            scores(j, 1)
            grads(nb - 1, 0, False)
            grads(j, 1, True)

        @pl.when(n_off % 2 == 0)
        def _():
            grads(j, 0, True)

    blk = lambda w: pl.BlockSpec((tb, w), lambda p, j: (j, p))
    stat = pl.BlockSpec((1, nb, 8, tb), lambda p, j: (p, 0, 0, 0))
    pair_t = lambda w: pl.BlockSpec((nb, w, tb), lambda p, j: (0, p, 0))
    blk_t = lambda w: pl.BlockSpec((1, w, tb), lambda p, j: (j, p, 0))
    return _pcall(body, name="mla_bwd", grid=(MLA_HEADS // 2, nb),
                  in_specs=[pair_t(256), blk(256), blk_t(256), blk(LANES), pair_t(LANES), stat, stat],
                  out_specs=[pair_t(256), blk_t(256), blk_t(LANES)],
                  out_shape=[SDS((nb, 2048, tb), F32), SDS((nb, 2048, tb), F32), SDS((nb, D, tb), F32)],
                  scratch=[pltpu.VMEM((2, 2, tb, tb), F32), pltpu.VMEM((2, 2, tb, tb), F32), pltpu.VMEM((2, tb, tb), BF16),
                           pltpu.VMEM((2, tb, tb), BF16), pltpu.VMEM((2, tb, LANES), BF16)],
                  sem=("parallel", "arbitrary"))(qt, km, kt, vm, dot, lse, delta)


def _swa_bwd(sinks, qs, ks, vs, do, o, lse):
    T = qs.shape[0]
    nb, cur, prev = _swa_specs(T)

    def body(sink_ref, q_ref, kc_ref, kp_ref, vc_ref, vp_ref, do_ref, o_ref, l_ref,
             dq_ref, dkc_ref, dkp_ref, dvc_ref, dvp_ref, dsink_ref, kb_ref, vb_ref, s_ref, dp_ref, p_ref, ds_ref):
        n = pl.program_id(0)

        @pl.when(n == 0)
        def _():
            dsink_ref[...] = jnp.zeros_like(dsink_ref)

        mask = _swa_mask(n)
        lo = lax.broadcasted_iota(jnp.int32, (WINDOW, LANES), 1) < 64
        hi = jnp.logical_not(lo)
        lane8 = lax.broadcasted_iota(jnp.int32, (8, LANES), 1)
        for g in range(2):
            gs = slice(LANES * g, LANES * (g + 1))
            kb_ref[g] = jnp.concatenate([kp_ref[:, gs], kc_ref[:, gs]], axis=0)
            vb_ref[g] = jnp.concatenate([vp_ref[:, gs], vc_ref[:, gs]], axis=0)

        def head(h):
            sl = slice(LANES * (h // 2), LANES * (h // 2 + 1))
            hm = lo if h % 2 == 0 else hi
            qp = q_ref[:, sl]
            return hm, sl, jnp.where(hm, qp, jnp.zeros_like(qp)), jnp.where(hm, do_ref[:, sl], 0.0).astype(BF16)

        for h in range(SWA_HEADS):
            _, _, qh, dom = head(h)
            s_ref[h] = _dot_nt(qh, kb_ref[h // 8])
            dp_ref[h] = _dot_nt(dom, vb_ref[h // 8])
        dsink = jnp.zeros((8, LANES), F32)
        for h in range(SWA_HEADS):
            hm, sl, _, _ = head(h)
            lse_h = jnp.max(jnp.where(hm, l_ref[:, sl], -jnp.inf), axis=1, keepdims=True)
            delta = jnp.sum(jnp.where(hm, do_ref[:, sl] * o_ref[:, sl], 0.0), axis=1, keepdims=True)
            p = jnp.exp(jnp.where(mask, s_ref[h] * SWA_SCALE, NEG) - lse_h)
            p_ref[h] = p.astype(BF16)
            ds_ref[h] = (p * (dp_ref[h] - delta) * SWA_SCALE).astype(BF16)
            d_sink = -jnp.sum(jnp.exp(sink_ref[h] - lse_h) * delta, axis=0, keepdims=True)
            dsink = dsink + jnp.where(lane8 == h, d_sink, 0.0)
        dsink_ref[...] += dsink
        for g in range(2):
            gs = slice(LANES * g, LANES * (g + 1))
            dkb = jnp.zeros((2 * WINDOW, LANES), F32)
            dvb = jnp.zeros((2 * WINDOW, LANES), F32)
            for j in range(4 * g, 4 * g + 4):
                dqs = []
                for h in (2 * j, 2 * j + 1):
                    _, _, qh, dom = head(h)
                    dvb = dvb + _dot_tn(p_ref[h], dom)
                    dkb = dkb + _dot_tn(ds_ref[h], qh)
                    dqs.append(_dot(ds_ref[h], kb_ref[g]))
                dq_ref[:, LANES * j:LANES * (j + 1)] = jnp.where(lo, dqs[0], dqs[1])
            dkp_ref[:, gs] = dkb[:WINDOW]
            dkc_ref[:, gs] = dkb[WINDOW:]
            dvp_ref[:, gs] = dvb[:WINDOW]
            dvc_ref[:, gs] = dvb[WINDOW:]

    band = pltpu.VMEM((2, 2 * WINDOW, LANES), BF16)
    return _pcall(body, name="swa_bwd", grid=(nb,),
                  in_specs=[pl.BlockSpec(memory_space=pltpu.SMEM), cur(D), cur(256), prev(256), cur(256), prev(256),
                            cur(D), cur(D), cur(D)],
                  out_specs=[cur(D), cur(256), cur(256), cur(256), cur(256), _full((8, LANES))],
                  out_shape=[SDS((T, D), F32), SDS((T, 256), F32), SDS((T, 256), F32), SDS((T, 256), F32), SDS((T, 256), F32),
                             SDS((8, LANES), F32)],
                  scratch=[band, band, pltpu.VMEM((SWA_HEADS, WINDOW, 2 * WINDOW), F32),
                           pltpu.VMEM((SWA_HEADS, WINDOW, 2 * WINDOW), F32), pltpu.VMEM((SWA_HEADS, WINDOW, 2 * WINDOW), BF16),
                           pltpu.VMEM((SWA_HEADS, WINDOW, 2 * WINDOW), BF16)],
                  sem=("arbitrary",))(sinks, qs, ks, ks, vs, vs, do, o, lse)


def _bwd_qkv(dqm, dkm, dvm, dqs, dkc, dkp, dvc, dvp, z, gq, gkv, wqb, wkn, wv, tab_m, tab_s):
    T = z.shape[0]
    tm = WINDOW
    nb = T // tm
    per = dqm.shape[2] // tm

    tab_mt = [t.T for t in tab_m]
    half = MLA_ROPE // 2

    def rope_t(v, c, a, b):
        return v * c + pltpu.roll(v, LANES - half, 0) * a + pltpu.roll(v, half, 0) * b

    def rms_bwd_t(dy, x, g):
        r = lax.rsqrt(jnp.mean(x * x, axis=0, keepdims=True) + EPS)
        xn = x * r
        dn = dy * g
        return r * (dn - xn * jnp.mean(dn * xn, axis=0, keepdims=True)), jnp.sum(dy * xn, axis=1, keepdims=True)

    def body(dqm_ref, dkm_ref, dvm_ref, dqs_ref, dkc_ref, dkp_ref, dvc_ref, dvp_ref, qa_ref, kva_ref, gq_ref, gkv_ref,
             wqb_ref, wkn_ref, wv_ref, cmt_ref, amt_ref, bmt_ref, cs_ref, as_ref, bs_ref,
             dq_out, dkn_out, dv_out, dsq_ref, drest_ref, dgq_ref, dgkv_ref):
        i = pl.program_id(0)

        @pl.when(i == 0)
        def _():
            dgq_ref[...] = jnp.zeros_like(dgq_ref)
            dgkv_ref[...] = jnp.zeros_like(dgkv_ref)

        cmt, amt, bmt = cmt_ref[...], -amt_ref[...], -bmt_ref[...]
        cs, as_, bs = cs_ref[...], -as_ref[...], -bs_ref[...]
        row = lax.broadcasted_iota(jnp.int32, (LANES, tm), 0)
        nope = row < MLA_NOPE
        roped = jnp.logical_and(row >= MLA_NOPE, row < MLA_NOPE + MLA_ROPE)
        dkr = jnp.zeros((LANES, tm), F32)
        for h in range(MLA_HEADS):
            sl = slice(LANES * h, LANES * (h + 1))
            dq_out[0, sl, :] = rope_t(dqm_ref[0, sl, :], cmt, amt, bmt).astype(BF16)
            dk_h = dkm_ref[0, sl, :]
            dkn_out[0, sl, :] = jnp.where(nope, dk_h, 0.0).astype(BF16)
            dkr = dkr + jnp.where(roped, dk_h, 0.0)
        dv_out[0] = dvm_ref[0].astype(BF16)
        dqn = _dot(wqb_ref[...], dq_out[0])
        dkvn = _dot(wkn_ref[...], dkn_out[0]) + _dot(wv_ref[...], dv_out[0])
        dqa, dgq = rms_bwd_t(dqn, qa_ref[...].T, gq_ref[...])
        dkva, dgkv = rms_bwd_t(dkvn, kva_ref[...].T, gkv_ref[...])
        dgq_ref[...] += dgq
        dgkv_ref[...] += dgkv
        for j in range(D // LANES):
            sl = slice(LANES * j, LANES * (j + 1))
            dsq_ref[:, sl] = _rope(dqs_ref[:, sl], cs, as_, bs, SWA_HD // 2).astype(BF16)
        keep = (i < nb - 1).astype(F32)
        drest_ref[:, 0:256] = dqa.T.astype(BF16)
        for j in range(2):
            sl = slice(LANES * j, LANES * (j + 1))
            dk = dkc_ref[:, sl] + keep * dkp_ref[:, sl]
            drest_ref[:, 256 + LANES * j:256 + LANES * (j + 1)] = _rope(dk, cs, as_, bs, SWA_HD // 2).astype(BF16)
        drest_ref[:, 512:768] = (dvc_ref[...] + keep * dvp_ref[...]).astype(BF16)
        drest_ref[:, 768:896] = dkva.T.astype(BF16)
        drest_ref[:, 896:1024] = rope_t(dkr, cmt, amt, bmt).T.astype(BF16)

    nxt = pl.BlockSpec((tm, 256), lambda i: (jnp.minimum(i + 1, nb - 1), 0))
    tab = [_rows(tm, LANES)] * 3
    tab_t = [pl.BlockSpec((LANES, tm), lambda i: (0, i))] * 3
    blk_t = lambda w: pl.BlockSpec((1, w, tm), lambda i: (i // per, 0, i % per))
    return _pcall(body, name="bwd_qkv", grid=(nb,),
                  in_specs=[blk_t(2048), blk_t(2048), blk_t(1024), _rows(tm, 1024), _rows(tm, 256), nxt,
                            _rows(tm, 256), nxt, _rows(tm, 256, 12), _rows(tm, 128, 30), _full((Q_LORA, 1)), _full((KV_LORA, 1)),
                            _full((Q_LORA, 2048)), _full((KV_LORA, 2048)), _full((KV_LORA, 1024))] + tab_t + tab,
                  out_specs=[blk_t(2048), blk_t(2048), blk_t(1024), _rows(tm, 1024), _rows(tm, 1024),
                             _full((Q_LORA, 1)), _full((KV_LORA, 1))],
                  out_shape=[SDS(dqm.shape, BF16), SDS(dkm.shape, BF16), SDS(dvm.shape, BF16), SDS((T, 1024), BF16),
                             SDS((T, 1024), BF16), SDS((Q_LORA, 1), F32), SDS((KV_LORA, 1), F32)],
                  sem=("arbitrary",))(dqm, dkm, dvm, dqs, dkc, dkp, dvc, dvp, z, z, gq.reshape(Q_LORA, 1),
                                      gkv.reshape(KV_LORA, 1), wqb, wkn, wv, *tab_mt, *tab_s)


def _bwd_in(dsq, dga, dgb, drest, w_in_p, x, g1, dx1, tm):
    T = x.shape[0]

    def body(a_ref, b_ref, c_ref, d_ref, w_ref, x_ref, g_ref, dx1_ref, dx_ref, dg_ref):
        @pl.when(pl.program_id(0) == 0)
        def _():
            dg_ref[...] = jnp.zeros_like(dg_ref)

        dh = (_dot_nt(a_ref[...], w_ref[:, 0:1024]) + _dot_nt(b_ref[...], w_ref[:, 1024:2048])
              + _dot_nt(c_ref[...], w_ref[:, 2048:3072]) + _dot_nt(d_ref[...], w_ref[:, 3072:4096]))
        dx, dg = _rms_bwd(dh, x_ref[...], g_ref[...])
        dg_ref[...] += dg
        dx_ref[...] = dx1_ref[...] + dx

    r = _rows(tm, D)
    return _pcall(body, name="bwd_in", grid=(T // tm,),
                  in_specs=[r, r, r, r, _full((D, NZ)), r, _full((1, D)), r],
                  out_specs=[r, _full((1, D))], out_shape=[SDS((T, D), F32), SDS((1, D), F32)],
                  sem=("arbitrary",))(dsq, dga, dgb, drest, w_in_p, x, g1, dx1)


def _wgrad(a, g, name, into=None):
    T, K = a.shape
    N = g.shape[1]
    tk, tn, tt = min(K, 1024), min(N, 1024), min(T, 1024)
    if into is not None:
        buf, weight = into
        _, row0, lane0 = PACK_AT[weight]
        shard = {n: (r, c) for n, r, c in BIG}[weight]
        assert lane0 == 0 and shard[1] == D and tk % shard[0] == 0
        per_step = tk // shard[0]
    assert K % tk == 0 and N % tn == 0 and T % tt == 0, (a.shape, g.shape)
    steps = T // tt

    def body(a_ref, g_ref, *rest):
        o_ref, acc_ref = rest[-2:]
        t = pl.program_id(2)

        @pl.when(t == 0)
        def _():
            acc_ref[...] = jnp.zeros_like(acc_ref)

        acc_ref[...] += _dot_tn(a_ref[...].astype(BF16), g_ref[...].astype(BF16))

        @pl.when(t == steps - 1)
        def _():
            o_ref[...] = acc_ref[...].astype(o_ref.dtype).reshape(o_ref.shape)

    in_specs = [pl.BlockSpec((tt, tk), lambda k, n, t: (t, k)), pl.BlockSpec((tt, tn), lambda k, n, t: (t, n))]
    if into is None:
        return _pcall(body, name=name, grid=(K // tk, N // tn, steps), in_specs=in_specs,
                      out_specs=pl.BlockSpec((tk, tn), lambda k, n, t: (k, n)), out_shape=SDS((K, N), F32),
                      scratch=[pltpu.VMEM((tk, tn), F32)], sem=("parallel", "parallel", "arbitrary"))(a, g)
    assert row0 % shard[0] == 0 and (K // tk) * (N // tn) * per_step == N_CHIPS
    return _pcall(body, name=name, grid=(K // tk, N // tn, steps), in_specs=in_specs + [ANY],
                  out_specs=pl.BlockSpec((per_step, shard[0], tn), lambda k, n, t: (k + n, row0 // shard[0], 0)),
                  out_shape=SDS(buf.shape, buf.dtype),
                  scratch=[pltpu.VMEM((tk, tn), F32)], sem=("parallel", "parallel", "arbitrary"), aliases={2: 0})(a, g, buf)


def _wgrad_t(at, g, name):
    nblk, K, tt = at.shape
    N = g.shape[1]
    tk = min(K, 1024)
    assert K % tk == 0 and g.shape[0] == nblk * tt

    def body(a_ref, g_ref, o_ref):
        @pl.when(pl.program_id(1) == 0)
        def _():
            o_ref[...] = jnp.zeros_like(o_ref)

        o_ref[...] += _dot(a_ref[0], g_ref[...].astype(BF16))

    return _pcall(body, name=name, grid=(K // tk, nblk),
                  in_specs=[pl.BlockSpec((1, tk, tt), lambda k, t: (t, k, 0)), pl.BlockSpec((tt, N), lambda k, t: (t, 0))],
                  out_specs=pl.BlockSpec((tk, N), lambda k, t: (k, 0)), out_shape=SDS((K, N), F32),
                  sem=("parallel", "arbitrary"))(at, g)


def _adamw(w, packed_g, m, v, name):
    _, R, C = w.shape
    _, row0, lane0 = PACK_AT[name]
    tr = min(R, 256 if row0 % 256 == 0 else 128)
    assert row0 % tr == 0 and R % tr == 0

    def body(w_ref, g_ref, m_ref, v_ref, go_ref, d_ref, m2_ref, v2_ref):
        g_ = g_ref[:, lane0:lane0 + C]
        go_ref[0] = g_
        m2 = ADAM_B1 * m_ref[0] + (1.0 - ADAM_B1) * g_
        v2 = ADAM_B2 * v_ref[0] + (1.0 - ADAM_B2) * jnp.square(g_)
        m_hat = m2 / (1.0 - ADAM_B1 ** ADAM_STEP)
        v_hat = v2 / (1.0 - ADAM_B2 ** ADAM_STEP)
        d_ref[0] = -ADAM_LR * (m_hat / (jnp.sqrt(v_hat) + ADAM_EPS) + ADAM_WD * w_ref[0])
        m2_ref[0] = m2
        v2_ref[0] = v2

    r = pl.BlockSpec((1, tr, C), lambda i: (0, i, 0))
    return _pcall(body, name="adamw_" + name, grid=(R // tr,),
                  in_specs=[r, pl.BlockSpec((tr, D), lambda i: (row0 // tr + i, 0)), r, r], out_specs=[r] * 4,
                  out_shape=[SDS((1, R, C), F32)] * 4, sem=("parallel",))(w, packed_g, m, v)


def _adamw_small(w, parts, m, v):
    def body(w_ref, p_ref, m_ref, v_ref, g_ref, d_ref, m2_ref, v2_ref):
        g_ = p_ref[0]
        for k in range(1, N_DEV):
            g_ = g_ + p_ref[k]
        g_ref[...] = g_
        m2 = ADAM_B1 * m_ref[...] + (1.0 - ADAM_B1) * g_
        v2 = ADAM_B2 * v_ref[...] + (1.0 - ADAM_B2) * jnp.square(g_)
        m_hat = m2 / (1.0 - ADAM_B1 ** ADAM_STEP)
        v_hat = v2 / (1.0 - ADAM_B2 ** ADAM_STEP)
        d_ref[...] = -ADAM_LR * (m_hat / (jnp.sqrt(v_hat) + ADAM_EPS) + ADAM_WD * w_ref[...])
        m2_ref[...] = m2
        v2_ref[...] = v2

    s = _full((8, D))
    return _pcall(body, name="adamw_small", grid=(1,), in_specs=[s, _full((N_DEV, 8, D)), s, s], out_specs=[s] * 4,
                  out_shape=[SDS((8, D), F32)] * 4, sem=("arbitrary",))(w, parts, m, v)


ANY = pl.BlockSpec(memory_space=pl.ANY)


def _place():
    x, y, c = lax.axis_index("x"), lax.axis_index("y"), lax.axis_index("c")
    chips = [(1 - x, y), (x, 1 - y), (1 - x, 1 - y)]
    return x, y, c, chips


def _all_gather(wpk):
    rows = wpk.shape[0]
    HALF = rows // 2
    assert HALF % 16 == 0

    def body(in_ref, out_ref, send_sems, recv_sems):
        x, y, c, chips = _place()
        half = pl.ds(pl.multiple_of(c * HALF, 16), HALF)
        other = pl.ds(pl.multiple_of((1 - c) * HALF, 16), HALF)

        def copy(k, src, dst, to):
            return pltpu.make_async_remote_copy(src_ref=src, dst_ref=dst, send_sem=send_sems.at[k], recv_sem=recv_sems.at[k],
                                                device_id=to, device_id_type=MESH)

        first = [copy(k, in_ref.at[half], out_ref.at[2 * x + y, half], (cx, cy, c)) for k, (cx, cy) in enumerate(chips)]
        for cp in first:
            cp.start()
        passed = []
        for k, (cx, cy) in enumerate(chips):
            slot = out_ref.at[2 * cx + cy, half]
            copy(k, slot, slot, (x, y, c)).wait_recv()
            fwd = copy(3 + k, slot, slot, (x, y, 1 - c))
            fwd.start()
            passed.append(fwd)
        for k, (cx, cy) in enumerate(chips):
            slot = out_ref.at[2 * cx + cy, other]
            copy(3 + k, slot, slot, (x, y, c)).wait_recv()
        for cp in first + passed:
            cp.wait_send()

    return _pcall(body, name="all_gather_weights", in_specs=[ANY], out_specs=ANY,
                  out_shape=SDS((N_CHIPS, rows, D), BF16),
                  scratch=[pltpu.SemaphoreType.DMA((6,)), pltpu.SemaphoreType.DMA((6,))])(wpk)


HBM = pl.BlockSpec(memory_space=pltpu.HBM)
SEM = pl.BlockSpec(memory_space=pltpu.SEMAPHORE)
DATAFLOW = pltpu.SideEffectType.DATAFLOW_SIDE_EFFECTING


def _in_hbm(a):
    return pltpu.with_memory_space_constraint(a, pltpu.HBM)


def _gather_late_start(wpk, after):
    rows = wpk.shape[0]

    def body(in_ref, land_ref, after_ref, send_sems, recv_sems, in_thru, land_thru, token):
        x, y, c, chips = _place()
        for k, (cx, cy) in enumerate(chips):
            pltpu.make_async_remote_copy(src_ref=in_ref, dst_ref=land_ref.at[2 * x + y], send_sem=send_sems.at[k],
                                         recv_sem=recv_sems.at[k], device_id=(cx, cy, c), device_id_type=MESH).start()
        token[...] = jnp.zeros_like(token)

    return pl.pallas_call(
        body, name="gather_late_start",
        out_shape=(pltpu.SemaphoreType.DMA((3,)), pltpu.SemaphoreType.DMA((3,)), pltpu.HBM(wpk.shape, wpk.dtype),
                   pltpu.HBM((N_CHIPS, rows, D), wpk.dtype), SDS((8, LANES), F32)),
        in_specs=(HBM, HBM, ANY), out_specs=(SEM, SEM, HBM, HBM, pl.BlockSpec(memory_space=pltpu.VMEM)),
        input_output_aliases={0: 2, 1: 3}, compiler_params=pltpu.CompilerParams(has_side_effects=DATAFLOW),
    )(_in_hbm(wpk), _in_hbm(lax.empty((N_CHIPS, rows, D), wpk.dtype)), after)


def _gather_late_wait(send_sems, recv_sems, in_thru, land_thru, after):
    def body(in_ref, land_ref, send_sems, recv_sems, after_ref, after2_ref, in_dead, got_ref):
        x, y, c, chips = _place()
        for k, (cx, cy) in enumerate(chips):
            cp = pltpu.make_async_remote_copy(src_ref=in_ref, dst_ref=land_ref.at[2 * cx + cy], send_sem=send_sems.at[k],
                                              recv_sem=recv_sems.at[k], device_id=(cx, cy, c), device_id_type=MESH)
            cp.wait_send()
            cp.wait_recv()

    return pl.pallas_call(
        body, name="gather_late_wait",
        out_shape=(pltpu.HBM(in_thru.shape, in_thru.dtype), pltpu.HBM(land_thru.shape, land_thru.dtype)),
        in_specs=(HBM, HBM, SEM, SEM, ANY, ANY), out_specs=(HBM, HBM), input_output_aliases={0: 0, 1: 1},
        compiler_params=pltpu.CompilerParams(has_side_effects=DATAFLOW),
    )(in_thru, land_thru, send_sems, recv_sems, *after)[1]


def _rs_sibling(gpk):
    HALF = gpk.shape[1] // 2

    def body(in_ref, out_ref, send_sem, recv_sem):
        x, y, c, _ = _place()
        theirs = pl.ds(pl.multiple_of((1 - c) * HALF, 8), HALF)
        cp = pltpu.make_async_remote_copy(src_ref=in_ref.at[:, theirs], dst_ref=out_ref, send_sem=send_sem, recv_sem=recv_sem,
                                          device_id=(x, y, 1 - c), device_id_type=MESH)
        cp.start()
        cp.wait()

    return _pcall(body, name="rs_sibling", in_specs=[ANY], out_specs=ANY, out_shape=SDS((N_CHIPS, HALF, D), F32),
                  scratch=[pltpu.SemaphoreType.DMA, pltpu.SemaphoreType.DMA])(gpk)


def _rs_add_sibling(cidx, gpk, got):
    HALF = got.shape[1]
    th = HALF // 4
    nh = HALF // th
    assert th % 16 == 0

    def body(c_ref, a_ref, b_ref, o_ref):
        o_ref[...] = (a_ref[...] + b_ref[...]).astype(BF16)

    gs = pltpu.PrefetchScalarGridSpec(
        num_scalar_prefetch=1, grid=(N_CHIPS, nh),
        in_specs=[pl.BlockSpec((1, th, D), lambda j, i, c: (j, c[0] * nh + i, 0)), pl.BlockSpec((1, th, D), lambda j, i, c: (j, i, 0))],
        out_specs=pl.BlockSpec((1, th, D), lambda j, i, c: (j, i, 0)))
    return pl.pallas_call(body, name="rs_add_sibling", grid_spec=gs, out_shape=SDS((N_CHIPS, HALF, D), BF16),
                          compiler_params=pltpu.CompilerParams(dimension_semantics=("parallel", "parallel"),
                                                               vmem_limit_bytes=48 << 20))(cidx, gpk, got)


def _rs_chips_start(part, small, after):
    def body(p_ref, s_ref, land_ref, sland_ref, after_ref, send_sems, recv_sems, p_thru, s_thru, land_thru, sland_thru, token):
        x, y, c, chips = _place()
        for k, (cx, cy) in enumerate(chips):
            pltpu.make_async_remote_copy(src_ref=p_ref.at[2 * cx + cy], dst_ref=land_ref.at[2 * x + y], send_sem=send_sems.at[k],
                                         recv_sem=recv_sems.at[k], device_id=(cx, cy, c), device_id_type=MESH).start()
        peers = [(x, y, 1 - c)] + [(cx, cy, c) for cx, cy in chips] + [(cx, cy, 1 - c) for cx, cy in chips]
        for k, to in enumerate(peers):
            pltpu.make_async_remote_copy(src_ref=s_ref, dst_ref=sland_ref.at[4 * x + 2 * y + c], send_sem=send_sems.at[3 + k],
                                         recv_sem=recv_sems.at[3 + k], device_id=to, device_id_type=MESH).start()
        token[...] = jnp.zeros_like(token)

    return pl.pallas_call(
        body, name="rs_chips_start",
        out_shape=(pltpu.SemaphoreType.DMA((10,)), pltpu.SemaphoreType.DMA((10,)), pltpu.HBM(part.shape, part.dtype),
                   pltpu.HBM(small.shape, small.dtype), pltpu.HBM(part.shape, part.dtype), pltpu.HBM((N_DEV, 8, D), F32),
                   SDS((8, LANES), F32)),
        in_specs=(HBM, HBM, HBM, HBM, ANY), out_specs=(SEM, SEM, HBM, HBM, HBM, HBM, pl.BlockSpec(memory_space=pltpu.VMEM)),
        input_output_aliases={0: 2, 1: 3, 2: 4, 3: 5}, compiler_params=pltpu.CompilerParams(has_side_effects=DATAFLOW),
    )(_in_hbm(part), _in_hbm(small), _in_hbm(lax.empty(part.shape, part.dtype)), _in_hbm(lax.empty((N_DEV, 8, D), F32)), after)


def _rs_chips_wait(send_sems, recv_sems, p_thru, s_thru, land_thru, sland_thru, after):
    def body(p_ref, s_ref, land_ref, sland_ref, send_sems, recv_sems, *after_and_outputs):
        x, y, c, chips = _place()
        for k, (cx, cy) in enumerate(chips):
            cp = pltpu.make_async_remote_copy(src_ref=p_ref.at[0], dst_ref=land_ref.at[2 * cx + cy], send_sem=send_sems.at[k],
                                              recv_sem=recv_sems.at[k], device_id=(cx, cy, c), device_id_type=MESH)
            cp.wait_send()
            cp.wait_recv()
        peers = [(x, y, 1 - c)] + [(cx, cy, c) for cx, cy in chips] + [(cx, cy, 1 - c) for cx, cy in chips]
        for k, (px, py, pc) in enumerate(peers):
            cp = pltpu.make_async_remote_copy(src_ref=s_ref, dst_ref=sland_ref.at[4 * px + 2 * py + pc], send_sem=send_sems.at[3 + k],
                                              recv_sem=recv_sems.at[3 + k], device_id=(px, py, pc), device_id_type=MESH)
            cp.wait_send()
            cp.wait_recv()

    hbm = lambda a: pltpu.HBM(a.shape, a.dtype)
    outs = pl.pallas_call(
        body, name="rs_chips_wait", out_shape=(hbm(p_thru), hbm(s_thru), hbm(land_thru), hbm(sland_thru)),
        in_specs=(HBM, HBM, HBM, HBM, SEM, SEM) + (ANY,) * len(after), out_specs=(HBM, HBM, HBM, HBM),
        input_output_aliases={0: 0, 1: 1, 2: 2, 3: 3}, compiler_params=pltpu.CompilerParams(has_side_effects=DATAFLOW),
    )(p_thru, s_thru, land_thru, sland_thru, send_sems, recv_sems, *after)
    return outs[0], outs[2], outs[3]


def _rs_add_chips(qidx, part, parts):
    HALF = part.shape[1]
    th = HALF // 4
    assert th % 16 == 0

    def body(q_ref, own_ref, p_ref, o_ref):
        for me in range(N_CHIPS):
            @pl.when(q_ref[0] == me)
            def _(me=me):
                t = [(own_ref[0] if j == me else p_ref[j]).astype(F32) for j in range(N_CHIPS)]
                o_ref[...] = ((t[0] + t[1]) + t[2]) + t[3]

    gs = pltpu.PrefetchScalarGridSpec(
        num_scalar_prefetch=1, grid=(HALF // th,),
        in_specs=[pl.BlockSpec((1, th, D), lambda i, q: (q[0], i, 0)), pl.BlockSpec((N_CHIPS, th, D), lambda i, q: (0, i, 0))],
        out_specs=pl.BlockSpec((th, D), lambda i, q: (i, 0)))
    return pl.pallas_call(body, name="rs_add_chips", grid_spec=gs, out_shape=SDS((HALF, D), F32),
                          compiler_params=pltpu.CompilerParams(dimension_semantics=("parallel",),
                                                               vmem_limit_bytes=48 << 20))(qidx, part, parts)


def _rs_join(mine, core, name):
    def body(in_ref, out_ref, send_sem, recv_sem):
        x, y, c, _ = _place()
        cp = pltpu.make_async_remote_copy(src_ref=in_ref, dst_ref=out_ref, send_sem=send_sem, recv_sem=recv_sem,
                                          device_id=(x, y, 1 - c), device_id_type=MESH)
        cp.start()
        cp.wait()

    theirs = _pcall(body, name=name, in_specs=[ANY], out_specs=ANY, out_shape=SDS(mine.shape, F32),
                    scratch=[pltpu.SemaphoreType.DMA, pltpu.SemaphoreType.DMA])(mine)
    return jnp.where(core == 0, jnp.concatenate([mine, theirs]), jnp.concatenate([theirs, mine]))


def _reduce_late_start(gpk, after):
    rows = gpk.shape[1]
    HALF = rows // 2
    assert HALF % 16 == 0

    def body(in_ref, land_ref, after_ref, send_sems, recv_sems, in_thru, land_thru, token):
        x, y, c, chips = _place()
        me = 4 * x + 2 * y + c
        peers = [(x, y, 1 - c)] + [(cx, cy, c) for cx, cy in chips] + [(cx, cy, 1 - c) for cx, cy in chips]
        for k, (px, py, pc) in enumerate(peers):
            src = in_ref.at[2 * px + py, pl.ds(pl.multiple_of(pc * HALF, 16), HALF)]
            pltpu.make_async_remote_copy(src_ref=src, dst_ref=land_ref.at[me], send_sem=send_sems.at[k], recv_sem=recv_sems.at[k],
                                         device_id=(px, py, pc), device_id_type=MESH).start()
        token[...] = jnp.zeros_like(token)

    return pl.pallas_call(
        body, name="reduce_late_start",
        out_shape=(pltpu.SemaphoreType.DMA((7,)), pltpu.SemaphoreType.DMA((7,)), pltpu.HBM(gpk.shape, gpk.dtype),
                   pltpu.HBM((N_DEV, HALF, D), gpk.dtype), SDS((8, LANES), F32)),
        in_specs=(HBM, HBM, ANY), out_specs=(SEM, SEM, HBM, HBM, pl.BlockSpec(memory_space=pltpu.VMEM)),
        input_output_aliases={0: 2, 1: 3}, compiler_params=pltpu.CompilerParams(has_side_effects=DATAFLOW),
    )(_in_hbm(gpk), _in_hbm(lax.empty((N_DEV, HALF, D), gpk.dtype)), after)


def _reduce_late_wait(send_sems, recv_sems, in_thru, land_thru, after):
    def body(in_ref, land_ref, send_sems, recv_sems, after_ref, in_out, got_ref):
        x, y, c, chips = _place()
        peers = [(x, y, 1 - c)] + [(cx, cy, c) for cx, cy in chips] + [(cx, cy, 1 - c) for cx, cy in chips]
        for k, (px, py, pc) in enumerate(peers):
            cp = pltpu.make_async_remote_copy(src_ref=land_ref.at[0], dst_ref=land_ref.at[4 * px + 2 * py + pc],
                                              send_sem=send_sems.at[k], recv_sem=recv_sems.at[k],
                                              device_id=(px, py, pc), device_id_type=MESH)
            cp.wait_send()
            cp.wait_recv()

    return pl.pallas_call(
        body, name="reduce_late_wait",
        out_shape=(pltpu.HBM(in_thru.shape, in_thru.dtype), pltpu.HBM(land_thru.shape, land_thru.dtype)),
        in_specs=(HBM, HBM, SEM, SEM, ANY), out_specs=(HBM, HBM), input_output_aliases={0: 0, 1: 1},
        compiler_params=pltpu.CompilerParams(has_side_effects=DATAFLOW),
    )(in_thru, land_thru, send_sems, recv_sems, after)


def _reduce_late_add(didx, gpk, parts):
    HALF = parts.shape[1]
    th = HALF // 4
    nh = HALF // th
    assert th % 16 == 0

    def body(d_ref, own_ref, p_ref, o_ref):
        for me in range(N_DEV):
            @pl.when(d_ref[0] == me)
            def _(me=me):
                t = [(own_ref[0] if j == me else p_ref[j]).astype(F32) for j in range(N_DEV)]
                o_ref[...] = ((((((t[0] + t[1]) + t[2]) + t[3]) + t[4]) + t[5]) + t[6]) + t[7]

    gs = pltpu.PrefetchScalarGridSpec(
        num_scalar_prefetch=1, grid=(nh,),
        in_specs=[pl.BlockSpec((1, th, D), lambda i, d: (d[1], d[2] * nh + i, 0)), pl.BlockSpec((N_DEV, th, D), lambda i, d: (0, i, 0))],
        out_specs=pl.BlockSpec((th, D), lambda i, d: (i, 0)))
    return pl.pallas_call(body, name="reduce_late_add", grid_spec=gs, out_shape=SDS((HALF, D), F32),
                          compiler_params=pltpu.CompilerParams(dimension_semantics=("parallel",),
                                                               vmem_limit_bytes=48 << 20))(didx, gpk, parts)


def _pack_early(b, dtype):
    lanes = lambda a: jnp.pad(a.astype(dtype), ((0, 0), (0, D - a.shape[1])))
    pair = jnp.concatenate([b["w_q_b"].astype(dtype), b["w_ple"].astype(dtype), jnp.zeros((256, D - 640), dtype)], axis=1)
    return jnp.concatenate([lanes(b["w_in"]), pair, lanes(b["w_kv_b"])], axis=0)


def _pack_late(b, dtype):
    return jnp.concatenate([b[n].astype(dtype) for n in ("w_mla_up", "w_swa_up", "w_out", "w_ple_gate", "w_mlp_up", "w_mlp_down")],
                           axis=0)


def _unpack_shards(pk, which):
    return {n: pk[PACK_AT[n][1]:PACK_AT[n][1] + r, PACK_AT[n][2]:PACK_AT[n][2] + c] for n, r, c in BIG if PACK_AT[n][0] == which}


def _full_weights(gathered, own, chip, which):
    own_b = _unpack_shards(own, which)
    per_chip = [{n: jnp.where(chip == j, own_b[n], blk) for n, blk in _unpack_shards(gathered[j], which).items()}
                for j in range(N_CHIPS)]
    out = {}
    for n in own_b:
        shards = [pc[n] for pc in per_chip]
        if n == "w_in":
            out["w_in_p"] = _w_in_internal(shards)
        else:
            out[n] = jnp.concatenate(shards, axis=1 if n in COL_SHARDED else 0)
    return out


def _split_full_grads(grads, pack, dtype):
    shard = {n: (r, c) for n, r, c in BIG}
    chunks = []
    for j in range(N_CHIPS):
        blocks = {}
        for n, g in grads.items():
            if n == "w_in_p":
                blocks["w_in"] = _w_in_grad_shard(g, j)
                continue
            r, c = shard[n]
            blocks[n] = g[:, j * c:(j + 1) * c] if n in COL_SHARDED else g[j * r:(j + 1) * r]
        chunks.append(pack(blocks, dtype))
    return jnp.stack(chunks)


W_IN_SHARD = 936
W_IN_SEGMENTS = ((0, 256, (3072,)), (256, 384, (3840,)), (384, 416, (4032,)), (416, 1440, (0,)), (1440, 1504, (3328, 3392)),
                 (1504, 1568, (3456, 3520)), (1568, 1632, (3584, 3648)), (1632, 1696, (3712, 3776)), (1696, 3744, (1024,)))


def _w_in_internal(shards):
    def cols(a, b):
        out = []
        for j, s in enumerate(shards):
            lo, hi = max(a, W_IN_SHARD * j), min(b, W_IN_SHARD * (j + 1))
            if lo < hi:
                out.append(s[:, lo - W_IN_SHARD * j:hi - W_IN_SHARD * j])
        return out

    pieces = {}
    for a, b, places in W_IN_SEGMENTS:
        for at in places:
            pieces[at] = cols(a, b)
    zeros = lambda n: [jnp.zeros((D, n), shards[0].dtype)]
    pieces[3968] = zeros(64)
    pieces[4064] = zeros(32)
    return jnp.concatenate([piece for at in sorted(pieces) for piece in pieces[at]], axis=1)


def _w_in_grad_shard(g, j):
    def internal(a, b):
        out = []
        while a < b:
            end = min(b, (a // D + 1) * D)
            out.append(g[a // D][:, a % D:a % D + end - a])
            a = end
        return out

    out = []
    for a, b, places in W_IN_SEGMENTS:
        lo, hi = max(a, W_IN_SHARD * j), min(b, W_IN_SHARD * (j + 1))
        if lo < hi:
            parts = [internal(at + lo - a, at + hi - a) for at in places]
            if len(parts) == 1:
                out += parts[0]
            else:
                assert len(parts[0]) == len(parts[1]) == 1
                out.append(parts[0][0] + parts[1][0])
    return jnp.concatenate(out, axis=1)


def _local_step(x, p, tgt, w, small, late_weights, late_grads_out):
    T = x.shape[0]
    tm = 256
    tb = 256
    w_in_p = w["w_in_p"]
    wqb = jnp.pad(w["w_q_b"].reshape(Q_LORA, MLA_HEADS, 96), ((0, 0), (0, 0), (0, 32))).reshape(Q_LORA, 2048)
    wkv = w["w_kv_b"].reshape(KV_LORA, MLA_HEADS, 128)
    wkn = jnp.pad(wkv[:, :, :64], ((0, 0), (0, 0), (0, 64))).reshape(KV_LORA, 2048)
    wv = wkv[:, :, 64:].reshape(KV_LORA, 1024)
    tab_m = _rope_tables(T, "mla")
    tab_s = _rope_tables(T, "swa")
    g1, gq, gkv, sinks = small["g_mix_pre"], small["g_q_a"], small["g_kv_a"], small["sinks"]
    g2, g3, g4, g5 = small["g_mix_post"], small["g_mlp_pre"], small["g_mlp_post"], small["g_ple"]
    sink_vec = sinks.reshape(SWA_HEADS)

    z, h1 = _fwd_in(x, g1, w_in_p, tm)
    qn, kvn, km, vm, qt, kt, vt, qs, ks, vs = _fwd_qkv(z, gq, gkv, wqb, wkn, wv, tab_m, tab_s, tb)
    om, lse_m = _mla_fwd(qt, km, vt, tb)
    os_, lse_s = _swa_fwd(sink_vec, qs, ks, vs)
    w = {**w, **late_weights((om, os_))}
    y, yo, au, bu, x1 = _fwd_mix(om, os_, z, x, w["w_mla_up"], w["w_swa_up"], w["w_out"], g2, tm)
    h2, u = _fwd_mlp_up(x1, g3, w["w_mlp_up"], tm)
    d, x2 = _fwd_mlp_down(u, w["w_mlp_down"], x1, g4, tm)
    loss, dx2, dgt, de0, dg5 = _ple_fwd_bwd(p, x2, tgt, w["w_ple"], g5, w["w_ple_gate"], tm)

    dd, da, dg4 = _bwd_mlp_down(dx2, d, g4, w["w_mlp_down"], u, tm)
    dx1, dg3 = _bwd_mlp_up(da, w["w_mlp_up"], x1, g3, dx2, tm)
    dyo, dg2, dau, dbu, dga, dgb, dos, delta_m, dom_t = _bwd_mix(dx1, yo, g2, w["w_out"], z, au, bu, w["w_mla_up"],
                                                                w["w_swa_up"], om, tb)
    gpk_late = lax.empty((N_CHIPS, PACK_ROWS["late"], D), BF16)
    for weight, a_, g_ in (("w_mla_up", om, dau), ("w_swa_up", os_, dbu), ("w_out", y, dyo), ("w_ple_gate", x2, dgt),
                           ("w_mlp_up", h2, da), ("w_mlp_down", u, dd)):
        gpk_late = _wgrad(a_, g_, "wgrad_" + weight[2:], into=(gpk_late, weight))
    token = late_grads_out(gpk_late)
    delta_m = delta_m + token[0, 0]
    dqm, dkm, dvm = _mla_bwd(qt, km, kt, vm, dom_t, lse_m, delta_m, tb)
    dqs, dkc, dkp, dvc, dvp, dsink = _swa_bwd(sink_vec, qs, ks, vs, dos, os_, lse_s)
    dqb, dknb, dvb, dsq, drest, dgq, dgkv = _bwd_qkv(dqm, dkm, dvm, dqs, dkc, dkp, dvc, dvp, z, gq, gkv, wqb, wkn, wv,
                                                      tab_m, tab_s)
    gx, dg1 = _bwd_in(dsq, dga, dgb, drest, w_in_p, x, g1, dx1, tm)

    g_in_p = [_wgrad(h1, dsq, "wgrad_in_sq"), _wgrad(h1, dga, "wgrad_in_ga"), _wgrad(h1, dgb, "wgrad_in_gb"),
              _wgrad(h1, drest, "wgrad_in_rest")]
    g_qb_p = _wgrad_t(dqb, qn, "wgrad_q_b").T
    g_kn_p = _wgrad_t(dknb, kvn, "wgrad_kv_b_nope").T
    g_v_p = _wgrad_t(dvb, kvn, "wgrad_kv_b_v").T
    grads = {
        "w_in_p": g_in_p,
        "w_q_b": g_qb_p.reshape(Q_LORA, MLA_HEADS, 128)[:, :, :96].reshape(Q_LORA, 1536),
        "w_kv_b": jnp.concatenate([g_kn_p.reshape(KV_LORA, MLA_HEADS, 128)[:, :, :64], g_v_p.reshape(KV_LORA, MLA_HEADS, 64)],
                                  axis=2).reshape(KV_LORA, 2048),
        "w_ple": _wgrad(p, de0, "wgrad_ple"),
    }
    small_grads = {"g_mix_pre": dg1, "g_q_a": dgq.reshape(1, Q_LORA), "g_kv_a": dgkv.reshape(1, KV_LORA), "sinks": dsink[0:1, 0:SWA_HEADS], "g_mix_post": dg2,
                   "g_mlp_pre": dg3, "g_mlp_post": dg4, "g_ple": dg5}
    return loss, gx, grads, small_grads


def _pack_small(vals, fill, scalar=None):
    wide = [vals[n] for n, k in SMALL if k == D]
    narrow = [vals[n] for n, k in SMALL if k != D]
    used = sum(k for _, k in SMALL if k != D)
    last = jnp.concatenate(narrow + [jnp.full((1, D - used), fill, F32)], axis=1)
    rest = jnp.full((2, D), fill, F32)
    if scalar is not None:
        rest = jnp.concatenate([jnp.concatenate([scalar, rest[0:1, 1:]], axis=1), rest[1:2]], axis=0)
    return jnp.concatenate(wide + [last, rest], axis=0)


def _unpack_small(pk):
    out, row, off = {}, 0, 0
    for n, k in SMALL:
        if k == D:
            out[n] = pk[row:row + 1]
            row += 1
    for n, k in SMALL:
        if k != D:
            out[n] = pk[5:6, off:off + k]
            off += k
    return out


def kernel(x, p, g_mix_pre, w_in, g_q_a, w_q_b, g_kv_a, w_kv_b, sinks, w_mla_up, w_swa_up, w_out, g_mix_post, g_mlp_pre, w_mlp_up, w_mlp_down, g_mlp_post, w_ple, g_ple, w_ple_gate, loss_target, m_g_mix_pre, m_w_in, m_g_q_a, m_w_q_b, m_g_kv_a, m_w_kv_b, m_sinks, m_w_mla_up, m_w_swa_up, m_w_out, m_g_mix_post, m_g_mlp_pre, m_w_mlp_up, m_w_mlp_down, m_g_mlp_post, m_w_ple, m_g_ple, m_w_ple_gate, v_g_mix_pre, v_w_in, v_g_q_a, v_w_q_b, v_g_kv_a, v_w_kv_b, v_sinks, v_w_mla_up, v_w_swa_up, v_w_out, v_g_mix_post, v_g_mlp_pre, v_w_mlp_up, v_w_mlp_down, v_g_mlp_post, v_w_ple, v_g_ple, v_w_ple_gate):
    given = dict(locals())
    big_w = {n: given[n][0] for n, _, _ in BIG}
    small_w = {n: given[n] for n, _ in SMALL}
    small_m = {n: given["m_" + n] for n, _ in SMALL}
    small_v = {n: given["v_" + n] for n, _ in SMALL}

    core = lax.axis_index("c")
    chip = 2 * lax.axis_index("x") + lax.axis_index("y")
    core_i = core.astype(jnp.int32).reshape(1)
    chip_i = chip.astype(jnp.int32).reshape(1)
    dev_i = jnp.stack([2 * chip + core, chip, core]).astype(jnp.int32)

    own_early = _pack_early(big_w, BF16)
    own_late = _pack_late(big_w, BF16)
    got_early = _all_gather(own_early)
    late_flight = _gather_late_start(own_late, got_early)
    weights = _full_weights(got_early, own_early, chip, "early")
    step_small = {**small_w, "g_mix_pre": small_w["g_mix_pre"] + late_flight[4][0, 0]}

    def late_weights(after):
        return _full_weights(_gather_late_wait(*late_flight[:4], after), own_late, chip, "late")

    flight = {}

    def late_grads_out(gpk_late):
        flight["late"] = _reduce_late_start(gpk_late, dev_i)
        return flight["late"][4]

    loss_blk, gx, grads, small_grads = _local_step(x[0], p[0, 0], loss_target[0], weights, step_small, late_weights,
                                                   late_grads_out)

    gpk = _split_full_grads(grads, _pack_early, F32)
    got = _rs_sibling(gpk)
    part = _rs_add_sibling(core_i, gpk, got)
    small_own = _pack_small(small_grads, 0.0, loss_blk[0:1, 0:1])
    early_flight = _rs_chips_start(part, small_own, dev_i)

    out_g, out_d, out_m, out_v = {}, {}, {}, {}
    gpk_late, parts_late = _reduce_late_wait(*flight["late"][:4], early_flight[6])
    joined_late = _rs_join(_reduce_late_add(dev_i, gpk_late, parts_late), core, "rs_join_late")
    for n, _, _ in BIG:
        if PACK_AT[n][0] == "late":
            out_g[n], out_d[n], out_m[n], out_v[n] = _adamw(given[n], joined_late, given["m_" + n], given["v_" + n], n)

    part, parts, small_parts = _rs_chips_wait(*early_flight[:6], [out_d[n] for n in out_d])
    joined_early = _rs_join(_rs_add_chips(chip_i, part, parts), core, "rs_join_early")
    for n, _, _ in BIG:
        if PACK_AT[n][0] == "early":
            out_g[n], out_d[n], out_m[n], out_v[n] = _adamw(given[n], joined_early, given["m_" + n], given["v_" + n], n)

    mine = (lax.broadcasted_iota(jnp.int32, (N_DEV, 1, 1), 0) == dev_i[0])
    g_small_pk, d_small_pk, m_small_pk, v_small_pk = _adamw_small(
        _pack_small(small_w, 0.0), jnp.where(mine, small_own[None], small_parts), _pack_small(small_m, 0.0),
        _pack_small(small_v, 1.0))
    loss = g_small_pk[6, 0]
    for out, pk in ((out_g, g_small_pk), (out_d, d_small_pk), (out_m, m_small_pk), (out_v, v_small_pk)):
        out.update(_unpack_small(pk))
    order = ["g_mix_pre", "w_in", "g_q_a", "w_q_b", "g_kv_a", "w_kv_b", "sinks", "w_mla_up", "w_swa_up", "w_out", "g_mix_post",
             "g_mlp_pre", "w_mlp_up", "w_mlp_down", "g_mlp_post", "w_ple", "g_ple", "w_ple_gate"]
    return (loss, gx[None], *[out_g[n] for n in order], *[out_d[n] for n in order], *[out_m[n] for n in order],
            *[out_v[n] for n in order])
```

```python
import math

import jax
import jax.numpy as jnp
from jax import lax
from jax.experimental import pallas as pl
from jax.experimental.pallas import tpu as pltpu

F32 = jnp.float32
BF16 = jnp.bfloat16
SDS = jax.ShapeDtypeStruct

D = 1024
D_FF = 4096
PLE = 256
Q_LORA = 256
KV_LORA = 128
MLA_HEADS = 16
MLA_NOPE = 64
MLA_ROPE = 32
SWA_HEADS = 16
SWA_HD = 64
WINDOW = 128
ROPE_THETA = 10000.0
EPS = 1e-6
NEG = -1e30
NZ = 4096
MLA_SCALE = (MLA_NOPE + MLA_ROPE) ** -0.5
LOG2_E = math.log2(math.e)
MLA_LOG2_SCALE = MLA_SCALE * LOG2_E
SWA_SCALE = SWA_HD ** -0.5

ADAM_LR = 0.001
ADAM_B1 = 0.9
ADAM_B2 = 0.999
ADAM_EPS = 1e-08
ADAM_WD = 0.01
ADAM_STEP = 10

LANES = 128
ATT_COLS = 128
N_CHIPS = 4
N_DEV = 8
MESH = pl.DeviceIdType.MESH

NT = (((1,), (1,)), ((), ()))
TN = (((0,), (0,)), ((), ()))

BIG = (("w_in", 1024, 936), ("w_q_b", 256, 384), ("w_kv_b", 128, 512), ("w_mla_up", 256, 1024),
       ("w_swa_up", 256, 1024), ("w_out", 256, 1024), ("w_mlp_up", 1024, 1024), ("w_mlp_down", 1024, 1024),
       ("w_ple", 256, 256), ("w_ple_gate", 256, 1024))
COL_SHARDED = ("w_in", "w_q_b", "w_kv_b", "w_mlp_up", "w_ple")
PACK_AT = {"w_in": ("early", 0, 0), "w_q_b": ("early", 1024, 0), "w_ple": ("early", 1024, 384), "w_kv_b": ("early", 1280, 0),
           "w_mla_up": ("late", 0, 0), "w_swa_up": ("late", 256, 0), "w_out": ("late", 512, 0), "w_ple_gate": ("late", 768, 0),
           "w_mlp_up": ("late", 1024, 0), "w_mlp_down": ("late", 2048, 0)}
PACK_ROWS = {"early": 1408, "late": 3072}
SMALL = (("g_mix_pre", 1024), ("g_q_a", 256), ("g_kv_a", 128), ("sinks", 16), ("g_mix_post", 1024),
         ("g_mlp_pre", 1024), ("g_mlp_post", 1024), ("g_ple", 1024))


def _dot(a, b):
    return jnp.dot(a, b, preferred_element_type=F32)


def _dot_nt(a, b):
    return lax.dot_general(a, b, NT, preferred_element_type=F32)


def _dot_tn(a, b):
    return lax.dot_general(a, b, TN, preferred_element_type=F32)


def _pcall(body, *, name, out_shape, grid=(), in_specs=None, out_specs=None, scratch=(), sem=None, vmem_mb=48, aliases=None):
    params = dict(vmem_limit_bytes=vmem_mb << 20)
    if sem is not None:
        params["dimension_semantics"] = sem
    return pl.pallas_call(body, name=name, grid=grid, in_specs=in_specs, out_specs=out_specs, out_shape=out_shape,
                          scratch_shapes=list(scratch), input_output_aliases=aliases or {},
                          compiler_params=pltpu.CompilerParams(**params))


def _rows(tm, n, col=0):
    return pl.BlockSpec((tm, n), lambda i: (i, col))


def _full(shape):
    return pl.BlockSpec(shape, lambda i: (0,) * len(shape))


def _rms(x, g):
    r = lax.rsqrt(jnp.mean(x * x, axis=-1, keepdims=True) + EPS)
    return x * r * g


def _rms_bwd(dy, x, g):
    r = lax.rsqrt(jnp.mean(x * x, axis=-1, keepdims=True) + EPS)
    xn = x * r
    dn = dy * g
    dx = r * (dn - xn * jnp.mean(dn * xn, axis=-1, keepdims=True))
    return dx, jnp.sum(dy * xn, axis=0, keepdims=True)


def _sigmoid(x):
    return 1.0 / (1.0 + jnp.exp(-x))


def _rope(x, c, a, b, half):
    return x * c + pltpu.roll(x, LANES - half, 1) * a + pltpu.roll(x, half, 1) * b


def _rope_tables(T, kind):
    lane = jnp.arange(LANES)
    if kind == "mla":
        half = MLA_ROPE // 2
        rel = lane - MLA_NOPE
        on = (rel >= 0) & (rel < MLA_ROPE)
        d = MLA_ROPE
    else:
        half = SWA_HD // 2
        rel = lane % SWA_HD
        on = jnp.ones((LANES,), bool)
        d = SWA_HD
    first = on & (rel < half)
    second = on & (rel >= half)
    f = jnp.where(first, rel, rel - half).astype(F32)
    inv = jnp.exp(-math.log(ROPE_THETA) * f * (2.0 / d))
    ang = jnp.arange(T, dtype=F32)[:, None] * inv[None, :]
    cos, sin = jnp.cos(ang), jnp.sin(ang)
    c = jnp.where(on[None], cos, 1.0)
    a = jnp.where(first[None], -sin, 0.0)
    b = jnp.where(second[None], sin, 0.0)
    return c, a, b


def _fwd_in(x, g1, w_in_p, tm):
    T = x.shape[0]

    def body(x_ref, g_ref, w_ref, z_ref, h_ref):
        h = _rms(x_ref[...], g_ref[...]).astype(BF16)
        h_ref[...] = h
        z_ref[...] = _dot(h, w_ref[...])

    return _pcall(body, name="fwd_in", grid=(T // tm,),
                  in_specs=[_rows(tm, D), _full((1, D)), _full((D, NZ))],
                  out_specs=[_rows(tm, NZ), _rows(tm, D)],
                  out_shape=[SDS((T, NZ), F32), SDS((T, D), BF16)], sem=("parallel",))(x, g1, w_in_p)


def _fwd_qkv(z, gq, gkv, wqb, wkn, wv, tab_m, tab_s, tm):
    T = z.shape[0]
    wqb_t, wkn_t, wv_t = wqb.T, wkn.T, wv.T
    tab_mt = [t.T for t in tab_m]

    def body(qa_ref, sq_ref, skd_ref, svd_ref, kva_ref, kr_ref, gq_ref, gkv_ref, wkn_ref, wv_ref, wqbt_ref, wknt_ref, wvt_ref,
             cm_ref, am_ref, bm_ref, cmt_ref, amt_ref, bmt_ref, cs_ref, as_ref, bs_ref,
             qn_ref, kvn_ref, km_ref, vm_ref, qt_ref, kt_ref, vt_ref, qs_ref, ks_ref, vs_ref):
        qn = _rms(qa_ref[...], gq_ref[...])
        qn_ref[...] = qn.astype(BF16)
        kvn = _rms(kva_ref[...], gkv_ref[...])
        kvn_b = kvn.astype(BF16)
        kvn_ref[...] = kvn_b
        qn_t = qn.T.astype(BF16)
        kvn_t = kvn.T.astype(BF16)
        cm, am, bm = cm_ref[...], am_ref[...], bm_ref[...]
        cmt, amt, bmt = cmt_ref[...], amt_ref[...], bmt_ref[...]
        cs, as_, bs = cs_ref[...], as_ref[...], bs_ref[...]
        k_rope = _rope(kr_ref[...], cm, am, bm, MLA_ROPE // 2)
        k_rope_t = k_rope.T
        half = MLA_ROPE // 2
        vm_ref[...] = _dot(kvn_b, wv_ref[...]).astype(BF16)
        km_all = _dot(kvn_b, wkn_ref[...])
        v_t = _dot(wvt_ref[...], kvn_t)
        q_t = _dot(wqbt_ref[...], qn_t)
        k_t = _dot(wknt_ref[...], kvn_t)
        ones_row = jnp.where(lax.broadcasted_iota(jnp.int32, (64, tm), 0) == 0, 1.0, 0.0)
        for h in range(MLA_HEADS):
            sl = slice(LANES * h, LANES * (h + 1))
            vt_ref[0, sl, :] = jnp.concatenate([v_t[64 * h:64 * (h + 1)], ones_row], axis=0).astype(BF16)
            qh = q_t[sl]
            qt_ref[0, sl, :] = (qh * cmt + pltpu.roll(qh, LANES - half, 0) * amt + pltpu.roll(qh, half, 0) * bmt).astype(BF16)
            km_ref[:, sl] = (km_all[:, sl] + k_rope).astype(BF16)
            kt_ref[0, sl, :] = (k_t[sl] + k_rope_t).astype(BF16)
        for j in range(D // LANES):
            sl = slice(LANES * j, LANES * (j + 1))
            qs_ref[:, sl] = _rope(sq_ref[:, sl], cs, as_, bs, SWA_HD // 2).astype(BF16)
        for j in range(2):
            sl = slice(LANES * j, LANES * (j + 1))
            ks_ref[:, sl] = _rope(skd_ref[:, sl], cs, as_, bs, SWA_HD // 2).astype(BF16)
        vs_ref[...] = svd_ref[...].astype(BF16)

    tab = [_rows(tm, LANES)] * 3
    tab_t = [pl.BlockSpec((LANES, tm), lambda i: (0, i))] * 3
    return _pcall(body, name="fwd_qkv", grid=(T // tm,),
                  in_specs=[_rows(tm, 256, 12), _rows(tm, 1024, 0), _rows(tm, 256, 13), _rows(tm, 256, 14),
                            _rows(tm, 128, 30), _rows(tm, 128, 31), _full((1, Q_LORA)), _full((1, KV_LORA)),
                            _full((KV_LORA, 2048)), _full((KV_LORA, 1024)), _full((2048, Q_LORA)), _full((2048, KV_LORA)),
                            _full((1024, KV_LORA))] + tab + tab_t + tab,
                  out_specs=[_rows(tm, Q_LORA), _rows(tm, KV_LORA), _rows(tm, 2048), _rows(tm, 1024),
                             pl.BlockSpec((1, 2048, tm), lambda i: (i, 0, 0)), pl.BlockSpec((1, 2048, tm), lambda i: (i, 0, 0)),
                             pl.BlockSpec((1, 2048, tm), lambda i: (i, 0, 0)),
                             _rows(tm, 1024), _rows(tm, 256), _rows(tm, 256)],
                  out_shape=[SDS((T, Q_LORA), BF16), SDS((T, KV_LORA), BF16), SDS((T, 2048), BF16),
                             SDS((T, 1024), BF16), SDS((T // tm, 2048, tm), BF16), SDS((T // tm, 2048, tm), BF16),
                             SDS((T // tm, 2048, tm), BF16),
                             SDS((T, 1024), BF16), SDS((T, 256), BF16), SDS((T, 256), BF16)],
                  sem=("parallel",))(z, z, z, z, z, z, gq, gkv, wkn, wv, wqb_t, wkn_t, wv_t, *tab_m, *tab_mt, *tab_s)


def _mla_fwd(qt, km, vt, tb):
    T = km.shape[0]
    nb = T // tb
    cc = ATT_COLS

    def body(q_ref, k_ref, vt_ref, o_ref, l_ref, s_ref, p_ref, al_ref, m_ref, acc_ref):
        i = pl.program_id(1)
        m_ref[...] = jnp.full(m_ref.shape, NEG, F32)
        acc_ref[...] = jnp.zeros_like(acc_ref)
        p_ref[1] = jnp.zeros(p_ref.shape[1:], BF16)
        al_ref[1] = jnp.ones(al_ref.shape[1:], F32)
        key = lax.broadcasted_iota(jnp.int32, (tb, cc), 0)
        qry = lax.broadcasted_iota(jnp.int32, (tb, cc), 1)

        def scores(j, slot):
            off = pl.multiple_of(j * tb, tb)
            for hh in range(2):
                sl = slice(LANES * hh, LANES * (hh + 1))
                s_ref[slot, hh] = _dot(k_ref[pl.ds(off, tb), sl], q_ref[0, sl, :])

        def softmax(slot, diagonal):
            chains = [(hh, slice(cc * c, cc * (c + 1)), c) for hh in range(2) for c in range(tb // cc)]

            def scaled(hh, cols, c):
                t = s_ref[slot, hh, :, cols] * MLA_LOG2_SCALE
                return jnp.where(key <= qry + cc * c, t, NEG) if diagonal else t

            tops = []
            for hh, cols, c in chains:
                if diagonal:
                    top = jnp.max(scaled(hh, cols, c), axis=0, keepdims=True)
                else:
                    top = jnp.max(s_ref[slot, hh, :, cols], axis=0, keepdims=True) * MLA_LOG2_SCALE
                m_old = m_ref[hh, :, cols]
                mn = jnp.maximum(m_old, top)
                m_ref[hh, :, cols] = mn
                al_ref[slot, hh, :, cols] = jnp.exp2(m_old - mn)
                tops.append(mn)
            for (hh, cols, c), mn in zip(chains, tops):
                p_ref[slot, hh, :, cols] = jnp.exp2(scaled(hh, cols, c) - mn).astype(BF16)

        def accumulate(j, slot):
            for hh in range(2):
                acc_ref[hh] = al_ref[slot, hh] * acc_ref[hh] + _dot(vt_ref[j, LANES * hh:LANES * (hh + 1), :], p_ref[slot, hh])

        def step(t, carry):
            scores(2 * t + 1, 1)
            accumulate(jnp.maximum(2 * t - 1, 0), 1)
            softmax(0, False)
            scores(2 * t + 2, 0)
            accumulate(2 * t, 0)
            softmax(1, False)
            return carry

        scores(0, 0)
        lax.fori_loop(0, i // 2, step, 0)

        @pl.when(i % 2 == 1)
        def _():
            scores(i, 1)
            accumulate(jnp.maximum(i - 2, 0), 1)
            softmax(0, False)
            accumulate(i - 1, 0)
            softmax(1, True)
            accumulate(i, 1)

        @pl.when(i % 2 == 0)
        def _():
            accumulate(jnp.maximum(i - 1, 0), 1)
            softmax(0, True)
            accumulate(i, 0)
        den = [acc_ref[hh, 64:65, :] for hh in range(2)]
        o_ref[...] = jnp.concatenate([acc_ref[hh, 0:64, :] / den[hh] for hh in range(2)], axis=0).T
        sub = lax.broadcasted_iota(jnp.int32, (8, tb), 0)
        lse = [m_ref[hh] + jnp.log(den[hh]) * LOG2_E for hh in range(2)]
        l_ref[0, 0] = jnp.where(sub == 0, lse[0], jnp.where(sub == 1, lse[1], 0.0))

    return _pcall(body, name="mla_fwd", grid=(MLA_HEADS // 2, nb),
                  in_specs=[pl.BlockSpec((1, 256, tb), lambda p, i: (i, p, 0)), pl.BlockSpec((T, 256), lambda p, i: (0, p)),
                            pl.BlockSpec((nb, 2 * LANES, tb), lambda p, i: (0, p, 0))],
                  out_specs=[pl.BlockSpec((tb, LANES), lambda p, i: (i, p)),
                             pl.BlockSpec((1, 1, 8, tb), lambda p, i: (p, i, 0, 0))],
                  out_shape=[SDS((T, D), F32), SDS((MLA_HEADS // 2, nb, 8, tb), F32)],
                  scratch=[pltpu.VMEM((2, 2, tb, tb), F32), pltpu.VMEM((2, 2, tb, tb), BF16), pltpu.VMEM((2, 2, 1, tb), F32),
                           pltpu.VMEM((2, 1, tb), F32), pltpu.VMEM((2, LANES, tb), F32)],
                  sem=("parallel", "arbitrary"))(qt, km, vt)


def _swa_mask(n):
    row = lax.broadcasted_iota(jnp.int32, (WINDOW, 2 * WINDOW), 0)
    col = lax.broadcasted_iota(jnp.int32, (WINDOW, 2 * WINDOW), 1)
    rel = row - col + WINDOW
    return (rel >= 0) & (rel < WINDOW) & ((col >= WINDOW) | (n > 0))


def _swa_specs(T):
    nb = T // WINDOW
    cur = lambda w: pl.BlockSpec((WINDOW, w), lambda n: (n, 0))
    prev = lambda w: pl.BlockSpec((WINDOW, w), lambda n: (jnp.maximum(n - 1, 0), 0))
    return nb, cur, prev


def _swa_fwd(sinks, qs, ks, vs):
    T = qs.shape[0]
    nb, cur, prev = _swa_specs(T)

    def body(sink_ref, q_ref, kc_ref, kp_ref, vc_ref, vp_ref, o_ref, l_ref, kb_ref, vb_ref, s_ref, p_ref):
        n = pl.program_id(0)
        mask = _swa_mask(n)
        lo = lax.broadcasted_iota(jnp.int32, (WINDOW, LANES), 1) < 64
        hi = jnp.logical_not(lo)
        for g in range(2):
            gs = slice(LANES * g, LANES * (g + 1))
            kb_ref[g] = jnp.concatenate([kp_ref[:, gs], kc_ref[:, gs]], axis=0)
            vb_ref[g] = jnp.concatenate([vp_ref[:, gs], vc_ref[:, gs]], axis=0)
        for h in range(SWA_HEADS):
            qp = q_ref[:, LANES * (h // 2):LANES * (h // 2 + 1)]
            qh = jnp.where(lo if h % 2 == 0 else hi, qp, jnp.zeros_like(qp))
            s_ref[h] = _dot_nt(qh, kb_ref[h // 8])
        for j in range(SWA_HEADS // 2):
            sl = slice(LANES * j, LANES * (j + 1))
            lses = []
            for h in (2 * j, 2 * j + 1):
                s = jnp.where(mask, s_ref[h] * SWA_SCALE, NEG)
                sk = sink_ref[h]
                m = jnp.maximum(jnp.max(s, axis=1, keepdims=True), sk)
                e = jnp.exp(s - m)
                den = jnp.sum(e, axis=1, keepdims=True) + jnp.exp(sk - m)
                p_ref[h] = (e / den).astype(BF16)
                lses.append(jnp.broadcast_to(m + jnp.log(den), (WINDOW, LANES)))
            l_ref[:, sl] = jnp.where(lo, lses[0], lses[1])
        for j in range(SWA_HEADS // 2):
            vb = vb_ref[j // 4]
            o_ref[:, LANES * j:LANES * (j + 1)] = jnp.where(lo, _dot(p_ref[2 * j], vb), _dot(p_ref[2 * j + 1], vb))

    return _pcall(body, name="swa_fwd", grid=(nb,),
                  in_specs=[pl.BlockSpec(memory_space=pltpu.SMEM), cur(D), cur(256), prev(256), cur(256), prev(256)],
                  out_specs=[cur(D), cur(D)], out_shape=[SDS((T, D), F32)] * 2,
                  scratch=[pltpu.VMEM((2, 2 * WINDOW, LANES), BF16), pltpu.VMEM((2, 2 * WINDOW, LANES), BF16),
                           pltpu.VMEM((SWA_HEADS, WINDOW, 2 * WINDOW), F32), pltpu.VMEM((SWA_HEADS, WINDOW, 2 * WINDOW), BF16)],
                  sem=("parallel",))(sinks, qs, ks, ks, vs, vs)


def _fwd_mix(om, os_, z, x, wmu, wsu, wo, g2, tm):
    T = x.shape[0]

    def body(om_ref, os_ref, ga_ref, gb_ref, x_ref, wmu_ref, wsu_ref, wo_ref, g2_ref,
             y_ref, yo_ref, au_ref, bu_ref, x1_ref):
        au = _dot(om_ref[...].astype(BF16), wmu_ref[...])
        bu = _dot(os_ref[...].astype(BF16), wsu_ref[...])
        au_ref[...] = au
        bu_ref[...] = bu
        y = (_sigmoid(ga_ref[...]) * au + _sigmoid(gb_ref[...]) * bu).astype(BF16)
        y_ref[...] = y
        yo = _dot(y, wo_ref[...])
        yo_ref[...] = yo
        x1_ref[...] = x_ref[...] + _rms(yo, g2_ref[...])

    r = _rows(tm, D)
    w = _full((D, D))
    return _pcall(body, name="fwd_mix", grid=(T // tm,),
                  in_specs=[r, r, _rows(tm, D, 1), _rows(tm, D, 2), r, w, w, w, _full((1, D))],
                  out_specs=[r] * 5,
                  out_shape=[SDS((T, D), BF16), SDS((T, D), F32), SDS((T, D), F32), SDS((T, D), F32), SDS((T, D), F32)],
                  sem=("parallel",))(om, os_, z, z, x, wmu, wsu, wo, g2)


def _fwd_mlp_up(x1, g3, w1, tm):
    T = x1.shape[0]

    def body(x_ref, g_ref, w_ref, h_ref, u_ref):
        h = _rms(x_ref[...], g_ref[...]).astype(BF16)
        h_ref[...] = h
        u_ref[...] = jnp.square(jnp.maximum(_dot(h, w_ref[...]), 0.0)).astype(BF16)

    return _pcall(body, name="fwd_mlp_up", grid=(T // tm,),
                  in_specs=[_rows(tm, D), _full((1, D)), _full((D, D_FF))],
                  out_specs=[_rows(tm, D), _rows(tm, D_FF)],
                  out_shape=[SDS((T, D), BF16), SDS((T, D_FF), BF16)],
                  sem=("parallel",))(x1, g3, w1)


def _fwd_mlp_down(u, w2, x1, g4, tm):
    T = x1.shape[0]

    def body(u_ref, w_ref, x_ref, g_ref, d_ref, x2_ref):
        d = _dot(u_ref[...], w_ref[...])
        d_ref[...] = d
        x2_ref[...] = x_ref[...] + _rms(d, g_ref[...])

    return _pcall(body, name="fwd_mlp_down", grid=(T // tm,),
                  in_specs=[_rows(tm, D_FF), _full((D_FF, D)), _rows(tm, D), _full((1, D))],
                  out_specs=[_rows(tm, D), _rows(tm, D)], out_shape=[SDS((T, D), F32)] * 2,
                  sem=("parallel",))(u, w2, x1, g4)


def _ple_fwd_bwd(p, x2, tgt, wple, g5, wpg, tm):
    T = x2.shape[0]

    def body(p_ref, x2_ref, t_ref, wple_ref, g5_ref, wpg_ref, loss_ref, dx2_ref, dgt_ref, de0_ref, dg5_ref):
        @pl.when(pl.program_id(0) == 0)
        def _():
            loss_ref[...] = jnp.zeros_like(loss_ref)
            dg5_ref[...] = jnp.zeros_like(dg5_ref)

        e0 = _dot(p_ref[...].astype(BF16), wple_ref[...])
        g5 = g5_ref[...]
        r = lax.rsqrt(jnp.mean(e0 * e0, axis=-1, keepdims=True) + EPS)
        en = e0 * r
        e = en * g5
        x2 = x2_ref[...]
        s = _sigmoid(_dot(x2.astype(BF16), wpg_ref[...]))
        diff = x2 + s * e - t_ref[...]
        sq = jnp.sum(jnp.sum(diff * diff, axis=1, keepdims=True), axis=0, keepdims=True)
        loss_ref[...] += jnp.broadcast_to(sq * (0.5 / D), loss_ref.shape)
        dx3 = diff * (1.0 / D)
        de = dx3 * s
        dgt = (dx3 * e * s * (1.0 - s)).astype(BF16)
        dgt_ref[...] = dgt
        dn = de * g5
        de0_ref[...] = (r * (dn - en * jnp.mean(dn * en, axis=-1, keepdims=True))).astype(BF16)
        dg5_ref[...] += jnp.sum(de * en, axis=0, keepdims=True)
        dx2_ref[...] = dx3 + _dot_nt(dgt, wpg_ref[...])

    r = _rows(tm, D)
    return _pcall(body, name="ple_fwd_bwd", grid=(T // tm,),
                  in_specs=[_rows(tm, PLE), r, r, _full((PLE, D)), _full((1, D)), _full((D, D))],
                  out_specs=[_full((8, LANES)), r, r, r, _full((1, D))],
                  out_shape=[SDS((8, LANES), F32), SDS((T, D), F32), SDS((T, D), BF16), SDS((T, D), BF16), SDS((1, D), F32)],
                  sem=("arbitrary",))(p, x2, tgt, wple, g5, wpg)


def _bwd_mlp_down(dx2, d, g4, w2, u, tm):
    T = dx2.shape[0]

    def body(dx_ref, d_ref, g_ref, w_ref, u_ref, dd_ref, da_ref, dg_ref):
        @pl.when(pl.program_id(0) == 0)
        def _():
            dg_ref[...] = jnp.zeros_like(dg_ref)

        dd, dg = _rms_bwd(dx_ref[...], d_ref[...], g_ref[...])
        dg_ref[...] += dg
        ddb = dd.astype(BF16)
        dd_ref[...] = ddb
        du = _dot_nt(ddb, w_ref[...])
        da_ref[...] = (du * (2.0 * jnp.sqrt(u_ref[...].astype(F32)))).astype(BF16)

    return _pcall(body, name="bwd_mlp_down", grid=(T // tm,),
                  in_specs=[_rows(tm, D), _rows(tm, D), _full((1, D)), _full((D_FF, D)), _rows(tm, D_FF)],
                  out_specs=[_rows(tm, D), _rows(tm, D_FF), _full((1, D))],
                  out_shape=[SDS((T, D), BF16), SDS((T, D_FF), BF16), SDS((1, D), F32)],
                  sem=("arbitrary",))(dx2, d, g4, w2, u)


def _bwd_mlp_up(da, w1, x1, g3, dx2, tm):
    T = dx2.shape[0]

    def body(da_ref, w_ref, x_ref, g_ref, dx2_ref, dx1_ref, dg_ref):
        @pl.when(pl.program_id(0) == 0)
        def _():
            dg_ref[...] = jnp.zeros_like(dg_ref)

        dh = _dot_nt(da_ref[...], w_ref[...])
        dx, dg = _rms_bwd(dh, x_ref[...], g_ref[...])
        dg_ref[...] += dg
        dx1_ref[...] = dx2_ref[...] + dx

    return _pcall(body, name="bwd_mlp_up", grid=(T // tm,),
                  in_specs=[_rows(tm, D_FF), _full((D, D_FF)), _rows(tm, D), _full((1, D)), _rows(tm, D)],
                  out_specs=[_rows(tm, D), _full((1, D))],
                  out_shape=[SDS((T, D), F32), SDS((1, D), F32)], sem=("arbitrary",))(da, w1, x1, g3, dx2)


def _bwd_mix(dx1, yo, g2, wo, z, au, bu, wmu, wsu, om, tm):
    T = dx1.shape[0]

    def body(dx_ref, yo_ref, g_ref, wo_ref, ga_ref, gb_ref, au_ref, bu_ref, wmu_ref, wsu_ref, om_ref,
             dyo_ref, dg_ref, dau_ref, dbu_ref, dga_ref, dgb_ref, dos_ref, dl_ref, dot_ref):
        @pl.when(pl.program_id(0) == 0)
        def _():
            dg_ref[...] = jnp.zeros_like(dg_ref)

        dyo, dg = _rms_bwd(dx_ref[...], yo_ref[...], g_ref[...])
        dg_ref[...] += dg
        dyob = dyo.astype(BF16)
        dyo_ref[...] = dyob
        dy = _dot_nt(dyob, wo_ref[...])
        sa = _sigmoid(ga_ref[...])
        sb = _sigmoid(gb_ref[...])
        dau = (dy * sa).astype(BF16)
        dbu = (dy * sb).astype(BF16)
        dau_ref[...] = dau
        dbu_ref[...] = dbu
        dga_ref[...] = (dy * au_ref[...] * sa * (1.0 - sa)).astype(BF16)
        dgb_ref[...] = (dy * bu_ref[...] * sb * (1.0 - sb)).astype(BF16)
        dom = _dot_nt(dau, wmu_ref[...])
        dos_ref[...] = _dot_nt(dbu, wsu_ref[...])
        prod = dom * om_ref[...]
        sub = lax.broadcasted_iota(jnp.int32, (8, tm), 0)
        for pr in range(MLA_HEADS // 2):
            sl = slice(LANES * pr, LANES * (pr + 1))
            pt = prod[:, sl].T
            d0 = jnp.sum(pt[0:64], axis=0, keepdims=True)
            d1 = jnp.sum(pt[64:128], axis=0, keepdims=True)
            dl_ref[pr, 0] = jnp.where(sub == 0, d0, jnp.where(sub == 1, d1, 0.0))
            dot_ref[0, sl, :] = dom[:, sl].T.astype(BF16)

    r = _rows(tm, D)
    w = _full((D, D))
    return _pcall(body, name="bwd_mix", grid=(T // tm,),
                  in_specs=[r, r, _full((1, D)), w, _rows(tm, D, 1), _rows(tm, D, 2), r, r, w, w, r],
                  out_specs=[r, _full((1, D)), r, r, r, r, r, pl.BlockSpec((MLA_HEADS // 2, 1, 8, tm), lambda i: (0, i, 0, 0)),
                             pl.BlockSpec((1, D, tm), lambda i: (i, 0, 0))],
                  out_shape=[SDS((T, D), BF16), SDS((1, D), F32), SDS((T, D), BF16), SDS((T, D), BF16), SDS((T, D), BF16),
                             SDS((T, D), BF16), SDS((T, D), F32), SDS((MLA_HEADS // 2, T // tm, 8, tm), F32),
                             SDS((T // tm, D, tm), BF16)],
                  sem=("arbitrary",))(dx1, yo, g2, wo, z, z, au, bu, wmu, wsu, om)


def _mla_bwd(qt, km, kt, vm, dot, lse, delta, tb):
    T = km.shape[0]
    nb = T // tb
    cc = ATT_COLS

    def body(qt_ref, k_ref, kt_ref, v_ref, dot_ref, l_ref, dl_ref, dqt_ref, dkt_ref, dvt_ref,
             s_ref, dp_ref, p_ref, ds_ref, vh_ref):
        j = pl.program_id(1)

        @pl.when(j == 0)
        def _():
            dqt_ref[...] = jnp.zeros_like(dqt_ref)

        dkt_ref[...] = jnp.zeros_like(dkt_ref)
        dvt_ref[...] = jnp.zeros_like(dvt_ref)
        lo = lax.broadcasted_iota(jnp.int32, (tb, LANES), 1) < 64
        key = lax.broadcasted_iota(jnp.int32, (tb, cc), 0)
        qry = lax.broadcasted_iota(jnp.int32, (tb, cc), 1)
        v = v_ref[...]
        vh_ref[0] = jnp.where(lo, v, jnp.zeros_like(v))
        vh_ref[1] = jnp.where(lo, jnp.zeros_like(v), v)

        def scores(i, slot):
            for hh in range(2):
                sl = slice(LANES * hh, LANES * (hh + 1))
                s_ref[slot, hh] = _dot(k_ref[:, sl], qt_ref[i, sl, :])
                dp_ref[slot, hh] = _dot(vh_ref[hh], dot_ref[i])

        def grads(i, slot, diagonal):
            lse_i = l_ref[0, i]
            delta_i = dl_ref[0, i]
            for hh in range(2):
                for c in range(tb // cc):
                    cols = slice(cc * c, cc * (c + 1))
                    p = jnp.exp2(s_ref[slot, hh, :, cols] * MLA_LOG2_SCALE - lse_i[hh:hh + 1, cols])
                    if diagonal:
                        p = jnp.where(key <= qry + cc * c, p, 0.0)
                    p_ref[hh, :, cols] = p.astype(BF16)
                    ds_ref[hh, :, cols] = (p * (dp_ref[slot, hh, :, cols] - delta_i[hh:hh + 1, cols]) * MLA_SCALE).astype(BF16)
            for hh in range(2):
                sl = slice(LANES * hh, LANES * (hh + 1))
                half = slice(64 * hh, 64 * (hh + 1))
                dvt_ref[0, half, :] += _dot_nt(dot_ref[i, half, :], p_ref[hh])
                dkt_ref[0, sl, :] += _dot_nt(qt_ref[i, sl, :], ds_ref[hh])
                dqt_ref[i, sl, :] += _dot(kt_ref[0, sl, :], ds_ref[hh])

        n_off = nb - 1 - j

        def step(u, carry):
            i0 = j + 1 + 2 * u
            scores(i0 + 1, 1)
            grads(i0, 0, False)
            scores(jnp.where(i0 + 2 < nb, i0 + 2, j), 0)
            grads(i0 + 1, 1, False)
            return carry

        scores(jnp.where(n_off > 0, j + 1, j), 0)
        lax.fori_loop(0, n_off // 2, step, 0)

        @pl.when(n_off % 2 == 1)
        def _():
            scores(j, 1)
            grads(nb - 1, 0, False)
            grads(j, 1, True)

        @pl.when(n_off % 2 == 0)
        def _():
            grads(j, 0, True)

    blk = lambda w: pl.BlockSpec((tb, w), lambda p, j: (j, p))
    stat = pl.BlockSpec((1, nb, 8, tb), lambda p, j: (p, 0, 0, 0))
    pair_t = lambda w: pl.BlockSpec((nb, w, tb), lambda p, j: (0, p, 0))
    blk_t = lambda w: pl.BlockSpec((1, w, tb), lambda p, j: (j, p, 0))
    return _pcall(body, name="mla_bwd", grid=(MLA_HEADS // 2, nb),
                  in_specs=[pair_t(256), blk(256), blk_t(256), blk(LANES), pair_t(LANES), stat, stat],
                  out_specs=[pair_t(256), blk_t(256), blk_t(LANES)],
                  out_shape=[SDS((nb, 2048, tb), F32), SDS((nb, 2048, tb), F32), SDS((nb, D, tb), F32)],
                  scratch=[pltpu.VMEM((2, 2, tb, tb), F32), pltpu.VMEM((2, 2, tb, tb), F32), pltpu.VMEM((2, tb, tb), BF16),
                           pltpu.VMEM((2, tb, tb), BF16), pltpu.VMEM((2, tb, LANES), BF16)],
                  sem=("parallel", "arbitrary"))(qt, km, kt, vm, dot, lse, delta)


def _swa_bwd(sinks, qs, ks, vs, do, o, lse):
    T = qs.shape[0]
    nb, cur, prev = _swa_specs(T)

    def body(sink_ref, q_ref, kc_ref, kp_ref, vc_ref, vp_ref, do_ref, o_ref, l_ref,
             dq_ref, dkc_ref, dkp_ref, dvc_ref, dvp_ref, dsink_ref, kb_ref, vb_ref, s_ref, dp_ref, p_ref, ds_ref):
        n = pl.program_id(0)

        @pl.when(n == 0)
        def _():
            dsink_ref[...] = jnp.zeros_like(dsink_ref)

        mask = _swa_mask(n)
        lo = lax.broadcasted_iota(jnp.int32, (WINDOW, LANES), 1) < 64
        hi = jnp.logical_not(lo)
        lane8 = lax.broadcasted_iota(jnp.int32, (8, LANES), 1)
        for g in range(2):
            gs = slice(LANES * g, LANES * (g + 1))
            kb_ref[g] = jnp.concatenate([kp_ref[:, gs], kc_ref[:, gs]], axis=0)
            vb_ref[g] = jnp.concatenate([vp_ref[:, gs], vc_ref[:, gs]], axis=0)

        def head(h):
            sl = slice(LANES * (h // 2), LANES * (h // 2 + 1))
            hm = lo if h % 2 == 0 else hi
            qp = q_ref[:, sl]
            return hm, sl, jnp.where(hm, qp, jnp.zeros_like(qp)), jnp.where(hm, do_ref[:, sl], 0.0).astype(BF16)

        for h in range(SWA_HEADS):
            _, _, qh, dom = head(h)
            s_ref[h] = _dot_nt(qh, kb_ref[h // 8])
            dp_ref[h] = _dot_nt(dom, vb_ref[h // 8])
        dsink = jnp.zeros((8, LANES), F32)
        for h in range(SWA_HEADS):
            hm, sl, _, _ = head(h)
            lse_h = jnp.max(jnp.where(hm, l_ref[:, sl], -jnp.inf), axis=1, keepdims=True)
            delta = jnp.sum(jnp.where(hm, do_ref[:, sl] * o_ref[:, sl], 0.0), axis=1, keepdims=True)
            p = jnp.exp(jnp.where(mask, s_ref[h] * SWA_SCALE, NEG) - lse_h)
            p_ref[h] = p.astype(BF16)
            ds_ref[h] = (p * (dp_ref[h] - delta) * SWA_SCALE).astype(BF16)
            d_sink = -jnp.sum(jnp.exp(sink_ref[h] - lse_h) * delta, axis=0, keepdims=True)
            dsink = dsink + jnp.where(lane8 == h, d_sink, 0.0)
        dsink_ref[...] += dsink
        for g in range(2):
            gs = slice(LANES * g, LANES * (g + 1))
            dkb = jnp.zeros((2 * WINDOW, LANES), F32)
            dvb = jnp.zeros((2 * WINDOW, LANES), F32)
            for j in range(4 * g, 4 * g + 4):
                dqs = []
                for h in (2 * j, 2 * j + 1):
                    _, _, qh, dom = head(h)
                    dvb = dvb + _dot_tn(p_ref[h], dom)
                    dkb = dkb + _dot_tn(ds_ref[h], qh)
                    dqs.append(_dot(ds_ref[h], kb_ref[g]))
                dq_ref[:, LANES * j:LANES * (j + 1)] = jnp.where(lo, dqs[0], dqs[1])
            dkp_ref[:, gs] = dkb[:WINDOW]
            dkc_ref[:, gs] = dkb[WINDOW:]
            dvp_ref[:, gs] = dvb[:WINDOW]
            dvc_ref[:, gs] = dvb[WINDOW:]

    band = pltpu.VMEM((2, 2 * WINDOW, LANES), BF16)
    return _pcall(body, name="swa_bwd", grid=(nb,),
                  in_specs=[pl.BlockSpec(memory_space=pltpu.SMEM), cur(D), cur(256), prev(256), cur(256), prev(256),
                            cur(D), cur(D), cur(D)],
                  out_specs=[cur(D), cur(256), cur(256), cur(256), cur(256), _full((8, LANES))],
                  out_shape=[SDS((T, D), F32), SDS((T, 256), F32), SDS((T, 256), F32), SDS((T, 256), F32), SDS((T, 256), F32),
                             SDS((8, LANES), F32)],
                  scratch=[band, band, pltpu.VMEM((SWA_HEADS, WINDOW, 2 * WINDOW), F32),
                           pltpu.VMEM((SWA_HEADS, WINDOW, 2 * WINDOW), F32), pltpu.VMEM((SWA_HEADS, WINDOW, 2 * WINDOW), BF16),
                           pltpu.VMEM((SWA_HEADS, WINDOW, 2 * WINDOW), BF16)],
                  sem=("arbitrary",))(sinks, qs, ks, ks, vs, vs, do, o, lse)


def _bwd_qkv(dqm, dkm, dvm, dqs, dkc, dkp, dvc, dvp, z, gq, gkv, wqb, wkn, wv, tab_m, tab_s):
    T = z.shape[0]
    tm = WINDOW
    nb = T // tm
    per = dqm.shape[2] // tm

    tab_mt = [t.T for t in tab_m]
    half = MLA_ROPE // 2

    def rope_t(v, c, a, b):
        return v * c + pltpu.roll(v, LANES - half, 0) * a + pltpu.roll(v, half, 0) * b

    def rms_bwd_t(dy, x, g):
        r = lax.rsqrt(jnp.mean(x * x, axis=0, keepdims=True) + EPS)
        xn = x * r
        dn = dy * g
        return r * (dn - xn * jnp.mean(dn * xn, axis=0, keepdims=True)), jnp.sum(dy * xn, axis=1, keepdims=True)

    def body(dqm_ref, dkm_ref, dvm_ref, dqs_ref, dkc_ref, dkp_ref, dvc_ref, dvp_ref, qa_ref, kva_ref, gq_ref, gkv_ref,
             wqb_ref, wkn_ref, wv_ref, cmt_ref, amt_ref, bmt_ref, cs_ref, as_ref, bs_ref,
             dq_out, dkn_out, dv_out, dsq_ref, drest_ref, dgq_ref, dgkv_ref):
        i = pl.program_id(0)

        @pl.when(i == 0)
        def _():
            dgq_ref[...] = jnp.zeros_like(dgq_ref)
            dgkv_ref[...] = jnp.zeros_like(dgkv_ref)

        cmt, amt, bmt = cmt_ref[...], -amt_ref[...], -bmt_ref[...]
        cs, as_, bs = cs_ref[...], -as_ref[...], -bs_ref[...]
        row = lax.broadcasted_iota(jnp.int32, (LANES, tm), 0)
        nope = row < MLA_NOPE
        roped = jnp.logical_and(row >= MLA_NOPE, row < MLA_NOPE + MLA_ROPE)
        dkr = jnp.zeros((LANES, tm), F32)
        for h in range(MLA_HEADS):
            sl = slice(LANES * h, LANES * (h + 1))
            dq_out[0, sl, :] = rope_t(dqm_ref[0, sl, :], cmt, amt, bmt).astype(BF16)
            dk_h = dkm_ref[0, sl, :]
            dkn_out[0, sl, :] = jnp.where(nope, dk_h, 0.0).astype(BF16)
            dkr = dkr + jnp.where(roped, dk_h, 0.0)
        dv_out[0] = dvm_ref[0].astype(BF16)
        dqn = _dot(wqb_ref[...], dq_out[0])
        dkvn = _dot(wkn_ref[...], dkn_out[0]) + _dot(wv_ref[...], dv_out[0])
        dqa, dgq = rms_bwd_t(dqn, qa_ref[...].T, gq_ref[...])
        dkva, dgkv = rms_bwd_t(dkvn, kva_ref[...].T, gkv_ref[...])
        dgq_ref[...] += dgq
        dgkv_ref[...] += dgkv
        for j in range(D // LANES):
            sl = slice(LANES * j, LANES * (j + 1))
            dsq_ref[:, sl] = _rope(dqs_ref[:, sl], cs, as_, bs, SWA_HD // 2).astype(BF16)
        keep = (i < nb - 1).astype(F32)
        drest_ref[:, 0:256] = dqa.T.astype(BF16)
        for j in range(2):
            sl = slice(LANES * j, LANES * (j + 1))
            dk = dkc_ref[:, sl] + keep * dkp_ref[:, sl]
            drest_ref[:, 256 + LANES * j:256 + LANES * (j + 1)] = _rope(dk, cs, as_, bs, SWA_HD // 2).astype(BF16)
        drest_ref[:, 512:768] = (dvc_ref[...] + keep * dvp_ref[...]).astype(BF16)
        drest_ref[:, 768:896] = dkva.T.astype(BF16)
        drest_ref[:, 896:1024] = rope_t(dkr, cmt, amt, bmt).T.astype(BF16)

    nxt = pl.BlockSpec((tm, 256), lambda i: (jnp.minimum(i + 1, nb - 1), 0))
    tab = [_rows(tm, LANES)] * 3
    tab_t = [pl.BlockSpec((LANES, tm), lambda i: (0, i))] * 3
    blk_t = lambda w: pl.BlockSpec((1, w, tm), lambda i: (i // per, 0, i % per))
    return _pcall(body, name="bwd_qkv", grid=(nb,),
                  in_specs=[blk_t(2048), blk_t(2048), blk_t(1024), _rows(tm, 1024), _rows(tm, 256), nxt,
                            _rows(tm, 256), nxt, _rows(tm, 256, 12), _rows(tm, 128, 30), _full((Q_LORA, 1)), _full((KV_LORA, 1)),
                            _full((Q_LORA, 2048)), _full((KV_LORA, 2048)), _full((KV_LORA, 1024))] + tab_t + tab,
                  out_specs=[blk_t(2048), blk_t(2048), blk_t(1024), _rows(tm, 1024), _rows(tm, 1024),
                             _full((Q_LORA, 1)), _full((KV_LORA, 1))],
                  out_shape=[SDS(dqm.shape, BF16), SDS(dkm.shape, BF16), SDS(dvm.shape, BF16), SDS((T, 1024), BF16),
                             SDS((T, 1024), BF16), SDS((Q_LORA, 1), F32), SDS((KV_LORA, 1), F32)],
                  sem=("arbitrary",))(dqm, dkm, dvm, dqs, dkc, dkp, dvc, dvp, z, z, gq.reshape(Q_LORA, 1),
                                      gkv.reshape(KV_LORA, 1), wqb, wkn, wv, *tab_mt, *tab_s)


def _bwd_in(dsq, dga, dgb, drest, w_in_p, x, g1, dx1, tm):
    T = x.shape[0]

    def body(a_ref, b_ref, c_ref, d_ref, w_ref, x_ref, g_ref, dx1_ref, dx_ref, dg_ref):
        @pl.when(pl.program_id(0) == 0)
        def _():
            dg_ref[...] = jnp.zeros_like(dg_ref)

        dh = (_dot_nt(a_ref[...], w_ref[:, 0:1024]) + _dot_nt(b_ref[...], w_ref[:, 1024:2048])
              + _dot_nt(c_ref[...], w_ref[:, 2048:3072]) + _dot_nt(d_ref[...], w_ref[:, 3072:4096]))
        dx, dg = _rms_bwd(dh, x_ref[...], g_ref[...])
        dg_ref[...] += dg
        dx_ref[...] = dx1_ref[...] + dx

    r = _rows(tm, D)
    return _pcall(body, name="bwd_in", grid=(T // tm,),
                  in_specs=[r, r, r, r, _full((D, NZ)), r, _full((1, D)), r],
                  out_specs=[r, _full((1, D))], out_shape=[SDS((T, D), F32), SDS((1, D), F32)],
                  sem=("arbitrary",))(dsq, dga, dgb, drest, w_in_p, x, g1, dx1)


def _wgrad(a, g, name, into=None):
    T, K = a.shape
    N = g.shape[1]
    tk, tn, tt = min(K, 1024), min(N, 1024), min(T, 1024)
    if into is not None:
        buf, weight = into
        _, row0, lane0 = PACK_AT[weight]
        shard = {n: (r, c) for n, r, c in BIG}[weight]
        assert lane0 == 0 and shard[1] == D and tk % shard[0] == 0
        per_step = tk // shard[0]
    assert K % tk == 0 and N % tn == 0 and T % tt == 0, (a.shape, g.shape)
    steps = T // tt

    def body(a_ref, g_ref, *rest):
        o_ref, acc_ref = rest[-2:]
        t = pl.program_id(2)

        @pl.when(t == 0)
        def _():
            acc_ref[...] = jnp.zeros_like(acc_ref)

        acc_ref[...] += _dot_tn(a_ref[...].astype(BF16), g_ref[...].astype(BF16))

        @pl.when(t == steps - 1)
        def _():
            o_ref[...] = acc_ref[...].astype(o_ref.dtype).reshape(o_ref.shape)

    in_specs = [pl.BlockSpec((tt, tk), lambda k, n, t: (t, k)), pl.BlockSpec((tt, tn), lambda k, n, t: (t, n))]
    if into is None:
        return _pcall(body, name=name, grid=(K // tk, N // tn, steps), in_specs=in_specs,
                      out_specs=pl.BlockSpec((tk, tn), lambda k, n, t: (k, n)), out_shape=SDS((K, N), F32),
                      scratch=[pltpu.VMEM((tk, tn), F32)], sem=("parallel", "parallel", "arbitrary"))(a, g)
    assert row0 % shard[0] == 0 and (K // tk) * (N // tn) * per_step == N_CHIPS
    return _pcall(body, name=name, grid=(K // tk, N // tn, steps), in_specs=in_specs + [ANY],
                  out_specs=pl.BlockSpec((per_step, shard[0], tn), lambda k, n, t: (k + n, row0 // shard[0], 0)),
                  out_shape=SDS(buf.shape, buf.dtype),
                  scratch=[pltpu.VMEM((tk, tn), F32)], sem=("parallel", "parallel", "arbitrary"), aliases={2: 0})(a, g, buf)


def _wgrad_t(at, g, name):
    nblk, K, tt = at.shape
    N = g.shape[1]
    tk = min(K, 1024)
    per_step = 4 if nblk % 4 == 0 else 1
    assert K % tk == 0 and g.shape[0] == nblk * tt

    def body(a_ref, g_ref, o_ref):
        @pl.when(pl.program_id(1) == 0)
        def _():
            o_ref[...] = jnp.zeros_like(o_ref)

        acc = _dot(a_ref[0], g_ref[0:tt, :].astype(BF16))
        for b in range(1, per_step):
            acc = acc + _dot(a_ref[b], g_ref[tt * b:tt * (b + 1), :].astype(BF16))
        o_ref[...] += acc

    return _pcall(body, name=name, grid=(K // tk, nblk // per_step),
                  in_specs=[pl.BlockSpec((per_step, tk, tt), lambda k, t: (t, k, 0)),
                            pl.BlockSpec((per_step * tt, N), lambda k, t: (t, 0))],
                  out_specs=pl.BlockSpec((tk, N), lambda k, t: (k, 0)), out_shape=SDS((K, N), F32),
                  sem=("parallel", "arbitrary"))(at, g)


def _adamw(w, packed_g, m, v, name):
    _, R, C = w.shape
    _, row0, lane0 = PACK_AT[name]
    tr = min(R, 256 if row0 % 256 == 0 else 128)
    assert row0 % tr == 0 and R % tr == 0

    def body(w_ref, g_ref, m_ref, v_ref, go_ref, d_ref, m2_ref, v2_ref):
        g_ = g_ref[:, lane0:lane0 + C]
        go_ref[0] = g_
        m2 = ADAM_B1 * m_ref[0] + (1.0 - ADAM_B1) * g_
        v2 = ADAM_B2 * v_ref[0] + (1.0 - ADAM_B2) * jnp.square(g_)
        m_hat = m2 / (1.0 - ADAM_B1 ** ADAM_STEP)
        v_hat = v2 / (1.0 - ADAM_B2 ** ADAM_STEP)
        d_ref[0] = -ADAM_LR * (m_hat / (jnp.sqrt(v_hat) + ADAM_EPS) + ADAM_WD * w_ref[0])
        m2_ref[0] = m2
        v2_ref[0] = v2

    r = pl.BlockSpec((1, tr, C), lambda i: (0, i, 0))
    return _pcall(body, name="adamw_" + name, grid=(R // tr,),
                  in_specs=[r, pl.BlockSpec((tr, D), lambda i: (row0 // tr + i, 0)), r, r], out_specs=[r] * 4,
                  out_shape=[SDS((1, R, C), F32)] * 4, sem=("parallel",))(w, packed_g, m, v)


def _adamw_small(w, parts, m, v):
    def body(w_ref, p_ref, m_ref, v_ref, g_ref, d_ref, m2_ref, v2_ref):
        g_ = p_ref[0]
        for k in range(1, N_DEV):
            g_ = g_ + p_ref[k]
        g_ref[...] = g_
        m2 = ADAM_B1 * m_ref[...] + (1.0 - ADAM_B1) * g_
        v2 = ADAM_B2 * v_ref[...] + (1.0 - ADAM_B2) * jnp.square(g_)
        m_hat = m2 / (1.0 - ADAM_B1 ** ADAM_STEP)
        v_hat = v2 / (1.0 - ADAM_B2 ** ADAM_STEP)
        d_ref[...] = -ADAM_LR * (m_hat / (jnp.sqrt(v_hat) + ADAM_EPS) + ADAM_WD * w_ref[...])
        m2_ref[...] = m2
        v2_ref[...] = v2

    s = _full((8, D))
    return _pcall(body, name="adamw_small", grid=(1,), in_specs=[s, _full((N_DEV, 8, D)), s, s], out_specs=[s] * 4,
                  out_shape=[SDS((8, D), F32)] * 4, sem=("arbitrary",))(w, parts, m, v)


ANY = pl.BlockSpec(memory_space=pl.ANY)


def _place():
    x, y, c = lax.axis_index("x"), lax.axis_index("y"), lax.axis_index("c")
    chips = [(1 - x, y), (x, 1 - y), (1 - x, 1 - y)]
    return x, y, c, chips


def _all_gather(wpk):
    rows = wpk.shape[0]
    HALF = rows // 2
    assert HALF % 16 == 0

    def body(in_ref, out_ref, send_sems, recv_sems):
        x, y, c, chips = _place()
        half = pl.ds(pl.multiple_of(c * HALF, 16), HALF)
        other = pl.ds(pl.multiple_of((1 - c) * HALF, 16), HALF)

        def copy(k, src, dst, to):
            return pltpu.make_async_remote_copy(src_ref=src, dst_ref=dst, send_sem=send_sems.at[k], recv_sem=recv_sems.at[k],
                                                device_id=to, device_id_type=MESH)

        first = [copy(k, in_ref.at[half], out_ref.at[2 * x + y, half], (cx, cy, c)) for k, (cx, cy) in enumerate(chips)]
        for cp in first:
            cp.start()
        passed = []
        for k, (cx, cy) in enumerate(chips):
            slot = out_ref.at[2 * cx + cy, half]
            copy(k, slot, slot, (x, y, c)).wait_recv()
            fwd = copy(3 + k, slot, slot, (x, y, 1 - c))
            fwd.start()
            passed.append(fwd)
        for k, (cx, cy) in enumerate(chips):
            slot = out_ref.at[2 * cx + cy, other]
            copy(3 + k, slot, slot, (x, y, c)).wait_recv()
        for cp in first + passed:
            cp.wait_send()

    return _pcall(body, name="all_gather_weights", in_specs=[ANY], out_specs=ANY,
                  out_shape=SDS((N_CHIPS, rows, D), BF16),
                  scratch=[pltpu.SemaphoreType.DMA((6,)), pltpu.SemaphoreType.DMA((6,))])(wpk)


HBM = pl.BlockSpec(memory_space=pltpu.HBM)
SEM = pl.BlockSpec(memory_space=pltpu.SEMAPHORE)
DATAFLOW = pltpu.SideEffectType.DATAFLOW_SIDE_EFFECTING


def _in_hbm(a):
    return pltpu.with_memory_space_constraint(a, pltpu.HBM)


def _gather_late_start(wpk, after):
    rows = wpk.shape[0]

    def body(in_ref, land_ref, after_ref, send_sems, recv_sems, in_thru, land_thru, token):
        x, y, c, chips = _place()
        for k, (cx, cy) in enumerate(chips):
            pltpu.make_async_remote_copy(src_ref=in_ref, dst_ref=land_ref.at[2 * x + y], send_sem=send_sems.at[k],
                                         recv_sem=recv_sems.at[k], device_id=(cx, cy, c), device_id_type=MESH).start()
        token[...] = jnp.zeros_like(token)

    return pl.pallas_call(
        body, name="gather_late_start",
        out_shape=(pltpu.SemaphoreType.DMA((3,)), pltpu.SemaphoreType.DMA((3,)), pltpu.HBM(wpk.shape, wpk.dtype),
                   pltpu.HBM((N_CHIPS, rows, D), wpk.dtype), SDS((8, LANES), F32)),
        in_specs=(HBM, HBM, ANY), out_specs=(SEM, SEM, HBM, HBM, pl.BlockSpec(memory_space=pltpu.VMEM)),
        input_output_aliases={0: 2, 1: 3}, compiler_params=pltpu.CompilerParams(has_side_effects=DATAFLOW),
    )(_in_hbm(wpk), _in_hbm(lax.empty((N_CHIPS, rows, D), wpk.dtype)), after)


def _gather_late_wait(send_sems, recv_sems, in_thru, land_thru, after):
    def body(in_ref, land_ref, send_sems, recv_sems, after_ref, after2_ref, in_dead, got_ref):
        x, y, c, chips = _place()
        for k, (cx, cy) in enumerate(chips):
            cp = pltpu.make_async_remote_copy(src_ref=in_ref, dst_ref=land_ref.at[2 * cx + cy], send_sem=send_sems.at[k],
                                              recv_sem=recv_sems.at[k], device_id=(cx, cy, c), device_id_type=MESH)
            cp.wait_send()
            cp.wait_recv()

    return pl.pallas_call(
        body, name="gather_late_wait",
        out_shape=(pltpu.HBM(in_thru.shape, in_thru.dtype), pltpu.HBM(land_thru.shape, land_thru.dtype)),
        in_specs=(HBM, HBM, SEM, SEM, ANY, ANY), out_specs=(HBM, HBM), input_output_aliases={0: 0, 1: 1},
        compiler_params=pltpu.CompilerParams(has_side_effects=DATAFLOW),
    )(in_thru, land_thru, send_sems, recv_sems, *after)[1]


def _rs_sibling(gpk):
    HALF = gpk.shape[1] // 2

    def body(in_ref, out_ref, send_sem, recv_sem):
        x, y, c, _ = _place()
        theirs = pl.ds(pl.multiple_of((1 - c) * HALF, 8), HALF)
        cp = pltpu.make_async_remote_copy(src_ref=in_ref.at[:, theirs], dst_ref=out_ref, send_sem=send_sem, recv_sem=recv_sem,
                                          device_id=(x, y, 1 - c), device_id_type=MESH)
        cp.start()
        cp.wait()

    return _pcall(body, name="rs_sibling", in_specs=[ANY], out_specs=ANY, out_shape=SDS((N_CHIPS, HALF, D), F32),
                  scratch=[pltpu.SemaphoreType.DMA, pltpu.SemaphoreType.DMA])(gpk)


def _rs_add_sibling(cidx, gpk, got):
    HALF = got.shape[1]
    th = HALF // 4
    nh = HALF // th
    assert th % 16 == 0

    def body(c_ref, a_ref, b_ref, o_ref):
        o_ref[...] = (a_ref[...] + b_ref[...]).astype(BF16)

    gs = pltpu.PrefetchScalarGridSpec(
        num_scalar_prefetch=1, grid=(N_CHIPS, nh),
        in_specs=[pl.BlockSpec((1, th, D), lambda j, i, c: (j, c[0] * nh + i, 0)), pl.BlockSpec((1, th, D), lambda j, i, c: (j, i, 0))],
        out_specs=pl.BlockSpec((1, th, D), lambda j, i, c: (j, i, 0)))
    return pl.pallas_call(body, name="rs_add_sibling", grid_spec=gs, out_shape=SDS((N_CHIPS, HALF, D), BF16),
                          compiler_params=pltpu.CompilerParams(dimension_semantics=("parallel", "parallel"),
                                                               vmem_limit_bytes=48 << 20))(cidx, gpk, got)


def _rs_chips_start(part, small, after):
    def body(p_ref, s_ref, land_ref, sland_ref, after_ref, send_sems, recv_sems, p_thru, s_thru, land_thru, sland_thru, token):
        x, y, c, chips = _place()
        for k, (cx, cy) in enumerate(chips):
            pltpu.make_async_remote_copy(src_ref=p_ref.at[2 * cx + cy], dst_ref=land_ref.at[2 * x + y], send_sem=send_sems.at[k],
                                         recv_sem=recv_sems.at[k], device_id=(cx, cy, c), device_id_type=MESH).start()
        peers = [(x, y, 1 - c)] + [(cx, cy, c) for cx, cy in chips] + [(cx, cy, 1 - c) for cx, cy in chips]
        for k, to in enumerate(peers):
            pltpu.make_async_remote_copy(src_ref=s_ref, dst_ref=sland_ref.at[4 * x + 2 * y + c], send_sem=send_sems.at[3 + k],
                                         recv_sem=recv_sems.at[3 + k], device_id=to, device_id_type=MESH).start()
        token[...] = jnp.zeros_like(token)

    return pl.pallas_call(
        body, name="rs_chips_start",
        out_shape=(pltpu.SemaphoreType.DMA((10,)), pltpu.SemaphoreType.DMA((10,)), pltpu.HBM(part.shape, part.dtype),
                   pltpu.HBM(small.shape, small.dtype), pltpu.HBM(part.shape, part.dtype), pltpu.HBM((N_DEV, 8, D), F32),
                   SDS((8, LANES), F32)),
        in_specs=(HBM, HBM, HBM, HBM, ANY), out_specs=(SEM, SEM, HBM, HBM, HBM, HBM, pl.BlockSpec(memory_space=pltpu.VMEM)),
        input_output_aliases={0: 2, 1: 3, 2: 4, 3: 5}, compiler_params=pltpu.CompilerParams(has_side_effects=DATAFLOW),
    )(_in_hbm(part), _in_hbm(small), _in_hbm(lax.empty(part.shape, part.dtype)), _in_hbm(lax.empty((N_DEV, 8, D), F32)), after)


def _rs_chips_wait(send_sems, recv_sems, p_thru, s_thru, land_thru, sland_thru, after):
    def body(p_ref, s_ref, land_ref, sland_ref, send_sems, recv_sems, *after_and_outputs):
        x, y, c, chips = _place()
        for k, (cx, cy) in enumerate(chips):
            cp = pltpu.make_async_remote_copy(src_ref=p_ref.at[0], dst_ref=land_ref.at[2 * cx + cy], send_sem=send_sems.at[k],
                                              recv_sem=recv_sems.at[k], device_id=(cx, cy, c), device_id_type=MESH)
            cp.wait_send()
            cp.wait_recv()
        peers = [(x, y, 1 - c)] + [(cx, cy, c) for cx, cy in chips] + [(cx, cy, 1 - c) for cx, cy in chips]
        for k, (px, py, pc) in enumerate(peers):
            cp = pltpu.make_async_remote_copy(src_ref=s_ref, dst_ref=sland_ref.at[4 * px + 2 * py + pc], send_sem=send_sems.at[3 + k],
                                              recv_sem=recv_sems.at[3 + k], device_id=(px, py, pc), device_id_type=MESH)
            cp.wait_send()
            cp.wait_recv()

    hbm = lambda a: pltpu.HBM(a.shape, a.dtype)
    outs = pl.pallas_call(
        body, name="rs_chips_wait", out_shape=(hbm(p_thru), hbm(s_thru), hbm(land_thru), hbm(sland_thru)),
        in_specs=(HBM, HBM, HBM, HBM, SEM, SEM) + (ANY,) * len(after), out_specs=(HBM, HBM, HBM, HBM),
        input_output_aliases={0: 0, 1: 1, 2: 2, 3: 3}, compiler_params=pltpu.CompilerParams(has_side_effects=DATAFLOW),
    )(p_thru, s_thru, land_thru, sland_thru, send_sems, recv_sems, *after)
    return outs[0], outs[2], outs[3]


def _rs_add_chips(qidx, part, parts):
    HALF = part.shape[1]
    th = HALF // 4
    assert th % 16 == 0

    def body(q_ref, own_ref, p_ref, o_ref):
        for me in range(N_CHIPS):
            @pl.when(q_ref[0] == me)
            def _(me=me):
                t = [(own_ref[0] if j == me else p_ref[j]).astype(F32) for j in range(N_CHIPS)]
                o_ref[...] = ((t[0] + t[1]) + t[2]) + t[3]

    gs = pltpu.PrefetchScalarGridSpec(
        num_scalar_prefetch=1, grid=(HALF // th,),
        in_specs=[pl.BlockSpec((1, th, D), lambda i, q: (q[0], i, 0)), pl.BlockSpec((N_CHIPS, th, D), lambda i, q: (0, i, 0))],
        out_specs=pl.BlockSpec((th, D), lambda i, q: (i, 0)))
    return pl.pallas_call(body, name="rs_add_chips", grid_spec=gs, out_shape=SDS((HALF, D), F32),
                          compiler_params=pltpu.CompilerParams(dimension_semantics=("parallel",),
                                                               vmem_limit_bytes=48 << 20))(qidx, part, parts)


def _rs_join(mine, core, name):
    def body(in_ref, out_ref, send_sem, recv_sem):
        x, y, c, _ = _place()
        cp = pltpu.make_async_remote_copy(src_ref=in_ref, dst_ref=out_ref, send_sem=send_sem, recv_sem=recv_sem,
                                          device_id=(x, y, 1 - c), device_id_type=MESH)
        cp.start()
        cp.wait()

    theirs = _pcall(body, name=name, in_specs=[ANY], out_specs=ANY, out_shape=SDS(mine.shape, F32),
                    scratch=[pltpu.SemaphoreType.DMA, pltpu.SemaphoreType.DMA])(mine)
    return jnp.where(core == 0, jnp.concatenate([mine, theirs]), jnp.concatenate([theirs, mine]))


def _reduce_late_start(gpk, after):
    rows = gpk.shape[1]
    HALF = rows // 2
    assert HALF % 16 == 0

    def body(in_ref, land_ref, after_ref, send_sems, recv_sems, in_thru, land_thru, token):
        x, y, c, chips = _place()
        me = 4 * x + 2 * y + c
        peers = [(x, y, 1 - c)] + [(cx, cy, c) for cx, cy in chips] + [(cx, cy, 1 - c) for cx, cy in chips]
        for k, (px, py, pc) in enumerate(peers):
            src = in_ref.at[2 * px + py, pl.ds(pl.multiple_of(pc * HALF, 16), HALF)]
            pltpu.make_async_remote_copy(src_ref=src, dst_ref=land_ref.at[me], send_sem=send_sems.at[k], recv_sem=recv_sems.at[k],
                                         device_id=(px, py, pc), device_id_type=MESH).start()
        token[...] = jnp.zeros_like(token)

    return pl.pallas_call(
        body, name="reduce_late_start",
        out_shape=(pltpu.SemaphoreType.DMA((7,)), pltpu.SemaphoreType.DMA((7,)), pltpu.HBM(gpk.shape, gpk.dtype),
                   pltpu.HBM((N_DEV, HALF, D), gpk.dtype), SDS((8, LANES), F32)),
        in_specs=(HBM, HBM, ANY), out_specs=(SEM, SEM, HBM, HBM, pl.BlockSpec(memory_space=pltpu.VMEM)),
        input_output_aliases={0: 2, 1: 3}, compiler_params=pltpu.CompilerParams(has_side_effects=DATAFLOW),
    )(_in_hbm(gpk), _in_hbm(lax.empty((N_DEV, HALF, D), gpk.dtype)), after)


def _reduce_late_wait(send_sems, recv_sems, in_thru, land_thru, after):
    def body(in_ref, land_ref, send_sems, recv_sems, after_ref, in_out, got_ref):
        x, y, c, chips = _place()
        peers = [(x, y, 1 - c)] + [(cx, cy, c) for cx, cy in chips] + [(cx, cy, 1 - c) for cx, cy in chips]
        for k, (px, py, pc) in enumerate(peers):
            cp = pltpu.make_async_remote_copy(src_ref=land_ref.at[0], dst_ref=land_ref.at[4 * px + 2 * py + pc],
                                              send_sem=send_sems.at[k], recv_sem=recv_sems.at[k],
                                              device_id=(px, py, pc), device_id_type=MESH)
            cp.wait_send()
            cp.wait_recv()

    return pl.pallas_call(
        body, name="reduce_late_wait",
        out_shape=(pltpu.HBM(in_thru.shape, in_thru.dtype), pltpu.HBM(land_thru.shape, land_thru.dtype)),
        in_specs=(HBM, HBM, SEM, SEM, ANY), out_specs=(HBM, HBM), input_output_aliases={0: 0, 1: 1},
        compiler_params=pltpu.CompilerParams(has_side_effects=DATAFLOW),
    )(in_thru, land_thru, send_sems, recv_sems, after)


def _reduce_late_add(didx, gpk, parts):
    HALF = parts.shape[1]
    th = HALF // 4
    nh = HALF // th
    assert th % 16 == 0

    def body(d_ref, own_ref, p_ref, o_ref):
        for me in range(N_DEV):
            @pl.when(d_ref[0] == me)
            def _(me=me):
                t = [(own_ref[0] if j == me else p_ref[j]).astype(F32) for j in range(N_DEV)]
                o_ref[...] = ((((((t[0] + t[1]) + t[2]) + t[3]) + t[4]) + t[5]) + t[6]) + t[7]

    gs = pltpu.PrefetchScalarGridSpec(
        num_scalar_prefetch=1, grid=(nh,),
        in_specs=[pl.BlockSpec((1, th, D), lambda i, d: (d[1], d[2] * nh + i, 0)), pl.BlockSpec((N_DEV, th, D), lambda i, d: (0, i, 0))],
        out_specs=pl.BlockSpec((th, D), lambda i, d: (i, 0)))
    return pl.pallas_call(body, name="reduce_late_add", grid_spec=gs, out_shape=SDS((HALF, D), F32),
                          compiler_params=pltpu.CompilerParams(dimension_semantics=("parallel",),
                                                               vmem_limit_bytes=48 << 20))(didx, gpk, parts)


def _pack_early(b, dtype):
    lanes = lambda a: jnp.pad(a.astype(dtype), ((0, 0), (0, D - a.shape[1])))
    pair = jnp.concatenate([b["w_q_b"].astype(dtype), b["w_ple"].astype(dtype), jnp.zeros((256, D - 640), dtype)], axis=1)
    return jnp.concatenate([lanes(b["w_in"]), pair, lanes(b["w_kv_b"])], axis=0)


def _pack_late(b, dtype):
    return jnp.concatenate([b[n].astype(dtype) for n in ("w_mla_up", "w_swa_up", "w_out", "w_ple_gate", "w_mlp_up", "w_mlp_down")],
                           axis=0)


def _unpack_shards(pk, which):
    return {n: pk[PACK_AT[n][1]:PACK_AT[n][1] + r, PACK_AT[n][2]:PACK_AT[n][2] + c] for n, r, c in BIG if PACK_AT[n][0] == which}


def _full_weights(gathered, own, chip, which):
    own_b = _unpack_shards(own, which)
    per_chip = [{n: jnp.where(chip == j, own_b[n], blk) for n, blk in _unpack_shards(gathered[j], which).items()}
                for j in range(N_CHIPS)]
    out = {}
    for n in own_b:
        shards = [pc[n] for pc in per_chip]
        if n == "w_in":
            out["w_in_p"] = _w_in_internal(shards)
        else:
            out[n] = jnp.concatenate(shards, axis=1 if n in COL_SHARDED else 0)
    return out


def _split_full_grads(grads, pack, dtype):
    shard = {n: (r, c) for n, r, c in BIG}
    chunks = []
    for j in range(N_CHIPS):
        blocks = {}
        for n, g in grads.items():
            if n == "w_in_p":
                blocks["w_in"] = _w_in_grad_shard(g, j)
                continue
            r, c = shard[n]
            blocks[n] = g[:, j * c:(j + 1) * c] if n in COL_SHARDED else g[j * r:(j + 1) * r]
        chunks.append(pack(blocks, dtype))
    return jnp.stack(chunks)


W_IN_SHARD = 936
W_IN_SEGMENTS = ((0, 256, (3072,)), (256, 384, (3840,)), (384, 416, (4032,)), (416, 1440, (0,)), (1440, 1504, (3328, 3392)),
                 (1504, 1568, (3456, 3520)), (1568, 1632, (3584, 3648)), (1632, 1696, (3712, 3776)), (1696, 3744, (1024,)))


def _w_in_internal(shards):
    def cols(a, b):
        out = []
        for j, s in enumerate(shards):
            lo, hi = max(a, W_IN_SHARD * j), min(b, W_IN_SHARD * (j + 1))
            if lo < hi:
                out.append(s[:, lo - W_IN_SHARD * j:hi - W_IN_SHARD * j])
        return out

    pieces = {}
    for a, b, places in W_IN_SEGMENTS:
        for at in places:
            pieces[at] = cols(a, b)
    zeros = lambda n: [jnp.zeros((D, n), shards[0].dtype)]
    pieces[3968] = zeros(64)
    pieces[4064] = zeros(32)
    return jnp.concatenate([piece for at in sorted(pieces) for piece in pieces[at]], axis=1)


def _w_in_grad_shard(g, j):
    def internal(a, b):
        out = []
        while a < b:
            end = min(b, (a // D + 1) * D)
            out.append(g[a // D][:, a % D:a % D + end - a])
            a = end
        return out

    out = []
    for a, b, places in W_IN_SEGMENTS:
        lo, hi = max(a, W_IN_SHARD * j), min(b, W_IN_SHARD * (j + 1))
        if lo < hi:
            parts = [internal(at + lo - a, at + hi - a) for at in places]
            if len(parts) == 1:
                out += parts[0]
            else:
                assert len(parts[0]) == len(parts[1]) == 1
                out.append(parts[0][0] + parts[1][0])
    return jnp.concatenate(out, axis=1)


def _local_step(x, p, tgt, w, small, late_weights, late_grads_out):
    T = x.shape[0]
    tm = 256
    tb = 256
    w_in_p = w["w_in_p"]
    wqb = jnp.pad(w["w_q_b"].reshape(Q_LORA, MLA_HEADS, 96), ((0, 0), (0, 0), (0, 32))).reshape(Q_LORA, 2048)
    wkv = w["w_kv_b"].reshape(KV_LORA, MLA_HEADS, 128)
    wkn = jnp.pad(wkv[:, :, :64], ((0, 0), (0, 0), (0, 64))).reshape(KV_LORA, 2048)
    wv = wkv[:, :, 64:].reshape(KV_LORA, 1024)
    tab_m = _rope_tables(T, "mla")
    tab_s = _rope_tables(T, "swa")
    g1, gq, gkv, sinks = small["g_mix_pre"], small["g_q_a"], small["g_kv_a"], small["sinks"]
    g2, g3, g4, g5 = small["g_mix_post"], small["g_mlp_pre"], small["g_mlp_post"], small["g_ple"]
    sink_vec = sinks.reshape(SWA_HEADS)

    z, h1 = _fwd_in(x, g1, w_in_p, tm)
    qn, kvn, km, vm, qt, kt, vt, qs, ks, vs = _fwd_qkv(z, gq, gkv, wqb, wkn, wv, tab_m, tab_s, tb)
    om, lse_m = _mla_fwd(qt, km, vt, tb)
    os_, lse_s = _swa_fwd(sink_vec, qs, ks, vs)
    w = {**w, **late_weights((om, os_))}
    y, yo, au, bu, x1 = _fwd_mix(om, os_, z, x, w["w_mla_up"], w["w_swa_up"], w["w_out"], g2, tm)
    h2, u = _fwd_mlp_up(x1, g3, w["w_mlp_up"], tm)
    d, x2 = _fwd_mlp_down(u, w["w_mlp_down"], x1, g4, tm)
    loss, dx2, dgt, de0, dg5 = _ple_fwd_bwd(p, x2, tgt, w["w_ple"], g5, w["w_ple_gate"], tm)

    dd, da, dg4 = _bwd_mlp_down(dx2, d, g4, w["w_mlp_down"], u, tm)
    dx1, dg3 = _bwd_mlp_up(da, w["w_mlp_up"], x1, g3, dx2, tm)
    dyo, dg2, dau, dbu, dga, dgb, dos, delta_m, dom_t = _bwd_mix(dx1, yo, g2, w["w_out"], z, au, bu, w["w_mla_up"],
                                                                w["w_swa_up"], om, tb)
    gpk_late = lax.empty((N_CHIPS, PACK_ROWS["late"], D), BF16)
    for weight, a_, g_ in (("w_mla_up", om, dau), ("w_swa_up", os_, dbu), ("w_out", y, dyo), ("w_ple_gate", x2, dgt),
                           ("w_mlp_up", h2, da), ("w_mlp_down", u, dd)):
        gpk_late = _wgrad(a_, g_, "wgrad_" + weight[2:], into=(gpk_late, weight))
    token = late_grads_out(gpk_late)
    delta_m = delta_m + token[0, 0]
    dqm, dkm, dvm = _mla_bwd(qt, km, kt, vm, dom_t, lse_m, delta_m, tb)
    dqs, dkc, dkp, dvc, dvp, dsink = _swa_bwd(sink_vec, qs, ks, vs, dos, os_, lse_s)
    dqb, dknb, dvb, dsq, drest, dgq, dgkv = _bwd_qkv(dqm, dkm, dvm, dqs, dkc, dkp, dvc, dvp, z, gq, gkv, wqb, wkn, wv,
                                                      tab_m, tab_s)
    gx, dg1 = _bwd_in(dsq, dga, dgb, drest, w_in_p, x, g1, dx1, tm)

    g_in_p = [_wgrad(h1, dsq, "wgrad_in_sq"), _wgrad(h1, dga, "wgrad_in_ga"), _wgrad(h1, dgb, "wgrad_in_gb"),
              _wgrad(h1, drest, "wgrad_in_rest")]
    g_qb_p = _wgrad_t(dqb, qn, "wgrad_q_b").T
    g_kn_p = _wgrad_t(dknb, kvn, "wgrad_kv_b_nope").T
    g_v_p = _wgrad_t(dvb, kvn, "wgrad_kv_b_v").T
    grads = {
        "w_in_p": g_in_p,
        "w_q_b": g_qb_p.reshape(Q_LORA, MLA_HEADS, 128)[:, :, :96].reshape(Q_LORA, 1536),
        "w_kv_b": jnp.concatenate([g_kn_p.reshape(KV_LORA, MLA_HEADS, 128)[:, :, :64], g_v_p.reshape(KV_LORA, MLA_HEADS, 64)],
                                  axis=2).reshape(KV_LORA, 2048),
        "w_ple": _wgrad(p, de0, "wgrad_ple"),
    }
    small_grads = {"g_mix_pre": dg1, "g_q_a": dgq.reshape(1, Q_LORA), "g_kv_a": dgkv.reshape(1, KV_LORA), "sinks": dsink[0:1, 0:SWA_HEADS], "g_mix_post": dg2,
                   "g_mlp_pre": dg3, "g_mlp_post": dg4, "g_ple": dg5}
    return loss, gx, grads, small_grads


def _pack_small(vals, fill, scalar=None):
    wide = [vals[n] for n, k in SMALL if k == D]
    narrow = [vals[n] for n, k in SMALL if k != D]
    used = sum(k for _, k in SMALL if k != D)
    last = jnp.concatenate(narrow + [jnp.full((1, D - used), fill, F32)], axis=1)
    rest = jnp.full((2, D), fill, F32)
    if scalar is not None:
        rest = jnp.concatenate([jnp.concatenate([scalar, rest[0:1, 1:]], axis=1), rest[1:2]], axis=0)
    return jnp.concatenate(wide + [last, rest], axis=0)


def _unpack_small(pk):
    out, row, off = {}, 0, 0
    for n, k in SMALL:
        if k == D:
            out[n] = pk[row:row + 1]
            row += 1
    for n, k in SMALL:
        if k != D:
            out[n] = pk[5:6, off:off + k]
            off += k
    return out


def kernel(x, p, g_mix_pre, w_in, g_q_a, w_q_b, g_kv_a, w_kv_b, sinks, w_mla_up, w_swa_up, w_out, g_mix_post, g_mlp_pre, w_mlp_up, w_mlp_down, g_mlp_post, w_ple, g_ple, w_ple_gate, loss_target, m_g_mix_pre, m_w_in, m_g_q_a, m_w_q_b, m_g_kv_a, m_w_kv_b, m_sinks, m_w_mla_up, m_w_swa_up, m_w_out, m_g_mix_post, m_g_mlp_pre, m_w_mlp_up, m_w_mlp_down, m_g_mlp_post, m_w_ple, m_g_ple, m_w_ple_gate, v_g_mix_pre, v_w_in, v_g_q_a, v_w_q_b, v_g_kv_a, v_w_kv_b, v_sinks, v_w_mla_up, v_w_swa_up, v_w_out, v_g_mix_post, v_g_mlp_pre, v_w_mlp_up, v_w_mlp_down, v_g_mlp_post, v_w_ple, v_g_ple, v_w_ple_gate):
    given = dict(locals())
    big_w = {n: given[n][0] for n, _, _ in BIG}
    small_w = {n: given[n] for n, _ in SMALL}
    small_m = {n: given["m_" + n] for n, _ in SMALL}
    small_v = {n: given["v_" + n] for n, _ in SMALL}

    core = lax.axis_index("c")
    chip = 2 * lax.axis_index("x") + lax.axis_index("y")
    core_i = core.astype(jnp.int32).reshape(1)
    chip_i = chip.astype(jnp.int32).reshape(1)
    dev_i = jnp.stack([2 * chip + core, chip, core]).astype(jnp.int32)

    own_early = _pack_early(big_w, BF16)
    own_late = _pack_late(big_w, BF16)
    got_early = _all_gather(own_early)
    late_flight = _gather_late_start(own_late, got_early)
    weights = _full_weights(got_early, own_early, chip, "early")
    step_small = {**small_w, "g_mix_pre": small_w["g_mix_pre"] + late_flight[4][0, 0]}

    def late_weights(after):
        return _full_weights(_gather_late_wait(*late_flight[:4], after), own_late, chip, "late")

    flight = {}

    def late_grads_out(gpk_late):
        flight["late"] = _reduce_late_start(gpk_late, dev_i)
        return flight["late"][4]

    loss_blk, gx, grads, small_grads = _local_step(x[0], p[0, 0], loss_target[0], weights, step_small, late_weights,
                                                   late_grads_out)

    gpk = _split_full_grads(grads, _pack_early, F32)
    got = _rs_sibling(gpk)
    part = _rs_add_sibling(core_i, gpk, got)
    small_own = _pack_small(small_grads, 0.0, loss_blk[0:1, 0:1])
    early_flight = _rs_chips_start(part, small_own, dev_i)

    out_g, out_d, out_m, out_v = {}, {}, {}, {}
    gpk_late, parts_late = _reduce_late_wait(*flight["late"][:4], early_flight[6])
    joined_late = _rs_join(_reduce_late_add(dev_i, gpk_late, parts_late), core, "rs_join_late")
    for n, _, _ in BIG:
        if PACK_AT[n][0] == "late":
            out_g[n], out_d[n], out_m[n], out_v[n] = _adamw(given[n], joined_late, given["m_" + n], given["v_" + n], n)

    part, parts, small_parts = _rs_chips_wait(*early_flight[:6], [out_d[n] for n in out_d])
    joined_early = _rs_join(_rs_add_chips(chip_i, part, parts), core, "rs_join_early")
    for n, _, _ in BIG:
        if PACK_AT[n][0] == "early":
            out_g[n], out_d[n], out_m[n], out_v[n] = _adamw(given[n], joined_early, given["m_" + n], given["v_" + n], n)

    mine = (lax.broadcasted_iota(jnp.int32, (N_DEV, 1, 1), 0) == dev_i[0])
    g_small_pk, d_small_pk, m_small_pk, v_small_pk = _adamw_small(
        _pack_small(small_w, 0.0), jnp.where(mine, small_own[None], small_parts), _pack_small(small_m, 0.0),
        _pack_small(small_v, 1.0))
    loss = g_small_pk[6, 0]
    for out, pk in ((out_g, g_small_pk), (out_d, d_small_pk), (out_m, m_small_pk), (out_v, v_small_pk)):
        out.update(_unpack_small(pk))
    order = ["g_mix_pre", "w_in", "g_q_a", "w_q_b", "g_kv_a", "w_kv_b", "sinks", "w_mla_up", "w_swa_up", "w_out", "g_mix_post",
             "g_mlp_pre", "w_mlp_up", "w_mlp_down", "g_mlp_post", "w_ple", "g_ple", "w_ple_gate"]
    return (loss, gx[None], *[out_g[n] for n in order], *[out_d[n] for n in order], *[out_m[n] for n in order],
            *[out_v[n] for n in order])
```

```python
import math

import jax
import jax.numpy as jnp
from jax import lax
from jax.experimental import pallas as pl
from jax.experimental.pallas import tpu as pltpu

F32 = jnp.float32
BF16 = jnp.bfloat16
SDS = jax.ShapeDtypeStruct

D = 1024
D_FF = 4096
PLE = 256
Q_LORA = 256
KV_LORA = 128
MLA_HEADS = 16
MLA_NOPE = 64
MLA_ROPE = 32
SWA_HEADS = 16
SWA_HD = 64
WINDOW = 128
ROPE_THETA = 10000.0
EPS = 1e-6
NEG = -1e30
NZ = 4096
MLA_SCALE = (MLA_NOPE + MLA_ROPE) ** -0.5
LOG2_E = math.log2(math.e)
MLA_LOG2_SCALE = MLA_SCALE * LOG2_E
SWA_SCALE = SWA_HD ** -0.5

ADAM_LR = 0.001
ADAM_B1 = 0.9
ADAM_B2 = 0.999
ADAM_EPS = 1e-08
ADAM_WD = 0.01
ADAM_STEP = 10

LANES = 128
ATT_COLS = 128
VT_ROWS = 80
N_CHIPS = 4
N_DEV = 8
MESH = pl.DeviceIdType.MESH

NT = (((1,), (1,)), ((), ()))
TN = (((0,), (0,)), ((), ()))

BIG = (("w_in", 1024, 936), ("w_q_b", 256, 384), ("w_kv_b", 128, 512), ("w_mla_up", 256, 1024),
       ("w_swa_up", 256, 1024), ("w_out", 256, 1024), ("w_mlp_up", 1024, 1024), ("w_mlp_down", 1024, 1024),
       ("w_ple", 256, 256), ("w_ple_gate", 256, 1024))
COL_SHARDED = ("w_in", "w_q_b", "w_kv_b", "w_mlp_up", "w_ple")
PACK_AT = {"w_in": ("early", 0, 0), "w_q_b": ("early", 1024, 0), "w_ple": ("early", 1024, 384), "w_kv_b": ("early", 1280, 0),
           "w_mla_up": ("late", 0, 0), "w_swa_up": ("late", 256, 0), "w_out": ("late", 512, 0), "w_ple_gate": ("late", 768, 0),
           "w_mlp_up": ("late", 1024, 0), "w_mlp_down": ("late", 2048, 0)}
PACK_ROWS = {"early": 1408, "late": 3072}
SMALL = (("g_mix_pre", 1024), ("g_q_a", 256), ("g_kv_a", 128), ("sinks", 16), ("g_mix_post", 1024),
         ("g_mlp_pre", 1024), ("g_mlp_post", 1024), ("g_ple", 1024))


def _dot(a, b):
    return jnp.dot(a, b, preferred_element_type=F32)


def _dot_nt(a, b):
    return lax.dot_general(a, b, NT, preferred_element_type=F32)


def _dot_tn(a, b):
    return lax.dot_general(a, b, TN, preferred_element_type=F32)


def _pcall(body, *, name, out_shape, grid=(), in_specs=None, out_specs=None, scratch=(), sem=None, vmem_mb=48, aliases=None):
    params = dict(vmem_limit_bytes=vmem_mb << 20)
    if sem is not None:
        params["dimension_semantics"] = sem
    return pl.pallas_call(body, name=name, grid=grid, in_specs=in_specs, out_specs=out_specs, out_shape=out_shape,
                          scratch_shapes=list(scratch), input_output_aliases=aliases or {},
                          compiler_params=pltpu.CompilerParams(**params))


def _rows(tm, n, col=0):
    return pl.BlockSpec((tm, n), lambda i: (i, col))


def _full(shape):
    return pl.BlockSpec(shape, lambda i: (0,) * len(shape))


def _rms(x, g):
    r = lax.rsqrt(jnp.mean(x * x, axis=-1, keepdims=True) + EPS)
    return x * r * g


def _rms_bwd(dy, x, g):
    r = lax.rsqrt(jnp.mean(x * x, axis=-1, keepdims=True) + EPS)
    xn = x * r
    dn = dy * g
    dx = r * (dn - xn * jnp.mean(dn * xn, axis=-1, keepdims=True))
    return dx, jnp.sum(dy * xn, axis=0, keepdims=True)


def _sigmoid(x):
    return 1.0 / (1.0 + jnp.exp(-x))


def _rope(x, c, a, b, half):
    return x * c + pltpu.roll(x, LANES - half, 1) * a + pltpu.roll(x, half, 1) * b


def _rope_tables(T, kind):
    lane = jnp.arange(LANES)
    if kind == "mla":
        half = MLA_ROPE // 2
        rel = lane - MLA_NOPE
        on = (rel >= 0) & (rel < MLA_ROPE)
        d = MLA_ROPE
    else:
        half = SWA_HD // 2
        rel = lane % SWA_HD
        on = jnp.ones((LANES,), bool)
        d = SWA_HD
    first = on & (rel < half)
    second = on & (rel >= half)
    f = jnp.where(first, rel, rel - half).astype(F32)
    inv = jnp.exp(-math.log(ROPE_THETA) * f * (2.0 / d))
    ang = jnp.arange(T, dtype=F32)[:, None] * inv[None, :]
    cos, sin = jnp.cos(ang), jnp.sin(ang)
    c = jnp.where(on[None], cos, 1.0)
    a = jnp.where(first[None], -sin, 0.0)
    b = jnp.where(second[None], sin, 0.0)
    return c, a, b


def _fwd_in(x, g1, w_in_p, tm):
    T = x.shape[0]

    def body(x_ref, g_ref, w_ref, z_ref, h_ref):
        h = _rms(x_ref[...], g_ref[...]).astype(BF16)
        h_ref[...] = h
        z_ref[...] = _dot(h, w_ref[...])

    return _pcall(body, name="fwd_in", grid=(T // tm,),
                  in_specs=[_rows(tm, D), _full((1, D)), _full((D, NZ))],
                  out_specs=[_rows(tm, NZ), _rows(tm, D)],
                  out_shape=[SDS((T, NZ), F32), SDS((T, D), BF16)], sem=("parallel",))(x, g1, w_in_p)


def _fwd_qkv(z, gq, gkv, wqb, wkn, wv, tab_m, tab_s, tm):
    T = z.shape[0]
    wqb_t, wkn_t, wv_t = wqb.T, wkn.T, wv.T
    tab_mt = [t.T for t in tab_m]

    def body(qa_ref, sq_ref, skd_ref, svd_ref, kva_ref, kr_ref, gq_ref, gkv_ref, wkn_ref, wv_ref, wqbt_ref, wknt_ref, wvt_ref,
             cm_ref, am_ref, bm_ref, cmt_ref, amt_ref, bmt_ref, cs_ref, as_ref, bs_ref,
             qn_ref, kvn_ref, km_ref, vm_ref, qt_ref, kt_ref, vt_ref, qs_ref, ks_ref, vs_ref):
        qn = _rms(qa_ref[...], gq_ref[...])
        qn_ref[...] = qn.astype(BF16)
        kvn = _rms(kva_ref[...], gkv_ref[...])
        kvn_b = kvn.astype(BF16)
        kvn_ref[...] = kvn_b
        qn_t = qn.T.astype(BF16)
        kvn_t = kvn.T.astype(BF16)
        cm, am, bm = cm_ref[...], am_ref[...], bm_ref[...]
        cmt, amt, bmt = cmt_ref[...], amt_ref[...], bmt_ref[...]
        cs, as_, bs = cs_ref[...], as_ref[...], bs_ref[...]
        k_rope = _rope(kr_ref[...], cm, am, bm, MLA_ROPE // 2)
        k_rope_t = k_rope.T
        half = MLA_ROPE // 2
        vm_ref[...] = _dot(kvn_b, wv_ref[...]).astype(BF16)
        km_all = _dot(kvn_b, wkn_ref[...])
        v_t = _dot(wvt_ref[...], kvn_t)
        q_t = _dot(wqbt_ref[...], qn_t)
        k_t = _dot(wknt_ref[...], kvn_t)
        ones_row = jnp.where(lax.broadcasted_iota(jnp.int32, (64, tm), 0) == 0, 1.0, 0.0)
        for h in range(MLA_HEADS):
            sl = slice(LANES * h, LANES * (h + 1))
            vt_ref[0, sl, :] = jnp.concatenate([v_t[64 * h:64 * (h + 1)], ones_row], axis=0).astype(BF16)
            qh = q_t[sl]
            qt_ref[0, sl, :] = (qh * cmt + pltpu.roll(qh, LANES - half, 0) * amt + pltpu.roll(qh, half, 0) * bmt).astype(BF16)
            km_ref[:, sl] = (km_all[:, sl] + k_rope).astype(BF16)
            kt_ref[0, sl, :] = (k_t[sl] + k_rope_t).astype(BF16)
        for j in range(D // LANES):
            sl = slice(LANES * j, LANES * (j + 1))
            qs_ref[:, sl] = _rope(sq_ref[:, sl], cs, as_, bs, SWA_HD // 2).astype(BF16)
        for j in range(2):
            sl = slice(LANES * j, LANES * (j + 1))
            ks_ref[:, sl] = _rope(skd_ref[:, sl], cs, as_, bs, SWA_HD // 2).astype(BF16)
        vs_ref[...] = svd_ref[...].astype(BF16)

    tab = [_rows(tm, LANES)] * 3
    tab_t = [pl.BlockSpec((LANES, tm), lambda i: (0, i))] * 3
    return _pcall(body, name="fwd_qkv", grid=(T // tm,),
                  in_specs=[_rows(tm, 256, 12), _rows(tm, 1024, 0), _rows(tm, 256, 13), _rows(tm, 256, 14),
                            _rows(tm, 128, 30), _rows(tm, 128, 31), _full((1, Q_LORA)), _full((1, KV_LORA)),
                            _full((KV_LORA, 2048)), _full((KV_LORA, 1024)), _full((2048, Q_LORA)), _full((2048, KV_LORA)),
                            _full((1024, KV_LORA))] + tab + tab_t + tab,
                  out_specs=[_rows(tm, Q_LORA), _rows(tm, KV_LORA), _rows(tm, 2048), _rows(tm, 1024),
                             pl.BlockSpec((1, 2048, tm), lambda i: (i, 0, 0)), pl.BlockSpec((1, 2048, tm), lambda i: (i, 0, 0)),
                             pl.BlockSpec((1, 2048, tm), lambda i: (i, 0, 0)),
                             _rows(tm, 1024), _rows(tm, 256), _rows(tm, 256)],
                  out_shape=[SDS((T, Q_LORA), BF16), SDS((T, KV_LORA), BF16), SDS((T, 2048), BF16),
                             SDS((T, 1024), BF16), SDS((T // tm, 2048, tm), BF16), SDS((T // tm, 2048, tm), BF16),
                             SDS((T // tm, 2048, tm), BF16),
                             SDS((T, 1024), BF16), SDS((T, 256), BF16), SDS((T, 256), BF16)],
                  sem=("parallel",))(z, z, z, z, z, z, gq, gkv, wkn, wv, wqb_t, wkn_t, wv_t, *tab_m, *tab_mt, *tab_s)


def _mla_fwd(qt, km, vt, tb):
    T = km.shape[0]
    nb = T // tb
    cc = ATT_COLS

    def body(q_ref, k_ref, vt_ref, o_ref, l_ref, s_ref, p_ref, al_ref, m_ref, acc_ref):
        i = pl.program_id(1)
        m_ref[...] = jnp.full(m_ref.shape, NEG, F32)
        acc_ref[...] = jnp.zeros_like(acc_ref)
        p_ref[1] = jnp.zeros(p_ref.shape[1:], BF16)
        al_ref[1] = jnp.ones(al_ref.shape[1:], F32)
        key = lax.broadcasted_iota(jnp.int32, (tb, cc), 0)
        qry = lax.broadcasted_iota(jnp.int32, (tb, cc), 1)

        def scores(j, slot):
            off = pl.multiple_of(j * tb, tb)
            for hh in range(2):
                sl = slice(LANES * hh, LANES * (hh + 1))
                s_ref[slot, hh] = _dot(k_ref[pl.ds(off, tb), sl], q_ref[0, sl, :])

        def softmax(slot, diagonal):
            chains = [(hh, slice(cc * c, cc * (c + 1)), c) for hh in range(2) for c in range(tb // cc)]

            def scaled(hh, cols, c):
                t = s_ref[slot, hh, :, cols] * MLA_LOG2_SCALE
                return jnp.where(key <= qry + cc * c, t, NEG) if diagonal else t

            tops = []
            for hh, cols, c in chains:
                if diagonal:
                    top = jnp.max(scaled(hh, cols, c), axis=0, keepdims=True)
                else:
                    top = jnp.max(s_ref[slot, hh, :, cols], axis=0, keepdims=True) * MLA_LOG2_SCALE
                m_old = m_ref[hh, :, cols]
                mn = jnp.maximum(m_old, top)
                m_ref[hh, :, cols] = mn
                al_ref[slot, hh, :, cols] = jnp.exp2(m_old - mn)
                tops.append(mn)
            for (hh, cols, c), mn in zip(chains, tops):
                p_ref[slot, hh, :, cols] = jnp.exp2(scaled(hh, cols, c) - mn).astype(BF16)

        def accumulate(j, slot):
            for hh in range(2):
                acc_ref[hh] = al_ref[slot, hh] * acc_ref[hh] + _dot(vt_ref[j, LANES * hh:LANES * hh + VT_ROWS, :], p_ref[slot, hh])

        def step(t, carry):
            scores(2 * t + 1, 1)
            accumulate(jnp.maximum(2 * t - 1, 0), 1)
            softmax(0, False)
            scores(2 * t + 2, 0)
            accumulate(2 * t, 0)
            softmax(1, False)
            return carry

        scores(0, 0)
        lax.fori_loop(0, i // 2, step, 0)

        @pl.when(i % 2 == 1)
        def _():
            scores(i, 1)
            accumulate(jnp.maximum(i - 2, 0), 1)
            softmax(0, False)
            accumulate(i - 1, 0)
            softmax(1, True)
            accumulate(i, 1)

        @pl.when(i % 2 == 0)
        def _():
            accumulate(jnp.maximum(i - 1, 0), 1)
            softmax(0, True)
            accumulate(i, 0)
        den = [acc_ref[hh, 64:65, :] for hh in range(2)]
        o_ref[...] = jnp.concatenate([acc_ref[hh, 0:64, :] / den[hh] for hh in range(2)], axis=0).T
        sub = lax.broadcasted_iota(jnp.int32, (8, tb), 0)
        lse = [m_ref[hh] + jnp.log(den[hh]) * LOG2_E for hh in range(2)]
        l_ref[0, 0] = jnp.where(sub == 0, lse[0], jnp.where(sub == 1, lse[1], 0.0))

    return _pcall(body, name="mla_fwd", grid=(MLA_HEADS // 2, nb),
                  in_specs=[pl.BlockSpec((1, 256, tb), lambda p, i: (i, p, 0)), pl.BlockSpec((T, 256), lambda p, i: (0, p)),
                            pl.BlockSpec((nb, 2 * LANES, tb), lambda p, i: (0, p, 0))],
                  out_specs=[pl.BlockSpec((tb, LANES), lambda p, i: (i, p)),
                             pl.BlockSpec((1, 1, 8, tb), lambda p, i: (p, i, 0, 0))],
                  out_shape=[SDS((T, D), F32), SDS((MLA_HEADS // 2, nb, 8, tb), F32)],
                  scratch=[pltpu.VMEM((2, 2, tb, tb), F32), pltpu.VMEM((2, 2, tb, tb), BF16), pltpu.VMEM((2, 2, 1, tb), F32),
                           pltpu.VMEM((2, 1, tb), F32), pltpu.VMEM((2, VT_ROWS, tb), F32)],
                  sem=("parallel", "arbitrary"))(qt, km, vt)


def _swa_mask(n):
    row = lax.broadcasted_iota(jnp.int32, (WINDOW, 2 * WINDOW), 0)
    col = lax.broadcasted_iota(jnp.int32, (WINDOW, 2 * WINDOW), 1)
    rel = row - col + WINDOW
    return (rel >= 0) & (rel < WINDOW) & ((col >= WINDOW) | (n > 0))


def _swa_specs(T):
    nb = T // WINDOW
    cur = lambda w: pl.BlockSpec((WINDOW, w), lambda n: (n, 0))
    prev = lambda w: pl.BlockSpec((WINDOW, w), lambda n: (jnp.maximum(n - 1, 0), 0))
    return nb, cur, prev


def _swa_fwd(sinks, qs, ks, vs):
    T = qs.shape[0]
    nb, cur, prev = _swa_specs(T)

    def body(sink_ref, q_ref, kc_ref, kp_ref, vc_ref, vp_ref, o_ref, l_ref, kb_ref, vb_ref, s_ref, p_ref):
        n = pl.program_id(0)
        mask = _swa_mask(n)
        lo = lax.broadcasted_iota(jnp.int32, (WINDOW, LANES), 1) < 64
        hi = jnp.logical_not(lo)
        for g in range(2):
            gs = slice(LANES * g, LANES * (g + 1))
            kb_ref[g] = jnp.concatenate([kp_ref[:, gs], kc_ref[:, gs]], axis=0)
            vb_ref[g] = jnp.concatenate([vp_ref[:, gs], vc_ref[:, gs]], axis=0)
        for h in range(SWA_HEADS):
            qp = q_ref[:, LANES * (h // 2):LANES * (h // 2 + 1)]
            qh = jnp.where(lo if h % 2 == 0 else hi, qp, jnp.zeros_like(qp))
            s_ref[h] = _dot_nt(qh, kb_ref[h // 8])
        for j in range(SWA_HEADS // 2):
            sl = slice(LANES * j, LANES * (j + 1))
            lses = []
            for h in (2 * j, 2 * j + 1):
                s = jnp.where(mask, s_ref[h] * SWA_SCALE, NEG)
                sk = sink_ref[h]
                m = jnp.maximum(jnp.max(s, axis=1, keepdims=True), sk)
                e = jnp.exp(s - m)
                den = jnp.sum(e, axis=1, keepdims=True) + jnp.exp(sk - m)
                p_ref[h] = (e / den).astype(BF16)
                lses.append(jnp.broadcast_to(m + jnp.log(den), (WINDOW, LANES)))
            l_ref[:, sl] = jnp.where(lo, lses[0], lses[1])
        for j in range(SWA_HEADS // 2):
            vb = vb_ref[j // 4]
            o_ref[:, LANES * j:LANES * (j + 1)] = jnp.where(lo, _dot(p_ref[2 * j], vb), _dot(p_ref[2 * j + 1], vb))

    return _pcall(body, name="swa_fwd", grid=(nb,),
                  in_specs=[pl.BlockSpec(memory_space=pltpu.SMEM), cur(D), cur(256), prev(256), cur(256), prev(256)],
                  out_specs=[cur(D), cur(D)], out_shape=[SDS((T, D), F32)] * 2,
                  scratch=[pltpu.VMEM((2, 2 * WINDOW, LANES), BF16), pltpu.VMEM((2, 2 * WINDOW, LANES), BF16),
                           pltpu.VMEM((SWA_HEADS, WINDOW, 2 * WINDOW), F32), pltpu.VMEM((SWA_HEADS, WINDOW, 2 * WINDOW), BF16)],
                  sem=("parallel",))(sinks, qs, ks, ks, vs, vs)


def _fwd_mix(om, os_, z, x, wmu, wsu, wo, g2, tm):
    T = x.shape[0]

    def body(om_ref, os_ref, ga_ref, gb_ref, x_ref, wmu_ref, wsu_ref, wo_ref, g2_ref,
             y_ref, yo_ref, au_ref, bu_ref, x1_ref):
        au = _dot(om_ref[...].astype(BF16), wmu_ref[...])
        bu = _dot(os_ref[...].astype(BF16), wsu_ref[...])
        au_ref[...] = au
        bu_ref[...] = bu
        y = (_sigmoid(ga_ref[...]) * au + _sigmoid(gb_ref[...]) * bu).astype(BF16)
        y_ref[...] = y
        yo = _dot(y, wo_ref[...])
        yo_ref[...] = yo
        x1_ref[...] = x_ref[...] + _rms(yo, g2_ref[...])

    r = _rows(tm, D)
    w = _full((D, D))
    return _pcall(body, name="fwd_mix", grid=(T // tm,),
                  in_specs=[r, r, _rows(tm, D, 1), _rows(tm, D, 2), r, w, w, w, _full((1, D))],
                  out_specs=[r] * 5,
                  out_shape=[SDS((T, D), BF16), SDS((T, D), F32), SDS((T, D), F32), SDS((T, D), F32), SDS((T, D), F32)],
                  sem=("parallel",))(om, os_, z, z, x, wmu, wsu, wo, g2)


def _fwd_mlp_up(x1, g3, w1, tm):
    T = x1.shape[0]

    def body(x_ref, g_ref, w_ref, h_ref, u_ref):
        h = _rms(x_ref[...], g_ref[...]).astype(BF16)
        h_ref[...] = h
        u_ref[...] = jnp.square(jnp.maximum(_dot(h, w_ref[...]), 0.0)).astype(BF16)

    return _pcall(body, name="fwd_mlp_up", grid=(T // tm,),
                  in_specs=[_rows(tm, D), _full((1, D)), _full((D, D_FF))],
                  out_specs=[_rows(tm, D), _rows(tm, D_FF)],
                  out_shape=[SDS((T, D), BF16), SDS((T, D_FF), BF16)],
                  sem=("parallel",))(x1, g3, w1)


def _fwd_mlp_down(u, w2, x1, g4, tm):
    T = x1.shape[0]

    def body(u_ref, w_ref, x_ref, g_ref, d_ref, x2_ref):
        d = _dot(u_ref[...], w_ref[...])
        d_ref[...] = d
        x2_ref[...] = x_ref[...] + _rms(d, g_ref[...])

    return _pcall(body, name="fwd_mlp_down", grid=(T // tm,),
                  in_specs=[_rows(tm, D_FF), _full((D_FF, D)), _rows(tm, D), _full((1, D))],
                  out_specs=[_rows(tm, D), _rows(tm, D)], out_shape=[SDS((T, D), F32)] * 2,
                  sem=("parallel",))(u, w2, x1, g4)


def _ple_fwd_bwd(p, x2, tgt, wple, g5, wpg, tm):
    T = x2.shape[0]

    def body(p_ref, x2_ref, t_ref, wple_ref, g5_ref, wpg_ref, loss_ref, dx2_ref, dgt_ref, de0_ref, dg5_ref):
        @pl.when(pl.program_id(0) == 0)
        def _():
            loss_ref[...] = jnp.zeros_like(loss_ref)
            dg5_ref[...] = jnp.zeros_like(dg5_ref)

        e0 = _dot(p_ref[...].astype(BF16), wple_ref[...])
        g5 = g5_ref[...]
        r = lax.rsqrt(jnp.mean(e0 * e0, axis=-1, keepdims=True) + EPS)
        en = e0 * r
        e = en * g5
        x2 = x2_ref[...]
        s = _sigmoid(_dot(x2.astype(BF16), wpg_ref[...]))
        diff = x2 + s * e - t_ref[...]
        sq = jnp.sum(jnp.sum(diff * diff, axis=1, keepdims=True), axis=0, keepdims=True)
        loss_ref[...] += jnp.broadcast_to(sq * (0.5 / D), loss_ref.shape)
        dx3 = diff * (1.0 / D)
        de = dx3 * s
        dgt = (dx3 * e * s * (1.0 - s)).astype(BF16)
        dgt_ref[...] = dgt
        dn = de * g5
        de0_ref[...] = (r * (dn - en * jnp.mean(dn * en, axis=-1, keepdims=True))).astype(BF16)
        dg5_ref[...] += jnp.sum(de * en, axis=0, keepdims=True)
        dx2_ref[...] = dx3 + _dot_nt(dgt, wpg_ref[...])

    r = _rows(tm, D)
    return _pcall(body, name="ple_fwd_bwd", grid=(T // tm,),
                  in_specs=[_rows(tm, PLE), r, r, _full((PLE, D)), _full((1, D)), _full((D, D))],
                  out_specs=[_full((8, LANES)), r, r, r, _full((1, D))],
                  out_shape=[SDS((8, LANES), F32), SDS((T, D), F32), SDS((T, D), BF16), SDS((T, D), BF16), SDS((1, D), F32)],
                  sem=("arbitrary",))(p, x2, tgt, wple, g5, wpg)


def _bwd_mlp_down(dx2, d, g4, w2, u, tm):
    T = dx2.shape[0]

    def body(dx_ref, d_ref, g_ref, w_ref, u_ref, dd_ref, da_ref, dg_ref):
        @pl.when(pl.program_id(0) == 0)
        def _():
            dg_ref[...] = jnp.zeros_like(dg_ref)

        dd, dg = _rms_bwd(dx_ref[...], d_ref[...], g_ref[...])
        dg_ref[...] += dg
        ddb = dd.astype(BF16)
        dd_ref[...] = ddb
        du = _dot_nt(ddb, w_ref[...])
        da_ref[...] = (du * (2.0 * jnp.sqrt(u_ref[...].astype(F32)))).astype(BF16)

    return _pcall(body, name="bwd_mlp_down", grid=(T // tm,),
                  in_specs=[_rows(tm, D), _rows(tm, D), _full((1, D)), _full((D_FF, D)), _rows(tm, D_FF)],
                  out_specs=[_rows(tm, D), _rows(tm, D_FF), _full((1, D))],
                  out_shape=[SDS((T, D), BF16), SDS((T, D_FF), BF16), SDS((1, D), F32)],
                  sem=("arbitrary",))(dx2, d, g4, w2, u)


def _bwd_mlp_up(da, w1, x1, g3, dx2, tm):
    T = dx2.shape[0]

    def body(da_ref, w_ref, x_ref, g_ref, dx2_ref, dx1_ref, dg_ref):
        @pl.when(pl.program_id(0) == 0)
        def _():
            dg_ref[...] = jnp.zeros_like(dg_ref)

        dh = _dot_nt(da_ref[...], w_ref[...])
        dx, dg = _rms_bwd(dh, x_ref[...], g_ref[...])
        dg_ref[...] += dg
        dx1_ref[...] = dx2_ref[...] + dx

    return _pcall(body, name="bwd_mlp_up", grid=(T // tm,),
                  in_specs=[_rows(tm, D_FF), _full((D, D_FF)), _rows(tm, D), _full((1, D)), _rows(tm, D)],
                  out_specs=[_rows(tm, D), _full((1, D))],
                  out_shape=[SDS((T, D), F32), SDS((1, D), F32)], sem=("arbitrary",))(da, w1, x1, g3, dx2)


def _bwd_mix(dx1, yo, g2, wo, z, au, bu, wmu, wsu, om, tm):
    T = dx1.shape[0]

    def body(dx_ref, yo_ref, g_ref, wo_ref, ga_ref, gb_ref, au_ref, bu_ref, wmu_ref, wsu_ref, om_ref,
             dyo_ref, dg_ref, dau_ref, dbu_ref, dga_ref, dgb_ref, dos_ref, dl_ref, dot_ref):
        @pl.when(pl.program_id(0) == 0)
        def _():
            dg_ref[...] = jnp.zeros_like(dg_ref)

        dyo, dg = _rms_bwd(dx_ref[...], yo_ref[...], g_ref[...])
        dg_ref[...] += dg
        dyob = dyo.astype(BF16)
        dyo_ref[...] = dyob
        dy = _dot_nt(dyob, wo_ref[...])
        sa = _sigmoid(ga_ref[...])
        sb = _sigmoid(gb_ref[...])
        dau = (dy * sa).astype(BF16)
        dbu = (dy * sb).astype(BF16)
        dau_ref[...] = dau
        dbu_ref[...] = dbu
        dga_ref[...] = (dy * au_ref[...] * sa * (1.0 - sa)).astype(BF16)
        dgb_ref[...] = (dy * bu_ref[...] * sb * (1.0 - sb)).astype(BF16)
        dom = _dot_nt(dau, wmu_ref[...])
        dos_ref[...] = _dot_nt(dbu, wsu_ref[...])
        prod = dom * om_ref[...]
        sub = lax.broadcasted_iota(jnp.int32, (8, tm), 0)
        for pr in range(MLA_HEADS // 2):
            sl = slice(LANES * pr, LANES * (pr + 1))
            pt = prod[:, sl].T
            d0 = jnp.sum(pt[0:64], axis=0, keepdims=True)
            d1 = jnp.sum(pt[64:128], axis=0, keepdims=True)
            dl_ref[pr, 0] = jnp.where(sub == 0, d0, jnp.where(sub == 1, d1, 0.0))
            dot_ref[0, sl, :] = dom[:, sl].T.astype(BF16)

    r = _rows(tm, D)
    w = _full((D, D))
    return _pcall(body, name="bwd_mix", grid=(T // tm,),
                  in_specs=[r, r, _full((1, D)), w, _rows(tm, D, 1), _rows(tm, D, 2), r, r, w, w, r],
                  out_specs=[r, _full((1, D)), r, r, r, r, r, pl.BlockSpec((MLA_HEADS // 2, 1, 8, tm), lambda i: (0, i, 0, 0)),
                             pl.BlockSpec((1, D, tm), lambda i: (i, 0, 0))],
                  out_shape=[SDS((T, D), BF16), SDS((1, D), F32), SDS((T, D), BF16), SDS((T, D), BF16), SDS((T, D), BF16),
                             SDS((T, D), BF16), SDS((T, D), F32), SDS((MLA_HEADS // 2, T // tm, 8, tm), F32),
                             SDS((T // tm, D, tm), BF16)],
                  sem=("arbitrary",))(dx1, yo, g2, wo, z, z, au, bu, wmu, wsu, om)


def _mla_bwd(qt, km, kt, vm, dot, lse, delta, tb):
    T = km.shape[0]
    nb = T // tb
    cc = ATT_COLS

    def body(qt_ref, k_ref, kt_ref, v_ref, dot_ref, l_ref, dl_ref, dqt_ref, dkt_ref, dvt_ref,
             s_ref, dp_ref, p_ref, ds_ref, vh_ref):
        j = pl.program_id(1)

        @pl.when(j == 0)
        def _():
            dqt_ref[...] = jnp.zeros_like(dqt_ref)

        dkt_ref[...] = jnp.zeros_like(dkt_ref)
        dvt_ref[...] = jnp.zeros_like(dvt_ref)
        lo = lax.broadcasted_iota(jnp.int32, (tb, LANES), 1) < 64
        key = lax.broadcasted_iota(jnp.int32, (tb, cc), 0)
        qry = lax.broadcasted_iota(jnp.int32, (tb, cc), 1)
        v = v_ref[...]
        vh_ref[0] = jnp.where(lo, v, jnp.zeros_like(v))
        vh_ref[1] = jnp.where(lo, jnp.zeros_like(v), v)

        def scores(i, slot):
            for hh in range(2):
                sl = slice(LANES * hh, LANES * (hh + 1))
                s_ref[slot, hh] = _dot(k_ref[:, sl], qt_ref[i, sl, :])
                dp_ref[slot, hh] = _dot(vh_ref[hh], dot_ref[i])

        def grads(i, slot, diagonal):
            lse_i = l_ref[0, i]
            delta_i = dl_ref[0, i]
            for hh in range(2):
                for c in range(tb // cc):
                    cols = slice(cc * c, cc * (c + 1))
                    p = jnp.exp2(s_ref[slot, hh, :, cols] * MLA_LOG2_SCALE - lse_i[hh:hh + 1, cols])
                    if diagonal:
                        p = jnp.where(key <= qry + cc * c, p, 0.0)
                    p_ref[hh, :, cols] = p.astype(BF16)
                    ds_ref[hh, :, cols] = (p * (dp_ref[slot, hh, :, cols] - delta_i[hh:hh + 1, cols]) * MLA_SCALE).astype(BF16)
            for hh in range(2):
                sl = slice(LANES * hh, LANES * (hh + 1))
                half = slice(64 * hh, 64 * (hh + 1))
                dvt_ref[0, half, :] += _dot_nt(dot_ref[i, half, :], p_ref[hh])
                real = slice(LANES * hh, LANES * hh + MLA_NOPE + MLA_ROPE)
                dkt_ref[0, real, :] += _dot_nt(qt_ref[i, real, :], ds_ref[hh])
                dqt_ref[i, real, :] += _dot(kt_ref[0, real, :], ds_ref[hh])

        n_off = nb - 1 - j

        def step(u, carry):
            i0 = j + 1 + 2 * u
            scores(i0 + 1, 1)
            grads(i0, 0, False)
            scores(jnp.where(i0 + 2 < nb, i0 + 2, j), 0)
            grads(i0 + 1, 1, False)
            return carry

        scores(jnp.where(n_off > 0, j + 1, j), 0)
        lax.fori_loop(0, n_off // 2, step, 0)

        @pl.when(n_off % 2 == 1)
        def _():
            scores(j, 1)
            grads(nb - 1, 0, False)
            grads(j, 1, True)

        @pl.when(n_off % 2 == 0)
        def _():
            grads(j, 0, True)

    blk = lambda w: pl.BlockSpec((tb, w), lambda p, j: (j, p))
    stat = pl.BlockSpec((1, nb, 8, tb), lambda p, j: (p, 0, 0, 0))
    pair_t = lambda w: pl.BlockSpec((nb, w, tb), lambda p, j: (0, p, 0))
    blk_t = lambda w: pl.BlockSpec((1, w, tb), lambda p, j: (j, p, 0))
    return _pcall(body, name="mla_bwd", grid=(MLA_HEADS // 2, nb),
                  in_specs=[pair_t(256), blk(256), blk_t(256), blk(LANES), pair_t(LANES), stat, stat],
                  out_specs=[pair_t(256), blk_t(256), blk_t(LANES)],
                  out_shape=[SDS((nb, 2048, tb), F32), SDS((nb, 2048, tb), F32), SDS((nb, D, tb), F32)],
                  scratch=[pltpu.VMEM((2, 2, tb, tb), F32), pltpu.VMEM((2, 2, tb, tb), F32), pltpu.VMEM((2, tb, tb), BF16),
                           pltpu.VMEM((2, tb, tb), BF16), pltpu.VMEM((2, tb, LANES), BF16)],
                  sem=("parallel", "arbitrary"))(qt, km, kt, vm, dot, lse, delta)


def _swa_bwd(sinks, qs, ks, vs, do, o, lse):
    T = qs.shape[0]
    nb, cur, prev = _swa_specs(T)

    def body(sink_ref, q_ref, kc_ref, kp_ref, vc_ref, vp_ref, do_ref, o_ref, l_ref,
             dq_ref, dkc_ref, dkp_ref, dvc_ref, dvp_ref, dsink_ref, kb_ref, vb_ref, s_ref, dp_ref, p_ref, ds_ref):
        n = pl.program_id(0)

        @pl.when(n == 0)
        def _():
            dsink_ref[...] = jnp.zeros_like(dsink_ref)

        mask = _swa_mask(n)
        lo = lax.broadcasted_iota(jnp.int32, (WINDOW, LANES), 1) < 64
        hi = jnp.logical_not(lo)
        lane8 = lax.broadcasted_iota(jnp.int32, (8, LANES), 1)
        for g in range(2):
            gs = slice(LANES * g, LANES * (g + 1))
            kb_ref[g] = jnp.concatenate([kp_ref[:, gs], kc_ref[:, gs]], axis=0)
            vb_ref[g] = jnp.concatenate([vp_ref[:, gs], vc_ref[:, gs]], axis=0)

        def head(h):
            sl = slice(LANES * (h // 2), LANES * (h // 2 + 1))
            hm = lo if h % 2 == 0 else hi
            qp = q_ref[:, sl]
            return hm, sl, jnp.where(hm, qp, jnp.zeros_like(qp)), jnp.where(hm, do_ref[:, sl], 0.0).astype(BF16)

        for h in range(SWA_HEADS):
            _, _, qh, dom = head(h)
            s_ref[h] = _dot_nt(qh, kb_ref[h // 8])
            dp_ref[h] = _dot_nt(dom, vb_ref[h // 8])
        dsink = jnp.zeros((8, LANES), F32)
        for h in range(SWA_HEADS):
            hm, sl, _, _ = head(h)
            lse_h = jnp.max(jnp.where(hm, l_ref[:, sl], -jnp.inf), axis=1, keepdims=True)
            delta = jnp.sum(jnp.where(hm, do_ref[:, sl] * o_ref[:, sl], 0.0), axis=1, keepdims=True)
            p = jnp.exp(jnp.where(mask, s_ref[h] * SWA_SCALE, NEG) - lse_h)
            p_ref[h] = p.astype(BF16)
            ds_ref[h] = (p * (dp_ref[h] - delta) * SWA_SCALE).astype(BF16)
            d_sink = -jnp.sum(jnp.exp(sink_ref[h] - lse_h) * delta, axis=0, keepdims=True)
            dsink = dsink + jnp.where(lane8 == h, d_sink, 0.0)
        dsink_ref[...] += dsink
        for g in range(2):
            gs = slice(LANES * g, LANES * (g + 1))
            dkb = jnp.zeros((2 * WINDOW, LANES), F32)
            dvb = jnp.zeros((2 * WINDOW, LANES), F32)
            for j in range(4 * g, 4 * g + 4):
                dqs = []
                for h in (2 * j, 2 * j + 1):
                    _, _, qh, dom = head(h)
                    dvb = dvb + _dot_tn(p_ref[h], dom)
                    dkb = dkb + _dot_tn(ds_ref[h], qh)
                    dqs.append(_dot(ds_ref[h], kb_ref[g]))
                dq_ref[:, LANES * j:LANES * (j + 1)] = jnp.where(lo, dqs[0], dqs[1])
            dkp_ref[:, gs] = dkb[:WINDOW]
            dkc_ref[:, gs] = dkb[WINDOW:]
            dvp_ref[:, gs] = dvb[:WINDOW]
            dvc_ref[:, gs] = dvb[WINDOW:]

    band = pltpu.VMEM((2, 2 * WINDOW, LANES), BF16)
    return _pcall(body, name="swa_bwd", grid=(nb,),
                  in_specs=[pl.BlockSpec(memory_space=pltpu.SMEM), cur(D), cur(256), prev(256), cur(256), prev(256),
                            cur(D), cur(D), cur(D)],
                  out_specs=[cur(D), cur(256), cur(256), cur(256), cur(256), _full((8, LANES))],
                  out_shape=[SDS((T, D), F32), SDS((T, 256), F32), SDS((T, 256), F32), SDS((T, 256), F32), SDS((T, 256), F32),
                             SDS((8, LANES), F32)],
                  scratch=[band, band, pltpu.VMEM((SWA_HEADS, WINDOW, 2 * WINDOW), F32),
                           pltpu.VMEM((SWA_HEADS, WINDOW, 2 * WINDOW), F32), pltpu.VMEM((SWA_HEADS, WINDOW, 2 * WINDOW), BF16),
                           pltpu.VMEM((SWA_HEADS, WINDOW, 2 * WINDOW), BF16)],
                  sem=("arbitrary",))(sinks, qs, ks, ks, vs, vs, do, o, lse)


def _bwd_qkv(dqm, dkm, dvm, dqs, dkc, dkp, dvc, dvp, z, gq, gkv, wqb, wkn, wv, tab_m, tab_s):
    T = z.shape[0]
    tm = WINDOW
    nb = T // tm
    per = dqm.shape[2] // tm

    tab_mt = [t.T for t in tab_m]
    half = MLA_ROPE // 2

    def rope_t(v, c, a, b):
        return v * c + pltpu.roll(v, LANES - half, 0) * a + pltpu.roll(v, half, 0) * b

    def rms_bwd_t(dy, x, g):
        r = lax.rsqrt(jnp.mean(x * x, axis=0, keepdims=True) + EPS)
        xn = x * r
        dn = dy * g
        return r * (dn - xn * jnp.mean(dn * xn, axis=0, keepdims=True)), jnp.sum(dy * xn, axis=1, keepdims=True)

    def body(dqm_ref, dkm_ref, dvm_ref, dqs_ref, dkc_ref, dkp_ref, dvc_ref, dvp_ref, qa_ref, kva_ref, gq_ref, gkv_ref,
             wqb_ref, wkn_ref, wv_ref, cmt_ref, amt_ref, bmt_ref, cs_ref, as_ref, bs_ref,
             dq_out, dkn_out, dv_out, dsq_ref, drest_ref, dgq_ref, dgkv_ref):
        i = pl.program_id(0)

        @pl.when(i == 0)
        def _():
            dgq_ref[...] = jnp.zeros_like(dgq_ref)
            dgkv_ref[...] = jnp.zeros_like(dgkv_ref)

        cmt, amt, bmt = cmt_ref[...], -amt_ref[...], -bmt_ref[...]
        cs, as_, bs = cs_ref[...], -as_ref[...], -bs_ref[...]
        row = lax.broadcasted_iota(jnp.int32, (LANES, tm), 0)
        nope = row < MLA_NOPE
        roped = jnp.logical_and(row >= MLA_NOPE, row < MLA_NOPE + MLA_ROPE)
        dkr = jnp.zeros((LANES, tm), F32)
        for h in range(MLA_HEADS):
            sl = slice(LANES * h, LANES * (h + 1))
            dq_out[0, sl, :] = rope_t(dqm_ref[0, sl, :], cmt, amt, bmt).astype(BF16)
            dk_h = dkm_ref[0, sl, :]
            dkn_out[0, sl, :] = jnp.where(nope, dk_h, 0.0).astype(BF16)
            dkr = dkr + jnp.where(roped, dk_h, 0.0)
        dv_out[0] = dvm_ref[0].astype(BF16)
        dqn = _dot(wqb_ref[...], dq_out[0])
        dkvn = _dot(wkn_ref[...], dkn_out[0]) + _dot(wv_ref[...], dv_out[0])
        dqa, dgq = rms_bwd_t(dqn, qa_ref[...].T, gq_ref[...])
        dkva, dgkv = rms_bwd_t(dkvn, kva_ref[...].T, gkv_ref[...])
        dgq_ref[...] += dgq
        dgkv_ref[...] += dgkv
        for j in range(D // LANES):
            sl = slice(LANES * j, LANES * (j + 1))
            dsq_ref[:, sl] = _rope(dqs_ref[:, sl], cs, as_, bs, SWA_HD // 2).astype(BF16)
        keep = (i < nb - 1).astype(F32)
        drest_ref[:, 0:256] = dqa.T.astype(BF16)
        for j in range(2):
            sl = slice(LANES * j, LANES * (j + 1))
            dk = dkc_ref[:, sl] + keep * dkp_ref[:, sl]
            drest_ref[:, 256 + LANES * j:256 + LANES * (j + 1)] = _rope(dk, cs, as_, bs, SWA_HD // 2).astype(BF16)
        drest_ref[:, 512:768] = (dvc_ref[...] + keep * dvp_ref[...]).astype(BF16)
        drest_ref[:, 768:896] = dkva.T.astype(BF16)
        drest_ref[:, 896:1024] = rope_t(dkr, cmt, amt, bmt).T.astype(BF16)

    nxt = pl.BlockSpec((tm, 256), lambda i: (jnp.minimum(i + 1, nb - 1), 0))
    tab = [_rows(tm, LANES)] * 3
    tab_t = [pl.BlockSpec((LANES, tm), lambda i: (0, i))] * 3
    blk_t = lambda w: pl.BlockSpec((1, w, tm), lambda i: (i // per, 0, i % per))
    return _pcall(body, name="bwd_qkv", grid=(nb,),
                  in_specs=[blk_t(2048), blk_t(2048), blk_t(1024), _rows(tm, 1024), _rows(tm, 256), nxt,
                            _rows(tm, 256), nxt, _rows(tm, 256, 12), _rows(tm, 128, 30), _full((Q_LORA, 1)), _full((KV_LORA, 1)),
                            _full((Q_LORA, 2048)), _full((KV_LORA, 2048)), _full((KV_LORA, 1024))] + tab_t + tab,
                  out_specs=[blk_t(2048), blk_t(2048), blk_t(1024), _rows(tm, 1024), _rows(tm, 1024),
                             _full((Q_LORA, 1)), _full((KV_LORA, 1))],
                  out_shape=[SDS(dqm.shape, BF16), SDS(dkm.shape, BF16), SDS(dvm.shape, BF16), SDS((T, 1024), BF16),
                             SDS((T, 1024), BF16), SDS((Q_LORA, 1), F32), SDS((KV_LORA, 1), F32)],
                  sem=("arbitrary",))(dqm, dkm, dvm, dqs, dkc, dkp, dvc, dvp, z, z, gq.reshape(Q_LORA, 1),
                                      gkv.reshape(KV_LORA, 1), wqb, wkn, wv, *tab_mt, *tab_s)


def _bwd_in(dsq, dga, dgb, drest, w_in_p, x, g1, dx1, tm):
    T = x.shape[0]

    def body(a_ref, b_ref, c_ref, d_ref, w_ref, x_ref, g_ref, dx1_ref, dx_ref, dg_ref):
        @pl.when(pl.program_id(0) == 0)
        def _():
            dg_ref[...] = jnp.zeros_like(dg_ref)

        dh = (_dot_nt(a_ref[...], w_ref[:, 0:1024]) + _dot_nt(b_ref[...], w_ref[:, 1024:2048])
              + _dot_nt(c_ref[...], w_ref[:, 2048:3072]) + _dot_nt(d_ref[...], w_ref[:, 3072:4096]))
        dx, dg = _rms_bwd(dh, x_ref[...], g_ref[...])
        dg_ref[...] += dg
        dx_ref[...] = dx1_ref[...] + dx

    r = _rows(tm, D)
    return _pcall(body, name="bwd_in", grid=(T // tm,),
                  in_specs=[r, r, r, r, _full((D, NZ)), r, _full((1, D)), r],
                  out_specs=[r, _full((1, D))], out_shape=[SDS((T, D), F32), SDS((1, D), F32)],
                  sem=("arbitrary",))(dsq, dga, dgb, drest, w_in_p, x, g1, dx1)


def _wgrad(a, g, name, into=None):
    T, K = a.shape
    N = g.shape[1]
    tk, tn, tt = min(K, 1024), min(N, 1024), min(T, 1024)
    if into is not None:
        buf, weight = into
        _, row0, lane0 = PACK_AT[weight]
        shard = {n: (r, c) for n, r, c in BIG}[weight]
        assert lane0 == 0 and shard[1] == D and tk % shard[0] == 0
        per_step = tk // shard[0]
    assert K % tk == 0 and N % tn == 0 and T % tt == 0, (a.shape, g.shape)
    steps = T // tt

    def body(a_ref, g_ref, *rest):
        o_ref, acc_ref = rest[-2:]
        t = pl.program_id(2)

        @pl.when(t == 0)
        def _():
            acc_ref[...] = jnp.zeros_like(acc_ref)

        acc_ref[...] += _dot_tn(a_ref[...].astype(BF16), g_ref[...].astype(BF16))

        @pl.when(t == steps - 1)
        def _():
            o_ref[...] = acc_ref[...].astype(o_ref.dtype).reshape(o_ref.shape)

    in_specs = [pl.BlockSpec((tt, tk), lambda k, n, t: (t, k)), pl.BlockSpec((tt, tn), lambda k, n, t: (t, n))]
    if into is None:
        return _pcall(body, name=name, grid=(K // tk, N // tn, steps), in_specs=in_specs,
                      out_specs=pl.BlockSpec((tk, tn), lambda k, n, t: (k, n)), out_shape=SDS((K, N), F32),
                      scratch=[pltpu.VMEM((tk, tn), F32)], sem=("parallel", "parallel", "arbitrary"))(a, g)
    assert row0 % shard[0] == 0 and (K // tk) * (N // tn) * per_step == N_CHIPS
    return _pcall(body, name=name, grid=(K // tk, N // tn, steps), in_specs=in_specs + [ANY],
                  out_specs=pl.BlockSpec((per_step, shard[0], tn), lambda k, n, t: (k + n, row0 // shard[0], 0)),
                  out_shape=SDS(buf.shape, buf.dtype),
                  scratch=[pltpu.VMEM((tk, tn), F32)], sem=("parallel", "parallel", "arbitrary"), aliases={2: 0})(a, g, buf)


def _wgrad_t(at, g, name):
    nblk, K, tt = at.shape
    N = g.shape[1]
    tk = min(K, 1024)
    per_step = 4 if nblk % 4 == 0 else 1
    assert K % tk == 0 and g.shape[0] == nblk * tt

    def body(a_ref, g_ref, o_ref):
        @pl.when(pl.program_id(1) == 0)
        def _():
            o_ref[...] = jnp.zeros_like(o_ref)

        acc = _dot(a_ref[0], g_ref[0:tt, :].astype(BF16))
        for b in range(1, per_step):
            acc = acc + _dot(a_ref[b], g_ref[tt * b:tt * (b + 1), :].astype(BF16))
        o_ref[...] += acc

    return _pcall(body, name=name, grid=(K // tk, nblk // per_step),
                  in_specs=[pl.BlockSpec((per_step, tk, tt), lambda k, t: (t, k, 0)),
                            pl.BlockSpec((per_step * tt, N), lambda k, t: (t, 0))],
                  out_specs=pl.BlockSpec((tk, N), lambda k, t: (k, 0)), out_shape=SDS((K, N), F32),
                  sem=("parallel", "arbitrary"))(at, g)


def _adamw(w, packed_g, m, v, name):
    _, R, C = w.shape
    _, row0, lane0 = PACK_AT[name]
    tr = min(R, 256 if row0 % 256 == 0 else 128)
    assert row0 % tr == 0 and R % tr == 0

    def body(w_ref, g_ref, m_ref, v_ref, go_ref, d_ref, m2_ref, v2_ref):
        g_ = g_ref[:, lane0:lane0 + C]
        go_ref[0] = g_
        m2 = ADAM_B1 * m_ref[0] + (1.0 - ADAM_B1) * g_
        v2 = ADAM_B2 * v_ref[0] + (1.0 - ADAM_B2) * jnp.square(g_)
        m_hat = m2 / (1.0 - ADAM_B1 ** ADAM_STEP)
        v_hat = v2 / (1.0 - ADAM_B2 ** ADAM_STEP)
        d_ref[0] = -ADAM_LR * (m_hat / (jnp.sqrt(v_hat) + ADAM_EPS) + ADAM_WD * w_ref[0])
        m2_ref[0] = m2
        v2_ref[0] = v2

    r = pl.BlockSpec((1, tr, C), lambda i: (0, i, 0))
    return _pcall(body, name="adamw_" + name, grid=(R // tr,),
                  in_specs=[r, pl.BlockSpec((tr, D), lambda i: (row0 // tr + i, 0)), r, r], out_specs=[r] * 4,
                  out_shape=[SDS((1, R, C), F32)] * 4, sem=("parallel",))(w, packed_g, m, v)


def _adamw_small(w, parts, m, v):
    def body(w_ref, p_ref, m_ref, v_ref, g_ref, d_ref, m2_ref, v2_ref):
        g_ = p_ref[0]
        for k in range(1, N_DEV):
            g_ = g_ + p_ref[k]
        g_ref[...] = g_
        m2 = ADAM_B1 * m_ref[...] + (1.0 - ADAM_B1) * g_
        v2 = ADAM_B2 * v_ref[...] + (1.0 - ADAM_B2) * jnp.square(g_)
        m_hat = m2 / (1.0 - ADAM_B1 ** ADAM_STEP)
        v_hat = v2 / (1.0 - ADAM_B2 ** ADAM_STEP)
        d_ref[...] = -ADAM_LR * (m_hat / (jnp.sqrt(v_hat) + ADAM_EPS) + ADAM_WD * w_ref[...])
        m2_ref[...] = m2
        v2_ref[...] = v2

    s = _full((8, D))
    return _pcall(body, name="adamw_small", grid=(1,), in_specs=[s, _full((N_DEV, 8, D)), s, s], out_specs=[s] * 4,
                  out_shape=[SDS((8, D), F32)] * 4, sem=("arbitrary",))(w, parts, m, v)


ANY = pl.BlockSpec(memory_space=pl.ANY)


def _place():
    x, y, c = lax.axis_index("x"), lax.axis_index("y"), lax.axis_index("c")
    chips = [(1 - x, y), (x, 1 - y), (1 - x, 1 - y)]
    return x, y, c, chips


def _all_gather(wpk):
    rows = wpk.shape[0]
    HALF = rows // 2
    assert HALF % 16 == 0

    def body(in_ref, out_ref, send_sems, recv_sems):
        x, y, c, chips = _place()
        half = pl.ds(pl.multiple_of(c * HALF, 16), HALF)
        other = pl.ds(pl.multiple_of((1 - c) * HALF, 16), HALF)

        def copy(k, src, dst, to):
            return pltpu.make_async_remote_copy(src_ref=src, dst_ref=dst, send_sem=send_sems.at[k], recv_sem=recv_sems.at[k],
                                                device_id=to, device_id_type=MESH)

        first = [copy(k, in_ref.at[half], out_ref.at[2 * x + y, half], (cx, cy, c)) for k, (cx, cy) in enumerate(chips)]
        for cp in first:
            cp.start()
        passed = []
        for k, (cx, cy) in enumerate(chips):
            slot = out_ref.at[2 * cx + cy, half]
            copy(k, slot, slot, (x, y, c)).wait_recv()
            fwd = copy(3 + k, slot, slot, (x, y, 1 - c))
            fwd.start()
            passed.append(fwd)
        for k, (cx, cy) in enumerate(chips):
            slot = out_ref.at[2 * cx + cy, other]
            copy(3 + k, slot, slot, (x, y, c)).wait_recv()
        for cp in first + passed:
            cp.wait_send()

    return _pcall(body, name="all_gather_weights", in_specs=[ANY], out_specs=ANY,
                  out_shape=SDS((N_CHIPS, rows, D), BF16),
                  scratch=[pltpu.SemaphoreType.DMA((6,)), pltpu.SemaphoreType.DMA((6,))])(wpk)


HBM = pl.BlockSpec(memory_space=pltpu.HBM)
SEM = pl.BlockSpec(memory_space=pltpu.SEMAPHORE)
DATAFLOW = pltpu.SideEffectType.DATAFLOW_SIDE_EFFECTING


def _in_hbm(a):
    return pltpu.with_memory_space_constraint(a, pltpu.HBM)


def _gather_late_start(wpk, after):
    rows = wpk.shape[0]

    def body(in_ref, land_ref, after_ref, send_sems, recv_sems, in_thru, land_thru, token):
        x, y, c, chips = _place()
        for k, (cx, cy) in enumerate(chips):
            pltpu.make_async_remote_copy(src_ref=in_ref, dst_ref=land_ref.at[2 * x + y], send_sem=send_sems.at[k],
                                         recv_sem=recv_sems.at[k], device_id=(cx, cy, c), device_id_type=MESH).start()
        token[...] = jnp.zeros_like(token)

    return pl.pallas_call(
        body, name="gather_late_start",
        out_shape=(pltpu.SemaphoreType.DMA((3,)), pltpu.SemaphoreType.DMA((3,)), pltpu.HBM(wpk.shape, wpk.dtype),
                   pltpu.HBM((N_CHIPS, rows, D), wpk.dtype), SDS((8, LANES), F32)),
        in_specs=(HBM, HBM, ANY), out_specs=(SEM, SEM, HBM, HBM, pl.BlockSpec(memory_space=pltpu.VMEM)),
        input_output_aliases={0: 2, 1: 3}, compiler_params=pltpu.CompilerParams(has_side_effects=DATAFLOW),
    )(_in_hbm(wpk), _in_hbm(lax.empty((N_CHIPS, rows, D), wpk.dtype)), after)


def _gather_late_wait(send_sems, recv_sems, in_thru, land_thru, after):
    def body(in_ref, land_ref, send_sems, recv_sems, after_ref, after2_ref, in_dead, got_ref):
        x, y, c, chips = _place()
        for k, (cx, cy) in enumerate(chips):
            cp = pltpu.make_async_remote_copy(src_ref=in_ref, dst_ref=land_ref.at[2 * cx + cy], send_sem=send_sems.at[k],
                                              recv_sem=recv_sems.at[k], device_id=(cx, cy, c), device_id_type=MESH)
            cp.wait_send()
            cp.wait_recv()

    return pl.pallas_call(
        body, name="gather_late_wait",
        out_shape=(pltpu.HBM(in_thru.shape, in_thru.dtype), pltpu.HBM(land_thru.shape, land_thru.dtype)),
        in_specs=(HBM, HBM, SEM, SEM, ANY, ANY), out_specs=(HBM, HBM), input_output_aliases={0: 0, 1: 1},
        compiler_params=pltpu.CompilerParams(has_side_effects=DATAFLOW),
    )(in_thru, land_thru, send_sems, recv_sems, *after)[1]


def _rs_sibling(gpk):
    HALF = gpk.shape[1] // 2

    def body(in_ref, out_ref, send_sem, recv_sem):
        x, y, c, _ = _place()
        theirs = pl.ds(pl.multiple_of((1 - c) * HALF, 8), HALF)
        cp = pltpu.make_async_remote_copy(src_ref=in_ref.at[:, theirs], dst_ref=out_ref, send_sem=send_sem, recv_sem=recv_sem,
                                          device_id=(x, y, 1 - c), device_id_type=MESH)
        cp.start()
        cp.wait()

    return _pcall(body, name="rs_sibling", in_specs=[ANY], out_specs=ANY, out_shape=SDS((N_CHIPS, HALF, D), F32),
                  scratch=[pltpu.SemaphoreType.DMA, pltpu.SemaphoreType.DMA])(gpk)


def _rs_add_sibling(cidx, gpk, got):
    HALF = got.shape[1]
    th = HALF // 4
    nh = HALF // th
    assert th % 16 == 0

    def body(c_ref, a_ref, b_ref, o_ref):
        o_ref[...] = (a_ref[...] + b_ref[...]).astype(BF16)

    gs = pltpu.PrefetchScalarGridSpec(
        num_scalar_prefetch=1, grid=(N_CHIPS, nh),
        in_specs=[pl.BlockSpec((1, th, D), lambda j, i, c: (j, c[0] * nh + i, 0)), pl.BlockSpec((1, th, D), lambda j, i, c: (j, i, 0))],
        out_specs=pl.BlockSpec((1, th, D), lambda j, i, c: (j, i, 0)))
    return pl.pallas_call(body, name="rs_add_sibling", grid_spec=gs, out_shape=SDS((N_CHIPS, HALF, D), BF16),
                          compiler_params=pltpu.CompilerParams(dimension_semantics=("parallel", "parallel"),
                                                               vmem_limit_bytes=48 << 20))(cidx, gpk, got)


def _rs_chips_start(part, small, after):
    def body(p_ref, s_ref, land_ref, sland_ref, after_ref, send_sems, recv_sems, p_thru, s_thru, land_thru, sland_thru, token):
        x, y, c, chips = _place()
        for k, (cx, cy) in enumerate(chips):
            pltpu.make_async_remote_copy(src_ref=p_ref.at[2 * cx + cy], dst_ref=land_ref.at[2 * x + y], send_sem=send_sems.at[k],
                                         recv_sem=recv_sems.at[k], device_id=(cx, cy, c), device_id_type=MESH).start()
        peers = [(x, y, 1 - c)] + [(cx, cy, c) for cx, cy in chips] + [(cx, cy, 1 - c) for cx, cy in chips]
        for k, to in enumerate(peers):
            pltpu.make_async_remote_copy(src_ref=s_ref, dst_ref=sland_ref.at[4 * x + 2 * y + c], send_sem=send_sems.at[3 + k],
                                         recv_sem=recv_sems.at[3 + k], device_id=to, device_id_type=MESH).start()
        token[...] = jnp.zeros_like(token)

    return pl.pallas_call(
        body, name="rs_chips_start",
        out_shape=(pltpu.SemaphoreType.DMA((10,)), pltpu.SemaphoreType.DMA((10,)), pltpu.HBM(part.shape, part.dtype),
                   pltpu.HBM(small.shape, small.dtype), pltpu.HBM(part.shape, part.dtype), pltpu.HBM((N_DEV, 8, D), F32),
                   SDS((8, LANES), F32)),
        in_specs=(HBM, HBM, HBM, HBM, ANY), out_specs=(SEM, SEM, HBM, HBM, HBM, HBM, pl.BlockSpec(memory_space=pltpu.VMEM)),
        input_output_aliases={0: 2, 1: 3, 2: 4, 3: 5}, compiler_params=pltpu.CompilerParams(has_side_effects=DATAFLOW),
    )(_in_hbm(part), _in_hbm(small), _in_hbm(lax.empty(part.shape, part.dtype)), _in_hbm(lax.empty((N_DEV, 8, D), F32)), after)


def _rs_chips_wait(send_sems, recv_sems, p_thru, s_thru, land_thru, sland_thru, after):
    def body(p_ref, s_ref, land_ref, sland_ref, send_sems, recv_sems, *after_and_outputs):
        x, y, c, chips = _place()
        for k, (cx, cy) in enumerate(chips):
            cp = pltpu.make_async_remote_copy(src_ref=p_ref.at[0], dst_ref=land_ref.at[2 * cx + cy], send_sem=send_sems.at[k],
                                              recv_sem=recv_sems.at[k], device_id=(cx, cy, c), device_id_type=MESH)
            cp.wait_send()
            cp.wait_recv()
        peers = [(x, y, 1 - c)] + [(cx, cy, c) for cx, cy in chips] + [(cx, cy, 1 - c) for cx, cy in chips]
        for k, (px, py, pc) in enumerate(peers):
            cp = pltpu.make_async_remote_copy(src_ref=s_ref, dst_ref=sland_ref.at[4 * px + 2 * py + pc], send_sem=send_sems.at[3 + k],
                                              recv_sem=recv_sems.at[3 + k], device_id=(px, py, pc), device_id_type=MESH)
            cp.wait_send()
            cp.wait_recv()

    hbm = lambda a: pltpu.HBM(a.shape, a.dtype)
    outs = pl.pallas_call(
        body, name="rs_chips_wait", out_shape=(hbm(p_thru), hbm(s_thru), hbm(land_thru), hbm(sland_thru)),
        in_specs=(HBM, HBM, HBM, HBM, SEM, SEM) + (ANY,) * len(after), out_specs=(HBM, HBM, HBM, HBM),
        input_output_aliases={0: 0, 1: 1, 2: 2, 3: 3}, compiler_params=pltpu.CompilerParams(has_side_effects=DATAFLOW),
    )(p_thru, s_thru, land_thru, sland_thru, send_sems, recv_sems, *after)
    return outs[0], outs[2], outs[3]


def _rs_add_chips(qidx, part, parts):
    HALF = part.shape[1]
    th = HALF // 4
    assert th % 16 == 0

    def body(q_ref, own_ref, p_ref, o_ref):
        for me in range(N_CHIPS):
            @pl.when(q_ref[0] == me)
            def _(me=me):
                t = [(own_ref[0] if j == me else p_ref[j]).astype(F32) for j in range(N_CHIPS)]
                o_ref[...] = ((t[0] + t[1]) + t[2]) + t[3]

    gs = pltpu.PrefetchScalarGridSpec(
        num_scalar_prefetch=1, grid=(HALF // th,),
        in_specs=[pl.BlockSpec((1, th, D), lambda i, q: (q[0], i, 0)), pl.BlockSpec((N_CHIPS, th, D), lambda i, q: (0, i, 0))],
        out_specs=pl.BlockSpec((th, D), lambda i, q: (i, 0)))
    return pl.pallas_call(body, name="rs_add_chips", grid_spec=gs, out_shape=SDS((HALF, D), F32),
                          compiler_params=pltpu.CompilerParams(dimension_semantics=("parallel",),
                                                               vmem_limit_bytes=48 << 20))(qidx, part, parts)


def _rs_join(mine, core, name):
    def body(in_ref, out_ref, send_sem, recv_sem):
        x, y, c, _ = _place()
        cp = pltpu.make_async_remote_copy(src_ref=in_ref, dst_ref=out_ref, send_sem=send_sem, recv_sem=recv_sem,
                                          device_id=(x, y, 1 - c), device_id_type=MESH)
        cp.start()
        cp.wait()

    theirs = _pcall(body, name=name, in_specs=[ANY], out_specs=ANY, out_shape=SDS(mine.shape, F32),
                    scratch=[pltpu.SemaphoreType.DMA, pltpu.SemaphoreType.DMA])(mine)
    return jnp.where(core == 0, jnp.concatenate([mine, theirs]), jnp.concatenate([theirs, mine]))


def _reduce_late_start(gpk, after):
    rows = gpk.shape[1]
    HALF = rows // 2
    assert HALF % 16 == 0

    def body(in_ref, land_ref, after_ref, send_sems, recv_sems, in_thru, land_thru, token):
        x, y, c, chips = _place()
        me = 4 * x + 2 * y + c
        peers = [(x, y, 1 - c)] + [(cx, cy, c) for cx, cy in chips] + [(cx, cy, 1 - c) for cx, cy in chips]
        for k, (px, py, pc) in enumerate(peers):
            src = in_ref.at[2 * px + py, pl.ds(pl.multiple_of(pc * HALF, 16), HALF)]
            pltpu.make_async_remote_copy(src_ref=src, dst_ref=land_ref.at[me], send_sem=send_sems.at[k], recv_sem=recv_sems.at[k],
                                         device_id=(px, py, pc), device_id_type=MESH).start()
        token[...] = jnp.zeros_like(token)

    return pl.pallas_call(
        body, name="reduce_late_start",
        out_shape=(pltpu.SemaphoreType.DMA((7,)), pltpu.SemaphoreType.DMA((7,)), pltpu.HBM(gpk.shape, gpk.dtype),
                   pltpu.HBM((N_DEV, HALF, D), gpk.dtype), SDS((8, LANES), F32)),
        in_specs=(HBM, HBM, ANY), out_specs=(SEM, SEM, HBM, HBM, pl.BlockSpec(memory_space=pltpu.VMEM)),
        input_output_aliases={0: 2, 1: 3}, compiler_params=pltpu.CompilerParams(has_side_effects=DATAFLOW),
    )(_in_hbm(gpk), _in_hbm(lax.empty((N_DEV, HALF, D), gpk.dtype)), after)


def _reduce_late_wait(send_sems, recv_sems, in_thru, land_thru, after):
    def body(in_ref, land_ref, send_sems, recv_sems, after_ref, in_out, got_ref):
        x, y, c, chips = _place()
        peers = [(x, y, 1 - c)] + [(cx, cy, c) for cx, cy in chips] + [(cx, cy, 1 - c) for cx, cy in chips]
        for k, (px, py, pc) in enumerate(peers):
            cp = pltpu.make_async_remote_copy(src_ref=land_ref.at[0], dst_ref=land_ref.at[4 * px + 2 * py + pc],
                                              send_sem=send_sems.at[k], recv_sem=recv_sems.at[k],
                                              device_id=(px, py, pc), device_id_type=MESH)
            cp.wait_send()
            cp.wait_recv()

    return pl.pallas_call(
        body, name="reduce_late_wait",
        out_shape=(pltpu.HBM(in_thru.shape, in_thru.dtype), pltpu.HBM(land_thru.shape, land_thru.dtype)),
        in_specs=(HBM, HBM, SEM, SEM, ANY), out_specs=(HBM, HBM), input_output_aliases={0: 0, 1: 1},
        compiler_params=pltpu.CompilerParams(has_side_effects=DATAFLOW),
    )(in_thru, land_thru, send_sems, recv_sems, after)


def _reduce_late_add(didx, gpk, parts):
    HALF = parts.shape[1]
    th = HALF // 4
    nh = HALF // th
    assert th % 16 == 0

    def body(d_ref, own_ref, p_ref, o_ref):
        for me in range(N_DEV):
            @pl.when(d_ref[0] == me)
            def _(me=me):
                t = [(own_ref[0] if j == me else p_ref[j]).astype(F32) for j in range(N_DEV)]
                o_ref[...] = ((((((t[0] + t[1]) + t[2]) + t[3]) + t[4]) + t[5]) + t[6]) + t[7]

    gs = pltpu.PrefetchScalarGridSpec(
        num_scalar_prefetch=1, grid=(nh,),
        in_specs=[pl.BlockSpec((1, th, D), lambda i, d: (d[1], d[2] * nh + i, 0)), pl.BlockSpec((N_DEV, th, D), lambda i, d: (0, i, 0))],
        out_specs=pl.BlockSpec((th, D), lambda i, d: (i, 0)))
    return pl.pallas_call(body, name="reduce_late_add", grid_spec=gs, out_shape=SDS((HALF, D), F32),
                          compiler_params=pltpu.CompilerParams(dimension_semantics=("parallel",),
                                                               vmem_limit_bytes=48 << 20))(didx, gpk, parts)


def _pack_early(b, dtype):
    lanes = lambda a: jnp.pad(a.astype(dtype), ((0, 0), (0, D - a.shape[1])))
    pair = jnp.concatenate([b["w_q_b"].astype(dtype), b["w_ple"].astype(dtype), jnp.zeros((256, D - 640), dtype)], axis=1)
    return jnp.concatenate([lanes(b["w_in"]), pair, lanes(b["w_kv_b"])], axis=0)


def _pack_late(b, dtype):
    return jnp.concatenate([b[n].astype(dtype) for n in ("w_mla_up", "w_swa_up", "w_out", "w_ple_gate", "w_mlp_up", "w_mlp_down")],
                           axis=0)


def _unpack_shards(pk, which):
    return {n: pk[PACK_AT[n][1]:PACK_AT[n][1] + r, PACK_AT[n][2]:PACK_AT[n][2] + c] for n, r, c in BIG if PACK_AT[n][0] == which}


def _full_weights(gathered, own, chip, which):
    own_b = _unpack_shards(own, which)
    per_chip = [{n: jnp.where(chip == j, own_b[n], blk) for n, blk in _unpack_shards(gathered[j], which).items()}
                for j in range(N_CHIPS)]
    out = {}
    for n in own_b:
        shards = [pc[n] for pc in per_chip]
        if n == "w_in":
            out["w_in_p"] = _w_in_internal(shards)
        else:
            out[n] = jnp.concatenate(shards, axis=1 if n in COL_SHARDED else 0)
    return out


def _split_full_grads(grads, pack, dtype):
    shard = {n: (r, c) for n, r, c in BIG}
    chunks = []
    for j in range(N_CHIPS):
        blocks = {}
        for n, g in grads.items():
            if n == "w_in_p":
                blocks["w_in"] = _w_in_grad_shard(g, j)
                continue
            r, c = shard[n]
            blocks[n] = g[:, j * c:(j + 1) * c] if n in COL_SHARDED else g[j * r:(j + 1) * r]
        chunks.append(pack(blocks, dtype))
    return jnp.stack(chunks)


W_IN_SHARD = 936
W_IN_SEGMENTS = ((0, 256, (3072,)), (256, 384, (3840,)), (384, 416, (4032,)), (416, 1440, (0,)), (1440, 1504, (3328, 3392)),
                 (1504, 1568, (3456, 3520)), (1568, 1632, (3584, 3648)), (1632, 1696, (3712, 3776)), (1696, 3744, (1024,)))


def _w_in_internal(shards):
    def cols(a, b):
        out = []
        for j, s in enumerate(shards):
            lo, hi = max(a, W_IN_SHARD * j), min(b, W_IN_SHARD * (j + 1))
            if lo < hi:
                out.append(s[:, lo - W_IN_SHARD * j:hi - W_IN_SHARD * j])
        return out

    pieces = {}
    for a, b, places in W_IN_SEGMENTS:
        for at in places:
            pieces[at] = cols(a, b)
    zeros = lambda n: [jnp.zeros((D, n), shards[0].dtype)]
    pieces[3968] = zeros(64)
    pieces[4064] = zeros(32)
    return jnp.concatenate([piece for at in sorted(pieces) for piece in pieces[at]], axis=1)


def _w_in_grad_shard(g, j):
    def internal(a, b):
        out = []
        while a < b:
            end = min(b, (a // D + 1) * D)
            out.append(g[a // D][:, a % D:a % D + end - a])
            a = end
        return out

    out = []
    for a, b, places in W_IN_SEGMENTS:
        lo, hi = max(a, W_IN_SHARD * j), min(b, W_IN_SHARD * (j + 1))
        if lo < hi:
            parts = [internal(at + lo - a, at + hi - a) for at in places]
            if len(parts) == 1:
                out += parts[0]
            else:
                assert len(parts[0]) == len(parts[1]) == 1
                out.append(parts[0][0] + parts[1][0])
    return jnp.concatenate(out, axis=1)


def _local_step(x, p, tgt, w, small, late_weights, late_grads_out):
    T = x.shape[0]
    tm = 256
    tb = 256
    w_in_p = w["w_in_p"]
    wqb = jnp.pad(w["w_q_b"].reshape(Q_LORA, MLA_HEADS, 96), ((0, 0), (0, 0), (0, 32))).reshape(Q_LORA, 2048)
    wkv = w["w_kv_b"].reshape(KV_LORA, MLA_HEADS, 128)
    wkn = jnp.pad(wkv[:, :, :64], ((0, 0), (0, 0), (0, 64))).reshape(KV_LORA, 2048)
    wv = wkv[:, :, 64:].reshape(KV_LORA, 1024)
    tab_m = _rope_tables(T, "mla")
    tab_s = _rope_tables(T, "swa")
    g1, gq, gkv, sinks = small["g_mix_pre"], small["g_q_a"], small["g_kv_a"], small["sinks"]
    g2, g3, g4, g5 = small["g_mix_post"], small["g_mlp_pre"], small["g_mlp_post"], small["g_ple"]
    sink_vec = sinks.reshape(SWA_HEADS)

    z, h1 = _fwd_in(x, g1, w_in_p, tm)
    qn, kvn, km, vm, qt, kt, vt, qs, ks, vs = _fwd_qkv(z, gq, gkv, wqb, wkn, wv, tab_m, tab_s, tb)
    om, lse_m = _mla_fwd(qt, km, vt, tb)
    os_, lse_s = _swa_fwd(sink_vec, qs, ks, vs)
    w = {**w, **late_weights((om, os_))}
    y, yo, au, bu, x1 = _fwd_mix(om, os_, z, x, w["w_mla_up"], w["w_swa_up"], w["w_out"], g2, tm)
    h2, u = _fwd_mlp_up(x1, g3, w["w_mlp_up"], tm)
    d, x2 = _fwd_mlp_down(u, w["w_mlp_down"], x1, g4, tm)
    loss, dx2, dgt, de0, dg5 = _ple_fwd_bwd(p, x2, tgt, w["w_ple"], g5, w["w_ple_gate"], tm)

    dd, da, dg4 = _bwd_mlp_down(dx2, d, g4, w["w_mlp_down"], u, tm)
    dx1, dg3 = _bwd_mlp_up(da, w["w_mlp_up"], x1, g3, dx2, tm)
    dyo, dg2, dau, dbu, dga, dgb, dos, delta_m, dom_t = _bwd_mix(dx1, yo, g2, w["w_out"], z, au, bu, w["w_mla_up"],
                                                                w["w_swa_up"], om, tb)
    gpk_late = lax.empty((N_CHIPS, PACK_ROWS["late"], D), BF16)
    for weight, a_, g_ in (("w_mla_up", om, dau), ("w_swa_up", os_, dbu), ("w_out", y, dyo), ("w_ple_gate", x2, dgt),
                           ("w_mlp_up", h2, da), ("w_mlp_down", u, dd)):
        gpk_late = _wgrad(a_, g_, "wgrad_" + weight[2:], into=(gpk_late, weight))
    token = late_grads_out(gpk_late)
    delta_m = delta_m + token[0, 0]
    dqm, dkm, dvm = _mla_bwd(qt, km, kt, vm, dom_t, lse_m, delta_m, tb)
    dqs, dkc, dkp, dvc, dvp, dsink = _swa_bwd(sink_vec, qs, ks, vs, dos, os_, lse_s)
    dqb, dknb, dvb, dsq, drest, dgq, dgkv = _bwd_qkv(dqm, dkm, dvm, dqs, dkc, dkp, dvc, dvp, z, gq, gkv, wqb, wkn, wv,
                                                      tab_m, tab_s)
    gx, dg1 = _bwd_in(dsq, dga, dgb, drest, w_in_p, x, g1, dx1, tm)

    g_in_p = [_wgrad(h1, dsq, "wgrad_in_sq"), _wgrad(h1, dga, "wgrad_in_ga"), _wgrad(h1, dgb, "wgrad_in_gb"),
              _wgrad(h1, drest, "wgrad_in_rest")]
    g_qb_p = _wgrad_t(dqb, qn, "wgrad_q_b").T
    g_kn_p = _wgrad_t(dknb, kvn, "wgrad_kv_b_nope").T
    g_v_p = _wgrad_t(dvb, kvn, "wgrad_kv_b_v").T
    grads = {
        "w_in_p": g_in_p,
        "w_q_b": g_qb_p.reshape(Q_LORA, MLA_HEADS, 128)[:, :, :96].reshape(Q_LORA, 1536),
        "w_kv_b": jnp.concatenate([g_kn_p.reshape(KV_LORA, MLA_HEADS, 128)[:, :, :64], g_v_p.reshape(KV_LORA, MLA_HEADS, 64)],
                                  axis=2).reshape(KV_LORA, 2048),
        "w_ple": _wgrad(p, de0, "wgrad_ple"),
    }
    small_grads = {"g_mix_pre": dg1, "g_q_a": dgq.reshape(1, Q_LORA), "g_kv_a": dgkv.reshape(1, KV_LORA), "sinks": dsink[0:1, 0:SWA_HEADS], "g_mix_post": dg2,
                   "g_mlp_pre": dg3, "g_mlp_post": dg4, "g_ple": dg5}
    return loss, gx, grads, small_grads


def _pack_small(vals, fill, scalar=None):
    wide = [vals[n] for n, k in SMALL if k == D]
    narrow = [vals[n] for n, k in SMALL if k != D]
    used = sum(k for _, k in SMALL if k != D)
    last = jnp.concatenate(narrow + [jnp.full((1, D - used), fill, F32)], axis=1)
    rest = jnp.full((2, D), fill, F32)
    if scalar is not None:
        rest = jnp.concatenate([jnp.concatenate([scalar, rest[0:1, 1:]], axis=1), rest[1:2]], axis=0)
    return jnp.concatenate(wide + [last, rest], axis=0)


def _unpack_small(pk):
    out, row, off = {}, 0, 0
    for n, k in SMALL:
        if k == D:
            out[n] = pk[row:row + 1]
            row += 1
    for n, k in SMALL:
        if k != D:
            out[n] = pk[5:6, off:off + k]
            off += k
    return out


def kernel(x, p, g_mix_pre, w_in, g_q_a, w_q_b, g_kv_a, w_kv_b, sinks, w_mla_up, w_swa_up, w_out, g_mix_post, g_mlp_pre, w_mlp_up, w_mlp_down, g_mlp_post, w_ple, g_ple, w_ple_gate, loss_target, m_g_mix_pre, m_w_in, m_g_q_a, m_w_q_b, m_g_kv_a, m_w_kv_b, m_sinks, m_w_mla_up, m_w_swa_up, m_w_out, m_g_mix_post, m_g_mlp_pre, m_w_mlp_up, m_w_mlp_down, m_g_mlp_post, m_w_ple, m_g_ple, m_w_ple_gate, v_g_mix_pre, v_w_in, v_g_q_a, v_w_q_b, v_g_kv_a, v_w_kv_b, v_sinks, v_w_mla_up, v_w_swa_up, v_w_out, v_g_mix_post, v_g_mlp_pre, v_w_mlp_up, v_w_mlp_down, v_g_mlp_post, v_w_ple, v_g_ple, v_w_ple_gate):
    given = dict(locals())
    big_w = {n: given[n][0] for n, _, _ in BIG}
    small_w = {n: given[n] for n, _ in SMALL}
    small_m = {n: given["m_" + n] for n, _ in SMALL}
    small_v = {n: given["v_" + n] for n, _ in SMALL}

    core = lax.axis_index("c")
    chip = 2 * lax.axis_index("x") + lax.axis_index("y")
    core_i = core.astype(jnp.int32).reshape(1)
    chip_i = chip.astype(jnp.int32).reshape(1)
    dev_i = jnp.stack([2 * chip + core, chip, core]).astype(jnp.int32)

    own_early = _pack_early(big_w, BF16)
    own_late = _pack_late(big_w, BF16)
    got_early = _all_gather(own_early)
    late_flight = _gather_late_start(own_late, got_early)
    weights = _full_weights(got_early, own_early, chip, "early")
    step_small = {**small_w, "g_mix_pre": small_w["g_mix_pre"] + late_flight[4][0, 0]}

    def late_weights(after):
        return _full_weights(_gather_late_wait(*late_flight[:4], after), own_late, chip, "late")

    flight = {}

    def late_grads_out(gpk_late):
        flight["late"] = _reduce_late_start(gpk_late, dev_i)
        return flight["late"][4]

    loss_blk, gx, grads, small_grads = _local_step(x[0], p[0, 0], loss_target[0], weights, step_small, late_weights,
                                                   late_grads_out)

    gpk = _split_full_grads(grads, _pack_early, F32)
    got = _rs_sibling(gpk)
    part = _rs_add_sibling(core_i, gpk, got)
    small_own = _pack_small(small_grads, 0.0, loss_blk[0:1, 0:1])
    early_flight = _rs_chips_start(part, small_own, dev_i)

    out_g, out_d, out_m, out_v = {}, {}, {}, {}
    gpk_late, parts_late = _reduce_late_wait(*flight["late"][:4], early_flight[6])
    joined_late = _rs_join(_reduce_late_add(dev_i, gpk_late, parts_late), core, "rs_join_late")
    for n, _, _ in BIG:
        if PACK_AT[n][0] == "late":
            out_g[n], out_d[n], out_m[n], out_v[n] = _adamw(given[n], joined_late, given["m_" + n], given["v_" + n], n)

    part, parts, small_parts = _rs_chips_wait(*early_flight[:6], [out_d[n] for n in out_d])
    joined_early = _rs_join(_rs_add_chips(chip_i, part, parts), core, "rs_join_early")
    for n, _, _ in BIG:
        if PACK_AT[n][0] == "early":
            out_g[n], out_d[n], out_m[n], out_v[n] = _adamw(given[n], joined_early, given["m_" + n], given["v_" + n], n)

    mine = (lax.broadcasted_iota(jnp.int32, (N_DEV, 1, 1), 0) == dev_i[0])
    g_small_pk, d_small_pk, m_small_pk, v_small_pk = _adamw_small(
        _pack_small(small_w, 0.0), jnp.where(mine, small_own[None], small_parts), _pack_small(small_m, 0.0),
        _pack_small(small_v, 1.0))
    loss = g_small_pk[6, 0]
    for out, pk in ((out_g, g_small_pk), (out_d, d_small_pk), (out_m, m_small_pk), (out_v, v_small_pk)):
        out.update(_unpack_small(pk))
    order = ["g_mix_pre", "w_in", "g_q_a", "w_q_b", "g_kv_a", "w_kv_b", "sinks", "w_mla_up", "w_swa_up", "w_out", "g_mix_post",
             "g_mlp_pre", "w_mlp_up", "w_mlp_down", "g_mlp_post", "w_ple", "g_ple", "w_ple_gate"]
    return (loss, gx[None], *[out_g[n] for n in order], *[out_d[n] for n in order], *[out_m[n] for n in order],
            *[out_v[n] for n in order])
```

```python
import math

import jax
import jax.numpy as jnp
from jax import lax
from jax.experimental import pallas as pl
from jax.experimental.pallas import tpu as pltpu

F32 = jnp.float32
BF16 = jnp.bfloat16
SDS = jax.ShapeDtypeStruct

D = 1024
D_FF = 4096
PLE = 256
Q_LORA = 256
KV_LORA = 128
MLA_HEADS = 16
MLA_NOPE = 64
MLA_ROPE = 32
SWA_HEADS = 16
SWA_HD = 64
WINDOW = 128
ROPE_THETA = 10000.0
EPS = 1e-6
NEG = -1e30
NZ = 4096
MLA_SCALE = (MLA_NOPE + MLA_ROPE) ** -0.5
LOG2_E = math.log2(math.e)
MLA_LOG2_SCALE = MLA_SCALE * LOG2_E
SWA_SCALE = SWA_HD ** -0.5

ADAM_LR = 0.001
ADAM_B1 = 0.9
ADAM_B2 = 0.999
ADAM_EPS = 1e-08
ADAM_WD = 0.01
ADAM_STEP = 10

LANES = 128
ATT_COLS = 128
VT_ROWS = 80
N_CHIPS = 4
N_DEV = 8
MESH = pl.DeviceIdType.MESH

NT = (((1,), (1,)), ((), ()))
TN = (((0,), (0,)), ((), ()))

BIG = (("w_in", 1024, 936), ("w_q_b", 256, 384), ("w_kv_b", 128, 512), ("w_mla_up", 256, 1024),
       ("w_swa_up", 256, 1024), ("w_out", 256, 1024), ("w_mlp_up", 1024, 1024), ("w_mlp_down", 1024, 1024),
       ("w_ple", 256, 256), ("w_ple_gate", 256, 1024))
COL_SHARDED = ("w_in", "w_q_b", "w_kv_b", "w_mlp_up", "w_ple")
PACK_AT = {"w_in": ("early", 0, 0), "w_q_b": ("early", 1024, 0), "w_ple": ("early", 1024, 384), "w_kv_b": ("early", 1280, 0),
           "w_mla_up": ("late", 0, 0), "w_swa_up": ("late", 256, 0), "w_out": ("late", 512, 0), "w_ple_gate": ("late", 768, 0),
           "w_mlp_up": ("late", 1024, 0), "w_mlp_down": ("late", 2048, 0)}
PACK_ROWS = {"early": 1408, "late": 3072}
SMALL = (("g_mix_pre", 1024), ("g_q_a", 256), ("g_kv_a", 128), ("sinks", 16), ("g_mix_post", 1024),
         ("g_mlp_pre", 1024), ("g_mlp_post", 1024), ("g_ple", 1024))


def _dot(a, b):
    return jnp.dot(a, b, preferred_element_type=F32)


def _dot_nt(a, b):
    return lax.dot_general(a, b, NT, preferred_element_type=F32)


def _dot_tn(a, b):
    return lax.dot_general(a, b, TN, preferred_element_type=F32)


def _pcall(body, *, name, out_shape, grid=(), in_specs=None, out_specs=None, scratch=(), sem=None, vmem_mb=48, aliases=None):
    params = dict(vmem_limit_bytes=vmem_mb << 20)
    if sem is not None:
        params["dimension_semantics"] = sem
    return pl.pallas_call(body, name=name, grid=grid, in_specs=in_specs, out_specs=out_specs, out_shape=out_shape,
                          scratch_shapes=list(scratch), input_output_aliases=aliases or {},
                          compiler_params=pltpu.CompilerParams(**params))


def _rows(tm, n, col=0):
    return pl.BlockSpec((tm, n), lambda i: (i, col))


def _full(shape):
    return pl.BlockSpec(shape, lambda i: (0,) * len(shape))


def _rms(x, g):
    r = lax.rsqrt(jnp.mean(x * x, axis=-1, keepdims=True) + EPS)
    return x * r * g


def _rms_bwd(dy, x, g):
    r = lax.rsqrt(jnp.mean(x * x, axis=-1, keepdims=True) + EPS)
    xn = x * r
    dn = dy * g
    dx = r * (dn - xn * jnp.mean(dn * xn, axis=-1, keepdims=True))
    return dx, jnp.sum(dy * xn, axis=0, keepdims=True)


def _sigmoid(x):
    return 1.0 / (1.0 + jnp.exp(-x))


def _rope(x, c, a, b, half):
    return x * c + pltpu.roll(x, LANES - half, 1) * a + pltpu.roll(x, half, 1) * b


def _rope_tables(T, kind):
    lane = jnp.arange(LANES)
    if kind == "mla":
        half = MLA_ROPE // 2
        rel = lane - MLA_NOPE
        on = (rel >= 0) & (rel < MLA_ROPE)
        d = MLA_ROPE
    else:
        half = SWA_HD // 2
        rel = lane % SWA_HD
        on = jnp.ones((LANES,), bool)
        d = SWA_HD
    first = on & (rel < half)
    second = on & (rel >= half)
    f = jnp.where(first, rel, rel - half).astype(F32)
    inv = jnp.exp(-math.log(ROPE_THETA) * f * (2.0 / d))
    ang = jnp.arange(T, dtype=F32)[:, None] * inv[None, :]
    cos, sin = jnp.cos(ang), jnp.sin(ang)
    c = jnp.where(on[None], cos, 1.0)
    a = jnp.where(first[None], -sin, 0.0)
    b = jnp.where(second[None], sin, 0.0)
    return c, a, b


def _fwd_in(x, g1, w_in_p, tm):
    T = x.shape[0]

    def body(x_ref, g_ref, w_ref, z_ref, h_ref):
        h = _rms(x_ref[...], g_ref[...]).astype(BF16)
        h_ref[...] = h
        z_ref[...] = _dot(h, w_ref[...])

    return _pcall(body, name="fwd_in", grid=(T // tm,),
                  in_specs=[_rows(tm, D), _full((1, D)), _full((D, NZ))],
                  out_specs=[_rows(tm, NZ), _rows(tm, D)],
                  out_shape=[SDS((T, NZ), F32), SDS((T, D), BF16)], sem=("parallel",))(x, g1, w_in_p)


def _fwd_qkv(z, gq, gkv, wqb, wkn, wv, tab_m, tab_s, tm):
    T = z.shape[0]
    wqb_t, wkn_t, wv_t = wqb.T, wkn.T, wv.T
    tab_mt = [t.T for t in tab_m]

    def body(qa_ref, sq_ref, skd_ref, svd_ref, kva_ref, kr_ref, gq_ref, gkv_ref, wkn_ref, wv_ref, wqbt_ref, wknt_ref, wvt_ref,
             cm_ref, am_ref, bm_ref, cmt_ref, amt_ref, bmt_ref, cs_ref, as_ref, bs_ref,
             qn_ref, kvn_ref, km_ref, vm_ref, qt_ref, kt_ref, vt_ref, qs_ref, ks_ref, vs_ref):
        qn = _rms(qa_ref[...], gq_ref[...])
        qn_ref[...] = qn.astype(BF16)
        kvn = _rms(kva_ref[...], gkv_ref[...])
        kvn_b = kvn.astype(BF16)
        kvn_ref[...] = kvn_b
        qn_t = qn.T.astype(BF16)
        kvn_t = kvn.T.astype(BF16)
        cm, am, bm = cm_ref[...], am_ref[...], bm_ref[...]
        cmt, amt, bmt = cmt_ref[...], amt_ref[...], bmt_ref[...]
        cs, as_, bs = cs_ref[...], as_ref[...], bs_ref[...]
        k_rope = _rope(kr_ref[...], cm, am, bm, MLA_ROPE // 2)
        k_rope_t = k_rope.T
        half = MLA_ROPE // 2
        vm_ref[...] = _dot(kvn_b, wv_ref[...]).astype(BF16)
        km_all = _dot(kvn_b, wkn_ref[...])
        v_t = _dot(wvt_ref[...], kvn_t)
        q_t = _dot(wqbt_ref[...], qn_t)
        k_t = _dot(wknt_ref[...], kvn_t)
        ones_row = jnp.where(lax.broadcasted_iota(jnp.int32, (64, tm), 0) == 0, 1.0, 0.0)
        for h in range(MLA_HEADS):
            sl = slice(LANES * h, LANES * (h + 1))
            vt_ref[0, sl, :] = jnp.concatenate([v_t[64 * h:64 * (h + 1)], ones_row], axis=0).astype(BF16)
            qh = q_t[sl]
            qt_ref[0, sl, :] = (qh * cmt + pltpu.roll(qh, LANES - half, 0) * amt + pltpu.roll(qh, half, 0) * bmt).astype(BF16)
            km_ref[:, sl] = (km_all[:, sl] + k_rope).astype(BF16)
            kt_ref[0, sl, :] = (k_t[sl] + k_rope_t).astype(BF16)
        for j in range(D // LANES):
            sl = slice(LANES * j, LANES * (j + 1))
            qs_ref[:, sl] = _rope(sq_ref[:, sl], cs, as_, bs, SWA_HD // 2).astype(BF16)
        for j in range(2):
            sl = slice(LANES * j, LANES * (j + 1))
            ks_ref[:, sl] = _rope(skd_ref[:, sl], cs, as_, bs, SWA_HD // 2).astype(BF16)
        vs_ref[...] = svd_ref[...].astype(BF16)

    tab = [_rows(tm, LANES)] * 3
    tab_t = [pl.BlockSpec((LANES, tm), lambda i: (0, i))] * 3
    return _pcall(body, name="fwd_qkv", grid=(T // tm,),
                  in_specs=[_rows(tm, 256, 12), _rows(tm, 1024, 0), _rows(tm, 256, 13), _rows(tm, 256, 14),
                            _rows(tm, 128, 30), _rows(tm, 128, 31), _full((1, Q_LORA)), _full((1, KV_LORA)),
                            _full((KV_LORA, 2048)), _full((KV_LORA, 1024)), _full((2048, Q_LORA)), _full((2048, KV_LORA)),
                            _full((1024, KV_LORA))] + tab + tab_t + tab,
                  out_specs=[_rows(tm, Q_LORA), _rows(tm, KV_LORA), _rows(tm, 2048), _rows(tm, 1024),
                             pl.BlockSpec((1, 2048, tm), lambda i: (i, 0, 0)), pl.BlockSpec((1, 2048, tm), lambda i: (i, 0, 0)),
                             pl.BlockSpec((1, 2048, tm), lambda i: (i, 0, 0)),
                             _rows(tm, 1024), _rows(tm, 256), _rows(tm, 256)],
                  out_shape=[SDS((T, Q_LORA), BF16), SDS((T, KV_LORA), BF16), SDS((T, 2048), BF16),
                             SDS((T, 1024), BF16), SDS((T // tm, 2048, tm), BF16), SDS((T // tm, 2048, tm), BF16),
                             SDS((T // tm, 2048, tm), BF16),
                             SDS((T, 1024), BF16), SDS((T, 256), BF16), SDS((T, 256), BF16)],
                  sem=("parallel",))(z, z, z, z, z, z, gq, gkv, wkn, wv, wqb_t, wkn_t, wv_t, *tab_m, *tab_mt, *tab_s)


def _mla_fwd(qt, km, vt, tb):
    T = km.shape[0]
    nb = T // tb
    cc = ATT_COLS

    def body(q_ref, k_ref, vt_ref, o_ref, l_ref, s_ref, p_ref, al_ref, m_ref, acc_ref):
        i = pl.program_id(1)
        m_ref[...] = jnp.full(m_ref.shape, NEG, F32)
        acc_ref[...] = jnp.zeros_like(acc_ref)
        p_ref[1] = jnp.zeros(p_ref.shape[1:], BF16)
        al_ref[1] = jnp.ones(al_ref.shape[1:], F32)
        key = lax.broadcasted_iota(jnp.int32, (tb, cc), 0)
        qry = lax.broadcasted_iota(jnp.int32, (tb, cc), 1)

        def scores(j, slot):
            off = pl.multiple_of(j * tb, tb)
            for hh in range(2):
                sl = slice(LANES * hh, LANES * (hh + 1))
                s_ref[slot, hh] = _dot(k_ref[pl.ds(off, tb), sl], q_ref[0, sl, :])

        def softmax(slot, diagonal):
            chains = [(hh, slice(cc * c, cc * (c + 1)), c) for hh in range(2) for c in range(tb // cc)]

            def scaled(hh, cols, c):
                t = s_ref[slot, hh, :, cols] * MLA_LOG2_SCALE
                return jnp.where(key <= qry + cc * c, t, NEG) if diagonal else t

            tops = []
            for hh, cols, c in chains:
                if diagonal:
                    top = jnp.max(scaled(hh, cols, c), axis=0, keepdims=True)
                else:
                    top = jnp.max(s_ref[slot, hh, :, cols], axis=0, keepdims=True) * MLA_LOG2_SCALE
                m_old = m_ref[hh, :, cols]
                mn = jnp.maximum(m_old, top)
                m_ref[hh, :, cols] = mn
                al_ref[slot, hh, :, cols] = jnp.exp2(m_old - mn)
                tops.append(mn)
            for (hh, cols, c), mn in zip(chains, tops):
                p_ref[slot, hh, :, cols] = jnp.exp2(scaled(hh, cols, c) - mn).astype(BF16)

        def accumulate(j, slot):
            for hh in range(2):
                acc_ref[hh] = al_ref[slot, hh] * acc_ref[hh] + _dot(vt_ref[j, LANES * hh:LANES * hh + VT_ROWS, :], p_ref[slot, hh])

        def step(t, carry):
            scores(2 * t + 1, 1)
            accumulate(jnp.maximum(2 * t - 1, 0), 1)
            softmax(0, False)
            scores(2 * t + 2, 0)
            accumulate(2 * t, 0)
            softmax(1, False)
            return carry

        scores(0, 0)
        lax.fori_loop(0, i // 2, step, 0)

        @pl.when(i % 2 == 1)
        def _():
            scores(i, 1)
            accumulate(jnp.maximum(i - 2, 0), 1)
            softmax(0, False)
            accumulate(i - 1, 0)
            softmax(1, True)
            accumulate(i, 1)

        @pl.when(i % 2 == 0)
        def _():
            accumulate(jnp.maximum(i - 1, 0), 1)
            softmax(0, True)
            accumulate(i, 0)
        den = [acc_ref[hh, 64:65, :] for hh in range(2)]
        o_ref[...] = jnp.concatenate([acc_ref[hh, 0:64, :] / den[hh] for hh in range(2)], axis=0).T
        sub = lax.broadcasted_iota(jnp.int32, (8, tb), 0)
        lse = [m_ref[hh] + jnp.log(den[hh]) * LOG2_E for hh in range(2)]
        l_ref[0, 0] = jnp.where(sub == 0, lse[0], jnp.where(sub == 1, lse[1], 0.0))

    return _pcall(body, name="mla_fwd", grid=(MLA_HEADS // 2, nb),
                  in_specs=[pl.BlockSpec((1, 256, tb), lambda p, i: (i, p, 0)), pl.BlockSpec((T, 256), lambda p, i: (0, p)),
                            pl.BlockSpec((nb, 2 * LANES, tb), lambda p, i: (0, p, 0))],
                  out_specs=[pl.BlockSpec((tb, LANES), lambda p, i: (i, p)),
                             pl.BlockSpec((1, 1, 8, tb), lambda p, i: (p, i, 0, 0))],
                  out_shape=[SDS((T, D), F32), SDS((MLA_HEADS // 2, nb, 8, tb), F32)],
                  scratch=[pltpu.VMEM((2, 2, tb, tb), F32), pltpu.VMEM((2, 2, tb, tb), BF16), pltpu.VMEM((2, 2, 1, tb), F32),
                           pltpu.VMEM((2, 1, tb), F32), pltpu.VMEM((2, VT_ROWS, tb), F32)],
                  sem=("parallel", "arbitrary"))(qt, km, vt)


def _swa_mask(n):
    row = lax.broadcasted_iota(jnp.int32, (WINDOW, 2 * WINDOW), 0)
    col = lax.broadcasted_iota(jnp.int32, (WINDOW, 2 * WINDOW), 1)
    rel = row - col + WINDOW
    return (rel >= 0) & (rel < WINDOW) & ((col >= WINDOW) | (n > 0))


def _swa_specs(T):
    nb = T // WINDOW
    cur = lambda w: pl.BlockSpec((WINDOW, w), lambda n: (n, 0))
    prev = lambda w: pl.BlockSpec((WINDOW, w), lambda n: (jnp.maximum(n - 1, 0), 0))
    return nb, cur, prev


def _swa_fwd(sinks, qs, ks, vs):
    T = qs.shape[0]
    nb, cur, prev = _swa_specs(T)

    def body(sink_ref, q_ref, kc_ref, kp_ref, vc_ref, vp_ref, o_ref, l_ref, kb_ref, vb_ref, s_ref, p_ref):
        n = pl.program_id(0)
        mask = _swa_mask(n)
        lo = lax.broadcasted_iota(jnp.int32, (WINDOW, LANES), 1) < 64
        hi = jnp.logical_not(lo)
        for g in range(2):
            gs = slice(LANES * g, LANES * (g + 1))
            kb_ref[g] = jnp.concatenate([kp_ref[:, gs], kc_ref[:, gs]], axis=0)
            vb_ref[g] = jnp.concatenate([vp_ref[:, gs], vc_ref[:, gs]], axis=0)
        for h in range(SWA_HEADS):
            qp = q_ref[:, LANES * (h // 2):LANES * (h // 2 + 1)]
            qh = jnp.where(lo if h % 2 == 0 else hi, qp, jnp.zeros_like(qp))
            s_ref[h] = _dot_nt(qh, kb_ref[h // 8])
        for j in range(SWA_HEADS // 2):
            sl = slice(LANES * j, LANES * (j + 1))
            lses = []
            for h in (2 * j, 2 * j + 1):
                s = jnp.where(mask, s_ref[h] * SWA_SCALE, NEG)
                sk = sink_ref[h]
                m = jnp.maximum(jnp.max(s, axis=1, keepdims=True), sk)
                e = jnp.exp(s - m)
                den = jnp.sum(e, axis=1, keepdims=True) + jnp.exp(sk - m)
                p_ref[h] = (e / den).astype(BF16)
                lses.append(jnp.broadcast_to(m + jnp.log(den), (WINDOW, LANES)))
            l_ref[:, sl] = jnp.where(lo, lses[0], lses[1])
        for j in range(SWA_HEADS // 2):
            vb = vb_ref[j // 4]
            o_ref[:, LANES * j:LANES * (j + 1)] = jnp.where(lo, _dot(p_ref[2 * j], vb), _dot(p_ref[2 * j + 1], vb))

    return _pcall(body, name="swa_fwd", grid=(nb,),
                  in_specs=[pl.BlockSpec(memory_space=pltpu.SMEM), cur(D), cur(256), prev(256), cur(256), prev(256)],
                  out_specs=[cur(D), cur(D)], out_shape=[SDS((T, D), F32)] * 2,
                  scratch=[pltpu.VMEM((2, 2 * WINDOW, LANES), BF16), pltpu.VMEM((2, 2 * WINDOW, LANES), BF16),
                           pltpu.VMEM((SWA_HEADS, WINDOW, 2 * WINDOW), F32), pltpu.VMEM((SWA_HEADS, WINDOW, 2 * WINDOW), BF16)],
                  sem=("parallel",))(sinks, qs, ks, ks, vs, vs)


def _fwd_mix(om, os_, z, x, wmu, wsu, wo, g2, tm):
    T = x.shape[0]

    def body(om_ref, os_ref, ga_ref, gb_ref, x_ref, wmu_ref, wsu_ref, wo_ref, g2_ref,
             y_ref, yo_ref, au_ref, bu_ref, x1_ref):
        au = _dot(om_ref[...].astype(BF16), wmu_ref[...])
        bu = _dot(os_ref[...].astype(BF16), wsu_ref[...])
        au_ref[...] = au
        bu_ref[...] = bu
        y = (_sigmoid(ga_ref[...]) * au + _sigmoid(gb_ref[...]) * bu).astype(BF16)
        y_ref[...] = y
        yo = _dot(y, wo_ref[...])
        yo_ref[...] = yo
        x1_ref[...] = x_ref[...] + _rms(yo, g2_ref[...])

    r = _rows(tm, D)
    w = _full((D, D))
    return _pcall(body, name="fwd_mix", grid=(T // tm,),
                  in_specs=[r, r, _rows(tm, D, 1), _rows(tm, D, 2), r, w, w, w, _full((1, D))],
                  out_specs=[r] * 5,
                  out_shape=[SDS((T, D), BF16), SDS((T, D), F32), SDS((T, D), F32), SDS((T, D), F32), SDS((T, D), F32)],
                  sem=("parallel",))(om, os_, z, z, x, wmu, wsu, wo, g2)


def _fwd_mlp_up(x1, g3, w1, tm):
    T = x1.shape[0]

    def body(x_ref, g_ref, w_ref, h_ref, u_ref):
        h = _rms(x_ref[...], g_ref[...]).astype(BF16)
        h_ref[...] = h
        u_ref[...] = jnp.square(jnp.maximum(_dot(h, w_ref[...]), 0.0)).astype(BF16)

    return _pcall(body, name="fwd_mlp_up", grid=(T // tm,),
                  in_specs=[_rows(tm, D), _full((1, D)), _full((D, D_FF))],
                  out_specs=[_rows(tm, D), _rows(tm, D_FF)],
                  out_shape=[SDS((T, D), BF16), SDS((T, D_FF), BF16)],
                  sem=("parallel",))(x1, g3, w1)


def _fwd_mlp_down(u, w2, x1, g4, tm):
    T = x1.shape[0]

    def body(u_ref, w_ref, x_ref, g_ref, d_ref, x2_ref):
        d = _dot(u_ref[...], w_ref[...])
        d_ref[...] = d
        x2_ref[...] = x_ref[...] + _rms(d, g_ref[...])

    return _pcall(body, name="fwd_mlp_down", grid=(T // tm,),
                  in_specs=[_rows(tm, D_FF), _full((D_FF, D)), _rows(tm, D), _full((1, D))],
                  out_specs=[_rows(tm, D), _rows(tm, D)], out_shape=[SDS((T, D), F32)] * 2,
                  sem=("parallel",))(u, w2, x1, g4)


def _ple_fwd_bwd(p, x2, tgt, wple, g5, wpg, tm):
    T = x2.shape[0]

    def body(p_ref, x2_ref, t_ref, wple_ref, g5_ref, wpg_ref, loss_ref, dx2_ref, dgt_ref, de0_ref, dg5_ref):
        @pl.when(pl.program_id(0) == 0)
        def _():
            loss_ref[...] = jnp.zeros_like(loss_ref)
            dg5_ref[...] = jnp.zeros_like(dg5_ref)

        e0 = _dot(p_ref[...].astype(BF16), wple_ref[...])
        g5 = g5_ref[...]
        r = lax.rsqrt(jnp.mean(e0 * e0, axis=-1, keepdims=True) + EPS)
        en = e0 * r
        e = en * g5
        x2 = x2_ref[...]
        s = _sigmoid(_dot(x2.astype(BF16), wpg_ref[...]))
        diff = x2 + s * e - t_ref[...]
        sq = jnp.sum(jnp.sum(diff * diff, axis=1, keepdims=True), axis=0, keepdims=True)
        loss_ref[...] += jnp.broadcast_to(sq * (0.5 / D), loss_ref.shape)
        dx3 = diff * (1.0 / D)
        de = dx3 * s
        dgt = (dx3 * e * s * (1.0 - s)).astype(BF16)
        dgt_ref[...] = dgt
        dn = de * g5
        de0_ref[...] = (r * (dn - en * jnp.mean(dn * en, axis=-1, keepdims=True))).astype(BF16)
        dg5_ref[...] += jnp.sum(de * en, axis=0, keepdims=True)
        dx2_ref[...] = dx3 + _dot_nt(dgt, wpg_ref[...])

    r = _rows(tm, D)
    return _pcall(body, name="ple_fwd_bwd", grid=(T // tm,),
                  in_specs=[_rows(tm, PLE), r, r, _full((PLE, D)), _full((1, D)), _full((D, D))],
                  out_specs=[_full((8, LANES)), r, r, r, _full((1, D))],
                  out_shape=[SDS((8, LANES), F32), SDS((T, D), F32), SDS((T, D), BF16), SDS((T, D), BF16), SDS((1, D), F32)],
                  sem=("arbitrary",))(p, x2, tgt, wple, g5, wpg)


def _bwd_mlp_down(dx2, d, g4, w2, u, tm):
    T = dx2.shape[0]

    def body(dx_ref, d_ref, g_ref, w_ref, u_ref, dd_ref, da_ref, dg_ref):
        @pl.when(pl.program_id(0) == 0)
        def _():
            dg_ref[...] = jnp.zeros_like(dg_ref)

        dd, dg = _rms_bwd(dx_ref[...], d_ref[...], g_ref[...])
        dg_ref[...] += dg
        ddb = dd.astype(BF16)
        dd_ref[...] = ddb
        du = _dot_nt(ddb, w_ref[...])
        da_ref[...] = (du * (2.0 * jnp.sqrt(u_ref[...].astype(F32)))).astype(BF16)

    return _pcall(body, name="bwd_mlp_down", grid=(T // tm,),
                  in_specs=[_rows(tm, D), _rows(tm, D), _full((1, D)), _full((D_FF, D)), _rows(tm, D_FF)],
                  out_specs=[_rows(tm, D), _rows(tm, D_FF), _full((1, D))],
                  out_shape=[SDS((T, D), BF16), SDS((T, D_FF), BF16), SDS((1, D), F32)],
                  sem=("arbitrary",))(dx2, d, g4, w2, u)


def _bwd_mlp_up(da, w1, x1, g3, dx2, tm):
    T = dx2.shape[0]

    def body(da_ref, w_ref, x_ref, g_ref, dx2_ref, dx1_ref, dg_ref):
        @pl.when(pl.program_id(0) == 0)
        def _():
            dg_ref[...] = jnp.zeros_like(dg_ref)

        dh = _dot_nt(da_ref[...], w_ref[...])
        dx, dg = _rms_bwd(dh, x_ref[...], g_ref[...])
        dg_ref[...] += dg
        dx1_ref[...] = dx2_ref[...] + dx

    return _pcall(body, name="bwd_mlp_up", grid=(T // tm,),
                  in_specs=[_rows(tm, D_FF), _full((D, D_FF)), _rows(tm, D), _full((1, D)), _rows(tm, D)],
                  out_specs=[_rows(tm, D), _full((1, D))],
                  out_shape=[SDS((T, D), F32), SDS((1, D), F32)], sem=("arbitrary",))(da, w1, x1, g3, dx2)


def _bwd_mix(dx1, yo, g2, wo, z, au, bu, wmu, wsu, om, tm):
    T = dx1.shape[0]

    def body(dx_ref, yo_ref, g_ref, wo_ref, ga_ref, gb_ref, au_ref, bu_ref, wmu_ref, wsu_ref, om_ref,
             dyo_ref, dg_ref, dau_ref, dbu_ref, dga_ref, dgb_ref, dos_ref, dl_ref, dot_ref):
        @pl.when(pl.program_id(0) == 0)
        def _():
            dg_ref[...] = jnp.zeros_like(dg_ref)

        dyo, dg = _rms_bwd(dx_ref[...], yo_ref[...], g_ref[...])
        dg_ref[...] += dg
        dyob = dyo.astype(BF16)
        dyo_ref[...] = dyob
        dy = _dot_nt(dyob, wo_ref[...])
        sa = _sigmoid(ga_ref[...])
        sb = _sigmoid(gb_ref[...])
        dau = (dy * sa).astype(BF16)
        dbu = (dy * sb).astype(BF16)
        dau_ref[...] = dau
        dbu_ref[...] = dbu
        dga_ref[...] = (dy * au_ref[...] * sa * (1.0 - sa)).astype(BF16)
        dgb_ref[...] = (dy * bu_ref[...] * sb * (1.0 - sb)).astype(BF16)
        dom = _dot_nt(dau, wmu_ref[...])
        dos_ref[...] = _dot_nt(dbu, wsu_ref[...])
        prod = dom * om_ref[...]
        sub = lax.broadcasted_iota(jnp.int32, (8, tm), 0)
        for pr in range(MLA_HEADS // 2):
            sl = slice(LANES * pr, LANES * (pr + 1))
            pt = prod[:, sl].T
            d0 = jnp.sum(pt[0:64], axis=0, keepdims=True)
            d1 = jnp.sum(pt[64:128], axis=0, keepdims=True)
            dl_ref[pr, 0] = jnp.where(sub == 0, d0, jnp.where(sub == 1, d1, 0.0))
            dot_ref[0, sl, :] = dom[:, sl].T.astype(BF16)

    r = _rows(tm, D)
    w = _full((D, D))
    return _pcall(body, name="bwd_mix", grid=(T // tm,),
                  in_specs=[r, r, _full((1, D)), w, _rows(tm, D, 1), _rows(tm, D, 2), r, r, w, w, r],
                  out_specs=[r, _full((1, D)), r, r, r, r, r, pl.BlockSpec((MLA_HEADS // 2, 1, 8, tm), lambda i: (0, i, 0, 0)),
                             pl.BlockSpec((1, D, tm), lambda i: (i, 0, 0))],
                  out_shape=[SDS((T, D), BF16), SDS((1, D), F32), SDS((T, D), BF16), SDS((T, D), BF16), SDS((T, D), BF16),
                             SDS((T, D), BF16), SDS((T, D), F32), SDS((MLA_HEADS // 2, T // tm, 8, tm), F32),
                             SDS((T // tm, D, tm), BF16)],
                  sem=("arbitrary",))(dx1, yo, g2, wo, z, z, au, bu, wmu, wsu, om)


def _mla_bwd(qt, km, kt, vm, dot, lse, delta, tb):
    T = km.shape[0]
    nb = T // tb
    cc = ATT_COLS

    def body(qt_ref, k_ref, kt_ref, v_ref, dot_ref, l_ref, dl_ref, dqt_ref, dkt_ref, dvt_ref,
             s_ref, dp_ref, p_ref, ds_ref, vh_ref):
        j = pl.program_id(1)

        @pl.when(j == 0)
        def _():
            dqt_ref[...] = jnp.zeros_like(dqt_ref)

        dkt_ref[...] = jnp.zeros_like(dkt_ref)
        dvt_ref[...] = jnp.zeros_like(dvt_ref)
        lo = lax.broadcasted_iota(jnp.int32, (tb, LANES), 1) < 64
        key = lax.broadcasted_iota(jnp.int32, (tb, cc), 0)
        qry = lax.broadcasted_iota(jnp.int32, (tb, cc), 1)
        v = v_ref[...]
        vh_ref[0] = jnp.where(lo, v, jnp.zeros_like(v))
        vh_ref[1] = jnp.where(lo, jnp.zeros_like(v), v)

        def scores(i, slot):
            for hh in range(2):
                sl = slice(LANES * hh, LANES * (hh + 1))
                s_ref[slot, hh] = _dot(k_ref[:, sl], qt_ref[i, sl, :])
                dp_ref[slot, hh] = _dot(vh_ref[hh], dot_ref[i])

        def grads(i, slot, diagonal):
            lse_i = l_ref[0, i]
            delta_i = dl_ref[0, i]
            for hh in range(2):
                for c in range(tb // cc):
                    cols = slice(cc * c, cc * (c + 1))
                    p = jnp.exp2(s_ref[slot, hh, :, cols] * MLA_LOG2_SCALE - lse_i[hh:hh + 1, cols])
                    if diagonal:
                        p = jnp.where(key <= qry + cc * c, p, 0.0)
                    p_ref[hh, :, cols] = p.astype(BF16)
                    ds_ref[hh, :, cols] = (p * (dp_ref[slot, hh, :, cols] - delta_i[hh:hh + 1, cols]) * MLA_SCALE).astype(BF16)
            for hh in range(2):
                sl = slice(LANES * hh, LANES * (hh + 1))
                half = slice(64 * hh, 64 * (hh + 1))
                dvt_ref[0, half, :] += _dot_nt(dot_ref[i, half, :], p_ref[hh])
                real = slice(LANES * hh, LANES * hh + MLA_NOPE + MLA_ROPE)
                dkt_ref[0, real, :] += _dot_nt(qt_ref[i, real, :], ds_ref[hh])
                dqt_ref[i, real, :] += _dot(kt_ref[0, real, :], ds_ref[hh])

        n_off = nb - 1 - j

        def step(u, carry):
            i0 = j + 1 + 2 * u
            scores(i0 + 1, 1)
            grads(i0, 0, False)
            scores(jnp.where(i0 + 2 < nb, i0 + 2, j), 0)
            grads(i0 + 1, 1, False)
            return carry

        scores(jnp.where(n_off > 0, j + 1, j), 0)
        lax.fori_loop(0, n_off // 2, step, 0)

        @pl.when(n_off % 2 == 1)
        def _():
            scores(j, 1)
            grads(nb - 1, 0, False)
            grads(j, 1, True)

        @pl.when(n_off % 2 == 0)
        def _():
            grads(j, 0, True)

    blk = lambda w: pl.BlockSpec((tb, w), lambda p, j: (j, p))
    stat = pl.BlockSpec((1, nb, 8, tb), lambda p, j: (p, 0, 0, 0))
    pair_t = lambda w: pl.BlockSpec((nb, w, tb), lambda p, j: (0, p, 0))
    blk_t = lambda w: pl.BlockSpec((1, w, tb), lambda p, j: (j, p, 0))
    return _pcall(body, name="mla_bwd", grid=(MLA_HEADS // 2, nb),
                  in_specs=[pair_t(256), blk(256), blk_t(256), blk(LANES), pair_t(LANES), stat, stat],
                  out_specs=[pair_t(256), blk_t(256), blk_t(LANES)],
                  out_shape=[SDS((nb, 2048, tb), F32), SDS((nb, 2048, tb), F32), SDS((nb, D, tb), F32)],
                  scratch=[pltpu.VMEM((2, 2, tb, tb), F32), pltpu.VMEM((2, 2, tb, tb), F32), pltpu.VMEM((2, tb, tb), BF16),
                           pltpu.VMEM((2, tb, tb), BF16), pltpu.VMEM((2, tb, LANES), BF16)],
                  sem=("parallel", "arbitrary"))(qt, km, kt, vm, dot, lse, delta)


def _swa_bwd(sinks, qs, ks, vs, do, o, lse):
    T = qs.shape[0]
    nb, cur, prev = _swa_specs(T)

    def body(sink_ref, q_ref, kc_ref, kp_ref, vc_ref, vp_ref, do_ref, o_ref, l_ref,
             dq_ref, dkc_ref, dkp_ref, dvc_ref, dvp_ref, dsink_ref, kb_ref, vb_ref, s_ref, dp_ref, p_ref, ds_ref, dkt_ref, dvt_ref):
        n = pl.program_id(0)

        @pl.when(n == 0)
        def _():
            dsink_ref[...] = jnp.zeros_like(dsink_ref)

        mask = _swa_mask(n)
        lo = lax.broadcasted_iota(jnp.int32, (WINDOW, LANES), 1) < 64
        hi = jnp.logical_not(lo)
        lane8 = lax.broadcasted_iota(jnp.int32, (8, LANES), 1)
        for g in range(2):
            gs = slice(LANES * g, LANES * (g + 1))
            kb_ref[g] = jnp.concatenate([kp_ref[:, gs], kc_ref[:, gs]], axis=0)
            vb_ref[g] = jnp.concatenate([vp_ref[:, gs], vc_ref[:, gs]], axis=0)

        def head(h):
            sl = slice(LANES * (h // 2), LANES * (h // 2 + 1))
            hm = lo if h % 2 == 0 else hi
            qp = q_ref[:, sl]
            return hm, sl, jnp.where(hm, qp, jnp.zeros_like(qp)), jnp.where(hm, do_ref[:, sl], 0.0).astype(BF16)

        for h in range(SWA_HEADS):
            _, _, qh, dom = head(h)
            s_ref[h] = _dot_nt(qh, kb_ref[h // 8])
            dp_ref[h] = _dot_nt(dom, vb_ref[h // 8])
        dsink = jnp.zeros((8, LANES), F32)
        for h in range(SWA_HEADS):
            hm, sl, _, _ = head(h)
            lse_h = jnp.max(jnp.where(hm, l_ref[:, sl], -jnp.inf), axis=1, keepdims=True)
            delta = jnp.sum(jnp.where(hm, do_ref[:, sl] * o_ref[:, sl], 0.0), axis=1, keepdims=True)
            p = jnp.exp(jnp.where(mask, s_ref[h] * SWA_SCALE, NEG) - lse_h)
            p_ref[h] = p.astype(BF16)
            ds_ref[h] = (p * (dp_ref[h] - delta) * SWA_SCALE).astype(BF16)
            d_sink = -jnp.sum(jnp.exp(sink_ref[h] - lse_h) * delta, axis=0, keepdims=True)
            dsink = dsink + jnp.where(lane8 == h, d_sink, 0.0)
        dsink_ref[...] += dsink
        for g in range(2):
            gs = slice(LANES * g, LANES * (g + 1))
            dkt_ref[...] = jnp.zeros_like(dkt_ref)
            dvt_ref[...] = jnp.zeros_like(dvt_ref)
            for j in range(4 * g, 4 * g + 4):
                sl = slice(LANES * j, LANES * (j + 1))
                q_t = q_ref[:, sl].astype(F32).T.astype(BF16)
                do_t = do_ref[:, sl].T.astype(BF16)
                dqs = []
                for hf, h in enumerate((2 * j, 2 * j + 1)):
                    rows = slice(64 * hf, 64 * (hf + 1))
                    dvt_ref[rows, :] += _dot(do_t[rows], p_ref[h])
                    dkt_ref[rows, :] += _dot(q_t[rows], ds_ref[h])
                    dqs.append(_dot(ds_ref[h], kb_ref[g]))
                dq_ref[:, sl] = jnp.where(lo, dqs[0], dqs[1])
            dkb = dkt_ref[...].T
            dvb = dvt_ref[...].T
            dkp_ref[:, gs] = dkb[:WINDOW]
            dkc_ref[:, gs] = dkb[WINDOW:]
            dvp_ref[:, gs] = dvb[:WINDOW]
            dvc_ref[:, gs] = dvb[WINDOW:]

    band = pltpu.VMEM((2, 2 * WINDOW, LANES), BF16)
    return _pcall(body, name="swa_bwd", grid=(nb,),
                  in_specs=[pl.BlockSpec(memory_space=pltpu.SMEM), cur(D), cur(256), prev(256), cur(256), prev(256),
                            cur(D), cur(D), cur(D)],
                  out_specs=[cur(D), cur(256), cur(256), cur(256), cur(256), _full((8, LANES))],
                  out_shape=[SDS((T, D), F32), SDS((T, 256), F32), SDS((T, 256), F32), SDS((T, 256), F32), SDS((T, 256), F32),
                             SDS((8, LANES), F32)],
                  scratch=[band, band, pltpu.VMEM((SWA_HEADS, WINDOW, 2 * WINDOW), F32),
                           pltpu.VMEM((SWA_HEADS, WINDOW, 2 * WINDOW), F32), pltpu.VMEM((SWA_HEADS, WINDOW, 2 * WINDOW), BF16),
                           pltpu.VMEM((SWA_HEADS, WINDOW, 2 * WINDOW), BF16), pltpu.VMEM((LANES, 2 * WINDOW), F32),
                           pltpu.VMEM((LANES, 2 * WINDOW), F32)],
                  sem=("arbitrary",))(sinks, qs, ks, ks, vs, vs, do, o, lse)


def _bwd_qkv(dqm, dkm, dvm, dqs, dkc, dkp, dvc, dvp, z, gq, gkv, wqb, wkn, wv, tab_m, tab_s):
    T = z.shape[0]
    tm = WINDOW
    nb = T // tm
    per = dqm.shape[2] // tm

    tab_mt = [t.T for t in tab_m]
    half = MLA_ROPE // 2

    def rope_t(v, c, a, b):
        return v * c + pltpu.roll(v, LANES - half, 0) * a + pltpu.roll(v, half, 0) * b

    def rms_bwd_t(dy, x, g):
        r = lax.rsqrt(jnp.mean(x * x, axis=0, keepdims=True) + EPS)
        xn = x * r
        dn = dy * g
        return r * (dn - xn * jnp.mean(dn * xn, axis=0, keepdims=True)), jnp.sum(dy * xn, axis=1, keepdims=True)

    def body(dqm_ref, dkm_ref, dvm_ref, dqs_ref, dkc_ref, dkp_ref, dvc_ref, dvp_ref, qa_ref, kva_ref, gq_ref, gkv_ref,
             wqb_ref, wkn_ref, wv_ref, cmt_ref, amt_ref, bmt_ref, cs_ref, as_ref, bs_ref,
             dq_out, dkn_out, dv_out, dsq_ref, drest_ref, dgq_ref, dgkv_ref):
        i = pl.program_id(0)

        @pl.when(i == 0)
        def _():
            dgq_ref[...] = jnp.zeros_like(dgq_ref)
            dgkv_ref[...] = jnp.zeros_like(dgkv_ref)

        cmt, amt, bmt = cmt_ref[...], -amt_ref[...], -bmt_ref[...]
        cs, as_, bs = cs_ref[...], -as_ref[...], -bs_ref[...]
        row = lax.broadcasted_iota(jnp.int32, (LANES, tm), 0)
        nope = row < MLA_NOPE
        roped = jnp.logical_and(row >= MLA_NOPE, row < MLA_NOPE + MLA_ROPE)
        dkr = jnp.zeros((LANES, tm), F32)
        for h in range(MLA_HEADS):
            sl = slice(LANES * h, LANES * (h + 1))
            dq_out[0, sl, :] = rope_t(dqm_ref[0, sl, :], cmt, amt, bmt).astype(BF16)
            dk_h = dkm_ref[0, sl, :]
            dkn_out[0, sl, :] = jnp.where(nope, dk_h, 0.0).astype(BF16)
            dkr = dkr + jnp.where(roped, dk_h, 0.0)
        dv_out[0] = dvm_ref[0].astype(BF16)
        dqn = _dot(wqb_ref[...], dq_out[0])
        dkvn = _dot(wkn_ref[...], dkn_out[0]) + _dot(wv_ref[...], dv_out[0])
        dqa, dgq = rms_bwd_t(dqn, qa_ref[...].T, gq_ref[...])
        dkva, dgkv = rms_bwd_t(dkvn, kva_ref[...].T, gkv_ref[...])
        dgq_ref[...] += dgq
        dgkv_ref[...] += dgkv
        for j in range(D // LANES):
            sl = slice(LANES * j, LANES * (j + 1))
            dsq_ref[:, sl] = _rope(dqs_ref[:, sl], cs, as_, bs, SWA_HD // 2).astype(BF16)
        keep = (i < nb - 1).astype(F32)
        drest_ref[:, 0:256] = dqa.T.astype(BF16)
        for j in range(2):
            sl = slice(LANES * j, LANES * (j + 1))
            dk = dkc_ref[:, sl] + keep * dkp_ref[:, sl]
            drest_ref[:, 256 + LANES * j:256 + LANES * (j + 1)] = _rope(dk, cs, as_, bs, SWA_HD // 2).astype(BF16)
        drest_ref[:, 512:768] = (dvc_ref[...] + keep * dvp_ref[...]).astype(BF16)
        drest_ref[:, 768:896] = dkva.T.astype(BF16)
        drest_ref[:, 896:1024] = rope_t(dkr, cmt, amt, bmt).T.astype(BF16)

    nxt = pl.BlockSpec((tm, 256), lambda i: (jnp.minimum(i + 1, nb - 1), 0))
    tab = [_rows(tm, LANES)] * 3
    tab_t = [pl.BlockSpec((LANES, tm), lambda i: (0, i))] * 3
    blk_t = lambda w: pl.BlockSpec((1, w, tm), lambda i: (i // per, 0, i % per))
    return _pcall(body, name="bwd_qkv", grid=(nb,),
                  in_specs=[blk_t(2048), blk_t(2048), blk_t(1024), _rows(tm, 1024), _rows(tm, 256), nxt,
                            _rows(tm, 256), nxt, _rows(tm, 256, 12), _rows(tm, 128, 30), _full((Q_LORA, 1)), _full((KV_LORA, 1)),
                            _full((Q_LORA, 2048)), _full((KV_LORA, 2048)), _full((KV_LORA, 1024))] + tab_t + tab,
                  out_specs=[blk_t(2048), blk_t(2048), blk_t(1024), _rows(tm, 1024), _rows(tm, 1024),
                             _full((Q_LORA, 1)), _full((KV_LORA, 1))],
                  out_shape=[SDS(dqm.shape, BF16), SDS(dkm.shape, BF16), SDS(dvm.shape, BF16), SDS((T, 1024), BF16),
                             SDS((T, 1024), BF16), SDS((Q_LORA, 1), F32), SDS((KV_LORA, 1), F32)],
                  sem=("arbitrary",))(dqm, dkm, dvm, dqs, dkc, dkp, dvc, dvp, z, z, gq.reshape(Q_LORA, 1),
                                      gkv.reshape(KV_LORA, 1), wqb, wkn, wv, *tab_mt, *tab_s)


def _bwd_in(dsq, dga, dgb, drest, w_in_p, x, g1, dx1, tm):
    T = x.shape[0]

    def body(a_ref, b_ref, c_ref, d_ref, w_ref, x_ref, g_ref, dx1_ref, dx_ref, dg_ref):
        @pl.when(pl.program_id(0) == 0)
        def _():
            dg_ref[...] = jnp.zeros_like(dg_ref)

        dh = (_dot_nt(a_ref[...], w_ref[:, 0:1024]) + _dot_nt(b_ref[...], w_ref[:, 1024:2048])
              + _dot_nt(c_ref[...], w_ref[:, 2048:3072]) + _dot_nt(d_ref[...], w_ref[:, 3072:4096]))
        dx, dg = _rms_bwd(dh, x_ref[...], g_ref[...])
        dg_ref[...] += dg
        dx_ref[...] = dx1_ref[...] + dx

    r = _rows(tm, D)
    return _pcall(body, name="bwd_in", grid=(T // tm,),
                  in_specs=[r, r, r, r, _full((D, NZ)), r, _full((1, D)), r],
                  out_specs=[r, _full((1, D))], out_shape=[SDS((T, D), F32), SDS((1, D), F32)],
                  sem=("arbitrary",))(dsq, dga, dgb, drest, w_in_p, x, g1, dx1)


def _wgrad(a, g, name, into=None):
    T, K = a.shape
    N = g.shape[1]
    tk, tn, tt = min(K, 1024), min(N, 1024), min(T, 1024)
    if into is not None:
        buf, weight = into
        _, row0, lane0 = PACK_AT[weight]
        shard = {n: (r, c) for n, r, c in BIG}[weight]
        assert lane0 == 0 and shard[1] == D and tk % shard[0] == 0
        per_step = tk // shard[0]
    assert K % tk == 0 and N % tn == 0 and T % tt == 0, (a.shape, g.shape)
    steps = T // tt

    def body(a_ref, g_ref, *rest):
        o_ref, acc_ref = rest[-2:]
        t = pl.program_id(2)

        @pl.when(t == 0)
        def _():
            acc_ref[...] = jnp.zeros_like(acc_ref)

        acc_ref[...] += _dot_tn(a_ref[...].astype(BF16), g_ref[...].astype(BF16))

        @pl.when(t == steps - 1)
        def _():
            o_ref[...] = acc_ref[...].astype(o_ref.dtype).reshape(o_ref.shape)

    in_specs = [pl.BlockSpec((tt, tk), lambda k, n, t: (t, k)), pl.BlockSpec((tt, tn), lambda k, n, t: (t, n))]
    if into is None:
        return _pcall(body, name=name, grid=(K // tk, N // tn, steps), in_specs=in_specs,
                      out_specs=pl.BlockSpec((tk, tn), lambda k, n, t: (k, n)), out_shape=SDS((K, N), F32),
                      scratch=[pltpu.VMEM((tk, tn), F32)], sem=("parallel", "parallel", "arbitrary"))(a, g)
    assert row0 % shard[0] == 0 and (K // tk) * (N // tn) * per_step == N_CHIPS
    return _pcall(body, name=name, grid=(K // tk, N // tn, steps), in_specs=in_specs + [ANY],
                  out_specs=pl.BlockSpec((per_step, shard[0], tn), lambda k, n, t: (k + n, row0 // shard[0], 0)),
                  out_shape=SDS(buf.shape, buf.dtype),
                  scratch=[pltpu.VMEM((tk, tn), F32)], sem=("parallel", "parallel", "arbitrary"), aliases={2: 0})(a, g, buf)


def _wgrad_t(at, g, name):
    nblk, K, tt = at.shape
    N = g.shape[1]
    tk = min(K, 1024)
    per_step = 4 if nblk % 4 == 0 else 1
    assert K % tk == 0 and g.shape[0] == nblk * tt

    def body(a_ref, g_ref, o_ref):
        @pl.when(pl.program_id(1) == 0)
        def _():
            o_ref[...] = jnp.zeros_like(o_ref)

        acc = _dot(a_ref[0], g_ref[0:tt, :].astype(BF16))
        for b in range(1, per_step):
            acc = acc + _dot(a_ref[b], g_ref[tt * b:tt * (b + 1), :].astype(BF16))
        o_ref[...] += acc

    return _pcall(body, name=name, grid=(K // tk, nblk // per_step),
                  in_specs=[pl.BlockSpec((per_step, tk, tt), lambda k, t: (t, k, 0)),
                            pl.BlockSpec((per_step * tt, N), lambda k, t: (t, 0))],
                  out_specs=pl.BlockSpec((tk, N), lambda k, t: (k, 0)), out_shape=SDS((K, N), F32),
                  sem=("parallel", "arbitrary"))(at, g)


def _adamw(w, packed_g, m, v, name):
    _, R, C = w.shape
    _, row0, lane0 = PACK_AT[name]
    tr = min(R, 256 if row0 % 256 == 0 else 128)
    assert row0 % tr == 0 and R % tr == 0

    def body(w_ref, g_ref, m_ref, v_ref, go_ref, d_ref, m2_ref, v2_ref):
        g_ = g_ref[:, lane0:lane0 + C]
        go_ref[0] = g_
        m2 = ADAM_B1 * m_ref[0] + (1.0 - ADAM_B1) * g_
        v2 = ADAM_B2 * v_ref[0] + (1.0 - ADAM_B2) * jnp.square(g_)
        m_hat = m2 / (1.0 - ADAM_B1 ** ADAM_STEP)
        v_hat = v2 / (1.0 - ADAM_B2 ** ADAM_STEP)
        d_ref[0] = -ADAM_LR * (m_hat / (jnp.sqrt(v_hat) + ADAM_EPS) + ADAM_WD * w_ref[0])
        m2_ref[0] = m2
        v2_ref[0] = v2

    r = pl.BlockSpec((1, tr, C), lambda i: (0, i, 0))
    return _pcall(body, name="adamw_" + name, grid=(R // tr,),
                  in_specs=[r, pl.BlockSpec((tr, D), lambda i: (row0 // tr + i, 0)), r, r], out_specs=[r] * 4,
                  out_shape=[SDS((1, R, C), F32)] * 4, sem=("parallel",))(w, packed_g, m, v)


def _adamw_small(w, parts, m, v):
    def body(w_ref, p_ref, m_ref, v_ref, g_ref, d_ref, m2_ref, v2_ref):
        g_ = p_ref[0]
        for k in range(1, N_DEV):
            g_ = g_ + p_ref[k]
        g_ref[...] = g_
        m2 = ADAM_B1 * m_ref[...] + (1.0 - ADAM_B1) * g_
        v2 = ADAM_B2 * v_ref[...] + (1.0 - ADAM_B2) * jnp.square(g_)
        m_hat = m2 / (1.0 - ADAM_B1 ** ADAM_STEP)
        v_hat = v2 / (1.0 - ADAM_B2 ** ADAM_STEP)
        d_ref[...] = -ADAM_LR * (m_hat / (jnp.sqrt(v_hat) + ADAM_EPS) + ADAM_WD * w_ref[...])
        m2_ref[...] = m2
        v2_ref[...] = v2

    s = _full((8, D))
    return _pcall(body, name="adamw_small", grid=(1,), in_specs=[s, _full((N_DEV, 8, D)), s, s], out_specs=[s] * 4,
                  out_shape=[SDS((8, D), F32)] * 4, sem=("arbitrary",))(w, parts, m, v)


ANY = pl.BlockSpec(memory_space=pl.ANY)


def _place():
    x, y, c = lax.axis_index("x"), lax.axis_index("y"), lax.axis_index("c")
    chips = [(1 - x, y), (x, 1 - y), (1 - x, 1 - y)]
    return x, y, c, chips


def _all_gather(wpk):
    rows = wpk.shape[0]
    HALF = rows // 2
    assert HALF % 16 == 0

    def body(in_ref, out_ref, send_sems, recv_sems):
        x, y, c, chips = _place()
        half = pl.ds(pl.multiple_of(c * HALF, 16), HALF)
        other = pl.ds(pl.multiple_of((1 - c) * HALF, 16), HALF)

        def copy(k, src, dst, to):
            return pltpu.make_async_remote_copy(src_ref=src, dst_ref=dst, send_sem=send_sems.at[k], recv_sem=recv_sems.at[k],
                                                device_id=to, device_id_type=MESH)

        first = [copy(k, in_ref.at[half], out_ref.at[2 * x + y, half], (cx, cy, c)) for k, (cx, cy) in enumerate(chips)]
        for cp in first:
            cp.start()
        passed = []
        for k, (cx, cy) in enumerate(chips):
            slot = out_ref.at[2 * cx + cy, half]
            copy(k, slot, slot, (x, y, c)).wait_recv()
            fwd = copy(3 + k, slot, slot, (x, y, 1 - c))
            fwd.start()
            passed.append(fwd)
        for k, (cx, cy) in enumerate(chips):
            slot = out_ref.at[2 * cx + cy, other]
            copy(3 + k, slot, slot, (x, y, c)).wait_recv()
        for cp in first + passed:
            cp.wait_send()

    return _pcall(body, name="all_gather_weights", in_specs=[ANY], out_specs=ANY,
                  out_shape=SDS((N_CHIPS, rows, D), BF16),
                  scratch=[pltpu.SemaphoreType.DMA((6,)), pltpu.SemaphoreType.DMA((6,))])(wpk)


HBM = pl.BlockSpec(memory_space=pltpu.HBM)
SEM = pl.BlockSpec(memory_space=pltpu.SEMAPHORE)
DATAFLOW = pltpu.SideEffectType.DATAFLOW_SIDE_EFFECTING


def _in_hbm(a):
    return pltpu.with_memory_space_constraint(a, pltpu.HBM)


def _gather_late_start(wpk, after):
    rows = wpk.shape[0]

    def body(in_ref, land_ref, after_ref, send_sems, recv_sems, in_thru, land_thru, token):
        x, y, c, chips = _place()
        for k, (cx, cy) in enumerate(chips):
            pltpu.make_async_remote_copy(src_ref=in_ref, dst_ref=land_ref.at[2 * x + y], send_sem=send_sems.at[k],
                                         recv_sem=recv_sems.at[k], device_id=(cx, cy, c), device_id_type=MESH).start()
        token[...] = jnp.zeros_like(token)

    return pl.pallas_call(
        body, name="gather_late_start",
        out_shape=(pltpu.SemaphoreType.DMA((3,)), pltpu.SemaphoreType.DMA((3,)), pltpu.HBM(wpk.shape, wpk.dtype),
                   pltpu.HBM((N_CHIPS, rows, D), wpk.dtype), SDS((8, LANES), F32)),
        in_specs=(HBM, HBM, ANY), out_specs=(SEM, SEM, HBM, HBM, pl.BlockSpec(memory_space=pltpu.VMEM)),
        input_output_aliases={0: 2, 1: 3}, compiler_params=pltpu.CompilerParams(has_side_effects=DATAFLOW),
    )(_in_hbm(wpk), _in_hbm(lax.empty((N_CHIPS, rows, D), wpk.dtype)), after)


def _gather_late_wait(send_sems, recv_sems, in_thru, land_thru, after):
    def body(in_ref, land_ref, send_sems, recv_sems, after_ref, after2_ref, in_dead, got_ref):
        x, y, c, chips = _place()
        for k, (cx, cy) in enumerate(chips):
            cp = pltpu.make_async_remote_copy(src_ref=in_ref, dst_ref=land_ref.at[2 * cx + cy], send_sem=send_sems.at[k],
                                              recv_sem=recv_sems.at[k], device_id=(cx, cy, c), device_id_type=MESH)
            cp.wait_send()
            cp.wait_recv()

    return pl.pallas_call(
        body, name="gather_late_wait",
        out_shape=(pltpu.HBM(in_thru.shape, in_thru.dtype), pltpu.HBM(land_thru.shape, land_thru.dtype)),
        in_specs=(HBM, HBM, SEM, SEM, ANY, ANY), out_specs=(HBM, HBM), input_output_aliases={0: 0, 1: 1},
        compiler_params=pltpu.CompilerParams(has_side_effects=DATAFLOW),
    )(in_thru, land_thru, send_sems, recv_sems, *after)[1]


def _rs_sibling(gpk):
    HALF = gpk.shape[1] // 2

    def body(in_ref, out_ref, send_sem, recv_sem):
        x, y, c, _ = _place()
        theirs = pl.ds(pl.multiple_of((1 - c) * HALF, 8), HALF)
        cp = pltpu.make_async_remote_copy(src_ref=in_ref.at[:, theirs], dst_ref=out_ref, send_sem=send_sem, recv_sem=recv_sem,
                                          device_id=(x, y, 1 - c), device_id_type=MESH)
        cp.start()
        cp.wait()

    return _pcall(body, name="rs_sibling", in_specs=[ANY], out_specs=ANY, out_shape=SDS((N_CHIPS, HALF, D), F32),
                  scratch=[pltpu.SemaphoreType.DMA, pltpu.SemaphoreType.DMA])(gpk)


def _rs_add_sibling(cidx, gpk, got):
    HALF = got.shape[1]
    th = HALF // 4
    nh = HALF // th
    assert th % 16 == 0

    def body(c_ref, a_ref, b_ref, o_ref):
        o_ref[...] = (a_ref[...] + b_ref[...]).astype(BF16)

    gs = pltpu.PrefetchScalarGridSpec(
        num_scalar_prefetch=1, grid=(N_CHIPS, nh),
        in_specs=[pl.BlockSpec((1, th, D), lambda j, i, c: (j, c[0] * nh + i, 0)), pl.BlockSpec((1, th, D), lambda j, i, c: (j, i, 0))],
        out_specs=pl.BlockSpec((1, th, D), lambda j, i, c: (j, i, 0)))
    return pl.pallas_call(body, name="rs_add_sibling", grid_spec=gs, out_shape=SDS((N_CHIPS, HALF, D), BF16),
                          compiler_params=pltpu.CompilerParams(dimension_semantics=("parallel", "parallel"),
                                                               vmem_limit_bytes=48 << 20))(cidx, gpk, got)


def _rs_chips_start(part, small, after):
    def body(p_ref, s_ref, land_ref, sland_ref, after_ref, send_sems, recv_sems, p_thru, s_thru, land_thru, sland_thru, token):
        x, y, c, chips = _place()
        for k, (cx, cy) in enumerate(chips):
            pltpu.make_async_remote_copy(src_ref=p_ref.at[2 * cx + cy], dst_ref=land_ref.at[2 * x + y], send_sem=send_sems.at[k],
                                         recv_sem=recv_sems.at[k], device_id=(cx, cy, c), device_id_type=MESH).start()
        peers = [(x, y, 1 - c)] + [(cx, cy, c) for cx, cy in chips] + [(cx, cy, 1 - c) for cx, cy in chips]
        for k, to in enumerate(peers):
            pltpu.make_async_remote_copy(src_ref=s_ref, dst_ref=sland_ref.at[4 * x + 2 * y + c], send_sem=send_sems.at[3 + k],
                                         recv_sem=recv_sems.at[3 + k], device_id=to, device_id_type=MESH).start()
        token[...] = jnp.zeros_like(token)

    return pl.pallas_call(
        body, name="rs_chips_start",
        out_shape=(pltpu.SemaphoreType.DMA((10,)), pltpu.SemaphoreType.DMA((10,)), pltpu.HBM(part.shape, part.dtype),
                   pltpu.HBM(small.shape, small.dtype), pltpu.HBM(part.shape, part.dtype), pltpu.HBM((N_DEV, 8, D), F32),
                   SDS((8, LANES), F32)),
        in_specs=(HBM, HBM, HBM, HBM, ANY), out_specs=(SEM, SEM, HBM, HBM, HBM, HBM, pl.BlockSpec(memory_space=pltpu.VMEM)),
        input_output_aliases={0: 2, 1: 3, 2: 4, 3: 5}, compiler_params=pltpu.CompilerParams(has_side_effects=DATAFLOW),
    )(_in_hbm(part), _in_hbm(small), _in_hbm(lax.empty(part.shape, part.dtype)), _in_hbm(lax.empty((N_DEV, 8, D), F32)), after)


def _rs_chips_wait(send_sems, recv_sems, p_thru, s_thru, land_thru, sland_thru, after):
    def body(p_ref, s_ref, land_ref, sland_ref, send_sems, recv_sems, *after_and_outputs):
        x, y, c, chips = _place()
        for k, (cx, cy) in enumerate(chips):
            cp = pltpu.make_async_remote_copy(src_ref=p_ref.at[0], dst_ref=land_ref.at[2 * cx + cy], send_sem=send_sems.at[k],
                                              recv_sem=recv_sems.at[k], device_id=(cx, cy, c), device_id_type=MESH)
            cp.wait_send()
            cp.wait_recv()
        peers = [(x, y, 1 - c)] + [(cx, cy, c) for cx, cy in chips] + [(cx, cy, 1 - c) for cx, cy in chips]
        for k, (px, py, pc) in enumerate(peers):
            cp = pltpu.make_async_remote_copy(src_ref=s_ref, dst_ref=sland_ref.at[4 * px + 2 * py + pc], send_sem=send_sems.at[3 + k],
                                              recv_sem=recv_sems.at[3 + k], device_id=(px, py, pc), device_id_type=MESH)
            cp.wait_send()
            cp.wait_recv()

    hbm = lambda a: pltpu.HBM(a.shape, a.dtype)
    outs = pl.pallas_call(
        body, name="rs_chips_wait", out_shape=(hbm(p_thru), hbm(s_thru), hbm(land_thru), hbm(sland_thru)),
        in_specs=(HBM, HBM, HBM, HBM, SEM, SEM) + (ANY,) * len(after), out_specs=(HBM, HBM, HBM, HBM),
        input_output_aliases={0: 0, 1: 1, 2: 2, 3: 3}, compiler_params=pltpu.CompilerParams(has_side_effects=DATAFLOW),
    )(p_thru, s_thru, land_thru, sland_thru, send_sems, recv_sems, *after)
    return outs[0], outs[2], outs[3]


def _rs_add_chips(qidx, part, parts):
    HALF = part.shape[1]
    th = HALF // 4
    assert th % 16 == 0

    def body(q_ref, own_ref, p_ref, o_ref):
        for me in range(N_CHIPS):
            @pl.when(q_ref[0] == me)
            def _(me=me):
                t = [(own_ref[0] if j == me else p_ref[j]).astype(F32) for j in range(N_CHIPS)]
                o_ref[...] = ((t[0] + t[1]) + t[2]) + t[3]

    gs = pltpu.PrefetchScalarGridSpec(
        num_scalar_prefetch=1, grid=(HALF // th,),
        in_specs=[pl.BlockSpec((1, th, D), lambda i, q: (q[0], i, 0)), pl.BlockSpec((N_CHIPS, th, D), lambda i, q: (0, i, 0))],
        out_specs=pl.BlockSpec((th, D), lambda i, q: (i, 0)))
    return pl.pallas_call(body, name="rs_add_chips", grid_spec=gs, out_shape=SDS((HALF, D), F32),
                          compiler_params=pltpu.CompilerParams(dimension_semantics=("parallel",),
                                                               vmem_limit_bytes=48 << 20))(qidx, part, parts)


def _rs_join(mine, core, name):
    def body(in_ref, out_ref, send_sem, recv_sem):
        x, y, c, _ = _place()
        cp = pltpu.make_async_remote_copy(src_ref=in_ref, dst_ref=out_ref, send_sem=send_sem, recv_sem=recv_sem,
                                          device_id=(x, y, 1 - c), device_id_type=MESH)
        cp.start()
        cp.wait()

    theirs = _pcall(body, name=name, in_specs=[ANY], out_specs=ANY, out_shape=SDS(mine.shape, F32),
                    scratch=[pltpu.SemaphoreType.DMA, pltpu.SemaphoreType.DMA])(mine)
    return jnp.where(core == 0, jnp.concatenate([mine, theirs]), jnp.concatenate([theirs, mine]))


def _reduce_late_start(gpk, after):
    rows = gpk.shape[1]
    HALF = rows // 2
    assert HALF % 16 == 0

    def body(in_ref, land_ref, after_ref, send_sems, recv_sems, in_thru, land_thru, token):
        x, y, c, chips = _place()
        me = 4 * x + 2 * y + c
        peers = [(x, y, 1 - c)] + [(cx, cy, c) for cx, cy in chips] + [(cx, cy, 1 - c) for cx, cy in chips]
        for k, (px, py, pc) in enumerate(peers):
            src = in_ref.at[2 * px + py, pl.ds(pl.multiple_of(pc * HALF, 16), HALF)]
            pltpu.make_async_remote_copy(src_ref=src, dst_ref=land_ref.at[me], send_sem=send_sems.at[k], recv_sem=recv_sems.at[k],
                                         device_id=(px, py, pc), device_id_type=MESH).start()
        token[...] = jnp.zeros_like(token)

    return pl.pallas_call(
        body, name="reduce_late_start",
        out_shape=(pltpu.SemaphoreType.DMA((7,)), pltpu.SemaphoreType.DMA((7,)), pltpu.HBM(gpk.shape, gpk.dtype),
                   pltpu.HBM((N_DEV, HALF, D), gpk.dtype), SDS((8, LANES), F32)),
        in_specs=(HBM, HBM, ANY), out_specs=(SEM, SEM, HBM, HBM, pl.BlockSpec(memory_space=pltpu.VMEM)),
        input_output_aliases={0: 2, 1: 3}, compiler_params=pltpu.CompilerParams(has_side_effects=DATAFLOW),
    )(_in_hbm(gpk), _in_hbm(lax.empty((N_DEV, HALF, D), gpk.dtype)), after)


def _reduce_late_wait(send_sems, recv_sems, in_thru, land_thru, after):
    def body(in_ref, land_ref, send_sems, recv_sems, after_ref, in_out, got_ref):
        x, y, c, chips = _place()
        peers = [(x, y, 1 - c)] + [(cx, cy, c) for cx, cy in chips] + [(cx, cy, 1 - c) for cx, cy in chips]
        for k, (px, py, pc) in enumerate(peers):
            cp = pltpu.make_async_remote_copy(src_ref=land_ref.at[0], dst_ref=land_ref.at[4 * px + 2 * py + pc],
                                              send_sem=send_sems.at[k], recv_sem=recv_sems.at[k],
                                              device_id=(px, py, pc), device_id_type=MESH)
            cp.wait_send()
            cp.wait_recv()

    return pl.pallas_call(
        body, name="reduce_late_wait",
        out_shape=(pltpu.HBM(in_thru.shape, in_thru.dtype), pltpu.HBM(land_thru.shape, land_thru.dtype)),
        in_specs=(HBM, HBM, SEM, SEM, ANY), out_specs=(HBM, HBM), input_output_aliases={0: 0, 1: 1},
        compiler_params=pltpu.CompilerParams(has_side_effects=DATAFLOW),
    )(in_thru, land_thru, send_sems, recv_sems, after)


def _reduce_late_add(didx, gpk, parts):
    HALF = parts.shape[1]
    th = HALF // 4
    nh = HALF // th
    assert th % 16 == 0

    def body(d_ref, own_ref, p_ref, o_ref):
        for me in range(N_DEV):
            @pl.when(d_ref[0] == me)
            def _(me=me):
                t = [(own_ref[0] if j == me else p_ref[j]).astype(F32) for j in range(N_DEV)]
                o_ref[...] = ((((((t[0] + t[1]) + t[2]) + t[3]) + t[4]) + t[5]) + t[6]) + t[7]

    gs = pltpu.PrefetchScalarGridSpec(
        num_scalar_prefetch=1, grid=(nh,),
        in_specs=[pl.BlockSpec((1, th, D), lambda i, d: (d[1], d[2] * nh + i, 0)), pl.BlockSpec((N_DEV, th, D), lambda i, d: (0, i, 0))],
        out_specs=pl.BlockSpec((th, D), lambda i, d: (i, 0)))
    return pl.pallas_call(body, name="reduce_late_add", grid_spec=gs, out_shape=SDS((HALF, D), F32),
                          compiler_params=pltpu.CompilerParams(dimension_semantics=("parallel",),
                                                               vmem_limit_bytes=48 << 20))(didx, gpk, parts)


def _pack_early(b, dtype):
    lanes = lambda a: jnp.pad(a.astype(dtype), ((0, 0), (0, D - a.shape[1])))
    pair = jnp.concatenate([b["w_q_b"].astype(dtype), b["w_ple"].astype(dtype), jnp.zeros((256, D - 640), dtype)], axis=1)
    return jnp.concatenate([lanes(b["w_in"]), pair, lanes(b["w_kv_b"])], axis=0)


def _pack_late(b, dtype):
    return jnp.concatenate([b[n].astype(dtype) for n in ("w_mla_up", "w_swa_up", "w_out", "w_ple_gate", "w_mlp_up", "w_mlp_down")],
                           axis=0)


def _unpack_shards(pk, which):
    return {n: pk[PACK_AT[n][1]:PACK_AT[n][1] + r, PACK_AT[n][2]:PACK_AT[n][2] + c] for n, r, c in BIG if PACK_AT[n][0] == which}


def _full_weights(gathered, own, chip, which):
    own_b = _unpack_shards(own, which)
    per_chip = [{n: jnp.where(chip == j, own_b[n], blk) for n, blk in _unpack_shards(gathered[j], which).items()}
                for j in range(N_CHIPS)]
    out = {}
    for n in own_b:
        shards = [pc[n] for pc in per_chip]
        if n == "w_in":
            out["w_in_p"] = _w_in_internal(shards)
        else:
            out[n] = jnp.concatenate(shards, axis=1 if n in COL_SHARDED else 0)
    return out


def _split_full_grads(grads, pack, dtype):
    shard = {n: (r, c) for n, r, c in BIG}
    chunks = []
    for j in range(N_CHIPS):
        blocks = {}
        for n, g in grads.items():
            if n == "w_in_p":
                blocks["w_in"] = _w_in_grad_shard(g, j)
                continue
            r, c = shard[n]
            blocks[n] = g[:, j * c:(j + 1) * c] if n in COL_SHARDED else g[j * r:(j + 1) * r]
        chunks.append(pack(blocks, dtype))
    return jnp.stack(chunks)


W_IN_SHARD = 936
W_IN_SEGMENTS = ((0, 256, (3072,)), (256, 384, (3840,)), (384, 416, (4032,)), (416, 1440, (0,)), (1440, 1504, (3328, 3392)),
                 (1504, 1568, (3456, 3520)), (1568, 1632, (3584, 3648)), (1632, 1696, (3712, 3776)), (1696, 3744, (1024,)))


def _w_in_internal(shards):
    def cols(a, b):
        out = []
        for j, s in enumerate(shards):
            lo, hi = max(a, W_IN_SHARD * j), min(b, W_IN_SHARD * (j + 1))
            if lo < hi:
                out.append(s[:, lo - W_IN_SHARD * j:hi - W_IN_SHARD * j])
        return out

    pieces = {}
    for a, b, places in W_IN_SEGMENTS:
        for at in places:
            pieces[at] = cols(a, b)
    zeros = lambda n: [jnp.zeros((D, n), shards[0].dtype)]
    pieces[3968] = zeros(64)
    pieces[4064] = zeros(32)
    return jnp.concatenate([piece for at in sorted(pieces) for piece in pieces[at]], axis=1)


def _w_in_grad_shard(g, j):
    def internal(a, b):
        out = []
        while a < b:
            end = min(b, (a // D + 1) * D)
            out.append(g[a // D][:, a % D:a % D + end - a])
            a = end
        return out

    out = []
    for a, b, places in W_IN_SEGMENTS:
        lo, hi = max(a, W_IN_SHARD * j), min(b, W_IN_SHARD * (j + 1))
        if lo < hi:
            parts = [internal(at + lo - a, at + hi - a) for at in places]
            if len(parts) == 1:
                out += parts[0]
            else:
                assert len(parts[0]) == len(parts[1]) == 1
                out.append(parts[0][0] + parts[1][0])
    return jnp.concatenate(out, axis=1)


def _local_step(x, p, tgt, w, small, late_weights, late_grads_out):
    T = x.shape[0]
    tm = 256
    tb = 256
    w_in_p = w["w_in_p"]
    wqb = jnp.pad(w["w_q_b"].reshape(Q_LORA, MLA_HEADS, 96), ((0, 0), (0, 0), (0, 32))).reshape(Q_LORA, 2048)
    wkv = w["w_kv_b"].reshape(KV_LORA, MLA_HEADS, 128)
    wkn = jnp.pad(wkv[:, :, :64], ((0, 0), (0, 0), (0, 64))).reshape(KV_LORA, 2048)
    wv = wkv[:, :, 64:].reshape(KV_LORA, 1024)
    tab_m = _rope_tables(T, "mla")
    tab_s = _rope_tables(T, "swa")
    g1, gq, gkv, sinks = small["g_mix_pre"], small["g_q_a"], small["g_kv_a"], small["sinks"]
    g2, g3, g4, g5 = small["g_mix_post"], small["g_mlp_pre"], small["g_mlp_post"], small["g_ple"]
    sink_vec = sinks.reshape(SWA_HEADS)

    z, h1 = _fwd_in(x, g1, w_in_p, tm)
    qn, kvn, km, vm, qt, kt, vt, qs, ks, vs = _fwd_qkv(z, gq, gkv, wqb, wkn, wv, tab_m, tab_s, tb)
    om, lse_m = _mla_fwd(qt, km, vt, tb)
    os_, lse_s = _swa_fwd(sink_vec, qs, ks, vs)
    w = {**w, **late_weights((om, os_))}
    y, yo, au, bu, x1 = _fwd_mix(om, os_, z, x, w["w_mla_up"], w["w_swa_up"], w["w_out"], g2, tm)
    h2, u = _fwd_mlp_up(x1, g3, w["w_mlp_up"], tm)
    d, x2 = _fwd_mlp_down(u, w["w_mlp_down"], x1, g4, tm)
    loss, dx2, dgt, de0, dg5 = _ple_fwd_bwd(p, x2, tgt, w["w_ple"], g5, w["w_ple_gate"], tm)

    dd, da, dg4 = _bwd_mlp_down(dx2, d, g4, w["w_mlp_down"], u, tm)
    dx1, dg3 = _bwd_mlp_up(da, w["w_mlp_up"], x1, g3, dx2, tm)
    dyo, dg2, dau, dbu, dga, dgb, dos, delta_m, dom_t = _bwd_mix(dx1, yo, g2, w["w_out"], z, au, bu, w["w_mla_up"],
                                                                w["w_swa_up"], om, tb)
    gpk_late = lax.empty((N_CHIPS, PACK_ROWS["late"], D), BF16)
    for weight, a_, g_ in (("w_mla_up", om, dau), ("w_swa_up", os_, dbu), ("w_out", y, dyo), ("w_ple_gate", x2, dgt),
                           ("w_mlp_up", h2, da), ("w_mlp_down", u, dd)):
        gpk_late = _wgrad(a_, g_, "wgrad_" + weight[2:], into=(gpk_late, weight))
    token = late_grads_out(gpk_late)
    delta_m = delta_m + token[0, 0]
    dqm, dkm, dvm = _mla_bwd(qt, km, kt, vm, dom_t, lse_m, delta_m, tb)
    dqs, dkc, dkp, dvc, dvp, dsink = _swa_bwd(sink_vec, qs, ks, vs, dos, os_, lse_s)
    dqb, dknb, dvb, dsq, drest, dgq, dgkv = _bwd_qkv(dqm, dkm, dvm, dqs, dkc, dkp, dvc, dvp, z, gq, gkv, wqb, wkn, wv,
                                                      tab_m, tab_s)
    gx, dg1 = _bwd_in(dsq, dga, dgb, drest, w_in_p, x, g1, dx1, tm)

    g_in_p = [_wgrad(h1, dsq, "wgrad_in_sq"), _wgrad(h1, dga, "wgrad_in_ga"), _wgrad(h1, dgb, "wgrad_in_gb"),
              _wgrad(h1, drest, "wgrad_in_rest")]
    g_qb_p = _wgrad_t(dqb, qn, "wgrad_q_b").T
    g_kn_p = _wgrad_t(dknb, kvn, "wgrad_kv_b_nope").T
    g_v_p = _wgrad_t(dvb, kvn, "wgrad_kv_b_v").T
    grads = {
        "w_in_p": g_in_p,
        "w_q_b": g_qb_p.reshape(Q_LORA, MLA_HEADS, 128)[:, :, :96].reshape(Q_LORA, 1536),
        "w_kv_b": jnp.concatenate([g_kn_p.reshape(KV_LORA, MLA_HEADS, 128)[:, :, :64], g_v_p.reshape(KV_LORA, MLA_HEADS, 64)],
                                  axis=2).reshape(KV_LORA, 2048),
        "w_ple": _wgrad(p, de0, "wgrad_ple"),
    }
    small_grads = {"g_mix_pre": dg1, "g_q_a": dgq.reshape(1, Q_LORA), "g_kv_a": dgkv.reshape(1, KV_LORA), "sinks": dsink[0:1, 0:SWA_HEADS], "g_mix_post": dg2,
                   "g_mlp_pre": dg3, "g_mlp_post": dg4, "g_ple": dg5}
    return loss, gx, grads, small_grads


def _pack_small(vals, fill, scalar=None):
    wide = [vals[n] for n, k in SMALL if k == D]
    narrow = [vals[n] for n, k in SMALL if k != D]
    used = sum(k for _, k in SMALL if k != D)
    last = jnp.concatenate(narrow + [jnp.full((1, D - used), fill, F32)], axis=1)
    rest = jnp.full((2, D), fill, F32)
    if scalar is not None:
        rest = jnp.concatenate([jnp.concatenate([scalar, rest[0:1, 1:]], axis=1), rest[1:2]], axis=0)
    return jnp.concatenate(wide + [last, rest], axis=0)


def _unpack_small(pk):
    out, row, off = {}, 0, 0
    for n, k in SMALL:
        if k == D:
            out[n] = pk[row:row + 1]
            row += 1
    for n, k in SMALL:
        if k != D:
            out[n] = pk[5:6, off:off + k]
            off += k
    return out


def kernel(x, p, g_mix_pre, w_in, g_q_a, w_q_b, g_kv_a, w_kv_b, sinks, w_mla_up, w_swa_up, w_out, g_mix_post, g_mlp_pre, w_mlp_up, w_mlp_down, g_mlp_post, w_ple, g_ple, w_ple_gate, loss_target, m_g_mix_pre, m_w_in, m_g_q_a, m_w_q_b, m_g_kv_a, m_w_kv_b, m_sinks, m_w_mla_up, m_w_swa_up, m_w_out, m_g_mix_post, m_g_mlp_pre, m_w_mlp_up, m_w_mlp_down, m_g_mlp_post, m_w_ple, m_g_ple, m_w_ple_gate, v_g_mix_pre, v_w_in, v_g_q_a, v_w_q_b, v_g_kv_a, v_w_kv_b, v_sinks, v_w_mla_up, v_w_swa_up, v_w_out, v_g_mix_post, v_g_mlp_pre, v_w_mlp_up, v_w_mlp_down, v_g_mlp_post, v_w_ple, v_g_ple, v_w_ple_gate):
    given = dict(locals())
    big_w = {n: given[n][0] for n, _, _ in BIG}
    small_w = {n: given[n] for n, _ in SMALL}
    small_m = {n: given["m_" + n] for n, _ in SMALL}
    small_v = {n: given["v_" + n] for n, _ in SMALL}

    core = lax.axis_index("c")
    chip = 2 * lax.axis_index("x") + lax.axis_index("y")
    core_i = core.astype(jnp.int32).reshape(1)
    chip_i = chip.astype(jnp.int32).reshape(1)
    dev_i = jnp.stack([2 * chip + core, chip, core]).astype(jnp.int32)

    own_early = _pack_early(big_w, BF16)
    own_late = _pack_late(big_w, BF16)
    got_early = _all_gather(own_early)
    late_flight = _gather_late_start(own_late, got_early)
    weights = _full_weights(got_early, own_early, chip, "early")
    step_small = {**small_w, "g_mix_pre": small_w["g_mix_pre"] + late_flight[4][0, 0]}

    def late_weights(after):
        return _full_weights(_gather_late_wait(*late_flight[:4], after), own_late, chip, "late")

    flight = {}

    def late_grads_out(gpk_late):
        flight["late"] = _reduce_late_start(gpk_late, dev_i)
        return flight["late"][4]

    loss_blk, gx, grads, small_grads = _local_step(x[0], p[0, 0], loss_target[0], weights, step_small, late_weights,
                                                   late_grads_out)

    gpk = _split_full_grads(grads, _pack_early, F32)
    got = _rs_sibling(gpk)
    part = _rs_add_sibling(core_i, gpk, got)
    small_own = _pack_small(small_grads, 0.0, loss_blk[0:1, 0:1])
    early_flight = _rs_chips_start(part, small_own, dev_i)

    out_g, out_d, out_m, out_v = {}, {}, {}, {}
    gpk_late, parts_late = _reduce_late_wait(*flight["late"][:4], early_flight[6])
    joined_late = _rs_join(_reduce_late_add(dev_i, gpk_late, parts_late), core, "rs_join_late")
    for n, _, _ in BIG:
        if PACK_AT[n][0] == "late":
            out_g[n], out_d[n], out_m[n], out_v[n] = _adamw(given[n], joined_late, given["m_" + n], given["v_" + n], n)

    part, parts, small_parts = _rs_chips_wait(*early_flight[:6], [out_d[n] for n in out_d])
    joined_early = _rs_join(_rs_add_chips(chip_i, part, parts), core, "rs_join_early")
    for n, _, _ in BIG:
        if PACK_AT[n][0] == "early":
            out_g[n], out_d[n], out_m[n], out_v[n] = _adamw(given[n], joined_early, given["m_" + n], given["v_" + n], n)

    mine = (lax.broadcasted_iota(jnp.int32, (N_DEV, 1, 1), 0) == dev_i[0])
    g_small_pk, d_small_pk, m_small_pk, v_small_pk = _adamw_small(
        _pack_small(small_w, 0.0), jnp.where(mine, small_own[None], small_parts), _pack_small(small_m, 0.0),
        _pack_small(small_v, 1.0))
    loss = g_small_pk[6, 0]
    for out, pk in ((out_g, g_small_pk), (out_d, d_small_pk), (out_m, m_small_pk), (out_v, v_small_pk)):
        out.update(_unpack_small(pk))
    order = ["g_mix_pre", "w_in", "g_q_a", "w_q_b", "g_kv_a", "w_kv_b", "sinks", "w_mla_up", "w_swa_up", "w_out", "g_mix_post",
             "g_mlp_pre", "w_mlp_up", "w_mlp_down", "g_mlp_post", "w_ple", "g_ple", "w_ple_gate"]
    return (loss, gx[None], *[out_g[n] for n in order], *[out_d[n] for n in order], *[out_m[n] for n in order],
            *[out_v[n] for n in order])
```

```python
import math

import jax
import jax.numpy as jnp
from jax import lax
from jax.experimental import pallas as pl
from jax.experimental.pallas import tpu as pltpu

F32 = jnp.float32
BF16 = jnp.bfloat16
SDS = jax.ShapeDtypeStruct

D = 1024
D_FF = 4096
PLE = 256
Q_LORA = 256
KV_LORA = 128
MLA_HEADS = 16
MLA_NOPE = 64
MLA_ROPE = 32
SWA_HEADS = 16
SWA_HD = 64
WINDOW = 128
ROPE_THETA = 10000.0
EPS = 1e-6
NEG = -1e30
NZ = 4096
MLA_SCALE = (MLA_NOPE + MLA_ROPE) ** -0.5
LOG2_E = math.log2(math.e)
MLA_LOG2_SCALE = MLA_SCALE * LOG2_E
SWA_SCALE = SWA_HD ** -0.5

ADAM_LR = 0.001
ADAM_B1 = 0.9
ADAM_B2 = 0.999
ADAM_EPS = 1e-08
ADAM_WD = 0.01
ADAM_STEP = 10

LANES = 128
ATT_COLS = 128
VT_ROWS = 80
N_CHIPS = 4
N_DEV = 8
MESH = pl.DeviceIdType.MESH

NT = (((1,), (1,)), ((), ()))
TN = (((0,), (0,)), ((), ()))

BIG = (("w_in", 1024, 936), ("w_q_b", 256, 384), ("w_kv_b", 128, 512), ("w_mla_up", 256, 1024),
       ("w_swa_up", 256, 1024), ("w_out", 256, 1024), ("w_mlp_up", 1024, 1024), ("w_mlp_down", 1024, 1024),
       ("w_ple", 256, 256), ("w_ple_gate", 256, 1024))
COL_SHARDED = ("w_in", "w_q_b", "w_kv_b", "w_mlp_up", "w_ple")
PACK_AT = {"w_in": ("early", 0, 0), "w_q_b": ("early", 1024, 0), "w_ple": ("early", 1024, 384), "w_kv_b": ("early", 1280, 0),
           "w_mla_up": ("late", 0, 0), "w_swa_up": ("late", 256, 0), "w_out": ("late", 512, 0), "w_ple_gate": ("late", 768, 0),
           "w_mlp_up": ("late", 1024, 0), "w_mlp_down": ("late", 2048, 0)}
PACK_ROWS = {"early": 1408, "late": 3072}
SMALL = (("g_mix_pre", 1024), ("g_q_a", 256), ("g_kv_a", 128), ("sinks", 16), ("g_mix_post", 1024),
         ("g_mlp_pre", 1024), ("g_mlp_post", 1024), ("g_ple", 1024))


def _dot(a, b):
    return jnp.dot(a, b, preferred_element_type=F32)


def _dot_nt(a, b):
    return lax.dot_general(a, b, NT, preferred_element_type=F32)


def _dot_tn(a, b):
    return lax.dot_general(a, b, TN, preferred_element_type=F32)


def _pcall(body, *, name, out_shape, grid=(), in_specs=None, out_specs=None, scratch=(), sem=None, vmem_mb=48, aliases=None):
    params = dict(vmem_limit_bytes=vmem_mb << 20)
    if sem is not None:
        params["dimension_semantics"] = sem
    return pl.pallas_call(body, name=name, grid=grid, in_specs=in_specs, out_specs=out_specs, out_shape=out_shape,
                          scratch_shapes=list(scratch), input_output_aliases=aliases or {},
                          compiler_params=pltpu.CompilerParams(**params))


def _rows(tm, n, col=0):
    return pl.BlockSpec((tm, n), lambda i: (i, col))


def _full(shape):
    return pl.BlockSpec(shape, lambda i: (0,) * len(shape))


def _rms(x, g):
    r = lax.rsqrt(jnp.mean(x * x, axis=-1, keepdims=True) + EPS)
    return x * r * g


def _rms_bwd(dy, x, g):
    r = lax.rsqrt(jnp.mean(x * x, axis=-1, keepdims=True) + EPS)
    xn = x * r
    dn = dy * g
    dx = r * (dn - xn * jnp.mean(dn * xn, axis=-1, keepdims=True))
    return dx, jnp.sum(dy * xn, axis=0, keepdims=True)


def _sigmoid(x):
    return 1.0 / (1.0 + jnp.exp(-x))


def _rope(x, c, a, b, half):
    return x * c + pltpu.roll(x, LANES - half, 1) * a + pltpu.roll(x, half, 1) * b


def _rope_tables(T, kind):
    lane = jnp.arange(LANES)
    if kind == "mla":
        half = MLA_ROPE // 2
        rel = lane - MLA_NOPE
        on = (rel >= 0) & (rel < MLA_ROPE)
        d = MLA_ROPE
    else:
        half = SWA_HD // 2
        rel = lane % SWA_HD
        on = jnp.ones((LANES,), bool)
        d = SWA_HD
    first = on & (rel < half)
    second = on & (rel >= half)
    f = jnp.where(first, rel, rel - half).astype(F32)
    inv = jnp.exp(-math.log(ROPE_THETA) * f * (2.0 / d))
    ang = jnp.arange(T, dtype=F32)[:, None] * inv[None, :]
    cos, sin = jnp.cos(ang), jnp.sin(ang)
    c = jnp.where(on[None], cos, 1.0)
    a = jnp.where(first[None], -sin, 0.0)
    b = jnp.where(second[None], sin, 0.0)
    return c, a, b


def _fwd_in(x, g1, w_in_p, tm):
    T = x.shape[0]

    def body(x_ref, g_ref, w_ref, z_ref, h_ref):
        h = _rms(x_ref[...], g_ref[...]).astype(BF16)
        h_ref[...] = h
        z_ref[...] = _dot(h, w_ref[...])

    return _pcall(body, name="fwd_in", grid=(T // tm,),
                  in_specs=[_rows(tm, D), _full((1, D)), _full((D, NZ))],
                  out_specs=[_rows(tm, NZ), _rows(tm, D)],
                  out_shape=[SDS((T, NZ), F32), SDS((T, D), BF16)], sem=("parallel",))(x, g1, w_in_p)


def _fwd_qkv(z, gq, gkv, wqb, wkn, wv, tab_m, tab_s, tm):
    T = z.shape[0]
    wqb_t, wkn_t, wv_t = wqb.T, wkn.T, wv.T
    tab_mt = [t.T for t in tab_m]

    def body(qa_ref, sq_ref, skd_ref, svd_ref, kva_ref, kr_ref, gq_ref, gkv_ref, wkn_ref, wv_ref, wqbt_ref, wknt_ref, wvt_ref,
             cm_ref, am_ref, bm_ref, cmt_ref, amt_ref, bmt_ref, cs_ref, as_ref, bs_ref,
             qn_ref, kvn_ref, km_ref, vm_ref, qt_ref, kt_ref, vt_ref, qs_ref, ks_ref, vs_ref):
        qn = _rms(qa_ref[...], gq_ref[...])
        qn_ref[...] = qn.astype(BF16)
        kvn = _rms(kva_ref[...], gkv_ref[...])
        kvn_b = kvn.astype(BF16)
        kvn_ref[...] = kvn_b
        qn_t = qn.T.astype(BF16)
        kvn_t = kvn.T.astype(BF16)
        cm, am, bm = cm_ref[...], am_ref[...], bm_ref[...]
        cmt, amt, bmt = cmt_ref[...], amt_ref[...], bmt_ref[...]
        cs, as_, bs = cs_ref[...], as_ref[...], bs_ref[...]
        k_rope = _rope(kr_ref[...], cm, am, bm, MLA_ROPE // 2)
        k_rope_t = k_rope.T
        half = MLA_ROPE // 2
        vm_ref[...] = _dot(kvn_b, wv_ref[...]).astype(BF16)
        km_all = _dot(kvn_b, wkn_ref[...])
        v_t = _dot(wvt_ref[...], kvn_t)
        q_t = _dot(wqbt_ref[...], qn_t)
        k_t = _dot(wknt_ref[...], kvn_t)
        ones_row = jnp.where(lax.broadcasted_iota(jnp.int32, (64, tm), 0) == 0, 1.0, 0.0)
        for h in range(MLA_HEADS):
            sl = slice(LANES * h, LANES * (h + 1))
            vt_ref[0, sl, :] = jnp.concatenate([v_t[64 * h:64 * (h + 1)], ones_row], axis=0).astype(BF16)
            qh = q_t[sl]
            qt_ref[0, sl, :] = (qh * cmt + pltpu.roll(qh, LANES - half, 0) * amt + pltpu.roll(qh, half, 0) * bmt).astype(BF16)
            km_ref[:, sl] = (km_all[:, sl] + k_rope).astype(BF16)
            kt_ref[0, sl, :] = (k_t[sl] + k_rope_t).astype(BF16)
        for j in range(D // LANES):
            sl = slice(LANES * j, LANES * (j + 1))
            qs_ref[:, sl] = _rope(sq_ref[:, sl], cs, as_, bs, SWA_HD // 2).astype(BF16)
        for j in range(2):
            sl = slice(LANES * j, LANES * (j + 1))
            ks_ref[:, sl] = _rope(skd_ref[:, sl], cs, as_, bs, SWA_HD // 2).astype(BF16)
        vs_ref[...] = svd_ref[...].astype(BF16)

    tab = [_rows(tm, LANES)] * 3
    tab_t = [pl.BlockSpec((LANES, tm), lambda i: (0, i))] * 3
    return _pcall(body, name="fwd_qkv", grid=(T // tm,),
                  in_specs=[_rows(tm, 256, 12), _rows(tm, 1024, 0), _rows(tm, 256, 13), _rows(tm, 256, 14),
                            _rows(tm, 128, 30), _rows(tm, 128, 31), _full((1, Q_LORA)), _full((1, KV_LORA)),
                            _full((KV_LORA, 2048)), _full((KV_LORA, 1024)), _full((2048, Q_LORA)), _full((2048, KV_LORA)),
                            _full((1024, KV_LORA))] + tab + tab_t + tab,
                  out_specs=[_rows(tm, Q_LORA), _rows(tm, KV_LORA), _rows(tm, 2048), _rows(tm, 1024),
                             pl.BlockSpec((1, 2048, tm), lambda i: (i, 0, 0)), pl.BlockSpec((1, 2048, tm), lambda i: (i, 0, 0)),
                             pl.BlockSpec((1, 2048, tm), lambda i: (i, 0, 0)),
                             _rows(tm, 1024), _rows(tm, 256), _rows(tm, 256)],
                  out_shape=[SDS((T, Q_LORA), BF16), SDS((T, KV_LORA), BF16), SDS((T, 2048), BF16),
                             SDS((T, 1024), BF16), SDS((T // tm, 2048, tm), BF16), SDS((T // tm, 2048, tm), BF16),
                             SDS((T // tm, 2048, tm), BF16),
                             SDS((T, 1024), BF16), SDS((T, 256), BF16), SDS((T, 256), BF16)],
                  sem=("parallel",))(z, z, z, z, z, z, gq, gkv, wkn, wv, wqb_t, wkn_t, wv_t, *tab_m, *tab_mt, *tab_s)


def _mla_fwd(qt, km, vt, tb):
    T = km.shape[0]
    nb = T // tb
    cc = ATT_COLS

    per = 2 if nb % 2 == 0 else 1

    def body(q_ref, k_ref, vt_ref, o_ref, l_ref, s_ref, p_ref, al_ref, m_ref, acc_ref):
        for blk in range(per):
            one_block(per * pl.program_id(1) + blk, blk, q_ref, k_ref, vt_ref, o_ref, l_ref, s_ref, p_ref, al_ref, m_ref, acc_ref)

    def one_block(i, blk, q_ref, k_ref, vt_ref, o_ref, l_ref, s_ref, p_ref, al_ref, m_ref, acc_ref):
        m_ref[...] = jnp.full(m_ref.shape, NEG, F32)
        acc_ref[...] = jnp.zeros_like(acc_ref)
        p_ref[1] = jnp.zeros(p_ref.shape[1:], BF16)
        al_ref[1] = jnp.ones(al_ref.shape[1:], F32)
        key = lax.broadcasted_iota(jnp.int32, (tb, cc), 0)
        qry = lax.broadcasted_iota(jnp.int32, (tb, cc), 1)

        def scores(j, slot):
            off = pl.multiple_of(j * tb, tb)
            for hh in range(2):
                sl = slice(LANES * hh, LANES * (hh + 1))
                s_ref[slot, hh] = _dot(k_ref[pl.ds(off, tb), sl], q_ref[blk, sl, :])

        def softmax(slot, diagonal):
            chains = [(hh, slice(cc * c, cc * (c + 1)), c) for hh in range(2) for c in range(tb // cc)]

            def scaled(hh, cols, c):
                t = s_ref[slot, hh, :, cols] * MLA_LOG2_SCALE
                return jnp.where(key <= qry + cc * c, t, NEG) if diagonal else t

            tops = []
            for hh, cols, c in chains:
                if diagonal:
                    top = jnp.max(scaled(hh, cols, c), axis=0, keepdims=True)
                else:
                    top = jnp.max(s_ref[slot, hh, :, cols], axis=0, keepdims=True) * MLA_LOG2_SCALE
                m_old = m_ref[hh, :, cols]
                mn = jnp.maximum(m_old, top)
                m_ref[hh, :, cols] = mn
                al_ref[slot, hh, :, cols] = jnp.exp2(m_old - mn)
                tops.append(mn)
            for (hh, cols, c), mn in zip(chains, tops):
                p_ref[slot, hh, :, cols] = jnp.exp2(scaled(hh, cols, c) - mn).astype(BF16)

        def accumulate(j, slot):
            for hh in range(2):
                acc_ref[hh] = al_ref[slot, hh] * acc_ref[hh] + _dot(vt_ref[j, LANES * hh:LANES * hh + VT_ROWS, :], p_ref[slot, hh])

        def step(t, carry):
            scores(2 * t + 1, 1)
            accumulate(jnp.maximum(2 * t - 1, 0), 1)
            softmax(0, False)
            scores(2 * t + 2, 0)
            accumulate(2 * t, 0)
            softmax(1, False)
            return carry

        scores(0, 0)
        lax.fori_loop(0, i // 2, step, 0)

        @pl.when(i % 2 == 1)
        def _():
            scores(i, 1)
            accumulate(jnp.maximum(i - 2, 0), 1)
            softmax(0, False)
            accumulate(i - 1, 0)
            softmax(1, True)
            accumulate(i, 1)

        @pl.when(i % 2 == 0)
        def _():
            accumulate(jnp.maximum(i - 1, 0), 1)
            softmax(0, True)
            accumulate(i, 0)
        den = [acc_ref[hh, 64:65, :] for hh in range(2)]
        o_ref[tb * blk:tb * (blk + 1), :] = jnp.concatenate([acc_ref[hh, 0:64, :] / den[hh] for hh in range(2)], axis=0).T
        sub = lax.broadcasted_iota(jnp.int32, (8, tb), 0)
        lse = [m_ref[hh] + jnp.log(den[hh]) * LOG2_E for hh in range(2)]
        l_ref[0, blk] = jnp.where(sub == 0, lse[0], jnp.where(sub == 1, lse[1], 0.0))

    return _pcall(body, name="mla_fwd", grid=(MLA_HEADS // 2, nb // per),
                  in_specs=[pl.BlockSpec((per, 256, tb), lambda p, i: (i, p, 0)), pl.BlockSpec((T, 256), lambda p, i: (0, p)),
                            pl.BlockSpec((nb, 2 * LANES, tb), lambda p, i: (0, p, 0))],
                  out_specs=[pl.BlockSpec((per * tb, LANES), lambda p, i: (i, p)),
                             pl.BlockSpec((1, per, 8, tb), lambda p, i: (p, i, 0, 0))],
                  out_shape=[SDS((T, D), F32), SDS((MLA_HEADS // 2, nb, 8, tb), F32)],
                  scratch=[pltpu.VMEM((2, 2, tb, tb), F32), pltpu.VMEM((2, 2, tb, tb), BF16), pltpu.VMEM((2, 2, 1, tb), F32),
                           pltpu.VMEM((2, 1, tb), F32), pltpu.VMEM((2, VT_ROWS, tb), F32)],
                  sem=("parallel", "arbitrary"))(qt, km, vt)


def _swa_mask(n):
    row = lax.broadcasted_iota(jnp.int32, (WINDOW, 2 * WINDOW), 0)
    col = lax.broadcasted_iota(jnp.int32, (WINDOW, 2 * WINDOW), 1)
    rel = row - col + WINDOW
    return (rel >= 0) & (rel < WINDOW) & ((col >= WINDOW) | (n > 0))


def _swa_specs(T):
    nb = T // WINDOW
    cur = lambda w: pl.BlockSpec((WINDOW, w), lambda n: (n, 0))
    prev = lambda w: pl.BlockSpec((WINDOW, w), lambda n: (jnp.maximum(n - 1, 0), 0))
    return nb, cur, prev


def _swa_fwd(sinks, qs, ks, vs):
    T = qs.shape[0]
    nb, cur, prev = _swa_specs(T)

    def body(sink_ref, q_ref, kc_ref, kp_ref, vc_ref, vp_ref, o_ref, l_ref, kb_ref, vb_ref, s_ref, p_ref):
        n = pl.program_id(0)
        mask = _swa_mask(n)
        lo = lax.broadcasted_iota(jnp.int32, (WINDOW, LANES), 1) < 64
        hi = jnp.logical_not(lo)
        for g in range(2):
            gs = slice(LANES * g, LANES * (g + 1))
            kb_ref[g] = jnp.concatenate([kp_ref[:, gs], kc_ref[:, gs]], axis=0)
            vb_ref[g] = jnp.concatenate([vp_ref[:, gs], vc_ref[:, gs]], axis=0)
        for h in range(SWA_HEADS):
            qp = q_ref[:, LANES * (h // 2):LANES * (h // 2 + 1)]
            qh = jnp.where(lo if h % 2 == 0 else hi, qp, jnp.zeros_like(qp))
            s_ref[h] = _dot_nt(qh, kb_ref[h // 8])
        for j in range(SWA_HEADS // 2):
            sl = slice(LANES * j, LANES * (j + 1))
            lses = []
            for h in (2 * j, 2 * j + 1):
                s = jnp.where(mask, s_ref[h] * SWA_SCALE, NEG)
                sk = sink_ref[h]
                m = jnp.maximum(jnp.max(s, axis=1, keepdims=True), sk)
                e = jnp.exp(s - m)
                den = jnp.sum(e, axis=1, keepdims=True) + jnp.exp(sk - m)
                p_ref[h] = (e / den).astype(BF16)
                lses.append(jnp.broadcast_to(m + jnp.log(den), (WINDOW, LANES)))
            l_ref[:, sl] = jnp.where(lo, lses[0], lses[1])
        for j in range(SWA_HEADS // 2):
            vb = vb_ref[j // 4]
            o_ref[:, LANES * j:LANES * (j + 1)] = jnp.where(lo, _dot(p_ref[2 * j], vb), _dot(p_ref[2 * j + 1], vb))

    return _pcall(body, name="swa_fwd", grid=(nb,),
                  in_specs=[pl.BlockSpec(memory_space=pltpu.SMEM), cur(D), cur(256), prev(256), cur(256), prev(256)],
                  out_specs=[cur(D), cur(D)], out_shape=[SDS((T, D), F32)] * 2,
                  scratch=[pltpu.VMEM((2, 2 * WINDOW, LANES), BF16), pltpu.VMEM((2, 2 * WINDOW, LANES), BF16),
                           pltpu.VMEM((SWA_HEADS, WINDOW, 2 * WINDOW), F32), pltpu.VMEM((SWA_HEADS, WINDOW, 2 * WINDOW), BF16)],
                  sem=("parallel",))(sinks, qs, ks, ks, vs, vs)


def _fwd_mix(om, os_, z, x, wmu, wsu, wo, g2, tm):
    T = x.shape[0]

    def body(om_ref, os_ref, ga_ref, gb_ref, x_ref, wmu_ref, wsu_ref, wo_ref, g2_ref,
             y_ref, yo_ref, au_ref, bu_ref, x1_ref):
        au = _dot(om_ref[...].astype(BF16), wmu_ref[...])
        bu = _dot(os_ref[...].astype(BF16), wsu_ref[...])
        au_ref[...] = au
        bu_ref[...] = bu
        y = (_sigmoid(ga_ref[...]) * au + _sigmoid(gb_ref[...]) * bu).astype(BF16)
        y_ref[...] = y
        yo = _dot(y, wo_ref[...])
        yo_ref[...] = yo
        x1_ref[...] = x_ref[...] + _rms(yo, g2_ref[...])

    r = _rows(tm, D)
    w = _full((D, D))
    return _pcall(body, name="fwd_mix", grid=(T // tm,),
                  in_specs=[r, r, _rows(tm, D, 1), _rows(tm, D, 2), r, w, w, w, _full((1, D))],
                  out_specs=[r] * 5,
                  out_shape=[SDS((T, D), BF16), SDS((T, D), F32), SDS((T, D), F32), SDS((T, D), F32), SDS((T, D), F32)],
                  sem=("parallel",))(om, os_, z, z, x, wmu, wsu, wo, g2)


def _fwd_mlp_up(x1, g3, w1, tm):
    T = x1.shape[0]

    def body(x_ref, g_ref, w_ref, h_ref, u_ref):
        h = _rms(x_ref[...], g_ref[...]).astype(BF16)
        h_ref[...] = h
        u_ref[...] = jnp.square(jnp.maximum(_dot(h, w_ref[...]), 0.0)).astype(BF16)

    return _pcall(body, name="fwd_mlp_up", grid=(T // tm,),
                  in_specs=[_rows(tm, D), _full((1, D)), _full((D, D_FF))],
                  out_specs=[_rows(tm, D), _rows(tm, D_FF)],
                  out_shape=[SDS((T, D), BF16), SDS((T, D_FF), BF16)],
                  sem=("parallel",))(x1, g3, w1)


def _fwd_mlp_down(u, w2, x1, g4, tm):
    T = x1.shape[0]

    def body(u_ref, w_ref, x_ref, g_ref, d_ref, x2_ref):
        d = _dot(u_ref[...], w_ref[...])
        d_ref[...] = d
        x2_ref[...] = x_ref[...] + _rms(d, g_ref[...])

    return _pcall(body, name="fwd_mlp_down", grid=(T // tm,),
                  in_specs=[_rows(tm, D_FF), _full((D_FF, D)), _rows(tm, D), _full((1, D))],
                  out_specs=[_rows(tm, D), _rows(tm, D)], out_shape=[SDS((T, D), F32)] * 2,
                  sem=("parallel",))(u, w2, x1, g4)


def _ple_fwd_bwd(p, x2, tgt, wple, g5, wpg, tm):
    T = x2.shape[0]

    def body(p_ref, x2_ref, t_ref, wple_ref, g5_ref, wpg_ref, loss_ref, dx2_ref, dgt_ref, de0_ref, dg5_ref):
        @pl.when(pl.program_id(0) == 0)
        def _():
            loss_ref[...] = jnp.zeros_like(loss_ref)
            dg5_ref[...] = jnp.zeros_like(dg5_ref)

        e0 = _dot(p_ref[...].astype(BF16), wple_ref[...])
        g5 = g5_ref[...]
        r = lax.rsqrt(jnp.mean(e0 * e0, axis=-1, keepdims=True) + EPS)
        en = e0 * r
        e = en * g5
        x2 = x2_ref[...]
        s = _sigmoid(_dot(x2.astype(BF16), wpg_ref[...]))
        diff = x2 + s * e - t_ref[...]
        sq = jnp.sum(jnp.sum(diff * diff, axis=1, keepdims=True), axis=0, keepdims=True)
        loss_ref[...] += jnp.broadcast_to(sq * (0.5 / D), loss_ref.shape)
        dx3 = diff * (1.0 / D)
        de = dx3 * s
        dgt = (dx3 * e * s * (1.0 - s)).astype(BF16)
        dgt_ref[...] = dgt
        dn = de * g5
        de0_ref[...] = (r * (dn - en * jnp.mean(dn * en, axis=-1, keepdims=True))).astype(BF16)
        dg5_ref[...] += jnp.sum(de * en, axis=0, keepdims=True)
        dx2_ref[...] = dx3 + _dot_nt(dgt, wpg_ref[...])

    r = _rows(tm, D)
    return _pcall(body, name="ple_fwd_bwd", grid=(T // tm,),
                  in_specs=[_rows(tm, PLE), r, r, _full((PLE, D)), _full((1, D)), _full((D, D))],
                  out_specs=[_full((8, LANES)), r, r, r, _full((1, D))],
                  out_shape=[SDS((8, LANES), F32), SDS((T, D), F32), SDS((T, D), BF16), SDS((T, D), BF16), SDS((1, D), F32)],
                  sem=("arbitrary",))(p, x2, tgt, wple, g5, wpg)


def _bwd_mlp_down(dx2, d, g4, w2, u, tm):
    T = dx2.shape[0]

    def body(dx_ref, d_ref, g_ref, w_ref, u_ref, dd_ref, da_ref, dg_ref):
        @pl.when(pl.program_id(0) == 0)
        def _():
            dg_ref[...] = jnp.zeros_like(dg_ref)

        dd, dg = _rms_bwd(dx_ref[...], d_ref[...], g_ref[...])
        dg_ref[...] += dg
        ddb = dd.astype(BF16)
        dd_ref[...] = ddb
        du = _dot_nt(ddb, w_ref[...])
        da_ref[...] = (du * (2.0 * jnp.sqrt(u_ref[...].astype(F32)))).astype(BF16)

    return _pcall(body, name="bwd_mlp_down", grid=(T // tm,),
                  in_specs=[_rows(tm, D), _rows(tm, D), _full((1, D)), _full((D_FF, D)), _rows(tm, D_FF)],
                  out_specs=[_rows(tm, D), _rows(tm, D_FF), _full((1, D))],
                  out_shape=[SDS((T, D), BF16), SDS((T, D_FF), BF16), SDS((1, D), F32)],
                  sem=("arbitrary",))(dx2, d, g4, w2, u)


def _bwd_mlp_up(da, w1, x1, g3, dx2, tm):
    T = dx2.shape[0]

    def body(da_ref, w_ref, x_ref, g_ref, dx2_ref, dx1_ref, dg_ref):
        @pl.when(pl.program_id(0) == 0)
        def _():
            dg_ref[...] = jnp.zeros_like(dg_ref)

        dh = _dot_nt(da_ref[...], w_ref[...])
        dx, dg = _rms_bwd(dh, x_ref[...], g_ref[...])
        dg_ref[...] += dg
        dx1_ref[...] = dx2_ref[...] + dx

    return _pcall(body, name="bwd_mlp_up", grid=(T // tm,),
                  in_specs=[_rows(tm, D_FF), _full((D, D_FF)), _rows(tm, D), _full((1, D)), _rows(tm, D)],
                  out_specs=[_rows(tm, D), _full((1, D))],
                  out_shape=[SDS((T, D), F32), SDS((1, D), F32)], sem=("arbitrary",))(da, w1, x1, g3, dx2)


def _bwd_mix(dx1, yo, g2, wo, z, au, bu, wmu, wsu, om, tm):
    T = dx1.shape[0]

    def body(dx_ref, yo_ref, g_ref, wo_ref, ga_ref, gb_ref, au_ref, bu_ref, wmu_ref, wsu_ref, om_ref,
             dyo_ref, dg_ref, dau_ref, dbu_ref, dga_ref, dgb_ref, dos_ref, dl_ref, dot_ref):
        @pl.when(pl.program_id(0) == 0)
        def _():
            dg_ref[...] = jnp.zeros_like(dg_ref)

        dyo, dg = _rms_bwd(dx_ref[...], yo_ref[...], g_ref[...])
        dg_ref[...] += dg
        dyob = dyo.astype(BF16)
        dyo_ref[...] = dyob
        dy = _dot_nt(dyob, wo_ref[...])
        sa = _sigmoid(ga_ref[...])
        sb = _sigmoid(gb_ref[...])
        dau = (dy * sa).astype(BF16)
        dbu = (dy * sb).astype(BF16)
        dau_ref[...] = dau
        dbu_ref[...] = dbu
        dga_ref[...] = (dy * au_ref[...] * sa * (1.0 - sa)).astype(BF16)
        dgb_ref[...] = (dy * bu_ref[...] * sb * (1.0 - sb)).astype(BF16)
        dom = _dot_nt(dau, wmu_ref[...])
        dos_ref[...] = _dot_nt(dbu, wsu_ref[...])
        prod = dom * om_ref[...]
        sub = lax.broadcasted_iota(jnp.int32, (8, tm), 0)
        for pr in range(MLA_HEADS // 2):
            sl = slice(LANES * pr, LANES * (pr + 1))
            pt = prod[:, sl].T
            d0 = jnp.sum(pt[0:64], axis=0, keepdims=True)
            d1 = jnp.sum(pt[64:128], axis=0, keepdims=True)
            dl_ref[pr, 0] = jnp.where(sub == 0, d0, jnp.where(sub == 1, d1, 0.0))
            dot_ref[0, sl, :] = dom[:, sl].T.astype(BF16)

    r = _rows(tm, D)
    w = _full((D, D))
    return _pcall(body, name="bwd_mix", grid=(T // tm,),
                  in_specs=[r, r, _full((1, D)), w, _rows(tm, D, 1), _rows(tm, D, 2), r, r, w, w, r],
                  out_specs=[r, _full((1, D)), r, r, r, r, r, pl.BlockSpec((MLA_HEADS // 2, 1, 8, tm), lambda i: (0, i, 0, 0)),
                             pl.BlockSpec((1, D, tm), lambda i: (i, 0, 0))],
                  out_shape=[SDS((T, D), BF16), SDS((1, D), F32), SDS((T, D), BF16), SDS((T, D), BF16), SDS((T, D), BF16),
                             SDS((T, D), BF16), SDS((T, D), F32), SDS((MLA_HEADS // 2, T // tm, 8, tm), F32),
                             SDS((T // tm, D, tm), BF16)],
                  sem=("arbitrary",))(dx1, yo, g2, wo, z, z, au, bu, wmu, wsu, om)


def _mla_bwd(qt, km, kt, vm, dot, lse, delta, tb):
    T = km.shape[0]
    nb = T // tb
    cc = ATT_COLS

    per = 2 if nb % 2 == 0 else 1

    def body(qt_ref, k_ref, kt_ref, v_ref, dot_ref, l_ref, dl_ref, dqt_ref, dkt_ref, dvt_ref,
             s_ref, dp_ref, p_ref, ds_ref, vh_ref):
        @pl.when(pl.program_id(1) == 0)
        def _():
            dqt_ref[...] = jnp.zeros_like(dqt_ref)

        dkt_ref[...] = jnp.zeros_like(dkt_ref)
        dvt_ref[...] = jnp.zeros_like(dvt_ref)
        for blk in range(per):
            one_block(per * pl.program_id(1) + blk, blk, qt_ref, k_ref, kt_ref, v_ref, dot_ref, l_ref, dl_ref, dqt_ref, dkt_ref,
                      dvt_ref, s_ref, dp_ref, p_ref, ds_ref, vh_ref)

    def one_block(j, blk, qt_ref, k_ref, kt_ref, v_ref, dot_ref, l_ref, dl_ref, dqt_ref, dkt_ref, dvt_ref,
                  s_ref, dp_ref, p_ref, ds_ref, vh_ref):
        lo = lax.broadcasted_iota(jnp.int32, (tb, LANES), 1) < 64
        key = lax.broadcasted_iota(jnp.int32, (tb, cc), 0)
        qry = lax.broadcasted_iota(jnp.int32, (tb, cc), 1)
        rows_j = slice(tb * blk, tb * (blk + 1))
        v = v_ref[rows_j, :]
        vh_ref[0] = jnp.where(lo, v, jnp.zeros_like(v))
        vh_ref[1] = jnp.where(lo, jnp.zeros_like(v), v)

        def scores(i, slot):
            for hh in range(2):
                sl = slice(LANES * hh, LANES * (hh + 1))
                s_ref[slot, hh] = _dot(k_ref[rows_j, sl], qt_ref[i, sl, :])
                dp_ref[slot, hh] = _dot(vh_ref[hh], dot_ref[i])

        def grads(i, slot, diagonal):
            lse_i = l_ref[0, i]
            delta_i = dl_ref[0, i]
            for hh in range(2):
                for c in range(tb // cc):
                    cols = slice(cc * c, cc * (c + 1))
                    p = jnp.exp2(s_ref[slot, hh, :, cols] * MLA_LOG2_SCALE - lse_i[hh:hh + 1, cols])
                    if diagonal:
                        p = jnp.where(key <= qry + cc * c, p, 0.0)
                    p_ref[hh, :, cols] = p.astype(BF16)
                    ds_ref[hh, :, cols] = (p * (dp_ref[slot, hh, :, cols] - delta_i[hh:hh + 1, cols]) * MLA_SCALE).astype(BF16)
            for hh in range(2):
                sl = slice(LANES * hh, LANES * (hh + 1))
                half = slice(64 * hh, 64 * (hh + 1))
                dvt_ref[blk, half, :] += _dot_nt(dot_ref[i, half, :], p_ref[hh])
                real = slice(LANES * hh, LANES * hh + MLA_NOPE + MLA_ROPE)
                dkt_ref[blk, real, :] += _dot_nt(qt_ref[i, real, :], ds_ref[hh])
                dqt_ref[i, real, :] += _dot(kt_ref[blk, real, :], ds_ref[hh])

        n_off = nb - 1 - j

        def step(u, carry):
            i0 = j + 1 + 2 * u
            scores(i0 + 1, 1)
            grads(i0, 0, False)
            scores(jnp.where(i0 + 2 < nb, i0 + 2, j), 0)
            grads(i0 + 1, 1, False)
            return carry

        scores(jnp.where(n_off > 0, j + 1, j), 0)
        lax.fori_loop(0, n_off // 2, step, 0)

        @pl.when(n_off % 2 == 1)
        def _():
            scores(j, 1)
            grads(nb - 1, 0, False)
            grads(j, 1, True)

        @pl.when(n_off % 2 == 0)
        def _():
            grads(j, 0, True)

    blk = lambda w: pl.BlockSpec((per * tb, w), lambda p, j: (j, p))
    stat = pl.BlockSpec((1, nb, 8, tb), lambda p, j: (p, 0, 0, 0))
    pair_t = lambda w: pl.BlockSpec((nb, w, tb), lambda p, j: (0, p, 0))
    blk_t = lambda w: pl.BlockSpec((per, w, tb), lambda p, j: (j, p, 0))
    return _pcall(body, name="mla_bwd", grid=(MLA_HEADS // 2, nb // per),
                  in_specs=[pair_t(256), blk(256), blk_t(256), blk(LANES), pair_t(LANES), stat, stat],
                  out_specs=[pair_t(256), blk_t(256), blk_t(LANES)],
                  out_shape=[SDS((nb, 2048, tb), F32), SDS((nb, 2048, tb), F32), SDS((nb, D, tb), F32)],
                  scratch=[pltpu.VMEM((2, 2, tb, tb), F32), pltpu.VMEM((2, 2, tb, tb), F32), pltpu.VMEM((2, tb, tb), BF16),
                           pltpu.VMEM((2, tb, tb), BF16), pltpu.VMEM((2, tb, LANES), BF16)],
                  sem=("parallel", "arbitrary"))(qt, km, kt, vm, dot, lse, delta)


def _swa_bwd(sinks, qs, ks, vs, do, o, lse):
    T = qs.shape[0]
    nb, cur, prev = _swa_specs(T)

    def body(sink_ref, q_ref, kc_ref, kp_ref, vc_ref, vp_ref, do_ref, o_ref, l_ref,
             dq_ref, dkc_ref, dkp_ref, dvc_ref, dvp_ref, dsink_ref, kb_ref, vb_ref, s_ref, dp_ref, p_ref, ds_ref):
        n = pl.program_id(0)

        @pl.when(n == 0)
        def _():
            dsink_ref[...] = jnp.zeros_like(dsink_ref)

        mask = _swa_mask(n)
        lo = lax.broadcasted_iota(jnp.int32, (WINDOW, LANES), 1) < 64
        hi = jnp.logical_not(lo)
        lane8 = lax.broadcasted_iota(jnp.int32, (8, LANES), 1)
        for g in range(2):
            gs = slice(LANES * g, LANES * (g + 1))
            kb_ref[g] = jnp.concatenate([kp_ref[:, gs], kc_ref[:, gs]], axis=0)
            vb_ref[g] = jnp.concatenate([vp_ref[:, gs], vc_ref[:, gs]], axis=0)

        def head(h):
            sl = slice(LANES * (h // 2), LANES * (h // 2 + 1))
            hm = lo if h % 2 == 0 else hi
            qp = q_ref[:, sl]
            return hm, sl, jnp.where(hm, qp, jnp.zeros_like(qp)), jnp.where(hm, do_ref[:, sl], 0.0).astype(BF16)

        for h in range(SWA_HEADS):
            _, _, qh, dom = head(h)
            s_ref[h] = _dot_nt(qh, kb_ref[h // 8])
            dp_ref[h] = _dot_nt(dom, vb_ref[h // 8])
        dsink = jnp.zeros((8, LANES), F32)
        for h in range(SWA_HEADS):
            hm, sl, _, _ = head(h)
            lse_h = jnp.max(jnp.where(hm, l_ref[:, sl], -jnp.inf), axis=1, keepdims=True)
            delta = jnp.sum(jnp.where(hm, do_ref[:, sl] * o_ref[:, sl], 0.0), axis=1, keepdims=True)
            p = jnp.exp(jnp.where(mask, s_ref[h] * SWA_SCALE, NEG) - lse_h)
            p_ref[h] = p.astype(BF16)
            ds_ref[h] = (p * (dp_ref[h] - delta) * SWA_SCALE).astype(BF16)
            d_sink = -jnp.sum(jnp.exp(sink_ref[h] - lse_h) * delta, axis=0, keepdims=True)
            dsink = dsink + jnp.where(lane8 == h, d_sink, 0.0)
        dsink_ref[...] += dsink
        for g in range(2):
            gs = slice(LANES * g, LANES * (g + 1))
            dkb = jnp.zeros((2 * WINDOW, LANES), F32)
            dvb = jnp.zeros((2 * WINDOW, LANES), F32)
            for j in range(4 * g, 4 * g + 4):
                dqs = []
                for h in (2 * j, 2 * j + 1):
                    _, _, qh, dom = head(h)
                    dvb = dvb + _dot_tn(p_ref[h], dom)
                    dkb = dkb + _dot_tn(ds_ref[h], qh)
                    dqs.append(_dot(ds_ref[h], kb_ref[g]))
                dq_ref[:, LANES * j:LANES * (j + 1)] = jnp.where(lo, dqs[0], dqs[1])
            dkp_ref[:, gs] = dkb[:WINDOW]
            dkc_ref[:, gs] = dkb[WINDOW:]
            dvp_ref[:, gs] = dvb[:WINDOW]
            dvc_ref[:, gs] = dvb[WINDOW:]

    band = pltpu.VMEM((2, 2 * WINDOW, LANES), BF16)
    return _pcall(body, name="swa_bwd", grid=(nb,),
                  in_specs=[pl.BlockSpec(memory_space=pltpu.SMEM), cur(D), cur(256), prev(256), cur(256), prev(256),
                            cur(D), cur(D), cur(D)],
                  out_specs=[cur(D), cur(256), cur(256), cur(256), cur(256), _full((8, LANES))],
                  out_shape=[SDS((T, D), F32), SDS((T, 256), F32), SDS((T, 256), F32), SDS((T, 256), F32), SDS((T, 256), F32),
                             SDS((8, LANES), F32)],
                  scratch=[band, band, pltpu.VMEM((SWA_HEADS, WINDOW, 2 * WINDOW), F32),
                           pltpu.VMEM((SWA_HEADS, WINDOW, 2 * WINDOW), F32), pltpu.VMEM((SWA_HEADS, WINDOW, 2 * WINDOW), BF16),
                           pltpu.VMEM((SWA_HEADS, WINDOW, 2 * WINDOW), BF16)],
                  sem=("arbitrary",))(sinks, qs, ks, ks, vs, vs, do, o, lse)


def _bwd_qkv(dqm, dkm, dvm, dqs, dkc, dkp, dvc, dvp, z, gq, gkv, wqb, wkn, wv, tab_m, tab_s):
    T = z.shape[0]
    tm = WINDOW
    nb = T // tm
    per = dqm.shape[2] // tm

    tab_mt = [t.T for t in tab_m]
    half = MLA_ROPE // 2

    def rope_t(v, c, a, b):
        return v * c + pltpu.roll(v, LANES - half, 0) * a + pltpu.roll(v, half, 0) * b

    def rms_bwd_t(dy, x, g):
        r = lax.rsqrt(jnp.mean(x * x, axis=0, keepdims=True) + EPS)
        xn = x * r
        dn = dy * g
        return r * (dn - xn * jnp.mean(dn * xn, axis=0, keepdims=True)), jnp.sum(dy * xn, axis=1, keepdims=True)

    def body(dqm_ref, dkm_ref, dvm_ref, dqs_ref, dkc_ref, dkp_ref, dvc_ref, dvp_ref, qa_ref, kva_ref, gq_ref, gkv_ref,
             wqb_ref, wkn_ref, wv_ref, cmt_ref, amt_ref, bmt_ref, cs_ref, as_ref, bs_ref,
             dq_out, dkn_out, dv_out, dsq_ref, drest_ref, dgq_ref, dgkv_ref):
        i = pl.program_id(0)

        @pl.when(i == 0)
        def _():
            dgq_ref[...] = jnp.zeros_like(dgq_ref)
            dgkv_ref[...] = jnp.zeros_like(dgkv_ref)

        cmt, amt, bmt = cmt_ref[...], -amt_ref[...], -bmt_ref[...]
        cs, as_, bs = cs_ref[...], -as_ref[...], -bs_ref[...]
        row = lax.broadcasted_iota(jnp.int32, (LANES, tm), 0)
        nope = row < MLA_NOPE
        roped = jnp.logical_and(row >= MLA_NOPE, row < MLA_NOPE + MLA_ROPE)
        dkr = jnp.zeros((LANES, tm), F32)
        for h in range(MLA_HEADS):
            sl = slice(LANES * h, LANES * (h + 1))
            dq_out[0, sl, :] = rope_t(dqm_ref[0, sl, :], cmt, amt, bmt).astype(BF16)
            dk_h = dkm_ref[0, sl, :]
            dkn_out[0, sl, :] = jnp.where(nope, dk_h, 0.0).astype(BF16)
            dkr = dkr + jnp.where(roped, dk_h, 0.0)
        dv_out[0] = dvm_ref[0].astype(BF16)
        dqn = _dot(wqb_ref[...], dq_out[0])
        dkvn = _dot(wkn_ref[...], dkn_out[0]) + _dot(wv_ref[...], dv_out[0])
        dqa, dgq = rms_bwd_t(dqn, qa_ref[...].T, gq_ref[...])
        dkva, dgkv = rms_bwd_t(dkvn, kva_ref[...].T, gkv_ref[...])
        dgq_ref[...] += dgq
        dgkv_ref[...] += dgkv
        for j in range(D // LANES):
            sl = slice(LANES * j, LANES * (j + 1))
            dsq_ref[:, sl] = _rope(dqs_ref[:, sl], cs, as_, bs, SWA_HD // 2).astype(BF16)
        keep = (i < nb - 1).astype(F32)
        drest_ref[:, 0:256] = dqa.T.astype(BF16)
        for j in range(2):
            sl = slice(LANES * j, LANES * (j + 1))
            dk = dkc_ref[:, sl] + keep * dkp_ref[:, sl]
            drest_ref[:, 256 + LANES * j:256 + LANES * (j + 1)] = _rope(dk, cs, as_, bs, SWA_HD // 2).astype(BF16)
        drest_ref[:, 512:768] = (dvc_ref[...] + keep * dvp_ref[...]).astype(BF16)
        drest_ref[:, 768:896] = dkva.T.astype(BF16)
        drest_ref[:, 896:1024] = rope_t(dkr, cmt, amt, bmt).T.astype(BF16)

    nxt = pl.BlockSpec((tm, 256), lambda i: (jnp.minimum(i + 1, nb - 1), 0))
    tab = [_rows(tm, LANES)] * 3
    tab_t = [pl.BlockSpec((LANES, tm), lambda i: (0, i))] * 3
    blk_t = lambda w: pl.BlockSpec((1, w, tm), lambda i: (i // per, 0, i % per))
    return _pcall(body, name="bwd_qkv", grid=(nb,),
                  in_specs=[blk_t(2048), blk_t(2048), blk_t(1024), _rows(tm, 1024), _rows(tm, 256), nxt,
                            _rows(tm, 256), nxt, _rows(tm, 256, 12), _rows(tm, 128, 30), _full((Q_LORA, 1)), _full((KV_LORA, 1)),
                            _full((Q_LORA, 2048)), _full((KV_LORA, 2048)), _full((KV_LORA, 1024))] + tab_t + tab,
                  out_specs=[blk_t(2048), blk_t(2048), blk_t(1024), _rows(tm, 1024), _rows(tm, 1024),
                             _full((Q_LORA, 1)), _full((KV_LORA, 1))],
                  out_shape=[SDS(dqm.shape, BF16), SDS(dkm.shape, BF16), SDS(dvm.shape, BF16), SDS((T, 1024), BF16),
                             SDS((T, 1024), BF16), SDS((Q_LORA, 1), F32), SDS((KV_LORA, 1), F32)],
                  sem=("arbitrary",))(dqm, dkm, dvm, dqs, dkc, dkp, dvc, dvp, z, z, gq.reshape(Q_LORA, 1),
                                      gkv.reshape(KV_LORA, 1), wqb, wkn, wv, *tab_mt, *tab_s)


def _bwd_in(dsq, dga, dgb, drest, w_in_p, x, g1, dx1, tm):
    T = x.shape[0]

    def body(a_ref, b_ref, c_ref, d_ref, w_ref, x_ref, g_ref, dx1_ref, dx_ref, dg_ref):
        @pl.when(pl.program_id(0) == 0)
        def _():
            dg_ref[...] = jnp.zeros_like(dg_ref)

        dh = (_dot_nt(a_ref[...], w_ref[:, 0:1024]) + _dot_nt(b_ref[...], w_ref[:, 1024:2048])
              + _dot_nt(c_ref[...], w_ref[:, 2048:3072]) + _dot_nt(d_ref[...], w_ref[:, 3072:4096]))
        dx, dg = _rms_bwd(dh, x_ref[...], g_ref[...])
        dg_ref[...] += dg
        dx_ref[...] = dx1_ref[...] + dx

    r = _rows(tm, D)
    return _pcall(body, name="bwd_in", grid=(T // tm,),
                  in_specs=[r, r, r, r, _full((D, NZ)), r, _full((1, D)), r],
                  out_specs=[r, _full((1, D))], out_shape=[SDS((T, D), F32), SDS((1, D), F32)],
                  sem=("arbitrary",))(dsq, dga, dgb, drest, w_in_p, x, g1, dx1)


def _wgrad(a, g, name, into=None):
    T, K = a.shape
    N = g.shape[1]
    tk, tn, tt = min(K, 1024), min(N, 1024), min(T, 1024)
    if into is not None:
        buf, weight = into
        _, row0, lane0 = PACK_AT[weight]
        shard = {n: (r, c) for n, r, c in BIG}[weight]
        assert lane0 == 0 and shard[1] == D and tk % shard[0] == 0
        per_step = tk // shard[0]
    assert K % tk == 0 and N % tn == 0 and T % tt == 0, (a.shape, g.shape)
    steps = T // tt

    def body(a_ref, g_ref, *rest):
        o_ref, acc_ref = rest[-2:]
        t = pl.program_id(2)

        @pl.when(t == 0)
        def _():
            acc_ref[...] = jnp.zeros_like(acc_ref)

        acc_ref[...] += _dot_tn(a_ref[...].astype(BF16), g_ref[...].astype(BF16))

        @pl.when(t == steps - 1)
        def _():
            o_ref[...] = acc_ref[...].astype(o_ref.dtype).reshape(o_ref.shape)

    in_specs = [pl.BlockSpec((tt, tk), lambda k, n, t: (t, k)), pl.BlockSpec((tt, tn), lambda k, n, t: (t, n))]
    if into is None:
        return _pcall(body, name=name, grid=(K // tk, N // tn, steps), in_specs=in_specs,
                      out_specs=pl.BlockSpec((tk, tn), lambda k, n, t: (k, n)), out_shape=SDS((K, N), F32),
                      scratch=[pltpu.VMEM((tk, tn), F32)], sem=("parallel", "parallel", "arbitrary"))(a, g)
    assert row0 % shard[0] == 0 and (K // tk) * (N // tn) * per_step == N_CHIPS
    return _pcall(body, name=name, grid=(K // tk, N // tn, steps), in_specs=in_specs + [ANY],
                  out_specs=pl.BlockSpec((per_step, shard[0], tn), lambda k, n, t: (k + n, row0 // shard[0], 0)),
                  out_shape=SDS(buf.shape, buf.dtype),
                  scratch=[pltpu.VMEM((tk, tn), F32)], sem=("parallel", "parallel", "arbitrary"), aliases={2: 0})(a, g, buf)


def _wgrad_t(at, g, name):
    nblk, K, tt = at.shape
    N = g.shape[1]
    tk = min(K, 1024)
    per_step = 4 if nblk % 4 == 0 else 1
    assert K % tk == 0 and g.shape[0] == nblk * tt

    def body(a_ref, g_ref, o_ref):
        @pl.when(pl.program_id(1) == 0)
        def _():
            o_ref[...] = jnp.zeros_like(o_ref)

        acc = _dot(a_ref[0], g_ref[0:tt, :].astype(BF16))
        for b in range(1, per_step):
            acc = acc + _dot(a_ref[b], g_ref[tt * b:tt * (b + 1), :].astype(BF16))
        o_ref[...] += acc

    return _pcall(body, name=name, grid=(K // tk, nblk // per_step),
                  in_specs=[pl.BlockSpec((per_step, tk, tt), lambda k, t: (t, k, 0)),
                            pl.BlockSpec((per_step * tt, N), lambda k, t: (t, 0))],
                  out_specs=pl.BlockSpec((tk, N), lambda k, t: (k, 0)), out_shape=SDS((K, N), F32),
                  sem=("parallel", "arbitrary"))(at, g)


def _adamw(w, packed_g, m, v, name):
    _, R, C = w.shape
    _, row0, lane0 = PACK_AT[name]
    tr = min(R, 256 if row0 % 256 == 0 else 128)
    assert row0 % tr == 0 and R % tr == 0

    def body(w_ref, g_ref, m_ref, v_ref, go_ref, d_ref, m2_ref, v2_ref):
        g_ = g_ref[:, lane0:lane0 + C]
        go_ref[0] = g_
        m2 = ADAM_B1 * m_ref[0] + (1.0 - ADAM_B1) * g_
        v2 = ADAM_B2 * v_ref[0] + (1.0 - ADAM_B2) * jnp.square(g_)
        m_hat = m2 / (1.0 - ADAM_B1 ** ADAM_STEP)
        v_hat = v2 / (1.0 - ADAM_B2 ** ADAM_STEP)
        d_ref[0] = -ADAM_LR * (m_hat / (jnp.sqrt(v_hat) + ADAM_EPS) + ADAM_WD * w_ref[0])
        m2_ref[0] = m2
        v2_ref[0] = v2

    r = pl.BlockSpec((1, tr, C), lambda i: (0, i, 0))
    return _pcall(body, name="adamw_" + name, grid=(R // tr,),
                  in_specs=[r, pl.BlockSpec((tr, D), lambda i: (row0 // tr + i, 0)), r, r], out_specs=[r] * 4,
                  out_shape=[SDS((1, R, C), F32)] * 4, sem=("parallel",))(w, packed_g, m, v)


def _adamw_small(w, parts, m, v):
    def body(w_ref, p_ref, m_ref, v_ref, g_ref, d_ref, m2_ref, v2_ref):
        g_ = p_ref[0]
        for k in range(1, N_DEV):
            g_ = g_ + p_ref[k]
        g_ref[...] = g_
        m2 = ADAM_B1 * m_ref[...] + (1.0 - ADAM_B1) * g_
        v2 = ADAM_B2 * v_ref[...] + (1.0 - ADAM_B2) * jnp.square(g_)
        m_hat = m2 / (1.0 - ADAM_B1 ** ADAM_STEP)
        v_hat = v2 / (1.0 - ADAM_B2 ** ADAM_STEP)
        d_ref[...] = -ADAM_LR * (m_hat / (jnp.sqrt(v_hat) + ADAM_EPS) + ADAM_WD * w_ref[...])
        m2_ref[...] = m2
        v2_ref[...] = v2

    s = _full((8, D))
    return _pcall(body, name="adamw_small", grid=(1,), in_specs=[s, _full((N_DEV, 8, D)), s, s], out_specs=[s] * 4,
                  out_shape=[SDS((8, D), F32)] * 4, sem=("arbitrary",))(w, parts, m, v)


ANY = pl.BlockSpec(memory_space=pl.ANY)


def _place():
    x, y, c = lax.axis_index("x"), lax.axis_index("y"), lax.axis_index("c")
    chips = [(1 - x, y), (x, 1 - y), (1 - x, 1 - y)]
    return x, y, c, chips


def _all_gather(wpk):
    rows = wpk.shape[0]
    HALF = rows // 2
    assert HALF % 16 == 0

    def body(in_ref, out_ref, send_sems, recv_sems):
        x, y, c, chips = _place()
        half = pl.ds(pl.multiple_of(c * HALF, 16), HALF)
        other = pl.ds(pl.multiple_of((1 - c) * HALF, 16), HALF)

        def copy(k, src, dst, to):
            return pltpu.make_async_remote_copy(src_ref=src, dst_ref=dst, send_sem=send_sems.at[k], recv_sem=recv_sems.at[k],
                                                device_id=to, device_id_type=MESH)

        first = [copy(k, in_ref.at[half], out_ref.at[2 * x + y, half], (cx, cy, c)) for k, (cx, cy) in enumerate(chips)]
        for cp in first:
            cp.start()
        passed = []
        for k, (cx, cy) in enumerate(chips):
            slot = out_ref.at[2 * cx + cy, half]
            copy(k, slot, slot, (x, y, c)).wait_recv()
            fwd = copy(3 + k, slot, slot, (x, y, 1 - c))
            fwd.start()
            passed.append(fwd)
        for k, (cx, cy) in enumerate(chips):
            slot = out_ref.at[2 * cx + cy, other]
            copy(3 + k, slot, slot, (x, y, c)).wait_recv()
        for cp in first + passed:
            cp.wait_send()

    return _pcall(body, name="all_gather_weights", in_specs=[ANY], out_specs=ANY,
                  out_shape=SDS((N_CHIPS, rows, D), BF16),
                  scratch=[pltpu.SemaphoreType.DMA((6,)), pltpu.SemaphoreType.DMA((6,))])(wpk)


HBM = pl.BlockSpec(memory_space=pltpu.HBM)
SEM = pl.BlockSpec(memory_space=pltpu.SEMAPHORE)
DATAFLOW = pltpu.SideEffectType.DATAFLOW_SIDE_EFFECTING


def _in_hbm(a):
    return pltpu.with_memory_space_constraint(a, pltpu.HBM)


def _gather_late_start(wpk, after):
    rows = wpk.shape[0]

    def body(in_ref, land_ref, after_ref, send_sems, recv_sems, in_thru, land_thru, token):
        x, y, c, chips = _place()
        for k, (cx, cy) in enumerate(chips):
            pltpu.make_async_remote_copy(src_ref=in_ref, dst_ref=land_ref.at[2 * x + y], send_sem=send_sems.at[k],
                                         recv_sem=recv_sems.at[k], device_id=(cx, cy, c), device_id_type=MESH).start()
        token[...] = jnp.zeros_like(token)

    return pl.pallas_call(
        body, name="gather_late_start",
        out_shape=(pltpu.SemaphoreType.DMA((3,)), pltpu.SemaphoreType.DMA((3,)), pltpu.HBM(wpk.shape, wpk.dtype),
                   pltpu.HBM((N_CHIPS, rows, D), wpk.dtype), SDS((8, LANES), F32)),
        in_specs=(HBM, HBM, ANY), out_specs=(SEM, SEM, HBM, HBM, pl.BlockSpec(memory_space=pltpu.VMEM)),
        input_output_aliases={0: 2, 1: 3}, compiler_params=pltpu.CompilerParams(has_side_effects=DATAFLOW),
    )(_in_hbm(wpk), _in_hbm(lax.empty((N_CHIPS, rows, D), wpk.dtype)), after)


def _gather_late_wait(send_sems, recv_sems, in_thru, land_thru, after):
    def body(in_ref, land_ref, send_sems, recv_sems, after_ref, after2_ref, in_dead, got_ref):
        x, y, c, chips = _place()
        for k, (cx, cy) in enumerate(chips):
            cp = pltpu.make_async_remote_copy(src_ref=in_ref, dst_ref=land_ref.at[2 * cx + cy], send_sem=send_sems.at[k],
                                              recv_sem=recv_sems.at[k], device_id=(cx, cy, c), device_id_type=MESH)
            cp.wait_send()
            cp.wait_recv()

    return pl.pallas_call(
        body, name="gather_late_wait",
        out_shape=(pltpu.HBM(in_thru.shape, in_thru.dtype), pltpu.HBM(land_thru.shape, land_thru.dtype)),
        in_specs=(HBM, HBM, SEM, SEM, ANY, ANY), out_specs=(HBM, HBM), input_output_aliases={0: 0, 1: 1},
        compiler_params=pltpu.CompilerParams(has_side_effects=DATAFLOW),
    )(in_thru, land_thru, send_sems, recv_sems, *after)[1]


def _rs_sibling(gpk):
    HALF = gpk.shape[1] // 2

    def body(in_ref, out_ref, send_sem, recv_sem):
        x, y, c, _ = _place()
        theirs = pl.ds(pl.multiple_of((1 - c) * HALF, 8), HALF)
        cp = pltpu.make_async_remote_copy(src_ref=in_ref.at[:, theirs], dst_ref=out_ref, send_sem=send_sem, recv_sem=recv_sem,
                                          device_id=(x, y, 1 - c), device_id_type=MESH)
        cp.start()
        cp.wait()

    return _pcall(body, name="rs_sibling", in_specs=[ANY], out_specs=ANY, out_shape=SDS((N_CHIPS, HALF, D), F32),
                  scratch=[pltpu.SemaphoreType.DMA, pltpu.SemaphoreType.DMA])(gpk)


def _rs_add_sibling(cidx, gpk, got):
    HALF = got.shape[1]
    th = HALF // 4
    nh = HALF // th
    assert th % 16 == 0

    def body(c_ref, a_ref, b_ref, o_ref):
        o_ref[...] = (a_ref[...] + b_ref[...]).astype(BF16)

    gs = pltpu.PrefetchScalarGridSpec(
        num_scalar_prefetch=1, grid=(N_CHIPS, nh),
        in_specs=[pl.BlockSpec((1, th, D), lambda j, i, c: (j, c[0] * nh + i, 0)), pl.BlockSpec((1, th, D), lambda j, i, c: (j, i, 0))],
        out_specs=pl.BlockSpec((1, th, D), lambda j, i, c: (j, i, 0)))
    return pl.pallas_call(body, name="rs_add_sibling", grid_spec=gs, out_shape=SDS((N_CHIPS, HALF, D), BF16),
                          compiler_params=pltpu.CompilerParams(dimension_semantics=("parallel", "parallel"),
                                                               vmem_limit_bytes=48 << 20))(cidx, gpk, got)


def _rs_chips_start(part, small, after):
    def body(p_ref, s_ref, land_ref, sland_ref, after_ref, send_sems, recv_sems, p_thru, s_thru, land_thru, sland_thru, token):
        x, y, c, chips = _place()
        for k, (cx, cy) in enumerate(chips):
            pltpu.make_async_remote_copy(src_ref=p_ref.at[2 * cx + cy], dst_ref=land_ref.at[2 * x + y], send_sem=send_sems.at[k],
                                         recv_sem=recv_sems.at[k], device_id=(cx, cy, c), device_id_type=MESH).start()
        peers = [(x, y, 1 - c)] + [(cx, cy, c) for cx, cy in chips] + [(cx, cy, 1 - c) for cx, cy in chips]
        for k, to in enumerate(peers):
            pltpu.make_async_remote_copy(src_ref=s_ref, dst_ref=sland_ref.at[4 * x + 2 * y + c], send_sem=send_sems.at[3 + k],
                                         recv_sem=recv_sems.at[3 + k], device_id=to, device_id_type=MESH).start()
        token[...] = jnp.zeros_like(token)

    return pl.pallas_call(
        body, name="rs_chips_start",
        out_shape=(pltpu.SemaphoreType.DMA((10,)), pltpu.SemaphoreType.DMA((10,)), pltpu.HBM(part.shape, part.dtype),
                   pltpu.HBM(small.shape, small.dtype), pltpu.HBM(part.shape, part.dtype), pltpu.HBM((N_DEV, 8, D), F32),
                   SDS((8, LANES), F32)),
        in_specs=(HBM, HBM, HBM, HBM, ANY), out_specs=(SEM, SEM, HBM, HBM, HBM, HBM, pl.BlockSpec(memory_space=pltpu.VMEM)),
        input_output_aliases={0: 2, 1: 3, 2: 4, 3: 5}, compiler_params=pltpu.CompilerParams(has_side_effects=DATAFLOW),
    )(_in_hbm(part), _in_hbm(small), _in_hbm(lax.empty(part.shape, part.dtype)), _in_hbm(lax.empty((N_DEV, 8, D), F32)), after)


def _rs_chips_wait(send_sems, recv_sems, p_thru, s_thru, land_thru, sland_thru, after):
    def body(p_ref, s_ref, land_ref, sland_ref, send_sems, recv_sems, *after_and_outputs):
        x, y, c, chips = _place()
        for k, (cx, cy) in enumerate(chips):
            cp = pltpu.make_async_remote_copy(src_ref=p_ref.at[0], dst_ref=land_ref.at[2 * cx + cy], send_sem=send_sems.at[k],
                                              recv_sem=recv_sems.at[k], device_id=(cx, cy, c), device_id_type=MESH)
            cp.wait_send()
            cp.wait_recv()
        peers = [(x, y, 1 - c)] + [(cx, cy, c) for cx, cy in chips] + [(cx, cy, 1 - c) for cx, cy in chips]
        for k, (px, py, pc) in enumerate(peers):
            cp = pltpu.make_async_remote_copy(src_ref=s_ref, dst_ref=sland_ref.at[4 * px + 2 * py + pc], send_sem=send_sems.at[3 + k],
                                              recv_sem=recv_sems.at[3 + k], device_id=(px, py, pc), device_id_type=MESH)
            cp.wait_send()
            cp.wait_recv()

    hbm = lambda a: pltpu.HBM(a.shape, a.dtype)
    outs = pl.pallas_call(
        body, name="rs_chips_wait", out_shape=(hbm(p_thru), hbm(s_thru), hbm(land_thru), hbm(sland_thru)),
        in_specs=(HBM, HBM, HBM, HBM, SEM, SEM) + (ANY,) * len(after), out_specs=(HBM, HBM, HBM, HBM),
        input_output_aliases={0: 0, 1: 1, 2: 2, 3: 3}, compiler_params=pltpu.CompilerParams(has_side_effects=DATAFLOW),
    )(p_thru, s_thru, land_thru, sland_thru, send_sems, recv_sems, *after)
    return outs[0], outs[2], outs[3]


def _rs_add_chips(qidx, part, parts):
    HALF = part.shape[1]
    th = HALF // 4
    assert th % 16 == 0

    def body(q_ref, own_ref, p_ref, o_ref):
        for me in range(N_CHIPS):
            @pl.when(q_ref[0] == me)
            def _(me=me):
                t = [(own_ref[0] if j == me else p_ref[j]).astype(F32) for j in range(N_CHIPS)]
                o_ref[...] = ((t[0] + t[1]) + t[2]) + t[3]

    gs = pltpu.PrefetchScalarGridSpec(
        num_scalar_prefetch=1, grid=(HALF // th,),
        in_specs=[pl.BlockSpec((1, th, D), lambda i, q: (q[0], i, 0)), pl.BlockSpec((N_CHIPS, th, D), lambda i, q: (0, i, 0))],
        out_specs=pl.BlockSpec((th, D), lambda i, q: (i, 0)))
    return pl.pallas_call(body, name="rs_add_chips", grid_spec=gs, out_shape=SDS((HALF, D), F32),
                          compiler_params=pltpu.CompilerParams(dimension_semantics=("parallel",),
                                                               vmem_limit_bytes=48 << 20))(qidx, part, parts)


def _rs_join(mine, core, name):
    def body(in_ref, out_ref, send_sem, recv_sem):
        x, y, c, _ = _place()
        cp = pltpu.make_async_remote_copy(src_ref=in_ref, dst_ref=out_ref, send_sem=send_sem, recv_sem=recv_sem,
                                          device_id=(x, y, 1 - c), device_id_type=MESH)
        cp.start()
        cp.wait()

    theirs = _pcall(body, name=name, in_specs=[ANY], out_specs=ANY, out_shape=SDS(mine.shape, F32),
                    scratch=[pltpu.SemaphoreType.DMA, pltpu.SemaphoreType.DMA])(mine)
    return jnp.where(core == 0, jnp.concatenate([mine, theirs]), jnp.concatenate([theirs, mine]))


def _reduce_late_start(gpk, after):
    rows = gpk.shape[1]
    HALF = rows // 2
    assert HALF % 16 == 0

    def body(in_ref, land_ref, after_ref, send_sems, recv_sems, in_thru, land_thru, token):
        x, y, c, chips = _place()
        me = 4 * x + 2 * y + c
        peers = [(x, y, 1 - c)] + [(cx, cy, c) for cx, cy in chips] + [(cx, cy, 1 - c) for cx, cy in chips]
        for k, (px, py, pc) in enumerate(peers):
            src = in_ref.at[2 * px + py, pl.ds(pl.multiple_of(pc * HALF, 16), HALF)]
            pltpu.make_async_remote_copy(src_ref=src, dst_ref=land_ref.at[me], send_sem=send_sems.at[k], recv_sem=recv_sems.at[k],
                                         device_id=(px, py, pc), device_id_type=MESH).start()
        token[...] = jnp.zeros_like(token)

    return pl.pallas_call(
        body, name="reduce_late_start",
        out_shape=(pltpu.SemaphoreType.DMA((7,)), pltpu.SemaphoreType.DMA((7,)), pltpu.HBM(gpk.shape, gpk.dtype),
                   pltpu.HBM((N_DEV, HALF, D), gpk.dtype), SDS((8, LANES), F32)),
        in_specs=(HBM, HBM, ANY), out_specs=(SEM, SEM, HBM, HBM, pl.BlockSpec(memory_space=pltpu.VMEM)),
        input_output_aliases={0: 2, 1: 3}, compiler_params=pltpu.CompilerParams(has_side_effects=DATAFLOW),
    )(_in_hbm(gpk), _in_hbm(lax.empty((N_DEV, HALF, D), gpk.dtype)), after)


def _reduce_late_wait(send_sems, recv_sems, in_thru, land_thru, after):
    def body(in_ref, land_ref, send_sems, recv_sems, after_ref, in_out, got_ref):
        x, y, c, chips = _place()
        peers = [(x, y, 1 - c)] + [(cx, cy, c) for cx, cy in chips] + [(cx, cy, 1 - c) for cx, cy in chips]
        for k, (px, py, pc) in enumerate(peers):
            cp = pltpu.make_async_remote_copy(src_ref=land_ref.at[0], dst_ref=land_ref.at[4 * px + 2 * py + pc],
                                              send_sem=send_sems.at[k], recv_sem=recv_sems.at[k],
                                              device_id=(px, py, pc), device_id_type=MESH)
            cp.wait_send()
            cp.wait_recv()

    return pl.pallas_call(
        body, name="reduce_late_wait",
        out_shape=(pltpu.HBM(in_thru.shape, in_thru.dtype), pltpu.HBM(land_thru.shape, land_thru.dtype)),
        in_specs=(HBM, HBM, SEM, SEM, ANY), out_specs=(HBM, HBM), input_output_aliases={0: 0, 1: 1},
        compiler_params=pltpu.CompilerParams(has_side_effects=DATAFLOW),
    )(in_thru, land_thru, send_sems, recv_sems, after)


def _reduce_late_add(didx, gpk, parts):
    HALF = parts.shape[1]
    th = HALF // 4
    nh = HALF // th
    assert th % 16 == 0

    def body(d_ref, own_ref, p_ref, o_ref):
        for me in range(N_DEV):
            @pl.when(d_ref[0] == me)
            def _(me=me):
                t = [(own_ref[0] if j == me else p_ref[j]).astype(F32) for j in range(N_DEV)]
                o_ref[...] = ((((((t[0] + t[1]) + t[2]) + t[3]) + t[4]) + t[5]) + t[6]) + t[7]

    gs = pltpu.PrefetchScalarGridSpec(
        num_scalar_prefetch=1, grid=(nh,),
        in_specs=[pl.BlockSpec((1, th, D), lambda i, d: (d[1], d[2] * nh + i, 0)), pl.BlockSpec((N_DEV, th, D), lambda i, d: (0, i, 0))],
        out_specs=pl.BlockSpec((th, D), lambda i, d: (i, 0)))
    return pl.pallas_call(body, name="reduce_late_add", grid_spec=gs, out_shape=SDS((HALF, D), F32),
                          compiler_params=pltpu.CompilerParams(dimension_semantics=("parallel",),
                                                               vmem_limit_bytes=48 << 20))(didx, gpk, parts)


def _pack_early(b, dtype):
    lanes = lambda a: jnp.pad(a.astype(dtype), ((0, 0), (0, D - a.shape[1])))
    pair = jnp.concatenate([b["w_q_b"].astype(dtype), b["w_ple"].astype(dtype), jnp.zeros((256, D - 640), dtype)], axis=1)
    return jnp.concatenate([lanes(b["w_in"]), pair, lanes(b["w_kv_b"])], axis=0)


def _pack_late(b, dtype):
    return jnp.concatenate([b[n].astype(dtype) for n in ("w_mla_up", "w_swa_up", "w_out", "w_ple_gate", "w_mlp_up", "w_mlp_down")],
                           axis=0)


def _unpack_shards(pk, which):
    return {n: pk[PACK_AT[n][1]:PACK_AT[n][1] + r, PACK_AT[n][2]:PACK_AT[n][2] + c] for n, r, c in BIG if PACK_AT[n][0] == which}


def _full_weights(gathered, own, chip, which):
    own_b = _unpack_shards(own, which)
    per_chip = [{n: jnp.where(chip == j, own_b[n], blk) for n, blk in _unpack_shards(gathered[j], which).items()}
                for j in range(N_CHIPS)]
    out = {}
    for n in own_b:
        shards = [pc[n] for pc in per_chip]
        if n == "w_in":
            out["w_in_p"] = _w_in_internal(shards)
        else:
            out[n] = jnp.concatenate(shards, axis=1 if n in COL_SHARDED else 0)
    return out


def _split_full_grads(grads, pack, dtype):
    shard = {n: (r, c) for n, r, c in BIG}
    chunks = []
    for j in range(N_CHIPS):
        blocks = {}
        for n, g in grads.items():
            if n == "w_in_p":
                blocks["w_in"] = _w_in_grad_shard(g, j)
                continue
            r, c = shard[n]
            blocks[n] = g[:, j * c:(j + 1) * c] if n in COL_SHARDED else g[j * r:(j + 1) * r]
        chunks.append(pack(blocks, dtype))
    return jnp.stack(chunks)


W_IN_SHARD = 936
W_IN_SEGMENTS = ((0, 256, (3072,)), (256, 384, (3840,)), (384, 416, (4032,)), (416, 1440, (0,)), (1440, 1504, (3328, 3392)),
                 (1504, 1568, (3456, 3520)), (1568, 1632, (3584, 3648)), (1632, 1696, (3712, 3776)), (1696, 3744, (1024,)))


def _w_in_internal(shards):
    def cols(a, b):
        out = []
        for j, s in enumerate(shards):
            lo, hi = max(a, W_IN_SHARD * j), min(b, W_IN_SHARD * (j + 1))
            if lo < hi:
                out.append(s[:, lo - W_IN_SHARD * j:hi - W_IN_SHARD * j])
        return out

    pieces = {}
    for a, b, places in W_IN_SEGMENTS:
        for at in places:
            pieces[at] = cols(a, b)
    zeros = lambda n: [jnp.zeros((D, n), shards[0].dtype)]
    pieces[3968] = zeros(64)
    pieces[4064] = zeros(32)
    return jnp.concatenate([piece for at in sorted(pieces) for piece in pieces[at]], axis=1)


def _w_in_grad_shard(g, j):
    def internal(a, b):
        out = []
        while a < b:
            end = min(b, (a // D + 1) * D)
            out.append(g[a // D][:, a % D:a % D + end - a])
            a = end
        return out

    out = []
    for a, b, places in W_IN_SEGMENTS:
        lo, hi = max(a, W_IN_SHARD * j), min(b, W_IN_SHARD * (j + 1))
        if lo < hi:
            parts = [internal(at + lo - a, at + hi - a) for at in places]
            if len(parts) == 1:
                out += parts[0]
            else:
                assert len(parts[0]) == len(parts[1]) == 1
                out.append(parts[0][0] + parts[1][0])
    return jnp.concatenate(out, axis=1)


def _local_step(x, p, tgt, w, small, late_weights, late_grads_out):
    T = x.shape[0]
    tm = 256
    tb = 256
    w_in_p = w["w_in_p"]
    wqb = jnp.pad(w["w_q_b"].reshape(Q_LORA, MLA_HEADS, 96), ((0, 0), (0, 0), (0, 32))).reshape(Q_LORA, 2048)
    wkv = w["w_kv_b"].reshape(KV_LORA, MLA_HEADS, 128)
    wkn = jnp.pad(wkv[:, :, :64], ((0, 0), (0, 0), (0, 64))).reshape(KV_LORA, 2048)
    wv = wkv[:, :, 64:].reshape(KV_LORA, 1024)
    tab_m = _rope_tables(T, "mla")
    tab_s = _rope_tables(T, "swa")
    g1, gq, gkv, sinks = small["g_mix_pre"], small["g_q_a"], small["g_kv_a"], small["sinks"]
    g2, g3, g4, g5 = small["g_mix_post"], small["g_mlp_pre"], small["g_mlp_post"], small["g_ple"]
    sink_vec = sinks.reshape(SWA_HEADS)

    z, h1 = _fwd_in(x, g1, w_in_p, tm)
    qn, kvn, km, vm, qt, kt, vt, qs, ks, vs = _fwd_qkv(z, gq, gkv, wqb, wkn, wv, tab_m, tab_s, tb)
    om, lse_m = _mla_fwd(qt, km, vt, tb)
    os_, lse_s = _swa_fwd(sink_vec, qs, ks, vs)
    w = {**w, **late_weights((om, os_))}
    y, yo, au, bu, x1 = _fwd_mix(om, os_, z, x, w["w_mla_up"], w["w_swa_up"], w["w_out"], g2, tm)
    h2, u = _fwd_mlp_up(x1, g3, w["w_mlp_up"], tm)
    d, x2 = _fwd_mlp_down(u, w["w_mlp_down"], x1, g4, tm)
    loss, dx2, dgt, de0, dg5 = _ple_fwd_bwd(p, x2, tgt, w["w_ple"], g5, w["w_ple_gate"], tm)

    dd, da, dg4 = _bwd_mlp_down(dx2, d, g4, w["w_mlp_down"], u, tm)
    dx1, dg3 = _bwd_mlp_up(da, w["w_mlp_up"], x1, g3, dx2, tm)
    dyo, dg2, dau, dbu, dga, dgb, dos, delta_m, dom_t = _bwd_mix(dx1, yo, g2, w["w_out"], z, au, bu, w["w_mla_up"],
                                                                w["w_swa_up"], om, tb)
    gpk_late = lax.empty((N_CHIPS, PACK_ROWS["late"], D), BF16)
    for weight, a_, g_ in (("w_mla_up", om, dau), ("w_swa_up", os_, dbu), ("w_out", y, dyo), ("w_ple_gate", x2, dgt),
                           ("w_mlp_up", h2, da), ("w_mlp_down", u, dd)):
        gpk_late = _wgrad(a_, g_, "wgrad_" + weight[2:], into=(gpk_late, weight))
    token = late_grads_out(gpk_late)
    delta_m = delta_m + token[0, 0]
    dqm, dkm, dvm = _mla_bwd(qt, km, kt, vm, dom_t, lse_m, delta_m, tb)
    dqs, dkc, dkp, dvc, dvp, dsink = _swa_bwd(sink_vec, qs, ks, vs, dos, os_, lse_s)
    dqb, dknb, dvb, dsq, drest, dgq, dgkv = _bwd_qkv(dqm, dkm, dvm, dqs, dkc, dkp, dvc, dvp, z, gq, gkv, wqb, wkn, wv,
                                                      tab_m, tab_s)
    gx, dg1 = _bwd_in(dsq, dga, dgb, drest, w_in_p, x, g1, dx1, tm)

    g_in_p = [_wgrad(h1, dsq, "wgrad_in_sq"), _wgrad(h1, dga, "wgrad_in_ga"), _wgrad(h1, dgb, "wgrad_in_gb"),
              _wgrad(h1, drest, "wgrad_in_rest")]
    g_qb_p = _wgrad_t(dqb, qn, "wgrad_q_b").T
    g_kn_p = _wgrad_t(dknb, kvn, "wgrad_kv_b_nope").T
    g_v_p = _wgrad_t(dvb, kvn, "wgrad_kv_b_v").T
    grads = {
        "w_in_p": g_in_p,
        "w_q_b": g_qb_p.reshape(Q_LORA, MLA_HEADS, 128)[:, :, :96].reshape(Q_LORA, 1536),
        "w_kv_b": jnp.concatenate([g_kn_p.reshape(KV_LORA, MLA_HEADS, 128)[:, :, :64], g_v_p.reshape(KV_LORA, MLA_HEADS, 64)],
                                  axis=2).reshape(KV_LORA, 2048),
        "w_ple": _wgrad(p, de0, "wgrad_ple"),
    }
    small_grads = {"g_mix_pre": dg1, "g_q_a": dgq.reshape(1, Q_LORA), "g_kv_a": dgkv.reshape(1, KV_LORA), "sinks": dsink[0:1, 0:SWA_HEADS], "g_mix_post": dg2,
                   "g_mlp_pre": dg3, "g_mlp_post": dg4, "g_ple": dg5}
    return loss, gx, grads, small_grads


def _pack_small(vals, fill, scalar=None):
    wide = [vals[n] for n, k in SMALL if k == D]
    narrow = [vals[n] for n, k in SMALL if k != D]
    used = sum(k for _, k in SMALL if k != D)
    last = jnp.concatenate(narrow + [jnp.full((1, D - used), fill, F32)], axis=1)
    rest = jnp.full((2, D), fill, F32)
    if scalar is not None:
        rest = jnp.concatenate([jnp.concatenate([scalar, rest[0:1, 1:]], axis=1), rest[1:2]], axis=0)
    return jnp.concatenate(wide + [last, rest], axis=0)


def _unpack_small(pk):
    out, row, off = {}, 0, 0
    for n, k in SMALL:
        if k == D:
            out[n] = pk[row:row + 1]
            row += 1
    for n, k in SMALL:
        if k != D:
            out[n] = pk[5:6, off:off + k]
            off += k
    return out


def kernel(x, p, g_mix_pre, w_in, g_q_a, w_q_b, g_kv_a, w_kv_b, sinks, w_mla_up, w_swa_up, w_out, g_mix_post, g_mlp_pre, w_mlp_up, w_mlp_down, g_mlp_post, w_ple, g_ple, w_ple_gate, loss_target, m_g_mix_pre, m_w_in, m_g_q_a, m_w_q_b, m_g_kv_a, m_w_kv_b, m_sinks, m_w_mla_up, m_w_swa_up, m_w_out, m_g_mix_post, m_g_mlp_pre, m_w_mlp_up, m_w_mlp_down, m_g_mlp_post, m_w_ple, m_g_ple, m_w_ple_gate, v_g_mix_pre, v_w_in, v_g_q_a, v_w_q_b, v_g_kv_a, v_w_kv_b, v_sinks, v_w_mla_up, v_w_swa_up, v_w_out, v_g_mix_post, v_g_mlp_pre, v_w_mlp_up, v_w_mlp_down, v_g_mlp_post, v_w_ple, v_g_ple, v_w_ple_gate):
    given = dict(locals())
    big_w = {n: given[n][0] for n, _, _ in BIG}
    small_w = {n: given[n] for n, _ in SMALL}
    small_m = {n: given["m_" + n] for n, _ in SMALL}
    small_v = {n: given["v_" + n] for n, _ in SMALL}

    core = lax.axis_index("c")
    chip = 2 * lax.axis_index("x") + lax.axis_index("y")
    core_i = core.astype(jnp.int32).reshape(1)
    chip_i = chip.astype(jnp.int32).reshape(1)
    dev_i = jnp.stack([2 * chip + core, chip, core]).astype(jnp.int32)

    own_early = _pack_early(big_w, BF16)
    own_late = _pack_late(big_w, BF16)
    got_early = _all_gather(own_early)
    late_flight = _gather_late_start(own_late, got_early)
    weights = _full_weights(got_early, own_early, chip, "early")
    step_small = {**small_w, "g_mix_pre": small_w["g_mix_pre"] + late_flight[4][0, 0]}

    def late_weights(after):
        return _full_weights(_gather_late_wait(*late_flight[:4], after), own_late, chip, "late")

    flight = {}

    def late_grads_out(gpk_late):
        flight["late"] = _reduce_late_start(gpk_late, dev_i)
        return flight["late"][4]

    loss_blk, gx, grads, small_grads = _local_step(x[0], p[0, 0], loss_target[0], weights, step_small, late_weights,
                                                   late_grads_out)

    gpk = _split_full_grads(grads, _pack_early, F32)
    got = _rs_sibling(gpk)
    part = _rs_add_sibling(core_i, gpk, got)
    small_own = _pack_small(small_grads, 0.0, loss_blk[0:1, 0:1])
    early_flight = _rs_chips_start(part, small_own, dev_i)

    out_g, out_d, out_m, out_v = {}, {}, {}, {}
    gpk_late, parts_late = _reduce_late_wait(*flight["late"][:4], early_flight[6])
    joined_late = _rs_join(_reduce_late_add(dev_i, gpk_late, parts_late), core, "rs_join_late")
    for n, _, _ in BIG:
        if PACK_AT[n][0] == "late":
            out_g[n], out_d[n], out_m[n], out_v[n] = _adamw(given[n], joined_late, given["m_" + n], given["v_" + n], n)

    part, parts, small_parts = _rs_chips_wait(*early_flight[:6], [out_d[n] for n in out_d])
    joined_early = _rs_join(_rs_add_chips(chip_i, part, parts), core, "rs_join_early")
    for n, _, _ in BIG:
        if PACK_AT[n][0] == "early":
            out_g[n], out_d[n], out_m[n], out_v[n] = _adamw(given[n], joined_early, given["m_" + n], given["v_" + n], n)

    mine = (lax.broadcasted_iota(jnp.int32, (N_DEV, 1, 1), 0) == dev_i[0])
    g_small_pk, d_small_pk, m_small_pk, v_small_pk = _adamw_small(
        _pack_small(small_w, 0.0), jnp.where(mine, small_own[None], small_parts), _pack_small(small_m, 0.0),
        _pack_small(small_v, 1.0))
    loss = g_small_pk[6, 0]
    for out, pk in ((out_g, g_small_pk), (out_d, d_small_pk), (out_m, m_small_pk), (out_v, v_small_pk)):
        out.update(_unpack_small(pk))
    order = ["g_mix_pre", "w_in", "g_q_a", "w_q_b", "g_kv_a", "w_kv_b", "sinks", "w_mla_up", "w_swa_up", "w_out", "g_mix_post",
             "g_mlp_pre", "w_mlp_up", "w_mlp_down", "g_mlp_post", "w_ple", "g_ple", "w_ple_gate"]
    return (loss, gx[None], *[out_g[n] for n in order], *[out_d[n] for n in order], *[out_m[n] for n in order],
            *[out_v[n] for n in order])
```

```python
import math

import jax
import jax.numpy as jnp
from jax import lax
from jax.experimental import pallas as pl
from jax.experimental.pallas import tpu as pltpu

F32 = jnp.float32
BF16 = jnp.bfloat16
SDS = jax.ShapeDtypeStruct

D = 1024
D_FF = 4096
PLE = 256
Q_LORA = 256
KV_LORA = 128
MLA_HEADS = 16
MLA_NOPE = 64
MLA_ROPE = 32
SWA_HEADS = 16
SWA_HD = 64
WINDOW = 128
ROPE_THETA = 10000.0
EPS = 1e-6
NEG = -1e30
NZ = 4096
MLA_SCALE = (MLA_NOPE + MLA_ROPE) ** -0.5
LOG2_E = math.log2(math.e)
MLA_LOG2_SCALE = MLA_SCALE * LOG2_E
SWA_SCALE = SWA_HD ** -0.5

ADAM_LR = 0.001
ADAM_B1 = 0.9
ADAM_B2 = 0.999
ADAM_EPS = 1e-08
ADAM_WD = 0.01
ADAM_STEP = 10

LANES = 128
ATT_COLS = 128
VT_ROWS = 80
N_CHIPS = 4
N_DEV = 8
MESH = pl.DeviceIdType.MESH

NT = (((1,), (1,)), ((), ()))
TN = (((0,), (0,)), ((), ()))

BIG = (("w_in", 1024, 936), ("w_q_b", 256, 384), ("w_kv_b", 128, 512), ("w_mla_up", 256, 1024),
       ("w_swa_up", 256, 1024), ("w_out", 256, 1024), ("w_mlp_up", 1024, 1024), ("w_mlp_down", 1024, 1024),
       ("w_ple", 256, 256), ("w_ple_gate", 256, 1024))
COL_SHARDED = ("w_in", "w_q_b", "w_kv_b", "w_mlp_up", "w_ple")
PACK_AT = {"w_in": ("early", 0, 0), "w_q_b": ("early", 1024, 0), "w_ple": ("early", 1024, 384), "w_kv_b": ("early", 1280, 0),
           "w_mla_up": ("late", 0, 0), "w_swa_up": ("late", 256, 0), "w_out": ("late", 512, 0), "w_ple_gate": ("late", 768, 0),
           "w_mlp_up": ("late", 1024, 0), "w_mlp_down": ("late", 2048, 0)}
PACK_ROWS = {"early": 1408, "late": 3072}
SMALL = (("g_mix_pre", 1024), ("g_q_a", 256), ("g_kv_a", 128), ("sinks", 16), ("g_mix_post", 1024),
         ("g_mlp_pre", 1024), ("g_mlp_post", 1024), ("g_ple", 1024))


def _dot(a, b):
    return jnp.dot(a, b, preferred_element_type=F32)


def _dot_nt(a, b):
    return lax.dot_general(a, b, NT, preferred_element_type=F32)


def _dot_tn(a, b):
    return lax.dot_general(a, b, TN, preferred_element_type=F32)


def _pcall(body, *, name, out_shape, grid=(), in_specs=None, out_specs=None, scratch=(), sem=None, vmem_mb=48, aliases=None):
    params = dict(vmem_limit_bytes=vmem_mb << 20)
    if sem is not None:
        params["dimension_semantics"] = sem
    return pl.pallas_call(body, name=name, grid=grid, in_specs=in_specs, out_specs=out_specs, out_shape=out_shape,
                          scratch_shapes=list(scratch), input_output_aliases=aliases or {},
                          compiler_params=pltpu.CompilerParams(**params))


def _rows(tm, n, col=0):
    return pl.BlockSpec((tm, n), lambda i: (i, col))


def _full(shape):
    return pl.BlockSpec(shape, lambda i: (0,) * len(shape))


def _rms(x, g):
    r = lax.rsqrt(jnp.mean(x * x, axis=-1, keepdims=True) + EPS)
    return x * r * g


def _rms_bwd(dy, x, g):
    r = lax.rsqrt(jnp.mean(x * x, axis=-1, keepdims=True) + EPS)
    xn = x * r
    dn = dy * g
    dx = r * (dn - xn * jnp.mean(dn * xn, axis=-1, keepdims=True))
    return dx, jnp.sum(dy * xn, axis=0, keepdims=True)


def _sigmoid(x):
    return 1.0 / (1.0 + jnp.exp(-x))


def _rope(x, c, a, b, half):
    return x * c + pltpu.roll(x, LANES - half, 1) * a + pltpu.roll(x, half, 1) * b


def _rope_tables(T, kind):
    lane = jnp.arange(LANES)
    if kind == "mla":
        half = MLA_ROPE // 2
        rel = lane - MLA_NOPE
        on = (rel >= 0) & (rel < MLA_ROPE)
        d = MLA_ROPE
    else:
        half = SWA_HD // 2
        rel = lane % SWA_HD
        on = jnp.ones((LANES,), bool)
        d = SWA_HD
    first = on & (rel < half)
    second = on & (rel >= half)
    f = jnp.where(first, rel, rel - half).astype(F32)
    inv = jnp.exp(-math.log(ROPE_THETA) * f * (2.0 / d))
    ang = jnp.arange(T, dtype=F32)[:, None] * inv[None, :]
    cos, sin = jnp.cos(ang), jnp.sin(ang)
    c = jnp.where(on[None], cos, 1.0)
    a = jnp.where(first[None], -sin, 0.0)
    b = jnp.where(second[None], sin, 0.0)
    return c, a, b


def _fwd_in(x, g1, w_in_p, tm):
    T = x.shape[0]

    def body(x_ref, g_ref, w_ref, z_ref, h_ref):
        h = _rms(x_ref[...], g_ref[...]).astype(BF16)
        h_ref[...] = h
        z_ref[...] = _dot(h, w_ref[...])

    return _pcall(body, name="fwd_in", grid=(T // tm,),
                  in_specs=[_rows(tm, D), _full((1, D)), _full((D, NZ))],
                  out_specs=[_rows(tm, NZ), _rows(tm, D)],
                  out_shape=[SDS((T, NZ), F32), SDS((T, D), BF16)], sem=("parallel",))(x, g1, w_in_p)


def _fwd_qkv(z, gq, gkv, wqb, wkn, wv, tab_m, tab_s, tm):
    T = z.shape[0]
    wqb_t, wkn_t, wv_t = wqb.T, wkn.T, wv.T
    tab_mt = [t.T for t in tab_m]

    def body(qa_ref, sq_ref, skd_ref, svd_ref, kva_ref, kr_ref, gq_ref, gkv_ref, wkn_ref, wv_ref, wqbt_ref, wknt_ref, wvt_ref,
             cm_ref, am_ref, bm_ref, cmt_ref, amt_ref, bmt_ref, cs_ref, as_ref, bs_ref,
             qn_ref, kvn_ref, km_ref, vm_ref, qt_ref, kt_ref, vt_ref, qs_ref, ks_ref, vs_ref):
        qn = _rms(qa_ref[...], gq_ref[...])
        qn_ref[...] = qn.astype(BF16)
        kvn = _rms(kva_ref[...], gkv_ref[...])
        kvn_b = kvn.astype(BF16)
        kvn_ref[...] = kvn_b
        qn_t = qn.T.astype(BF16)
        kvn_t = kvn.T.astype(BF16)
        cm, am, bm = cm_ref[...], am_ref[...], bm_ref[...]
        cmt, amt, bmt = cmt_ref[...], amt_ref[...], bmt_ref[...]
        cs, as_, bs = cs_ref[...], as_ref[...], bs_ref[...]
        k_rope = _rope(kr_ref[...], cm, am, bm, MLA_ROPE // 2)
        k_rope_t = k_rope.T
        half = MLA_ROPE // 2
        vm_ref[...] = _dot(kvn_b, wv_ref[...]).astype(BF16)
        km_all = _dot(kvn_b, wkn_ref[...])
        v_t = _dot(wvt_ref[...], kvn_t)
        q_t = _dot(wqbt_ref[...], qn_t)
        k_t = _dot(wknt_ref[...], kvn_t)
        ones_row = jnp.where(lax.broadcasted_iota(jnp.int32, (64, tm), 0) == 0, 1.0, 0.0)
        for h in range(MLA_HEADS):
            sl = slice(LANES * h, LANES * (h + 1))
            vt_ref[0, sl, :] = jnp.concatenate([v_t[64 * h:64 * (h + 1)], ones_row], axis=0).astype(BF16)
            qh = q_t[sl]
            qt_ref[0, sl, :] = (qh * cmt + pltpu.roll(qh, LANES - half, 0) * amt + pltpu.roll(qh, half, 0) * bmt).astype(BF16)
            km_ref[:, sl] = (km_all[:, sl] + k_rope).astype(BF16)
            kt_ref[0, sl, :] = (k_t[sl] + k_rope_t).astype(BF16)
        for j in range(D // LANES):
            sl = slice(LANES * j, LANES * (j + 1))
            qs_ref[:, sl] = _rope(sq_ref[:, sl], cs, as_, bs, SWA_HD // 2).astype(BF16)
        for j in range(2):
            sl = slice(LANES * j, LANES * (j + 1))
            ks_ref[:, sl] = _rope(skd_ref[:, sl], cs, as_, bs, SWA_HD // 2).astype(BF16)
        vs_ref[...] = svd_ref[...].astype(BF16)

    tab = [_rows(tm, LANES)] * 3
    tab_t = [pl.BlockSpec((LANES, tm), lambda i: (0, i))] * 3
    return _pcall(body, name="fwd_qkv", grid=(T // tm,),
                  in_specs=[_rows(tm, 256, 12), _rows(tm, 1024, 0), _rows(tm, 256, 13), _rows(tm, 256, 14),
                            _rows(tm, 128, 30), _rows(tm, 128, 31), _full((1, Q_LORA)), _full((1, KV_LORA)),
                            _full((KV_LORA, 2048)), _full((KV_LORA, 1024)), _full((2048, Q_LORA)), _full((2048, KV_LORA)),
                            _full((1024, KV_LORA))] + tab + tab_t + tab,
                  out_specs=[_rows(tm, Q_LORA), _rows(tm, KV_LORA), _rows(tm, 2048), _rows(tm, 1024),
                             pl.BlockSpec((1, 2048, tm), lambda i: (i, 0, 0)), pl.BlockSpec((1, 2048, tm), lambda i: (i, 0, 0)),
                             pl.BlockSpec((1, 2048, tm), lambda i: (i, 0, 0)),
                             _rows(tm, 1024), _rows(tm, 256), _rows(tm, 256)],
                  out_shape=[SDS((T, Q_LORA), BF16), SDS((T, KV_LORA), BF16), SDS((T, 2048), BF16),
                             SDS((T, 1024), BF16), SDS((T // tm, 2048, tm), BF16), SDS((T // tm, 2048, tm), BF16),
                             SDS((T // tm, 2048, tm), BF16),
                             SDS((T, 1024), BF16), SDS((T, 256), BF16), SDS((T, 256), BF16)],
                  sem=("parallel",))(z, z, z, z, z, z, gq, gkv, wkn, wv, wqb_t, wkn_t, wv_t, *tab_m, *tab_mt, *tab_s)


def _mla_fwd(qt, km, vt, tb):
    T = km.shape[0]
    nb = T // tb
    cc = ATT_COLS

    per = max(n for n in (1, 2, 4) if nb % n == 0)

    def body(q_ref, k_ref, vt_ref, o_ref, l_ref, s_ref, p_ref, al_ref, m_ref, acc_ref):
        for blk in range(per):
            one_block(per * pl.program_id(1) + blk, blk, q_ref, k_ref, vt_ref, o_ref, l_ref, s_ref, p_ref, al_ref, m_ref, acc_ref)

    def one_block(i, blk, q_ref, k_ref, vt_ref, o_ref, l_ref, s_ref, p_ref, al_ref, m_ref, acc_ref):
        m_ref[...] = jnp.full(m_ref.shape, NEG, F32)
        acc_ref[...] = jnp.zeros_like(acc_ref)
        p_ref[1] = jnp.zeros(p_ref.shape[1:], BF16)
        al_ref[1] = jnp.ones(al_ref.shape[1:], F32)
        key = lax.broadcasted_iota(jnp.int32, (tb, cc), 0)
        qry = lax.broadcasted_iota(jnp.int32, (tb, cc), 1)

        def scores(j, slot):
            off = pl.multiple_of(j * tb, tb)
            for hh in range(2):
                sl = slice(LANES * hh, LANES * (hh + 1))
                s_ref[slot, hh] = _dot(k_ref[pl.ds(off, tb), sl], q_ref[blk, sl, :])

        def softmax(slot, diagonal):
            chains = [(hh, slice(cc * c, cc * (c + 1)), c) for hh in range(2) for c in range(tb // cc)]

            def scaled(hh, cols, c):
                t = s_ref[slot, hh, :, cols] * MLA_LOG2_SCALE
                return jnp.where(key <= qry + cc * c, t, NEG) if diagonal else t

            tops = []
            for hh, cols, c in chains:
                if diagonal:
                    top = jnp.max(scaled(hh, cols, c), axis=0, keepdims=True)
                else:
                    top = jnp.max(s_ref[slot, hh, :, cols], axis=0, keepdims=True) * MLA_LOG2_SCALE
                m_old = m_ref[hh, :, cols]
                mn = jnp.maximum(m_old, top)
                m_ref[hh, :, cols] = mn
                al_ref[slot, hh, :, cols] = jnp.exp2(m_old - mn)
                tops.append(mn)
            for (hh, cols, c), mn in zip(chains, tops):
                p_ref[slot, hh, :, cols] = jnp.exp2(scaled(hh, cols, c) - mn).astype(BF16)

        def accumulate(j, slot):
            for hh in range(2):
                acc_ref[hh] = al_ref[slot, hh] * acc_ref[hh] + _dot(vt_ref[j, LANES * hh:LANES * hh + VT_ROWS, :], p_ref[slot, hh])

        def step(t, carry):
            scores(2 * t + 1, 1)
            accumulate(jnp.maximum(2 * t - 1, 0), 1)
            softmax(0, False)
            scores(2 * t + 2, 0)
            accumulate(2 * t, 0)
            softmax(1, False)
            return carry

        scores(0, 0)
        lax.fori_loop(0, i // 2, step, 0)

        @pl.when(i % 2 == 1)
        def _():
            scores(i, 1)
            accumulate(jnp.maximum(i - 2, 0), 1)
            softmax(0, False)
            accumulate(i - 1, 0)
            softmax(1, True)
            accumulate(i, 1)

        @pl.when(i % 2 == 0)
        def _():
            accumulate(jnp.maximum(i - 1, 0), 1)
            softmax(0, True)
            accumulate(i, 0)
        den = [acc_ref[hh, 64:65, :] for hh in range(2)]
        o_ref[tb * blk:tb * (blk + 1), :] = jnp.concatenate([acc_ref[hh, 0:64, :] / den[hh] for hh in range(2)], axis=0).T
        sub = lax.broadcasted_iota(jnp.int32, (8, tb), 0)
        lse = [m_ref[hh] + jnp.log(den[hh]) * LOG2_E for hh in range(2)]
        l_ref[0, blk] = jnp.where(sub == 0, lse[0], jnp.where(sub == 1, lse[1], 0.0))

    return _pcall(body, name="mla_fwd", grid=(MLA_HEADS // 2, nb // per),
                  in_specs=[pl.BlockSpec((per, 256, tb), lambda p, i: (i, p, 0)), pl.BlockSpec((T, 256), lambda p, i: (0, p)),
                            pl.BlockSpec((nb, 2 * LANES, tb), lambda p, i: (0, p, 0))],
                  out_specs=[pl.BlockSpec((per * tb, LANES), lambda p, i: (i, p)),
                             pl.BlockSpec((1, per, 8, tb), lambda p, i: (p, i, 0, 0))],
                  out_shape=[SDS((T, D), F32), SDS((MLA_HEADS // 2, nb, 8, tb), F32)],
                  scratch=[pltpu.VMEM((2, 2, tb, tb), F32), pltpu.VMEM((2, 2, tb, tb), BF16), pltpu.VMEM((2, 2, 1, tb), F32),
                           pltpu.VMEM((2, 1, tb), F32), pltpu.VMEM((2, VT_ROWS, tb), F32)],
                  sem=("parallel", "arbitrary"))(qt, km, vt)


def _swa_mask(n):
    row = lax.broadcasted_iota(jnp.int32, (WINDOW, 2 * WINDOW), 0)
    col = lax.broadcasted_iota(jnp.int32, (WINDOW, 2 * WINDOW), 1)
    rel = row - col + WINDOW
    return (rel >= 0) & (rel < WINDOW) & ((col >= WINDOW) | (n > 0))


def _swa_specs(T):
    nb = T // WINDOW
    cur = lambda w: pl.BlockSpec((WINDOW, w), lambda n: (n, 0))
    prev = lambda w: pl.BlockSpec((WINDOW, w), lambda n: (jnp.maximum(n - 1, 0), 0))
    return nb, cur, prev


def _swa_fwd(sinks, qs, ks, vs):
    T = qs.shape[0]
    nb, cur, prev = _swa_specs(T)

    def body(sink_ref, q_ref, kc_ref, kp_ref, vc_ref, vp_ref, o_ref, l_ref, kb_ref, vb_ref, s_ref, p_ref):
        n = pl.program_id(0)
        mask = _swa_mask(n)
        lo = lax.broadcasted_iota(jnp.int32, (WINDOW, LANES), 1) < 64
        hi = jnp.logical_not(lo)
        for g in range(2):
            gs = slice(LANES * g, LANES * (g + 1))
            kb_ref[g] = jnp.concatenate([kp_ref[:, gs], kc_ref[:, gs]], axis=0)
            vb_ref[g] = jnp.concatenate([vp_ref[:, gs], vc_ref[:, gs]], axis=0)
        for h in range(SWA_HEADS):
            qp = q_ref[:, LANES * (h // 2):LANES * (h // 2 + 1)]
            qh = jnp.where(lo if h % 2 == 0 else hi, qp, jnp.zeros_like(qp))
            s_ref[h] = _dot_nt(qh, kb_ref[h // 8])
        for j in range(SWA_HEADS // 2):
            sl = slice(LANES * j, LANES * (j + 1))
            lses = []
            for h in (2 * j, 2 * j + 1):
                s = jnp.where(mask, s_ref[h] * SWA_SCALE, NEG)
                sk = sink_ref[h]
                m = jnp.maximum(jnp.max(s, axis=1, keepdims=True), sk)
                e = jnp.exp(s - m)
                den = jnp.sum(e, axis=1, keepdims=True) + jnp.exp(sk - m)
                p_ref[h] = (e / den).astype(BF16)
                lses.append(jnp.broadcast_to(m + jnp.log(den), (WINDOW, LANES)))
            l_ref[:, sl] = jnp.where(lo, lses[0], lses[1])
        for j in range(SWA_HEADS // 2):
            vb = vb_ref[j // 4]
            o_ref[:, LANES * j:LANES * (j + 1)] = jnp.where(lo, _dot(p_ref[2 * j], vb), _dot(p_ref[2 * j + 1], vb))

    return _pcall(body, name="swa_fwd", grid=(nb,),
                  in_specs=[pl.BlockSpec(memory_space=pltpu.SMEM), cur(D), cur(256), prev(256), cur(256), prev(256)],
                  out_specs=[cur(D), cur(D)], out_shape=[SDS((T, D), F32)] * 2,
                  scratch=[pltpu.VMEM((2, 2 * WINDOW, LANES), BF16), pltpu.VMEM((2, 2 * WINDOW, LANES), BF16),
                           pltpu.VMEM((SWA_HEADS, WINDOW, 2 * WINDOW), F32), pltpu.VMEM((SWA_HEADS, WINDOW, 2 * WINDOW), BF16)],
                  sem=("parallel",))(sinks, qs, ks, ks, vs, vs)


def _fwd_mix(om, os_, z, x, wmu, wsu, wo, g2, tm):
    T = x.shape[0]

    def body(om_ref, os_ref, ga_ref, gb_ref, x_ref, wmu_ref, wsu_ref, wo_ref, g2_ref,
             y_ref, yo_ref, au_ref, bu_ref, x1_ref):
        au = _dot(om_ref[...].astype(BF16), wmu_ref[...])
        bu = _dot(os_ref[...].astype(BF16), wsu_ref[...])
        au_ref[...] = au
        bu_ref[...] = bu
        y = (_sigmoid(ga_ref[...]) * au + _sigmoid(gb_ref[...]) * bu).astype(BF16)
        y_ref[...] = y
        yo = _dot(y, wo_ref[...])
        yo_ref[...] = yo
        x1_ref[...] = x_ref[...] + _rms(yo, g2_ref[...])

    r = _rows(tm, D)
    w = _full((D, D))
    return _pcall(body, name="fwd_mix", grid=(T // tm,),
                  in_specs=[r, r, _rows(tm, D, 1), _rows(tm, D, 2), r, w, w, w, _full((1, D))],
                  out_specs=[r] * 5,
                  out_shape=[SDS((T, D), BF16), SDS((T, D), F32), SDS((T, D), F32), SDS((T, D), F32), SDS((T, D), F32)],
                  sem=("parallel",))(om, os_, z, z, x, wmu, wsu, wo, g2)


def _fwd_mlp_up(x1, g3, w1, tm):
    T = x1.shape[0]

    def body(x_ref, g_ref, w_ref, h_ref, u_ref):
        h = _rms(x_ref[...], g_ref[...]).astype(BF16)
        h_ref[...] = h
        u_ref[...] = jnp.square(jnp.maximum(_dot(h, w_ref[...]), 0.0)).astype(BF16)

    return _pcall(body, name="fwd_mlp_up", grid=(T // tm,),
                  in_specs=[_rows(tm, D), _full((1, D)), _full((D, D_FF))],
                  out_specs=[_rows(tm, D), _rows(tm, D_FF)],
                  out_shape=[SDS((T, D), BF16), SDS((T, D_FF), BF16)],
                  sem=("parallel",))(x1, g3, w1)


def _fwd_mlp_down(u, w2, x1, g4, tm):
    T = x1.shape[0]

    def body(u_ref, w_ref, x_ref, g_ref, d_ref, x2_ref):
        d = _dot(u_ref[...], w_ref[...])
        d_ref[...] = d
        x2_ref[...] = x_ref[...] + _rms(d, g_ref[...])

    return _pcall(body, name="fwd_mlp_down", grid=(T // tm,),
                  in_specs=[_rows(tm, D_FF), _full((D_FF, D)), _rows(tm, D), _full((1, D))],
                  out_specs=[_rows(tm, D), _rows(tm, D)], out_shape=[SDS((T, D), F32)] * 2,
                  sem=("parallel",))(u, w2, x1, g4)


def _ple_fwd_bwd(p, x2, tgt, wple, g5, wpg, tm):
    T = x2.shape[0]

    def body(p_ref, x2_ref, t_ref, wple_ref, g5_ref, wpg_ref, loss_ref, dx2_ref, dgt_ref, de0_ref, dg5_ref):
        @pl.when(pl.program_id(0) == 0)
        def _():
            loss_ref[...] = jnp.zeros_like(loss_ref)
            dg5_ref[...] = jnp.zeros_like(dg5_ref)

        e0 = _dot(p_ref[...].astype(BF16), wple_ref[...])
        g5 = g5_ref[...]
        r = lax.rsqrt(jnp.mean(e0 * e0, axis=-1, keepdims=True) + EPS)
        en = e0 * r
        e = en * g5
        x2 = x2_ref[...]
        s = _sigmoid(_dot(x2.astype(BF16), wpg_ref[...]))
        diff = x2 + s * e - t_ref[...]
        sq = jnp.sum(jnp.sum(diff * diff, axis=1, keepdims=True), axis=0, keepdims=True)
        loss_ref[...] += jnp.broadcast_to(sq * (0.5 / D), loss_ref.shape)
        dx3 = diff * (1.0 / D)
        de = dx3 * s
        dgt = (dx3 * e * s * (1.0 - s)).astype(BF16)
        dgt_ref[...] = dgt
        dn = de * g5
        de0_ref[...] = (r * (dn - en * jnp.mean(dn * en, axis=-1, keepdims=True))).astype(BF16)
        dg5_ref[...] += jnp.sum(de * en, axis=0, keepdims=True)
        dx2_ref[...] = dx3 + _dot_nt(dgt, wpg_ref[...])

    r = _rows(tm, D)
    return _pcall(body, name="ple_fwd_bwd", grid=(T // tm,),
                  in_specs=[_rows(tm, PLE), r, r, _full((PLE, D)), _full((1, D)), _full((D, D))],
                  out_specs=[_full((8, LANES)), r, r, r, _full((1, D))],
                  out_shape=[SDS((8, LANES), F32), SDS((T, D), F32), SDS((T, D), BF16), SDS((T, D), BF16), SDS((1, D), F32)],
                  sem=("arbitrary",))(p, x2, tgt, wple, g5, wpg)


def _bwd_mlp_down(dx2, d, g4, w2, u, tm):
    T = dx2.shape[0]

    def body(dx_ref, d_ref, g_ref, w_ref, u_ref, dd_ref, da_ref, dg_ref):
        @pl.when(pl.program_id(0) == 0)
        def _():
            dg_ref[...] = jnp.zeros_like(dg_ref)

        dd, dg = _rms_bwd(dx_ref[...], d_ref[...], g_ref[...])
        dg_ref[...] += dg
        ddb = dd.astype(BF16)
        dd_ref[...] = ddb
        du = _dot_nt(ddb, w_ref[...])
        da_ref[...] = (du * (2.0 * jnp.sqrt(u_ref[...].astype(F32)))).astype(BF16)

    return _pcall(body, name="bwd_mlp_down", grid=(T // tm,),
                  in_specs=[_rows(tm, D), _rows(tm, D), _full((1, D)), _full((D_FF, D)), _rows(tm, D_FF)],
                  out_specs=[_rows(tm, D), _rows(tm, D_FF), _full((1, D))],
                  out_shape=[SDS((T, D), BF16), SDS((T, D_FF), BF16), SDS((1, D), F32)],
                  sem=("arbitrary",))(dx2, d, g4, w2, u)


def _bwd_mlp_up(da, w1, x1, g3, dx2, tm):
    T = dx2.shape[0]

    def body(da_ref, w_ref, x_ref, g_ref, dx2_ref, dx1_ref, dg_ref):
        @pl.when(pl.program_id(0) == 0)
        def _():
            dg_ref[...] = jnp.zeros_like(dg_ref)

        dh = _dot_nt(da_ref[...], w_ref[...])
        dx, dg = _rms_bwd(dh, x_ref[...], g_ref[...])
        dg_ref[...] += dg
        dx1_ref[...] = dx2_ref[...] + dx

    return _pcall(body, name="bwd_mlp_up", grid=(T // tm,),
                  in_specs=[_rows(tm, D_FF), _full((D, D_FF)), _rows(tm, D), _full((1, D)), _rows(tm, D)],
                  out_specs=[_rows(tm, D), _full((1, D))],
                  out_shape=[SDS((T, D), F32), SDS((1, D), F32)], sem=("arbitrary",))(da, w1, x1, g3, dx2)


def _bwd_mix(dx1, yo, g2, wo, z, au, bu, wmu, wsu, om, tm):
    T = dx1.shape[0]

    def body(dx_ref, yo_ref, g_ref, wo_ref, ga_ref, gb_ref, au_ref, bu_ref, wmu_ref, wsu_ref, om_ref,
             dyo_ref, dg_ref, dau_ref, dbu_ref, dga_ref, dgb_ref, dos_ref, dl_ref, dot_ref):
        @pl.when(pl.program_id(0) == 0)
        def _():
            dg_ref[...] = jnp.zeros_like(dg_ref)

        dyo, dg = _rms_bwd(dx_ref[...], yo_ref[...], g_ref[...])
        dg_ref[...] += dg
        dyob = dyo.astype(BF16)
        dyo_ref[...] = dyob
        dy = _dot_nt(dyob, wo_ref[...])
        sa = _sigmoid(ga_ref[...])
        sb = _sigmoid(gb_ref[...])
        dau = (dy * sa).astype(BF16)
        dbu = (dy * sb).astype(BF16)
        dau_ref[...] = dau
        dbu_ref[...] = dbu
        dga_ref[...] = (dy * au_ref[...] * sa * (1.0 - sa)).astype(BF16)
        dgb_ref[...] = (dy * bu_ref[...] * sb * (1.0 - sb)).astype(BF16)
        dom = _dot_nt(dau, wmu_ref[...])
        dos_ref[...] = _dot_nt(dbu, wsu_ref[...])
        prod = dom * om_ref[...]
        sub = lax.broadcasted_iota(jnp.int32, (8, tm), 0)
        for pr in range(MLA_HEADS // 2):
            sl = slice(LANES * pr, LANES * (pr + 1))
            pt = prod[:, sl].T
            d0 = jnp.sum(pt[0:64], axis=0, keepdims=True)
            d1 = jnp.sum(pt[64:128], axis=0, keepdims=True)
            dl_ref[pr, 0] = jnp.where(sub == 0, d0, jnp.where(sub == 1, d1, 0.0))
            dot_ref[0, sl, :] = dom[:, sl].T.astype(BF16)

    r = _rows(tm, D)
    w = _full((D, D))
    return _pcall(body, name="bwd_mix", grid=(T // tm,),
                  in_specs=[r, r, _full((1, D)), w, _rows(tm, D, 1), _rows(tm, D, 2), r, r, w, w, r],
                  out_specs=[r, _full((1, D)), r, r, r, r, r, pl.BlockSpec((MLA_HEADS // 2, 1, 8, tm), lambda i: (0, i, 0, 0)),
                             pl.BlockSpec((1, D, tm), lambda i: (i, 0, 0))],
                  out_shape=[SDS((T, D), BF16), SDS((1, D), F32), SDS((T, D), BF16), SDS((T, D), BF16), SDS((T, D), BF16),
                             SDS((T, D), BF16), SDS((T, D), F32), SDS((MLA_HEADS // 2, T // tm, 8, tm), F32),
                             SDS((T // tm, D, tm), BF16)],
                  sem=("arbitrary",))(dx1, yo, g2, wo, z, z, au, bu, wmu, wsu, om)


def _mla_bwd(qt, km, kt, vm, dot, lse, delta, tb):
    T = km.shape[0]
    nb = T // tb
    cc = ATT_COLS

    per = max(n for n in (1, 2, 4) if nb % n == 0)

    def body(qt_ref, k_ref, kt_ref, v_ref, dot_ref, l_ref, dl_ref, dqt_ref, dkt_ref, dvt_ref,
             s_ref, dp_ref, p_ref, ds_ref, vh_ref):
        @pl.when(pl.program_id(1) == 0)
        def _():
            dqt_ref[...] = jnp.zeros_like(dqt_ref)

        dkt_ref[...] = jnp.zeros_like(dkt_ref)
        dvt_ref[...] = jnp.zeros_like(dvt_ref)
        for blk in range(per):
            one_block(per * pl.program_id(1) + blk, blk, qt_ref, k_ref, kt_ref, v_ref, dot_ref, l_ref, dl_ref, dqt_ref, dkt_ref,
                      dvt_ref, s_ref, dp_ref, p_ref, ds_ref, vh_ref)

    def one_block(j, blk, qt_ref, k_ref, kt_ref, v_ref, dot_ref, l_ref, dl_ref, dqt_ref, dkt_ref, dvt_ref,
                  s_ref, dp_ref, p_ref, ds_ref, vh_ref):
        lo = lax.broadcasted_iota(jnp.int32, (tb, LANES), 1) < 64
        key = lax.broadcasted_iota(jnp.int32, (tb, cc), 0)
        qry = lax.broadcasted_iota(jnp.int32, (tb, cc), 1)
        rows_j = slice(tb * blk, tb * (blk + 1))
        v = v_ref[rows_j, :]
        vh_ref[0] = jnp.where(lo, v, jnp.zeros_like(v))
        vh_ref[1] = jnp.where(lo, jnp.zeros_like(v), v)

        def scores(i, slot):
            for hh in range(2):
                sl = slice(LANES * hh, LANES * (hh + 1))
                s_ref[slot, hh] = _dot(k_ref[rows_j, sl], qt_ref[i, sl, :])
                dp_ref[slot, hh] = _dot(vh_ref[hh], dot_ref[i])

        def grads(i, slot, diagonal):
            lse_i = l_ref[0, i]
            delta_i = dl_ref[0, i]
            for hh in range(2):
                for c in range(tb // cc):
                    cols = slice(cc * c, cc * (c + 1))
                    p = jnp.exp2(s_ref[slot, hh, :, cols] * MLA_LOG2_SCALE - lse_i[hh:hh + 1, cols])
                    if diagonal:
                        p = jnp.where(key <= qry + cc * c, p, 0.0)
                    p_ref[hh, :, cols] = p.astype(BF16)
                    ds_ref[hh, :, cols] = (p * (dp_ref[slot, hh, :, cols] - delta_i[hh:hh + 1, cols]) * MLA_SCALE).astype(BF16)
            for hh in range(2):
                sl = slice(LANES * hh, LANES * (hh + 1))
                half = slice(64 * hh, 64 * (hh + 1))
                dvt_ref[blk, half, :] += _dot_nt(dot_ref[i, half, :], p_ref[hh])
                real = slice(LANES * hh, LANES * hh + MLA_NOPE + MLA_ROPE)
                dkt_ref[blk, real, :] += _dot_nt(qt_ref[i, real, :], ds_ref[hh])
                dqt_ref[i, real, :] += _dot(kt_ref[blk, real, :], ds_ref[hh])

        n_off = nb - 1 - j

        def step(u, carry):
            i0 = j + 1 + 2 * u
            scores(i0 + 1, 1)
            grads(i0, 0, False)
            scores(jnp.where(i0 + 2 < nb, i0 + 2, j), 0)
            grads(i0 + 1, 1, False)
            return carry

        scores(jnp.where(n_off > 0, j + 1, j), 0)
        lax.fori_loop(0, n_off // 2, step, 0)

        @pl.when(n_off % 2 == 1)
        def _():
            scores(j, 1)
            grads(nb - 1, 0, False)
            grads(j, 1, True)

        @pl.when(n_off % 2 == 0)
        def _():
            grads(j, 0, True)

    blk = lambda w: pl.BlockSpec((per * tb, w), lambda p, j: (j, p))
    stat = pl.BlockSpec((1, nb, 8, tb), lambda p, j: (p, 0, 0, 0))
    pair_t = lambda w: pl.BlockSpec((nb, w, tb), lambda p, j: (0, p, 0))
    blk_t = lambda w: pl.BlockSpec((per, w, tb), lambda p, j: (j, p, 0))
    return _pcall(body, name="mla_bwd", grid=(MLA_HEADS // 2, nb // per),
                  in_specs=[pair_t(256), blk(256), blk_t(256), blk(LANES), pair_t(LANES), stat, stat],
                  out_specs=[pair_t(256), blk_t(256), blk_t(LANES)],
                  out_shape=[SDS((nb, 2048, tb), F32), SDS((nb, 2048, tb), F32), SDS((nb, D, tb), F32)],
                  scratch=[pltpu.VMEM((2, 2, tb, tb), F32), pltpu.VMEM((2, 2, tb, tb), F32), pltpu.VMEM((2, tb, tb), BF16),
                           pltpu.VMEM((2, tb, tb), BF16), pltpu.VMEM((2, tb, LANES), BF16)],
                  sem=("parallel", "arbitrary"))(qt, km, kt, vm, dot, lse, delta)


def _swa_bwd(sinks, qs, ks, vs, do, o, lse):
    T = qs.shape[0]
    nb, cur, prev = _swa_specs(T)

    def body(sink_ref, q_ref, kc_ref, kp_ref, vc_ref, vp_ref, do_ref, o_ref, l_ref,
             dq_ref, dkc_ref, dkp_ref, dvc_ref, dvp_ref, dsink_ref, kb_ref, vb_ref, s_ref, dp_ref, p_ref, ds_ref):
        n = pl.program_id(0)

        @pl.when(n == 0)
        def _():
            dsink_ref[...] = jnp.zeros_like(dsink_ref)

        mask = _swa_mask(n)
        lo = lax.broadcasted_iota(jnp.int32, (WINDOW, LANES), 1) < 64
        hi = jnp.logical_not(lo)
        lane8 = lax.broadcasted_iota(jnp.int32, (8, LANES), 1)
        for g in range(2):
            gs = slice(LANES * g, LANES * (g + 1))
            kb_ref[g] = jnp.concatenate([kp_ref[:, gs], kc_ref[:, gs]], axis=0)
            vb_ref[g] = jnp.concatenate([vp_ref[:, gs], vc_ref[:, gs]], axis=0)

        def head(h):
            sl = slice(LANES * (h // 2), LANES * (h // 2 + 1))
            hm = lo if h % 2 == 0 else hi
            qp = q_ref[:, sl]
            return hm, sl, jnp.where(hm, qp, jnp.zeros_like(qp)), jnp.where(hm, do_ref[:, sl], 0.0).astype(BF16)

        for h in range(SWA_HEADS):
            _, _, qh, dom = head(h)
            s_ref[h] = _dot_nt(qh, kb_ref[h // 8])
            dp_ref[h] = _dot_nt(dom, vb_ref[h // 8])
        dsink = jnp.zeros((8, LANES), F32)
        for h in range(SWA_HEADS):
            hm, sl, _, _ = head(h)
            lse_h = jnp.max(jnp.where(hm, l_ref[:, sl], -jnp.inf), axis=1, keepdims=True)
            delta = jnp.sum(jnp.where(hm, do_ref[:, sl] * o_ref[:, sl], 0.0), axis=1, keepdims=True)
            p = jnp.exp(jnp.where(mask, s_ref[h] * SWA_SCALE, NEG) - lse_h)
            p_ref[h] = p.astype(BF16)
            ds_ref[h] = (p * (dp_ref[h] - delta) * SWA_SCALE).astype(BF16)
            d_sink = -jnp.sum(jnp.exp(sink_ref[h] - lse_h) * delta, axis=0, keepdims=True)
            dsink = dsink + jnp.where(lane8 == h, d_sink, 0.0)
        dsink_ref[...] += dsink
        for g in range(2):
            gs = slice(LANES * g, LANES * (g + 1))
            dkb = jnp.zeros((2 * WINDOW, LANES), F32)
            dvb = jnp.zeros((2 * WINDOW, LANES), F32)
            for j in range(4 * g, 4 * g + 4):
                dqs = []
                for h in (2 * j, 2 * j + 1):
                    _, _, qh, dom = head(h)
                    dvb = dvb + _dot_tn(p_ref[h], dom)
                    dkb = dkb + _dot_tn(ds_ref[h], qh)
                    dqs.append(_dot(ds_ref[h], kb_ref[g]))
                dq_ref[:, LANES * j:LANES * (j + 1)] = jnp.where(lo, dqs[0], dqs[1])
            dkp_ref[:, gs] = dkb[:WINDOW]
            dkc_ref[:, gs] = dkb[WINDOW:]
            dvp_ref[:, gs] = dvb[:WINDOW]
            dvc_ref[:, gs] = dvb[WINDOW:]

    band = pltpu.VMEM((2, 2 * WINDOW, LANES), BF16)
    return _pcall(body, name="swa_bwd", grid=(nb,),
                  in_specs=[pl.BlockSpec(memory_space=pltpu.SMEM), cur(D), cur(256), prev(256), cur(256), prev(256),
                            cur(D), cur(D), cur(D)],
                  out_specs=[cur(D), cur(256), cur(256), cur(256), cur(256), _full((8, LANES))],
                  out_shape=[SDS((T, D), F32), SDS((T, 256), F32), SDS((T, 256), F32), SDS((T, 256), F32), SDS((T, 256), F32),
                             SDS((8, LANES), F32)],
                  scratch=[band, band, pltpu.VMEM((SWA_HEADS, WINDOW, 2 * WINDOW), F32),
                           pltpu.VMEM((SWA_HEADS, WINDOW, 2 * WINDOW), F32), pltpu.VMEM((SWA_HEADS, WINDOW, 2 * WINDOW), BF16),
                           pltpu.VMEM((SWA_HEADS, WINDOW, 2 * WINDOW), BF16)],
                  sem=("arbitrary",))(sinks, qs, ks, ks, vs, vs, do, o, lse)


def _bwd_qkv(dqm, dkm, dvm, dqs, dkc, dkp, dvc, dvp, z, gq, gkv, wqb, wkn, wv, tab_m, tab_s):
    T = z.shape[0]
    tm = WINDOW
    nb = T // tm
    per = dqm.shape[2] // tm

    tab_mt = [t.T for t in tab_m]
    half = MLA_ROPE // 2

    def rope_t(v, c, a, b):
        return v * c + pltpu.roll(v, LANES - half, 0) * a + pltpu.roll(v, half, 0) * b

    def rms_bwd_t(dy, x, g):
        r = lax.rsqrt(jnp.mean(x * x, axis=0, keepdims=True) + EPS)
        xn = x * r
        dn = dy * g
        return r * (dn - xn * jnp.mean(dn * xn, axis=0, keepdims=True)), jnp.sum(dy * xn, axis=1, keepdims=True)

    def body(dqm_ref, dkm_ref, dvm_ref, dqs_ref, dkc_ref, dkp_ref, dvc_ref, dvp_ref, qa_ref, kva_ref, gq_ref, gkv_ref,
             wqb_ref, wkn_ref, wv_ref, cmt_ref, amt_ref, bmt_ref, cs_ref, as_ref, bs_ref,
             dq_out, dkn_out, dv_out, dsq_ref, drest_ref, dgq_ref, dgkv_ref):
        i = pl.program_id(0)

        @pl.when(i == 0)
        def _():
            dgq_ref[...] = jnp.zeros_like(dgq_ref)
            dgkv_ref[...] = jnp.zeros_like(dgkv_ref)

        cmt, amt, bmt = cmt_ref[...], -amt_ref[...], -bmt_ref[...]
        cs, as_, bs = cs_ref[...], -as_ref[...], -bs_ref[...]
        row = lax.broadcasted_iota(jnp.int32, (LANES, tm), 0)
        nope = row < MLA_NOPE
        roped = jnp.logical_and(row >= MLA_NOPE, row < MLA_NOPE + MLA_ROPE)
        dkr = jnp.zeros((LANES, tm), F32)
        for h in range(MLA_HEADS):
            sl = slice(LANES * h, LANES * (h + 1))
            dq_out[0, sl, :] = rope_t(dqm_ref[0, sl, :], cmt, amt, bmt).astype(BF16)
            dk_h = dkm_ref[0, sl, :]
            dkn_out[0, sl, :] = jnp.where(nope, dk_h, 0.0).astype(BF16)
            dkr = dkr + jnp.where(roped, dk_h, 0.0)
        dv_out[0] = dvm_ref[0].astype(BF16)
        dqn = _dot(wqb_ref[...], dq_out[0])
        dkvn = _dot(wkn_ref[...], dkn_out[0]) + _dot(wv_ref[...], dv_out[0])
        dqa, dgq = rms_bwd_t(dqn, qa_ref[...].T, gq_ref[...])
        dkva, dgkv = rms_bwd_t(dkvn, kva_ref[...].T, gkv_ref[...])
        dgq_ref[...] += dgq
        dgkv_ref[...] += dgkv
        for j in range(D // LANES):
            sl = slice(LANES * j, LANES * (j + 1))
            dsq_ref[:, sl] = _rope(dqs_ref[:, sl], cs, as_, bs, SWA_HD // 2).astype(BF16)
        keep = (i < nb - 1).astype(F32)
        drest_ref[:, 0:256] = dqa.T.astype(BF16)
        for j in range(2):
            sl = slice(LANES * j, LANES * (j + 1))
            dk = dkc_ref[:, sl] + keep * dkp_ref[:, sl]
            drest_ref[:, 256 + LANES * j:256 + LANES * (j + 1)] = _rope(dk, cs, as_, bs, SWA_HD // 2).astype(BF16)
        drest_ref[:, 512:768] = (dvc_ref[...] + keep * dvp_ref[...]).astype(BF16)
        drest_ref[:, 768:896] = dkva.T.astype(BF16)
        drest_ref[:, 896:1024] = rope_t(dkr, cmt, amt, bmt).T.astype(BF16)

    nxt = pl.BlockSpec((tm, 256), lambda i: (jnp.minimum(i + 1, nb - 1), 0))
    tab = [_rows(tm, LANES)] * 3
    tab_t = [pl.BlockSpec((LANES, tm), lambda i: (0, i))] * 3
    blk_t = lambda w: pl.BlockSpec((1, w, tm), lambda i: (i // per, 0, i % per))
    return _pcall(body, name="bwd_qkv", grid=(nb,),
                  in_specs=[blk_t(2048), blk_t(2048), blk_t(1024), _rows(tm, 1024), _rows(tm, 256), nxt,
                            _rows(tm, 256), nxt, _rows(tm, 256, 12), _rows(tm, 128, 30), _full((Q_LORA, 1)), _full((KV_LORA, 1)),
                            _full((Q_LORA, 2048)), _full((KV_LORA, 2048)), _full((KV_LORA, 1024))] + tab_t + tab,
                  out_specs=[blk_t(2048), blk_t(2048), blk_t(1024), _rows(tm, 1024), _rows(tm, 1024),
                             _full((Q_LORA, 1)), _full((KV_LORA, 1))],
                  out_shape=[SDS(dqm.shape, BF16), SDS(dkm.shape, BF16), SDS(dvm.shape, BF16), SDS((T, 1024), BF16),
                             SDS((T, 1024), BF16), SDS((Q_LORA, 1), F32), SDS((KV_LORA, 1), F32)],
                  sem=("arbitrary",))(dqm, dkm, dvm, dqs, dkc, dkp, dvc, dvp, z, z, gq.reshape(Q_LORA, 1),
                                      gkv.reshape(KV_LORA, 1), wqb, wkn, wv, *tab_mt, *tab_s)


def _bwd_in(dsq, dga, dgb, drest, w_in_p, x, g1, dx1, tm):
    T = x.shape[0]

    def body(a_ref, b_ref, c_ref, d_ref, w_ref, x_ref, g_ref, dx1_ref, dx_ref, dg_ref):
        @pl.when(pl.program_id(0) == 0)
        def _():
            dg_ref[...] = jnp.zeros_like(dg_ref)

        dh = (_dot_nt(a_ref[...], w_ref[:, 0:1024]) + _dot_nt(b_ref[...], w_ref[:, 1024:2048])
              + _dot_nt(c_ref[...], w_ref[:, 2048:3072]) + _dot_nt(d_ref[...], w_ref[:, 3072:4096]))
        dx, dg = _rms_bwd(dh, x_ref[...], g_ref[...])
        dg_ref[...] += dg
        dx_ref[...] = dx1_ref[...] + dx

    r = _rows(tm, D)
    return _pcall(body, name="bwd_in", grid=(T // tm,),
                  in_specs=[r, r, r, r, _full((D, NZ)), r, _full((1, D)), r],
                  out_specs=[r, _full((1, D))], out_shape=[SDS((T, D), F32), SDS((1, D), F32)],
                  sem=("arbitrary",))(dsq, dga, dgb, drest, w_in_p, x, g1, dx1)


def _wgrad(a, g, name, into=None):
    T, K = a.shape
    N = g.shape[1]
    tk, tn, tt = min(K, 1024), min(N, 1024), min(T, 1024)
    if into is not None:
        buf, weight = into
        _, row0, lane0 = PACK_AT[weight]
        shard = {n: (r, c) for n, r, c in BIG}[weight]
        assert lane0 == 0 and shard[1] == D and tk % shard[0] == 0
        per_step = tk // shard[0]
    assert K % tk == 0 and N % tn == 0 and T % tt == 0, (a.shape, g.shape)
    steps = T // tt

    def body(a_ref, g_ref, *rest):
        o_ref, acc_ref = rest[-2:]
        t = pl.program_id(2)

        @pl.when(t == 0)
        def _():
            acc_ref[...] = jnp.zeros_like(acc_ref)

        acc_ref[...] += _dot_tn(a_ref[...].astype(BF16), g_ref[...].astype(BF16))

        @pl.when(t == steps - 1)
        def _():
            o_ref[...] = acc_ref[...].astype(o_ref.dtype).reshape(o_ref.shape)

    in_specs = [pl.BlockSpec((tt, tk), lambda k, n, t: (t, k)), pl.BlockSpec((tt, tn), lambda k, n, t: (t, n))]
    if into is None:
        return _pcall(body, name=name, grid=(K // tk, N // tn, steps), in_specs=in_specs,
                      out_specs=pl.BlockSpec((tk, tn), lambda k, n, t: (k, n)), out_shape=SDS((K, N), F32),
                      scratch=[pltpu.VMEM((tk, tn), F32)], sem=("parallel", "parallel", "arbitrary"))(a, g)
    assert row0 % shard[0] == 0 and (K // tk) * (N // tn) * per_step == N_CHIPS
    return _pcall(body, name=name, grid=(K // tk, N // tn, steps), in_specs=in_specs + [ANY],
                  out_specs=pl.BlockSpec((per_step, shard[0], tn), lambda k, n, t: (k + n, row0 // shard[0], 0)),
                  out_shape=SDS(buf.shape, buf.dtype),
                  scratch=[pltpu.VMEM((tk, tn), F32)], sem=("parallel", "parallel", "arbitrary"), aliases={2: 0})(a, g, buf)


def _wgrad_t(at, g, name):
    nblk, K, tt = at.shape
    N = g.shape[1]
    tk = min(K, 1024)
    per_step = 4 if nblk % 4 == 0 else 1
    assert K % tk == 0 and g.shape[0] == nblk * tt

    def body(a_ref, g_ref, o_ref):
        @pl.when(pl.program_id(1) == 0)
        def _():
            o_ref[...] = jnp.zeros_like(o_ref)

        acc = _dot(a_ref[0], g_ref[0:tt, :].astype(BF16))
        for b in range(1, per_step):
            acc = acc + _dot(a_ref[b], g_ref[tt * b:tt * (b + 1), :].astype(BF16))
        o_ref[...] += acc

    return _pcall(body, name=name, grid=(K // tk, nblk // per_step),
                  in_specs=[pl.BlockSpec((per_step, tk, tt), lambda k, t: (t, k, 0)),
                            pl.BlockSpec((per_step * tt, N), lambda k, t: (t, 0))],
                  out_specs=pl.BlockSpec((tk, N), lambda k, t: (k, 0)), out_shape=SDS((K, N), F32),
                  sem=("parallel", "arbitrary"))(at, g)


def _adamw(w, packed_g, m, v, name):
    _, R, C = w.shape
    _, row0, lane0 = PACK_AT[name]
    tr = min(R, 256 if row0 % 256 == 0 else 128)
    assert row0 % tr == 0 and R % tr == 0

    def body(w_ref, g_ref, m_ref, v_ref, go_ref, d_ref, m2_ref, v2_ref):
        g_ = g_ref[:, lane0:lane0 + C]
        go_ref[0] = g_
        m2 = ADAM_B1 * m_ref[0] + (1.0 - ADAM_B1) * g_
        v2 = ADAM_B2 * v_ref[0] + (1.0 - ADAM_B2) * jnp.square(g_)
        m_hat = m2 / (1.0 - ADAM_B1 ** ADAM_STEP)
        v_hat = v2 / (1.0 - ADAM_B2 ** ADAM_STEP)
        d_ref[0] = -ADAM_LR * (m_hat / (jnp.sqrt(v_hat) + ADAM_EPS) + ADAM_WD * w_ref[0])
        m2_ref[0] = m2
        v2_ref[0] = v2

    r = pl.BlockSpec((1, tr, C), lambda i: (0, i, 0))
    return _pcall(body, name="adamw_" + name, grid=(R // tr,),
                  in_specs=[r, pl.BlockSpec((tr, D), lambda i: (row0 // tr + i, 0)), r, r], out_specs=[r] * 4,
                  out_shape=[SDS((1, R, C), F32)] * 4, sem=("parallel",))(w, packed_g, m, v)


def _adamw_small(w, parts, m, v):
    def body(w_ref, p_ref, m_ref, v_ref, g_ref, d_ref, m2_ref, v2_ref):
        g_ = p_ref[0]
        for k in range(1, N_DEV):
            g_ = g_ + p_ref[k]
        g_ref[...] = g_
        m2 = ADAM_B1 * m_ref[...] + (1.0 - ADAM_B1) * g_
        v2 = ADAM_B2 * v_ref[...] + (1.0 - ADAM_B2) * jnp.square(g_)
        m_hat = m2 / (1.0 - ADAM_B1 ** ADAM_STEP)
        v_hat = v2 / (1.0 - ADAM_B2 ** ADAM_STEP)
        d_ref[...] = -ADAM_LR * (m_hat / (jnp.sqrt(v_hat) + ADAM_EPS) + ADAM_WD * w_ref[...])
        m2_ref[...] = m2
        v2_ref[...] = v2

    s = _full((8, D))
    return _pcall(body, name="adamw_small", grid=(1,), in_specs=[s, _full((N_DEV, 8, D)), s, s], out_specs=[s] * 4,
                  out_shape=[SDS((8, D), F32)] * 4, sem=("arbitrary",))(w, parts, m, v)


ANY = pl.BlockSpec(memory_space=pl.ANY)


def _place():
    x, y, c = lax.axis_index("x"), lax.axis_index("y"), lax.axis_index("c")
    chips = [(1 - x, y), (x, 1 - y), (1 - x, 1 - y)]
    return x, y, c, chips


def _all_gather(wpk):
    rows = wpk.shape[0]
    HALF = rows // 2
    assert HALF % 16 == 0

    def body(in_ref, out_ref, send_sems, recv_sems):
        x, y, c, chips = _place()
        half = pl.ds(pl.multiple_of(c * HALF, 16), HALF)
        other = pl.ds(pl.multiple_of((1 - c) * HALF, 16), HALF)

        def copy(k, src, dst, to):
            return pltpu.make_async_remote_copy(src_ref=src, dst_ref=dst, send_sem=send_sems.at[k], recv_sem=recv_sems.at[k],
                                                device_id=to, device_id_type=MESH)

        first = [copy(k, in_ref.at[half], out_ref.at[2 * x + y, half], (cx, cy, c)) for k, (cx, cy) in enumerate(chips)]
        for cp in first:
            cp.start()
        passed = []
        for k, (cx, cy) in enumerate(chips):
            slot = out_ref.at[2 * cx + cy, half]
            copy(k, slot, slot, (x, y, c)).wait_recv()
            fwd = copy(3 + k, slot, slot, (x, y, 1 - c))
            fwd.start()
            passed.append(fwd)
        for k, (cx, cy) in enumerate(chips):
            slot = out_ref.at[2 * cx + cy, other]
            copy(3 + k, slot, slot, (x, y, c)).wait_recv()
        for cp in first + passed:
            cp.wait_send()

    return _pcall(body, name="all_gather_weights", in_specs=[ANY], out_specs=ANY,
                  out_shape=SDS((N_CHIPS, rows, D), BF16),
                  scratch=[pltpu.SemaphoreType.DMA((6,)), pltpu.SemaphoreType.DMA((6,))])(wpk)


HBM = pl.BlockSpec(memory_space=pltpu.HBM)
SEM = pl.BlockSpec(memory_space=pltpu.SEMAPHORE)
DATAFLOW = pltpu.SideEffectType.DATAFLOW_SIDE_EFFECTING


def _in_hbm(a):
    return pltpu.with_memory_space_constraint(a, pltpu.HBM)


def _gather_late_start(wpk, after):
    rows = wpk.shape[0]

    def body(in_ref, land_ref, after_ref, send_sems, recv_sems, in_thru, land_thru, token):
        x, y, c, chips = _place()
        for k, (cx, cy) in enumerate(chips):
            pltpu.make_async_remote_copy(src_ref=in_ref, dst_ref=land_ref.at[2 * x + y], send_sem=send_sems.at[k],
                                         recv_sem=recv_sems.at[k], device_id=(cx, cy, c), device_id_type=MESH).start()
        token[...] = jnp.zeros_like(token)

    return pl.pallas_call(
        body, name="gather_late_start",
        out_shape=(pltpu.SemaphoreType.DMA((3,)), pltpu.SemaphoreType.DMA((3,)), pltpu.HBM(wpk.shape, wpk.dtype),
                   pltpu.HBM((N_CHIPS, rows, D), wpk.dtype), SDS((8, LANES), F32)),
        in_specs=(HBM, HBM, ANY), out_specs=(SEM, SEM, HBM, HBM, pl.BlockSpec(memory_space=pltpu.VMEM)),
        input_output_aliases={0: 2, 1: 3}, compiler_params=pltpu.CompilerParams(has_side_effects=DATAFLOW),
    )(_in_hbm(wpk), _in_hbm(lax.empty((N_CHIPS, rows, D), wpk.dtype)), after)


def _gather_late_wait(send_sems, recv_sems, in_thru, land_thru, after):
    def body(in_ref, land_ref, send_sems, recv_sems, after_ref, after2_ref, in_dead, got_ref):
        x, y, c, chips = _place()
        for k, (cx, cy) in enumerate(chips):
            cp = pltpu.make_async_remote_copy(src_ref=in_ref, dst_ref=land_ref.at[2 * cx + cy], send_sem=send_sems.at[k],
                                              recv_sem=recv_sems.at[k], device_id=(cx, cy, c), device_id_type=MESH)
            cp.wait_send()
            cp.wait_recv()

    return pl.pallas_call(
        body, name="gather_late_wait",
        out_shape=(pltpu.HBM(in_thru.shape, in_thru.dtype), pltpu.HBM(land_thru.shape, land_thru.dtype)),
        in_specs=(HBM, HBM, SEM, SEM, ANY, ANY), out_specs=(HBM, HBM), input_output_aliases={0: 0, 1: 1},
        compiler_params=pltpu.CompilerParams(has_side_effects=DATAFLOW),
    )(in_thru, land_thru, send_sems, recv_sems, *after)[1]


def _rs_sibling(gpk):
    HALF = gpk.shape[1] // 2

    def body(in_ref, out_ref, send_sem, recv_sem):
        x, y, c, _ = _place()
        theirs = pl.ds(pl.multiple_of((1 - c) * HALF, 8), HALF)
        cp = pltpu.make_async_remote_copy(src_ref=in_ref.at[:, theirs], dst_ref=out_ref, send_sem=send_sem, recv_sem=recv_sem,
                                          device_id=(x, y, 1 - c), device_id_type=MESH)
        cp.start()
        cp.wait()

    return _pcall(body, name="rs_sibling", in_specs=[ANY], out_specs=ANY, out_shape=SDS((N_CHIPS, HALF, D), F32),
                  scratch=[pltpu.SemaphoreType.DMA, pltpu.SemaphoreType.DMA])(gpk)


def _rs_add_sibling(cidx, gpk, got):
    HALF = got.shape[1]
    th = HALF // 4
    nh = HALF // th
    assert th % 16 == 0

    def body(c_ref, a_ref, b_ref, o_ref):
        o_ref[...] = (a_ref[...] + b_ref[...]).astype(BF16)

    gs = pltpu.PrefetchScalarGridSpec(
        num_scalar_prefetch=1, grid=(N_CHIPS, nh),
        in_specs=[pl.BlockSpec((1, th, D), lambda j, i, c: (j, c[0] * nh + i, 0)), pl.BlockSpec((1, th, D), lambda j, i, c: (j, i, 0))],
        out_specs=pl.BlockSpec((1, th, D), lambda j, i, c: (j, i, 0)))
    return pl.pallas_call(body, name="rs_add_sibling", grid_spec=gs, out_shape=SDS((N_CHIPS, HALF, D), BF16),
                          compiler_params=pltpu.CompilerParams(dimension_semantics=("parallel", "parallel"),
                                                               vmem_limit_bytes=48 << 20))(cidx, gpk, got)


def _rs_chips_start(part, small, after):
    def body(p_ref, s_ref, land_ref, sland_ref, after_ref, send_sems, recv_sems, p_thru, s_thru, land_thru, sland_thru, token):
        x, y, c, chips = _place()
        for k, (cx, cy) in enumerate(chips):
            pltpu.make_async_remote_copy(src_ref=p_ref.at[2 * cx + cy], dst_ref=land_ref.at[2 * x + y], send_sem=send_sems.at[k],
                                         recv_sem=recv_sems.at[k], device_id=(cx, cy, c), device_id_type=MESH).start()
        peers = [(x, y, 1 - c)] + [(cx, cy, c) for cx, cy in chips] + [(cx, cy, 1 - c) for cx, cy in chips]
        for k, to in enumerate(peers):
            pltpu.make_async_remote_copy(src_ref=s_ref, dst_ref=sland_ref.at[4 * x + 2 * y + c], send_sem=send_sems.at[3 + k],
                                         recv_sem=recv_sems.at[3 + k], device_id=to, device_id_type=MESH).start()
        token[...] = jnp.zeros_like(token)

    return pl.pallas_call(
        body, name="rs_chips_start",
        out_shape=(pltpu.SemaphoreType.DMA((10,)), pltpu.SemaphoreType.DMA((10,)), pltpu.HBM(part.shape, part.dtype),
                   pltpu.HBM(small.shape, small.dtype), pltpu.HBM(part.shape, part.dtype), pltpu.HBM((N_DEV, 8, D), F32),
                   SDS((8, LANES), F32)),
        in_specs=(HBM, HBM, HBM, HBM, ANY), out_specs=(SEM, SEM, HBM, HBM, HBM, HBM, pl.BlockSpec(memory_space=pltpu.VMEM)),
        input_output_aliases={0: 2, 1: 3, 2: 4, 3: 5}, compiler_params=pltpu.CompilerParams(has_side_effects=DATAFLOW),
    )(_in_hbm(part), _in_hbm(small), _in_hbm(lax.empty(part.shape, part.dtype)), _in_hbm(lax.empty((N_DEV, 8, D), F32)), after)


def _rs_chips_wait(send_sems, recv_sems, p_thru, s_thru, land_thru, sland_thru, after):
    def body(p_ref, s_ref, land_ref, sland_ref, send_sems, recv_sems, *after_and_outputs):
        x, y, c, chips = _place()
        for k, (cx, cy) in enumerate(chips):
            cp = pltpu.make_async_remote_copy(src_ref=p_ref.at[0], dst_ref=land_ref.at[2 * cx + cy], send_sem=send_sems.at[k],
                                              recv_sem=recv_sems.at[k], device_id=(cx, cy, c), device_id_type=MESH)
            cp.wait_send()
            cp.wait_recv()
        peers = [(x, y, 1 - c)] + [(cx, cy, c) for cx, cy in chips] + [(cx, cy, 1 - c) for cx, cy in chips]
        for k, (px, py, pc) in enumerate(peers):
            cp = pltpu.make_async_remote_copy(src_ref=s_ref, dst_ref=sland_ref.at[4 * px + 2 * py + pc], send_sem=send_sems.at[3 + k],
                                              recv_sem=recv_sems.at[3 + k], device_id=(px, py, pc), device_id_type=MESH)
            cp.wait_send()
            cp.wait_recv()

    hbm = lambda a: pltpu.HBM(a.shape, a.dtype)
    outs = pl.pallas_call(
        body, name="rs_chips_wait", out_shape=(hbm(p_thru), hbm(s_thru), hbm(land_thru), hbm(sland_thru)),
        in_specs=(HBM, HBM, HBM, HBM, SEM, SEM) + (ANY,) * len(after), out_specs=(HBM, HBM, HBM, HBM),
        input_output_aliases={0: 0, 1: 1, 2: 2, 3: 3}, compiler_params=pltpu.CompilerParams(has_side_effects=DATAFLOW),
    )(p_thru, s_thru, land_thru, sland_thru, send_sems, recv_sems, *after)
    return outs[0], outs[2], outs[3]


def _rs_add_chips(qidx, part, parts):
    HALF = part.shape[1]
    th = HALF // 4
    assert th % 16 == 0

    def body(q_ref, own_ref, p_ref, o_ref):
        for me in range(N_CHIPS):
            @pl.when(q_ref[0] == me)
            def _(me=me):
                t = [(own_ref[0] if j == me else p_ref[j]).astype(F32) for j in range(N_CHIPS)]
                o_ref[...] = ((t[0] + t[1]) + t[2]) + t[3]

    gs = pltpu.PrefetchScalarGridSpec(
        num_scalar_prefetch=1, grid=(HALF // th,),
        in_specs=[pl.BlockSpec((1, th, D), lambda i, q: (q[0], i, 0)), pl.BlockSpec((N_CHIPS, th, D), lambda i, q: (0, i, 0))],
        out_specs=pl.BlockSpec((th, D), lambda i, q: (i, 0)))
    return pl.pallas_call(body, name="rs_add_chips", grid_spec=gs, out_shape=SDS((HALF, D), F32),
                          compiler_params=pltpu.CompilerParams(dimension_semantics=("parallel",),
                                                               vmem_limit_bytes=48 << 20))(qidx, part, parts)


def _rs_join(mine, core, name):
    def body(in_ref, out_ref, send_sem, recv_sem):
        x, y, c, _ = _place()
        cp = pltpu.make_async_remote_copy(src_ref=in_ref, dst_ref=out_ref, send_sem=send_sem, recv_sem=recv_sem,
                                          device_id=(x, y, 1 - c), device_id_type=MESH)
        cp.start()
        cp.wait()

    theirs = _pcall(body, name=name, in_specs=[ANY], out_specs=ANY, out_shape=SDS(mine.shape, F32),
                    scratch=[pltpu.SemaphoreType.DMA, pltpu.SemaphoreType.DMA])(mine)
    return jnp.where(core == 0, jnp.concatenate([mine, theirs]), jnp.concatenate([theirs, mine]))


def _reduce_late_start(gpk, after):
    rows = gpk.shape[1]
    HALF = rows // 2
    assert HALF % 16 == 0

    def body(in_ref, land_ref, after_ref, send_sems, recv_sems, in_thru, land_thru, token):
        x, y, c, chips = _place()
        me = 4 * x + 2 * y + c
        peers = [(x, y, 1 - c)] + [(cx, cy, c) for cx, cy in chips] + [(cx, cy, 1 - c) for cx, cy in chips]
        for k, (px, py, pc) in enumerate(peers):
            src = in_ref.at[2 * px + py, pl.ds(pl.multiple_of(pc * HALF, 16), HALF)]
            pltpu.make_async_remote_copy(src_ref=src, dst_ref=land_ref.at[me], send_sem=send_sems.at[k], recv_sem=recv_sems.at[k],
                                         device_id=(px, py, pc), device_id_type=MESH).start()
        token[...] = jnp.zeros_like(token)

    return pl.pallas_call(
        body, name="reduce_late_start",
        out_shape=(pltpu.SemaphoreType.DMA((7,)), pltpu.SemaphoreType.DMA((7,)), pltpu.HBM(gpk.shape, gpk.dtype),
                   pltpu.HBM((N_DEV, HALF, D), gpk.dtype), SDS((8, LANES), F32)),
        in_specs=(HBM, HBM, ANY), out_specs=(SEM, SEM, HBM, HBM, pl.BlockSpec(memory_space=pltpu.VMEM)),
        input_output_aliases={0: 2, 1: 3}, compiler_params=pltpu.CompilerParams(has_side_effects=DATAFLOW),
    )(_in_hbm(gpk), _in_hbm(lax.empty((N_DEV, HALF, D), gpk.dtype)), after)


def _reduce_late_wait(send_sems, recv_sems, in_thru, land_thru, after):
    def body(in_ref, land_ref, send_sems, recv_sems, after_ref, in_out, got_ref):
        x, y, c, chips = _place()
        peers = [(x, y, 1 - c)] + [(cx, cy, c) for cx, cy in chips] + [(cx, cy, 1 - c) for cx, cy in chips]
        for k, (px, py, pc) in enumerate(peers):
            cp = pltpu.make_async_remote_copy(src_ref=land_ref.at[0], dst_ref=land_ref.at[4 * px + 2 * py + pc],
                                              send_sem=send_sems.at[k], recv_sem=recv_sems.at[k],
                                              device_id=(px, py, pc), device_id_type=MESH)
            cp.wait_send()
            cp.wait_recv()

    return pl.pallas_call(
        body, name="reduce_late_wait",
        out_shape=(pltpu.HBM(in_thru.shape, in_thru.dtype), pltpu.HBM(land_thru.shape, land_thru.dtype)),
        in_specs=(HBM, HBM, SEM, SEM, ANY), out_specs=(HBM, HBM), input_output_aliases={0: 0, 1: 1},
        compiler_params=pltpu.CompilerParams(has_side_effects=DATAFLOW),
    )(in_thru, land_thru, send_sems, recv_sems, after)


def _reduce_late_add(didx, gpk, parts):
    HALF = parts.shape[1]
    th = HALF // 4
    nh = HALF // th
    assert th % 16 == 0

    def body(d_ref, own_ref, p_ref, o_ref):
        for me in range(N_DEV):
            @pl.when(d_ref[0] == me)
            def _(me=me):
                t = [(own_ref[0] if j == me else p_ref[j]).astype(F32) for j in range(N_DEV)]
                o_ref[...] = ((((((t[0] + t[1]) + t[2]) + t[3]) + t[4]) + t[5]) + t[6]) + t[7]

    gs = pltpu.PrefetchScalarGridSpec(
        num_scalar_prefetch=1, grid=(nh,),
        in_specs=[pl.BlockSpec((1, th, D), lambda i, d: (d[1], d[2] * nh + i, 0)), pl.BlockSpec((N_DEV, th, D), lambda i, d: (0, i, 0))],
        out_specs=pl.BlockSpec((th, D), lambda i, d: (i, 0)))
    return pl.pallas_call(body, name="reduce_late_add", grid_spec=gs, out_shape=SDS((HALF, D), F32),
                          compiler_params=pltpu.CompilerParams(dimension_semantics=("parallel",),
                                                               vmem_limit_bytes=48 << 20))(didx, gpk, parts)


def _pack_early(b, dtype):
    lanes = lambda a: jnp.pad(a.astype(dtype), ((0, 0), (0, D - a.shape[1])))
    pair = jnp.concatenate([b["w_q_b"].astype(dtype), b["w_ple"].astype(dtype), jnp.zeros((256, D - 640), dtype)], axis=1)
    return jnp.concatenate([lanes(b["w_in"]), pair, lanes(b["w_kv_b"])], axis=0)


def _pack_late(b, dtype):
    return jnp.concatenate([b[n].astype(dtype) for n in ("w_mla_up", "w_swa_up", "w_out", "w_ple_gate", "w_mlp_up", "w_mlp_down")],
                           axis=0)


def _unpack_shards(pk, which):
    return {n: pk[PACK_AT[n][1]:PACK_AT[n][1] + r, PACK_AT[n][2]:PACK_AT[n][2] + c] for n, r, c in BIG if PACK_AT[n][0] == which}


def _full_weights(gathered, own, chip, which):
    own_b = _unpack_shards(own, which)
    per_chip = [{n: jnp.where(chip == j, own_b[n], blk) for n, blk in _unpack_shards(gathered[j], which).items()}
                for j in range(N_CHIPS)]
    out = {}
    for n in own_b:
        shards = [pc[n] for pc in per_chip]
        if n == "w_in":
            out["w_in_p"] = _w_in_internal(shards)
        else:
            out[n] = jnp.concatenate(shards, axis=1 if n in COL_SHARDED else 0)
    return out


def _split_full_grads(grads, pack, dtype):
    shard = {n: (r, c) for n, r, c in BIG}
    chunks = []
    for j in range(N_CHIPS):
        blocks = {}
        for n, g in grads.items():
            if n == "w_in_p":
                blocks["w_in"] = _w_in_grad_shard(g, j)
                continue
            r, c = shard[n]
            blocks[n] = g[:, j * c:(j + 1) * c] if n in COL_SHARDED else g[j * r:(j + 1) * r]
        chunks.append(pack(blocks, dtype))
    return jnp.stack(chunks)


W_IN_SHARD = 936
W_IN_SEGMENTS = ((0, 256, (3072,)), (256, 384, (3840,)), (384, 416, (4032,)), (416, 1440, (0,)), (1440, 1504, (3328, 3392)),
                 (1504, 1568, (3456, 3520)), (1568, 1632, (3584, 3648)), (1632, 1696, (3712, 3776)), (1696, 3744, (1024,)))


def _w_in_internal(shards):
    def cols(a, b):
        out = []
        for j, s in enumerate(shards):
            lo, hi = max(a, W_IN_SHARD * j), min(b, W_IN_SHARD * (j + 1))
            if lo < hi:
                out.append(s[:, lo - W_IN_SHARD * j:hi - W_IN_SHARD * j])
        return out

    pieces = {}
    for a, b, places in W_IN_SEGMENTS:
        for at in places:
            pieces[at] = cols(a, b)
    zeros = lambda n: [jnp.zeros((D, n), shards[0].dtype)]
    pieces[3968] = zeros(64)
    pieces[4064] = zeros(32)
    return jnp.concatenate([piece for at in sorted(pieces) for piece in pieces[at]], axis=1)


def _w_in_grad_shard(g, j):
    def internal(a, b):
        out = []
        while a < b:
            end = min(b, (a // D + 1) * D)
            out.append(g[a // D][:, a % D:a % D + end - a])
            a = end
        return out

    out = []
    for a, b, places in W_IN_SEGMENTS:
        lo, hi = max(a, W_IN_SHARD * j), min(b, W_IN_SHARD * (j + 1))
        if lo < hi:
            parts = [internal(at + lo - a, at + hi - a) for at in places]
            if len(parts) == 1:
                out += parts[0]
            else:
                assert len(parts[0]) == len(parts[1]) == 1
                out.append(parts[0][0] + parts[1][0])
    return jnp.concatenate(out, axis=1)


def _local_step(x, p, tgt, w, small, late_weights, late_grads_out):
    T = x.shape[0]
    tm = 256
    tb = 256
    w_in_p = w["w_in_p"]
    wqb = jnp.pad(w["w_q_b"].reshape(Q_LORA, MLA_HEADS, 96), ((0, 0), (0, 0), (0, 32))).reshape(Q_LORA, 2048)
    wkv = w["w_kv_b"].reshape(KV_LORA, MLA_HEADS, 128)
    wkn = jnp.pad(wkv[:, :, :64], ((0, 0), (0, 0), (0, 64))).reshape(KV_LORA, 2048)
    wv = wkv[:, :, 64:].reshape(KV_LORA, 1024)
    tab_m = _rope_tables(T, "mla")
    tab_s = _rope_tables(T, "swa")
    g1, gq, gkv, sinks = small["g_mix_pre"], small["g_q_a"], small["g_kv_a"], small["sinks"]
    g2, g3, g4, g5 = small["g_mix_post"], small["g_mlp_pre"], small["g_mlp_post"], small["g_ple"]
    sink_vec = sinks.reshape(SWA_HEADS)

    z, h1 = _fwd_in(x, g1, w_in_p, tm)
    qn, kvn, km, vm, qt, kt, vt, qs, ks, vs = _fwd_qkv(z, gq, gkv, wqb, wkn, wv, tab_m, tab_s, tb)
    om, lse_m = _mla_fwd(qt, km, vt, tb)
    os_, lse_s = _swa_fwd(sink_vec, qs, ks, vs)
    w = {**w, **late_weights((om, os_))}
    y, yo, au, bu, x1 = _fwd_mix(om, os_, z, x, w["w_mla_up"], w["w_swa_up"], w["w_out"], g2, tm)
    h2, u = _fwd_mlp_up(x1, g3, w["w_mlp_up"], tm)
    d, x2 = _fwd_mlp_down(u, w["w_mlp_down"], x1, g4, tm)
    loss, dx2, dgt, de0, dg5 = _ple_fwd_bwd(p, x2, tgt, w["w_ple"], g5, w["w_ple_gate"], tm)

    dd, da, dg4 = _bwd_mlp_down(dx2, d, g4, w["w_mlp_down"], u, tm)
    dx1, dg3 = _bwd_mlp_up(da, w["w_mlp_up"], x1, g3, dx2, tm)
    dyo, dg2, dau, dbu, dga, dgb, dos, delta_m, dom_t = _bwd_mix(dx1, yo, g2, w["w_out"], z, au, bu, w["w_mla_up"],
                                                                w["w_swa_up"], om, tb)
    gpk_late = lax.empty((N_CHIPS, PACK_ROWS["late"], D), BF16)
    for weight, a_, g_ in (("w_mla_up", om, dau), ("w_swa_up", os_, dbu), ("w_out", y, dyo), ("w_ple_gate", x2, dgt),
                           ("w_mlp_up", h2, da), ("w_mlp_down", u, dd)):
        gpk_late = _wgrad(a_, g_, "wgrad_" + weight[2:], into=(gpk_late, weight))
    token = late_grads_out(gpk_late)
    delta_m = delta_m + token[0, 0]
    dqm, dkm, dvm = _mla_bwd(qt, km, kt, vm, dom_t, lse_m, delta_m, tb)
    dqs, dkc, dkp, dvc, dvp, dsink = _swa_bwd(sink_vec, qs, ks, vs, dos, os_, lse_s)
    dqb, dknb, dvb, dsq, drest, dgq, dgkv = _bwd_qkv(dqm, dkm, dvm, dqs, dkc, dkp, dvc, dvp, z, gq, gkv, wqb, wkn, wv,
                                                      tab_m, tab_s)
    gx, dg1 = _bwd_in(dsq, dga, dgb, drest, w_in_p, x, g1, dx1, tm)

    g_in_p = [_wgrad(h1, dsq, "wgrad_in_sq"), _wgrad(h1, dga, "wgrad_in_ga"), _wgrad(h1, dgb, "wgrad_in_gb"),
              _wgrad(h1, drest, "wgrad_in_rest")]
    g_qb_p = _wgrad_t(dqb, qn, "wgrad_q_b").T
    g_kn_p = _wgrad_t(dknb, kvn, "wgrad_kv_b_nope").T
    g_v_p = _wgrad_t(dvb, kvn, "wgrad_kv_b_v").T
    grads = {
        "w_in_p": g_in_p,
        "w_q_b": g_qb_p.reshape(Q_LORA, MLA_HEADS, 128)[:, :, :96].reshape(Q_LORA, 1536),
        "w_kv_b": jnp.concatenate([g_kn_p.reshape(KV_LORA, MLA_HEADS, 128)[:, :, :64], g_v_p.reshape(KV_LORA, MLA_HEADS, 64)],
                                  axis=2).reshape(KV_LORA, 2048),
        "w_ple": _wgrad(p, de0, "wgrad_ple"),
    }
    small_grads = {"g_mix_pre": dg1, "g_q_a": dgq.reshape(1, Q_LORA), "g_kv_a": dgkv.reshape(1, KV_LORA), "sinks": dsink[0:1, 0:SWA_HEADS], "g_mix_post": dg2,
                   "g_mlp_pre": dg3, "g_mlp_post": dg4, "g_ple": dg5}
    return loss, gx, grads, small_grads


def _pack_small(vals, fill, scalar=None):
    wide = [vals[n] for n, k in SMALL if k == D]
    narrow = [vals[n] for n, k in SMALL if k != D]
    used = sum(k for _, k in SMALL if k != D)
    last = jnp.concatenate(narrow + [jnp.full((1, D - used), fill, F32)], axis=1)
    rest = jnp.full((2, D), fill, F32)
    if scalar is not None:
        rest = jnp.concatenate([jnp.concatenate([scalar, rest[0:1, 1:]], axis=1), rest[1:2]], axis=0)
    return jnp.concatenate(wide + [last, rest], axis=0)


def _unpack_small(pk):
    out, row, off = {}, 0, 0
    for n, k in SMALL:
        if k == D:
            out[n] = pk[row:row + 1]
            row += 1
    for n, k in SMALL:
        if k != D:
            out[n] = pk[5:6, off:off + k]
            off += k
    return out


def kernel(x, p, g_mix_pre, w_in, g_q_a, w_q_b, g_kv_a, w_kv_b, sinks, w_mla_up, w_swa_up, w_out, g_mix_post, g_mlp_pre, w_mlp_up, w_mlp_down, g_mlp_post, w_ple, g_ple, w_ple_gate, loss_target, m_g_mix_pre, m_w_in, m_g_q_a, m_w_q_b, m_g_kv_a, m_w_kv_b, m_sinks, m_w_mla_up, m_w_swa_up, m_w_out, m_g_mix_post, m_g_mlp_pre, m_w_mlp_up, m_w_mlp_down, m_g_mlp_post, m_w_ple, m_g_ple, m_w_ple_gate, v_g_mix_pre, v_w_in, v_g_q_a, v_w_q_b, v_g_kv_a, v_w_kv_b, v_sinks, v_w_mla_up, v_w_swa_up, v_w_out, v_g_mix_post, v_g_mlp_pre, v_w_mlp_up, v_w_mlp_down, v_g_mlp_post, v_w_ple, v_g_ple, v_w_ple_gate):
    given = dict(locals())
    big_w = {n: given[n][0] for n, _, _ in BIG}
    small_w = {n: given[n] for n, _ in SMALL}
    small_m = {n: given["m_" + n] for n, _ in SMALL}
    small_v = {n: given["v_" + n] for n, _ in SMALL}

    core = lax.axis_index("c")
    chip = 2 * lax.axis_index("x") + lax.axis_index("y")
    core_i = core.astype(jnp.int32).reshape(1)
    chip_i = chip.astype(jnp.int32).reshape(1)
    dev_i = jnp.stack([2 * chip + core, chip, core]).astype(jnp.int32)

    own_early = _pack_early(big_w, BF16)
    own_late = _pack_late(big_w, BF16)
    got_early = _all_gather(own_early)
    late_flight = _gather_late_start(own_late, got_early)
    weights = _full_weights(got_early, own_early, chip, "early")
    step_small = {**small_w, "g_mix_pre": small_w["g_mix_pre"] + late_flight[4][0, 0]}

    def late_weights(after):
        return _full_weights(_gather_late_wait(*late_flight[:4], after), own_late, chip, "late")

    flight = {}

    def late_grads_out(gpk_late):
        flight["late"] = _reduce_late_start(gpk_late, dev_i)
        return flight["late"][4]

    loss_blk, gx, grads, small_grads = _local_step(x[0], p[0, 0], loss_target[0], weights, step_small, late_weights,
                                                   late_grads_out)

    gpk = _split_full_grads(grads, _pack_early, F32)
    got = _rs_sibling(gpk)
    part = _rs_add_sibling(core_i, gpk, got)
    small_own = _pack_small(small_grads, 0.0, loss_blk[0:1, 0:1])
    early_flight = _rs_chips_start(part, small_own, dev_i)

    out_g, out_d, out_m, out_v = {}, {}, {}, {}
    gpk_late, parts_late = _reduce_late_wait(*flight["late"][:4], early_flight[6])
    joined_late = _rs_join(_reduce_late_add(dev_i, gpk_late, parts_late), core, "rs_join_late")
    for n, _, _ in BIG:
        if PACK_AT[n][0] == "late":
            out_g[n], out_d[n], out_m[n], out_v[n] = _adamw(given[n], joined_late, given["m_" + n], given["v_" + n], n)

    part, parts, small_parts = _rs_chips_wait(*early_flight[:6], [out_d[n] for n in out_d])
    joined_early = _rs_join(_rs_add_chips(chip_i, part, parts), core, "rs_join_early")
    for n, _, _ in BIG:
        if PACK_AT[n][0] == "early":
            out_g[n], out_d[n], out_m[n], out_v[n] = _adamw(given[n], joined_early, given["m_" + n], given["v_" + n], n)

    mine = (lax.broadcasted_iota(jnp.int32, (N_DEV, 1, 1), 0) == dev_i[0])
    g_small_pk, d_small_pk, m_small_pk, v_small_pk = _adamw_small(
        _pack_small(small_w, 0.0), jnp.where(mine, small_own[None], small_parts), _pack_small(small_m, 0.0),
        _pack_small(small_v, 1.0))
    loss = g_small_pk[6, 0]
    for out, pk in ((out_g, g_small_pk), (out_d, d_small_pk), (out_m, m_small_pk), (out_v, v_small_pk)):
        out.update(_unpack_small(pk))
    order = ["g_mix_pre", "w_in", "g_q_a", "w_q_b", "g_kv_a", "w_kv_b", "sinks", "w_mla_up", "w_swa_up", "w_out", "g_mix_post",
             "g_mlp_pre", "w_mlp_up", "w_mlp_down", "g_mlp_post", "w_ple", "g_ple", "w_ple_gate"]
    return (loss, gx[None], *[out_g[n] for n in order], *[out_d[n] for n in order], *[out_m[n] for n in order],
            *[out_v[n] for n in order])
```

```python
import math

import jax
import jax.numpy as jnp
from jax import lax
from jax.experimental import pallas as pl
from jax.experimental.pallas import tpu as pltpu

F32 = jnp.float32
BF16 = jnp.bfloat16
SDS = jax.ShapeDtypeStruct

D = 1024
D_FF = 4096
PLE = 256
Q_LORA = 256
KV_LORA = 128
MLA_HEADS = 16
MLA_NOPE = 64
MLA_ROPE = 32
SWA_HEADS = 16
SWA_HD = 64
WINDOW = 128
ROPE_THETA = 10000.0
EPS = 1e-6
NEG = -1e30
NZ = 4096
MLA_SCALE = (MLA_NOPE + MLA_ROPE) ** -0.5
LOG2_E = math.log2(math.e)
MLA_LOG2_SCALE = MLA_SCALE * LOG2_E
SWA_SCALE = SWA_HD ** -0.5

ADAM_LR = 0.001
ADAM_B1 = 0.9
ADAM_B2 = 0.999
ADAM_EPS = 1e-08
ADAM_WD = 0.01
ADAM_STEP = 10

LANES = 128
ATT_COLS = 128
VT_ROWS = 80
N_CHIPS = 4
N_DEV = 8
MESH = pl.DeviceIdType.MESH

NT = (((1,), (1,)), ((), ()))
TN = (((0,), (0,)), ((), ()))

BIG = (("w_in", 1024, 936), ("w_q_b", 256, 384), ("w_kv_b", 128, 512), ("w_mla_up", 256, 1024),
       ("w_swa_up", 256, 1024), ("w_out", 256, 1024), ("w_mlp_up", 1024, 1024), ("w_mlp_down", 1024, 1024),
       ("w_ple", 256, 256), ("w_ple_gate", 256, 1024))
COL_SHARDED = ("w_in", "w_q_b", "w_kv_b", "w_mlp_up", "w_ple")
PACK_AT = {"w_in": ("early", 0, 0), "w_q_b": ("early", 1024, 0), "w_ple": ("early", 1024, 384), "w_kv_b": ("early", 1280, 0),
           "w_mla_up": ("late", 0, 0), "w_swa_up": ("late", 256, 0), "w_out": ("late", 512, 0), "w_ple_gate": ("late", 768, 0),
           "w_mlp_up": ("late", 1024, 0), "w_mlp_down": ("late", 2048, 0)}
PACK_ROWS = {"early": 1408, "late": 3072}
SMALL = (("g_mix_pre", 1024), ("g_q_a", 256), ("g_kv_a", 128), ("sinks", 16), ("g_mix_post", 1024),
         ("g_mlp_pre", 1024), ("g_mlp_post", 1024), ("g_ple", 1024))


def _dot(a, b):
    return jnp.dot(a, b, preferred_element_type=F32)


def _dot_nt(a, b):
    return lax.dot_general(a, b, NT, preferred_element_type=F32)


def _dot_tn(a, b):
    return lax.dot_general(a, b, TN, preferred_element_type=F32)


def _pcall(body, *, name, out_shape, grid=(), in_specs=None, out_specs=None, scratch=(), sem=None, vmem_mb=48, aliases=None):
    params = dict(vmem_limit_bytes=vmem_mb << 20)
    if sem is not None:
        params["dimension_semantics"] = sem
    return pl.pallas_call(body, name=name, grid=grid, in_specs=in_specs, out_specs=out_specs, out_shape=out_shape,
                          scratch_shapes=list(scratch), input_output_aliases=aliases or {},
                          compiler_params=pltpu.CompilerParams(**params))


def _rows(tm, n, col=0):
    return pl.BlockSpec((tm, n), lambda i: (i, col))


def _full(shape):
    return pl.BlockSpec(shape, lambda i: (0,) * len(shape))


def _rms(x, g):
    r = lax.rsqrt(jnp.mean(x * x, axis=-1, keepdims=True) + EPS)
    return x * r * g


def _rms_bwd(dy, x, g):
    r = lax.rsqrt(jnp.mean(x * x, axis=-1, keepdims=True) + EPS)
    xn = x * r
    dn = dy * g
    dx = r * (dn - xn * jnp.mean(dn * xn, axis=-1, keepdims=True))
    return dx, jnp.sum(dy * xn, axis=0, keepdims=True)


def _sigmoid(x):
    return 1.0 / (1.0 + jnp.exp(-x))


def _rope(x, c, a, b, half):
    return x * c + pltpu.roll(x, LANES - half, 1) * a + pltpu.roll(x, half, 1) * b


def _rope_tables(T, kind):
    lane = jnp.arange(LANES)
    if kind == "mla":
        half = MLA_ROPE // 2
        rel = lane - MLA_NOPE
        on = (rel >= 0) & (rel < MLA_ROPE)
        d = MLA_ROPE
    else:
        half = SWA_HD // 2
        rel = lane % SWA_HD
        on = jnp.ones((LANES,), bool)
        d = SWA_HD
    first = on & (rel < half)
    second = on & (rel >= half)
    f = jnp.where(first, rel, rel - half).astype(F32)
    inv = jnp.exp(-math.log(ROPE_THETA) * f * (2.0 / d))
    ang = jnp.arange(T, dtype=F32)[:, None] * inv[None, :]
    cos, sin = jnp.cos(ang), jnp.sin(ang)
    c = jnp.where(on[None], cos, 1.0)
    a = jnp.where(first[None], -sin, 0.0)
    b = jnp.where(second[None], sin, 0.0)
    return c, a, b


def _fwd_in(x, g1, w_in_p, tm):
    T = x.shape[0]

    def body(x_ref, g_ref, w_ref, z_ref, h_ref):
        h = _rms(x_ref[...], g_ref[...]).astype(BF16)
        h_ref[...] = h
        z_ref[...] = _dot(h, w_ref[...])

    return _pcall(body, name="fwd_in", grid=(T // tm,),
                  in_specs=[_rows(tm, D), _full((1, D)), _full((D, NZ))],
                  out_specs=[_rows(tm, NZ), _rows(tm, D)],
                  out_shape=[SDS((T, NZ), F32), SDS((T, D), BF16)], sem=("parallel",))(x, g1, w_in_p)


def _fwd_qkv(z, gq, gkv, wqb, wkn, wv, tab_m, tab_s, tm):
    T = z.shape[0]
    wqb_t, wkn_t, wv_t = wqb.T, wkn.T, wv.T
    tab_mt = [t.T for t in tab_m]

    def body(qa_ref, sq_ref, skd_ref, svd_ref, kva_ref, kr_ref, gq_ref, gkv_ref, wkn_ref, wv_ref, wqbt_ref, wknt_ref, wvt_ref,
             cm_ref, am_ref, bm_ref, cmt_ref, amt_ref, bmt_ref, cs_ref, as_ref, bs_ref,
             qn_ref, kvn_ref, km_ref, vm_ref, qt_ref, kt_ref, vt_ref, qs_ref, ks_ref, vs_ref):
        qn = _rms(qa_ref[...], gq_ref[...])
        qn_ref[...] = qn.astype(BF16)
        kvn = _rms(kva_ref[...], gkv_ref[...])
        kvn_b = kvn.astype(BF16)
        kvn_ref[...] = kvn_b
        qn_t = qn.T.astype(BF16)
        kvn_t = kvn.T.astype(BF16)
        cm, am, bm = cm_ref[...], am_ref[...], bm_ref[...]
        cmt, amt, bmt = cmt_ref[...], amt_ref[...], bmt_ref[...]
        cs, as_, bs = cs_ref[...], as_ref[...], bs_ref[...]
        k_rope = _rope(kr_ref[...], cm, am, bm, MLA_ROPE // 2)
        k_rope_t = k_rope.T
        half = MLA_ROPE // 2
        vm_ref[...] = _dot(kvn_b, wv_ref[...]).astype(BF16)
        km_all = _dot(kvn_b, wkn_ref[...])
        v_t = _dot(wvt_ref[...], kvn_t)
        q_t = _dot(wqbt_ref[...], qn_t)
        k_t = _dot(wknt_ref[...], kvn_t)
        ones_row = jnp.where(lax.broadcasted_iota(jnp.int32, (64, tm), 0) == 0, 1.0, 0.0)
        for h in range(MLA_HEADS):
            sl = slice(LANES * h, LANES * (h + 1))
            vt_ref[0, sl, :] = jnp.concatenate([v_t[64 * h:64 * (h + 1)], ones_row], axis=0).astype(BF16)
            qh = q_t[sl]
            qt_ref[0, sl, :] = (qh * cmt + pltpu.roll(qh, LANES - half, 0) * amt + pltpu.roll(qh, half, 0) * bmt).astype(BF16)
            km_ref[:, sl] = (km_all[:, sl] + k_rope).astype(BF16)
            kt_ref[0, sl, :] = (k_t[sl] + k_rope_t).astype(BF16)
        for j in range(D // LANES):
            sl = slice(LANES * j, LANES * (j + 1))
            qs_ref[:, sl] = _rope(sq_ref[:, sl], cs, as_, bs, SWA_HD // 2).astype(BF16)
        for j in range(2):
            sl = slice(LANES * j, LANES * (j + 1))
            ks_ref[:, sl] = _rope(skd_ref[:, sl], cs, as_, bs, SWA_HD // 2).astype(BF16)
        vs_ref[...] = svd_ref[...].astype(BF16)

    tab = [_rows(tm, LANES)] * 3
    tab_t = [pl.BlockSpec((LANES, tm), lambda i: (0, i))] * 3
    return _pcall(body, name="fwd_qkv", grid=(T // tm,),
                  in_specs=[_rows(tm, 256, 12), _rows(tm, 1024, 0), _rows(tm, 256, 13), _rows(tm, 256, 14),
                            _rows(tm, 128, 30), _rows(tm, 128, 31), _full((1, Q_LORA)), _full((1, KV_LORA)),
                            _full((KV_LORA, 2048)), _full((KV_LORA, 1024)), _full((2048, Q_LORA)), _full((2048, KV_LORA)),
                            _full((1024, KV_LORA))] + tab + tab_t + tab,
                  out_specs=[_rows(tm, Q_LORA), _rows(tm, KV_LORA), _rows(tm, 2048), _rows(tm, 1024),
                             pl.BlockSpec((1, 2048, tm), lambda i: (i, 0, 0)), pl.BlockSpec((1, 2048, tm), lambda i: (i, 0, 0)),
                             pl.BlockSpec((1, 2048, tm), lambda i: (i, 0, 0)),
                             _rows(tm, 1024), _rows(tm, 256), _rows(tm, 256)],
                  out_shape=[SDS((T, Q_LORA), BF16), SDS((T, KV_LORA), BF16), SDS((T, 2048), BF16),
                             SDS((T, 1024), BF16), SDS((T // tm, 2048, tm), BF16), SDS((T // tm, 2048, tm), BF16),
                             SDS((T // tm, 2048, tm), BF16),
                             SDS((T, 1024), BF16), SDS((T, 256), BF16), SDS((T, 256), BF16)],
                  sem=("parallel",))(z, z, z, z, z, z, gq, gkv, wkn, wv, wqb_t, wkn_t, wv_t, *tab_m, *tab_mt, *tab_s)


def _mla_fwd(qt, km, vt, tb):
    T = km.shape[0]
    nb = T // tb
    cc = ATT_COLS

    per = 2 if nb % 2 == 0 else 1

    def body(q_ref, k_ref, vt_ref, o_ref, l_ref, s_ref, p_ref, al_ref, m_ref, acc_ref):
        for blk in range(per):
            one_block(per * pl.program_id(1) + blk, blk, q_ref, k_ref, vt_ref, o_ref, l_ref, s_ref, p_ref, al_ref, m_ref, acc_ref)

    def one_block(i, blk, q_ref, k_ref, vt_ref, o_ref, l_ref, s_ref, p_ref, al_ref, m_ref, acc_ref):
        m_ref[...] = jnp.full(m_ref.shape, NEG, F32)
        acc_ref[...] = jnp.zeros_like(acc_ref)
        p_ref[1] = jnp.zeros(p_ref.shape[1:], BF16)
        al_ref[1] = jnp.ones(al_ref.shape[1:], F32)
        key = lax.broadcasted_iota(jnp.int32, (tb, cc), 0)
        qry = lax.broadcasted_iota(jnp.int32, (tb, cc), 1)

        def scores(j, slot):
            off = pl.multiple_of(j * tb, tb)
            for hh in range(2):
                sl = slice(LANES * hh, LANES * (hh + 1))
                s_ref[slot, hh] = _dot(k_ref[pl.ds(off, tb), sl], q_ref[blk, sl, :])

        def softmax(slot, diagonal):
            chains = [(hh, slice(cc * c, cc * (c + 1)), c) for hh in range(2) for c in range(tb // cc)]

            def scaled(hh, cols, c):
                t = s_ref[slot, hh, :, cols] * MLA_LOG2_SCALE
                return jnp.where(key <= qry + cc * c, t, NEG) if diagonal else t

            tops = []
            for hh, cols, c in chains:
                if diagonal:
                    top = jnp.max(scaled(hh, cols, c), axis=0, keepdims=True)
                else:
                    top = jnp.max(s_ref[slot, hh, :, cols], axis=0, keepdims=True) * MLA_LOG2_SCALE
                m_old = m_ref[hh, :, cols]
                mn = jnp.maximum(m_old, top)
                m_ref[hh, :, cols] = mn
                al_ref[slot, hh, :, cols] = jnp.exp2(m_old - mn)
                tops.append(mn)
            for (hh, cols, c), mn in zip(chains, tops):
                p_ref[slot, hh, :, cols] = jnp.exp2(scaled(hh, cols, c) - mn).astype(BF16)

        def accumulate(j, slot):
            for hh in range(2):
                acc_ref[hh] = al_ref[slot, hh] * acc_ref[hh] + _dot(vt_ref[j, LANES * hh:LANES * hh + VT_ROWS, :], p_ref[slot, hh])

        def step(t, carry):
            scores(2 * t + 1, 1)
            accumulate(jnp.maximum(2 * t - 1, 0), 1)
            softmax(0, False)
            scores(2 * t + 2, 0)
            accumulate(2 * t, 0)
            softmax(1, False)
            return carry

        scores(0, 0)
        lax.fori_loop(0, i // 2, step, 0)

        @pl.when(i % 2 == 1)
        def _():
            scores(i, 1)
            accumulate(jnp.maximum(i - 2, 0), 1)
            softmax(0, False)
            accumulate(i - 1, 0)
            softmax(1, True)
            accumulate(i, 1)

        @pl.when(i % 2 == 0)
        def _():
            accumulate(jnp.maximum(i - 1, 0), 1)
            softmax(0, True)
            accumulate(i, 0)
        den = [acc_ref[hh, 64:65, :] for hh in range(2)]
        o_ref[tb * blk:tb * (blk + 1), :] = jnp.concatenate([acc_ref[hh, 0:64, :] / den[hh] for hh in range(2)], axis=0).T
        sub = lax.broadcasted_iota(jnp.int32, (8, tb), 0)
        lse = [m_ref[hh] + jnp.log(den[hh]) * LOG2_E for hh in range(2)]
        l_ref[0, blk] = jnp.where(sub == 0, lse[0], jnp.where(sub == 1, lse[1], 0.0))

    return _pcall(body, name="mla_fwd", grid=(MLA_HEADS // 2, nb // per),
                  in_specs=[pl.BlockSpec((per, 256, tb), lambda p, i: (i, p, 0)), pl.BlockSpec((T, 256), lambda p, i: (0, p)),
                            pl.BlockSpec((nb, 2 * LANES, tb), lambda p, i: (0, p, 0))],
                  out_specs=[pl.BlockSpec((per * tb, LANES), lambda p, i: (i, p)),
                             pl.BlockSpec((1, per, 8, tb), lambda p, i: (p, i, 0, 0))],
                  out_shape=[SDS((T, D), F32), SDS((MLA_HEADS // 2, nb, 8, tb), F32)],
                  scratch=[pltpu.VMEM((2, 2, tb, tb), F32), pltpu.VMEM((2, 2, tb, tb), BF16), pltpu.VMEM((2, 2, 1, tb), F32),
                           pltpu.VMEM((2, 1, tb), F32), pltpu.VMEM((2, VT_ROWS, tb), F32)],
                  sem=("parallel", "arbitrary"))(qt, km, vt)


def _swa_mask(n):
    row = lax.broadcasted_iota(jnp.int32, (WINDOW, 2 * WINDOW), 0)
    col = lax.broadcasted_iota(jnp.int32, (WINDOW, 2 * WINDOW), 1)
    rel = row - col + WINDOW
    return (rel >= 0) & (rel < WINDOW) & ((col >= WINDOW) | (n > 0))


def _swa_specs(T):
    nb = T // WINDOW
    cur = lambda w: pl.BlockSpec((WINDOW, w), lambda n: (n, 0))
    prev = lambda w: pl.BlockSpec((WINDOW, w), lambda n: (jnp.maximum(n - 1, 0), 0))
    return nb, cur, prev


def _swa_fwd(sinks, qs, ks, vs):
    T = qs.shape[0]
    nb, cur, prev = _swa_specs(T)

    def body(sink_ref, q_ref, kc_ref, kp_ref, vc_ref, vp_ref, o_ref, l_ref, kb_ref, vb_ref, s_ref, p_ref):
        n = pl.program_id(0)
        mask = _swa_mask(n)
        lo = lax.broadcasted_iota(jnp.int32, (WINDOW, LANES), 1) < 64
        hi = jnp.logical_not(lo)
        for g in range(2):
            gs = slice(LANES * g, LANES * (g + 1))
            kb_ref[g] = jnp.concatenate([kp_ref[:, gs], kc_ref[:, gs]], axis=0)
            vb_ref[g] = jnp.concatenate([vp_ref[:, gs], vc_ref[:, gs]], axis=0)
        for h in range(SWA_HEADS):
            qp = q_ref[:, LANES * (h // 2):LANES * (h // 2 + 1)]
            qh = jnp.where(lo if h % 2 == 0 else hi, qp, jnp.zeros_like(qp))
            s_ref[h] = _dot_nt(qh, kb_ref[h // 8])
        for j in range(SWA_HEADS // 2):
            sl = slice(LANES * j, LANES * (j + 1))
            lses = []
            for h in (2 * j, 2 * j + 1):
                s = jnp.where(mask, s_ref[h] * SWA_SCALE, NEG)
                sk = sink_ref[h]
                m = jnp.maximum(jnp.max(s, axis=1, keepdims=True), sk)
                e = jnp.exp(s - m)
                den = jnp.sum(e, axis=1, keepdims=True) + jnp.exp(sk - m)
                p_ref[h] = (e / den).astype(BF16)
                lses.append(jnp.broadcast_to(m + jnp.log(den), (WINDOW, LANES)))
            l_ref[:, sl] = jnp.where(lo, lses[0], lses[1])
        for j in range(SWA_HEADS // 2):
            vb = vb_ref[j // 4]
            o_ref[:, LANES * j:LANES * (j + 1)] = jnp.where(lo, _dot(p_ref[2 * j], vb), _dot(p_ref[2 * j + 1], vb))

    return _pcall(body, name="swa_fwd", grid=(nb,),
                  in_specs=[pl.BlockSpec(memory_space=pltpu.SMEM), cur(D), cur(256), prev(256), cur(256), prev(256)],
                  out_specs=[cur(D), cur(D)], out_shape=[SDS((T, D), F32)] * 2,
                  scratch=[pltpu.VMEM((2, 2 * WINDOW, LANES), BF16), pltpu.VMEM((2, 2 * WINDOW, LANES), BF16),
                           pltpu.VMEM((SWA_HEADS, WINDOW, 2 * WINDOW), F32), pltpu.VMEM((SWA_HEADS, WINDOW, 2 * WINDOW), BF16)],
                  sem=("parallel",))(sinks, qs, ks, ks, vs, vs)


def _fwd_mix(om, os_, z, x, wmu, wsu, wo, g2, tm):
    T = x.shape[0]

    def body(om_ref, os_ref, ga_ref, gb_ref, x_ref, wmu_ref, wsu_ref, wo_ref, g2_ref,
             y_ref, yo_ref, au_ref, bu_ref, x1_ref):
        au = _dot(om_ref[...].astype(BF16), wmu_ref[...])
        bu = _dot(os_ref[...].astype(BF16), wsu_ref[...])
        au_ref[...] = au
        bu_ref[...] = bu
        y = (_sigmoid(ga_ref[...]) * au + _sigmoid(gb_ref[...]) * bu).astype(BF16)
        y_ref[...] = y
        yo = _dot(y, wo_ref[...])
        yo_ref[...] = yo
        x1_ref[...] = x_ref[...] + _rms(yo, g2_ref[...])

    r = _rows(tm, D)
    w = _full((D, D))
    return _pcall(body, name="fwd_mix", grid=(T // tm,),
                  in_specs=[r, r, _rows(tm, D, 1), _rows(tm, D, 2), r, w, w, w, _full((1, D))],
                  out_specs=[r] * 5,
                  out_shape=[SDS((T, D), BF16), SDS((T, D), F32), SDS((T, D), F32), SDS((T, D), F32), SDS((T, D), F32)],
                  sem=("parallel",))(om, os_, z, z, x, wmu, wsu, wo, g2)


def _fwd_mlp_up(x1, g3, w1, tm):
    T = x1.shape[0]

    def body(x_ref, g_ref, w_ref, h_ref, u_ref):
        h = _rms(x_ref[...], g_ref[...]).astype(BF16)
        h_ref[...] = h
        u_ref[...] = jnp.square(jnp.maximum(_dot(h, w_ref[...]), 0.0)).astype(BF16)

    return _pcall(body, name="fwd_mlp_up", grid=(T // tm,),
                  in_specs=[_rows(tm, D), _full((1, D)), _full((D, D_FF))],
                  out_specs=[_rows(tm, D), _rows(tm, D_FF)],
                  out_shape=[SDS((T, D), BF16), SDS((T, D_FF), BF16)],
                  sem=("parallel",))(x1, g3, w1)


def _fwd_mlp_down(u, w2, x1, g4, tm):
    T = x1.shape[0]

    def body(u_ref, w_ref, x_ref, g_ref, d_ref, x2_ref):
        d = _dot(u_ref[...], w_ref[...])
        d_ref[...] = d
        x2_ref[...] = x_ref[...] + _rms(d, g_ref[...])

    return _pcall(body, name="fwd_mlp_down", grid=(T // tm,),
                  in_specs=[_rows(tm, D_FF), _full((D_FF, D)), _rows(tm, D), _full((1, D))],
                  out_specs=[_rows(tm, D), _rows(tm, D)], out_shape=[SDS((T, D), F32)] * 2,
                  sem=("parallel",))(u, w2, x1, g4)


def _ple_fwd_bwd(p, x2, tgt, wple, g5, wpg, tm):
    T = x2.shape[0]

    def body(p_ref, x2_ref, t_ref, wple_ref, g5_ref, wpg_ref, loss_ref, dx2_ref, dgt_ref, de0_ref, dg5_ref):
        @pl.when(pl.program_id(0) == 0)
        def _():
            loss_ref[...] = jnp.zeros_like(loss_ref)
            dg5_ref[...] = jnp.zeros_like(dg5_ref)

        e0 = _dot(p_ref[...].astype(BF16), wple_ref[...])
        g5 = g5_ref[...]
        r = lax.rsqrt(jnp.mean(e0 * e0, axis=-1, keepdims=True) + EPS)
        en = e0 * r
        e = en * g5
        x2 = x2_ref[...]
        s = _sigmoid(_dot(x2.astype(BF16), wpg_ref[...]))
        diff = x2 + s * e - t_ref[...]
        sq = jnp.sum(jnp.sum(diff * diff, axis=1, keepdims=True), axis=0, keepdims=True)
        loss_ref[...] += jnp.broadcast_to(sq * (0.5 / D), loss_ref.shape)
        dx3 = diff * (1.0 / D)
        de = dx3 * s
        dgt = (dx3 * e * s * (1.0 - s)).astype(BF16)
        dgt_ref[...] = dgt
        dn = de * g5
        de0_ref[...] = (r * (dn - en * jnp.mean(dn * en, axis=-1, keepdims=True))).astype(BF16)
        dg5_ref[...] += jnp.sum(de * en, axis=0, keepdims=True)
        dx2_ref[...] = dx3 + _dot_nt(dgt, wpg_ref[...])

    r = _rows(tm, D)
    return _pcall(body, name="ple_fwd_bwd", grid=(T // tm,),
                  in_specs=[_rows(tm, PLE), r, r, _full((PLE, D)), _full((1, D)), _full((D, D))],
                  out_specs=[_full((8, LANES)), r, r, r, _full((1, D))],
                  out_shape=[SDS((8, LANES), F32), SDS((T, D), F32), SDS((T, D), BF16), SDS((T, D), BF16), SDS((1, D), F32)],
                  sem=("arbitrary",))(p, x2, tgt, wple, g5, wpg)


def _bwd_mlp_down(dx2, d, g4, w2, u, tm):
    T = dx2.shape[0]

    def body(dx_ref, d_ref, g_ref, w_ref, u_ref, dd_ref, da_ref, dg_ref):
        @pl.when(pl.program_id(0) == 0)
        def _():
            dg_ref[...] = jnp.zeros_like(dg_ref)

        dd, dg = _rms_bwd(dx_ref[...], d_ref[...], g_ref[...])
        dg_ref[...] += dg
        ddb = dd.astype(BF16)
        dd_ref[...] = ddb
        du = _dot_nt(ddb, w_ref[...])
        da_ref[...] = (du * (2.0 * jnp.sqrt(u_ref[...].astype(F32)))).astype(BF16)

    return _pcall(body, name="bwd_mlp_down", grid=(T // tm,),
                  in_specs=[_rows(tm, D), _rows(tm, D), _full((1, D)), _full((D_FF, D)), _rows(tm, D_FF)],
                  out_specs=[_rows(tm, D), _rows(tm, D_FF), _full((1, D))],
                  out_shape=[SDS((T, D), BF16), SDS((T, D_FF), BF16), SDS((1, D), F32)],
                  sem=("arbitrary",))(dx2, d, g4, w2, u)


def _bwd_mlp_up(da, w1, x1, g3, dx2, tm):
    T = dx2.shape[0]

    def body(da_ref, w_ref, x_ref, g_ref, dx2_ref, dx1_ref, dg_ref):
        @pl.when(pl.program_id(0) == 0)
        def _():
            dg_ref[...] = jnp.zeros_like(dg_ref)

        dh = _dot_nt(da_ref[...], w_ref[...])
        dx, dg = _rms_bwd(dh, x_ref[...], g_ref[...])
        dg_ref[...] += dg
        dx1_ref[...] = dx2_ref[...] + dx

    return _pcall(body, name="bwd_mlp_up", grid=(T // tm,),
                  in_specs=[_rows(tm, D_FF), _full((D, D_FF)), _rows(tm, D), _full((1, D)), _rows(tm, D)],
                  out_specs=[_rows(tm, D), _full((1, D))],
                  out_shape=[SDS((T, D), F32), SDS((1, D), F32)], sem=("arbitrary",))(da, w1, x1, g3, dx2)


def _bwd_mix(dx1, yo, g2, wo, z, au, bu, wmu, wsu, om, tm):
    T = dx1.shape[0]

    def body(dx_ref, yo_ref, g_ref, wo_ref, ga_ref, gb_ref, au_ref, bu_ref, wmu_ref, wsu_ref, om_ref,
             dyo_ref, dg_ref, dau_ref, dbu_ref, dga_ref, dgb_ref, dos_ref, dl_ref, dot_ref):
        @pl.when(pl.program_id(0) == 0)
        def _():
            dg_ref[...] = jnp.zeros_like(dg_ref)

        dyo, dg = _rms_bwd(dx_ref[...], yo_ref[...], g_ref[...])
        dg_ref[...] += dg
        dyob = dyo.astype(BF16)
        dyo_ref[...] = dyob
        dy = _dot_nt(dyob, wo_ref[...])
        sa = _sigmoid(ga_ref[...])
        sb = _sigmoid(gb_ref[...])
        dau = (dy * sa).astype(BF16)
        dbu = (dy * sb).astype(BF16)
        dau_ref[...] = dau
        dbu_ref[...] = dbu
        dga_ref[...] = (dy * au_ref[...] * sa * (1.0 - sa)).astype(BF16)
        dgb_ref[...] = (dy * bu_ref[...] * sb * (1.0 - sb)).astype(BF16)
        dom = _dot_nt(dau, wmu_ref[...])
        dos_ref[...] = _dot_nt(dbu, wsu_ref[...])
        prod = dom * om_ref[...]
        sub = lax.broadcasted_iota(jnp.int32, (8, tm), 0)
        for pr in range(MLA_HEADS // 2):
            sl = slice(LANES * pr, LANES * (pr + 1))
            pt = prod[:, sl].T
            d0 = jnp.sum(pt[0:64], axis=0, keepdims=True)
            d1 = jnp.sum(pt[64:128], axis=0, keepdims=True)
            dl_ref[pr, 0] = jnp.where(sub == 0, d0, jnp.where(sub == 1, d1, 0.0))
            dot_ref[0, sl, :] = dom[:, sl].T.astype(BF16)

    r = _rows(tm, D)
    w = _full((D, D))
    return _pcall(body, name="bwd_mix", grid=(T // tm,),
                  in_specs=[r, r, _full((1, D)), w, _rows(tm, D, 1), _rows(tm, D, 2), r, r, w, w, r],
                  out_specs=[r, _full((1, D)), r, r, r, r, r, pl.BlockSpec((MLA_HEADS // 2, 1, 8, tm), lambda i: (0, i, 0, 0)),
                             pl.BlockSpec((1, D, tm), lambda i: (i, 0, 0))],
                  out_shape=[SDS((T, D), BF16), SDS((1, D), F32), SDS((T, D), BF16), SDS((T, D), BF16), SDS((T, D), BF16),
                             SDS((T, D), BF16), SDS((T, D), F32), SDS((MLA_HEADS // 2, T // tm, 8, tm), F32),
                             SDS((T // tm, D, tm), BF16)],
                  sem=("arbitrary",))(dx1, yo, g2, wo, z, z, au, bu, wmu, wsu, om)


def _mla_bwd(qt, km, kt, vm, dot, lse, delta, tb):
    T = km.shape[0]
    nb = T // tb
    cc = ATT_COLS

    per = 2 if nb % 2 == 0 else 1

    def body(qt_ref, k_ref, kt_ref, v_ref, dot_ref, l_ref, dl_ref, dqt_ref, dkt_ref, dvt_ref,
             s_ref, dp_ref, p_ref, ds_ref, vh_ref):
        @pl.when(pl.program_id(1) == 0)
        def _():
            dqt_ref[...] = jnp.zeros_like(dqt_ref)

        dkt_ref[...] = jnp.zeros_like(dkt_ref)
        dvt_ref[...] = jnp.zeros_like(dvt_ref)
        for blk in range(per):
            one_block(per * pl.program_id(1) + blk, blk, qt_ref, k_ref, kt_ref, v_ref, dot_ref, l_ref, dl_ref, dqt_ref, dkt_ref,
                      dvt_ref, s_ref, dp_ref, p_ref, ds_ref, vh_ref)

    def one_block(j, blk, qt_ref, k_ref, kt_ref, v_ref, dot_ref, l_ref, dl_ref, dqt_ref, dkt_ref, dvt_ref,
                  s_ref, dp_ref, p_ref, ds_ref, vh_ref):
        lo = lax.broadcasted_iota(jnp.int32, (tb, LANES), 1) < 64
        key = lax.broadcasted_iota(jnp.int32, (tb, cc), 0)
        qry = lax.broadcasted_iota(jnp.int32, (tb, cc), 1)
        rows_j = slice(tb * blk, tb * (blk + 1))
        v = v_ref[rows_j, :]
        vh_ref[0] = jnp.where(lo, v, jnp.zeros_like(v))
        vh_ref[1] = jnp.where(lo, jnp.zeros_like(v), v)

        def scores(i, slot):
            for hh in range(2):
                sl = slice(LANES * hh, LANES * (hh + 1))
                s_ref[slot, hh] = _dot(k_ref[rows_j, sl], qt_ref[i, sl, :])
                dp_ref[slot, hh] = _dot(vh_ref[hh], dot_ref[i])

        def grads(i, slot, diagonal):
            lse_i = l_ref[0, i]
            delta_i = dl_ref[0, i]
            for hh in range(2):
                for c in range(tb // cc):
                    cols = slice(cc * c, cc * (c + 1))
                    p = jnp.exp2(s_ref[slot, hh, :, cols] * MLA_LOG2_SCALE - lse_i[hh:hh + 1, cols])
                    if diagonal:
                        p = jnp.where(key <= qry + cc * c, p, 0.0)
                    p_ref[hh, :, cols] = p.astype(BF16)
                    ds_ref[hh, :, cols] = (p * (dp_ref[slot, hh, :, cols] - delta_i[hh:hh + 1, cols]) * MLA_SCALE).astype(BF16)
            for hh in range(2):
                sl = slice(LANES * hh, LANES * (hh + 1))
                half = slice(64 * hh, 64 * (hh + 1))
                dvt_ref[blk, half, :] += _dot_nt(dot_ref[i, half, :], p_ref[hh])
                real = slice(LANES * hh, LANES * hh + MLA_NOPE + MLA_ROPE)
                dkt_ref[blk, real, :] += _dot_nt(qt_ref[i, real, :], ds_ref[hh])
                dqt_ref[i, real, :] += _dot(kt_ref[blk, real, :], ds_ref[hh])

        n_off = nb - 1 - j

        def step(u, carry):
            i0 = j + 1 + 2 * u
            scores(i0 + 1, 1)
            grads(i0, 0, False)
            scores(jnp.where(i0 + 2 < nb, i0 + 2, j), 0)
            grads(i0 + 1, 1, False)
            return carry

        scores(jnp.where(n_off > 0, j + 1, j), 0)
        lax.fori_loop(0, n_off // 2, step, 0)

        @pl.when(n_off % 2 == 1)
        def _():
            scores(j, 1)
            grads(nb - 1, 0, False)
            grads(j, 1, True)

        @pl.when(n_off % 2 == 0)
        def _():
            grads(j, 0, True)

    blk = lambda w: pl.BlockSpec((per * tb, w), lambda p, j: (j, p))
    stat = pl.BlockSpec((1, nb, 8, tb), lambda p, j: (p, 0, 0, 0))
    pair_t = lambda w: pl.BlockSpec((nb, w, tb), lambda p, j: (0, p, 0))
    blk_t = lambda w: pl.BlockSpec((per, w, tb), lambda p, j: (j, p, 0))
    return _pcall(body, name="mla_bwd", grid=(MLA_HEADS // 2, nb // per),
                  in_specs=[pair_t(256), blk(256), blk_t(256), blk(LANES), pair_t(LANES), stat, stat],
                  out_specs=[pair_t(256), blk_t(256), blk_t(LANES)],
                  out_shape=[SDS((nb, 2048, tb), F32), SDS((nb, 2048, tb), F32), SDS((nb, D, tb), F32)],
                  scratch=[pltpu.VMEM((2, 2, tb, tb), F32), pltpu.VMEM((2, 2, tb, tb), F32), pltpu.VMEM((2, tb, tb), BF16),
                           pltpu.VMEM((2, tb, tb), BF16), pltpu.VMEM((2, tb, LANES), BF16)],
                  sem=("parallel", "arbitrary"))(qt, km, kt, vm, dot, lse, delta)


def _swa_bwd(sinks, qs, ks, vs, do, o, lse):
    T = qs.shape[0]
    nb, cur, prev = _swa_specs(T)

    def body(sink_ref, q_ref, kc_ref, kp_ref, vc_ref, vp_ref, do_ref, o_ref, l_ref,
             dq_ref, dkc_ref, dkp_ref, dvc_ref, dvp_ref, dsink_ref, kb_ref, vb_ref, s_ref, dp_ref, p_ref, ds_ref):
        n = pl.program_id(0)

        @pl.when(n == 0)
        def _():
            dsink_ref[...] = jnp.zeros_like(dsink_ref)

        mask = _swa_mask(n)
        lo = lax.broadcasted_iota(jnp.int32, (WINDOW, LANES), 1) < 64
        hi = jnp.logical_not(lo)
        lane8 = lax.broadcasted_iota(jnp.int32, (8, LANES), 1)
        for g in range(2):
            gs = slice(LANES * g, LANES * (g + 1))
            kb_ref[g] = jnp.concatenate([kp_ref[:, gs], kc_ref[:, gs]], axis=0)
            vb_ref[g] = jnp.concatenate([vp_ref[:, gs], vc_ref[:, gs]], axis=0)

        def head(h):
            sl = slice(LANES * (h // 2), LANES * (h // 2 + 1))
            hm = lo if h % 2 == 0 else hi
            qp = q_ref[:, sl]
            return hm, sl, jnp.where(hm, qp, jnp.zeros_like(qp)), jnp.where(hm, do_ref[:, sl], 0.0).astype(BF16)

        for h in range(SWA_HEADS):
            _, _, qh, dom = head(h)
            s_ref[h] = _dot_nt(qh, kb_ref[h // 8])
            dp_ref[h] = _dot_nt(dom, vb_ref[h // 8])
        dsink = jnp.zeros((8, LANES), F32)
        for h in range(SWA_HEADS):
            hm, sl, _, _ = head(h)
            lse_h = jnp.max(jnp.where(hm, l_ref[:, sl], -jnp.inf), axis=1, keepdims=True)
            delta = jnp.sum(jnp.where(hm, do_ref[:, sl] * o_ref[:, sl], 0.0), axis=1, keepdims=True)
            p = jnp.exp(jnp.where(mask, s_ref[h] * SWA_SCALE, NEG) - lse_h)
            p_ref[h] = p.astype(BF16)
            ds_ref[h] = (p * (dp_ref[h] - delta) * SWA_SCALE).astype(BF16)
            d_sink = -jnp.sum(jnp.exp(sink_ref[h] - lse_h) * delta, axis=0, keepdims=True)
            dsink = dsink + jnp.where(lane8 == h, d_sink, 0.0)
        dsink_ref[...] += dsink
        for g in range(2):
            gs = slice(LANES * g, LANES * (g + 1))
            dkb = jnp.zeros((2 * WINDOW, LANES), F32)
            dvb = jnp.zeros((2 * WINDOW, LANES), F32)
            for j in range(4 * g, 4 * g + 4):
                dqs = []
                for h in (2 * j, 2 * j + 1):
                    _, _, qh, dom = head(h)
                    dvb = dvb + _dot_tn(p_ref[h], dom)
                    dkb = dkb + _dot_tn(ds_ref[h], qh)
                    dqs.append(_dot(ds_ref[h], kb_ref[g]))
                dq_ref[:, LANES * j:LANES * (j + 1)] = jnp.where(lo, dqs[0], dqs[1])
            dkp_ref[:, gs] = dkb[:WINDOW]
            dkc_ref[:, gs] = dkb[WINDOW:]
            dvp_ref[:, gs] = dvb[:WINDOW]
            dvc_ref[:, gs] = dvb[WINDOW:]

    band = pltpu.VMEM((2, 2 * WINDOW, LANES), BF16)
    return _pcall(body, name="swa_bwd", grid=(nb,),
                  in_specs=[pl.BlockSpec(memory_space=pltpu.SMEM), cur(D), cur(256), prev(256), cur(256), prev(256),
                            cur(D), cur(D), cur(D)],
                  out_specs=[cur(D), cur(256), cur(256), cur(256), cur(256), _full((8, LANES))],
                  out_shape=[SDS((T, D), F32), SDS((T, 256), F32), SDS((T, 256), F32), SDS((T, 256), F32), SDS((T, 256), F32),
                             SDS((8, LANES), F32)],
                  scratch=[band, band, pltpu.VMEM((SWA_HEADS, WINDOW, 2 * WINDOW), F32),
                           pltpu.VMEM((SWA_HEADS, WINDOW, 2 * WINDOW), F32), pltpu.VMEM((SWA_HEADS, WINDOW, 2 * WINDOW), BF16),
                           pltpu.VMEM((SWA_HEADS, WINDOW, 2 * WINDOW), BF16)],
                  sem=("arbitrary",))(sinks, qs, ks, ks, vs, vs, do, o, lse)


def _bwd_qkv(dqm, dkm, dvm, dqs, dkc, dkp, dvc, dvp, z, gq, gkv, wqb, wkn, wv, tab_m, tab_s):
    T = z.shape[0]
    tm = WINDOW
    nb = T // tm
    per = dqm.shape[2] // tm

    tab_mt = [t.T for t in tab_m]
    half = MLA_ROPE // 2

    def rope_t(v, c, a, b):
        return v * c + pltpu.roll(v, LANES - half, 0) * a + pltpu.roll(v, half, 0) * b

    def rms_bwd_t(dy, x, g):
        r = lax.rsqrt(jnp.mean(x * x, axis=0, keepdims=True) + EPS)
        xn = x * r
        dn = dy * g
        return r * (dn - xn * jnp.mean(dn * xn, axis=0, keepdims=True)), jnp.sum(dy * xn, axis=1, keepdims=True)

    def body(dqm_ref, dkm_ref, dvm_ref, dqs_ref, dkc_ref, dkp_ref, dvc_ref, dvp_ref, qa_ref, kva_ref, gq_ref, gkv_ref,
             wqb_ref, wkn_ref, wv_ref, cmt_ref, amt_ref, bmt_ref, cs_ref, as_ref, bs_ref,
             dq_out, dkn_out, dv_out, dsq_ref, drest_ref, dgq_ref, dgkv_ref):
        i = pl.program_id(0)

        @pl.when(i == 0)
        def _():
            dgq_ref[...] = jnp.zeros_like(dgq_ref)
            dgkv_ref[...] = jnp.zeros_like(dgkv_ref)

        cmt, amt, bmt = cmt_ref[...], -amt_ref[...], -bmt_ref[...]
        cs, as_, bs = cs_ref[...], -as_ref[...], -bs_ref[...]
        row = lax.broadcasted_iota(jnp.int32, (LANES, tm), 0)
        nope = row < MLA_NOPE
        roped = jnp.logical_and(row >= MLA_NOPE, row < MLA_NOPE + MLA_ROPE)
        dkr = jnp.zeros((LANES, tm), F32)
        for h in range(MLA_HEADS):
            sl = slice(LANES * h, LANES * (h + 1))
            dq_out[0, sl, :] = rope_t(dqm_ref[0, sl, :], cmt, amt, bmt).astype(BF16)
            dk_h = dkm_ref[0, sl, :]
            dkn_out[0, sl, :] = jnp.where(nope, dk_h, 0.0).astype(BF16)
            dkr = dkr + jnp.where(roped, dk_h, 0.0)
        dv_out[0] = dvm_ref[0].astype(BF16)
        dqn = _dot(wqb_ref[...], dq_out[0])
        dkvn = _dot(wkn_ref[...], dkn_out[0]) + _dot(wv_ref[...], dv_out[0])
        dqa, dgq = rms_bwd_t(dqn, qa_ref[...].T, gq_ref[...])
        dkva, dgkv = rms_bwd_t(dkvn, kva_ref[...].T, gkv_ref[...])
        dgq_ref[...] += dgq
        dgkv_ref[...] += dgkv
        for j in range(D // LANES):
            sl = slice(LANES * j, LANES * (j + 1))
            dsq_ref[:, sl] = _rope(dqs_ref[:, sl], cs, as_, bs, SWA_HD // 2).astype(BF16)
        keep = (i < nb - 1).astype(F32)
        drest_ref[:, 0:256] = dqa.T.astype(BF16)
        for j in range(2):
            sl = slice(LANES * j, LANES * (j + 1))
            dk = dkc_ref[:, sl] + keep * dkp_ref[:, sl]
            drest_ref[:, 256 + LANES * j:256 + LANES * (j + 1)] = _rope(dk, cs, as_, bs, SWA_HD // 2).astype(BF16)
        drest_ref[:, 512:768] = (dvc_ref[...] + keep * dvp_ref[...]).astype(BF16)
        drest_ref[:, 768:896] = dkva.T.astype(BF16)
        drest_ref[:, 896:1024] = rope_t(dkr, cmt, amt, bmt).T.astype(BF16)

    nxt = pl.BlockSpec((tm, 256), lambda i: (jnp.minimum(i + 1, nb - 1), 0))
    tab = [_rows(tm, LANES)] * 3
    tab_t = [pl.BlockSpec((LANES, tm), lambda i: (0, i))] * 3
    blk_t = lambda w: pl.BlockSpec((1, w, tm), lambda i: (i // per, 0, i % per))
    return _pcall(body, name="bwd_qkv", grid=(nb,),
                  in_specs=[blk_t(2048), blk_t(2048), blk_t(1024), _rows(tm, 1024), _rows(tm, 256), nxt,
                            _rows(tm, 256), nxt, _rows(tm, 256, 12), _rows(tm, 128, 30), _full((Q_LORA, 1)), _full((KV_LORA, 1)),
                            _full((Q_LORA, 2048)), _full((KV_LORA, 2048)), _full((KV_LORA, 1024))] + tab_t + tab,
                  out_specs=[blk_t(2048), blk_t(2048), blk_t(1024), _rows(tm, 1024), _rows(tm, 1024),
                             _full((Q_LORA, 1)), _full((KV_LORA, 1))],
                  out_shape=[SDS(dqm.shape, BF16), SDS(dkm.shape, BF16), SDS(dvm.shape, BF16), SDS((T, 1024), BF16),
                             SDS((T, 1024), BF16), SDS((Q_LORA, 1), F32), SDS((KV_LORA, 1), F32)],
                  sem=("arbitrary",))(dqm, dkm, dvm, dqs, dkc, dkp, dvc, dvp, z, z, gq.reshape(Q_LORA, 1),
                                      gkv.reshape(KV_LORA, 1), wqb, wkn, wv, *tab_mt, *tab_s)


def _bwd_in(dsq, dga, dgb, drest, w_in_p, x, g1, dx1, tm):
    T = x.shape[0]

    def body(a_ref, b_ref, c_ref, d_ref, w_ref, x_ref, g_ref, dx1_ref, dx_ref, dg_ref):
        @pl.when(pl.program_id(0) == 0)
        def _():
            dg_ref[...] = jnp.zeros_like(dg_ref)

        dh = (_dot_nt(a_ref[...], w_ref[:, 0:1024]) + _dot_nt(b_ref[...], w_ref[:, 1024:2048])
              + _dot_nt(c_ref[...], w_ref[:, 2048:3072]) + _dot_nt(d_ref[...], w_ref[:, 3072:4096]))
        dx, dg = _rms_bwd(dh, x_ref[...], g_ref[...])
        dg_ref[...] += dg
        dx_ref[...] = dx1_ref[...] + dx

    r = _rows(tm, D)
    return _pcall(body, name="bwd_in", grid=(T // tm,),
                  in_specs=[r, r, r, r, _full((D, NZ)), r, _full((1, D)), r],
                  out_specs=[r, _full((1, D))], out_shape=[SDS((T, D), F32), SDS((1, D), F32)],
                  sem=("arbitrary",))(dsq, dga, dgb, drest, w_in_p, x, g1, dx1)


def _wgrad(a, g, name, into=None):
    T, K = a.shape
    N = g.shape[1]
    tk, tn, tt = min(K, 1024), min(N, 1024), min(T, 1024)
    if into is not None:
        buf, weight = into
        _, row0, lane0 = PACK_AT[weight]
        shard = {n: (r, c) for n, r, c in BIG}[weight]
        assert lane0 == 0 and shard[1] == D and tk % shard[0] == 0
        per_step = tk // shard[0]
    assert K % tk == 0 and N % tn == 0 and T % tt == 0, (a.shape, g.shape)
    steps = T // tt

    def body(a_ref, g_ref, *rest):
        o_ref, acc_ref = rest[-2:]
        t = pl.program_id(2)

        @pl.when(t == 0)
        def _():
            acc_ref[...] = jnp.zeros_like(acc_ref)

        acc_ref[...] += _dot_tn(a_ref[...].astype(BF16), g_ref[...].astype(BF16))

        @pl.when(t == steps - 1)
        def _():
            o_ref[...] = acc_ref[...].astype(o_ref.dtype).reshape(o_ref.shape)

    in_specs = [pl.BlockSpec((tt, tk), lambda k, n, t: (t, k)), pl.BlockSpec((tt, tn), lambda k, n, t: (t, n))]
    if into is None:
        return _pcall(body, name=name, grid=(K // tk, N // tn, steps), in_specs=in_specs,
                      out_specs=pl.BlockSpec((tk, tn), lambda k, n, t: (k, n)), out_shape=SDS((K, N), F32),
                      scratch=[pltpu.VMEM((tk, tn), F32)], sem=("parallel", "parallel", "arbitrary"))(a, g)
    assert row0 % shard[0] == 0 and (K // tk) * (N // tn) * per_step == N_CHIPS
    return _pcall(body, name=name, grid=(K // tk, N // tn, steps), in_specs=in_specs + [ANY],
                  out_specs=pl.BlockSpec((per_step, shard[0], tn), lambda k, n, t: (k + n, row0 // shard[0], 0)),
                  out_shape=SDS(buf.shape, buf.dtype),
                  scratch=[pltpu.VMEM((tk, tn), F32)], sem=("parallel", "parallel", "arbitrary"), aliases={2: 0})(a, g, buf)


def _wgrad_t(at, g, name):
    nblk, K, tt = at.shape
    N = g.shape[1]
    tk = min(K, 1024)
    per_step = 4 if nblk % 4 == 0 else 1
    assert K % tk == 0 and g.shape[0] == nblk * tt

    def body(a_ref, g_ref, o_ref):
        @pl.when(pl.program_id(1) == 0)
        def _():
            o_ref[...] = jnp.zeros_like(o_ref)

        acc = _dot(a_ref[0], g_ref[0:tt, :].astype(BF16))
        for b in range(1, per_step):
            acc = acc + _dot(a_ref[b], g_ref[tt * b:tt * (b + 1), :].astype(BF16))
        o_ref[...] += acc

    return _pcall(body, name=name, grid=(K // tk, nblk // per_step),
                  in_specs=[pl.BlockSpec((per_step, tk, tt), lambda k, t: (t, k, 0)),
                            pl.BlockSpec((per_step * tt, N), lambda k, t: (t, 0))],
                  out_specs=pl.BlockSpec((tk, N), lambda k, t: (k, 0)), out_shape=SDS((K, N), F32),
                  sem=("parallel", "arbitrary"))(at, g)


def _adamw(w, packed_g, m, v, name):
    _, R, C = w.shape
    _, row0, lane0 = PACK_AT[name]
    tr = min(R, 256 if row0 % 256 == 0 else 128)
    assert row0 % tr == 0 and R % tr == 0

    def body(w_ref, g_ref, m_ref, v_ref, go_ref, d_ref, m2_ref, v2_ref):
        g_ = g_ref[:, lane0:lane0 + C]
        go_ref[0] = g_
        m2 = ADAM_B1 * m_ref[0] + (1.0 - ADAM_B1) * g_
        v2 = ADAM_B2 * v_ref[0] + (1.0 - ADAM_B2) * jnp.square(g_)
        m_hat = m2 / (1.0 - ADAM_B1 ** ADAM_STEP)
        v_hat = v2 / (1.0 - ADAM_B2 ** ADAM_STEP)
        d_ref[0] = -ADAM_LR * (m_hat / (jnp.sqrt(v_hat) + ADAM_EPS) + ADAM_WD * w_ref[0])
        m2_ref[0] = m2
        v2_ref[0] = v2

    r = pl.BlockSpec((1, tr, C), lambda i: (0, i, 0))
    return _pcall(body, name="adamw_" + name, grid=(R // tr,),
                  in_specs=[r, pl.BlockSpec((tr, D), lambda i: (row0 // tr + i, 0)), r, r], out_specs=[r] * 4,
                  out_shape=[SDS((1, R, C), F32)] * 4, sem=("parallel",))(w, packed_g, m, v)


def _adamw_small(w, parts, m, v):
    def body(w_ref, p_ref, m_ref, v_ref, g_ref, d_ref, m2_ref, v2_ref):
        g_ = p_ref[0]
        for k in range(1, N_DEV):
            g_ = g_ + p_ref[k]
        g_ref[...] = g_
        m2 = ADAM_B1 * m_ref[...] + (1.0 - ADAM_B1) * g_
        v2 = ADAM_B2 * v_ref[...] + (1.0 - ADAM_B2) * jnp.square(g_)
        m_hat = m2 / (1.0 - ADAM_B1 ** ADAM_STEP)
        v_hat = v2 / (1.0 - ADAM_B2 ** ADAM_STEP)
        d_ref[...] = -ADAM_LR * (m_hat / (jnp.sqrt(v_hat) + ADAM_EPS) + ADAM_WD * w_ref[...])
        m2_ref[...] = m2
        v2_ref[...] = v2

    s = _full((8, D))
    return _pcall(body, name="adamw_small", grid=(1,), in_specs=[s, _full((N_DEV, 8, D)), s, s], out_specs=[s] * 4,
                  out_shape=[SDS((8, D), F32)] * 4, sem=("arbitrary",))(w, parts, m, v)


ANY = pl.BlockSpec(memory_space=pl.ANY)


def _place():
    x, y, c = lax.axis_index("x"), lax.axis_index("y"), lax.axis_index("c")
    chips = [(1 - x, y), (x, 1 - y), (1 - x, 1 - y)]
    return x, y, c, chips


def _all_gather(wpk):
    rows = wpk.shape[0]
    HALF = rows // 2
    assert HALF % 16 == 0

    def body(in_ref, out_ref, send_sems, recv_sems):
        x, y, c, chips = _place()
        half = pl.ds(pl.multiple_of(c * HALF, 16), HALF)
        other = pl.ds(pl.multiple_of((1 - c) * HALF, 16), HALF)

        def copy(k, src, dst, to):
            return pltpu.make_async_remote_copy(src_ref=src, dst_ref=dst, send_sem=send_sems.at[k], recv_sem=recv_sems.at[k],
                                                device_id=to, device_id_type=MESH)

        first = [copy(k, in_ref.at[half], out_ref.at[2 * x + y, half], (cx, cy, c)) for k, (cx, cy) in enumerate(chips)]
        for cp in first:
            cp.start()
        passed = []
        for k, (cx, cy) in enumerate(chips):
            slot = out_ref.at[2 * cx + cy, half]
            copy(k, slot, slot, (x, y, c)).wait_recv()
            fwd = copy(3 + k, slot, slot, (x, y, 1 - c))
            fwd.start()
            passed.append(fwd)
        for k, (cx, cy) in enumerate(chips):
            slot = out_ref.at[2 * cx + cy, other]
            copy(3 + k, slot, slot, (x, y, c)).wait_recv()
        for cp in first + passed:
            cp.wait_send()

    return _pcall(body, name="all_gather_weights", in_specs=[ANY], out_specs=ANY,
                  out_shape=SDS((N_CHIPS, rows, D), BF16),
                  scratch=[pltpu.SemaphoreType.DMA((6,)), pltpu.SemaphoreType.DMA((6,))])(wpk)


HBM = pl.BlockSpec(memory_space=pltpu.HBM)
SEM = pl.BlockSpec(memory_space=pltpu.SEMAPHORE)
DATAFLOW = pltpu.SideEffectType.DATAFLOW_SIDE_EFFECTING


def _in_hbm(a):
    return pltpu.with_memory_space_constraint(a, pltpu.HBM)


def _gather_late_start(wpk, after):
    rows = wpk.shape[0]

    def body(in_ref, land_ref, after_ref, send_sems, recv_sems, in_thru, land_thru, token):
        x, y, c, chips = _place()
        for k, (cx, cy) in enumerate(chips):
            pltpu.make_async_remote_copy(src_ref=in_ref, dst_ref=land_ref.at[2 * x + y], send_sem=send_sems.at[k],
                                         recv_sem=recv_sems.at[k], device_id=(cx, cy, c), device_id_type=MESH).start()
        token[...] = jnp.zeros_like(token)

    return pl.pallas_call(
        body, name="gather_late_start",
        out_shape=(pltpu.SemaphoreType.DMA((3,)), pltpu.SemaphoreType.DMA((3,)), pltpu.HBM(wpk.shape, wpk.dtype),
                   pltpu.HBM((N_CHIPS, rows, D), wpk.dtype), SDS((8, LANES), F32)),
        in_specs=(HBM, HBM, ANY), out_specs=(SEM, SEM, HBM, HBM, pl.BlockSpec(memory_space=pltpu.VMEM)),
        input_output_aliases={0: 2, 1: 3}, compiler_params=pltpu.CompilerParams(has_side_effects=DATAFLOW),
    )(_in_hbm(wpk), _in_hbm(lax.empty((N_CHIPS, rows, D), wpk.dtype)), after)


def _gather_late_wait(send_sems, recv_sems, in_thru, land_thru, after):
    def body(in_ref, land_ref, send_sems, recv_sems, after_ref, after2_ref, in_dead, got_ref):
        x, y, c, chips = _place()
        for k, (cx, cy) in enumerate(chips):
            cp = pltpu.make_async_remote_copy(src_ref=in_ref, dst_ref=land_ref.at[2 * cx + cy], send_sem=send_sems.at[k],
                                              recv_sem=recv_sems.at[k], device_id=(cx, cy, c), device_id_type=MESH)
            cp.wait_send()
            cp.wait_recv()

    return pl.pallas_call(
        body, name="gather_late_wait",
        out_shape=(pltpu.HBM(in_thru.shape, in_thru.dtype), pltpu.HBM(land_thru.shape, land_thru.dtype)),
        in_specs=(HBM, HBM, SEM, SEM, ANY, ANY), out_specs=(HBM, HBM), input_output_aliases={0: 0, 1: 1},
        compiler_params=pltpu.CompilerParams(has_side_effects=DATAFLOW),
    )(in_thru, land_thru, send_sems, recv_sems, *after)[1]


def _rs_sibling(gpk):
    HALF = gpk.shape[1] // 2

    def body(in_ref, out_ref, send_sem, recv_sem):
        x, y, c, _ = _place()
        theirs = pl.ds(pl.multiple_of((1 - c) * HALF, 8), HALF)
        cp = pltpu.make_async_remote_copy(src_ref=in_ref.at[:, theirs], dst_ref=out_ref, send_sem=send_sem, recv_sem=recv_sem,
                                          device_id=(x, y, 1 - c), device_id_type=MESH)
        cp.start()
        cp.wait()

    return _pcall(body, name="rs_sibling", in_specs=[ANY], out_specs=ANY, out_shape=SDS((N_CHIPS, HALF, D), F32),
                  scratch=[pltpu.SemaphoreType.DMA, pltpu.SemaphoreType.DMA])(gpk)


def _rs_add_sibling(cidx, gpk, got):
    HALF = got.shape[1]
    th = HALF // 4
    nh = HALF // th
    assert th % 16 == 0

    def body(c_ref, a_ref, b_ref, o_ref):
        o_ref[...] = (a_ref[...] + b_ref[...]).astype(BF16)

    gs = pltpu.PrefetchScalarGridSpec(
        num_scalar_prefetch=1, grid=(N_CHIPS, nh),
        in_specs=[pl.BlockSpec((1, th, D), lambda j, i, c: (j, c[0] * nh + i, 0)), pl.BlockSpec((1, th, D), lambda j, i, c: (j, i, 0))],
        out_specs=pl.BlockSpec((1, th, D), lambda j, i, c: (j, i, 0)))
    return pl.pallas_call(body, name="rs_add_sibling", grid_spec=gs, out_shape=SDS((N_CHIPS, HALF, D), BF16),
                          compiler_params=pltpu.CompilerParams(dimension_semantics=("parallel", "parallel"),
                                                               vmem_limit_bytes=48 << 20))(cidx, gpk, got)


def _rs_chips_start(part, small, after):
    def body(p_ref, s_ref, land_ref, sland_ref, after_ref, send_sems, recv_sems, p_thru, s_thru, land_thru, sland_thru, token):
        x, y, c, chips = _place()
        for k, (cx, cy) in enumerate(chips):
            pltpu.make_async_remote_copy(src_ref=p_ref.at[2 * cx + cy], dst_ref=land_ref.at[2 * x + y], send_sem=send_sems.at[k],
                                         recv_sem=recv_sems.at[k], device_id=(cx, cy, c), device_id_type=MESH).start()
        peers = [(x, y, 1 - c)] + [(cx, cy, c) for cx, cy in chips] + [(cx, cy, 1 - c) for cx, cy in chips]
        for k, to in enumerate(peers):
            pltpu.make_async_remote_copy(src_ref=s_ref, dst_ref=sland_ref.at[4 * x + 2 * y + c], send_sem=send_sems.at[3 + k],
                                         recv_sem=recv_sems.at[3 + k], device_id=to, device_id_type=MESH).start()
        token[...] = jnp.zeros_like(token)

    return pl.pallas_call(
        body, name="rs_chips_start",
        out_shape=(pltpu.SemaphoreType.DMA((10,)), pltpu.SemaphoreType.DMA((10,)), pltpu.HBM(part.shape, part.dtype),
                   pltpu.HBM(small.shape, small.dtype), pltpu.HBM(part.shape, part.dtype), pltpu.HBM((N_DEV, 8, D), F32),
                   SDS((8, LANES), F32)),
        in_specs=(HBM, HBM, HBM, HBM, ANY), out_specs=(SEM, SEM, HBM, HBM, HBM, HBM, pl.BlockSpec(memory_space=pltpu.VMEM)),
        input_output_aliases={0: 2, 1: 3, 2: 4, 3: 5}, compiler_params=pltpu.CompilerParams(has_side_effects=DATAFLOW),
    )(_in_hbm(part), _in_hbm(small), _in_hbm(lax.empty(part.shape, part.dtype)), _in_hbm(lax.empty((N_DEV, 8, D), F32)), after)


def _rs_chips_wait(send_sems, recv_sems, p_thru, s_thru, land_thru, sland_thru, after):
    def body(p_ref, s_ref, land_ref, sland_ref, send_sems, recv_sems, *after_and_outputs):
        x, y, c, chips = _place()
        for k, (cx, cy) in enumerate(chips):
            cp = pltpu.make_async_remote_copy(src_ref=p_ref.at[0], dst_ref=land_ref.at[2 * cx + cy], send_sem=send_sems.at[k],
                                              recv_sem=recv_sems.at[k], device_id=(cx, cy, c), device_id_type=MESH)
            cp.wait_send()
            cp.wait_recv()
        peers = [(x, y, 1 - c)] + [(cx, cy, c) for cx, cy in chips] + [(cx, cy, 1 - c) for cx, cy in chips]
        for k, (px, py, pc) in enumerate(peers):
            cp = pltpu.make_async_remote_copy(src_ref=s_ref, dst_ref=sland_ref.at[4 * px + 2 * py + pc], send_sem=send_sems.at[3 + k],
                                              recv_sem=recv_sems.at[3 + k], device_id=(px, py, pc), device_id_type=MESH)
            cp.wait_send()
            cp.wait_recv()

    hbm = lambda a: pltpu.HBM(a.shape, a.dtype)
    outs = pl.pallas_call(
        body, name="rs_chips_wait", out_shape=(hbm(p_thru), hbm(s_thru), hbm(land_thru), hbm(sland_thru)),
        in_specs=(HBM, HBM, HBM, HBM, SEM, SEM) + (ANY,) * len(after), out_specs=(HBM, HBM, HBM, HBM),
        input_output_aliases={0: 0, 1: 1, 2: 2, 3: 3}, compiler_params=pltpu.CompilerParams(has_side_effects=DATAFLOW),
    )(p_thru, s_thru, land_thru, sland_thru, send_sems, recv_sems, *after)
    return outs[0], outs[2], outs[3]


def _rs_add_chips(qidx, part, parts):
    HALF = part.shape[1]
    th = HALF // 4
    nh = HALF // th
    assert th % 16 == 0

    def body(q_ref, own_ref, p_ref, o_ref):
        for me in range(N_CHIPS):
            @pl.when(q_ref[0] == me)
            def _(me=me):
                t = [(own_ref[0] if j == me else p_ref[j]).astype(F32) for j in range(N_CHIPS)]
                o_ref[...] = ((t[0] + t[1]) + t[2]) + t[3]

    gs = pltpu.PrefetchScalarGridSpec(
        num_scalar_prefetch=1, grid=(HALF // th,),
        in_specs=[pl.BlockSpec((1, th, D), lambda i, q: (q[0], i, 0)), pl.BlockSpec((N_CHIPS, th, D), lambda i, q: (0, i, 0))],
        out_specs=pl.BlockSpec((th, D), lambda i, q: (q[1] * nh + i, 0)))
    return pl.pallas_call(body, name="rs_add_chips", grid_spec=gs, out_shape=SDS((2 * HALF, D), F32),
                          compiler_params=pltpu.CompilerParams(dimension_semantics=("parallel",),
                                                               vmem_limit_bytes=48 << 20))(qidx, part, parts)


def _rs_join(shard, name):
    HALF = shard.shape[0] // 2

    def body(in_ref, out_ref, send_sem, recv_sem):
        x, y, c, _ = _place()
        rows = pl.ds(pl.multiple_of(c * HALF, 16), HALF)
        cp = pltpu.make_async_remote_copy(src_ref=in_ref.at[rows], dst_ref=out_ref.at[rows], send_sem=send_sem, recv_sem=recv_sem,
                                          device_id=(x, y, 1 - c), device_id_type=MESH)
        cp.start()
        cp.wait()

    return _pcall(body, name=name, in_specs=[ANY], out_specs=ANY, out_shape=SDS(shard.shape, F32),
                  scratch=[pltpu.SemaphoreType.DMA, pltpu.SemaphoreType.DMA], aliases={0: 0})(shard)


def _reduce_late_start(gpk, after):
    rows = gpk.shape[1]
    HALF = rows // 2
    assert HALF % 16 == 0

    def body(in_ref, land_ref, after_ref, send_sems, recv_sems, in_thru, land_thru, token):
        x, y, c, chips = _place()
        me = 4 * x + 2 * y + c
        peers = [(x, y, 1 - c)] + [(cx, cy, c) for cx, cy in chips] + [(cx, cy, 1 - c) for cx, cy in chips]
        for k, (px, py, pc) in enumerate(peers):
            src = in_ref.at[2 * px + py, pl.ds(pl.multiple_of(pc * HALF, 16), HALF)]
            pltpu.make_async_remote_copy(src_ref=src, dst_ref=land_ref.at[me], send_sem=send_sems.at[k], recv_sem=recv_sems.at[k],
                                         device_id=(px, py, pc), device_id_type=MESH).start()
        token[...] = jnp.zeros_like(token)

    return pl.pallas_call(
        body, name="reduce_late_start",
        out_shape=(pltpu.SemaphoreType.DMA((7,)), pltpu.SemaphoreType.DMA((7,)), pltpu.HBM(gpk.shape, gpk.dtype),
                   pltpu.HBM((N_DEV, HALF, D), gpk.dtype), SDS((8, LANES), F32)),
        in_specs=(HBM, HBM, ANY), out_specs=(SEM, SEM, HBM, HBM, pl.BlockSpec(memory_space=pltpu.VMEM)),
        input_output_aliases={0: 2, 1: 3}, compiler_params=pltpu.CompilerParams(has_side_effects=DATAFLOW),
    )(_in_hbm(gpk), _in_hbm(lax.empty((N_DEV, HALF, D), gpk.dtype)), after)


def _reduce_late_wait(send_sems, recv_sems, in_thru, land_thru, after):
    def body(in_ref, land_ref, send_sems, recv_sems, after_ref, in_out, got_ref):
        x, y, c, chips = _place()
        peers = [(x, y, 1 - c)] + [(cx, cy, c) for cx, cy in chips] + [(cx, cy, 1 - c) for cx, cy in chips]
        for k, (px, py, pc) in enumerate(peers):
            cp = pltpu.make_async_remote_copy(src_ref=land_ref.at[0], dst_ref=land_ref.at[4 * px + 2 * py + pc],
                                              send_sem=send_sems.at[k], recv_sem=recv_sems.at[k],
                                              device_id=(px, py, pc), device_id_type=MESH)
            cp.wait_send()
            cp.wait_recv()

    return pl.pallas_call(
        body, name="reduce_late_wait",
        out_shape=(pltpu.HBM(in_thru.shape, in_thru.dtype), pltpu.HBM(land_thru.shape, land_thru.dtype)),
        in_specs=(HBM, HBM, SEM, SEM, ANY), out_specs=(HBM, HBM), input_output_aliases={0: 0, 1: 1},
        compiler_params=pltpu.CompilerParams(has_side_effects=DATAFLOW),
    )(in_thru, land_thru, send_sems, recv_sems, after)


def _reduce_late_add(didx, gpk, parts):
    HALF = parts.shape[1]
    th = HALF // 4
    nh = HALF // th
    assert th % 16 == 0

    def body(d_ref, own_ref, p_ref, o_ref):
        for me in range(N_DEV):
            @pl.when(d_ref[0] == me)
            def _(me=me):
                t = [(own_ref[0] if j == me else p_ref[j]).astype(F32) for j in range(N_DEV)]
                o_ref[...] = ((((((t[0] + t[1]) + t[2]) + t[3]) + t[4]) + t[5]) + t[6]) + t[7]

    gs = pltpu.PrefetchScalarGridSpec(
        num_scalar_prefetch=1, grid=(nh,),
        in_specs=[pl.BlockSpec((1, th, D), lambda i, d: (d[1], d[2] * nh + i, 0)), pl.BlockSpec((N_DEV, th, D), lambda i, d: (0, i, 0))],
        out_specs=pl.BlockSpec((th, D), lambda i, d: (d[2] * nh + i, 0)))
    return pl.pallas_call(body, name="reduce_late_add", grid_spec=gs, out_shape=SDS((2 * HALF, D), F32),
                          compiler_params=pltpu.CompilerParams(dimension_semantics=("parallel",),
                                                               vmem_limit_bytes=48 << 20))(didx, gpk, parts)


def _pack_early(b, dtype):
    lanes = lambda a: jnp.pad(a.astype(dtype), ((0, 0), (0, D - a.shape[1])))
    pair = jnp.concatenate([b["w_q_b"].astype(dtype), b["w_ple"].astype(dtype), jnp.zeros((256, D - 640), dtype)], axis=1)
    return jnp.concatenate([lanes(b["w_in"]), pair, lanes(b["w_kv_b"])], axis=0)


def _pack_late(b, dtype):
    return jnp.concatenate([b[n].astype(dtype) for n in ("w_mla_up", "w_swa_up", "w_out", "w_ple_gate", "w_mlp_up", "w_mlp_down")],
                           axis=0)


def _unpack_shards(pk, which):
    return {n: pk[PACK_AT[n][1]:PACK_AT[n][1] + r, PACK_AT[n][2]:PACK_AT[n][2] + c] for n, r, c in BIG if PACK_AT[n][0] == which}


def _full_weights(gathered, own, chip, which):
    own_b = _unpack_shards(own, which)
    per_chip = [{n: jnp.where(chip == j, own_b[n], blk) for n, blk in _unpack_shards(gathered[j], which).items()}
                for j in range(N_CHIPS)]
    out = {}
    for n in own_b:
        shards = [pc[n] for pc in per_chip]
        if n == "w_in":
            out["w_in_p"] = _w_in_internal(shards)
        else:
            out[n] = jnp.concatenate(shards, axis=1 if n in COL_SHARDED else 0)
    return out


def _split_full_grads(grads, pack, dtype):
    shard = {n: (r, c) for n, r, c in BIG}
    chunks = []
    for j in range(N_CHIPS):
        blocks = {}
        for n, g in grads.items():
            if n == "w_in_p":
                blocks["w_in"] = _w_in_grad_shard(g, j)
                continue
            r, c = shard[n]
            blocks[n] = g[:, j * c:(j + 1) * c] if n in COL_SHARDED else g[j * r:(j + 1) * r]
        chunks.append(pack(blocks, dtype))
    return jnp.stack(chunks)


W_IN_SHARD = 936
W_IN_SEGMENTS = ((0, 256, (3072,)), (256, 384, (3840,)), (384, 416, (4032,)), (416, 1440, (0,)), (1440, 1504, (3328, 3392)),
                 (1504, 1568, (3456, 3520)), (1568, 1632, (3584, 3648)), (1632, 1696, (3712, 3776)), (1696, 3744, (1024,)))


def _w_in_internal(shards):
    def cols(a, b):
        out = []
        for j, s in enumerate(shards):
            lo, hi = max(a, W_IN_SHARD * j), min(b, W_IN_SHARD * (j + 1))
            if lo < hi:
                out.append(s[:, lo - W_IN_SHARD * j:hi - W_IN_SHARD * j])
        return out

    pieces = {}
    for a, b, places in W_IN_SEGMENTS:
        for at in places:
            pieces[at] = cols(a, b)
    zeros = lambda n: [jnp.zeros((D, n), shards[0].dtype)]
    pieces[3968] = zeros(64)
    pieces[4064] = zeros(32)
    return jnp.concatenate([piece for at in sorted(pieces) for piece in pieces[at]], axis=1)


def _w_in_grad_shard(g, j):
    def internal(a, b):
        out = []
        while a < b:
            end = min(b, (a // D + 1) * D)
            out.append(g[a // D][:, a % D:a % D + end - a])
            a = end
        return out

    out = []
    for a, b, places in W_IN_SEGMENTS:
        lo, hi = max(a, W_IN_SHARD * j), min(b, W_IN_SHARD * (j + 1))
        if lo < hi:
            parts = [internal(at + lo - a, at + hi - a) for at in places]
            if len(parts) == 1:
                out += parts[0]
            else:
                assert len(parts[0]) == len(parts[1]) == 1
                out.append(parts[0][0] + parts[1][0])
    return jnp.concatenate(out, axis=1)


def _local_step(x, p, tgt, w, small, late_weights, late_grads_out):
    T = x.shape[0]
    tm = 256
    tb = 256
    w_in_p = w["w_in_p"]
    wqb = jnp.pad(w["w_q_b"].reshape(Q_LORA, MLA_HEADS, 96), ((0, 0), (0, 0), (0, 32))).reshape(Q_LORA, 2048)
    wkv = w["w_kv_b"].reshape(KV_LORA, MLA_HEADS, 128)
    wkn = jnp.pad(wkv[:, :, :64], ((0, 0), (0, 0), (0, 64))).reshape(KV_LORA, 2048)
    wv = wkv[:, :, 64:].reshape(KV_LORA, 1024)
    tab_m = _rope_tables(T, "mla")
    tab_s = _rope_tables(T, "swa")
    g1, gq, gkv, sinks = small["g_mix_pre"], small["g_q_a"], small["g_kv_a"], small["sinks"]
    g2, g3, g4, g5 = small["g_mix_post"], small["g_mlp_pre"], small["g_mlp_post"], small["g_ple"]
    sink_vec = sinks.reshape(SWA_HEADS)

    z, h1 = _fwd_in(x, g1, w_in_p, tm)
    qn, kvn, km, vm, qt, kt, vt, qs, ks, vs = _fwd_qkv(z, gq, gkv, wqb, wkn, wv, tab_m, tab_s, tb)
    om, lse_m = _mla_fwd(qt, km, vt, tb)
    os_, lse_s = _swa_fwd(sink_vec, qs, ks, vs)
    w = {**w, **late_weights((om, os_))}
    y, yo, au, bu, x1 = _fwd_mix(om, os_, z, x, w["w_mla_up"], w["w_swa_up"], w["w_out"], g2, tm)
    h2, u = _fwd_mlp_up(x1, g3, w["w_mlp_up"], tm)
    d, x2 = _fwd_mlp_down(u, w["w_mlp_down"], x1, g4, tm)
    loss, dx2, dgt, de0, dg5 = _ple_fwd_bwd(p, x2, tgt, w["w_ple"], g5, w["w_ple_gate"], tm)

    dd, da, dg4 = _bwd_mlp_down(dx2, d, g4, w["w_mlp_down"], u, tm)
    dx1, dg3 = _bwd_mlp_up(da, w["w_mlp_up"], x1, g3, dx2, tm)
    dyo, dg2, dau, dbu, dga, dgb, dos, delta_m, dom_t = _bwd_mix(dx1, yo, g2, w["w_out"], z, au, bu, w["w_mla_up"],
                                                                w["w_swa_up"], om, tb)
    gpk_late = lax.empty((N_CHIPS, PACK_ROWS["late"], D), BF16)
    for weight, a_, g_ in (("w_mla_up", om, dau), ("w_swa_up", os_, dbu), ("w_out", y, dyo), ("w_ple_gate", x2, dgt),
                           ("w_mlp_up", h2, da), ("w_mlp_down", u, dd)):
        gpk_late = _wgrad(a_, g_, "wgrad_" + weight[2:], into=(gpk_late, weight))
    token = late_grads_out(gpk_late)
    delta_m = delta_m + token[0, 0]
    dqm, dkm, dvm = _mla_bwd(qt, km, kt, vm, dom_t, lse_m, delta_m, tb)
    dqs, dkc, dkp, dvc, dvp, dsink = _swa_bwd(sink_vec, qs, ks, vs, dos, os_, lse_s)
    dqb, dknb, dvb, dsq, drest, dgq, dgkv = _bwd_qkv(dqm, dkm, dvm, dqs, dkc, dkp, dvc, dvp, z, gq, gkv, wqb, wkn, wv,
                                                      tab_m, tab_s)
    gx, dg1 = _bwd_in(dsq, dga, dgb, drest, w_in_p, x, g1, dx1, tm)

    g_in_p = [_wgrad(h1, dsq, "wgrad_in_sq"), _wgrad(h1, dga, "wgrad_in_ga"), _wgrad(h1, dgb, "wgrad_in_gb"),
              _wgrad(h1, drest, "wgrad_in_rest")]
    g_qb_p = _wgrad_t(dqb, qn, "wgrad_q_b").T
    g_kn_p = _wgrad_t(dknb, kvn, "wgrad_kv_b_nope").T
    g_v_p = _wgrad_t(dvb, kvn, "wgrad_kv_b_v").T
    grads = {
        "w_in_p": g_in_p,
        "w_q_b": g_qb_p.reshape(Q_LORA, MLA_HEADS, 128)[:, :, :96].reshape(Q_LORA, 1536),
        "w_kv_b": jnp.concatenate([g_kn_p.reshape(KV_LORA, MLA_HEADS, 128)[:, :, :64], g_v_p.reshape(KV_LORA, MLA_HEADS, 64)],
                                  axis=2).reshape(KV_LORA, 2048),
        "w_ple": _wgrad(p, de0, "wgrad_ple"),
    }
    small_grads = {"g_mix_pre": dg1, "g_q_a": dgq.reshape(1, Q_LORA), "g_kv_a": dgkv.reshape(1, KV_LORA), "sinks": dsink[0:1, 0:SWA_HEADS], "g_mix_post": dg2,
                   "g_mlp_pre": dg3, "g_mlp_post": dg4, "g_ple": dg5}
    return loss, gx, grads, small_grads


def _pack_small(vals, fill, scalar=None):
    wide = [vals[n] for n, k in SMALL if k == D]
    narrow = [vals[n] for n, k in SMALL if k != D]
    used = sum(k for _, k in SMALL if k != D)
    last = jnp.concatenate(narrow + [jnp.full((1, D - used), fill, F32)], axis=1)
    rest = jnp.full((2, D), fill, F32)
    if scalar is not None:
        rest = jnp.concatenate([jnp.concatenate([scalar, rest[0:1, 1:]], axis=1), rest[1:2]], axis=0)
    return jnp.concatenate(wide + [last, rest], axis=0)


def _unpack_small(pk):
    out, row, off = {}, 0, 0
    for n, k in SMALL:
        if k == D:
            out[n] = pk[row:row + 1]
            row += 1
    for n, k in SMALL:
        if k != D:
            out[n] = pk[5:6, off:off + k]
            off += k
    return out


def kernel(x, p, g_mix_pre, w_in, g_q_a, w_q_b, g_kv_a, w_kv_b, sinks, w_mla_up, w_swa_up, w_out, g_mix_post, g_mlp_pre, w_mlp_up, w_mlp_down, g_mlp_post, w_ple, g_ple, w_ple_gate, loss_target, m_g_mix_pre, m_w_in, m_g_q_a, m_w_q_b, m_g_kv_a, m_w_kv_b, m_sinks, m_w_mla_up, m_w_swa_up, m_w_out, m_g_mix_post, m_g_mlp_pre, m_w_mlp_up, m_w_mlp_down, m_g_mlp_post, m_w_ple, m_g_ple, m_w_ple_gate, v_g_mix_pre, v_w_in, v_g_q_a, v_w_q_b, v_g_kv_a, v_w_kv_b, v_sinks, v_w_mla_up, v_w_swa_up, v_w_out, v_g_mix_post, v_g_mlp_pre, v_w_mlp_up, v_w_mlp_down, v_g_mlp_post, v_w_ple, v_g_ple, v_w_ple_gate):
    given = dict(locals())
    big_w = {n: given[n][0] for n, _, _ in BIG}
    small_w = {n: given[n] for n, _ in SMALL}
    small_m = {n: given["m_" + n] for n, _ in SMALL}
    small_v = {n: given["v_" + n] for n, _ in SMALL}

    core = lax.axis_index("c")
    chip = 2 * lax.axis_index("x") + lax.axis_index("y")
    core_i = core.astype(jnp.int32).reshape(1)
    dev_i = jnp.stack([2 * chip + core, chip, core]).astype(jnp.int32)

    own_early = _pack_early(big_w, BF16)
    own_late = _pack_late(big_w, BF16)
    got_early = _all_gather(own_early)
    late_flight = _gather_late_start(own_late, got_early)
    weights = _full_weights(got_early, own_early, chip, "early")
    step_small = {**small_w, "g_mix_pre": small_w["g_mix_pre"] + late_flight[4][0, 0]}

    def late_weights(after):
        return _full_weights(_gather_late_wait(*late_flight[:4], after), own_late, chip, "late")

    flight = {}

    def late_grads_out(gpk_late):
        flight["late"] = _reduce_late_start(gpk_late, dev_i)
        return flight["late"][4]

    loss_blk, gx, grads, small_grads = _local_step(x[0], p[0, 0], loss_target[0], weights, step_small, late_weights,
                                                   late_grads_out)

    gpk = _split_full_grads(grads, _pack_early, F32)
    got = _rs_sibling(gpk)
    part = _rs_add_sibling(core_i, gpk, got)
    small_own = _pack_small(small_grads, 0.0, loss_blk[0:1, 0:1])
    early_flight = _rs_chips_start(part, small_own, dev_i)

    out_g, out_d, out_m, out_v = {}, {}, {}, {}
    gpk_late, parts_late = _reduce_late_wait(*flight["late"][:4], early_flight[6])
    joined_late = _rs_join(_reduce_late_add(dev_i, gpk_late, parts_late), "rs_join_late")
    for n, _, _ in BIG:
        if PACK_AT[n][0] == "late":
            out_g[n], out_d[n], out_m[n], out_v[n] = _adamw(given[n], joined_late, given["m_" + n], given["v_" + n], n)

    part, parts, small_parts = _rs_chips_wait(*early_flight[:6], [out_d[n] for n in out_d])
    joined_early = _rs_join(_rs_add_chips(dev_i[1:3], part, parts), "rs_join_early")
    for n, _, _ in BIG:
        if PACK_AT[n][0] == "early":
            out_g[n], out_d[n], out_m[n], out_v[n] = _adamw(given[n], joined_early, given["m_" + n], given["v_" + n], n)

    mine = (lax.broadcasted_iota(jnp.int32, (N_DEV, 1, 1), 0) == dev_i[0])
    g_small_pk, d_small_pk, m_small_pk, v_small_pk = _adamw_small(
        _pack_small(small_w, 0.0), jnp.where(mine, small_own[None], small_parts), _pack_small(small_m, 0.0),
        _pack_small(small_v, 1.0))
    loss = g_small_pk[6, 0]
    for out, pk in ((out_g, g_small_pk), (out_d, d_small_pk), (out_m, m_small_pk), (out_v, v_small_pk)):
        out.update(_unpack_small(pk))
    order = ["g_mix_pre", "w_in", "g_q_a", "w_q_b", "g_kv_a", "w_kv_b", "sinks", "w_mla_up", "w_swa_up", "w_out", "g_mix_post",
             "g_mlp_pre", "w_mlp_up", "w_mlp_down", "g_mlp_post", "w_ple", "g_ple", "w_ple_gate"]
    return (loss, gx[None], *[out_g[n] for n in order], *[out_d[n] for n in order], *[out_m[n] for n in order],
            *[out_v[n] for n in order])
```

```python
import math

import jax
import jax.numpy as jnp
from jax import lax
from jax.experimental import pallas as pl
from jax.experimental.pallas import tpu as pltpu

F32 = jnp.float32
BF16 = jnp.bfloat16
SDS = jax.ShapeDtypeStruct

D = 1024
D_FF = 4096
PLE = 256
Q_LORA = 256
KV_LORA = 128
MLA_HEADS = 16
MLA_NOPE = 64
MLA_ROPE = 32
SWA_HEADS = 16
SWA_HD = 64
WINDOW = 128
ROPE_THETA = 10000.0
EPS = 1e-6
NEG = -1e30
NZ = 4096
MLA_SCALE = (MLA_NOPE + MLA_ROPE) ** -0.5
LOG2_E = math.log2(math.e)
MLA_LOG2_SCALE = MLA_SCALE * LOG2_E
SWA_SCALE = SWA_HD ** -0.5

ADAM_LR = 0.001
ADAM_B1 = 0.9
ADAM_B2 = 0.999
ADAM_EPS = 1e-08
ADAM_WD = 0.01
ADAM_STEP = 10

LANES = 128
ATT_COLS = 128
VT_ROWS = 80
N_CHIPS = 4
N_DEV = 8
MESH = pl.DeviceIdType.MESH

NT = (((1,), (1,)), ((), ()))
TN = (((0,), (0,)), ((), ()))

BIG = (("w_in", 1024, 936), ("w_q_b", 256, 384), ("w_kv_b", 128, 512), ("w_mla_up", 256, 1024),
       ("w_swa_up", 256, 1024), ("w_out", 256, 1024), ("w_mlp_up", 1024, 1024), ("w_mlp_down", 1024, 1024),
       ("w_ple", 256, 256), ("w_ple_gate", 256, 1024))
COL_SHARDED = ("w_in", "w_q_b", "w_kv_b", "w_mlp_up", "w_ple")
PACK_AT = {"w_in": ("early", 0, 0), "w_q_b": ("early", 1024, 0), "w_ple": ("early", 1024, 384), "w_kv_b": ("early", 1280, 0),
           "w_mla_up": ("late", 0, 0), "w_swa_up": ("late", 256, 0), "w_out": ("late", 512, 0), "w_ple_gate": ("late", 768, 0),
           "w_mlp_up": ("late", 1024, 0), "w_mlp_down": ("late", 2048, 0)}
PACK_ROWS = {"early": 1408, "late": 3072}
SMALL = (("g_mix_pre", 1024), ("g_q_a", 256), ("g_kv_a", 128), ("sinks", 16), ("g_mix_post", 1024),
         ("g_mlp_pre", 1024), ("g_mlp_post", 1024), ("g_ple", 1024))


def _dot(a, b):
    return jnp.dot(a, b, preferred_element_type=F32)


def _dot_nt(a, b):
    return lax.dot_general(a, b, NT, preferred_element_type=F32)


def _dot_tn(a, b):
    return lax.dot_general(a, b, TN, preferred_element_type=F32)


def _pcall(body, *, name, out_shape, grid=(), in_specs=None, out_specs=None, scratch=(), sem=None, vmem_mb=48, aliases=None):
    params = dict(vmem_limit_bytes=vmem_mb << 20)
    if sem is not None:
        params["dimension_semantics"] = sem
    return pl.pallas_call(body, name=name, grid=grid, in_specs=in_specs, out_specs=out_specs, out_shape=out_shape,
                          scratch_shapes=list(scratch), input_output_aliases=aliases or {},
                          compiler_params=pltpu.CompilerParams(**params))


def _rows(tm, n, col=0):
    return pl.BlockSpec((tm, n), lambda i: (i, col))


def _full(shape):
    return pl.BlockSpec(shape, lambda i: (0,) * len(shape))


def _rms(x, g):
    r = lax.rsqrt(jnp.mean(x * x, axis=-1, keepdims=True) + EPS)
    return x * r * g


def _rms_bwd(dy, x, g):
    r = lax.rsqrt(jnp.mean(x * x, axis=-1, keepdims=True) + EPS)
    xn = x * r
    dn = dy * g
    dx = r * (dn - xn * jnp.mean(dn * xn, axis=-1, keepdims=True))
    return dx, jnp.sum(dy * xn, axis=0, keepdims=True)


def _sigmoid(x):
    return 1.0 / (1.0 + jnp.exp(-x))


def _rope(x, c, a, b, half):
    return x * c + pltpu.roll(x, LANES - half, 1) * a + pltpu.roll(x, half, 1) * b


def _rope_tables(T, kind):
    lane = jnp.arange(LANES)
    if kind == "mla":
        half = MLA_ROPE // 2
        rel = lane - MLA_NOPE
        on = (rel >= 0) & (rel < MLA_ROPE)
        d = MLA_ROPE
    else:
        half = SWA_HD // 2
        rel = lane % SWA_HD
        on = jnp.ones((LANES,), bool)
        d = SWA_HD
    first = on & (rel < half)
    second = on & (rel >= half)
    f = jnp.where(first, rel, rel - half).astype(F32)
    inv = jnp.exp(-math.log(ROPE_THETA) * f * (2.0 / d))
    ang = jnp.arange(T, dtype=F32)[:, None] * inv[None, :]
    cos, sin = jnp.cos(ang), jnp.sin(ang)
    c = jnp.where(on[None], cos, 1.0)
    a = jnp.where(first[None], -sin, 0.0)
    b = jnp.where(second[None], sin, 0.0)
    return c, a, b


def _fwd_in(x, g1, w_in_p, tm):
    T = x.shape[0]

    def body(x_ref, g_ref, w_ref, z_ref, h_ref):
        h = _rms(x_ref[...], g_ref[...]).astype(BF16)
        h_ref[...] = h
        z_ref[...] = _dot(h, w_ref[...])

    return _pcall(body, name="fwd_in", grid=(T // tm,),
                  in_specs=[_rows(tm, D), _full((1, D)), _full((D, NZ))],
                  out_specs=[_rows(tm, NZ), _rows(tm, D)],
                  out_shape=[SDS((T, NZ), F32), SDS((T, D), BF16)], sem=("parallel",))(x, g1, w_in_p)


def _fwd_qkv(z, gq, gkv, wqb, wkn, wv, tab_m, tab_s, tm):
    T = z.shape[0]
    wqb_t, wkn_t, wv_t = wqb.T, wkn.T, wv.T
    tab_mt = [t.T for t in tab_m]

    def body(qa_ref, sq_ref, skd_ref, svd_ref, kva_ref, kr_ref, gq_ref, gkv_ref, wkn_ref, wv_ref, wqbt_ref, wknt_ref, wvt_ref,
             cm_ref, am_ref, bm_ref, cmt_ref, amt_ref, bmt_ref, cs_ref, as_ref, bs_ref,
             qn_ref, kvn_ref, km_ref, vm_ref, qt_ref, kt_ref, vt_ref, qs_ref, ks_ref, vs_ref):
        qn = _rms(qa_ref[...], gq_ref[...])
        qn_ref[...] = qn.astype(BF16)
        kvn = _rms(kva_ref[...], gkv_ref[...])
        kvn_b = kvn.astype(BF16)
        kvn_ref[...] = kvn_b
        qn_t = qn.T.astype(BF16)
        kvn_t = kvn.T.astype(BF16)
        cm, am, bm = cm_ref[...], am_ref[...], bm_ref[...]
        cmt, amt, bmt = cmt_ref[...], amt_ref[...], bmt_ref[...]
        cs, as_, bs = cs_ref[...], as_ref[...], bs_ref[...]
        k_rope = _rope(kr_ref[...], cm, am, bm, MLA_ROPE // 2)
        k_rope_t = k_rope.T
        half = MLA_ROPE // 2
        vm_ref[...] = _dot(kvn_b, wv_ref[...]).astype(BF16)
        km_all = _dot(kvn_b, wkn_ref[...])
        v_t = _dot(wvt_ref[...], kvn_t)
        q_t = _dot(wqbt_ref[...], qn_t)
        k_t = _dot(wknt_ref[...], kvn_t)
        ones_row = jnp.where(lax.broadcasted_iota(jnp.int32, (64, tm), 0) == 0, 1.0, 0.0)
        for h in range(MLA_HEADS):
            sl = slice(LANES * h, LANES * (h + 1))
            vt_ref[0, sl, :] = jnp.concatenate([v_t[64 * h:64 * (h + 1)], ones_row], axis=0).astype(BF16)
            qh = q_t[sl]
            qt_ref[0, sl, :] = (qh * cmt + pltpu.roll(qh, LANES - half, 0) * amt + pltpu.roll(qh, half, 0) * bmt).astype(BF16)
            km_ref[:, sl] = (km_all[:, sl] + k_rope).astype(BF16)
            kt_ref[0, sl, :] = (k_t[sl] + k_rope_t).astype(BF16)
        for j in range(D // LANES):
            sl = slice(LANES * j, LANES * (j + 1))
            qs_ref[:, sl] = _rope(sq_ref[:, sl], cs, as_, bs, SWA_HD // 2).astype(BF16)
        for j in range(2):
            sl = slice(LANES * j, LANES * (j + 1))
            ks_ref[:, sl] = _rope(skd_ref[:, sl], cs, as_, bs, SWA_HD // 2).astype(BF16)
        vs_ref[...] = svd_ref[...].astype(BF16)

    tab = [_rows(tm, LANES)] * 3
    tab_t = [pl.BlockSpec((LANES, tm), lambda i: (0, i))] * 3
    return _pcall(body, name="fwd_qkv", grid=(T // tm,),
                  in_specs=[_rows(tm, 256, 12), _rows(tm, 1024, 0), _rows(tm, 256, 13), _rows(tm, 256, 14),
                            _rows(tm, 128, 30), _rows(tm, 128, 31), _full((1, Q_LORA)), _full((1, KV_LORA)),
                            _full((KV_LORA, 2048)), _full((KV_LORA, 1024)), _full((2048, Q_LORA)), _full((2048, KV_LORA)),
                            _full((1024, KV_LORA))] + tab + tab_t + tab,
                  out_specs=[_rows(tm, Q_LORA), _rows(tm, KV_LORA), _rows(tm, 2048), _rows(tm, 1024),
                             pl.BlockSpec((1, 2048, tm), lambda i: (i, 0, 0)), pl.BlockSpec((1, 2048, tm), lambda i: (i, 0, 0)),
                             pl.BlockSpec((1, 2048, tm), lambda i: (i, 0, 0)),
                             _rows(tm, 1024), _rows(tm, 256), _rows(tm, 256)],
                  out_shape=[SDS((T, Q_LORA), BF16), SDS((T, KV_LORA), BF16), SDS((T, 2048), BF16),
                             SDS((T, 1024), BF16), SDS((T // tm, 2048, tm), BF16), SDS((T // tm, 2048, tm), BF16),
                             SDS((T // tm, 2048, tm), BF16),
                             SDS((T, 1024), BF16), SDS((T, 256), BF16), SDS((T, 256), BF16)],
                  sem=("parallel",))(z, z, z, z, z, z, gq, gkv, wkn, wv, wqb_t, wkn_t, wv_t, *tab_m, *tab_mt, *tab_s)


def _mla_fwd(qt, km, vt, tb):
    T = km.shape[0]
    nb = T // tb
    cc = ATT_COLS

    per = 2 if nb % 2 == 0 else 1

    def body(q_ref, k_ref, vt_ref, o_ref, l_ref, s_ref, p_ref, al_ref, m_ref, acc_ref):
        for blk in range(per):
            one_block(per * pl.program_id(1) + blk, blk, q_ref, k_ref, vt_ref, o_ref, l_ref, s_ref, p_ref, al_ref, m_ref, acc_ref)

    def one_block(i, blk, q_ref, k_ref, vt_ref, o_ref, l_ref, s_ref, p_ref, al_ref, m_ref, acc_ref):
        m_ref[...] = jnp.full(m_ref.shape, NEG, F32)
        acc_ref[...] = jnp.zeros_like(acc_ref)
        p_ref[1] = jnp.zeros(p_ref.shape[1:], BF16)
        al_ref[1] = jnp.ones(al_ref.shape[1:], F32)
        key = lax.broadcasted_iota(jnp.int32, (tb, cc), 0)
        qry = lax.broadcasted_iota(jnp.int32, (tb, cc), 1)

        def scores(j, slot):
            off = pl.multiple_of(j * tb, tb)
            for hh in range(2):
                sl = slice(LANES * hh, LANES * (hh + 1))
                s_ref[slot, hh] = _dot(k_ref[pl.ds(off, tb), sl], q_ref[blk, sl, :])

        def softmax(slot, diagonal):
            chains = [(hh, slice(cc * c, cc * (c + 1)), c) for hh in range(2) for c in range(tb // cc)]

            def scaled(hh, cols, c):
                t = s_ref[slot, hh, :, cols] * MLA_LOG2_SCALE
                return jnp.where(key <= qry + cc * c, t, NEG) if diagonal else t

            tops = []
            for hh, cols, c in chains:
                if diagonal:
                    top = jnp.max(scaled(hh, cols, c), axis=0, keepdims=True)
                else:
                    top = jnp.max(s_ref[slot, hh, :, cols], axis=0, keepdims=True) * MLA_LOG2_SCALE
                m_old = m_ref[hh, :, cols]
                mn = jnp.maximum(m_old, top)
                m_ref[hh, :, cols] = mn
                al_ref[slot, hh, :, cols] = jnp.exp2(m_old - mn)
                tops.append(mn)
            for (hh, cols, c), mn in zip(chains, tops):
                p_ref[slot, hh, :, cols] = jnp.exp2(scaled(hh, cols, c) - mn).astype(BF16)

        def accumulate(j, slot):
            for hh in range(2):
                acc_ref[hh] = al_ref[slot, hh] * acc_ref[hh] + _dot(vt_ref[j, LANES * hh:LANES * hh + VT_ROWS, :], p_ref[slot, hh])

        def step(t, carry):
            scores(2 * t + 1, 1)
            accumulate(jnp.maximum(2 * t - 1, 0), 1)
            softmax(0, False)
            scores(2 * t + 2, 0)
            accumulate(2 * t, 0)
            softmax(1, False)
            return carry

        scores(0, 0)
        lax.fori_loop(0, i // 2, step, 0)

        @pl.when(i % 2 == 1)
        def _():
            scores(i, 1)
            accumulate(jnp.maximum(i - 2, 0), 1)
            softmax(0, False)
            accumulate(i - 1, 0)
            softmax(1, True)
            accumulate(i, 1)

        @pl.when(i % 2 == 0)
        def _():
            accumulate(jnp.maximum(i - 1, 0), 1)
            softmax(0, True)
            accumulate(i, 0)
        den = [acc_ref[hh, 64:65, :] for hh in range(2)]
        o_ref[tb * blk:tb * (blk + 1), :] = jnp.concatenate([acc_ref[hh, 0:64, :] / den[hh] for hh in range(2)], axis=0).T
        sub = lax.broadcasted_iota(jnp.int32, (8, tb), 0)
        lse = [m_ref[hh] + jnp.log(den[hh]) * LOG2_E for hh in range(2)]
        l_ref[0, blk] = jnp.where(sub == 0, lse[0], jnp.where(sub == 1, lse[1], 0.0))

    return _pcall(body, name="mla_fwd", grid=(MLA_HEADS // 2, nb // per),
                  in_specs=[pl.BlockSpec((per, 256, tb), lambda p, i: (i, p, 0)), pl.BlockSpec((T, 256), lambda p, i: (0, p)),
                            pl.BlockSpec((nb, 2 * LANES, tb), lambda p, i: (0, p, 0))],
                  out_specs=[pl.BlockSpec((per * tb, LANES), lambda p, i: (i, p)),
                             pl.BlockSpec((1, per, 8, tb), lambda p, i: (p, i, 0, 0))],
                  out_shape=[SDS((T, D), F32), SDS((MLA_HEADS // 2, nb, 8, tb), F32)],
                  scratch=[pltpu.VMEM((2, 2, tb, tb), F32), pltpu.VMEM((2, 2, tb, tb), BF16), pltpu.VMEM((2, 2, 1, tb), F32),
                           pltpu.VMEM((2, 1, tb), F32), pltpu.VMEM((2, VT_ROWS, tb), F32)],
                  sem=("parallel", "arbitrary"))(qt, km, vt)


def _swa_mask(n):
    row = lax.broadcasted_iota(jnp.int32, (WINDOW, 2 * WINDOW), 0)
    col = lax.broadcasted_iota(jnp.int32, (WINDOW, 2 * WINDOW), 1)
    rel = row - col + WINDOW
    return (rel >= 0) & (rel < WINDOW) & ((col >= WINDOW) | (n > 0))


def _swa_specs(T):
    nb = T // WINDOW
    cur = lambda w: pl.BlockSpec((WINDOW, w), lambda n: (n, 0))
    prev = lambda w: pl.BlockSpec((WINDOW, w), lambda n: (jnp.maximum(n - 1, 0), 0))
    return nb, cur, prev


def _swa_fwd(sinks, qs, ks, vs):
    T = qs.shape[0]
    nb, cur, prev = _swa_specs(T)

    def body(sink_ref, q_ref, kc_ref, kp_ref, vc_ref, vp_ref, o_ref, l_ref, kb_ref, vb_ref, s_ref, p_ref):
        n = pl.program_id(0)
        mask = _swa_mask(n)
        lo = lax.broadcasted_iota(jnp.int32, (WINDOW, LANES), 1) < 64
        hi = jnp.logical_not(lo)
        for g in range(2):
            gs = slice(LANES * g, LANES * (g + 1))
            kb_ref[g] = jnp.concatenate([kp_ref[:, gs], kc_ref[:, gs]], axis=0)
            vb_ref[g] = jnp.concatenate([vp_ref[:, gs], vc_ref[:, gs]], axis=0)
        for h in range(SWA_HEADS):
            qp = q_ref[:, LANES * (h // 2):LANES * (h // 2 + 1)]
            qh = jnp.where(lo if h % 2 == 0 else hi, qp, jnp.zeros_like(qp))
            s_ref[h] = _dot_nt(qh, kb_ref[h // 8])
        for j in range(SWA_HEADS // 2):
            sl = slice(LANES * j, LANES * (j + 1))
            lses = []
            for h in (2 * j, 2 * j + 1):
                s = jnp.where(mask, s_ref[h] * SWA_SCALE, NEG)
                sk = sink_ref[h]
                m = jnp.maximum(jnp.max(s, axis=1, keepdims=True), sk)
                e = jnp.exp(s - m)
                den = jnp.sum(e, axis=1, keepdims=True) + jnp.exp(sk - m)
                p_ref[h] = (e / den).astype(BF16)
                lses.append(jnp.broadcast_to(m + jnp.log(den), (WINDOW, LANES)))
            l_ref[:, sl] = jnp.where(lo, lses[0], lses[1])
        for j in range(SWA_HEADS // 2):
            vb = vb_ref[j // 4]
            o_ref[:, LANES * j:LANES * (j + 1)] = jnp.where(lo, _dot(p_ref[2 * j], vb), _dot(p_ref[2 * j + 1], vb))

    return _pcall(body, name="swa_fwd", grid=(nb,),
                  in_specs=[pl.BlockSpec(memory_space=pltpu.SMEM), cur(D), cur(256), prev(256), cur(256), prev(256)],
                  out_specs=[cur(D), cur(D)], out_shape=[SDS((T, D), F32)] * 2,
                  scratch=[pltpu.VMEM((2, 2 * WINDOW, LANES), BF16), pltpu.VMEM((2, 2 * WINDOW, LANES), BF16),
                           pltpu.VMEM((SWA_HEADS, WINDOW, 2 * WINDOW), F32), pltpu.VMEM((SWA_HEADS, WINDOW, 2 * WINDOW), BF16)],
                  sem=("parallel",))(sinks, qs, ks, ks, vs, vs)


def _fwd_mix(om, os_, z, x, wmu, wsu, wo, g2, tm):
    T = x.shape[0]

    def body(om_ref, os_ref, ga_ref, gb_ref, x_ref, wmu_ref, wsu_ref, wo_ref, g2_ref,
             y_ref, yo_ref, au_ref, bu_ref, x1_ref):
        au = _dot(om_ref[...].astype(BF16), wmu_ref[...])
        bu = _dot(os_ref[...].astype(BF16), wsu_ref[...])
        au_ref[...] = au
        bu_ref[...] = bu
        y = (_sigmoid(ga_ref[...]) * au + _sigmoid(gb_ref[...]) * bu).astype(BF16)
        y_ref[...] = y
        yo = _dot(y, wo_ref[...])
        yo_ref[...] = yo
        x1_ref[...] = x_ref[...] + _rms(yo, g2_ref[...])

    r = _rows(tm, D)
    w = _full((D, D))
    return _pcall(body, name="fwd_mix", grid=(T // tm,),
                  in_specs=[r, r, _rows(tm, D, 1), _rows(tm, D, 2), r, w, w, w, _full((1, D))],
                  out_specs=[r] * 5,
                  out_shape=[SDS((T, D), BF16), SDS((T, D), F32), SDS((T, D), F32), SDS((T, D), F32), SDS((T, D), F32)],
                  sem=("parallel",))(om, os_, z, z, x, wmu, wsu, wo, g2)


def _fwd_mlp_up(x1, g3, w1, tm):
    T = x1.shape[0]

    def body(x_ref, g_ref, w_ref, h_ref, u_ref):
        h = _rms(x_ref[...], g_ref[...]).astype(BF16)
        h_ref[...] = h
        u_ref[...] = jnp.square(jnp.maximum(_dot(h, w_ref[...]), 0.0)).astype(BF16)

    return _pcall(body, name="fwd_mlp_up", grid=(T // tm,),
                  in_specs=[_rows(tm, D), _full((1, D)), _full((D, D_FF))],
                  out_specs=[_rows(tm, D), _rows(tm, D_FF)],
                  out_shape=[SDS((T, D), BF16), SDS((T, D_FF), BF16)],
                  sem=("parallel",))(x1, g3, w1)


def _fwd_mlp_down(u, w2, x1, g4, tm):
    T = x1.shape[0]

    def body(u_ref, w_ref, x_ref, g_ref, d_ref, x2_ref):
        d = _dot(u_ref[...], w_ref[...])
        d_ref[...] = d
        x2_ref[...] = x_ref[...] + _rms(d, g_ref[...])

    return _pcall(body, name="fwd_mlp_down", grid=(T // tm,),
                  in_specs=[_rows(tm, D_FF), _full((D_FF, D)), _rows(tm, D), _full((1, D))],
                  out_specs=[_rows(tm, D), _rows(tm, D)], out_shape=[SDS((T, D), F32)] * 2,
                  sem=("parallel",))(u, w2, x1, g4)


def _ple_fwd_bwd(p, x2, tgt, wple, g5, wpg, tm):
    T = x2.shape[0]

    def body(p_ref, x2_ref, t_ref, wple_ref, g5_ref, wpg_ref, loss_ref, dx2_ref, dgt_ref, de0_ref, dg5_ref):
        @pl.when(pl.program_id(0) == 0)
        def _():
            loss_ref[...] = jnp.zeros_like(loss_ref)
            dg5_ref[...] = jnp.zeros_like(dg5_ref)

        e0 = _dot(p_ref[...].astype(BF16), wple_ref[...])
        g5 = g5_ref[...]
        r = lax.rsqrt(jnp.mean(e0 * e0, axis=-1, keepdims=True) + EPS)
        en = e0 * r
        e = en * g5
        x2 = x2_ref[...]
        s = _sigmoid(_dot(x2.astype(BF16), wpg_ref[...]))
        diff = x2 + s * e - t_ref[...]
        sq = jnp.sum(jnp.sum(diff * diff, axis=1, keepdims=True), axis=0, keepdims=True)
        loss_ref[...] += jnp.broadcast_to(sq * (0.5 / D), loss_ref.shape)
        dx3 = diff * (1.0 / D)
        de = dx3 * s
        dgt = (dx3 * e * s * (1.0 - s)).astype(BF16)
        dgt_ref[...] = dgt
        dn = de * g5
        de0_ref[...] = (r * (dn - en * jnp.mean(dn * en, axis=-1, keepdims=True))).astype(BF16)
        dg5_ref[...] += jnp.sum(de * en, axis=0, keepdims=True)
        dx2_ref[...] = dx3 + _dot_nt(dgt, wpg_ref[...])

    r = _rows(tm, D)
    return _pcall(body, name="ple_fwd_bwd", grid=(T // tm,),
                  in_specs=[_rows(tm, PLE), r, r, _full((PLE, D)), _full((1, D)), _full((D, D))],
                  out_specs=[_full((8, LANES)), r, r, r, _full((1, D))],
                  out_shape=[SDS((8, LANES), F32), SDS((T, D), F32), SDS((T, D), BF16), SDS((T, D), BF16), SDS((1, D), F32)],
                  sem=("arbitrary",))(p, x2, tgt, wple, g5, wpg)


def _bwd_mlp_down(dx2, d, g4, w2, u, tm):
    T = dx2.shape[0]

    def body(dx_ref, d_ref, g_ref, w_ref, u_ref, dd_ref, da_ref, dg_ref):
        @pl.when(pl.program_id(0) == 0)
        def _():
            dg_ref[...] = jnp.zeros_like(dg_ref)

        dd, dg = _rms_bwd(dx_ref[...], d_ref[...], g_ref[...])
        dg_ref[...] += dg
        ddb = dd.astype(BF16)
        dd_ref[...] = ddb
        du = _dot_nt(ddb, w_ref[...])
        da_ref[...] = (du * (2.0 * jnp.sqrt(u_ref[...].astype(F32)))).astype(BF16)

    return _pcall(body, name="bwd_mlp_down", grid=(T // tm,),
                  in_specs=[_rows(tm, D), _rows(tm, D), _full((1, D)), _full((D_FF, D)), _rows(tm, D_FF)],
                  out_specs=[_rows(tm, D), _rows(tm, D_FF), _full((1, D))],
                  out_shape=[SDS((T, D), BF16), SDS((T, D_FF), BF16), SDS((1, D), F32)],
                  sem=("arbitrary",))(dx2, d, g4, w2, u)


def _bwd_mlp_up(da, w1, x1, g3, dx2, tm):
    T = dx2.shape[0]

    def body(da_ref, w_ref, x_ref, g_ref, dx2_ref, dx1_ref, dg_ref):
        @pl.when(pl.program_id(0) == 0)
        def _():
            dg_ref[...] = jnp.zeros_like(dg_ref)

        dh = _dot_nt(da_ref[...], w_ref[...])
        dx, dg = _rms_bwd(dh, x_ref[...], g_ref[...])
        dg_ref[...] += dg
        dx1_ref[...] = dx2_ref[...] + dx

    return _pcall(body, name="bwd_mlp_up", grid=(T // tm,),
                  in_specs=[_rows(tm, D_FF), _full((D, D_FF)), _rows(tm, D), _full((1, D)), _rows(tm, D)],
                  out_specs=[_rows(tm, D), _full((1, D))],
                  out_shape=[SDS((T, D), F32), SDS((1, D), F32)], sem=("arbitrary",))(da, w1, x1, g3, dx2)


def _bwd_mix(dx1, yo, g2, wo, z, au, bu, wmu, wsu, om, tm):
    T = dx1.shape[0]

    def body(dx_ref, yo_ref, g_ref, wo_ref, ga_ref, gb_ref, au_ref, bu_ref, wmu_ref, wsu_ref, om_ref,
             dyo_ref, dg_ref, dau_ref, dbu_ref, dga_ref, dgb_ref, dos_ref, dl_ref, dot_ref):
        @pl.when(pl.program_id(0) == 0)
        def _():
            dg_ref[...] = jnp.zeros_like(dg_ref)

        dyo, dg = _rms_bwd(dx_ref[...], yo_ref[...], g_ref[...])
        dg_ref[...] += dg
        dyob = dyo.astype(BF16)
        dyo_ref[...] = dyob
        dy = _dot_nt(dyob, wo_ref[...])
        sa = _sigmoid(ga_ref[...])
        sb = _sigmoid(gb_ref[...])
        dau = (dy * sa).astype(BF16)
        dbu = (dy * sb).astype(BF16)
        dau_ref[...] = dau
        dbu_ref[...] = dbu
        dga_ref[...] = (dy * au_ref[...] * sa * (1.0 - sa)).astype(BF16)
        dgb_ref[...] = (dy * bu_ref[...] * sb * (1.0 - sb)).astype(BF16)
        dom = _dot_nt(dau, wmu_ref[...])
        dos_ref[...] = _dot_nt(dbu, wsu_ref[...])
        prod = dom * om_ref[...]
        sub = lax.broadcasted_iota(jnp.int32, (8, tm), 0)
        for pr in range(MLA_HEADS // 2):
            sl = slice(LANES * pr, LANES * (pr + 1))
            pt = prod[:, sl].T
            d0 = jnp.sum(pt[0:64], axis=0, keepdims=True)
            d1 = jnp.sum(pt[64:128], axis=0, keepdims=True)
            dl_ref[pr, 0] = jnp.where(sub == 0, d0, jnp.where(sub == 1, d1, 0.0))
            dot_ref[0, sl, :] = dom[:, sl].T.astype(BF16)

    r = _rows(tm, D)
    w = _full((D, D))
    return _pcall(body, name="bwd_mix", grid=(T // tm,),
                  in_specs=[r, r, _full((1, D)), w, _rows(tm, D, 1), _rows(tm, D, 2), r, r, w, w, r],
                  out_specs=[r, _full((1, D)), r, r, r, r, r, pl.BlockSpec((MLA_HEADS // 2, 1, 8, tm), lambda i: (0, i, 0, 0)),
                             pl.BlockSpec((1, D, tm), lambda i: (i, 0, 0))],
                  out_shape=[SDS((T, D), BF16), SDS((1, D), F32), SDS((T, D), BF16), SDS((T, D), BF16), SDS((T, D), BF16),
                             SDS((T, D), BF16), SDS((T, D), F32), SDS((MLA_HEADS // 2, T // tm, 8, tm), F32),
                             SDS((T // tm, D, tm), BF16)],
                  sem=("arbitrary",))(dx1, yo, g2, wo, z, z, au, bu, wmu, wsu, om)


def _mla_bwd(qt, km, kt, vm, dot, lse, delta, tb):
    T = km.shape[0]
    nb = T // tb
    cc = ATT_COLS

    per = 2 if nb % 2 == 0 else 1

    def body(qt_ref, k_ref, kt_ref, v_ref, dot_ref, l_ref, dl_ref, dqt_ref, dkt_ref, dvt_ref,
             s_ref, dp_ref, p_ref, ds_ref, vh_ref):
        @pl.when(pl.program_id(1) == 0)
        def _():
            dqt_ref[...] = jnp.zeros_like(dqt_ref)

        dkt_ref[...] = jnp.zeros_like(dkt_ref)
        dvt_ref[...] = jnp.zeros_like(dvt_ref)
        for blk in range(per):
            one_block(per * pl.program_id(1) + blk, blk, qt_ref, k_ref, kt_ref, v_ref, dot_ref, l_ref, dl_ref, dqt_ref, dkt_ref,
                      dvt_ref, s_ref, dp_ref, p_ref, ds_ref, vh_ref)

    def one_block(j, blk, qt_ref, k_ref, kt_ref, v_ref, dot_ref, l_ref, dl_ref, dqt_ref, dkt_ref, dvt_ref,
                  s_ref, dp_ref, p_ref, ds_ref, vh_ref):
        lo = lax.broadcasted_iota(jnp.int32, (tb, LANES), 1) < 64
        key = lax.broadcasted_iota(jnp.int32, (tb, cc), 0)
        qry = lax.broadcasted_iota(jnp.int32, (tb, cc), 1)
        rows_j = slice(tb * blk, tb * (blk + 1))
        v = v_ref[rows_j, :]
        vh_ref[0] = jnp.where(lo, v, jnp.zeros_like(v))
        vh_ref[1] = jnp.where(lo, jnp.zeros_like(v), v)

        def scores(i, slot):
            for hh in range(2):
                sl = slice(LANES * hh, LANES * (hh + 1))
                s_ref[slot, hh] = _dot(k_ref[rows_j, sl], qt_ref[i, sl, :])
                dp_ref[slot, hh] = _dot(vh_ref[hh], dot_ref[i])

        def grads(i, slot, diagonal):
            lse_i = l_ref[0, i]
            delta_i = dl_ref[0, i]
            for hh in range(2):
                for c in range(tb // cc):
                    cols = slice(cc * c, cc * (c + 1))
                    p = jnp.exp2(s_ref[slot, hh, :, cols] * MLA_LOG2_SCALE - lse_i[hh:hh + 1, cols])
                    if diagonal:
                        p = jnp.where(key <= qry + cc * c, p, 0.0)
                    p_ref[hh, :, cols] = p.astype(BF16)
                    ds_ref[hh, :, cols] = (p * (dp_ref[slot, hh, :, cols] - delta_i[hh:hh + 1, cols]) * MLA_SCALE).astype(BF16)
            for hh in range(2):
                sl = slice(LANES * hh, LANES * (hh + 1))
                half = slice(64 * hh, 64 * (hh + 1))
                dvt_ref[blk, half, :] += _dot_nt(dot_ref[i, half, :], p_ref[hh])
                real = slice(LANES * hh, LANES * hh + MLA_NOPE + MLA_ROPE)
                dkt_ref[blk, real, :] += _dot_nt(qt_ref[i, real, :], ds_ref[hh])
                dqt_ref[i, real, :] += _dot(kt_ref[blk, real, :], ds_ref[hh])

        n_off = nb - 1 - j

        def step(u, carry):
            i0 = j + 1 + 2 * u
            scores(i0 + 1, 1)
            grads(i0, 0, False)
            scores(jnp.where(i0 + 2 < nb, i0 + 2, j), 0)
            grads(i0 + 1, 1, False)
            return carry

        scores(jnp.where(n_off > 0, j + 1, j), 0)
        lax.fori_loop(0, n_off // 2, step, 0)

        @pl.when(n_off % 2 == 1)
        def _():
            scores(j, 1)
            grads(nb - 1, 0, False)
            grads(j, 1, True)

        @pl.when(n_off % 2 == 0)
        def _():
            grads(j, 0, True)

    blk = lambda w: pl.BlockSpec((per * tb, w), lambda p, j: (j, p))
    stat = pl.BlockSpec((1, nb, 8, tb), lambda p, j: (p, 0, 0, 0))
    pair_t = lambda w: pl.BlockSpec((nb, w, tb), lambda p, j: (0, p, 0))
    blk_t = lambda w: pl.BlockSpec((per, w, tb), lambda p, j: (j, p, 0))
    return _pcall(body, name="mla_bwd", grid=(MLA_HEADS // 2, nb // per),
                  in_specs=[pair_t(256), blk(256), blk_t(256), blk(LANES), pair_t(LANES), stat, stat],
                  out_specs=[pair_t(256), blk_t(256), blk_t(LANES)],
                  out_shape=[SDS((nb, 2048, tb), F32), SDS((nb, 2048, tb), F32), SDS((nb, D, tb), F32)],
                  scratch=[pltpu.VMEM((2, 2, tb, tb), F32), pltpu.VMEM((2, 2, tb, tb), F32), pltpu.VMEM((2, tb, tb), BF16),
                           pltpu.VMEM((2, tb, tb), BF16), pltpu.VMEM((2, tb, LANES), BF16)],
                  sem=("parallel", "arbitrary"))(qt, km, kt, vm, dot, lse, delta)


def _swa_bwd(sinks, qs, ks, vs, do, o, lse):
    T = qs.shape[0]
    nb, cur, prev = _swa_specs(T)

    def body(sink_ref, q_ref, kc_ref, kp_ref, vc_ref, vp_ref, do_ref, o_ref, l_ref,
             dq_ref, dkc_ref, dkp_ref, dvc_ref, dvp_ref, dsink_ref, kb_ref, vb_ref, s_ref, dp_ref, p_ref, ds_ref):
        n = pl.program_id(0)

        @pl.when(n == 0)
        def _():
            dsink_ref[...] = jnp.zeros_like(dsink_ref)

        mask = _swa_mask(n)
        lo = lax.broadcasted_iota(jnp.int32, (WINDOW, LANES), 1) < 64
        hi = jnp.logical_not(lo)
        lane8 = lax.broadcasted_iota(jnp.int32, (8, LANES), 1)
        for g in range(2):
            gs = slice(LANES * g, LANES * (g + 1))
            kb_ref[g] = jnp.concatenate([kp_ref[:, gs], kc_ref[:, gs]], axis=0)
            vb_ref[g] = jnp.concatenate([vp_ref[:, gs], vc_ref[:, gs]], axis=0)

        def head(h):
            sl = slice(LANES * (h // 2), LANES * (h // 2 + 1))
            hm = lo if h % 2 == 0 else hi
            qp = q_ref[:, sl]
            return hm, sl, jnp.where(hm, qp, jnp.zeros_like(qp)), jnp.where(hm, do_ref[:, sl], 0.0).astype(BF16)

        for h in range(SWA_HEADS):
            _, _, qh, dom = head(h)
            s_ref[h] = _dot_nt(qh, kb_ref[h // 8])
            dp_ref[h] = _dot_nt(dom, vb_ref[h // 8])
        dsink = jnp.zeros((8, LANES), F32)
        for h in range(SWA_HEADS):
            hm, sl, _, _ = head(h)
            lse_h = jnp.max(jnp.where(hm, l_ref[:, sl], -jnp.inf), axis=1, keepdims=True)
            delta = jnp.sum(jnp.where(hm, do_ref[:, sl] * o_ref[:, sl], 0.0), axis=1, keepdims=True)
            p = jnp.exp(jnp.where(mask, s_ref[h] * SWA_SCALE, NEG) - lse_h)
            p_ref[h] = p.astype(BF16)
            ds_ref[h] = (p * (dp_ref[h] - delta) * SWA_SCALE).astype(BF16)
            d_sink = -jnp.sum(jnp.exp(sink_ref[h] - lse_h) * delta, axis=0, keepdims=True)
            dsink = dsink + jnp.where(lane8 == h, d_sink, 0.0)
        dsink_ref[...] += dsink
        for g in range(2):
            gs = slice(LANES * g, LANES * (g + 1))
            dkb = jnp.zeros((2 * WINDOW, LANES), F32)
            dvb = jnp.zeros((2 * WINDOW, LANES), F32)
            for j in range(4 * g, 4 * g + 4):
                dqs = []
                for h in (2 * j, 2 * j + 1):
                    _, _, qh, dom = head(h)
                    dvb = dvb + _dot_tn(p_ref[h], dom)
                    dkb = dkb + _dot_tn(ds_ref[h], qh)
                    dqs.append(_dot(ds_ref[h], kb_ref[g]))
                dq_ref[:, LANES * j:LANES * (j + 1)] = jnp.where(lo, dqs[0], dqs[1])
            dkp_ref[:, gs] = dkb[:WINDOW]
            dkc_ref[:, gs] = dkb[WINDOW:]
            dvp_ref[:, gs] = dvb[:WINDOW]
            dvc_ref[:, gs] = dvb[WINDOW:]

    band = pltpu.VMEM((2, 2 * WINDOW, LANES), BF16)
    return _pcall(body, name="swa_bwd", grid=(nb,),
                  in_specs=[pl.BlockSpec(memory_space=pltpu.SMEM), cur(D), cur(256), prev(256), cur(256), prev(256),
                            cur(D), cur(D), cur(D)],
                  out_specs=[cur(D), cur(256), cur(256), cur(256), cur(256), _full((8, LANES))],
                  out_shape=[SDS((T, D), F32), SDS((T, 256), F32), SDS((T, 256), F32), SDS((T, 256), F32), SDS((T, 256), F32),
                             SDS((8, LANES), F32)],
                  scratch=[band, band, pltpu.VMEM((SWA_HEADS, WINDOW, 2 * WINDOW), F32),
                           pltpu.VMEM((SWA_HEADS, WINDOW, 2 * WINDOW), F32), pltpu.VMEM((SWA_HEADS, WINDOW, 2 * WINDOW), BF16),
                           pltpu.VMEM((SWA_HEADS, WINDOW, 2 * WINDOW), BF16)],
                  sem=("arbitrary",))(sinks, qs, ks, ks, vs, vs, do, o, lse)


def _bwd_qkv(dqm, dkm, dvm, dqs, dkc, dkp, dvc, dvp, z, gq, gkv, wqb, wkn, wv, tab_m, tab_s):
    T = z.shape[0]
    tm = WINDOW
    nb = T // tm
    per = dqm.shape[2] // tm

    tab_mt = [t.T for t in tab_m]
    half = MLA_ROPE // 2

    def rope_t(v, c, a, b):
        return v * c + pltpu.roll(v, LANES - half, 0) * a + pltpu.roll(v, half, 0) * b

    def rms_bwd_t(dy, x, g):
        r = lax.rsqrt(jnp.mean(x * x, axis=0, keepdims=True) + EPS)
        xn = x * r
        dn = dy * g
        return r * (dn - xn * jnp.mean(dn * xn, axis=0, keepdims=True)), jnp.sum(dy * xn, axis=1, keepdims=True)

    def body(dqm_ref, dkm_ref, dvm_ref, dqs_ref, dkc_ref, dkp_ref, dvc_ref, dvp_ref, qa_ref, kva_ref, gq_ref, gkv_ref,
             wqb_ref, wkn_ref, wv_ref, cmt_ref, amt_ref, bmt_ref, cs_ref, as_ref, bs_ref,
             dq_out, dkn_out, dv_out, dsq_ref, drest_ref, dgq_ref, dgkv_ref):
        i = pl.program_id(0)

        @pl.when(i == 0)
        def _():
            dgq_ref[...] = jnp.zeros_like(dgq_ref)
            dgkv_ref[...] = jnp.zeros_like(dgkv_ref)

        cmt, amt, bmt = cmt_ref[...], -amt_ref[...], -bmt_ref[...]
        cs, as_, bs = cs_ref[...], -as_ref[...], -bs_ref[...]
        row = lax.broadcasted_iota(jnp.int32, (LANES, tm), 0)
        nope = row < MLA_NOPE
        roped = jnp.logical_and(row >= MLA_NOPE, row < MLA_NOPE + MLA_ROPE)
        dkr = jnp.zeros((LANES, tm), F32)
        for h in range(MLA_HEADS):
            sl = slice(LANES * h, LANES * (h + 1))
            dq_out[0, sl, :] = rope_t(dqm_ref[0, sl, :], cmt, amt, bmt).astype(BF16)
            dk_h = dkm_ref[0, sl, :]
            dkn_out[0, sl, :] = jnp.where(nope, dk_h, 0.0).astype(BF16)
            dkr = dkr + jnp.where(roped, dk_h, 0.0)
        dv_out[0] = dvm_ref[0].astype(BF16)
        dqn = _dot(wqb_ref[...], dq_out[0])
        dkvn = _dot(wkn_ref[...], dkn_out[0]) + _dot(wv_ref[...], dv_out[0])
        dqa, dgq = rms_bwd_t(dqn, qa_ref[...].T, gq_ref[...])
        dkva, dgkv = rms_bwd_t(dkvn, kva_ref[...].T, gkv_ref[...])
        dgq_ref[...] += dgq
        dgkv_ref[...] += dgkv
        for j in range(D // LANES):
            sl = slice(LANES * j, LANES * (j + 1))
            dsq_ref[:, sl] = _rope(dqs_ref[:, sl], cs, as_, bs, SWA_HD // 2).astype(BF16)
        keep = (i < nb - 1).astype(F32)
        drest_ref[:, 0:256] = dqa.T.astype(BF16)
        for j in range(2):
            sl = slice(LANES * j, LANES * (j + 1))
            dk = dkc_ref[:, sl] + keep * dkp_ref[:, sl]
            drest_ref[:, 256 + LANES * j:256 + LANES * (j + 1)] = _rope(dk, cs, as_, bs, SWA_HD // 2).astype(BF16)
        drest_ref[:, 512:768] = (dvc_ref[...] + keep * dvp_ref[...]).astype(BF16)
        drest_ref[:, 768:896] = dkva.T.astype(BF16)
        drest_ref[:, 896:1024] = rope_t(dkr, cmt, amt, bmt).T.astype(BF16)

    nxt = pl.BlockSpec((tm, 256), lambda i: (jnp.minimum(i + 1, nb - 1), 0))
    tab = [_rows(tm, LANES)] * 3
    tab_t = [pl.BlockSpec((LANES, tm), lambda i: (0, i))] * 3
    blk_t = lambda w: pl.BlockSpec((1, w, tm), lambda i: (i // per, 0, i % per))
    return _pcall(body, name="bwd_qkv", grid=(nb,),
                  in_specs=[blk_t(2048), blk_t(2048), blk_t(1024), _rows(tm, 1024), _rows(tm, 256), nxt,
                            _rows(tm, 256), nxt, _rows(tm, 256, 12), _rows(tm, 128, 30), _full((Q_LORA, 1)), _full((KV_LORA, 1)),
                            _full((Q_LORA, 2048)), _full((KV_LORA, 2048)), _full((KV_LORA, 1024))] + tab_t + tab,
                  out_specs=[blk_t(2048), blk_t(2048), blk_t(1024), _rows(tm, 1024), _rows(tm, 1024),
                             _full((Q_LORA, 1)), _full((KV_LORA, 1))],
                  out_shape=[SDS(dqm.shape, BF16), SDS(dkm.shape, BF16), SDS(dvm.shape, BF16), SDS((T, 1024), BF16),
                             SDS((T, 1024), BF16), SDS((Q_LORA, 1), F32), SDS((KV_LORA, 1), F32)],
                  sem=("arbitrary",))(dqm, dkm, dvm, dqs, dkc, dkp, dvc, dvp, z, z, gq.reshape(Q_LORA, 1),
                                      gkv.reshape(KV_LORA, 1), wqb, wkn, wv, *tab_mt, *tab_s)


def _bwd_in(dsq, dga, dgb, drest, w_in_p, x, g1, dx1, tm):
    T = x.shape[0]

    def body(a_ref, b_ref, c_ref, d_ref, w_ref, x_ref, g_ref, dx1_ref, dx_ref, dg_ref):
        @pl.when(pl.program_id(0) == 0)
        def _():
            dg_ref[...] = jnp.zeros_like(dg_ref)

        dh = (_dot_nt(a_ref[...], w_ref[:, 0:1024]) + _dot_nt(b_ref[...], w_ref[:, 1024:2048])
              + _dot_nt(c_ref[...], w_ref[:, 2048:3072]) + _dot_nt(d_ref[...], w_ref[:, 3072:4096]))
        dx, dg = _rms_bwd(dh, x_ref[...], g_ref[...])
        dg_ref[...] += dg
        dx_ref[...] = dx1_ref[...] + dx

    r = _rows(tm, D)
    return _pcall(body, name="bwd_in", grid=(T // tm,),
                  in_specs=[r, r, r, r, _full((D, NZ)), r, _full((1, D)), r],
                  out_specs=[r, _full((1, D))], out_shape=[SDS((T, D), F32), SDS((1, D), F32)],
                  sem=("arbitrary",))(dsq, dga, dgb, drest, w_in_p, x, g1, dx1)


def _wgrad(a, g, name, into=None):
    T, K = a.shape
    N = g.shape[1]
    tk, tn, tt = min(K, 1024), min(N, 1024), min(T, 1024)
    if into is not None:
        buf, weight = into
        _, row0, lane0 = PACK_AT[weight]
        shard = {n: (r, c) for n, r, c in BIG}[weight]
        assert lane0 == 0 and shard[1] == D and tk % shard[0] == 0
        per_step = tk // shard[0]
    assert K % tk == 0 and N % tn == 0 and T % tt == 0, (a.shape, g.shape)
    steps = T // tt

    def body(a_ref, g_ref, *rest):
        o_ref, acc_ref = rest[-2:]
        t = pl.program_id(2)

        @pl.when(t == 0)
        def _():
            acc_ref[...] = jnp.zeros_like(acc_ref)

        acc_ref[...] += _dot_tn(a_ref[...].astype(BF16), g_ref[...].astype(BF16))

        @pl.when(t == steps - 1)
        def _():
            o_ref[...] = acc_ref[...].astype(o_ref.dtype).reshape(o_ref.shape)

    in_specs = [pl.BlockSpec((tt, tk), lambda k, n, t: (t, k)), pl.BlockSpec((tt, tn), lambda k, n, t: (t, n))]
    if into is None:
        return _pcall(body, name=name, grid=(K // tk, N // tn, steps), in_specs=in_specs,
                      out_specs=pl.BlockSpec((tk, tn), lambda k, n, t: (k, n)), out_shape=SDS((K, N), F32),
                      scratch=[pltpu.VMEM((tk, tn), F32)], sem=("parallel", "parallel", "arbitrary"))(a, g)
    assert row0 % shard[0] == 0 and (K // tk) * (N // tn) * per_step == N_CHIPS
    return _pcall(body, name=name, grid=(K // tk, N // tn, steps), in_specs=in_specs + [ANY],
                  out_specs=pl.BlockSpec((per_step, shard[0], tn), lambda k, n, t: (k + n, row0 // shard[0], 0)),
                  out_shape=SDS(buf.shape, buf.dtype),
                  scratch=[pltpu.VMEM((tk, tn), F32)], sem=("parallel", "parallel", "arbitrary"), aliases={2: 0})(a, g, buf)


def _wgrad_t(at, g, name):
    nblk, K, tt = at.shape
    N = g.shape[1]
    tk = min(K, 1024)
    per_step = 4 if nblk % 4 == 0 else 1
    assert K % tk == 0 and g.shape[0] == nblk * tt

    def body(a_ref, g_ref, o_ref):
        @pl.when(pl.program_id(1) == 0)
        def _():
            o_ref[...] = jnp.zeros_like(o_ref)

        acc = _dot(a_ref[0], g_ref[0:tt, :].astype(BF16))
        for b in range(1, per_step):
            acc = acc + _dot(a_ref[b], g_ref[tt * b:tt * (b + 1), :].astype(BF16))
        o_ref[...] += acc

    return _pcall(body, name=name, grid=(K // tk, nblk // per_step),
                  in_specs=[pl.BlockSpec((per_step, tk, tt), lambda k, t: (t, k, 0)),
                            pl.BlockSpec((per_step * tt, N), lambda k, t: (t, 0))],
                  out_specs=pl.BlockSpec((tk, N), lambda k, t: (k, 0)), out_shape=SDS((K, N), F32),
                  sem=("parallel", "arbitrary"))(at, g)


def _adamw(w, packed_g, m, v, name):
    _, R, C = w.shape
    _, row0, lane0 = PACK_AT[name]
    tr = min(R, 256 if row0 % 256 == 0 else 128)
    assert row0 % tr == 0 and R % tr == 0

    def body(w_ref, g_ref, m_ref, v_ref, go_ref, d_ref, m2_ref, v2_ref):
        g_ = g_ref[:, lane0:lane0 + C]
        go_ref[0] = g_
        m2 = ADAM_B1 * m_ref[0] + (1.0 - ADAM_B1) * g_
        v2 = ADAM_B2 * v_ref[0] + (1.0 - ADAM_B2) * jnp.square(g_)
        m_hat = m2 / (1.0 - ADAM_B1 ** ADAM_STEP)
        v_hat = v2 / (1.0 - ADAM_B2 ** ADAM_STEP)
        d_ref[0] = -ADAM_LR * (m_hat / (jnp.sqrt(v_hat) + ADAM_EPS) + ADAM_WD * w_ref[0])
        m2_ref[0] = m2
        v2_ref[0] = v2

    r = pl.BlockSpec((1, tr, C), lambda i: (0, i, 0))
    return _pcall(body, name="adamw_" + name, grid=(R // tr,),
                  in_specs=[r, pl.BlockSpec((tr, D), lambda i: (row0 // tr + i, 0)), r, r], out_specs=[r] * 4,
                  out_shape=[SDS((1, R, C), F32)] * 4, sem=("parallel",))(w, packed_g, m, v)


def _adamw_small(w, parts, m, v):
    def body(w_ref, p_ref, m_ref, v_ref, g_ref, d_ref, m2_ref, v2_ref):
        g_ = p_ref[0]
        for k in range(1, N_DEV):
            g_ = g_ + p_ref[k]
        g_ref[...] = g_
        m2 = ADAM_B1 * m_ref[...] + (1.0 - ADAM_B1) * g_
        v2 = ADAM_B2 * v_ref[...] + (1.0 - ADAM_B2) * jnp.square(g_)
        m_hat = m2 / (1.0 - ADAM_B1 ** ADAM_STEP)
        v_hat = v2 / (1.0 - ADAM_B2 ** ADAM_STEP)
        d_ref[...] = -ADAM_LR * (m_hat / (jnp.sqrt(v_hat) + ADAM_EPS) + ADAM_WD * w_ref[...])
        m2_ref[...] = m2
        v2_ref[...] = v2

    s = _full((8, D))
    return _pcall(body, name="adamw_small", grid=(1,), in_specs=[s, _full((N_DEV, 8, D)), s, s], out_specs=[s] * 4,
                  out_shape=[SDS((8, D), F32)] * 4, sem=("arbitrary",))(w, parts, m, v)


ANY = pl.BlockSpec(memory_space=pl.ANY)


def _place():
    x, y, c = lax.axis_index("x"), lax.axis_index("y"), lax.axis_index("c")
    chips = [(1 - x, y), (x, 1 - y), (1 - x, 1 - y)]
    return x, y, c, chips


def _all_gather(wpk):
    rows = wpk.shape[0]
    HALF = rows // 2
    assert HALF % 16 == 0

    def body(in_ref, out_ref, send_sems, recv_sems):
        x, y, c, chips = _place()
        half = pl.ds(pl.multiple_of(c * HALF, 16), HALF)
        other = pl.ds(pl.multiple_of((1 - c) * HALF, 16), HALF)

        def copy(k, src, dst, to):
            return pltpu.make_async_remote_copy(src_ref=src, dst_ref=dst, send_sem=send_sems.at[k], recv_sem=recv_sems.at[k],
                                                device_id=to, device_id_type=MESH)

        first = [copy(k, in_ref.at[half], out_ref.at[2 * x + y, half], (cx, cy, c)) for k, (cx, cy) in enumerate(chips)]
        for cp in first:
            cp.start()
        passed = []
        for k, (cx, cy) in enumerate(chips):
            slot = out_ref.at[2 * cx + cy, half]
            copy(k, slot, slot, (x, y, c)).wait_recv()
            fwd = copy(3 + k, slot, slot, (x, y, 1 - c))
            fwd.start()
            passed.append(fwd)
        for k, (cx, cy) in enumerate(chips):
            slot = out_ref.at[2 * cx + cy, other]
            copy(3 + k, slot, slot, (x, y, c)).wait_recv()
        for cp in first + passed:
            cp.wait_send()

    return _pcall(body, name="all_gather_weights", in_specs=[ANY], out_specs=ANY,
                  out_shape=SDS((N_CHIPS, rows, D), BF16),
                  scratch=[pltpu.SemaphoreType.DMA((6,)), pltpu.SemaphoreType.DMA((6,))])(wpk)


HBM = pl.BlockSpec(memory_space=pltpu.HBM)
SEM = pl.BlockSpec(memory_space=pltpu.SEMAPHORE)
DATAFLOW = pltpu.SideEffectType.DATAFLOW_SIDE_EFFECTING


def _in_hbm(a):
    return pltpu.with_memory_space_constraint(a, pltpu.HBM)


def _gather_late_start(wpk, after):
    rows = wpk.shape[0]

    def body(in_ref, land_ref, after_ref, send_sems, recv_sems, in_thru, land_thru, token):
        x, y, c, chips = _place()
        for k, (cx, cy) in enumerate(chips):
            pltpu.make_async_remote_copy(src_ref=in_ref, dst_ref=land_ref.at[2 * x + y], send_sem=send_sems.at[k],
                                         recv_sem=recv_sems.at[k], device_id=(cx, cy, c), device_id_type=MESH).start()
        token[...] = jnp.zeros_like(token)

    return pl.pallas_call(
        body, name="gather_late_start",
        out_shape=(pltpu.SemaphoreType.DMA((3,)), pltpu.SemaphoreType.DMA((3,)), pltpu.HBM(wpk.shape, wpk.dtype),
                   pltpu.HBM((N_CHIPS, rows, D), wpk.dtype), SDS((8, LANES), F32)),
        in_specs=(HBM, HBM, ANY), out_specs=(SEM, SEM, HBM, HBM, pl.BlockSpec(memory_space=pltpu.VMEM)),
        input_output_aliases={0: 2, 1: 3}, compiler_params=pltpu.CompilerParams(has_side_effects=DATAFLOW),
    )(_in_hbm(wpk), _in_hbm(lax.empty((N_CHIPS, rows, D), wpk.dtype)), after)


def _gather_late_wait(send_sems, recv_sems, in_thru, land_thru, after):
    def body(in_ref, land_ref, send_sems, recv_sems, after_ref, after2_ref, in_dead, got_ref):
        x, y, c, chips = _place()
        for k, (cx, cy) in enumerate(chips):
            cp = pltpu.make_async_remote_copy(src_ref=in_ref, dst_ref=land_ref.at[2 * cx + cy], send_sem=send_sems.at[k],
                                              recv_sem=recv_sems.at[k], device_id=(cx, cy, c), device_id_type=MESH)
            cp.wait_send()
            cp.wait_recv()

    return pl.pallas_call(
        body, name="gather_late_wait",
        out_shape=(pltpu.HBM(in_thru.shape, in_thru.dtype), pltpu.HBM(land_thru.shape, land_thru.dtype)),
        in_specs=(HBM, HBM, SEM, SEM, ANY, ANY), out_specs=(HBM, HBM), input_output_aliases={0: 0, 1: 1},
        compiler_params=pltpu.CompilerParams(has_side_effects=DATAFLOW),
    )(in_thru, land_thru, send_sems, recv_sems, *after)[1]


def _rs_sibling(gpk):
    HALF = gpk.shape[1] // 2

    def body(in_ref, out_ref, send_sem, recv_sem):
        x, y, c, _ = _place()
        theirs = pl.ds(pl.multiple_of((1 - c) * HALF, 16), HALF)
        cp = pltpu.make_async_remote_copy(src_ref=in_ref.at[:, theirs], dst_ref=out_ref, send_sem=send_sem, recv_sem=recv_sem,
                                          device_id=(x, y, 1 - c), device_id_type=MESH)
        cp.start()
        cp.wait()

    return _pcall(body, name="rs_sibling", in_specs=[ANY], out_specs=ANY, out_shape=SDS((N_CHIPS, HALF, D), gpk.dtype),
                  scratch=[pltpu.SemaphoreType.DMA, pltpu.SemaphoreType.DMA])(gpk)


def _rs_add_sibling(cidx, gpk, got):
    HALF = got.shape[1]
    th = HALF // 4
    nh = HALF // th
    assert th % 16 == 0

    def body(c_ref, a_ref, b_ref, o_ref):
        o_ref[...] = (a_ref[...].astype(F32) + b_ref[...].astype(F32)).astype(BF16)

    gs = pltpu.PrefetchScalarGridSpec(
        num_scalar_prefetch=1, grid=(N_CHIPS, nh),
        in_specs=[pl.BlockSpec((1, th, D), lambda j, i, c: (j, c[0] * nh + i, 0)), pl.BlockSpec((1, th, D), lambda j, i, c: (j, i, 0))],
        out_specs=pl.BlockSpec((1, th, D), lambda j, i, c: (j, i, 0)))
    return pl.pallas_call(body, name="rs_add_sibling", grid_spec=gs, out_shape=SDS((N_CHIPS, HALF, D), BF16),
                          compiler_params=pltpu.CompilerParams(dimension_semantics=("parallel", "parallel"),
                                                               vmem_limit_bytes=48 << 20))(cidx, gpk, got)


def _rs_chips_start(part, small, after):
    def body(p_ref, s_ref, land_ref, sland_ref, after_ref, send_sems, recv_sems, p_thru, s_thru, land_thru, sland_thru, token):
        x, y, c, chips = _place()
        for k, (cx, cy) in enumerate(chips):
            pltpu.make_async_remote_copy(src_ref=p_ref.at[2 * cx + cy], dst_ref=land_ref.at[2 * x + y], send_sem=send_sems.at[k],
                                         recv_sem=recv_sems.at[k], device_id=(cx, cy, c), device_id_type=MESH).start()
        peers = [(x, y, 1 - c)] + [(cx, cy, c) for cx, cy in chips] + [(cx, cy, 1 - c) for cx, cy in chips]
        for k, to in enumerate(peers):
            pltpu.make_async_remote_copy(src_ref=s_ref, dst_ref=sland_ref.at[4 * x + 2 * y + c], send_sem=send_sems.at[3 + k],
                                         recv_sem=recv_sems.at[3 + k], device_id=to, device_id_type=MESH).start()
        token[...] = jnp.zeros_like(token)

    return pl.pallas_call(
        body, name="rs_chips_start",
        out_shape=(pltpu.SemaphoreType.DMA((10,)), pltpu.SemaphoreType.DMA((10,)), pltpu.HBM(part.shape, part.dtype),
                   pltpu.HBM(small.shape, small.dtype), pltpu.HBM(part.shape, part.dtype), pltpu.HBM((N_DEV, 8, D), F32),
                   SDS((8, LANES), F32)),
        in_specs=(HBM, HBM, HBM, HBM, ANY), out_specs=(SEM, SEM, HBM, HBM, HBM, HBM, pl.BlockSpec(memory_space=pltpu.VMEM)),
        input_output_aliases={0: 2, 1: 3, 2: 4, 3: 5}, compiler_params=pltpu.CompilerParams(has_side_effects=DATAFLOW),
    )(_in_hbm(part), _in_hbm(small), _in_hbm(lax.empty(part.shape, part.dtype)), _in_hbm(lax.empty((N_DEV, 8, D), F32)), after)


def _rs_chips_wait(send_sems, recv_sems, p_thru, s_thru, land_thru, sland_thru, after):
    def body(p_ref, s_ref, land_ref, sland_ref, send_sems, recv_sems, *after_and_outputs):
        x, y, c, chips = _place()
        for k, (cx, cy) in enumerate(chips):
            cp = pltpu.make_async_remote_copy(src_ref=p_ref.at[0], dst_ref=land_ref.at[2 * cx + cy], send_sem=send_sems.at[k],
                                              recv_sem=recv_sems.at[k], device_id=(cx, cy, c), device_id_type=MESH)
            cp.wait_send()
            cp.wait_recv()
        peers = [(x, y, 1 - c)] + [(cx, cy, c) for cx, cy in chips] + [(cx, cy, 1 - c) for cx, cy in chips]
        for k, (px, py, pc) in enumerate(peers):
            cp = pltpu.make_async_remote_copy(src_ref=s_ref, dst_ref=sland_ref.at[4 * px + 2 * py + pc], send_sem=send_sems.at[3 + k],
                                              recv_sem=recv_sems.at[3 + k], device_id=(px, py, pc), device_id_type=MESH)
            cp.wait_send()
            cp.wait_recv()

    hbm = lambda a: pltpu.HBM(a.shape, a.dtype)
    outs = pl.pallas_call(
        body, name="rs_chips_wait", out_shape=(hbm(p_thru), hbm(s_thru), hbm(land_thru), hbm(sland_thru)),
        in_specs=(HBM, HBM, HBM, HBM, SEM, SEM) + (ANY,) * len(after), out_specs=(HBM, HBM, HBM, HBM),
        input_output_aliases={0: 0, 1: 1, 2: 2, 3: 3}, compiler_params=pltpu.CompilerParams(has_side_effects=DATAFLOW),
    )(p_thru, s_thru, land_thru, sland_thru, send_sems, recv_sems, *after)
    return outs[0], outs[2], outs[3]


def _rs_add_chips(qidx, part, parts):
    HALF = part.shape[1]
    th = HALF // 4
    nh = HALF // th
    assert th % 16 == 0

    def body(q_ref, own_ref, p_ref, o_ref):
        for me in range(N_CHIPS):
            @pl.when(q_ref[0] == me)
            def _(me=me):
                t = [(own_ref[0] if j == me else p_ref[j]).astype(F32) for j in range(N_CHIPS)]
                o_ref[...] = ((t[0] + t[1]) + t[2]) + t[3]

    gs = pltpu.PrefetchScalarGridSpec(
        num_scalar_prefetch=1, grid=(HALF // th,),
        in_specs=[pl.BlockSpec((1, th, D), lambda i, q: (q[0], i, 0)), pl.BlockSpec((N_CHIPS, th, D), lambda i, q: (0, i, 0))],
        out_specs=pl.BlockSpec((th, D), lambda i, q: (q[1] * nh + i, 0)))
    return pl.pallas_call(body, name="rs_add_chips", grid_spec=gs, out_shape=SDS((2 * HALF, D), F32),
                          compiler_params=pltpu.CompilerParams(dimension_semantics=("parallel",),
                                                               vmem_limit_bytes=48 << 20))(qidx, part, parts)


def _rs_join(shard, name):
    HALF = shard.shape[0] // 2

    def body(in_ref, out_ref, send_sem, recv_sem):
        x, y, c, _ = _place()
        rows = pl.ds(pl.multiple_of(c * HALF, 16), HALF)
        cp = pltpu.make_async_remote_copy(src_ref=in_ref.at[rows], dst_ref=out_ref.at[rows], send_sem=send_sem, recv_sem=recv_sem,
                                          device_id=(x, y, 1 - c), device_id_type=MESH)
        cp.start()
        cp.wait()

    return _pcall(body, name=name, in_specs=[ANY], out_specs=ANY, out_shape=SDS(shard.shape, F32),
                  scratch=[pltpu.SemaphoreType.DMA, pltpu.SemaphoreType.DMA], aliases={0: 0})(shard)


def _reduce_late_start(gpk, after):
    rows = gpk.shape[1]
    HALF = rows // 2
    assert HALF % 16 == 0

    def body(in_ref, land_ref, after_ref, send_sems, recv_sems, in_thru, land_thru, token):
        x, y, c, chips = _place()
        me = 4 * x + 2 * y + c
        peers = [(x, y, 1 - c)] + [(cx, cy, c) for cx, cy in chips] + [(cx, cy, 1 - c) for cx, cy in chips]
        for k, (px, py, pc) in enumerate(peers):
            src = in_ref.at[2 * px + py, pl.ds(pl.multiple_of(pc * HALF, 16), HALF)]
            pltpu.make_async_remote_copy(src_ref=src, dst_ref=land_ref.at[me], send_sem=send_sems.at[k], recv_sem=recv_sems.at[k],
                                         device_id=(px, py, pc), device_id_type=MESH).start()
        token[...] = jnp.zeros_like(token)

    return pl.pallas_call(
        body, name="reduce_late_start",
        out_shape=(pltpu.SemaphoreType.DMA((7,)), pltpu.SemaphoreType.DMA((7,)), pltpu.HBM(gpk.shape, gpk.dtype),
                   pltpu.HBM((N_DEV, HALF, D), gpk.dtype), SDS((8, LANES), F32)),
        in_specs=(HBM, HBM, ANY), out_specs=(SEM, SEM, HBM, HBM, pl.BlockSpec(memory_space=pltpu.VMEM)),
        input_output_aliases={0: 2, 1: 3}, compiler_params=pltpu.CompilerParams(has_side_effects=DATAFLOW),
    )(_in_hbm(gpk), _in_hbm(lax.empty((N_DEV, HALF, D), gpk.dtype)), after)


def _reduce_late_wait(send_sems, recv_sems, in_thru, land_thru, after):
    def body(in_ref, land_ref, send_sems, recv_sems, after_ref, in_out, got_ref):
        x, y, c, chips = _place()
        peers = [(x, y, 1 - c)] + [(cx, cy, c) for cx, cy in chips] + [(cx, cy, 1 - c) for cx, cy in chips]
        for k, (px, py, pc) in enumerate(peers):
            cp = pltpu.make_async_remote_copy(src_ref=land_ref.at[0], dst_ref=land_ref.at[4 * px + 2 * py + pc],
                                              send_sem=send_sems.at[k], recv_sem=recv_sems.at[k],
                                              device_id=(px, py, pc), device_id_type=MESH)
            cp.wait_send()
            cp.wait_recv()

    return pl.pallas_call(
        body, name="reduce_late_wait",
        out_shape=(pltpu.HBM(in_thru.shape, in_thru.dtype), pltpu.HBM(land_thru.shape, land_thru.dtype)),
        in_specs=(HBM, HBM, SEM, SEM, ANY), out_specs=(HBM, HBM), input_output_aliases={0: 0, 1: 1},
        compiler_params=pltpu.CompilerParams(has_side_effects=DATAFLOW),
    )(in_thru, land_thru, send_sems, recv_sems, after)


def _reduce_late_add(didx, gpk, parts):
    HALF = parts.shape[1]
    th = HALF // 4
    nh = HALF // th
    assert th % 16 == 0

    def body(d_ref, own_ref, p_ref, o_ref):
        for me in range(N_DEV):
            @pl.when(d_ref[0] == me)
            def _(me=me):
                t = [(own_ref[0] if j == me else p_ref[j]).astype(F32) for j in range(N_DEV)]
                o_ref[...] = ((((((t[0] + t[1]) + t[2]) + t[3]) + t[4]) + t[5]) + t[6]) + t[7]

    gs = pltpu.PrefetchScalarGridSpec(
        num_scalar_prefetch=1, grid=(nh,),
        in_specs=[pl.BlockSpec((1, th, D), lambda i, d: (d[1], d[2] * nh + i, 0)), pl.BlockSpec((N_DEV, th, D), lambda i, d: (0, i, 0))],
        out_specs=pl.BlockSpec((th, D), lambda i, d: (d[2] * nh + i, 0)))
    return pl.pallas_call(body, name="reduce_late_add", grid_spec=gs, out_shape=SDS((2 * HALF, D), F32),
                          compiler_params=pltpu.CompilerParams(dimension_semantics=("parallel",),
                                                               vmem_limit_bytes=48 << 20))(didx, gpk, parts)


def _pack_early(b, dtype):
    lanes = lambda a: jnp.pad(a.astype(dtype), ((0, 0), (0, D - a.shape[1])))
    pair = jnp.concatenate([b["w_q_b"].astype(dtype), b["w_ple"].astype(dtype), jnp.zeros((256, D - 640), dtype)], axis=1)
    return jnp.concatenate([lanes(b["w_in"]), pair, lanes(b["w_kv_b"])], axis=0)


def _pack_late(b, dtype):
    return jnp.concatenate([b[n].astype(dtype) for n in ("w_mla_up", "w_swa_up", "w_out", "w_ple_gate", "w_mlp_up", "w_mlp_down")],
                           axis=0)


def _unpack_shards(pk, which):
    return {n: pk[PACK_AT[n][1]:PACK_AT[n][1] + r, PACK_AT[n][2]:PACK_AT[n][2] + c] for n, r, c in BIG if PACK_AT[n][0] == which}


def _full_weights(gathered, own, chip, which):
    own_b = _unpack_shards(own, which)
    per_chip = [{n: jnp.where(chip == j, own_b[n], blk) for n, blk in _unpack_shards(gathered[j], which).items()}
                for j in range(N_CHIPS)]
    out = {}
    for n in own_b:
        shards = [pc[n] for pc in per_chip]
        if n == "w_in":
            out["w_in_p"] = _w_in_internal(shards)
        else:
            out[n] = jnp.concatenate(shards, axis=1 if n in COL_SHARDED else 0)
    return out


def _split_full_grads(grads, pack, dtype):
    shard = {n: (r, c) for n, r, c in BIG}
    chunks = []
    for j in range(N_CHIPS):
        blocks = {}
        for n, g in grads.items():
            if n == "w_in_p":
                blocks["w_in"] = _w_in_grad_shard(g, j)
                continue
            r, c = shard[n]
            blocks[n] = g[:, j * c:(j + 1) * c] if n in COL_SHARDED else g[j * r:(j + 1) * r]
        chunks.append(pack(blocks, dtype))
    return jnp.stack(chunks)


W_IN_SHARD = 936
W_IN_SEGMENTS = ((0, 256, (3072,)), (256, 384, (3840,)), (384, 416, (4032,)), (416, 1440, (0,)), (1440, 1504, (3328, 3392)),
                 (1504, 1568, (3456, 3520)), (1568, 1632, (3584, 3648)), (1632, 1696, (3712, 3776)), (1696, 3744, (1024,)))


def _w_in_internal(shards):
    def cols(a, b):
        out = []
        for j, s in enumerate(shards):
            lo, hi = max(a, W_IN_SHARD * j), min(b, W_IN_SHARD * (j + 1))
            if lo < hi:
                out.append(s[:, lo - W_IN_SHARD * j:hi - W_IN_SHARD * j])
        return out

    pieces = {}
    for a, b, places in W_IN_SEGMENTS:
        for at in places:
            pieces[at] = cols(a, b)
    zeros = lambda n: [jnp.zeros((D, n), shards[0].dtype)]
    pieces[3968] = zeros(64)
    pieces[4064] = zeros(32)
    return jnp.concatenate([piece for at in sorted(pieces) for piece in pieces[at]], axis=1)


def _w_in_grad_shard(g, j):
    def internal(a, b):
        out = []
        while a < b:
            end = min(b, (a // D + 1) * D)
            out.append(g[a // D][:, a % D:a % D + end - a])
            a = end
        return out

    out = []
    for a, b, places in W_IN_SEGMENTS:
        lo, hi = max(a, W_IN_SHARD * j), min(b, W_IN_SHARD * (j + 1))
        if lo < hi:
            parts = [internal(at + lo - a, at + hi - a) for at in places]
            if len(parts) == 1:
                out += parts[0]
            else:
                assert len(parts[0]) == len(parts[1]) == 1
                out.append(parts[0][0] + parts[1][0])
    return jnp.concatenate(out, axis=1)


def _local_step(x, p, tgt, w, small, late_weights, late_grads_out):
    T = x.shape[0]
    tm = 256
    tb = 256
    w_in_p = w["w_in_p"]
    wqb = jnp.pad(w["w_q_b"].reshape(Q_LORA, MLA_HEADS, 96), ((0, 0), (0, 0), (0, 32))).reshape(Q_LORA, 2048)
    wkv = w["w_kv_b"].reshape(KV_LORA, MLA_HEADS, 128)
    wkn = jnp.pad(wkv[:, :, :64], ((0, 0), (0, 0), (0, 64))).reshape(KV_LORA, 2048)
    wv = wkv[:, :, 64:].reshape(KV_LORA, 1024)
    tab_m = _rope_tables(T, "mla")
    tab_s = _rope_tables(T, "swa")
    g1, gq, gkv, sinks = small["g_mix_pre"], small["g_q_a"], small["g_kv_a"], small["sinks"]
    g2, g3, g4, g5 = small["g_mix_post"], small["g_mlp_pre"], small["g_mlp_post"], small["g_ple"]
    sink_vec = sinks.reshape(SWA_HEADS)

    z, h1 = _fwd_in(x, g1, w_in_p, tm)
    qn, kvn, km, vm, qt, kt, vt, qs, ks, vs = _fwd_qkv(z, gq, gkv, wqb, wkn, wv, tab_m, tab_s, tb)
    om, lse_m = _mla_fwd(qt, km, vt, tb)
    os_, lse_s = _swa_fwd(sink_vec, qs, ks, vs)
    w = {**w, **late_weights((om, os_))}
    y, yo, au, bu, x1 = _fwd_mix(om, os_, z, x, w["w_mla_up"], w["w_swa_up"], w["w_out"], g2, tm)
    h2, u = _fwd_mlp_up(x1, g3, w["w_mlp_up"], tm)
    d, x2 = _fwd_mlp_down(u, w["w_mlp_down"], x1, g4, tm)
    loss, dx2, dgt, de0, dg5 = _ple_fwd_bwd(p, x2, tgt, w["w_ple"], g5, w["w_ple_gate"], tm)

    dd, da, dg4 = _bwd_mlp_down(dx2, d, g4, w["w_mlp_down"], u, tm)
    dx1, dg3 = _bwd_mlp_up(da, w["w_mlp_up"], x1, g3, dx2, tm)
    dyo, dg2, dau, dbu, dga, dgb, dos, delta_m, dom_t = _bwd_mix(dx1, yo, g2, w["w_out"], z, au, bu, w["w_mla_up"],
                                                                w["w_swa_up"], om, tb)
    gpk_late = lax.empty((N_CHIPS, PACK_ROWS["late"], D), BF16)
    for weight, a_, g_ in (("w_mla_up", om, dau), ("w_swa_up", os_, dbu), ("w_out", y, dyo), ("w_ple_gate", x2, dgt),
                           ("w_mlp_up", h2, da), ("w_mlp_down", u, dd)):
        gpk_late = _wgrad(a_, g_, "wgrad_" + weight[2:], into=(gpk_late, weight))
    token = late_grads_out(gpk_late)
    delta_m = delta_m + token[0, 0]
    dqm, dkm, dvm = _mla_bwd(qt, km, kt, vm, dom_t, lse_m, delta_m, tb)
    dqs, dkc, dkp, dvc, dvp, dsink = _swa_bwd(sink_vec, qs, ks, vs, dos, os_, lse_s)
    dqb, dknb, dvb, dsq, drest, dgq, dgkv = _bwd_qkv(dqm, dkm, dvm, dqs, dkc, dkp, dvc, dvp, z, gq, gkv, wqb, wkn, wv,
                                                      tab_m, tab_s)
    gx, dg1 = _bwd_in(dsq, dga, dgb, drest, w_in_p, x, g1, dx1, tm)

    g_in_p = [_wgrad(h1, dsq, "wgrad_in_sq"), _wgrad(h1, dga, "wgrad_in_ga"), _wgrad(h1, dgb, "wgrad_in_gb"),
              _wgrad(h1, drest, "wgrad_in_rest")]
    g_qb_p = _wgrad_t(dqb, qn, "wgrad_q_b").T
    g_kn_p = _wgrad_t(dknb, kvn, "wgrad_kv_b_nope").T
    g_v_p = _wgrad_t(dvb, kvn, "wgrad_kv_b_v").T
    grads = {
        "w_in_p": g_in_p,
        "w_q_b": g_qb_p.reshape(Q_LORA, MLA_HEADS, 128)[:, :, :96].reshape(Q_LORA, 1536),
        "w_kv_b": jnp.concatenate([g_kn_p.reshape(KV_LORA, MLA_HEADS, 128)[:, :, :64], g_v_p.reshape(KV_LORA, MLA_HEADS, 64)],
                                  axis=2).reshape(KV_LORA, 2048),
        "w_ple": _wgrad(p, de0, "wgrad_ple"),
    }
    small_grads = {"g_mix_pre": dg1, "g_q_a": dgq.reshape(1, Q_LORA), "g_kv_a": dgkv.reshape(1, KV_LORA), "sinks": dsink[0:1, 0:SWA_HEADS], "g_mix_post": dg2,
                   "g_mlp_pre": dg3, "g_mlp_post": dg4, "g_ple": dg5}
    return loss, gx, grads, small_grads


def _pack_small(vals, fill, scalar=None):
    wide = [vals[n] for n, k in SMALL if k == D]
    narrow = [vals[n] for n, k in SMALL if k != D]
    used = sum(k for _, k in SMALL if k != D)
    last = jnp.concatenate(narrow + [jnp.full((1, D - used), fill, F32)], axis=1)
    rest = jnp.full((2, D), fill, F32)
    if scalar is not None:
        rest = jnp.concatenate([jnp.concatenate([scalar, rest[0:1, 1:]], axis=1), rest[1:2]], axis=0)
    return jnp.concatenate(wide + [last, rest], axis=0)


def _unpack_small(pk):
    out, row, off = {}, 0, 0
    for n, k in SMALL:
        if k == D:
            out[n] = pk[row:row + 1]
            row += 1
    for n, k in SMALL:
        if k != D:
            out[n] = pk[5:6, off:off + k]
            off += k
    return out


def kernel(x, p, g_mix_pre, w_in, g_q_a, w_q_b, g_kv_a, w_kv_b, sinks, w_mla_up, w_swa_up, w_out, g_mix_post, g_mlp_pre, w_mlp_up, w_mlp_down, g_mlp_post, w_ple, g_ple, w_ple_gate, loss_target, m_g_mix_pre, m_w_in, m_g_q_a, m_w_q_b, m_g_kv_a, m_w_kv_b, m_sinks, m_w_mla_up, m_w_swa_up, m_w_out, m_g_mix_post, m_g_mlp_pre, m_w_mlp_up, m_w_mlp_down, m_g_mlp_post, m_w_ple, m_g_ple, m_w_ple_gate, v_g_mix_pre, v_w_in, v_g_q_a, v_w_q_b, v_g_kv_a, v_w_kv_b, v_sinks, v_w_mla_up, v_w_swa_up, v_w_out, v_g_mix_post, v_g_mlp_pre, v_w_mlp_up, v_w_mlp_down, v_g_mlp_post, v_w_ple, v_g_ple, v_w_ple_gate):
    given = dict(locals())
    big_w = {n: given[n][0] for n, _, _ in BIG}
    small_w = {n: given[n] for n, _ in SMALL}
    small_m = {n: given["m_" + n] for n, _ in SMALL}
    small_v = {n: given["v_" + n] for n, _ in SMALL}

    core = lax.axis_index("c")
    chip = 2 * lax.axis_index("x") + lax.axis_index("y")
    core_i = core.astype(jnp.int32).reshape(1)
    dev_i = jnp.stack([2 * chip + core, chip, core]).astype(jnp.int32)

    own_early = _pack_early(big_w, BF16)
    own_late = _pack_late(big_w, BF16)
    got_early = _all_gather(own_early)
    late_flight = _gather_late_start(own_late, got_early)
    weights = _full_weights(got_early, own_early, chip, "early")
    step_small = {**small_w, "g_mix_pre": small_w["g_mix_pre"] + late_flight[4][0, 0]}

    def late_weights(after):
        return _full_weights(_gather_late_wait(*late_flight[:4], after), own_late, chip, "late")

    flight = {}

    def late_grads_out(gpk_late):
        flight["late"] = _reduce_late_start(gpk_late, dev_i)
        return flight["late"][4]

    loss_blk, gx, grads, small_grads = _local_step(x[0], p[0, 0], loss_target[0], weights, step_small, late_weights,
                                                   late_grads_out)

    gpk = _split_full_grads(grads, _pack_early, BF16)
    got = _rs_sibling(gpk)
    part = _rs_add_sibling(core_i, gpk, got)
    small_own = _pack_small(small_grads, 0.0, loss_blk[0:1, 0:1])
    early_flight = _rs_chips_start(part, small_own, dev_i)

    out_g, out_d, out_m, out_v = {}, {}, {}, {}
    gpk_late, parts_late = _reduce_late_wait(*flight["late"][:4], early_flight[6])
    joined_late = _rs_join(_reduce_late_add(dev_i, gpk_late, parts_late), "rs_join_late")
    for n, _, _ in BIG:
        if PACK_AT[n][0] == "late":
            out_g[n], out_d[n], out_m[n], out_v[n] = _adamw(given[n], joined_late, given["m_" + n], given["v_" + n], n)

    part, parts, small_parts = _rs_chips_wait(*early_flight[:6], [out_d[n] for n in out_d])
    joined_early = _rs_join(_rs_add_chips(dev_i[1:3], part, parts), "rs_join_early")
    for n, _, _ in BIG:
        if PACK_AT[n][0] == "early":
            out_g[n], out_d[n], out_m[n], out_v[n] = _adamw(given[n], joined_early, given["m_" + n], given["v_" + n], n)

    mine = (lax.broadcasted_iota(jnp.int32, (N_DEV, 1, 1), 0) == dev_i[0])
    g_small_pk, d_small_pk, m_small_pk, v_small_pk = _adamw_small(
        _pack_small(small_w, 0.0), jnp.where(mine, small_own[None], small_parts), _pack_small(small_m, 0.0),
        _pack_small(small_v, 1.0))
    loss = g_small_pk[6, 0]
    for out, pk in ((out_g, g_small_pk), (out_d, d_small_pk), (out_m, m_small_pk), (out_v, v_small_pk)):
        out.update(_unpack_small(pk))
    order = ["g_mix_pre", "w_in", "g_q_a", "w_q_b", "g_kv_a", "w_kv_b", "sinks", "w_mla_up", "w_swa_up", "w_out", "g_mix_post",
             "g_mlp_pre", "w_mlp_up", "w_mlp_down", "g_mlp_post", "w_ple", "g_ple", "w_ple_gate"]
    return (loss, gx[None], *[out_g[n] for n in order], *[out_d[n] for n in order], *[out_m[n] for n in order],
            *[out_v[n] for n in order])
```

```python
import math

import jax
import jax.numpy as jnp
from jax import lax
from jax.experimental import pallas as pl
from jax.experimental.pallas import tpu as pltpu

F32 = jnp.float32
BF16 = jnp.bfloat16
SDS = jax.ShapeDtypeStruct

D = 1024
D_FF = 4096
PLE = 256
Q_LORA = 256
KV_LORA = 128
MLA_HEADS = 16
MLA_NOPE = 64
MLA_ROPE = 32
SWA_HEADS = 16
SWA_HD = 64
WINDOW = 128
ROPE_THETA = 10000.0
EPS = 1e-6
NEG = -1e30
NZ = 4096
MLA_SCALE = (MLA_NOPE + MLA_ROPE) ** -0.5
LOG2_E = math.log2(math.e)
MLA_LOG2_SCALE = MLA_SCALE * LOG2_E
SWA_SCALE = SWA_HD ** -0.5

ADAM_LR = 0.001
ADAM_B1 = 0.9
ADAM_B2 = 0.999
ADAM_EPS = 1e-08
ADAM_WD = 0.01
ADAM_STEP = 10

LANES = 128
ATT_COLS = 128
VT_ROWS = 80
N_CHIPS = 4
N_DEV = 8
MESH = pl.DeviceIdType.MESH

NT = (((1,), (1,)), ((), ()))
TN = (((0,), (0,)), ((), ()))

BIG = (("w_in", 1024, 936), ("w_q_b", 256, 384), ("w_kv_b", 128, 512), ("w_mla_up", 256, 1024),
       ("w_swa_up", 256, 1024), ("w_out", 256, 1024), ("w_mlp_up", 1024, 1024), ("w_mlp_down", 1024, 1024),
       ("w_ple", 256, 256), ("w_ple_gate", 256, 1024))
COL_SHARDED = ("w_in", "w_q_b", "w_kv_b", "w_mlp_up", "w_ple")
PACK_AT = {"w_in": ("early", 0, 0), "w_q_b": ("early", 1024, 0), "w_ple": ("early", 1024, 384), "w_kv_b": ("early", 1280, 0),
           "w_mla_up": ("late", 0, 0), "w_swa_up": ("late", 256, 0), "w_out": ("late", 512, 0), "w_ple_gate": ("late", 768, 0),
           "w_mlp_up": ("late", 1024, 0), "w_mlp_down": ("late", 2048, 0)}
PACK_ROWS = {"early": 1408, "late": 3072}
SMALL = (("g_mix_pre", 1024), ("g_q_a", 256), ("g_kv_a", 128), ("sinks", 16), ("g_mix_post", 1024),
         ("g_mlp_pre", 1024), ("g_mlp_post", 1024), ("g_ple", 1024))


def _dot(a, b):
    return jnp.dot(a, b, preferred_element_type=F32)


def _dot_nt(a, b):
    return lax.dot_general(a, b, NT, preferred_element_type=F32)


def _dot_tn(a, b):
    return lax.dot_general(a, b, TN, preferred_element_type=F32)


def _pcall(body, *, name, out_shape, grid=(), in_specs=None, out_specs=None, scratch=(), sem=None, vmem_mb=48, aliases=None):
    params = dict(vmem_limit_bytes=vmem_mb << 20)
    if sem is not None:
        params["dimension_semantics"] = sem
    return pl.pallas_call(body, name=name, grid=grid, in_specs=in_specs, out_specs=out_specs, out_shape=out_shape,
                          scratch_shapes=list(scratch), input_output_aliases=aliases or {},
                          compiler_params=pltpu.CompilerParams(**params))


def _rows(tm, n, col=0):
    return pl.BlockSpec((tm, n), lambda i: (i, col))


def _full(shape):
    return pl.BlockSpec(shape, lambda i: (0,) * len(shape))


def _rms(x, g):
    r = lax.rsqrt(jnp.mean(x * x, axis=-1, keepdims=True) + EPS)
    return x * r * g


def _rms_bwd(dy, x, g):
    r = lax.rsqrt(jnp.mean(x * x, axis=-1, keepdims=True) + EPS)
    xn = x * r
    dn = dy * g
    dx = r * (dn - xn * jnp.mean(dn * xn, axis=-1, keepdims=True))
    return dx, jnp.sum(dy * xn, axis=0, keepdims=True)


def _sigmoid(x):
    return 1.0 / (1.0 + jnp.exp(-x))


def _rope(x, c, a, b, half):
    return x * c + pltpu.roll(x, LANES - half, 1) * a + pltpu.roll(x, half, 1) * b


def _rope_tables(T, kind):
    lane = jnp.arange(LANES)
    if kind == "mla":
        half = MLA_ROPE // 2
        rel = lane - MLA_NOPE
        on = (rel >= 0) & (rel < MLA_ROPE)
        d = MLA_ROPE
    else:
        half = SWA_HD // 2
        rel = lane % SWA_HD
        on = jnp.ones((LANES,), bool)
        d = SWA_HD
    first = on & (rel < half)
    second = on & (rel >= half)
    f = jnp.where(first, rel, rel - half).astype(F32)
    inv = jnp.exp(-math.log(ROPE_THETA) * f * (2.0 / d))
    ang = jnp.arange(T, dtype=F32)[:, None] * inv[None, :]
    cos, sin = jnp.cos(ang), jnp.sin(ang)
    c = jnp.where(on[None], cos, 1.0)
    a = jnp.where(first[None], -sin, 0.0)
    b = jnp.where(second[None], sin, 0.0)
    return c, a, b


def _fwd_in(x, g1, w_in_p, tm):
    T = x.shape[0]

    def body(x_ref, g_ref, w_ref, z_ref, h_ref):
        h = _rms(x_ref[...], g_ref[...]).astype(BF16)
        h_ref[...] = h
        z_ref[...] = _dot(h, w_ref[...])

    return _pcall(body, name="fwd_in", grid=(T // tm,),
                  in_specs=[_rows(tm, D), _full((1, D)), _full((D, NZ))],
                  out_specs=[_rows(tm, NZ), _rows(tm, D)],
                  out_shape=[SDS((T, NZ), F32), SDS((T, D), BF16)], sem=("parallel",))(x, g1, w_in_p)


def _fwd_qkv(z, gq, gkv, wqb, wkn, wv, tab_m, tab_s, tm):
    T = z.shape[0]
    wqb_t, wkn_t, wv_t = wqb.T, wkn.T, wv.T
    tab_mt = [t.T for t in tab_m]

    def body(qa_ref, sq_ref, skd_ref, svd_ref, kva_ref, kr_ref, gq_ref, gkv_ref, wkn_ref, wv_ref, wqbt_ref, wknt_ref, wvt_ref,
             cm_ref, am_ref, bm_ref, cmt_ref, amt_ref, bmt_ref, cs_ref, as_ref, bs_ref,
             qn_ref, kvn_ref, km_ref, vm_ref, qt_ref, kt_ref, vt_ref, qs_ref, ks_ref, vs_ref):
        qn = _rms(qa_ref[...], gq_ref[...])
        qn_ref[...] = qn.astype(BF16)
        kvn = _rms(kva_ref[...], gkv_ref[...])
        kvn_b = kvn.astype(BF16)
        kvn_ref[...] = kvn_b
        qn_t = qn.T.astype(BF16)
        kvn_t = kvn.T.astype(BF16)
        cm, am, bm = cm_ref[...], am_ref[...], bm_ref[...]
        cmt, amt, bmt = cmt_ref[...], amt_ref[...], bmt_ref[...]
        cs, as_, bs = cs_ref[...], as_ref[...], bs_ref[...]
        k_rope = _rope(kr_ref[...], cm, am, bm, MLA_ROPE // 2)
        k_rope_t = k_rope.T
        half = MLA_ROPE // 2
        vm_ref[...] = _dot(kvn_b, wv_ref[...]).astype(BF16)
        km_all = _dot(kvn_b, wkn_ref[...])
        v_t = _dot(wvt_ref[...], kvn_t)
        q_t = _dot(wqbt_ref[...], qn_t)
        k_t = _dot(wknt_ref[...], kvn_t)
        ones_row = jnp.where(lax.broadcasted_iota(jnp.int32, (64, tm), 0) == 0, 1.0, 0.0)
        for h in range(MLA_HEADS):
            sl = slice(LANES * h, LANES * (h + 1))
            vt_ref[0, sl, :] = jnp.concatenate([v_t[64 * h:64 * (h + 1)], ones_row], axis=0).astype(BF16)
            qh = q_t[sl]
            qt_ref[0, sl, :] = (qh * cmt + pltpu.roll(qh, LANES - half, 0) * amt + pltpu.roll(qh, half, 0) * bmt).astype(BF16)
            km_ref[:, sl] = (km_all[:, sl] + k_rope).astype(BF16)
            kt_ref[0, sl, :] = (k_t[sl] + k_rope_t).astype(BF16)
        for j in range(D // LANES):
            sl = slice(LANES * j, LANES * (j + 1))
            qs_ref[:, sl] = _rope(sq_ref[:, sl], cs, as_, bs, SWA_HD // 2).astype(BF16)
        for j in range(2):
            sl = slice(LANES * j, LANES * (j + 1))
            ks_ref[:, sl] = _rope(skd_ref[:, sl], cs, as_, bs, SWA_HD // 2).astype(BF16)
        vs_ref[...] = svd_ref[...].astype(BF16)

    tab = [_rows(tm, LANES)] * 3
    tab_t = [pl.BlockSpec((LANES, tm), lambda i: (0, i))] * 3
    return _pcall(body, name="fwd_qkv", grid=(T // tm,),
                  in_specs=[_rows(tm, 256, 12), _rows(tm, 1024, 0), _rows(tm, 256, 13), _rows(tm, 256, 14),
                            _rows(tm, 128, 30), _rows(tm, 128, 31), _full((1, Q_LORA)), _full((1, KV_LORA)),
                            _full((KV_LORA, 2048)), _full((KV_LORA, 1024)), _full((2048, Q_LORA)), _full((2048, KV_LORA)),
                            _full((1024, KV_LORA))] + tab + tab_t + tab,
                  out_specs=[_rows(tm, Q_LORA), _rows(tm, KV_LORA), _rows(tm, 2048), _rows(tm, 1024),
                             pl.BlockSpec((1, 2048, tm), lambda i: (i, 0, 0)), pl.BlockSpec((1, 2048, tm), lambda i: (i, 0, 0)),
                             pl.BlockSpec((1, 2048, tm), lambda i: (i, 0, 0)),
                             _rows(tm, 1024), _rows(tm, 256), _rows(tm, 256)],
                  out_shape=[SDS((T, Q_LORA), BF16), SDS((T, KV_LORA), BF16), SDS((T, 2048), BF16),
                             SDS((T, 1024), BF16), SDS((T // tm, 2048, tm), BF16), SDS((T // tm, 2048, tm), BF16),
                             SDS((T // tm, 2048, tm), BF16),
                             SDS((T, 1024), BF16), SDS((T, 256), BF16), SDS((T, 256), BF16)],
                  sem=("parallel",))(z, z, z, z, z, z, gq, gkv, wkn, wv, wqb_t, wkn_t, wv_t, *tab_m, *tab_mt, *tab_s)


def _mla_fwd(qt, km, vt, tb):
    T = km.shape[0]
    nb = T // tb
    cc = ATT_COLS

    per = 2 if nb % 2 == 0 else 1

    def body(q_ref, k_ref, vt_ref, o_ref, l_ref, s_ref, p_ref, al_ref, m_ref, acc_ref):
        for blk in range(per):
            one_block(per * pl.program_id(1) + blk, blk, q_ref, k_ref, vt_ref, o_ref, l_ref, s_ref, p_ref, al_ref, m_ref, acc_ref)

    def one_block(i, blk, q_ref, k_ref, vt_ref, o_ref, l_ref, s_ref, p_ref, al_ref, m_ref, acc_ref):
        m_ref[...] = jnp.full(m_ref.shape, NEG, F32)
        acc_ref[...] = jnp.zeros_like(acc_ref)
        p_ref[1] = jnp.zeros(p_ref.shape[1:], BF16)
        al_ref[1] = jnp.ones(al_ref.shape[1:], F32)
        key = lax.broadcasted_iota(jnp.int32, (tb, cc), 0)
        qry = lax.broadcasted_iota(jnp.int32, (tb, cc), 1)

        def scores(j, slot):
            off = pl.multiple_of(j * tb, tb)
            for hh in range(2):
                sl = slice(LANES * hh, LANES * (hh + 1))
                s_ref[slot, hh] = _dot(k_ref[pl.ds(off, tb), sl], q_ref[blk, sl, :])

        def softmax(slot, diagonal):
            chains = [(hh, slice(cc * c, cc * (c + 1)), c) for hh in range(2) for c in range(tb // cc)]

            def scaled(hh, cols, c):
                t = s_ref[slot, hh, :, cols] * MLA_LOG2_SCALE
                return jnp.where(key <= qry + cc * c, t, NEG) if diagonal else t

            tops = []
            for hh, cols, c in chains:
                if diagonal:
                    top = jnp.max(scaled(hh, cols, c), axis=0, keepdims=True)
                else:
                    top = jnp.max(s_ref[slot, hh, :, cols], axis=0, keepdims=True) * MLA_LOG2_SCALE
                m_old = m_ref[hh, :, cols]
                mn = jnp.maximum(m_old, top)
                m_ref[hh, :, cols] = mn
                al_ref[slot, hh, :, cols] = jnp.exp2(m_old - mn)
                tops.append(mn)
            for (hh, cols, c), mn in zip(chains, tops):
                p_ref[slot, hh, :, cols] = jnp.exp2(scaled(hh, cols, c) - mn).astype(BF16)

        def accumulate(j, slot):
            for hh in range(2):
                acc_ref[hh] = al_ref[slot, hh] * acc_ref[hh] + _dot(vt_ref[j, LANES * hh:LANES * hh + VT_ROWS, :], p_ref[slot, hh])

        def step(t, carry):
            scores(2 * t + 1, 1)
            accumulate(jnp.maximum(2 * t - 1, 0), 1)
            softmax(0, False)
            scores(2 * t + 2, 0)
            accumulate(2 * t, 0)
            softmax(1, False)
            return carry

        scores(0, 0)
        lax.fori_loop(0, i // 2, step, 0)

        @pl.when(i % 2 == 1)
        def _():
            scores(i, 1)
            accumulate(jnp.maximum(i - 2, 0), 1)
            softmax(0, False)
            accumulate(i - 1, 0)
            softmax(1, True)
            accumulate(i, 1)

        @pl.when(i % 2 == 0)
        def _():
            accumulate(jnp.maximum(i - 1, 0), 1)
            softmax(0, True)
            accumulate(i, 0)
        den = [acc_ref[hh, 64:65, :] for hh in range(2)]
        o_ref[tb * blk:tb * (blk + 1), :] = jnp.concatenate([acc_ref[hh, 0:64, :] / den[hh] for hh in range(2)], axis=0).T
        sub = lax.broadcasted_iota(jnp.int32, (8, tb), 0)
        lse = [m_ref[hh] + jnp.log(den[hh]) * LOG2_E for hh in range(2)]
        l_ref[0, blk] = jnp.where(sub == 0, lse[0], jnp.where(sub == 1, lse[1], 0.0))

    return _pcall(body, name="mla_fwd", grid=(MLA_HEADS // 2, nb // per),
                  in_specs=[pl.BlockSpec((per, 256, tb), lambda p, i: (i, p, 0)), pl.BlockSpec((T, 256), lambda p, i: (0, p)),
                            pl.BlockSpec((nb, 2 * LANES, tb), lambda p, i: (0, p, 0))],
                  out_specs=[pl.BlockSpec((per * tb, LANES), lambda p, i: (i, p)),
                             pl.BlockSpec((1, per, 8, tb), lambda p, i: (p, i, 0, 0))],
                  out_shape=[SDS((T, D), F32), SDS((MLA_HEADS // 2, nb, 8, tb), F32)],
                  scratch=[pltpu.VMEM((2, 2, tb, tb), F32), pltpu.VMEM((2, 2, tb, tb), BF16), pltpu.VMEM((2, 2, 1, tb), F32),
                           pltpu.VMEM((2, 1, tb), F32), pltpu.VMEM((2, VT_ROWS, tb), F32)],
                  sem=("parallel", "arbitrary"))(qt, km, vt)


def _swa_mask(n):
    row = lax.broadcasted_iota(jnp.int32, (WINDOW, 2 * WINDOW), 0)
    col = lax.broadcasted_iota(jnp.int32, (WINDOW, 2 * WINDOW), 1)
    rel = row - col + WINDOW
    return (rel >= 0) & (rel < WINDOW) & ((col >= WINDOW) | (n > 0))


def _swa_specs(T):
    nb = T // WINDOW
    cur = lambda w: pl.BlockSpec((WINDOW, w), lambda n: (n, 0))
    prev = lambda w: pl.BlockSpec((WINDOW, w), lambda n: (jnp.maximum(n - 1, 0), 0))
    return nb, cur, prev


def _swa_fwd(sinks, qs, ks, vs):
    T = qs.shape[0]
    nb, cur, prev = _swa_specs(T)

    def body(sink_ref, q_ref, kc_ref, kp_ref, vc_ref, vp_ref, o_ref, l_ref, kb_ref, vb_ref, s_ref, p_ref):
        n = pl.program_id(0)
        mask = _swa_mask(n)
        lo = lax.broadcasted_iota(jnp.int32, (WINDOW, LANES), 1) < 64
        hi = jnp.logical_not(lo)
        for g in range(2):
            gs = slice(LANES * g, LANES * (g + 1))
            kb_ref[g] = jnp.concatenate([kp_ref[:, gs], kc_ref[:, gs]], axis=0)
            vb_ref[g] = jnp.concatenate([vp_ref[:, gs], vc_ref[:, gs]], axis=0)
        for h in range(SWA_HEADS):
            qp = q_ref[:, LANES * (h // 2):LANES * (h // 2 + 1)]
            qh = jnp.where(lo if h % 2 == 0 else hi, qp, jnp.zeros_like(qp))
            s_ref[h] = _dot_nt(qh, kb_ref[h // 8])
        for j in range(SWA_HEADS // 2):
            sl = slice(LANES * j, LANES * (j + 1))
            lses = []
            for h in (2 * j, 2 * j + 1):
                s = jnp.where(mask, s_ref[h] * SWA_SCALE, NEG)
                sk = sink_ref[h]
                m = jnp.maximum(jnp.max(s, axis=1, keepdims=True), sk)
                e = jnp.exp(s - m)
                den = jnp.sum(e, axis=1, keepdims=True) + jnp.exp(sk - m)
                p_ref[h] = (e / den).astype(BF16)
                lses.append(jnp.broadcast_to(m + jnp.log(den), (WINDOW, LANES)))
            l_ref[:, sl] = jnp.where(lo, lses[0], lses[1])
        for j in range(SWA_HEADS // 2):
            vb = vb_ref[j // 4]
            o_ref[:, LANES * j:LANES * (j + 1)] = jnp.where(lo, _dot(p_ref[2 * j], vb), _dot(p_ref[2 * j + 1], vb))

    return _pcall(body, name="swa_fwd", grid=(nb,),
                  in_specs=[pl.BlockSpec(memory_space=pltpu.SMEM), cur(D), cur(256), prev(256), cur(256), prev(256)],
                  out_specs=[cur(D), cur(D)], out_shape=[SDS((T, D), F32)] * 2,
                  scratch=[pltpu.VMEM((2, 2 * WINDOW, LANES), BF16), pltpu.VMEM((2, 2 * WINDOW, LANES), BF16),
                           pltpu.VMEM((SWA_HEADS, WINDOW, 2 * WINDOW), F32), pltpu.VMEM((SWA_HEADS, WINDOW, 2 * WINDOW), BF16)],
                  sem=("parallel",))(sinks, qs, ks, ks, vs, vs)


def _fwd_mix(om, os_, z, x, wmu, wsu, wo, g2, tm):
    T = x.shape[0]

    def body(om_ref, os_ref, ga_ref, gb_ref, x_ref, wmu_ref, wsu_ref, wo_ref, g2_ref,
             y_ref, yo_ref, au_ref, bu_ref, x1_ref):
        au = _dot(om_ref[...].astype(BF16), wmu_ref[...])
        bu = _dot(os_ref[...].astype(BF16), wsu_ref[...])
        au_ref[...] = au
        bu_ref[...] = bu
        y = (_sigmoid(ga_ref[...]) * au + _sigmoid(gb_ref[...]) * bu).astype(BF16)
        y_ref[...] = y
        yo = _dot(y, wo_ref[...])
        yo_ref[...] = yo
        x1_ref[...] = x_ref[...] + _rms(yo, g2_ref[...])

    r = _rows(tm, D)
    w = _full((D, D))
    return _pcall(body, name="fwd_mix", grid=(T // tm,),
                  in_specs=[r, r, _rows(tm, D, 1), _rows(tm, D, 2), r, w, w, w, _full((1, D))],
                  out_specs=[r] * 5,
                  out_shape=[SDS((T, D), BF16), SDS((T, D), F32), SDS((T, D), F32), SDS((T, D), F32), SDS((T, D), F32)],
                  sem=("parallel",))(om, os_, z, z, x, wmu, wsu, wo, g2)


def _fwd_mlp_up(x1, g3, w1, tm):
    T = x1.shape[0]

    def body(x_ref, g_ref, w_ref, h_ref, u_ref):
        h = _rms(x_ref[...], g_ref[...]).astype(BF16)
        h_ref[...] = h
        u_ref[...] = jnp.square(jnp.maximum(_dot(h, w_ref[...]), 0.0)).astype(BF16)

    return _pcall(body, name="fwd_mlp_up", grid=(T // tm,),
                  in_specs=[_rows(tm, D), _full((1, D)), _full((D, D_FF))],
                  out_specs=[_rows(tm, D), _rows(tm, D_FF)],
                  out_shape=[SDS((T, D), BF16), SDS((T, D_FF), BF16)],
                  sem=("parallel",))(x1, g3, w1)


def _fwd_mlp_down(u, w2, x1, g4, tm):
    T = x1.shape[0]

    def body(u_ref, w_ref, x_ref, g_ref, d_ref, x2_ref):
        d = _dot(u_ref[...], w_ref[...])
        d_ref[...] = d
        x2_ref[...] = x_ref[...] + _rms(d, g_ref[...])

    return _pcall(body, name="fwd_mlp_down", grid=(T // tm,),
                  in_specs=[_rows(tm, D_FF), _full((D_FF, D)), _rows(tm, D), _full((1, D))],
                  out_specs=[_rows(tm, D), _rows(tm, D)], out_shape=[SDS((T, D), F32)] * 2,
                  sem=("parallel",))(u, w2, x1, g4)


def _ple_fwd_bwd(p, x2, tgt, wple, g5, wpg, tm):
    T = x2.shape[0]

    def body(p_ref, x2_ref, t_ref, wple_ref, g5_ref, wpg_ref, loss_ref, dx2_ref, dgt_ref, de0_ref, dg5_ref):
        @pl.when(pl.program_id(0) == 0)
        def _():
            loss_ref[...] = jnp.zeros_like(loss_ref)
            dg5_ref[...] = jnp.zeros_like(dg5_ref)

        e0 = _dot(p_ref[...].astype(BF16), wple_ref[...])
        g5 = g5_ref[...]
        r = lax.rsqrt(jnp.mean(e0 * e0, axis=-1, keepdims=True) + EPS)
        en = e0 * r
        e = en * g5
        x2 = x2_ref[...]
        s = _sigmoid(_dot(x2.astype(BF16), wpg_ref[...]))
        diff = x2 + s * e - t_ref[...]
        sq = jnp.sum(jnp.sum(diff * diff, axis=1, keepdims=True), axis=0, keepdims=True)
        loss_ref[...] += jnp.broadcast_to(sq * (0.5 / D), loss_ref.shape)
        dx3 = diff * (1.0 / D)
        de = dx3 * s
        dgt = (dx3 * e * s * (1.0 - s)).astype(BF16)
        dgt_ref[...] = dgt
        dn = de * g5
        de0_ref[...] = (r * (dn - en * jnp.mean(dn * en, axis=-1, keepdims=True))).astype(BF16)
        dg5_ref[...] += jnp.sum(de * en, axis=0, keepdims=True)
        dx2_ref[...] = dx3 + _dot_nt(dgt, wpg_ref[...])

    r = _rows(tm, D)
    return _pcall(body, name="ple_fwd_bwd", grid=(T // tm,),
                  in_specs=[_rows(tm, PLE), r, r, _full((PLE, D)), _full((1, D)), _full((D, D))],
                  out_specs=[_full((8, LANES)), r, r, r, _full((1, D))],
                  out_shape=[SDS((8, LANES), F32), SDS((T, D), F32), SDS((T, D), BF16), SDS((T, D), BF16), SDS((1, D), F32)],
                  sem=("arbitrary",))(p, x2, tgt, wple, g5, wpg)


def _bwd_mlp_down(dx2, d, g4, w2, u, tm):
    T = dx2.shape[0]

    def body(dx_ref, d_ref, g_ref, w_ref, u_ref, dd_ref, da_ref, dg_ref):
        @pl.when(pl.program_id(0) == 0)
        def _():
            dg_ref[...] = jnp.zeros_like(dg_ref)

        dd, dg = _rms_bwd(dx_ref[...], d_ref[...], g_ref[...])
        dg_ref[...] += dg
        ddb = dd.astype(BF16)
        dd_ref[...] = ddb
        du = _dot_nt(ddb, w_ref[...])
        da_ref[...] = (du * (2.0 * jnp.sqrt(u_ref[...].astype(F32)))).astype(BF16)

    return _pcall(body, name="bwd_mlp_down", grid=(T // tm,),
                  in_specs=[_rows(tm, D), _rows(tm, D), _full((1, D)), _full((D_FF, D)), _rows(tm, D_FF)],
                  out_specs=[_rows(tm, D), _rows(tm, D_FF), _full((1, D))],
                  out_shape=[SDS((T, D), BF16), SDS((T, D_FF), BF16), SDS((1, D), F32)],
                  sem=("arbitrary",))(dx2, d, g4, w2, u)


def _bwd_mlp_up(da, w1, x1, g3, dx2, tm):
    T = dx2.shape[0]

    def body(da_ref, w_ref, x_ref, g_ref, dx2_ref, dx1_ref, dg_ref):
        @pl.when(pl.program_id(0) == 0)
        def _():
            dg_ref[...] = jnp.zeros_like(dg_ref)

        dh = _dot_nt(da_ref[...], w_ref[...])
        dx, dg = _rms_bwd(dh, x_ref[...], g_ref[...])
        dg_ref[...] += dg
        dx1_ref[...] = dx2_ref[...] + dx

    return _pcall(body, name="bwd_mlp_up", grid=(T // tm,),
                  in_specs=[_rows(tm, D_FF), _full((D, D_FF)), _rows(tm, D), _full((1, D)), _rows(tm, D)],
                  out_specs=[_rows(tm, D), _full((1, D))],
                  out_shape=[SDS((T, D), F32), SDS((1, D), F32)], sem=("arbitrary",))(da, w1, x1, g3, dx2)


def _bwd_mix(dx1, yo, g2, wo, z, au, bu, wmu, wsu, om, tm):
    T = dx1.shape[0]

    def body(dx_ref, yo_ref, g_ref, wo_ref, ga_ref, gb_ref, au_ref, bu_ref, wmu_ref, wsu_ref, om_ref,
             dyo_ref, dg_ref, dau_ref, dbu_ref, dga_ref, dgb_ref, dos_ref, dl_ref, dot_ref):
        @pl.when(pl.program_id(0) == 0)
        def _():
            dg_ref[...] = jnp.zeros_like(dg_ref)

        dyo, dg = _rms_bwd(dx_ref[...], yo_ref[...], g_ref[...])
        dg_ref[...] += dg
        dyob = dyo.astype(BF16)
        dyo_ref[...] = dyob
        dy = _dot_nt(dyob, wo_ref[...])
        sa = _sigmoid(ga_ref[...])
        sb = _sigmoid(gb_ref[...])
        dau = (dy * sa).astype(BF16)
        dbu = (dy * sb).astype(BF16)
        dau_ref[...] = dau
        dbu_ref[...] = dbu
        dga_ref[...] = (dy * au_ref[...] * sa * (1.0 - sa)).astype(BF16)
        dgb_ref[...] = (dy * bu_ref[...] * sb * (1.0 - sb)).astype(BF16)
        dom = _dot_nt(dau, wmu_ref[...])
        dos_ref[...] = _dot_nt(dbu, wsu_ref[...])
        prod = dom * om_ref[...]
        sub = lax.broadcasted_iota(jnp.int32, (8, tm), 0)
        for pr in range(MLA_HEADS // 2):
            sl = slice(LANES * pr, LANES * (pr + 1))
            pt = prod[:, sl].T
            d0 = jnp.sum(pt[0:64], axis=0, keepdims=True)
            d1 = jnp.sum(pt[64:128], axis=0, keepdims=True)
            dl_ref[pr, 0] = jnp.where(sub == 0, d0, jnp.where(sub == 1, d1, 0.0))
            dot_ref[0, sl, :] = dom[:, sl].T.astype(BF16)

    r = _rows(tm, D)
    w = _full((D, D))
    return _pcall(body, name="bwd_mix", grid=(T // tm,),
                  in_specs=[r, r, _full((1, D)), w, _rows(tm, D, 1), _rows(tm, D, 2), r, r, w, w, r],
                  out_specs=[r, _full((1, D)), r, r, r, r, r, pl.BlockSpec((MLA_HEADS // 2, 1, 8, tm), lambda i: (0, i, 0, 0)),
                             pl.BlockSpec((1, D, tm), lambda i: (i, 0, 0))],
                  out_shape=[SDS((T, D), BF16), SDS((1, D), F32), SDS((T, D), BF16), SDS((T, D), BF16), SDS((T, D), BF16),
                             SDS((T, D), BF16), SDS((T, D), F32), SDS((MLA_HEADS // 2, T // tm, 8, tm), F32),
                             SDS((T // tm, D, tm), BF16)],
                  sem=("arbitrary",))(dx1, yo, g2, wo, z, z, au, bu, wmu, wsu, om)


def _mla_bwd(qt, km, kt, vm, dot, lse, delta, tb):
    T = km.shape[0]
    nb = T // tb
    cc = ATT_COLS

    per = 2 if nb % 2 == 0 else 1

    def body(qt_ref, k_ref, kt_ref, v_ref, dot_ref, l_ref, dl_ref, dqt_ref, dkt_ref, dvt_ref,
             s_ref, dp_ref, p_ref, ds_ref, vh_ref):
        @pl.when(pl.program_id(1) == 0)
        def _():
            dqt_ref[...] = jnp.zeros_like(dqt_ref)

        dkt_ref[...] = jnp.zeros_like(dkt_ref)
        dvt_ref[...] = jnp.zeros_like(dvt_ref)
        for blk in range(per):
            one_block(per * pl.program_id(1) + blk, blk, qt_ref, k_ref, kt_ref, v_ref, dot_ref, l_ref, dl_ref, dqt_ref, dkt_ref,
                      dvt_ref, s_ref, dp_ref, p_ref, ds_ref, vh_ref)

    def one_block(j, blk, qt_ref, k_ref, kt_ref, v_ref, dot_ref, l_ref, dl_ref, dqt_ref, dkt_ref, dvt_ref,
                  s_ref, dp_ref, p_ref, ds_ref, vh_ref):
        lo = lax.broadcasted_iota(jnp.int32, (tb, LANES), 1) < 64
        key = lax.broadcasted_iota(jnp.int32, (tb, cc), 0)
        qry = lax.broadcasted_iota(jnp.int32, (tb, cc), 1)
        rows_j = slice(tb * blk, tb * (blk + 1))
        v = v_ref[rows_j, :]
        vh_ref[0] = jnp.where(lo, v, jnp.zeros_like(v))
        vh_ref[1] = jnp.where(lo, jnp.zeros_like(v), v)

        def scores(i, slot):
            for hh in range(2):
                sl = slice(LANES * hh, LANES * (hh + 1))
                s_ref[slot, hh] = _dot(k_ref[rows_j, sl], qt_ref[i, sl, :])
                dp_ref[slot, hh] = _dot(vh_ref[hh], dot_ref[i])

        def grads(i, slot, diagonal):
            lse_i = l_ref[0, i]
            delta_i = dl_ref[0, i]
            for hh in range(2):
                for c in range(tb // cc):
                    cols = slice(cc * c, cc * (c + 1))
                    p = jnp.exp2(s_ref[slot, hh, :, cols] * MLA_LOG2_SCALE - lse_i[hh:hh + 1, cols])
                    if diagonal:
                        p = jnp.where(key <= qry + cc * c, p, 0.0)
                    p_ref[hh, :, cols] = p.astype(BF16)
                    ds_ref[hh, :, cols] = (p * (dp_ref[slot, hh, :, cols] - delta_i[hh:hh + 1, cols]) * MLA_SCALE).astype(BF16)
            for hh in range(2):
                sl = slice(LANES * hh, LANES * (hh + 1))
                half = slice(64 * hh, 64 * (hh + 1))
                dvt_ref[blk, half, :] += _dot_nt(dot_ref[i, half, :], p_ref[hh])
                real = slice(LANES * hh, LANES * hh + MLA_NOPE + MLA_ROPE)
                dkt_ref[blk, real, :] += _dot_nt(qt_ref[i, real, :], ds_ref[hh])
                dqt_ref[i, real, :] += _dot(kt_ref[blk, real, :], ds_ref[hh])

        n_off = nb - 1 - j

        def step(u, carry):
            i0 = j + 1 + 2 * u
            scores(i0 + 1, 1)
            grads(i0, 0, False)
            scores(jnp.where(i0 + 2 < nb, i0 + 2, j), 0)
            grads(i0 + 1, 1, False)
            return carry

        scores(jnp.where(n_off > 0, j + 1, j), 0)
        lax.fori_loop(0, n_off // 2, step, 0)

        @pl.when(n_off % 2 == 1)
        def _():
            scores(j, 1)
            grads(nb - 1, 0, False)
            grads(j, 1, True)

        @pl.when(n_off % 2 == 0)
        def _():
            grads(j, 0, True)

    blk = lambda w: pl.BlockSpec((per * tb, w), lambda p, j: (j, p))
    stat = pl.BlockSpec((1, nb, 8, tb), lambda p, j: (p, 0, 0, 0))
    pair_t = lambda w: pl.BlockSpec((nb, w, tb), lambda p, j: (0, p, 0))
    blk_t = lambda w: pl.BlockSpec((per, w, tb), lambda p, j: (j, p, 0))
    return _pcall(body, name="mla_bwd", grid=(MLA_HEADS // 2, nb // per),
                  in_specs=[pair_t(256), blk(256), blk_t(256), blk(LANES), pair_t(LANES), stat, stat],
                  out_specs=[pair_t(256), blk_t(256), blk_t(LANES)],
                  out_shape=[SDS((nb, 2048, tb), F32), SDS((nb, 2048, tb), F32), SDS((nb, D, tb), F32)],
                  scratch=[pltpu.VMEM((2, 2, tb, tb), F32), pltpu.VMEM((2, 2, tb, tb), F32), pltpu.VMEM((2, tb, tb), BF16),
                           pltpu.VMEM((2, tb, tb), BF16), pltpu.VMEM((2, tb, LANES), BF16)],
                  sem=("parallel", "arbitrary"))(qt, km, kt, vm, dot, lse, delta)


def _swa_bwd(sinks, qs, ks, vs, do, o, lse):
    T = qs.shape[0]
    nb, cur, prev = _swa_specs(T)

    def body(sink_ref, q_ref, kc_ref, kp_ref, vc_ref, vp_ref, do_ref, o_ref, l_ref,
             dq_ref, dkc_ref, dkp_ref, dvc_ref, dvp_ref, dsink_ref, kb_ref, vb_ref, s_ref, dp_ref, p_ref, ds_ref):
        n = pl.program_id(0)

        @pl.when(n == 0)
        def _():
            dsink_ref[...] = jnp.zeros_like(dsink_ref)

        mask = _swa_mask(n)
        lo = lax.broadcasted_iota(jnp.int32, (WINDOW, LANES), 1) < 64
        hi = jnp.logical_not(lo)
        lane8 = lax.broadcasted_iota(jnp.int32, (8, LANES), 1)
        for g in range(2):
            gs = slice(LANES * g, LANES * (g + 1))
            kb_ref[g] = jnp.concatenate([kp_ref[:, gs], kc_ref[:, gs]], axis=0)
            vb_ref[g] = jnp.concatenate([vp_ref[:, gs], vc_ref[:, gs]], axis=0)

        def head(h):
            sl = slice(LANES * (h // 2), LANES * (h // 2 + 1))
            hm = lo if h % 2 == 0 else hi
            qp = q_ref[:, sl]
            return hm, sl, jnp.where(hm, qp, jnp.zeros_like(qp)), jnp.where(hm, do_ref[:, sl], 0.0).astype(BF16)

        for h in range(SWA_HEADS):
            _, _, qh, dom = head(h)
            s_ref[h] = _dot_nt(qh, kb_ref[h // 8])
            dp_ref[h] = _dot_nt(dom, vb_ref[h // 8])
        dsink = jnp.zeros((8, LANES), F32)
        for h in range(SWA_HEADS):
            hm, sl, _, _ = head(h)
            lse_h = jnp.max(jnp.where(hm, l_ref[:, sl], -jnp.inf), axis=1, keepdims=True)
            delta = jnp.sum(jnp.where(hm, do_ref[:, sl] * o_ref[:, sl], 0.0), axis=1, keepdims=True)
            p = jnp.exp(jnp.where(mask, s_ref[h] * SWA_SCALE, NEG) - lse_h)
            p_ref[h] = p.astype(BF16)
            ds_ref[h] = (p * (dp_ref[h] - delta) * SWA_SCALE).astype(BF16)
            d_sink = -jnp.sum(jnp.exp(sink_ref[h] - lse_h) * delta, axis=0, keepdims=True)
            dsink = dsink + jnp.where(lane8 == h, d_sink, 0.0)
        dsink_ref[...] += dsink
        for g in range(2):
            gs = slice(LANES * g, LANES * (g + 1))
            dkb = jnp.zeros((2 * WINDOW, LANES), F32)
            dvb = jnp.zeros((2 * WINDOW, LANES), F32)
            for j in range(4 * g, 4 * g + 4):
                dqs = []
                for h in (2 * j, 2 * j + 1):
                    _, _, qh, dom = head(h)
                    dvb = dvb + _dot_tn(p_ref[h], dom)
                    dkb = dkb + _dot_tn(ds_ref[h], qh)
                    dqs.append(_dot(ds_ref[h], kb_ref[g]))
                dq_ref[:, LANES * j:LANES * (j + 1)] = jnp.where(lo, dqs[0], dqs[1])
            dkp_ref[:, gs] = dkb[:WINDOW]
            dkc_ref[:, gs] = dkb[WINDOW:]
            dvp_ref[:, gs] = dvb[:WINDOW]
            dvc_ref[:, gs] = dvb[WINDOW:]

    band = pltpu.VMEM((2, 2 * WINDOW, LANES), BF16)
    return _pcall(body, name="swa_bwd", grid=(nb,),
                  in_specs=[pl.BlockSpec(memory_space=pltpu.SMEM), cur(D), cur(256), prev(256), cur(256), prev(256),
                            cur(D), cur(D), cur(D)],
                  out_specs=[cur(D), cur(256), cur(256), cur(256), cur(256), _full((8, LANES))],
                  out_shape=[SDS((T, D), F32), SDS((T, 256), F32), SDS((T, 256), F32), SDS((T, 256), F32), SDS((T, 256), F32),
                             SDS((8, LANES), F32)],
                  scratch=[band, band, pltpu.VMEM((SWA_HEADS, WINDOW, 2 * WINDOW), F32),
                           pltpu.VMEM((SWA_HEADS, WINDOW, 2 * WINDOW), F32), pltpu.VMEM((SWA_HEADS, WINDOW, 2 * WINDOW), BF16),
                           pltpu.VMEM((SWA_HEADS, WINDOW, 2 * WINDOW), BF16)],
                  sem=("arbitrary",))(sinks, qs, ks, ks, vs, vs, do, o, lse)


def _bwd_qkv(dqm, dkm, dvm, dqs, dkc, dkp, dvc, dvp, z, gq, gkv, wqb, wkn, wv, tab_m, tab_s):
    T = z.shape[0]
    tm = WINDOW
    nb = T // tm
    per = dqm.shape[2] // tm

    tab_mt = [t.T for t in tab_m]
    half = MLA_ROPE // 2

    def rope_t(v, c, a, b):
        return v * c + pltpu.roll(v, LANES - half, 0) * a + pltpu.roll(v, half, 0) * b

    def rms_bwd_t(dy, x, g):
        r = lax.rsqrt(jnp.mean(x * x, axis=0, keepdims=True) + EPS)
        xn = x * r
        dn = dy * g
        return r * (dn - xn * jnp.mean(dn * xn, axis=0, keepdims=True)), jnp.sum(dy * xn, axis=1, keepdims=True)

    def body(dqm_ref, dkm_ref, dvm_ref, dqs_ref, dkc_ref, dkp_ref, dvc_ref, dvp_ref, qa_ref, kva_ref, gq_ref, gkv_ref,
             wqb_ref, wkn_ref, wv_ref, cmt_ref, amt_ref, bmt_ref, cs_ref, as_ref, bs_ref,
             dq_out, dkn_out, dv_out, dsq_ref, drest_ref, dgq_ref, dgkv_ref):
        i = pl.program_id(0)

        @pl.when(i == 0)
        def _():
            dgq_ref[...] = jnp.zeros_like(dgq_ref)
            dgkv_ref[...] = jnp.zeros_like(dgkv_ref)

        cmt, amt, bmt = cmt_ref[...], -amt_ref[...], -bmt_ref[...]
        cs, as_, bs = cs_ref[...], -as_ref[...], -bs_ref[...]
        row = lax.broadcasted_iota(jnp.int32, (LANES, tm), 0)
        nope = row < MLA_NOPE
        roped = jnp.logical_and(row >= MLA_NOPE, row < MLA_NOPE + MLA_ROPE)
        dkr = jnp.zeros((LANES, tm), F32)
        for h in range(MLA_HEADS):
            sl = slice(LANES * h, LANES * (h + 1))
            dq_out[0, sl, :] = rope_t(dqm_ref[0, sl, :], cmt, amt, bmt).astype(BF16)
            dk_h = dkm_ref[0, sl, :]
            dkn_out[0, sl, :] = jnp.where(nope, dk_h, 0.0).astype(BF16)
            dkr = dkr + jnp.where(roped, dk_h, 0.0)
        dv_out[0] = dvm_ref[0].astype(BF16)
        dqn = _dot(wqb_ref[...], dq_out[0])
        dkvn = _dot(wkn_ref[...], dkn_out[0]) + _dot(wv_ref[...], dv_out[0])
        dqa, dgq = rms_bwd_t(dqn, qa_ref[...].T, gq_ref[...])
        dkva, dgkv = rms_bwd_t(dkvn, kva_ref[...].T, gkv_ref[...])
        dgq_ref[...] += dgq
        dgkv_ref[...] += dgkv
        for j in range(D // LANES):
            sl = slice(LANES * j, LANES * (j + 1))
            dsq_ref[:, sl] = _rope(dqs_ref[:, sl], cs, as_, bs, SWA_HD // 2).astype(BF16)
        keep = (i < nb - 1).astype(F32)
        drest_ref[:, 0:256] = dqa.T.astype(BF16)
        for j in range(2):
            sl = slice(LANES * j, LANES * (j + 1))
            dk = dkc_ref[:, sl] + keep * dkp_ref[:, sl]
            drest_ref[:, 256 + LANES * j:256 + LANES * (j + 1)] = _rope(dk, cs, as_, bs, SWA_HD // 2).astype(BF16)
        drest_ref[:, 512:768] = (dvc_ref[...] + keep * dvp_ref[...]).astype(BF16)
        drest_ref[:, 768:896] = dkva.T.astype(BF16)
        drest_ref[:, 896:1024] = rope_t(dkr, cmt, amt, bmt).T.astype(BF16)

    nxt = pl.BlockSpec((tm, 256), lambda i: (jnp.minimum(i + 1, nb - 1), 0))
    tab = [_rows(tm, LANES)] * 3
    tab_t = [pl.BlockSpec((LANES, tm), lambda i: (0, i))] * 3
    blk_t = lambda w: pl.BlockSpec((1, w, tm), lambda i: (i // per, 0, i % per))
    return _pcall(body, name="bwd_qkv", grid=(nb,),
                  in_specs=[blk_t(2048), blk_t(2048), blk_t(1024), _rows(tm, 1024), _rows(tm, 256), nxt,
                            _rows(tm, 256), nxt, _rows(tm, 256, 12), _rows(tm, 128, 30), _full((Q_LORA, 1)), _full((KV_LORA, 1)),
                            _full((Q_LORA, 2048)), _full((KV_LORA, 2048)), _full((KV_LORA, 1024))] + tab_t + tab,
                  out_specs=[blk_t(2048), blk_t(2048), blk_t(1024), _rows(tm, 1024), _rows(tm, 1024),
                             _full((Q_LORA, 1)), _full((KV_LORA, 1))],
                  out_shape=[SDS(dqm.shape, BF16), SDS(dkm.shape, BF16), SDS(dvm.shape, BF16), SDS((T, 1024), BF16),
                             SDS((T, 1024), BF16), SDS((Q_LORA, 1), F32), SDS((KV_LORA, 1), F32)],
                  sem=("arbitrary",))(dqm, dkm, dvm, dqs, dkc, dkp, dvc, dvp, z, z, gq.reshape(Q_LORA, 1),
                                      gkv.reshape(KV_LORA, 1), wqb, wkn, wv, *tab_mt, *tab_s)


def _bwd_in(dsq, dga, dgb, drest, w_in_p, x, g1, dx1, tm):
    T = x.shape[0]

    def body(a_ref, b_ref, c_ref, d_ref, w_ref, x_ref, g_ref, dx1_ref, dx_ref, dg_ref):
        @pl.when(pl.program_id(0) == 0)
        def _():
            dg_ref[...] = jnp.zeros_like(dg_ref)

        dh = (_dot_nt(a_ref[...], w_ref[:, 0:1024]) + _dot_nt(b_ref[...], w_ref[:, 1024:2048])
              + _dot_nt(c_ref[...], w_ref[:, 2048:3072]) + _dot_nt(d_ref[...], w_ref[:, 3072:4096]))
        dx, dg = _rms_bwd(dh, x_ref[...], g_ref[...])
        dg_ref[...] += dg
        dx_ref[...] = dx1_ref[...] + dx

    r = _rows(tm, D)
    return _pcall(body, name="bwd_in", grid=(T // tm,),
                  in_specs=[r, r, r, r, _full((D, NZ)), r, _full((1, D)), r],
                  out_specs=[r, _full((1, D))], out_shape=[SDS((T, D), F32), SDS((1, D), F32)],
                  sem=("arbitrary",))(dsq, dga, dgb, drest, w_in_p, x, g1, dx1)


def _wgrad(a, g, name, into=None):
    T, K = a.shape
    N = g.shape[1]
    tk, tn, tt = min(K, 1024), min(N, 1024), min(T, 1024)
    if into is not None:
        buf, weight = into
        _, row0, lane0 = PACK_AT[weight]
        shard = {n: (r, c) for n, r, c in BIG}[weight]
        assert lane0 == 0 and shard[1] == D and tk % shard[0] == 0
        per_step = tk // shard[0]
    assert K % tk == 0 and N % tn == 0 and T % tt == 0, (a.shape, g.shape)
    steps = T // tt

    def body(a_ref, g_ref, *rest):
        o_ref, acc_ref = rest[-2:]
        t = pl.program_id(2)

        @pl.when(t == 0)
        def _():
            acc_ref[...] = jnp.zeros_like(acc_ref)

        acc_ref[...] += _dot_tn(a_ref[...].astype(BF16), g_ref[...].astype(BF16))

        @pl.when(t == steps - 1)
        def _():
            o_ref[...] = acc_ref[...].astype(o_ref.dtype).reshape(o_ref.shape)

    in_specs = [pl.BlockSpec((tt, tk), lambda k, n, t: (t, k)), pl.BlockSpec((tt, tn), lambda k, n, t: (t, n))]
    if into is None:
        return _pcall(body, name=name, grid=(K // tk, N // tn, steps), in_specs=in_specs,
                      out_specs=pl.BlockSpec((tk, tn), lambda k, n, t: (k, n)), out_shape=SDS((K, N), F32),
                      scratch=[pltpu.VMEM((tk, tn), F32)], sem=("parallel", "parallel", "arbitrary"))(a, g)
    assert row0 % shard[0] == 0 and (K // tk) * (N // tn) * per_step == N_CHIPS
    return _pcall(body, name=name, grid=(K // tk, N // tn, steps), in_specs=in_specs + [ANY],
                  out_specs=pl.BlockSpec((per_step, shard[0], tn), lambda k, n, t: (k + n, row0 // shard[0], 0)),
                  out_shape=SDS(buf.shape, buf.dtype),
                  scratch=[pltpu.VMEM((tk, tn), F32)], sem=("parallel", "parallel", "arbitrary"), aliases={2: 0})(a, g, buf)


def _wgrad_t(at, g, name):
    nblk, K, tt = at.shape
    N = g.shape[1]
    tk = min(K, 1024)
    per_step = 4 if nblk % 4 == 0 else 1
    assert K % tk == 0 and g.shape[0] == nblk * tt

    def body(a_ref, g_ref, o_ref):
        @pl.when(pl.program_id(1) == 0)
        def _():
            o_ref[...] = jnp.zeros_like(o_ref)

        acc = _dot(a_ref[0], g_ref[0:tt, :].astype(BF16))
        for b in range(1, per_step):
            acc = acc + _dot(a_ref[b], g_ref[tt * b:tt * (b + 1), :].astype(BF16))
        o_ref[...] += acc

    return _pcall(body, name=name, grid=(K // tk, nblk // per_step),
                  in_specs=[pl.BlockSpec((per_step, tk, tt), lambda k, t: (t, k, 0)),
                            pl.BlockSpec((per_step * tt, N), lambda k, t: (t, 0))],
                  out_specs=pl.BlockSpec((tk, N), lambda k, t: (k, 0)), out_shape=SDS((K, N), F32),
                  sem=("parallel", "arbitrary"))(at, g)


def _adamw(w, packed_g, m, v, name):
    _, R, C = w.shape
    _, row0, lane0 = PACK_AT[name]
    tr = min(R, 256 if row0 % 256 == 0 else 128)
    assert row0 % tr == 0 and R % tr == 0

    def body(w_ref, g_ref, m_ref, v_ref, go_ref, d_ref, m2_ref, v2_ref):
        g_ = g_ref[:, lane0:lane0 + C]
        go_ref[0] = g_
        m2 = ADAM_B1 * m_ref[0] + (1.0 - ADAM_B1) * g_
        v2 = ADAM_B2 * v_ref[0] + (1.0 - ADAM_B2) * jnp.square(g_)
        m_hat = m2 / (1.0 - ADAM_B1 ** ADAM_STEP)
        v_hat = v2 / (1.0 - ADAM_B2 ** ADAM_STEP)
        d_ref[0] = -ADAM_LR * (m_hat / (jnp.sqrt(v_hat) + ADAM_EPS) + ADAM_WD * w_ref[0])
        m2_ref[0] = m2
        v2_ref[0] = v2

    r = pl.BlockSpec((1, tr, C), lambda i: (0, i, 0))
    return _pcall(body, name="adamw_" + name, grid=(R // tr,),
                  in_specs=[r, pl.BlockSpec((tr, D), lambda i: (row0 // tr + i, 0)), r, r], out_specs=[r] * 4,
                  out_shape=[SDS((1, R, C), F32)] * 4, sem=("parallel",))(w, packed_g, m, v)


def _adamw_small(w, parts, m, v):
    def body(w_ref, p_ref, m_ref, v_ref, g_ref, d_ref, m2_ref, v2_ref):
        g_ = p_ref[0]
        for k in range(1, N_DEV):
            g_ = g_ + p_ref[k]
        g_ref[...] = g_
        m2 = ADAM_B1 * m_ref[...] + (1.0 - ADAM_B1) * g_
        v2 = ADAM_B2 * v_ref[...] + (1.0 - ADAM_B2) * jnp.square(g_)
        m_hat = m2 / (1.0 - ADAM_B1 ** ADAM_STEP)
        v_hat = v2 / (1.0 - ADAM_B2 ** ADAM_STEP)
        d_ref[...] = -ADAM_LR * (m_hat / (jnp.sqrt(v_hat) + ADAM_EPS) + ADAM_WD * w_ref[...])
        m2_ref[...] = m2
        v2_ref[...] = v2

    s = _full((8, D))
    return _pcall(body, name="adamw_small", grid=(1,), in_specs=[s, _full((N_DEV, 8, D)), s, s], out_specs=[s] * 4,
                  out_shape=[SDS((8, D), F32)] * 4, sem=("arbitrary",))(w, parts, m, v)


ANY = pl.BlockSpec(memory_space=pl.ANY)


def _place():
    x, y, c = lax.axis_index("x"), lax.axis_index("y"), lax.axis_index("c")
    chips = [(1 - x, y), (x, 1 - y), (1 - x, 1 - y)]
    return x, y, c, chips


def _all_gather(wpk):
    rows = wpk.shape[0]
    HALF = rows // 2
    assert HALF % 16 == 0

    def body(in_ref, out_ref, send_sems, recv_sems):
        x, y, c, chips = _place()
        half = pl.ds(pl.multiple_of(c * HALF, 16), HALF)
        other = pl.ds(pl.multiple_of((1 - c) * HALF, 16), HALF)

        def copy(k, src, dst, to):
            return pltpu.make_async_remote_copy(src_ref=src, dst_ref=dst, send_sem=send_sems.at[k], recv_sem=recv_sems.at[k],
                                                device_id=to, device_id_type=MESH)

        first = [copy(k, in_ref.at[half], out_ref.at[2 * x + y, half], (cx, cy, c)) for k, (cx, cy) in enumerate(chips)]
        first.append(copy(6, in_ref, out_ref.at[2 * x + y], (x, y, 1 - c)))
        for cp in first:
            cp.start()
        passed = []
        for k, (cx, cy) in enumerate(chips):
            slot = out_ref.at[2 * cx + cy, half]
            copy(k, slot, slot, (x, y, c)).wait_recv()
            fwd = copy(3 + k, slot, slot, (x, y, 1 - c))
            fwd.start()
            passed.append(fwd)
        for k, (cx, cy) in enumerate(chips):
            slot = out_ref.at[2 * cx + cy, other]
            copy(3 + k, slot, slot, (x, y, c)).wait_recv()
        copy(6, in_ref, out_ref.at[2 * x + y], (x, y, c)).wait_recv()
        for cp in first + passed:
            cp.wait_send()

    return _pcall(body, name="all_gather_weights", in_specs=[ANY], out_specs=ANY,
                  out_shape=SDS((N_CHIPS, rows, D), BF16),
                  scratch=[pltpu.SemaphoreType.DMA((7,)), pltpu.SemaphoreType.DMA((7,))])(wpk)


HBM = pl.BlockSpec(memory_space=pltpu.HBM)
SEM = pl.BlockSpec(memory_space=pltpu.SEMAPHORE)
DATAFLOW = pltpu.SideEffectType.DATAFLOW_SIDE_EFFECTING


def _in_hbm(a):
    return pltpu.with_memory_space_constraint(a, pltpu.HBM)


def _gather_late_start(wpk, after):
    rows = wpk.shape[0]

    def body(in_ref, land_ref, after_ref, send_sems, recv_sems, in_thru, land_thru, token):
        x, y, c, chips = _place()
        for k, to in enumerate([(cx, cy, c) for cx, cy in chips] + [(x, y, 1 - c)]):
            pltpu.make_async_remote_copy(src_ref=in_ref, dst_ref=land_ref.at[2 * x + y], send_sem=send_sems.at[k],
                                         recv_sem=recv_sems.at[k], device_id=to, device_id_type=MESH).start()
        token[...] = jnp.zeros_like(token)

    return pl.pallas_call(
        body, name="gather_late_start",
        out_shape=(pltpu.SemaphoreType.DMA((4,)), pltpu.SemaphoreType.DMA((4,)), pltpu.HBM(wpk.shape, wpk.dtype),
                   pltpu.HBM((N_CHIPS, rows, D), wpk.dtype), SDS((8, LANES), F32)),
        in_specs=(HBM, HBM, ANY), out_specs=(SEM, SEM, HBM, HBM, pl.BlockSpec(memory_space=pltpu.VMEM)),
        input_output_aliases={0: 2, 1: 3}, compiler_params=pltpu.CompilerParams(has_side_effects=DATAFLOW),
    )(_in_hbm(wpk), _in_hbm(lax.empty((N_CHIPS, rows, D), wpk.dtype)), after)


def _gather_late_wait(send_sems, recv_sems, in_thru, land_thru, after):
    def body(in_ref, land_ref, send_sems, recv_sems, after_ref, after2_ref, in_dead, got_ref):
        x, y, c, chips = _place()
        for k, (sx, sy) in enumerate(chips + [(x, y)]):
            cp = pltpu.make_async_remote_copy(src_ref=in_ref, dst_ref=land_ref.at[2 * sx + sy], send_sem=send_sems.at[k],
                                              recv_sem=recv_sems.at[k], device_id=(x, y, c), device_id_type=MESH)
            cp.wait_send()
            cp.wait_recv()

    return pl.pallas_call(
        body, name="gather_late_wait",
        out_shape=(pltpu.HBM(in_thru.shape, in_thru.dtype), pltpu.HBM(land_thru.shape, land_thru.dtype)),
        in_specs=(HBM, HBM, SEM, SEM, ANY, ANY), out_specs=(HBM, HBM), input_output_aliases={0: 0, 1: 1},
        compiler_params=pltpu.CompilerParams(has_side_effects=DATAFLOW),
    )(in_thru, land_thru, send_sems, recv_sems, *after)[1]


def _rs_sibling(gpk):
    HALF = gpk.shape[1] // 2

    def body(in_ref, out_ref, send_sem, recv_sem):
        x, y, c, _ = _place()
        theirs = pl.ds(pl.multiple_of((1 - c) * HALF, 16), HALF)
        cp = pltpu.make_async_remote_copy(src_ref=in_ref.at[:, theirs], dst_ref=out_ref, send_sem=send_sem, recv_sem=recv_sem,
                                          device_id=(x, y, 1 - c), device_id_type=MESH)
        cp.start()
        cp.wait()

    return _pcall(body, name="rs_sibling", in_specs=[ANY], out_specs=ANY, out_shape=SDS((N_CHIPS, HALF, D), gpk.dtype),
                  scratch=[pltpu.SemaphoreType.DMA, pltpu.SemaphoreType.DMA])(gpk)


def _rs_add_sibling(cidx, gpk, got):
    HALF = got.shape[1]
    th = HALF // 4
    nh = HALF // th
    assert th % 16 == 0

    def body(c_ref, a_ref, b_ref, o_ref):
        o_ref[...] = (a_ref[...].astype(F32) + b_ref[...].astype(F32)).astype(BF16)

    gs = pltpu.PrefetchScalarGridSpec(
        num_scalar_prefetch=1, grid=(N_CHIPS, nh),
        in_specs=[pl.BlockSpec((1, th, D), lambda j, i, c: (j, c[0] * nh + i, 0)), pl.BlockSpec((1, th, D), lambda j, i, c: (j, i, 0))],
        out_specs=pl.BlockSpec((1, th, D), lambda j, i, c: (j, i, 0)))
    return pl.pallas_call(body, name="rs_add_sibling", grid_spec=gs, out_shape=SDS((N_CHIPS, HALF, D), BF16),
                          compiler_params=pltpu.CompilerParams(dimension_semantics=("parallel", "parallel"),
                                                               vmem_limit_bytes=48 << 20))(cidx, gpk, got)


def _rs_chips_start(part, small, after):
    def body(p_ref, s_ref, land_ref, sland_ref, after_ref, send_sems, recv_sems, p_thru, s_thru, land_thru, sland_thru, token):
        x, y, c, chips = _place()
        for k, (cx, cy) in enumerate(chips):
            pltpu.make_async_remote_copy(src_ref=p_ref.at[2 * cx + cy], dst_ref=land_ref.at[2 * x + y], send_sem=send_sems.at[k],
                                         recv_sem=recv_sems.at[k], device_id=(cx, cy, c), device_id_type=MESH).start()
        peers = [(x, y, 1 - c)] + [(cx, cy, c) for cx, cy in chips] + [(cx, cy, 1 - c) for cx, cy in chips]
        for k, to in enumerate(peers):
            pltpu.make_async_remote_copy(src_ref=s_ref, dst_ref=sland_ref.at[4 * x + 2 * y + c], send_sem=send_sems.at[3 + k],
                                         recv_sem=recv_sems.at[3 + k], device_id=to, device_id_type=MESH).start()
        token[...] = jnp.zeros_like(token)

    return pl.pallas_call(
        body, name="rs_chips_start",
        out_shape=(pltpu.SemaphoreType.DMA((10,)), pltpu.SemaphoreType.DMA((10,)), pltpu.HBM(part.shape, part.dtype),
                   pltpu.HBM(small.shape, small.dtype), pltpu.HBM(part.shape, part.dtype), pltpu.HBM((N_DEV, 8, D), F32),
                   SDS((8, LANES), F32)),
        in_specs=(HBM, HBM, HBM, HBM, ANY), out_specs=(SEM, SEM, HBM, HBM, HBM, HBM, pl.BlockSpec(memory_space=pltpu.VMEM)),
        input_output_aliases={0: 2, 1: 3, 2: 4, 3: 5}, compiler_params=pltpu.CompilerParams(has_side_effects=DATAFLOW),
    )(_in_hbm(part), _in_hbm(small), _in_hbm(lax.empty(part.shape, part.dtype)), _in_hbm(lax.empty((N_DEV, 8, D), F32)), after)


def _rs_chips_wait(send_sems, recv_sems, p_thru, s_thru, land_thru, sland_thru, after):
    def body(p_ref, s_ref, land_ref, sland_ref, send_sems, recv_sems, *after_and_outputs):
        x, y, c, chips = _place()
        for k, (cx, cy) in enumerate(chips):
            cp = pltpu.make_async_remote_copy(src_ref=p_ref.at[0], dst_ref=land_ref.at[2 * cx + cy], send_sem=send_sems.at[k],
                                              recv_sem=recv_sems.at[k], device_id=(cx, cy, c), device_id_type=MESH)
            cp.wait_send()
            cp.wait_recv()
        peers = [(x, y, 1 - c)] + [(cx, cy, c) for cx, cy in chips] + [(cx, cy, 1 - c) for cx, cy in chips]
        for k, (px, py, pc) in enumerate(peers):
            cp = pltpu.make_async_remote_copy(src_ref=s_ref, dst_ref=sland_ref.at[4 * px + 2 * py + pc], send_sem=send_sems.at[3 + k],
                                              recv_sem=recv_sems.at[3 + k], device_id=(px, py, pc), device_id_type=MESH)
            cp.wait_send()
            cp.wait_recv()

    hbm = lambda a: pltpu.HBM(a.shape, a.dtype)
    outs = pl.pallas_call(
        body, name="rs_chips_wait", out_shape=(hbm(p_thru), hbm(s_thru), hbm(land_thru), hbm(sland_thru)),
        in_specs=(HBM, HBM, HBM, HBM, SEM, SEM) + (ANY,) * len(after), out_specs=(HBM, HBM, HBM, HBM),
        input_output_aliases={0: 0, 1: 1, 2: 2, 3: 3}, compiler_params=pltpu.CompilerParams(has_side_effects=DATAFLOW),
    )(p_thru, s_thru, land_thru, sland_thru, send_sems, recv_sems, *after)
    return outs[0], outs[2], outs[3]


def _rs_add_chips(qidx, part, parts):
    HALF = part.shape[1]
    th = HALF // 4
    nh = HALF // th
    assert th % 16 == 0

    def body(q_ref, own_ref, p_ref, o_ref):
        for me in range(N_CHIPS):
            @pl.when(q_ref[0] == me)
            def _(me=me):
                t = [(own_ref[0] if j == me else p_ref[j]).astype(F32) for j in range(N_CHIPS)]
                o_ref[...] = ((t[0] + t[1]) + t[2]) + t[3]

    gs = pltpu.PrefetchScalarGridSpec(
        num_scalar_prefetch=1, grid=(HALF // th,),
        in_specs=[pl.BlockSpec((1, th, D), lambda i, q: (q[0], i, 0)), pl.BlockSpec((N_CHIPS, th, D), lambda i, q: (0, i, 0))],
        out_specs=pl.BlockSpec((th, D), lambda i, q: (q[1] * nh + i, 0)))
    return pl.pallas_call(body, name="rs_add_chips", grid_spec=gs, out_shape=SDS((2 * HALF, D), F32),
                          compiler_params=pltpu.CompilerParams(dimension_semantics=("parallel",),
                                                               vmem_limit_bytes=48 << 20))(qidx, part, parts)


def _rs_join(shard, name):
    HALF = shard.shape[0] // 2

    def body(in_ref, out_ref, send_sem, recv_sem):
        x, y, c, _ = _place()
        rows = pl.ds(pl.multiple_of(c * HALF, 16), HALF)
        cp = pltpu.make_async_remote_copy(src_ref=in_ref.at[rows], dst_ref=out_ref.at[rows], send_sem=send_sem, recv_sem=recv_sem,
                                          device_id=(x, y, 1 - c), device_id_type=MESH)
        cp.start()
        cp.wait()

    return _pcall(body, name=name, in_specs=[ANY], out_specs=ANY, out_shape=SDS(shard.shape, F32),
                  scratch=[pltpu.SemaphoreType.DMA, pltpu.SemaphoreType.DMA], aliases={0: 0})(shard)


def _reduce_late_start(gpk, after):
    rows = gpk.shape[1]
    HALF = rows // 2
    assert HALF % 16 == 0

    def body(in_ref, land_ref, after_ref, send_sems, recv_sems, in_thru, land_thru, token):
        x, y, c, chips = _place()
        me = 4 * x + 2 * y + c
        peers = [(x, y, 1 - c)] + [(cx, cy, c) for cx, cy in chips] + [(cx, cy, 1 - c) for cx, cy in chips]
        for k, (px, py, pc) in enumerate(peers):
            src = in_ref.at[2 * px + py, pl.ds(pl.multiple_of(pc * HALF, 16), HALF)]
            pltpu.make_async_remote_copy(src_ref=src, dst_ref=land_ref.at[me], send_sem=send_sems.at[k], recv_sem=recv_sems.at[k],
                                         device_id=(px, py, pc), device_id_type=MESH).start()
        token[...] = jnp.zeros_like(token)

    return pl.pallas_call(
        body, name="reduce_late_start",
        out_shape=(pltpu.SemaphoreType.DMA((7,)), pltpu.SemaphoreType.DMA((7,)), pltpu.HBM(gpk.shape, gpk.dtype),
                   pltpu.HBM((N_DEV, HALF, D), gpk.dtype), SDS((8, LANES), F32)),
        in_specs=(HBM, HBM, ANY), out_specs=(SEM, SEM, HBM, HBM, pl.BlockSpec(memory_space=pltpu.VMEM)),
        input_output_aliases={0: 2, 1: 3}, compiler_params=pltpu.CompilerParams(has_side_effects=DATAFLOW),
    )(_in_hbm(gpk), _in_hbm(lax.empty((N_DEV, HALF, D), gpk.dtype)), after)


def _reduce_late_wait(send_sems, recv_sems, in_thru, land_thru, after):
    def body(in_ref, land_ref, send_sems, recv_sems, after_ref, in_out, got_ref):
        x, y, c, chips = _place()
        peers = [(x, y, 1 - c)] + [(cx, cy, c) for cx, cy in chips] + [(cx, cy, 1 - c) for cx, cy in chips]
        for k, (px, py, pc) in enumerate(peers):
            cp = pltpu.make_async_remote_copy(src_ref=land_ref.at[0], dst_ref=land_ref.at[4 * px + 2 * py + pc],
                                              send_sem=send_sems.at[k], recv_sem=recv_sems.at[k],
                                              device_id=(px, py, pc), device_id_type=MESH)
            cp.wait_send()
            cp.wait_recv()

    return pl.pallas_call(
        body, name="reduce_late_wait",
        out_shape=(pltpu.HBM(in_thru.shape, in_thru.dtype), pltpu.HBM(land_thru.shape, land_thru.dtype)),
        in_specs=(HBM, HBM, SEM, SEM, ANY), out_specs=(HBM, HBM), input_output_aliases={0: 0, 1: 1},
        compiler_params=pltpu.CompilerParams(has_side_effects=DATAFLOW),
    )(in_thru, land_thru, send_sems, recv_sems, after)


def _reduce_late_add(didx, gpk, parts):
    HALF = parts.shape[1]
    th = HALF // 4
    nh = HALF // th
    assert th % 16 == 0

    def body(d_ref, own_ref, p_ref, o_ref):
        for me in range(N_DEV):
            @pl.when(d_ref[0] == me)
            def _(me=me):
                t = [(own_ref[0] if j == me else p_ref[j]).astype(F32) for j in range(N_DEV)]
                o_ref[...] = ((((((t[0] + t[1]) + t[2]) + t[3]) + t[4]) + t[5]) + t[6]) + t[7]

    gs = pltpu.PrefetchScalarGridSpec(
        num_scalar_prefetch=1, grid=(nh,),
        in_specs=[pl.BlockSpec((1, th, D), lambda i, d: (d[1], d[2] * nh + i, 0)), pl.BlockSpec((N_DEV, th, D), lambda i, d: (0, i, 0))],
        out_specs=pl.BlockSpec((th, D), lambda i, d: (d[2] * nh + i, 0)))
    return pl.pallas_call(body, name="reduce_late_add", grid_spec=gs, out_shape=SDS((2 * HALF, D), F32),
                          compiler_params=pltpu.CompilerParams(dimension_semantics=("parallel",),
                                                               vmem_limit_bytes=48 << 20))(didx, gpk, parts)


def _pack_early(b, dtype):
    lanes = lambda a: jnp.pad(a.astype(dtype), ((0, 0), (0, D - a.shape[1])))
    pair = jnp.concatenate([b["w_q_b"].astype(dtype), b["w_ple"].astype(dtype), jnp.zeros((256, D - 640), dtype)], axis=1)
    return jnp.concatenate([lanes(b["w_in"]), pair, lanes(b["w_kv_b"])], axis=0)


def _pack_late(b, dtype):
    return jnp.concatenate([b[n].astype(dtype) for n in ("w_mla_up", "w_swa_up", "w_out", "w_ple_gate", "w_mlp_up", "w_mlp_down")],
                           axis=0)


def _unpack_shards(pk, which):
    return {n: pk[PACK_AT[n][1]:PACK_AT[n][1] + r, PACK_AT[n][2]:PACK_AT[n][2] + c] for n, r, c in BIG if PACK_AT[n][0] == which}


def _full_weights(gathered, which):
    per_chip = [_unpack_shards(gathered[j], which) for j in range(N_CHIPS)]
    out = {}
    for n in per_chip[0]:
        shards = [pc[n] for pc in per_chip]
        if n == "w_in":
            out["w_in_p"] = _w_in_internal(shards)
        else:
            out[n] = jnp.concatenate(shards, axis=1 if n in COL_SHARDED else 0)
    return out


def _split_full_grads(grads, pack, dtype):
    shard = {n: (r, c) for n, r, c in BIG}
    chunks = []
    for j in range(N_CHIPS):
        blocks = {}
        for n, g in grads.items():
            if n == "w_in_p":
                blocks["w_in"] = _w_in_grad_shard(g, j)
                continue
            r, c = shard[n]
            blocks[n] = g[:, j * c:(j + 1) * c] if n in COL_SHARDED else g[j * r:(j + 1) * r]
        chunks.append(pack(blocks, dtype))
    return jnp.stack(chunks)


W_IN_SHARD = 936
W_IN_SEGMENTS = ((0, 256, (3072,)), (256, 384, (3840,)), (384, 416, (4032,)), (416, 1440, (0,)), (1440, 1504, (3328, 3392)),
                 (1504, 1568, (3456, 3520)), (1568, 1632, (3584, 3648)), (1632, 1696, (3712, 3776)), (1696, 3744, (1024,)))


def _w_in_internal(shards):
    def cols(a, b):
        out = []
        for j, s in enumerate(shards):
            lo, hi = max(a, W_IN_SHARD * j), min(b, W_IN_SHARD * (j + 1))
            if lo < hi:
                out.append(s[:, lo - W_IN_SHARD * j:hi - W_IN_SHARD * j])
        return out

    pieces = {}
    for a, b, places in W_IN_SEGMENTS:
        for at in places:
            pieces[at] = cols(a, b)
    zeros = lambda n: [jnp.zeros((D, n), shards[0].dtype)]
    pieces[3968] = zeros(64)
    pieces[4064] = zeros(32)
    return jnp.concatenate([piece for at in sorted(pieces) for piece in pieces[at]], axis=1)


def _w_in_grad_shard(g, j):
    def internal(a, b):
        out = []
        while a < b:
            end = min(b, (a // D + 1) * D)
            out.append(g[a // D][:, a % D:a % D + end - a])
            a = end
        return out

    out = []
    for a, b, places in W_IN_SEGMENTS:
        lo, hi = max(a, W_IN_SHARD * j), min(b, W_IN_SHARD * (j + 1))
        if lo < hi:
            parts = [internal(at + lo - a, at + hi - a) for at in places]
            if len(parts) == 1:
                out += parts[0]
            else:
                assert len(parts[0]) == len(parts[1]) == 1
                out.append(parts[0][0] + parts[1][0])
    return jnp.concatenate(out, axis=1)


def _local_step(x, p, tgt, w, small, late_weights, late_grads_out):
    T = x.shape[0]
    tm = 256
    tb = 256
    w_in_p = w["w_in_p"]
    wqb = jnp.pad(w["w_q_b"].reshape(Q_LORA, MLA_HEADS, 96), ((0, 0), (0, 0), (0, 32))).reshape(Q_LORA, 2048)
    wkv = w["w_kv_b"].reshape(KV_LORA, MLA_HEADS, 128)
    wkn = jnp.pad(wkv[:, :, :64], ((0, 0), (0, 0), (0, 64))).reshape(KV_LORA, 2048)
    wv = wkv[:, :, 64:].reshape(KV_LORA, 1024)
    tab_m = _rope_tables(T, "mla")
    tab_s = _rope_tables(T, "swa")
    g1, gq, gkv, sinks = small["g_mix_pre"], small["g_q_a"], small["g_kv_a"], small["sinks"]
    g2, g3, g4, g5 = small["g_mix_post"], small["g_mlp_pre"], small["g_mlp_post"], small["g_ple"]
    sink_vec = sinks.reshape(SWA_HEADS)

    z, h1 = _fwd_in(x, g1, w_in_p, tm)
    qn, kvn, km, vm, qt, kt, vt, qs, ks, vs = _fwd_qkv(z, gq, gkv, wqb, wkn, wv, tab_m, tab_s, tb)
    om, lse_m = _mla_fwd(qt, km, vt, tb)
    os_, lse_s = _swa_fwd(sink_vec, qs, ks, vs)
    w = {**w, **late_weights((om, os_))}
    y, yo, au, bu, x1 = _fwd_mix(om, os_, z, x, w["w_mla_up"], w["w_swa_up"], w["w_out"], g2, tm)
    h2, u = _fwd_mlp_up(x1, g3, w["w_mlp_up"], tm)
    d, x2 = _fwd_mlp_down(u, w["w_mlp_down"], x1, g4, tm)
    loss, dx2, dgt, de0, dg5 = _ple_fwd_bwd(p, x2, tgt, w["w_ple"], g5, w["w_ple_gate"], tm)

    dd, da, dg4 = _bwd_mlp_down(dx2, d, g4, w["w_mlp_down"], u, tm)
    dx1, dg3 = _bwd_mlp_up(da, w["w_mlp_up"], x1, g3, dx2, tm)
    dyo, dg2, dau, dbu, dga, dgb, dos, delta_m, dom_t = _bwd_mix(dx1, yo, g2, w["w_out"], z, au, bu, w["w_mla_up"],
                                                                w["w_swa_up"], om, tb)
    gpk_late = lax.empty((N_CHIPS, PACK_ROWS["late"], D), BF16)
    for weight, a_, g_ in (("w_mla_up", om, dau), ("w_swa_up", os_, dbu), ("w_out", y, dyo), ("w_ple_gate", x2, dgt),
                           ("w_mlp_up", h2, da), ("w_mlp_down", u, dd)):
        gpk_late = _wgrad(a_, g_, "wgrad_" + weight[2:], into=(gpk_late, weight))
    token = late_grads_out(gpk_late)
    delta_m = delta_m + token[0, 0]
    dqm, dkm, dvm = _mla_bwd(qt, km, kt, vm, dom_t, lse_m, delta_m, tb)
    dqs, dkc, dkp, dvc, dvp, dsink = _swa_bwd(sink_vec, qs, ks, vs, dos, os_, lse_s)
    dqb, dknb, dvb, dsq, drest, dgq, dgkv = _bwd_qkv(dqm, dkm, dvm, dqs, dkc, dkp, dvc, dvp, z, gq, gkv, wqb, wkn, wv,
                                                      tab_m, tab_s)
    gx, dg1 = _bwd_in(dsq, dga, dgb, drest, w_in_p, x, g1, dx1, tm)

    g_in_p = [_wgrad(h1, dsq, "wgrad_in_sq"), _wgrad(h1, dga, "wgrad_in_ga"), _wgrad(h1, dgb, "wgrad_in_gb"),
              _wgrad(h1, drest, "wgrad_in_rest")]
    g_qb_p = _wgrad_t(dqb, qn, "wgrad_q_b").T
    g_kn_p = _wgrad_t(dknb, kvn, "wgrad_kv_b_nope").T
    g_v_p = _wgrad_t(dvb, kvn, "wgrad_kv_b_v").T
    grads = {
        "w_in_p": g_in_p,
        "w_q_b": g_qb_p.reshape(Q_LORA, MLA_HEADS, 128)[:, :, :96].reshape(Q_LORA, 1536),
        "w_kv_b": jnp.concatenate([g_kn_p.reshape(KV_LORA, MLA_HEADS, 128)[:, :, :64], g_v_p.reshape(KV_LORA, MLA_HEADS, 64)],
                                  axis=2).reshape(KV_LORA, 2048),
        "w_ple": _wgrad(p, de0, "wgrad_ple"),
    }
    small_grads = {"g_mix_pre": dg1, "g_q_a": dgq.reshape(1, Q_LORA), "g_kv_a": dgkv.reshape(1, KV_LORA), "sinks": dsink[0:1, 0:SWA_HEADS], "g_mix_post": dg2,
                   "g_mlp_pre": dg3, "g_mlp_post": dg4, "g_ple": dg5}
    return loss, gx, grads, small_grads


def _pack_small(vals, fill, scalar=None):
    wide = [vals[n] for n, k in SMALL if k == D]
    narrow = [vals[n] for n, k in SMALL if k != D]
    used = sum(k for _, k in SMALL if k != D)
    last = jnp.concatenate(narrow + [jnp.full((1, D - used), fill, F32)], axis=1)
    rest = jnp.full((2, D), fill, F32)
    if scalar is not None:
        rest = jnp.concatenate([jnp.concatenate([scalar, rest[0:1, 1:]], axis=1), rest[1:2]], axis=0)
    return jnp.concatenate(wide + [last, rest], axis=0)


def _unpack_small(pk):
    out, row, off = {}, 0, 0
    for n, k in SMALL:
        if k == D:
            out[n] = pk[row:row + 1]
            row += 1
    for n, k in SMALL:
        if k != D:
            out[n] = pk[5:6, off:off + k]
            off += k
    return out


def kernel(x, p, g_mix_pre, w_in, g_q_a, w_q_b, g_kv_a, w_kv_b, sinks, w_mla_up, w_swa_up, w_out, g_mix_post, g_mlp_pre, w_mlp_up, w_mlp_down, g_mlp_post, w_ple, g_ple, w_ple_gate, loss_target, m_g_mix_pre, m_w_in, m_g_q_a, m_w_q_b, m_g_kv_a, m_w_kv_b, m_sinks, m_w_mla_up, m_w_swa_up, m_w_out, m_g_mix_post, m_g_mlp_pre, m_w_mlp_up, m_w_mlp_down, m_g_mlp_post, m_w_ple, m_g_ple, m_w_ple_gate, v_g_mix_pre, v_w_in, v_g_q_a, v_w_q_b, v_g_kv_a, v_w_kv_b, v_sinks, v_w_mla_up, v_w_swa_up, v_w_out, v_g_mix_post, v_g_mlp_pre, v_w_mlp_up, v_w_mlp_down, v_g_mlp_post, v_w_ple, v_g_ple, v_w_ple_gate):
    given = dict(locals())
    big_w = {n: given[n][0] for n, _, _ in BIG}
    small_w = {n: given[n] for n, _ in SMALL}
    small_m = {n: given["m_" + n] for n, _ in SMALL}
    small_v = {n: given["v_" + n] for n, _ in SMALL}

    core = lax.axis_index("c")
    chip = 2 * lax.axis_index("x") + lax.axis_index("y")
    core_i = core.astype(jnp.int32).reshape(1)
    dev_i = jnp.stack([2 * chip + core, chip, core]).astype(jnp.int32)

    own_early = _pack_early(big_w, BF16)
    own_late = _pack_late(big_w, BF16)
    got_early = _all_gather(own_early)
    late_flight = _gather_late_start(own_late, got_early)
    weights = _full_weights(got_early, "early")
    step_small = {**small_w, "g_mix_pre": small_w["g_mix_pre"] + late_flight[4][0, 0]}

    def late_weights(after):
        return _full_weights(_gather_late_wait(*late_flight[:4], after), "late")

    flight = {}

    def late_grads_out(gpk_late):
        flight["late"] = _reduce_late_start(gpk_late, dev_i)
        return flight["late"][4]

    loss_blk, gx, grads, small_grads = _local_step(x[0], p[0, 0], loss_target[0], weights, step_small, late_weights,
                                                   late_grads_out)

    gpk = _split_full_grads(grads, _pack_early, BF16)
    got = _rs_sibling(gpk)
    part = _rs_add_sibling(core_i, gpk, got)
    small_own = _pack_small(small_grads, 0.0, loss_blk[0:1, 0:1])
    early_flight = _rs_chips_start(part, small_own, dev_i)

    out_g, out_d, out_m, out_v = {}, {}, {}, {}
    gpk_late, parts_late = _reduce_late_wait(*flight["late"][:4], early_flight[6])
    joined_late = _rs_join(_reduce_late_add(dev_i, gpk_late, parts_late), "rs_join_late")
    for n, _, _ in BIG:
        if PACK_AT[n][0] == "late":
            out_g[n], out_d[n], out_m[n], out_v[n] = _adamw(given[n], joined_late, given["m_" + n], given["v_" + n], n)

    part, parts, small_parts = _rs_chips_wait(*early_flight[:6], [out_d[n] for n in out_d])
    joined_early = _rs_join(_rs_add_chips(dev_i[1:3], part, parts), "rs_join_early")
    for n, _, _ in BIG:
        if PACK_AT[n][0] == "early":
            out_g[n], out_d[n], out_m[n], out_v[n] = _adamw(given[n], joined_early, given["m_" + n], given["v_" + n], n)

    mine = (lax.broadcasted_iota(jnp.int32, (N_DEV, 1, 1), 0) == dev_i[0])
    g_small_pk, d_small_pk, m_small_pk, v_small_pk = _adamw_small(
        _pack_small(small_w, 0.0), jnp.where(mine, small_own[None], small_parts), _pack_small(small_m, 0.0),
        _pack_small(small_v, 1.0))
    loss = g_small_pk[6, 0]
    for out, pk in ((out_g, g_small_pk), (out_d, d_small_pk), (out_m, m_small_pk), (out_v, v_small_pk)):
        out.update(_unpack_small(pk))
    order = ["g_mix_pre", "w_in", "g_q_a", "w_q_b", "g_kv_a", "w_kv_b", "sinks", "w_mla_up", "w_swa_up", "w_out", "g_mix_post",
             "g_mlp_pre", "w_mlp_up", "w_mlp_down", "g_mlp_post", "w_ple", "g_ple", "w_ple_gate"]
    return (loss, gx[None], *[out_g[n] for n in order], *[out_d[n] for n in order], *[out_m[n] for n in order],
            *[out_v[n] for n in order])
```

```python
import math

import jax
import jax.numpy as jnp
from jax import lax
from jax.experimental import pallas as pl
from jax.experimental.pallas import tpu as pltpu

F32 = jnp.float32
BF16 = jnp.bfloat16
SDS = jax.ShapeDtypeStruct

D = 1024
D_FF = 4096
PLE = 256
Q_LORA = 256
KV_LORA = 128
MLA_HEADS = 16
MLA_NOPE = 64
MLA_ROPE = 32
SWA_HEADS = 16
SWA_HD = 64
WINDOW = 128
ROPE_THETA = 10000.0
EPS = 1e-6
NEG = -1e30
NZ = 4096
MLA_SCALE = (MLA_NOPE + MLA_ROPE) ** -0.5
LOG2_E = math.log2(math.e)
MLA_LOG2_SCALE = MLA_SCALE * LOG2_E
SWA_SCALE = SWA_HD ** -0.5

ADAM_LR = 0.001
ADAM_B1 = 0.9
ADAM_B2 = 0.999
ADAM_EPS = 1e-08
ADAM_WD = 0.01
ADAM_STEP = 10

LANES = 128
ATT_COLS = 128
VT_ROWS = 80
N_CHIPS = 4
N_DEV = 8
MESH = pl.DeviceIdType.MESH

NT = (((1,), (1,)), ((), ()))
TN = (((0,), (0,)), ((), ()))

BIG = (("w_in", 1024, 936), ("w_q_b", 256, 384), ("w_kv_b", 128, 512), ("w_mla_up", 256, 1024),
       ("w_swa_up", 256, 1024), ("w_out", 256, 1024), ("w_mlp_up", 1024, 1024), ("w_mlp_down", 1024, 1024),
       ("w_ple", 256, 256), ("w_ple_gate", 256, 1024))
COL_SHARDED = ("w_in", "w_q_b", "w_kv_b", "w_mlp_up", "w_ple")
PACK_AT = {"w_in": ("early", 0, 0), "w_q_b": ("early", 1024, 0), "w_ple": ("early", 1024, 384), "w_kv_b": ("early", 1280, 0),
           "w_mla_up": ("late", 0, 0), "w_swa_up": ("late", 256, 0), "w_out": ("late", 512, 0), "w_ple_gate": ("late", 768, 0),
           "w_mlp_up": ("late", 1024, 0), "w_mlp_down": ("late", 2048, 0)}
PACK_ROWS = {"early": 1408, "late": 3072}
SMALL = (("g_mix_pre", 1024), ("g_q_a", 256), ("g_kv_a", 128), ("sinks", 16), ("g_mix_post", 1024),
         ("g_mlp_pre", 1024), ("g_mlp_post", 1024), ("g_ple", 1024))


def _dot(a, b):
    return jnp.dot(a, b, preferred_element_type=F32)


def _dot_nt(a, b):
    return lax.dot_general(a, b, NT, preferred_element_type=F32)


def _dot_tn(a, b):
    return lax.dot_general(a, b, TN, preferred_element_type=F32)


def _pcall(body, *, name, out_shape, grid=(), in_specs=None, out_specs=None, scratch=(), sem=None, vmem_mb=48, aliases=None):
    params = dict(vmem_limit_bytes=vmem_mb << 20)
    if sem is not None:
        params["dimension_semantics"] = sem
    return pl.pallas_call(body, name=name, grid=grid, in_specs=in_specs, out_specs=out_specs, out_shape=out_shape,
                          scratch_shapes=list(scratch), input_output_aliases=aliases or {},
                          compiler_params=pltpu.CompilerParams(**params))


def _rows(tm, n, col=0):
    return pl.BlockSpec((tm, n), lambda i: (i, col))


def _full(shape):
    return pl.BlockSpec(shape, lambda i: (0,) * len(shape))


def _rms(x, g):
    r = lax.rsqrt(jnp.mean(x * x, axis=-1, keepdims=True) + EPS)
    return x * r * g


def _rms_bwd(dy, x, g):
    r = lax.rsqrt(jnp.mean(x * x, axis=-1, keepdims=True) + EPS)
    xn = x * r
    dn = dy * g
    dx = r * (dn - xn * jnp.mean(dn * xn, axis=-1, keepdims=True))
    return dx, jnp.sum(dy * xn, axis=0, keepdims=True)


def _sigmoid(x):
    return 1.0 / (1.0 + jnp.exp(-x))


def _rope(x, c, a, b, half):
    return x * c + pltpu.roll(x, LANES - half, 1) * a + pltpu.roll(x, half, 1) * b


def _rope_tables(T, kind):
    lane = jnp.arange(LANES)
    if kind == "mla":
        half = MLA_ROPE // 2
        rel = lane - MLA_NOPE
        on = (rel >= 0) & (rel < MLA_ROPE)
        d = MLA_ROPE
    else:
        half = SWA_HD // 2
        rel = lane % SWA_HD
        on = jnp.ones((LANES,), bool)
        d = SWA_HD
    first = on & (rel < half)
    second = on & (rel >= half)
    f = jnp.where(first, rel, rel - half).astype(F32)
    inv = jnp.exp(-math.log(ROPE_THETA) * f * (2.0 / d))
    ang = jnp.arange(T, dtype=F32)[:, None] * inv[None, :]
    cos, sin = jnp.cos(ang), jnp.sin(ang)
    c = jnp.where(on[None], cos, 1.0)
    a = jnp.where(first[None], -sin, 0.0)
    b = jnp.where(second[None], sin, 0.0)
    return c, a, b


def _fwd_in(x, g1, w_in_p, tm):
    T = x.shape[0]

    def body(x_ref, g_ref, w_ref, z_ref, h_ref):
        h = _rms(x_ref[...], g_ref[...]).astype(BF16)
        h_ref[...] = h
        z_ref[...] = _dot(h, w_ref[...])

    return _pcall(body, name="fwd_in", grid=(T // tm,),
                  in_specs=[_rows(tm, D), _full((1, D)), _full((D, NZ))],
                  out_specs=[_rows(tm, NZ), _rows(tm, D)],
                  out_shape=[SDS((T, NZ), F32), SDS((T, D), BF16)], sem=("parallel",))(x, g1, w_in_p)


def _fwd_qkv(z, gq, gkv, wqb, wkn, wv, tab_m, tab_s, tm):
    T = z.shape[0]
    wqb_t, wkn_t, wv_t = wqb.T, wkn.T, wv.T
    tab_mt = [t.T for t in tab_m]

    def body(qa_ref, sq_ref, skd_ref, svd_ref, kva_ref, kr_ref, gq_ref, gkv_ref, wkn_ref, wv_ref, wqbt_ref, wknt_ref, wvt_ref,
             cm_ref, am_ref, bm_ref, cmt_ref, amt_ref, bmt_ref, cs_ref, as_ref, bs_ref,
             qn_ref, kvn_ref, km_ref, vm_ref, qt_ref, kt_ref, vt_ref, qs_ref, ks_ref, vs_ref):
        qn = _rms(qa_ref[...], gq_ref[...])
        qn_ref[...] = qn.astype(BF16)
        kvn = _rms(kva_ref[...], gkv_ref[...])
        kvn_b = kvn.astype(BF16)
        kvn_ref[...] = kvn_b
        qn_t = qn.T.astype(BF16)
        kvn_t = kvn.T.astype(BF16)
        cm, am, bm = cm_ref[...], am_ref[...], bm_ref[...]
        cmt, amt, bmt = cmt_ref[...], amt_ref[...], bmt_ref[...]
        cs, as_, bs = cs_ref[...], as_ref[...], bs_ref[...]
        k_rope = _rope(kr_ref[...], cm, am, bm, MLA_ROPE // 2)
        k_rope_t = k_rope.T
        half = MLA_ROPE // 2
        vm_ref[...] = _dot(kvn_b, wv_ref[...]).astype(BF16)
        km_all = _dot(kvn_b, wkn_ref[...])
        v_t = _dot(wvt_ref[...], kvn_t)
        q_t = _dot(wqbt_ref[...], qn_t)
        k_t = _dot(wknt_ref[...], kvn_t)
        ones_row = jnp.where(lax.broadcasted_iota(jnp.int32, (64, tm), 0) == 0, 1.0, 0.0)
        for h in range(MLA_HEADS):
            sl = slice(LANES * h, LANES * (h + 1))
            vt_ref[0, sl, :] = jnp.concatenate([v_t[64 * h:64 * (h + 1)], ones_row], axis=0).astype(BF16)
            qh = q_t[sl]
            qt_ref[0, sl, :] = (qh * cmt + pltpu.roll(qh, LANES - half, 0) * amt + pltpu.roll(qh, half, 0) * bmt).astype(BF16)
            km_ref[:, sl] = (km_all[:, sl] + k_rope).astype(BF16)
            kt_ref[0, sl, :] = (k_t[sl] + k_rope_t).astype(BF16)
        for j in range(D // LANES):
            sl = slice(LANES * j, LANES * (j + 1))
            qs_ref[:, sl] = _rope(sq_ref[:, sl], cs, as_, bs, SWA_HD // 2).astype(BF16)
        for j in range(2):
            sl = slice(LANES * j, LANES * (j + 1))
            ks_ref[:, sl] = _rope(skd_ref[:, sl], cs, as_, bs, SWA_HD // 2).astype(BF16)
        vs_ref[...] = svd_ref[...].astype(BF16)

    tab = [_rows(tm, LANES)] * 3
    tab_t = [pl.BlockSpec((LANES, tm), lambda i: (0, i))] * 3
    return _pcall(body, name="fwd_qkv", grid=(T // tm,),
                  in_specs=[_rows(tm, 256, 12), _rows(tm, 1024, 0), _rows(tm, 256, 13), _rows(tm, 256, 14),
                            _rows(tm, 128, 30), _rows(tm, 128, 31), _full((1, Q_LORA)), _full((1, KV_LORA)),
                            _full((KV_LORA, 2048)), _full((KV_LORA, 1024)), _full((2048, Q_LORA)), _full((2048, KV_LORA)),
                            _full((1024, KV_LORA))] + tab + tab_t + tab,
                  out_specs=[_rows(tm, Q_LORA), _rows(tm, KV_LORA), _rows(tm, 2048), _rows(tm, 1024),
                             pl.BlockSpec((1, 2048, tm), lambda i: (i, 0, 0)), pl.BlockSpec((1, 2048, tm), lambda i: (i, 0, 0)),
                             pl.BlockSpec((1, 2048, tm), lambda i: (i, 0, 0)),
                             _rows(tm, 1024), _rows(tm, 256), _rows(tm, 256)],
                  out_shape=[SDS((T, Q_LORA), BF16), SDS((T, KV_LORA), BF16), SDS((T, 2048), BF16),
                             SDS((T, 1024), BF16), SDS((T // tm, 2048, tm), BF16), SDS((T // tm, 2048, tm), BF16),
                             SDS((T // tm, 2048, tm), BF16),
                             SDS((T, 1024), BF16), SDS((T, 256), BF16), SDS((T, 256), BF16)],
                  sem=("parallel",))(z, z, z, z, z, z, gq, gkv, wkn, wv, wqb_t, wkn_t, wv_t, *tab_m, *tab_mt, *tab_s)


def _mla_fwd(qt, km, vt, tb):
    T = km.shape[0]
    nb = T // tb
    cc = ATT_COLS

    per = 2 if nb % 2 == 0 else 1

    def body(q_ref, k_ref, vt_ref, o_ref, l_ref, s_ref, p_ref, al_ref, m_ref, acc_ref):
        for blk in range(per):
            one_block(per * pl.program_id(1) + blk, blk, q_ref, k_ref, vt_ref, o_ref, l_ref, s_ref, p_ref, al_ref, m_ref, acc_ref)

    def one_block(i, blk, q_ref, k_ref, vt_ref, o_ref, l_ref, s_ref, p_ref, al_ref, m_ref, acc_ref):
        m_ref[...] = jnp.full(m_ref.shape, NEG, F32)
        acc_ref[...] = jnp.zeros_like(acc_ref)
        p_ref[1] = jnp.zeros(p_ref.shape[1:], BF16)
        al_ref[1] = jnp.ones(al_ref.shape[1:], F32)
        key = lax.broadcasted_iota(jnp.int32, (tb, cc), 0)
        qry = lax.broadcasted_iota(jnp.int32, (tb, cc), 1)

        def scores(j, slot):
            off = pl.multiple_of(j * tb, tb)
            for hh in range(2):
                sl = slice(LANES * hh, LANES * (hh + 1))
                s_ref[slot, hh] = _dot(k_ref[pl.ds(off, tb), sl], q_ref[blk, sl, :])

        def softmax(slot, diagonal):
            chains = [(hh, slice(cc * c, cc * (c + 1)), c) for hh in range(2) for c in range(tb // cc)]

            def scaled(hh, cols, c):
                t = s_ref[slot, hh, :, cols] * MLA_LOG2_SCALE
                return jnp.where(key <= qry + cc * c, t, NEG) if diagonal else t

            tops = []
            for hh, cols, c in chains:
                if diagonal:
                    top = jnp.max(scaled(hh, cols, c), axis=0, keepdims=True)
                else:
                    top = jnp.max(s_ref[slot, hh, :, cols], axis=0, keepdims=True) * MLA_LOG2_SCALE
                m_old = m_ref[hh, :, cols]
                mn = jnp.maximum(m_old, top)
                m_ref[hh, :, cols] = mn
                al_ref[slot, hh, :, cols] = jnp.exp2(m_old - mn)
                tops.append(mn)
            for (hh, cols, c), mn in zip(chains, tops):
                p_ref[slot, hh, :, cols] = jnp.exp2(scaled(hh, cols, c) - mn).astype(BF16)

        def accumulate(j, slot):
            for hh in range(2):
                acc_ref[hh] = al_ref[slot, hh] * acc_ref[hh] + _dot(vt_ref[j, LANES * hh:LANES * hh + VT_ROWS, :], p_ref[slot, hh])

        def step(t, carry):
            scores(2 * t + 1, 1)
            accumulate(jnp.maximum(2 * t - 1, 0), 1)
            softmax(0, False)
            scores(2 * t + 2, 0)
            accumulate(2 * t, 0)
            softmax(1, False)
            return carry

        scores(0, 0)
        lax.fori_loop(0, i // 2, step, 0)

        @pl.when(i % 2 == 1)
        def _():
            scores(i, 1)
            accumulate(jnp.maximum(i - 2, 0), 1)
            softmax(0, False)
            accumulate(i - 1, 0)
            softmax(1, True)
            accumulate(i, 1)

        @pl.when(i % 2 == 0)
        def _():
            accumulate(jnp.maximum(i - 1, 0), 1)
            softmax(0, True)
            accumulate(i, 0)
        den = [acc_ref[hh, 64:65, :] for hh in range(2)]
        o_ref[tb * blk:tb * (blk + 1), :] = jnp.concatenate([acc_ref[hh, 0:64, :] / den[hh] for hh in range(2)], axis=0).T
        sub = lax.broadcasted_iota(jnp.int32, (8, tb), 0)
        lse = [m_ref[hh] + jnp.log(den[hh]) * LOG2_E for hh in range(2)]
        l_ref[0, blk] = jnp.where(sub == 0, lse[0], jnp.where(sub == 1, lse[1], 0.0))

    return _pcall(body, name="mla_fwd", grid=(MLA_HEADS // 2, nb // per),
                  in_specs=[pl.BlockSpec((per, 256, tb), lambda p, i: (i, p, 0)), pl.BlockSpec((T, 256), lambda p, i: (0, p)),
                            pl.BlockSpec((nb, 2 * LANES, tb), lambda p, i: (0, p, 0))],
                  out_specs=[pl.BlockSpec((per * tb, LANES), lambda p, i: (i, p)),
                             pl.BlockSpec((1, per, 8, tb), lambda p, i: (p, i, 0, 0))],
                  out_shape=[SDS((T, D), F32), SDS((MLA_HEADS // 2, nb, 8, tb), F32)],
                  scratch=[pltpu.VMEM((2, 2, tb, tb), F32), pltpu.VMEM((2, 2, tb, tb), BF16), pltpu.VMEM((2, 2, 1, tb), F32),
                           pltpu.VMEM((2, 1, tb), F32), pltpu.VMEM((2, VT_ROWS, tb), F32)],
                  sem=("parallel", "arbitrary"))(qt, km, vt)


def _swa_mask(n):
    row = lax.broadcasted_iota(jnp.int32, (WINDOW, 2 * WINDOW), 0)
    col = lax.broadcasted_iota(jnp.int32, (WINDOW, 2 * WINDOW), 1)
    rel = row - col + WINDOW
    return (rel >= 0) & (rel < WINDOW) & ((col >= WINDOW) | (n > 0))


def _swa_specs(T):
    nb = T // WINDOW
    cur = lambda w: pl.BlockSpec((WINDOW, w), lambda n: (n, 0))
    prev = lambda w: pl.BlockSpec((WINDOW, w), lambda n: (jnp.maximum(n - 1, 0), 0))
    return nb, cur, prev


def _swa_fwd(sinks, qs, ks, vs):
    T = qs.shape[0]
    nb, cur, prev = _swa_specs(T)

    def body(sink_ref, q_ref, kc_ref, kp_ref, vc_ref, vp_ref, o_ref, l_ref, kb_ref, vb_ref, s_ref, p_ref):
        n = pl.program_id(0)
        mask = _swa_mask(n)
        lo = lax.broadcasted_iota(jnp.int32, (WINDOW, LANES), 1) < 64
        hi = jnp.logical_not(lo)
        for g in range(2):
            gs = slice(LANES * g, LANES * (g + 1))
            kb_ref[g] = jnp.concatenate([kp_ref[:, gs], kc_ref[:, gs]], axis=0)
            vb_ref[g] = jnp.concatenate([vp_ref[:, gs], vc_ref[:, gs]], axis=0)
        for h in range(SWA_HEADS):
            qp = q_ref[:, LANES * (h // 2):LANES * (h // 2 + 1)]
            qh = jnp.where(lo if h % 2 == 0 else hi, qp, jnp.zeros_like(qp))
            s_ref[h] = _dot_nt(qh, kb_ref[h // 8])
        for j in range(SWA_HEADS // 2):
            sl = slice(LANES * j, LANES * (j + 1))
            lses = []
            for h in (2 * j, 2 * j + 1):
                s = jnp.where(mask, s_ref[h] * SWA_SCALE, NEG)
                sk = sink_ref[h]
                m = jnp.maximum(jnp.max(s, axis=1, keepdims=True), sk)
                e = jnp.exp(s - m)
                den = jnp.sum(e, axis=1, keepdims=True) + jnp.exp(sk - m)
                p_ref[h] = (e / den).astype(BF16)
                lses.append(jnp.broadcast_to(m + jnp.log(den), (WINDOW, LANES)))
            l_ref[:, sl] = jnp.where(lo, lses[0], lses[1])
        for j in range(SWA_HEADS // 2):
            vb = vb_ref[j // 4]
            o_ref[:, LANES * j:LANES * (j + 1)] = jnp.where(lo, _dot(p_ref[2 * j], vb), _dot(p_ref[2 * j + 1], vb))

    return _pcall(body, name="swa_fwd", grid=(nb,),
                  in_specs=[pl.BlockSpec(memory_space=pltpu.SMEM), cur(D), cur(256), prev(256), cur(256), prev(256)],
                  out_specs=[cur(D), cur(D)], out_shape=[SDS((T, D), F32)] * 2,
                  scratch=[pltpu.VMEM((2, 2 * WINDOW, LANES), BF16), pltpu.VMEM((2, 2 * WINDOW, LANES), BF16),
                           pltpu.VMEM((SWA_HEADS, WINDOW, 2 * WINDOW), F32), pltpu.VMEM((SWA_HEADS, WINDOW, 2 * WINDOW), BF16)],
                  sem=("parallel",))(sinks, qs, ks, ks, vs, vs)


def _fwd_mix(om, os_, z, x, wmu, wsu, wo, g2, tm):
    T = x.shape[0]

    def body(om_ref, os_ref, ga_ref, gb_ref, x_ref, wmu_ref, wsu_ref, wo_ref, g2_ref,
             y_ref, yo_ref, au_ref, bu_ref, x1_ref):
        au = _dot(om_ref[...].astype(BF16), wmu_ref[...])
        bu = _dot(os_ref[...].astype(BF16), wsu_ref[...])
        au_ref[...] = au
        bu_ref[...] = bu
        y = (_sigmoid(ga_ref[...]) * au + _sigmoid(gb_ref[...]) * bu).astype(BF16)
        y_ref[...] = y
        yo = _dot(y, wo_ref[...])
        yo_ref[...] = yo
        x1_ref[...] = x_ref[...] + _rms(yo, g2_ref[...])

    r = _rows(tm, D)
    w = _full((D, D))
    return _pcall(body, name="fwd_mix", grid=(T // tm,),
                  in_specs=[r, r, _rows(tm, D, 1), _rows(tm, D, 2), r, w, w, w, _full((1, D))],
                  out_specs=[r] * 5,
                  out_shape=[SDS((T, D), BF16), SDS((T, D), F32), SDS((T, D), F32), SDS((T, D), F32), SDS((T, D), F32)],
                  sem=("parallel",))(om, os_, z, z, x, wmu, wsu, wo, g2)


def _fwd_mlp_up(x1, g3, w1, tm):
    T = x1.shape[0]

    def body(x_ref, g_ref, w_ref, h_ref, u_ref):
        h = _rms(x_ref[...], g_ref[...]).astype(BF16)
        h_ref[...] = h
        u_ref[...] = jnp.square(jnp.maximum(_dot(h, w_ref[...]), 0.0)).astype(BF16)

    return _pcall(body, name="fwd_mlp_up", grid=(T // tm,),
                  in_specs=[_rows(tm, D), _full((1, D)), _full((D, D_FF))],
                  out_specs=[_rows(tm, D), _rows(tm, D_FF)],
                  out_shape=[SDS((T, D), BF16), SDS((T, D_FF), BF16)],
                  sem=("parallel",))(x1, g3, w1)


def _fwd_mlp_down(u, w2, x1, g4, tm):
    T = x1.shape[0]

    def body(u_ref, w_ref, x_ref, g_ref, d_ref, x2_ref):
        d = _dot(u_ref[...], w_ref[...])
        d_ref[...] = d
        x2_ref[...] = x_ref[...] + _rms(d, g_ref[...])

    return _pcall(body, name="fwd_mlp_down", grid=(T // tm,),
                  in_specs=[_rows(tm, D_FF), _full((D_FF, D)), _rows(tm, D), _full((1, D))],
                  out_specs=[_rows(tm, D), _rows(tm, D)], out_shape=[SDS((T, D), F32)] * 2,
                  sem=("parallel",))(u, w2, x1, g4)


def _ple_fwd_bwd(p, x2, tgt, wple, g5, wpg, tm):
    T = x2.shape[0]

    def body(p_ref, x2_ref, t_ref, wple_ref, g5_ref, wpg_ref, loss_ref, dx2_ref, dgt_ref, de0_ref, dg5_ref):
        @pl.when(pl.program_id(0) == 0)
        def _():
            loss_ref[...] = jnp.zeros_like(loss_ref)
            dg5_ref[...] = jnp.zeros_like(dg5_ref)

        e0 = _dot(p_ref[...].astype(BF16), wple_ref[...])
        g5 = g5_ref[...]
        r = lax.rsqrt(jnp.mean(e0 * e0, axis=-1, keepdims=True) + EPS)
        en = e0 * r
        e = en * g5
        x2 = x2_ref[...]
        s = _sigmoid(_dot(x2.astype(BF16), wpg_ref[...]))
        diff = x2 + s * e - t_ref[...]
        sq = jnp.sum(jnp.sum(diff * diff, axis=1, keepdims=True), axis=0, keepdims=True)
        loss_ref[...] += jnp.broadcast_to(sq * (0.5 / D), loss_ref.shape)
        dx3 = diff * (1.0 / D)
        de = dx3 * s
        dgt = (dx3 * e * s * (1.0 - s)).astype(BF16)
        dgt_ref[...] = dgt
        dn = de * g5
        de0_ref[...] = (r * (dn - en * jnp.mean(dn * en, axis=-1, keepdims=True))).astype(BF16)
        dg5_ref[...] += jnp.sum(de * en, axis=0, keepdims=True)
        dx2_ref[...] = dx3 + _dot_nt(dgt, wpg_ref[...])

    r = _rows(tm, D)
    return _pcall(body, name="ple_fwd_bwd", grid=(T // tm,),
                  in_specs=[_rows(tm, PLE), r, r, _full((PLE, D)), _full((1, D)), _full((D, D))],
                  out_specs=[_full((8, LANES)), r, r, r, _full((1, D))],
                  out_shape=[SDS((8, LANES), F32), SDS((T, D), F32), SDS((T, D), BF16), SDS((T, D), BF16), SDS((1, D), F32)],
                  sem=("arbitrary",))(p, x2, tgt, wple, g5, wpg)


def _bwd_mlp_down(dx2, d, g4, w2, u, tm):
    T = dx2.shape[0]

    def body(dx_ref, d_ref, g_ref, w_ref, u_ref, dd_ref, da_ref, dg_ref):
        @pl.when(pl.program_id(0) == 0)
        def _():
            dg_ref[...] = jnp.zeros_like(dg_ref)

        dd, dg = _rms_bwd(dx_ref[...], d_ref[...], g_ref[...])
        dg_ref[...] += dg
        ddb = dd.astype(BF16)
        dd_ref[...] = ddb
        du = _dot_nt(ddb, w_ref[...])
        da_ref[...] = (du * (2.0 * jnp.sqrt(u_ref[...].astype(F32)))).astype(BF16)

    return _pcall(body, name="bwd_mlp_down", grid=(T // tm,),
                  in_specs=[_rows(tm, D), _rows(tm, D), _full((1, D)), _full((D_FF, D)), _rows(tm, D_FF)],
                  out_specs=[_rows(tm, D), _rows(tm, D_FF), _full((1, D))],
                  out_shape=[SDS((T, D), BF16), SDS((T, D_FF), BF16), SDS((1, D), F32)],
                  sem=("arbitrary",))(dx2, d, g4, w2, u)


def _bwd_mlp_up(da, w1, x1, g3, dx2, tm):
    T = dx2.shape[0]

    def body(da_ref, w_ref, x_ref, g_ref, dx2_ref, dx1_ref, dg_ref):
        @pl.when(pl.program_id(0) == 0)
        def _():
            dg_ref[...] = jnp.zeros_like(dg_ref)

        dh = _dot_nt(da_ref[...], w_ref[...])
        dx, dg = _rms_bwd(dh, x_ref[...], g_ref[...])
        dg_ref[...] += dg
        dx1_ref[...] = dx2_ref[...] + dx

    return _pcall(body, name="bwd_mlp_up", grid=(T // tm,),
                  in_specs=[_rows(tm, D_FF), _full((D, D_FF)), _rows(tm, D), _full((1, D)), _rows(tm, D)],
                  out_specs=[_rows(tm, D), _full((1, D))],
                  out_shape=[SDS((T, D), F32), SDS((1, D), F32)], sem=("arbitrary",))(da, w1, x1, g3, dx2)


def _bwd_mix(dx1, yo, g2, wo, z, au, bu, wmu, wsu, om, tm):
    T = dx1.shape[0]

    def body(dx_ref, yo_ref, g_ref, wo_ref, ga_ref, gb_ref, au_ref, bu_ref, wmu_ref, wsu_ref, om_ref,
             dyo_ref, dg_ref, dau_ref, dbu_ref, dga_ref, dgb_ref, dos_ref, dl_ref, dot_ref):
        @pl.when(pl.program_id(0) == 0)
        def _():
            dg_ref[...] = jnp.zeros_like(dg_ref)

        dyo, dg = _rms_bwd(dx_ref[...], yo_ref[...], g_ref[...])
        dg_ref[...] += dg
        dyob = dyo.astype(BF16)
        dyo_ref[...] = dyob
        dy = _dot_nt(dyob, wo_ref[...])
        sa = _sigmoid(ga_ref[...])
        sb = _sigmoid(gb_ref[...])
        dau = (dy * sa).astype(BF16)
        dbu = (dy * sb).astype(BF16)
        dau_ref[...] = dau
        dbu_ref[...] = dbu
        dga_ref[...] = (dy * au_ref[...] * sa * (1.0 - sa)).astype(BF16)
        dgb_ref[...] = (dy * bu_ref[...] * sb * (1.0 - sb)).astype(BF16)
        dom = _dot_nt(dau, wmu_ref[...])
        dos_ref[...] = _dot_nt(dbu, wsu_ref[...])
        prod = dom * om_ref[...]
        sub = lax.broadcasted_iota(jnp.int32, (8, tm), 0)
        for pr in range(MLA_HEADS // 2):
            sl = slice(LANES * pr, LANES * (pr + 1))
            pt = prod[:, sl].T
            d0 = jnp.sum(pt[0:64], axis=0, keepdims=True)
            d1 = jnp.sum(pt[64:128], axis=0, keepdims=True)
            dl_ref[pr, 0] = jnp.where(sub == 0, d0, jnp.where(sub == 1, d1, 0.0))
            dot_ref[0, sl, :] = dom[:, sl].T.astype(BF16)

    r = _rows(tm, D)
    w = _full((D, D))
    return _pcall(body, name="bwd_mix", grid=(T // tm,),
                  in_specs=[r, r, _full((1, D)), w, _rows(tm, D, 1), _rows(tm, D, 2), r, r, w, w, r],
                  out_specs=[r, _full((1, D)), r, r, r, r, r, pl.BlockSpec((MLA_HEADS // 2, 1, 8, tm), lambda i: (0, i, 0, 0)),
                             pl.BlockSpec((1, D, tm), lambda i: (i, 0, 0))],
                  out_shape=[SDS((T, D), BF16), SDS((1, D), F32), SDS((T, D), BF16), SDS((T, D), BF16), SDS((T, D), BF16),
                             SDS((T, D), BF16), SDS((T, D), F32), SDS((MLA_HEADS // 2, T // tm, 8, tm), F32),
                             SDS((T // tm, D, tm), BF16)],
                  sem=("arbitrary",))(dx1, yo, g2, wo, z, z, au, bu, wmu, wsu, om)


def _mla_bwd(qt, km, kt, vm, dot, lse, delta, tb):
    T = km.shape[0]
    nb = T // tb
    cc = ATT_COLS

    per = 2 if nb % 2 == 0 else 1

    def body(qt_ref, k_ref, kt_ref, v_ref, dot_ref, l_ref, dl_ref, dqt_ref, dkt_ref, dvt_ref,
             s_ref, dp_ref, p_ref, ds_ref, vh_ref):
        @pl.when(pl.program_id(1) == 0)
        def _():
            dqt_ref[...] = jnp.zeros_like(dqt_ref)

        dkt_ref[...] = jnp.zeros_like(dkt_ref)
        dvt_ref[...] = jnp.zeros_like(dvt_ref)
        for blk in range(per):
            one_block(per * pl.program_id(1) + blk, blk, qt_ref, k_ref, kt_ref, v_ref, dot_ref, l_ref, dl_ref, dqt_ref, dkt_ref,
                      dvt_ref, s_ref, dp_ref, p_ref, ds_ref, vh_ref)

    def one_block(j, blk, qt_ref, k_ref, kt_ref, v_ref, dot_ref, l_ref, dl_ref, dqt_ref, dkt_ref, dvt_ref,
                  s_ref, dp_ref, p_ref, ds_ref, vh_ref):
        lo = lax.broadcasted_iota(jnp.int32, (tb, LANES), 1) < 64
        key = lax.broadcasted_iota(jnp.int32, (tb, cc), 0)
        qry = lax.broadcasted_iota(jnp.int32, (tb, cc), 1)
        rows_j = slice(tb * blk, tb * (blk + 1))
        v = v_ref[rows_j, :]
        vh_ref[0] = jnp.where(lo, v, jnp.zeros_like(v))
        vh_ref[1] = jnp.where(lo, jnp.zeros_like(v), v)

        def scores(i, slot):
            for hh in range(2):
                sl = slice(LANES * hh, LANES * (hh + 1))
                s_ref[slot, hh] = _dot(k_ref[rows_j, sl], qt_ref[i, sl, :])
                dp_ref[slot, hh] = _dot(vh_ref[hh], dot_ref[i])

        def grads(i, slot, diagonal):
            lse_i = l_ref[0, i]
            delta_i = dl_ref[0, i]
            for hh in range(2):
                for c in range(tb // cc):
                    cols = slice(cc * c, cc * (c + 1))
                    p = jnp.exp2(s_ref[slot, hh, :, cols] * MLA_LOG2_SCALE - lse_i[hh:hh + 1, cols])
                    if diagonal:
                        p = jnp.where(key <= qry + cc * c, p, 0.0)
                    p_ref[hh, :, cols] = p.astype(BF16)
                    ds_ref[hh, :, cols] = (p * (dp_ref[slot, hh, :, cols] - delta_i[hh:hh + 1, cols]) * MLA_SCALE).astype(BF16)
            for hh in range(2):
                sl = slice(LANES * hh, LANES * (hh + 1))
                half = slice(64 * hh, 64 * (hh + 1))
                dvt_ref[blk, half, :] += _dot_nt(dot_ref[i, half, :], p_ref[hh])
                real = slice(LANES * hh, LANES * hh + MLA_NOPE + MLA_ROPE)
                dkt_ref[blk, real, :] += _dot_nt(qt_ref[i, real, :], ds_ref[hh])
                dqt_ref[i, real, :] += _dot(kt_ref[blk, real, :], ds_ref[hh])

        n_off = nb - 1 - j

        def step(u, carry):
            i0 = j + 1 + 2 * u
            scores(i0 + 1, 1)
            grads(i0, 0, False)
            scores(jnp.where(i0 + 2 < nb, i0 + 2, j), 0)
            grads(i0 + 1, 1, False)
            return carry

        scores(jnp.where(n_off > 0, j + 1, j), 0)
        lax.fori_loop(0, n_off // 2, step, 0)

        @pl.when(n_off % 2 == 1)
        def _():
            scores(j, 1)
            grads(nb - 1, 0, False)
            grads(j, 1, True)

        @pl.when(n_off % 2 == 0)
        def _():
            grads(j, 0, True)

    blk = lambda w: pl.BlockSpec((per * tb, w), lambda p, j: (j, p))
    stat = pl.BlockSpec((1, nb, 8, tb), lambda p, j: (p, 0, 0, 0))
    pair_t = lambda w: pl.BlockSpec((nb, w, tb), lambda p, j: (0, p, 0))
    blk_t = lambda w: pl.BlockSpec((per, w, tb), lambda p, j: (j, p, 0))
    return _pcall(body, name="mla_bwd", grid=(MLA_HEADS // 2, nb // per),
                  in_specs=[pair_t(256), blk(256), blk_t(256), blk(LANES), pair_t(LANES), stat, stat],
                  out_specs=[pair_t(256), blk_t(256), blk_t(LANES)],
                  out_shape=[SDS((nb, 2048, tb), F32), SDS((nb, 2048, tb), F32), SDS((nb, D, tb), F32)],
                  scratch=[pltpu.VMEM((2, 2, tb, tb), F32), pltpu.VMEM((2, 2, tb, tb), F32), pltpu.VMEM((2, tb, tb), BF16),
                           pltpu.VMEM((2, tb, tb), BF16), pltpu.VMEM((2, tb, LANES), BF16)],
                  sem=("parallel", "arbitrary"))(qt, km, kt, vm, dot, lse, delta)


def _swa_bwd(sinks, qs, ks, vs, do, o, lse):
    T = qs.shape[0]
    nb, cur, prev = _swa_specs(T)

    def body(sink_ref, q_ref, kc_ref, kp_ref, vc_ref, vp_ref, do_ref, o_ref, l_ref,
             dq_ref, dkc_ref, dkp_ref, dvc_ref, dvp_ref, dsink_ref, kb_ref, vb_ref, s_ref, dp_ref, p_ref, ds_ref):
        n = pl.program_id(0)

        @pl.when(n == 0)
        def _():
            dsink_ref[...] = jnp.zeros_like(dsink_ref)

        mask = _swa_mask(n)
        lo = lax.broadcasted_iota(jnp.int32, (WINDOW, LANES), 1) < 64
        hi = jnp.logical_not(lo)
        lane8 = lax.broadcasted_iota(jnp.int32, (8, LANES), 1)
        for g in range(2):
            gs = slice(LANES * g, LANES * (g + 1))
            kb_ref[g] = jnp.concatenate([kp_ref[:, gs], kc_ref[:, gs]], axis=0)
            vb_ref[g] = jnp.concatenate([vp_ref[:, gs], vc_ref[:, gs]], axis=0)

        def head(h):
            sl = slice(LANES * (h // 2), LANES * (h // 2 + 1))
            hm = lo if h % 2 == 0 else hi
            qp = q_ref[:, sl]
            return hm, sl, jnp.where(hm, qp, jnp.zeros_like(qp)), jnp.where(hm, do_ref[:, sl], 0.0).astype(BF16)

        for h in range(SWA_HEADS):
            _, _, qh, dom = head(h)
            s_ref[h] = _dot_nt(qh, kb_ref[h // 8])
            dp_ref[h] = _dot_nt(dom, vb_ref[h // 8])
        dsink = jnp.zeros((8, LANES), F32)
        for h in range(SWA_HEADS):
            hm, sl, _, _ = head(h)
            lse_h = jnp.max(jnp.where(hm, l_ref[:, sl], -jnp.inf), axis=1, keepdims=True)
            delta = jnp.sum(jnp.where(hm, do_ref[:, sl] * o_ref[:, sl], 0.0), axis=1, keepdims=True)
            p = jnp.exp(jnp.where(mask, s_ref[h] * SWA_SCALE, NEG) - lse_h)
            p_ref[h] = p.astype(BF16)
            ds_ref[h] = (p * (dp_ref[h] - delta) * SWA_SCALE).astype(BF16)
            d_sink = -jnp.sum(jnp.exp(sink_ref[h] - lse_h) * delta, axis=0, keepdims=True)
            dsink = dsink + jnp.where(lane8 == h, d_sink, 0.0)
        dsink_ref[...] += dsink
        for g in range(2):
            gs = slice(LANES * g, LANES * (g + 1))
            dkb = jnp.zeros((2 * WINDOW, LANES), F32)
            dvb = jnp.zeros((2 * WINDOW, LANES), F32)
            for j in range(4 * g, 4 * g + 4):
                dqs = []
                for h in (2 * j, 2 * j + 1):
                    _, _, qh, dom = head(h)
                    dvb = dvb + _dot_tn(p_ref[h], dom)
                    dkb = dkb + _dot_tn(ds_ref[h], qh)
                    dqs.append(_dot(ds_ref[h], kb_ref[g]))
                dq_ref[:, LANES * j:LANES * (j + 1)] = jnp.where(lo, dqs[0], dqs[1])
            dkp_ref[:, gs] = dkb[:WINDOW]
            dkc_ref[:, gs] = dkb[WINDOW:]
            dvp_ref[:, gs] = dvb[:WINDOW]
            dvc_ref[:, gs] = dvb[WINDOW:]

    band = pltpu.VMEM((2, 2 * WINDOW, LANES), BF16)
    return _pcall(body, name="swa_bwd", grid=(nb,),
                  in_specs=[pl.BlockSpec(memory_space=pltpu.SMEM), cur(D), cur(256), prev(256), cur(256), prev(256),
                            cur(D), cur(D), cur(D)],
                  out_specs=[cur(D), cur(256), cur(256), cur(256), cur(256), _full((8, LANES))],
                  out_shape=[SDS((T, D), F32), SDS((T, 256), F32), SDS((T, 256), F32), SDS((T, 256), F32), SDS((T, 256), F32),
                             SDS((8, LANES), F32)],
                  scratch=[band, band, pltpu.VMEM((SWA_HEADS, WINDOW, 2 * WINDOW), F32),
                           pltpu.VMEM((SWA_HEADS, WINDOW, 2 * WINDOW), F32), pltpu.VMEM((SWA_HEADS, WINDOW, 2 * WINDOW), BF16),
                           pltpu.VMEM((SWA_HEADS, WINDOW, 2 * WINDOW), BF16)],
                  sem=("arbitrary",))(sinks, qs, ks, ks, vs, vs, do, o, lse)


def _bwd_qkv(dqm, dkm, dvm, dqs, dkc, dkp, dvc, dvp, z, gq, gkv, wqb, wkn, wv, tab_m, tab_s):
    T = z.shape[0]
    tm = WINDOW
    nb = T // tm
    per = dqm.shape[2] // tm

    tab_mt = [t.T for t in tab_m]
    half = MLA_ROPE // 2

    def rope_t(v, c, a, b):
        return v * c + pltpu.roll(v, LANES - half, 0) * a + pltpu.roll(v, half, 0) * b

    def rms_bwd_t(dy, x, g):
        r = lax.rsqrt(jnp.mean(x * x, axis=0, keepdims=True) + EPS)
        xn = x * r
        dn = dy * g
        return r * (dn - xn * jnp.mean(dn * xn, axis=0, keepdims=True)), jnp.sum(dy * xn, axis=1, keepdims=True)

    def body(dqm_ref, dkm_ref, dvm_ref, dqs_ref, dkc_ref, dkp_ref, dvc_ref, dvp_ref, qa_ref, kva_ref, gq_ref, gkv_ref,
             wqb_ref, wkn_ref, wv_ref, cmt_ref, amt_ref, bmt_ref, cs_ref, as_ref, bs_ref,
             dq_out, dkn_out, dv_out, dsq_ref, drest_ref, dgq_ref, dgkv_ref):
        i = pl.program_id(0)

        @pl.when(i == 0)
        def _():
            dgq_ref[...] = jnp.zeros_like(dgq_ref)
            dgkv_ref[...] = jnp.zeros_like(dgkv_ref)

        cmt, amt, bmt = cmt_ref[...], -amt_ref[...], -bmt_ref[...]
        cs, as_, bs = cs_ref[...], -as_ref[...], -bs_ref[...]
        row = lax.broadcasted_iota(jnp.int32, (LANES, tm), 0)
        nope = row < MLA_NOPE
        roped = jnp.logical_and(row >= MLA_NOPE, row < MLA_NOPE + MLA_ROPE)
        dkr = jnp.zeros((LANES, tm), F32)
        for h in range(MLA_HEADS):
            sl = slice(LANES * h, LANES * (h + 1))
            dq_out[0, sl, :] = rope_t(dqm_ref[0, sl, :], cmt, amt, bmt).astype(BF16)
            dk_h = dkm_ref[0, sl, :]
            dkn_out[0, sl, :] = jnp.where(nope, dk_h, 0.0).astype(BF16)
            dkr = dkr + jnp.where(roped, dk_h, 0.0)
        dv_out[0] = dvm_ref[0].astype(BF16)
        dqn = _dot(wqb_ref[...], dq_out[0])
        dkvn = _dot(wkn_ref[...], dkn_out[0]) + _dot(wv_ref[...], dv_out[0])
        dqa, dgq = rms_bwd_t(dqn, qa_ref[...].T, gq_ref[...])
        dkva, dgkv = rms_bwd_t(dkvn, kva_ref[...].T, gkv_ref[...])
        dgq_ref[...] += dgq
        dgkv_ref[...] += dgkv
        for j in range(D // LANES):
            sl = slice(LANES * j, LANES * (j + 1))
            dsq_ref[:, sl] = _rope(dqs_ref[:, sl], cs, as_, bs, SWA_HD // 2).astype(BF16)
        keep = (i < nb - 1).astype(F32)
        drest_ref[:, 0:256] = dqa.T.astype(BF16)
        for j in range(2):
            sl = slice(LANES * j, LANES * (j + 1))
            dk = dkc_ref[:, sl] + keep * dkp_ref[:, sl]
            drest_ref[:, 256 + LANES * j:256 + LANES * (j + 1)] = _rope(dk, cs, as_, bs, SWA_HD // 2).astype(BF16)
        drest_ref[:, 512:768] = (dvc_ref[...] + keep * dvp_ref[...]).astype(BF16)
        drest_ref[:, 768:896] = dkva.T.astype(BF16)
        drest_ref[:, 896:1024] = rope_t(dkr, cmt, amt, bmt).T.astype(BF16)

    nxt = pl.BlockSpec((tm, 256), lambda i: (jnp.minimum(i + 1, nb - 1), 0))
    tab = [_rows(tm, LANES)] * 3
    tab_t = [pl.BlockSpec((LANES, tm), lambda i: (0, i))] * 3
    blk_t = lambda w: pl.BlockSpec((1, w, tm), lambda i: (i // per, 0, i % per))
    return _pcall(body, name="bwd_qkv", grid=(nb,),
                  in_specs=[blk_t(2048), blk_t(2048), blk_t(1024), _rows(tm, 1024), _rows(tm, 256), nxt,
                            _rows(tm, 256), nxt, _rows(tm, 256, 12), _rows(tm, 128, 30), _full((Q_LORA, 1)), _full((KV_LORA, 1)),
                            _full((Q_LORA, 2048)), _full((KV_LORA, 2048)), _full((KV_LORA, 1024))] + tab_t + tab,
                  out_specs=[blk_t(2048), blk_t(2048), blk_t(1024), _rows(tm, 1024), _rows(tm, 1024),
                             _full((Q_LORA, 1)), _full((KV_LORA, 1))],
                  out_shape=[SDS(dqm.shape, BF16), SDS(dkm.shape, BF16), SDS(dvm.shape, BF16), SDS((T, 1024), BF16),
                             SDS((T, 1024), BF16), SDS((Q_LORA, 1), F32), SDS((KV_LORA, 1), F32)],
                  sem=("arbitrary",))(dqm, dkm, dvm, dqs, dkc, dkp, dvc, dvp, z, z, gq.reshape(Q_LORA, 1),
                                      gkv.reshape(KV_LORA, 1), wqb, wkn, wv, *tab_mt, *tab_s)


def _bwd_in(dsq, dga, dgb, drest, w_in_p, x, g1, dx1, tm):
    T = x.shape[0]

    def body(a_ref, b_ref, c_ref, d_ref, w_ref, x_ref, g_ref, dx1_ref, dx_ref, dg_ref):
        @pl.when(pl.program_id(0) == 0)
        def _():
            dg_ref[...] = jnp.zeros_like(dg_ref)

        dh = (_dot_nt(a_ref[...], w_ref[:, 0:1024]) + _dot_nt(b_ref[...], w_ref[:, 1024:2048])
              + _dot_nt(c_ref[...], w_ref[:, 2048:3072]) + _dot_nt(d_ref[...], w_ref[:, 3072:4096]))
        dx, dg = _rms_bwd(dh, x_ref[...], g_ref[...])
        dg_ref[...] += dg
        dx_ref[...] = dx1_ref[...] + dx

    r = _rows(tm, D)
    return _pcall(body, name="bwd_in", grid=(T // tm,),
                  in_specs=[r, r, r, r, _full((D, NZ)), r, _full((1, D)), r],
                  out_specs=[r, _full((1, D))], out_shape=[SDS((T, D), F32), SDS((1, D), F32)],
                  sem=("arbitrary",))(dsq, dga, dgb, drest, w_in_p, x, g1, dx1)


def _wgrad(a, g, name, into=None):
    T, K = a.shape
    N = g.shape[1]
    tk, tn, tt = min(K, 1024), min(N, 1024), min(T, 1024)
    if into is not None:
        buf, weight = into
        _, row0, lane0 = PACK_AT[weight]
        shard = {n: (r, c) for n, r, c in BIG}[weight]
        assert lane0 == 0 and shard[1] == D and tk % shard[0] == 0
        per_step = tk // shard[0]
    assert K % tk == 0 and N % tn == 0 and T % tt == 0, (a.shape, g.shape)
    steps = T // tt

    def body(a_ref, g_ref, *rest):
        o_ref, acc_ref = rest[-2:]
        t = pl.program_id(2)

        @pl.when(t == 0)
        def _():
            acc_ref[...] = jnp.zeros_like(acc_ref)

        acc_ref[...] += _dot_tn(a_ref[...].astype(BF16), g_ref[...].astype(BF16))

        @pl.when(t == steps - 1)
        def _():
            o_ref[...] = acc_ref[...].astype(o_ref.dtype).reshape(o_ref.shape)

    in_specs = [pl.BlockSpec((tt, tk), lambda k, n, t: (t, k)), pl.BlockSpec((tt, tn), lambda k, n, t: (t, n))]
    if into is None:
        return _pcall(body, name=name, grid=(K // tk, N // tn, steps), in_specs=in_specs,
                      out_specs=pl.BlockSpec((tk, tn), lambda k, n, t: (k, n)), out_shape=SDS((K, N), F32),
                      scratch=[pltpu.VMEM((tk, tn), F32)], sem=("parallel", "parallel", "arbitrary"))(a, g)
    assert row0 % shard[0] == 0 and (K // tk) * (N // tn) * per_step == N_CHIPS
    return _pcall(body, name=name, grid=(K // tk, N // tn, steps), in_specs=in_specs + [ANY],
                  out_specs=pl.BlockSpec((per_step, shard[0], tn), lambda k, n, t: (k + n, row0 // shard[0], 0)),
                  out_shape=SDS(buf.shape, buf.dtype),
                  scratch=[pltpu.VMEM((tk, tn), F32)], sem=("parallel", "parallel", "arbitrary"), aliases={2: 0})(a, g, buf)


def _wgrad_t(at, g, name):
    nblk, K, tt = at.shape
    N = g.shape[1]
    tk = min(K, 1024)
    per_step = 4 if nblk % 4 == 0 else 1
    assert K % tk == 0 and g.shape[0] == nblk * tt

    def body(a_ref, g_ref, o_ref):
        @pl.when(pl.program_id(1) == 0)
        def _():
            o_ref[...] = jnp.zeros_like(o_ref)

        acc = _dot(a_ref[0], g_ref[0:tt, :].astype(BF16))
        for b in range(1, per_step):
            acc = acc + _dot(a_ref[b], g_ref[tt * b:tt * (b + 1), :].astype(BF16))
        o_ref[...] += acc

    return _pcall(body, name=name, grid=(K // tk, nblk // per_step),
                  in_specs=[pl.BlockSpec((per_step, tk, tt), lambda k, t: (t, k, 0)),
                            pl.BlockSpec((per_step * tt, N), lambda k, t: (t, 0))],
                  out_specs=pl.BlockSpec((tk, N), lambda k, t: (k, 0)), out_shape=SDS((K, N), F32),
                  sem=("parallel", "arbitrary"))(at, g)


def _adamw(w, packed_g, m, v, name, transposed=False):
    R, C = w.shape[1:][::-1] if transposed else w.shape[1:]
    _, row0, lane0 = PACK_AT[name]
    tr = min(R, 256 if row0 % 256 == 0 else 128)
    assert row0 % tr == 0 and R % tr == 0

    def body(w_ref, g_ref, m_ref, v_ref, go_ref, d_ref, m2_ref, v2_ref):
        g_ = g_ref[...].T[lane0:lane0 + C] if transposed else g_ref[:, lane0:lane0 + C]
        go_ref[0] = g_
        m2 = ADAM_B1 * m_ref[0] + (1.0 - ADAM_B1) * g_
        v2 = ADAM_B2 * v_ref[0] + (1.0 - ADAM_B2) * jnp.square(g_)
        m_hat = m2 / (1.0 - ADAM_B1 ** ADAM_STEP)
        v_hat = v2 / (1.0 - ADAM_B2 ** ADAM_STEP)
        d_ref[0] = -ADAM_LR * (m_hat / (jnp.sqrt(v_hat) + ADAM_EPS) + ADAM_WD * w_ref[0])
        m2_ref[0] = m2
        v2_ref[0] = v2

    r = pl.BlockSpec((1, C, tr), lambda i: (0, 0, i)) if transposed else pl.BlockSpec((1, tr, C), lambda i: (0, i, 0))
    return _pcall(body, name="adamw_" + name, grid=(R // tr,),
                  in_specs=[r, pl.BlockSpec((tr, D), lambda i: (row0 // tr + i, 0)), r, r], out_specs=[r] * 4,
                  out_shape=[SDS(w.shape, F32)] * 4, sem=("parallel",))(w, packed_g, m, v)


def _adamw_small(w, parts, m, v):
    def body(w_ref, p_ref, m_ref, v_ref, g_ref, d_ref, m2_ref, v2_ref):
        g_ = p_ref[0]
        for k in range(1, N_DEV):
            g_ = g_ + p_ref[k]
        g_ref[...] = g_
        m2 = ADAM_B1 * m_ref[...] + (1.0 - ADAM_B1) * g_
        v2 = ADAM_B2 * v_ref[...] + (1.0 - ADAM_B2) * jnp.square(g_)
        m_hat = m2 / (1.0 - ADAM_B1 ** ADAM_STEP)
        v_hat = v2 / (1.0 - ADAM_B2 ** ADAM_STEP)
        d_ref[...] = -ADAM_LR * (m_hat / (jnp.sqrt(v_hat) + ADAM_EPS) + ADAM_WD * w_ref[...])
        m2_ref[...] = m2
        v2_ref[...] = v2

    s = _full((8, D))
    return _pcall(body, name="adamw_small", grid=(1,), in_specs=[s, _full((N_DEV, 8, D)), s, s], out_specs=[s] * 4,
                  out_shape=[SDS((8, D), F32)] * 4, sem=("arbitrary",))(w, parts, m, v)


ANY = pl.BlockSpec(memory_space=pl.ANY)


def _place():
    x, y, c = lax.axis_index("x"), lax.axis_index("y"), lax.axis_index("c")
    chips = [(1 - x, y), (x, 1 - y), (1 - x, 1 - y)]
    return x, y, c, chips


def _all_gather(wpk):
    rows = wpk.shape[0]
    HALF = rows // 2
    assert HALF % 16 == 0

    def body(in_ref, out_ref, send_sems, recv_sems):
        x, y, c, chips = _place()
        half = pl.ds(pl.multiple_of(c * HALF, 16), HALF)
        other = pl.ds(pl.multiple_of((1 - c) * HALF, 16), HALF)

        def copy(k, src, dst, to):
            return pltpu.make_async_remote_copy(src_ref=src, dst_ref=dst, send_sem=send_sems.at[k], recv_sem=recv_sems.at[k],
                                                device_id=to, device_id_type=MESH)

        first = [copy(k, in_ref.at[half], out_ref.at[2 * x + y, half], (cx, cy, c)) for k, (cx, cy) in enumerate(chips)]
        first.append(copy(6, in_ref, out_ref.at[2 * x + y], (x, y, 1 - c)))
        for cp in first:
            cp.start()
        passed = []
        for k, (cx, cy) in enumerate(chips):
            slot = out_ref.at[2 * cx + cy, half]
            copy(k, slot, slot, (x, y, c)).wait_recv()
            fwd = copy(3 + k, slot, slot, (x, y, 1 - c))
            fwd.start()
            passed.append(fwd)
        for k, (cx, cy) in enumerate(chips):
            slot = out_ref.at[2 * cx + cy, other]
            copy(3 + k, slot, slot, (x, y, c)).wait_recv()
        copy(6, in_ref, out_ref.at[2 * x + y], (x, y, c)).wait_recv()
        for cp in first + passed:
            cp.wait_send()

    return _pcall(body, name="all_gather_weights", in_specs=[ANY], out_specs=ANY,
                  out_shape=SDS((N_CHIPS, rows, D), BF16),
                  scratch=[pltpu.SemaphoreType.DMA((7,)), pltpu.SemaphoreType.DMA((7,))])(wpk)


HBM = pl.BlockSpec(memory_space=pltpu.HBM)
SEM = pl.BlockSpec(memory_space=pltpu.SEMAPHORE)
DATAFLOW = pltpu.SideEffectType.DATAFLOW_SIDE_EFFECTING


def _in_hbm(a):
    return pltpu.with_memory_space_constraint(a, pltpu.HBM)


def _gather_late_start(wpk, after):
    rows = wpk.shape[0]

    def body(in_ref, land_ref, after_ref, send_sems, recv_sems, in_thru, land_thru, token):
        x, y, c, chips = _place()
        for k, to in enumerate([(cx, cy, c) for cx, cy in chips] + [(x, y, 1 - c)]):
            pltpu.make_async_remote_copy(src_ref=in_ref, dst_ref=land_ref.at[2 * x + y], send_sem=send_sems.at[k],
                                         recv_sem=recv_sems.at[k], device_id=to, device_id_type=MESH).start()
        token[...] = jnp.zeros_like(token)

    return pl.pallas_call(
        body, name="gather_late_start",
        out_shape=(pltpu.SemaphoreType.DMA((4,)), pltpu.SemaphoreType.DMA((4,)), pltpu.HBM(wpk.shape, wpk.dtype),
                   pltpu.HBM((N_CHIPS, rows, D), wpk.dtype), SDS((8, LANES), F32)),
        in_specs=(HBM, HBM, ANY), out_specs=(SEM, SEM, HBM, HBM, pl.BlockSpec(memory_space=pltpu.VMEM)),
        input_output_aliases={0: 2, 1: 3}, compiler_params=pltpu.CompilerParams(has_side_effects=DATAFLOW),
    )(_in_hbm(wpk), _in_hbm(lax.empty((N_CHIPS, rows, D), wpk.dtype)), after)


def _gather_late_wait(send_sems, recv_sems, in_thru, land_thru, after):
    def body(in_ref, land_ref, send_sems, recv_sems, after_ref, after2_ref, in_dead, got_ref):
        x, y, c, chips = _place()
        for k, (sx, sy) in enumerate(chips + [(x, y)]):
            cp = pltpu.make_async_remote_copy(src_ref=in_ref, dst_ref=land_ref.at[2 * sx + sy], send_sem=send_sems.at[k],
                                              recv_sem=recv_sems.at[k], device_id=(x, y, c), device_id_type=MESH)
            cp.wait_send()
            cp.wait_recv()

    return pl.pallas_call(
        body, name="gather_late_wait",
        out_shape=(pltpu.HBM(in_thru.shape, in_thru.dtype), pltpu.HBM(land_thru.shape, land_thru.dtype)),
        in_specs=(HBM, HBM, SEM, SEM, ANY, ANY), out_specs=(HBM, HBM), input_output_aliases={0: 0, 1: 1},
        compiler_params=pltpu.CompilerParams(has_side_effects=DATAFLOW),
    )(in_thru, land_thru, send_sems, recv_sems, *after)[1]


def _rs_sibling(gpk):
    HALF = gpk.shape[1] // 2

    def body(in_ref, out_ref, send_sem, recv_sem):
        x, y, c, _ = _place()
        theirs = pl.ds(pl.multiple_of((1 - c) * HALF, 16), HALF)
        cp = pltpu.make_async_remote_copy(src_ref=in_ref.at[:, theirs], dst_ref=out_ref, send_sem=send_sem, recv_sem=recv_sem,
                                          device_id=(x, y, 1 - c), device_id_type=MESH)
        cp.start()
        cp.wait()

    return _pcall(body, name="rs_sibling", in_specs=[ANY], out_specs=ANY, out_shape=SDS((N_CHIPS, HALF, D), gpk.dtype),
                  scratch=[pltpu.SemaphoreType.DMA, pltpu.SemaphoreType.DMA])(gpk)


def _rs_add_sibling(cidx, gpk, got):
    HALF = got.shape[1]
    th = HALF // 4
    nh = HALF // th
    assert th % 16 == 0

    def body(c_ref, a_ref, b_ref, o_ref):
        o_ref[...] = (a_ref[...].astype(F32) + b_ref[...].astype(F32)).astype(BF16)

    gs = pltpu.PrefetchScalarGridSpec(
        num_scalar_prefetch=1, grid=(N_CHIPS, nh),
        in_specs=[pl.BlockSpec((1, th, D), lambda j, i, c: (j, c[0] * nh + i, 0)), pl.BlockSpec((1, th, D), lambda j, i, c: (j, i, 0))],
        out_specs=pl.BlockSpec((1, th, D), lambda j, i, c: (j, i, 0)))
    return pl.pallas_call(body, name="rs_add_sibling", grid_spec=gs, out_shape=SDS((N_CHIPS, HALF, D), BF16),
                          compiler_params=pltpu.CompilerParams(dimension_semantics=("parallel", "parallel"),
                                                               vmem_limit_bytes=48 << 20))(cidx, gpk, got)


def _rs_chips_start(part, small, after):
    def body(p_ref, s_ref, land_ref, sland_ref, after_ref, send_sems, recv_sems, p_thru, s_thru, land_thru, sland_thru, token):
        x, y, c, chips = _place()
        for k, (cx, cy) in enumerate(chips):
            pltpu.make_async_remote_copy(src_ref=p_ref.at[2 * cx + cy], dst_ref=land_ref.at[2 * x + y], send_sem=send_sems.at[k],
                                         recv_sem=recv_sems.at[k], device_id=(cx, cy, c), device_id_type=MESH).start()
        peers = [(x, y, 1 - c)] + [(cx, cy, c) for cx, cy in chips] + [(cx, cy, 1 - c) for cx, cy in chips]
        for k, to in enumerate(peers):
            pltpu.make_async_remote_copy(src_ref=s_ref, dst_ref=sland_ref.at[4 * x + 2 * y + c], send_sem=send_sems.at[3 + k],
                                         recv_sem=recv_sems.at[3 + k], device_id=to, device_id_type=MESH).start()
        token[...] = jnp.zeros_like(token)

    return pl.pallas_call(
        body, name="rs_chips_start",
        out_shape=(pltpu.SemaphoreType.DMA((10,)), pltpu.SemaphoreType.DMA((10,)), pltpu.HBM(part.shape, part.dtype),
                   pltpu.HBM(small.shape, small.dtype), pltpu.HBM(part.shape, part.dtype), pltpu.HBM((N_DEV, 8, D), F32),
                   SDS((8, LANES), F32)),
        in_specs=(HBM, HBM, HBM, HBM, ANY), out_specs=(SEM, SEM, HBM, HBM, HBM, HBM, pl.BlockSpec(memory_space=pltpu.VMEM)),
        input_output_aliases={0: 2, 1: 3, 2: 4, 3: 5}, compiler_params=pltpu.CompilerParams(has_side_effects=DATAFLOW),
    )(_in_hbm(part), _in_hbm(small), _in_hbm(lax.empty(part.shape, part.dtype)), _in_hbm(lax.empty((N_DEV, 8, D), F32)), after)


def _rs_chips_wait(send_sems, recv_sems, p_thru, s_thru, land_thru, sland_thru, after):
    def body(p_ref, s_ref, land_ref, sland_ref, send_sems, recv_sems, *after_and_outputs):
        x, y, c, chips = _place()
        for k, (cx, cy) in enumerate(chips):
            cp = pltpu.make_async_remote_copy(src_ref=p_ref.at[0], dst_ref=land_ref.at[2 * cx + cy], send_sem=send_sems.at[k],
                                              recv_sem=recv_sems.at[k], device_id=(cx, cy, c), device_id_type=MESH)
            cp.wait_send()
            cp.wait_recv()
        peers = [(x, y, 1 - c)] + [(cx, cy, c) for cx, cy in chips] + [(cx, cy, 1 - c) for cx, cy in chips]
        for k, (px, py, pc) in enumerate(peers):
            cp = pltpu.make_async_remote_copy(src_ref=s_ref, dst_ref=sland_ref.at[4 * px + 2 * py + pc], send_sem=send_sems.at[3 + k],
                                              recv_sem=recv_sems.at[3 + k], device_id=(px, py, pc), device_id_type=MESH)
            cp.wait_send()
            cp.wait_recv()

    hbm = lambda a: pltpu.HBM(a.shape, a.dtype)
    outs = pl.pallas_call(
        body, name="rs_chips_wait", out_shape=(hbm(p_thru), hbm(s_thru), hbm(land_thru), hbm(sland_thru)),
        in_specs=(HBM, HBM, HBM, HBM, SEM, SEM) + (ANY,) * len(after), out_specs=(HBM, HBM, HBM, HBM),
        input_output_aliases={0: 0, 1: 1, 2: 2, 3: 3}, compiler_params=pltpu.CompilerParams(has_side_effects=DATAFLOW),
    )(p_thru, s_thru, land_thru, sland_thru, send_sems, recv_sems, *after)
    return outs[0], outs[2], outs[3]


def _rs_add_chips(qidx, part, parts):
    HALF = part.shape[1]
    th = HALF // 4
    nh = HALF // th
    assert th % 16 == 0

    def body(q_ref, own_ref, p_ref, o_ref):
        for me in range(N_CHIPS):
            @pl.when(q_ref[0] == me)
            def _(me=me):
                t = [(own_ref[0] if j == me else p_ref[j]).astype(F32) for j in range(N_CHIPS)]
                o_ref[...] = ((t[0] + t[1]) + t[2]) + t[3]

    gs = pltpu.PrefetchScalarGridSpec(
        num_scalar_prefetch=1, grid=(HALF // th,),
        in_specs=[pl.BlockSpec((1, th, D), lambda i, q: (q[0], i, 0)), pl.BlockSpec((N_CHIPS, th, D), lambda i, q: (0, i, 0))],
        out_specs=pl.BlockSpec((th, D), lambda i, q: (q[1] * nh + i, 0)))
    return pl.pallas_call(body, name="rs_add_chips", grid_spec=gs, out_shape=SDS((2 * HALF, D), F32),
                          compiler_params=pltpu.CompilerParams(dimension_semantics=("parallel",),
                                                               vmem_limit_bytes=48 << 20))(qidx, part, parts)


def _rs_join(shard, name):
    HALF = shard.shape[0] // 2

    def body(in_ref, out_ref, send_sem, recv_sem):
        x, y, c, _ = _place()
        rows = pl.ds(pl.multiple_of(c * HALF, 16), HALF)
        cp = pltpu.make_async_remote_copy(src_ref=in_ref.at[rows], dst_ref=out_ref.at[rows], send_sem=send_sem, recv_sem=recv_sem,
                                          device_id=(x, y, 1 - c), device_id_type=MESH)
        cp.start()
        cp.wait()

    return _pcall(body, name=name, in_specs=[ANY], out_specs=ANY, out_shape=SDS(shard.shape, F32),
                  scratch=[pltpu.SemaphoreType.DMA, pltpu.SemaphoreType.DMA], aliases={0: 0})(shard)


def _reduce_late_start(gpk, after):
    rows = gpk.shape[1]
    HALF = rows // 2
    assert HALF % 16 == 0

    def body(in_ref, land_ref, after_ref, send_sems, recv_sems, in_thru, land_thru, token):
        x, y, c, chips = _place()
        me = 4 * x + 2 * y + c
        peers = [(x, y, 1 - c)] + [(cx, cy, c) for cx, cy in chips] + [(cx, cy, 1 - c) for cx, cy in chips]
        for k, (px, py, pc) in enumerate(peers):
            src = in_ref.at[2 * px + py, pl.ds(pl.multiple_of(pc * HALF, 16), HALF)]
            pltpu.make_async_remote_copy(src_ref=src, dst_ref=land_ref.at[me], send_sem=send_sems.at[k], recv_sem=recv_sems.at[k],
                                         device_id=(px, py, pc), device_id_type=MESH).start()
        token[...] = jnp.zeros_like(token)

    return pl.pallas_call(
        body, name="reduce_late_start",
        out_shape=(pltpu.SemaphoreType.DMA((7,)), pltpu.SemaphoreType.DMA((7,)), pltpu.HBM(gpk.shape, gpk.dtype),
                   pltpu.HBM((N_DEV, HALF, D), gpk.dtype), SDS((8, LANES), F32)),
        in_specs=(HBM, HBM, ANY), out_specs=(SEM, SEM, HBM, HBM, pl.BlockSpec(memory_space=pltpu.VMEM)),
        input_output_aliases={0: 2, 1: 3}, compiler_params=pltpu.CompilerParams(has_side_effects=DATAFLOW),
    )(_in_hbm(gpk), _in_hbm(lax.empty((N_DEV, HALF, D), gpk.dtype)), after)


def _reduce_late_wait(send_sems, recv_sems, in_thru, land_thru, after):
    def body(in_ref, land_ref, send_sems, recv_sems, after_ref, in_out, got_ref):
        x, y, c, chips = _place()
        peers = [(x, y, 1 - c)] + [(cx, cy, c) for cx, cy in chips] + [(cx, cy, 1 - c) for cx, cy in chips]
        for k, (px, py, pc) in enumerate(peers):
            cp = pltpu.make_async_remote_copy(src_ref=land_ref.at[0], dst_ref=land_ref.at[4 * px + 2 * py + pc],
                                              send_sem=send_sems.at[k], recv_sem=recv_sems.at[k],
                                              device_id=(px, py, pc), device_id_type=MESH)
            cp.wait_send()
            cp.wait_recv()

    return pl.pallas_call(
        body, name="reduce_late_wait",
        out_shape=(pltpu.HBM(in_thru.shape, in_thru.dtype), pltpu.HBM(land_thru.shape, land_thru.dtype)),
        in_specs=(HBM, HBM, SEM, SEM, ANY), out_specs=(HBM, HBM), input_output_aliases={0: 0, 1: 1},
        compiler_params=pltpu.CompilerParams(has_side_effects=DATAFLOW),
    )(in_thru, land_thru, send_sems, recv_sems, after)


def _reduce_late_add(didx, gpk, parts):
    HALF = parts.shape[1]
    th = HALF // 4
    nh = HALF // th
    assert th % 16 == 0

    def body(d_ref, own_ref, p_ref, o_ref):
        for me in range(N_DEV):
            @pl.when(d_ref[0] == me)
            def _(me=me):
                t = [(own_ref[0] if j == me else p_ref[j]).astype(F32) for j in range(N_DEV)]
                o_ref[...] = ((((((t[0] + t[1]) + t[2]) + t[3]) + t[4]) + t[5]) + t[6]) + t[7]

    gs = pltpu.PrefetchScalarGridSpec(
        num_scalar_prefetch=1, grid=(nh,),
        in_specs=[pl.BlockSpec((1, th, D), lambda i, d: (d[1], d[2] * nh + i, 0)), pl.BlockSpec((N_DEV, th, D), lambda i, d: (0, i, 0))],
        out_specs=pl.BlockSpec((th, D), lambda i, d: (d[2] * nh + i, 0)))
    return pl.pallas_call(body, name="reduce_late_add", grid_spec=gs, out_shape=SDS((2 * HALF, D), F32),
                          compiler_params=pltpu.CompilerParams(dimension_semantics=("parallel",),
                                                               vmem_limit_bytes=48 << 20))(didx, gpk, parts)


def _pack_early(b, dtype):
    lanes = lambda a: jnp.pad(a.astype(dtype), ((0, 0), (0, D - a.shape[1])))
    pair = jnp.concatenate([b["w_q_b"].astype(dtype), b["w_ple"].astype(dtype), jnp.zeros((256, D - 640), dtype)], axis=1)
    return jnp.concatenate([lanes(b["w_in"]), pair, lanes(b["w_kv_b"])], axis=0)


def _pack_late(b, dtype):
    return jnp.concatenate([b[n].astype(dtype) for n in ("w_mla_up", "w_swa_up", "w_out", "w_ple_gate", "w_mlp_up", "w_mlp_down")],
                           axis=0)


def _unpack_shards(pk, which):
    return {n: pk[PACK_AT[n][1]:PACK_AT[n][1] + r, PACK_AT[n][2]:PACK_AT[n][2] + c] for n, r, c in BIG if PACK_AT[n][0] == which}


def _full_weights(gathered, which):
    per_chip = [_unpack_shards(gathered[j], which) for j in range(N_CHIPS)]
    out = {}
    for n in per_chip[0]:
        shards = [pc[n] for pc in per_chip]
        if n == "w_in":
            out["w_in_p"] = _w_in_internal(shards)
        else:
            out[n] = jnp.concatenate(shards, axis=1 if n in COL_SHARDED else 0)
    return out


def _split_full_grads(grads, pack, dtype):
    shard = {n: (r, c) for n, r, c in BIG}
    chunks = []
    for j in range(N_CHIPS):
        blocks = {}
        for n, g in grads.items():
            if n == "w_in_p":
                blocks["w_in"] = _w_in_grad_shard(g, j)
                continue
            r, c = shard[n]
            blocks[n] = g[:, j * c:(j + 1) * c] if n in COL_SHARDED else g[j * r:(j + 1) * r]
        chunks.append(pack(blocks, dtype))
    return jnp.stack(chunks)


W_IN_SHARD = 936
W_IN_SEGMENTS = ((0, 256, (3072,)), (256, 384, (3840,)), (384, 416, (4032,)), (416, 1440, (0,)), (1440, 1504, (3328, 3392)),
                 (1504, 1568, (3456, 3520)), (1568, 1632, (3584, 3648)), (1632, 1696, (3712, 3776)), (1696, 3744, (1024,)))


def _w_in_internal(shards):
    def cols(a, b):
        out = []
        for j, s in enumerate(shards):
            lo, hi = max(a, W_IN_SHARD * j), min(b, W_IN_SHARD * (j + 1))
            if lo < hi:
                out.append(s[:, lo - W_IN_SHARD * j:hi - W_IN_SHARD * j])
        return out

    pieces = {}
    for a, b, places in W_IN_SEGMENTS:
        for at in places:
            pieces[at] = cols(a, b)
    zeros = lambda n: [jnp.zeros((D, n), shards[0].dtype)]
    pieces[3968] = zeros(64)
    pieces[4064] = zeros(32)
    return jnp.concatenate([piece for at in sorted(pieces) for piece in pieces[at]], axis=1)


def _w_in_grad_shard(g, j):
    def internal(a, b):
        out = []
        while a < b:
            end = min(b, (a // D + 1) * D)
            out.append(g[a // D][:, a % D:a % D + end - a])
            a = end
        return out

    out = []
    for a, b, places in W_IN_SEGMENTS:
        lo, hi = max(a, W_IN_SHARD * j), min(b, W_IN_SHARD * (j + 1))
        if lo < hi:
            parts = [internal(at + lo - a, at + hi - a) for at in places]
            if len(parts) == 1:
                out += parts[0]
            else:
                assert len(parts[0]) == len(parts[1]) == 1
                out.append(parts[0][0] + parts[1][0])
    return jnp.concatenate(out, axis=1)


def _local_step(x, p, tgt, w, small, late_weights, late_grads_out):
    T = x.shape[0]
    tm = 256
    tb = 256
    w_in_p = w["w_in_p"]
    wqb = jnp.pad(w["w_q_b"].reshape(Q_LORA, MLA_HEADS, 96), ((0, 0), (0, 0), (0, 32))).reshape(Q_LORA, 2048)
    wkv = w["w_kv_b"].reshape(KV_LORA, MLA_HEADS, 128)
    wkn = jnp.pad(wkv[:, :, :64], ((0, 0), (0, 0), (0, 64))).reshape(KV_LORA, 2048)
    wv = wkv[:, :, 64:].reshape(KV_LORA, 1024)
    tab_m = _rope_tables(T, "mla")
    tab_s = _rope_tables(T, "swa")
    g1, gq, gkv, sinks = small["g_mix_pre"], small["g_q_a"], small["g_kv_a"], small["sinks"]
    g2, g3, g4, g5 = small["g_mix_post"], small["g_mlp_pre"], small["g_mlp_post"], small["g_ple"]
    sink_vec = sinks.reshape(SWA_HEADS)

    z, h1 = _fwd_in(x, g1, w_in_p, tm)
    qn, kvn, km, vm, qt, kt, vt, qs, ks, vs = _fwd_qkv(z, gq, gkv, wqb, wkn, wv, tab_m, tab_s, tb)
    om, lse_m = _mla_fwd(qt, km, vt, tb)
    os_, lse_s = _swa_fwd(sink_vec, qs, ks, vs)
    w = {**w, **late_weights((om, os_))}
    y, yo, au, bu, x1 = _fwd_mix(om, os_, z, x, w["w_mla_up"], w["w_swa_up"], w["w_out"], g2, tm)
    h2, u = _fwd_mlp_up(x1, g3, w["w_mlp_up"], tm)
    d, x2 = _fwd_mlp_down(u, w["w_mlp_down"], x1, g4, tm)
    loss, dx2, dgt, de0, dg5 = _ple_fwd_bwd(p, x2, tgt, w["w_ple"], g5, w["w_ple_gate"], tm)

    dd, da, dg4 = _bwd_mlp_down(dx2, d, g4, w["w_mlp_down"], u, tm)
    dx1, dg3 = _bwd_mlp_up(da, w["w_mlp_up"], x1, g3, dx2, tm)
    dyo, dg2, dau, dbu, dga, dgb, dos, delta_m, dom_t = _bwd_mix(dx1, yo, g2, w["w_out"], z, au, bu, w["w_mla_up"],
                                                                w["w_swa_up"], om, tb)
    gpk_late = lax.empty((N_CHIPS, PACK_ROWS["late"], D), BF16)
    for weight, a_, g_ in (("w_mla_up", om, dau), ("w_swa_up", os_, dbu), ("w_out", y, dyo), ("w_ple_gate", x2, dgt),
                           ("w_mlp_up", h2, da), ("w_mlp_down", u, dd)):
        gpk_late = _wgrad(a_, g_, "wgrad_" + weight[2:], into=(gpk_late, weight))
    token = late_grads_out(gpk_late)
    delta_m = delta_m + token[0, 0]
    dqm, dkm, dvm = _mla_bwd(qt, km, kt, vm, dom_t, lse_m, delta_m, tb)
    dqs, dkc, dkp, dvc, dvp, dsink = _swa_bwd(sink_vec, qs, ks, vs, dos, os_, lse_s)
    dqb, dknb, dvb, dsq, drest, dgq, dgkv = _bwd_qkv(dqm, dkm, dvm, dqs, dkc, dkp, dvc, dvp, z, gq, gkv, wqb, wkn, wv,
                                                      tab_m, tab_s)
    gx, dg1 = _bwd_in(dsq, dga, dgb, drest, w_in_p, x, g1, dx1, tm)

    g_in_p = [_wgrad(h1, dsq, "wgrad_in_sq"), _wgrad(h1, dga, "wgrad_in_ga"), _wgrad(h1, dgb, "wgrad_in_gb"),
              _wgrad(h1, drest, "wgrad_in_rest")]
    g_qb_p = _wgrad_t(dqb, qn, "wgrad_q_b").T
    g_kn_p = _wgrad_t(dknb, kvn, "wgrad_kv_b_nope").T
    g_v_p = _wgrad_t(dvb, kvn, "wgrad_kv_b_v").T
    grads = {
        "w_in_p": g_in_p,
        "w_q_b": g_qb_p.reshape(Q_LORA, MLA_HEADS, 128)[:, :, :96].reshape(Q_LORA, 1536),
        "w_kv_b": jnp.concatenate([g_kn_p.reshape(KV_LORA, MLA_HEADS, 128)[:, :, :64], g_v_p.reshape(KV_LORA, MLA_HEADS, 64)],
                                  axis=2).reshape(KV_LORA, 2048),
        "w_ple": _wgrad(p, de0, "wgrad_ple"),
    }
    small_grads = {"g_mix_pre": dg1, "g_q_a": dgq.reshape(1, Q_LORA), "g_kv_a": dgkv.reshape(1, KV_LORA), "sinks": dsink[0:1, 0:SWA_HEADS], "g_mix_post": dg2,
                   "g_mlp_pre": dg3, "g_mlp_post": dg4, "g_ple": dg5}
    return loss, gx, grads, small_grads


def _pack_small(vals, fill, scalar=None):
    wide = [vals[n] for n, k in SMALL if k == D]
    narrow = [vals[n] for n, k in SMALL if k != D]
    used = sum(k for _, k in SMALL if k != D)
    last = jnp.concatenate(narrow + [jnp.full((1, D - used), fill, F32)], axis=1)
    rest = jnp.full((2, D), fill, F32)
    if scalar is not None:
        rest = jnp.concatenate([jnp.concatenate([scalar, rest[0:1, 1:]], axis=1), rest[1:2]], axis=0)
    return jnp.concatenate(wide + [last, rest], axis=0)


def _unpack_small(pk):
    out, row, off = {}, 0, 0
    for n, k in SMALL:
        if k == D:
            out[n] = pk[row:row + 1]
            row += 1
    for n, k in SMALL:
        if k != D:
            out[n] = pk[5:6, off:off + k]
            off += k
    return out


def kernel(x, p, g_mix_pre, w_in, g_q_a, w_q_b, g_kv_a, w_kv_b, sinks, w_mla_up, w_swa_up, w_out, g_mix_post, g_mlp_pre, w_mlp_up, w_mlp_down, g_mlp_post, w_ple, g_ple, w_ple_gate, loss_target, m_g_mix_pre, m_w_in, m_g_q_a, m_w_q_b, m_g_kv_a, m_w_kv_b, m_sinks, m_w_mla_up, m_w_swa_up, m_w_out, m_g_mix_post, m_g_mlp_pre, m_w_mlp_up, m_w_mlp_down, m_g_mlp_post, m_w_ple, m_g_ple, m_w_ple_gate, v_g_mix_pre, v_w_in, v_g_q_a, v_w_q_b, v_g_kv_a, v_w_kv_b, v_sinks, v_w_mla_up, v_w_swa_up, v_w_out, v_g_mix_post, v_g_mlp_pre, v_w_mlp_up, v_w_mlp_down, v_g_mlp_post, v_w_ple, v_g_ple, v_w_ple_gate):
    given = dict(locals())
    big_w = {n: given[n][0] for n, _, _ in BIG}
    small_w = {n: given[n] for n, _ in SMALL}
    small_m = {n: given["m_" + n] for n, _ in SMALL}
    small_v = {n: given["v_" + n] for n, _ in SMALL}

    core = lax.axis_index("c")
    chip = 2 * lax.axis_index("x") + lax.axis_index("y")
    core_i = core.astype(jnp.int32).reshape(1)
    dev_i = jnp.stack([2 * chip + core, chip, core]).astype(jnp.int32)

    own_early = _pack_early(big_w, BF16)
    own_late = _pack_late(big_w, BF16)
    got_early = _all_gather(own_early)
    late_flight = _gather_late_start(own_late, got_early)
    weights = _full_weights(got_early, "early")
    step_small = {**small_w, "g_mix_pre": small_w["g_mix_pre"] + late_flight[4][0, 0]}

    def late_weights(after):
        return _full_weights(_gather_late_wait(*late_flight[:4], after), "late")

    flight = {}

    def late_grads_out(gpk_late):
        flight["late"] = _reduce_late_start(gpk_late, dev_i)
        return flight["late"][4]

    loss_blk, gx, grads, small_grads = _local_step(x[0], p[0, 0], loss_target[0], weights, step_small, late_weights,
                                                   late_grads_out)

    gpk = _split_full_grads(grads, _pack_early, BF16)
    got = _rs_sibling(gpk)
    part = _rs_add_sibling(core_i, gpk, got)
    small_own = _pack_small(small_grads, 0.0, loss_blk[0:1, 0:1])
    early_flight = _rs_chips_start(part, small_own, dev_i)

    out_g, out_d, out_m, out_v = {}, {}, {}, {}
    gpk_late, parts_late = _reduce_late_wait(*flight["late"][:4], early_flight[6])
    joined_late = _rs_join(_reduce_late_add(dev_i, gpk_late, parts_late), "rs_join_late")
    for n, _, _ in BIG:
        if PACK_AT[n][0] == "late":
            out_g[n], out_d[n], out_m[n], out_v[n] = _adamw(given[n], joined_late, given["m_" + n], given["v_" + n], n)

    part, parts, small_parts = _rs_chips_wait(*early_flight[:6], [out_d[n] for n in out_d])
    joined_early = _rs_join(_rs_add_chips(dev_i[1:3], part, parts), "rs_join_early")
    for n, _, _ in BIG:
        if PACK_AT[n][0] == "early":
            t = given[n].shape[2] % LANES != 0
            wmv = [jnp.swapaxes(a, 1, 2) if t else a for a in (given[n], given["m_" + n], given["v_" + n])]
            res = _adamw(wmv[0], joined_early, wmv[1], wmv[2], n, transposed=t)
            out_g[n], out_d[n], out_m[n], out_v[n] = [jnp.swapaxes(a, 1, 2) if t else a for a in res]

    mine = (lax.broadcasted_iota(jnp.int32, (N_DEV, 1, 1), 0) == dev_i[0])
    g_small_pk, d_small_pk, m_small_pk, v_small_pk = _adamw_small(
        _pack_small(small_w, 0.0), jnp.where(mine, small_own[None], small_parts), _pack_small(small_m, 0.0),
        _pack_small(small_v, 1.0))
    loss = g_small_pk[6, 0]
    for out, pk in ((out_g, g_small_pk), (out_d, d_small_pk), (out_m, m_small_pk), (out_v, v_small_pk)):
        out.update(_unpack_small(pk))
    order = ["g_mix_pre", "w_in", "g_q_a", "w_q_b", "g_kv_a", "w_kv_b", "sinks", "w_mla_up", "w_swa_up", "w_out", "g_mix_post",
             "g_mlp_pre", "w_mlp_up", "w_mlp_down", "g_mlp_post", "w_ple", "g_ple", "w_ple_gate"]
    return (loss, gx[None], *[out_g[n] for n in order], *[out_d[n] for n in order], *[out_m[n] for n in order],
            *[out_v[n] for n in order])
```

```python
import math

import jax
import jax.numpy as jnp
import numpy as np
from jax import lax
from jax.experimental import pallas as pl
from jax.experimental.pallas import tpu as pltpu

F32 = jnp.float32
BF16 = jnp.bfloat16
SDS = jax.ShapeDtypeStruct

D = 1024
D_FF = 4096
PLE = 256
Q_LORA = 256
KV_LORA = 128
MLA_HEADS = 16
MLA_NOPE = 64
MLA_ROPE = 32
SWA_HEADS = 16
SWA_HD = 64
WINDOW = 128
ROPE_THETA = 10000.0
EPS = 1e-6
NEG = -1e30
NZ = 4096
MLA_SCALE = (MLA_NOPE + MLA_ROPE) ** -0.5
LOG2_E = math.log2(math.e)
MLA_LOG2_SCALE = MLA_SCALE * LOG2_E
SWA_SCALE = SWA_HD ** -0.5

ADAM_LR = 0.001
ADAM_B1 = 0.9
ADAM_B2 = 0.999
ADAM_EPS = 1e-08
ADAM_WD = 0.01
ADAM_STEP = 10

LANES = 128
ATT_COLS = 128
VT_ROWS = 80
N_CHIPS = 4
N_DEV = 8
MESH = pl.DeviceIdType.MESH

NT = (((1,), (1,)), ((), ()))
TN = (((0,), (0,)), ((), ()))

BIG = (("w_in", 1024, 936), ("w_q_b", 256, 384), ("w_kv_b", 128, 512), ("w_mla_up", 256, 1024),
       ("w_swa_up", 256, 1024), ("w_out", 256, 1024), ("w_mlp_up", 1024, 1024), ("w_mlp_down", 1024, 1024),
       ("w_ple", 256, 256), ("w_ple_gate", 256, 1024))
COL_SHARDED = ("w_in", "w_q_b", "w_kv_b", "w_mlp_up", "w_ple")
PACK_AT = {"w_in": ("early", 0, 0), "w_q_b": ("early", 1024, 0), "w_ple": ("early", 1024, 384), "w_kv_b": ("early", 1280, 0),
           "w_mla_up": ("late", 0, 0), "w_swa_up": ("late", 256, 0), "w_out": ("late", 512, 0), "w_ple_gate": ("late", 768, 0),
           "w_mlp_up": ("late", 1024, 0), "w_mlp_down": ("late", 2048, 0)}
PACK_ROWS = {"early": 1408, "late": 3072}
SMALL = (("g_mix_pre", 1024), ("g_q_a", 256), ("g_kv_a", 128), ("sinks", 16), ("g_mix_post", 1024),
         ("g_mlp_pre", 1024), ("g_mlp_post", 1024), ("g_ple", 1024))


def _dot(a, b):
    return jnp.dot(a, b, preferred_element_type=F32)


def _dot_nt(a, b):
    return lax.dot_general(a, b, NT, preferred_element_type=F32)


def _dot_tn(a, b):
    return lax.dot_general(a, b, TN, preferred_element_type=F32)


def _pcall(body, *, name, out_shape, grid=(), in_specs=None, out_specs=None, scratch=(), sem=None, vmem_mb=48, aliases=None):
    params = dict(vmem_limit_bytes=vmem_mb << 20)
    if sem is not None:
        params["dimension_semantics"] = sem
    return pl.pallas_call(body, name=name, grid=grid, in_specs=in_specs, out_specs=out_specs, out_shape=out_shape,
                          scratch_shapes=list(scratch), input_output_aliases=aliases or {},
                          compiler_params=pltpu.CompilerParams(**params))


def _rows(tm, n, col=0):
    return pl.BlockSpec((tm, n), lambda i: (i, col))


def _full(shape):
    return pl.BlockSpec(shape, lambda i: (0,) * len(shape))


def _rms(x, g):
    r = lax.rsqrt(jnp.mean(x * x, axis=-1, keepdims=True) + EPS)
    return x * r * g


def _rms_bwd(dy, x, g):
    r = lax.rsqrt(jnp.mean(x * x, axis=-1, keepdims=True) + EPS)
    xn = x * r
    dn = dy * g
    dx = r * (dn - xn * jnp.mean(dn * xn, axis=-1, keepdims=True))
    return dx, jnp.sum(dy * xn, axis=0, keepdims=True)


def _sigmoid(x):
    return 1.0 / (1.0 + jnp.exp(-x))


def _rope(x, c, a, b, half):
    return x * c + pltpu.roll(x, LANES - half, 1) * a + pltpu.roll(x, half, 1) * b


def _rope_tables(T, kind):
    lane = np.arange(LANES)
    if kind == "mla":
        half = MLA_ROPE // 2
        rel = lane - MLA_NOPE
        on = (rel >= 0) & (rel < MLA_ROPE)
        d = MLA_ROPE
    else:
        half = SWA_HD // 2
        rel = lane % SWA_HD
        on = np.ones((LANES,), bool)
        d = SWA_HD
    first = on & (rel < half)
    second = on & (rel >= half)
    f = np.where(first, rel, rel - half).astype(np.float32)
    inv = np.exp(np.float32(-math.log(ROPE_THETA)) * f * np.float32(2.0 / d)).astype(np.float32)
    ang = (np.arange(T, dtype=np.float32)[:, None] * inv[None, :]).astype(np.float64)
    cos, sin = np.cos(ang).astype(np.float32), np.sin(ang).astype(np.float32)
    c = np.where(on[None], cos, np.float32(1.0))
    a = np.where(first[None], -sin, np.float32(0.0))
    b = np.where(second[None], sin, np.float32(0.0))
    return c, a, b


def _fwd_in(x, g1, w_in_p, tm):
    T = x.shape[0]

    def body(x_ref, g_ref, w_ref, z_ref, h_ref):
        h = _rms(x_ref[...], g_ref[...]).astype(BF16)
        h_ref[...] = h
        z_ref[...] = _dot(h, w_ref[...])

    return _pcall(body, name="fwd_in", grid=(T // tm,),
                  in_specs=[_rows(tm, D), _full((1, D)), _full((D, NZ))],
                  out_specs=[_rows(tm, NZ), _rows(tm, D)],
                  out_shape=[SDS((T, NZ), F32), SDS((T, D), BF16)], sem=("parallel",))(x, g1, w_in_p)


def _fwd_qkv(z, gq, gkv, wqb, wkn, wv, tab_m, tab_s, tm):
    T = z.shape[0]
    wqb_t, wkn_t, wv_t = wqb.T, wkn.T, wv.T
    tab_mt = [t.T for t in tab_m]

    def body(qa_ref, sq_ref, skd_ref, svd_ref, kva_ref, kr_ref, gq_ref, gkv_ref, wkn_ref, wv_ref, wqbt_ref, wknt_ref, wvt_ref,
             cm_ref, am_ref, bm_ref, cmt_ref, amt_ref, bmt_ref, cs_ref, as_ref, bs_ref,
             qn_ref, kvn_ref, km_ref, vm_ref, qt_ref, kt_ref, vt_ref, qs_ref, ks_ref, vs_ref):
        qn = _rms(qa_ref[...], gq_ref[...])
        qn_ref[...] = qn.astype(BF16)
        kvn = _rms(kva_ref[...], gkv_ref[...])
        kvn_b = kvn.astype(BF16)
        kvn_ref[...] = kvn_b
        qn_t = qn.T.astype(BF16)
        kvn_t = kvn.T.astype(BF16)
        cm, am, bm = cm_ref[...], am_ref[...], bm_ref[...]
        cmt, amt, bmt = cmt_ref[...], amt_ref[...], bmt_ref[...]
        cs, as_, bs = cs_ref[...], as_ref[...], bs_ref[...]
        k_rope = _rope(kr_ref[...], cm, am, bm, MLA_ROPE // 2)
        k_rope_t = k_rope.T
        half = MLA_ROPE // 2
        vm_ref[...] = _dot(kvn_b, wv_ref[...]).astype(BF16)
        km_all = _dot(kvn_b, wkn_ref[...])
        v_t = _dot(wvt_ref[...], kvn_t)
        q_t = _dot(wqbt_ref[...], qn_t)
        k_t = _dot(wknt_ref[...], kvn_t)
        ones_row = jnp.where(lax.broadcasted_iota(jnp.int32, (64, tm), 0) == 0, 1.0, 0.0)
        for h in range(MLA_HEADS):
            sl = slice(LANES * h, LANES * (h + 1))
            vt_ref[0, sl, :] = jnp.concatenate([v_t[64 * h:64 * (h + 1)], ones_row], axis=0).astype(BF16)
            qh = q_t[sl]
            qt_ref[0, sl, :] = (qh * cmt + pltpu.roll(qh, LANES - half, 0) * amt + pltpu.roll(qh, half, 0) * bmt).astype(BF16)
            km_ref[:, sl] = (km_all[:, sl] + k_rope).astype(BF16)
            kt_ref[0, sl, :] = (k_t[sl] + k_rope_t).astype(BF16)
        for j in range(D // LANES):
            sl = slice(LANES * j, LANES * (j + 1))
            qs_ref[:, sl] = _rope(sq_ref[:, sl], cs, as_, bs, SWA_HD // 2).astype(BF16)
        for j in range(2):
            sl = slice(LANES * j, LANES * (j + 1))
            ks_ref[:, sl] = _rope(skd_ref[:, sl], cs, as_, bs, SWA_HD // 2).astype(BF16)
        vs_ref[...] = svd_ref[...].astype(BF16)

    tab = [_rows(tm, LANES)] * 3
    tab_t = [pl.BlockSpec((LANES, tm), lambda i: (0, i))] * 3
    return _pcall(body, name="fwd_qkv", grid=(T // tm,),
                  in_specs=[_rows(tm, 256, 12), _rows(tm, 1024, 0), _rows(tm, 256, 13), _rows(tm, 256, 14),
                            _rows(tm, 128, 30), _rows(tm, 128, 31), _full((1, Q_LORA)), _full((1, KV_LORA)),
                            _full((KV_LORA, 2048)), _full((KV_LORA, 1024)), _full((2048, Q_LORA)), _full((2048, KV_LORA)),
                            _full((1024, KV_LORA))] + tab + tab_t + tab,
                  out_specs=[_rows(tm, Q_LORA), _rows(tm, KV_LORA), _rows(tm, 2048), _rows(tm, 1024),
                             pl.BlockSpec((1, 2048, tm), lambda i: (i, 0, 0)), pl.BlockSpec((1, 2048, tm), lambda i: (i, 0, 0)),
                             pl.BlockSpec((1, 2048, tm), lambda i: (i, 0, 0)),
                             _rows(tm, 1024), _rows(tm, 256), _rows(tm, 256)],
                  out_shape=[SDS((T, Q_LORA), BF16), SDS((T, KV_LORA), BF16), SDS((T, 2048), BF16),
                             SDS((T, 1024), BF16), SDS((T // tm, 2048, tm), BF16), SDS((T // tm, 2048, tm), BF16),
                             SDS((T // tm, 2048, tm), BF16),
                             SDS((T, 1024), BF16), SDS((T, 256), BF16), SDS((T, 256), BF16)],
                  sem=("parallel",))(z, z, z, z, z, z, gq, gkv, wkn, wv, wqb_t, wkn_t, wv_t, *tab_m, *tab_mt, *tab_s)


def _mla_fwd(qt, km, vt, tb):
    T = km.shape[0]
    nb = T // tb
    cc = ATT_COLS

    per = 2 if nb % 2 == 0 else 1

    def body(q_ref, k_ref, vt_ref, o_ref, l_ref, s_ref, p_ref, al_ref, m_ref, acc_ref):
        for blk in range(per):
            one_block(per * pl.program_id(1) + blk, blk, q_ref, k_ref, vt_ref, o_ref, l_ref, s_ref, p_ref, al_ref, m_ref, acc_ref)

    def one_block(i, blk, q_ref, k_ref, vt_ref, o_ref, l_ref, s_ref, p_ref, al_ref, m_ref, acc_ref):
        m_ref[...] = jnp.full(m_ref.shape, NEG, F32)
        acc_ref[...] = jnp.zeros_like(acc_ref)
        p_ref[1] = jnp.zeros(p_ref.shape[1:], BF16)
        al_ref[1] = jnp.ones(al_ref.shape[1:], F32)
        key = lax.broadcasted_iota(jnp.int32, (tb, cc), 0)
        qry = lax.broadcasted_iota(jnp.int32, (tb, cc), 1)

        def scores(j, slot):
            off = pl.multiple_of(j * tb, tb)
            for hh in range(2):
                sl = slice(LANES * hh, LANES * (hh + 1))
                s_ref[slot, hh] = _dot(k_ref[pl.ds(off, tb), sl], q_ref[blk, sl, :])

        def softmax(slot, diagonal):
            chains = [(hh, slice(cc * c, cc * (c + 1)), c) for hh in range(2) for c in range(tb // cc)]

            def scaled(hh, cols, c):
                t = s_ref[slot, hh, :, cols] * MLA_LOG2_SCALE
                return jnp.where(key <= qry + cc * c, t, NEG) if diagonal else t

            tops = []
            for hh, cols, c in chains:
                if diagonal:
                    top = jnp.max(scaled(hh, cols, c), axis=0, keepdims=True)
                else:
                    top = jnp.max(s_ref[slot, hh, :, cols], axis=0, keepdims=True) * MLA_LOG2_SCALE
                m_old = m_ref[hh, :, cols]
                mn = jnp.maximum(m_old, top)
                m_ref[hh, :, cols] = mn
                al_ref[slot, hh, :, cols] = jnp.exp2(m_old - mn)
                tops.append(mn)
            for (hh, cols, c), mn in zip(chains, tops):
                p_ref[slot, hh, :, cols] = jnp.exp2(scaled(hh, cols, c) - mn).astype(BF16)

        def accumulate(j, slot):
            for hh in range(2):
                acc_ref[hh] = al_ref[slot, hh] * acc_ref[hh] + _dot(vt_ref[j, LANES * hh:LANES * hh + VT_ROWS, :], p_ref[slot, hh])

        def step(t, carry):
            scores(2 * t + 1, 1)
            accumulate(jnp.maximum(2 * t - 1, 0), 1)
            softmax(0, False)
            scores(2 * t + 2, 0)
            accumulate(2 * t, 0)
            softmax(1, False)
            return carry

        scores(0, 0)
        lax.fori_loop(0, i // 2, step, 0)

        @pl.when(i % 2 == 1)
        def _():
            scores(i, 1)
            accumulate(jnp.maximum(i - 2, 0), 1)
            softmax(0, False)
            accumulate(i - 1, 0)
            softmax(1, True)
            accumulate(i, 1)

        @pl.when(i % 2 == 0)
        def _():
            accumulate(jnp.maximum(i - 1, 0), 1)
            softmax(0, True)
            accumulate(i, 0)
        den = [acc_ref[hh, 64:65, :] for hh in range(2)]
        o_ref[tb * blk:tb * (blk + 1), :] = jnp.concatenate([acc_ref[hh, 0:64, :] / den[hh] for hh in range(2)], axis=0).T
        sub = lax.broadcasted_iota(jnp.int32, (8, tb), 0)
        lse = [m_ref[hh] + jnp.log(den[hh]) * LOG2_E for hh in range(2)]
        l_ref[0, blk] = jnp.where(sub == 0, lse[0], jnp.where(sub == 1, lse[1], 0.0))

    return _pcall(body, name="mla_fwd", grid=(MLA_HEADS // 2, nb // per),
                  in_specs=[pl.BlockSpec((per, 256, tb), lambda p, i: (i, p, 0)), pl.BlockSpec((T, 256), lambda p, i: (0, p)),
                            pl.BlockSpec((nb, 2 * LANES, tb), lambda p, i: (0, p, 0))],
                  out_specs=[pl.BlockSpec((per * tb, LANES), lambda p, i: (i, p)),
                             pl.BlockSpec((1, per, 8, tb), lambda p, i: (p, i, 0, 0))],
                  out_shape=[SDS((T, D), F32), SDS((MLA_HEADS // 2, nb, 8, tb), F32)],
                  scratch=[pltpu.VMEM((2, 2, tb, tb), F32), pltpu.VMEM((2, 2, tb, tb), BF16), pltpu.VMEM((2, 2, 1, tb), F32),
                           pltpu.VMEM((2, 1, tb), F32), pltpu.VMEM((2, VT_ROWS, tb), F32)],
                  sem=("parallel", "arbitrary"))(qt, km, vt)


def _swa_mask(n):
    row = lax.broadcasted_iota(jnp.int32, (WINDOW, 2 * WINDOW), 0)
    col = lax.broadcasted_iota(jnp.int32, (WINDOW, 2 * WINDOW), 1)
    rel = row - col + WINDOW
    return (rel >= 0) & (rel < WINDOW) & ((col >= WINDOW) | (n > 0))


def _swa_specs(T):
    nb = T // WINDOW
    cur = lambda w: pl.BlockSpec((WINDOW, w), lambda n: (n, 0))
    prev = lambda w: pl.BlockSpec((WINDOW, w), lambda n: (jnp.maximum(n - 1, 0), 0))
    return nb, cur, prev


def _swa_fwd(sinks, qs, ks, vs):
    T = qs.shape[0]
    nb, cur, prev = _swa_specs(T)

    def body(sink_ref, q_ref, kc_ref, kp_ref, vc_ref, vp_ref, o_ref, l_ref, kb_ref, vb_ref, s_ref, p_ref):
        n = pl.program_id(0)
        mask = _swa_mask(n)
        lo = lax.broadcasted_iota(jnp.int32, (WINDOW, LANES), 1) < 64
        hi = jnp.logical_not(lo)
        for g in range(2):
            gs = slice(LANES * g, LANES * (g + 1))
            kb_ref[g] = jnp.concatenate([kp_ref[:, gs], kc_ref[:, gs]], axis=0)
            vb_ref[g] = jnp.concatenate([vp_ref[:, gs], vc_ref[:, gs]], axis=0)
        for h in range(SWA_HEADS):
            qp = q_ref[:, LANES * (h // 2):LANES * (h // 2 + 1)]
            qh = jnp.where(lo if h % 2 == 0 else hi, qp, jnp.zeros_like(qp))
            s_ref[h] = _dot_nt(qh, kb_ref[h // 8])
        for j in range(SWA_HEADS // 2):
            sl = slice(LANES * j, LANES * (j + 1))
            lses = []
            for h in (2 * j, 2 * j + 1):
                s = jnp.where(mask, s_ref[h] * SWA_SCALE, NEG)
                sk = sink_ref[h]
                m = jnp.maximum(jnp.max(s, axis=1, keepdims=True), sk)
                e = jnp.exp(s - m)
                den = jnp.sum(e, axis=1, keepdims=True) + jnp.exp(sk - m)
                p_ref[h] = (e / den).astype(BF16)
                lses.append(jnp.broadcast_to(m + jnp.log(den), (WINDOW, LANES)))
            l_ref[:, sl] = jnp.where(lo, lses[0], lses[1])
        for j in range(SWA_HEADS // 2):
            vb = vb_ref[j // 4]
            o_ref[:, LANES * j:LANES * (j + 1)] = jnp.where(lo, _dot(p_ref[2 * j], vb), _dot(p_ref[2 * j + 1], vb))

    return _pcall(body, name="swa_fwd", grid=(nb,),
                  in_specs=[pl.BlockSpec(memory_space=pltpu.SMEM), cur(D), cur(256), prev(256), cur(256), prev(256)],
                  out_specs=[cur(D), cur(D)], out_shape=[SDS((T, D), F32)] * 2,
                  scratch=[pltpu.VMEM((2, 2 * WINDOW, LANES), BF16), pltpu.VMEM((2, 2 * WINDOW, LANES), BF16),
                           pltpu.VMEM((SWA_HEADS, WINDOW, 2 * WINDOW), F32), pltpu.VMEM((SWA_HEADS, WINDOW, 2 * WINDOW), BF16)],
                  sem=("parallel",))(sinks, qs, ks, ks, vs, vs)


def _fwd_mix(om, os_, z, x, wmu, wsu, wo, g2, tm):
    T = x.shape[0]

    def body(om_ref, os_ref, ga_ref, gb_ref, x_ref, wmu_ref, wsu_ref, wo_ref, g2_ref,
             y_ref, yo_ref, au_ref, bu_ref, x1_ref):
        au = _dot(om_ref[...].astype(BF16), wmu_ref[...])
        bu = _dot(os_ref[...].astype(BF16), wsu_ref[...])
        au_ref[...] = au
        bu_ref[...] = bu
        y = (_sigmoid(ga_ref[...]) * au + _sigmoid(gb_ref[...]) * bu).astype(BF16)
        y_ref[...] = y
        yo = _dot(y, wo_ref[...])
        yo_ref[...] = yo
        x1_ref[...] = x_ref[...] + _rms(yo, g2_ref[...])

    r = _rows(tm, D)
    w = _full((D, D))
    return _pcall(body, name="fwd_mix", grid=(T // tm,),
                  in_specs=[r, r, _rows(tm, D, 1), _rows(tm, D, 2), r, w, w, w, _full((1, D))],
                  out_specs=[r] * 5,
                  out_shape=[SDS((T, D), BF16), SDS((T, D), F32), SDS((T, D), F32), SDS((T, D), F32), SDS((T, D), F32)],
                  sem=("parallel",))(om, os_, z, z, x, wmu, wsu, wo, g2)


def _fwd_mlp_up(x1, g3, w1, tm):
    T = x1.shape[0]

    def body(x_ref, g_ref, w_ref, h_ref, u_ref):
        h = _rms(x_ref[...], g_ref[...]).astype(BF16)
        h_ref[...] = h
        u_ref[...] = jnp.square(jnp.maximum(_dot(h, w_ref[...]), 0.0)).astype(BF16)

    return _pcall(body, name="fwd_mlp_up", grid=(T // tm,),
                  in_specs=[_rows(tm, D), _full((1, D)), _full((D, D_FF))],
                  out_specs=[_rows(tm, D), _rows(tm, D_FF)],
                  out_shape=[SDS((T, D), BF16), SDS((T, D_FF), BF16)],
                  sem=("parallel",))(x1, g3, w1)


def _fwd_mlp_down(u, w2, x1, g4, tm):
    T = x1.shape[0]

    def body(u_ref, w_ref, x_ref, g_ref, d_ref, x2_ref):
        d = _dot(u_ref[...], w_ref[...])
        d_ref[...] = d
        x2_ref[...] = x_ref[...] + _rms(d, g_ref[...])

    return _pcall(body, name="fwd_mlp_down", grid=(T // tm,),
                  in_specs=[_rows(tm, D_FF), _full((D_FF, D)), _rows(tm, D), _full((1, D))],
                  out_specs=[_rows(tm, D), _rows(tm, D)], out_shape=[SDS((T, D), F32)] * 2,
                  sem=("parallel",))(u, w2, x1, g4)


def _ple_fwd_bwd(p, x2, tgt, wple, g5, wpg, tm):
    T = x2.shape[0]

    def body(p_ref, x2_ref, t_ref, wple_ref, g5_ref, wpg_ref, loss_ref, dx2_ref, dgt_ref, de0_ref, dg5_ref):
        @pl.when(pl.program_id(0) == 0)
        def _():
            loss_ref[...] = jnp.zeros_like(loss_ref)
            dg5_ref[...] = jnp.zeros_like(dg5_ref)

        e0 = _dot(p_ref[...].astype(BF16), wple_ref[...])
        g5 = g5_ref[...]
        r = lax.rsqrt(jnp.mean(e0 * e0, axis=-1, keepdims=True) + EPS)
        en = e0 * r
        e = en * g5
        x2 = x2_ref[...]
        s = _sigmoid(_dot(x2.astype(BF16), wpg_ref[...]))
        diff = x2 + s * e - t_ref[...]
        sq = jnp.sum(jnp.sum(diff * diff, axis=1, keepdims=True), axis=0, keepdims=True)
        loss_ref[...] += jnp.broadcast_to(sq * (0.5 / D), loss_ref.shape)
        dx3 = diff * (1.0 / D)
        de = dx3 * s
        dgt = (dx3 * e * s * (1.0 - s)).astype(BF16)
        dgt_ref[...] = dgt
        dn = de * g5
        de0_ref[...] = (r * (dn - en * jnp.mean(dn * en, axis=-1, keepdims=True))).astype(BF16)
        dg5_ref[...] += jnp.sum(de * en, axis=0, keepdims=True)
        dx2_ref[...] = dx3 + _dot_nt(dgt, wpg_ref[...])

    r = _rows(tm, D)
    return _pcall(body, name="ple_fwd_bwd", grid=(T // tm,),
                  in_specs=[_rows(tm, PLE), r, r, _full((PLE, D)), _full((1, D)), _full((D, D))],
                  out_specs=[_full((8, LANES)), r, r, r, _full((1, D))],
                  out_shape=[SDS((8, LANES), F32), SDS((T, D), F32), SDS((T, D), BF16), SDS((T, D), BF16), SDS((1, D), F32)],
                  sem=("arbitrary",))(p, x2, tgt, wple, g5, wpg)


def _bwd_mlp_down(dx2, d, g4, w2, u, tm):
    T = dx2.shape[0]

    def body(dx_ref, d_ref, g_ref, w_ref, u_ref, dd_ref, da_ref, dg_ref):
        @pl.when(pl.program_id(0) == 0)
        def _():
            dg_ref[...] = jnp.zeros_like(dg_ref)

        dd, dg = _rms_bwd(dx_ref[...], d_ref[...], g_ref[...])
        dg_ref[...] += dg
        ddb = dd.astype(BF16)
        dd_ref[...] = ddb
        du = _dot_nt(ddb, w_ref[...])
        da_ref[...] = (du * (2.0 * jnp.sqrt(u_ref[...].astype(F32)))).astype(BF16)

    return _pcall(body, name="bwd_mlp_down", grid=(T // tm,),
                  in_specs=[_rows(tm, D), _rows(tm, D), _full((1, D)), _full((D_FF, D)), _rows(tm, D_FF)],
                  out_specs=[_rows(tm, D), _rows(tm, D_FF), _full((1, D))],
                  out_shape=[SDS((T, D), BF16), SDS((T, D_FF), BF16), SDS((1, D), F32)],
                  sem=("arbitrary",))(dx2, d, g4, w2, u)


def _bwd_mlp_up(da, w1, x1, g3, dx2, tm):
    T = dx2.shape[0]

    def body(da_ref, w_ref, x_ref, g_ref, dx2_ref, dx1_ref, dg_ref):
        @pl.when(pl.program_id(0) == 0)
        def _():
            dg_ref[...] = jnp.zeros_like(dg_ref)

        dh = _dot_nt(da_ref[...], w_ref[...])
        dx, dg = _rms_bwd(dh, x_ref[...], g_ref[...])
        dg_ref[...] += dg
        dx1_ref[...] = dx2_ref[...] + dx

    return _pcall(body, name="bwd_mlp_up", grid=(T // tm,),
                  in_specs=[_rows(tm, D_FF), _full((D, D_FF)), _rows(tm, D), _full((1, D)), _rows(tm, D)],
                  out_specs=[_rows(tm, D), _full((1, D))],
                  out_shape=[SDS((T, D), F32), SDS((1, D), F32)], sem=("arbitrary",))(da, w1, x1, g3, dx2)


def _bwd_mix(dx1, yo, g2, wo, z, au, bu, wmu, wsu, om, tm):
    T = dx1.shape[0]

    def body(dx_ref, yo_ref, g_ref, wo_ref, ga_ref, gb_ref, au_ref, bu_ref, wmu_ref, wsu_ref, om_ref,
             dyo_ref, dg_ref, dau_ref, dbu_ref, dga_ref, dgb_ref, dos_ref, dl_ref, dot_ref):
        @pl.when(pl.program_id(0) == 0)
        def _():
            dg_ref[...] = jnp.zeros_like(dg_ref)

        dyo, dg = _rms_bwd(dx_ref[...], yo_ref[...], g_ref[...])
        dg_ref[...] += dg
        dyob = dyo.astype(BF16)
        dyo_ref[...] = dyob
        dy = _dot_nt(dyob, wo_ref[...])
        sa = _sigmoid(ga_ref[...])
        sb = _sigmoid(gb_ref[...])
        dau = (dy * sa).astype(BF16)
        dbu = (dy * sb).astype(BF16)
        dau_ref[...] = dau
        dbu_ref[...] = dbu
        dga_ref[...] = (dy * au_ref[...] * sa * (1.0 - sa)).astype(BF16)
        dgb_ref[...] = (dy * bu_ref[...] * sb * (1.0 - sb)).astype(BF16)
        dom = _dot_nt(dau, wmu_ref[...])
        dos_ref[...] = _dot_nt(dbu, wsu_ref[...])
        prod = dom * om_ref[...]
        sub = lax.broadcasted_iota(jnp.int32, (8, tm), 0)
        for pr in range(MLA_HEADS // 2):
            sl = slice(LANES * pr, LANES * (pr + 1))
            pt = prod[:, sl].T
            d0 = jnp.sum(pt[0:64], axis=0, keepdims=True)
            d1 = jnp.sum(pt[64:128], axis=0, keepdims=True)
            dl_ref[pr, 0] = jnp.where(sub == 0, d0, jnp.where(sub == 1, d1, 0.0))
            dot_ref[0, sl, :] = dom[:, sl].T.astype(BF16)

    r = _rows(tm, D)
    w = _full((D, D))
    return _pcall(body, name="bwd_mix", grid=(T // tm,),
                  in_specs=[r, r, _full((1, D)), w, _rows(tm, D, 1), _rows(tm, D, 2), r, r, w, w, r],
                  out_specs=[r, _full((1, D)), r, r, r, r, r, pl.BlockSpec((MLA_HEADS // 2, 1, 8, tm), lambda i: (0, i, 0, 0)),
                             pl.BlockSpec((1, D, tm), lambda i: (i, 0, 0))],
                  out_shape=[SDS((T, D), BF16), SDS((1, D), F32), SDS((T, D), BF16), SDS((T, D), BF16), SDS((T, D), BF16),
                             SDS((T, D), BF16), SDS((T, D), F32), SDS((MLA_HEADS // 2, T // tm, 8, tm), F32),
                             SDS((T // tm, D, tm), BF16)],
                  sem=("arbitrary",))(dx1, yo, g2, wo, z, z, au, bu, wmu, wsu, om)


def _mla_bwd(qt, km, kt, vm, dot, lse, delta, tb):
    T = km.shape[0]
    nb = T // tb
    cc = ATT_COLS

    per = 2 if nb % 2 == 0 else 1

    def body(qt_ref, k_ref, kt_ref, v_ref, dot_ref, l_ref, dl_ref, dqt_ref, dkt_ref, dvt_ref,
             s_ref, dp_ref, p_ref, ds_ref, vh_ref):
        @pl.when(pl.program_id(1) == 0)
        def _():
            dqt_ref[...] = jnp.zeros_like(dqt_ref)

        dkt_ref[...] = jnp.zeros_like(dkt_ref)
        dvt_ref[...] = jnp.zeros_like(dvt_ref)
        for blk in range(per):
            one_block(per * pl.program_id(1) + blk, blk, qt_ref, k_ref, kt_ref, v_ref, dot_ref, l_ref, dl_ref, dqt_ref, dkt_ref,
                      dvt_ref, s_ref, dp_ref, p_ref, ds_ref, vh_ref)

    def one_block(j, blk, qt_ref, k_ref, kt_ref, v_ref, dot_ref, l_ref, dl_ref, dqt_ref, dkt_ref, dvt_ref,
                  s_ref, dp_ref, p_ref, ds_ref, vh_ref):
        lo = lax.broadcasted_iota(jnp.int32, (tb, LANES), 1) < 64
        key = lax.broadcasted_iota(jnp.int32, (tb, cc), 0)
        qry = lax.broadcasted_iota(jnp.int32, (tb, cc), 1)
        rows_j = slice(tb * blk, tb * (blk + 1))
        v = v_ref[rows_j, :]
        vh_ref[0] = jnp.where(lo, v, jnp.zeros_like(v))
        vh_ref[1] = jnp.where(lo, jnp.zeros_like(v), v)

        def scores(i, slot):
            for hh in range(2):
                sl = slice(LANES * hh, LANES * (hh + 1))
                s_ref[slot, hh] = _dot(k_ref[rows_j, sl], qt_ref[i, sl, :])
                dp_ref[slot, hh] = _dot(vh_ref[hh], dot_ref[i])

        def grads(i, slot, diagonal):
            lse_i = l_ref[0, i]
            delta_i = dl_ref[0, i]
            for hh in range(2):
                for c in range(tb // cc):
                    cols = slice(cc * c, cc * (c + 1))
                    p = jnp.exp2(s_ref[slot, hh, :, cols] * MLA_LOG2_SCALE - lse_i[hh:hh + 1, cols])
                    if diagonal:
                        p = jnp.where(key <= qry + cc * c, p, 0.0)
                    p_ref[hh, :, cols] = p.astype(BF16)
                    ds_ref[hh, :, cols] = (p * (dp_ref[slot, hh, :, cols] - delta_i[hh:hh + 1, cols]) * MLA_SCALE).astype(BF16)
            for hh in range(2):
                sl = slice(LANES * hh, LANES * (hh + 1))
                half = slice(64 * hh, 64 * (hh + 1))
                dvt_ref[blk, half, :] += _dot_nt(dot_ref[i, half, :], p_ref[hh])
                real = slice(LANES * hh, LANES * hh + MLA_NOPE + MLA_ROPE)
                dkt_ref[blk, real, :] += _dot_nt(qt_ref[i, real, :], ds_ref[hh])
                dqt_ref[i, real, :] += _dot(kt_ref[blk, real, :], ds_ref[hh])

        n_off = nb - 1 - j

        def step(u, carry):
            i0 = j + 1 + 2 * u
            scores(i0 + 1, 1)
            grads(i0, 0, False)
            scores(jnp.where(i0 + 2 < nb, i0 + 2, j), 0)
            grads(i0 + 1, 1, False)
            return carry

        scores(jnp.where(n_off > 0, j + 1, j), 0)
        lax.fori_loop(0, n_off // 2, step, 0)

        @pl.when(n_off % 2 == 1)
        def _():
            scores(j, 1)
            grads(nb - 1, 0, False)
            grads(j, 1, True)

        @pl.when(n_off % 2 == 0)
        def _():
            grads(j, 0, True)

    blk = lambda w: pl.BlockSpec((per * tb, w), lambda p, j: (j, p))
    stat = pl.BlockSpec((1, nb, 8, tb), lambda p, j: (p, 0, 0, 0))
    pair_t = lambda w: pl.BlockSpec((nb, w, tb), lambda p, j: (0, p, 0))
    blk_t = lambda w: pl.BlockSpec((per, w, tb), lambda p, j: (j, p, 0))
    return _pcall(body, name="mla_bwd", grid=(MLA_HEADS // 2, nb // per),
                  in_specs=[pair_t(256), blk(256), blk_t(256), blk(LANES), pair_t(LANES), stat, stat],
                  out_specs=[pair_t(256), blk_t(256), blk_t(LANES)],
                  out_shape=[SDS((nb, 2048, tb), F32), SDS((nb, 2048, tb), F32), SDS((nb, D, tb), F32)],
                  scratch=[pltpu.VMEM((2, 2, tb, tb), F32), pltpu.VMEM((2, 2, tb, tb), F32), pltpu.VMEM((2, tb, tb), BF16),
                           pltpu.VMEM((2, tb, tb), BF16), pltpu.VMEM((2, tb, LANES), BF16)],
                  sem=("parallel", "arbitrary"))(qt, km, kt, vm, dot, lse, delta)


def _swa_bwd(sinks, qs, ks, vs, do, o, lse):
    T = qs.shape[0]
    nb, cur, prev = _swa_specs(T)

    def body(sink_ref, q_ref, kc_ref, kp_ref, vc_ref, vp_ref, do_ref, o_ref, l_ref,
             dq_ref, dkc_ref, dkp_ref, dvc_ref, dvp_ref, dsink_ref, kb_ref, vb_ref, s_ref, dp_ref, p_ref, ds_ref):
        n = pl.program_id(0)

        @pl.when(n == 0)
        def _():
            dsink_ref[...] = jnp.zeros_like(dsink_ref)

        mask = _swa_mask(n)
        lo = lax.broadcasted_iota(jnp.int32, (WINDOW, LANES), 1) < 64
        hi = jnp.logical_not(lo)
        lane8 = lax.broadcasted_iota(jnp.int32, (8, LANES), 1)
        for g in range(2):
            gs = slice(LANES * g, LANES * (g + 1))
            kb_ref[g] = jnp.concatenate([kp_ref[:, gs], kc_ref[:, gs]], axis=0)
            vb_ref[g] = jnp.concatenate([vp_ref[:, gs], vc_ref[:, gs]], axis=0)

        def head(h):
            sl = slice(LANES * (h // 2), LANES * (h // 2 + 1))
            hm = lo if h % 2 == 0 else hi
            qp = q_ref[:, sl]
            return hm, sl, jnp.where(hm, qp, jnp.zeros_like(qp)), jnp.where(hm, do_ref[:, sl], 0.0).astype(BF16)

        for h in range(SWA_HEADS):
            _, _, qh, dom = head(h)
            s_ref[h] = _dot_nt(qh, kb_ref[h // 8])
            dp_ref[h] = _dot_nt(dom, vb_ref[h // 8])
        dsink = jnp.zeros((8, LANES), F32)
        for h in range(SWA_HEADS):
            hm, sl, _, _ = head(h)
            lse_h = jnp.max(jnp.where(hm, l_ref[:, sl], -jnp.inf), axis=1, keepdims=True)
            delta = jnp.sum(jnp.where(hm, do_ref[:, sl] * o_ref[:, sl], 0.0), axis=1, keepdims=True)
            p = jnp.exp(jnp.where(mask, s_ref[h] * SWA_SCALE, NEG) - lse_h)
            p_ref[h] = p.astype(BF16)
            ds_ref[h] = (p * (dp_ref[h] - delta) * SWA_SCALE).astype(BF16)
            d_sink = -jnp.sum(jnp.exp(sink_ref[h] - lse_h) * delta, axis=0, keepdims=True)
            dsink = dsink + jnp.where(lane8 == h, d_sink, 0.0)
        dsink_ref[...] += dsink
        for g in range(2):
            gs = slice(LANES * g, LANES * (g + 1))
            dkb = jnp.zeros((2 * WINDOW, LANES), F32)
            dvb = jnp.zeros((2 * WINDOW, LANES), F32)
            for j in range(4 * g, 4 * g + 4):
                dqs = []
                for h in (2 * j, 2 * j + 1):
                    _, _, qh, dom = head(h)
                    dvb = dvb + _dot_tn(p_ref[h], dom)
                    dkb = dkb + _dot_tn(ds_ref[h], qh)
                    dqs.append(_dot(ds_ref[h], kb_ref[g]))
                dq_ref[:, LANES * j:LANES * (j + 1)] = jnp.where(lo, dqs[0], dqs[1])
            dkp_ref[:, gs] = dkb[:WINDOW]
            dkc_ref[:, gs] = dkb[WINDOW:]
            dvp_ref[:, gs] = dvb[:WINDOW]
            dvc_ref[:, gs] = dvb[WINDOW:]

    band = pltpu.VMEM((2, 2 * WINDOW, LANES), BF16)
    return _pcall(body, name="swa_bwd", grid=(nb,),
                  in_specs=[pl.BlockSpec(memory_space=pltpu.SMEM), cur(D), cur(256), prev(256), cur(256), prev(256),
                            cur(D), cur(D), cur(D)],
                  out_specs=[cur(D), cur(256), cur(256), cur(256), cur(256), _full((8, LANES))],
                  out_shape=[SDS((T, D), F32), SDS((T, 256), F32), SDS((T, 256), F32), SDS((T, 256), F32), SDS((T, 256), F32),
                             SDS((8, LANES), F32)],
                  scratch=[band, band, pltpu.VMEM((SWA_HEADS, WINDOW, 2 * WINDOW), F32),
                           pltpu.VMEM((SWA_HEADS, WINDOW, 2 * WINDOW), F32), pltpu.VMEM((SWA_HEADS, WINDOW, 2 * WINDOW), BF16),
                           pltpu.VMEM((SWA_HEADS, WINDOW, 2 * WINDOW), BF16)],
                  sem=("arbitrary",))(sinks, qs, ks, ks, vs, vs, do, o, lse)


def _bwd_qkv(dqm, dkm, dvm, dqs, dkc, dkp, dvc, dvp, z, gq, gkv, wqb, wkn, wv, tab_m, tab_s):
    T = z.shape[0]
    tm = WINDOW
    nb = T // tm
    per = dqm.shape[2] // tm

    tab_mt = [t.T for t in tab_m]
    half = MLA_ROPE // 2

    def rope_t(v, c, a, b):
        return v * c + pltpu.roll(v, LANES - half, 0) * a + pltpu.roll(v, half, 0) * b

    def rms_bwd_t(dy, x, g):
        r = lax.rsqrt(jnp.mean(x * x, axis=0, keepdims=True) + EPS)
        xn = x * r
        dn = dy * g
        return r * (dn - xn * jnp.mean(dn * xn, axis=0, keepdims=True)), jnp.sum(dy * xn, axis=1, keepdims=True)

    def body(dqm_ref, dkm_ref, dvm_ref, dqs_ref, dkc_ref, dkp_ref, dvc_ref, dvp_ref, qa_ref, kva_ref, gq_ref, gkv_ref,
             wqb_ref, wkn_ref, wv_ref, cmt_ref, amt_ref, bmt_ref, cs_ref, as_ref, bs_ref,
             dq_out, dkn_out, dv_out, dsq_ref, drest_ref, dgq_ref, dgkv_ref):
        i = pl.program_id(0)

        @pl.when(i == 0)
        def _():
            dgq_ref[...] = jnp.zeros_like(dgq_ref)
            dgkv_ref[...] = jnp.zeros_like(dgkv_ref)

        cmt, amt, bmt = cmt_ref[...], -amt_ref[...], -bmt_ref[...]
        cs, as_, bs = cs_ref[...], -as_ref[...], -bs_ref[...]
        row = lax.broadcasted_iota(jnp.int32, (LANES, tm), 0)
        nope = row < MLA_NOPE
        roped = jnp.logical_and(row >= MLA_NOPE, row < MLA_NOPE + MLA_ROPE)
        dkr = jnp.zeros((LANES, tm), F32)
        for h in range(MLA_HEADS):
            sl = slice(LANES * h, LANES * (h + 1))
            dq_out[0, sl, :] = rope_t(dqm_ref[0, sl, :], cmt, amt, bmt).astype(BF16)
            dk_h = dkm_ref[0, sl, :]
            dkn_out[0, sl, :] = jnp.where(nope, dk_h, 0.0).astype(BF16)
            dkr = dkr + jnp.where(roped, dk_h, 0.0)
        dv_out[0] = dvm_ref[0].astype(BF16)
        dqn = _dot(wqb_ref[...], dq_out[0])
        dkvn = _dot(wkn_ref[...], dkn_out[0]) + _dot(wv_ref[...], dv_out[0])
        dqa, dgq = rms_bwd_t(dqn, qa_ref[...].T, gq_ref[...])
        dkva, dgkv = rms_bwd_t(dkvn, kva_ref[...].T, gkv_ref[...])
        dgq_ref[...] += dgq
        dgkv_ref[...] += dgkv
        for j in range(D // LANES):
            sl = slice(LANES * j, LANES * (j + 1))
            dsq_ref[:, sl] = _rope(dqs_ref[:, sl], cs, as_, bs, SWA_HD // 2).astype(BF16)
        keep = (i < nb - 1).astype(F32)
        drest_ref[:, 0:256] = dqa.T.astype(BF16)
        for j in range(2):
            sl = slice(LANES * j, LANES * (j + 1))
            dk = dkc_ref[:, sl] + keep * dkp_ref[:, sl]
            drest_ref[:, 256 + LANES * j:256 + LANES * (j + 1)] = _rope(dk, cs, as_, bs, SWA_HD // 2).astype(BF16)
        drest_ref[:, 512:768] = (dvc_ref[...] + keep * dvp_ref[...]).astype(BF16)
        drest_ref[:, 768:896] = dkva.T.astype(BF16)
        drest_ref[:, 896:1024] = rope_t(dkr, cmt, amt, bmt).T.astype(BF16)

    nxt = pl.BlockSpec((tm, 256), lambda i: (jnp.minimum(i + 1, nb - 1), 0))
    tab = [_rows(tm, LANES)] * 3
    tab_t = [pl.BlockSpec((LANES, tm), lambda i: (0, i))] * 3
    blk_t = lambda w: pl.BlockSpec((1, w, tm), lambda i: (i // per, 0, i % per))
    return _pcall(body, name="bwd_qkv", grid=(nb,),
                  in_specs=[blk_t(2048), blk_t(2048), blk_t(1024), _rows(tm, 1024), _rows(tm, 256), nxt,
                            _rows(tm, 256), nxt, _rows(tm, 256, 12), _rows(tm, 128, 30), _full((Q_LORA, 1)), _full((KV_LORA, 1)),
                            _full((Q_LORA, 2048)), _full((KV_LORA, 2048)), _full((KV_LORA, 1024))] + tab_t + tab,
                  out_specs=[blk_t(2048), blk_t(2048), blk_t(1024), _rows(tm, 1024), _rows(tm, 1024),
                             _full((Q_LORA, 1)), _full((KV_LORA, 1))],
                  out_shape=[SDS(dqm.shape, BF16), SDS(dkm.shape, BF16), SDS(dvm.shape, BF16), SDS((T, 1024), BF16),
                             SDS((T, 1024), BF16), SDS((Q_LORA, 1), F32), SDS((KV_LORA, 1), F32)],
                  sem=("arbitrary",))(dqm, dkm, dvm, dqs, dkc, dkp, dvc, dvp, z, z, gq.reshape(Q_LORA, 1),
                                      gkv.reshape(KV_LORA, 1), wqb, wkn, wv, *tab_mt, *tab_s)


def _bwd_in(dsq, dga, dgb, drest, w_in_p, x, g1, dx1, tm):
    T = x.shape[0]

    def body(a_ref, b_ref, c_ref, d_ref, w_ref, x_ref, g_ref, dx1_ref, dx_ref, dg_ref):
        @pl.when(pl.program_id(0) == 0)
        def _():
            dg_ref[...] = jnp.zeros_like(dg_ref)

        dh = (_dot_nt(a_ref[...], w_ref[:, 0:1024]) + _dot_nt(b_ref[...], w_ref[:, 1024:2048])
              + _dot_nt(c_ref[...], w_ref[:, 2048:3072]) + _dot_nt(d_ref[...], w_ref[:, 3072:4096]))
        dx, dg = _rms_bwd(dh, x_ref[...], g_ref[...])
        dg_ref[...] += dg
        dx_ref[...] = dx1_ref[...] + dx

    r = _rows(tm, D)
    return _pcall(body, name="bwd_in", grid=(T // tm,),
                  in_specs=[r, r, r, r, _full((D, NZ)), r, _full((1, D)), r],
                  out_specs=[r, _full((1, D))], out_shape=[SDS((T, D), F32), SDS((1, D), F32)],
                  sem=("arbitrary",))(dsq, dga, dgb, drest, w_in_p, x, g1, dx1)


def _wgrad(a, g, name, into=None):
    T, K = a.shape
    N = g.shape[1]
    tk, tn, tt = min(K, 1024), min(N, 1024), min(T, 1024)
    if into is not None:
        buf, weight = into
        _, row0, lane0 = PACK_AT[weight]
        shard = {n: (r, c) for n, r, c in BIG}[weight]
        assert lane0 == 0 and shard[1] == D and tk % shard[0] == 0
        per_step = tk // shard[0]
    assert K % tk == 0 and N % tn == 0 and T % tt == 0, (a.shape, g.shape)
    steps = T // tt

    def body(a_ref, g_ref, *rest):
        o_ref, acc_ref = rest[-2:]
        t = pl.program_id(2)

        @pl.when(t == 0)
        def _():
            acc_ref[...] = jnp.zeros_like(acc_ref)

        acc_ref[...] += _dot_tn(a_ref[...].astype(BF16), g_ref[...].astype(BF16))

        @pl.when(t == steps - 1)
        def _():
            o_ref[...] = acc_ref[...].astype(o_ref.dtype).reshape(o_ref.shape)

    in_specs = [pl.BlockSpec((tt, tk), lambda k, n, t: (t, k)), pl.BlockSpec((tt, tn), lambda k, n, t: (t, n))]
    if into is None:
        return _pcall(body, name=name, grid=(K // tk, N // tn, steps), in_specs=in_specs,
                      out_specs=pl.BlockSpec((tk, tn), lambda k, n, t: (k, n)), out_shape=SDS((K, N), F32),
                      scratch=[pltpu.VMEM((tk, tn), F32)], sem=("parallel", "parallel", "arbitrary"))(a, g)
    assert row0 % shard[0] == 0 and (K // tk) * (N // tn) * per_step == N_CHIPS
    return _pcall(body, name=name, grid=(K // tk, N // tn, steps), in_specs=in_specs + [ANY],
                  out_specs=pl.BlockSpec((per_step, shard[0], tn), lambda k, n, t: (k + n, row0 // shard[0], 0)),
                  out_shape=SDS(buf.shape, buf.dtype),
                  scratch=[pltpu.VMEM((tk, tn), F32)], sem=("parallel", "parallel", "arbitrary"), aliases={2: 0})(a, g, buf)


def _wgrad_t(at, g, name):
    nblk, K, tt = at.shape
    N = g.shape[1]
    tk = min(K, 1024)
    per_step = 4 if nblk % 4 == 0 else 1
    assert K % tk == 0 and g.shape[0] == nblk * tt

    def body(a_ref, g_ref, o_ref):
        @pl.when(pl.program_id(1) == 0)
        def _():
            o_ref[...] = jnp.zeros_like(o_ref)

        acc = _dot(a_ref[0], g_ref[0:tt, :].astype(BF16))
        for b in range(1, per_step):
            acc = acc + _dot(a_ref[b], g_ref[tt * b:tt * (b + 1), :].astype(BF16))
        o_ref[...] += acc

    return _pcall(body, name=name, grid=(K // tk, nblk // per_step),
                  in_specs=[pl.BlockSpec((per_step, tk, tt), lambda k, t: (t, k, 0)),
                            pl.BlockSpec((per_step * tt, N), lambda k, t: (t, 0))],
                  out_specs=pl.BlockSpec((tk, N), lambda k, t: (k, 0)), out_shape=SDS((K, N), F32),
                  sem=("parallel", "arbitrary"))(at, g)


def _adamw(w, packed_g, m, v, name, transposed=False):
    R, C = w.shape[1:][::-1] if transposed else w.shape[1:]
    _, row0, lane0 = PACK_AT[name]
    tr = min(R, 256 if row0 % 256 == 0 else 128)
    assert row0 % tr == 0 and R % tr == 0

    def body(w_ref, g_ref, m_ref, v_ref, go_ref, d_ref, m2_ref, v2_ref):
        g_ = g_ref[...].T[lane0:lane0 + C] if transposed else g_ref[:, lane0:lane0 + C]
        go_ref[0] = g_
        m2 = ADAM_B1 * m_ref[0] + (1.0 - ADAM_B1) * g_
        v2 = ADAM_B2 * v_ref[0] + (1.0 - ADAM_B2) * jnp.square(g_)
        m_hat = m2 / (1.0 - ADAM_B1 ** ADAM_STEP)
        v_hat = v2 / (1.0 - ADAM_B2 ** ADAM_STEP)
        d_ref[0] = -ADAM_LR * (m_hat / (jnp.sqrt(v_hat) + ADAM_EPS) + ADAM_WD * w_ref[0])
        m2_ref[0] = m2
        v2_ref[0] = v2

    r = pl.BlockSpec((1, C, tr), lambda i: (0, 0, i)) if transposed else pl.BlockSpec((1, tr, C), lambda i: (0, i, 0))
    return _pcall(body, name="adamw_" + name, grid=(R // tr,),
                  in_specs=[r, pl.BlockSpec((tr, D), lambda i: (row0 // tr + i, 0)), r, r], out_specs=[r] * 4,
                  out_shape=[SDS(w.shape, F32)] * 4, sem=("parallel",))(w, packed_g, m, v)


def _adamw_small(w, parts, m, v):
    def body(w_ref, p_ref, m_ref, v_ref, g_ref, d_ref, m2_ref, v2_ref):
        g_ = p_ref[0]
        for k in range(1, N_DEV):
            g_ = g_ + p_ref[k]
        g_ref[...] = g_
        m2 = ADAM_B1 * m_ref[...] + (1.0 - ADAM_B1) * g_
        v2 = ADAM_B2 * v_ref[...] + (1.0 - ADAM_B2) * jnp.square(g_)
        m_hat = m2 / (1.0 - ADAM_B1 ** ADAM_STEP)
        v_hat = v2 / (1.0 - ADAM_B2 ** ADAM_STEP)
        d_ref[...] = -ADAM_LR * (m_hat / (jnp.sqrt(v_hat) + ADAM_EPS) + ADAM_WD * w_ref[...])
        m2_ref[...] = m2
        v2_ref[...] = v2

    s = _full((8, D))
    return _pcall(body, name="adamw_small", grid=(1,), in_specs=[s, _full((N_DEV, 8, D)), s, s], out_specs=[s] * 4,
                  out_shape=[SDS((8, D), F32)] * 4, sem=("arbitrary",))(w, parts, m, v)


ANY = pl.BlockSpec(memory_space=pl.ANY)


def _place():
    x, y, c = lax.axis_index("x"), lax.axis_index("y"), lax.axis_index("c")
    chips = [(1 - x, y), (x, 1 - y), (1 - x, 1 - y)]
    return x, y, c, chips


def _all_gather(wpk):
    rows = wpk.shape[0]
    HALF = rows // 2
    assert HALF % 16 == 0

    def body(in_ref, out_ref, send_sems, recv_sems):
        x, y, c, chips = _place()
        half = pl.ds(pl.multiple_of(c * HALF, 16), HALF)
        other = pl.ds(pl.multiple_of((1 - c) * HALF, 16), HALF)

        def copy(k, src, dst, to):
            return pltpu.make_async_remote_copy(src_ref=src, dst_ref=dst, send_sem=send_sems.at[k], recv_sem=recv_sems.at[k],
                                                device_id=to, device_id_type=MESH)

        first = [copy(k, in_ref.at[half], out_ref.at[2 * x + y, half], (cx, cy, c)) for k, (cx, cy) in enumerate(chips)]
        first.append(copy(6, in_ref, out_ref.at[2 * x + y], (x, y, 1 - c)))
        for cp in first:
            cp.start()
        passed = []
        for k, (cx, cy) in enumerate(chips):
            slot = out_ref.at[2 * cx + cy, half]
            copy(k, slot, slot, (x, y, c)).wait_recv()
            fwd = copy(3 + k, slot, slot, (x, y, 1 - c))
            fwd.start()
            passed.append(fwd)
        for k, (cx, cy) in enumerate(chips):
            slot = out_ref.at[2 * cx + cy, other]
            copy(3 + k, slot, slot, (x, y, c)).wait_recv()
        copy(6, in_ref, out_ref.at[2 * x + y], (x, y, c)).wait_recv()
        for cp in first + passed:
            cp.wait_send()

    return _pcall(body, name="all_gather_weights", in_specs=[ANY], out_specs=ANY,
                  out_shape=SDS((N_CHIPS, rows, D), BF16),
                  scratch=[pltpu.SemaphoreType.DMA((7,)), pltpu.SemaphoreType.DMA((7,))])(wpk)


HBM = pl.BlockSpec(memory_space=pltpu.HBM)
SEM = pl.BlockSpec(memory_space=pltpu.SEMAPHORE)
DATAFLOW = pltpu.SideEffectType.DATAFLOW_SIDE_EFFECTING


def _in_hbm(a):
    return pltpu.with_memory_space_constraint(a, pltpu.HBM)


def _gather_late_start(wpk, after):
    rows = wpk.shape[0]

    def body(in_ref, land_ref, after_ref, send_sems, recv_sems, in_thru, land_thru, token):
        x, y, c, chips = _place()
        for k, to in enumerate([(cx, cy, c) for cx, cy in chips] + [(x, y, 1 - c)]):
            pltpu.make_async_remote_copy(src_ref=in_ref, dst_ref=land_ref.at[2 * x + y], send_sem=send_sems.at[k],
                                         recv_sem=recv_sems.at[k], device_id=to, device_id_type=MESH).start()
        token[...] = jnp.zeros_like(token)

    return pl.pallas_call(
        body, name="gather_late_start",
        out_shape=(pltpu.SemaphoreType.DMA((4,)), pltpu.SemaphoreType.DMA((4,)), pltpu.HBM(wpk.shape, wpk.dtype),
                   pltpu.HBM((N_CHIPS, rows, D), wpk.dtype), SDS((8, LANES), F32)),
        in_specs=(HBM, HBM, ANY), out_specs=(SEM, SEM, HBM, HBM, pl.BlockSpec(memory_space=pltpu.VMEM)),
        input_output_aliases={0: 2, 1: 3}, compiler_params=pltpu.CompilerParams(has_side_effects=DATAFLOW),
    )(_in_hbm(wpk), _in_hbm(lax.empty((N_CHIPS, rows, D), wpk.dtype)), after)


def _gather_late_wait(send_sems, recv_sems, in_thru, land_thru, after):
    def body(in_ref, land_ref, send_sems, recv_sems, after_ref, after2_ref, in_dead, got_ref):
        x, y, c, chips = _place()
        for k, (sx, sy) in enumerate(chips + [(x, y)]):
            cp = pltpu.make_async_remote_copy(src_ref=in_ref, dst_ref=land_ref.at[2 * sx + sy], send_sem=send_sems.at[k],
                                              recv_sem=recv_sems.at[k], device_id=(x, y, c), device_id_type=MESH)
            cp.wait_send()
            cp.wait_recv()

    return pl.pallas_call(
        body, name="gather_late_wait",
        out_shape=(pltpu.HBM(in_thru.shape, in_thru.dtype), pltpu.HBM(land_thru.shape, land_thru.dtype)),
        in_specs=(HBM, HBM, SEM, SEM, ANY, ANY), out_specs=(HBM, HBM), input_output_aliases={0: 0, 1: 1},
        compiler_params=pltpu.CompilerParams(has_side_effects=DATAFLOW),
    )(in_thru, land_thru, send_sems, recv_sems, *after)[1]


def _rs_sibling(gpk):
    HALF = gpk.shape[1] // 2

    def body(in_ref, out_ref, send_sem, recv_sem):
        x, y, c, _ = _place()
        theirs = pl.ds(pl.multiple_of((1 - c) * HALF, 16), HALF)
        cp = pltpu.make_async_remote_copy(src_ref=in_ref.at[:, theirs], dst_ref=out_ref, send_sem=send_sem, recv_sem=recv_sem,
                                          device_id=(x, y, 1 - c), device_id_type=MESH)
        cp.start()
        cp.wait()

    return _pcall(body, name="rs_sibling", in_specs=[ANY], out_specs=ANY, out_shape=SDS((N_CHIPS, HALF, D), gpk.dtype),
                  scratch=[pltpu.SemaphoreType.DMA, pltpu.SemaphoreType.DMA])(gpk)


def _rs_add_sibling(cidx, gpk, got):
    HALF = got.shape[1]
    th = HALF // 4
    nh = HALF // th
    assert th % 16 == 0

    def body(c_ref, a_ref, b_ref, o_ref):
        o_ref[...] = (a_ref[...].astype(F32) + b_ref[...].astype(F32)).astype(BF16)

    gs = pltpu.PrefetchScalarGridSpec(
        num_scalar_prefetch=1, grid=(N_CHIPS, nh),
        in_specs=[pl.BlockSpec((1, th, D), lambda j, i, c: (j, c[0] * nh + i, 0)), pl.BlockSpec((1, th, D), lambda j, i, c: (j, i, 0))],
        out_specs=pl.BlockSpec((1, th, D), lambda j, i, c: (j, i, 0)))
    return pl.pallas_call(body, name="rs_add_sibling", grid_spec=gs, out_shape=SDS((N_CHIPS, HALF, D), BF16),
                          compiler_params=pltpu.CompilerParams(dimension_semantics=("parallel", "parallel"),
                                                               vmem_limit_bytes=48 << 20))(cidx, gpk, got)


def _rs_chips_start(part, small, after):
    def body(p_ref, s_ref, land_ref, sland_ref, after_ref, send_sems, recv_sems, p_thru, s_thru, land_thru, sland_thru, token):
        x, y, c, chips = _place()
        for k, (cx, cy) in enumerate(chips):
            pltpu.make_async_remote_copy(src_ref=p_ref.at[2 * cx + cy], dst_ref=land_ref.at[2 * x + y], send_sem=send_sems.at[k],
                                         recv_sem=recv_sems.at[k], device_id=(cx, cy, c), device_id_type=MESH).start()
        peers = [(x, y, 1 - c)] + [(cx, cy, c) for cx, cy in chips] + [(cx, cy, 1 - c) for cx, cy in chips]
        for k, to in enumerate(peers):
            pltpu.make_async_remote_copy(src_ref=s_ref, dst_ref=sland_ref.at[4 * x + 2 * y + c], send_sem=send_sems.at[3 + k],
                                         recv_sem=recv_sems.at[3 + k], device_id=to, device_id_type=MESH).start()
        token[...] = jnp.zeros_like(token)

    return pl.pallas_call(
        body, name="rs_chips_start",
        out_shape=(pltpu.SemaphoreType.DMA((10,)), pltpu.SemaphoreType.DMA((10,)), pltpu.HBM(part.shape, part.dtype),
                   pltpu.HBM(small.shape, small.dtype), pltpu.HBM(part.shape, part.dtype), pltpu.HBM((N_DEV, 8, D), F32),
                   SDS((8, LANES), F32)),
        in_specs=(HBM, HBM, HBM, HBM, ANY), out_specs=(SEM, SEM, HBM, HBM, HBM, HBM, pl.BlockSpec(memory_space=pltpu.VMEM)),
        input_output_aliases={0: 2, 1: 3, 2: 4, 3: 5}, compiler_params=pltpu.CompilerParams(has_side_effects=DATAFLOW),
    )(_in_hbm(part), _in_hbm(small), _in_hbm(lax.empty(part.shape, part.dtype)), _in_hbm(lax.empty((N_DEV, 8, D), F32)), after)


def _rs_chips_wait(send_sems, recv_sems, p_thru, s_thru, land_thru, sland_thru, after):
    def body(p_ref, s_ref, land_ref, sland_ref, send_sems, recv_sems, *after_and_outputs):
        x, y, c, chips = _place()
        for k, (cx, cy) in enumerate(chips):
            cp = pltpu.make_async_remote_copy(src_ref=p_ref.at[0], dst_ref=land_ref.at[2 * cx + cy], send_sem=send_sems.at[k],
                                              recv_sem=recv_sems.at[k], device_id=(cx, cy, c), device_id_type=MESH)
            cp.wait_send()
            cp.wait_recv()
        peers = [(x, y, 1 - c)] + [(cx, cy, c) for cx, cy in chips] + [(cx, cy, 1 - c) for cx, cy in chips]
        for k, (px, py, pc) in enumerate(peers):
            cp = pltpu.make_async_remote_copy(src_ref=s_ref, dst_ref=sland_ref.at[4 * px + 2 * py + pc], send_sem=send_sems.at[3 + k],
                                              recv_sem=recv_sems.at[3 + k], device_id=(px, py, pc), device_id_type=MESH)
            cp.wait_send()
            cp.wait_recv()

    hbm = lambda a: pltpu.HBM(a.shape, a.dtype)
    outs = pl.pallas_call(
        body, name="rs_chips_wait", out_shape=(hbm(p_thru), hbm(s_thru), hbm(land_thru), hbm(sland_thru)),
        in_specs=(HBM, HBM, HBM, HBM, SEM, SEM) + (ANY,) * len(after), out_specs=(HBM, HBM, HBM, HBM),
        input_output_aliases={0: 0, 1: 1, 2: 2, 3: 3}, compiler_params=pltpu.CompilerParams(has_side_effects=DATAFLOW),
    )(p_thru, s_thru, land_thru, sland_thru, send_sems, recv_sems, *after)
    return outs[0], outs[2], outs[3]


def _rs_add_chips(qidx, part, parts):
    HALF = part.shape[1]
    th = HALF // 4
    nh = HALF // th
    assert th % 16 == 0

    def body(q_ref, own_ref, p_ref, o_ref):
        for me in range(N_CHIPS):
            @pl.when(q_ref[0] == me)
            def _(me=me):
                t = [(own_ref[0] if j == me else p_ref[j]).astype(F32) for j in range(N_CHIPS)]
                o_ref[...] = ((t[0] + t[1]) + t[2]) + t[3]

    gs = pltpu.PrefetchScalarGridSpec(
        num_scalar_prefetch=1, grid=(HALF // th,),
        in_specs=[pl.BlockSpec((1, th, D), lambda i, q: (q[0], i, 0)), pl.BlockSpec((N_CHIPS, th, D), lambda i, q: (0, i, 0))],
        out_specs=pl.BlockSpec((th, D), lambda i, q: (q[1] * nh + i, 0)))
    return pl.pallas_call(body, name="rs_add_chips", grid_spec=gs, out_shape=SDS((2 * HALF, D), F32),
                          compiler_params=pltpu.CompilerParams(dimension_semantics=("parallel",),
                                                               vmem_limit_bytes=48 << 20))(qidx, part, parts)


def _rs_join(shard, name):
    HALF = shard.shape[0] // 2

    def body(in_ref, out_ref, send_sem, recv_sem):
        x, y, c, _ = _place()
        rows = pl.ds(pl.multiple_of(c * HALF, 16), HALF)
        cp = pltpu.make_async_remote_copy(src_ref=in_ref.at[rows], dst_ref=out_ref.at[rows], send_sem=send_sem, recv_sem=recv_sem,
                                          device_id=(x, y, 1 - c), device_id_type=MESH)
        cp.start()
        cp.wait()

    return _pcall(body, name=name, in_specs=[ANY], out_specs=ANY, out_shape=SDS(shard.shape, F32),
                  scratch=[pltpu.SemaphoreType.DMA, pltpu.SemaphoreType.DMA], aliases={0: 0})(shard)


def _reduce_late_start(gpk, after):
    rows = gpk.shape[1]
    HALF = rows // 2
    assert HALF % 16 == 0

    def body(in_ref, land_ref, after_ref, send_sems, recv_sems, in_thru, land_thru, token):
        x, y, c, chips = _place()
        me = 4 * x + 2 * y + c
        peers = [(x, y, 1 - c)] + [(cx, cy, c) for cx, cy in chips] + [(cx, cy, 1 - c) for cx, cy in chips]
        for k, (px, py, pc) in enumerate(peers):
            src = in_ref.at[2 * px + py, pl.ds(pl.multiple_of(pc * HALF, 16), HALF)]
            pltpu.make_async_remote_copy(src_ref=src, dst_ref=land_ref.at[me], send_sem=send_sems.at[k], recv_sem=recv_sems.at[k],
                                         device_id=(px, py, pc), device_id_type=MESH).start()
        token[...] = jnp.zeros_like(token)

    return pl.pallas_call(
        body, name="reduce_late_start",
        out_shape=(pltpu.SemaphoreType.DMA((7,)), pltpu.SemaphoreType.DMA((7,)), pltpu.HBM(gpk.shape, gpk.dtype),
                   pltpu.HBM((N_DEV, HALF, D), gpk.dtype), SDS((8, LANES), F32)),
        in_specs=(HBM, HBM, ANY), out_specs=(SEM, SEM, HBM, HBM, pl.BlockSpec(memory_space=pltpu.VMEM)),
        input_output_aliases={0: 2, 1: 3}, compiler_params=pltpu.CompilerParams(has_side_effects=DATAFLOW),
    )(_in_hbm(gpk), _in_hbm(lax.empty((N_DEV, HALF, D), gpk.dtype)), after)


def _reduce_late_wait(send_sems, recv_sems, in_thru, land_thru, after):
    def body(in_ref, land_ref, send_sems, recv_sems, after_ref, in_out, got_ref):
        x, y, c, chips = _place()
        peers = [(x, y, 1 - c)] + [(cx, cy, c) for cx, cy in chips] + [(cx, cy, 1 - c) for cx, cy in chips]
        for k, (px, py, pc) in enumerate(peers):
            cp = pltpu.make_async_remote_copy(src_ref=land_ref.at[0], dst_ref=land_ref.at[4 * px + 2 * py + pc],
                                              send_sem=send_sems.at[k], recv_sem=recv_sems.at[k],
                                              device_id=(px, py, pc), device_id_type=MESH)
            cp.wait_send()
            cp.wait_recv()

    return pl.pallas_call(
        body, name="reduce_late_wait",
        out_shape=(pltpu.HBM(in_thru.shape, in_thru.dtype), pltpu.HBM(land_thru.shape, land_thru.dtype)),
        in_specs=(HBM, HBM, SEM, SEM, ANY), out_specs=(HBM, HBM), input_output_aliases={0: 0, 1: 1},
        compiler_params=pltpu.CompilerParams(has_side_effects=DATAFLOW),
    )(in_thru, land_thru, send_sems, recv_sems, after)


def _reduce_late_add(didx, gpk, parts):
    HALF = parts.shape[1]
    th = HALF // 4
    nh = HALF // th
    assert th % 16 == 0

    def body(d_ref, own_ref, p_ref, o_ref):
        for me in range(N_DEV):
            @pl.when(d_ref[0] == me)
            def _(me=me):
                t = [(own_ref[0] if j == me else p_ref[j]).astype(F32) for j in range(N_DEV)]
                o_ref[...] = ((((((t[0] + t[1]) + t[2]) + t[3]) + t[4]) + t[5]) + t[6]) + t[7]

    gs = pltpu.PrefetchScalarGridSpec(
        num_scalar_prefetch=1, grid=(nh,),
        in_specs=[pl.BlockSpec((1, th, D), lambda i, d: (d[1], d[2] * nh + i, 0)), pl.BlockSpec((N_DEV, th, D), lambda i, d: (0, i, 0))],
        out_specs=pl.BlockSpec((th, D), lambda i, d: (d[2] * nh + i, 0)))
    return pl.pallas_call(body, name="reduce_late_add", grid_spec=gs, out_shape=SDS((2 * HALF, D), F32),
                          compiler_params=pltpu.CompilerParams(dimension_semantics=("parallel",),
                                                               vmem_limit_bytes=48 << 20))(didx, gpk, parts)


def _pack_early(b, dtype):
    lanes = lambda a: jnp.pad(a.astype(dtype), ((0, 0), (0, D - a.shape[1])))
    pair = jnp.concatenate([b["w_q_b"].astype(dtype), b["w_ple"].astype(dtype), jnp.zeros((256, D - 640), dtype)], axis=1)
    return jnp.concatenate([lanes(b["w_in"]), pair, lanes(b["w_kv_b"])], axis=0)


def _pack_late(b, dtype):
    return jnp.concatenate([b[n].astype(dtype) for n in ("w_mla_up", "w_swa_up", "w_out", "w_ple_gate", "w_mlp_up", "w_mlp_down")],
                           axis=0)


def _unpack_shards(pk, which):
    return {n: pk[PACK_AT[n][1]:PACK_AT[n][1] + r, PACK_AT[n][2]:PACK_AT[n][2] + c] for n, r, c in BIG if PACK_AT[n][0] == which}


def _full_weights(gathered, which):
    per_chip = [_unpack_shards(gathered[j], which) for j in range(N_CHIPS)]
    out = {}
    for n in per_chip[0]:
        shards = [pc[n] for pc in per_chip]
        if n == "w_in":
            out["w_in_p"] = _w_in_internal(shards)
        else:
            out[n] = jnp.concatenate(shards, axis=1 if n in COL_SHARDED else 0)
    return out


def _split_full_grads(grads, pack, dtype):
    shard = {n: (r, c) for n, r, c in BIG}
    chunks = []
    for j in range(N_CHIPS):
        blocks = {}
        for n, g in grads.items():
            if n == "w_in_p":
                blocks["w_in"] = _w_in_grad_shard(g, j)
                continue
            r, c = shard[n]
            blocks[n] = g[:, j * c:(j + 1) * c] if n in COL_SHARDED else g[j * r:(j + 1) * r]
        chunks.append(pack(blocks, dtype))
    return jnp.stack(chunks)


W_IN_SHARD = 936
W_IN_SEGMENTS = ((0, 256, (3072,)), (256, 384, (3840,)), (384, 416, (4032,)), (416, 1440, (0,)), (1440, 1504, (3328, 3392)),
                 (1504, 1568, (3456, 3520)), (1568, 1632, (3584, 3648)), (1632, 1696, (3712, 3776)), (1696, 3744, (1024,)))


def _w_in_internal(shards):
    def cols(a, b):
        out = []
        for j, s in enumerate(shards):
            lo, hi = max(a, W_IN_SHARD * j), min(b, W_IN_SHARD * (j + 1))
            if lo < hi:
                out.append(s[:, lo - W_IN_SHARD * j:hi - W_IN_SHARD * j])
        return out

    pieces = {}
    for a, b, places in W_IN_SEGMENTS:
        for at in places:
            pieces[at] = cols(a, b)
    zeros = lambda n: [jnp.zeros((D, n), shards[0].dtype)]
    pieces[3968] = zeros(64)
    pieces[4064] = zeros(32)
    return jnp.concatenate([piece for at in sorted(pieces) for piece in pieces[at]], axis=1)


def _w_in_grad_shard(g, j):
    def internal(a, b):
        out = []
        while a < b:
            end = min(b, (a // D + 1) * D)
            out.append(g[a // D][:, a % D:a % D + end - a])
            a = end
        return out

    out = []
    for a, b, places in W_IN_SEGMENTS:
        lo, hi = max(a, W_IN_SHARD * j), min(b, W_IN_SHARD * (j + 1))
        if lo < hi:
            parts = [internal(at + lo - a, at + hi - a) for at in places]
            if len(parts) == 1:
                out += parts[0]
            else:
                assert len(parts[0]) == len(parts[1]) == 1
                out.append(parts[0][0] + parts[1][0])
    return jnp.concatenate(out, axis=1)


def _local_step(x, p, tgt, w, small, late_weights, late_grads_out):
    T = x.shape[0]
    tm = 256
    tb = 256
    w_in_p = w["w_in_p"]
    wqb = jnp.pad(w["w_q_b"].reshape(Q_LORA, MLA_HEADS, 96), ((0, 0), (0, 0), (0, 32))).reshape(Q_LORA, 2048)
    wkv = w["w_kv_b"].reshape(KV_LORA, MLA_HEADS, 128)
    wkn = jnp.pad(wkv[:, :, :64], ((0, 0), (0, 0), (0, 64))).reshape(KV_LORA, 2048)
    wv = wkv[:, :, 64:].reshape(KV_LORA, 1024)
    tab_m = _rope_tables(T, "mla")
    tab_s = _rope_tables(T, "swa")
    g1, gq, gkv, sinks = small["g_mix_pre"], small["g_q_a"], small["g_kv_a"], small["sinks"]
    g2, g3, g4, g5 = small["g_mix_post"], small["g_mlp_pre"], small["g_mlp_post"], small["g_ple"]
    sink_vec = sinks.reshape(SWA_HEADS)

    z, h1 = _fwd_in(x, g1, w_in_p, tm)
    qn, kvn, km, vm, qt, kt, vt, qs, ks, vs = _fwd_qkv(z, gq, gkv, wqb, wkn, wv, tab_m, tab_s, tb)
    om, lse_m = _mla_fwd(qt, km, vt, tb)
    os_, lse_s = _swa_fwd(sink_vec, qs, ks, vs)
    w = {**w, **late_weights((om, os_))}
    y, yo, au, bu, x1 = _fwd_mix(om, os_, z, x, w["w_mla_up"], w["w_swa_up"], w["w_out"], g2, tm)
    h2, u = _fwd_mlp_up(x1, g3, w["w_mlp_up"], tm)
    d, x2 = _fwd_mlp_down(u, w["w_mlp_down"], x1, g4, tm)
    loss, dx2, dgt, de0, dg5 = _ple_fwd_bwd(p, x2, tgt, w["w_ple"], g5, w["w_ple_gate"], tm)

    dd, da, dg4 = _bwd_mlp_down(dx2, d, g4, w["w_mlp_down"], u, tm)
    dx1, dg3 = _bwd_mlp_up(da, w["w_mlp_up"], x1, g3, dx2, tm)
    dyo, dg2, dau, dbu, dga, dgb, dos, delta_m, dom_t = _bwd_mix(dx1, yo, g2, w["w_out"], z, au, bu, w["w_mla_up"],
                                                                w["w_swa_up"], om, tb)
    gpk_late = lax.empty((N_CHIPS, PACK_ROWS["late"], D), BF16)
    for weight, a_, g_ in (("w_mla_up", om, dau), ("w_swa_up", os_, dbu), ("w_out", y, dyo), ("w_ple_gate", x2, dgt),
                           ("w_mlp_up", h2, da), ("w_mlp_down", u, dd)):
        gpk_late = _wgrad(a_, g_, "wgrad_" + weight[2:], into=(gpk_late, weight))
    token = late_grads_out(gpk_late)
    delta_m = delta_m + token[0, 0]
    dqm, dkm, dvm = _mla_bwd(qt, km, kt, vm, dom_t, lse_m, delta_m, tb)
    dqs, dkc, dkp, dvc, dvp, dsink = _swa_bwd(sink_vec, qs, ks, vs, dos, os_, lse_s)
    dqb, dknb, dvb, dsq, drest, dgq, dgkv = _bwd_qkv(dqm, dkm, dvm, dqs, dkc, dkp, dvc, dvp, z, gq, gkv, wqb, wkn, wv,
                                                      tab_m, tab_s)
    gx, dg1 = _bwd_in(dsq, dga, dgb, drest, w_in_p, x, g1, dx1, tm)

    g_in_p = [_wgrad(h1, dsq, "wgrad_in_sq"), _wgrad(h1, dga, "wgrad_in_ga"), _wgrad(h1, dgb, "wgrad_in_gb"),
              _wgrad(h1, drest, "wgrad_in_rest")]
    g_qb_p = _wgrad_t(dqb, qn, "wgrad_q_b").T
    g_kn_p = _wgrad_t(dknb, kvn, "wgrad_kv_b_nope").T
    g_v_p = _wgrad_t(dvb, kvn, "wgrad_kv_b_v").T
    grads = {
        "w_in_p": g_in_p,
        "w_q_b": g_qb_p.reshape(Q_LORA, MLA_HEADS, 128)[:, :, :96].reshape(Q_LORA, 1536),
        "w_kv_b": jnp.concatenate([g_kn_p.reshape(KV_LORA, MLA_HEADS, 128)[:, :, :64], g_v_p.reshape(KV_LORA, MLA_HEADS, 64)],
                                  axis=2).reshape(KV_LORA, 2048),
        "w_ple": _wgrad(p, de0, "wgrad_ple"),
    }
    small_grads = {"g_mix_pre": dg1, "g_q_a": dgq.reshape(1, Q_LORA), "g_kv_a": dgkv.reshape(1, KV_LORA), "sinks": dsink[0:1, 0:SWA_HEADS], "g_mix_post": dg2,
                   "g_mlp_pre": dg3, "g_mlp_post": dg4, "g_ple": dg5}
    return loss, gx, grads, small_grads


def _pack_small(vals, fill, scalar=None):
    wide = [vals[n] for n, k in SMALL if k == D]
    narrow = [vals[n] for n, k in SMALL if k != D]
    used = sum(k for _, k in SMALL if k != D)
    last = jnp.concatenate(narrow + [jnp.full((1, D - used), fill, F32)], axis=1)
    rest = jnp.full((2, D), fill, F32)
    if scalar is not None:
        rest = jnp.concatenate([jnp.concatenate([scalar, rest[0:1, 1:]], axis=1), rest[1:2]], axis=0)
    return jnp.concatenate(wide + [last, rest], axis=0)


def _unpack_small(pk):
    out, row, off = {}, 0, 0
    for n, k in SMALL:
        if k == D:
            out[n] = pk[row:row + 1]
            row += 1
    for n, k in SMALL:
        if k != D:
            out[n] = pk[5:6, off:off + k]
            off += k
    return out


def kernel(x, p, g_mix_pre, w_in, g_q_a, w_q_b, g_kv_a, w_kv_b, sinks, w_mla_up, w_swa_up, w_out, g_mix_post, g_mlp_pre, w_mlp_up, w_mlp_down, g_mlp_post, w_ple, g_ple, w_ple_gate, loss_target, m_g_mix_pre, m_w_in, m_g_q_a, m_w_q_b, m_g_kv_a, m_w_kv_b, m_sinks, m_w_mla_up, m_w_swa_up, m_w_out, m_g_mix_post, m_g_mlp_pre, m_w_mlp_up, m_w_mlp_down, m_g_mlp_post, m_w_ple, m_g_ple, m_w_ple_gate, v_g_mix_pre, v_w_in, v_g_q_a, v_w_q_b, v_g_kv_a, v_w_kv_b, v_sinks, v_w_mla_up, v_w_swa_up, v_w_out, v_g_mix_post, v_g_mlp_pre, v_w_mlp_up, v_w_mlp_down, v_g_mlp_post, v_w_ple, v_g_ple, v_w_ple_gate):
    given = dict(locals())
    big_w = {n: given[n][0] for n, _, _ in BIG}
    small_w = {n: given[n] for n, _ in SMALL}
    small_m = {n: given["m_" + n] for n, _ in SMALL}
    small_v = {n: given["v_" + n] for n, _ in SMALL}

    core = lax.axis_index("c")
    chip = 2 * lax.axis_index("x") + lax.axis_index("y")
    core_i = core.astype(jnp.int32).reshape(1)
    dev_i = jnp.stack([2 * chip + core, chip, core]).astype(jnp.int32)

    own_early = _pack_early(big_w, BF16)
    own_late = _pack_late(big_w, BF16)
    got_early = _all_gather(own_early)
    late_flight = _gather_late_start(own_late, got_early)
    weights = _full_weights(got_early, "early")
    step_small = {**small_w, "g_mix_pre": small_w["g_mix_pre"] + late_flight[4][0, 0]}

    def late_weights(after):
        return _full_weights(_gather_late_wait(*late_flight[:4], after), "late")

    flight = {}

    def late_grads_out(gpk_late):
        flight["late"] = _reduce_late_start(gpk_late, dev_i)
        return flight["late"][4]

    loss_blk, gx, grads, small_grads = _local_step(x[0], p[0, 0], loss_target[0], weights, step_small, late_weights,
                                                   late_grads_out)

    gpk = _split_full_grads(grads, _pack_early, BF16)
    got = _rs_sibling(gpk)
    part = _rs_add_sibling(core_i, gpk, got)
    small_own = _pack_small(small_grads, 0.0, loss_blk[0:1, 0:1])
    early_flight = _rs_chips_start(part, small_own, dev_i)

    out_g, out_d, out_m, out_v = {}, {}, {}, {}
    gpk_late, parts_late = _reduce_late_wait(*flight["late"][:4], early_flight[6])
    joined_late = _rs_join(_reduce_late_add(dev_i, gpk_late, parts_late), "rs_join_late")
    for n, _, _ in BIG:
        if PACK_AT[n][0] == "late":
            out_g[n], out_d[n], out_m[n], out_v[n] = _adamw(given[n], joined_late, given["m_" + n], given["v_" + n], n)

    part, parts, small_parts = _rs_chips_wait(*early_flight[:6], [out_d[n] for n in out_d])
    joined_early = _rs_join(_rs_add_chips(dev_i[1:3], part, parts), "rs_join_early")
    for n, _, _ in BIG:
        if PACK_AT[n][0] == "early":
            t = given[n].shape[2] % LANES != 0
            wmv = [jnp.swapaxes(a, 1, 2) if t else a for a in (given[n], given["m_" + n], given["v_" + n])]
            res = _adamw(wmv[0], joined_early, wmv[1], wmv[2], n, transposed=t)
            out_g[n], out_d[n], out_m[n], out_v[n] = [jnp.swapaxes(a, 1, 2) if t else a for a in res]

    mine = (lax.broadcasted_iota(jnp.int32, (N_DEV, 1, 1), 0) == dev_i[0])
    g_small_pk, d_small_pk, m_small_pk, v_small_pk = _adamw_small(
        _pack_small(small_w, 0.0), jnp.where(mine, small_own[None], small_parts), _pack_small(small_m, 0.0),
        _pack_small(small_v, 1.0))
    loss = g_small_pk[6, 0]
    for out, pk in ((out_g, g_small_pk), (out_d, d_small_pk), (out_m, m_small_pk), (out_v, v_small_pk)):
        out.update(_unpack_small(pk))
    order = ["g_mix_pre", "w_in", "g_q_a", "w_q_b", "g_kv_a", "w_kv_b", "sinks", "w_mla_up", "w_swa_up", "w_out", "g_mix_post",
             "g_mlp_pre", "w_mlp_up", "w_mlp_down", "g_mlp_post", "w_ple", "g_ple", "w_ple_gate"]
    return (loss, gx[None], *[out_g[n] for n in order], *[out_d[n] for n in order], *[out_m[n] for n in order],
            *[out_v[n] for n in order])
```

```python
import math

import jax
import jax.numpy as jnp
import numpy as np
from jax import lax
from jax.experimental import pallas as pl
from jax.experimental.pallas import tpu as pltpu

F32 = jnp.float32
BF16 = jnp.bfloat16
SDS = jax.ShapeDtypeStruct

D = 1024
D_FF = 4096
PLE = 256
Q_LORA = 256
KV_LORA = 128
MLA_HEADS = 16
MLA_NOPE = 64
MLA_ROPE = 32
SWA_HEADS = 16
SWA_HD = 64
WINDOW = 128
ROPE_THETA = 10000.0
EPS = 1e-6
NEG = -1e30
NZ = 4096
MLA_SCALE = (MLA_NOPE + MLA_ROPE) ** -0.5
LOG2_E = math.log2(math.e)
MLA_LOG2_SCALE = MLA_SCALE * LOG2_E
SWA_SCALE = SWA_HD ** -0.5

ADAM_LR = 0.001
ADAM_B1 = 0.9
ADAM_B2 = 0.999
ADAM_EPS = 1e-08
ADAM_WD = 0.01
ADAM_STEP = 10

LANES = 128
ATT_COLS = 128
VT_ROWS = 80
N_CHIPS = 4
N_DEV = 8
MESH = pl.DeviceIdType.MESH

NT = (((1,), (1,)), ((), ()))
TN = (((0,), (0,)), ((), ()))

BIG = (("w_in", 1024, 936), ("w_q_b", 256, 384), ("w_kv_b", 128, 512), ("w_mla_up", 256, 1024),
       ("w_swa_up", 256, 1024), ("w_out", 256, 1024), ("w_mlp_up", 1024, 1024), ("w_mlp_down", 1024, 1024),
       ("w_ple", 256, 256), ("w_ple_gate", 256, 1024))
COL_SHARDED = ("w_in", "w_q_b", "w_kv_b", "w_mlp_up", "w_ple")
PACK_AT = {"w_in": ("early", 0, 0), "w_q_b": ("early", 1024, 0), "w_ple": ("early", 1024, 384), "w_kv_b": ("early", 1280, 0),
           "w_mla_up": ("late", 0, 0), "w_swa_up": ("late", 256, 0), "w_out": ("late", 512, 0), "w_ple_gate": ("late", 768, 0),
           "w_mlp_up": ("late", 1024, 0), "w_mlp_down": ("late", 2048, 0)}
PACK_ROWS = {"early": 1408, "late": 3072}
SMALL = (("g_mix_pre", 1024), ("g_q_a", 256), ("g_kv_a", 128), ("sinks", 16), ("g_mix_post", 1024),
         ("g_mlp_pre", 1024), ("g_mlp_post", 1024), ("g_ple", 1024))


def _dot(a, b):
    return jnp.dot(a, b, preferred_element_type=F32)


def _dot_nt(a, b):
    return lax.dot_general(a, b, NT, preferred_element_type=F32)


def _dot_tn(a, b):
    return lax.dot_general(a, b, TN, preferred_element_type=F32)


def _pcall(body, *, name, out_shape, grid=(), in_specs=None, out_specs=None, scratch=(), sem=None, vmem_mb=48, aliases=None):
    params = dict(vmem_limit_bytes=vmem_mb << 20)
    if sem is not None:
        params["dimension_semantics"] = sem
    return pl.pallas_call(body, name=name, grid=grid, in_specs=in_specs, out_specs=out_specs, out_shape=out_shape,
                          scratch_shapes=list(scratch), input_output_aliases=aliases or {},
                          compiler_params=pltpu.CompilerParams(**params))


def _rows(tm, n, col=0):
    return pl.BlockSpec((tm, n), lambda i: (i, col))


def _full(shape):
    return pl.BlockSpec(shape, lambda i: (0,) * len(shape))


def _rms(x, g):
    r = lax.rsqrt(jnp.mean(x * x, axis=-1, keepdims=True) + EPS)
    return x * r * g


def _rms_bwd(dy, x, g):
    r = lax.rsqrt(jnp.mean(x * x, axis=-1, keepdims=True) + EPS)
    xn = x * r
    dn = dy * g
    dx = r * (dn - xn * jnp.mean(dn * xn, axis=-1, keepdims=True))
    return dx, jnp.sum(dy * xn, axis=0, keepdims=True)


def _sigmoid(x):
    return 1.0 / (1.0 + jnp.exp(-x))


def _rope(x, c, a, b, half):
    return x * c + pltpu.roll(x, LANES - half, 1) * a + pltpu.roll(x, half, 1) * b


def _rope_tables(T, kind):
    lane = np.arange(LANES)
    if kind == "mla":
        half = MLA_ROPE // 2
        rel = lane - MLA_NOPE
        on = (rel >= 0) & (rel < MLA_ROPE)
        d = MLA_ROPE
    else:
        half = SWA_HD // 2
        rel = lane % SWA_HD
        on = np.ones((LANES,), bool)
        d = SWA_HD
    first = on & (rel < half)
    second = on & (rel >= half)
    f = np.where(first, rel, rel - half).astype(np.float32)
    inv = np.exp(np.float32(-math.log(ROPE_THETA)) * f * np.float32(2.0 / d)).astype(np.float32)
    ang = (np.arange(T, dtype=np.float32)[:, None] * inv[None, :]).astype(np.float64)
    cos, sin = np.cos(ang).astype(np.float32), np.sin(ang).astype(np.float32)
    c = np.where(on[None], cos, np.float32(1.0))
    a = np.where(first[None], -sin, np.float32(0.0))
    b = np.where(second[None], sin, np.float32(0.0))
    return c, a, b


def _fwd_in(x, g1, w_in_p, tm):
    T = x.shape[0]

    def body(x_ref, g_ref, w_ref, z_ref, h_ref):
        h = _rms(x_ref[...], g_ref[...]).astype(BF16)
        h_ref[...] = h
        z_ref[...] = _dot(h, w_ref[...])

    return _pcall(body, name="fwd_in", grid=(T // tm,),
                  in_specs=[_rows(tm, D), _full((1, D)), _full((D, NZ))],
                  out_specs=[_rows(tm, NZ), _rows(tm, D)],
                  out_shape=[SDS((T, NZ), F32), SDS((T, D), BF16)], sem=("parallel",))(x, g1, w_in_p)


def _fwd_qkv(z, gq, gkv, wqb, wkn, wv, tab_m, tab_s, tm):
    T = z.shape[0]
    wqb_t, wkn_t, wv_t = wqb.T, wkn.T, wv.T
    tab_mt = [t.T for t in tab_m]

    def body(qa_ref, sq_ref, skd_ref, svd_ref, kva_ref, kr_ref, gq_ref, gkv_ref, wkn_ref, wv_ref, wqbt_ref, wknt_ref, wvt_ref,
             cm_ref, am_ref, bm_ref, cmt_ref, amt_ref, bmt_ref, cs_ref, as_ref, bs_ref,
             qn_ref, kvn_ref, km_ref, vm_ref, qt_ref, kt_ref, vt_ref, qs_ref, ks_ref, vs_ref):
        qn = _rms(qa_ref[...], gq_ref[...])
        qn_ref[...] = qn.astype(BF16)
        kvn = _rms(kva_ref[...], gkv_ref[...])
        kvn_b = kvn.astype(BF16)
        kvn_ref[...] = kvn_b
        qn_t = qn.T.astype(BF16)
        kvn_t = kvn.T.astype(BF16)
        cm, am, bm = cm_ref[...], am_ref[...], bm_ref[...]
        cmt, amt, bmt = cmt_ref[...], amt_ref[...], bmt_ref[...]
        cs, as_, bs = cs_ref[...], as_ref[...], bs_ref[...]
        k_rope = _rope(kr_ref[...], cm, am, bm, MLA_ROPE // 2)
        k_rope_t = k_rope.T
        half = MLA_ROPE // 2
        vm_ref[...] = _dot(kvn_b, wv_ref[...]).astype(BF16)
        km_all = _dot(kvn_b, wkn_ref[...])
        v_t = _dot(wvt_ref[...], kvn_t)
        q_t = _dot(wqbt_ref[...], qn_t)
        k_t = _dot(wknt_ref[...], kvn_t)
        ones_row = jnp.where(lax.broadcasted_iota(jnp.int32, (64, tm), 0) == 0, 1.0, 0.0)
        for h in range(MLA_HEADS):
            sl = slice(LANES * h, LANES * (h + 1))
            vt_ref[0, sl, :] = jnp.concatenate([v_t[64 * h:64 * (h + 1)], ones_row], axis=0).astype(BF16)
            qh = q_t[sl]
            qt_ref[0, sl, :] = (qh * cmt + pltpu.roll(qh, LANES - half, 0) * amt + pltpu.roll(qh, half, 0) * bmt).astype(BF16)
            km_ref[:, sl] = (km_all[:, sl] + k_rope).astype(BF16)
            kt_ref[0, sl, :] = (k_t[sl] + k_rope_t).astype(BF16)
        for j in range(D // LANES):
            sl = slice(LANES * j, LANES * (j + 1))
            qs_ref[:, sl] = _rope(sq_ref[:, sl], cs, as_, bs, SWA_HD // 2).astype(BF16)
        for j in range(2):
            sl = slice(LANES * j, LANES * (j + 1))
            ks_ref[:, sl] = _rope(skd_ref[:, sl], cs, as_, bs, SWA_HD // 2).astype(BF16)
        vs_ref[...] = svd_ref[...].astype(BF16)

    tab = [_rows(tm, LANES)] * 3
    tab_t = [pl.BlockSpec((LANES, tm), lambda i: (0, i))] * 3
    return _pcall(body, name="fwd_qkv", grid=(T // tm,),
                  in_specs=[_rows(tm, 256, 12), _rows(tm, 1024, 0), _rows(tm, 256, 13), _rows(tm, 256, 14),
                            _rows(tm, 128, 30), _rows(tm, 128, 31), _full((1, Q_LORA)), _full((1, KV_LORA)),
                            _full((KV_LORA, 2048)), _full((KV_LORA, 1024)), _full((2048, Q_LORA)), _full((2048, KV_LORA)),
                            _full((1024, KV_LORA))] + tab + tab_t + tab,
                  out_specs=[_rows(tm, Q_LORA), _rows(tm, KV_LORA), _rows(tm, 2048), _rows(tm, 1024),
                             pl.BlockSpec((1, 2048, tm), lambda i: (i, 0, 0)), pl.BlockSpec((1, 2048, tm), lambda i: (i, 0, 0)),
                             pl.BlockSpec((1, 2048, tm), lambda i: (i, 0, 0)),
                             _rows(tm, 1024), _rows(tm, 256), _rows(tm, 256)],
                  out_shape=[SDS((T, Q_LORA), BF16), SDS((T, KV_LORA), BF16), SDS((T, 2048), BF16),
                             SDS((T, 1024), BF16), SDS((T // tm, 2048, tm), BF16), SDS((T // tm, 2048, tm), BF16),
                             SDS((T // tm, 2048, tm), BF16),
                             SDS((T, 1024), BF16), SDS((T, 256), BF16), SDS((T, 256), BF16)],
                  sem=("parallel",))(z, z, z, z, z, z, gq, gkv, wkn, wv, wqb_t, wkn_t, wv_t, *tab_m, *tab_mt, *tab_s)


def _mla_fwd(qt, km, vt, tb):
    T = km.shape[0]
    nb = T // tb
    cc = ATT_COLS

    per = 2 if nb % 2 == 0 else 1

    def body(q_ref, k_ref, vt_ref, o_ref, l_ref, s_ref, p_ref, al_ref, m_ref, acc_ref):
        for blk in range(per):
            one_block(per * pl.program_id(1) + blk, blk, q_ref, k_ref, vt_ref, o_ref, l_ref, s_ref, p_ref, al_ref, m_ref, acc_ref)

    def one_block(i, blk, q_ref, k_ref, vt_ref, o_ref, l_ref, s_ref, p_ref, al_ref, m_ref, acc_ref):
        m_ref[...] = jnp.full(m_ref.shape, NEG, F32)
        acc_ref[...] = jnp.zeros_like(acc_ref)
        p_ref[1] = jnp.zeros(p_ref.shape[1:], BF16)
        al_ref[1] = jnp.ones(al_ref.shape[1:], F32)
        key = lax.broadcasted_iota(jnp.int32, (tb, cc), 0)
        qry = lax.broadcasted_iota(jnp.int32, (tb, cc), 1)

        def scores(j, slot):
            off = pl.multiple_of(j * tb, tb)
            for hh in range(2):
                sl = slice(LANES * hh, LANES * (hh + 1))
                s_ref[slot, hh] = _dot(k_ref[pl.ds(off, tb), sl], q_ref[blk, sl, :])

        def softmax(slot, diagonal):
            chains = [(hh, slice(cc * c, cc * (c + 1)), c) for hh in range(2) for c in range(tb // cc)]

            def scaled(hh, cols, c):
                t = s_ref[slot, hh, :, cols] * MLA_LOG2_SCALE
                return jnp.where(key <= qry + cc * c, t, NEG) if diagonal else t

            tops = []
            for hh, cols, c in chains:
                if diagonal:
                    top = jnp.max(scaled(hh, cols, c), axis=0, keepdims=True)
                else:
                    top = jnp.max(s_ref[slot, hh, :, cols], axis=0, keepdims=True) * MLA_LOG2_SCALE
                m_old = m_ref[hh, :, cols]
                mn = jnp.maximum(m_old, top)
                m_ref[hh, :, cols] = mn
                al_ref[slot, hh, :, cols] = jnp.exp2(m_old - mn)
                tops.append(mn)
            for (hh, cols, c), mn in zip(chains, tops):
                p_ref[slot, hh, :, cols] = jnp.exp2(scaled(hh, cols, c) - mn).astype(BF16)

        def accumulate(j, slot):
            for hh in range(2):
                acc_ref[hh] = al_ref[slot, hh] * acc_ref[hh] + _dot(vt_ref[j, LANES * hh:LANES * hh + VT_ROWS, :], p_ref[slot, hh])

        def step(t, carry):
            scores(2 * t + 1, 1)
            accumulate(jnp.maximum(2 * t - 1, 0), 1)
            softmax(0, False)
            scores(2 * t + 2, 0)
            accumulate(2 * t, 0)
            softmax(1, False)
            return carry

        scores(0, 0)
        lax.fori_loop(0, i // 2, step, 0)

        @pl.when(i % 2 == 1)
        def _():
            scores(i, 1)
            accumulate(jnp.maximum(i - 2, 0), 1)
            softmax(0, False)
            accumulate(i - 1, 0)
            softmax(1, True)
            accumulate(i, 1)

        @pl.when(i % 2 == 0)
        def _():
            accumulate(jnp.maximum(i - 1, 0), 1)
            softmax(0, True)
            accumulate(i, 0)
        den = [acc_ref[hh, 64:65, :] for hh in range(2)]
        o_ref[tb * blk:tb * (blk + 1), :] = jnp.concatenate([acc_ref[hh, 0:64, :] / den[hh] for hh in range(2)], axis=0).T
        sub = lax.broadcasted_iota(jnp.int32, (8, tb), 0)
        lse = [m_ref[hh] + jnp.log(den[hh]) * LOG2_E for hh in range(2)]
        l_ref[0, blk] = jnp.where(sub == 0, lse[0], jnp.where(sub == 1, lse[1], 0.0))

    return _pcall(body, name="mla_fwd", grid=(MLA_HEADS // 2, nb // per),
                  in_specs=[pl.BlockSpec((per, 256, tb), lambda p, i: (i, p, 0)), pl.BlockSpec((T, 256), lambda p, i: (0, p)),
                            pl.BlockSpec((nb, 2 * LANES, tb), lambda p, i: (0, p, 0))],
                  out_specs=[pl.BlockSpec((per * tb, LANES), lambda p, i: (i, p)),
                             pl.BlockSpec((1, per, 8, tb), lambda p, i: (p, i, 0, 0))],
                  out_shape=[SDS((T, D), F32), SDS((MLA_HEADS // 2, nb, 8, tb), F32)],
                  scratch=[pltpu.VMEM((2, 2, tb, tb), F32), pltpu.VMEM((2, 2, tb, tb), BF16), pltpu.VMEM((2, 2, 1, tb), F32),
                           pltpu.VMEM((2, 1, tb), F32), pltpu.VMEM((2, VT_ROWS, tb), F32)],
                  sem=("parallel", "arbitrary"))(qt, km, vt)


def _swa_mask(n):
    row = lax.broadcasted_iota(jnp.int32, (WINDOW, 2 * WINDOW), 0)
    col = lax.broadcasted_iota(jnp.int32, (WINDOW, 2 * WINDOW), 1)
    rel = row - col + WINDOW
    return (rel >= 0) & (rel < WINDOW) & ((col >= WINDOW) | (n > 0))


def _swa_specs(T):
    nb = T // WINDOW
    cur = lambda w: pl.BlockSpec((WINDOW, w), lambda n: (n, 0))
    prev = lambda w: pl.BlockSpec((WINDOW, w), lambda n: (jnp.maximum(n - 1, 0), 0))
    return nb, cur, prev


def _swa_fwd(sinks, qs, ks, vs):
    T = qs.shape[0]
    nb, cur, prev = _swa_specs(T)

    def body(sink_ref, q_ref, kc_ref, kp_ref, vc_ref, vp_ref, o_ref, l_ref, kb_ref, vb_ref, s_ref, p_ref):
        n = pl.program_id(0)
        mask = _swa_mask(n)
        lo = lax.broadcasted_iota(jnp.int32, (WINDOW, LANES), 1) < 64
        hi = jnp.logical_not(lo)
        for g in range(2):
            gs = slice(LANES * g, LANES * (g + 1))
            kb_ref[g] = jnp.concatenate([kp_ref[:, gs], kc_ref[:, gs]], axis=0)
            vb_ref[g] = jnp.concatenate([vp_ref[:, gs], vc_ref[:, gs]], axis=0)
        for h in range(SWA_HEADS):
            qp = q_ref[:, LANES * (h // 2):LANES * (h // 2 + 1)]
            qh = jnp.where(lo if h % 2 == 0 else hi, qp, jnp.zeros_like(qp))
            s_ref[h] = _dot_nt(qh, kb_ref[h // 8])
        for j in range(SWA_HEADS // 2):
            sl = slice(LANES * j, LANES * (j + 1))
            lses = []
            for h in (2 * j, 2 * j + 1):
                s = jnp.where(mask, s_ref[h] * SWA_SCALE, NEG)
                sk = sink_ref[h]
                m = jnp.maximum(jnp.max(s, axis=1, keepdims=True), sk)
                e = jnp.exp(s - m)
                den = jnp.sum(e, axis=1, keepdims=True) + jnp.exp(sk - m)
                p_ref[h] = (e / den).astype(BF16)
                lses.append(jnp.broadcast_to(m + jnp.log(den), (WINDOW, LANES)))
            l_ref[:, sl] = jnp.where(lo, lses[0], lses[1])
        for j in range(SWA_HEADS // 2):
            vb = vb_ref[j // 4]
            o_ref[:, LANES * j:LANES * (j + 1)] = jnp.where(lo, _dot(p_ref[2 * j], vb), _dot(p_ref[2 * j + 1], vb))

    return _pcall(body, name="swa_fwd", grid=(nb,),
                  in_specs=[pl.BlockSpec(memory_space=pltpu.SMEM), cur(D), cur(256), prev(256), cur(256), prev(256)],
                  out_specs=[cur(D), cur(D)], out_shape=[SDS((T, D), F32)] * 2,
                  scratch=[pltpu.VMEM((2, 2 * WINDOW, LANES), BF16), pltpu.VMEM((2, 2 * WINDOW, LANES), BF16),
                           pltpu.VMEM((SWA_HEADS, WINDOW, 2 * WINDOW), F32), pltpu.VMEM((SWA_HEADS, WINDOW, 2 * WINDOW), BF16)],
                  sem=("parallel",))(sinks, qs, ks, ks, vs, vs)


def _fwd_mix(om, os_, z, x, wmu, wsu, wo, g2, tm):
    T = x.shape[0]

    def body(om_ref, os_ref, ga_ref, gb_ref, x_ref, wmu_ref, wsu_ref, wo_ref, g2_ref,
             y_ref, yo_ref, au_ref, bu_ref, x1_ref):
        au = _dot(om_ref[...].astype(BF16), wmu_ref[...])
        bu = _dot(os_ref[...].astype(BF16), wsu_ref[...])
        au_ref[...] = au
        bu_ref[...] = bu
        y = (_sigmoid(ga_ref[...]) * au + _sigmoid(gb_ref[...]) * bu).astype(BF16)
        y_ref[...] = y
        yo = _dot(y, wo_ref[...])
        yo_ref[...] = yo
        x1_ref[...] = x_ref[...] + _rms(yo, g2_ref[...])

    r = _rows(tm, D)
    w = _full((D, D))
    return _pcall(body, name="fwd_mix", grid=(T // tm,),
                  in_specs=[r, r, _rows(tm, D, 1), _rows(tm, D, 2), r, w, w, w, _full((1, D))],
                  out_specs=[r] * 5,
                  out_shape=[SDS((T, D), BF16), SDS((T, D), F32), SDS((T, D), F32), SDS((T, D), F32), SDS((T, D), F32)],
                  sem=("parallel",))(om, os_, z, z, x, wmu, wsu, wo, g2)


def _fwd_mlp_up(x1, g3, w1, tm):
    T = x1.shape[0]

    def body(x_ref, g_ref, w_ref, h_ref, u_ref):
        h = _rms(x_ref[...], g_ref[...]).astype(BF16)
        h_ref[...] = h
        u_ref[...] = jnp.square(jnp.maximum(_dot(h, w_ref[...]), 0.0)).astype(BF16)

    return _pcall(body, name="fwd_mlp_up", grid=(T // tm,),
                  in_specs=[_rows(tm, D), _full((1, D)), _full((D, D_FF))],
                  out_specs=[_rows(tm, D), _rows(tm, D_FF)],
                  out_shape=[SDS((T, D), BF16), SDS((T, D_FF), BF16)],
                  sem=("parallel",))(x1, g3, w1)


def _fwd_mlp_down(u, w2, x1, g4, tm):
    T = x1.shape[0]

    def body(u_ref, w_ref, x_ref, g_ref, d_ref, x2_ref):
        d = _dot(u_ref[...], w_ref[...])
        d_ref[...] = d
        x2_ref[...] = x_ref[...] + _rms(d, g_ref[...])

    return _pcall(body, name="fwd_mlp_down", grid=(T // tm,),
                  in_specs=[_rows(tm, D_FF), _full((D_FF, D)), _rows(tm, D), _full((1, D))],
                  out_specs=[_rows(tm, D), _rows(tm, D)], out_shape=[SDS((T, D), F32)] * 2,
                  sem=("parallel",))(u, w2, x1, g4)


def _ple_fwd_bwd(p, x2, tgt, wple, g5, wpg, tm):
    T = x2.shape[0]

    def body(p_ref, x2_ref, t_ref, wple_ref, g5_ref, wpg_ref, loss_ref, dx2_ref, dgt_ref, de0_ref, dg5_ref):
        @pl.when(pl.program_id(0) == 0)
        def _():
            loss_ref[...] = jnp.zeros_like(loss_ref)
            dg5_ref[...] = jnp.zeros_like(dg5_ref)

        e0 = _dot(p_ref[...].astype(BF16), wple_ref[...])
        g5 = g5_ref[...]
        r = lax.rsqrt(jnp.mean(e0 * e0, axis=-1, keepdims=True) + EPS)
        en = e0 * r
        e = en * g5
        x2 = x2_ref[...]
        s = _sigmoid(_dot(x2.astype(BF16), wpg_ref[...]))
        diff = x2 + s * e - t_ref[...]
        sq = jnp.sum(jnp.sum(diff * diff, axis=1, keepdims=True), axis=0, keepdims=True)
        loss_ref[...] += jnp.broadcast_to(sq * (0.5 / D), loss_ref.shape)
        dx3 = diff * (1.0 / D)
        de = dx3 * s
        dgt = (dx3 * e * s * (1.0 - s)).astype(BF16)
        dgt_ref[...] = dgt
        dn = de * g5
        de0_ref[...] = (r * (dn - en * jnp.mean(dn * en, axis=-1, keepdims=True))).astype(BF16)
        dg5_ref[...] += jnp.sum(de * en, axis=0, keepdims=True)
        dx2_ref[...] = dx3 + _dot_nt(dgt, wpg_ref[...])

    r = _rows(tm, D)
    return _pcall(body, name="ple_fwd_bwd", grid=(T // tm,),
                  in_specs=[_rows(tm, PLE), r, r, _full((PLE, D)), _full((1, D)), _full((D, D))],
                  out_specs=[_full((8, LANES)), r, r, r, _full((1, D))],
                  out_shape=[SDS((8, LANES), F32), SDS((T, D), F32), SDS((T, D), BF16), SDS((T, D), BF16), SDS((1, D), F32)],
                  sem=("arbitrary",))(p, x2, tgt, wple, g5, wpg)


def _bwd_mlp_down(dx2, d, g4, w2, u, tm):
    T = dx2.shape[0]

    def body(dx_ref, d_ref, g_ref, w_ref, u_ref, dd_ref, da_ref, dg_ref):
        @pl.when(pl.program_id(0) == 0)
        def _():
            dg_ref[...] = jnp.zeros_like(dg_ref)

        dd, dg = _rms_bwd(dx_ref[...], d_ref[...], g_ref[...])
        dg_ref[...] += dg
        ddb = dd.astype(BF16)
        dd_ref[...] = ddb
        du = _dot_nt(ddb, w_ref[...])
        da_ref[...] = (du * (2.0 * jnp.sqrt(u_ref[...].astype(F32)))).astype(BF16)

    return _pcall(body, name="bwd_mlp_down", grid=(T // tm,),
                  in_specs=[_rows(tm, D), _rows(tm, D), _full((1, D)), _full((D_FF, D)), _rows(tm, D_FF)],
                  out_specs=[_rows(tm, D), _rows(tm, D_FF), _full((1, D))],
                  out_shape=[SDS((T, D), BF16), SDS((T, D_FF), BF16), SDS((1, D), F32)],
                  sem=("arbitrary",))(dx2, d, g4, w2, u)


def _bwd_mlp_up(da, w1, x1, g3, dx2, tm):
    T = dx2.shape[0]

    def body(da_ref, w_ref, x_ref, g_ref, dx2_ref, dx1_ref, dg_ref):
        @pl.when(pl.program_id(0) == 0)
        def _():
            dg_ref[...] = jnp.zeros_like(dg_ref)

        dh = _dot_nt(da_ref[...], w_ref[...])
        dx, dg = _rms_bwd(dh, x_ref[...], g_ref[...])
        dg_ref[...] += dg
        dx1_ref[...] = dx2_ref[...] + dx

    return _pcall(body, name="bwd_mlp_up", grid=(T // tm,),
                  in_specs=[_rows(tm, D_FF), _full((D, D_FF)), _rows(tm, D), _full((1, D)), _rows(tm, D)],
                  out_specs=[_rows(tm, D), _full((1, D))],
                  out_shape=[SDS((T, D), F32), SDS((1, D), F32)], sem=("arbitrary",))(da, w1, x1, g3, dx2)


def _bwd_mix(dx1, yo, g2, wo, z, au, bu, wmu, wsu, om, tm):
    T = dx1.shape[0]

    def body(dx_ref, yo_ref, g_ref, wo_ref, ga_ref, gb_ref, au_ref, bu_ref, wmu_ref, wsu_ref, om_ref,
             dyo_ref, dg_ref, dau_ref, dbu_ref, dga_ref, dgb_ref, dos_ref, dl_ref, dot_ref):
        @pl.when(pl.program_id(0) == 0)
        def _():
            dg_ref[...] = jnp.zeros_like(dg_ref)

        dyo, dg = _rms_bwd(dx_ref[...], yo_ref[...], g_ref[...])
        dg_ref[...] += dg
        dyob = dyo.astype(BF16)
        dyo_ref[...] = dyob
        dy = _dot_nt(dyob, wo_ref[...])
        sa = _sigmoid(ga_ref[...])
        sb = _sigmoid(gb_ref[...])
        dau = (dy * sa).astype(BF16)
        dbu = (dy * sb).astype(BF16)
        dau_ref[...] = dau
        dbu_ref[...] = dbu
        dga_ref[...] = (dy * au_ref[...] * sa * (1.0 - sa)).astype(BF16)
        dgb_ref[...] = (dy * bu_ref[...] * sb * (1.0 - sb)).astype(BF16)
        dom = _dot_nt(dau, wmu_ref[...])
        dos_ref[...] = _dot_nt(dbu, wsu_ref[...])
        prod = dom * om_ref[...]
        sub = lax.broadcasted_iota(jnp.int32, (8, tm), 0)
        for pr in range(MLA_HEADS // 2):
            sl = slice(LANES * pr, LANES * (pr + 1))
            pt = prod[:, sl].T
            d0 = jnp.sum(pt[0:64], axis=0, keepdims=True)
            d1 = jnp.sum(pt[64:128], axis=0, keepdims=True)
            dl_ref[pr, 0] = jnp.where(sub == 0, d0, jnp.where(sub == 1, d1, 0.0))
            dot_ref[0, sl, :] = dom[:, sl].T.astype(BF16)

    r = _rows(tm, D)
    w = _full((D, D))
    return _pcall(body, name="bwd_mix", grid=(T // tm,),
                  in_specs=[r, r, _full((1, D)), w, _rows(tm, D, 1), _rows(tm, D, 2), r, r, w, w, r],
                  out_specs=[r, _full((1, D)), r, r, r, r, r, pl.BlockSpec((MLA_HEADS // 2, 1, 8, tm), lambda i: (0, i, 0, 0)),
                             pl.BlockSpec((1, D, tm), lambda i: (i, 0, 0))],
                  out_shape=[SDS((T, D), BF16), SDS((1, D), F32), SDS((T, D), BF16), SDS((T, D), BF16), SDS((T, D), BF16),
                             SDS((T, D), BF16), SDS((T, D), F32), SDS((MLA_HEADS // 2, T // tm, 8, tm), F32),
                             SDS((T // tm, D, tm), BF16)],
                  sem=("arbitrary",))(dx1, yo, g2, wo, z, z, au, bu, wmu, wsu, om)


def _mla_bwd(qt, km, kt, vm, dot, lse, delta, tb):
    T = km.shape[0]
    nb = T // tb
    cc = ATT_COLS

    per = 2 if nb % 2 == 0 else 1

    def body(qt_ref, k_ref, kt_ref, v_ref, dot_ref, l_ref, dl_ref, dqt_ref, dkt_ref, dvt_ref,
             s_ref, dp_ref, p_ref, ds_ref, vh_ref):
        @pl.when(pl.program_id(1) == 0)
        def _():
            dqt_ref[...] = jnp.zeros_like(dqt_ref)

        dkt_ref[...] = jnp.zeros_like(dkt_ref)
        dvt_ref[...] = jnp.zeros_like(dvt_ref)
        for blk in range(per):
            one_block(per * pl.program_id(1) + blk, blk, qt_ref, k_ref, kt_ref, v_ref, dot_ref, l_ref, dl_ref, dqt_ref, dkt_ref,
                      dvt_ref, s_ref, dp_ref, p_ref, ds_ref, vh_ref)

    def one_block(j, blk, qt_ref, k_ref, kt_ref, v_ref, dot_ref, l_ref, dl_ref, dqt_ref, dkt_ref, dvt_ref,
                  s_ref, dp_ref, p_ref, ds_ref, vh_ref):
        lo = lax.broadcasted_iota(jnp.int32, (tb, LANES), 1) < 64
        key = lax.broadcasted_iota(jnp.int32, (tb, cc), 0)
        qry = lax.broadcasted_iota(jnp.int32, (tb, cc), 1)
        rows_j = slice(tb * blk, tb * (blk + 1))
        v = v_ref[rows_j, :]
        vh_ref[0] = jnp.where(lo, v, jnp.zeros_like(v))
        vh_ref[1] = jnp.where(lo, jnp.zeros_like(v), v)

        def scores(i, slot):
            for hh in range(2):
                sl = slice(LANES * hh, LANES * (hh + 1))
                s_ref[slot, hh] = _dot(k_ref[rows_j, sl], qt_ref[i, sl, :])
                dp_ref[slot, hh] = _dot(vh_ref[hh], dot_ref[i])

        def grads(i, slot, diagonal):
            lse_i = l_ref[0, i]
            delta_i = dl_ref[0, i]
            for hh in range(2):
                for c in range(tb // cc):
                    cols = slice(cc * c, cc * (c + 1))
                    p = jnp.exp2(s_ref[slot, hh, :, cols] * MLA_LOG2_SCALE - lse_i[hh:hh + 1, cols])
                    if diagonal:
                        p = jnp.where(key <= qry + cc * c, p, 0.0)
                    p_ref[hh, :, cols] = p.astype(BF16)
                    ds_ref[hh, :, cols] = (p * (dp_ref[slot, hh, :, cols] - delta_i[hh:hh + 1, cols]) * MLA_SCALE).astype(BF16)
            for hh in range(2):
                sl = slice(LANES * hh, LANES * (hh + 1))
                half = slice(64 * hh, 64 * (hh + 1))
                dvt_ref[blk, half, :] += _dot_nt(dot_ref[i, half, :], p_ref[hh])
                real = slice(LANES * hh, LANES * hh + MLA_NOPE + MLA_ROPE)
                dkt_ref[blk, real, :] += _dot_nt(qt_ref[i, real, :], ds_ref[hh])
                dqt_ref[i, real, :] += _dot(kt_ref[blk, real, :], ds_ref[hh])

        n_off = nb - 1 - j

        def step(u, carry):
            i0 = j + 1 + 2 * u
            scores(i0 + 1, 1)
            grads(i0, 0, False)
            scores(jnp.where(i0 + 2 < nb, i0 + 2, j), 0)
            grads(i0 + 1, 1, False)
            return carry

        scores(jnp.where(n_off > 0, j + 1, j), 0)
        lax.fori_loop(0, n_off // 2, step, 0)

        @pl.when(n_off % 2 == 1)
        def _():
            scores(j, 1)
            grads(nb - 1, 0, False)
            grads(j, 1, True)

        @pl.when(n_off % 2 == 0)
        def _():
            grads(j, 0, True)

    blk = lambda w: pl.BlockSpec((per * tb, w), lambda p, j: (j, p))
    stat = pl.BlockSpec((1, nb, 8, tb), lambda p, j: (p, 0, 0, 0))
    pair_t = lambda w: pl.BlockSpec((nb, w, tb), lambda p, j: (0, p, 0))
    blk_t = lambda w: pl.BlockSpec((per, w, tb), lambda p, j: (j, p, 0))
    return _pcall(body, name="mla_bwd", grid=(MLA_HEADS // 2, nb // per),
                  in_specs=[pair_t(256), blk(256), blk_t(256), blk(LANES), pair_t(LANES), stat, stat],
                  out_specs=[pair_t(256), blk_t(256), blk_t(LANES)],
                  out_shape=[SDS((nb, 2048, tb), F32), SDS((nb, 2048, tb), F32), SDS((nb, D, tb), F32)],
                  scratch=[pltpu.VMEM((2, 2, tb, tb), F32), pltpu.VMEM((2, 2, tb, tb), F32), pltpu.VMEM((2, tb, tb), BF16),
                           pltpu.VMEM((2, tb, tb), BF16), pltpu.VMEM((2, tb, LANES), BF16)],
                  sem=("parallel", "arbitrary"))(qt, km, kt, vm, dot, lse, delta)


def _swa_bwd(sinks, qs, ks, vs, do, o, lse):
    T = qs.shape[0]
    nb, cur, prev = _swa_specs(T)

    def body(sink_ref, q_ref, kc_ref, kp_ref, vc_ref, vp_ref, do_ref, o_ref, l_ref,
             dq_ref, dkc_ref, dkp_ref, dvc_ref, dvp_ref, dsink_ref, kb_ref, vb_ref, s_ref, dp_ref, p_ref, ds_ref):
        n = pl.program_id(0)

        @pl.when(n == 0)
        def _():
            dsink_ref[...] = jnp.zeros_like(dsink_ref)

        mask = _swa_mask(n)
        lo = lax.broadcasted_iota(jnp.int32, (WINDOW, LANES), 1) < 64
        hi = jnp.logical_not(lo)
        lane8 = lax.broadcasted_iota(jnp.int32, (8, LANES), 1)
        for g in range(2):
            gs = slice(LANES * g, LANES * (g + 1))
            kb_ref[g] = jnp.concatenate([kp_ref[:, gs], kc_ref[:, gs]], axis=0)
            vb_ref[g] = jnp.concatenate([vp_ref[:, gs], vc_ref[:, gs]], axis=0)

        def head(h):
            sl = slice(LANES * (h // 2), LANES * (h // 2 + 1))
            hm = lo if h % 2 == 0 else hi
            qp = q_ref[:, sl]
            return hm, sl, jnp.where(hm, qp, jnp.zeros_like(qp)), jnp.where(hm, do_ref[:, sl], 0.0).astype(BF16)

        for h in range(SWA_HEADS):
            _, _, qh, dom = head(h)
            s_ref[h] = _dot_nt(qh, kb_ref[h // 8])
            dp_ref[h] = _dot_nt(dom, vb_ref[h // 8])
        dsink = jnp.zeros((8, LANES), F32)
        for h in range(SWA_HEADS):
            hm, sl, _, _ = head(h)
            lse_h = jnp.max(jnp.where(hm, l_ref[:, sl], -jnp.inf), axis=1, keepdims=True)
            delta = jnp.sum(jnp.where(hm, do_ref[:, sl] * o_ref[:, sl], 0.0), axis=1, keepdims=True)
            p = jnp.exp(jnp.where(mask, s_ref[h] * SWA_SCALE, NEG) - lse_h)
            p_ref[h] = p.astype(BF16)
            ds_ref[h] = (p * (dp_ref[h] - delta) * SWA_SCALE).astype(BF16)
            d_sink = -jnp.sum(jnp.exp(sink_ref[h] - lse_h) * delta, axis=0, keepdims=True)
            dsink = dsink + jnp.where(lane8 == h, d_sink, 0.0)
        dsink_ref[...] += dsink
        for g in range(2):
            gs = slice(LANES * g, LANES * (g + 1))
            dkb = jnp.zeros((2 * WINDOW, LANES), F32)
            dvb = jnp.zeros((2 * WINDOW, LANES), F32)
            for j in range(4 * g, 4 * g + 4):
                dqs = []
                for h in (2 * j, 2 * j + 1):
                    _, _, qh, dom = head(h)
                    dvb = dvb + _dot_tn(p_ref[h], dom)
                    dkb = dkb + _dot_tn(ds_ref[h], qh)
                    dqs.append(_dot(ds_ref[h], kb_ref[g]))
                dq_ref[:, LANES * j:LANES * (j + 1)] = jnp.where(lo, dqs[0], dqs[1])
            dkp_ref[:, gs] = dkb[:WINDOW]
            dkc_ref[:, gs] = dkb[WINDOW:]
            dvp_ref[:, gs] = dvb[:WINDOW]
            dvc_ref[:, gs] = dvb[WINDOW:]

    band = pltpu.VMEM((2, 2 * WINDOW, LANES), BF16)
    return _pcall(body, name="swa_bwd", grid=(nb,),
                  in_specs=[pl.BlockSpec(memory_space=pltpu.SMEM), cur(D), cur(256), prev(256), cur(256), prev(256),
                            cur(D), cur(D), cur(D)],
                  out_specs=[cur(D), cur(256), cur(256), cur(256), cur(256), _full((8, LANES))],
                  out_shape=[SDS((T, D), F32), SDS((T, 256), F32), SDS((T, 256), F32), SDS((T, 256), F32), SDS((T, 256), F32),
                             SDS((8, LANES), F32)],
                  scratch=[band, band, pltpu.VMEM((SWA_HEADS, WINDOW, 2 * WINDOW), F32),
                           pltpu.VMEM((SWA_HEADS, WINDOW, 2 * WINDOW), F32), pltpu.VMEM((SWA_HEADS, WINDOW, 2 * WINDOW), BF16),
                           pltpu.VMEM((SWA_HEADS, WINDOW, 2 * WINDOW), BF16)],
                  sem=("arbitrary",))(sinks, qs, ks, ks, vs, vs, do, o, lse)


def _bwd_qkv(dqm, dkm, dvm, dqs, dkc, dkp, dvc, dvp, z, gq, gkv, wqb, wkn, wv, tab_m, tab_s):
    T = z.shape[0]
    tm = WINDOW
    nb = T // tm
    per = dqm.shape[2] // tm

    tab_mt = [t.T for t in tab_m]
    half = MLA_ROPE // 2

    def rope_t(v, c, a, b):
        return v * c + pltpu.roll(v, LANES - half, 0) * a + pltpu.roll(v, half, 0) * b

    def rms_bwd_t(dy, x, g):
        r = lax.rsqrt(jnp.mean(x * x, axis=0, keepdims=True) + EPS)
        xn = x * r
        dn = dy * g
        return r * (dn - xn * jnp.mean(dn * xn, axis=0, keepdims=True)), jnp.sum(dy * xn, axis=1, keepdims=True)

    def body(dqm_ref, dkm_ref, dvm_ref, dqs_ref, dkc_ref, dkp_ref, dvc_ref, dvp_ref, qa_ref, kva_ref, gq_ref, gkv_ref,
             wqb_ref, wkn_ref, wv_ref, cmt_ref, amt_ref, bmt_ref, cs_ref, as_ref, bs_ref,
             dq_out, dkn_out, dv_out, dsq_ref, drest_ref, dgq_ref, dgkv_ref):
        i = pl.program_id(0)

        @pl.when(i == 0)
        def _():
            dgq_ref[...] = jnp.zeros_like(dgq_ref)
            dgkv_ref[...] = jnp.zeros_like(dgkv_ref)

        cmt, amt, bmt = cmt_ref[...], -amt_ref[...], -bmt_ref[...]
        cs, as_, bs = cs_ref[...], -as_ref[...], -bs_ref[...]
        row = lax.broadcasted_iota(jnp.int32, (LANES, tm), 0)
        nope = row < MLA_NOPE
        roped = jnp.logical_and(row >= MLA_NOPE, row < MLA_NOPE + MLA_ROPE)
        dkr = jnp.zeros((LANES, tm), F32)
        for h in range(MLA_HEADS):
            sl = slice(LANES * h, LANES * (h + 1))
            dq_out[0, sl, :] = rope_t(dqm_ref[0, sl, :], cmt, amt, bmt).astype(BF16)
            dk_h = dkm_ref[0, sl, :]
            dkn_out[0, sl, :] = jnp.where(nope, dk_h, 0.0).astype(BF16)
            dkr = dkr + jnp.where(roped, dk_h, 0.0)
        dv_out[0] = dvm_ref[0].astype(BF16)
        dqn = _dot(wqb_ref[...], dq_out[0])
        dkvn = _dot(wkn_ref[...], dkn_out[0]) + _dot(wv_ref[...], dv_out[0])
        dqa, dgq = rms_bwd_t(dqn, qa_ref[...].T, gq_ref[...])
        dkva, dgkv = rms_bwd_t(dkvn, kva_ref[...].T, gkv_ref[...])
        dgq_ref[...] += dgq
        dgkv_ref[...] += dgkv
        for j in range(D // LANES):
            sl = slice(LANES * j, LANES * (j + 1))
            dsq_ref[:, sl] = _rope(dqs_ref[:, sl], cs, as_, bs, SWA_HD // 2).astype(BF16)
        keep = (i < nb - 1).astype(F32)
        drest_ref[:, 0:256] = dqa.T.astype(BF16)
        for j in range(2):
            sl = slice(LANES * j, LANES * (j + 1))
            dk = dkc_ref[:, sl] + keep * dkp_ref[:, sl]
            drest_ref[:, 256 + LANES * j:256 + LANES * (j + 1)] = _rope(dk, cs, as_, bs, SWA_HD // 2).astype(BF16)
        drest_ref[:, 512:768] = (dvc_ref[...] + keep * dvp_ref[...]).astype(BF16)
        drest_ref[:, 768:896] = dkva.T.astype(BF16)
        drest_ref[:, 896:1024] = rope_t(dkr, cmt, amt, bmt).T.astype(BF16)

    nxt = pl.BlockSpec((tm, 256), lambda i: (jnp.minimum(i + 1, nb - 1), 0))
    tab = [_rows(tm, LANES)] * 3
    tab_t = [pl.BlockSpec((LANES, tm), lambda i: (0, i))] * 3
    blk_t = lambda w: pl.BlockSpec((1, w, tm), lambda i: (i // per, 0, i % per))
    return _pcall(body, name="bwd_qkv", grid=(nb,),
                  in_specs=[blk_t(2048), blk_t(2048), blk_t(1024), _rows(tm, 1024), _rows(tm, 256), nxt,
                            _rows(tm, 256), nxt, _rows(tm, 256, 12), _rows(tm, 128, 30), _full((Q_LORA, 1)), _full((KV_LORA, 1)),
                            _full((Q_LORA, 2048)), _full((KV_LORA, 2048)), _full((KV_LORA, 1024))] + tab_t + tab,
                  out_specs=[blk_t(2048), blk_t(2048), blk_t(1024), _rows(tm, 1024), _rows(tm, 1024),
                             _full((Q_LORA, 1)), _full((KV_LORA, 1))],
                  out_shape=[SDS(dqm.shape, BF16), SDS(dkm.shape, BF16), SDS(dvm.shape, BF16), SDS((T, 1024), BF16),
                             SDS((T, 1024), BF16), SDS((Q_LORA, 1), F32), SDS((KV_LORA, 1), F32)],
                  sem=("arbitrary",))(dqm, dkm, dvm, dqs, dkc, dkp, dvc, dvp, z, z, gq.reshape(Q_LORA, 1),
                                      gkv.reshape(KV_LORA, 1), wqb, wkn, wv, *tab_mt, *tab_s)


def _bwd_in(dsq, dga, dgb, drest, w_in_p, x, g1, dx1, tm):
    T = x.shape[0]

    def body(a_ref, b_ref, c_ref, d_ref, w_ref, x_ref, g_ref, dx1_ref, dx_ref, dg_ref):
        @pl.when(pl.program_id(0) == 0)
        def _():
            dg_ref[...] = jnp.zeros_like(dg_ref)

        dh = (_dot_nt(a_ref[...], w_ref[:, 0:1024]) + _dot_nt(b_ref[...], w_ref[:, 1024:2048])
              + _dot_nt(c_ref[...], w_ref[:, 2048:3072]) + _dot_nt(d_ref[...], w_ref[:, 3072:4096]))
        dx, dg = _rms_bwd(dh, x_ref[...], g_ref[...])
        dg_ref[...] += dg
        dx_ref[...] = dx1_ref[...] + dx

    r = _rows(tm, D)
    return _pcall(body, name="bwd_in", grid=(T // tm,),
                  in_specs=[r, r, r, r, _full((D, NZ)), r, _full((1, D)), r],
                  out_specs=[r, _full((1, D))], out_shape=[SDS((T, D), F32), SDS((1, D), F32)],
                  sem=("arbitrary",))(dsq, dga, dgb, drest, w_in_p, x, g1, dx1)


def _wgrad(a, g, name, into=None):
    T, K = a.shape
    N = g.shape[1]
    tk, tn, tt = min(K, 1024), min(N, 1024), min(T, 1024)
    if into is not None:
        buf, weight = into
        _, row0, lane0 = PACK_AT[weight]
        shard = {n: (r, c) for n, r, c in BIG}[weight]
        assert lane0 == 0 and shard[1] == D and tk % shard[0] == 0
        per_step = tk // shard[0]
    assert K % tk == 0 and N % tn == 0 and T % tt == 0, (a.shape, g.shape)
    steps = T // tt

    def body(a_ref, g_ref, *rest):
        o_ref, acc_ref = rest[-2:]
        t = pl.program_id(2)

        @pl.when(t == 0)
        def _():
            acc_ref[...] = jnp.zeros_like(acc_ref)

        acc_ref[...] += _dot_tn(a_ref[...].astype(BF16), g_ref[...].astype(BF16))

        @pl.when(t == steps - 1)
        def _():
            o_ref[...] = acc_ref[...].astype(o_ref.dtype).reshape(o_ref.shape)

    in_specs = [pl.BlockSpec((tt, tk), lambda k, n, t: (t, k)), pl.BlockSpec((tt, tn), lambda k, n, t: (t, n))]
    if into is None:
        return _pcall(body, name=name, grid=(K // tk, N // tn, steps), in_specs=in_specs,
                      out_specs=pl.BlockSpec((tk, tn), lambda k, n, t: (k, n)), out_shape=SDS((K, N), F32),
                      scratch=[pltpu.VMEM((tk, tn), F32)], sem=("parallel", "parallel", "arbitrary"))(a, g)
    assert row0 % shard[0] == 0 and (K // tk) * (N // tn) * per_step == N_CHIPS
    return _pcall(body, name=name, grid=(K // tk, N // tn, steps), in_specs=in_specs + [ANY],
                  out_specs=pl.BlockSpec((per_step, shard[0], tn), lambda k, n, t: (k + n, row0 // shard[0], 0)),
                  out_shape=SDS(buf.shape, buf.dtype),
                  scratch=[pltpu.VMEM((tk, tn), F32)], sem=("parallel", "parallel", "arbitrary"), aliases={2: 0})(a, g, buf)


def _wgrad_t(at, g, name):
    nblk, K, tt = at.shape
    N = g.shape[1]
    tk = min(K, 1024)
    per_step = 4 if nblk % 4 == 0 else 1
    assert K % tk == 0 and g.shape[0] == nblk * tt

    def body(a_ref, g_ref, o_ref):
        @pl.when(pl.program_id(1) == 0)
        def _():
            o_ref[...] = jnp.zeros_like(o_ref)

        acc = _dot(a_ref[0], g_ref[0:tt, :].astype(BF16))
        for b in range(1, per_step):
            acc = acc + _dot(a_ref[b], g_ref[tt * b:tt * (b + 1), :].astype(BF16))
        o_ref[...] += acc

    return _pcall(body, name=name, grid=(K // tk, nblk // per_step),
                  in_specs=[pl.BlockSpec((per_step, tk, tt), lambda k, t: (t, k, 0)),
                            pl.BlockSpec((per_step * tt, N), lambda k, t: (t, 0))],
                  out_specs=pl.BlockSpec((tk, N), lambda k, t: (k, 0)), out_shape=SDS((K, N), F32),
                  sem=("parallel", "arbitrary"))(at, g)


def _adamw(w, packed_g, m, v, name, transposed=False):
    R, C = w.shape[1:][::-1] if transposed else w.shape[1:]
    _, row0, lane0 = PACK_AT[name]
    tr = min(R, 256 if row0 % 256 == 0 else 128)
    assert row0 % tr == 0 and R % tr == 0

    def body(w_ref, g_ref, m_ref, v_ref, go_ref, d_ref, m2_ref, v2_ref):
        g_ = g_ref[...].T[lane0:lane0 + C] if transposed else g_ref[:, lane0:lane0 + C]
        go_ref[0] = g_
        m2 = ADAM_B1 * m_ref[0] + (1.0 - ADAM_B1) * g_
        v2 = ADAM_B2 * v_ref[0] + (1.0 - ADAM_B2) * jnp.square(g_)
        m_hat = m2 / (1.0 - ADAM_B1 ** ADAM_STEP)
        v_hat = v2 / (1.0 - ADAM_B2 ** ADAM_STEP)
        d_ref[0] = -ADAM_LR * (m_hat / (jnp.sqrt(v_hat) + ADAM_EPS) + ADAM_WD * w_ref[0])
        m2_ref[0] = m2
        v2_ref[0] = v2

    r = pl.BlockSpec((1, C, tr), lambda i: (0, 0, i)) if transposed else pl.BlockSpec((1, tr, C), lambda i: (0, i, 0))
    return _pcall(body, name="adamw_" + name, grid=(R // tr,),
                  in_specs=[r, pl.BlockSpec((tr, D), lambda i: (row0 // tr + i, 0)), r, r], out_specs=[r] * 4,
                  out_shape=[SDS(w.shape, F32)] * 4, sem=("parallel",))(w, packed_g, m, v)


def _adamw_small(w, parts, m, v):
    def body(w_ref, p_ref, m_ref, v_ref, g_ref, d_ref, m2_ref, v2_ref):
        g_ = p_ref[0]
        for k in range(1, N_DEV):
            g_ = g_ + p_ref[k]
        g_ref[...] = g_
        m2 = ADAM_B1 * m_ref[...] + (1.0 - ADAM_B1) * g_
        v2 = ADAM_B2 * v_ref[...] + (1.0 - ADAM_B2) * jnp.square(g_)
        m_hat = m2 / (1.0 - ADAM_B1 ** ADAM_STEP)
        v_hat = v2 / (1.0 - ADAM_B2 ** ADAM_STEP)
        d_ref[...] = -ADAM_LR * (m_hat / (jnp.sqrt(v_hat) + ADAM_EPS) + ADAM_WD * w_ref[...])
        m2_ref[...] = m2
        v2_ref[...] = v2

    s = _full((8, D))
    return _pcall(body, name="adamw_small", grid=(1,), in_specs=[s, _full((N_DEV, 8, D)), s, s], out_specs=[s] * 4,
                  out_shape=[SDS((8, D), F32)] * 4, sem=("arbitrary",))(w, parts, m, v)


ANY = pl.BlockSpec(memory_space=pl.ANY)


def _place():
    x, y, c = lax.axis_index("x"), lax.axis_index("y"), lax.axis_index("c")
    chips = [(1 - x, y), (x, 1 - y), (1 - x, 1 - y)]
    return x, y, c, chips


def _all_gather(wpk):
    rows = wpk.shape[0]
    HALF = rows // 2
    assert HALF % 16 == 0

    def body(in_ref, out_ref, send_sems, recv_sems):
        x, y, c, chips = _place()
        half = pl.ds(pl.multiple_of(c * HALF, 16), HALF)
        other = pl.ds(pl.multiple_of((1 - c) * HALF, 16), HALF)

        def copy(k, src, dst, to):
            return pltpu.make_async_remote_copy(src_ref=src, dst_ref=dst, send_sem=send_sems.at[k], recv_sem=recv_sems.at[k],
                                                device_id=to, device_id_type=MESH)

        first = [copy(k, in_ref.at[half], out_ref.at[2 * x + y, half], (cx, cy, c)) for k, (cx, cy) in enumerate(chips)]
        first.append(copy(6, in_ref, out_ref.at[2 * x + y], (x, y, 1 - c)))
        for cp in first:
            cp.start()
        passed = []
        for k, (cx, cy) in enumerate(chips):
            slot = out_ref.at[2 * cx + cy, half]
            copy(k, slot, slot, (x, y, c)).wait_recv()
            fwd = copy(3 + k, slot, slot, (x, y, 1 - c))
            fwd.start()
            passed.append(fwd)
        for k, (cx, cy) in enumerate(chips):
            slot = out_ref.at[2 * cx + cy, other]
            copy(3 + k, slot, slot, (x, y, c)).wait_recv()
        copy(6, in_ref, out_ref.at[2 * x + y], (x, y, c)).wait_recv()
        for cp in first + passed:
            cp.wait_send()

    return _pcall(body, name="all_gather_weights", in_specs=[ANY], out_specs=ANY,
                  out_shape=SDS((N_CHIPS, rows, D), BF16),
                  scratch=[pltpu.SemaphoreType.DMA((7,)), pltpu.SemaphoreType.DMA((7,))])(wpk)


HBM = pl.BlockSpec(memory_space=pltpu.HBM)
SEM = pl.BlockSpec(memory_space=pltpu.SEMAPHORE)
DATAFLOW = pltpu.SideEffectType.DATAFLOW_SIDE_EFFECTING


def _in_hbm(a):
    return pltpu.with_memory_space_constraint(a, pltpu.HBM)


def _gather_late_start(wpk, after):
    rows = wpk.shape[0]

    def body(in_ref, land_ref, after_ref, send_sems, recv_sems, in_thru, land_thru, token):
        x, y, c, chips = _place()
        for k, to in enumerate([(cx, cy, c) for cx, cy in chips] + [(x, y, 1 - c)]):
            pltpu.make_async_remote_copy(src_ref=in_ref, dst_ref=land_ref.at[2 * x + y], send_sem=send_sems.at[k],
                                         recv_sem=recv_sems.at[k], device_id=to, device_id_type=MESH).start()
        token[...] = jnp.zeros_like(token)

    return pl.pallas_call(
        body, name="gather_late_start",
        out_shape=(pltpu.SemaphoreType.DMA((4,)), pltpu.SemaphoreType.DMA((4,)), pltpu.HBM(wpk.shape, wpk.dtype),
                   pltpu.HBM((N_CHIPS, rows, D), wpk.dtype), SDS((8, LANES), F32)),
        in_specs=(HBM, HBM, ANY), out_specs=(SEM, SEM, HBM, HBM, pl.BlockSpec(memory_space=pltpu.VMEM)),
        input_output_aliases={0: 2, 1: 3}, compiler_params=pltpu.CompilerParams(has_side_effects=DATAFLOW),
    )(_in_hbm(wpk), _in_hbm(lax.empty((N_CHIPS, rows, D), wpk.dtype)), after)


def _gather_late_wait(send_sems, recv_sems, in_thru, land_thru, after):
    def body(in_ref, land_ref, send_sems, recv_sems, after_ref, after2_ref, in_dead, got_ref):
        x, y, c, chips = _place()
        for k, (sx, sy) in enumerate(chips + [(x, y)]):
            cp = pltpu.make_async_remote_copy(src_ref=in_ref, dst_ref=land_ref.at[2 * sx + sy], send_sem=send_sems.at[k],
                                              recv_sem=recv_sems.at[k], device_id=(x, y, c), device_id_type=MESH)
            cp.wait_send()
            cp.wait_recv()

    return pl.pallas_call(
        body, name="gather_late_wait",
        out_shape=(pltpu.HBM(in_thru.shape, in_thru.dtype), pltpu.HBM(land_thru.shape, land_thru.dtype)),
        in_specs=(HBM, HBM, SEM, SEM, ANY, ANY), out_specs=(HBM, HBM), input_output_aliases={0: 0, 1: 1},
        compiler_params=pltpu.CompilerParams(has_side_effects=DATAFLOW),
    )(in_thru, land_thru, send_sems, recv_sems, *after)[1]


def _rs_sibling(gpk):
    HALF = gpk.shape[1] // 2

    def body(in_ref, out_ref, send_sem, recv_sem):
        x, y, c, _ = _place()
        theirs = pl.ds(pl.multiple_of((1 - c) * HALF, 16), HALF)
        cp = pltpu.make_async_remote_copy(src_ref=in_ref.at[:, theirs], dst_ref=out_ref, send_sem=send_sem, recv_sem=recv_sem,
                                          device_id=(x, y, 1 - c), device_id_type=MESH)
        cp.start()
        cp.wait()

    return _pcall(body, name="rs_sibling", in_specs=[ANY], out_specs=ANY, out_shape=SDS((N_CHIPS, HALF, D), gpk.dtype),
                  scratch=[pltpu.SemaphoreType.DMA, pltpu.SemaphoreType.DMA])(gpk)


def _rs_add_sibling(cidx, gpk, got):
    HALF = got.shape[1]
    th = HALF // 4
    nh = HALF // th
    assert th % 16 == 0

    def body(c_ref, a_ref, b_ref, o_ref):
        o_ref[...] = (a_ref[...].astype(F32) + b_ref[...].astype(F32)).astype(BF16)

    gs = pltpu.PrefetchScalarGridSpec(
        num_scalar_prefetch=1, grid=(N_CHIPS, nh),
        in_specs=[pl.BlockSpec((1, th, D), lambda j, i, c: (j, c[0] * nh + i, 0)), pl.BlockSpec((1, th, D), lambda j, i, c: (j, i, 0))],
        out_specs=pl.BlockSpec((1, th, D), lambda j, i, c: (j, i, 0)))
    return pl.pallas_call(body, name="rs_add_sibling", grid_spec=gs, out_shape=SDS((N_CHIPS, HALF, D), BF16),
                          compiler_params=pltpu.CompilerParams(dimension_semantics=("parallel", "parallel"),
                                                               vmem_limit_bytes=48 << 20))(cidx, gpk, got)


def _rs_chips_start(part, small, after):
    def body(p_ref, s_ref, land_ref, sland_ref, after_ref, send_sems, recv_sems, p_thru, s_thru, land_thru, sland_thru, token):
        x, y, c, chips = _place()
        for k, (cx, cy) in enumerate(chips):
            pltpu.make_async_remote_copy(src_ref=p_ref.at[2 * cx + cy], dst_ref=land_ref.at[2 * x + y], send_sem=send_sems.at[k],
                                         recv_sem=recv_sems.at[k], device_id=(cx, cy, c), device_id_type=MESH).start()
        peers = [(x, y, 1 - c)] + [(cx, cy, c) for cx, cy in chips] + [(cx, cy, 1 - c) for cx, cy in chips]
        for k, to in enumerate(peers):
            pltpu.make_async_remote_copy(src_ref=s_ref, dst_ref=sland_ref.at[4 * x + 2 * y + c], send_sem=send_sems.at[3 + k],
                                         recv_sem=recv_sems.at[3 + k], device_id=to, device_id_type=MESH).start()
        token[...] = jnp.zeros_like(token)

    return pl.pallas_call(
        body, name="rs_chips_start",
        out_shape=(pltpu.SemaphoreType.DMA((10,)), pltpu.SemaphoreType.DMA((10,)), pltpu.HBM(part.shape, part.dtype),
                   pltpu.HBM(small.shape, small.dtype), pltpu.HBM(part.shape, part.dtype), pltpu.HBM((N_DEV, 8, D), F32),
                   SDS((8, LANES), F32)),
        in_specs=(HBM, HBM, HBM, HBM, ANY), out_specs=(SEM, SEM, HBM, HBM, HBM, HBM, pl.BlockSpec(memory_space=pltpu.VMEM)),
        input_output_aliases={0: 2, 1: 3, 2: 4, 3: 5}, compiler_params=pltpu.CompilerParams(has_side_effects=DATAFLOW),
    )(_in_hbm(part), _in_hbm(small), _in_hbm(lax.empty(part.shape, part.dtype)), _in_hbm(lax.empty((N_DEV, 8, D), F32)), after)


def _rs_chips_wait(send_sems, recv_sems, p_thru, s_thru, land_thru, sland_thru, after):
    def body(p_ref, s_ref, land_ref, sland_ref, send_sems, recv_sems, *after_and_outputs):
        x, y, c, chips = _place()
        for k, (cx, cy) in enumerate(chips):
            cp = pltpu.make_async_remote_copy(src_ref=p_ref.at[0], dst_ref=land_ref.at[2 * cx + cy], send_sem=send_sems.at[k],
                                              recv_sem=recv_sems.at[k], device_id=(cx, cy, c), device_id_type=MESH)
            cp.wait_send()
            cp.wait_recv()
        peers = [(x, y, 1 - c)] + [(cx, cy, c) for cx, cy in chips] + [(cx, cy, 1 - c) for cx, cy in chips]
        for k, (px, py, pc) in enumerate(peers):
            cp = pltpu.make_async_remote_copy(src_ref=s_ref, dst_ref=sland_ref.at[4 * px + 2 * py + pc], send_sem=send_sems.at[3 + k],
                                              recv_sem=recv_sems.at[3 + k], device_id=(px, py, pc), device_id_type=MESH)
            cp.wait_send()
            cp.wait_recv()

    hbm = lambda a: pltpu.HBM(a.shape, a.dtype)
    outs = pl.pallas_call(
        body, name="rs_chips_wait", out_shape=(hbm(p_thru), hbm(s_thru), hbm(land_thru), hbm(sland_thru)),
        in_specs=(HBM, HBM, HBM, HBM, SEM, SEM) + (ANY,) * len(after), out_specs=(HBM, HBM, HBM, HBM),
        input_output_aliases={0: 0, 1: 1, 2: 2, 3: 3}, compiler_params=pltpu.CompilerParams(has_side_effects=DATAFLOW),
    )(p_thru, s_thru, land_thru, sland_thru, send_sems, recv_sems, *after)
    return outs[0], outs[2], outs[3]


def _rs_add_chips(qidx, part, parts):
    HALF = part.shape[1]
    th = HALF // 4
    nh = HALF // th
    assert th % 16 == 0

    def body(q_ref, own_ref, p_ref, o_ref):
        for me in range(N_CHIPS):
            @pl.when(q_ref[0] == me)
            def _(me=me):
                t = [(own_ref[0] if j == me else p_ref[j]).astype(F32) for j in range(N_CHIPS)]
                o_ref[...] = ((t[0] + t[1]) + t[2]) + t[3]

    gs = pltpu.PrefetchScalarGridSpec(
        num_scalar_prefetch=1, grid=(HALF // th,),
        in_specs=[pl.BlockSpec((1, th, D), lambda i, q: (q[0], i, 0)), pl.BlockSpec((N_CHIPS, th, D), lambda i, q: (0, i, 0))],
        out_specs=pl.BlockSpec((th, D), lambda i, q: (q[1] * nh + i, 0)))
    return pl.pallas_call(body, name="rs_add_chips", grid_spec=gs, out_shape=SDS((2 * HALF, D), F32),
                          compiler_params=pltpu.CompilerParams(dimension_semantics=("parallel",),
                                                               vmem_limit_bytes=48 << 20))(qidx, part, parts)


def _rs_join(shard, name):
    HALF = shard.shape[0] // 2

    def body(in_ref, out_ref, send_sem, recv_sem):
        x, y, c, _ = _place()
        rows = pl.ds(pl.multiple_of(c * HALF, 16), HALF)
        cp = pltpu.make_async_remote_copy(src_ref=in_ref.at[rows], dst_ref=out_ref.at[rows], send_sem=send_sem, recv_sem=recv_sem,
                                          device_id=(x, y, 1 - c), device_id_type=MESH)
        cp.start()
        cp.wait()

    return _pcall(body, name=name, in_specs=[ANY], out_specs=ANY, out_shape=SDS(shard.shape, F32),
                  scratch=[pltpu.SemaphoreType.DMA, pltpu.SemaphoreType.DMA], aliases={0: 0})(shard)


def _reduce_late_start(gpk, after):
    rows = gpk.shape[1]
    HALF = rows // 2
    assert HALF % 16 == 0

    def body(in_ref, land_ref, after_ref, send_sems, recv_sems, in_thru, land_thru, token):
        x, y, c, chips = _place()
        me = 4 * x + 2 * y + c
        peers = [(x, y, 1 - c)] + [(cx, cy, c) for cx, cy in chips] + [(cx, cy, 1 - c) for cx, cy in chips]
        for k, (px, py, pc) in enumerate(peers):
            src = in_ref.at[2 * px + py, pl.ds(pl.multiple_of(pc * HALF, 16), HALF)]
            pltpu.make_async_remote_copy(src_ref=src, dst_ref=land_ref.at[me], send_sem=send_sems.at[k], recv_sem=recv_sems.at[k],
                                         device_id=(px, py, pc), device_id_type=MESH).start()
        token[...] = jnp.zeros_like(token)

    return pl.pallas_call(
        body, name="reduce_late_start",
        out_shape=(pltpu.SemaphoreType.DMA((7,)), pltpu.SemaphoreType.DMA((7,)), pltpu.HBM(gpk.shape, gpk.dtype),
                   pltpu.HBM((N_DEV, HALF, D), gpk.dtype), SDS((8, LANES), F32)),
        in_specs=(HBM, HBM, ANY), out_specs=(SEM, SEM, HBM, HBM, pl.BlockSpec(memory_space=pltpu.VMEM)),
        input_output_aliases={0: 2, 1: 3}, compiler_params=pltpu.CompilerParams(has_side_effects=DATAFLOW),
    )(_in_hbm(gpk), _in_hbm(lax.empty((N_DEV, HALF, D), gpk.dtype)), after)


def _reduce_late_wait(send_sems, recv_sems, in_thru, land_thru, after):
    def body(in_ref, land_ref, send_sems, recv_sems, after_ref, in_out, got_ref):
        x, y, c, chips = _place()
        peers = [(x, y, 1 - c)] + [(cx, cy, c) for cx, cy in chips] + [(cx, cy, 1 - c) for cx, cy in chips]
        for k, (px, py, pc) in enumerate(peers):
            cp = pltpu.make_async_remote_copy(src_ref=land_ref.at[0], dst_ref=land_ref.at[4 * px + 2 * py + pc],
                                              send_sem=send_sems.at[k], recv_sem=recv_sems.at[k],
                                              device_id=(px, py, pc), device_id_type=MESH)
            cp.wait_send()
            cp.wait_recv()

    return pl.pallas_call(
        body, name="reduce_late_wait",
        out_shape=(pltpu.HBM(in_thru.shape, in_thru.dtype), pltpu.HBM(land_thru.shape, land_thru.dtype)),
        in_specs=(HBM, HBM, SEM, SEM, ANY), out_specs=(HBM, HBM), input_output_aliases={0: 0, 1: 1},
        compiler_params=pltpu.CompilerParams(has_side_effects=DATAFLOW),
    )(in_thru, land_thru, send_sems, recv_sems, after)


def _reduce_late_add(didx, gpk, parts):
    HALF = parts.shape[1]
    th = HALF // 4
    nh = HALF // th
    assert th % 16 == 0

    def body(d_ref, own_ref, p_ref, o_ref):
        for me in range(N_DEV):
            @pl.when(d_ref[0] == me)
            def _(me=me):
                t = [(own_ref[0] if j == me else p_ref[j]).astype(F32) for j in range(N_DEV)]
                o_ref[...] = ((((((t[0] + t[1]) + t[2]) + t[3]) + t[4]) + t[5]) + t[6]) + t[7]

    gs = pltpu.PrefetchScalarGridSpec(
        num_scalar_prefetch=1, grid=(nh,),
        in_specs=[pl.BlockSpec((1, th, D), lambda i, d: (d[1], d[2] * nh + i, 0)), pl.BlockSpec((N_DEV, th, D), lambda i, d: (0, i, 0))],
        out_specs=pl.BlockSpec((th, D), lambda i, d: (d[2] * nh + i, 0)))
    return pl.pallas_call(body, name="reduce_late_add", grid_spec=gs, out_shape=SDS((2 * HALF, D), F32),
                          compiler_params=pltpu.CompilerParams(dimension_semantics=("parallel",),
                                                               vmem_limit_bytes=48 << 20))(didx, gpk, parts)


def _pack_early(b, dtype):
    lanes = lambda a: jnp.pad(a.astype(dtype), ((0, 0), (0, D - a.shape[1])))
    pair = jnp.concatenate([b["w_q_b"].astype(dtype), b["w_ple"].astype(dtype), jnp.zeros((256, D - 640), dtype)], axis=1)
    return jnp.concatenate([lanes(b["w_in"]), pair, lanes(b["w_kv_b"])], axis=0)


def _pack_late(b, dtype):
    return jnp.concatenate([b[n].astype(dtype) for n in ("w_mla_up", "w_swa_up", "w_out", "w_ple_gate", "w_mlp_up", "w_mlp_down")],
                           axis=0)


def _full_weights(gathered, which):
    out = {}
    for n, r, c in BIG:
        buf, row0, lane0 = PACK_AT[n]
        if buf != which:
            continue
        blk = gathered[:, row0:row0 + r, lane0:lane0 + c]
        if n == "w_in":
            out["w_in_p"] = _w_in_internal([blk[j] for j in range(N_CHIPS)])
        elif n in COL_SHARDED:
            out[n] = jnp.swapaxes(blk, 0, 1).reshape(r, N_CHIPS * c)
        else:
            out[n] = blk.reshape(N_CHIPS * r, c)
    return out


def _split_full_grads(grads, pack, dtype):
    shard = {n: (r, c) for n, r, c in BIG}
    chunks = []
    for j in range(N_CHIPS):
        blocks = {}
        for n, g in grads.items():
            if n == "w_in_p":
                blocks["w_in"] = _w_in_grad_shard(g, j)
                continue
            r, c = shard[n]
            blocks[n] = g[:, j * c:(j + 1) * c] if n in COL_SHARDED else g[j * r:(j + 1) * r]
        chunks.append(pack(blocks, dtype))
    return jnp.stack(chunks)


W_IN_SHARD = 936
W_IN_SEGMENTS = ((0, 256, (3072,)), (256, 384, (3840,)), (384, 416, (4032,)), (416, 1440, (0,)), (1440, 1504, (3328, 3392)),
                 (1504, 1568, (3456, 3520)), (1568, 1632, (3584, 3648)), (1632, 1696, (3712, 3776)), (1696, 3744, (1024,)))


def _w_in_internal(shards):
    def cols(a, b):
        out = []
        for j, s in enumerate(shards):
            lo, hi = max(a, W_IN_SHARD * j), min(b, W_IN_SHARD * (j + 1))
            if lo < hi:
                out.append(s[:, lo - W_IN_SHARD * j:hi - W_IN_SHARD * j])
        return out

    pieces = {}
    for a, b, places in W_IN_SEGMENTS:
        for at in places:
            pieces[at] = cols(a, b)
    zeros = lambda n: [jnp.zeros((D, n), shards[0].dtype)]
    pieces[3968] = zeros(64)
    pieces[4064] = zeros(32)
    return jnp.concatenate([piece for at in sorted(pieces) for piece in pieces[at]], axis=1)


def _w_in_grad_shard(g, j):
    def internal(a, b):
        out = []
        while a < b:
            end = min(b, (a // D + 1) * D)
            out.append(g[a // D][:, a % D:a % D + end - a])
            a = end
        return out

    out = []
    for a, b, places in W_IN_SEGMENTS:
        lo, hi = max(a, W_IN_SHARD * j), min(b, W_IN_SHARD * (j + 1))
        if lo < hi:
            parts = [internal(at + lo - a, at + hi - a) for at in places]
            if len(parts) == 1:
                out += parts[0]
            else:
                assert len(parts[0]) == len(parts[1]) == 1
                out.append(parts[0][0] + parts[1][0])
    return jnp.concatenate(out, axis=1)


def _local_step(x, p, tgt, w, small, late_weights, late_grads_out):
    T = x.shape[0]
    tm = 256
    tb = 256
    w_in_p = w["w_in_p"]
    wqb = jnp.pad(w["w_q_b"].reshape(Q_LORA, MLA_HEADS, 96), ((0, 0), (0, 0), (0, 32))).reshape(Q_LORA, 2048)
    wkv = w["w_kv_b"].reshape(KV_LORA, MLA_HEADS, 128)
    wkn = jnp.pad(wkv[:, :, :64], ((0, 0), (0, 0), (0, 64))).reshape(KV_LORA, 2048)
    wv = wkv[:, :, 64:].reshape(KV_LORA, 1024)
    tab_m = _rope_tables(T, "mla")
    tab_s = _rope_tables(T, "swa")
    g1, gq, gkv, sinks = small["g_mix_pre"], small["g_q_a"], small["g_kv_a"], small["sinks"]
    g2, g3, g4, g5 = small["g_mix_post"], small["g_mlp_pre"], small["g_mlp_post"], small["g_ple"]
    sink_vec = sinks.reshape(SWA_HEADS)

    z, h1 = _fwd_in(x, g1, w_in_p, tm)
    qn, kvn, km, vm, qt, kt, vt, qs, ks, vs = _fwd_qkv(z, gq, gkv, wqb, wkn, wv, tab_m, tab_s, tb)
    om, lse_m = _mla_fwd(qt, km, vt, tb)
    os_, lse_s = _swa_fwd(sink_vec, qs, ks, vs)
    w = {**w, **late_weights((om, os_))}
    y, yo, au, bu, x1 = _fwd_mix(om, os_, z, x, w["w_mla_up"], w["w_swa_up"], w["w_out"], g2, tm)
    h2, u = _fwd_mlp_up(x1, g3, w["w_mlp_up"], tm)
    d, x2 = _fwd_mlp_down(u, w["w_mlp_down"], x1, g4, tm)
    loss, dx2, dgt, de0, dg5 = _ple_fwd_bwd(p, x2, tgt, w["w_ple"], g5, w["w_ple_gate"], tm)

    dd, da, dg4 = _bwd_mlp_down(dx2, d, g4, w["w_mlp_down"], u, tm)
    dx1, dg3 = _bwd_mlp_up(da, w["w_mlp_up"], x1, g3, dx2, tm)
    dyo, dg2, dau, dbu, dga, dgb, dos, delta_m, dom_t = _bwd_mix(dx1, yo, g2, w["w_out"], z, au, bu, w["w_mla_up"],
                                                                w["w_swa_up"], om, tb)
    gpk_late = lax.empty((N_CHIPS, PACK_ROWS["late"], D), BF16)
    for weight, a_, g_ in (("w_mla_up", om, dau), ("w_swa_up", os_, dbu), ("w_out", y, dyo), ("w_ple_gate", x2, dgt),
                           ("w_mlp_up", h2, da), ("w_mlp_down", u, dd)):
        gpk_late = _wgrad(a_, g_, "wgrad_" + weight[2:], into=(gpk_late, weight))
    token = late_grads_out(gpk_late)
    delta_m = delta_m + token[0, 0]
    dqm, dkm, dvm = _mla_bwd(qt, km, kt, vm, dom_t, lse_m, delta_m, tb)
    dqs, dkc, dkp, dvc, dvp, dsink = _swa_bwd(sink_vec, qs, ks, vs, dos, os_, lse_s)
    dqb, dknb, dvb, dsq, drest, dgq, dgkv = _bwd_qkv(dqm, dkm, dvm, dqs, dkc, dkp, dvc, dvp, z, gq, gkv, wqb, wkn, wv,
                                                      tab_m, tab_s)
    gx, dg1 = _bwd_in(dsq, dga, dgb, drest, w_in_p, x, g1, dx1, tm)

    g_in_p = [_wgrad(h1, dsq, "wgrad_in_sq"), _wgrad(h1, dga, "wgrad_in_ga"), _wgrad(h1, dgb, "wgrad_in_gb"),
              _wgrad(h1, drest, "wgrad_in_rest")]
    g_qb_p = _wgrad_t(dqb, qn, "wgrad_q_b").T
    g_kn_p = _wgrad_t(dknb, kvn, "wgrad_kv_b_nope").T
    g_v_p = _wgrad_t(dvb, kvn, "wgrad_kv_b_v").T
    grads = {
        "w_in_p": g_in_p,
        "w_q_b": g_qb_p.reshape(Q_LORA, MLA_HEADS, 128)[:, :, :96].reshape(Q_LORA, 1536),
        "w_kv_b": jnp.concatenate([g_kn_p.reshape(KV_LORA, MLA_HEADS, 128)[:, :, :64], g_v_p.reshape(KV_LORA, MLA_HEADS, 64)],
                                  axis=2).reshape(KV_LORA, 2048),
        "w_ple": _wgrad(p, de0, "wgrad_ple"),
    }
    small_grads = {"g_mix_pre": dg1, "g_q_a": dgq.reshape(1, Q_LORA), "g_kv_a": dgkv.reshape(1, KV_LORA), "sinks": dsink[0:1, 0:SWA_HEADS], "g_mix_post": dg2,
                   "g_mlp_pre": dg3, "g_mlp_post": dg4, "g_ple": dg5}
    return loss, gx, grads, small_grads


def _pack_small(vals, fill, scalar=None):
    wide = [vals[n] for n, k in SMALL if k == D]
    narrow = [vals[n] for n, k in SMALL if k != D]
    used = sum(k for _, k in SMALL if k != D)
    last = jnp.concatenate(narrow + [jnp.full((1, D - used), fill, F32)], axis=1)
    rest = jnp.full((2, D), fill, F32)
    if scalar is not None:
        rest = jnp.concatenate([jnp.concatenate([scalar, rest[0:1, 1:]], axis=1), rest[1:2]], axis=0)
    return jnp.concatenate(wide + [last, rest], axis=0)


def _unpack_small(pk):
    out, row, off = {}, 0, 0
    for n, k in SMALL:
        if k == D:
            out[n] = pk[row:row + 1]
            row += 1
    for n, k in SMALL:
        if k != D:
            out[n] = pk[5:6, off:off + k]
            off += k
    return out


def kernel(x, p, g_mix_pre, w_in, g_q_a, w_q_b, g_kv_a, w_kv_b, sinks, w_mla_up, w_swa_up, w_out, g_mix_post, g_mlp_pre, w_mlp_up, w_mlp_down, g_mlp_post, w_ple, g_ple, w_ple_gate, loss_target, m_g_mix_pre, m_w_in, m_g_q_a, m_w_q_b, m_g_kv_a, m_w_kv_b, m_sinks, m_w_mla_up, m_w_swa_up, m_w_out, m_g_mix_post, m_g_mlp_pre, m_w_mlp_up, m_w_mlp_down, m_g_mlp_post, m_w_ple, m_g_ple, m_w_ple_gate, v_g_mix_pre, v_w_in, v_g_q_a, v_w_q_b, v_g_kv_a, v_w_kv_b, v_sinks, v_w_mla_up, v_w_swa_up, v_w_out, v_g_mix_post, v_g_mlp_pre, v_w_mlp_up, v_w_mlp_down, v_g_mlp_post, v_w_ple, v_g_ple, v_w_ple_gate):
    given = dict(locals())
    big_w = {n: given[n][0] for n, _, _ in BIG}
    small_w = {n: given[n] for n, _ in SMALL}
    small_m = {n: given["m_" + n] for n, _ in SMALL}
    small_v = {n: given["v_" + n] for n, _ in SMALL}

    core = lax.axis_index("c")
    chip = 2 * lax.axis_index("x") + lax.axis_index("y")
    core_i = core.astype(jnp.int32).reshape(1)
    dev_i = jnp.stack([2 * chip + core, chip, core]).astype(jnp.int32)

    own_early = _pack_early(big_w, BF16)
    own_late = _pack_late(big_w, BF16)
    got_early = _all_gather(own_early)
    late_flight = _gather_late_start(own_late, got_early)
    weights = _full_weights(got_early, "early")
    step_small = {**small_w, "g_mix_pre": small_w["g_mix_pre"] + late_flight[4][0, 0]}

    def late_weights(after):
        return _full_weights(_gather_late_wait(*late_flight[:4], after), "late")

    flight = {}

    def late_grads_out(gpk_late):
        flight["late"] = _reduce_late_start(gpk_late, dev_i)
        return flight["late"][4]

    loss_blk, gx, grads, small_grads = _local_step(x[0], p[0, 0], loss_target[0], weights, step_small, late_weights,
                                                   late_grads_out)

    gpk = _split_full_grads(grads, _pack_early, BF16)
    got = _rs_sibling(gpk)
    part = _rs_add_sibling(core_i, gpk, got)
    small_own = _pack_small(small_grads, 0.0, loss_blk[0:1, 0:1])
    early_flight = _rs_chips_start(part, small_own, dev_i)

    out_g, out_d, out_m, out_v = {}, {}, {}, {}
    gpk_late, parts_late = _reduce_late_wait(*flight["late"][:4], early_flight[6])
    joined_late = _rs_join(_reduce_late_add(dev_i, gpk_late, parts_late), "rs_join_late")
    for n, _, _ in BIG:
        if PACK_AT[n][0] == "late":
            out_g[n], out_d[n], out_m[n], out_v[n] = _adamw(given[n], joined_late, given["m_" + n], given["v_" + n], n)

    part, parts, small_parts = _rs_chips_wait(*early_flight[:6], [out_d[n] for n in out_d])
    joined_early = _rs_join(_rs_add_chips(dev_i[1:3], part, parts), "rs_join_early")
    for n, _, _ in BIG:
        if PACK_AT[n][0] == "early":
            t = given[n].shape[2] % LANES != 0
            wmv = [jnp.swapaxes(a, 1, 2) if t else a for a in (given[n], given["m_" + n], given["v_" + n])]
            res = _adamw(wmv[0], joined_early, wmv[1], wmv[2], n, transposed=t)
            out_g[n], out_d[n], out_m[n], out_v[n] = [jnp.swapaxes(a, 1, 2) if t else a for a in res]

    mine = (lax.broadcasted_iota(jnp.int32, (N_DEV, 1, 1), 0) == dev_i[0])
    g_small_pk, d_small_pk, m_small_pk, v_small_pk = _adamw_small(
        _pack_small(small_w, 0.0), jnp.where(mine, small_own[None], small_parts), _pack_small(small_m, 0.0),
        _pack_small(small_v, 1.0))
    loss = g_small_pk[6, 0]
    for out, pk in ((out_g, g_small_pk), (out_d, d_small_pk), (out_m, m_small_pk), (out_v, v_small_pk)):
        out.update(_unpack_small(pk))
    order = ["g_mix_pre", "w_in", "g_q_a", "w_q_b", "g_kv_a", "w_kv_b", "sinks", "w_mla_up", "w_swa_up", "w_out", "g_mix_post",
             "g_mlp_pre", "w_mlp_up", "w_mlp_down", "g_mlp_post", "w_ple", "g_ple", "w_ple_gate"]
    return (loss, gx[None], *[out_g[n] for n in order], *[out_d[n] for n in order], *[out_m[n] for n in order],
            *[out_v[n] for n in order])
```

```python
import math

import jax
import jax.numpy as jnp
import numpy as np
from jax import lax
from jax.experimental import pallas as pl
from jax.experimental.pallas import tpu as pltpu

F32 = jnp.float32
BF16 = jnp.bfloat16
SDS = jax.ShapeDtypeStruct

D = 1024
D_FF = 4096
PLE = 256
Q_LORA = 256
KV_LORA = 128
MLA_HEADS = 16
MLA_NOPE = 64
MLA_ROPE = 32
SWA_HEADS = 16
SWA_HD = 64
WINDOW = 128
ROPE_THETA = 10000.0
EPS = 1e-6
NEG = -1e30
NZ = 4096
MLA_SCALE = (MLA_NOPE + MLA_ROPE) ** -0.5
LOG2_E = math.log2(math.e)
MLA_LOG2_SCALE = MLA_SCALE * LOG2_E
SWA_SCALE = SWA_HD ** -0.5

ADAM_LR = 0.001
ADAM_B1 = 0.9
ADAM_B2 = 0.999
ADAM_EPS = 1e-08
ADAM_WD = 0.01
ADAM_STEP = 10

LANES = 128
ATT_COLS = 128
VT_ROWS = 80
N_CHIPS = 4
N_DEV = 8
MESH = pl.DeviceIdType.MESH

NT = (((1,), (1,)), ((), ()))
TN = (((0,), (0,)), ((), ()))

BIG = (("w_in", 1024, 936), ("w_q_b", 256, 384), ("w_kv_b", 128, 512), ("w_mla_up", 256, 1024),
       ("w_swa_up", 256, 1024), ("w_out", 256, 1024), ("w_mlp_up", 1024, 1024), ("w_mlp_down", 1024, 1024),
       ("w_ple", 256, 256), ("w_ple_gate", 256, 1024))
COL_SHARDED = ("w_in", "w_q_b", "w_kv_b", "w_mlp_up", "w_ple")
PACK_AT = {"w_in": ("early", 0, 0), "w_q_b": ("early", 1024, 0), "w_ple": ("early", 1024, 384), "w_kv_b": ("early", 1280, 0),
           "w_mla_up": ("late", 0, 0), "w_swa_up": ("late", 256, 0), "w_out": ("late", 512, 0), "w_ple_gate": ("late", 768, 0),
           "w_mlp_up": ("late", 1024, 0), "w_mlp_down": ("late", 2048, 0)}
PACK_ROWS = {"early": 1408, "late": 3072}
SMALL = (("g_mix_pre", 1024), ("g_q_a", 256), ("g_kv_a", 128), ("sinks", 16), ("g_mix_post", 1024),
         ("g_mlp_pre", 1024), ("g_mlp_post", 1024), ("g_ple", 1024))


def _dot(a, b):
    return jnp.dot(a, b, preferred_element_type=F32)


def _dot_nt(a, b):
    return lax.dot_general(a, b, NT, preferred_element_type=F32)


def _dot_tn(a, b):
    return lax.dot_general(a, b, TN, preferred_element_type=F32)


def _pcall(body, *, name, out_shape, grid=(), in_specs=None, out_specs=None, scratch=(), sem=None, vmem_mb=48, aliases=None):
    params = dict(vmem_limit_bytes=vmem_mb << 20)
    if sem is not None:
        params["dimension_semantics"] = sem
    return pl.pallas_call(body, name=name, grid=grid, in_specs=in_specs, out_specs=out_specs, out_shape=out_shape,
                          scratch_shapes=list(scratch), input_output_aliases=aliases or {},
                          compiler_params=pltpu.CompilerParams(**params))


def _rows(tm, n, col=0):
    return pl.BlockSpec((tm, n), lambda i: (i, col))


def _full(shape):
    return pl.BlockSpec(shape, lambda i: (0,) * len(shape))


def _chunks(k):
    assert D_FF == N_CHIPS * D
    return pl.BlockSpec((N_CHIPS, D, D), lambda i: (0, k, 0))


def _rms(x, g):
    r = lax.rsqrt(jnp.mean(x * x, axis=-1, keepdims=True) + EPS)
    return x * r * g


def _rms_bwd(dy, x, g):
    r = lax.rsqrt(jnp.mean(x * x, axis=-1, keepdims=True) + EPS)
    xn = x * r
    dn = dy * g
    dx = r * (dn - xn * jnp.mean(dn * xn, axis=-1, keepdims=True))
    return dx, jnp.sum(dy * xn, axis=0, keepdims=True)


def _sigmoid(x):
    return 1.0 / (1.0 + jnp.exp(-x))


def _rope(x, c, a, b, half):
    return x * c + pltpu.roll(x, LANES - half, 1) * a + pltpu.roll(x, half, 1) * b


def _rope_tables(T, kind):
    lane = np.arange(LANES)
    if kind == "mla":
        half = MLA_ROPE // 2
        rel = lane - MLA_NOPE
        on = (rel >= 0) & (rel < MLA_ROPE)
        d = MLA_ROPE
    else:
        half = SWA_HD // 2
        rel = lane % SWA_HD
        on = np.ones((LANES,), bool)
        d = SWA_HD
    first = on & (rel < half)
    second = on & (rel >= half)
    f = np.where(first, rel, rel - half).astype(np.float32)
    inv = np.exp(np.float32(-math.log(ROPE_THETA)) * f * np.float32(2.0 / d)).astype(np.float32)
    ang = (np.arange(T, dtype=np.float32)[:, None] * inv[None, :]).astype(np.float64)
    cos, sin = np.cos(ang).astype(np.float32), np.sin(ang).astype(np.float32)
    c = np.where(on[None], cos, np.float32(1.0))
    a = np.where(first[None], -sin, np.float32(0.0))
    b = np.where(second[None], sin, np.float32(0.0))
    return c, a, b


def _fwd_in(x, g1, w_in_p, tm):
    T = x.shape[0]

    def body(x_ref, g_ref, w_ref, z_ref, h_ref):
        h = _rms(x_ref[...], g_ref[...]).astype(BF16)
        h_ref[...] = h
        z_ref[...] = _dot(h, w_ref[...])

    return _pcall(body, name="fwd_in", grid=(T // tm,),
                  in_specs=[_rows(tm, D), _full((1, D)), _full((D, NZ))],
                  out_specs=[_rows(tm, NZ), _rows(tm, D)],
                  out_shape=[SDS((T, NZ), F32), SDS((T, D), BF16)], sem=("parallel",))(x, g1, w_in_p)


def _fwd_qkv(z, gq, gkv, wqb, wkn, wv, tab_m, tab_s, tm):
    T = z.shape[0]
    wqb_t, wkn_t, wv_t = wqb.T, wkn.T, wv.T
    tab_mt = [t.T for t in tab_m]

    def body(qa_ref, sq_ref, skd_ref, svd_ref, kva_ref, kr_ref, gq_ref, gkv_ref, wkn_ref, wv_ref, wqbt_ref, wknt_ref, wvt_ref,
             cm_ref, am_ref, bm_ref, cmt_ref, amt_ref, bmt_ref, cs_ref, as_ref, bs_ref,
             qn_ref, kvn_ref, km_ref, vm_ref, qt_ref, kt_ref, vt_ref, qs_ref, ks_ref, vs_ref):
        qn = _rms(qa_ref[...], gq_ref[...])
        qn_ref[...] = qn.astype(BF16)
        kvn = _rms(kva_ref[...], gkv_ref[...])
        kvn_b = kvn.astype(BF16)
        kvn_ref[...] = kvn_b
        qn_t = qn.T.astype(BF16)
        kvn_t = kvn.T.astype(BF16)
        cm, am, bm = cm_ref[...], am_ref[...], bm_ref[...]
        cmt, amt, bmt = cmt_ref[...], amt_ref[...], bmt_ref[...]
        cs, as_, bs = cs_ref[...], as_ref[...], bs_ref[...]
        k_rope = _rope(kr_ref[...], cm, am, bm, MLA_ROPE // 2)
        k_rope_t = k_rope.T
        half = MLA_ROPE // 2
        vm_ref[...] = _dot(kvn_b, wv_ref[...]).astype(BF16)
        km_all = _dot(kvn_b, wkn_ref[...])
        v_t = _dot(wvt_ref[...], kvn_t)
        q_t = _dot(wqbt_ref[...], qn_t)
        k_t = _dot(wknt_ref[...], kvn_t)
        ones_row = jnp.where(lax.broadcasted_iota(jnp.int32, (64, tm), 0) == 0, 1.0, 0.0)
        for h in range(MLA_HEADS):
            sl = slice(LANES * h, LANES * (h + 1))
            vt_ref[0, sl, :] = jnp.concatenate([v_t[64 * h:64 * (h + 1)], ones_row], axis=0).astype(BF16)
            qh = q_t[sl]
            qt_ref[0, sl, :] = (qh * cmt + pltpu.roll(qh, LANES - half, 0) * amt + pltpu.roll(qh, half, 0) * bmt).astype(BF16)
            km_ref[:, sl] = (km_all[:, sl] + k_rope).astype(BF16)
            kt_ref[0, sl, :] = (k_t[sl] + k_rope_t).astype(BF16)
        for j in range(D // LANES):
            sl = slice(LANES * j, LANES * (j + 1))
            qs_ref[:, sl] = _rope(sq_ref[:, sl], cs, as_, bs, SWA_HD // 2).astype(BF16)
        for j in range(2):
            sl = slice(LANES * j, LANES * (j + 1))
            ks_ref[:, sl] = _rope(skd_ref[:, sl], cs, as_, bs, SWA_HD // 2).astype(BF16)
        vs_ref[...] = svd_ref[...].astype(BF16)

    tab = [_rows(tm, LANES)] * 3
    tab_t = [pl.BlockSpec((LANES, tm), lambda i: (0, i))] * 3
    return _pcall(body, name="fwd_qkv", grid=(T // tm,),
                  in_specs=[_rows(tm, 256, 12), _rows(tm, 1024, 0), _rows(tm, 256, 13), _rows(tm, 256, 14),
                            _rows(tm, 128, 30), _rows(tm, 128, 31), _full((1, Q_LORA)), _full((1, KV_LORA)),
                            _full((KV_LORA, 2048)), _full((KV_LORA, 1024)), _full((2048, Q_LORA)), _full((2048, KV_LORA)),
                            _full((1024, KV_LORA))] + tab + tab_t + tab,
                  out_specs=[_rows(tm, Q_LORA), _rows(tm, KV_LORA), _rows(tm, 2048), _rows(tm, 1024),
                             pl.BlockSpec((1, 2048, tm), lambda i: (i, 0, 0)), pl.BlockSpec((1, 2048, tm), lambda i: (i, 0, 0)),
                             pl.BlockSpec((1, 2048, tm), lambda i: (i, 0, 0)),
                             _rows(tm, 1024), _rows(tm, 256), _rows(tm, 256)],
                  out_shape=[SDS((T, Q_LORA), BF16), SDS((T, KV_LORA), BF16), SDS((T, 2048), BF16),
                             SDS((T, 1024), BF16), SDS((T // tm, 2048, tm), BF16), SDS((T // tm, 2048, tm), BF16),
                             SDS((T // tm, 2048, tm), BF16),
                             SDS((T, 1024), BF16), SDS((T, 256), BF16), SDS((T, 256), BF16)],
                  sem=("parallel",))(z, z, z, z, z, z, gq, gkv, wkn, wv, wqb_t, wkn_t, wv_t, *tab_m, *tab_mt, *tab_s)


def _mla_fwd(qt, km, vt, tb):
    T = km.shape[0]
    nb = T // tb
    cc = ATT_COLS

    per = 2 if nb % 2 == 0 else 1

    def body(q_ref, k_ref, vt_ref, o_ref, l_ref, s_ref, p_ref, al_ref, m_ref, acc_ref):
        for blk in range(per):
            one_block(per * pl.program_id(1) + blk, blk, q_ref, k_ref, vt_ref, o_ref, l_ref, s_ref, p_ref, al_ref, m_ref, acc_ref)

    def one_block(i, blk, q_ref, k_ref, vt_ref, o_ref, l_ref, s_ref, p_ref, al_ref, m_ref, acc_ref):
        m_ref[...] = jnp.full(m_ref.shape, NEG, F32)
        acc_ref[...] = jnp.zeros_like(acc_ref)
        p_ref[1] = jnp.zeros(p_ref.shape[1:], BF16)
        al_ref[1] = jnp.ones(al_ref.shape[1:], F32)
        key = lax.broadcasted_iota(jnp.int32, (tb, cc), 0)
        qry = lax.broadcasted_iota(jnp.int32, (tb, cc), 1)

        def scores(j, slot):
            off = pl.multiple_of(j * tb, tb)
            for hh in range(2):
                sl = slice(LANES * hh, LANES * (hh + 1))
                s_ref[slot, hh] = _dot(k_ref[pl.ds(off, tb), sl], q_ref[blk, sl, :])

        def softmax(slot, diagonal):
            chains = [(hh, slice(cc * c, cc * (c + 1)), c) for hh in range(2) for c in range(tb // cc)]

            def scaled(hh, cols, c):
                t = s_ref[slot, hh, :, cols] * MLA_LOG2_SCALE
                return jnp.where(key <= qry + cc * c, t, NEG) if diagonal else t

            tops = []
            for hh, cols, c in chains:
                if diagonal:
                    top = jnp.max(scaled(hh, cols, c), axis=0, keepdims=True)
                else:
                    top = jnp.max(s_ref[slot, hh, :, cols], axis=0, keepdims=True) * MLA_LOG2_SCALE
                m_old = m_ref[hh, :, cols]
                mn = jnp.maximum(m_old, top)
                m_ref[hh, :, cols] = mn
                al_ref[slot, hh, :, cols] = jnp.exp2(m_old - mn)
                tops.append(mn)
            for (hh, cols, c), mn in zip(chains, tops):
                p_ref[slot, hh, :, cols] = jnp.exp2(scaled(hh, cols, c) - mn).astype(BF16)

        def accumulate(j, slot):
            for hh in range(2):
                acc_ref[hh] = al_ref[slot, hh] * acc_ref[hh] + _dot(vt_ref[j, LANES * hh:LANES * hh + VT_ROWS, :], p_ref[slot, hh])

        def step(t, carry):
            scores(2 * t + 1, 1)
            accumulate(jnp.maximum(2 * t - 1, 0), 1)
            softmax(0, False)
            scores(2 * t + 2, 0)
            accumulate(2 * t, 0)
            softmax(1, False)
            return carry

        scores(0, 0)
        lax.fori_loop(0, i // 2, step, 0)

        @pl.when(i % 2 == 1)
        def _():
            scores(i, 1)
            accumulate(jnp.maximum(i - 2, 0), 1)
            softmax(0, False)
            accumulate(i - 1, 0)
            softmax(1, True)
            accumulate(i, 1)

        @pl.when(i % 2 == 0)
        def _():
            accumulate(jnp.maximum(i - 1, 0), 1)
            softmax(0, True)
            accumulate(i, 0)
        den = [acc_ref[hh, 64:65, :] for hh in range(2)]
        o_ref[tb * blk:tb * (blk + 1), :] = jnp.concatenate([acc_ref[hh, 0:64, :] / den[hh] for hh in range(2)], axis=0).T
        sub = lax.broadcasted_iota(jnp.int32, (8, tb), 0)
        lse = [m_ref[hh] + jnp.log(den[hh]) * LOG2_E for hh in range(2)]
        l_ref[0, blk] = jnp.where(sub == 0, lse[0], jnp.where(sub == 1, lse[1], 0.0))

    return _pcall(body, name="mla_fwd", grid=(MLA_HEADS // 2, nb // per),
                  in_specs=[pl.BlockSpec((per, 256, tb), lambda p, i: (i, p, 0)), pl.BlockSpec((T, 256), lambda p, i: (0, p)),
                            pl.BlockSpec((nb, 2 * LANES, tb), lambda p, i: (0, p, 0))],
                  out_specs=[pl.BlockSpec((per * tb, LANES), lambda p, i: (i, p)),
                             pl.BlockSpec((1, per, 8, tb), lambda p, i: (p, i, 0, 0))],
                  out_shape=[SDS((T, D), F32), SDS((MLA_HEADS // 2, nb, 8, tb), F32)],
                  scratch=[pltpu.VMEM((2, 2, tb, tb), F32), pltpu.VMEM((2, 2, tb, tb), BF16), pltpu.VMEM((2, 2, 1, tb), F32),
                           pltpu.VMEM((2, 1, tb), F32), pltpu.VMEM((2, VT_ROWS, tb), F32)],
                  sem=("parallel", "arbitrary"))(qt, km, vt)


def _swa_mask(n):
    row = lax.broadcasted_iota(jnp.int32, (WINDOW, 2 * WINDOW), 0)
    col = lax.broadcasted_iota(jnp.int32, (WINDOW, 2 * WINDOW), 1)
    rel = row - col + WINDOW
    return (rel >= 0) & (rel < WINDOW) & ((col >= WINDOW) | (n > 0))


def _swa_specs(T):
    nb = T // WINDOW
    cur = lambda w: pl.BlockSpec((WINDOW, w), lambda n: (n, 0))
    prev = lambda w: pl.BlockSpec((WINDOW, w), lambda n: (jnp.maximum(n - 1, 0), 0))
    return nb, cur, prev


def _swa_fwd(sinks, qs, ks, vs):
    T = qs.shape[0]
    nb, cur, prev = _swa_specs(T)

    def body(sink_ref, q_ref, kc_ref, kp_ref, vc_ref, vp_ref, o_ref, l_ref, kb_ref, vb_ref, s_ref, p_ref):
        n = pl.program_id(0)
        mask = _swa_mask(n)
        lo = lax.broadcasted_iota(jnp.int32, (WINDOW, LANES), 1) < 64
        hi = jnp.logical_not(lo)
        for g in range(2):
            gs = slice(LANES * g, LANES * (g + 1))
            kb_ref[g] = jnp.concatenate([kp_ref[:, gs], kc_ref[:, gs]], axis=0)
            vb_ref[g] = jnp.concatenate([vp_ref[:, gs], vc_ref[:, gs]], axis=0)
        for h in range(SWA_HEADS):
            qp = q_ref[:, LANES * (h // 2):LANES * (h // 2 + 1)]
            qh = jnp.where(lo if h % 2 == 0 else hi, qp, jnp.zeros_like(qp))
            s_ref[h] = _dot_nt(qh, kb_ref[h // 8])
        for j in range(SWA_HEADS // 2):
            sl = slice(LANES * j, LANES * (j + 1))
            lses = []
            for h in (2 * j, 2 * j + 1):
                s = jnp.where(mask, s_ref[h] * SWA_SCALE, NEG)
                sk = sink_ref[h]
                m = jnp.maximum(jnp.max(s, axis=1, keepdims=True), sk)
                e = jnp.exp(s - m)
                den = jnp.sum(e, axis=1, keepdims=True) + jnp.exp(sk - m)
                p_ref[h] = (e / den).astype(BF16)
                lses.append(jnp.broadcast_to(m + jnp.log(den), (WINDOW, LANES)))
            l_ref[:, sl] = jnp.where(lo, lses[0], lses[1])
        for j in range(SWA_HEADS // 2):
            vb = vb_ref[j // 4]
            o_ref[:, LANES * j:LANES * (j + 1)] = jnp.where(lo, _dot(p_ref[2 * j], vb), _dot(p_ref[2 * j + 1], vb))

    return _pcall(body, name="swa_fwd", grid=(nb,),
                  in_specs=[pl.BlockSpec(memory_space=pltpu.SMEM), cur(D), cur(256), prev(256), cur(256), prev(256)],
                  out_specs=[cur(D), cur(D)], out_shape=[SDS((T, D), F32)] * 2,
                  scratch=[pltpu.VMEM((2, 2 * WINDOW, LANES), BF16), pltpu.VMEM((2, 2 * WINDOW, LANES), BF16),
                           pltpu.VMEM((SWA_HEADS, WINDOW, 2 * WINDOW), F32), pltpu.VMEM((SWA_HEADS, WINDOW, 2 * WINDOW), BF16)],
                  sem=("parallel",))(sinks, qs, ks, ks, vs, vs)


def _fwd_mix(om, os_, z, x, wmu, wsu, wo, g2, tm):
    T = x.shape[0]

    def body(om_ref, os_ref, ga_ref, gb_ref, x_ref, wmu_ref, wsu_ref, wo_ref, g2_ref,
             y_ref, yo_ref, au_ref, bu_ref, x1_ref):
        au = _dot(om_ref[...].astype(BF16), wmu_ref[...])
        bu = _dot(os_ref[...].astype(BF16), wsu_ref[...])
        au_ref[...] = au
        bu_ref[...] = bu
        y = (_sigmoid(ga_ref[...]) * au + _sigmoid(gb_ref[...]) * bu).astype(BF16)
        y_ref[...] = y
        yo = _dot(y, wo_ref[...])
        yo_ref[...] = yo
        x1_ref[...] = x_ref[...] + _rms(yo, g2_ref[...])

    r = _rows(tm, D)
    w = _full((D, D))
    return _pcall(body, name="fwd_mix", grid=(T // tm,),
                  in_specs=[r, r, _rows(tm, D, 1), _rows(tm, D, 2), r, w, w, w, _full((1, D))],
                  out_specs=[r] * 5,
                  out_shape=[SDS((T, D), BF16), SDS((T, D), F32), SDS((T, D), F32), SDS((T, D), F32), SDS((T, D), F32)],
                  sem=("parallel",))(om, os_, z, z, x, wmu, wsu, wo, g2)


def _fwd_mlp_up(x1, g3, w1, tm):
    T = x1.shape[0]

    def body(x_ref, g_ref, w_ref, h_ref, u_ref):
        h = _rms(x_ref[...], g_ref[...]).astype(BF16)
        h_ref[...] = h
        for j in range(N_CHIPS):
            u_ref[:, j * D:(j + 1) * D] = jnp.square(jnp.maximum(_dot(h, w_ref[j]), 0.0)).astype(BF16)

    return _pcall(body, name="fwd_mlp_up", grid=(T // tm,),
                  in_specs=[_rows(tm, D), _full((1, D)), _chunks(w1[1])],
                  out_specs=[_rows(tm, D), _rows(tm, D_FF)],
                  out_shape=[SDS((T, D), BF16), SDS((T, D_FF), BF16)],
                  sem=("parallel",))(x1, g3, w1[0])


def _fwd_mlp_down(u, w2, x1, g4, tm):
    T = x1.shape[0]

    def body(u_ref, w_ref, x_ref, g_ref, d_ref, x2_ref):
        d = _dot(u_ref[...], w_ref[...].reshape(D_FF, D))
        d_ref[...] = d
        x2_ref[...] = x_ref[...] + _rms(d, g_ref[...])

    return _pcall(body, name="fwd_mlp_down", grid=(T // tm,),
                  in_specs=[_rows(tm, D_FF), _chunks(w2[1]), _rows(tm, D), _full((1, D))],
                  out_specs=[_rows(tm, D), _rows(tm, D)], out_shape=[SDS((T, D), F32)] * 2,
                  sem=("parallel",))(u, w2[0], x1, g4)


def _ple_fwd_bwd(p, x2, tgt, wple, g5, wpg, tm):
    T = x2.shape[0]

    def body(p_ref, x2_ref, t_ref, wple_ref, g5_ref, wpg_ref, loss_ref, dx2_ref, dgt_ref, de0_ref, dg5_ref):
        @pl.when(pl.program_id(0) == 0)
        def _():
            loss_ref[...] = jnp.zeros_like(loss_ref)
            dg5_ref[...] = jnp.zeros_like(dg5_ref)

        e0 = _dot(p_ref[...].astype(BF16), wple_ref[...])
        g5 = g5_ref[...]
        r = lax.rsqrt(jnp.mean(e0 * e0, axis=-1, keepdims=True) + EPS)
        en = e0 * r
        e = en * g5
        x2 = x2_ref[...]
        s = _sigmoid(_dot(x2.astype(BF16), wpg_ref[...]))
        diff = x2 + s * e - t_ref[...]
        sq = jnp.sum(jnp.sum(diff * diff, axis=1, keepdims=True), axis=0, keepdims=True)
        loss_ref[...] += jnp.broadcast_to(sq * (0.5 / D), loss_ref.shape)
        dx3 = diff * (1.0 / D)
        de = dx3 * s
        dgt = (dx3 * e * s * (1.0 - s)).astype(BF16)
        dgt_ref[...] = dgt
        dn = de * g5
        de0_ref[...] = (r * (dn - en * jnp.mean(dn * en, axis=-1, keepdims=True))).astype(BF16)
        dg5_ref[...] += jnp.sum(de * en, axis=0, keepdims=True)
        dx2_ref[...] = dx3 + _dot_nt(dgt, wpg_ref[...])

    r = _rows(tm, D)
    return _pcall(body, name="ple_fwd_bwd", grid=(T // tm,),
                  in_specs=[_rows(tm, PLE), r, r, _full((PLE, D)), _full((1, D)), _full((D, D))],
                  out_specs=[_full((8, LANES)), r, r, r, _full((1, D))],
                  out_shape=[SDS((8, LANES), F32), SDS((T, D), F32), SDS((T, D), BF16), SDS((T, D), BF16), SDS((1, D), F32)],
                  sem=("arbitrary",))(p, x2, tgt, wple, g5, wpg)


def _bwd_mlp_down(dx2, d, g4, w2, u, tm):
    T = dx2.shape[0]

    def body(dx_ref, d_ref, g_ref, w_ref, u_ref, dd_ref, da_ref, dg_ref):
        @pl.when(pl.program_id(0) == 0)
        def _():
            dg_ref[...] = jnp.zeros_like(dg_ref)

        dd, dg = _rms_bwd(dx_ref[...], d_ref[...], g_ref[...])
        dg_ref[...] += dg
        ddb = dd.astype(BF16)
        dd_ref[...] = ddb
        du = _dot_nt(ddb, w_ref[...].reshape(D_FF, D))
        da_ref[...] = (du * (2.0 * jnp.sqrt(u_ref[...].astype(F32)))).astype(BF16)

    return _pcall(body, name="bwd_mlp_down", grid=(T // tm,),
                  in_specs=[_rows(tm, D), _rows(tm, D), _full((1, D)), _chunks(w2[1]), _rows(tm, D_FF)],
                  out_specs=[_rows(tm, D), _rows(tm, D_FF), _full((1, D))],
                  out_shape=[SDS((T, D), BF16), SDS((T, D_FF), BF16), SDS((1, D), F32)],
                  sem=("arbitrary",))(dx2, d, g4, w2[0], u)


def _bwd_mlp_up(da, w1, x1, g3, dx2, tm):
    T = dx2.shape[0]

    def body(da_ref, w_ref, x_ref, g_ref, dx2_ref, dx1_ref, dg_ref):
        @pl.when(pl.program_id(0) == 0)
        def _():
            dg_ref[...] = jnp.zeros_like(dg_ref)

        dh = _dot_nt(da_ref[:, 0:D], w_ref[0])
        for j in range(1, N_CHIPS):
            dh += _dot_nt(da_ref[:, j * D:(j + 1) * D], w_ref[j])
        dx, dg = _rms_bwd(dh, x_ref[...], g_ref[...])
        dg_ref[...] += dg
        dx1_ref[...] = dx2_ref[...] + dx

    return _pcall(body, name="bwd_mlp_up", grid=(T // tm,),
                  in_specs=[_rows(tm, D_FF), _chunks(w1[1]), _rows(tm, D), _full((1, D)), _rows(tm, D)],
                  out_specs=[_rows(tm, D), _full((1, D))],
                  out_shape=[SDS((T, D), F32), SDS((1, D), F32)], sem=("arbitrary",))(da, w1[0], x1, g3, dx2)


def _bwd_mix(dx1, yo, g2, wo, z, au, bu, wmu, wsu, om, tm):
    T = dx1.shape[0]

    def body(dx_ref, yo_ref, g_ref, wo_ref, ga_ref, gb_ref, au_ref, bu_ref, wmu_ref, wsu_ref, om_ref,
             dyo_ref, dg_ref, dau_ref, dbu_ref, dga_ref, dgb_ref, dos_ref, dl_ref, dot_ref):
        @pl.when(pl.program_id(0) == 0)
        def _():
            dg_ref[...] = jnp.zeros_like(dg_ref)

        dyo, dg = _rms_bwd(dx_ref[...], yo_ref[...], g_ref[...])
        dg_ref[...] += dg
        dyob = dyo.astype(BF16)
        dyo_ref[...] = dyob
        dy = _dot_nt(dyob, wo_ref[...])
        sa = _sigmoid(ga_ref[...])
        sb = _sigmoid(gb_ref[...])
        dau = (dy * sa).astype(BF16)
        dbu = (dy * sb).astype(BF16)
        dau_ref[...] = dau
        dbu_ref[...] = dbu
        dga_ref[...] = (dy * au_ref[...] * sa * (1.0 - sa)).astype(BF16)
        dgb_ref[...] = (dy * bu_ref[...] * sb * (1.0 - sb)).astype(BF16)
        dom = _dot_nt(dau, wmu_ref[...])
        dos_ref[...] = _dot_nt(dbu, wsu_ref[...])
        prod = dom * om_ref[...]
        sub = lax.broadcasted_iota(jnp.int32, (8, tm), 0)
        for pr in range(MLA_HEADS // 2):
            sl = slice(LANES * pr, LANES * (pr + 1))
            pt = prod[:, sl].T
            d0 = jnp.sum(pt[0:64], axis=0, keepdims=True)
            d1 = jnp.sum(pt[64:128], axis=0, keepdims=True)
            dl_ref[pr, 0] = jnp.where(sub == 0, d0, jnp.where(sub == 1, d1, 0.0))
            dot_ref[0, sl, :] = dom[:, sl].T.astype(BF16)

    r = _rows(tm, D)
    w = _full((D, D))
    return _pcall(body, name="bwd_mix", grid=(T // tm,),
                  in_specs=[r, r, _full((1, D)), w, _rows(tm, D, 1), _rows(tm, D, 2), r, r, w, w, r],
                  out_specs=[r, _full((1, D)), r, r, r, r, r, pl.BlockSpec((MLA_HEADS // 2, 1, 8, tm), lambda i: (0, i, 0, 0)),
                             pl.BlockSpec((1, D, tm), lambda i: (i, 0, 0))],
                  out_shape=[SDS((T, D), BF16), SDS((1, D), F32), SDS((T, D), BF16), SDS((T, D), BF16), SDS((T, D), BF16),
                             SDS((T, D), BF16), SDS((T, D), F32), SDS((MLA_HEADS // 2, T // tm, 8, tm), F32),
                             SDS((T // tm, D, tm), BF16)],
                  sem=("arbitrary",))(dx1, yo, g2, wo, z, z, au, bu, wmu, wsu, om)


def _mla_bwd(qt, km, kt, vm, dot, lse, delta, tb):
    T = km.shape[0]
    nb = T // tb
    cc = ATT_COLS

    per = 2 if nb % 2 == 0 else 1

    def body(qt_ref, k_ref, kt_ref, v_ref, dot_ref, l_ref, dl_ref, dqt_ref, dkt_ref, dvt_ref,
             s_ref, dp_ref, p_ref, ds_ref, vh_ref):
        @pl.when(pl.program_id(1) == 0)
        def _():
            dqt_ref[...] = jnp.zeros_like(dqt_ref)

        dkt_ref[...] = jnp.zeros_like(dkt_ref)
        dvt_ref[...] = jnp.zeros_like(dvt_ref)
        for blk in range(per):
            one_block(per * pl.program_id(1) + blk, blk, qt_ref, k_ref, kt_ref, v_ref, dot_ref, l_ref, dl_ref, dqt_ref, dkt_ref,
                      dvt_ref, s_ref, dp_ref, p_ref, ds_ref, vh_ref)

    def one_block(j, blk, qt_ref, k_ref, kt_ref, v_ref, dot_ref, l_ref, dl_ref, dqt_ref, dkt_ref, dvt_ref,
                  s_ref, dp_ref, p_ref, ds_ref, vh_ref):
        lo = lax.broadcasted_iota(jnp.int32, (tb, LANES), 1) < 64
        key = lax.broadcasted_iota(jnp.int32, (tb, cc), 0)
        qry = lax.broadcasted_iota(jnp.int32, (tb, cc), 1)
        rows_j = slice(tb * blk, tb * (blk + 1))
        v = v_ref[rows_j, :]
        vh_ref[0] = jnp.where(lo, v, jnp.zeros_like(v))
        vh_ref[1] = jnp.where(lo, jnp.zeros_like(v), v)

        def scores(i, slot):
            for hh in range(2):
                sl = slice(LANES * hh, LANES * (hh + 1))
                s_ref[slot, hh] = _dot(k_ref[rows_j, sl], qt_ref[i, sl, :])
                dp_ref[slot, hh] = _dot(vh_ref[hh], dot_ref[i])

        def grads(i, slot, diagonal):
            lse_i = l_ref[0, i]
            delta_i = dl_ref[0, i]
            for hh in range(2):
                for c in range(tb // cc):
                    cols = slice(cc * c, cc * (c + 1))
                    p = jnp.exp2(s_ref[slot, hh, :, cols] * MLA_LOG2_SCALE - lse_i[hh:hh + 1, cols])
                    if diagonal:
                        p = jnp.where(key <= qry + cc * c, p, 0.0)
                    p_ref[hh, :, cols] = p.astype(BF16)
                    ds_ref[hh, :, cols] = (p * (dp_ref[slot, hh, :, cols] - delta_i[hh:hh + 1, cols]) * MLA_SCALE).astype(BF16)
            for hh in range(2):
                sl = slice(LANES * hh, LANES * (hh + 1))
                half = slice(64 * hh, 64 * (hh + 1))
                dvt_ref[blk, half, :] += _dot_nt(dot_ref[i, half, :], p_ref[hh])
                real = slice(LANES * hh, LANES * hh + MLA_NOPE + MLA_ROPE)
                dkt_ref[blk, real, :] += _dot_nt(qt_ref[i, real, :], ds_ref[hh])
                dqt_ref[i, real, :] += _dot(kt_ref[blk, real, :], ds_ref[hh])

        n_off = nb - 1 - j

        def step(u, carry):
            i0 = j + 1 + 2 * u
            scores(i0 + 1, 1)
            grads(i0, 0, False)
            scores(jnp.where(i0 + 2 < nb, i0 + 2, j), 0)
            grads(i0 + 1, 1, False)
            return carry

        scores(jnp.where(n_off > 0, j + 1, j), 0)
        lax.fori_loop(0, n_off // 2, step, 0)

        @pl.when(n_off % 2 == 1)
        def _():
            scores(j, 1)
            grads(nb - 1, 0, False)
            grads(j, 1, True)

        @pl.when(n_off % 2 == 0)
        def _():
            grads(j, 0, True)

    blk = lambda w: pl.BlockSpec((per * tb, w), lambda p, j: (j, p))
    stat = pl.BlockSpec((1, nb, 8, tb), lambda p, j: (p, 0, 0, 0))
    pair_t = lambda w: pl.BlockSpec((nb, w, tb), lambda p, j: (0, p, 0))
    blk_t = lambda w: pl.BlockSpec((per, w, tb), lambda p, j: (j, p, 0))
    return _pcall(body, name="mla_bwd", grid=(MLA_HEADS // 2, nb // per),
                  in_specs=[pair_t(256), blk(256), blk_t(256), blk(LANES), pair_t(LANES), stat, stat],
                  out_specs=[pair_t(256), blk_t(256), blk_t(LANES)],
                  out_shape=[SDS((nb, 2048, tb), F32), SDS((nb, 2048, tb), F32), SDS((nb, D, tb), F32)],
                  scratch=[pltpu.VMEM((2, 2, tb, tb), F32), pltpu.VMEM((2, 2, tb, tb), F32), pltpu.VMEM((2, tb, tb), BF16),
                           pltpu.VMEM((2, tb, tb), BF16), pltpu.VMEM((2, tb, LANES), BF16)],
                  sem=("parallel", "arbitrary"))(qt, km, kt, vm, dot, lse, delta)


def _swa_bwd(sinks, qs, ks, vs, do, o, lse):
    T = qs.shape[0]
    nb, cur, prev = _swa_specs(T)

    def body(sink_ref, q_ref, kc_ref, kp_ref, vc_ref, vp_ref, do_ref, o_ref, l_ref,
             dq_ref, dkc_ref, dkp_ref, dvc_ref, dvp_ref, dsink_ref, kb_ref, vb_ref, s_ref, dp_ref, p_ref, ds_ref):
        n = pl.program_id(0)

        @pl.when(n == 0)
        def _():
            dsink_ref[...] = jnp.zeros_like(dsink_ref)

        mask = _swa_mask(n)
        lo = lax.broadcasted_iota(jnp.int32, (WINDOW, LANES), 1) < 64
        hi = jnp.logical_not(lo)
        lane8 = lax.broadcasted_iota(jnp.int32, (8, LANES), 1)
        for g in range(2):
            gs = slice(LANES * g, LANES * (g + 1))
            kb_ref[g] = jnp.concatenate([kp_ref[:, gs], kc_ref[:, gs]], axis=0)
            vb_ref[g] = jnp.concatenate([vp_ref[:, gs], vc_ref[:, gs]], axis=0)

        def head(h):
            sl = slice(LANES * (h // 2), LANES * (h // 2 + 1))
            hm = lo if h % 2 == 0 else hi
            qp = q_ref[:, sl]
            return hm, sl, jnp.where(hm, qp, jnp.zeros_like(qp)), jnp.where(hm, do_ref[:, sl], 0.0).astype(BF16)

        for h in range(SWA_HEADS):
            _, _, qh, dom = head(h)
            s_ref[h] = _dot_nt(qh, kb_ref[h // 8])
            dp_ref[h] = _dot_nt(dom, vb_ref[h // 8])
        dsink = jnp.zeros((8, LANES), F32)
        for h in range(SWA_HEADS):
            hm, sl, _, _ = head(h)
            lse_h = jnp.max(jnp.where(hm, l_ref[:, sl], -jnp.inf), axis=1, keepdims=True)
            delta = jnp.sum(jnp.where(hm, do_ref[:, sl] * o_ref[:, sl], 0.0), axis=1, keepdims=True)
            p = jnp.exp(jnp.where(mask, s_ref[h] * SWA_SCALE, NEG) - lse_h)
            p_ref[h] = p.astype(BF16)
            ds_ref[h] = (p * (dp_ref[h] - delta) * SWA_SCALE).astype(BF16)
            d_sink = -jnp.sum(jnp.exp(sink_ref[h] - lse_h) * delta, axis=0, keepdims=True)
            dsink = dsink + jnp.where(lane8 == h, d_sink, 0.0)
        dsink_ref[...] += dsink
        for g in range(2):
            gs = slice(LANES * g, LANES * (g + 1))
            dkb = jnp.zeros((2 * WINDOW, LANES), F32)
            dvb = jnp.zeros((2 * WINDOW, LANES), F32)
            for j in range(4 * g, 4 * g + 4):
                dqs = []
                for h in (2 * j, 2 * j + 1):
                    _, _, qh, dom = head(h)
                    dvb = dvb + _dot_tn(p_ref[h], dom)
                    dkb = dkb + _dot_tn(ds_ref[h], qh)
                    dqs.append(_dot(ds_ref[h], kb_ref[g]))
                dq_ref[:, LANES * j:LANES * (j + 1)] = jnp.where(lo, dqs[0], dqs[1])
            dkp_ref[:, gs] = dkb[:WINDOW]
            dkc_ref[:, gs] = dkb[WINDOW:]
            dvp_ref[:, gs] = dvb[:WINDOW]
            dvc_ref[:, gs] = dvb[WINDOW:]

    band = pltpu.VMEM((2, 2 * WINDOW, LANES), BF16)
    return _pcall(body, name="swa_bwd", grid=(nb,),
                  in_specs=[pl.BlockSpec(memory_space=pltpu.SMEM), cur(D), cur(256), prev(256), cur(256), prev(256),
                            cur(D), cur(D), cur(D)],
                  out_specs=[cur(D), cur(256), cur(256), cur(256), cur(256), _full((8, LANES))],
                  out_shape=[SDS((T, D), F32), SDS((T, 256), F32), SDS((T, 256), F32), SDS((T, 256), F32), SDS((T, 256), F32),
                             SDS((8, LANES), F32)],
                  scratch=[band, band, pltpu.VMEM((SWA_HEADS, WINDOW, 2 * WINDOW), F32),
                           pltpu.VMEM((SWA_HEADS, WINDOW, 2 * WINDOW), F32), pltpu.VMEM((SWA_HEADS, WINDOW, 2 * WINDOW), BF16),
                           pltpu.VMEM((SWA_HEADS, WINDOW, 2 * WINDOW), BF16)],
                  sem=("arbitrary",))(sinks, qs, ks, ks, vs, vs, do, o, lse)


def _bwd_qkv(dqm, dkm, dvm, dqs, dkc, dkp, dvc, dvp, z, gq, gkv, wqb, wkn, wv, tab_m, tab_s):
    T = z.shape[0]
    tm = WINDOW
    nb = T // tm
    per = dqm.shape[2] // tm

    tab_mt = [t.T for t in tab_m]
    half = MLA_ROPE // 2

    def rope_t(v, c, a, b):
        return v * c + pltpu.roll(v, LANES - half, 0) * a + pltpu.roll(v, half, 0) * b

    def rms_bwd_t(dy, x, g):
        r = lax.rsqrt(jnp.mean(x * x, axis=0, keepdims=True) + EPS)
        xn = x * r
        dn = dy * g
        return r * (dn - xn * jnp.mean(dn * xn, axis=0, keepdims=True)), jnp.sum(dy * xn, axis=1, keepdims=True)

    def body(dqm_ref, dkm_ref, dvm_ref, dqs_ref, dkc_ref, dkp_ref, dvc_ref, dvp_ref, qa_ref, kva_ref, gq_ref, gkv_ref,
             wqb_ref, wkn_ref, wv_ref, cmt_ref, amt_ref, bmt_ref, cs_ref, as_ref, bs_ref,
             dq_out, dkn_out, dv_out, dsq_ref, drest_ref, dgq_ref, dgkv_ref):
        i = pl.program_id(0)

        @pl.when(i == 0)
        def _():
            dgq_ref[...] = jnp.zeros_like(dgq_ref)
            dgkv_ref[...] = jnp.zeros_like(dgkv_ref)

        cmt, amt, bmt = cmt_ref[...], -amt_ref[...], -bmt_ref[...]
        cs, as_, bs = cs_ref[...], -as_ref[...], -bs_ref[...]
        row = lax.broadcasted_iota(jnp.int32, (LANES, tm), 0)
        nope = row < MLA_NOPE
        roped = jnp.logical_and(row >= MLA_NOPE, row < MLA_NOPE + MLA_ROPE)
        dkr = jnp.zeros((LANES, tm), F32)
        for h in range(MLA_HEADS):
            sl = slice(LANES * h, LANES * (h + 1))
            dq_out[0, sl, :] = rope_t(dqm_ref[0, sl, :], cmt, amt, bmt).astype(BF16)
            dk_h = dkm_ref[0, sl, :]
            dkn_out[0, sl, :] = jnp.where(nope, dk_h, 0.0).astype(BF16)
            dkr = dkr + jnp.where(roped, dk_h, 0.0)
        dv_out[0] = dvm_ref[0].astype(BF16)
        dqn = _dot(wqb_ref[...], dq_out[0])
        dkvn = _dot(wkn_ref[...], dkn_out[0]) + _dot(wv_ref[...], dv_out[0])
        dqa, dgq = rms_bwd_t(dqn, qa_ref[...].T, gq_ref[...])
        dkva, dgkv = rms_bwd_t(dkvn, kva_ref[...].T, gkv_ref[...])
        dgq_ref[...] += dgq
        dgkv_ref[...] += dgkv
        for j in range(D // LANES):
            sl = slice(LANES * j, LANES * (j + 1))
            dsq_ref[:, sl] = _rope(dqs_ref[:, sl], cs, as_, bs, SWA_HD // 2).astype(BF16)
        keep = (i < nb - 1).astype(F32)
        drest_ref[:, 0:256] = dqa.T.astype(BF16)
        for j in range(2):
            sl = slice(LANES * j, LANES * (j + 1))
            dk = dkc_ref[:, sl] + keep * dkp_ref[:, sl]
            drest_ref[:, 256 + LANES * j:256 + LANES * (j + 1)] = _rope(dk, cs, as_, bs, SWA_HD // 2).astype(BF16)
        drest_ref[:, 512:768] = (dvc_ref[...] + keep * dvp_ref[...]).astype(BF16)
        drest_ref[:, 768:896] = dkva.T.astype(BF16)
        drest_ref[:, 896:1024] = rope_t(dkr, cmt, amt, bmt).T.astype(BF16)

    nxt = pl.BlockSpec((tm, 256), lambda i: (jnp.minimum(i + 1, nb - 1), 0))
    tab = [_rows(tm, LANES)] * 3
    tab_t = [pl.BlockSpec((LANES, tm), lambda i: (0, i))] * 3
    blk_t = lambda w: pl.BlockSpec((1, w, tm), lambda i: (i // per, 0, i % per))
    return _pcall(body, name="bwd_qkv", grid=(nb,),
                  in_specs=[blk_t(2048), blk_t(2048), blk_t(1024), _rows(tm, 1024), _rows(tm, 256), nxt,
                            _rows(tm, 256), nxt, _rows(tm, 256, 12), _rows(tm, 128, 30), _full((Q_LORA, 1)), _full((KV_LORA, 1)),
                            _full((Q_LORA, 2048)), _full((KV_LORA, 2048)), _full((KV_LORA, 1024))] + tab_t + tab,
                  out_specs=[blk_t(2048), blk_t(2048), blk_t(1024), _rows(tm, 1024), _rows(tm, 1024),
                             _full((Q_LORA, 1)), _full((KV_LORA, 1))],
                  out_shape=[SDS(dqm.shape, BF16), SDS(dkm.shape, BF16), SDS(dvm.shape, BF16), SDS((T, 1024), BF16),
                             SDS((T, 1024), BF16), SDS((Q_LORA, 1), F32), SDS((KV_LORA, 1), F32)],
                  sem=("arbitrary",))(dqm, dkm, dvm, dqs, dkc, dkp, dvc, dvp, z, z, gq.reshape(Q_LORA, 1),
                                      gkv.reshape(KV_LORA, 1), wqb, wkn, wv, *tab_mt, *tab_s)


def _bwd_in(dsq, dga, dgb, drest, w_in_p, x, g1, dx1, tm):
    T = x.shape[0]

    def body(a_ref, b_ref, c_ref, d_ref, w_ref, x_ref, g_ref, dx1_ref, dx_ref, dg_ref):
        @pl.when(pl.program_id(0) == 0)
        def _():
            dg_ref[...] = jnp.zeros_like(dg_ref)

        dh = (_dot_nt(a_ref[...], w_ref[:, 0:1024]) + _dot_nt(b_ref[...], w_ref[:, 1024:2048])
              + _dot_nt(c_ref[...], w_ref[:, 2048:3072]) + _dot_nt(d_ref[...], w_ref[:, 3072:4096]))
        dx, dg = _rms_bwd(dh, x_ref[...], g_ref[...])
        dg_ref[...] += dg
        dx_ref[...] = dx1_ref[...] + dx

    r = _rows(tm, D)
    return _pcall(body, name="bwd_in", grid=(T // tm,),
                  in_specs=[r, r, r, r, _full((D, NZ)), r, _full((1, D)), r],
                  out_specs=[r, _full((1, D))], out_shape=[SDS((T, D), F32), SDS((1, D), F32)],
                  sem=("arbitrary",))(dsq, dga, dgb, drest, w_in_p, x, g1, dx1)


def _wgrad(a, g, name, into=None):
    T, K = a.shape
    N = g.shape[1]
    tk, tn, tt = min(K, 1024), min(N, 1024), min(T, 1024)
    if into is not None:
        buf, weight = into
        _, row0, lane0 = PACK_AT[weight]
        shard = {n: (r, c) for n, r, c in BIG}[weight]
        assert lane0 == 0 and shard[1] == D and tk % shard[0] == 0
        per_step = tk // shard[0]
    assert K % tk == 0 and N % tn == 0 and T % tt == 0, (a.shape, g.shape)
    steps = T // tt

    def body(a_ref, g_ref, *rest):
        o_ref, acc_ref = rest[-2:]
        t = pl.program_id(2)

        @pl.when(t == 0)
        def _():
            acc_ref[...] = jnp.zeros_like(acc_ref)

        acc_ref[...] += _dot_tn(a_ref[...].astype(BF16), g_ref[...].astype(BF16))

        @pl.when(t == steps - 1)
        def _():
            o_ref[...] = acc_ref[...].astype(o_ref.dtype).reshape(o_ref.shape)

    in_specs = [pl.BlockSpec((tt, tk), lambda k, n, t: (t, k)), pl.BlockSpec((tt, tn), lambda k, n, t: (t, n))]
    if into is None:
        return _pcall(body, name=name, grid=(K // tk, N // tn, steps), in_specs=in_specs,
                      out_specs=pl.BlockSpec((tk, tn), lambda k, n, t: (k, n)), out_shape=SDS((K, N), F32),
                      scratch=[pltpu.VMEM((tk, tn), F32)], sem=("parallel", "parallel", "arbitrary"))(a, g)
    assert row0 % shard[0] == 0 and (K // tk) * (N // tn) * per_step == N_CHIPS
    return _pcall(body, name=name, grid=(K // tk, N // tn, steps), in_specs=in_specs + [ANY],
                  out_specs=pl.BlockSpec((per_step, shard[0], tn), lambda k, n, t: (k + n, row0 // shard[0], 0)),
                  out_shape=SDS(buf.shape, buf.dtype),
                  scratch=[pltpu.VMEM((tk, tn), F32)], sem=("parallel", "parallel", "arbitrary"), aliases={2: 0})(a, g, buf)


def _wgrad_t(at, g, name):
    nblk, K, tt = at.shape
    N = g.shape[1]
    tk = min(K, 1024)
    per_step = 4 if nblk % 4 == 0 else 1
    assert K % tk == 0 and g.shape[0] == nblk * tt

    def body(a_ref, g_ref, o_ref):
        @pl.when(pl.program_id(1) == 0)
        def _():
            o_ref[...] = jnp.zeros_like(o_ref)

        acc = _dot(a_ref[0], g_ref[0:tt, :].astype(BF16))
        for b in range(1, per_step):
            acc = acc + _dot(a_ref[b], g_ref[tt * b:tt * (b + 1), :].astype(BF16))
        o_ref[...] += acc

    return _pcall(body, name=name, grid=(K // tk, nblk // per_step),
                  in_specs=[pl.BlockSpec((per_step, tk, tt), lambda k, t: (t, k, 0)),
                            pl.BlockSpec((per_step * tt, N), lambda k, t: (t, 0))],
                  out_specs=pl.BlockSpec((tk, N), lambda k, t: (k, 0)), out_shape=SDS((K, N), F32),
                  sem=("parallel", "arbitrary"))(at, g)


def _adamw(w, packed_g, m, v, name, transposed=False):
    R, C = w.shape[1:][::-1] if transposed else w.shape[1:]
    _, row0, lane0 = PACK_AT[name]
    tr = min(R, 256 if row0 % 256 == 0 else 128)
    assert row0 % tr == 0 and R % tr == 0

    def body(w_ref, g_ref, m_ref, v_ref, go_ref, d_ref, m2_ref, v2_ref):
        g_ = g_ref[...].T[lane0:lane0 + C] if transposed else g_ref[:, lane0:lane0 + C]
        go_ref[0] = g_
        m2 = ADAM_B1 * m_ref[0] + (1.0 - ADAM_B1) * g_
        v2 = ADAM_B2 * v_ref[0] + (1.0 - ADAM_B2) * jnp.square(g_)
        m_hat = m2 / (1.0 - ADAM_B1 ** ADAM_STEP)
        v_hat = v2 / (1.0 - ADAM_B2 ** ADAM_STEP)
        d_ref[0] = -ADAM_LR * (m_hat / (jnp.sqrt(v_hat) + ADAM_EPS) + ADAM_WD * w_ref[0])
        m2_ref[0] = m2
        v2_ref[0] = v2

    r = pl.BlockSpec((1, C, tr), lambda i: (0, 0, i)) if transposed else pl.BlockSpec((1, tr, C), lambda i: (0, i, 0))
    return _pcall(body, name="adamw_" + name, grid=(R // tr,),
                  in_specs=[r, pl.BlockSpec((tr, D), lambda i: (row0 // tr + i, 0)), r, r], out_specs=[r] * 4,
                  out_shape=[SDS(w.shape, F32)] * 4, sem=("parallel",))(w, packed_g, m, v)


def _adamw_small(w, parts, m, v):
    def body(w_ref, p_ref, m_ref, v_ref, g_ref, d_ref, m2_ref, v2_ref):
        g_ = p_ref[0]
        for k in range(1, N_DEV):
            g_ = g_ + p_ref[k]
        g_ref[...] = g_
        m2 = ADAM_B1 * m_ref[...] + (1.0 - ADAM_B1) * g_
        v2 = ADAM_B2 * v_ref[...] + (1.0 - ADAM_B2) * jnp.square(g_)
        m_hat = m2 / (1.0 - ADAM_B1 ** ADAM_STEP)
        v_hat = v2 / (1.0 - ADAM_B2 ** ADAM_STEP)
        d_ref[...] = -ADAM_LR * (m_hat / (jnp.sqrt(v_hat) + ADAM_EPS) + ADAM_WD * w_ref[...])
        m2_ref[...] = m2
        v2_ref[...] = v2

    s = _full((8, D))
    return _pcall(body, name="adamw_small", grid=(1,), in_specs=[s, _full((N_DEV, 8, D)), s, s], out_specs=[s] * 4,
                  out_shape=[SDS((8, D), F32)] * 4, sem=("arbitrary",))(w, parts, m, v)


ANY = pl.BlockSpec(memory_space=pl.ANY)


def _place():
    x, y, c = lax.axis_index("x"), lax.axis_index("y"), lax.axis_index("c")
    chips = [(1 - x, y), (x, 1 - y), (1 - x, 1 - y)]
    return x, y, c, chips


def _all_gather(wpk):
    rows = wpk.shape[0]
    HALF = rows // 2
    assert HALF % 16 == 0

    def body(in_ref, out_ref, send_sems, recv_sems):
        x, y, c, chips = _place()
        half = pl.ds(pl.multiple_of(c * HALF, 16), HALF)
        other = pl.ds(pl.multiple_of((1 - c) * HALF, 16), HALF)

        def copy(k, src, dst, to):
            return pltpu.make_async_remote_copy(src_ref=src, dst_ref=dst, send_sem=send_sems.at[k], recv_sem=recv_sems.at[k],
                                                device_id=to, device_id_type=MESH)

        first = [copy(k, in_ref.at[half], out_ref.at[2 * x + y, half], (cx, cy, c)) for k, (cx, cy) in enumerate(chips)]
        first.append(copy(6, in_ref, out_ref.at[2 * x + y], (x, y, 1 - c)))
        for cp in first:
            cp.start()
        passed = []
        for k, (cx, cy) in enumerate(chips):
            slot = out_ref.at[2 * cx + cy, half]
            copy(k, slot, slot, (x, y, c)).wait_recv()
            fwd = copy(3 + k, slot, slot, (x, y, 1 - c))
            fwd.start()
            passed.append(fwd)
        for k, (cx, cy) in enumerate(chips):
            slot = out_ref.at[2 * cx + cy, other]
            copy(3 + k, slot, slot, (x, y, c)).wait_recv()
        copy(6, in_ref, out_ref.at[2 * x + y], (x, y, c)).wait_recv()
        for cp in first + passed:
            cp.wait_send()

    return _pcall(body, name="all_gather_weights", in_specs=[ANY], out_specs=ANY,
                  out_shape=SDS((N_CHIPS, rows, D), BF16),
                  scratch=[pltpu.SemaphoreType.DMA((7,)), pltpu.SemaphoreType.DMA((7,))])(wpk)


HBM = pl.BlockSpec(memory_space=pltpu.HBM)
SEM = pl.BlockSpec(memory_space=pltpu.SEMAPHORE)
DATAFLOW = pltpu.SideEffectType.DATAFLOW_SIDE_EFFECTING


def _in_hbm(a):
    return pltpu.with_memory_space_constraint(a, pltpu.HBM)


def _gather_late_start(wpk, after):
    rows = wpk.shape[0]

    def body(in_ref, land_ref, after_ref, send_sems, recv_sems, in_thru, land_thru, token):
        x, y, c, chips = _place()
        for k, to in enumerate([(cx, cy, c) for cx, cy in chips] + [(x, y, 1 - c)]):
            pltpu.make_async_remote_copy(src_ref=in_ref, dst_ref=land_ref.at[2 * x + y], send_sem=send_sems.at[k],
                                         recv_sem=recv_sems.at[k], device_id=to, device_id_type=MESH).start()
        token[...] = jnp.zeros_like(token)

    return pl.pallas_call(
        body, name="gather_late_start",
        out_shape=(pltpu.SemaphoreType.DMA((4,)), pltpu.SemaphoreType.DMA((4,)), pltpu.HBM(wpk.shape, wpk.dtype),
                   pltpu.HBM((N_CHIPS, rows, D), wpk.dtype), SDS((8, LANES), F32)),
        in_specs=(HBM, HBM, ANY), out_specs=(SEM, SEM, HBM, HBM, pl.BlockSpec(memory_space=pltpu.VMEM)),
        input_output_aliases={0: 2, 1: 3}, compiler_params=pltpu.CompilerParams(has_side_effects=DATAFLOW),
    )(_in_hbm(wpk), _in_hbm(lax.empty((N_CHIPS, rows, D), wpk.dtype)), after)


def _gather_late_wait(send_sems, recv_sems, in_thru, land_thru, after):
    def body(in_ref, land_ref, send_sems, recv_sems, after_ref, after2_ref, in_dead, got_ref):
        x, y, c, chips = _place()
        for k, (sx, sy) in enumerate(chips + [(x, y)]):
            cp = pltpu.make_async_remote_copy(src_ref=in_ref, dst_ref=land_ref.at[2 * sx + sy], send_sem=send_sems.at[k],
                                              recv_sem=recv_sems.at[k], device_id=(x, y, c), device_id_type=MESH)
            cp.wait_send()
            cp.wait_recv()

    return pl.pallas_call(
        body, name="gather_late_wait",
        out_shape=(pltpu.HBM(in_thru.shape, in_thru.dtype), pltpu.HBM(land_thru.shape, land_thru.dtype)),
        in_specs=(HBM, HBM, SEM, SEM, ANY, ANY), out_specs=(HBM, HBM), input_output_aliases={0: 0, 1: 1},
        compiler_params=pltpu.CompilerParams(has_side_effects=DATAFLOW),
    )(in_thru, land_thru, send_sems, recv_sems, *after)[1]


def _rs_sibling(gpk):
    HALF = gpk.shape[1] // 2

    def body(in_ref, out_ref, send_sem, recv_sem):
        x, y, c, _ = _place()
        theirs = pl.ds(pl.multiple_of((1 - c) * HALF, 16), HALF)
        cp = pltpu.make_async_remote_copy(src_ref=in_ref.at[:, theirs], dst_ref=out_ref, send_sem=send_sem, recv_sem=recv_sem,
                                          device_id=(x, y, 1 - c), device_id_type=MESH)
        cp.start()
        cp.wait()

    return _pcall(body, name="rs_sibling", in_specs=[ANY], out_specs=ANY, out_shape=SDS((N_CHIPS, HALF, D), gpk.dtype),
                  scratch=[pltpu.SemaphoreType.DMA, pltpu.SemaphoreType.DMA])(gpk)


def _rs_add_sibling(cidx, gpk, got):
    HALF = got.shape[1]
    th = HALF // 4
    nh = HALF // th
    assert th % 16 == 0

    def body(c_ref, a_ref, b_ref, o_ref):
        o_ref[...] = (a_ref[...].astype(F32) + b_ref[...].astype(F32)).astype(BF16)

    gs = pltpu.PrefetchScalarGridSpec(
        num_scalar_prefetch=1, grid=(N_CHIPS, nh),
        in_specs=[pl.BlockSpec((1, th, D), lambda j, i, c: (j, c[0] * nh + i, 0)), pl.BlockSpec((1, th, D), lambda j, i, c: (j, i, 0))],
        out_specs=pl.BlockSpec((1, th, D), lambda j, i, c: (j, i, 0)))
    return pl.pallas_call(body, name="rs_add_sibling", grid_spec=gs, out_shape=SDS((N_CHIPS, HALF, D), BF16),
                          compiler_params=pltpu.CompilerParams(dimension_semantics=("parallel", "parallel"),
                                                               vmem_limit_bytes=48 << 20))(cidx, gpk, got)


def _rs_chips_start(part, small, after):
    def body(p_ref, s_ref, land_ref, sland_ref, after_ref, send_sems, recv_sems, p_thru, s_thru, land_thru, sland_thru, token):
        x, y, c, chips = _place()
        for k, (cx, cy) in enumerate(chips):
            pltpu.make_async_remote_copy(src_ref=p_ref.at[2 * cx + cy], dst_ref=land_ref.at[2 * x + y], send_sem=send_sems.at[k],
                                         recv_sem=recv_sems.at[k], device_id=(cx, cy, c), device_id_type=MESH).start()
        peers = [(x, y, 1 - c)] + [(cx, cy, c) for cx, cy in chips] + [(cx, cy, 1 - c) for cx, cy in chips]
        for k, to in enumerate(peers):
            pltpu.make_async_remote_copy(src_ref=s_ref, dst_ref=sland_ref.at[4 * x + 2 * y + c], send_sem=send_sems.at[3 + k],
                                         recv_sem=recv_sems.at[3 + k], device_id=to, device_id_type=MESH).start()
        token[...] = jnp.zeros_like(token)

    return pl.pallas_call(
        body, name="rs_chips_start",
        out_shape=(pltpu.SemaphoreType.DMA((10,)), pltpu.SemaphoreType.DMA((10,)), pltpu.HBM(part.shape, part.dtype),
                   pltpu.HBM(small.shape, small.dtype), pltpu.HBM(part.shape, part.dtype), pltpu.HBM((N_DEV, 8, D), F32),
                   SDS((8, LANES), F32)),
        in_specs=(HBM, HBM, HBM, HBM, ANY), out_specs=(SEM, SEM, HBM, HBM, HBM, HBM, pl.BlockSpec(memory_space=pltpu.VMEM)),
        input_output_aliases={0: 2, 1: 3, 2: 4, 3: 5}, compiler_params=pltpu.CompilerParams(has_side_effects=DATAFLOW),
    )(_in_hbm(part), _in_hbm(small), _in_hbm(lax.empty(part.shape, part.dtype)), _in_hbm(lax.empty((N_DEV, 8, D), F32)), after)


def _rs_chips_wait(send_sems, recv_sems, p_thru, s_thru, land_thru, sland_thru, after):
    def body(p_ref, s_ref, land_ref, sland_ref, send_sems, recv_sems, *after_and_outputs):
        x, y, c, chips = _place()
        for k, (cx, cy) in enumerate(chips):
            cp = pltpu.make_async_remote_copy(src_ref=p_ref.at[0], dst_ref=land_ref.at[2 * cx + cy], send_sem=send_sems.at[k],
                                              recv_sem=recv_sems.at[k], device_id=(cx, cy, c), device_id_type=MESH)
            cp.wait_send()
            cp.wait_recv()
        peers = [(x, y, 1 - c)] + [(cx, cy, c) for cx, cy in chips] + [(cx, cy, 1 - c) for cx, cy in chips]
        for k, (px, py, pc) in enumerate(peers):
            cp = pltpu.make_async_remote_copy(src_ref=s_ref, dst_ref=sland_ref.at[4 * px + 2 * py + pc], send_sem=send_sems.at[3 + k],
                                              recv_sem=recv_sems.at[3 + k], device_id=(px, py, pc), device_id_type=MESH)
            cp.wait_send()
            cp.wait_recv()

    hbm = lambda a: pltpu.HBM(a.shape, a.dtype)
    outs = pl.pallas_call(
        body, name="rs_chips_wait", out_shape=(hbm(p_thru), hbm(s_thru), hbm(land_thru), hbm(sland_thru)),
        in_specs=(HBM, HBM, HBM, HBM, SEM, SEM) + (ANY,) * len(after), out_specs=(HBM, HBM, HBM, HBM),
        input_output_aliases={0: 0, 1: 1, 2: 2, 3: 3}, compiler_params=pltpu.CompilerParams(has_side_effects=DATAFLOW),
    )(p_thru, s_thru, land_thru, sland_thru, send_sems, recv_sems, *after)
    return outs[0], outs[2], outs[3]


def _rs_add_chips(qidx, part, parts):
    HALF = part.shape[1]
    th = HALF // 4
    nh = HALF // th
    assert th % 16 == 0

    def body(q_ref, own_ref, p_ref, o_ref):
        for me in range(N_CHIPS):
            @pl.when(q_ref[0] == me)
            def _(me=me):
                t = [(own_ref[0] if j == me else p_ref[j]).astype(F32) for j in range(N_CHIPS)]
                o_ref[...] = ((t[0] + t[1]) + t[2]) + t[3]

    gs = pltpu.PrefetchScalarGridSpec(
        num_scalar_prefetch=1, grid=(HALF // th,),
        in_specs=[pl.BlockSpec((1, th, D), lambda i, q: (q[0], i, 0)), pl.BlockSpec((N_CHIPS, th, D), lambda i, q: (0, i, 0))],
        out_specs=pl.BlockSpec((th, D), lambda i, q: (q[1] * nh + i, 0)))
    return pl.pallas_call(body, name="rs_add_chips", grid_spec=gs, out_shape=SDS((2 * HALF, D), F32),
                          compiler_params=pltpu.CompilerParams(dimension_semantics=("parallel",),
                                                               vmem_limit_bytes=48 << 20))(qidx, part, parts)


def _rs_join(shard, name):
    HALF = shard.shape[0] // 2

    def body(in_ref, out_ref, send_sem, recv_sem):
        x, y, c, _ = _place()
        rows = pl.ds(pl.multiple_of(c * HALF, 16), HALF)
        cp = pltpu.make_async_remote_copy(src_ref=in_ref.at[rows], dst_ref=out_ref.at[rows], send_sem=send_sem, recv_sem=recv_sem,
                                          device_id=(x, y, 1 - c), device_id_type=MESH)
        cp.start()
        cp.wait()

    return _pcall(body, name=name, in_specs=[ANY], out_specs=ANY, out_shape=SDS(shard.shape, F32),
                  scratch=[pltpu.SemaphoreType.DMA, pltpu.SemaphoreType.DMA], aliases={0: 0})(shard)


def _reduce_late_start(gpk, after):
    rows = gpk.shape[1]
    HALF = rows // 2
    assert HALF % 16 == 0

    def body(in_ref, land_ref, after_ref, send_sems, recv_sems, in_thru, land_thru, token):
        x, y, c, chips = _place()
        me = 4 * x + 2 * y + c
        peers = [(x, y, 1 - c)] + [(cx, cy, c) for cx, cy in chips] + [(cx, cy, 1 - c) for cx, cy in chips]
        for k, (px, py, pc) in enumerate(peers):
            src = in_ref.at[2 * px + py, pl.ds(pl.multiple_of(pc * HALF, 16), HALF)]
            pltpu.make_async_remote_copy(src_ref=src, dst_ref=land_ref.at[me], send_sem=send_sems.at[k], recv_sem=recv_sems.at[k],
                                         device_id=(px, py, pc), device_id_type=MESH).start()
        token[...] = jnp.zeros_like(token)

    return pl.pallas_call(
        body, name="reduce_late_start",
        out_shape=(pltpu.SemaphoreType.DMA((7,)), pltpu.SemaphoreType.DMA((7,)), pltpu.HBM(gpk.shape, gpk.dtype),
                   pltpu.HBM((N_DEV, HALF, D), gpk.dtype), SDS((8, LANES), F32)),
        in_specs=(HBM, HBM, ANY), out_specs=(SEM, SEM, HBM, HBM, pl.BlockSpec(memory_space=pltpu.VMEM)),
        input_output_aliases={0: 2, 1: 3}, compiler_params=pltpu.CompilerParams(has_side_effects=DATAFLOW),
    )(_in_hbm(gpk), _in_hbm(lax.empty((N_DEV, HALF, D), gpk.dtype)), after)


def _reduce_late_wait(send_sems, recv_sems, in_thru, land_thru, after):
    def body(in_ref, land_ref, send_sems, recv_sems, after_ref, in_out, got_ref):
        x, y, c, chips = _place()
        peers = [(x, y, 1 - c)] + [(cx, cy, c) for cx, cy in chips] + [(cx, cy, 1 - c) for cx, cy in chips]
        for k, (px, py, pc) in enumerate(peers):
            cp = pltpu.make_async_remote_copy(src_ref=land_ref.at[0], dst_ref=land_ref.at[4 * px + 2 * py + pc],
                                              send_sem=send_sems.at[k], recv_sem=recv_sems.at[k],
                                              device_id=(px, py, pc), device_id_type=MESH)
            cp.wait_send()
            cp.wait_recv()

    return pl.pallas_call(
        body, name="reduce_late_wait",
        out_shape=(pltpu.HBM(in_thru.shape, in_thru.dtype), pltpu.HBM(land_thru.shape, land_thru.dtype)),
        in_specs=(HBM, HBM, SEM, SEM, ANY), out_specs=(HBM, HBM), input_output_aliases={0: 0, 1: 1},
        compiler_params=pltpu.CompilerParams(has_side_effects=DATAFLOW),
    )(in_thru, land_thru, send_sems, recv_sems, after)


def _reduce_late_add(didx, gpk, parts):
    HALF = parts.shape[1]
    th = HALF // 4
    nh = HALF // th
    assert th % 16 == 0

    def body(d_ref, own_ref, p_ref, o_ref):
        for me in range(N_DEV):
            @pl.when(d_ref[0] == me)
            def _(me=me):
                t = [(own_ref[0] if j == me else p_ref[j]).astype(F32) for j in range(N_DEV)]
                o_ref[...] = ((((((t[0] + t[1]) + t[2]) + t[3]) + t[4]) + t[5]) + t[6]) + t[7]

    gs = pltpu.PrefetchScalarGridSpec(
        num_scalar_prefetch=1, grid=(nh,),
        in_specs=[pl.BlockSpec((1, th, D), lambda i, d: (d[1], d[2] * nh + i, 0)), pl.BlockSpec((N_DEV, th, D), lambda i, d: (0, i, 0))],
        out_specs=pl.BlockSpec((th, D), lambda i, d: (d[2] * nh + i, 0)))
    return pl.pallas_call(body, name="reduce_late_add", grid_spec=gs, out_shape=SDS((2 * HALF, D), F32),
                          compiler_params=pltpu.CompilerParams(dimension_semantics=("parallel",),
                                                               vmem_limit_bytes=48 << 20))(didx, gpk, parts)


def _pack_early(b, dtype):
    lanes = lambda a: jnp.pad(a.astype(dtype), ((0, 0), (0, D - a.shape[1])))
    pair = jnp.concatenate([b["w_q_b"].astype(dtype), b["w_ple"].astype(dtype), jnp.zeros((256, D - 640), dtype)], axis=1)
    return jnp.concatenate([lanes(b["w_in"]), pair, lanes(b["w_kv_b"])], axis=0)


def _pack_late(b, dtype):
    return jnp.concatenate([b[n].astype(dtype) for n in ("w_mla_up", "w_swa_up", "w_out", "w_ple_gate", "w_mlp_up", "w_mlp_down")],
                           axis=0)


def _full_weights(gathered, which):
    out = {}
    for n, r, c in BIG:
        buf, row0, lane0 = PACK_AT[n]
        if buf != which:
            continue
        blk = gathered[:, row0:row0 + r, lane0:lane0 + c]
        if n in ("w_mlp_up", "w_mlp_down"):
            assert (r, c) == (D, D) and row0 % D == 0
            out[n] = (gathered, row0 // D)
        elif n == "w_in":
            out["w_in_p"] = _w_in_internal([blk[j] for j in range(N_CHIPS)])
        elif n in COL_SHARDED:
            out[n] = jnp.swapaxes(blk, 0, 1).reshape(r, N_CHIPS * c)
        else:
            out[n] = blk.reshape(N_CHIPS * r, c)
    return out


def _split_full_grads(grads, pack, dtype):
    shard = {n: (r, c) for n, r, c in BIG}
    chunks = []
    for j in range(N_CHIPS):
        blocks = {}
        for n, g in grads.items():
            if n == "w_in_p":
                blocks["w_in"] = _w_in_grad_shard(g, j)
                continue
            r, c = shard[n]
            blocks[n] = g[:, j * c:(j + 1) * c] if n in COL_SHARDED else g[j * r:(j + 1) * r]
        chunks.append(pack(blocks, dtype))
    return jnp.stack(chunks)


W_IN_SHARD = 936
W_IN_SEGMENTS = ((0, 256, (3072,)), (256, 384, (3840,)), (384, 416, (4032,)), (416, 1440, (0,)), (1440, 1504, (3328, 3392)),
                 (1504, 1568, (3456, 3520)), (1568, 1632, (3584, 3648)), (1632, 1696, (3712, 3776)), (1696, 3744, (1024,)))


def _w_in_internal(shards):
    def cols(a, b):
        out = []
        for j, s in enumerate(shards):
            lo, hi = max(a, W_IN_SHARD * j), min(b, W_IN_SHARD * (j + 1))
            if lo < hi:
                out.append(s[:, lo - W_IN_SHARD * j:hi - W_IN_SHARD * j])
        return out

    pieces = {}
    for a, b, places in W_IN_SEGMENTS:
        for at in places:
            pieces[at] = cols(a, b)
    zeros = lambda n: [jnp.zeros((D, n), shards[0].dtype)]
    pieces[3968] = zeros(64)
    pieces[4064] = zeros(32)
    return jnp.concatenate([piece for at in sorted(pieces) for piece in pieces[at]], axis=1)


def _w_in_grad_shard(g, j):
    def internal(a, b):
        out = []
        while a < b:
            end = min(b, (a // D + 1) * D)
            out.append(g[a // D][:, a % D:a % D + end - a])
            a = end
        return out

    out = []
    for a, b, places in W_IN_SEGMENTS:
        lo, hi = max(a, W_IN_SHARD * j), min(b, W_IN_SHARD * (j + 1))
        if lo < hi:
            parts = [internal(at + lo - a, at + hi - a) for at in places]
            if len(parts) == 1:
                out += parts[0]
            else:
                assert len(parts[0]) == len(parts[1]) == 1
                out.append(parts[0][0] + parts[1][0])
    return jnp.concatenate(out, axis=1)


def _local_step(x, p, tgt, w, small, late_weights, late_grads_out):
    T = x.shape[0]
    tm = 256
    tb = 256
    w_in_p = w["w_in_p"]
    wqb = jnp.pad(w["w_q_b"].reshape(Q_LORA, MLA_HEADS, 96), ((0, 0), (0, 0), (0, 32))).reshape(Q_LORA, 2048)
    wkv = w["w_kv_b"].reshape(KV_LORA, MLA_HEADS, 128)
    wkn = jnp.pad(wkv[:, :, :64], ((0, 0), (0, 0), (0, 64))).reshape(KV_LORA, 2048)
    wv = wkv[:, :, 64:].reshape(KV_LORA, 1024)
    tab_m = _rope_tables(T, "mla")
    tab_s = _rope_tables(T, "swa")
    g1, gq, gkv, sinks = small["g_mix_pre"], small["g_q_a"], small["g_kv_a"], small["sinks"]
    g2, g3, g4, g5 = small["g_mix_post"], small["g_mlp_pre"], small["g_mlp_post"], small["g_ple"]
    sink_vec = sinks.reshape(SWA_HEADS)

    z, h1 = _fwd_in(x, g1, w_in_p, tm)
    qn, kvn, km, vm, qt, kt, vt, qs, ks, vs = _fwd_qkv(z, gq, gkv, wqb, wkn, wv, tab_m, tab_s, tb)
    om, lse_m = _mla_fwd(qt, km, vt, tb)
    os_, lse_s = _swa_fwd(sink_vec, qs, ks, vs)
    w = {**w, **late_weights((om, os_))}
    y, yo, au, bu, x1 = _fwd_mix(om, os_, z, x, w["w_mla_up"], w["w_swa_up"], w["w_out"], g2, tm)
    h2, u = _fwd_mlp_up(x1, g3, w["w_mlp_up"], tm)
    d, x2 = _fwd_mlp_down(u, w["w_mlp_down"], x1, g4, tm)
    loss, dx2, dgt, de0, dg5 = _ple_fwd_bwd(p, x2, tgt, w["w_ple"], g5, w["w_ple_gate"], tm)

    dd, da, dg4 = _bwd_mlp_down(dx2, d, g4, w["w_mlp_down"], u, tm)
    dx1, dg3 = _bwd_mlp_up(da, w["w_mlp_up"], x1, g3, dx2, tm)
    dyo, dg2, dau, dbu, dga, dgb, dos, delta_m, dom_t = _bwd_mix(dx1, yo, g2, w["w_out"], z, au, bu, w["w_mla_up"],
                                                                w["w_swa_up"], om, tb)
    gpk_late = lax.empty((N_CHIPS, PACK_ROWS["late"], D), BF16)
    for weight, a_, g_ in (("w_mla_up", om, dau), ("w_swa_up", os_, dbu), ("w_out", y, dyo), ("w_ple_gate", x2, dgt),
                           ("w_mlp_up", h2, da), ("w_mlp_down", u, dd)):
        gpk_late = _wgrad(a_, g_, "wgrad_" + weight[2:], into=(gpk_late, weight))
    token = late_grads_out(gpk_late)
    delta_m = delta_m + token[0, 0]
    dqm, dkm, dvm = _mla_bwd(qt, km, kt, vm, dom_t, lse_m, delta_m, tb)
    dqs, dkc, dkp, dvc, dvp, dsink = _swa_bwd(sink_vec, qs, ks, vs, dos, os_, lse_s)
    dqb, dknb, dvb, dsq, drest, dgq, dgkv = _bwd_qkv(dqm, dkm, dvm, dqs, dkc, dkp, dvc, dvp, z, gq, gkv, wqb, wkn, wv,
                                                      tab_m, tab_s)
    gx, dg1 = _bwd_in(dsq, dga, dgb, drest, w_in_p, x, g1, dx1, tm)

    g_in_p = [_wgrad(h1, dsq, "wgrad_in_sq"), _wgrad(h1, dga, "wgrad_in_ga"), _wgrad(h1, dgb, "wgrad_in_gb"),
              _wgrad(h1, drest, "wgrad_in_rest")]
    g_qb_p = _wgrad_t(dqb, qn, "wgrad_q_b").T
    g_kn_p = _wgrad_t(dknb, kvn, "wgrad_kv_b_nope").T
    g_v_p = _wgrad_t(dvb, kvn, "wgrad_kv_b_v").T
    grads = {
        "w_in_p": g_in_p,
        "w_q_b": g_qb_p.reshape(Q_LORA, MLA_HEADS, 128)[:, :, :96].reshape(Q_LORA, 1536),
        "w_kv_b": jnp.concatenate([g_kn_p.reshape(KV_LORA, MLA_HEADS, 128)[:, :, :64], g_v_p.reshape(KV_LORA, MLA_HEADS, 64)],
                                  axis=2).reshape(KV_LORA, 2048),
        "w_ple": _wgrad(p, de0, "wgrad_ple"),
    }
    small_grads = {"g_mix_pre": dg1, "g_q_a": dgq.reshape(1, Q_LORA), "g_kv_a": dgkv.reshape(1, KV_LORA), "sinks": dsink[0:1, 0:SWA_HEADS], "g_mix_post": dg2,
                   "g_mlp_pre": dg3, "g_mlp_post": dg4, "g_ple": dg5}
    return loss, gx, grads, small_grads


def _pack_small(vals, fill, scalar=None):
    wide = [vals[n] for n, k in SMALL if k == D]
    narrow = [vals[n] for n, k in SMALL if k != D]
    used = sum(k for _, k in SMALL if k != D)
    last = jnp.concatenate(narrow + [jnp.full((1, D - used), fill, F32)], axis=1)
    rest = jnp.full((2, D), fill, F32)
    if scalar is not None:
        rest = jnp.concatenate([jnp.concatenate([scalar, rest[0:1, 1:]], axis=1), rest[1:2]], axis=0)
    return jnp.concatenate(wide + [last, rest], axis=0)


def _unpack_small(pk):
    out, row, off = {}, 0, 0
    for n, k in SMALL:
        if k == D:
            out[n] = pk[row:row + 1]
            row += 1
    for n, k in SMALL:
        if k != D:
            out[n] = pk[5:6, off:off + k]
            off += k
    return out


def kernel(x, p, g_mix_pre, w_in, g_q_a, w_q_b, g_kv_a, w_kv_b, sinks, w_mla_up, w_swa_up, w_out, g_mix_post, g_mlp_pre, w_mlp_up, w_mlp_down, g_mlp_post, w_ple, g_ple, w_ple_gate, loss_target, m_g_mix_pre, m_w_in, m_g_q_a, m_w_q_b, m_g_kv_a, m_w_kv_b, m_sinks, m_w_mla_up, m_w_swa_up, m_w_out, m_g_mix_post, m_g_mlp_pre, m_w_mlp_up, m_w_mlp_down, m_g_mlp_post, m_w_ple, m_g_ple, m_w_ple_gate, v_g_mix_pre, v_w_in, v_g_q_a, v_w_q_b, v_g_kv_a, v_w_kv_b, v_sinks, v_w_mla_up, v_w_swa_up, v_w_out, v_g_mix_post, v_g_mlp_pre, v_w_mlp_up, v_w_mlp_down, v_g_mlp_post, v_w_ple, v_g_ple, v_w_ple_gate):
    given = dict(locals())
    big_w = {n: given[n][0] for n, _, _ in BIG}
    small_w = {n: given[n] for n, _ in SMALL}
    small_m = {n: given["m_" + n] for n, _ in SMALL}
    small_v = {n: given["v_" + n] for n, _ in SMALL}

    core = lax.axis_index("c")
    chip = 2 * lax.axis_index("x") + lax.axis_index("y")
    core_i = core.astype(jnp.int32).reshape(1)
    dev_i = jnp.stack([2 * chip + core, chip, core]).astype(jnp.int32)

    own_early = _pack_early(big_w, BF16)
    own_late = _pack_late(big_w, BF16)
    got_early = _all_gather(own_early)
    late_flight = _gather_late_start(own_late, got_early)
    weights = _full_weights(got_early, "early")
    step_small = {**small_w, "g_mix_pre": small_w["g_mix_pre"] + late_flight[4][0, 0]}

    def late_weights(after):
        return _full_weights(_gather_late_wait(*late_flight[:4], after), "late")

    flight = {}

    def late_grads_out(gpk_late):
        flight["late"] = _reduce_late_start(gpk_late, dev_i)
        return flight["late"][4]

    loss_blk, gx, grads, small_grads = _local_step(x[0], p[0, 0], loss_target[0], weights, step_small, late_weights,
                                                   late_grads_out)

    gpk = _split_full_grads(grads, _pack_early, BF16)
    got = _rs_sibling(gpk)
    part = _rs_add_sibling(core_i, gpk, got)
    small_own = _pack_small(small_grads, 0.0, loss_blk[0:1, 0:1])
    early_flight = _rs_chips_start(part, small_own, dev_i)

    out_g, out_d, out_m, out_v = {}, {}, {}, {}
    gpk_late, parts_late = _reduce_late_wait(*flight["late"][:4], early_flight[6])
    joined_late = _rs_join(_reduce_late_add(dev_i, gpk_late, parts_late), "rs_join_late")
    for n, _, _ in BIG:
        if PACK_AT[n][0] == "late":
            out_g[n], out_d[n], out_m[n], out_v[n] = _adamw(given[n], joined_late, given["m_" + n], given["v_" + n], n)

    part, parts, small_parts = _rs_chips_wait(*early_flight[:6], [out_d[n] for n in out_d])
    joined_early = _rs_join(_rs_add_chips(dev_i[1:3], part, parts), "rs_join_early")
    for n, _, _ in BIG:
        if PACK_AT[n][0] == "early":
            t = given[n].shape[2] % LANES != 0
            wmv = [jnp.swapaxes(a, 1, 2) if t else a for a in (given[n], given["m_" + n], given["v_" + n])]
            res = _adamw(wmv[0], joined_early, wmv[1], wmv[2], n, transposed=t)
            out_g[n], out_d[n], out_m[n], out_v[n] = [jnp.swapaxes(a, 1, 2) if t else a for a in res]

    mine = (lax.broadcasted_iota(jnp.int32, (N_DEV, 1, 1), 0) == dev_i[0])
    g_small_pk, d_small_pk, m_small_pk, v_small_pk = _adamw_small(
        _pack_small(small_w, 0.0), jnp.where(mine, small_own[None], small_parts), _pack_small(small_m, 0.0),
        _pack_small(small_v, 1.0))
    loss = g_small_pk[6, 0]
    for out, pk in ((out_g, g_small_pk), (out_d, d_small_pk), (out_m, m_small_pk), (out_v, v_small_pk)):
        out.update(_unpack_small(pk))
    order = ["g_mix_pre", "w_in", "g_q_a", "w_q_b", "g_kv_a", "w_kv_b", "sinks", "w_mla_up", "w_swa_up", "w_out", "g_mix_post",
             "g_mlp_pre", "w_mlp_up", "w_mlp_down", "g_mlp_post", "w_ple", "g_ple", "w_ple_gate"]
    return (loss, gx[None], *[out_g[n] for n in order], *[out_d[n] for n in order], *[out_m[n] for n in order],
            *[out_v[n] for n in order])
```

```python
import math

import jax
import jax.numpy as jnp
import numpy as np
from jax import lax
from jax.experimental import pallas as pl
from jax.experimental.pallas import tpu as pltpu

F32 = jnp.float32
BF16 = jnp.bfloat16
SDS = jax.ShapeDtypeStruct

D = 1024
D_FF = 4096
PLE = 256
Q_LORA = 256
KV_LORA = 128
MLA_HEADS = 16
MLA_NOPE = 64
MLA_ROPE = 32
SWA_HEADS = 16
SWA_HD = 64
WINDOW = 128
ROPE_THETA = 10000.0
EPS = 1e-6
NEG = -1e30
NZ = 4096
MLA_SCALE = (MLA_NOPE + MLA_ROPE) ** -0.5
LOG2_E = math.log2(math.e)
MLA_LOG2_SCALE = MLA_SCALE * LOG2_E
SWA_SCALE = SWA_HD ** -0.5

ADAM_LR = 0.001
ADAM_B1 = 0.9
ADAM_B2 = 0.999
ADAM_EPS = 1e-08
ADAM_WD = 0.01
ADAM_STEP = 10

LANES = 128
ATT_COLS = 128
VT_ROWS = 80
N_CHIPS = 4
N_DEV = 8
MESH = pl.DeviceIdType.MESH

NT = (((1,), (1,)), ((), ()))
TN = (((0,), (0,)), ((), ()))

BIG = (("w_in", 1024, 936), ("w_q_b", 256, 384), ("w_kv_b", 128, 512), ("w_mla_up", 256, 1024),
       ("w_swa_up", 256, 1024), ("w_out", 256, 1024), ("w_mlp_up", 1024, 1024), ("w_mlp_down", 1024, 1024),
       ("w_ple", 256, 256), ("w_ple_gate", 256, 1024))
COL_SHARDED = ("w_in", "w_q_b", "w_kv_b", "w_mlp_up", "w_ple")
PACK_AT = {"w_in": ("early", 0, 0), "w_q_b": ("early", 1024, 0), "w_ple": ("early", 1024, 384), "w_kv_b": ("early", 1280, 0),
           "w_mla_up": ("late", 0, 0), "w_swa_up": ("late", 256, 0), "w_out": ("late", 512, 0), "w_ple_gate": ("late", 768, 0),
           "w_mlp_up": ("late", 1024, 0), "w_mlp_down": ("late", 2048, 0)}
PACK_ROWS = {"early": 1408, "late": 3072}
SMALL = (("g_mix_pre", 1024), ("g_q_a", 256), ("g_kv_a", 128), ("sinks", 16), ("g_mix_post", 1024),
         ("g_mlp_pre", 1024), ("g_mlp_post", 1024), ("g_ple", 1024))


def _dot(a, b):
    return jnp.dot(a, b, preferred_element_type=F32)


def _dot_nt(a, b):
    return lax.dot_general(a, b, NT, preferred_element_type=F32)


def _dot_tn(a, b):
    return lax.dot_general(a, b, TN, preferred_element_type=F32)


def _pcall(body, *, name, out_shape, grid=(), in_specs=None, out_specs=None, scratch=(), sem=None, vmem_mb=48, aliases=None):
    params = dict(vmem_limit_bytes=vmem_mb << 20)
    if sem is not None:
        params["dimension_semantics"] = sem
    return pl.pallas_call(body, name=name, grid=grid, in_specs=in_specs, out_specs=out_specs, out_shape=out_shape,
                          scratch_shapes=list(scratch), input_output_aliases=aliases or {},
                          compiler_params=pltpu.CompilerParams(**params))


def _rows(tm, n, col=0):
    return pl.BlockSpec((tm, n), lambda i: (i, col))


def _full(shape):
    return pl.BlockSpec(shape, lambda i: (0,) * len(shape))


def _chunks(k):
    assert D_FF == N_CHIPS * D
    return pl.BlockSpec((N_CHIPS, D, D), lambda i: (0, k, 0))


def _rms(x, g):
    r = lax.rsqrt(jnp.mean(x * x, axis=-1, keepdims=True) + EPS)
    return x * r * g


def _rms_bwd(dy, x, g):
    r = lax.rsqrt(jnp.mean(x * x, axis=-1, keepdims=True) + EPS)
    xn = x * r
    dn = dy * g
    dx = r * (dn - xn * jnp.mean(dn * xn, axis=-1, keepdims=True))
    return dx, jnp.sum(dy * xn, axis=0, keepdims=True)


def _sigmoid(x):
    return 1.0 / (1.0 + jnp.exp(-x))


def _rope(x, c, a, b, half):
    return x * c + pltpu.roll(x, LANES - half, 1) * a + pltpu.roll(x, half, 1) * b


def _rope_tables(T, kind):
    lane = np.arange(LANES)
    if kind == "mla":
        half = MLA_ROPE // 2
        rel = lane - MLA_NOPE
        on = (rel >= 0) & (rel < MLA_ROPE)
        d = MLA_ROPE
    else:
        half = SWA_HD // 2
        rel = lane % SWA_HD
        on = np.ones((LANES,), bool)
        d = SWA_HD
    first = on & (rel < half)
    second = on & (rel >= half)
    f = np.where(first, rel, rel - half).astype(np.float32)
    inv = np.exp(np.float32(-math.log(ROPE_THETA)) * f * np.float32(2.0 / d)).astype(np.float32)
    ang = (np.arange(T, dtype=np.float32)[:, None] * inv[None, :]).astype(np.float64)
    cos, sin = np.cos(ang).astype(np.float32), np.sin(ang).astype(np.float32)
    c = np.where(on[None], cos, np.float32(1.0))
    a = np.where(first[None], -sin, np.float32(0.0))
    b = np.where(second[None], sin, np.float32(0.0))
    return c, a, b


def _fwd_in(x, g1, w_in_p, tm):
    T = x.shape[0]

    def body(x_ref, g_ref, w_ref, z_ref, h_ref):
        h = _rms(x_ref[...], g_ref[...]).astype(BF16)
        h_ref[...] = h
        z_ref[...] = _dot(h, w_ref[...])

    return _pcall(body, name="fwd_in", grid=(T // tm,),
                  in_specs=[_rows(tm, D), _full((1, D)), _full((D, NZ))],
                  out_specs=[_rows(tm, NZ), _rows(tm, D)],
                  out_shape=[SDS((T, NZ), F32), SDS((T, D), BF16)], sem=("parallel",))(x, g1, w_in_p)


def _fwd_qkv(z, gq, gkv, wqb, wkn, wv, tab_m, tab_s, tm):
    T = z.shape[0]
    wqb_t, wkn_t, wv_t = wqb.T, wkn.T, wv.T
    tab_mt = [t.T for t in tab_m]

    def body(qa_ref, sq_ref, skd_ref, svd_ref, kva_ref, kr_ref, gq_ref, gkv_ref, wkn_ref, wv_ref, wqbt_ref, wknt_ref, wvt_ref,
             cm_ref, am_ref, bm_ref, cmt_ref, amt_ref, bmt_ref, cs_ref, as_ref, bs_ref,
             qn_ref, kvn_ref, km_ref, vm_ref, qt_ref, kt_ref, vt_ref, qs_ref, ks_ref, vs_ref):
        qn = _rms(qa_ref[...], gq_ref[...])
        qn_ref[...] = qn.astype(BF16)
        kvn = _rms(kva_ref[...], gkv_ref[...])
        kvn_b = kvn.astype(BF16)
        kvn_ref[...] = kvn_b
        qn_t = qn.T.astype(BF16)
        kvn_t = kvn.T.astype(BF16)
        cm, am, bm = cm_ref[...], am_ref[...], bm_ref[...]
        cmt, amt, bmt = cmt_ref[...], amt_ref[...], bmt_ref[...]
        cs, as_, bs = cs_ref[...], as_ref[...], bs_ref[...]
        k_rope = _rope(kr_ref[...], cm, am, bm, MLA_ROPE // 2)
        k_rope_t = k_rope.T
        half = MLA_ROPE // 2
        vm_ref[...] = _dot(kvn_b, wv_ref[...]).astype(BF16)
        km_all = _dot(kvn_b, wkn_ref[...])
        v_t = _dot(wvt_ref[...], kvn_t)
        q_t = _dot(wqbt_ref[...], qn_t)
        k_t = _dot(wknt_ref[...], kvn_t)
        ones_row = jnp.where(lax.broadcasted_iota(jnp.int32, (64, tm), 0) == 0, 1.0, 0.0)
        for h in range(MLA_HEADS):
            sl = slice(LANES * h, LANES * (h + 1))
            vt_ref[0, sl, :] = jnp.concatenate([v_t[64 * h:64 * (h + 1)], ones_row], axis=0).astype(BF16)
            qh = q_t[sl]
            qt_ref[0, sl, :] = (qh * cmt + pltpu.roll(qh, LANES - half, 0) * amt + pltpu.roll(qh, half, 0) * bmt).astype(BF16)
            km_ref[:, sl] = (km_all[:, sl] + k_rope).astype(BF16)
            kt_ref[0, sl, :] = (k_t[sl] + k_rope_t).astype(BF16)
        for j in range(D // LANES):
            sl = slice(LANES * j, LANES * (j + 1))
            qs_ref[:, sl] = _rope(sq_ref[:, sl], cs, as_, bs, SWA_HD // 2).astype(BF16)
        for j in range(2):
            sl = slice(LANES * j, LANES * (j + 1))
            ks_ref[:, sl] = _rope(skd_ref[:, sl], cs, as_, bs, SWA_HD // 2).astype(BF16)
        vs_ref[...] = svd_ref[...].astype(BF16)

    tab = [_rows(tm, LANES)] * 3
    tab_t = [pl.BlockSpec((LANES, tm), lambda i: (0, i))] * 3
    return _pcall(body, name="fwd_qkv", grid=(T // tm,),
                  in_specs=[_rows(tm, 256, 12), _rows(tm, 1024, 0), _rows(tm, 256, 13), _rows(tm, 256, 14),
                            _rows(tm, 128, 30), _rows(tm, 128, 31), _full((1, Q_LORA)), _full((1, KV_LORA)),
                            _full((KV_LORA, 2048)), _full((KV_LORA, 1024)), _full((2048, Q_LORA)), _full((2048, KV_LORA)),
                            _full((1024, KV_LORA))] + tab + tab_t + tab,
                  out_specs=[_rows(tm, Q_LORA), _rows(tm, KV_LORA), _rows(tm, 2048), _rows(tm, 1024),
                             pl.BlockSpec((1, 2048, tm), lambda i: (i, 0, 0)), pl.BlockSpec((1, 2048, tm), lambda i: (i, 0, 0)),
                             pl.BlockSpec((1, 2048, tm), lambda i: (i, 0, 0)),
                             _rows(tm, 1024), _rows(tm, 256), _rows(tm, 256)],
                  out_shape=[SDS((T, Q_LORA), BF16), SDS((T, KV_LORA), BF16), SDS((T, 2048), BF16),
                             SDS((T, 1024), BF16), SDS((T // tm, 2048, tm), BF16), SDS((T // tm, 2048, tm), BF16),
                             SDS((T // tm, 2048, tm), BF16),
                             SDS((T, 1024), BF16), SDS((T, 256), BF16), SDS((T, 256), BF16)],
                  sem=("parallel",))(z, z, z, z, z, z, gq, gkv, wkn, wv, wqb_t, wkn_t, wv_t, *tab_m, *tab_mt, *tab_s)


def _mla_fwd(qt, km, vt, tb):
    T = km.shape[0]
    nb = T // tb
    cc = ATT_COLS

    per = 2 if nb % 2 == 0 else 1

    def body(q_ref, k_ref, vt_ref, o_ref, l_ref, s_ref, p_ref, al_ref, m_ref, acc_ref):
        for blk in range(per):
            one_block(per * pl.program_id(1) + blk, blk, q_ref, k_ref, vt_ref, o_ref, l_ref, s_ref, p_ref, al_ref, m_ref, acc_ref)

    def one_block(i, blk, q_ref, k_ref, vt_ref, o_ref, l_ref, s_ref, p_ref, al_ref, m_ref, acc_ref):
        m_ref[...] = jnp.full(m_ref.shape, NEG, F32)
        acc_ref[...] = jnp.zeros_like(acc_ref)
        p_ref[1] = jnp.zeros(p_ref.shape[1:], BF16)
        al_ref[1] = jnp.ones(al_ref.shape[1:], F32)
        key = lax.broadcasted_iota(jnp.int32, (tb, cc), 0)
        qry = lax.broadcasted_iota(jnp.int32, (tb, cc), 1)

        def scores(j, slot):
            off = pl.multiple_of(j * tb, tb)
            for hh in range(2):
                sl = slice(LANES * hh, LANES * (hh + 1))
                s_ref[slot, hh] = _dot(k_ref[pl.ds(off, tb), sl], q_ref[blk, sl, :])

        def softmax(slot, diagonal):
            chains = [(hh, slice(cc * c, cc * (c + 1)), c) for hh in range(2) for c in range(tb // cc)]

            def scaled(hh, cols, c):
                t = s_ref[slot, hh, :, cols] * MLA_LOG2_SCALE
                return jnp.where(key <= qry + cc * c, t, NEG) if diagonal else t

            tops = []
            for hh, cols, c in chains:
                if diagonal:
                    top = jnp.max(scaled(hh, cols, c), axis=0, keepdims=True)
                else:
                    top = jnp.max(s_ref[slot, hh, :, cols], axis=0, keepdims=True) * MLA_LOG2_SCALE
                m_old = m_ref[hh, :, cols]
                mn = jnp.maximum(m_old, top)
                m_ref[hh, :, cols] = mn
                al_ref[slot, hh, :, cols] = jnp.exp2(m_old - mn)
                tops.append(mn)
            for (hh, cols, c), mn in zip(chains, tops):
                p_ref[slot, hh, :, cols] = jnp.exp2(scaled(hh, cols, c) - mn).astype(BF16)

        def accumulate(j, slot):
            for hh in range(2):
                acc_ref[hh] = al_ref[slot, hh] * acc_ref[hh] + _dot(vt_ref[j, LANES * hh:LANES * hh + VT_ROWS, :], p_ref[slot, hh])

        def step(t, carry):
            scores(2 * t + 1, 1)
            accumulate(jnp.maximum(2 * t - 1, 0), 1)
            softmax(0, False)
            scores(2 * t + 2, 0)
            accumulate(2 * t, 0)
            softmax(1, False)
            return carry

        scores(0, 0)
        lax.fori_loop(0, i // 2, step, 0)

        @pl.when(i % 2 == 1)
        def _():
            scores(i, 1)
            accumulate(jnp.maximum(i - 2, 0), 1)
            softmax(0, False)
            accumulate(i - 1, 0)
            softmax(1, True)
            accumulate(i, 1)

        @pl.when(i % 2 == 0)
        def _():
            accumulate(jnp.maximum(i - 1, 0), 1)
            softmax(0, True)
            accumulate(i, 0)
        den = [acc_ref[hh, 64:65, :] for hh in range(2)]
        o_ref[tb * blk:tb * (blk + 1), :] = jnp.concatenate([acc_ref[hh, 0:64, :] / den[hh] for hh in range(2)], axis=0).T
        sub = lax.broadcasted_iota(jnp.int32, (8, tb), 0)
        lse = [m_ref[hh] + jnp.log(den[hh]) * LOG2_E for hh in range(2)]
        l_ref[0, blk] = jnp.where(sub == 0, lse[0], jnp.where(sub == 1, lse[1], 0.0))

    return _pcall(body, name="mla_fwd", grid=(MLA_HEADS // 2, nb // per),
                  in_specs=[pl.BlockSpec((per, 256, tb), lambda p, i: (i, p, 0)), pl.BlockSpec((T, 256), lambda p, i: (0, p)),
                            pl.BlockSpec((nb, 2 * LANES, tb), lambda p, i: (0, p, 0))],
                  out_specs=[pl.BlockSpec((per * tb, LANES), lambda p, i: (i, p)),
                             pl.BlockSpec((1, per, 8, tb), lambda p, i: (p, i, 0, 0))],
                  out_shape=[SDS((T, D), F32), SDS((MLA_HEADS // 2, nb, 8, tb), F32)],
                  scratch=[pltpu.VMEM((2, 2, tb, tb), F32), pltpu.VMEM((2, 2, tb, tb), BF16), pltpu.VMEM((2, 2, 1, tb), F32),
                           pltpu.VMEM((2, 1, tb), F32), pltpu.VMEM((2, VT_ROWS, tb), F32)],
                  sem=("parallel", "arbitrary"))(qt, km, vt)


def _swa_mask(n):
    row = lax.broadcasted_iota(jnp.int32, (WINDOW, 2 * WINDOW), 0)
    col = lax.broadcasted_iota(jnp.int32, (WINDOW, 2 * WINDOW), 1)
    rel = row - col + WINDOW
    return (rel >= 0) & (rel < WINDOW) & ((col >= WINDOW) | (n > 0))


def _swa_specs(T):
    nb = T // WINDOW
    cur = lambda w: pl.BlockSpec((WINDOW, w), lambda n: (n, 0))
    prev = lambda w: pl.BlockSpec((WINDOW, w), lambda n: (jnp.maximum(n - 1, 0), 0))
    return nb, cur, prev


def _swa_fwd(sinks, qs, ks, vs):
    T = qs.shape[0]
    nb, cur, prev = _swa_specs(T)

    def body(sink_ref, q_ref, kc_ref, kp_ref, vc_ref, vp_ref, o_ref, l_ref, kb_ref, vb_ref, s_ref, p_ref):
        n = pl.program_id(0)
        mask = _swa_mask(n)
        lo = lax.broadcasted_iota(jnp.int32, (WINDOW, LANES), 1) < 64
        hi = jnp.logical_not(lo)
        for g in range(2):
            gs = slice(LANES * g, LANES * (g + 1))
            kb_ref[g] = jnp.concatenate([kp_ref[:, gs], kc_ref[:, gs]], axis=0)
            vb_ref[g] = jnp.concatenate([vp_ref[:, gs], vc_ref[:, gs]], axis=0)
        for h in range(SWA_HEADS):
            qp = q_ref[:, LANES * (h // 2):LANES * (h // 2 + 1)]
            qh = jnp.where(lo if h % 2 == 0 else hi, qp, jnp.zeros_like(qp))
            s_ref[h] = _dot_nt(qh, kb_ref[h // 8])
        for j in range(SWA_HEADS // 2):
            sl = slice(LANES * j, LANES * (j + 1))
            lses = []
            for h in (2 * j, 2 * j + 1):
                s = jnp.where(mask, s_ref[h] * SWA_SCALE, NEG)
                sk = sink_ref[h]
                m = jnp.maximum(jnp.max(s, axis=1, keepdims=True), sk)
                e = jnp.exp(s - m)
                den = jnp.sum(e, axis=1, keepdims=True) + jnp.exp(sk - m)
                p_ref[h] = (e / den).astype(BF16)
                lses.append(jnp.broadcast_to(m + jnp.log(den), (WINDOW, LANES)))
            l_ref[:, sl] = jnp.where(lo, lses[0], lses[1])
        for j in range(SWA_HEADS // 2):
            vb = vb_ref[j // 4]
            o_ref[:, LANES * j:LANES * (j + 1)] = jnp.where(lo, _dot(p_ref[2 * j], vb), _dot(p_ref[2 * j + 1], vb))

    return _pcall(body, name="swa_fwd", grid=(nb,),
                  in_specs=[pl.BlockSpec(memory_space=pltpu.SMEM), cur(D), cur(256), prev(256), cur(256), prev(256)],
                  out_specs=[cur(D), cur(D)], out_shape=[SDS((T, D), F32)] * 2,
                  scratch=[pltpu.VMEM((2, 2 * WINDOW, LANES), BF16), pltpu.VMEM((2, 2 * WINDOW, LANES), BF16),
                           pltpu.VMEM((SWA_HEADS, WINDOW, 2 * WINDOW), F32), pltpu.VMEM((SWA_HEADS, WINDOW, 2 * WINDOW), BF16)],
                  sem=("parallel",))(sinks, qs, ks, ks, vs, vs)


def _fwd_mix(om, os_, z, x, wmu, wsu, wo, g2, tm):
    T = x.shape[0]

    def body(om_ref, os_ref, ga_ref, gb_ref, x_ref, wmu_ref, wsu_ref, wo_ref, g2_ref,
             y_ref, yo_ref, au_ref, bu_ref, x1_ref):
        au = _dot(om_ref[...].astype(BF16), wmu_ref[...])
        bu = _dot(os_ref[...].astype(BF16), wsu_ref[...])
        au_ref[...] = au
        bu_ref[...] = bu
        y = (_sigmoid(ga_ref[...]) * au + _sigmoid(gb_ref[...]) * bu).astype(BF16)
        y_ref[...] = y
        yo = _dot(y, wo_ref[...])
        yo_ref[...] = yo
        x1_ref[...] = x_ref[...] + _rms(yo, g2_ref[...])

    r = _rows(tm, D)
    w = _full((D, D))
    return _pcall(body, name="fwd_mix", grid=(T // tm,),
                  in_specs=[r, r, _rows(tm, D, 1), _rows(tm, D, 2), r, w, w, w, _full((1, D))],
                  out_specs=[r] * 5,
                  out_shape=[SDS((T, D), BF16), SDS((T, D), F32), SDS((T, D), F32), SDS((T, D), F32), SDS((T, D), F32)],
                  sem=("parallel",))(om, os_, z, z, x, wmu, wsu, wo, g2)


def _fwd_mlp_up(x1, g3, w1, tm):
    T = x1.shape[0]

    def body(x_ref, g_ref, w_ref, h_ref, u_ref):
        h = _rms(x_ref[...], g_ref[...]).astype(BF16)
        h_ref[...] = h
        for j in range(N_CHIPS):
            u_ref[:, j * D:(j + 1) * D] = jnp.square(jnp.maximum(_dot(h, w_ref[j]), 0.0)).astype(BF16)

    return _pcall(body, name="fwd_mlp_up", grid=(T // tm,),
                  in_specs=[_rows(tm, D), _full((1, D)), _chunks(w1[1])],
                  out_specs=[_rows(tm, D), _rows(tm, D_FF)],
                  out_shape=[SDS((T, D), BF16), SDS((T, D_FF), BF16)],
                  sem=("parallel",))(x1, g3, w1[0])


def _fwd_mlp_down(u, w2, x1, g4, tm):
    T = x1.shape[0]

    def body(u_ref, w_ref, x_ref, g_ref, d_ref, x2_ref):
        d = _dot(u_ref[...], w_ref[...].reshape(D_FF, D))
        d_ref[...] = d
        x2_ref[...] = x_ref[...] + _rms(d, g_ref[...])

    return _pcall(body, name="fwd_mlp_down", grid=(T // tm,),
                  in_specs=[_rows(tm, D_FF), _chunks(w2[1]), _rows(tm, D), _full((1, D))],
                  out_specs=[_rows(tm, D), _rows(tm, D)], out_shape=[SDS((T, D), F32)] * 2,
                  sem=("parallel",))(u, w2[0], x1, g4)


def _ple_fwd_bwd(p, x2, tgt, wple, g5, wpg, tm):
    T = x2.shape[0]

    def body(p_ref, x2_ref, t_ref, wple_ref, g5_ref, wpg_ref, loss_ref, dx2_ref, dgt_ref, de0_ref, dg5_ref):
        @pl.when(pl.program_id(0) == 0)
        def _():
            loss_ref[...] = jnp.zeros_like(loss_ref)
            dg5_ref[...] = jnp.zeros_like(dg5_ref)

        e0 = _dot(p_ref[...].astype(BF16), wple_ref[...])
        g5 = g5_ref[...]
        r = lax.rsqrt(jnp.mean(e0 * e0, axis=-1, keepdims=True) + EPS)
        en = e0 * r
        e = en * g5
        x2 = x2_ref[...]
        s = _sigmoid(_dot(x2.astype(BF16), wpg_ref[...]))
        diff = x2 + s * e - t_ref[...]
        sq = jnp.sum(jnp.sum(diff * diff, axis=1, keepdims=True), axis=0, keepdims=True)
        loss_ref[...] += jnp.broadcast_to(sq * (0.5 / D), loss_ref.shape)
        dx3 = diff * (1.0 / D)
        de = dx3 * s
        dgt = (dx3 * e * s * (1.0 - s)).astype(BF16)
        dgt_ref[...] = dgt
        dn = de * g5
        de0_ref[...] = (r * (dn - en * jnp.mean(dn * en, axis=-1, keepdims=True))).astype(BF16)
        dg5_ref[...] += jnp.sum(de * en, axis=0, keepdims=True)
        dx2_ref[...] = dx3 + _dot_nt(dgt, wpg_ref[...])

    r = _rows(tm, D)
    return _pcall(body, name="ple_fwd_bwd", grid=(T // tm,),
                  in_specs=[_rows(tm, PLE), r, r, _full((PLE, D)), _full((1, D)), _full((D, D))],
                  out_specs=[_full((8, LANES)), r, r, r, _full((1, D))],
                  out_shape=[SDS((8, LANES), F32), SDS((T, D), F32), SDS((T, D), BF16), SDS((T, D), BF16), SDS((1, D), F32)],
                  sem=("arbitrary",))(p, x2, tgt, wple, g5, wpg)


def _bwd_mlp_down(dx2, d, g4, w2, u, tm):
    T = dx2.shape[0]

    def body(dx_ref, d_ref, g_ref, w_ref, u_ref, dd_ref, da_ref, dg_ref):
        @pl.when(pl.program_id(0) == 0)
        def _():
            dg_ref[...] = jnp.zeros_like(dg_ref)

        dd, dg = _rms_bwd(dx_ref[...], d_ref[...], g_ref[...])
        dg_ref[...] += dg
        ddb = dd.astype(BF16)
        dd_ref[...] = ddb
        du = _dot_nt(ddb, w_ref[...].reshape(D_FF, D))
        da_ref[...] = (du * (2.0 * jnp.sqrt(u_ref[...].astype(F32)))).astype(BF16)

    return _pcall(body, name="bwd_mlp_down", grid=(T // tm,),
                  in_specs=[_rows(tm, D), _rows(tm, D), _full((1, D)), _chunks(w2[1]), _rows(tm, D_FF)],
                  out_specs=[_rows(tm, D), _rows(tm, D_FF), _full((1, D))],
                  out_shape=[SDS((T, D), BF16), SDS((T, D_FF), BF16), SDS((1, D), F32)],
                  sem=("arbitrary",))(dx2, d, g4, w2[0], u)


def _bwd_mlp_up(da, w1, x1, g3, dx2, tm):
    T = dx2.shape[0]

    def body(da_ref, w_ref, x_ref, g_ref, dx2_ref, dx1_ref, dg_ref):
        @pl.when(pl.program_id(0) == 0)
        def _():
            dg_ref[...] = jnp.zeros_like(dg_ref)

        dh = _dot_nt(da_ref[:, 0:D], w_ref[0])
        for j in range(1, N_CHIPS):
            dh += _dot_nt(da_ref[:, j * D:(j + 1) * D], w_ref[j])
        dx, dg = _rms_bwd(dh, x_ref[...], g_ref[...])
        dg_ref[...] += dg
        dx1_ref[...] = dx2_ref[...] + dx

    return _pcall(body, name="bwd_mlp_up", grid=(T // tm,),
                  in_specs=[_rows(tm, D_FF), _chunks(w1[1]), _rows(tm, D), _full((1, D)), _rows(tm, D)],
                  out_specs=[_rows(tm, D), _full((1, D))],
                  out_shape=[SDS((T, D), F32), SDS((1, D), F32)], sem=("arbitrary",))(da, w1[0], x1, g3, dx2)


def _bwd_mix(dx1, yo, g2, wo, z, au, bu, wmu, wsu, om, tm):
    T = dx1.shape[0]

    def body(dx_ref, yo_ref, g_ref, wo_ref, ga_ref, gb_ref, au_ref, bu_ref, wmu_ref, wsu_ref, om_ref,
             dyo_ref, dg_ref, dau_ref, dbu_ref, dga_ref, dgb_ref, dos_ref, dl_ref, dot_ref):
        @pl.when(pl.program_id(0) == 0)
        def _():
            dg_ref[...] = jnp.zeros_like(dg_ref)

        dyo, dg = _rms_bwd(dx_ref[...], yo_ref[...], g_ref[...])
        dg_ref[...] += dg
        dyob = dyo.astype(BF16)
        dyo_ref[...] = dyob
        dy = _dot_nt(dyob, wo_ref[...])
        sa = _sigmoid(ga_ref[...])
        sb = _sigmoid(gb_ref[...])
        dau = (dy * sa).astype(BF16)
        dbu = (dy * sb).astype(BF16)
        dau_ref[...] = dau
        dbu_ref[...] = dbu
        dga_ref[...] = (dy * au_ref[...] * sa * (1.0 - sa)).astype(BF16)
        dgb_ref[...] = (dy * bu_ref[...] * sb * (1.0 - sb)).astype(BF16)
        dom = _dot_nt(dau, wmu_ref[...])
        dos_ref[...] = _dot_nt(dbu, wsu_ref[...])
        prod = dom * om_ref[...]
        sub = lax.broadcasted_iota(jnp.int32, (8, tm), 0)
        for pr in range(MLA_HEADS // 2):
            sl = slice(LANES * pr, LANES * (pr + 1))
            pt = prod[:, sl].T
            d0 = jnp.sum(pt[0:64], axis=0, keepdims=True)
            d1 = jnp.sum(pt[64:128], axis=0, keepdims=True)
            dl_ref[pr, 0] = jnp.where(sub == 0, d0, jnp.where(sub == 1, d1, 0.0))
            dot_ref[0, sl, :] = dom[:, sl].T.astype(BF16)

    r = _rows(tm, D)
    w = _full((D, D))
    return _pcall(body, name="bwd_mix", grid=(T // tm,),
                  in_specs=[r, r, _full((1, D)), w, _rows(tm, D, 1), _rows(tm, D, 2), r, r, w, w, r],
                  out_specs=[r, _full((1, D)), r, r, r, r, r, pl.BlockSpec((MLA_HEADS // 2, 1, 8, tm), lambda i: (0, i, 0, 0)),
                             pl.BlockSpec((1, D, tm), lambda i: (i, 0, 0))],
                  out_shape=[SDS((T, D), BF16), SDS((1, D), F32), SDS((T, D), BF16), SDS((T, D), BF16), SDS((T, D), BF16),
                             SDS((T, D), BF16), SDS((T, D), F32), SDS((MLA_HEADS // 2, T // tm, 8, tm), F32),
                             SDS((T // tm, D, tm), BF16)],
                  sem=("arbitrary",))(dx1, yo, g2, wo, z, z, au, bu, wmu, wsu, om)


def _mla_bwd(qt, km, kt, vm, dot, lse, delta, tb):
    T = km.shape[0]
    nb = T // tb
    cc = ATT_COLS

    per = 2 if nb % 2 == 0 else 1

    def body(qt_ref, k_ref, kt_ref, v_ref, dot_ref, l_ref, dl_ref, dqt_ref, dkt_ref, dvt_ref,
             s_ref, dp_ref, p_ref, ds_ref, vh_ref):
        @pl.when(pl.program_id(1) == 0)
        def _():
            dqt_ref[...] = jnp.zeros_like(dqt_ref)

        dkt_ref[...] = jnp.zeros_like(dkt_ref)
        dvt_ref[...] = jnp.zeros_like(dvt_ref)
        for blk in range(per):
            one_block(per * pl.program_id(1) + blk, blk, qt_ref, k_ref, kt_ref, v_ref, dot_ref, l_ref, dl_ref, dqt_ref, dkt_ref,
                      dvt_ref, s_ref, dp_ref, p_ref, ds_ref, vh_ref)

    def one_block(j, blk, qt_ref, k_ref, kt_ref, v_ref, dot_ref, l_ref, dl_ref, dqt_ref, dkt_ref, dvt_ref,
                  s_ref, dp_ref, p_ref, ds_ref, vh_ref):
        lo = lax.broadcasted_iota(jnp.int32, (tb, LANES), 1) < 64
        key = lax.broadcasted_iota(jnp.int32, (tb, cc), 0)
        qry = lax.broadcasted_iota(jnp.int32, (tb, cc), 1)
        rows_j = slice(tb * blk, tb * (blk + 1))
        v = v_ref[rows_j, :]
        vh_ref[0] = jnp.where(lo, v, jnp.zeros_like(v))
        vh_ref[1] = jnp.where(lo, jnp.zeros_like(v), v)

        def scores(i, slot):
            for hh in range(2):
                sl = slice(LANES * hh, LANES * (hh + 1))
                s_ref[slot, hh] = _dot(k_ref[rows_j, sl], qt_ref[i, sl, :])
                dp_ref[slot, hh] = _dot(vh_ref[hh], dot_ref[i])

        def grads(i, slot, diagonal):
            lse_i = l_ref[0, i]
            delta_i = dl_ref[0, i]
            for hh in range(2):
                for c in range(tb // cc):
                    cols = slice(cc * c, cc * (c + 1))
                    p = jnp.exp2(s_ref[slot, hh, :, cols] * MLA_LOG2_SCALE - lse_i[hh:hh + 1, cols])
                    if diagonal:
                        p = jnp.where(key <= qry + cc * c, p, 0.0)
                    p_ref[hh, :, cols] = p.astype(BF16)
                    ds_ref[hh, :, cols] = (p * (dp_ref[slot, hh, :, cols] - delta_i[hh:hh + 1, cols]) * MLA_SCALE).astype(BF16)
            for hh in range(2):
                sl = slice(LANES * hh, LANES * (hh + 1))
                half = slice(64 * hh, 64 * (hh + 1))
                dvt_ref[blk, half, :] += _dot_nt(dot_ref[i, half, :], p_ref[hh])
                real = slice(LANES * hh, LANES * hh + MLA_NOPE + MLA_ROPE)
                dkt_ref[blk, real, :] += _dot_nt(qt_ref[i, real, :], ds_ref[hh])
                dqt_ref[i, real, :] += _dot(kt_ref[blk, real, :], ds_ref[hh])

        n_off = nb - 1 - j

        def step(u, carry):
            i0 = j + 1 + 2 * u
            scores(i0 + 1, 1)
            grads(i0, 0, False)
            scores(jnp.where(i0 + 2 < nb, i0 + 2, j), 0)
            grads(i0 + 1, 1, False)
            return carry

        scores(jnp.where(n_off > 0, j + 1, j), 0)
        lax.fori_loop(0, n_off // 2, step, 0)

        @pl.when(n_off % 2 == 1)
        def _():
            scores(j, 1)
            grads(nb - 1, 0, False)
            grads(j, 1, True)

        @pl.when(n_off % 2 == 0)
        def _():
            grads(j, 0, True)

    blk = lambda w: pl.BlockSpec((per * tb, w), lambda p, j: (j, p))
    stat = pl.BlockSpec((1, nb, 8, tb), lambda p, j: (p, 0, 0, 0))
    pair_t = lambda w: pl.BlockSpec((nb, w, tb), lambda p, j: (0, p, 0))
    blk_t = lambda w: pl.BlockSpec((per, w, tb), lambda p, j: (j, p, 0))
    return _pcall(body, name="mla_bwd", grid=(MLA_HEADS // 2, nb // per),
                  in_specs=[pair_t(256), blk(256), blk_t(256), blk(LANES), pair_t(LANES), stat, stat],
                  out_specs=[pair_t(256), blk_t(256), blk_t(LANES)],
                  out_shape=[SDS((nb, 2048, tb), F32), SDS((nb, 2048, tb), F32), SDS((nb, D, tb), F32)],
                  scratch=[pltpu.VMEM((2, 2, tb, tb), F32), pltpu.VMEM((2, 2, tb, tb), F32), pltpu.VMEM((2, tb, tb), BF16),
                           pltpu.VMEM((2, tb, tb), BF16), pltpu.VMEM((2, tb, LANES), BF16)],
                  sem=("parallel", "arbitrary"))(qt, km, kt, vm, dot, lse, delta)


def _swa_bwd(sinks, qs, ks, vs, do, o, lse):
    T = qs.shape[0]
    nb, cur, prev = _swa_specs(T)

    def body(sink_ref, q_ref, kc_ref, kp_ref, vc_ref, vp_ref, do_ref, o_ref, l_ref,
             dq_ref, dkc_ref, dkp_ref, dvc_ref, dvp_ref, dsink_ref, kb_ref, vb_ref, s_ref, dp_ref, p_ref, ds_ref):
        n = pl.program_id(0)

        @pl.when(n == 0)
        def _():
            dsink_ref[...] = jnp.zeros_like(dsink_ref)

        mask = _swa_mask(n)
        lo = lax.broadcasted_iota(jnp.int32, (WINDOW, LANES), 1) < 64
        hi = jnp.logical_not(lo)
        lane8 = lax.broadcasted_iota(jnp.int32, (8, LANES), 1)
        for g in range(2):
            gs = slice(LANES * g, LANES * (g + 1))
            kb_ref[g] = jnp.concatenate([kp_ref[:, gs], kc_ref[:, gs]], axis=0)
            vb_ref[g] = jnp.concatenate([vp_ref[:, gs], vc_ref[:, gs]], axis=0)

        def head(h):
            sl = slice(LANES * (h // 2), LANES * (h // 2 + 1))
            hm = lo if h % 2 == 0 else hi
            qp = q_ref[:, sl]
            return hm, sl, jnp.where(hm, qp, jnp.zeros_like(qp)), jnp.where(hm, do_ref[:, sl], 0.0).astype(BF16)

        for h in range(SWA_HEADS):
            _, _, qh, dom = head(h)
            s_ref[h] = _dot_nt(qh, kb_ref[h // 8])
            dp_ref[h] = _dot_nt(dom, vb_ref[h // 8])
        dsink = jnp.zeros((8, LANES), F32)
        for h in range(SWA_HEADS):
            hm, sl, _, _ = head(h)
            lse_h = jnp.max(jnp.where(hm, l_ref[:, sl], -jnp.inf), axis=1, keepdims=True)
            delta = jnp.sum(jnp.where(hm, do_ref[:, sl] * o_ref[:, sl], 0.0), axis=1, keepdims=True)
            p = jnp.exp(jnp.where(mask, s_ref[h] * SWA_SCALE, NEG) - lse_h)
            p_ref[h] = p.astype(BF16)
            ds_ref[h] = (p * (dp_ref[h] - delta) * SWA_SCALE).astype(BF16)
            d_sink = -jnp.sum(jnp.exp(sink_ref[h] - lse_h) * delta, axis=0, keepdims=True)
            dsink = dsink + jnp.where(lane8 == h, d_sink, 0.0)
        dsink_ref[...] += dsink
        for g in range(2):
            gs = slice(LANES * g, LANES * (g + 1))
            dkb = jnp.zeros((2 * WINDOW, LANES), F32)
            dvb = jnp.zeros((2 * WINDOW, LANES), F32)
            for j in range(4 * g, 4 * g + 4):
                dqs = []
                for h in (2 * j, 2 * j + 1):
                    _, _, qh, dom = head(h)
                    dvb = dvb + _dot_tn(p_ref[h], dom)
                    dkb = dkb + _dot_tn(ds_ref[h], qh)
                    dqs.append(_dot(ds_ref[h], kb_ref[g]))
                dq_ref[:, LANES * j:LANES * (j + 1)] = jnp.where(lo, dqs[0], dqs[1])
            dkp_ref[:, gs] = dkb[:WINDOW]
            dkc_ref[:, gs] = dkb[WINDOW:]
            dvp_ref[:, gs] = dvb[:WINDOW]
            dvc_ref[:, gs] = dvb[WINDOW:]

    band = pltpu.VMEM((2, 2 * WINDOW, LANES), BF16)
    return _pcall(body, name="swa_bwd", grid=(nb,),
                  in_specs=[pl.BlockSpec(memory_space=pltpu.SMEM), cur(D), cur(256), prev(256), cur(256), prev(256),
                            cur(D), cur(D), cur(D)],
                  out_specs=[cur(D), cur(256), cur(256), cur(256), cur(256), _full((8, LANES))],
                  out_shape=[SDS((T, D), F32), SDS((T, 256), F32), SDS((T, 256), F32), SDS((T, 256), F32), SDS((T, 256), F32),
                             SDS((8, LANES), F32)],
                  scratch=[band, band, pltpu.VMEM((SWA_HEADS, WINDOW, 2 * WINDOW), F32),
                           pltpu.VMEM((SWA_HEADS, WINDOW, 2 * WINDOW), F32), pltpu.VMEM((SWA_HEADS, WINDOW, 2 * WINDOW), BF16),
                           pltpu.VMEM((SWA_HEADS, WINDOW, 2 * WINDOW), BF16)],
                  sem=("arbitrary",))(sinks, qs, ks, ks, vs, vs, do, o, lse)


def _bwd_qkv(dqm, dkm, dvm, dqs, dkc, dkp, dvc, dvp, z, gq, gkv, wqb, wkn, wv, tab_m, tab_s):
    T = z.shape[0]
    tm = WINDOW
    nb = T // tm
    per = dqm.shape[2] // tm

    tab_mt = [t.T for t in tab_m]
    half = MLA_ROPE // 2

    def rope_t(v, c, a, b):
        return v * c + pltpu.roll(v, LANES - half, 0) * a + pltpu.roll(v, half, 0) * b

    def rms_bwd_t(dy, x, g):
        r = lax.rsqrt(jnp.mean(x * x, axis=0, keepdims=True) + EPS)
        xn = x * r
        dn = dy * g
        return r * (dn - xn * jnp.mean(dn * xn, axis=0, keepdims=True)), jnp.sum(dy * xn, axis=1, keepdims=True)

    def body(dqm_ref, dkm_ref, dvm_ref, dqs_ref, dkc_ref, dkp_ref, dvc_ref, dvp_ref, qa_ref, kva_ref, gq_ref, gkv_ref,
             wqb_ref, wkn_ref, wv_ref, cmt_ref, amt_ref, bmt_ref, cs_ref, as_ref, bs_ref,
             dq_out, dkn_out, dv_out, dsq_ref, drest_ref, dgq_ref, dgkv_ref):
        i = pl.program_id(0)

        @pl.when(i == 0)
        def _():
            dgq_ref[...] = jnp.zeros_like(dgq_ref)
            dgkv_ref[...] = jnp.zeros_like(dgkv_ref)

        cmt, amt, bmt = cmt_ref[...], -amt_ref[...], -bmt_ref[...]
        cs, as_, bs = cs_ref[...], -as_ref[...], -bs_ref[...]
        row = lax.broadcasted_iota(jnp.int32, (LANES, tm), 0)
        nope = row < MLA_NOPE
        roped = jnp.logical_and(row >= MLA_NOPE, row < MLA_NOPE + MLA_ROPE)
        dkr = jnp.zeros((LANES, tm), F32)
        for h in range(MLA_HEADS):
            sl = slice(LANES * h, LANES * (h + 1))
            dq_out[0, sl, :] = rope_t(dqm_ref[0, sl, :], cmt, amt, bmt).astype(BF16)
            dk_h = dkm_ref[0, sl, :]
            dkn_out[0, sl, :] = jnp.where(nope, dk_h, 0.0).astype(BF16)
            dkr = dkr + jnp.where(roped, dk_h, 0.0)
        dv_out[0] = dvm_ref[0].astype(BF16)
        dqn = _dot(wqb_ref[...], dq_out[0])
        dkvn = _dot(wkn_ref[...], dkn_out[0]) + _dot(wv_ref[...], dv_out[0])
        dqa, dgq = rms_bwd_t(dqn, qa_ref[...].T, gq_ref[...])
        dkva, dgkv = rms_bwd_t(dkvn, kva_ref[...].T, gkv_ref[...])
        dgq_ref[...] += dgq
        dgkv_ref[...] += dgkv
        for j in range(D // LANES):
            sl = slice(LANES * j, LANES * (j + 1))
            dsq_ref[:, sl] = _rope(dqs_ref[:, sl], cs, as_, bs, SWA_HD // 2).astype(BF16)
        keep = (i < nb - 1).astype(F32)
        drest_ref[:, 0:256] = dqa.T.astype(BF16)
        for j in range(2):
            sl = slice(LANES * j, LANES * (j + 1))
            dk = dkc_ref[:, sl] + keep * dkp_ref[:, sl]
            drest_ref[:, 256 + LANES * j:256 + LANES * (j + 1)] = _rope(dk, cs, as_, bs, SWA_HD // 2).astype(BF16)
        drest_ref[:, 512:768] = (dvc_ref[...] + keep * dvp_ref[...]).astype(BF16)
        drest_ref[:, 768:896] = dkva.T.astype(BF16)
        drest_ref[:, 896:1024] = rope_t(dkr, cmt, amt, bmt).T.astype(BF16)

    nxt = pl.BlockSpec((tm, 256), lambda i: (jnp.minimum(i + 1, nb - 1), 0))
    tab = [_rows(tm, LANES)] * 3
    tab_t = [pl.BlockSpec((LANES, tm), lambda i: (0, i))] * 3
    blk_t = lambda w: pl.BlockSpec((1, w, tm), lambda i: (i // per, 0, i % per))
    return _pcall(body, name="bwd_qkv", grid=(nb,),
                  in_specs=[blk_t(2048), blk_t(2048), blk_t(1024), _rows(tm, 1024), _rows(tm, 256), nxt,
                            _rows(tm, 256), nxt, _rows(tm, 256, 12), _rows(tm, 128, 30), _full((Q_LORA, 1)), _full((KV_LORA, 1)),
                            _full((Q_LORA, 2048)), _full((KV_LORA, 2048)), _full((KV_LORA, 1024))] + tab_t + tab,
                  out_specs=[blk_t(2048), blk_t(2048), blk_t(1024), _rows(tm, 1024), _rows(tm, 1024),
                             _full((Q_LORA, 1)), _full((KV_LORA, 1))],
                  out_shape=[SDS(dqm.shape, BF16), SDS(dkm.shape, BF16), SDS(dvm.shape, BF16), SDS((T, 1024), BF16),
                             SDS((T, 1024), BF16), SDS((Q_LORA, 1), F32), SDS((KV_LORA, 1), F32)],
                  sem=("arbitrary",))(dqm, dkm, dvm, dqs, dkc, dkp, dvc, dvp, z, z, gq.reshape(Q_LORA, 1),
                                      gkv.reshape(KV_LORA, 1), wqb, wkn, wv, *tab_mt, *tab_s)


def _bwd_in(dsq, dga, dgb, drest, w_in_p, x, g1, dx1, tm):
    T = x.shape[0]

    def body(a_ref, b_ref, c_ref, d_ref, w_ref, x_ref, g_ref, dx1_ref, dx_ref, dg_ref):
        @pl.when(pl.program_id(0) == 0)
        def _():
            dg_ref[...] = jnp.zeros_like(dg_ref)

        dh = (_dot_nt(a_ref[...], w_ref[:, 0:1024]) + _dot_nt(b_ref[...], w_ref[:, 1024:2048])
              + _dot_nt(c_ref[...], w_ref[:, 2048:3072]) + _dot_nt(d_ref[...], w_ref[:, 3072:4096]))
        dx, dg = _rms_bwd(dh, x_ref[...], g_ref[...])
        dg_ref[...] += dg
        dx_ref[...] = dx1_ref[...] + dx

    r = _rows(tm, D)
    return _pcall(body, name="bwd_in", grid=(T // tm,),
                  in_specs=[r, r, r, r, _full((D, NZ)), r, _full((1, D)), r],
                  out_specs=[r, _full((1, D))], out_shape=[SDS((T, D), F32), SDS((1, D), F32)],
                  sem=("arbitrary",))(dsq, dga, dgb, drest, w_in_p, x, g1, dx1)


def _wgrad(a, g, name, into=None):
    T, K = a.shape
    N = g.shape[1]
    tk, tn, tt = min(K, 1024), min(N, 1024), min(T, 1024)
    if into is not None:
        buf, weight = into
        _, row0, lane0 = PACK_AT[weight]
        shard = {n: (r, c) for n, r, c in BIG}[weight]
        assert lane0 == 0 and shard[1] == D and tk % shard[0] == 0
        per_step = tk // shard[0]
    assert K % tk == 0 and N % tn == 0 and T % tt == 0, (a.shape, g.shape)
    steps = T // tt

    def body(a_ref, g_ref, *rest):
        o_ref, acc_ref = rest[-2:]
        t = pl.program_id(2)

        @pl.when(t == 0)
        def _():
            acc_ref[...] = jnp.zeros_like(acc_ref)

        acc_ref[...] += _dot_tn(a_ref[...].astype(BF16), g_ref[...].astype(BF16))

        @pl.when(t == steps - 1)
        def _():
            o_ref[...] = acc_ref[...].astype(o_ref.dtype).reshape(o_ref.shape)

    in_specs = [pl.BlockSpec((tt, tk), lambda k, n, t: (t, k)), pl.BlockSpec((tt, tn), lambda k, n, t: (t, n))]
    if into is None:
        return _pcall(body, name=name, grid=(K // tk, N // tn, steps), in_specs=in_specs,
                      out_specs=pl.BlockSpec((tk, tn), lambda k, n, t: (k, n)), out_shape=SDS((K, N), F32),
                      scratch=[pltpu.VMEM((tk, tn), F32)], sem=("parallel", "parallel", "arbitrary"))(a, g)
    assert row0 % shard[0] == 0 and (K // tk) * (N // tn) * per_step == N_CHIPS
    return _pcall(body, name=name, grid=(K // tk, N // tn, steps), in_specs=in_specs + [ANY],
                  out_specs=pl.BlockSpec((per_step, shard[0], tn), lambda k, n, t: (k + n, row0 // shard[0], 0)),
                  out_shape=SDS(buf.shape, buf.dtype),
                  scratch=[pltpu.VMEM((tk, tn), F32)], sem=("parallel", "parallel", "arbitrary"), aliases={2: 0})(a, g, buf)


def _wgrad_t(at, g, name):
    nblk, K, tt = at.shape
    N = g.shape[1]
    tk = min(K, 1024)
    per_step = 4 if nblk % 4 == 0 else 1
    assert K % tk == 0 and g.shape[0] == nblk * tt

    def body(a_ref, g_ref, o_ref):
        @pl.when(pl.program_id(1) == 0)
        def _():
            o_ref[...] = jnp.zeros_like(o_ref)

        acc = _dot(a_ref[0], g_ref[0:tt, :].astype(BF16))
        for b in range(1, per_step):
            acc = acc + _dot(a_ref[b], g_ref[tt * b:tt * (b + 1), :].astype(BF16))
        o_ref[...] += acc

    return _pcall(body, name=name, grid=(K // tk, nblk // per_step),
                  in_specs=[pl.BlockSpec((per_step, tk, tt), lambda k, t: (t, k, 0)),
                            pl.BlockSpec((per_step * tt, N), lambda k, t: (t, 0))],
                  out_specs=pl.BlockSpec((tk, N), lambda k, t: (k, 0)), out_shape=SDS((K, N), F32),
                  sem=("parallel", "arbitrary"))(at, g)


def _adamw(w, packed_g, m, v, name, transposed=False):
    R, C = w.shape[1:][::-1] if transposed else w.shape[1:]
    _, row0, lane0 = PACK_AT[name]
    tr = min(R, 256 if row0 % 256 == 0 else 128)
    assert row0 % tr == 0 and R % tr == 0

    def body(w_ref, g_ref, m_ref, v_ref, go_ref, d_ref, m2_ref, v2_ref):
        g_ = g_ref[...].T[lane0:lane0 + C] if transposed else g_ref[:, lane0:lane0 + C]
        go_ref[0] = g_
        m2 = ADAM_B1 * m_ref[0] + (1.0 - ADAM_B1) * g_
        v2 = ADAM_B2 * v_ref[0] + (1.0 - ADAM_B2) * jnp.square(g_)
        m_hat = m2 / (1.0 - ADAM_B1 ** ADAM_STEP)
        v_hat = v2 / (1.0 - ADAM_B2 ** ADAM_STEP)
        d_ref[0] = -ADAM_LR * (m_hat / (jnp.sqrt(v_hat) + ADAM_EPS) + ADAM_WD * w_ref[0])
        m2_ref[0] = m2
        v2_ref[0] = v2

    r = pl.BlockSpec((1, C, tr), lambda i: (0, 0, i)) if transposed else pl.BlockSpec((1, tr, C), lambda i: (0, i, 0))
    return _pcall(body, name="adamw_" + name, grid=(R // tr,),
                  in_specs=[r, pl.BlockSpec((tr, D), lambda i: (row0 // tr + i, 0)), r, r], out_specs=[r] * 4,
                  out_shape=[SDS(w.shape, F32)] * 4, sem=("parallel",))(w, packed_g, m, v)


def _adamw_small(w, parts, m, v):
    def body(w_ref, p_ref, m_ref, v_ref, g_ref, d_ref, m2_ref, v2_ref):
        g_ = p_ref[0]
        for k in range(1, N_DEV):
            g_ = g_ + p_ref[k]
        g_ref[...] = g_
        m2 = ADAM_B1 * m_ref[...] + (1.0 - ADAM_B1) * g_
        v2 = ADAM_B2 * v_ref[...] + (1.0 - ADAM_B2) * jnp.square(g_)
        m_hat = m2 / (1.0 - ADAM_B1 ** ADAM_STEP)
        v_hat = v2 / (1.0 - ADAM_B2 ** ADAM_STEP)
        d_ref[...] = -ADAM_LR * (m_hat / (jnp.sqrt(v_hat) + ADAM_EPS) + ADAM_WD * w_ref[...])
        m2_ref[...] = m2
        v2_ref[...] = v2

    s = _full((8, D))
    return _pcall(body, name="adamw_small", grid=(1,), in_specs=[s, _full((N_DEV, 8, D)), s, s], out_specs=[s] * 4,
                  out_shape=[SDS((8, D), F32)] * 4, sem=("arbitrary",))(w, parts, m, v)


ANY = pl.BlockSpec(memory_space=pl.ANY)


def _place():
    x, y, c = lax.axis_index("x"), lax.axis_index("y"), lax.axis_index("c")
    chips = [(1 - x, y), (x, 1 - y), (1 - x, 1 - y)]
    return x, y, c, chips


def _all_gather(wpk):
    rows = wpk.shape[0]
    HALF = rows // 2
    assert HALF % 16 == 0

    def body(in_ref, out_ref, send_sems, recv_sems):
        x, y, c, chips = _place()
        half = pl.ds(pl.multiple_of(c * HALF, 16), HALF)
        other = pl.ds(pl.multiple_of((1 - c) * HALF, 16), HALF)

        def copy(k, src, dst, to):
            return pltpu.make_async_remote_copy(src_ref=src, dst_ref=dst, send_sem=send_sems.at[k], recv_sem=recv_sems.at[k],
                                                device_id=to, device_id_type=MESH)

        first = [copy(k, in_ref.at[half], out_ref.at[2 * x + y, half], (cx, cy, c)) for k, (cx, cy) in enumerate(chips)]
        first.append(copy(6, in_ref, out_ref.at[2 * x + y], (x, y, 1 - c)))
        for cp in first:
            cp.start()
        passed = []
        for k, (cx, cy) in enumerate(chips):
            slot = out_ref.at[2 * cx + cy, half]
            copy(k, slot, slot, (x, y, c)).wait_recv()
            fwd = copy(3 + k, slot, slot, (x, y, 1 - c))
            fwd.start()
            passed.append(fwd)
        for k, (cx, cy) in enumerate(chips):
            slot = out_ref.at[2 * cx + cy, other]
            copy(3 + k, slot, slot, (x, y, c)).wait_recv()
        copy(6, in_ref, out_ref.at[2 * x + y], (x, y, c)).wait_recv()
        for cp in first + passed:
            cp.wait_send()

    return _pcall(body, name="all_gather_weights", in_specs=[ANY], out_specs=ANY,
                  out_shape=SDS((N_CHIPS, rows, D), BF16),
                  scratch=[pltpu.SemaphoreType.DMA((7,)), pltpu.SemaphoreType.DMA((7,))])(wpk)


HBM = pl.BlockSpec(memory_space=pltpu.HBM)
SEM = pl.BlockSpec(memory_space=pltpu.SEMAPHORE)
DATAFLOW = pltpu.SideEffectType.DATAFLOW_SIDE_EFFECTING


def _in_hbm(a):
    return pltpu.with_memory_space_constraint(a, pltpu.HBM)


def _gather_late_start(wpk, after):
    rows = wpk.shape[0]

    def body(in_ref, land_ref, after_ref, send_sems, recv_sems, in_thru, land_thru, token):
        x, y, c, chips = _place()
        for k, to in enumerate([(cx, cy, c) for cx, cy in chips] + [(x, y, 1 - c)]):
            pltpu.make_async_remote_copy(src_ref=in_ref, dst_ref=land_ref.at[2 * x + y], send_sem=send_sems.at[k],
                                         recv_sem=recv_sems.at[k], device_id=to, device_id_type=MESH).start()
        token[...] = jnp.zeros_like(token)

    return pl.pallas_call(
        body, name="gather_late_start",
        out_shape=(pltpu.SemaphoreType.DMA((4,)), pltpu.SemaphoreType.DMA((4,)), pltpu.HBM(wpk.shape, wpk.dtype),
                   pltpu.HBM((N_CHIPS, rows, D), wpk.dtype), SDS((8, LANES), F32)),
        in_specs=(HBM, HBM, ANY), out_specs=(SEM, SEM, HBM, HBM, pl.BlockSpec(memory_space=pltpu.VMEM)),
        input_output_aliases={0: 2, 1: 3}, compiler_params=pltpu.CompilerParams(has_side_effects=DATAFLOW),
    )(_in_hbm(wpk), _in_hbm(lax.empty((N_CHIPS, rows, D), wpk.dtype)), after)


def _gather_late_wait(send_sems, recv_sems, in_thru, land_thru, after):
    def body(in_ref, land_ref, send_sems, recv_sems, after_ref, after2_ref, in_dead, got_ref):
        x, y, c, chips = _place()
        for k, (sx, sy) in enumerate(chips + [(x, y)]):
            cp = pltpu.make_async_remote_copy(src_ref=in_ref, dst_ref=land_ref.at[2 * sx + sy], send_sem=send_sems.at[k],
                                              recv_sem=recv_sems.at[k], device_id=(x, y, c), device_id_type=MESH)
            cp.wait_send()
            cp.wait_recv()

    return pl.pallas_call(
        body, name="gather_late_wait",
        out_shape=(pltpu.HBM(in_thru.shape, in_thru.dtype), pltpu.HBM(land_thru.shape, land_thru.dtype)),
        in_specs=(HBM, HBM, SEM, SEM, ANY, ANY), out_specs=(HBM, HBM), input_output_aliases={0: 0, 1: 1},
        compiler_params=pltpu.CompilerParams(has_side_effects=DATAFLOW),
    )(in_thru, land_thru, send_sems, recv_sems, *after)[1]


def _rs_sibling(gpk):
    HALF = gpk.shape[1] // 2

    def body(in_ref, out_ref, send_sem, recv_sem):
        x, y, c, _ = _place()
        theirs = pl.ds(pl.multiple_of((1 - c) * HALF, 16), HALF)
        cp = pltpu.make_async_remote_copy(src_ref=in_ref.at[:, theirs], dst_ref=out_ref, send_sem=send_sem, recv_sem=recv_sem,
                                          device_id=(x, y, 1 - c), device_id_type=MESH)
        cp.start()
        cp.wait()

    return _pcall(body, name="rs_sibling", in_specs=[ANY], out_specs=ANY, out_shape=SDS((N_CHIPS, HALF, D), gpk.dtype),
                  scratch=[pltpu.SemaphoreType.DMA, pltpu.SemaphoreType.DMA])(gpk)


def _rs_add_sibling(cidx, gpk, got):
    HALF = got.shape[1]
    th = HALF // 4
    nh = HALF // th
    assert th % 16 == 0

    def body(c_ref, a_ref, b_ref, o_ref):
        o_ref[...] = (a_ref[...].astype(F32) + b_ref[...].astype(F32)).astype(BF16)

    gs = pltpu.PrefetchScalarGridSpec(
        num_scalar_prefetch=1, grid=(N_CHIPS, nh),
        in_specs=[pl.BlockSpec((1, th, D), lambda j, i, c: (j, c[0] * nh + i, 0)), pl.BlockSpec((1, th, D), lambda j, i, c: (j, i, 0))],
        out_specs=pl.BlockSpec((1, th, D), lambda j, i, c: (j, i, 0)))
    return pl.pallas_call(body, name="rs_add_sibling", grid_spec=gs, out_shape=SDS((N_CHIPS, HALF, D), BF16),
                          compiler_params=pltpu.CompilerParams(dimension_semantics=("parallel", "parallel"),
                                                               vmem_limit_bytes=48 << 20))(cidx, gpk, got)


def _rs_chips_start(part, small, after):
    def body(p_ref, s_ref, land_ref, sland_ref, after_ref, send_sems, recv_sems, p_thru, s_thru, land_thru, sland_thru, token):
        x, y, c, chips = _place()
        for k, (cx, cy) in enumerate(chips):
            pltpu.make_async_remote_copy(src_ref=p_ref.at[2 * cx + cy], dst_ref=land_ref.at[2 * x + y], send_sem=send_sems.at[k],
                                         recv_sem=recv_sems.at[k], device_id=(cx, cy, c), device_id_type=MESH).start()
        peers = [(x, y, 1 - c)] + [(cx, cy, c) for cx, cy in chips] + [(cx, cy, 1 - c) for cx, cy in chips]
        for k, to in enumerate(peers):
            pltpu.make_async_remote_copy(src_ref=s_ref, dst_ref=sland_ref.at[4 * x + 2 * y + c], send_sem=send_sems.at[3 + k],
                                         recv_sem=recv_sems.at[3 + k], device_id=to, device_id_type=MESH).start()
        token[...] = jnp.zeros_like(token)

    return pl.pallas_call(
        body, name="rs_chips_start",
        out_shape=(pltpu.SemaphoreType.DMA((10,)), pltpu.SemaphoreType.DMA((10,)), pltpu.HBM(part.shape, part.dtype),
                   pltpu.HBM(small.shape, small.dtype), pltpu.HBM(part.shape, part.dtype), pltpu.HBM((N_DEV, 8, D), F32),
                   SDS((8, LANES), F32)),
        in_specs=(HBM, HBM, HBM, HBM, ANY), out_specs=(SEM, SEM, HBM, HBM, HBM, HBM, pl.BlockSpec(memory_space=pltpu.VMEM)),
        input_output_aliases={0: 2, 1: 3, 2: 4, 3: 5}, compiler_params=pltpu.CompilerParams(has_side_effects=DATAFLOW),
    )(_in_hbm(part), _in_hbm(small), _in_hbm(lax.empty(part.shape, part.dtype)), _in_hbm(lax.empty((N_DEV, 8, D), F32)), after)


def _rs_chips_wait(send_sems, recv_sems, p_thru, s_thru, land_thru, sland_thru, after):
    def body(p_ref, s_ref, land_ref, sland_ref, send_sems, recv_sems, *after_and_outputs):
        x, y, c, chips = _place()
        for k, (cx, cy) in enumerate(chips):
            cp = pltpu.make_async_remote_copy(src_ref=p_ref.at[0], dst_ref=land_ref.at[2 * cx + cy], send_sem=send_sems.at[k],
                                              recv_sem=recv_sems.at[k], device_id=(cx, cy, c), device_id_type=MESH)
            cp.wait_send()
            cp.wait_recv()
        peers = [(x, y, 1 - c)] + [(cx, cy, c) for cx, cy in chips] + [(cx, cy, 1 - c) for cx, cy in chips]
        for k, (px, py, pc) in enumerate(peers):
            cp = pltpu.make_async_remote_copy(src_ref=s_ref, dst_ref=sland_ref.at[4 * px + 2 * py + pc], send_sem=send_sems.at[3 + k],
                                              recv_sem=recv_sems.at[3 + k], device_id=(px, py, pc), device_id_type=MESH)
            cp.wait_send()
            cp.wait_recv()

    hbm = lambda a: pltpu.HBM(a.shape, a.dtype)
    outs = pl.pallas_call(
        body, name="rs_chips_wait", out_shape=(hbm(p_thru), hbm(s_thru), hbm(land_thru), hbm(sland_thru)),
        in_specs=(HBM, HBM, HBM, HBM, SEM, SEM) + (ANY,) * len(after), out_specs=(HBM, HBM, HBM, HBM),
        input_output_aliases={0: 0, 1: 1, 2: 2, 3: 3}, compiler_params=pltpu.CompilerParams(has_side_effects=DATAFLOW),
    )(p_thru, s_thru, land_thru, sland_thru, send_sems, recv_sems, *after)
    return outs[0], outs[2], outs[3]


def _rs_add_chips(qidx, part, parts):
    HALF = part.shape[1]
    th = HALF // 4
    nh = HALF // th
    assert th % 16 == 0

    def body(q_ref, own_ref, p_ref, o_ref):
        for me in range(N_CHIPS):
            @pl.when(q_ref[0] == me)
            def _(me=me):
                t = [(own_ref[0] if j == me else p_ref[j]).astype(F32) for j in range(N_CHIPS)]
                o_ref[...] = ((t[0] + t[1]) + t[2]) + t[3]

    gs = pltpu.PrefetchScalarGridSpec(
        num_scalar_prefetch=1, grid=(HALF // th,),
        in_specs=[pl.BlockSpec((1, th, D), lambda i, q: (q[0], i, 0)), pl.BlockSpec((N_CHIPS, th, D), lambda i, q: (0, i, 0))],
        out_specs=pl.BlockSpec((th, D), lambda i, q: (q[1] * nh + i, 0)))
    return pl.pallas_call(body, name="rs_add_chips", grid_spec=gs, out_shape=SDS((2 * HALF, D), F32),
                          compiler_params=pltpu.CompilerParams(dimension_semantics=("parallel",),
                                                               vmem_limit_bytes=48 << 20))(qidx, part, parts)


def _rs_join(shard, name):
    HALF = shard.shape[0] // 2

    def body(in_ref, out_ref, send_sem, recv_sem):
        x, y, c, _ = _place()
        rows = pl.ds(pl.multiple_of(c * HALF, 16), HALF)
        cp = pltpu.make_async_remote_copy(src_ref=in_ref.at[rows], dst_ref=out_ref.at[rows], send_sem=send_sem, recv_sem=recv_sem,
                                          device_id=(x, y, 1 - c), device_id_type=MESH)
        cp.start()
        cp.wait()

    return _pcall(body, name=name, in_specs=[ANY], out_specs=ANY, out_shape=SDS(shard.shape, F32),
                  scratch=[pltpu.SemaphoreType.DMA, pltpu.SemaphoreType.DMA], aliases={0: 0})(shard)


def _reduce_late_start(gpk, after):
    rows = gpk.shape[1]
    HALF = rows // 2
    assert HALF % 16 == 0

    def body(in_ref, land_ref, after_ref, send_sems, recv_sems, in_thru, land_thru, token):
        x, y, c, chips = _place()
        me = 4 * x + 2 * y + c
        peers = [(x, y, 1 - c)] + [(cx, cy, c) for cx, cy in chips] + [(cx, cy, 1 - c) for cx, cy in chips]
        for k, (px, py, pc) in enumerate(peers):
            src = in_ref.at[2 * px + py, pl.ds(pl.multiple_of(pc * HALF, 16), HALF)]
            pltpu.make_async_remote_copy(src_ref=src, dst_ref=land_ref.at[me], send_sem=send_sems.at[k], recv_sem=recv_sems.at[k],
                                         device_id=(px, py, pc), device_id_type=MESH).start()
        token[...] = jnp.zeros_like(token)

    return pl.pallas_call(
        body, name="reduce_late_start",
        out_shape=(pltpu.SemaphoreType.DMA((7,)), pltpu.SemaphoreType.DMA((7,)), pltpu.HBM(gpk.shape, gpk.dtype),
                   pltpu.HBM((N_DEV, HALF, D), gpk.dtype), SDS((8, LANES), F32)),
        in_specs=(HBM, HBM, ANY), out_specs=(SEM, SEM, HBM, HBM, pl.BlockSpec(memory_space=pltpu.VMEM)),
        input_output_aliases={0: 2, 1: 3}, compiler_params=pltpu.CompilerParams(has_side_effects=DATAFLOW),
    )(_in_hbm(gpk), _in_hbm(lax.empty((N_DEV, HALF, D), gpk.dtype)), after)


def _reduce_late_wait(send_sems, recv_sems, in_thru, land_thru, after):
    def body(in_ref, land_ref, send_sems, recv_sems, after_ref, in_out, got_ref):
        x, y, c, chips = _place()
        peers = [(x, y, 1 - c)] + [(cx, cy, c) for cx, cy in chips] + [(cx, cy, 1 - c) for cx, cy in chips]
        for k, (px, py, pc) in enumerate(peers):
            cp = pltpu.make_async_remote_copy(src_ref=land_ref.at[0], dst_ref=land_ref.at[4 * px + 2 * py + pc],
                                              send_sem=send_sems.at[k], recv_sem=recv_sems.at[k],
                                              device_id=(px, py, pc), device_id_type=MESH)
            cp.wait_send()
            cp.wait_recv()

    return pl.pallas_call(
        body, name="reduce_late_wait",
        out_shape=(pltpu.HBM(in_thru.shape, in_thru.dtype), pltpu.HBM(land_thru.shape, land_thru.dtype)),
        in_specs=(HBM, HBM, SEM, SEM, ANY), out_specs=(HBM, HBM), input_output_aliases={0: 0, 1: 1},
        compiler_params=pltpu.CompilerParams(has_side_effects=DATAFLOW),
    )(in_thru, land_thru, send_sems, recv_sems, after)


def _reduce_late_add(didx, gpk, parts):
    HALF = parts.shape[1]
    th = HALF // 4
    nh = HALF // th
    assert th % 16 == 0

    def body(d_ref, own_ref, p_ref, o_ref):
        for me in range(N_DEV):
            @pl.when(d_ref[0] == me)
            def _(me=me):
                t = [(own_ref[0] if j == me else p_ref[j]).astype(F32) for j in range(N_DEV)]
                o_ref[...] = ((((((t[0] + t[1]) + t[2]) + t[3]) + t[4]) + t[5]) + t[6]) + t[7]

    gs = pltpu.PrefetchScalarGridSpec(
        num_scalar_prefetch=1, grid=(nh,),
        in_specs=[pl.BlockSpec((1, th, D), lambda i, d: (d[1], d[2] * nh + i, 0)), pl.BlockSpec((N_DEV, th, D), lambda i, d: (0, i, 0))],
        out_specs=pl.BlockSpec((th, D), lambda i, d: (d[2] * nh + i, 0)))
    return pl.pallas_call(body, name="reduce_late_add", grid_spec=gs, out_shape=SDS((2 * HALF, D), F32),
                          compiler_params=pltpu.CompilerParams(dimension_semantics=("parallel",),
                                                               vmem_limit_bytes=48 << 20))(didx, gpk, parts)


def _pack_early(b, dtype):
    out = jnp.zeros((PACK_ROWS["early"], D), dtype)
    for n in ("w_in", "w_q_b", "w_ple", "w_kv_b"):
        out = lax.dynamic_update_slice(out, b[n].astype(dtype), PACK_AT[n][1:])
    return out


def _pack_late(b, dtype):
    return jnp.concatenate([b[n].astype(dtype) for n in ("w_mla_up", "w_swa_up", "w_out", "w_ple_gate", "w_mlp_up", "w_mlp_down")],
                           axis=0)


def _full_weights(gathered, which):
    out = {}
    for n, r, c in BIG:
        buf, row0, lane0 = PACK_AT[n]
        if buf != which:
            continue
        blk = gathered[:, row0:row0 + r, lane0:lane0 + c]
        if n in ("w_mlp_up", "w_mlp_down"):
            assert (r, c) == (D, D) and row0 % D == 0
            out[n] = (gathered, row0 // D)
        elif n == "w_in":
            out["w_in_p"] = _w_in_internal([blk[j] for j in range(N_CHIPS)])
        elif n in COL_SHARDED:
            out[n] = jnp.swapaxes(blk, 0, 1).reshape(r, N_CHIPS * c)
        else:
            out[n] = blk.reshape(N_CHIPS * r, c)
    return out


def _split_full_grads(grads, which, dtype):
    shard = {n: (r, c) for n, r, c in BIG}
    out = jnp.zeros((N_CHIPS, PACK_ROWS[which], D), dtype)
    for j in range(N_CHIPS):
        for n, g in grads.items():
            if n == "w_in_p":
                n, blk = "w_in", _w_in_grad_shard(g, j)
            else:
                r, c = shard[n]
                blk = g[:, j * c:(j + 1) * c] if n in COL_SHARDED else g[j * r:(j + 1) * r]
            assert PACK_AT[n][0] == which
            out = lax.dynamic_update_slice(out, blk.astype(dtype)[None], (j,) + PACK_AT[n][1:])
    return out


W_IN_SHARD = 936
W_IN_SEGMENTS = ((0, 256, (3072,)), (256, 384, (3840,)), (384, 416, (4032,)), (416, 1440, (0,)), (1440, 1504, (3328, 3392)),
                 (1504, 1568, (3456, 3520)), (1568, 1632, (3584, 3648)), (1632, 1696, (3712, 3776)), (1696, 3744, (1024,)))


def _w_in_internal(shards):
    def cols(a, b):
        out = []
        for j, s in enumerate(shards):
            lo, hi = max(a, W_IN_SHARD * j), min(b, W_IN_SHARD * (j + 1))
            if lo < hi:
                out.append(s[:, lo - W_IN_SHARD * j:hi - W_IN_SHARD * j])
        return out

    pieces = {}
    for a, b, places in W_IN_SEGMENTS:
        for at in places:
            pieces[at] = cols(a, b)
    zeros = lambda n: [jnp.zeros((D, n), shards[0].dtype)]
    pieces[3968] = zeros(64)
    pieces[4064] = zeros(32)
    return jnp.concatenate([piece for at in sorted(pieces) for piece in pieces[at]], axis=1)


def _w_in_grad_shard(g, j):
    def internal(a, b):
        out = []
        while a < b:
            end = min(b, (a // D + 1) * D)
            out.append(g[a // D][:, a % D:a % D + end - a])
            a = end
        return out

    out = []
    for a, b, places in W_IN_SEGMENTS:
        lo, hi = max(a, W_IN_SHARD * j), min(b, W_IN_SHARD * (j + 1))
        if lo < hi:
            parts = [internal(at + lo - a, at + hi - a) for at in places]
            if len(parts) == 1:
                out += parts[0]
            else:
                assert len(parts[0]) == len(parts[1]) == 1
                out.append(parts[0][0] + parts[1][0])
    return jnp.concatenate(out, axis=1)


def _local_step(x, p, tgt, w, small, late_weights, late_grads_out):
    T = x.shape[0]
    tm = 256
    tb = 256
    w_in_p = w["w_in_p"]
    wqb = jnp.pad(w["w_q_b"].reshape(Q_LORA, MLA_HEADS, 96), ((0, 0), (0, 0), (0, 32))).reshape(Q_LORA, 2048)
    wkv = w["w_kv_b"].reshape(KV_LORA, MLA_HEADS, 128)
    wkn = jnp.pad(wkv[:, :, :64], ((0, 0), (0, 0), (0, 64))).reshape(KV_LORA, 2048)
    wv = wkv[:, :, 64:].reshape(KV_LORA, 1024)
    tab_m = _rope_tables(T, "mla")
    tab_s = _rope_tables(T, "swa")
    g1, gq, gkv, sinks = small["g_mix_pre"], small["g_q_a"], small["g_kv_a"], small["sinks"]
    g2, g3, g4, g5 = small["g_mix_post"], small["g_mlp_pre"], small["g_mlp_post"], small["g_ple"]
    sink_vec = sinks.reshape(SWA_HEADS)

    z, h1 = _fwd_in(x, g1, w_in_p, tm)
    qn, kvn, km, vm, qt, kt, vt, qs, ks, vs = _fwd_qkv(z, gq, gkv, wqb, wkn, wv, tab_m, tab_s, tb)
    om, lse_m = _mla_fwd(qt, km, vt, tb)
    os_, lse_s = _swa_fwd(sink_vec, qs, ks, vs)
    w = {**w, **late_weights((om, os_))}
    y, yo, au, bu, x1 = _fwd_mix(om, os_, z, x, w["w_mla_up"], w["w_swa_up"], w["w_out"], g2, tm)
    h2, u = _fwd_mlp_up(x1, g3, w["w_mlp_up"], tm)
    d, x2 = _fwd_mlp_down(u, w["w_mlp_down"], x1, g4, tm)
    loss, dx2, dgt, de0, dg5 = _ple_fwd_bwd(p, x2, tgt, w["w_ple"], g5, w["w_ple_gate"], tm)

    dd, da, dg4 = _bwd_mlp_down(dx2, d, g4, w["w_mlp_down"], u, tm)
    dx1, dg3 = _bwd_mlp_up(da, w["w_mlp_up"], x1, g3, dx2, tm)
    dyo, dg2, dau, dbu, dga, dgb, dos, delta_m, dom_t = _bwd_mix(dx1, yo, g2, w["w_out"], z, au, bu, w["w_mla_up"],
                                                                w["w_swa_up"], om, tb)
    gpk_late = lax.empty((N_CHIPS, PACK_ROWS["late"], D), BF16)
    for weight, a_, g_ in (("w_mla_up", om, dau), ("w_swa_up", os_, dbu), ("w_out", y, dyo), ("w_ple_gate", x2, dgt),
                           ("w_mlp_up", h2, da), ("w_mlp_down", u, dd)):
        gpk_late = _wgrad(a_, g_, "wgrad_" + weight[2:], into=(gpk_late, weight))
    token = late_grads_out(gpk_late)
    delta_m = delta_m + token[0, 0]
    dqm, dkm, dvm = _mla_bwd(qt, km, kt, vm, dom_t, lse_m, delta_m, tb)
    dqs, dkc, dkp, dvc, dvp, dsink = _swa_bwd(sink_vec, qs, ks, vs, dos, os_, lse_s)
    dqb, dknb, dvb, dsq, drest, dgq, dgkv = _bwd_qkv(dqm, dkm, dvm, dqs, dkc, dkp, dvc, dvp, z, gq, gkv, wqb, wkn, wv,
                                                      tab_m, tab_s)
    gx, dg1 = _bwd_in(dsq, dga, dgb, drest, w_in_p, x, g1, dx1, tm)

    g_in_p = [_wgrad(h1, dsq, "wgrad_in_sq"), _wgrad(h1, dga, "wgrad_in_ga"), _wgrad(h1, dgb, "wgrad_in_gb"),
              _wgrad(h1, drest, "wgrad_in_rest")]
    g_qb_p = _wgrad_t(dqb, qn, "wgrad_q_b").T
    g_kn_p = _wgrad_t(dknb, kvn, "wgrad_kv_b_nope").T
    g_v_p = _wgrad_t(dvb, kvn, "wgrad_kv_b_v").T
    grads = {
        "w_in_p": g_in_p,
        "w_q_b": g_qb_p.reshape(Q_LORA, MLA_HEADS, 128)[:, :, :96].reshape(Q_LORA, 1536),
        "w_kv_b": jnp.concatenate([g_kn_p.reshape(KV_LORA, MLA_HEADS, 128)[:, :, :64], g_v_p.reshape(KV_LORA, MLA_HEADS, 64)],
                                  axis=2).reshape(KV_LORA, 2048),
        "w_ple": _wgrad(p, de0, "wgrad_ple"),
    }
    small_grads = {"g_mix_pre": dg1, "g_q_a": dgq.reshape(1, Q_LORA), "g_kv_a": dgkv.reshape(1, KV_LORA), "sinks": dsink[0:1, 0:SWA_HEADS], "g_mix_post": dg2,
                   "g_mlp_pre": dg3, "g_mlp_post": dg4, "g_ple": dg5}
    return loss, gx, grads, small_grads


def _pack_small(vals, fill, scalar=None):
    wide = [vals[n] for n, k in SMALL if k == D]
    narrow = [vals[n] for n, k in SMALL if k != D]
    used = sum(k for _, k in SMALL if k != D)
    last = jnp.concatenate(narrow + [jnp.full((1, D - used), fill, F32)], axis=1)
    rest = jnp.full((2, D), fill, F32)
    if scalar is not None:
        rest = jnp.concatenate([jnp.concatenate([scalar, rest[0:1, 1:]], axis=1), rest[1:2]], axis=0)
    return jnp.concatenate(wide + [last, rest], axis=0)


def _unpack_small(pk):
    out, row, off = {}, 0, 0
    for n, k in SMALL:
        if k == D:
            out[n] = pk[row:row + 1]
            row += 1
    for n, k in SMALL:
        if k != D:
            out[n] = pk[5:6, off:off + k]
            off += k
    return out


def kernel(x, p, g_mix_pre, w_in, g_q_a, w_q_b, g_kv_a, w_kv_b, sinks, w_mla_up, w_swa_up, w_out, g_mix_post, g_mlp_pre, w_mlp_up, w_mlp_down, g_mlp_post, w_ple, g_ple, w_ple_gate, loss_target, m_g_mix_pre, m_w_in, m_g_q_a, m_w_q_b, m_g_kv_a, m_w_kv_b, m_sinks, m_w_mla_up, m_w_swa_up, m_w_out, m_g_mix_post, m_g_mlp_pre, m_w_mlp_up, m_w_mlp_down, m_g_mlp_post, m_w_ple, m_g_ple, m_w_ple_gate, v_g_mix_pre, v_w_in, v_g_q_a, v_w_q_b, v_g_kv_a, v_w_kv_b, v_sinks, v_w_mla_up, v_w_swa_up, v_w_out, v_g_mix_post, v_g_mlp_pre, v_w_mlp_up, v_w_mlp_down, v_g_mlp_post, v_w_ple, v_g_ple, v_w_ple_gate):
    given = dict(locals())
    big_w = {n: given[n][0] for n, _, _ in BIG}
    small_w = {n: given[n] for n, _ in SMALL}
    small_m = {n: given["m_" + n] for n, _ in SMALL}
    small_v = {n: given["v_" + n] for n, _ in SMALL}

    core = lax.axis_index("c")
    chip = 2 * lax.axis_index("x") + lax.axis_index("y")
    core_i = core.astype(jnp.int32).reshape(1)
    dev_i = jnp.stack([2 * chip + core, chip, core]).astype(jnp.int32)

    own_early = _pack_early(big_w, BF16)
    own_late = _pack_late(big_w, BF16)
    got_early = _all_gather(own_early)
    late_flight = _gather_late_start(own_late, got_early)
    weights = _full_weights(got_early, "early")
    step_small = {**small_w, "g_mix_pre": small_w["g_mix_pre"] + late_flight[4][0, 0]}

    def late_weights(after):
        return _full_weights(_gather_late_wait(*late_flight[:4], after), "late")

    flight = {}

    def late_grads_out(gpk_late):
        flight["late"] = _reduce_late_start(gpk_late, dev_i)
        return flight["late"][4]

    loss_blk, gx, grads, small_grads = _local_step(x[0], p[0, 0], loss_target[0], weights, step_small, late_weights,
                                                   late_grads_out)

    gpk = _split_full_grads(grads, "early", BF16)
    got = _rs_sibling(gpk)
    part = _rs_add_sibling(core_i, gpk, got)
    small_own = _pack_small(small_grads, 0.0, loss_blk[0:1, 0:1])
    early_flight = _rs_chips_start(part, small_own, dev_i)

    out_g, out_d, out_m, out_v = {}, {}, {}, {}
    gpk_late, parts_late = _reduce_late_wait(*flight["late"][:4], early_flight[6])
    joined_late = _rs_join(_reduce_late_add(dev_i, gpk_late, parts_late), "rs_join_late")
    for n, _, _ in BIG:
        if PACK_AT[n][0] == "late":
            out_g[n], out_d[n], out_m[n], out_v[n] = _adamw(given[n], joined_late, given["m_" + n], given["v_" + n], n)

    part, parts, small_parts = _rs_chips_wait(*early_flight[:6], [out_d[n] for n in out_d])
    joined_early = _rs_join(_rs_add_chips(dev_i[1:3], part, parts), "rs_join_early")
    for n, _, _ in BIG:
        if PACK_AT[n][0] == "early":
            t = given[n].shape[2] % LANES != 0
            wmv = [jnp.swapaxes(a, 1, 2) if t else a for a in (given[n], given["m_" + n], given["v_" + n])]
            res = _adamw(wmv[0], joined_early, wmv[1], wmv[2], n, transposed=t)
            out_g[n], out_d[n], out_m[n], out_v[n] = [jnp.swapaxes(a, 1, 2) if t else a for a in res]

    mine = (lax.broadcasted_iota(jnp.int32, (N_DEV, 1, 1), 0) == dev_i[0])
    g_small_pk, d_small_pk, m_small_pk, v_small_pk = _adamw_small(
        _pack_small(small_w, 0.0), jnp.where(mine, small_own[None], small_parts), _pack_small(small_m, 0.0),
        _pack_small(small_v, 1.0))
    loss = g_small_pk[6, 0]
    for out, pk in ((out_g, g_small_pk), (out_d, d_small_pk), (out_m, m_small_pk), (out_v, v_small_pk)):
        out.update(_unpack_small(pk))
    order = ["g_mix_pre", "w_in", "g_q_a", "w_q_b", "g_kv_a", "w_kv_b", "sinks", "w_mla_up", "w_swa_up", "w_out", "g_mix_post",
             "g_mlp_pre", "w_mlp_up", "w_mlp_down", "g_mlp_post", "w_ple", "g_ple", "w_ple_gate"]
    return (loss, gx[None], *[out_g[n] for n in order], *[out_d[n] for n in order], *[out_m[n] for n in order],
            *[out_v[n] for n in order])
```

```python
import math

import jax
import jax.numpy as jnp
import numpy as np
from jax import lax
from jax.experimental import pallas as pl
from jax.experimental.pallas import tpu as pltpu

F32 = jnp.float32
BF16 = jnp.bfloat16
SDS = jax.ShapeDtypeStruct

D = 1024
D_FF = 4096
PLE = 256
Q_LORA = 256
KV_LORA = 128
MLA_HEADS = 16
MLA_NOPE = 64
MLA_ROPE = 32
SWA_HEADS = 16
SWA_HD = 64
WINDOW = 128
ROPE_THETA = 10000.0
EPS = 1e-6
NEG = -1e30
NZ = 4096
MLA_SCALE = (MLA_NOPE + MLA_ROPE) ** -0.5
LOG2_E = math.log2(math.e)
MLA_LOG2_SCALE = MLA_SCALE * LOG2_E
SWA_SCALE = SWA_HD ** -0.5

ADAM_LR = 0.001
ADAM_B1 = 0.9
ADAM_B2 = 0.999
ADAM_EPS = 1e-08
ADAM_WD = 0.01
ADAM_STEP = 10

LANES = 128
ATT_COLS = 128
VT_ROWS = 80
N_CHIPS = 4
N_DEV = 8
MESH = pl.DeviceIdType.MESH

NT = (((1,), (1,)), ((), ()))
TN = (((0,), (0,)), ((), ()))

BIG = (("w_in", 1024, 936), ("w_q_b", 256, 384), ("w_kv_b", 128, 512), ("w_mla_up", 256, 1024),
       ("w_swa_up", 256, 1024), ("w_out", 256, 1024), ("w_mlp_up", 1024, 1024), ("w_mlp_down", 1024, 1024),
       ("w_ple", 256, 256), ("w_ple_gate", 256, 1024))
COL_SHARDED = ("w_in", "w_q_b", "w_kv_b", "w_mlp_up", "w_ple")
PACK_AT = {"w_in": ("early", 0, 0), "w_q_b": ("early", 1024, 0), "w_ple": ("early", 1024, 384), "w_kv_b": ("early", 1280, 0),
           "w_mla_up": ("late", 0, 0), "w_swa_up": ("late", 256, 0), "w_out": ("late", 512, 0), "w_ple_gate": ("late", 768, 0),
           "w_mlp_up": ("late", 1024, 0), "w_mlp_down": ("late", 2048, 0)}
PACK_ROWS = {"early": 1408, "late": 3072}
SMALL = (("g_mix_pre", 1024), ("g_q_a", 256), ("g_kv_a", 128), ("sinks", 16), ("g_mix_post", 1024),
         ("g_mlp_pre", 1024), ("g_mlp_post", 1024), ("g_ple", 1024))


def _dot(a, b):
    return jnp.dot(a, b, preferred_element_type=F32)


def _dot_nt(a, b):
    return lax.dot_general(a, b, NT, preferred_element_type=F32)


def _dot_tn(a, b):
    return lax.dot_general(a, b, TN, preferred_element_type=F32)


def _pcall(body, *, name, out_shape, grid=(), in_specs=None, out_specs=None, scratch=(), sem=None, vmem_mb=48, aliases=None):
    params = dict(vmem_limit_bytes=vmem_mb << 20)
    if sem is not None:
        params["dimension_semantics"] = sem
    return pl.pallas_call(body, name=name, grid=grid, in_specs=in_specs, out_specs=out_specs, out_shape=out_shape,
                          scratch_shapes=list(scratch), input_output_aliases=aliases or {},
                          compiler_params=pltpu.CompilerParams(**params))


def _rows(tm, n, col=0):
    return pl.BlockSpec((tm, n), lambda i: (i, col))


def _full(shape):
    return pl.BlockSpec(shape, lambda i: (0,) * len(shape))


def _chunks(k):
    assert D_FF == N_CHIPS * D
    return pl.BlockSpec((N_CHIPS, D, D), lambda i: (0, k, 0))


def _rms(x, g):
    r = lax.rsqrt(jnp.mean(x * x, axis=-1, keepdims=True) + EPS)
    return x * r * g


def _rms_bwd(dy, x, g):
    r = lax.rsqrt(jnp.mean(x * x, axis=-1, keepdims=True) + EPS)
    xn = x * r
    dn = dy * g
    dx = r * (dn - xn * jnp.mean(dn * xn, axis=-1, keepdims=True))
    return dx, jnp.sum(dy * xn, axis=0, keepdims=True)


def _sigmoid(x):
    return 1.0 / (1.0 + jnp.exp(-x))


def _rope(x, c, a, b, half):
    return x * c + pltpu.roll(x, LANES - half, 1) * a + pltpu.roll(x, half, 1) * b


def _rope_tables(T, kind):
    lane = np.arange(LANES)
    if kind == "mla":
        half = MLA_ROPE // 2
        rel = lane - MLA_NOPE
        on = (rel >= 0) & (rel < MLA_ROPE)
        d = MLA_ROPE
    else:
        half = SWA_HD // 2
        rel = lane % SWA_HD
        on = np.ones((LANES,), bool)
        d = SWA_HD
    first = on & (rel < half)
    second = on & (rel >= half)
    f = np.where(first, rel, rel - half).astype(np.float32)
    inv = np.exp(np.float32(-math.log(ROPE_THETA)) * f * np.float32(2.0 / d)).astype(np.float32)
    ang = (np.arange(T, dtype=np.float32)[:, None] * inv[None, :]).astype(np.float64)
    cos, sin = np.cos(ang).astype(np.float32), np.sin(ang).astype(np.float32)
    c = np.where(on[None], cos, np.float32(1.0))
    a = np.where(first[None], -sin, np.float32(0.0))
    b = np.where(second[None], sin, np.float32(0.0))
    return c, a, b


def _fwd_in(x, g1, w_in_p, tm):
    T = x.shape[0]

    def body(x_ref, g_ref, w_ref, z_ref, h_ref):
        h = _rms(x_ref[...], g_ref[...]).astype(BF16)
        h_ref[...] = h
        z_ref[...] = _dot(h, w_ref[...])

    return _pcall(body, name="fwd_in", grid=(T // tm,),
                  in_specs=[_rows(tm, D), _full((1, D)), _full((D, NZ))],
                  out_specs=[_rows(tm, NZ), _rows(tm, D)],
                  out_shape=[SDS((T, NZ), F32), SDS((T, D), BF16)], sem=("parallel",))(x, g1, w_in_p)


def _fwd_qkv(z, gq, gkv, wqb, wkn, wv, tab_m, tab_s, tm):
    T = z.shape[0]
    wqb_t, wkn_t, wv_t = wqb.T, wkn.T, wv.T
    tab_mt = [t.T for t in tab_m]

    def body(qa_ref, sq_ref, skd_ref, svd_ref, kva_ref, kr_ref, gq_ref, gkv_ref, wkn_ref, wv_ref, wqbt_ref, wknt_ref, wvt_ref,
             cm_ref, am_ref, bm_ref, cmt_ref, amt_ref, bmt_ref, cs_ref, as_ref, bs_ref,
             qn_ref, kvn_ref, km_ref, vm_ref, qt_ref, kt_ref, vt_ref, qs_ref, ks_ref, vs_ref):
        qn = _rms(qa_ref[...], gq_ref[...])
        qn_ref[...] = qn.astype(BF16)
        kvn = _rms(kva_ref[...], gkv_ref[...])
        kvn_b = kvn.astype(BF16)
        kvn_ref[...] = kvn_b
        qn_t = qn.T.astype(BF16)
        kvn_t = kvn.T.astype(BF16)
        cm, am, bm = cm_ref[...], am_ref[...], bm_ref[...]
        cmt, amt, bmt = cmt_ref[...], amt_ref[...], bmt_ref[...]
        cs, as_, bs = cs_ref[...], as_ref[...], bs_ref[...]
        k_rope = _rope(kr_ref[...], cm, am, bm, MLA_ROPE // 2)
        k_rope_t = k_rope.T
        half = MLA_ROPE // 2
        vm_ref[...] = _dot(kvn_b, wv_ref[...]).astype(BF16)
        km_all = _dot(kvn_b, wkn_ref[...])
        v_t = _dot(wvt_ref[...], kvn_t)
        q_t = _dot(wqbt_ref[...], qn_t)
        k_t = _dot(wknt_ref[...], kvn_t)
        ones_row = jnp.where(lax.broadcasted_iota(jnp.int32, (64, tm), 0) == 0, 1.0, 0.0)
        for h in range(MLA_HEADS):
            sl = slice(LANES * h, LANES * (h + 1))
            vt_ref[0, sl, :] = jnp.concatenate([v_t[64 * h:64 * (h + 1)], ones_row], axis=0).astype(BF16)
            qh = q_t[sl]
            qt_ref[0, sl, :] = (qh * cmt + pltpu.roll(qh, LANES - half, 0) * amt + pltpu.roll(qh, half, 0) * bmt).astype(BF16)
            km_ref[:, sl] = (km_all[:, sl] + k_rope).astype(BF16)
            kt_ref[0, sl, :] = (k_t[sl] + k_rope_t).astype(BF16)
        for j in range(D // LANES):
            sl = slice(LANES * j, LANES * (j + 1))
            qs_ref[:, sl] = _rope(sq_ref[:, sl], cs, as_, bs, SWA_HD // 2).astype(BF16)
        for j in range(2):
            sl = slice(LANES * j, LANES * (j + 1))
            ks_ref[:, sl] = _rope(skd_ref[:, sl], cs, as_, bs, SWA_HD // 2).astype(BF16)
        vs_ref[...] = svd_ref[...].astype(BF16)

    tab = [_rows(tm, LANES)] * 3
    tab_t = [pl.BlockSpec((LANES, tm), lambda i: (0, i))] * 3
    return _pcall(body, name="fwd_qkv", grid=(T // tm,),
                  in_specs=[_rows(tm, 256, 12), _rows(tm, 1024, 0), _rows(tm, 256, 13), _rows(tm, 256, 14),
                            _rows(tm, 128, 30), _rows(tm, 128, 31), _full((1, Q_LORA)), _full((1, KV_LORA)),
                            _full((KV_LORA, 2048)), _full((KV_LORA, 1024)), _full((2048, Q_LORA)), _full((2048, KV_LORA)),
                            _full((1024, KV_LORA))] + tab + tab_t + tab,
                  out_specs=[_rows(tm, Q_LORA), _rows(tm, KV_LORA), _rows(tm, 2048), _rows(tm, 1024),
                             pl.BlockSpec((1, 2048, tm), lambda i: (i, 0, 0)), pl.BlockSpec((1, 2048, tm), lambda i: (i, 0, 0)),
                             pl.BlockSpec((1, 2048, tm), lambda i: (i, 0, 0)),
                             _rows(tm, 1024), _rows(tm, 256), _rows(tm, 256)],
                  out_shape=[SDS((T, Q_LORA), BF16), SDS((T, KV_LORA), BF16), SDS((T, 2048), BF16),
                             SDS((T, 1024), BF16), SDS((T // tm, 2048, tm), BF16), SDS((T // tm, 2048, tm), BF16),
                             SDS((T // tm, 2048, tm), BF16),
                             SDS((T, 1024), BF16), SDS((T, 256), BF16), SDS((T, 256), BF16)],
                  sem=("parallel",))(z, z, z, z, z, z, gq, gkv, wkn, wv, wqb_t, wkn_t, wv_t, *tab_m, *tab_mt, *tab_s)


def _mla_fwd(qt, km, vt, tb):
    T = km.shape[0]
    nb = T // tb
    cc = ATT_COLS

    per = 2 if nb % 2 == 0 else 1

    def body(q_ref, k_ref, vt_ref, o_ref, l_ref, s_ref, p_ref, al_ref, m_ref, acc_ref):
        for blk in range(per):
            one_block(per * pl.program_id(1) + blk, blk, q_ref, k_ref, vt_ref, o_ref, l_ref, s_ref, p_ref, al_ref, m_ref, acc_ref)

    def one_block(i, blk, q_ref, k_ref, vt_ref, o_ref, l_ref, s_ref, p_ref, al_ref, m_ref, acc_ref):
        m_ref[...] = jnp.full(m_ref.shape, NEG, F32)
        acc_ref[...] = jnp.zeros_like(acc_ref)
        p_ref[1] = jnp.zeros(p_ref.shape[1:], BF16)
        al_ref[1] = jnp.ones(al_ref.shape[1:], F32)
        key = lax.broadcasted_iota(jnp.int32, (tb, cc), 0)
        qry = lax.broadcasted_iota(jnp.int32, (tb, cc), 1)

        def scores(j, slot):
            off = pl.multiple_of(j * tb, tb)
            for hh in range(2):
                sl = slice(LANES * hh, LANES * (hh + 1))
                s_ref[slot, hh] = _dot(k_ref[pl.ds(off, tb), sl], q_ref[blk, sl, :])

        def softmax(slot, diagonal):
            chains = [(hh, slice(cc * c, cc * (c + 1)), c) for hh in range(2) for c in range(tb // cc)]

            def scaled(hh, cols, c):
                t = s_ref[slot, hh, :, cols] * MLA_LOG2_SCALE
                return jnp.where(key <= qry + cc * c, t, NEG) if diagonal else t

            tops = []
            for hh, cols, c in chains:
                if diagonal:
                    top = jnp.max(scaled(hh, cols, c), axis=0, keepdims=True)
                else:
                    top = jnp.max(s_ref[slot, hh, :, cols], axis=0, keepdims=True) * MLA_LOG2_SCALE
                m_old = m_ref[hh, :, cols]
                mn = jnp.maximum(m_old, top)
                m_ref[hh, :, cols] = mn
                al_ref[slot, hh, :, cols] = jnp.exp2(m_old - mn)
                tops.append(mn)
            for (hh, cols, c), mn in zip(chains, tops):
                p_ref[slot, hh, :, cols] = jnp.exp2(scaled(hh, cols, c) - mn).astype(BF16)

        def accumulate(j, slot):
            for hh in range(2):
                acc_ref[hh] = al_ref[slot, hh] * acc_ref[hh] + _dot(vt_ref[j, LANES * hh:LANES * hh + VT_ROWS, :], p_ref[slot, hh])

        def step(t, carry):
            scores(2 * t + 1, 1)
            accumulate(jnp.maximum(2 * t - 1, 0), 1)
            softmax(0, False)
            scores(2 * t + 2, 0)
            accumulate(2 * t, 0)
            softmax(1, False)
            return carry

        scores(0, 0)
        lax.fori_loop(0, i // 2, step, 0)

        @pl.when(i % 2 == 1)
        def _():
            scores(i, 1)
            accumulate(jnp.maximum(i - 2, 0), 1)
            softmax(0, False)
            accumulate(i - 1, 0)
            softmax(1, True)
            accumulate(i, 1)

        @pl.when(i % 2 == 0)
        def _():
            accumulate(jnp.maximum(i - 1, 0), 1)
            softmax(0, True)
            accumulate(i, 0)
        den = [acc_ref[hh, 64:65, :] for hh in range(2)]
        o_ref[tb * blk:tb * (blk + 1), :] = jnp.concatenate([acc_ref[hh, 0:64, :] / den[hh] for hh in range(2)], axis=0).T
        sub = lax.broadcasted_iota(jnp.int32, (8, tb), 0)
        lse = [m_ref[hh] + jnp.log(den[hh]) * LOG2_E for hh in range(2)]
        l_ref[0, blk] = jnp.where(sub == 0, lse[0], jnp.where(sub == 1, lse[1], 0.0))

    return _pcall(body, name="mla_fwd", grid=(MLA_HEADS // 2, nb // per),
                  in_specs=[pl.BlockSpec((per, 256, tb), lambda p, i: (i, p, 0)), pl.BlockSpec((T, 256), lambda p, i: (0, p)),
                            pl.BlockSpec((nb, 2 * LANES, tb), lambda p, i: (0, p, 0))],
                  out_specs=[pl.BlockSpec((per * tb, LANES), lambda p, i: (i, p)),
                             pl.BlockSpec((1, per, 8, tb), lambda p, i: (p, i, 0, 0))],
                  out_shape=[SDS((T, D), F32), SDS((MLA_HEADS // 2, nb, 8, tb), F32)],
                  scratch=[pltpu.VMEM((2, 2, tb, tb), F32), pltpu.VMEM((2, 2, tb, tb), BF16), pltpu.VMEM((2, 2, 1, tb), F32),
                           pltpu.VMEM((2, 1, tb), F32), pltpu.VMEM((2, VT_ROWS, tb), F32)],
                  sem=("parallel", "arbitrary"))(qt, km, vt)


def _swa_mask(n):
    row = lax.broadcasted_iota(jnp.int32, (WINDOW, 2 * WINDOW), 0)
    col = lax.broadcasted_iota(jnp.int32, (WINDOW, 2 * WINDOW), 1)
    rel = row - col + WINDOW
    return (rel >= 0) & (rel < WINDOW) & ((col >= WINDOW) | (n > 0))


def _swa_specs(T):
    nb = T // WINDOW
    cur = lambda w: pl.BlockSpec((WINDOW, w), lambda n: (n, 0))
    prev = lambda w: pl.BlockSpec((WINDOW, w), lambda n: (jnp.maximum(n - 1, 0), 0))
    return nb, cur, prev


def _swa_fwd(sinks, qs, ks, vs):
    T = qs.shape[0]
    nb, cur, prev = _swa_specs(T)

    def body(sink_ref, q_ref, kc_ref, kp_ref, vc_ref, vp_ref, o_ref, l_ref, kb_ref, vb_ref, s_ref, p_ref):
        n = pl.program_id(0)
        mask = _swa_mask(n)
        lo = lax.broadcasted_iota(jnp.int32, (WINDOW, LANES), 1) < 64
        hi = jnp.logical_not(lo)
        for g in range(2):
            gs = slice(LANES * g, LANES * (g + 1))
            kb_ref[g] = jnp.concatenate([kp_ref[:, gs], kc_ref[:, gs]], axis=0)
            vb_ref[g] = jnp.concatenate([vp_ref[:, gs], vc_ref[:, gs]], axis=0)
        for h in range(SWA_HEADS):
            qp = q_ref[:, LANES * (h // 2):LANES * (h // 2 + 1)]
            qh = jnp.where(lo if h % 2 == 0 else hi, qp, jnp.zeros_like(qp))
            s_ref[h] = _dot_nt(qh, kb_ref[h // 8])
        for j in range(SWA_HEADS // 2):
            sl = slice(LANES * j, LANES * (j + 1))
            lses = []
            for h in (2 * j, 2 * j + 1):
                s = jnp.where(mask, s_ref[h] * SWA_SCALE, NEG)
                sk = sink_ref[h]
                m = jnp.maximum(jnp.max(s, axis=1, keepdims=True), sk)
                e = jnp.exp(s - m)
                den = jnp.sum(e, axis=1, keepdims=True) + jnp.exp(sk - m)
                p_ref[h] = (e / den).astype(BF16)
                lses.append(jnp.broadcast_to(m + jnp.log(den), (WINDOW, LANES)))
            l_ref[:, sl] = jnp.where(lo, lses[0], lses[1])
        for j in range(SWA_HEADS // 2):
            vb = vb_ref[j // 4]
            o_ref[:, LANES * j:LANES * (j + 1)] = jnp.where(lo, _dot(p_ref[2 * j], vb), _dot(p_ref[2 * j + 1], vb))

    return _pcall(body, name="swa_fwd", grid=(nb,),
                  in_specs=[pl.BlockSpec(memory_space=pltpu.SMEM), cur(D), cur(256), prev(256), cur(256), prev(256)],
                  out_specs=[cur(D), cur(D)], out_shape=[SDS((T, D), F32)] * 2,
                  scratch=[pltpu.VMEM((2, 2 * WINDOW, LANES), BF16), pltpu.VMEM((2, 2 * WINDOW, LANES), BF16),
                           pltpu.VMEM((SWA_HEADS, WINDOW, 2 * WINDOW), F32), pltpu.VMEM((SWA_HEADS, WINDOW, 2 * WINDOW), BF16)],
                  sem=("parallel",))(sinks, qs, ks, ks, vs, vs)


def _fwd_mix(om, os_, z, x, wmu, wsu, wo, g2, tm):
    T = x.shape[0]

    def body(om_ref, os_ref, ga_ref, gb_ref, x_ref, wmu_ref, wsu_ref, wo_ref, g2_ref,
             y_ref, yo_ref, au_ref, bu_ref, x1_ref):
        au = _dot(om_ref[...].astype(BF16), wmu_ref[...])
        bu = _dot(os_ref[...].astype(BF16), wsu_ref[...])
        au_ref[...] = au
        bu_ref[...] = bu
        y = (_sigmoid(ga_ref[...]) * au + _sigmoid(gb_ref[...]) * bu).astype(BF16)
        y_ref[...] = y
        yo = _dot(y, wo_ref[...])
        yo_ref[...] = yo
        x1_ref[...] = x_ref[...] + _rms(yo, g2_ref[...])

    r = _rows(tm, D)
    w = _full((D, D))
    return _pcall(body, name="fwd_mix", grid=(T // tm,),
                  in_specs=[r, r, _rows(tm, D, 1), _rows(tm, D, 2), r, w, w, w, _full((1, D))],
                  out_specs=[r] * 5,
                  out_shape=[SDS((T, D), BF16), SDS((T, D), F32), SDS((T, D), F32), SDS((T, D), F32), SDS((T, D), F32)],
                  sem=("parallel",))(om, os_, z, z, x, wmu, wsu, wo, g2)


def _fwd_mlp_up(x1, g3, w1, tm):
    T = x1.shape[0]

    def body(x_ref, g_ref, w_ref, h_ref, u_ref):
        h = _rms(x_ref[...], g_ref[...]).astype(BF16)
        h_ref[...] = h
        for j in range(N_CHIPS):
            u_ref[:, j * D:(j + 1) * D] = jnp.square(jnp.maximum(_dot(h, w_ref[j]), 0.0)).astype(BF16)

    return _pcall(body, name="fwd_mlp_up", grid=(T // tm,),
                  in_specs=[_rows(tm, D), _full((1, D)), _chunks(w1[1])],
                  out_specs=[_rows(tm, D), _rows(tm, D_FF)],
                  out_shape=[SDS((T, D), BF16), SDS((T, D_FF), BF16)],
                  sem=("parallel",))(x1, g3, w1[0])


def _fwd_mlp_down(u, w2, x1, g4, tm):
    T = x1.shape[0]

    def body(u_ref, w_ref, x_ref, g_ref, d_ref, x2_ref):
        d = _dot(u_ref[...], w_ref[...].reshape(D_FF, D))
        d_ref[...] = d
        x2_ref[...] = x_ref[...] + _rms(d, g_ref[...])

    return _pcall(body, name="fwd_mlp_down", grid=(T // tm,),
                  in_specs=[_rows(tm, D_FF), _chunks(w2[1]), _rows(tm, D), _full((1, D))],
                  out_specs=[_rows(tm, D), _rows(tm, D)], out_shape=[SDS((T, D), F32)] * 2,
                  sem=("parallel",))(u, w2[0], x1, g4)


def _ple_fwd_bwd(p, x2, tgt, wple, g5, wpg, tm):
    T = x2.shape[0]

    def body(p_ref, x2_ref, t_ref, wple_ref, g5_ref, wpg_ref, loss_ref, dx2_ref, dgt_ref, de0_ref, dg5_ref):
        @pl.when(pl.program_id(0) == 0)
        def _():
            loss_ref[...] = jnp.zeros_like(loss_ref)
            dg5_ref[...] = jnp.zeros_like(dg5_ref)

        e0 = _dot(p_ref[...].astype(BF16), wple_ref[...])
        g5 = g5_ref[...]
        r = lax.rsqrt(jnp.mean(e0 * e0, axis=-1, keepdims=True) + EPS)
        en = e0 * r
        e = en * g5
        x2 = x2_ref[...]
        s = _sigmoid(_dot(x2.astype(BF16), wpg_ref[...]))
        diff = x2 + s * e - t_ref[...]
        sq = jnp.sum(jnp.sum(diff * diff, axis=1, keepdims=True), axis=0, keepdims=True)
        loss_ref[...] += jnp.broadcast_to(sq * (0.5 / D), loss_ref.shape)
        dx3 = diff * (1.0 / D)
        de = dx3 * s
        dgt = (dx3 * e * s * (1.0 - s)).astype(BF16)
        dgt_ref[...] = dgt
        dn = de * g5
        de0_ref[...] = (r * (dn - en * jnp.mean(dn * en, axis=-1, keepdims=True))).astype(BF16)
        dg5_ref[...] += jnp.sum(de * en, axis=0, keepdims=True)
        dx2_ref[...] = dx3 + _dot_nt(dgt, wpg_ref[...])

    r = _rows(tm, D)
    return _pcall(body, name="ple_fwd_bwd", grid=(T // tm,),
                  in_specs=[_rows(tm, PLE), r, r, _full((PLE, D)), _full((1, D)), _full((D, D))],
                  out_specs=[_full((8, LANES)), r, r, r, _full((1, D))],
                  out_shape=[SDS((8, LANES), F32), SDS((T, D), F32), SDS((T, D), BF16), SDS((T, D), BF16), SDS((1, D), F32)],
                  sem=("arbitrary",))(p, x2, tgt, wple, g5, wpg)


def _bwd_mlp_down(dx2, d, g4, w2, u, tm):
    T = dx2.shape[0]

    def body(dx_ref, d_ref, g_ref, w_ref, u_ref, dd_ref, da_ref, dg_ref):
        @pl.when(pl.program_id(0) == 0)
        def _():
            dg_ref[...] = jnp.zeros_like(dg_ref)

        dd, dg = _rms_bwd(dx_ref[...], d_ref[...], g_ref[...])
        dg_ref[...] += dg
        ddb = dd.astype(BF16)
        dd_ref[...] = ddb
        du = _dot_nt(ddb, w_ref[...].reshape(D_FF, D))
        da_ref[...] = (du * (2.0 * jnp.sqrt(u_ref[...].astype(F32)))).astype(BF16)

    return _pcall(body, name="bwd_mlp_down", grid=(T // tm,),
                  in_specs=[_rows(tm, D), _rows(tm, D), _full((1, D)), _chunks(w2[1]), _rows(tm, D_FF)],
                  out_specs=[_rows(tm, D), _rows(tm, D_FF), _full((1, D))],
                  out_shape=[SDS((T, D), BF16), SDS((T, D_FF), BF16), SDS((1, D), F32)],
                  sem=("arbitrary",))(dx2, d, g4, w2[0], u)


def _bwd_mlp_up(da, w1, x1, g3, dx2, tm):
    T = dx2.shape[0]

    def body(da_ref, w_ref, x_ref, g_ref, dx2_ref, dx1_ref, dg_ref):
        @pl.when(pl.program_id(0) == 0)
        def _():
            dg_ref[...] = jnp.zeros_like(dg_ref)

        dh = _dot_nt(da_ref[:, 0:D], w_ref[0])
        for j in range(1, N_CHIPS):
            dh += _dot_nt(da_ref[:, j * D:(j + 1) * D], w_ref[j])
        dx, dg = _rms_bwd(dh, x_ref[...], g_ref[...])
        dg_ref[...] += dg
        dx1_ref[...] = dx2_ref[...] + dx

    return _pcall(body, name="bwd_mlp_up", grid=(T // tm,),
                  in_specs=[_rows(tm, D_FF), _chunks(w1[1]), _rows(tm, D), _full((1, D)), _rows(tm, D)],
                  out_specs=[_rows(tm, D), _full((1, D))],
                  out_shape=[SDS((T, D), F32), SDS((1, D), F32)], sem=("arbitrary",))(da, w1[0], x1, g3, dx2)


def _bwd_mix(dx1, yo, g2, wo, z, au, bu, wmu, wsu, om, tm):
    T = dx1.shape[0]

    def body(dx_ref, yo_ref, g_ref, wo_ref, ga_ref, gb_ref, au_ref, bu_ref, wmu_ref, wsu_ref, om_ref,
             dyo_ref, dg_ref, dau_ref, dbu_ref, dga_ref, dgb_ref, dos_ref, dl_ref, dot_ref):
        @pl.when(pl.program_id(0) == 0)
        def _():
            dg_ref[...] = jnp.zeros_like(dg_ref)

        dyo, dg = _rms_bwd(dx_ref[...], yo_ref[...], g_ref[...])
        dg_ref[...] += dg
        dyob = dyo.astype(BF16)
        dyo_ref[...] = dyob
        dy = _dot_nt(dyob, wo_ref[...])
        sa = _sigmoid(ga_ref[...])
        sb = _sigmoid(gb_ref[...])
        dau = (dy * sa).astype(BF16)
        dbu = (dy * sb).astype(BF16)
        dau_ref[...] = dau
        dbu_ref[...] = dbu
        dga_ref[...] = (dy * au_ref[...] * sa * (1.0 - sa)).astype(BF16)
        dgb_ref[...] = (dy * bu_ref[...] * sb * (1.0 - sb)).astype(BF16)
        dom = _dot_nt(dau, wmu_ref[...])
        dos_ref[...] = _dot_nt(dbu, wsu_ref[...])
        prod = dom * om_ref[...]
        sub = lax.broadcasted_iota(jnp.int32, (8, tm), 0)
        for pr in range(MLA_HEADS // 2):
            sl = slice(LANES * pr, LANES * (pr + 1))
            pt = prod[:, sl].T
            d0 = jnp.sum(pt[0:64], axis=0, keepdims=True)
            d1 = jnp.sum(pt[64:128], axis=0, keepdims=True)
            dl_ref[pr, 0] = jnp.where(sub == 0, d0, jnp.where(sub == 1, d1, 0.0))
            dot_ref[0, sl, :] = dom[:, sl].T.astype(BF16)

    r = _rows(tm, D)
    w = _full((D, D))
    return _pcall(body, name="bwd_mix", grid=(T // tm,),
                  in_specs=[r, r, _full((1, D)), w, _rows(tm, D, 1), _rows(tm, D, 2), r, r, w, w, r],
                  out_specs=[r, _full((1, D)), r, r, r, r, r, pl.BlockSpec((MLA_HEADS // 2, 1, 8, tm), lambda i: (0, i, 0, 0)),
                             pl.BlockSpec((1, D, tm), lambda i: (i, 0, 0))],
                  out_shape=[SDS((T, D), BF16), SDS((1, D), F32), SDS((T, D), BF16), SDS((T, D), BF16), SDS((T, D), BF16),
                             SDS((T, D), BF16), SDS((T, D), F32), SDS((MLA_HEADS // 2, T // tm, 8, tm), F32),
                             SDS((T // tm, D, tm), BF16)],
                  sem=("arbitrary",))(dx1, yo, g2, wo, z, z, au, bu, wmu, wsu, om)


def _mla_bwd(qt, km, kt, vm, dot, lse, delta, tb):
    T = km.shape[0]
    nb = T // tb
    cc = ATT_COLS

    per = 2 if nb % 2 == 0 else 1

    def body(qt_ref, k_ref, kt_ref, v_ref, dot_ref, l_ref, dl_ref, dqt_ref, dkt_ref, dvt_ref,
             s_ref, dp_ref, p_ref, ds_ref, vh_ref):
        @pl.when(pl.program_id(1) == 0)
        def _():
            dqt_ref[...] = jnp.zeros_like(dqt_ref)

        dkt_ref[...] = jnp.zeros_like(dkt_ref)
        dvt_ref[...] = jnp.zeros_like(dvt_ref)
        for blk in range(per):
            one_block(per * pl.program_id(1) + blk, blk, qt_ref, k_ref, kt_ref, v_ref, dot_ref, l_ref, dl_ref, dqt_ref, dkt_ref,
                      dvt_ref, s_ref, dp_ref, p_ref, ds_ref, vh_ref)

    def one_block(j, blk, qt_ref, k_ref, kt_ref, v_ref, dot_ref, l_ref, dl_ref, dqt_ref, dkt_ref, dvt_ref,
                  s_ref, dp_ref, p_ref, ds_ref, vh_ref):
        lo = lax.broadcasted_iota(jnp.int32, (tb, LANES), 1) < 64
        key = lax.broadcasted_iota(jnp.int32, (tb, cc), 0)
        qry = lax.broadcasted_iota(jnp.int32, (tb, cc), 1)
        rows_j = slice(tb * blk, tb * (blk + 1))
        v = v_ref[rows_j, :]
        vh_ref[0] = jnp.where(lo, v, jnp.zeros_like(v))
        vh_ref[1] = jnp.where(lo, jnp.zeros_like(v), v)

        def scores(i, slot):
            for hh in range(2):
                sl = slice(LANES * hh, LANES * (hh + 1))
                s_ref[slot, hh] = _dot(k_ref[rows_j, sl], qt_ref[i, sl, :])
                dp_ref[slot, hh] = _dot(vh_ref[hh], dot_ref[i])

        def grads(i, slot, diagonal):
            lse_i = l_ref[0, i]
            delta_i = dl_ref[0, i]
            for hh in range(2):
                for c in range(tb // cc):
                    cols = slice(cc * c, cc * (c + 1))
                    p = jnp.exp2(s_ref[slot, hh, :, cols] * MLA_LOG2_SCALE - lse_i[hh:hh + 1, cols])
                    if diagonal:
                        p = jnp.where(key <= qry + cc * c, p, 0.0)
                    p_ref[hh, :, cols] = p.astype(BF16)
                    ds_ref[hh, :, cols] = (p * (dp_ref[slot, hh, :, cols] - delta_i[hh:hh + 1, cols]) * MLA_SCALE).astype(BF16)
            for hh in range(2):
                sl = slice(LANES * hh, LANES * (hh + 1))
                half = slice(64 * hh, 64 * (hh + 1))
                dvt_ref[blk, half, :] += _dot_nt(dot_ref[i, half, :], p_ref[hh])
                real = slice(LANES * hh, LANES * hh + MLA_NOPE + MLA_ROPE)
                dkt_ref[blk, real, :] += _dot_nt(qt_ref[i, real, :], ds_ref[hh])
                dqt_ref[i, real, :] += _dot(kt_ref[blk, real, :], ds_ref[hh])

        n_off = nb - 1 - j

        def step(u, carry):
            i0 = j + 1 + 2 * u
            scores(i0 + 1, 1)
            grads(i0, 0, False)
            scores(jnp.where(i0 + 2 < nb, i0 + 2, j), 0)
            grads(i0 + 1, 1, False)
            return carry

        scores(jnp.where(n_off > 0, j + 1, j), 0)
        lax.fori_loop(0, n_off // 2, step, 0)

        @pl.when(n_off % 2 == 1)
        def _():
            scores(j, 1)
            grads(nb - 1, 0, False)
            grads(j, 1, True)

        @pl.when(n_off % 2 == 0)
        def _():
            grads(j, 0, True)

    blk = lambda w: pl.BlockSpec((per * tb, w), lambda p, j: (j, p))
    stat = pl.BlockSpec((1, nb, 8, tb), lambda p, j: (p, 0, 0, 0))
    pair_t = lambda w: pl.BlockSpec((nb, w, tb), lambda p, j: (0, p, 0))
    blk_t = lambda w: pl.BlockSpec((per, w, tb), lambda p, j: (j, p, 0))
    return _pcall(body, name="mla_bwd", grid=(MLA_HEADS // 2, nb // per),
                  in_specs=[pair_t(256), blk(256), blk_t(256), blk(LANES), pair_t(LANES), stat, stat],
                  out_specs=[pair_t(256), blk_t(256), blk_t(LANES)],
                  out_shape=[SDS((nb, 2048, tb), F32), SDS((nb, 2048, tb), F32), SDS((nb, D, tb), F32)],
                  scratch=[pltpu.VMEM((2, 2, tb, tb), F32), pltpu.VMEM((2, 2, tb, tb), F32), pltpu.VMEM((2, tb, tb), BF16),
                           pltpu.VMEM((2, tb, tb), BF16), pltpu.VMEM((2, tb, LANES), BF16)],
                  sem=("parallel", "arbitrary"))(qt, km, kt, vm, dot, lse, delta)


def _swa_bwd(sinks, qs, ks, vs, do, o, lse):
    T = qs.shape[0]
    nb, cur, prev = _swa_specs(T)

    def body(sink_ref, q_ref, kc_ref, kp_ref, vc_ref, vp_ref, do_ref, o_ref, l_ref,
             dq_ref, dkc_ref, dkp_ref, dvc_ref, dvp_ref, dsink_ref, kb_ref, vb_ref, s_ref, dp_ref, p_ref, ds_ref):
        n = pl.program_id(0)

        @pl.when(n == 0)
        def _():
            dsink_ref[...] = jnp.zeros_like(dsink_ref)

        mask = _swa_mask(n)
        lo = lax.broadcasted_iota(jnp.int32, (WINDOW, LANES), 1) < 64
        hi = jnp.logical_not(lo)
        lane8 = lax.broadcasted_iota(jnp.int32, (8, LANES), 1)
        for g in range(2):
            gs = slice(LANES * g, LANES * (g + 1))
            kb_ref[g] = jnp.concatenate([kp_ref[:, gs], kc_ref[:, gs]], axis=0)
            vb_ref[g] = jnp.concatenate([vp_ref[:, gs], vc_ref[:, gs]], axis=0)

        def head(h):
            sl = slice(LANES * (h // 2), LANES * (h // 2 + 1))
            hm = lo if h % 2 == 0 else hi
            qp = q_ref[:, sl]
            return hm, sl, jnp.where(hm, qp, jnp.zeros_like(qp)), jnp.where(hm, do_ref[:, sl], 0.0).astype(BF16)

        for h in range(SWA_HEADS):
            _, _, qh, dom = head(h)
            s_ref[h] = _dot_nt(qh, kb_ref[h // 8])
            dp_ref[h] = _dot_nt(dom, vb_ref[h // 8])
        dsink = jnp.zeros((8, LANES), F32)
        for h in range(SWA_HEADS):
            hm, sl, _, _ = head(h)
            lse_h = jnp.max(jnp.where(hm, l_ref[:, sl], -jnp.inf), axis=1, keepdims=True)
            delta = jnp.sum(jnp.where(hm, do_ref[:, sl] * o_ref[:, sl], 0.0), axis=1, keepdims=True)
            p = jnp.exp(jnp.where(mask, s_ref[h] * SWA_SCALE, NEG) - lse_h)
            p_ref[h] = p.astype(BF16)
            ds_ref[h] = (p * (dp_ref[h] - delta) * SWA_SCALE).astype(BF16)
            d_sink = -jnp.sum(jnp.exp(sink_ref[h] - lse_h) * delta, axis=0, keepdims=True)
            dsink = dsink + jnp.where(lane8 == h, d_sink, 0.0)
        dsink_ref[...] += dsink
        for g in range(2):
            gs = slice(LANES * g, LANES * (g + 1))
            dkb = jnp.zeros((2 * WINDOW, LANES), F32)
            dvb = jnp.zeros((2 * WINDOW, LANES), F32)
            for j in range(4 * g, 4 * g + 4):
                dqs = []
                for h in (2 * j, 2 * j + 1):
                    _, _, qh, dom = head(h)
                    dvb = dvb + _dot_tn(p_ref[h], dom)
                    dkb = dkb + _dot_tn(ds_ref[h], qh)
                    dqs.append(_dot(ds_ref[h], kb_ref[g]))
                dq_ref[:, LANES * j:LANES * (j + 1)] = jnp.where(lo, dqs[0], dqs[1])
            dkp_ref[:, gs] = dkb[:WINDOW]
            dkc_ref[:, gs] = dkb[WINDOW:]
            dvp_ref[:, gs] = dvb[:WINDOW]
            dvc_ref[:, gs] = dvb[WINDOW:]

    band = pltpu.VMEM((2, 2 * WINDOW, LANES), BF16)
    return _pcall(body, name="swa_bwd", grid=(nb,),
                  in_specs=[pl.BlockSpec(memory_space=pltpu.SMEM), cur(D), cur(256), prev(256), cur(256), prev(256),
                            cur(D), cur(D), cur(D)],
                  out_specs=[cur(D), cur(256), cur(256), cur(256), cur(256), _full((8, LANES))],
                  out_shape=[SDS((T, D), F32), SDS((T, 256), F32), SDS((T, 256), F32), SDS((T, 256), F32), SDS((T, 256), F32),
                             SDS((8, LANES), F32)],
                  scratch=[band, band, pltpu.VMEM((SWA_HEADS, WINDOW, 2 * WINDOW), F32),
                           pltpu.VMEM((SWA_HEADS, WINDOW, 2 * WINDOW), F32), pltpu.VMEM((SWA_HEADS, WINDOW, 2 * WINDOW), BF16),
                           pltpu.VMEM((SWA_HEADS, WINDOW, 2 * WINDOW), BF16)],
                  sem=("arbitrary",))(sinks, qs, ks, ks, vs, vs, do, o, lse)


def _bwd_qkv(dqm, dkm, dvm, dqs, dkc, dkp, dvc, dvp, z, gq, gkv, wqb, wkn, wv, tab_m, tab_s):
    T = z.shape[0]
    tm = WINDOW
    nb = T // tm
    per = dqm.shape[2] // tm

    tab_mt = [t.T for t in tab_m]
    half = MLA_ROPE // 2

    def rope_t(v, c, a, b):
        return v * c + pltpu.roll(v, LANES - half, 0) * a + pltpu.roll(v, half, 0) * b

    def rms_bwd_t(dy, x, g):
        r = lax.rsqrt(jnp.mean(x * x, axis=0, keepdims=True) + EPS)
        xn = x * r
        dn = dy * g
        return r * (dn - xn * jnp.mean(dn * xn, axis=0, keepdims=True)), jnp.sum(dy * xn, axis=1, keepdims=True)

    def body(dqm_ref, dkm_ref, dvm_ref, dqs_ref, dkc_ref, dkp_ref, dvc_ref, dvp_ref, qa_ref, kva_ref, gq_ref, gkv_ref,
             wqb_ref, wkn_ref, wv_ref, cmt_ref, amt_ref, bmt_ref, cs_ref, as_ref, bs_ref,
             dq_out, dkn_out, dv_out, dsq_ref, drest_ref, dgq_ref, dgkv_ref):
        i = pl.program_id(0)

        @pl.when(i == 0)
        def _():
            dgq_ref[...] = jnp.zeros_like(dgq_ref)
            dgkv_ref[...] = jnp.zeros_like(dgkv_ref)

        cmt, amt, bmt = cmt_ref[...], -amt_ref[...], -bmt_ref[...]
        cs, as_, bs = cs_ref[...], -as_ref[...], -bs_ref[...]
        row = lax.broadcasted_iota(jnp.int32, (LANES, tm), 0)
        nope = row < MLA_NOPE
        roped = jnp.logical_and(row >= MLA_NOPE, row < MLA_NOPE + MLA_ROPE)
        dkr = jnp.zeros((LANES, tm), F32)
        for h in range(MLA_HEADS):
            sl = slice(LANES * h, LANES * (h + 1))
            dq_out[0, sl, :] = rope_t(dqm_ref[0, sl, :], cmt, amt, bmt).astype(BF16)
            dk_h = dkm_ref[0, sl, :]
            dkn_out[0, sl, :] = jnp.where(nope, dk_h, 0.0).astype(BF16)
            dkr = dkr + jnp.where(roped, dk_h, 0.0)
        dv_out[0] = dvm_ref[0].astype(BF16)
        dqn = _dot(wqb_ref[...], dq_out[0])
        dkvn = _dot(wkn_ref[...], dkn_out[0]) + _dot(wv_ref[...], dv_out[0])
        dqa, dgq = rms_bwd_t(dqn, qa_ref[...].T, gq_ref[...])
        dkva, dgkv = rms_bwd_t(dkvn, kva_ref[...].T, gkv_ref[...])
        dgq_ref[...] += dgq
        dgkv_ref[...] += dgkv
        for j in range(D // LANES):
            sl = slice(LANES * j, LANES * (j + 1))
            dsq_ref[:, sl] = _rope(dqs_ref[:, sl], cs, as_, bs, SWA_HD // 2).astype(BF16)
        keep = (i < nb - 1).astype(F32)
        drest_ref[:, 0:256] = dqa.T.astype(BF16)
        for j in range(2):
            sl = slice(LANES * j, LANES * (j + 1))
            dk = dkc_ref[:, sl] + keep * dkp_ref[:, sl]
            drest_ref[:, 256 + LANES * j:256 + LANES * (j + 1)] = _rope(dk, cs, as_, bs, SWA_HD // 2).astype(BF16)
        drest_ref[:, 512:768] = (dvc_ref[...] + keep * dvp_ref[...]).astype(BF16)
        drest_ref[:, 768:896] = dkva.T.astype(BF16)
        drest_ref[:, 896:1024] = rope_t(dkr, cmt, amt, bmt).T.astype(BF16)

    nxt = pl.BlockSpec((tm, 256), lambda i: (jnp.minimum(i + 1, nb - 1), 0))
    tab = [_rows(tm, LANES)] * 3
    tab_t = [pl.BlockSpec((LANES, tm), lambda i: (0, i))] * 3
    blk_t = lambda w: pl.BlockSpec((1, w, tm), lambda i: (i // per, 0, i % per))
    return _pcall(body, name="bwd_qkv", grid=(nb,),
                  in_specs=[blk_t(2048), blk_t(2048), blk_t(1024), _rows(tm, 1024), _rows(tm, 256), nxt,
                            _rows(tm, 256), nxt, _rows(tm, 256, 12), _rows(tm, 128, 30), _full((Q_LORA, 1)), _full((KV_LORA, 1)),
                            _full((Q_LORA, 2048)), _full((KV_LORA, 2048)), _full((KV_LORA, 1024))] + tab_t + tab,
                  out_specs=[blk_t(2048), blk_t(2048), blk_t(1024), _rows(tm, 1024), _rows(tm, 1024),
                             _full((Q_LORA, 1)), _full((KV_LORA, 1))],
                  out_shape=[SDS(dqm.shape, BF16), SDS(dkm.shape, BF16), SDS(dvm.shape, BF16), SDS((T, 1024), BF16),
                             SDS((T, 1024), BF16), SDS((Q_LORA, 1), F32), SDS((KV_LORA, 1), F32)],
                  sem=("arbitrary",))(dqm, dkm, dvm, dqs, dkc, dkp, dvc, dvp, z, z, gq.reshape(Q_LORA, 1),
                                      gkv.reshape(KV_LORA, 1), wqb, wkn, wv, *tab_mt, *tab_s)


def _bwd_in(dsq, dga, dgb, drest, w_in_p, x, g1, dx1, tm):
    T = x.shape[0]

    def body(a_ref, b_ref, c_ref, d_ref, w_ref, x_ref, g_ref, dx1_ref, dx_ref, dg_ref):
        @pl.when(pl.program_id(0) == 0)
        def _():
            dg_ref[...] = jnp.zeros_like(dg_ref)

        dh = (_dot_nt(a_ref[...], w_ref[:, 0:1024]) + _dot_nt(b_ref[...], w_ref[:, 1024:2048])
              + _dot_nt(c_ref[...], w_ref[:, 2048:3072]) + _dot_nt(d_ref[...], w_ref[:, 3072:4096]))
        dx, dg = _rms_bwd(dh, x_ref[...], g_ref[...])
        dg_ref[...] += dg
        dx_ref[...] = dx1_ref[...] + dx

    r = _rows(tm, D)
    return _pcall(body, name="bwd_in", grid=(T // tm,),
                  in_specs=[r, r, r, r, _full((D, NZ)), r, _full((1, D)), r],
                  out_specs=[r, _full((1, D))], out_shape=[SDS((T, D), F32), SDS((1, D), F32)],
                  sem=("arbitrary",))(dsq, dga, dgb, drest, w_in_p, x, g1, dx1)


def _wgrad(a, g, name, into=None):
    T, K = a.shape
    N = g.shape[1]
    tk, tn, tt = min(K, 1024), min(N, 1024), min(T, 1024)
    if into is not None:
        buf, weight = into
        _, row0, lane0 = PACK_AT[weight]
        shard = {n: (r, c) for n, r, c in BIG}[weight]
        assert lane0 == 0 and shard[1] == D and tk % shard[0] == 0
        per_step = tk // shard[0]
    assert K % tk == 0 and N % tn == 0 and T % tt == 0, (a.shape, g.shape)
    steps = T // tt

    def body(a_ref, g_ref, *rest):
        o_ref, acc_ref = rest[-2:]
        t = pl.program_id(2)

        @pl.when(t == 0)
        def _():
            acc_ref[...] = jnp.zeros_like(acc_ref)

        acc_ref[...] += _dot_tn(a_ref[...].astype(BF16), g_ref[...].astype(BF16))

        @pl.when(t == steps - 1)
        def _():
            o_ref[...] = acc_ref[...].astype(o_ref.dtype).reshape(o_ref.shape)

    in_specs = [pl.BlockSpec((tt, tk), lambda k, n, t: (t, k)), pl.BlockSpec((tt, tn), lambda k, n, t: (t, n))]
    if into is None:
        return _pcall(body, name=name, grid=(K // tk, N // tn, steps), in_specs=in_specs,
                      out_specs=pl.BlockSpec((tk, tn), lambda k, n, t: (k, n)), out_shape=SDS((K, N), F32),
                      scratch=[pltpu.VMEM((tk, tn), F32)], sem=("parallel", "parallel", "arbitrary"))(a, g)
    assert row0 % shard[0] == 0 and (K // tk) * (N // tn) * per_step == N_CHIPS
    return _pcall(body, name=name, grid=(K // tk, N // tn, steps), in_specs=in_specs + [ANY],
                  out_specs=pl.BlockSpec((per_step, shard[0], tn), lambda k, n, t: (k + n, row0 // shard[0], 0)),
                  out_shape=SDS(buf.shape, buf.dtype),
                  scratch=[pltpu.VMEM((tk, tn), F32)], sem=("parallel", "parallel", "arbitrary"), aliases={2: 0})(a, g, buf)


def _wgrad_t(at, g, name):
    nblk, K, tt = at.shape
    N = g.shape[1]
    tk = min(K, 1024)
    per_step = 4 if nblk % 4 == 0 else 1
    assert K % tk == 0 and g.shape[0] == nblk * tt

    def body(a_ref, g_ref, o_ref):
        @pl.when(pl.program_id(1) == 0)
        def _():
            o_ref[...] = jnp.zeros_like(o_ref)

        acc = _dot(a_ref[0], g_ref[0:tt, :].astype(BF16))
        for b in range(1, per_step):
            acc = acc + _dot(a_ref[b], g_ref[tt * b:tt * (b + 1), :].astype(BF16))
        o_ref[...] += acc

    return _pcall(body, name=name, grid=(K // tk, nblk // per_step),
                  in_specs=[pl.BlockSpec((per_step, tk, tt), lambda k, t: (t, k, 0)),
                            pl.BlockSpec((per_step * tt, N), lambda k, t: (t, 0))],
                  out_specs=pl.BlockSpec((tk, N), lambda k, t: (k, 0)), out_shape=SDS((K, N), F32),
                  sem=("parallel", "arbitrary"))(at, g)


def _adamw(w, packed_g, m, v, name, transposed=False):
    R, C = w.shape[1:][::-1] if transposed else w.shape[1:]
    _, row0, lane0 = PACK_AT[name]
    tr = min(R, 256 if row0 % 256 == 0 else 128)
    assert row0 % tr == 0 and R % tr == 0

    def body(w_ref, g_ref, m_ref, v_ref, go_ref, d_ref, m2_ref, v2_ref):
        g_ = g_ref[...].T[lane0:lane0 + C] if transposed else g_ref[:, lane0:lane0 + C]
        go_ref[0] = g_
        m2 = ADAM_B1 * m_ref[0] + (1.0 - ADAM_B1) * g_
        v2 = ADAM_B2 * v_ref[0] + (1.0 - ADAM_B2) * jnp.square(g_)
        m_hat = m2 / (1.0 - ADAM_B1 ** ADAM_STEP)
        v_hat = v2 / (1.0 - ADAM_B2 ** ADAM_STEP)
        d_ref[0] = -ADAM_LR * (m_hat / (jnp.sqrt(v_hat) + ADAM_EPS) + ADAM_WD * w_ref[0])
        m2_ref[0] = m2
        v2_ref[0] = v2

    r = pl.BlockSpec((1, C, tr), lambda i: (0, 0, i)) if transposed else pl.BlockSpec((1, tr, C), lambda i: (0, i, 0))
    return _pcall(body, name="adamw_" + name, grid=(R // tr,),
                  in_specs=[r, pl.BlockSpec((tr, D), lambda i: (row0 // tr + i, 0)), r, r], out_specs=[r] * 4,
                  out_shape=[SDS(w.shape, F32)] * 4, sem=("parallel",))(w, packed_g, m, v)


def _adamw_small(w, parts, m, v):
    def body(w_ref, p_ref, m_ref, v_ref, g_ref, d_ref, m2_ref, v2_ref):
        g_ = p_ref[0]
        for k in range(1, N_DEV):
            g_ = g_ + p_ref[k]
        g_ref[...] = g_
        m2 = ADAM_B1 * m_ref[...] + (1.0 - ADAM_B1) * g_
        v2 = ADAM_B2 * v_ref[...] + (1.0 - ADAM_B2) * jnp.square(g_)
        m_hat = m2 / (1.0 - ADAM_B1 ** ADAM_STEP)
        v_hat = v2 / (1.0 - ADAM_B2 ** ADAM_STEP)
        d_ref[...] = -ADAM_LR * (m_hat / (jnp.sqrt(v_hat) + ADAM_EPS) + ADAM_WD * w_ref[...])
        m2_ref[...] = m2
        v2_ref[...] = v2

    s = _full((8, D))
    return _pcall(body, name="adamw_small", grid=(1,), in_specs=[s, _full((N_DEV, 8, D)), s, s], out_specs=[s] * 4,
                  out_shape=[SDS((8, D), F32)] * 4, sem=("arbitrary",))(w, parts, m, v)


ANY = pl.BlockSpec(memory_space=pl.ANY)


def _place():
    x, y, c = lax.axis_index("x"), lax.axis_index("y"), lax.axis_index("c")
    chips = [(1 - x, y), (x, 1 - y), (1 - x, 1 - y)]
    return x, y, c, chips


def _all_gather(wpk):
    rows = wpk.shape[0]
    HALF = rows // 2
    assert HALF % 16 == 0

    def body(in_ref, out_ref, send_sems, recv_sems):
        x, y, c, chips = _place()
        half = pl.ds(pl.multiple_of(c * HALF, 16), HALF)
        other = pl.ds(pl.multiple_of((1 - c) * HALF, 16), HALF)

        def copy(k, src, dst, to):
            return pltpu.make_async_remote_copy(src_ref=src, dst_ref=dst, send_sem=send_sems.at[k], recv_sem=recv_sems.at[k],
                                                device_id=to, device_id_type=MESH)

        first = [copy(k, in_ref.at[half], out_ref.at[2 * x + y, half], (cx, cy, c)) for k, (cx, cy) in enumerate(chips)]
        first.append(copy(6, in_ref, out_ref.at[2 * x + y], (x, y, 1 - c)))
        for cp in first:
            cp.start()
        passed = []
        for k, (cx, cy) in enumerate(chips):
            slot = out_ref.at[2 * cx + cy, half]
            copy(k, slot, slot, (x, y, c)).wait_recv()
            fwd = copy(3 + k, slot, slot, (x, y, 1 - c))
            fwd.start()
            passed.append(fwd)
        for k, (cx, cy) in enumerate(chips):
            slot = out_ref.at[2 * cx + cy, other]
            copy(3 + k, slot, slot, (x, y, c)).wait_recv()
        copy(6, in_ref, out_ref.at[2 * x + y], (x, y, c)).wait_recv()
        for cp in first + passed:
            cp.wait_send()

    return _pcall(body, name="all_gather_weights", in_specs=[ANY], out_specs=ANY,
                  out_shape=SDS((N_CHIPS, rows, D), BF16),
                  scratch=[pltpu.SemaphoreType.DMA((7,)), pltpu.SemaphoreType.DMA((7,))])(wpk)


HBM = pl.BlockSpec(memory_space=pltpu.HBM)
SEM = pl.BlockSpec(memory_space=pltpu.SEMAPHORE)
DATAFLOW = pltpu.SideEffectType.DATAFLOW_SIDE_EFFECTING


def _in_hbm(a):
    return pltpu.with_memory_space_constraint(a, pltpu.HBM)


def _gather_late_start(wpk, after):
    rows = wpk.shape[0]

    def body(in_ref, land_ref, after_ref, send_sems, recv_sems, in_thru, land_thru, token):
        x, y, c, chips = _place()
        for k, to in enumerate([(cx, cy, c) for cx, cy in chips] + [(x, y, 1 - c)]):
            pltpu.make_async_remote_copy(src_ref=in_ref, dst_ref=land_ref.at[2 * x + y], send_sem=send_sems.at[k],
                                         recv_sem=recv_sems.at[k], device_id=to, device_id_type=MESH).start()
        token[...] = jnp.zeros_like(token)

    return pl.pallas_call(
        body, name="gather_late_start",
        out_shape=(pltpu.SemaphoreType.DMA((4,)), pltpu.SemaphoreType.DMA((4,)), pltpu.HBM(wpk.shape, wpk.dtype),
                   pltpu.HBM((N_CHIPS, rows, D), wpk.dtype), SDS((8, LANES), F32)),
        in_specs=(HBM, HBM, ANY), out_specs=(SEM, SEM, HBM, HBM, pl.BlockSpec(memory_space=pltpu.VMEM)),
        input_output_aliases={0: 2, 1: 3}, compiler_params=pltpu.CompilerParams(has_side_effects=DATAFLOW),
    )(_in_hbm(wpk), _in_hbm(lax.empty((N_CHIPS, rows, D), wpk.dtype)), after)


def _gather_late_wait(send_sems, recv_sems, in_thru, land_thru, after):
    def body(in_ref, land_ref, send_sems, recv_sems, after_ref, after2_ref, in_dead, got_ref):
        x, y, c, chips = _place()
        for k, (sx, sy) in enumerate(chips + [(x, y)]):
            cp = pltpu.make_async_remote_copy(src_ref=in_ref, dst_ref=land_ref.at[2 * sx + sy], send_sem=send_sems.at[k],
                                              recv_sem=recv_sems.at[k], device_id=(x, y, c), device_id_type=MESH)
            cp.wait_send()
            cp.wait_recv()

    return pl.pallas_call(
        body, name="gather_late_wait",
        out_shape=(pltpu.HBM(in_thru.shape, in_thru.dtype), pltpu.HBM(land_thru.shape, land_thru.dtype)),
        in_specs=(HBM, HBM, SEM, SEM, ANY, ANY), out_specs=(HBM, HBM), input_output_aliases={0: 0, 1: 1},
        compiler_params=pltpu.CompilerParams(has_side_effects=DATAFLOW),
    )(in_thru, land_thru, send_sems, recv_sems, *after)[1]


def _rs_sibling(gpk):
    HALF = gpk.shape[1] // 2

    def body(in_ref, out_ref, send_sem, recv_sem):
        x, y, c, _ = _place()
        theirs = pl.ds(pl.multiple_of((1 - c) * HALF, 16), HALF)
        cp = pltpu.make_async_remote_copy(src_ref=in_ref.at[:, theirs], dst_ref=out_ref, send_sem=send_sem, recv_sem=recv_sem,
                                          device_id=(x, y, 1 - c), device_id_type=MESH)
        cp.start()
        cp.wait()

    return _pcall(body, name="rs_sibling", in_specs=[ANY], out_specs=ANY, out_shape=SDS((N_CHIPS, HALF, D), gpk.dtype),
                  scratch=[pltpu.SemaphoreType.DMA, pltpu.SemaphoreType.DMA])(gpk)


def _rs_add_sibling(cidx, gpk, got):
    HALF = got.shape[1]
    th = HALF // 4
    nh = HALF // th
    assert th % 16 == 0

    def body(c_ref, a_ref, b_ref, o_ref):
        o_ref[...] = (a_ref[...].astype(F32) + b_ref[...].astype(F32)).astype(BF16)

    gs = pltpu.PrefetchScalarGridSpec(
        num_scalar_prefetch=1, grid=(N_CHIPS, nh),
        in_specs=[pl.BlockSpec((1, th, D), lambda j, i, c: (j, c[0] * nh + i, 0)), pl.BlockSpec((1, th, D), lambda j, i, c: (j, i, 0))],
        out_specs=pl.BlockSpec((1, th, D), lambda j, i, c: (j, i, 0)))
    return pl.pallas_call(body, name="rs_add_sibling", grid_spec=gs, out_shape=SDS((N_CHIPS, HALF, D), BF16),
                          compiler_params=pltpu.CompilerParams(dimension_semantics=("parallel", "parallel"),
                                                               vmem_limit_bytes=48 << 20))(cidx, gpk, got)


def _rs_sibling_add(gpk):
    HALF = gpk.shape[1] // 2
    tr = 32
    assert HALF % tr == 0
    buf = pltpu.VMEM((N_CHIPS, HALF, D), BF16)
    sems = pltpu.SemaphoreType.DMA((N_CHIPS,))

    def body(in_ref, out_ref, own_v, got_v, sum_v, send_sems, recv_sems, load_sems, store_sems):
        x, y, c, _ = _place()
        mine = pl.ds(pl.multiple_of(c * HALF, 16), HALF)
        theirs = pl.ds(pl.multiple_of((1 - c) * HALF, 16), HALF)
        sends = [pltpu.make_async_remote_copy(src_ref=in_ref.at[j, theirs], dst_ref=got_v.at[j], send_sem=send_sems.at[j],
                                              recv_sem=recv_sems.at[j], device_id=(x, y, 1 - c), device_id_type=MESH)
                 for j in range(N_CHIPS)]
        loads = [pltpu.make_async_copy(in_ref.at[j, mine], own_v.at[j], load_sems.at[j]) for j in range(N_CHIPS)]
        stores = [pltpu.make_async_copy(sum_v.at[j], out_ref.at[j], store_sems.at[j]) for j in range(N_CHIPS)]
        for j in range(N_CHIPS):
            sends[j].start()
            loads[j].start()
        for j in range(N_CHIPS):
            sends[j].wait_recv()
            loads[j].wait()

            def tile(i, carry, j=j):
                r = pl.ds(pl.multiple_of(i * tr, tr), tr)
                sum_v[j, r] = (own_v[j, r].astype(F32) + got_v[j, r].astype(F32)).astype(BF16)
                return carry

            lax.fori_loop(0, HALF // tr, tile, 0)
            stores[j].start()
        for j in range(N_CHIPS):
            sends[j].wait_send()
            stores[j].wait()

    return _pcall(body, name="rs_sibling_add", in_specs=[ANY], out_specs=ANY, out_shape=SDS((N_CHIPS, HALF, D), BF16),
                  scratch=[buf, buf, buf, sems, sems, sems, sems])(gpk)


def _rs_chips_start(part, small, after):
    def body(p_ref, s_ref, land_ref, sland_ref, after_ref, send_sems, recv_sems, p_thru, s_thru, land_thru, sland_thru, token):
        x, y, c, chips = _place()
        for k, (cx, cy) in enumerate(chips):
            pltpu.make_async_remote_copy(src_ref=p_ref.at[2 * cx + cy], dst_ref=land_ref.at[2 * x + y], send_sem=send_sems.at[k],
                                         recv_sem=recv_sems.at[k], device_id=(cx, cy, c), device_id_type=MESH).start()
        peers = [(x, y, 1 - c)] + [(cx, cy, c) for cx, cy in chips] + [(cx, cy, 1 - c) for cx, cy in chips]
        for k, to in enumerate(peers):
            pltpu.make_async_remote_copy(src_ref=s_ref, dst_ref=sland_ref.at[4 * x + 2 * y + c], send_sem=send_sems.at[3 + k],
                                         recv_sem=recv_sems.at[3 + k], device_id=to, device_id_type=MESH).start()
        token[...] = jnp.zeros_like(token)

    return pl.pallas_call(
        body, name="rs_chips_start",
        out_shape=(pltpu.SemaphoreType.DMA((10,)), pltpu.SemaphoreType.DMA((10,)), pltpu.HBM(part.shape, part.dtype),
                   pltpu.HBM(small.shape, small.dtype), pltpu.HBM(part.shape, part.dtype), pltpu.HBM((N_DEV, 8, D), F32),
                   SDS((8, LANES), F32)),
        in_specs=(HBM, HBM, HBM, HBM, ANY), out_specs=(SEM, SEM, HBM, HBM, HBM, HBM, pl.BlockSpec(memory_space=pltpu.VMEM)),
        input_output_aliases={0: 2, 1: 3, 2: 4, 3: 5}, compiler_params=pltpu.CompilerParams(has_side_effects=DATAFLOW),
    )(_in_hbm(part), _in_hbm(small), _in_hbm(lax.empty(part.shape, part.dtype)), _in_hbm(lax.empty((N_DEV, 8, D), F32)), after)


def _rs_chips_wait(send_sems, recv_sems, p_thru, s_thru, land_thru, sland_thru, after):
    def body(p_ref, s_ref, land_ref, sland_ref, send_sems, recv_sems, *after_and_outputs):
        x, y, c, chips = _place()
        for k, (cx, cy) in enumerate(chips):
            cp = pltpu.make_async_remote_copy(src_ref=p_ref.at[0], dst_ref=land_ref.at[2 * cx + cy], send_sem=send_sems.at[k],
                                              recv_sem=recv_sems.at[k], device_id=(cx, cy, c), device_id_type=MESH)
            cp.wait_send()
            cp.wait_recv()
        peers = [(x, y, 1 - c)] + [(cx, cy, c) for cx, cy in chips] + [(cx, cy, 1 - c) for cx, cy in chips]
        for k, (px, py, pc) in enumerate(peers):
            cp = pltpu.make_async_remote_copy(src_ref=s_ref, dst_ref=sland_ref.at[4 * px + 2 * py + pc], send_sem=send_sems.at[3 + k],
                                              recv_sem=recv_sems.at[3 + k], device_id=(px, py, pc), device_id_type=MESH)
            cp.wait_send()
            cp.wait_recv()

    hbm = lambda a: pltpu.HBM(a.shape, a.dtype)
    outs = pl.pallas_call(
        body, name="rs_chips_wait", out_shape=(hbm(p_thru), hbm(s_thru), hbm(land_thru), hbm(sland_thru)),
        in_specs=(HBM, HBM, HBM, HBM, SEM, SEM) + (ANY,) * len(after), out_specs=(HBM, HBM, HBM, HBM),
        input_output_aliases={0: 0, 1: 1, 2: 2, 3: 3}, compiler_params=pltpu.CompilerParams(has_side_effects=DATAFLOW),
    )(p_thru, s_thru, land_thru, sland_thru, send_sems, recv_sems, *after)
    return outs[0], outs[2], outs[3]


def _rs_add_chips(qidx, part, parts):
    HALF = part.shape[1]
    th = HALF // 4
    nh = HALF // th
    assert th % 16 == 0

    def body(q_ref, own_ref, p_ref, o_ref):
        for me in range(N_CHIPS):
            @pl.when(q_ref[0] == me)
            def _(me=me):
                t = [(own_ref[0] if j == me else p_ref[j]).astype(F32) for j in range(N_CHIPS)]
                o_ref[...] = ((t[0] + t[1]) + t[2]) + t[3]

    gs = pltpu.PrefetchScalarGridSpec(
        num_scalar_prefetch=1, grid=(HALF // th,),
        in_specs=[pl.BlockSpec((1, th, D), lambda i, q: (q[0], i, 0)), pl.BlockSpec((N_CHIPS, th, D), lambda i, q: (0, i, 0))],
        out_specs=pl.BlockSpec((th, D), lambda i, q: (q[1] * nh + i, 0)))
    return pl.pallas_call(body, name="rs_add_chips", grid_spec=gs, out_shape=SDS((2 * HALF, D), F32),
                          compiler_params=pltpu.CompilerParams(dimension_semantics=("parallel",),
                                                               vmem_limit_bytes=48 << 20))(qidx, part, parts)


def _rs_join(shard, name):
    HALF = shard.shape[0] // 2

    def body(in_ref, out_ref, send_sem, recv_sem):
        x, y, c, _ = _place()
        rows = pl.ds(pl.multiple_of(c * HALF, 16), HALF)
        cp = pltpu.make_async_remote_copy(src_ref=in_ref.at[rows], dst_ref=out_ref.at[rows], send_sem=send_sem, recv_sem=recv_sem,
                                          device_id=(x, y, 1 - c), device_id_type=MESH)
        cp.start()
        cp.wait()

    return _pcall(body, name=name, in_specs=[ANY], out_specs=ANY, out_shape=SDS(shard.shape, F32),
                  scratch=[pltpu.SemaphoreType.DMA, pltpu.SemaphoreType.DMA], aliases={0: 0})(shard)


def _reduce_late_start(gpk, after):
    rows = gpk.shape[1]
    HALF = rows // 2
    assert HALF % 16 == 0

    def body(in_ref, land_ref, after_ref, send_sems, recv_sems, in_thru, land_thru, token):
        x, y, c, chips = _place()
        me = 4 * x + 2 * y + c
        peers = [(x, y, 1 - c)] + [(cx, cy, c) for cx, cy in chips] + [(cx, cy, 1 - c) for cx, cy in chips]
        for k, (px, py, pc) in enumerate(peers):
            src = in_ref.at[2 * px + py, pl.ds(pl.multiple_of(pc * HALF, 16), HALF)]
            pltpu.make_async_remote_copy(src_ref=src, dst_ref=land_ref.at[me], send_sem=send_sems.at[k], recv_sem=recv_sems.at[k],
                                         device_id=(px, py, pc), device_id_type=MESH).start()
        token[...] = jnp.zeros_like(token)

    return pl.pallas_call(
        body, name="reduce_late_start",
        out_shape=(pltpu.SemaphoreType.DMA((7,)), pltpu.SemaphoreType.DMA((7,)), pltpu.HBM(gpk.shape, gpk.dtype),
                   pltpu.HBM((N_DEV, HALF, D), gpk.dtype), SDS((8, LANES), F32)),
        in_specs=(HBM, HBM, ANY), out_specs=(SEM, SEM, HBM, HBM, pl.BlockSpec(memory_space=pltpu.VMEM)),
        input_output_aliases={0: 2, 1: 3}, compiler_params=pltpu.CompilerParams(has_side_effects=DATAFLOW),
    )(_in_hbm(gpk), _in_hbm(lax.empty((N_DEV, HALF, D), gpk.dtype)), after)


def _reduce_late_wait(send_sems, recv_sems, in_thru, land_thru, after):
    def body(in_ref, land_ref, send_sems, recv_sems, after_ref, in_out, got_ref):
        x, y, c, chips = _place()
        peers = [(x, y, 1 - c)] + [(cx, cy, c) for cx, cy in chips] + [(cx, cy, 1 - c) for cx, cy in chips]
        for k, (px, py, pc) in enumerate(peers):
            cp = pltpu.make_async_remote_copy(src_ref=land_ref.at[0], dst_ref=land_ref.at[4 * px + 2 * py + pc],
                                              send_sem=send_sems.at[k], recv_sem=recv_sems.at[k],
                                              device_id=(px, py, pc), device_id_type=MESH)
            cp.wait_send()
            cp.wait_recv()

    return pl.pallas_call(
        body, name="reduce_late_wait",
        out_shape=(pltpu.HBM(in_thru.shape, in_thru.dtype), pltpu.HBM(land_thru.shape, land_thru.dtype)),
        in_specs=(HBM, HBM, SEM, SEM, ANY), out_specs=(HBM, HBM), input_output_aliases={0: 0, 1: 1},
        compiler_params=pltpu.CompilerParams(has_side_effects=DATAFLOW),
    )(in_thru, land_thru, send_sems, recv_sems, after)


def _reduce_late_add(didx, gpk, parts):
    HALF = parts.shape[1]
    th = HALF // 4
    nh = HALF // th
    assert th % 16 == 0

    def body(d_ref, own_ref, p_ref, o_ref):
        for me in range(N_DEV):
            @pl.when(d_ref[0] == me)
            def _(me=me):
                t = [(own_ref[0] if j == me else p_ref[j]).astype(F32) for j in range(N_DEV)]
                o_ref[...] = ((((((t[0] + t[1]) + t[2]) + t[3]) + t[4]) + t[5]) + t[6]) + t[7]

    gs = pltpu.PrefetchScalarGridSpec(
        num_scalar_prefetch=1, grid=(nh,),
        in_specs=[pl.BlockSpec((1, th, D), lambda i, d: (d[1], d[2] * nh + i, 0)), pl.BlockSpec((N_DEV, th, D), lambda i, d: (0, i, 0))],
        out_specs=pl.BlockSpec((th, D), lambda i, d: (d[2] * nh + i, 0)))
    return pl.pallas_call(body, name="reduce_late_add", grid_spec=gs, out_shape=SDS((2 * HALF, D), F32),
                          compiler_params=pltpu.CompilerParams(dimension_semantics=("parallel",),
                                                               vmem_limit_bytes=48 << 20))(didx, gpk, parts)


def _pack_early(b, dtype):
    lanes = lambda a: jnp.pad(a.astype(dtype), ((0, 0), (0, D - a.shape[1])))
    pair = jnp.concatenate([b["w_q_b"].astype(dtype), b["w_ple"].astype(dtype), jnp.zeros((256, D - 640), dtype)], axis=1)
    return jnp.concatenate([lanes(b["w_in"]), pair, lanes(b["w_kv_b"])], axis=0)


def _pack_late(b, dtype):
    return jnp.concatenate([b[n].astype(dtype) for n in ("w_mla_up", "w_swa_up", "w_out", "w_ple_gate", "w_mlp_up", "w_mlp_down")],
                           axis=0)


def _full_weights(gathered, which):
    out = {}
    for n, r, c in BIG:
        buf, row0, lane0 = PACK_AT[n]
        if buf != which:
            continue
        blk = gathered[:, row0:row0 + r, lane0:lane0 + c]
        if n in ("w_mlp_up", "w_mlp_down"):
            assert (r, c) == (D, D) and row0 % D == 0
            out[n] = (gathered, row0 // D)
        elif n == "w_in":
            out["w_in_p"] = _w_in_internal([blk[j] for j in range(N_CHIPS)])
        elif n in COL_SHARDED:
            out[n] = jnp.swapaxes(blk, 0, 1).reshape(r, N_CHIPS * c)
        else:
            out[n] = blk.reshape(N_CHIPS * r, c)
    return out


def _split_full_grads(grads, pack, dtype):
    shard = {n: (r, c) for n, r, c in BIG}
    chunks = []
    for j in range(N_CHIPS):
        blocks = {}
        for n, g in grads.items():
            if n == "w_in_p":
                blocks["w_in"] = _w_in_grad_shard(g, j)
                continue
            r, c = shard[n]
            blocks[n] = g[:, j * c:(j + 1) * c] if n in COL_SHARDED else g[j * r:(j + 1) * r]
        chunks.append(pack(blocks, dtype))
    return jnp.stack(chunks)


W_IN_SHARD = 936
W_IN_SEGMENTS = ((0, 256, (3072,)), (256, 384, (3840,)), (384, 416, (4032,)), (416, 1440, (0,)), (1440, 1504, (3328, 3392)),
                 (1504, 1568, (3456, 3520)), (1568, 1632, (3584, 3648)), (1632, 1696, (3712, 3776)), (1696, 3744, (1024,)))


def _w_in_internal(shards):
    def cols(a, b):
        out = []
        for j, s in enumerate(shards):
            lo, hi = max(a, W_IN_SHARD * j), min(b, W_IN_SHARD * (j + 1))
            if lo < hi:
                out.append(s[:, lo - W_IN_SHARD * j:hi - W_IN_SHARD * j])
        return out

    pieces = {}
    for a, b, places in W_IN_SEGMENTS:
        for at in places:
            pieces[at] = cols(a, b)
    zeros = lambda n: [jnp.zeros((D, n), shards[0].dtype)]
    pieces[3968] = zeros(64)
    pieces[4064] = zeros(32)
    return jnp.concatenate([piece for at in sorted(pieces) for piece in pieces[at]], axis=1)


def _w_in_grad_shard(g, j):
    def internal(a, b):
        out = []
        while a < b:
            end = min(b, (a // D + 1) * D)
            out.append(g[a // D][:, a % D:a % D + end - a])
            a = end
        return out

    out = []
    for a, b, places in W_IN_SEGMENTS:
        lo, hi = max(a, W_IN_SHARD * j), min(b, W_IN_SHARD * (j + 1))
        if lo < hi:
            parts = [internal(at + lo - a, at + hi - a) for at in places]
            if len(parts) == 1:
                out += parts[0]
            else:
                assert len(parts[0]) == len(parts[1]) == 1
                out.append(parts[0][0] + parts[1][0])
    return jnp.concatenate(out, axis=1)


def _local_step(x, p, tgt, w, small, late_weights, late_grads_out):
    T = x.shape[0]
    tm = 256
    tb = 256
    w_in_p = w["w_in_p"]
    wqb = jnp.pad(w["w_q_b"].reshape(Q_LORA, MLA_HEADS, 96), ((0, 0), (0, 0), (0, 32))).reshape(Q_LORA, 2048)
    wkv = w["w_kv_b"].reshape(KV_LORA, MLA_HEADS, 128)
    wkn = jnp.pad(wkv[:, :, :64], ((0, 0), (0, 0), (0, 64))).reshape(KV_LORA, 2048)
    wv = wkv[:, :, 64:].reshape(KV_LORA, 1024)
    tab_m = _rope_tables(T, "mla")
    tab_s = _rope_tables(T, "swa")
    g1, gq, gkv, sinks = small["g_mix_pre"], small["g_q_a"], small["g_kv_a"], small["sinks"]
    g2, g3, g4, g5 = small["g_mix_post"], small["g_mlp_pre"], small["g_mlp_post"], small["g_ple"]
    sink_vec = sinks.reshape(SWA_HEADS)

    z, h1 = _fwd_in(x, g1, w_in_p, tm)
    qn, kvn, km, vm, qt, kt, vt, qs, ks, vs = _fwd_qkv(z, gq, gkv, wqb, wkn, wv, tab_m, tab_s, tb)
    om, lse_m = _mla_fwd(qt, km, vt, tb)
    os_, lse_s = _swa_fwd(sink_vec, qs, ks, vs)
    w = {**w, **late_weights((om, os_))}
    y, yo, au, bu, x1 = _fwd_mix(om, os_, z, x, w["w_mla_up"], w["w_swa_up"], w["w_out"], g2, tm)
    h2, u = _fwd_mlp_up(x1, g3, w["w_mlp_up"], tm)
    d, x2 = _fwd_mlp_down(u, w["w_mlp_down"], x1, g4, tm)
    loss, dx2, dgt, de0, dg5 = _ple_fwd_bwd(p, x2, tgt, w["w_ple"], g5, w["w_ple_gate"], tm)

    dd, da, dg4 = _bwd_mlp_down(dx2, d, g4, w["w_mlp_down"], u, tm)
    dx1, dg3 = _bwd_mlp_up(da, w["w_mlp_up"], x1, g3, dx2, tm)
    dyo, dg2, dau, dbu, dga, dgb, dos, delta_m, dom_t = _bwd_mix(dx1, yo, g2, w["w_out"], z, au, bu, w["w_mla_up"],
                                                                w["w_swa_up"], om, tb)
    gpk_late = lax.empty((N_CHIPS, PACK_ROWS["late"], D), BF16)
    for weight, a_, g_ in (("w_mla_up", om, dau), ("w_swa_up", os_, dbu), ("w_out", y, dyo), ("w_ple_gate", x2, dgt),
                           ("w_mlp_up", h2, da), ("w_mlp_down", u, dd)):
        gpk_late = _wgrad(a_, g_, "wgrad_" + weight[2:], into=(gpk_late, weight))
    token = late_grads_out(gpk_late)
    delta_m = delta_m + token[0, 0]
    dqm, dkm, dvm = _mla_bwd(qt, km, kt, vm, dom_t, lse_m, delta_m, tb)
    dqs, dkc, dkp, dvc, dvp, dsink = _swa_bwd(sink_vec, qs, ks, vs, dos, os_, lse_s)
    dqb, dknb, dvb, dsq, drest, dgq, dgkv = _bwd_qkv(dqm, dkm, dvm, dqs, dkc, dkp, dvc, dvp, z, gq, gkv, wqb, wkn, wv,
                                                      tab_m, tab_s)
    gx, dg1 = _bwd_in(dsq, dga, dgb, drest, w_in_p, x, g1, dx1, tm)

    g_in_p = [_wgrad(h1, dsq, "wgrad_in_sq"), _wgrad(h1, dga, "wgrad_in_ga"), _wgrad(h1, dgb, "wgrad_in_gb"),
              _wgrad(h1, drest, "wgrad_in_rest")]
    g_qb_p = _wgrad_t(dqb, qn, "wgrad_q_b").T
    g_kn_p = _wgrad_t(dknb, kvn, "wgrad_kv_b_nope").T
    g_v_p = _wgrad_t(dvb, kvn, "wgrad_kv_b_v").T
    grads = {
        "w_in_p": g_in_p,
        "w_q_b": g_qb_p.reshape(Q_LORA, MLA_HEADS, 128)[:, :, :96].reshape(Q_LORA, 1536),
        "w_kv_b": jnp.concatenate([g_kn_p.reshape(KV_LORA, MLA_HEADS, 128)[:, :, :64], g_v_p.reshape(KV_LORA, MLA_HEADS, 64)],
                                  axis=2).reshape(KV_LORA, 2048),
        "w_ple": _wgrad(p, de0, "wgrad_ple"),
    }
    small_grads = {"g_mix_pre": dg1, "g_q_a": dgq.reshape(1, Q_LORA), "g_kv_a": dgkv.reshape(1, KV_LORA), "sinks": dsink[0:1, 0:SWA_HEADS], "g_mix_post": dg2,
                   "g_mlp_pre": dg3, "g_mlp_post": dg4, "g_ple": dg5}
    return loss, gx, grads, small_grads


def _pack_small(vals, fill, scalar=None):
    wide = [vals[n] for n, k in SMALL if k == D]
    narrow = [vals[n] for n, k in SMALL if k != D]
    used = sum(k for _, k in SMALL if k != D)
    last = jnp.concatenate(narrow + [jnp.full((1, D - used), fill, F32)], axis=1)
    rest = jnp.full((2, D), fill, F32)
    if scalar is not None:
        rest = jnp.concatenate([jnp.concatenate([scalar, rest[0:1, 1:]], axis=1), rest[1:2]], axis=0)
    return jnp.concatenate(wide + [last, rest], axis=0)


def _unpack_small(pk):
    out, row, off = {}, 0, 0
    for n, k in SMALL:
        if k == D:
            out[n] = pk[row:row + 1]
            row += 1
    for n, k in SMALL:
        if k != D:
            out[n] = pk[5:6, off:off + k]
            off += k
    return out


def kernel(x, p, g_mix_pre, w_in, g_q_a, w_q_b, g_kv_a, w_kv_b, sinks, w_mla_up, w_swa_up, w_out, g_mix_post, g_mlp_pre, w_mlp_up, w_mlp_down, g_mlp_post, w_ple, g_ple, w_ple_gate, loss_target, m_g_mix_pre, m_w_in, m_g_q_a, m_w_q_b, m_g_kv_a, m_w_kv_b, m_sinks, m_w_mla_up, m_w_swa_up, m_w_out, m_g_mix_post, m_g_mlp_pre, m_w_mlp_up, m_w_mlp_down, m_g_mlp_post, m_w_ple, m_g_ple, m_w_ple_gate, v_g_mix_pre, v_w_in, v_g_q_a, v_w_q_b, v_g_kv_a, v_w_kv_b, v_sinks, v_w_mla_up, v_w_swa_up, v_w_out, v_g_mix_post, v_g_mlp_pre, v_w_mlp_up, v_w_mlp_down, v_g_mlp_post, v_w_ple, v_g_ple, v_w_ple_gate):
    given = dict(locals())
    big_w = {n: given[n][0] for n, _, _ in BIG}
    small_w = {n: given[n] for n, _ in SMALL}
    small_m = {n: given["m_" + n] for n, _ in SMALL}
    small_v = {n: given["v_" + n] for n, _ in SMALL}

    core = lax.axis_index("c")
    chip = 2 * lax.axis_index("x") + lax.axis_index("y")
    core_i = core.astype(jnp.int32).reshape(1)
    dev_i = jnp.stack([2 * chip + core, chip, core]).astype(jnp.int32)

    own_early = _pack_early(big_w, BF16)
    own_late = _pack_late(big_w, BF16)
    got_early = _all_gather(own_early)
    late_flight = _gather_late_start(own_late, got_early)
    weights = _full_weights(got_early, "early")
    step_small = {**small_w, "g_mix_pre": small_w["g_mix_pre"] + late_flight[4][0, 0]}

    def late_weights(after):
        return _full_weights(_gather_late_wait(*late_flight[:4], after), "late")

    flight = {}

    def late_grads_out(gpk_late):
        flight["late"] = _reduce_late_start(gpk_late, dev_i)
        return flight["late"][4]

    loss_blk, gx, grads, small_grads = _local_step(x[0], p[0, 0], loss_target[0], weights, step_small, late_weights,
                                                   late_grads_out)

    gpk = _split_full_grads(grads, _pack_early, BF16)
    part = _rs_sibling_add(gpk)
    small_own = _pack_small(small_grads, 0.0, loss_blk[0:1, 0:1])
    early_flight = _rs_chips_start(part, small_own, dev_i)

    out_g, out_d, out_m, out_v = {}, {}, {}, {}
    gpk_late, parts_late = _reduce_late_wait(*flight["late"][:4], early_flight[6])
    joined_late = _rs_join(_reduce_late_add(dev_i, gpk_late, parts_late), "rs_join_late")
    for n, _, _ in BIG:
        if PACK_AT[n][0] == "late":
            out_g[n], out_d[n], out_m[n], out_v[n] = _adamw(given[n], joined_late, given["m_" + n], given["v_" + n], n)

    part, parts, small_parts = _rs_chips_wait(*early_flight[:6], [out_d[n] for n in out_d])
    joined_early = _rs_join(_rs_add_chips(dev_i[1:3], part, parts), "rs_join_early")
    for n, _, _ in BIG:
        if PACK_AT[n][0] == "early":
            t = given[n].shape[2] % LANES != 0
            wmv = [jnp.swapaxes(a, 1, 2) if t else a for a in (given[n], given["m_" + n], given["v_" + n])]
            res = _adamw(wmv[0], joined_early, wmv[1], wmv[2], n, transposed=t)
            out_g[n], out_d[n], out_m[n], out_v[n] = [jnp.swapaxes(a, 1, 2) if t else a for a in res]

    mine = (lax.broadcasted_iota(jnp.int32, (N_DEV, 1, 1), 0) == dev_i[0])
    g_small_pk, d_small_pk, m_small_pk, v_small_pk = _adamw_small(
        _pack_small(small_w, 0.0), jnp.where(mine, small_own[None], small_parts), _pack_small(small_m, 0.0),
        _pack_small(small_v, 1.0))
    loss = g_small_pk[6, 0]
    for out, pk in ((out_g, g_small_pk), (out_d, d_small_pk), (out_m, m_small_pk), (out_v, v_small_pk)):
        out.update(_unpack_small(pk))
    order = ["g_mix_pre", "w_in", "g_q_a", "w_q_b", "g_kv_a", "w_kv_b", "sinks", "w_mla_up", "w_swa_up", "w_out", "g_mix_post",
             "g_mlp_pre", "w_mlp_up", "w_mlp_down", "g_mlp_post", "w_ple", "g_ple", "w_ple_gate"]
    return (loss, gx[None], *[out_g[n] for n in order], *[out_d[n] for n in order], *[out_m[n] for n in order],
            *[out_v[n] for n in order])
```

```python
import math

import jax
import jax.numpy as jnp
import numpy as np
from jax import lax
from jax.experimental import pallas as pl
from jax.experimental.pallas import tpu as pltpu

F32 = jnp.float32
BF16 = jnp.bfloat16
SDS = jax.ShapeDtypeStruct

D = 1024
D_FF = 4096
PLE = 256
Q_LORA = 256
KV_LORA = 128
MLA_HEADS = 16
MLA_NOPE = 64
MLA_ROPE = 32
SWA_HEADS = 16
SWA_HD = 64
WINDOW = 128
ROPE_THETA = 10000.0
EPS = 1e-6
NEG = -1e30
NZ = 4096
MLA_SCALE = (MLA_NOPE + MLA_ROPE) ** -0.5
LOG2_E = math.log2(math.e)
MLA_LOG2_SCALE = MLA_SCALE * LOG2_E
SWA_SCALE = SWA_HD ** -0.5

ADAM_LR = 0.001
ADAM_B1 = 0.9
ADAM_B2 = 0.999
ADAM_EPS = 1e-08
ADAM_WD = 0.01
ADAM_STEP = 10

LANES = 128
ATT_COLS = 128
VT_ROWS = 80
N_CHIPS = 4
N_DEV = 8
MESH = pl.DeviceIdType.MESH

NT = (((1,), (1,)), ((), ()))
TN = (((0,), (0,)), ((), ()))

BIG = (("w_in", 1024, 936), ("w_q_b", 256, 384), ("w_kv_b", 128, 512), ("w_mla_up", 256, 1024),
       ("w_swa_up", 256, 1024), ("w_out", 256, 1024), ("w_mlp_up", 1024, 1024), ("w_mlp_down", 1024, 1024),
       ("w_ple", 256, 256), ("w_ple_gate", 256, 1024))
COL_SHARDED = ("w_in", "w_q_b", "w_kv_b", "w_mlp_up", "w_ple")
PACK_AT = {"w_in": ("early", 0, 0), "w_q_b": ("early", 1024, 0), "w_ple": ("early", 1024, 384), "w_kv_b": ("early", 1280, 0),
           "w_mla_up": ("late", 0, 0), "w_swa_up": ("late", 256, 0), "w_out": ("late", 512, 0), "w_ple_gate": ("late", 768, 0),
           "w_mlp_up": ("late", 1024, 0), "w_mlp_down": ("late", 2048, 0)}
PACK_ROWS = {"early": 1408, "late": 3072}
SMALL = (("g_mix_pre", 1024), ("g_q_a", 256), ("g_kv_a", 128), ("sinks", 16), ("g_mix_post", 1024),
         ("g_mlp_pre", 1024), ("g_mlp_post", 1024), ("g_ple", 1024))


def _dot(a, b):
    return jnp.dot(a, b, preferred_element_type=F32)


def _dot_nt(a, b):
    return lax.dot_general(a, b, NT, preferred_element_type=F32)


def _dot_tn(a, b):
    return lax.dot_general(a, b, TN, preferred_element_type=F32)


def _pcall(body, *, name, out_shape, grid=(), in_specs=None, out_specs=None, scratch=(), sem=None, vmem_mb=48, aliases=None):
    params = dict(vmem_limit_bytes=vmem_mb << 20)
    if sem is not None:
        params["dimension_semantics"] = sem
    return pl.pallas_call(body, name=name, grid=grid, in_specs=in_specs, out_specs=out_specs, out_shape=out_shape,
                          scratch_shapes=list(scratch), input_output_aliases=aliases or {},
                          compiler_params=pltpu.CompilerParams(**params))


def _rows(tm, n, col=0):
    return pl.BlockSpec((tm, n), lambda i: (i, col))


def _full(shape):
    return pl.BlockSpec(shape, lambda i: (0,) * len(shape))


def _chunks(k):
    assert D_FF == N_CHIPS * D
    return pl.BlockSpec((N_CHIPS, D, D), lambda i: (0, k, 0))


def _rms(x, g):
    r = lax.rsqrt(jnp.mean(x * x, axis=-1, keepdims=True) + EPS)
    return x * r * g


def _rms_bwd(dy, x, g):
    r = lax.rsqrt(jnp.mean(x * x, axis=-1, keepdims=True) + EPS)
    xn = x * r
    dn = dy * g
    dx = r * (dn - xn * jnp.mean(dn * xn, axis=-1, keepdims=True))
    return dx, jnp.sum(dy * xn, axis=0, keepdims=True)


def _sigmoid(x):
    return 1.0 / (1.0 + jnp.exp(-x))


def _rope(x, c, a, b, half):
    return x * c + pltpu.roll(x, LANES - half, 1) * a + pltpu.roll(x, half, 1) * b


def _rope_tables(T, kind):
    lane = np.arange(LANES)
    if kind == "mla":
        half = MLA_ROPE // 2
        rel = lane - MLA_NOPE
        on = (rel >= 0) & (rel < MLA_ROPE)
        d = MLA_ROPE
    else:
        half = SWA_HD // 2
        rel = lane % SWA_HD
        on = np.ones((LANES,), bool)
        d = SWA_HD
    first = on & (rel < half)
    second = on & (rel >= half)
    f = np.where(first, rel, rel - half).astype(np.float32)
    inv = np.exp(np.float32(-math.log(ROPE_THETA)) * f * np.float32(2.0 / d)).astype(np.float32)
    ang = (np.arange(T, dtype=np.float32)[:, None] * inv[None, :]).astype(np.float64)
    cos, sin = np.cos(ang).astype(np.float32), np.sin(ang).astype(np.float32)
    c = np.where(on[None], cos, np.float32(1.0))
    a = np.where(first[None], -sin, np.float32(0.0))
    b = np.where(second[None], sin, np.float32(0.0))
    return c, a, b


def _fwd_in(x, g1, w_in_p, tm):
    T = x.shape[0]

    def body(x_ref, g_ref, w_ref, z_ref, h_ref):
        h = _rms(x_ref[...], g_ref[...]).astype(BF16)
        h_ref[...] = h
        z_ref[...] = _dot(h, w_ref[...])

    return _pcall(body, name="fwd_in", grid=(T // tm,),
                  in_specs=[_rows(tm, D), _full((1, D)), _full((D, NZ))],
                  out_specs=[_rows(tm, NZ), _rows(tm, D)],
                  out_shape=[SDS((T, NZ), F32), SDS((T, D), BF16)], sem=("parallel",))(x, g1, w_in_p)


def _fwd_qkv(z, gq, gkv, wqb, wkn, wv, tab_m, tab_s, tm):
    T = z.shape[0]
    wqb_t, wkn_t, wv_t = wqb.T, wkn.T, wv.T
    tab_mt = [t.T for t in tab_m]

    def body(qa_ref, sq_ref, skd_ref, svd_ref, kva_ref, kr_ref, gq_ref, gkv_ref, wkn_ref, wv_ref, wqbt_ref, wknt_ref, wvt_ref,
             cm_ref, am_ref, bm_ref, cmt_ref, amt_ref, bmt_ref, cs_ref, as_ref, bs_ref,
             qn_ref, kvn_ref, km_ref, vm_ref, qt_ref, kt_ref, vt_ref, qs_ref, ks_ref, vs_ref):
        qn = _rms(qa_ref[...], gq_ref[...])
        qn_ref[...] = qn.astype(BF16)
        kvn = _rms(kva_ref[...], gkv_ref[...])
        kvn_b = kvn.astype(BF16)
        kvn_ref[...] = kvn_b
        qn_t = qn.T.astype(BF16)
        kvn_t = kvn.T.astype(BF16)
        cm, am, bm = cm_ref[...], am_ref[...], bm_ref[...]
        cmt, amt, bmt = cmt_ref[...], amt_ref[...], bmt_ref[...]
        cs, as_, bs = cs_ref[...], as_ref[...], bs_ref[...]
        k_rope = _rope(kr_ref[...], cm, am, bm, MLA_ROPE // 2)
        k_rope_t = k_rope.T
        half = MLA_ROPE // 2
        vm_ref[...] = _dot(kvn_b, wv_ref[...]).astype(BF16)
        km_all = _dot(kvn_b, wkn_ref[...])
        v_t = _dot(wvt_ref[...], kvn_t)
        q_t = _dot(wqbt_ref[...], qn_t)
        k_t = _dot(wknt_ref[...], kvn_t)
        ones_row = jnp.where(lax.broadcasted_iota(jnp.int32, (64, tm), 0) == 0, 1.0, 0.0)
        for h in range(MLA_HEADS):
            sl = slice(LANES * h, LANES * (h + 1))
            vt_ref[0, sl, :] = jnp.concatenate([v_t[64 * h:64 * (h + 1)], ones_row], axis=0).astype(BF16)
            qh = q_t[sl]
            qt_ref[0, sl, :] = (qh * cmt + pltpu.roll(qh, LANES - half, 0) * amt + pltpu.roll(qh, half, 0) * bmt).astype(BF16)
            km_ref[:, sl] = (km_all[:, sl] + k_rope).astype(BF16)
            kt_ref[0, sl, :] = (k_t[sl] + k_rope_t).astype(BF16)
        for j in range(D // LANES):
            sl = slice(LANES * j, LANES * (j + 1))
            qs_ref[:, sl] = _rope(sq_ref[:, sl], cs, as_, bs, SWA_HD // 2).astype(BF16)
        for j in range(2):
            sl = slice(LANES * j, LANES * (j + 1))
            ks_ref[:, sl] = _rope(skd_ref[:, sl], cs, as_, bs, SWA_HD // 2).astype(BF16)
        vs_ref[...] = svd_ref[...].astype(BF16)

    tab = [_rows(tm, LANES)] * 3
    tab_t = [pl.BlockSpec((LANES, tm), lambda i: (0, i))] * 3
    return _pcall(body, name="fwd_qkv", grid=(T // tm,),
                  in_specs=[_rows(tm, 256, 12), _rows(tm, 1024, 0), _rows(tm, 256, 13), _rows(tm, 256, 14),
                            _rows(tm, 128, 30), _rows(tm, 128, 31), _full((1, Q_LORA)), _full((1, KV_LORA)),
                            _full((KV_LORA, 2048)), _full((KV_LORA, 1024)), _full((2048, Q_LORA)), _full((2048, KV_LORA)),
                            _full((1024, KV_LORA))] + tab + tab_t + tab,
                  out_specs=[_rows(tm, Q_LORA), _rows(tm, KV_LORA), _rows(tm, 2048), _rows(tm, 1024),
                             pl.BlockSpec((1, 2048, tm), lambda i: (i, 0, 0)), pl.BlockSpec((1, 2048, tm), lambda i: (i, 0, 0)),
                             pl.BlockSpec((1, 2048, tm), lambda i: (i, 0, 0)),
                             _rows(tm, 1024), _rows(tm, 256), _rows(tm, 256)],
                  out_shape=[SDS((T, Q_LORA), BF16), SDS((T, KV_LORA), BF16), SDS((T, 2048), BF16),
                             SDS((T, 1024), BF16), SDS((T // tm, 2048, tm), BF16), SDS((T // tm, 2048, tm), BF16),
                             SDS((T // tm, 2048, tm), BF16),
                             SDS((T, 1024), BF16), SDS((T, 256), BF16), SDS((T, 256), BF16)],
                  sem=("parallel",))(z, z, z, z, z, z, gq, gkv, wkn, wv, wqb_t, wkn_t, wv_t, *tab_m, *tab_mt, *tab_s)


def _mla_fwd(qt, km, vt, tb):
    T = km.shape[0]
    nb = T // tb
    cc = ATT_COLS

    per = 2 if nb % 2 == 0 else 1

    def body(q_ref, k_ref, vt_ref, o_ref, l_ref, s_ref, p_ref, al_ref, m_ref, acc_ref):
        for blk in range(per):
            one_block(per * pl.program_id(1) + blk, blk, q_ref, k_ref, vt_ref, o_ref, l_ref, s_ref, p_ref, al_ref, m_ref, acc_ref)

    def one_block(i, blk, q_ref, k_ref, vt_ref, o_ref, l_ref, s_ref, p_ref, al_ref, m_ref, acc_ref):
        m_ref[...] = jnp.full(m_ref.shape, NEG, F32)
        acc_ref[...] = jnp.zeros_like(acc_ref)
        p_ref[1] = jnp.zeros(p_ref.shape[1:], BF16)
        al_ref[1] = jnp.ones(al_ref.shape[1:], F32)
        key = lax.broadcasted_iota(jnp.int32, (tb, cc), 0)
        qry = lax.broadcasted_iota(jnp.int32, (tb, cc), 1)

        def scores(j, slot):
            off = pl.multiple_of(j * tb, tb)
            for hh in range(2):
                sl = slice(LANES * hh, LANES * (hh + 1))
                s_ref[slot, hh] = _dot(k_ref[pl.ds(off, tb), sl], q_ref[blk, sl, :])

        def softmax(slot, diagonal):
            chains = [(hh, slice(cc * c, cc * (c + 1)), c) for hh in range(2) for c in range(tb // cc)]

            def scaled(hh, cols, c):
                t = s_ref[slot, hh, :, cols] * MLA_LOG2_SCALE
                return jnp.where(key <= qry + cc * c, t, NEG) if diagonal else t

            tops = []
            for hh, cols, c in chains:
                if diagonal:
                    top = jnp.max(scaled(hh, cols, c), axis=0, keepdims=True)
                else:
                    top = jnp.max(s_ref[slot, hh, :, cols], axis=0, keepdims=True) * MLA_LOG2_SCALE
                m_old = m_ref[hh, :, cols]
                mn = jnp.maximum(m_old, top)
                m_ref[hh, :, cols] = mn
                al_ref[slot, hh, :, cols] = jnp.exp2(m_old - mn)
                tops.append(mn)
            for (hh, cols, c), mn in zip(chains, tops):
                p_ref[slot, hh, :, cols] = jnp.exp2(scaled(hh, cols, c) - mn).astype(BF16)

        def accumulate(j, slot):
            for hh in range(2):
                acc_ref[hh] = al_ref[slot, hh] * acc_ref[hh] + _dot(vt_ref[j, LANES * hh:LANES * hh + VT_ROWS, :], p_ref[slot, hh])

        def step(t, carry):
            scores(2 * t + 1, 1)
            accumulate(jnp.maximum(2 * t - 1, 0), 1)
            softmax(0, False)
            scores(2 * t + 2, 0)
            accumulate(2 * t, 0)
            softmax(1, False)
            return carry

        scores(0, 0)
        lax.fori_loop(0, i // 2, step, 0)

        @pl.when(i % 2 == 1)
        def _():
            scores(i, 1)
            accumulate(jnp.maximum(i - 2, 0), 1)
            softmax(0, False)
            accumulate(i - 1, 0)
            softmax(1, True)
            accumulate(i, 1)

        @pl.when(i % 2 == 0)
        def _():
            accumulate(jnp.maximum(i - 1, 0), 1)
            softmax(0, True)
            accumulate(i, 0)
        den = [acc_ref[hh, 64:65, :] for hh in range(2)]
        o_ref[tb * blk:tb * (blk + 1), :] = jnp.concatenate([acc_ref[hh, 0:64, :] / den[hh] for hh in range(2)], axis=0).T
        sub = lax.broadcasted_iota(jnp.int32, (8, tb), 0)
        lse = [m_ref[hh] + jnp.log(den[hh]) * LOG2_E for hh in range(2)]
        l_ref[0, blk] = jnp.where(sub == 0, lse[0], jnp.where(sub == 1, lse[1], 0.0))

    return _pcall(body, name="mla_fwd", grid=(MLA_HEADS // 2, nb // per),
                  in_specs=[pl.BlockSpec((per, 256, tb), lambda p, i: (i, p, 0)), pl.BlockSpec((T, 256), lambda p, i: (0, p)),
                            pl.BlockSpec((nb, 2 * LANES, tb), lambda p, i: (0, p, 0))],
                  out_specs=[pl.BlockSpec((per * tb, LANES), lambda p, i: (i, p)),
                             pl.BlockSpec((1, per, 8, tb), lambda p, i: (p, i, 0, 0))],
                  out_shape=[SDS((T, D), F32), SDS((MLA_HEADS // 2, nb, 8, tb), F32)],
                  scratch=[pltpu.VMEM((2, 2, tb, tb), F32), pltpu.VMEM((2, 2, tb, tb), BF16), pltpu.VMEM((2, 2, 1, tb), F32),
                           pltpu.VMEM((2, 1, tb), F32), pltpu.VMEM((2, VT_ROWS, tb), F32)],
                  sem=("parallel", "arbitrary"))(qt, km, vt)


def _swa_mask(n):
    row = lax.broadcasted_iota(jnp.int32, (WINDOW, 2 * WINDOW), 0)
    col = lax.broadcasted_iota(jnp.int32, (WINDOW, 2 * WINDOW), 1)
    rel = row - col + WINDOW
    return (rel >= 0) & (rel < WINDOW) & ((col >= WINDOW) | (n > 0))


def _swa_specs(T):
    nb = T // WINDOW
    cur = lambda w: pl.BlockSpec((WINDOW, w), lambda n: (n, 0))
    prev = lambda w: pl.BlockSpec((WINDOW, w), lambda n: (jnp.maximum(n - 1, 0), 0))
    return nb, cur, prev


def _swa_fwd(sinks, qs, ks, vs):
    T = qs.shape[0]
    nb, cur, prev = _swa_specs(T)

    def body(sink_ref, q_ref, kc_ref, kp_ref, vc_ref, vp_ref, o_ref, l_ref, kb_ref, vb_ref, s_ref, p_ref):
        n = pl.program_id(0)
        mask = _swa_mask(n)
        lo = lax.broadcasted_iota(jnp.int32, (WINDOW, LANES), 1) < 64
        hi = jnp.logical_not(lo)
        for g in range(2):
            gs = slice(LANES * g, LANES * (g + 1))
            kb_ref[g] = jnp.concatenate([kp_ref[:, gs], kc_ref[:, gs]], axis=0)
            vb_ref[g] = jnp.concatenate([vp_ref[:, gs], vc_ref[:, gs]], axis=0)
        for h in range(SWA_HEADS):
            qp = q_ref[:, LANES * (h // 2):LANES * (h // 2 + 1)]
            qh = jnp.where(lo if h % 2 == 0 else hi, qp, jnp.zeros_like(qp))
            s_ref[h] = _dot_nt(qh, kb_ref[h // 8])
        for j in range(SWA_HEADS // 2):
            sl = slice(LANES * j, LANES * (j + 1))
            lses = []
            for h in (2 * j, 2 * j + 1):
                s = jnp.where(mask, s_ref[h] * SWA_SCALE, NEG)
                sk = sink_ref[h]
                m = jnp.maximum(jnp.max(s, axis=1, keepdims=True), sk)
                e = jnp.exp(s - m)
                den = jnp.sum(e, axis=1, keepdims=True) + jnp.exp(sk - m)
                p_ref[h] = (e / den).astype(BF16)
                lses.append(jnp.broadcast_to(m + jnp.log(den), (WINDOW, LANES)))
            l_ref[:, sl] = jnp.where(lo, lses[0], lses[1])
        for j in range(SWA_HEADS // 2):
            vb = vb_ref[j // 4]
            o_ref[:, LANES * j:LANES * (j + 1)] = jnp.where(lo, _dot(p_ref[2 * j], vb), _dot(p_ref[2 * j + 1], vb))

    return _pcall(body, name="swa_fwd", grid=(nb,),
                  in_specs=[pl.BlockSpec(memory_space=pltpu.SMEM), cur(D), cur(256), prev(256), cur(256), prev(256)],
                  out_specs=[cur(D), cur(D)], out_shape=[SDS((T, D), F32)] * 2,
                  scratch=[pltpu.VMEM((2, 2 * WINDOW, LANES), BF16), pltpu.VMEM((2, 2 * WINDOW, LANES), BF16),
                           pltpu.VMEM((SWA_HEADS, WINDOW, 2 * WINDOW), F32), pltpu.VMEM((SWA_HEADS, WINDOW, 2 * WINDOW), BF16)],
                  sem=("parallel",))(sinks, qs, ks, ks, vs, vs)


def _fwd_mix(om, os_, z, x, wmu, wsu, wo, g2, tm):
    T = x.shape[0]

    def body(om_ref, os_ref, ga_ref, gb_ref, x_ref, wmu_ref, wsu_ref, wo_ref, g2_ref,
             y_ref, yo_ref, au_ref, bu_ref, x1_ref):
        au = _dot(om_ref[...].astype(BF16), wmu_ref[...])
        bu = _dot(os_ref[...].astype(BF16), wsu_ref[...])
        au_ref[...] = au
        bu_ref[...] = bu
        y = (_sigmoid(ga_ref[...]) * au + _sigmoid(gb_ref[...]) * bu).astype(BF16)
        y_ref[...] = y
        yo = _dot(y, wo_ref[...])
        yo_ref[...] = yo
        x1_ref[...] = x_ref[...] + _rms(yo, g2_ref[...])

    r = _rows(tm, D)
    w = _full((D, D))
    return _pcall(body, name="fwd_mix", grid=(T // tm,),
                  in_specs=[r, r, _rows(tm, D, 1), _rows(tm, D, 2), r, w, w, w, _full((1, D))],
                  out_specs=[r] * 5,
                  out_shape=[SDS((T, D), BF16), SDS((T, D), F32), SDS((T, D), F32), SDS((T, D), F32), SDS((T, D), F32)],
                  sem=("parallel",))(om, os_, z, z, x, wmu, wsu, wo, g2)


def _fwd_mlp_up(x1, g3, w1, tm):
    T = x1.shape[0]

    def body(x_ref, g_ref, w_ref, h_ref, u_ref):
        h = _rms(x_ref[...], g_ref[...]).astype(BF16)
        h_ref[...] = h
        for j in range(N_CHIPS):
            u_ref[:, j * D:(j + 1) * D] = jnp.square(jnp.maximum(_dot(h, w_ref[j]), 0.0)).astype(BF16)

    return _pcall(body, name="fwd_mlp_up", grid=(T // tm,),
                  in_specs=[_rows(tm, D), _full((1, D)), _chunks(w1[1])],
                  out_specs=[_rows(tm, D), _rows(tm, D_FF)],
                  out_shape=[SDS((T, D), BF16), SDS((T, D_FF), BF16)],
                  sem=("parallel",))(x1, g3, w1[0])


def _fwd_mlp_down(u, w2, x1, g4, tm):
    T = x1.shape[0]

    def body(u_ref, w_ref, x_ref, g_ref, d_ref, x2_ref):
        d = _dot(u_ref[...], w_ref[...].reshape(D_FF, D))
        d_ref[...] = d
        x2_ref[...] = x_ref[...] + _rms(d, g_ref[...])

    return _pcall(body, name="fwd_mlp_down", grid=(T // tm,),
                  in_specs=[_rows(tm, D_FF), _chunks(w2[1]), _rows(tm, D), _full((1, D))],
                  out_specs=[_rows(tm, D), _rows(tm, D)], out_shape=[SDS((T, D), F32)] * 2,
                  sem=("parallel",))(u, w2[0], x1, g4)


def _ple_fwd_bwd(p, x2, tgt, wple, g5, wpg, tm):
    T = x2.shape[0]

    def body(p_ref, x2_ref, t_ref, wple_ref, g5_ref, wpg_ref, loss_ref, dx2_ref, dgt_ref, de0_ref, dg5_ref):
        @pl.when(pl.program_id(0) == 0)
        def _():
            loss_ref[...] = jnp.zeros_like(loss_ref)
            dg5_ref[...] = jnp.zeros_like(dg5_ref)

        e0 = _dot(p_ref[...].astype(BF16), wple_ref[...])
        g5 = g5_ref[...]
        r = lax.rsqrt(jnp.mean(e0 * e0, axis=-1, keepdims=True) + EPS)
        en = e0 * r
        e = en * g5
        x2 = x2_ref[...]
        s = _sigmoid(_dot(x2.astype(BF16), wpg_ref[...]))
        diff = x2 + s * e - t_ref[...]
        sq = jnp.sum(jnp.sum(diff * diff, axis=1, keepdims=True), axis=0, keepdims=True)
        loss_ref[...] += jnp.broadcast_to(sq * (0.5 / D), loss_ref.shape)
        dx3 = diff * (1.0 / D)
        de = dx3 * s
        dgt = (dx3 * e * s * (1.0 - s)).astype(BF16)
        dgt_ref[...] = dgt
        dn = de * g5
        de0_ref[...] = (r * (dn - en * jnp.mean(dn * en, axis=-1, keepdims=True))).astype(BF16)
        dg5_ref[...] += jnp.sum(de * en, axis=0, keepdims=True)
        dx2_ref[...] = dx3 + _dot_nt(dgt, wpg_ref[...])

    r = _rows(tm, D)
    return _pcall(body, name="ple_fwd_bwd", grid=(T // tm,),
                  in_specs=[_rows(tm, PLE), r, r, _full((PLE, D)), _full((1, D)), _full((D, D))],
                  out_specs=[_full((8, LANES)), r, r, r, _full((1, D))],
                  out_shape=[SDS((8, LANES), F32), SDS((T, D), F32), SDS((T, D), BF16), SDS((T, D), BF16), SDS((1, D), F32)],
                  sem=("arbitrary",))(p, x2, tgt, wple, g5, wpg)


def _bwd_mlp_down(dx2, d, g4, w2, u, tm):
    T = dx2.shape[0]

    def body(dx_ref, d_ref, g_ref, w_ref, u_ref, dd_ref, da_ref, dg_ref):
        @pl.when(pl.program_id(0) == 0)
        def _():
            dg_ref[...] = jnp.zeros_like(dg_ref)

        dd, dg = _rms_bwd(dx_ref[...], d_ref[...], g_ref[...])
        dg_ref[...] += dg
        ddb = dd.astype(BF16)
        dd_ref[...] = ddb
        du = _dot_nt(ddb, w_ref[...].reshape(D_FF, D))
        da_ref[...] = (du * (2.0 * jnp.sqrt(u_ref[...].astype(F32)))).astype(BF16)

    return _pcall(body, name="bwd_mlp_down", grid=(T // tm,),
                  in_specs=[_rows(tm, D), _rows(tm, D), _full((1, D)), _chunks(w2[1]), _rows(tm, D_FF)],
                  out_specs=[_rows(tm, D), _rows(tm, D_FF), _full((1, D))],
                  out_shape=[SDS((T, D), BF16), SDS((T, D_FF), BF16), SDS((1, D), F32)],
                  sem=("arbitrary",))(dx2, d, g4, w2[0], u)


def _bwd_mlp_up(da, w1, x1, g3, dx2, tm):
    T = dx2.shape[0]

    def body(da_ref, w_ref, x_ref, g_ref, dx2_ref, dx1_ref, dg_ref):
        @pl.when(pl.program_id(0) == 0)
        def _():
            dg_ref[...] = jnp.zeros_like(dg_ref)

        dh = _dot_nt(da_ref[:, 0:D], w_ref[0])
        for j in range(1, N_CHIPS):
            dh += _dot_nt(da_ref[:, j * D:(j + 1) * D], w_ref[j])
        dx, dg = _rms_bwd(dh, x_ref[...], g_ref[...])
        dg_ref[...] += dg
        dx1_ref[...] = dx2_ref[...] + dx

    return _pcall(body, name="bwd_mlp_up", grid=(T // tm,),
                  in_specs=[_rows(tm, D_FF), _chunks(w1[1]), _rows(tm, D), _full((1, D)), _rows(tm, D)],
                  out_specs=[_rows(tm, D), _full((1, D))],
                  out_shape=[SDS((T, D), F32), SDS((1, D), F32)], sem=("arbitrary",))(da, w1[0], x1, g3, dx2)


def _bwd_mix(dx1, yo, g2, wo, z, au, bu, wmu, wsu, om, tm):
    T = dx1.shape[0]

    def body(dx_ref, yo_ref, g_ref, wo_ref, ga_ref, gb_ref, au_ref, bu_ref, wmu_ref, wsu_ref, om_ref,
             dyo_ref, dg_ref, dau_ref, dbu_ref, dga_ref, dgb_ref, dos_ref, dl_ref, dot_ref):
        @pl.when(pl.program_id(0) == 0)
        def _():
            dg_ref[...] = jnp.zeros_like(dg_ref)

        dyo, dg = _rms_bwd(dx_ref[...], yo_ref[...], g_ref[...])
        dg_ref[...] += dg
        dyob = dyo.astype(BF16)
        dyo_ref[...] = dyob
        dy = _dot_nt(dyob, wo_ref[...])
        sa = _sigmoid(ga_ref[...])
        sb = _sigmoid(gb_ref[...])
        dau = (dy * sa).astype(BF16)
        dbu = (dy * sb).astype(BF16)
        dau_ref[...] = dau
        dbu_ref[...] = dbu
        dga_ref[...] = (dy * au_ref[...] * sa * (1.0 - sa)).astype(BF16)
        dgb_ref[...] = (dy * bu_ref[...] * sb * (1.0 - sb)).astype(BF16)
        dom = _dot_nt(dau, wmu_ref[...])
        dos_ref[...] = _dot_nt(dbu, wsu_ref[...])
        prod = dom * om_ref[...]
        sub = lax.broadcasted_iota(jnp.int32, (8, tm), 0)
        for pr in range(MLA_HEADS // 2):
            sl = slice(LANES * pr, LANES * (pr + 1))
            pt = prod[:, sl].T
            d0 = jnp.sum(pt[0:64], axis=0, keepdims=True)
            d1 = jnp.sum(pt[64:128], axis=0, keepdims=True)
            dl_ref[pr, 0] = jnp.where(sub == 0, d0, jnp.where(sub == 1, d1, 0.0))
            dot_ref[0, sl, :] = dom[:, sl].T.astype(BF16)

    r = _rows(tm, D)
    w = _full((D, D))
    return _pcall(body, name="bwd_mix", grid=(T // tm,),
                  in_specs=[r, r, _full((1, D)), w, _rows(tm, D, 1), _rows(tm, D, 2), r, r, w, w, r],
                  out_specs=[r, _full((1, D)), r, r, r, r, r, pl.BlockSpec((MLA_HEADS // 2, 1, 8, tm), lambda i: (0, i, 0, 0)),
                             pl.BlockSpec((1, D, tm), lambda i: (i, 0, 0))],
                  out_shape=[SDS((T, D), BF16), SDS((1, D), F32), SDS((T, D), BF16), SDS((T, D), BF16), SDS((T, D), BF16),
                             SDS((T, D), BF16), SDS((T, D), F32), SDS((MLA_HEADS // 2, T // tm, 8, tm), F32),
                             SDS((T // tm, D, tm), BF16)],
                  sem=("arbitrary",))(dx1, yo, g2, wo, z, z, au, bu, wmu, wsu, om)


def _mla_bwd(qt, km, kt, vm, dot, lse, delta, tb):
    T = km.shape[0]
    nb = T // tb
    cc = ATT_COLS

    per = 2 if nb % 2 == 0 else 1

    def body(qt_ref, k_ref, kt_ref, v_ref, dot_ref, l_ref, dl_ref, dqt_ref, dkt_ref, dvt_ref,
             s_ref, dp_ref, p_ref, ds_ref, vh_ref):
        @pl.when(pl.program_id(1) == 0)
        def _():
            dqt_ref[...] = jnp.zeros_like(dqt_ref)

        dkt_ref[...] = jnp.zeros_like(dkt_ref)
        dvt_ref[...] = jnp.zeros_like(dvt_ref)
        for blk in range(per):
            one_block(per * pl.program_id(1) + blk, blk, qt_ref, k_ref, kt_ref, v_ref, dot_ref, l_ref, dl_ref, dqt_ref, dkt_ref,
                      dvt_ref, s_ref, dp_ref, p_ref, ds_ref, vh_ref)

    def one_block(j, blk, qt_ref, k_ref, kt_ref, v_ref, dot_ref, l_ref, dl_ref, dqt_ref, dkt_ref, dvt_ref,
                  s_ref, dp_ref, p_ref, ds_ref, vh_ref):
        lo = lax.broadcasted_iota(jnp.int32, (tb, LANES), 1) < 64
        key = lax.broadcasted_iota(jnp.int32, (tb, cc), 0)
        qry = lax.broadcasted_iota(jnp.int32, (tb, cc), 1)
        rows_j = slice(tb * blk, tb * (blk + 1))
        v = v_ref[rows_j, :]
        vh_ref[0] = jnp.where(lo, v, jnp.zeros_like(v))
        vh_ref[1] = jnp.where(lo, jnp.zeros_like(v), v)

        def scores(i, slot):
            for hh in range(2):
                sl = slice(LANES * hh, LANES * (hh + 1))
                s_ref[slot, hh] = _dot(k_ref[rows_j, sl], qt_ref[i, sl, :])
                dp_ref[slot, hh] = _dot(vh_ref[hh], dot_ref[i])

        def grads(i, slot, diagonal):
            lse_i = l_ref[0, i]
            delta_i = dl_ref[0, i]
            for hh in range(2):
                for c in range(tb // cc):
                    cols = slice(cc * c, cc * (c + 1))
                    p = jnp.exp2(s_ref[slot, hh, :, cols] * MLA_LOG2_SCALE - lse_i[hh:hh + 1, cols])
                    if diagonal:
                        p = jnp.where(key <= qry + cc * c, p, 0.0)
                    p_ref[hh, :, cols] = p.astype(BF16)
                    ds_ref[hh, :, cols] = (p * (dp_ref[slot, hh, :, cols] - delta_i[hh:hh + 1, cols]) * MLA_SCALE).astype(BF16)
            for hh in range(2):
                sl = slice(LANES * hh, LANES * (hh + 1))
                half = slice(64 * hh, 64 * (hh + 1))
                dvt_ref[blk, half, :] += _dot_nt(dot_ref[i, half, :], p_ref[hh])
                real = slice(LANES * hh, LANES * hh + MLA_NOPE + MLA_ROPE)
                dkt_ref[blk, real, :] += _dot_nt(qt_ref[i, real, :], ds_ref[hh])
                dqt_ref[i, real, :] += _dot(kt_ref[blk, real, :], ds_ref[hh])

        n_off = nb - 1 - j

        def step(u, carry):
            i0 = j + 1 + 2 * u
            scores(i0 + 1, 1)
            grads(i0, 0, False)
            scores(jnp.where(i0 + 2 < nb, i0 + 2, j), 0)
            grads(i0 + 1, 1, False)
            return carry

        scores(jnp.where(n_off > 0, j + 1, j), 0)
        lax.fori_loop(0, n_off // 2, step, 0)

        @pl.when(n_off % 2 == 1)
        def _():
            scores(j, 1)
            grads(nb - 1, 0, False)
            grads(j, 1, True)

        @pl.when(n_off % 2 == 0)
        def _():
            grads(j, 0, True)

    blk = lambda w: pl.BlockSpec((per * tb, w), lambda p, j: (j, p))
    stat = pl.BlockSpec((1, nb, 8, tb), lambda p, j: (p, 0, 0, 0))
    pair_t = lambda w: pl.BlockSpec((nb, w, tb), lambda p, j: (0, p, 0))
    blk_t = lambda w: pl.BlockSpec((per, w, tb), lambda p, j: (j, p, 0))
    return _pcall(body, name="mla_bwd", grid=(MLA_HEADS // 2, nb // per),
                  in_specs=[pair_t(256), blk(256), blk_t(256), blk(LANES), pair_t(LANES), stat, stat],
                  out_specs=[pair_t(256), blk_t(256), blk_t(LANES)],
                  out_shape=[SDS((nb, 2048, tb), F32), SDS((nb, 2048, tb), F32), SDS((nb, D, tb), F32)],
                  scratch=[pltpu.VMEM((2, 2, tb, tb), F32), pltpu.VMEM((2, 2, tb, tb), F32), pltpu.VMEM((2, tb, tb), BF16),
                           pltpu.VMEM((2, tb, tb), BF16), pltpu.VMEM((2, tb, LANES), BF16)],
                  sem=("parallel", "arbitrary"))(qt, km, kt, vm, dot, lse, delta)


def _swa_bwd(sinks, qs, ks, vs, do, o, lse):
    T = qs.shape[0]
    nb, cur, prev = _swa_specs(T)

    def body(sink_ref, q_ref, kc_ref, kp_ref, vc_ref, vp_ref, do_ref, o_ref, l_ref,
             dq_ref, dkc_ref, dkp_ref, dvc_ref, dvp_ref, dsink_ref, kb_ref, vb_ref, s_ref, dp_ref, p_ref, ds_ref):
        n = pl.program_id(0)

        @pl.when(n == 0)
        def _():
            dsink_ref[...] = jnp.zeros_like(dsink_ref)

        mask = _swa_mask(n)
        lo = lax.broadcasted_iota(jnp.int32, (WINDOW, LANES), 1) < 64
        hi = jnp.logical_not(lo)
        lane8 = lax.broadcasted_iota(jnp.int32, (8, LANES), 1)
        for g in range(2):
            gs = slice(LANES * g, LANES * (g + 1))
            kb_ref[g] = jnp.concatenate([kp_ref[:, gs], kc_ref[:, gs]], axis=0)
            vb_ref[g] = jnp.concatenate([vp_ref[:, gs], vc_ref[:, gs]], axis=0)

        def head(h):
            sl = slice(LANES * (h // 2), LANES * (h // 2 + 1))
            hm = lo if h % 2 == 0 else hi
            qp = q_ref[:, sl]
            return hm, sl, jnp.where(hm, qp, jnp.zeros_like(qp)), jnp.where(hm, do_ref[:, sl], 0.0).astype(BF16)

        for h in range(SWA_HEADS):
            _, _, qh, dom = head(h)
            s_ref[h] = _dot_nt(qh, kb_ref[h // 8])
            dp_ref[h] = _dot_nt(dom, vb_ref[h // 8])
        dsink = jnp.zeros((8, LANES), F32)
        for h in range(SWA_HEADS):
            hm, sl, _, _ = head(h)
            lse_h = jnp.max(jnp.where(hm, l_ref[:, sl], -jnp.inf), axis=1, keepdims=True)
            delta = jnp.sum(jnp.where(hm, do_ref[:, sl] * o_ref[:, sl], 0.0), axis=1, keepdims=True)
            p = jnp.exp(jnp.where(mask, s_ref[h] * SWA_SCALE, NEG) - lse_h)
            p_ref[h] = p.astype(BF16)
            ds_ref[h] = (p * (dp_ref[h] - delta) * SWA_SCALE).astype(BF16)
            d_sink = -jnp.sum(jnp.exp(sink_ref[h] - lse_h) * delta, axis=0, keepdims=True)
            dsink = dsink + jnp.where(lane8 == h, d_sink, 0.0)
        dsink_ref[...] += dsink
        for g in range(2):
            gs = slice(LANES * g, LANES * (g + 1))
            dkb = jnp.zeros((2 * WINDOW, LANES), F32)
            dvb = jnp.zeros((2 * WINDOW, LANES), F32)
            for j in range(4 * g, 4 * g + 4):
                dqs = []
                for h in (2 * j, 2 * j + 1):
                    _, _, qh, dom = head(h)
                    dvb = dvb + _dot_tn(p_ref[h], dom)
                    dkb = dkb + _dot_tn(ds_ref[h], qh)
                    dqs.append(_dot(ds_ref[h], kb_ref[g]))
                dq_ref[:, LANES * j:LANES * (j + 1)] = jnp.where(lo, dqs[0], dqs[1])
            dkp_ref[:, gs] = dkb[:WINDOW]
            dkc_ref[:, gs] = dkb[WINDOW:]
            dvp_ref[:, gs] = dvb[:WINDOW]
            dvc_ref[:, gs] = dvb[WINDOW:]

    band = pltpu.VMEM((2, 2 * WINDOW, LANES), BF16)
    return _pcall(body, name="swa_bwd", grid=(nb,),
                  in_specs=[pl.BlockSpec(memory_space=pltpu.SMEM), cur(D), cur(256), prev(256), cur(256), prev(256),
                            cur(D), cur(D), cur(D)],
                  out_specs=[cur(D), cur(256), cur(256), cur(256), cur(256), _full((8, LANES))],
                  out_shape=[SDS((T, D), F32), SDS((T, 256), F32), SDS((T, 256), F32), SDS((T, 256), F32), SDS((T, 256), F32),
                             SDS((8, LANES), F32)],
                  scratch=[band, band, pltpu.VMEM((SWA_HEADS, WINDOW, 2 * WINDOW), F32),
                           pltpu.VMEM((SWA_HEADS, WINDOW, 2 * WINDOW), F32), pltpu.VMEM((SWA_HEADS, WINDOW, 2 * WINDOW), BF16),
                           pltpu.VMEM((SWA_HEADS, WINDOW, 2 * WINDOW), BF16)],
                  sem=("arbitrary",))(sinks, qs, ks, ks, vs, vs, do, o, lse)


def _bwd_qkv(dqm, dkm, dvm, dqs, dkc, dkp, dvc, dvp, z, gq, gkv, wqb, wkn, wv, tab_m, tab_s):
    T = z.shape[0]
    tm = WINDOW
    nb = T // tm
    per = dqm.shape[2] // tm

    tab_mt = [t.T for t in tab_m]
    half = MLA_ROPE // 2

    def rope_t(v, c, a, b):
        return v * c + pltpu.roll(v, LANES - half, 0) * a + pltpu.roll(v, half, 0) * b

    def rms_bwd_t(dy, x, g):
        r = lax.rsqrt(jnp.mean(x * x, axis=0, keepdims=True) + EPS)
        xn = x * r
        dn = dy * g
        return r * (dn - xn * jnp.mean(dn * xn, axis=0, keepdims=True)), jnp.sum(dy * xn, axis=1, keepdims=True)

    def body(dqm_ref, dkm_ref, dvm_ref, dqs_ref, dkc_ref, dkp_ref, dvc_ref, dvp_ref, qa_ref, kva_ref, gq_ref, gkv_ref,
             wqb_ref, wkn_ref, wv_ref, cmt_ref, amt_ref, bmt_ref, cs_ref, as_ref, bs_ref,
             dq_out, dkn_out, dv_out, dsq_ref, drest_ref, dgq_ref, dgkv_ref):
        i = pl.program_id(0)

        @pl.when(i == 0)
        def _():
            dgq_ref[...] = jnp.zeros_like(dgq_ref)
            dgkv_ref[...] = jnp.zeros_like(dgkv_ref)

        cmt, amt, bmt = cmt_ref[...], -amt_ref[...], -bmt_ref[...]
        cs, as_, bs = cs_ref[...], -as_ref[...], -bs_ref[...]
        row = lax.broadcasted_iota(jnp.int32, (LANES, tm), 0)
        nope = row < MLA_NOPE
        roped = jnp.logical_and(row >= MLA_NOPE, row < MLA_NOPE + MLA_ROPE)
        dkr = jnp.zeros((LANES, tm), F32)
        for h in range(MLA_HEADS):
            sl = slice(LANES * h, LANES * (h + 1))
            dq_out[0, sl, :] = rope_t(dqm_ref[0, sl, :], cmt, amt, bmt).astype(BF16)
            dk_h = dkm_ref[0, sl, :]
            dkn_out[0, sl, :] = jnp.where(nope, dk_h, 0.0).astype(BF16)
            dkr = dkr + jnp.where(roped, dk_h, 0.0)
        dv_out[0] = dvm_ref[0].astype(BF16)
        dqn = _dot(wqb_ref[...], dq_out[0])
        dkvn = _dot(wkn_ref[...], dkn_out[0]) + _dot(wv_ref[...], dv_out[0])
        dqa, dgq = rms_bwd_t(dqn, qa_ref[...].T, gq_ref[...])
        dkva, dgkv = rms_bwd_t(dkvn, kva_ref[...].T, gkv_ref[...])
        dgq_ref[...] += dgq
        dgkv_ref[...] += dgkv
        for j in range(D // LANES):
            sl = slice(LANES * j, LANES * (j + 1))
            dsq_ref[:, sl] = _rope(dqs_ref[:, sl], cs, as_, bs, SWA_HD // 2).astype(BF16)
        keep = (i < nb - 1).astype(F32)
        drest_ref[:, 0:256] = dqa.T.astype(BF16)
        for j in range(2):
            sl = slice(LANES * j, LANES * (j + 1))
            dk = dkc_ref[:, sl] + keep * dkp_ref[:, sl]
            drest_ref[:, 256 + LANES * j:256 + LANES * (j + 1)] = _rope(dk, cs, as_, bs, SWA_HD // 2).astype(BF16)
        drest_ref[:, 512:768] = (dvc_ref[...] + keep * dvp_ref[...]).astype(BF16)
        drest_ref[:, 768:896] = dkva.T.astype(BF16)
        drest_ref[:, 896:1024] = rope_t(dkr, cmt, amt, bmt).T.astype(BF16)

    nxt = pl.BlockSpec((tm, 256), lambda i: (jnp.minimum(i + 1, nb - 1), 0))
    tab = [_rows(tm, LANES)] * 3
    tab_t = [pl.BlockSpec((LANES, tm), lambda i: (0, i))] * 3
    blk_t = lambda w: pl.BlockSpec((1, w, tm), lambda i: (i // per, 0, i % per))
    return _pcall(body, name="bwd_qkv", grid=(nb,),
                  in_specs=[blk_t(2048), blk_t(2048), blk_t(1024), _rows(tm, 1024), _rows(tm, 256), nxt,
                            _rows(tm, 256), nxt, _rows(tm, 256, 12), _rows(tm, 128, 30), _full((Q_LORA, 1)), _full((KV_LORA, 1)),
                            _full((Q_LORA, 2048)), _full((KV_LORA, 2048)), _full((KV_LORA, 1024))] + tab_t + tab,
                  out_specs=[blk_t(2048), blk_t(2048), blk_t(1024), _rows(tm, 1024), _rows(tm, 1024),
                             _full((Q_LORA, 1)), _full((KV_LORA, 1))],
                  out_shape=[SDS(dqm.shape, BF16), SDS(dkm.shape, BF16), SDS(dvm.shape, BF16), SDS((T, 1024), BF16),
                             SDS((T, 1024), BF16), SDS((Q_LORA, 1), F32), SDS((KV_LORA, 1), F32)],
                  sem=("arbitrary",))(dqm, dkm, dvm, dqs, dkc, dkp, dvc, dvp, z, z, gq.reshape(Q_LORA, 1),
                                      gkv.reshape(KV_LORA, 1), wqb, wkn, wv, *tab_mt, *tab_s)


def _bwd_in(dsq, dga, dgb, drest, w_in_p, x, g1, dx1, tm):
    T = x.shape[0]

    def body(a_ref, b_ref, c_ref, d_ref, w_ref, x_ref, g_ref, dx1_ref, dx_ref, dg_ref):
        @pl.when(pl.program_id(0) == 0)
        def _():
            dg_ref[...] = jnp.zeros_like(dg_ref)

        dh = (_dot_nt(a_ref[...], w_ref[:, 0:1024]) + _dot_nt(b_ref[...], w_ref[:, 1024:2048])
              + _dot_nt(c_ref[...], w_ref[:, 2048:3072]) + _dot_nt(d_ref[...], w_ref[:, 3072:4096]))
        dx, dg = _rms_bwd(dh, x_ref[...], g_ref[...])
        dg_ref[...] += dg
        dx_ref[...] = dx1_ref[...] + dx

    r = _rows(tm, D)
    return _pcall(body, name="bwd_in", grid=(T // tm,),
                  in_specs=[r, r, r, r, _full((D, NZ)), r, _full((1, D)), r],
                  out_specs=[r, _full((1, D))], out_shape=[SDS((T, D), F32), SDS((1, D), F32)],
                  sem=("arbitrary",))(dsq, dga, dgb, drest, w_in_p, x, g1, dx1)


def _wgrad(a, g, name, into=None):
    T, K = a.shape
    N = g.shape[1]
    tk, tn, tt = min(K, 1024), min(N, 1024), min(T, 1024)
    if into is not None:
        buf, weight = into
        _, row0, lane0 = PACK_AT[weight]
        shard = {n: (r, c) for n, r, c in BIG}[weight]
        assert lane0 == 0 and shard[1] == D and tk % shard[0] == 0
        per_step = tk // shard[0]
    assert K % tk == 0 and N % tn == 0 and T % tt == 0, (a.shape, g.shape)
    steps = T // tt

    def body(a_ref, g_ref, *rest):
        o_ref, acc_ref = rest[-2:]
        t = pl.program_id(2)

        @pl.when(t == 0)
        def _():
            acc_ref[...] = jnp.zeros_like(acc_ref)

        acc_ref[...] += _dot_tn(a_ref[...].astype(BF16), g_ref[...].astype(BF16))

        @pl.when(t == steps - 1)
        def _():
            o_ref[...] = acc_ref[...].astype(o_ref.dtype).reshape(o_ref.shape)

    in_specs = [pl.BlockSpec((tt, tk), lambda k, n, t: (t, k)), pl.BlockSpec((tt, tn), lambda k, n, t: (t, n))]
    if into is None:
        return _pcall(body, name=name, grid=(K // tk, N // tn, steps), in_specs=in_specs,
                      out_specs=pl.BlockSpec((tk, tn), lambda k, n, t: (k, n)), out_shape=SDS((K, N), F32),
                      scratch=[pltpu.VMEM((tk, tn), F32)], sem=("parallel", "parallel", "arbitrary"))(a, g)
    assert row0 % shard[0] == 0 and (K // tk) * (N // tn) * per_step == N_CHIPS
    return _pcall(body, name=name, grid=(K // tk, N // tn, steps), in_specs=in_specs + [ANY],
                  out_specs=pl.BlockSpec((per_step, shard[0], tn), lambda k, n, t: (k + n, row0 // shard[0], 0)),
                  out_shape=SDS(buf.shape, buf.dtype),
                  scratch=[pltpu.VMEM((tk, tn), F32)], sem=("parallel", "parallel", "arbitrary"), aliases={2: 0})(a, g, buf)


def _wgrad_t(at, g, name):
    nblk, K, tt = at.shape
    N = g.shape[1]
    tk = min(K, 1024)
    per_step = 4 if nblk % 4 == 0 else 1
    assert K % tk == 0 and g.shape[0] == nblk * tt

    def body(a_ref, g_ref, o_ref):
        @pl.when(pl.program_id(1) == 0)
        def _():
            o_ref[...] = jnp.zeros_like(o_ref)

        acc = _dot(a_ref[0], g_ref[0:tt, :].astype(BF16))
        for b in range(1, per_step):
            acc = acc + _dot(a_ref[b], g_ref[tt * b:tt * (b + 1), :].astype(BF16))
        o_ref[...] += acc

    return _pcall(body, name=name, grid=(K // tk, nblk // per_step),
                  in_specs=[pl.BlockSpec((per_step, tk, tt), lambda k, t: (t, k, 0)),
                            pl.BlockSpec((per_step * tt, N), lambda k, t: (t, 0))],
                  out_specs=pl.BlockSpec((tk, N), lambda k, t: (k, 0)), out_shape=SDS((K, N), F32),
                  sem=("parallel", "arbitrary"))(at, g)


def _adamw(w, packed_g, m, v, name, transposed=False):
    R, C = w.shape[1:][::-1] if transposed else w.shape[1:]
    _, row0, lane0 = PACK_AT[name]
    tr = min(R, 256 if row0 % 256 == 0 else 128)
    assert row0 % tr == 0 and R % tr == 0

    def body(w_ref, g_ref, m_ref, v_ref, go_ref, d_ref, m2_ref, v2_ref):
        g_ = g_ref[...].T[lane0:lane0 + C] if transposed else g_ref[:, lane0:lane0 + C]
        go_ref[0] = g_
        m2 = ADAM_B1 * m_ref[0] + (1.0 - ADAM_B1) * g_
        v2 = ADAM_B2 * v_ref[0] + (1.0 - ADAM_B2) * jnp.square(g_)
        m_hat = m2 / (1.0 - ADAM_B1 ** ADAM_STEP)
        v_hat = v2 / (1.0 - ADAM_B2 ** ADAM_STEP)
        d_ref[0] = -ADAM_LR * (m_hat / (jnp.sqrt(v_hat) + ADAM_EPS) + ADAM_WD * w_ref[0])
        m2_ref[0] = m2
        v2_ref[0] = v2

    r = pl.BlockSpec((1, C, tr), lambda i: (0, 0, i)) if transposed else pl.BlockSpec((1, tr, C), lambda i: (0, i, 0))
    return _pcall(body, name="adamw_" + name, grid=(R // tr,),
                  in_specs=[r, pl.BlockSpec((tr, D), lambda i: (row0 // tr + i, 0)), r, r], out_specs=[r] * 4,
                  out_shape=[SDS(w.shape, F32)] * 4, sem=("parallel",))(w, packed_g, m, v)


def _adamw_small(w, parts, m, v):
    def body(w_ref, p_ref, m_ref, v_ref, g_ref, d_ref, m2_ref, v2_ref):
        g_ = p_ref[0]
        for k in range(1, N_DEV):
            g_ = g_ + p_ref[k]
        g_ref[...] = g_
        m2 = ADAM_B1 * m_ref[...] + (1.0 - ADAM_B1) * g_
        v2 = ADAM_B2 * v_ref[...] + (1.0 - ADAM_B2) * jnp.square(g_)
        m_hat = m2 / (1.0 - ADAM_B1 ** ADAM_STEP)
        v_hat = v2 / (1.0 - ADAM_B2 ** ADAM_STEP)
        d_ref[...] = -ADAM_LR * (m_hat / (jnp.sqrt(v_hat) + ADAM_EPS) + ADAM_WD * w_ref[...])
        m2_ref[...] = m2
        v2_ref[...] = v2

    s = _full((8, D))
    return _pcall(body, name="adamw_small", grid=(1,), in_specs=[s, _full((N_DEV, 8, D)), s, s], out_specs=[s] * 4,
                  out_shape=[SDS((8, D), F32)] * 4, sem=("arbitrary",))(w, parts, m, v)


ANY = pl.BlockSpec(memory_space=pl.ANY)


def _place():
    x, y, c = lax.axis_index("x"), lax.axis_index("y"), lax.axis_index("c")
    chips = [(1 - x, y), (x, 1 - y), (1 - x, 1 - y)]
    return x, y, c, chips


def _all_gather(wpk):
    rows = wpk.shape[0]
    HALF = rows // 2
    assert HALF % 16 == 0

    def body(in_ref, out_ref, send_sems, recv_sems):
        x, y, c, chips = _place()
        half = pl.ds(pl.multiple_of(c * HALF, 16), HALF)
        other = pl.ds(pl.multiple_of((1 - c) * HALF, 16), HALF)

        def copy(k, src, dst, to):
            return pltpu.make_async_remote_copy(src_ref=src, dst_ref=dst, send_sem=send_sems.at[k], recv_sem=recv_sems.at[k],
                                                device_id=to, device_id_type=MESH)

        first = [copy(k, in_ref.at[half], out_ref.at[2 * x + y, half], (cx, cy, c)) for k, (cx, cy) in enumerate(chips)]
        first.append(copy(6, in_ref, out_ref.at[2 * x + y], (x, y, 1 - c)))
        for cp in first:
            cp.start()
        passed = []
        for k, (cx, cy) in enumerate(chips):
            slot = out_ref.at[2 * cx + cy, half]
            copy(k, slot, slot, (x, y, c)).wait_recv()
            fwd = copy(3 + k, slot, slot, (x, y, 1 - c))
            fwd.start()
            passed.append(fwd)
        for k, (cx, cy) in enumerate(chips):
            slot = out_ref.at[2 * cx + cy, other]
            copy(3 + k, slot, slot, (x, y, c)).wait_recv()
        copy(6, in_ref, out_ref.at[2 * x + y], (x, y, c)).wait_recv()
        for cp in first + passed:
            cp.wait_send()

    return _pcall(body, name="all_gather_weights", in_specs=[ANY], out_specs=ANY,
                  out_shape=SDS((N_CHIPS, rows, D), BF16),
                  scratch=[pltpu.SemaphoreType.DMA((7,)), pltpu.SemaphoreType.DMA((7,))])(wpk)


HBM = pl.BlockSpec(memory_space=pltpu.HBM)
SEM = pl.BlockSpec(memory_space=pltpu.SEMAPHORE)
DATAFLOW = pltpu.SideEffectType.DATAFLOW_SIDE_EFFECTING


def _in_hbm(a):
    return pltpu.with_memory_space_constraint(a, pltpu.HBM)


def _gather_late_start(wpk, after):
    rows = wpk.shape[0]

    def body(in_ref, land_ref, after_ref, send_sems, recv_sems, in_thru, land_thru, token):
        x, y, c, chips = _place()
        for k, to in enumerate([(cx, cy, c) for cx, cy in chips] + [(x, y, 1 - c)]):
            pltpu.make_async_remote_copy(src_ref=in_ref, dst_ref=land_ref.at[2 * x + y], send_sem=send_sems.at[k],
                                         recv_sem=recv_sems.at[k], device_id=to, device_id_type=MESH).start()
        token[...] = jnp.zeros_like(token)

    return pl.pallas_call(
        body, name="gather_late_start",
        out_shape=(pltpu.SemaphoreType.DMA((4,)), pltpu.SemaphoreType.DMA((4,)), pltpu.HBM(wpk.shape, wpk.dtype),
                   pltpu.HBM((N_CHIPS, rows, D), wpk.dtype), SDS((8, LANES), F32)),
        in_specs=(HBM, HBM, ANY), out_specs=(SEM, SEM, HBM, HBM, pl.BlockSpec(memory_space=pltpu.VMEM)),
        input_output_aliases={0: 2, 1: 3}, compiler_params=pltpu.CompilerParams(has_side_effects=DATAFLOW),
    )(_in_hbm(wpk), _in_hbm(lax.empty((N_CHIPS, rows, D), wpk.dtype)), after)


def _gather_late_wait(send_sems, recv_sems, in_thru, land_thru, after):
    def body(in_ref, land_ref, send_sems, recv_sems, after_ref, after2_ref, in_dead, got_ref):
        x, y, c, chips = _place()
        for k, (sx, sy) in enumerate(chips + [(x, y)]):
            cp = pltpu.make_async_remote_copy(src_ref=in_ref, dst_ref=land_ref.at[2 * sx + sy], send_sem=send_sems.at[k],
                                              recv_sem=recv_sems.at[k], device_id=(x, y, c), device_id_type=MESH)
            cp.wait_send()
            cp.wait_recv()

    return pl.pallas_call(
        body, name="gather_late_wait",
        out_shape=(pltpu.HBM(in_thru.shape, in_thru.dtype), pltpu.HBM(land_thru.shape, land_thru.dtype)),
        in_specs=(HBM, HBM, SEM, SEM, ANY, ANY), out_specs=(HBM, HBM), input_output_aliases={0: 0, 1: 1},
        compiler_params=pltpu.CompilerParams(has_side_effects=DATAFLOW),
    )(in_thru, land_thru, send_sems, recv_sems, *after)[1]


def _rs_sibling(gpk):
    HALF = gpk.shape[1] // 2

    def body(in_ref, out_ref, send_sem, recv_sem):
        x, y, c, _ = _place()
        theirs = pl.ds(pl.multiple_of((1 - c) * HALF, 16), HALF)
        cp = pltpu.make_async_remote_copy(src_ref=in_ref.at[:, theirs], dst_ref=out_ref, send_sem=send_sem, recv_sem=recv_sem,
                                          device_id=(x, y, 1 - c), device_id_type=MESH)
        cp.start()
        cp.wait()

    return _pcall(body, name="rs_sibling", in_specs=[ANY], out_specs=ANY, out_shape=SDS((N_CHIPS, HALF, D), gpk.dtype),
                  scratch=[pltpu.SemaphoreType.DMA, pltpu.SemaphoreType.DMA])(gpk)


def _rs_add_sibling(cidx, gpk, got):
    HALF = got.shape[1]
    th = HALF // 4
    nh = HALF // th
    assert th % 16 == 0

    def body(c_ref, a_ref, b_ref, o_ref):
        o_ref[...] = (a_ref[...].astype(F32) + b_ref[...].astype(F32)).astype(BF16)

    gs = pltpu.PrefetchScalarGridSpec(
        num_scalar_prefetch=1, grid=(N_CHIPS, nh),
        in_specs=[pl.BlockSpec((1, th, D), lambda j, i, c: (j, c[0] * nh + i, 0)), pl.BlockSpec((1, th, D), lambda j, i, c: (j, i, 0))],
        out_specs=pl.BlockSpec((1, th, D), lambda j, i, c: (j, i, 0)))
    return pl.pallas_call(body, name="rs_add_sibling", grid_spec=gs, out_shape=SDS((N_CHIPS, HALF, D), BF16),
                          compiler_params=pltpu.CompilerParams(dimension_semantics=("parallel", "parallel"),
                                                               vmem_limit_bytes=48 << 20))(cidx, gpk, got)


def _rs_sibling_add(gpk):
    HALF = gpk.shape[1] // 2
    tr = 32
    assert HALF % tr == 0
    buf = pltpu.VMEM((N_CHIPS, HALF, D), BF16)
    sems = pltpu.SemaphoreType.DMA((N_CHIPS,))

    def body(in_ref, out_ref, own_v, got_v, sum_v, send_sems, recv_sems, load_sems, store_sems):
        x, y, c, _ = _place()
        mine = pl.ds(pl.multiple_of(c * HALF, 16), HALF)
        theirs = pl.ds(pl.multiple_of((1 - c) * HALF, 16), HALF)
        sends = [pltpu.make_async_remote_copy(src_ref=in_ref.at[j, theirs], dst_ref=got_v.at[j], send_sem=send_sems.at[j],
                                              recv_sem=recv_sems.at[j], device_id=(x, y, 1 - c), device_id_type=MESH)
                 for j in range(N_CHIPS)]
        loads = [pltpu.make_async_copy(in_ref.at[j, mine], own_v.at[j], load_sems.at[j]) for j in range(N_CHIPS)]
        stores = [pltpu.make_async_copy(sum_v.at[j], out_ref.at[j], store_sems.at[j]) for j in range(N_CHIPS)]
        for j in range(N_CHIPS):
            sends[j].start()
            loads[j].start()
        for j in range(N_CHIPS):
            sends[j].wait_recv()
            loads[j].wait()

            def tile(i, carry, j=j):
                r = pl.ds(pl.multiple_of(i * tr, tr), tr)
                sum_v[j, r] = (own_v[j, r].astype(F32) + got_v[j, r].astype(F32)).astype(BF16)
                return carry

            lax.fori_loop(0, HALF // tr, tile, 0)
            stores[j].start()
        for j in range(N_CHIPS):
            sends[j].wait_send()
            stores[j].wait()

    return _pcall(body, name="rs_sibling_add", in_specs=[ANY], out_specs=ANY, out_shape=SDS((N_CHIPS, HALF, D), BF16),
                  scratch=[buf, buf, buf, sems, sems, sems, sems])(gpk)


def _rs_chips_start(part, small, after):
    def body(p_ref, s_ref, land_ref, sland_ref, after_ref, send_sems, recv_sems, p_thru, s_thru, land_thru, sland_thru, token):
        x, y, c, chips = _place()
        for k, (cx, cy) in enumerate(chips):
            pltpu.make_async_remote_copy(src_ref=p_ref.at[2 * cx + cy], dst_ref=land_ref.at[2 * x + y], send_sem=send_sems.at[k],
                                         recv_sem=recv_sems.at[k], device_id=(cx, cy, c), device_id_type=MESH).start()
        peers = [(x, y, 1 - c)] + [(cx, cy, c) for cx, cy in chips] + [(cx, cy, 1 - c) for cx, cy in chips]
        for k, to in enumerate(peers):
            pltpu.make_async_remote_copy(src_ref=s_ref, dst_ref=sland_ref.at[4 * x + 2 * y + c], send_sem=send_sems.at[3 + k],
                                         recv_sem=recv_sems.at[3 + k], device_id=to, device_id_type=MESH).start()
        token[...] = jnp.zeros_like(token)

    return pl.pallas_call(
        body, name="rs_chips_start",
        out_shape=(pltpu.SemaphoreType.DMA((10,)), pltpu.SemaphoreType.DMA((10,)), pltpu.HBM(part.shape, part.dtype),
                   pltpu.HBM(small.shape, small.dtype), pltpu.HBM(part.shape, part.dtype), pltpu.HBM((N_DEV, 8, D), F32),
                   SDS((8, LANES), F32)),
        in_specs=(HBM, HBM, HBM, HBM, ANY), out_specs=(SEM, SEM, HBM, HBM, HBM, HBM, pl.BlockSpec(memory_space=pltpu.VMEM)),
        input_output_aliases={0: 2, 1: 3, 2: 4, 3: 5}, compiler_params=pltpu.CompilerParams(has_side_effects=DATAFLOW),
    )(_in_hbm(part), _in_hbm(small), _in_hbm(lax.empty(part.shape, part.dtype)), _in_hbm(lax.empty((N_DEV, 8, D), F32)), after)


def _rs_chips_wait(send_sems, recv_sems, p_thru, s_thru, land_thru, sland_thru, after):
    def body(p_ref, s_ref, land_ref, sland_ref, send_sems, recv_sems, *after_and_outputs):
        x, y, c, chips = _place()
        for k, (cx, cy) in enumerate(chips):
            cp = pltpu.make_async_remote_copy(src_ref=p_ref.at[0], dst_ref=land_ref.at[2 * cx + cy], send_sem=send_sems.at[k],
                                              recv_sem=recv_sems.at[k], device_id=(cx, cy, c), device_id_type=MESH)
            cp.wait_send()
            cp.wait_recv()
        peers = [(x, y, 1 - c)] + [(cx, cy, c) for cx, cy in chips] + [(cx, cy, 1 - c) for cx, cy in chips]
        for k, (px, py, pc) in enumerate(peers):
            cp = pltpu.make_async_remote_copy(src_ref=s_ref, dst_ref=sland_ref.at[4 * px + 2 * py + pc], send_sem=send_sems.at[3 + k],
                                              recv_sem=recv_sems.at[3 + k], device_id=(px, py, pc), device_id_type=MESH)
            cp.wait_send()
            cp.wait_recv()

    hbm = lambda a: pltpu.HBM(a.shape, a.dtype)
    outs = pl.pallas_call(
        body, name="rs_chips_wait", out_shape=(hbm(p_thru), hbm(s_thru), hbm(land_thru), hbm(sland_thru)),
        in_specs=(HBM, HBM, HBM, HBM, SEM, SEM) + (ANY,) * len(after), out_specs=(HBM, HBM, HBM, HBM),
        input_output_aliases={0: 0, 1: 1, 2: 2, 3: 3}, compiler_params=pltpu.CompilerParams(has_side_effects=DATAFLOW),
    )(p_thru, s_thru, land_thru, sland_thru, send_sems, recv_sems, *after)
    return outs[0], outs[2], outs[3]


def _rs_add_chips(qidx, part, parts):
    HALF = part.shape[1]
    th = HALF // 4
    nh = HALF // th
    assert th % 16 == 0

    def body(q_ref, own_ref, p_ref, o_ref):
        for me in range(N_CHIPS):
            @pl.when(q_ref[0] == me)
            def _(me=me):
                t = [(own_ref[0] if j == me else p_ref[j]).astype(F32) for j in range(N_CHIPS)]
                o_ref[...] = ((t[0] + t[1]) + t[2]) + t[3]

    gs = pltpu.PrefetchScalarGridSpec(
        num_scalar_prefetch=1, grid=(HALF // th,),
        in_specs=[pl.BlockSpec((1, th, D), lambda i, q: (q[0], i, 0)), pl.BlockSpec((N_CHIPS, th, D), lambda i, q: (0, i, 0))],
        out_specs=pl.BlockSpec((th, D), lambda i, q: (q[1] * nh + i, 0)))
    return pl.pallas_call(body, name="rs_add_chips", grid_spec=gs, out_shape=SDS((2 * HALF, D), F32),
                          compiler_params=pltpu.CompilerParams(dimension_semantics=("parallel",),
                                                               vmem_limit_bytes=48 << 20))(qidx, part, parts)


def _rs_join(shard, name):
    HALF = shard.shape[0] // 2

    def body(in_ref, out_ref, send_sem, recv_sem):
        x, y, c, _ = _place()
        rows = pl.ds(pl.multiple_of(c * HALF, 16), HALF)
        cp = pltpu.make_async_remote_copy(src_ref=in_ref.at[rows], dst_ref=out_ref.at[rows], send_sem=send_sem, recv_sem=recv_sem,
                                          device_id=(x, y, 1 - c), device_id_type=MESH)
        cp.start()
        cp.wait()

    return _pcall(body, name=name, in_specs=[ANY], out_specs=ANY, out_shape=SDS(shard.shape, F32),
                  scratch=[pltpu.SemaphoreType.DMA, pltpu.SemaphoreType.DMA], aliases={0: 0})(shard)


def _reduce_late_start(gpk, after):
    rows = gpk.shape[1]
    HALF = rows // 2
    assert HALF % 16 == 0

    def body(in_ref, land_ref, after_ref, send_sems, recv_sems, in_thru, land_thru, token):
        x, y, c, chips = _place()
        me = 4 * x + 2 * y + c
        peers = [(x, y, 1 - c)] + [(cx, cy, c) for cx, cy in chips] + [(cx, cy, 1 - c) for cx, cy in chips]
        for k, (px, py, pc) in enumerate(peers):
            src = in_ref.at[2 * px + py, pl.ds(pl.multiple_of(pc * HALF, 16), HALF)]
            pltpu.make_async_remote_copy(src_ref=src, dst_ref=land_ref.at[me], send_sem=send_sems.at[k], recv_sem=recv_sems.at[k],
                                         device_id=(px, py, pc), device_id_type=MESH).start()
        token[...] = jnp.zeros_like(token)

    return pl.pallas_call(
        body, name="reduce_late_start",
        out_shape=(pltpu.SemaphoreType.DMA((7,)), pltpu.SemaphoreType.DMA((7,)), pltpu.HBM(gpk.shape, gpk.dtype),
                   pltpu.HBM((N_DEV, HALF, D), gpk.dtype), SDS((8, LANES), F32)),
        in_specs=(HBM, HBM, ANY), out_specs=(SEM, SEM, HBM, HBM, pl.BlockSpec(memory_space=pltpu.VMEM)),
        input_output_aliases={0: 2, 1: 3}, compiler_params=pltpu.CompilerParams(has_side_effects=DATAFLOW),
    )(_in_hbm(gpk), _in_hbm(lax.empty((N_DEV, HALF, D), gpk.dtype)), after)


def _reduce_late_wait(send_sems, recv_sems, in_thru, land_thru, after):
    def body(in_ref, land_ref, send_sems, recv_sems, after_ref, in_out, got_ref):
        x, y, c, chips = _place()
        peers = [(x, y, 1 - c)] + [(cx, cy, c) for cx, cy in chips] + [(cx, cy, 1 - c) for cx, cy in chips]
        for k, (px, py, pc) in enumerate(peers):
            cp = pltpu.make_async_remote_copy(src_ref=land_ref.at[0], dst_ref=land_ref.at[4 * px + 2 * py + pc],
                                              send_sem=send_sems.at[k], recv_sem=recv_sems.at[k],
                                              device_id=(px, py, pc), device_id_type=MESH)
            cp.wait_send()
            cp.wait_recv()

    return pl.pallas_call(
        body, name="reduce_late_wait",
        out_shape=(pltpu.HBM(in_thru.shape, in_thru.dtype), pltpu.HBM(land_thru.shape, land_thru.dtype)),
        in_specs=(HBM, HBM, SEM, SEM, ANY), out_specs=(HBM, HBM), input_output_aliases={0: 0, 1: 1},
        compiler_params=pltpu.CompilerParams(has_side_effects=DATAFLOW),
    )(in_thru, land_thru, send_sems, recv_sems, after)


def _reduce_late_add(didx, gpk, parts):
    HALF = parts.shape[1]
    th = HALF // 4
    nh = HALF // th
    assert th % 16 == 0

    def body(d_ref, own_ref, p_ref, o_ref):
        for me in range(N_DEV):
            @pl.when(d_ref[0] == me)
            def _(me=me):
                t = [(own_ref[0] if j == me else p_ref[j]).astype(F32) for j in range(N_DEV)]
                o_ref[...] = ((((((t[0] + t[1]) + t[2]) + t[3]) + t[4]) + t[5]) + t[6]) + t[7]

    gs = pltpu.PrefetchScalarGridSpec(
        num_scalar_prefetch=1, grid=(nh,),
        in_specs=[pl.BlockSpec((1, th, D), lambda i, d: (d[1], d[2] * nh + i, 0)), pl.BlockSpec((N_DEV, th, D), lambda i, d: (0, i, 0))],
        out_specs=pl.BlockSpec((th, D), lambda i, d: (d[2] * nh + i, 0)))
    return pl.pallas_call(body, name="reduce_late_add", grid_spec=gs, out_shape=SDS((2 * HALF, D), F32),
                          compiler_params=pltpu.CompilerParams(dimension_semantics=("parallel",),
                                                               vmem_limit_bytes=48 << 20))(didx, gpk, parts)


def _reduce_late_add_join(gpk, parts):
    HALF = parts.shape[1]
    nh, tr = 4, 32
    th = HALF // nh
    assert HALF % nh == 0 and th % tr == 0

    def body(in_ref, p_ref, out_ref, in_v, sum_v, load_sems, store_sems, send_sems, recv_sems):
        x, y, c, _ = _place()
        dev = 4 * x + 2 * y + c

        def rows(core, i):
            return pl.ds(pl.multiple_of(core * HALF + i * th, 16), th)

        def landed(i, j):
            return pltpu.make_async_copy(p_ref.at[j, pl.ds(i * th, th)], in_v.at[i, j], load_sems.at[N_DEV * i + j])

        for i in range(nh):
            for j in range(N_DEV):
                @pl.when(dev == j)
                def _(i=i, j=j):
                    pltpu.make_async_copy(in_ref.at[2 * x + y, rows(c, i)], in_v.at[i, j], load_sems.at[N_DEV * i + j]).start()

                @pl.when(dev != j)
                def _(i=i, j=j):
                    landed(i, j).start()
        sends = [pltpu.make_async_remote_copy(src_ref=sum_v.at[i], dst_ref=out_ref.at[rows(c, i)], send_sem=send_sems.at[i],
                                              recv_sem=recv_sems.at[i], device_id=(x, y, 1 - c), device_id_type=MESH)
                 for i in range(nh)]
        stores = [pltpu.make_async_copy(sum_v.at[i], out_ref.at[rows(c, i)], store_sems.at[i]) for i in range(nh)]
        for i in range(nh):
            for j in range(N_DEV):
                landed(i, j).wait()

            def tile(k, carry, i=i):
                r = pl.ds(pl.multiple_of(k * tr, tr), tr)
                t = [in_v[i, j, r].astype(F32) for j in range(N_DEV)]
                sum_v[i, r] = ((((((t[0] + t[1]) + t[2]) + t[3]) + t[4]) + t[5]) + t[6]) + t[7]
                return carry

            lax.fori_loop(0, th // tr, tile, 0)
            sends[i].start()
            stores[i].start()
        for i in range(nh):
            sends[i].wait_recv()
            sends[i].wait_send()
            stores[i].wait()

    return _pcall(body, name="reduce_late_add_join", in_specs=[ANY, ANY], out_specs=ANY, out_shape=SDS((2 * HALF, D), F32),
                  scratch=[pltpu.VMEM((nh, N_DEV, th, D), BF16), pltpu.VMEM((nh, th, D), F32),
                           pltpu.SemaphoreType.DMA((nh * N_DEV,)), pltpu.SemaphoreType.DMA((nh,)),
                           pltpu.SemaphoreType.DMA((nh,)), pltpu.SemaphoreType.DMA((nh,))])(gpk, parts)


def _pack_early(b, dtype):
    lanes = lambda a: jnp.pad(a.astype(dtype), ((0, 0), (0, D - a.shape[1])))
    pair = jnp.concatenate([b["w_q_b"].astype(dtype), b["w_ple"].astype(dtype), jnp.zeros((256, D - 640), dtype)], axis=1)
    return jnp.concatenate([lanes(b["w_in"]), pair, lanes(b["w_kv_b"])], axis=0)


def _pack_late(b, dtype):
    return jnp.concatenate([b[n].astype(dtype) for n in ("w_mla_up", "w_swa_up", "w_out", "w_ple_gate", "w_mlp_up", "w_mlp_down")],
                           axis=0)


def _full_weights(gathered, which):
    out = {}
    for n, r, c in BIG:
        buf, row0, lane0 = PACK_AT[n]
        if buf != which:
            continue
        blk = gathered[:, row0:row0 + r, lane0:lane0 + c]
        if n in ("w_mlp_up", "w_mlp_down"):
            assert (r, c) == (D, D) and row0 % D == 0
            out[n] = (gathered, row0 // D)
        elif n == "w_in":
            out["w_in_p"] = _w_in_internal([blk[j] for j in range(N_CHIPS)])
        elif n in COL_SHARDED:
            out[n] = jnp.swapaxes(blk, 0, 1).reshape(r, N_CHIPS * c)
        else:
            out[n] = blk.reshape(N_CHIPS * r, c)
    return out


def _split_full_grads(grads, pack, dtype):
    shard = {n: (r, c) for n, r, c in BIG}
    chunks = []
    for j in range(N_CHIPS):
        blocks = {}
        for n, g in grads.items():
            if n == "w_in_p":
                blocks["w_in"] = _w_in_grad_shard(g, j)
                continue
            r, c = shard[n]
            blocks[n] = g[:, j * c:(j + 1) * c] if n in COL_SHARDED else g[j * r:(j + 1) * r]
        chunks.append(pack(blocks, dtype))
    return jnp.stack(chunks)


W_IN_SHARD = 936
W_IN_SEGMENTS = ((0, 256, (3072,)), (256, 384, (3840,)), (384, 416, (4032,)), (416, 1440, (0,)), (1440, 1504, (3328, 3392)),
                 (1504, 1568, (3456, 3520)), (1568, 1632, (3584, 3648)), (1632, 1696, (3712, 3776)), (1696, 3744, (1024,)))


def _w_in_internal(shards):
    def cols(a, b):
        out = []
        for j, s in enumerate(shards):
            lo, hi = max(a, W_IN_SHARD * j), min(b, W_IN_SHARD * (j + 1))
            if lo < hi:
                out.append(s[:, lo - W_IN_SHARD * j:hi - W_IN_SHARD * j])
        return out

    pieces = {}
    for a, b, places in W_IN_SEGMENTS:
        for at in places:
            pieces[at] = cols(a, b)
    zeros = lambda n: [jnp.zeros((D, n), shards[0].dtype)]
    pieces[3968] = zeros(64)
    pieces[4064] = zeros(32)
    return jnp.concatenate([piece for at in sorted(pieces) for piece in pieces[at]], axis=1)


def _w_in_grad_shard(g, j):
    def internal(a, b):
        out = []
        while a < b:
            end = min(b, (a // D + 1) * D)
            out.append(g[a // D][:, a % D:a % D + end - a])
            a = end
        return out

    out = []
    for a, b, places in W_IN_SEGMENTS:
        lo, hi = max(a, W_IN_SHARD * j), min(b, W_IN_SHARD * (j + 1))
        if lo < hi:
            parts = [internal(at + lo - a, at + hi - a) for at in places]
            if len(parts) == 1:
                out += parts[0]
            else:
                assert len(parts[0]) == len(parts[1]) == 1
                out.append(parts[0][0] + parts[1][0])
    return jnp.concatenate(out, axis=1)


def _local_step(x, p, tgt, w, small, late_weights, late_grads_out):
    T = x.shape[0]
    tm = 256
    tb = 256
    w_in_p = w["w_in_p"]
    wqb = jnp.pad(w["w_q_b"].reshape(Q_LORA, MLA_HEADS, 96), ((0, 0), (0, 0), (0, 32))).reshape(Q_LORA, 2048)
    wkv = w["w_kv_b"].reshape(KV_LORA, MLA_HEADS, 128)
    wkn = jnp.pad(wkv[:, :, :64], ((0, 0), (0, 0), (0, 64))).reshape(KV_LORA, 2048)
    wv = wkv[:, :, 64:].reshape(KV_LORA, 1024)
    tab_m = _rope_tables(T, "mla")
    tab_s = _rope_tables(T, "swa")
    g1, gq, gkv, sinks = small["g_mix_pre"], small["g_q_a"], small["g_kv_a"], small["sinks"]
    g2, g3, g4, g5 = small["g_mix_post"], small["g_mlp_pre"], small["g_mlp_post"], small["g_ple"]
    sink_vec = sinks.reshape(SWA_HEADS)

    z, h1 = _fwd_in(x, g1, w_in_p, tm)
    qn, kvn, km, vm, qt, kt, vt, qs, ks, vs = _fwd_qkv(z, gq, gkv, wqb, wkn, wv, tab_m, tab_s, tb)
    om, lse_m = _mla_fwd(qt, km, vt, tb)
    os_, lse_s = _swa_fwd(sink_vec, qs, ks, vs)
    w = {**w, **late_weights((om, os_))}
    y, yo, au, bu, x1 = _fwd_mix(om, os_, z, x, w["w_mla_up"], w["w_swa_up"], w["w_out"], g2, tm)
    h2, u = _fwd_mlp_up(x1, g3, w["w_mlp_up"], tm)
    d, x2 = _fwd_mlp_down(u, w["w_mlp_down"], x1, g4, tm)
    loss, dx2, dgt, de0, dg5 = _ple_fwd_bwd(p, x2, tgt, w["w_ple"], g5, w["w_ple_gate"], tm)

    dd, da, dg4 = _bwd_mlp_down(dx2, d, g4, w["w_mlp_down"], u, tm)
    dx1, dg3 = _bwd_mlp_up(da, w["w_mlp_up"], x1, g3, dx2, tm)
    dyo, dg2, dau, dbu, dga, dgb, dos, delta_m, dom_t = _bwd_mix(dx1, yo, g2, w["w_out"], z, au, bu, w["w_mla_up"],
                                                                w["w_swa_up"], om, tb)
    gpk_late = lax.empty((N_CHIPS, PACK_ROWS["late"], D), BF16)
    for weight, a_, g_ in (("w_mla_up", om, dau), ("w_swa_up", os_, dbu), ("w_out", y, dyo), ("w_ple_gate", x2, dgt),
                           ("w_mlp_up", h2, da), ("w_mlp_down", u, dd)):
        gpk_late = _wgrad(a_, g_, "wgrad_" + weight[2:], into=(gpk_late, weight))
    token = late_grads_out(gpk_late)
    delta_m = delta_m + token[0, 0]
    dqm, dkm, dvm = _mla_bwd(qt, km, kt, vm, dom_t, lse_m, delta_m, tb)
    dqs, dkc, dkp, dvc, dvp, dsink = _swa_bwd(sink_vec, qs, ks, vs, dos, os_, lse_s)
    dqb, dknb, dvb, dsq, drest, dgq, dgkv = _bwd_qkv(dqm, dkm, dvm, dqs, dkc, dkp, dvc, dvp, z, gq, gkv, wqb, wkn, wv,
                                                      tab_m, tab_s)
    gx, dg1 = _bwd_in(dsq, dga, dgb, drest, w_in_p, x, g1, dx1, tm)

    g_in_p = [_wgrad(h1, dsq, "wgrad_in_sq"), _wgrad(h1, dga, "wgrad_in_ga"), _wgrad(h1, dgb, "wgrad_in_gb"),
              _wgrad(h1, drest, "wgrad_in_rest")]
    g_qb_p = _wgrad_t(dqb, qn, "wgrad_q_b").T
    g_kn_p = _wgrad_t(dknb, kvn, "wgrad_kv_b_nope").T
    g_v_p = _wgrad_t(dvb, kvn, "wgrad_kv_b_v").T
    grads = {
        "w_in_p": g_in_p,
        "w_q_b": g_qb_p.reshape(Q_LORA, MLA_HEADS, 128)[:, :, :96].reshape(Q_LORA, 1536),
        "w_kv_b": jnp.concatenate([g_kn_p.reshape(KV_LORA, MLA_HEADS, 128)[:, :, :64], g_v_p.reshape(KV_LORA, MLA_HEADS, 64)],
                                  axis=2).reshape(KV_LORA, 2048),
        "w_ple": _wgrad(p, de0, "wgrad_ple"),
    }
    small_grads = {"g_mix_pre": dg1, "g_q_a": dgq.reshape(1, Q_LORA), "g_kv_a": dgkv.reshape(1, KV_LORA), "sinks": dsink[0:1, 0:SWA_HEADS], "g_mix_post": dg2,
                   "g_mlp_pre": dg3, "g_mlp_post": dg4, "g_ple": dg5}
    return loss, gx, grads, small_grads


def _pack_small(vals, fill, scalar=None):
    wide = [vals[n] for n, k in SMALL if k == D]
    narrow = [vals[n] for n, k in SMALL if k != D]
    used = sum(k for _, k in SMALL if k != D)
    last = jnp.concatenate(narrow + [jnp.full((1, D - used), fill, F32)], axis=1)
    rest = jnp.full((2, D), fill, F32)
    if scalar is not None:
        rest = jnp.concatenate([jnp.concatenate([scalar, rest[0:1, 1:]], axis=1), rest[1:2]], axis=0)
    return jnp.concatenate(wide + [last, rest], axis=0)


def _unpack_small(pk):
    out, row, off = {}, 0, 0
    for n, k in SMALL:
        if k == D:
            out[n] = pk[row:row + 1]
            row += 1
    for n, k in SMALL:
        if k != D:
            out[n] = pk[5:6, off:off + k]
            off += k
    return out


def kernel(x, p, g_mix_pre, w_in, g_q_a, w_q_b, g_kv_a, w_kv_b, sinks, w_mla_up, w_swa_up, w_out, g_mix_post, g_mlp_pre, w_mlp_up, w_mlp_down, g_mlp_post, w_ple, g_ple, w_ple_gate, loss_target, m_g_mix_pre, m_w_in, m_g_q_a, m_w_q_b, m_g_kv_a, m_w_kv_b, m_sinks, m_w_mla_up, m_w_swa_up, m_w_out, m_g_mix_post, m_g_mlp_pre, m_w_mlp_up, m_w_mlp_down, m_g_mlp_post, m_w_ple, m_g_ple, m_w_ple_gate, v_g_mix_pre, v_w_in, v_g_q_a, v_w_q_b, v_g_kv_a, v_w_kv_b, v_sinks, v_w_mla_up, v_w_swa_up, v_w_out, v_g_mix_post, v_g_mlp_pre, v_w_mlp_up, v_w_mlp_down, v_g_mlp_post, v_w_ple, v_g_ple, v_w_ple_gate):
    given = dict(locals())
    big_w = {n: given[n][0] for n, _, _ in BIG}
    small_w = {n: given[n] for n, _ in SMALL}
    small_m = {n: given["m_" + n] for n, _ in SMALL}
    small_v = {n: given["v_" + n] for n, _ in SMALL}

    core = lax.axis_index("c")
    chip = 2 * lax.axis_index("x") + lax.axis_index("y")
    core_i = core.astype(jnp.int32).reshape(1)
    dev_i = jnp.stack([2 * chip + core, chip, core]).astype(jnp.int32)

    own_early = _pack_early(big_w, BF16)
    own_late = _pack_late(big_w, BF16)
    got_early = _all_gather(own_early)
    late_flight = _gather_late_start(own_late, got_early)
    weights = _full_weights(got_early, "early")
    step_small = {**small_w, "g_mix_pre": small_w["g_mix_pre"] + late_flight[4][0, 0]}

    def late_weights(after):
        return _full_weights(_gather_late_wait(*late_flight[:4], after), "late")

    flight = {}

    def late_grads_out(gpk_late):
        flight["late"] = _reduce_late_start(gpk_late, dev_i)
        return flight["late"][4]

    loss_blk, gx, grads, small_grads = _local_step(x[0], p[0, 0], loss_target[0], weights, step_small, late_weights,
                                                   late_grads_out)

    gpk = _split_full_grads(grads, _pack_early, BF16)
    part = _rs_sibling_add(gpk)
    small_own = _pack_small(small_grads, 0.0, loss_blk[0:1, 0:1])
    early_flight = _rs_chips_start(part, small_own, dev_i)

    out_g, out_d, out_m, out_v = {}, {}, {}, {}
    gpk_late, parts_late = _reduce_late_wait(*flight["late"][:4], early_flight[6])
    joined_late = _reduce_late_add_join(gpk_late, parts_late)
    for n, _, _ in BIG:
        if PACK_AT[n][0] == "late":
            out_g[n], out_d[n], out_m[n], out_v[n] = _adamw(given[n], joined_late, given["m_" + n], given["v_" + n], n)

    part, parts, small_parts = _rs_chips_wait(*early_flight[:6], [out_d[n] for n in out_d])
    joined_early = _rs_join(_rs_add_chips(dev_i[1:3], part, parts), "rs_join_early")
    for n, _, _ in BIG:
        if PACK_AT[n][0] == "early":
            t = given[n].shape[2] % LANES != 0
            wmv = [jnp.swapaxes(a, 1, 2) if t else a for a in (given[n], given["m_" + n], given["v_" + n])]
            res = _adamw(wmv[0], joined_early, wmv[1], wmv[2], n, transposed=t)
            out_g[n], out_d[n], out_m[n], out_v[n] = [jnp.swapaxes(a, 1, 2) if t else a for a in res]

    mine = (lax.broadcasted_iota(jnp.int32, (N_DEV, 1, 1), 0) == dev_i[0])
    g_small_pk, d_small_pk, m_small_pk, v_small_pk = _adamw_small(
        _pack_small(small_w, 0.0), jnp.where(mine, small_own[None], small_parts), _pack_small(small_m, 0.0),
        _pack_small(small_v, 1.0))
    loss = g_small_pk[6, 0]
    for out, pk in ((out_g, g_small_pk), (out_d, d_small_pk), (out_m, m_small_pk), (out_v, v_small_pk)):
        out.update(_unpack_small(pk))
    order = ["g_mix_pre", "w_in", "g_q_a", "w_q_b", "g_kv_a", "w_kv_b", "sinks", "w_mla_up", "w_swa_up", "w_out", "g_mix_post",
             "g_mlp_pre", "w_mlp_up", "w_mlp_down", "g_mlp_post", "w_ple", "g_ple", "w_ple_gate"]
    return (loss, gx[None], *[out_g[n] for n in order], *[out_d[n] for n in order], *[out_m[n] for n in order],
            *[out_v[n] for n in order])
```
